```python
import jax, jax.numpy as jnp
from jax import lax
import numpy as np

D_MODEL = 1024
BATCH = 8
SEQ = 2048
DEPTH = 1

CHUNK = 64
N_MEM = 256
HEAD_DIM = 64
D_MIX = D_MODEL
FOX_HEADS = D_MIX // 2 // HEAD_DIM
CHK_HEADS = D_MIX // 2 // HEAD_DIM
D_FOX = FOX_HEADS * HEAD_DIM
D_CHK = CHK_HEADS * HEAD_DIM
LEFT_CHUNKS = 8
BAND = (LEFT_CHUNKS + 1) * CHUNK
MAX_REL = 128
N_REL = 2 * MAX_REL + 1
Q_BLOCK = 128
MEM_HEADS = 4
MEM_HEAD_DIM = D_MODEL // MEM_HEADS
D_FF = 4 * D_MODEL
EPS = 1e-6
D_IN = 3 * D_FOX + FOX_HEADS + 3 * D_CHK

kernel_name = 'hybrid_fox_chunkrel_memxattn_block'


def rmsnorm(x, g):
    xf = x.astype(jnp.float32)
    y = xf * lax.rsqrt(jnp.mean(xf * xf, axis=-1, keepdims=True) + EPS) * g.astype(jnp.float32)
    return y.astype(x.dtype)


def forgetting_attention(q, k, v, f_logit):
    S = q.shape[1]
    Dh = q.shape[-1]
    scale = Dh ** -0.5
    logf = jax.nn.log_sigmoid(f_logit.astype(jnp.float32))
    c = jnp.cumsum(logf, axis=1).transpose(0, 2, 1)
    pos = jnp.arange(S)
    outs = []
    for i in range(S // Q_BLOCK):
        q0, q1 = i * Q_BLOCK, (i + 1) * Q_BLOCK
        qb = q[:, q0:q1]
        kb = k[:, :q1]
        vb = v[:, :q1]
        logits = jnp.einsum('bqhd,bkhd->bhqk', qb, kb).astype(jnp.float32) * scale
        logits = logits + c[:, :, q0:q1, None] - c[:, :, None, :q1]
        causal = pos[q0:q1, None] >= pos[None, :q1]
        logits = jnp.where(causal[None, None], logits, -jnp.inf)
        p = jax.nn.softmax(logits, axis=-1).astype(v.dtype)
        outs.append(jnp.einsum('bhqk,bkhd->bqhd', p, vb))
    return jnp.concatenate(outs, axis=1)


def _rel_index():
    i = np.arange(CHUNK)[:, None]
    m = np.arange(BAND)[None, :]
    rel = i + LEFT_CHUNKS * CHUNK - m
    return np.clip(rel, -MAX_REL, MAX_REL) + MAX_REL


def chunked_relpos_attention(q, k, v, rel_table):
    B, S, H, Dh = q.shape
    NC = S // CHUNK
    scale = Dh ** -0.5
    qc = q.reshape(B, NC, CHUNK, H, Dh)
    pad = ((0, 0), (LEFT_CHUNKS * CHUNK, 0), (0, 0), (0, 0))
    kp = jnp.pad(k, pad).reshape(B, NC + LEFT_CHUNKS, CHUNK, H, Dh)
    vp = jnp.pad(v, pad).reshape(B, NC + LEFT_CHUNKS, CHUNK, H, Dh)
    kband = jnp.concatenate([kp[:, j:j + NC] for j in range(LEFT_CHUNKS + 1)], axis=2)
    vband = jnp.concatenate([vp[:, j:j + NC] for j in range(LEFT_CHUNKS + 1)], axis=2)
    bias = rel_table.astype(jnp.float32)[:, _rel_index()]
    key_pos = jnp.arange(NC)[:, None] * CHUNK + jnp.arange(BAND)[None, :] - LEFT_CHUNKS * CHUNK
    valid = key_pos >= 0
    logits = jnp.einsum('bcqhd,bckhd->bhcqk', qc, kband).astype(jnp.float32) * scale
    logits = logits + bias[None, :, None]
    logits = jnp.where(valid[None, None, :, None, :], logits, -jnp.inf)
    p = jax.nn.softmax(logits, axis=-1).astype(v.dtype)
    out = jnp.einsum('bhcqk,bckhd->bcqhd', p, vband)
    return out.reshape(B, S, H, Dh)


def memory_cross_attention(h, mem_n, w_mq, w_mk, w_mv, w_mo):
    B, S, _ = h.shape
    M = mem_n.shape[1]
    q = (h @ w_mq).reshape(B, S, MEM_HEADS, MEM_HEAD_DIM)
    k = (mem_n @ w_mk).reshape(B, M, MEM_HEADS, MEM_HEAD_DIM)
    v = (mem_n @ w_mv).reshape(B, M, MEM_HEADS, MEM_HEAD_DIM)
    logits = jnp.einsum('bshd,bmhd->bhsm', q, k).astype(jnp.float32) * (MEM_HEAD_DIM ** -0.5)
    p = jax.nn.softmax(logits, axis=-1).astype(v.dtype)
    o = jnp.einsum('bhsm,bmhd->bshd', p, v).reshape(B, S, D_MODEL)
    return o @ w_mo


def _fwd_setup_inputs(seed: int = 0) -> dict:
    key = jax.random.key(seed)
    ks = jax.random.split(key, 24)
    f32 = jnp.float32

    def w(k, shape, fan_in):
        return jax.random.normal(k, shape, f32) * fan_in ** -0.5

    def gain(k, n):
        return 1.0 + 0.05 * jax.random.normal(k, (DEPTH, n), f32)

    return {
        'x': jax.random.normal(ks[0], (BATCH, SEQ, D_MODEL), f32),
        'mem': jax.random.normal(ks[1], (BATCH, N_MEM, D_MODEL), f32),
        'w_in': w(ks[2], (DEPTH, D_MODEL, D_IN), D_MODEL),
        'b_fgt': 3.0 + 0.1 * jax.random.normal(ks[3], (DEPTH, FOX_HEADS), f32),
        'rel_bias': 0.2 * jax.random.normal(ks[4], (DEPTH, CHK_HEADS, N_REL), f32),
        'g_fox_out': gain(ks[5], D_FOX),
        'g_chk_out': gain(ks[6], D_CHK),
        'w_out': w(ks[7], (DEPTH, D_MIX, D_MODEL), D_MIX),
        'g_mix_pre': gain(ks[8], D_MODEL),
        'g_mix_post': gain(ks[9], D_MODEL),
        'g_mem_kv': gain(ks[10], D_MODEL),
        'w_mq': w(ks[11], (DEPTH, D_MODEL, D_MODEL), D_MODEL),
        'w_mk': w(ks[12], (DEPTH, D_MODEL, D_MODEL), D_MODEL),
        'w_mv': w(ks[13], (DEPTH, D_MODEL, D_MODEL), D_MODEL),
        'w_mo': w(ks[14], (DEPTH, D_MODEL, D_MODEL), D_MODEL),
        'g_mem_pre': gain(ks[15], D_MODEL),
        'g_mem_post': gain(ks[16], D_MODEL),
        'w_ff1': w(ks[17], (DEPTH, D_MODEL, D_FF), D_MODEL),
        'w_ff2': w(ks[18], (DEPTH, D_FF, D_MODEL), D_FF),
        'g_ff_pre': gain(ks[19], D_MODEL),
        'g_ff_post': gain(ks[20], D_MODEL),
    }


def _fwd_reference(x, mem, w_in, b_fgt, rel_bias, g_fox_out, g_chk_out, w_out, g_mix_pre, g_mix_post,
              g_mem_kv, w_mq, w_mk, w_mv, w_mo, g_mem_pre, g_mem_post,
              w_ff1, w_ff2, g_ff_pre, g_ff_post):
    B, S, _ = x.shape
    for l in range(DEPTH):
        h = rmsnorm(x, g_mix_pre[l])
        proj = h @ w_in[l]
        o0 = 0
        fq = proj[..., o0:o0 + D_FOX]; o0 += D_FOX
        fk = proj[..., o0:o0 + D_FOX]; o0 += D_FOX
        fv = proj[..., o0:o0 + D_FOX]; o0 += D_FOX
        f_logit = proj[..., o0:o0 + FOX_HEADS] + b_fgt[l]; o0 += FOX_HEADS
        cq = proj[..., o0:o0 + D_CHK]; o0 += D_CHK
        ck = proj[..., o0:o0 + D_CHK]; o0 += D_CHK
        cv = proj[..., o0:o0 + D_CHK]
        shp_f = (B, S, FOX_HEADS, HEAD_DIM)
        shp_c = (B, S, CHK_HEADS, HEAD_DIM)
        y_fox = forgetting_attention(fq.reshape(shp_f), fk.reshape(shp_f), fv.reshape(shp_f), f_logit)
        y_chk = chunked_relpos_attention(cq.reshape(shp_c), ck.reshape(shp_c), cv.reshape(shp_c), rel_bias[l])
        y = jnp.concatenate([rmsnorm(y_fox.reshape(B, S, D_FOX), g_fox_out[l]),
                             rmsnorm(y_chk.reshape(B, S, D_CHK), g_chk_out[l])], axis=-1)
        x = x + rmsnorm(y @ w_out[l], g_mix_post[l])
        h = rmsnorm(x, g_mem_pre[l])
        mem_n = rmsnorm(mem, g_mem_kv[l])
        y = memory_cross_attention(h, mem_n, w_mq[l], w_mk[l], w_mv[l], w_mo[l])
        x = x + rmsnorm(y, g_mem_post[l])
        h = rmsnorm(x, g_ff_pre[l])
        y = jnp.square(jax.nn.relu(h @ w_ff1[l])) @ w_ff2[l]
        x = x + rmsnorm(y, g_ff_post[l])
    return x


import jax as _jax
import jax.numpy as _jnp

TWIN_FORMAT = 'train_step'
FWD_PARAMS = ['x', 'mem', 'w_in', 'b_fgt', 'rel_bias', 'g_fox_out', 'g_chk_out', 'w_out', 'g_mix_pre', 'g_mix_post', 'g_mem_kv', 'w_mq', 'w_mk', 'w_mv', 'w_mo', 'g_mem_pre', 'g_mem_post', 'w_ff1', 'w_ff2', 'g_ff_pre', 'g_ff_post']
TWIN_WEIGHTS = ['w_in', 'b_fgt', 'rel_bias', 'g_fox_out', 'g_chk_out', 'w_out', 'g_mix_pre', 'g_mix_post', 'g_mem_kv', 'w_mq', 'w_mk', 'w_mv', 'w_mo', 'g_mem_pre', 'g_mem_post', 'w_ff1', 'w_ff2', 'g_ff_pre', 'g_ff_post']
TWIN_DIFF_INPUT = 'x'
TWIN_INPUTS = ['x', 'mem', 'w_in', 'b_fgt', 'rel_bias', 'g_fox_out', 'g_chk_out', 'w_out', 'g_mix_pre', 'g_mix_post', 'g_mem_kv', 'w_mq', 'w_mk', 'w_mv', 'w_mo', 'g_mem_pre', 'g_mem_post', 'w_ff1', 'w_ff2', 'g_ff_pre', 'g_ff_post', 'loss_target', 'm_w_in', 'm_b_fgt', 'm_rel_bias', 'm_g_fox_out', 'm_g_chk_out', 'm_w_out', 'm_g_mix_pre', 'm_g_mix_post', 'm_g_mem_kv', 'm_w_mq', 'm_w_mk', 'm_w_mv', 'm_w_mo', 'm_g_mem_pre', 'm_g_mem_post', 'm_w_ff1', 'm_w_ff2', 'm_g_ff_pre', 'm_g_ff_post', 'v_w_in', 'v_b_fgt', 'v_rel_bias', 'v_g_fox_out', 'v_g_chk_out', 'v_w_out', 'v_g_mix_pre', 'v_g_mix_post', 'v_g_mem_kv', 'v_w_mq', 'v_w_mk', 'v_w_mv', 'v_w_mo', 'v_g_mem_pre', 'v_g_mem_post', 'v_w_ff1', 'v_w_ff2', 'v_g_ff_pre', 'v_g_ff_post']
TWIN_OUTPUTS = ['loss', 'grad_x', 'grad_w_in', 'grad_b_fgt', 'grad_rel_bias', 'grad_g_fox_out', 'grad_g_chk_out', 'grad_w_out', 'grad_g_mix_pre', 'grad_g_mix_post', 'grad_g_mem_kv', 'grad_w_mq', 'grad_w_mk', 'grad_w_mv', 'grad_w_mo', 'grad_g_mem_pre', 'grad_g_mem_post', 'grad_w_ff1', 'grad_w_ff2', 'grad_g_ff_pre', 'grad_g_ff_post', 'delta_w_in', 'delta_b_fgt', 'delta_rel_bias', 'delta_g_fox_out', 'delta_g_chk_out', 'delta_w_out', 'delta_g_mix_pre', 'delta_g_mix_post', 'delta_g_mem_kv', 'delta_w_mq', 'delta_w_mk', 'delta_w_mv', 'delta_w_mo', 'delta_g_mem_pre', 'delta_g_mem_post', 'delta_w_ff1', 'delta_w_ff2', 'delta_g_ff_pre', 'delta_g_ff_post', 'new_m_w_in', 'new_m_b_fgt', 'new_m_rel_bias', 'new_m_g_fox_out', 'new_m_g_chk_out', 'new_m_w_out', 'new_m_g_mix_pre', 'new_m_g_mix_post', 'new_m_g_mem_kv', 'new_m_w_mq', 'new_m_w_mk', 'new_m_w_mv', 'new_m_w_mo', 'new_m_g_mem_pre', 'new_m_g_mem_post', 'new_m_w_ff1', 'new_m_w_ff2', 'new_m_g_ff_pre', 'new_m_g_ff_post', 'new_v_w_in', 'new_v_b_fgt', 'new_v_rel_bias', 'new_v_g_fox_out', 'new_v_g_chk_out', 'new_v_w_out', 'new_v_g_mix_pre', 'new_v_g_mix_post', 'new_v_g_mem_kv', 'new_v_w_mq', 'new_v_w_mk', 'new_v_w_mv', 'new_v_w_mo', 'new_v_g_mem_pre', 'new_v_g_mem_post', 'new_v_w_ff1', 'new_v_w_ff2', 'new_v_g_ff_pre', 'new_v_g_ff_post']
TWIN_LEAF_KINDS = {'loss': 'loss', 'grad_x': 'grad_x', 'grad_w_in': 'grad_w', 'grad_b_fgt': 'grad_w', 'grad_rel_bias': 'grad_w', 'grad_g_fox_out': 'grad_w', 'grad_g_chk_out': 'grad_w', 'grad_w_out': 'grad_w', 'grad_g_mix_pre': 'grad_w', 'grad_g_mix_post': 'grad_w', 'grad_g_mem_kv': 'grad_w', 'grad_w_mq': 'grad_w', 'grad_w_mk': 'grad_w', 'grad_w_mv': 'grad_w', 'grad_w_mo': 'grad_w', 'grad_g_mem_pre': 'grad_w', 'grad_g_mem_post': 'grad_w', 'grad_w_ff1': 'grad_w', 'grad_w_ff2': 'grad_w', 'grad_g_ff_pre': 'grad_w', 'grad_g_ff_post': 'grad_w', 'delta_w_in': 'delta_w', 'delta_b_fgt': 'delta_w', 'delta_rel_bias': 'delta_w', 'delta_g_fox_out': 'delta_w', 'delta_g_chk_out': 'delta_w', 'delta_w_out': 'delta_w', 'delta_g_mix_pre': 'delta_w', 'delta_g_mix_post': 'delta_w', 'delta_g_mem_kv': 'delta_w', 'delta_w_mq': 'delta_w', 'delta_w_mk': 'delta_w', 'delta_w_mv': 'delta_w', 'delta_w_mo': 'delta_w', 'delta_g_mem_pre': 'delta_w', 'delta_g_mem_post': 'delta_w', 'delta_w_ff1': 'delta_w', 'delta_w_ff2': 'delta_w', 'delta_g_ff_pre': 'delta_w', 'delta_g_ff_post': 'delta_w', 'new_m_w_in': 'new_m', 'new_m_b_fgt': 'new_m', 'new_m_rel_bias': 'new_m', 'new_m_g_fox_out': 'new_m', 'new_m_g_chk_out': 'new_m', 'new_m_w_out': 'new_m', 'new_m_g_mix_pre': 'new_m', 'new_m_g_mix_post': 'new_m', 'new_m_g_mem_kv': 'new_m', 'new_m_w_mq': 'new_m', 'new_m_w_mk': 'new_m', 'new_m_w_mv': 'new_m', 'new_m_w_mo': 'new_m', 'new_m_g_mem_pre': 'new_m', 'new_m_g_mem_post': 'new_m', 'new_m_w_ff1': 'new_m', 'new_m_w_ff2': 'new_m', 'new_m_g_ff_pre': 'new_m', 'new_m_g_ff_post': 'new_m', 'new_v_w_in': 'new_v', 'new_v_b_fgt': 'new_v', 'new_v_rel_bias': 'new_v', 'new_v_g_fox_out': 'new_v', 'new_v_g_chk_out': 'new_v', 'new_v_w_out': 'new_v', 'new_v_g_mix_pre': 'new_v', 'new_v_g_mix_post': 'new_v', 'new_v_g_mem_kv': 'new_v', 'new_v_w_mq': 'new_v', 'new_v_w_mk': 'new_v', 'new_v_w_mv': 'new_v', 'new_v_w_mo': 'new_v', 'new_v_g_mem_pre': 'new_v', 'new_v_g_mem_post': 'new_v', 'new_v_w_ff1': 'new_v', 'new_v_w_ff2': 'new_v', 'new_v_g_ff_pre': 'new_v', 'new_v_g_ff_post': 'new_v'}


def _forward(args):
    return _fwd_reference(*[args[k] for k in FWD_PARAMS])


def _output_shape():
    out = _jax.eval_shape(lambda: _forward(_fwd_setup_inputs(0)))
    return out.shape, out.dtype

N_MICROBATCH = 1
ADAM_LR = 0.001
ADAM_B1 = 0.9
ADAM_B2 = 0.999
ADAM_EPS = 1e-08
ADAM_WD = 0.01
ADAM_STEP = 10
PER_EXAMPLE_BATCH_AXIS = {'x': 0, 'mem': 0, 'loss_target': 0}
SHARED_INPUTS = []
_WEIGHT_DTYPES = {'w_in': _jnp.float32, 'b_fgt': _jnp.float32, 'rel_bias': _jnp.float32, 'g_fox_out': _jnp.float32, 'g_chk_out': _jnp.float32, 'w_out': _jnp.float32, 'g_mix_pre': _jnp.float32, 'g_mix_post': _jnp.float32, 'g_mem_kv': _jnp.float32, 'w_mq': _jnp.float32, 'w_mk': _jnp.float32, 'w_mv': _jnp.float32, 'w_mo': _jnp.float32, 'g_mem_pre': _jnp.float32, 'g_mem_post': _jnp.float32, 'w_ff1': _jnp.float32, 'w_ff2': _jnp.float32, 'g_ff_pre': _jnp.float32, 'g_ff_post': _jnp.float32}
MOMENT_SCALE = {'w_in': 8.894401e-01, 'b_fgt': 2.503900e+00, 'rel_bias': 3.212639e-01, 'g_fox_out': 7.015692e-01, 'g_chk_out': 2.146716e+00, 'w_out': 1.534494e+00, 'g_mix_pre': 1.506681e+00, 'g_mix_post': 1.591844e+01, 'g_mem_kv': 2.795155e+00, 'w_mq': 6.599846e-01, 'w_mk': 6.905991e-01, 'w_mv': 2.733929e+00, 'w_mo': 2.659897e+00, 'g_mem_pre': 6.572801e-01, 'g_mem_post': 1.673536e+01, 'w_ff1': 7.072265e-01, 'w_ff2': 2.282878e+00, 'g_ff_pre': 1.268360e+00, 'g_ff_post': 1.685674e+01}


def _to_microbatches(a, axis):
    t = _jnp.moveaxis(a, axis, 0)
    t = t.reshape((N_MICROBATCH, t.shape[0] // N_MICROBATCH) + t.shape[1:])
    return _jnp.moveaxis(t, 1, axis + 1)


def setup_inputs(seed: int = 0) -> dict:
    inp = _fwd_setup_inputs(seed)
    key = _jax.random.fold_in(_jax.random.key(seed), 7919)
    shape, _ = _output_shape()
    out = dict(inp)
    out["loss_target"] = _jax.random.normal(_jax.random.fold_in(key, 0), shape, _jnp.float32)
    for i, name in enumerate(TWIN_WEIGHTS):
        w = inp[name].astype(_jnp.float32)
        if MOMENT_SCALE is None:
            s = _jnp.sqrt(_jnp.mean(_jnp.square(w)) + 1e-30)
        else:
            s = MOMENT_SCALE[name]
        km, kv = _jax.random.split(_jax.random.fold_in(key, i + 1))
        out[name] = w
        out["m_" + name] = s * _jax.random.normal(km, w.shape, _jnp.float32)
        out["v_" + name] = (s * s) * _jax.random.uniform(kv, w.shape, _jnp.float32, 0.5, 1.5)
    if N_MICROBATCH > 1:
        for name, axis in PER_EXAMPLE_BATCH_AXIS.items():
            out[name] = _to_microbatches(out[name], axis)
    return {'x': out['x'], 'mem': out['mem'], 'w_in': out['w_in'], 'b_fgt': out['b_fgt'], 'rel_bias': out['rel_bias'], 'g_fox_out': out['g_fox_out'], 'g_chk_out': out['g_chk_out'], 'w_out': out['w_out'], 'g_mix_pre': out['g_mix_pre'], 'g_mix_post': out['g_mix_post'], 'g_mem_kv': out['g_mem_kv'], 'w_mq': out['w_mq'], 'w_mk': out['w_mk'], 'w_mv': out['w_mv'], 'w_mo': out['w_mo'], 'g_mem_pre': out['g_mem_pre'], 'g_mem_post': out['g_mem_post'], 'w_ff1': out['w_ff1'], 'w_ff2': out['w_ff2'], 'g_ff_pre': out['g_ff_pre'], 'g_ff_post': out['g_ff_post'], 'loss_target': out['loss_target'], 'm_w_in': out['m_w_in'], 'm_b_fgt': out['m_b_fgt'], 'm_rel_bias': out['m_rel_bias'], 'm_g_fox_out': out['m_g_fox_out'], 'm_g_chk_out': out['m_g_chk_out'], 'm_w_out': out['m_w_out'], 'm_g_mix_pre': out['m_g_mix_pre'], 'm_g_mix_post': out['m_g_mix_post'], 'm_g_mem_kv': out['m_g_mem_kv'], 'm_w_mq': out['m_w_mq'], 'm_w_mk': out['m_w_mk'], 'm_w_mv': out['m_w_mv'], 'm_w_mo': out['m_w_mo'], 'm_g_mem_pre': out['m_g_mem_pre'], 'm_g_mem_post': out['m_g_mem_post'], 'm_w_ff1': out['m_w_ff1'], 'm_w_ff2': out['m_w_ff2'], 'm_g_ff_pre': out['m_g_ff_pre'], 'm_g_ff_post': out['m_g_ff_post'], 'v_w_in': out['v_w_in'], 'v_b_fgt': out['v_b_fgt'], 'v_rel_bias': out['v_rel_bias'], 'v_g_fox_out': out['v_g_fox_out'], 'v_g_chk_out': out['v_g_chk_out'], 'v_w_out': out['v_w_out'], 'v_g_mix_pre': out['v_g_mix_pre'], 'v_g_mix_post': out['v_g_mix_post'], 'v_g_mem_kv': out['v_g_mem_kv'], 'v_w_mq': out['v_w_mq'], 'v_w_mk': out['v_w_mk'], 'v_w_mv': out['v_w_mv'], 'v_w_mo': out['v_w_mo'], 'v_g_mem_pre': out['v_g_mem_pre'], 'v_g_mem_post': out['v_g_mem_post'], 'v_w_ff1': out['v_w_ff1'], 'v_w_ff2': out['v_w_ff2'], 'v_g_ff_pre': out['v_g_ff_pre'], 'v_g_ff_post': out['v_g_ff_post']}


def _loss(weights, diff, rest, loss_target):
    with _jax.named_scope("forward"):
        args = {**rest, TWIN_DIFF_INPUT: diff, **{k: w.astype(_WEIGHT_DTYPES[k]) for k, w in weights.items()}}
        y = _forward(args)
    with _jax.named_scope("loss_head"):
        err = _jnp.square(y.astype(_jnp.float32) - loss_target)
        return 0.5 * _jnp.sum(_jnp.mean(err, axis=-1)) if err.ndim else 0.5 * err


def _adamw(w, g, m, v):
    m = ADAM_B1 * m + (1.0 - ADAM_B1) * g
    v = ADAM_B2 * v + (1.0 - ADAM_B2) * _jnp.square(g)
    m_hat = m / (1.0 - ADAM_B1 ** ADAM_STEP)
    v_hat = v / (1.0 - ADAM_B2 ** ADAM_STEP)
    delta = -ADAM_LR * (m_hat / (_jnp.sqrt(v_hat) + ADAM_EPS) + ADAM_WD * w)
    return delta, m, v


def reference(x, mem, w_in, b_fgt, rel_bias, g_fox_out, g_chk_out, w_out, g_mix_pre, g_mix_post, g_mem_kv, w_mq, w_mk, w_mv, w_mo, g_mem_pre, g_mem_post, w_ff1, w_ff2, g_ff_pre, g_ff_post, loss_target, m_w_in, m_b_fgt, m_rel_bias, m_g_fox_out, m_g_chk_out, m_w_out, m_g_mix_pre, m_g_mix_post, m_g_mem_kv, m_w_mq, m_w_mk, m_w_mv, m_w_mo, m_g_mem_pre, m_g_mem_post, m_w_ff1, m_w_ff2, m_g_ff_pre, m_g_ff_post, v_w_in, v_b_fgt, v_rel_bias, v_g_fox_out, v_g_chk_out, v_w_out, v_g_mix_pre, v_g_mix_post, v_g_mem_kv, v_w_mq, v_w_mk, v_w_mv, v_w_mo, v_g_mem_pre, v_g_mem_post, v_w_ff1, v_w_ff2, v_g_ff_pre, v_g_ff_post):
    given = dict(x=x, mem=mem, w_in=w_in, b_fgt=b_fgt, rel_bias=rel_bias, g_fox_out=g_fox_out, g_chk_out=g_chk_out, w_out=w_out, g_mix_pre=g_mix_pre, g_mix_post=g_mix_post, g_mem_kv=g_mem_kv, w_mq=w_mq, w_mk=w_mk, w_mv=w_mv, w_mo=w_mo, g_mem_pre=g_mem_pre, g_mem_post=g_mem_post, w_ff1=w_ff1, w_ff2=w_ff2, g_ff_pre=g_ff_pre, g_ff_post=g_ff_post, loss_target=loss_target, m_w_in=m_w_in, m_b_fgt=m_b_fgt, m_rel_bias=m_rel_bias, m_g_fox_out=m_g_fox_out, m_g_chk_out=m_g_chk_out, m_w_out=m_w_out, m_g_mix_pre=m_g_mix_pre, m_g_mix_post=m_g_mix_post, m_g_mem_kv=m_g_mem_kv, m_w_mq=m_w_mq, m_w_mk=m_w_mk, m_w_mv=m_w_mv, m_w_mo=m_w_mo, m_g_mem_pre=m_g_mem_pre, m_g_mem_post=m_g_mem_post, m_w_ff1=m_w_ff1, m_w_ff2=m_w_ff2, m_g_ff_pre=m_g_ff_pre, m_g_ff_post=m_g_ff_post, v_w_in=v_w_in, v_b_fgt=v_b_fgt, v_rel_bias=v_rel_bias, v_g_fox_out=v_g_fox_out, v_g_chk_out=v_g_chk_out, v_w_out=v_w_out, v_g_mix_pre=v_g_mix_pre, v_g_mix_post=v_g_mix_post, v_g_mem_kv=v_g_mem_kv, v_w_mq=v_w_mq, v_w_mk=v_w_mk, v_w_mv=v_w_mv, v_w_mo=v_w_mo, v_g_mem_pre=v_g_mem_pre, v_g_mem_post=v_g_mem_post, v_w_ff1=v_w_ff1, v_w_ff2=v_w_ff2, v_g_ff_pre=v_g_ff_pre, v_g_ff_post=v_g_ff_post)
    weights = {n: given[n] for n in TWIN_WEIGHTS}
    shared = {n: given[n] for n in SHARED_INPUTS}
    per_example = {n: given[n] for n in ['x', 'mem']}
    grad_fn = _jax.value_and_grad(_loss, argnums=(0, 1))

    def one_microbatch(ex, loss_target):
        ex = dict(ex)
        diff = ex.pop(TWIN_DIFF_INPUT)
        return grad_fn(weights, diff, {**shared, **ex}, loss_target)

    if N_MICROBATCH == 1:
        loss, (grad_w, grad_x) = one_microbatch(per_example, given["loss_target"])
    else:
        def body(carry, xs):
            loss_sum, grad_sum = carry
            l_k, (gw_k, gx_k) = one_microbatch(xs[0], xs[1])
            with _jax.named_scope("update"):
                return (loss_sum + l_k, _jax.tree.map(_jnp.add, grad_sum, gw_k)), gx_k

        init = (_jnp.zeros((), _jnp.float32), _jax.tree.map(_jnp.zeros_like, weights))
        (loss, grad_w), grad_x = _jax.lax.scan(body, init, (per_example, given["loss_target"]))
    with _jax.named_scope("update"):
        delta_w, new_m, new_v = {}, {}, {}
        for n in TWIN_WEIGHTS:
            delta_w[n], new_m[n], new_v[n] = _adamw(weights[n], grad_w[n], given["m_" + n], given["v_" + n])
    return (loss, grad_x, *[grad_w[n] for n in TWIN_WEIGHTS], *[delta_w[n] for n in TWIN_WEIGHTS],
            *[new_m[n] for n in TWIN_WEIGHTS], *[new_v[n] for n in TWIN_WEIGHTS])
```

```python
import functools

import jax
import jax.numpy as jnp
from jax import lax
from jax.experimental import pallas as pl
from jax.experimental.pallas import tpu as pltpu

F32 = jnp.float32
BF16 = jnp.bfloat16
MESH = pl.DeviceIdType.MESH

T = 2048
D = 1024
NMEM = 256
DFF = 4096
EPS = 1e-6
TM = 256
TQ = 256
TK = 256
HD = 64
SCALE = HD ** -0.5
MEM_HEADS = 4
MEM_HD = 256
MEM_SCALE = MEM_HD ** -0.5
NEG = -1e30
LEFT = 512
WIN = LEFT + TQ
VW = 1024
NREL_PAD = 384
PROJ = 3200
GATE0 = 1536
CHK0 = 1664
VMEM_BIG = 56 * 1024 * 1024

ADAM_LR = 0.001
ADAM_B1 = 0.9
ADAM_B2 = 0.999
ADAM_EPS = 1e-08
ADAM_WD = 0.01
ADAM_STEP = 10

R_IN, R_OUT, R_MQ, R_MK, R_MV, R_MO, R_FF1, R_FF2, R_ALL = 0, 400, 528, 656, 784, 912, 1040, 1552, 2064
N_IN = 385
SMALL_ROWS = 24

WEIGHTS = ['w_in', 'b_fgt', 'rel_bias', 'g_fox_out', 'g_chk_out', 'w_out', 'g_mix_pre', 'g_mix_post', 'g_mem_kv',
           'w_mq', 'w_mk', 'w_mv', 'w_mo', 'g_mem_pre', 'g_mem_post', 'w_ff1', 'w_ff2', 'g_ff_pre', 'g_ff_post']
BIG = ['w_in', 'w_out', 'w_mq', 'w_mk', 'w_mv', 'w_mo', 'w_ff1', 'w_ff2']
SMALL = [n for n in WEIGHTS if n not in BIG]


def _pcall(body, **kw):
    return pl.pallas_call(body, **kw)


def _nn(a, b):
    return jnp.dot(a, b, preferred_element_type=F32)


def _nt(a, b):
    return lax.dot_general(a, b, (((1,), (1,)), ((), ())), preferred_element_type=F32)


def _tn(a, b):
    return lax.dot_general(a, b, (((0,), (0,)), ((), ())), preferred_element_type=F32)


def _rstd(x):
    return lax.rsqrt(jnp.mean(x * x, axis=-1, keepdims=True) + EPS)


def _rms(x, g):
    return x * _rstd(x) * g


def _rms_bwd(x, g, dy):
    r = _rstd(x)
    xh = x * r
    dg = jnp.sum(dy * xh, axis=0, keepdims=True)
    dxh = dy * g
    dx = r * (dxh - xh * jnp.mean(dxh * xh, axis=-1, keepdims=True))
    return dx, dg


def _tok_call(body, name, tiled, full, outs_tiled, outs_acc=(), rows=T, tm=TM, vmem=None):
    in_specs = [pl.BlockSpec((tm, a.shape[1]), lambda i: (i, 0)) for a in tiled]
    in_specs += [pl.BlockSpec(a.shape, lambda i, nd=a.ndim: (0,) * nd, pipeline_mode=pl.Buffered(1)) for a in full]
    out_shape = [jax.ShapeDtypeStruct((rows, c), dt) for c, dt in outs_tiled]
    out_shape += [jax.ShapeDtypeStruct(s, F32) for s in outs_acc]
    out_specs = [pl.BlockSpec((tm, c), lambda i: (i, 0)) for c, _ in outs_tiled]
    out_specs += [pl.BlockSpec(s, lambda i, nd=len(s): (0,) * nd) for s in outs_acc]
    return _pcall(
        body, name=name, grid=(rows // tm,), in_specs=in_specs, out_specs=out_specs, out_shape=out_shape,
        compiler_params=pltpu.CompilerParams(dimension_semantics=("arbitrary",), vmem_limit_bytes=vmem),
    )(*tiled, *full)


def _one_call(body, name, ins, outs, vmem=None):
    whole = lambda s: pl.BlockSpec(s, lambda i, nd=len(s): (0,) * nd)
    return _pcall(
        body, name=name, grid=(1,), in_specs=[whole(a.shape) for a in ins], out_specs=[whole(s) for s, _ in outs],
        out_shape=[jax.ShapeDtypeStruct(s, dt) for s, dt in outs],
        compiler_params=pltpu.CompilerParams(dimension_semantics=("arbitrary",), vmem_limit_bytes=vmem),
    )(*ins)


def _premix_fwd(x, g_pre, win_t):
    def body(x_ref, g_ref, w_ref, h_ref, proj_ref, flog_ref):
        h = _rms(x_ref[...], g_ref[...]).astype(BF16)
        h_ref[...] = h
        p = _nt(h, w_ref[...])
        proj_ref[...] = p.astype(BF16)
        flog_ref[...] = p[:, GATE0:GATE0 + 128]

    return _tok_call(body, "premix_fwd", [x], [g_pre, win_t],
                     [(D, BF16), (PROJ, BF16), (128, F32)], vmem=VMEM_BIG)


def _postmix_fwd(x, o_f, o_c, g_fo, g_co, w_out, g_post, g_mpre, w_mq):
    def body(x_ref, of_ref, oc_ref, gfo_ref, gco_ref, wo_ref, gp_ref, gm_ref, wq_ref,
             y_ref, z_ref, x1_ref, h2_ref, qm_ref):
        y_ref[:, :512] = _rms(of_ref[...], gfo_ref[...]).astype(BF16)
        y_ref[:, 512:] = _rms(oc_ref[...], gco_ref[...]).astype(BF16)
        z = _nn(y_ref[...], wo_ref[...])
        z_ref[...] = z
        x1 = x_ref[...] + _rms(z, gp_ref[...])
        x1_ref[...] = x1
        h2 = _rms(x1, gm_ref[...]).astype(BF16)
        h2_ref[...] = h2
        qm_ref[...] = _nn(h2, wq_ref[...]).astype(BF16)

    return _tok_call(body, "postmix_fwd", [x, o_f, o_c], [g_fo, g_co, w_out, g_post, g_mpre, w_mq],
                     [(D, BF16), (D, F32), (D, F32), (D, BF16), (D, BF16)], vmem=VMEM_BIG)


def _memkv_fwd(mem, g_kv, w_mk, w_mv):
    def body(m_ref, g_ref, wk_ref, wv_ref, mn_ref, k_ref, v_ref):
        mn = _rms(m_ref[...], g_ref[...]).astype(BF16)
        mn_ref[...] = mn
        k_ref[...] = _nn(mn, wk_ref[...]).astype(BF16)
        v_ref[...] = _nn(mn, wv_ref[...]).astype(BF16)

    return _tok_call(body, "memkv_fwd", [mem], [g_kv, w_mk, w_mv],
                     [(D, BF16), (D, BF16), (D, BF16)], rows=NMEM, tm=NMEM, vmem=VMEM_BIG)


def _mem_fwd(qm, x1, km, vm, w_mo, g_post, g_fpre):
    def body(q_ref, x1_ref, k_ref, v_ref, wo_ref, gp_ref, gf_ref, om_ref, ym_ref, x2_ref, h3_ref):
        for h in range(MEM_HEADS):
            sl = slice(h * MEM_HD, (h + 1) * MEM_HD)
            s = _nt(q_ref[:, sl], k_ref[:, sl]) * MEM_SCALE
            p = jnp.exp(s - jnp.max(s, axis=-1, keepdims=True))
            p = p / jnp.sum(p, axis=-1, keepdims=True)
            om_ref[:, sl] = _nn(p.astype(BF16), v_ref[:, sl]).astype(BF16)
        ym = _nn(om_ref[...], wo_ref[...])
        ym_ref[...] = ym
        x2 = x1_ref[...] + _rms(ym, gp_ref[...])
        x2_ref[...] = x2
        h3_ref[...] = _rms(x2, gf_ref[...]).astype(BF16)

    return _tok_call(body, "mem_fwd", [qm, x1], [km, vm, w_mo, g_post, g_fpre],
                     [(D, BF16), (D, F32), (D, F32), (D, BF16)], vmem=VMEM_BIG)


def _ffn_fwd(h3, x2, tgt, w1_t, w2, g_post):
    def body(h_ref, x2_ref, t_ref, w1_ref, w2_ref, g_ref, a_ref, y_ref, dx_ref, loss_ref):
        @pl.when(pl.program_id(0) == 0)
        def _():
            loss_ref[...] = jnp.zeros_like(loss_ref)

        a = _nt(h_ref[...], w1_ref[...])
        a_ref[...] = a.astype(BF16)
        r = jnp.square(jnp.maximum(a, 0.0)).astype(BF16)
        y = _nn(r, w2_ref[...])
        y_ref[...] = y
        e = x2_ref[...] + _rms(y, g_ref[...]) - t_ref[...]
        dx_ref[...] = e * (1.0 / D)
        loss_ref[...] += 0.5 * jnp.sum(jnp.sum(e * e, axis=-1, keepdims=True) * (1.0 / D))

    return _tok_call(body, "ffn_fwd", [h3, x2, tgt], [w1_t, w2, g_post],
                     [(DFF, BF16), (D, F32), (D, F32)], [(8, 128)], vmem=VMEM_BIG)


def _tri(lower):
    r = lax.broadcasted_iota(jnp.int32, (128, 128), 0)
    c = lax.broadcasted_iota(jnp.int32, (128, 128), 1)
    return jnp.where(r >= c if lower else c >= r, 1.0, 0.0).astype(F32)


def _hdot(a, b):
    return jnp.dot(a, b, preferred_element_type=F32, precision=lax.Precision.HIGHEST)


def _gate_fwd(flog, b_pad):
    def body(f_ref, b_ref, c_ref):
        tri = _tri(True)

        def step(i, carry):
            rows = pl.ds(pl.multiple_of(i * 128, 128), 128)
            z = f_ref[rows, :] + b_ref[...]
            lf = jnp.minimum(z, 0.0) - jnp.log(1.0 + jnp.exp(-jnp.abs(z)))
            cb = _hdot(tri, lf) + carry
            c_ref[rows, :] = cb
            return cb[127:128, :]

        lax.fori_loop(0, T // 128, step, jnp.zeros((1, 128), F32))

    return _one_call(body, "gate_fwd", [flog, b_pad], [((T, 128), F32)])[0]


def _gate_bwd(dc, flog, b_pad):
    def body(dc_ref, f_ref, b_ref, df_ref, db_ref):
        tri = _tri(False)

        def step(j, carry):
            run, db = carry
            i = T // 128 - 1 - j
            rows = pl.ds(pl.multiple_of(i * 128, 128), 128)
            dcb = dc_ref[rows, :]
            rb = _hdot(tri, dcb) + run
            z = f_ref[rows, :] + b_ref[...]
            df = rb * (1.0 / (1.0 + jnp.exp(z)))
            df_ref[rows, :] = df
            return run + jnp.sum(dcb, axis=0, keepdims=True), db + jnp.sum(df, axis=0, keepdims=True)

        _, db = lax.fori_loop(0, T // 128, step, (jnp.zeros((1, 128), F32), jnp.zeros((1, 128), F32)))
        db_ref[...] = jnp.broadcast_to(db, (8, 128))

    return _one_call(body, "gate_bwd", [dc, flog, b_pad], [((T, 128), F32), ((8, 128), F32)])


def _lane_lo():
    return lax.broadcasted_iota(jnp.int32, (TQ, 128), 1) < HD


def _half(v, lo, a):
    keep = lo if a == 0 else jnp.logical_not(lo)
    return jnp.where(keep, v.astype(F32), 0.0).astype(BF16)


def _attn_specs(qcol, kcol, vcol):
    return [pl.BlockSpec((TQ, 128), lambda h, i: (i, qcol + h)),
            pl.BlockSpec((T, 128), lambda h, i: (0, kcol + h)),
            pl.BlockSpec((T, 128), lambda h, i: (0, vcol + h))]


def _fox_fwd(proj, c2, ct3):
    def body(q_ref, k_ref, v_ref, c_ref, ct_ref, o_ref, l_ref):
        i = pl.program_id(1)
        lo = _lane_lo()
        row = i * TQ + lax.broadcasted_iota(jnp.int32, (TQ, TK), 0)
        col = lax.broadcasted_iota(jnp.int32, (TQ, TK), 1)
        q = q_ref[...]
        outs = []
        for a in range(2):
            qa = _half(q, lo, a)
            cq = c_ref[:, 128 * a:128 * a + 1]

            def step(kb, carry, a=a, qa=qa, cq=cq):
                m, l, acc = carry
                off = pl.multiple_of(kb * TK, TK)
                s = _nt(qa, k_ref[pl.ds(off, TK), :]) * SCALE + (cq - ct_ref[a:a + 1, pl.ds(off, TK)])
                s = jnp.where(col + off <= row, s, NEG)
                m2 = jnp.maximum(m, jnp.max(s, axis=-1, keepdims=True))
                p = jnp.exp(s - m2)
                alpha = jnp.exp(m - m2)
                l2 = alpha * l + jnp.sum(p, axis=-1, keepdims=True)
                acc2 = alpha * acc + _nn(p.astype(BF16), v_ref[pl.ds(off, TK), :])
                return m2, l2, acc2

            init = (jnp.full((TQ, 1), NEG, F32), jnp.zeros((TQ, 1), F32), jnp.zeros((TQ, 128), F32))
            m, l, acc = lax.fori_loop(0, i + 1, step, init)
            outs.append(acc / l)
            l_ref[:, 128 * a:128 * a + 128] = jnp.broadcast_to(m + jnp.log(l), (TQ, 128))
        o_ref[...] = jnp.where(lo, outs[0], outs[1])

    return _pcall(
        body, name="fox_fwd", grid=(4, T // TQ),
        in_specs=_attn_specs(0, 4, 8) + [pl.BlockSpec((TQ, 256), lambda h, i: (i, h)),
                                         pl.BlockSpec((None, 2, T), lambda h, i: (h, 0, 0))],
        out_specs=[pl.BlockSpec((TQ, 128), lambda h, i: (i, h)), pl.BlockSpec((TQ, 256), lambda h, i: (i, h))],
        out_shape=[jax.ShapeDtypeStruct((T, 512), F32), jax.ShapeDtypeStruct((T, 1024), F32)],
        compiler_params=pltpu.CompilerParams(dimension_semantics=("arbitrary", "arbitrary")),
    )(proj, proj, proj, c2, ct3)


def _fox_bwd(proj, c2, ct3, o, lse, do):
    def body(q_ref, k_ref, v_ref, c_ref, ct_ref, o_ref, l_ref, do_ref, dq_ref, dk_ref, dv_ref, dct_ref, dcq_ref):
        i = pl.program_id(1)

        @pl.when(i == 0)
        def _():
            dk_ref[...] = jnp.zeros_like(dk_ref)
            dv_ref[...] = jnp.zeros_like(dv_ref)
            dct_ref[...] = jnp.zeros_like(dct_ref)

        lo = _lane_lo()
        row = i * TQ + lax.broadcasted_iota(jnp.int32, (TQ, TK), 0)
        col = lax.broadcasted_iota(jnp.int32, (TQ, TK), 1)
        q = q_ref[...]
        do_v = do_ref[...]
        prod = do_v * o_ref[...]
        dqs = []
        for a in range(2):
            keep = lo if a == 0 else jnp.logical_not(lo)
            qa = _half(q, lo, a)
            doa = _half(do_v, lo, a)
            delta = jnp.sum(jnp.where(keep, prod, 0.0), axis=-1, keepdims=True)
            cq = c_ref[:, 128 * a:128 * a + 1]
            la = l_ref[:, 128 * a:128 * a + 1]

            def step(kb, carry, a=a, qa=qa, doa=doa, delta=delta, cq=cq, la=la):
                dq_acc, rs = carry
                off = pl.multiple_of(kb * TK, TK)
                kblk = k_ref[pl.ds(off, TK), :]
                s = _nt(qa, kblk) * SCALE + (cq - ct_ref[a:a + 1, pl.ds(off, TK)])
                p = jnp.exp(jnp.where(col + off <= row, s, NEG) - la)
                dp = _nt(doa, v_ref[pl.ds(off, TK), :])
                ds = p * (dp - delta)
                dsb = ds.astype(BF16)
                dk_ref[pl.ds(off, TK), :] += _tn(dsb, qa) * SCALE
                dv_ref[pl.ds(off, TK), :] += _tn(p.astype(BF16), doa)
                dct_ref[a:a + 1, pl.ds(off, TK)] -= jnp.sum(ds, axis=0, keepdims=True)
                return dq_acc + _nn(dsb, kblk), rs + jnp.sum(ds, axis=-1, keepdims=True)

            dq_a, rs = lax.fori_loop(0, i + 1, step, (jnp.zeros((TQ, 128), F32), jnp.zeros((TQ, 1), F32)))
            dqs.append(dq_a)
            dcq_ref[:, 128 * a:128 * a + 128] = jnp.broadcast_to(rs, (TQ, 128))
        dq_ref[...] = jnp.where(lo, dqs[0], dqs[1]) * SCALE

    blk = pl.BlockSpec((TQ, 128), lambda h, i: (i, h))
    wide = pl.BlockSpec((TQ, 256), lambda h, i: (i, h))
    rows = pl.BlockSpec((None, 2, T), lambda h, i: (h, 0, 0))
    col = pl.BlockSpec((T, 128), lambda h, i: (0, h))
    return _pcall(
        body, name="fox_bwd", grid=(4, T // TQ),
        in_specs=_attn_specs(0, 4, 8) + [wide, rows, blk, wide, blk],
        out_specs=[blk, col, col, rows, wide],
        out_shape=[jax.ShapeDtypeStruct((T, 512), F32), jax.ShapeDtypeStruct((T, 512), F32),
                   jax.ShapeDtypeStruct((T, 512), F32), jax.ShapeDtypeStruct((4, 2, T), F32),
                   jax.ShapeDtypeStruct((T, 1024), F32)],
        compiler_params=pltpu.CompilerParams(dimension_semantics=("arbitrary", "arbitrary")),
    )(proj, proj, proj, c2, ct3, o, lse, do)


def _rel_onehot():
    ridx = lax.broadcasted_iota(jnp.int32, (NREL_PAD, VW), 0)
    j = lax.broadcasted_iota(jnp.int32, (NREL_PAD, VW), 1)
    return jnp.where(ridx == jnp.clip(TQ + LEFT - 1 - j, -128, 128) + 128, 1.0, 0.0).astype(F32)


def _relvec_fwd(tbl):
    def body(t_ref, v_ref):
        v_ref[...] = _hdot(t_ref[...], _rel_onehot())

    return _one_call(body, "relvec_fwd", [tbl], [((8, VW), F32)])[0]


def _relvec_bwd(gv):
    def body(g_ref, t_ref):
        t_ref[...] = lax.dot_general(g_ref[...], _rel_onehot(), (((1,), (1,)), ((), ())),
                                     preferred_element_type=F32, precision=lax.Precision.HIGHEST)

    return _one_call(body, "relvec_bwd", [gv], [((8, NREL_PAD), F32)])[0]


def _chk_bias(vt_ref, a):
    vb = jnp.broadcast_to(vt_ref[a:a + 1, :], (TQ, VW))
    y = pltpu.roll(vb, VW - (TQ - 1), 1, stride=1, stride_axis=0)[:, :WIN]
    cr = lax.broadcasted_iota(jnp.int32, (TQ, WIN), 0) // 64
    cm = lax.broadcasted_iota(jnp.int32, (TQ, WIN), 1) // 64
    return jnp.where((cm >= cr) & (cm <= cr + 8), y, NEG)


def _chk_specs():
    return [pl.BlockSpec((TQ, 128), lambda h, i: (i, CHK0 // 128 + h)),
            pl.BlockSpec((T + LEFT, 128), lambda h, i: (0, h)),
            pl.BlockSpec((T + LEFT, 128), lambda h, i: (0, 4 + h)),
            pl.BlockSpec((None, 2, VW), lambda h, i: (h, 0, 0))]


def _chk_fwd(proj, kvp, vt3):
    def body(q_ref, k_ref, v_ref, vt_ref, o_ref, l_ref, bias_ref):
        i = pl.program_id(1)

        @pl.when(i == 0)
        def _():
            for a in range(2):
                bias_ref[a] = _chk_bias(vt_ref, a)

        lo = _lane_lo()
        off = pl.multiple_of(i * TQ, TQ)
        kw = k_ref[pl.ds(off, WIN), :]
        vw = v_ref[pl.ds(off, WIN), :]
        real = lax.broadcasted_iota(jnp.int32, (TQ, WIN), 1) + off >= LEFT
        q = q_ref[...]
        outs = []
        for a in range(2):
            s = jnp.where(real, _nt(_half(q, lo, a), kw) * SCALE + bias_ref[a], NEG)
            m = jnp.max(s, axis=-1, keepdims=True)
            p = jnp.exp(s - m)
            l = jnp.sum(p, axis=-1, keepdims=True)
            outs.append(_nn(p.astype(BF16), vw) / l)
            l_ref[:, 128 * a:128 * a + 128] = jnp.broadcast_to(m + jnp.log(l), (TQ, 128))
        o_ref[...] = jnp.where(lo, outs[0], outs[1])

    return _pcall(
        body, name="chk_fwd", grid=(4, T // TQ), in_specs=_chk_specs(),
        out_specs=[pl.BlockSpec((TQ, 128), lambda h, i: (i, h)), pl.BlockSpec((TQ, 256), lambda h, i: (i, h))],
        out_shape=[jax.ShapeDtypeStruct((T, 512), F32), jax.ShapeDtypeStruct((T, 1024), F32)],
        scratch_shapes=[pltpu.VMEM((2, TQ, WIN), F32)],
        compiler_params=pltpu.CompilerParams(dimension_semantics=("arbitrary", "arbitrary")),
    )(proj, kvp, kvp, vt3)


def _chk_bwd(proj, kvp, vt3, o, lse, do):
    nq = T // TQ

    def body(q_ref, k_ref, v_ref, vt_ref, o_ref, l_ref, do_ref, dq_ref, dk_ref, dv_ref, gv_ref, bias_ref, dsum_ref):
        i = pl.program_id(1)

        @pl.when(i == 0)
        def _():
            for a in range(2):
                bias_ref[a] = _chk_bias(vt_ref, a)
            dsum_ref[...] = jnp.zeros_like(dsum_ref)
            dk_ref[...] = jnp.zeros_like(dk_ref)
            dv_ref[...] = jnp.zeros_like(dv_ref)

        lo = _lane_lo()
        off = pl.multiple_of(i * TQ, TQ)
        kw = k_ref[pl.ds(off, WIN), :]
        vw = v_ref[pl.ds(off, WIN), :]
        real = lax.broadcasted_iota(jnp.int32, (TQ, WIN), 1) + off >= LEFT
        q = q_ref[...]
        do_v = do_ref[...]
        prod = do_v * o_ref[...]
        dqs = []
        for a in range(2):
            keep = lo if a == 0 else jnp.logical_not(lo)
            qa = _half(q, lo, a)
            doa = _half(do_v, lo, a)
            delta = jnp.sum(jnp.where(keep, prod, 0.0), axis=-1, keepdims=True)
            s = jnp.where(real, _nt(qa, kw) * SCALE + bias_ref[a], NEG)
            p = jnp.exp(s - l_ref[:, 128 * a:128 * a + 1])
            ds = p * (_nt(doa, vw) - delta)
            dsum_ref[a] += ds
            dsb = ds.astype(BF16)
            dk_ref[pl.ds(off, WIN), :] += _tn(dsb, qa) * SCALE
            dv_ref[pl.ds(off, WIN), :] += _tn(p.astype(BF16), doa)
            dqs.append(_nn(dsb, kw))
        dq_ref[...] = jnp.where(lo, dqs[0], dqs[1]) * SCALE

        @pl.when(i == nq - 1)
        def _():
            rr = lax.broadcasted_iota(jnp.int32, (TQ, TQ), 0)
            cc = lax.broadcasted_iota(jnp.int32, (TQ, TQ), 1)
            flip = jnp.where(rr + cc == TQ - 1, 1.0, 0.0).astype(F32)
            for a in range(2):
                dpad = jnp.concatenate([dsum_ref[a], jnp.zeros((TQ, VW - WIN), F32)], axis=1)
                z = pltpu.roll(_hdot(flip, dpad), 0, 1, stride=1, stride_axis=0)
                gv_ref[a:a + 1, :] = jnp.sum(z, axis=0, keepdims=True)

    blk = pl.BlockSpec((TQ, 128), lambda h, i: (i, h))
    wide = pl.BlockSpec((TQ, 256), lambda h, i: (i, h))
    col = pl.BlockSpec((T + LEFT, 128), lambda h, i: (0, h))
    return _pcall(
        body, name="chk_bwd", grid=(4, nq), in_specs=_chk_specs() + [blk, wide, blk],
        out_specs=[blk, col, col, pl.BlockSpec((None, 2, VW), lambda h, i: (h, 0, 0))],
        out_shape=[jax.ShapeDtypeStruct((T, 512), F32), jax.ShapeDtypeStruct((T + LEFT, 512), F32),
                   jax.ShapeDtypeStruct((T + LEFT, 512), F32), jax.ShapeDtypeStruct((4, 2, VW), F32)],
        scratch_shapes=[pltpu.VMEM((2, TQ, WIN), F32), pltpu.VMEM((2, TQ, WIN), F32)],
        compiler_params=pltpu.CompilerParams(dimension_semantics=("arbitrary", "arbitrary")),
    )(proj, kvp, kvp, vt3, o, lse, do)


def _zero_at_start(*refs):
    @pl.when(pl.program_id(0) == 0)
    def _():
        for r in refs:
            r[...] = jnp.zeros_like(r)


def _ffn_bwd(dx3, y3, x2, a, w1_t, w2, g_post, g_pre):
    def body(dx3_ref, y_ref, x2_ref, a_ref, w1_ref, w2_ref, gp_ref, gf_ref,
             dx2_ref, da_ref, dy_ref, r_ref, dgp_ref, dgf_ref):
        _zero_at_start(dgp_ref, dgf_ref)
        dx3_v = dx3_ref[...]
        dy, dgp = _rms_bwd(y_ref[...], gp_ref[...], dx3_v)
        dgp_ref[...] += dgp
        dyb = dy.astype(BF16)
        dy_ref[...] = dyb
        ra = jnp.maximum(a_ref[...].astype(F32), 0.0)
        r_ref[...] = jnp.square(ra).astype(BF16)
        da = (_nt(dyb, w2_ref[...]) * (2.0 * ra)).astype(BF16)
        da_ref[...] = da
        dh, dgf = _rms_bwd(x2_ref[...], gf_ref[...], _nn(da, w1_ref[...]))
        dgf_ref[...] += dgf
        dx2_ref[...] = dx3_v + dh

    return _tok_call(body, "ffn_bwd", [dx3, y3, x2, a], [w1_t, w2, g_post, g_pre],
                     [(D, F32), (DFF, BF16), (D, BF16), (DFF, BF16)], [(1, D), (1, D)], vmem=VMEM_BIG)


def _mem_bwd(dx2, ym, x1, qm, km, vm, w_mo, w_mq, g_post, g_pre):
    def body(dx2_ref, ym_ref, x1_ref, q_ref, k_ref, v_ref, wo_ref, wq_ref, gp_ref, gm_ref,
             dx1_ref, dym_ref, dq_ref, dk_ref, dv_ref, dgp_ref, dgm_ref, dom_ref):
        _zero_at_start(dk_ref, dv_ref, dgp_ref, dgm_ref)
        dx2_v = dx2_ref[...]
        dym, dgp = _rms_bwd(ym_ref[...], gp_ref[...], dx2_v)
        dgp_ref[...] += dgp
        dymb = dym.astype(BF16)
        dym_ref[...] = dymb
        dom_ref[...] = _nt(dymb, wo_ref[...]).astype(BF16)
        for h in range(MEM_HEADS):
            sl = slice(h * MEM_HD, (h + 1) * MEM_HD)
            qh, kh, doh = q_ref[:, sl], k_ref[:, sl], dom_ref[:, sl]
            s = _nt(qh, kh) * MEM_SCALE
            p = jnp.exp(s - jnp.max(s, axis=-1, keepdims=True))
            p = p / jnp.sum(p, axis=-1, keepdims=True)
            dp = _nt(doh, v_ref[:, sl])
            ds = (p * (dp - jnp.sum(p * dp, axis=-1, keepdims=True))).astype(BF16)
            dq_ref[:, sl] = (_nn(ds, kh) * MEM_SCALE).astype(BF16)
            dk_ref[:, sl] += _tn(ds, qh) * MEM_SCALE
            dv_ref[:, sl] += _tn(p.astype(BF16), doh)
        dh, dgm = _rms_bwd(x1_ref[...], gm_ref[...], _nt(dq_ref[...], wq_ref[...]))
        dgm_ref[...] += dgm
        dx1_ref[...] = dx2_v + dh

    in_specs = [pl.BlockSpec((TM, D), lambda i: (i, 0))] * 4
    in_specs += [pl.BlockSpec(a.shape, lambda i, nd=a.ndim: (0,) * nd, pipeline_mode=pl.Buffered(1))
                 for a in (km, vm, w_mo, w_mq, g_post, g_pre)]
    tiled = pl.BlockSpec((TM, D), lambda i: (i, 0))
    kv = pl.BlockSpec((NMEM, D), lambda i: (0, 0))
    vec = pl.BlockSpec((1, D), lambda i: (0, 0))
    return _pcall(
        body, name="mem_bwd", grid=(T // TM,), in_specs=in_specs,
        out_specs=[tiled, tiled, tiled, kv, kv, vec, vec],
        out_shape=[jax.ShapeDtypeStruct((T, D), F32), jax.ShapeDtypeStruct((T, D), BF16),
                   jax.ShapeDtypeStruct((T, D), BF16), jax.ShapeDtypeStruct((NMEM, D), F32),
                   jax.ShapeDtypeStruct((NMEM, D), F32), jax.ShapeDtypeStruct((1, D), F32),
                   jax.ShapeDtypeStruct((1, D), F32)],
        scratch_shapes=[pltpu.VMEM((TM, D), BF16)],
        compiler_params=pltpu.CompilerParams(dimension_semantics=("arbitrary",), vmem_limit_bytes=VMEM_BIG),
    )(dx2, ym, x1, qm, km, vm, w_mo, w_mq, g_post, g_pre)


def _memkv_bwd(dkm, dvm, mem, w_mk, w_mv):
    def body(dk_ref, dv_ref, m_ref, wk_ref, wv_ref, dg_ref):
        dmn = _nt(dk_ref[...].astype(BF16), wk_ref[...]) + _nt(dv_ref[...].astype(BF16), wv_ref[...])
        mv = m_ref[...]
        dg_ref[...] = jnp.sum(dmn * (mv * _rstd(mv)), axis=0, keepdims=True)

    return _one_call(body, "memkv_bwd", [dkm, dvm, mem, w_mk, w_mv], [((1, D), F32)], vmem=VMEM_BIG)[0]


def _postmix_bwd(dx1, z, o_f, o_c, w_out, g_post, g_fo, g_co):
    def body(dx1_ref, z_ref, of_ref, oc_ref, wo_ref, gp_ref, gfo_ref, gco_ref,
             dz_ref, dof_ref, doc_ref, dgp_ref, dgfo_ref, dgco_ref):
        _zero_at_start(dgp_ref, dgfo_ref, dgco_ref)
        dz, dgp = _rms_bwd(z_ref[...], gp_ref[...], dx1_ref[...])
        dgp_ref[...] += dgp
        dzb = dz.astype(BF16)
        dz_ref[...] = dzb
        dy = _nt(dzb, wo_ref[...])
        dof, dgfo = _rms_bwd(of_ref[...], gfo_ref[...], dy[:, :512])
        doc, dgco = _rms_bwd(oc_ref[...], gco_ref[...], dy[:, 512:])
        dof_ref[...] = dof
        doc_ref[...] = doc
        dgfo_ref[...] += dgfo
        dgco_ref[...] += dgco

    return _tok_call(body, "postmix_bwd", [dx1, z, o_f, o_c], [w_out, g_post, g_fo, g_co],
                     [(D, BF16), (512, F32), (512, F32)], [(1, D), (1, 512), (1, 512)], vmem=VMEM_BIG)


def _premix_bwd(dx1, x, dproj, win_t, g_pre):
    def body(dx1_ref, x_ref, dp_ref, w_ref, g_ref, dx_ref, dg_ref):
        _zero_at_start(dg_ref)
        dh, dg = _rms_bwd(x_ref[...], g_ref[...], _nn(dp_ref[...], w_ref[...]))
        dg_ref[...] += dg
        dx_ref[...] = dx1_ref[...] + dh

    return _tok_call(body, "premix_bwd", [dx1, x, dproj], [win_t, g_pre], [(D, F32)], [(1, D)], vmem=VMEM_BIG)


def _wgrad(a, b, name):
    k, m = a.shape
    n = b.shape[1]
    tm = 640 if m % 640 == 0 and m > 1024 else min(m, 512)
    tn = min(n, 1024)

    def body(a_ref, b_ref, o_ref):
        o_ref[...] = _tn(a_ref[...].astype(BF16), b_ref[...].astype(BF16))

    return _pcall(
        body, name=name, grid=(m // tm, n // tn),
        in_specs=[pl.BlockSpec((k, tm), lambda i, j: (0, i)), pl.BlockSpec((k, tn), lambda i, j: (0, j))],
        out_specs=pl.BlockSpec((tm, tn), lambda i, j: (i, j)),
        out_shape=jax.ShapeDtypeStruct((m, n), F32),
        compiler_params=pltpu.CompilerParams(dimension_semantics=("arbitrary", "arbitrary"), vmem_limit_bytes=VMEM_BIG),
    )(a, b)


def _adam_math(w, g, m, v):
    m2 = ADAM_B1 * m + (1.0 - ADAM_B1) * g
    v2 = ADAM_B2 * v + (1.0 - ADAM_B2) * jnp.square(g)
    m_hat = m2 / (1.0 - ADAM_B1 ** ADAM_STEP)
    v_hat = v2 / (1.0 - ADAM_B2 ** ADAM_STEP)
    delta = -ADAM_LR * (m_hat / (jnp.sqrt(v_hat) + ADAM_EPS) + ADAM_WD * w)
    return delta, m2, v2


def _adamw(w, g, m, v, name):
    rows, cols = w.shape
    tr = 256 if rows % 256 == 0 else rows

    def body(w_ref, g_ref, m_ref, v_ref, d_ref, m2_ref, v2_ref):
        d_ref[...], m2_ref[...], v2_ref[...] = _adam_math(w_ref[...], g_ref[...], m_ref[...], v_ref[...])

    spec = pl.BlockSpec((tr, cols), lambda i: (i, 0))
    return _pcall(
        body, name=name, grid=(rows // tr,), in_specs=[spec] * 4, out_specs=[spec] * 3,
        out_shape=[jax.ShapeDtypeStruct(w.shape, F32)] * 3,
        compiler_params=pltpu.CompilerParams(dimension_semantics=("arbitrary",)),
    )(w, g, m, v)


def _adamw_small(w, gparts, m, v):
    def body(w_ref, g_ref, m_ref, v_ref, gs_ref, d_ref, m2_ref, v2_ref):
        g = g_ref[0]
        for k in range(1, 8):
            g = g + g_ref[k]
        gs_ref[...] = g
        d_ref[...], m2_ref[...], v2_ref[...] = _adam_math(w_ref[...], g, m_ref[...], v_ref[...])

    return _one_call(body, "adamw_small", [w, gparts, m, v], [(w.shape, F32)] * 4)


def _add_halves(mine, theirs):
    tr = 688

    def body(a_ref, b_ref, o_ref):
        o_ref[...] = (a_ref[...] + b_ref[...]).astype(BF16)

    spec = pl.BlockSpec((None, tr, D), lambda j, i: (j, i, 0))
    return _pcall(
        body, name="rs_add_halves", grid=(4, R_ALL // tr), in_specs=[spec, spec], out_specs=spec,
        out_shape=jax.ShapeDtypeStruct((4, R_ALL, D), BF16),
        compiler_params=pltpu.CompilerParams(dimension_semantics=("arbitrary", "arbitrary")),
    )(mine, theirs)


def _sum_chips(parts):
    tr = 688

    def body(p_ref, o_ref):
        o_ref[...] = ((p_ref[0].astype(F32) + p_ref[1].astype(F32)) + p_ref[2].astype(F32)) + p_ref[3].astype(F32)

    return _pcall(
        body, name="rs_sum_chips", grid=(R_ALL // tr,),
        in_specs=[pl.BlockSpec((4, tr, D), lambda i: (0, i, 0))], out_specs=pl.BlockSpec((tr, D), lambda i: (i, 0)),
        out_shape=jax.ShapeDtypeStruct((R_ALL, D), F32),
        compiler_params=pltpu.CompilerParams(dimension_semantics=("arbitrary",)),
    )(parts)


def _place():
    return lax.axis_index("x"), lax.axis_index("y"), lax.axis_index("c")


def _allgather(block, name):
    def body(x_ref, out_ref, send_sems, recv_sems, local_sem):
        x, y, c = _place()
        me, sibling = (x, y, c), (x, y, 1 - c)
        chips = [(1 - x, y), (x, 1 - y), (1 - x, 1 - y)]

        def slot(px, py, pc):
            return out_ref.at[4 * px + 2 * py + pc]

        def copy(k, blk, to, src=None):
            return pltpu.make_async_remote_copy(
                src_ref=slot(*blk) if src is None else src, dst_ref=slot(*blk),
                send_sem=send_sems.at[k], recv_sem=recv_sems.at[k], device_id=to, device_id_type=MESH)

        mine = pltpu.make_async_copy(x_ref, slot(*me), local_sem)
        mine.start()
        first = [copy(0, me, sibling, src=x_ref)]
        first += [copy(1 + j, me, (*chip, c), src=x_ref) for j, chip in enumerate(chips)]
        for cp in first:
            cp.start()
        passed = [copy(4 + j, (*chip, c), sibling) for j, chip in enumerate(chips)]
        for j, chip in enumerate(chips):
            copy(1 + j, (*chip, c), me).wait_recv()
            passed[j].start()
        copy(0, sibling, me).wait_recv()
        for j, chip in enumerate(chips):
            copy(4 + j, (*chip, 1 - c), me).wait_recv()
        for cp in first + passed:
            cp.wait_send()
        mine.wait()

    return _pcall(
        body, name=name, out_shape=jax.ShapeDtypeStruct((8,) + block.shape, block.dtype),
        in_specs=[pl.BlockSpec(memory_space=pl.ANY)], out_specs=pl.BlockSpec(memory_space=pl.ANY),
        scratch_shapes=[pltpu.SemaphoreType.DMA((7,)), pltpu.SemaphoreType.DMA((7,)), pltpu.SemaphoreType.DMA(())],
        compiler_params=pltpu.CompilerParams(has_side_effects=True),
    )(block)


def _swap_sibling(v):
    def body(v_ref, got_ref, send_sem, recv_sem):
        x, y, c = _place()
        cp = pltpu.make_async_remote_copy(src_ref=v_ref, dst_ref=got_ref, send_sem=send_sem, recv_sem=recv_sem,
                                          device_id=(x, y, 1 - c), device_id_type=MESH)
        cp.start()
        cp.wait()

    return _pcall(
        body, name="rs_swap_sibling", out_shape=jax.ShapeDtypeStruct(v.shape, v.dtype),
        in_specs=[pl.BlockSpec(memory_space=pl.ANY)], out_specs=pl.BlockSpec(memory_space=pl.ANY),
        scratch_shapes=[pltpu.SemaphoreType.DMA(()), pltpu.SemaphoreType.DMA(())],
        compiler_params=pltpu.CompilerParams(has_side_effects=True),
    )(v)


def _exchange_chips(h):
    def body(h_ref, out_ref, send_sems, recv_sems, local_sem):
        x, y, c = _place()
        mychip = 2 * x + y
        chips = [(1 - x, y), (x, 1 - y), (1 - x, 1 - y)]
        mine = pltpu.make_async_copy(h_ref.at[mychip], out_ref.at[mychip], local_sem)
        mine.start()
        cps = []
        for k, (px, py) in enumerate(chips):
            cps.append(pltpu.make_async_remote_copy(
                src_ref=h_ref.at[2 * px + py], dst_ref=out_ref.at[mychip],
                send_sem=send_sems.at[k], recv_sem=recv_sems.at[k], device_id=(px, py, c), device_id_type=MESH))
            cps[-1].start()
        for k, (px, py) in enumerate(chips):
            pltpu.make_async_remote_copy(
                src_ref=h_ref.at[mychip], dst_ref=out_ref.at[2 * px + py],
                send_sem=send_sems.at[k], recv_sem=recv_sems.at[k], device_id=(px, py, c), device_id_type=MESH).wait_recv()
        for cp in cps:
            cp.wait_send()
        mine.wait()

    return _pcall(
        body, name="rs_exchange_chips", out_shape=jax.ShapeDtypeStruct(h.shape, h.dtype),
        in_specs=[pl.BlockSpec(memory_space=pl.ANY)], out_specs=pl.BlockSpec(memory_space=pl.ANY),
        scratch_shapes=[pltpu.SemaphoreType.DMA((3,)), pltpu.SemaphoreType.DMA((3,)), pltpu.SemaphoreType.DMA(())],
        compiler_params=pltpu.CompilerParams(has_side_effects=True),
    )(h)


def _pack_small(p):
    z = lambda a, n: jnp.pad(a, ((0, 0), (0, n - a.shape[1])))
    rows = [z(p['rel_bias'], D), z(p['b_fgt'], D), jnp.concatenate([p['g_fox_out'], p['g_chk_out']], axis=1)]
    rows += [p[n] for n in ('g_mix_pre', 'g_mix_post', 'g_mem_kv', 'g_mem_pre', 'g_mem_post', 'g_ff_pre', 'g_ff_post')]
    rows.append(jnp.zeros((SMALL_ROWS - 17, D), F32))
    return jnp.concatenate(rows, axis=0)


def _unpack_small(a):
    out = {'rel_bias': a[0:8, :257], 'b_fgt': a[8:9, :8], 'g_fox_out': a[9:10, :512], 'g_chk_out': a[9:10, 512:]}
    for k, n in enumerate(('g_mix_pre', 'g_mix_post', 'g_mem_kv', 'g_mem_pre', 'g_mem_post', 'g_ff_pre', 'g_ff_post')):
        out[n] = a[10 + k:11 + k]
    return out


def _local_grads(x, mem, tgt, wts, sm):
    win_t, w_out, w_mq, w_mk, w_mv, w_mo, w1_t, w2 = (wts[n] for n in BIG)
    b_pad = jnp.pad(sm['b_fgt'], ((0, 0), (0, 120)))
    tbl = jnp.pad(sm['rel_bias'], ((0, 0), (0, NREL_PAD - 257)))

    h1, proj, flog = _premix_fwd(x, sm['g_mix_pre'], win_t)
    c = _gate_fwd(flog, b_pad)
    c8 = c[:, :8]
    c2 = jnp.repeat(c8, 128, axis=1)
    ct3 = c8.T.reshape(4, 2, T)
    o_f, lse_f = _fox_fwd(proj, c2, ct3)
    vt3 = _relvec_fwd(tbl).reshape(4, 2, VW)
    kvp = jnp.pad(proj[:, CHK0 + 512:], ((LEFT, 0), (0, 0)))
    o_c, lse_c = _chk_fwd(proj, kvp, vt3)
    ycat, z, x1, h2, qm = _postmix_fwd(x, o_f, o_c, sm['g_fox_out'], sm['g_chk_out'], w_out,
                                       sm['g_mix_post'], sm['g_mem_pre'], w_mq)
    memn, km, vm = _memkv_fwd(mem, sm['g_mem_kv'], w_mk, w_mv)
    om, ym, x2, h3 = _mem_fwd(qm, x1, km, vm, w_mo, sm['g_mem_post'], sm['g_ff_pre'])
    a, y3, dx3, loss_acc = _ffn_fwd(h3, x2, tgt, w1_t, w2, sm['g_ff_post'])

    gs = {}
    dx2, da, dy3, r, gs['g_ff_post'], gs['g_ff_pre'] = _ffn_bwd(dx3, y3, x2, a, w1_t, w2, sm['g_ff_post'], sm['g_ff_pre'])
    dx1, dym, dqm, dkm, dvm, gs['g_mem_post'], gs['g_mem_pre'] = _mem_bwd(
        dx2, ym, x1, qm, km, vm, w_mo, w_mq, sm['g_mem_post'], sm['g_mem_pre'])
    gs['g_mem_kv'] = _memkv_bwd(dkm, dvm, mem, w_mk, w_mv)
    dz, dof, doc, gs['g_mix_post'], gs['g_fox_out'], gs['g_chk_out'] = _postmix_bwd(
        dx1, z, o_f, o_c, w_out, sm['g_mix_post'], sm['g_fox_out'], sm['g_chk_out'])
    dq_f, dk_f, dv_f, dct, dcq = _fox_bwd(proj, c2, ct3, o_f, lse_f, dof)
    dq_c, dkp, dvp, gv = _chk_bwd(proj, kvp, vt3, o_c, lse_c, doc)
    gs['rel_bias'] = _relvec_bwd(gv.reshape(8, VW))[:, :257]
    dc = jnp.pad(dct.reshape(8, T).T + dcq[:, ::128], ((0, 0), (0, 120)))
    dflog, db = _gate_bwd(dc, flog, b_pad)
    gs['b_fgt'] = db[0:1, :8]
    dproj = jnp.concatenate([dq_f, dk_f, dv_f, dflog, dq_c, dkp[LEFT:], dvp[LEFT:]], axis=1).astype(BF16)
    grad_x, gs['g_mix_pre'] = _premix_bwd(dx1, x, dproj, win_t, sm['g_mix_pre'])

    gb = {
        'w_in': _wgrad(dproj, h1, "wgrad_in"),
        'w_out': _wgrad(ycat, dz, "wgrad_out"),
        'w_mq': _wgrad(h2, dqm, "wgrad_mq"),
        'w_mk': _wgrad(memn, dkm, "wgrad_mk"),
        'w_mv': _wgrad(memn, dvm, "wgrad_mv"),
        'w_mo': _wgrad(om, dym, "wgrad_mo"),
        'w_ff1': _wgrad(da, h3, "wgrad_ff1"),
        'w_ff2': _wgrad(r, dy3, "wgrad_ff2"),
    }
    return loss_acc[0, 0], grad_x, gb, gs


def _unpack_gathered(gw):
    wt = gw[:, R_IN:R_IN + N_IN].reshape(8 * N_IN, D)
    gate = jnp.pad(wt[GATE0:GATE0 + 8], ((0, 120), (0, 0)))
    full = lambda r0, r1: gw[:, r0:r1].reshape(8 * (r1 - r0), D)
    return {
        'w_in': jnp.concatenate([wt[:GATE0], gate, wt[GATE0 + 8:]], axis=0),
        'w_out': full(R_OUT, R_MQ), 'w_mq': full(R_MQ, R_MK), 'w_mk': full(R_MK, R_MV), 'w_mv': full(R_MV, R_MO),
        'w_mo': full(R_MO, R_FF1), 'w_ff1': full(R_FF1, R_FF2), 'w_ff2': full(R_FF2, R_ALL),
    }


def _pack_grads(gb):
    gin = jnp.concatenate([gb['w_in'][:GATE0 + 8], gb['w_in'][CHK0:]], axis=0).reshape(8, N_IN, D)
    parts = [jnp.pad(gin, ((0, 0), (0, R_OUT - N_IN), (0, 0)))]
    parts += [gb[n].reshape(8, -1, D) for n in BIG[1:]]
    return jnp.concatenate(parts, axis=1)


def kernel(x, mem, w_in, b_fgt, rel_bias, g_fox_out, g_chk_out, w_out, g_mix_pre, g_mix_post, g_mem_kv, w_mq, w_mk, w_mv, w_mo, g_mem_pre, g_mem_post, w_ff1, w_ff2, g_ff_pre, g_ff_post, loss_target, m_w_in, m_b_fgt, m_rel_bias, m_g_fox_out, m_g_chk_out, m_w_out, m_g_mix_pre, m_g_mix_post, m_g_mem_kv, m_w_mq, m_w_mk, m_w_mv, m_w_mo, m_g_mem_pre, m_g_mem_post, m_w_ff1, m_w_ff2, m_g_ff_pre, m_g_ff_post, v_w_in, v_b_fgt, v_rel_bias, v_g_fox_out, v_g_chk_out, v_w_out, v_g_mix_pre, v_g_mix_post, v_g_mem_kv, v_w_mq, v_w_mk, v_w_mv, v_w_mo, v_g_mem_pre, v_g_mem_post, v_w_ff1, v_w_ff2, v_g_ff_pre, v_g_ff_post):
    args = dict(locals())
    two_d = lambda a: a.reshape(a.shape[-2:])
    w = {n: two_d(args[n]) for n in WEIGHTS}
    m = {n: two_d(args['m_' + n]) for n in WEIGHTS}
    v = {n: two_d(args['v_' + n]) for n in WEIGHTS}

    shard = jnp.concatenate([jnp.pad(w['w_in'].T, ((0, R_OUT - N_IN), (0, 0))), w['w_out'], w['w_mq'], w['w_mk'],
                             w['w_mv'], w['w_mo'], w['w_ff1'].T, w['w_ff2']], axis=0).astype(BF16)
    wts = _unpack_gathered(_allgather(shard, "allgather_weights"))
    sm = {n: w[n] for n in SMALL}

    loss_local, grad_x, gb, gs = _local_grads(x[0], mem[0], loss_target[0], wts, sm)
    loss = lax.psum(loss_local, ("x", "y", "c"))

    g4 = _pack_grads(gb).reshape(4, 2, R_ALL, D)
    c = lax.axis_index("c")
    mine = lax.dynamic_index_in_dim(g4, c, axis=1, keepdims=False)
    theirs = lax.dynamic_index_in_dim(g4, 1 - c, axis=1, keepdims=False)
    gsh = _sum_chips(_exchange_chips(_add_halves(mine, _swap_sibling(theirs))))
    g_big = {
        'w_in': gsh[R_IN:R_IN + N_IN].T, 'w_out': gsh[R_OUT:R_MQ], 'w_mq': gsh[R_MQ:R_MK], 'w_mk': gsh[R_MK:R_MV],
        'w_mv': gsh[R_MV:R_MO], 'w_mo': gsh[R_MO:R_FF1], 'w_ff1': gsh[R_FF1:R_FF2].T, 'w_ff2': gsh[R_FF2:R_ALL],
    }

    grads, deltas, new_m, new_v = {}, {}, {}, {}
    for n in BIG:
        grads[n] = g_big[n]
        deltas[n], new_m[n], new_v[n] = _adamw(w[n], g_big[n], m[n], v[n], "adamw_" + n)

    gparts = _allgather(_pack_small(gs), "allgather_small_grads")
    gsum, d_s, m_s, v_s = _adamw_small(_pack_small(sm), gparts, _pack_small({n: m[n] for n in SMALL}),
                                       _pack_small({n: v[n] for n in SMALL}))
    for dst, packed in ((grads, gsum), (deltas, d_s), (new_m, m_s), (new_v, v_s)):
        dst.update(_unpack_small(packed))

    out = [loss, grad_x[None]]
    for group in (grads, deltas, new_m, new_v):
        out += [group[n].reshape(args[n].shape) for n in WEIGHTS]
    return tuple(out)
```

```python
import functools

import jax
import jax.numpy as jnp
from jax import lax
from jax.experimental import pallas as pl
from jax.experimental.pallas import tpu as pltpu

F32 = jnp.float32
BF16 = jnp.bfloat16
MESH = pl.DeviceIdType.MESH

T = 2048
D = 1024
NMEM = 256
DFF = 4096
EPS = 1e-6
TM = 256
TQ = 256
FQ = 512
HD = 64
SCALE = HD ** -0.5
MEM_HEADS = 4
MEM_HD = 256
MEM_SCALE = MEM_HD ** -0.5
NEG = -1e30
LEFT = 512
WIN = LEFT + TQ
VW = 1024
NREL_PAD = 384
PROJ = 3200
GATE0 = 1536
CHK0 = 1664
VMEM_BIG = 56 * 1024 * 1024

ADAM_LR = 0.001
ADAM_B1 = 0.9
ADAM_B2 = 0.999
ADAM_EPS = 1e-08
ADAM_WD = 0.01
ADAM_STEP = 10

R_IN, R_OUT, R_MQ, R_MK, R_MV, R_MO, R_FF1, R_FF2, R_ALL = 0, 400, 528, 656, 784, 912, 1040, 1552, 2064
N_IN = 385
SMALL_ROWS = 24

WEIGHTS = ['w_in', 'b_fgt', 'rel_bias', 'g_fox_out', 'g_chk_out', 'w_out', 'g_mix_pre', 'g_mix_post', 'g_mem_kv',
           'w_mq', 'w_mk', 'w_mv', 'w_mo', 'g_mem_pre', 'g_mem_post', 'w_ff1', 'w_ff2', 'g_ff_pre', 'g_ff_post']
BIG = ['w_in', 'w_out', 'w_mq', 'w_mk', 'w_mv', 'w_mo', 'w_ff1', 'w_ff2']
SMALL = [n for n in WEIGHTS if n not in BIG]


def _pcall(body, **kw):
    return pl.pallas_call(body, **kw)


def _nn(a, b):
    return jnp.dot(a, b, preferred_element_type=F32)


def _nt(a, b):
    return lax.dot_general(a, b, (((1,), (1,)), ((), ())), preferred_element_type=F32)


def _tn(a, b):
    return lax.dot_general(a, b, (((0,), (0,)), ((), ())), preferred_element_type=F32)


def _rstd(x):
    return lax.rsqrt(jnp.mean(x * x, axis=-1, keepdims=True) + EPS)


def _rms(x, g):
    return x * _rstd(x) * g


def _rms_bwd(x, g, dy):
    r = _rstd(x)
    xh = x * r
    dg = jnp.sum(dy * xh, axis=0, keepdims=True)
    dxh = dy * g
    dx = r * (dxh - xh * jnp.mean(dxh * xh, axis=-1, keepdims=True))
    return dx, dg


def _tok_call(body, name, tiled, full, outs_tiled, outs_acc=(), rows=T, tm=TM, vmem=None):
    in_specs = [pl.BlockSpec((tm, a.shape[1]), lambda i: (i, 0)) for a in tiled]
    in_specs += [pl.BlockSpec(a.shape, lambda i, nd=a.ndim: (0,) * nd, pipeline_mode=pl.Buffered(1)) for a in full]
    out_shape = [jax.ShapeDtypeStruct((rows, c), dt) for c, dt in outs_tiled]
    out_shape += [jax.ShapeDtypeStruct(s, F32) for s in outs_acc]
    out_specs = [pl.BlockSpec((tm, c), lambda i: (i, 0)) for c, _ in outs_tiled]
    out_specs += [pl.BlockSpec(s, lambda i, nd=len(s): (0,) * nd) for s in outs_acc]
    return _pcall(
        body, name=name, grid=(rows // tm,), in_specs=in_specs, out_specs=out_specs, out_shape=out_shape,
        compiler_params=pltpu.CompilerParams(dimension_semantics=("arbitrary",), vmem_limit_bytes=vmem),
    )(*tiled, *full)


def _one_call(body, name, ins, outs, vmem=None):
    whole = lambda s: pl.BlockSpec(s, lambda i, nd=len(s): (0,) * nd)
    return _pcall(
        body, name=name, grid=(1,), in_specs=[whole(a.shape) for a in ins], out_specs=[whole(s) for s, _ in outs],
        out_shape=[jax.ShapeDtypeStruct(s, dt) for s, dt in outs],
        compiler_params=pltpu.CompilerParams(dimension_semantics=("arbitrary",), vmem_limit_bytes=vmem),
    )(*ins)


def _premix_fwd(x, g_pre, win_t):
    def body(x_ref, g_ref, w_ref, h_ref, proj_ref, flog_ref):
        h = _rms(x_ref[...], g_ref[...]).astype(BF16)
        h_ref[...] = h
        p = _nt(h, w_ref[...])
        proj_ref[...] = p.astype(BF16)
        flog_ref[...] = p[:, GATE0:GATE0 + 128]

    return _tok_call(body, "premix_fwd", [x], [g_pre, win_t],
                     [(D, BF16), (PROJ, BF16), (128, F32)], vmem=VMEM_BIG)


def _postmix_fwd(x, o_f, o_c, g_fo, g_co, w_out, g_post, g_mpre, w_mq):
    def body(x_ref, of_ref, oc_ref, gfo_ref, gco_ref, wo_ref, gp_ref, gm_ref, wq_ref,
             y_ref, z_ref, x1_ref, h2_ref, qm_ref):
        y_ref[:, :512] = _rms(of_ref[...], gfo_ref[...]).astype(BF16)
        y_ref[:, 512:] = _rms(oc_ref[...], gco_ref[...]).astype(BF16)
        z = _nn(y_ref[...], wo_ref[...])
        z_ref[...] = z
        x1 = x_ref[...] + _rms(z, gp_ref[...])
        x1_ref[...] = x1
        h2 = _rms(x1, gm_ref[...]).astype(BF16)
        h2_ref[...] = h2
        qm_ref[...] = _nn(h2, wq_ref[...]).astype(BF16)

    return _tok_call(body, "postmix_fwd", [x, o_f, o_c], [g_fo, g_co, w_out, g_post, g_mpre, w_mq],
                     [(D, BF16), (D, F32), (D, F32), (D, BF16), (D, BF16)], vmem=VMEM_BIG)


def _memkv_fwd(mem, g_kv, w_mk, w_mv):
    def body(m_ref, g_ref, wk_ref, wv_ref, mn_ref, k_ref, v_ref):
        mn = _rms(m_ref[...], g_ref[...]).astype(BF16)
        mn_ref[...] = mn
        k_ref[...] = _nn(mn, wk_ref[...]).astype(BF16)
        v_ref[...] = _nn(mn, wv_ref[...]).astype(BF16)

    return _tok_call(body, "memkv_fwd", [mem], [g_kv, w_mk, w_mv],
                     [(D, BF16), (D, BF16), (D, BF16)], rows=NMEM, tm=NMEM, vmem=VMEM_BIG)


def _mem_fwd(qm, x1, km, vm, w_mo, g_post, g_fpre):
    def body(q_ref, x1_ref, k_ref, v_ref, wo_ref, gp_ref, gf_ref, om_ref, ym_ref, x2_ref, h3_ref):
        for h in range(MEM_HEADS):
            sl = slice(h * MEM_HD, (h + 1) * MEM_HD)
            s = _nt(q_ref[:, sl], k_ref[:, sl]) * MEM_SCALE
            p = jnp.exp(s - jnp.max(s, axis=-1, keepdims=True))
            p = p / jnp.sum(p, axis=-1, keepdims=True)
            om_ref[:, sl] = _nn(p.astype(BF16), v_ref[:, sl]).astype(BF16)
        ym = _nn(om_ref[...], wo_ref[...])
        ym_ref[...] = ym
        x2 = x1_ref[...] + _rms(ym, gp_ref[...])
        x2_ref[...] = x2
        h3_ref[...] = _rms(x2, gf_ref[...]).astype(BF16)

    return _tok_call(body, "mem_fwd", [qm, x1], [km, vm, w_mo, g_post, g_fpre],
                     [(D, BF16), (D, F32), (D, F32), (D, BF16)], vmem=VMEM_BIG)


def _ffn_fwd(h3, x2, tgt, w1_t, w2, g_post):
    def body(h_ref, x2_ref, t_ref, w1_ref, w2_ref, g_ref, a_ref, y_ref, dx_ref, loss_ref):
        @pl.when(pl.program_id(0) == 0)
        def _():
            loss_ref[...] = jnp.zeros_like(loss_ref)

        a = _nt(h_ref[...], w1_ref[...])
        a_ref[...] = a.astype(BF16)
        r = jnp.square(jnp.maximum(a, 0.0)).astype(BF16)
        y = _nn(r, w2_ref[...])
        y_ref[...] = y
        e = x2_ref[...] + _rms(y, g_ref[...]) - t_ref[...]
        dx_ref[...] = e * (1.0 / D)
        loss_ref[...] += 0.5 * jnp.sum(jnp.sum(e * e, axis=-1, keepdims=True) * (1.0 / D))

    return _tok_call(body, "ffn_fwd", [h3, x2, tgt], [w1_t, w2, g_post],
                     [(DFF, BF16), (D, F32), (D, F32)], [(8, 128)], vmem=VMEM_BIG)


def _tri(lower):
    r = lax.broadcasted_iota(jnp.int32, (128, 128), 0)
    c = lax.broadcasted_iota(jnp.int32, (128, 128), 1)
    return jnp.where(r >= c if lower else c >= r, 1.0, 0.0).astype(F32)


def _hdot(a, b):
    return jnp.dot(a, b, preferred_element_type=F32, precision=lax.Precision.HIGHEST)


def _gate_fwd(flog, b_pad):
    def body(f_ref, b_ref, c_ref):
        tri = _tri(True)

        def step(i, carry):
            rows = pl.ds(pl.multiple_of(i * 128, 128), 128)
            z = f_ref[rows, :] + b_ref[...]
            lf = jnp.minimum(z, 0.0) - jnp.log(1.0 + jnp.exp(-jnp.abs(z)))
            cb = _hdot(tri, lf) + carry
            c_ref[rows, :] = cb
            return cb[127:128, :]

        lax.fori_loop(0, T // 128, step, jnp.zeros((1, 128), F32))

    return _one_call(body, "gate_fwd", [flog, b_pad], [((T, 128), F32)])[0]


def _gate_bwd(dc, flog, b_pad):
    def body(dc_ref, f_ref, b_ref, df_ref, db_ref):
        tri = _tri(False)

        def step(j, carry):
            run, db = carry
            i = T // 128 - 1 - j
            rows = pl.ds(pl.multiple_of(i * 128, 128), 128)
            dcb = dc_ref[rows, :]
            rb = _hdot(tri, dcb) + run
            z = f_ref[rows, :] + b_ref[...]
            df = rb * (1.0 / (1.0 + jnp.exp(z)))
            df_ref[rows, :] = df
            return run + jnp.sum(dcb, axis=0, keepdims=True), db + jnp.sum(df, axis=0, keepdims=True)

        _, db = lax.fori_loop(0, T // 128, step, (jnp.zeros((1, 128), F32), jnp.zeros((1, 128), F32)))
        db_ref[...] = jnp.broadcast_to(db, (8, 128))

    return _one_call(body, "gate_bwd", [dc, flog, b_pad], [((T, 128), F32), ((8, 128), F32)])


def _lane_lo(rows=TQ):
    return lax.broadcasted_iota(jnp.int32, (rows, 128), 1) < HD


def _half(v, lo, a, scale=None):
    keep = lo if a == 0 else jnp.logical_not(lo)
    v = v.astype(F32) if scale is None else v.astype(F32) * scale
    return jnp.where(keep, v, 0.0).astype(BF16)


def _fox_specs():
    return [pl.BlockSpec((FQ, 128), lambda h, i: (i, h)),
            pl.BlockSpec((T, 128), lambda h, i: (0, 4 + h)),
            pl.BlockSpec((T, 128), lambda h, i: (0, 8 + h))]


def _fox_fwd(proj, c2, ct3):
    def body(q_ref, k_ref, v_ref, c_ref, ct_ref, o_ref, l_ref):
        i = pl.program_id(1)
        lo = _lane_lo(FQ)
        causal = lax.broadcasted_iota(jnp.int32, (FQ, FQ), 1) <= lax.broadcasted_iota(jnp.int32, (FQ, FQ), 0)
        q = q_ref[...]
        qs = [_half(q, lo, a, SCALE) for a in range(2)]
        cqs = [c_ref[:, 128 * a:128 * a + 1] for a in range(2)]

        def tile(off, carry, diagonal):
            kblk = k_ref[pl.ds(off, FQ), :]
            vblk = v_ref[pl.ds(off, FQ), :]
            new = []
            for a in range(2):
                m, l, acc = carry[a]
                s = _nt(qs[a], kblk) + (cqs[a] - ct_ref[a:a + 1, pl.ds(off, FQ)])
                if diagonal:
                    s = jnp.where(causal, s, NEG)
                m2 = jnp.maximum(m, jnp.max(s, axis=-1, keepdims=True))
                p = jnp.exp(s - m2)
                alpha = jnp.exp(m - m2)
                new.append((m2, alpha * l + jnp.sum(p, axis=-1, keepdims=True),
                            alpha * acc + _nn(p.astype(BF16), vblk)))
            return tuple(new)

        init = (jnp.full((FQ, 1), NEG, F32), jnp.zeros((FQ, 1), F32), jnp.zeros((FQ, 128), F32))
        carry = lax.fori_loop(0, i, lambda kb, c: tile(pl.multiple_of(kb * FQ, FQ), c, False), (init, init))
        carry = tile(pl.multiple_of(i * FQ, FQ), carry, True)
        outs = []
        for a in range(2):
            m, l, acc = carry[a]
            outs.append(acc / l)
            l_ref[:, 128 * a:128 * a + 128] = jnp.broadcast_to(m + jnp.log(l), (FQ, 128))
        o_ref[...] = jnp.where(lo, outs[0], outs[1])

    return _pcall(
        body, name="fox_fwd", grid=(4, T // FQ),
        in_specs=_fox_specs() + [pl.BlockSpec((FQ, 256), lambda h, i: (i, h)),
                                 pl.BlockSpec((None, 2, T), lambda h, i: (h, 0, 0))],
        out_specs=[pl.BlockSpec((FQ, 128), lambda h, i: (i, h)), pl.BlockSpec((FQ, 256), lambda h, i: (i, h))],
        out_shape=[jax.ShapeDtypeStruct((T, 512), F32), jax.ShapeDtypeStruct((T, 1024), F32)],
        compiler_params=pltpu.CompilerParams(dimension_semantics=("arbitrary", "arbitrary"), vmem_limit_bytes=VMEM_BIG),
    )(proj, proj, proj, c2, ct3)


def _fox_bwd(proj, c2, ct3, o, lse, do):
    def body(q_ref, k_ref, v_ref, c_ref, ct_ref, o_ref, l_ref, do_ref, dq_ref, dk_ref, dv_ref, dct_ref, dcq_ref):
        i = pl.program_id(1)

        @pl.when(i == 0)
        def _():
            dk_ref[...] = jnp.zeros_like(dk_ref)
            dv_ref[...] = jnp.zeros_like(dv_ref)
            dct_ref[...] = jnp.zeros_like(dct_ref)

        lo = _lane_lo(FQ)
        causal = lax.broadcasted_iota(jnp.int32, (FQ, FQ), 1) <= lax.broadcasted_iota(jnp.int32, (FQ, FQ), 0)
        q = q_ref[...]
        do_v = do_ref[...]
        prod = do_v * o_ref[...]
        qs = [_half(q, lo, a, SCALE) for a in range(2)]
        dos = [_half(do_v, lo, a) for a in range(2)]
        deltas = [jnp.sum(jnp.where(lo if a == 0 else jnp.logical_not(lo), prod, 0.0), axis=-1, keepdims=True)
                  for a in range(2)]
        cqs = [c_ref[:, 128 * a:128 * a + 1] for a in range(2)]
        las = [l_ref[:, 128 * a:128 * a + 1] for a in range(2)]

        def tile(off, carry, diagonal):
            kblk = k_ref[pl.ds(off, FQ), :]
            vblk = v_ref[pl.ds(off, FQ), :]
            new = []
            dk = jnp.zeros((FQ, 128), F32)
            dv = jnp.zeros((FQ, 128), F32)
            for a in range(2):
                dq_acc, rs = carry[a]
                s = _nt(qs[a], kblk) + (cqs[a] - ct_ref[a:a + 1, pl.ds(off, FQ)])
                if diagonal:
                    s = jnp.where(causal, s, NEG)
                p = jnp.exp(s - las[a])
                ds = p * (_nt(dos[a], vblk) - deltas[a])
                dsb = ds.astype(BF16)
                dk = dk + _tn(dsb, qs[a])
                dv = dv + _tn(p.astype(BF16), dos[a])
                dct_ref[a:a + 1, pl.ds(off, FQ)] -= jnp.sum(ds, axis=0, keepdims=True)
                new.append((dq_acc + _nn(dsb, kblk), rs + jnp.sum(ds, axis=-1, keepdims=True)))
            dk_ref[pl.ds(off, FQ), :] += dk
            dv_ref[pl.ds(off, FQ), :] += dv
            return tuple(new)

        init = (jnp.zeros((FQ, 128), F32), jnp.zeros((FQ, 1), F32))
        carry = lax.fori_loop(0, i, lambda kb, c: tile(pl.multiple_of(kb * FQ, FQ), c, False), (init, init))
        carry = tile(pl.multiple_of(i * FQ, FQ), carry, True)
        for a in range(2):
            dcq_ref[:, 128 * a:128 * a + 128] = jnp.broadcast_to(carry[a][1], (FQ, 128))
        dq_ref[...] = jnp.where(lo, carry[0][0], carry[1][0]) * SCALE

    blk = pl.BlockSpec((FQ, 128), lambda h, i: (i, h))
    wide = pl.BlockSpec((FQ, 256), lambda h, i: (i, h))
    rows = pl.BlockSpec((None, 2, T), lambda h, i: (h, 0, 0))
    col = pl.BlockSpec((T, 128), lambda h, i: (0, h))
    return _pcall(
        body, name="fox_bwd", grid=(4, T // FQ),
        in_specs=_fox_specs() + [wide, rows, blk, wide, blk],
        out_specs=[blk, col, col, rows, wide],
        out_shape=[jax.ShapeDtypeStruct((T, 512), F32), jax.ShapeDtypeStruct((T, 512), F32),
                   jax.ShapeDtypeStruct((T, 512), F32), jax.ShapeDtypeStruct((4, 2, T), F32),
                   jax.ShapeDtypeStruct((T, 1024), F32)],
        compiler_params=pltpu.CompilerParams(dimension_semantics=("arbitrary", "arbitrary"), vmem_limit_bytes=VMEM_BIG),
    )(proj, proj, proj, c2, ct3, o, lse, do)


def _rel_onehot():
    ridx = lax.broadcasted_iota(jnp.int32, (NREL_PAD, VW), 0)
    j = lax.broadcasted_iota(jnp.int32, (NREL_PAD, VW), 1)
    return jnp.where(ridx == jnp.clip(TQ + LEFT - 1 - j, -128, 128) + 128, 1.0, 0.0).astype(F32)


def _relvec_fwd(tbl):
    def body(t_ref, v_ref):
        v_ref[...] = _hdot(t_ref[...], _rel_onehot())

    return _one_call(body, "relvec_fwd", [tbl], [((8, VW), F32)])[0]


def _relvec_bwd(gv):
    def body(g_ref, t_ref):
        t_ref[...] = lax.dot_general(g_ref[...], _rel_onehot(), (((1,), (1,)), ((), ())),
                                     preferred_element_type=F32, precision=lax.Precision.HIGHEST)

    return _one_call(body, "relvec_bwd", [gv], [((8, NREL_PAD), F32)])[0]


def _chk_bias(vt_ref, a):
    vb = jnp.broadcast_to(vt_ref[a:a + 1, :], (TQ, VW))
    y = pltpu.roll(vb, VW - (TQ - 1), 1, stride=1, stride_axis=0)[:, :WIN]
    cr = lax.broadcasted_iota(jnp.int32, (TQ, WIN), 0) // 64
    cm = lax.broadcasted_iota(jnp.int32, (TQ, WIN), 1) // 64
    return jnp.where((cm >= cr) & (cm <= cr + 8), y, NEG)


def _chk_specs():
    return [pl.BlockSpec((TQ, 128), lambda h, i: (i, CHK0 // 128 + h)),
            pl.BlockSpec((T + LEFT, 128), lambda h, i: (0, h)),
            pl.BlockSpec((T + LEFT, 128), lambda h, i: (0, 4 + h)),
            pl.BlockSpec((None, 2, VW), lambda h, i: (h, 0, 0))]


def _chk_fwd(proj, kvp, vt3):
    def body(q_ref, k_ref, v_ref, vt_ref, o_ref, l_ref, bias_ref):
        i = pl.program_id(1)

        @pl.when(i == 0)
        def _():
            for a in range(2):
                bias_ref[a] = _chk_bias(vt_ref, a)

        lo = _lane_lo()
        off = pl.multiple_of(i * TQ, TQ)
        kw = k_ref[pl.ds(off, WIN), :]
        vw = v_ref[pl.ds(off, WIN), :]
        real = lax.broadcasted_iota(jnp.int32, (TQ, WIN), 1) + off >= LEFT
        q = q_ref[...]
        outs = []
        for a in range(2):
            s = jnp.where(real, _nt(_half(q, lo, a), kw) * SCALE + bias_ref[a], NEG)
            m = jnp.max(s, axis=-1, keepdims=True)
            p = jnp.exp(s - m)
            l = jnp.sum(p, axis=-1, keepdims=True)
            outs.append(_nn(p.astype(BF16), vw) / l)
            l_ref[:, 128 * a:128 * a + 128] = jnp.broadcast_to(m + jnp.log(l), (TQ, 128))
        o_ref[...] = jnp.where(lo, outs[0], outs[1])

    return _pcall(
        body, name="chk_fwd", grid=(4, T // TQ), in_specs=_chk_specs(),
        out_specs=[pl.BlockSpec((TQ, 128), lambda h, i: (i, h)), pl.BlockSpec((TQ, 256), lambda h, i: (i, h))],
        out_shape=[jax.ShapeDtypeStruct((T, 512), F32), jax.ShapeDtypeStruct((T, 1024), F32)],
        scratch_shapes=[pltpu.VMEM((2, TQ, WIN), F32)],
        compiler_params=pltpu.CompilerParams(dimension_semantics=("arbitrary", "arbitrary")),
    )(proj, kvp, kvp, vt3)


def _chk_bwd(proj, kvp, vt3, o, lse, do):
    nq = T // TQ

    def body(q_ref, k_ref, v_ref, vt_ref, o_ref, l_ref, do_ref, dq_ref, dk_ref, dv_ref, gv_ref, bias_ref, dsum_ref):
        i = pl.program_id(1)

        @pl.when(i == 0)
        def _():
            for a in range(2):
                bias_ref[a] = _chk_bias(vt_ref, a)
            dsum_ref[...] = jnp.zeros_like(dsum_ref)
            dk_ref[...] = jnp.zeros_like(dk_ref)
            dv_ref[...] = jnp.zeros_like(dv_ref)

        lo = _lane_lo()
        off = pl.multiple_of(i * TQ, TQ)
        kw = k_ref[pl.ds(off, WIN), :]
        vw = v_ref[pl.ds(off, WIN), :]
        real = lax.broadcasted_iota(jnp.int32, (TQ, WIN), 1) + off >= LEFT
        q = q_ref[...]
        do_v = do_ref[...]
        prod = do_v * o_ref[...]
        dqs = []
        for a in range(2):
            keep = lo if a == 0 else jnp.logical_not(lo)
            qa = _half(q, lo, a)
            doa = _half(do_v, lo, a)
            delta = jnp.sum(jnp.where(keep, prod, 0.0), axis=-1, keepdims=True)
            s = jnp.where(real, _nt(qa, kw) * SCALE + bias_ref[a], NEG)
            p = jnp.exp(s - l_ref[:, 128 * a:128 * a + 1])
            ds = p * (_nt(doa, vw) - delta)
            dsum_ref[a] += ds
            dsb = ds.astype(BF16)
            dk_ref[pl.ds(off, WIN), :] += _tn(dsb, qa) * SCALE
            dv_ref[pl.ds(off, WIN), :] += _tn(p.astype(BF16), doa)
            dqs.append(_nn(dsb, kw))
        dq_ref[...] = jnp.where(lo, dqs[0], dqs[1]) * SCALE

        @pl.when(i == nq - 1)
        def _():
            rr = lax.broadcasted_iota(jnp.int32, (TQ, TQ), 0)
            cc = lax.broadcasted_iota(jnp.int32, (TQ, TQ), 1)
            flip = jnp.where(rr + cc == TQ - 1, 1.0, 0.0).astype(F32)
            for a in range(2):
                dpad = jnp.concatenate([dsum_ref[a], jnp.zeros((TQ, VW - WIN), F32)], axis=1)
                z = pltpu.roll(_hdot(flip, dpad), 0, 1, stride=1, stride_axis=0)
                gv_ref[a:a + 1, :] = jnp.sum(z, axis=0, keepdims=True)

    blk = pl.BlockSpec((TQ, 128), lambda h, i: (i, h))
    wide = pl.BlockSpec((TQ, 256), lambda h, i: (i, h))
    col = pl.BlockSpec((T + LEFT, 128), lambda h, i: (0, h))
    return _pcall(
        body, name="chk_bwd", grid=(4, nq), in_specs=_chk_specs() + [blk, wide, blk],
        out_specs=[blk, col, col, pl.BlockSpec((None, 2, VW), lambda h, i: (h, 0, 0))],
        out_shape=[jax.ShapeDtypeStruct((T, 512), F32), jax.ShapeDtypeStruct((T + LEFT, 512), F32),
                   jax.ShapeDtypeStruct((T + LEFT, 512), F32), jax.ShapeDtypeStruct((4, 2, VW), F32)],
        scratch_shapes=[pltpu.VMEM((2, TQ, WIN), F32), pltpu.VMEM((2, TQ, WIN), F32)],
        compiler_params=pltpu.CompilerParams(dimension_semantics=("arbitrary", "arbitrary")),
    )(proj, kvp, kvp, vt3, o, lse, do)


def _zero_at_start(*refs):
    @pl.when(pl.program_id(0) == 0)
    def _():
        for r in refs:
            r[...] = jnp.zeros_like(r)


def _ffn_bwd(dx3, y3, x2, a, w1_t, w2, g_post, g_pre):
    def body(dx3_ref, y_ref, x2_ref, a_ref, w1_ref, w2_ref, gp_ref, gf_ref,
             dx2_ref, da_ref, dy_ref, r_ref, dgp_ref, dgf_ref):
        _zero_at_start(dgp_ref, dgf_ref)
        dx3_v = dx3_ref[...]
        dy, dgp = _rms_bwd(y_ref[...], gp_ref[...], dx3_v)
        dgp_ref[...] += dgp
        dyb = dy.astype(BF16)
        dy_ref[...] = dyb
        ra = jnp.maximum(a_ref[...].astype(F32), 0.0)
        r_ref[...] = jnp.square(ra).astype(BF16)
        da = (_nt(dyb, w2_ref[...]) * (2.0 * ra)).astype(BF16)
        da_ref[...] = da
        dh, dgf = _rms_bwd(x2_ref[...], gf_ref[...], _nn(da, w1_ref[...]))
        dgf_ref[...] += dgf
        dx2_ref[...] = dx3_v + dh

    return _tok_call(body, "ffn_bwd", [dx3, y3, x2, a], [w1_t, w2, g_post, g_pre],
                     [(D, F32), (DFF, BF16), (D, BF16), (DFF, BF16)], [(1, D), (1, D)], vmem=VMEM_BIG)


def _mem_bwd(dx2, ym, x1, qm, km, vm, w_mo, w_mq, g_post, g_pre):
    def body(dx2_ref, ym_ref, x1_ref, q_ref, k_ref, v_ref, wo_ref, wq_ref, gp_ref, gm_ref,
             dx1_ref, dym_ref, dq_ref, dk_ref, dv_ref, dgp_ref, dgm_ref, dom_ref):
        _zero_at_start(dk_ref, dv_ref, dgp_ref, dgm_ref)
        dx2_v = dx2_ref[...]
        dym, dgp = _rms_bwd(ym_ref[...], gp_ref[...], dx2_v)
        dgp_ref[...] += dgp
        dymb = dym.astype(BF16)
        dym_ref[...] = dymb
        dom_ref[...] = _nt(dymb, wo_ref[...]).astype(BF16)
        for h in range(MEM_HEADS):
            sl = slice(h * MEM_HD, (h + 1) * MEM_HD)
            qh, kh, doh = q_ref[:, sl], k_ref[:, sl], dom_ref[:, sl]
            s = _nt(qh, kh) * MEM_SCALE
            p = jnp.exp(s - jnp.max(s, axis=-1, keepdims=True))
            p = p / jnp.sum(p, axis=-1, keepdims=True)
            dp = _nt(doh, v_ref[:, sl])
            ds = (p * (dp - jnp.sum(p * dp, axis=-1, keepdims=True))).astype(BF16)
            dq_ref[:, sl] = (_nn(ds, kh) * MEM_SCALE).astype(BF16)
            dk_ref[:, sl] += _tn(ds, qh) * MEM_SCALE
            dv_ref[:, sl] += _tn(p.astype(BF16), doh)
        dh, dgm = _rms_bwd(x1_ref[...], gm_ref[...], _nt(dq_ref[...], wq_ref[...]))
        dgm_ref[...] += dgm
        dx1_ref[...] = dx2_v + dh

    in_specs = [pl.BlockSpec((TM, D), lambda i: (i, 0))] * 4
    in_specs += [pl.BlockSpec(a.shape, lambda i, nd=a.ndim: (0,) * nd, pipeline_mode=pl.Buffered(1))
                 for a in (km, vm, w_mo, w_mq, g_post, g_pre)]
    tiled = pl.BlockSpec((TM, D), lambda i: (i, 0))
    kv = pl.BlockSpec((NMEM, D), lambda i: (0, 0))
    vec = pl.BlockSpec((1, D), lambda i: (0, 0))
    return _pcall(
        body, name="mem_bwd", grid=(T // TM,), in_specs=in_specs,
        out_specs=[tiled, tiled, tiled, kv, kv, vec, vec],
        out_shape=[jax.ShapeDtypeStruct((T, D), F32), jax.ShapeDtypeStruct((T, D), BF16),
                   jax.ShapeDtypeStruct((T, D), BF16), jax.ShapeDtypeStruct((NMEM, D), F32),
                   jax.ShapeDtypeStruct((NMEM, D), F32), jax.ShapeDtypeStruct((1, D), F32),
                   jax.ShapeDtypeStruct((1, D), F32)],
        scratch_shapes=[pltpu.VMEM((TM, D), BF16)],
        compiler_params=pltpu.CompilerParams(dimension_semantics=("arbitrary",), vmem_limit_bytes=VMEM_BIG),
    )(dx2, ym, x1, qm, km, vm, w_mo, w_mq, g_post, g_pre)


def _memkv_bwd(dkm, dvm, mem, w_mk, w_mv):
    def body(dk_ref, dv_ref, m_ref, wk_ref, wv_ref, dg_ref):
        dmn = _nt(dk_ref[...].astype(BF16), wk_ref[...]) + _nt(dv_ref[...].astype(BF16), wv_ref[...])
        mv = m_ref[...]
        dg_ref[...] = jnp.sum(dmn * (mv * _rstd(mv)), axis=0, keepdims=True)

    return _one_call(body, "memkv_bwd", [dkm, dvm, mem, w_mk, w_mv], [((1, D), F32)], vmem=VMEM_BIG)[0]


def _postmix_bwd(dx1, z, o_f, o_c, w_out, g_post, g_fo, g_co):
    def body(dx1_ref, z_ref, of_ref, oc_ref, wo_ref, gp_ref, gfo_ref, gco_ref,
             dz_ref, dof_ref, doc_ref, dgp_ref, dgfo_ref, dgco_ref):
        _zero_at_start(dgp_ref, dgfo_ref, dgco_ref)
        dz, dgp = _rms_bwd(z_ref[...], gp_ref[...], dx1_ref[...])
        dgp_ref[...] += dgp
        dzb = dz.astype(BF16)
        dz_ref[...] = dzb
        dy = _nt(dzb, wo_ref[...])
        dof, dgfo = _rms_bwd(of_ref[...], gfo_ref[...], dy[:, :512])
        doc, dgco = _rms_bwd(oc_ref[...], gco_ref[...], dy[:, 512:])
        dof_ref[...] = dof
        doc_ref[...] = doc
        dgfo_ref[...] += dgfo
        dgco_ref[...] += dgco

    return _tok_call(body, "postmix_bwd", [dx1, z, o_f, o_c], [w_out, g_post, g_fo, g_co],
                     [(D, BF16), (512, F32), (512, F32)], [(1, D), (1, 512), (1, 512)], vmem=VMEM_BIG)


def _premix_bwd(dx1, x, dproj, win_t, g_pre):
    def body(dx1_ref, x_ref, dp_ref, w_ref, g_ref, dx_ref, dg_ref):
        _zero_at_start(dg_ref)
        dh, dg = _rms_bwd(x_ref[...], g_ref[...], _nn(dp_ref[...], w_ref[...]))
        dg_ref[...] += dg
        dx_ref[...] = dx1_ref[...] + dh

    return _tok_call(body, "premix_bwd", [dx1, x, dproj], [win_t, g_pre], [(D, F32)], [(1, D)], vmem=VMEM_BIG)


def _wgrad(a, b, name):
    k, m = a.shape
    n = b.shape[1]
    tm = 640 if m % 640 == 0 and m > 1024 else min(m, 512)
    tn = min(n, 1024)

    def body(a_ref, b_ref, o_ref):
        o_ref[...] = _tn(a_ref[...].astype(BF16), b_ref[...].astype(BF16))

    return _pcall(
        body, name=name, grid=(m // tm, n // tn),
        in_specs=[pl.BlockSpec((k, tm), lambda i, j: (0, i)), pl.BlockSpec((k, tn), lambda i, j: (0, j))],
        out_specs=pl.BlockSpec((tm, tn), lambda i, j: (i, j)),
        out_shape=jax.ShapeDtypeStruct((m, n), F32),
        compiler_params=pltpu.CompilerParams(dimension_semantics=("arbitrary", "arbitrary"), vmem_limit_bytes=VMEM_BIG),
    )(a, b)


def _adam_math(w, g, m, v):
    m2 = ADAM_B1 * m + (1.0 - ADAM_B1) * g
    v2 = ADAM_B2 * v + (1.0 - ADAM_B2) * jnp.square(g)
    m_hat = m2 / (1.0 - ADAM_B1 ** ADAM_STEP)
    v_hat = v2 / (1.0 - ADAM_B2 ** ADAM_STEP)
    delta = -ADAM_LR * (m_hat / (jnp.sqrt(v_hat) + ADAM_EPS) + ADAM_WD * w)
    return delta, m2, v2


def _adamw(w, g, m, v, name):
    rows, cols = w.shape
    tr = 256 if rows % 256 == 0 else rows

    def body(w_ref, g_ref, m_ref, v_ref, d_ref, m2_ref, v2_ref):
        d_ref[...], m2_ref[...], v2_ref[...] = _adam_math(w_ref[...], g_ref[...], m_ref[...], v_ref[...])

    spec = pl.BlockSpec((tr, cols), lambda i: (i, 0))
    return _pcall(
        body, name=name, grid=(rows // tr,), in_specs=[spec] * 4, out_specs=[spec] * 3,
        out_shape=[jax.ShapeDtypeStruct(w.shape, F32)] * 3,
        compiler_params=pltpu.CompilerParams(dimension_semantics=("arbitrary",)),
    )(w, g, m, v)


def _adamw_small(w, gparts, m, v):
    def body(w_ref, g_ref, m_ref, v_ref, gs_ref, d_ref, m2_ref, v2_ref):
        g = g_ref[0]
        for k in range(1, 8):
            g = g + g_ref[k]
        gs_ref[...] = g
        d_ref[...], m2_ref[...], v2_ref[...] = _adam_math(w_ref[...], g, m_ref[...], v_ref[...])

    return _one_call(body, "adamw_small", [w, gparts, m, v], [(w.shape, F32)] * 4)


def _add_halves(mine, theirs):
    tr = 688

    def body(a_ref, b_ref, o_ref):
        o_ref[...] = (a_ref[...] + b_ref[...]).astype(BF16)

    spec = pl.BlockSpec((None, tr, D), lambda j, i: (j, i, 0))
    return _pcall(
        body, name="rs_add_halves", grid=(4, R_ALL // tr), in_specs=[spec, spec], out_specs=spec,
        out_shape=jax.ShapeDtypeStruct((4, R_ALL, D), BF16),
        compiler_params=pltpu.CompilerParams(dimension_semantics=("arbitrary", "arbitrary")),
    )(mine, theirs)


def _sum_chips(parts):
    tr = 688

    def body(p_ref, o_ref):
        o_ref[...] = ((p_ref[0].astype(F32) + p_ref[1].astype(F32)) + p_ref[2].astype(F32)) + p_ref[3].astype(F32)

    return _pcall(
        body, name="rs_sum_chips", grid=(R_ALL // tr,),
        in_specs=[pl.BlockSpec((4, tr, D), lambda i: (0, i, 0))], out_specs=pl.BlockSpec((tr, D), lambda i: (i, 0)),
        out_shape=jax.ShapeDtypeStruct((R_ALL, D), F32),
        compiler_params=pltpu.CompilerParams(dimension_semantics=("arbitrary",)),
    )(parts)


def _place():
    return lax.axis_index("x"), lax.axis_index("y"), lax.axis_index("c")


def _allgather(block, name):
    def body(x_ref, out_ref, send_sems, recv_sems, local_sem):
        x, y, c = _place()
        me, sibling = (x, y, c), (x, y, 1 - c)
        chips = [(1 - x, y), (x, 1 - y), (1 - x, 1 - y)]

        def slot(px, py, pc):
            return out_ref.at[4 * px + 2 * py + pc]

        def copy(k, blk, to, src=None):
            return pltpu.make_async_remote_copy(
                src_ref=slot(*blk) if src is None else src, dst_ref=slot(*blk),
                send_sem=send_sems.at[k], recv_sem=recv_sems.at[k], device_id=to, device_id_type=MESH)

        mine = pltpu.make_async_copy(x_ref, slot(*me), local_sem)
        mine.start()
        first = [copy(0, me, sibling, src=x_ref)]
        first += [copy(1 + j, me, (*chip, c), src=x_ref) for j, chip in enumerate(chips)]
        for cp in first:
            cp.start()
        passed = [copy(4 + j, (*chip, c), sibling) for j, chip in enumerate(chips)]
        for j, chip in enumerate(chips):
            copy(1 + j, (*chip, c), me).wait_recv()
            passed[j].start()
        copy(0, sibling, me).wait_recv()
        for j, chip in enumerate(chips):
            copy(4 + j, (*chip, 1 - c), me).wait_recv()
        for cp in first + passed:
            cp.wait_send()
        mine.wait()

    return _pcall(
        body, name=name, out_shape=jax.ShapeDtypeStruct((8,) + block.shape, block.dtype),
        in_specs=[pl.BlockSpec(memory_space=pl.ANY)], out_specs=pl.BlockSpec(memory_space=pl.ANY),
        scratch_shapes=[pltpu.SemaphoreType.DMA((7,)), pltpu.SemaphoreType.DMA((7,)), pltpu.SemaphoreType.DMA(())],
        compiler_params=pltpu.CompilerParams(has_side_effects=True),
    )(block)


def _swap_sibling(v):
    def body(v_ref, got_ref, send_sem, recv_sem):
        x, y, c = _place()
        cp = pltpu.make_async_remote_copy(src_ref=v_ref, dst_ref=got_ref, send_sem=send_sem, recv_sem=recv_sem,
                                          device_id=(x, y, 1 - c), device_id_type=MESH)
        cp.start()
        cp.wait()

    return _pcall(
        body, name="rs_swap_sibling", out_shape=jax.ShapeDtypeStruct(v.shape, v.dtype),
        in_specs=[pl.BlockSpec(memory_space=pl.ANY)], out_specs=pl.BlockSpec(memory_space=pl.ANY),
        scratch_shapes=[pltpu.SemaphoreType.DMA(()), pltpu.SemaphoreType.DMA(())],
        compiler_params=pltpu.CompilerParams(has_side_effects=True),
    )(v)


def _exchange_chips(h):
    def body(h_ref, out_ref, send_sems, recv_sems, local_sem):
        x, y, c = _place()
        mychip = 2 * x + y
        chips = [(1 - x, y), (x, 1 - y), (1 - x, 1 - y)]
        mine = pltpu.make_async_copy(h_ref.at[mychip], out_ref.at[mychip], local_sem)
        mine.start()
        cps = []
        for k, (px, py) in enumerate(chips):
            cps.append(pltpu.make_async_remote_copy(
                src_ref=h_ref.at[2 * px + py], dst_ref=out_ref.at[mychip],
                send_sem=send_sems.at[k], recv_sem=recv_sems.at[k], device_id=(px, py, c), device_id_type=MESH))
            cps[-1].start()
        for k, (px, py) in enumerate(chips):
            pltpu.make_async_remote_copy(
                src_ref=h_ref.at[mychip], dst_ref=out_ref.at[2 * px + py],
                send_sem=send_sems.at[k], recv_sem=recv_sems.at[k], device_id=(px, py, c), device_id_type=MESH).wait_recv()
        for cp in cps:
            cp.wait_send()
        mine.wait()

    return _pcall(
        body, name="rs_exchange_chips", out_shape=jax.ShapeDtypeStruct(h.shape, h.dtype),
        in_specs=[pl.BlockSpec(memory_space=pl.ANY)], out_specs=pl.BlockSpec(memory_space=pl.ANY),
        scratch_shapes=[pltpu.SemaphoreType.DMA((3,)), pltpu.SemaphoreType.DMA((3,)), pltpu.SemaphoreType.DMA(())],
        compiler_params=pltpu.CompilerParams(has_side_effects=True),
    )(h)


def _pack_small(p):
    z = lambda a, n: jnp.pad(a, ((0, 0), (0, n - a.shape[1])))
    rows = [z(p['rel_bias'], D), z(p['b_fgt'], D), jnp.concatenate([p['g_fox_out'], p['g_chk_out']], axis=1)]
    rows += [p[n] for n in ('g_mix_pre', 'g_mix_post', 'g_mem_kv', 'g_mem_pre', 'g_mem_post', 'g_ff_pre', 'g_ff_post')]
    rows.append(jnp.zeros((SMALL_ROWS - 17, D), F32))
    return jnp.concatenate(rows, axis=0)


def _unpack_small(a):
    out = {'rel_bias': a[0:8, :257], 'b_fgt': a[8:9, :8], 'g_fox_out': a[9:10, :512], 'g_chk_out': a[9:10, 512:]}
    for k, n in enumerate(('g_mix_pre', 'g_mix_post', 'g_mem_kv', 'g_mem_pre', 'g_mem_post', 'g_ff_pre', 'g_ff_post')):
        out[n] = a[10 + k:11 + k]
    return out


def _local_grads(x, mem, tgt, wts, sm):
    win_t, w_out, w_mq, w_mk, w_mv, w_mo, w1_t, w2 = (wts[n] for n in BIG)
    b_pad = jnp.pad(sm['b_fgt'], ((0, 0), (0, 120)))
    tbl = jnp.pad(sm['rel_bias'], ((0, 0), (0, NREL_PAD - 257)))

    h1, proj, flog = _premix_fwd(x, sm['g_mix_pre'], win_t)
    c = _gate_fwd(flog, b_pad)
    c8 = c[:, :8]
    c2 = jnp.repeat(c8, 128, axis=1)
    ct3 = c8.T.reshape(4, 2, T)
    o_f, lse_f = _fox_fwd(proj, c2, ct3)
    vt3 = _relvec_fwd(tbl).reshape(4, 2, VW)
    kvp = jnp.pad(proj[:, CHK0 + 512:], ((LEFT, 0), (0, 0)))
    o_c, lse_c = _chk_fwd(proj, kvp, vt3)
    ycat, z, x1, h2, qm = _postmix_fwd(x, o_f, o_c, sm['g_fox_out'], sm['g_chk_out'], w_out,
                                       sm['g_mix_post'], sm['g_mem_pre'], w_mq)
    memn, km, vm = _memkv_fwd(mem, sm['g_mem_kv'], w_mk, w_mv)
    om, ym, x2, h3 = _mem_fwd(qm, x1, km, vm, w_mo, sm['g_mem_post'], sm['g_ff_pre'])
    a, y3, dx3, loss_acc = _ffn_fwd(h3, x2, tgt, w1_t, w2, sm['g_ff_post'])

    gs = {}
    dx2, da, dy3, r, gs['g_ff_post'], gs['g_ff_pre'] = _ffn_bwd(dx3, y3, x2, a, w1_t, w2, sm['g_ff_post'], sm['g_ff_pre'])
    dx1, dym, dqm, dkm, dvm, gs['g_mem_post'], gs['g_mem_pre'] = _mem_bwd(
        dx2, ym, x1, qm, km, vm, w_mo, w_mq, sm['g_mem_post'], sm['g_mem_pre'])
    gs['g_mem_kv'] = _memkv_bwd(dkm, dvm, mem, w_mk, w_mv)
    dz, dof, doc, gs['g_mix_post'], gs['g_fox_out'], gs['g_chk_out'] = _postmix_bwd(
        dx1, z, o_f, o_c, w_out, sm['g_mix_post'], sm['g_fox_out'], sm['g_chk_out'])
    dq_f, dk_f, dv_f, dct, dcq = _fox_bwd(proj, c2, ct3, o_f, lse_f, dof)
    dq_c, dkp, dvp, gv = _chk_bwd(proj, kvp, vt3, o_c, lse_c, doc)
    gs['rel_bias'] = _relvec_bwd(gv.reshape(8, VW))[:, :257]
    dc = jnp.pad(dct.reshape(8, T).T + dcq[:, ::128], ((0, 0), (0, 120)))
    dflog, db = _gate_bwd(dc, flog, b_pad)
    gs['b_fgt'] = db[0:1, :8]
    dproj = jnp.concatenate([dq_f, dk_f, dv_f, dflog, dq_c, dkp[LEFT:], dvp[LEFT:]], axis=1).astype(BF16)
    grad_x, gs['g_mix_pre'] = _premix_bwd(dx1, x, dproj, win_t, sm['g_mix_pre'])

    gb = {
        'w_in': _wgrad(dproj, h1, "wgrad_in"),
        'w_out': _wgrad(ycat, dz, "wgrad_out"),
        'w_mq': _wgrad(h2, dqm, "wgrad_mq"),
        'w_mk': _wgrad(memn, dkm, "wgrad_mk"),
        'w_mv': _wgrad(memn, dvm, "wgrad_mv"),
        'w_mo': _wgrad(om, dym, "wgrad_mo"),
        'w_ff1': _wgrad(da, h3, "wgrad_ff1"),
        'w_ff2': _wgrad(r, dy3, "wgrad_ff2"),
    }
    return loss_acc[0, 0], grad_x, gb, gs


def _unpack_gathered(gw):
    wt = gw[:, R_IN:R_IN + N_IN].reshape(8 * N_IN, D)
    gate = jnp.pad(wt[GATE0:GATE0 + 8], ((0, 120), (0, 0)))
    full = lambda r0, r1: gw[:, r0:r1].reshape(8 * (r1 - r0), D)
    return {
        'w_in': jnp.concatenate([wt[:GATE0], gate, wt[GATE0 + 8:]], axis=0),
        'w_out': full(R_OUT, R_MQ), 'w_mq': full(R_MQ, R_MK), 'w_mk': full(R_MK, R_MV), 'w_mv': full(R_MV, R_MO),
        'w_mo': full(R_MO, R_FF1), 'w_ff1': full(R_FF1, R_FF2), 'w_ff2': full(R_FF2, R_ALL),
    }


def _pack_grads(gb):
    gin = jnp.concatenate([gb['w_in'][:GATE0 + 8], gb['w_in'][CHK0:]], axis=0).reshape(8, N_IN, D)
    parts = [jnp.pad(gin, ((0, 0), (0, R_OUT - N_IN), (0, 0)))]
    parts += [gb[n].reshape(8, -1, D) for n in BIG[1:]]
    return jnp.concatenate(parts, axis=1)


def kernel(x, mem, w_in, b_fgt, rel_bias, g_fox_out, g_chk_out, w_out, g_mix_pre, g_mix_post, g_mem_kv, w_mq, w_mk, w_mv, w_mo, g_mem_pre, g_mem_post, w_ff1, w_ff2, g_ff_pre, g_ff_post, loss_target, m_w_in, m_b_fgt, m_rel_bias, m_g_fox_out, m_g_chk_out, m_w_out, m_g_mix_pre, m_g_mix_post, m_g_mem_kv, m_w_mq, m_w_mk, m_w_mv, m_w_mo, m_g_mem_pre, m_g_mem_post, m_w_ff1, m_w_ff2, m_g_ff_pre, m_g_ff_post, v_w_in, v_b_fgt, v_rel_bias, v_g_fox_out, v_g_chk_out, v_w_out, v_g_mix_pre, v_g_mix_post, v_g_mem_kv, v_w_mq, v_w_mk, v_w_mv, v_w_mo, v_g_mem_pre, v_g_mem_post, v_w_ff1, v_w_ff2, v_g_ff_pre, v_g_ff_post):
    args = dict(locals())
    two_d = lambda a: a.reshape(a.shape[-2:])
    w = {n: two_d(args[n]) for n in WEIGHTS}
    m = {n: two_d(args['m_' + n]) for n in WEIGHTS}
    v = {n: two_d(args['v_' + n]) for n in WEIGHTS}

    shard = jnp.concatenate([jnp.pad(w['w_in'].T, ((0, R_OUT - N_IN), (0, 0))), w['w_out'], w['w_mq'], w['w_mk'],
                             w['w_mv'], w['w_mo'], w['w_ff1'].T, w['w_ff2']], axis=0).astype(BF16)
    wts = _unpack_gathered(_allgather(shard, "allgather_weights"))
    sm = {n: w[n] for n in SMALL}

    loss_local, grad_x, gb, gs = _local_grads(x[0], mem[0], loss_target[0], wts, sm)
    loss = lax.psum(loss_local, ("x", "y", "c"))

    g4 = _pack_grads(gb).reshape(4, 2, R_ALL, D)
    c = lax.axis_index("c")
    mine = lax.dynamic_index_in_dim(g4, c, axis=1, keepdims=False)
    theirs = lax.dynamic_index_in_dim(g4, 1 - c, axis=1, keepdims=False)
    gsh = _sum_chips(_exchange_chips(_add_halves(mine, _swap_sibling(theirs))))
    g_big = {
        'w_in': gsh[R_IN:R_IN + N_IN].T, 'w_out': gsh[R_OUT:R_MQ], 'w_mq': gsh[R_MQ:R_MK], 'w_mk': gsh[R_MK:R_MV],
        'w_mv': gsh[R_MV:R_MO], 'w_mo': gsh[R_MO:R_FF1], 'w_ff1': gsh[R_FF1:R_FF2].T, 'w_ff2': gsh[R_FF2:R_ALL],
    }

    grads, deltas, new_m, new_v = {}, {}, {}, {}
    for n in BIG:
        grads[n] = g_big[n]
        deltas[n], new_m[n], new_v[n] = _adamw(w[n], g_big[n], m[n], v[n], "adamw_" + n)

    gparts = _allgather(_pack_small(gs), "allgather_small_grads")
    gsum, d_s, m_s, v_s = _adamw_small(_pack_small(sm), gparts, _pack_small({n: m[n] for n in SMALL}),
                                       _pack_small({n: v[n] for n in SMALL}))
    for dst, packed in ((grads, gsum), (deltas, d_s), (new_m, m_s), (new_v, v_s)):
        dst.update(_unpack_small(packed))

    out = [loss, grad_x[None]]
    for group in (grads, deltas, new_m, new_v):
        out += [group[n].reshape(args[n].shape) for n in WEIGHTS]
    return tuple(out)
```

```python
import functools

import jax
import jax.numpy as jnp
from jax import lax
from jax.experimental import pallas as pl
from jax.experimental.pallas import tpu as pltpu

F32 = jnp.float32
BF16 = jnp.bfloat16
MESH = pl.DeviceIdType.MESH

T = 2048
D = 1024
NMEM = 256
DFF = 4096
EPS = 1e-6
TM = 256
TQ = 256
FQ = 512
HD = 64
SCALE = HD ** -0.5
MEM_HEADS = 4
MEM_HD = 256
MEM_SCALE = MEM_HD ** -0.5
NEG = -1e30
LEFT = 512
WIN = LEFT + TQ
VW = 1024
NREL_PAD = 384
PROJ = 3200
GATE0 = 1536
CHK0 = 1664
VMEM_BIG = 56 * 1024 * 1024

ADAM_LR = 0.001
ADAM_B1 = 0.9
ADAM_B2 = 0.999
ADAM_EPS = 1e-08
ADAM_WD = 0.01
ADAM_STEP = 10

N_IN = 385
R_IN = 400
R_REST = 1664
W_ROWS = {'w_ff1': (0, 512), 'w_ff2': (512, 512),
          'w_out': (1024, 128), 'w_mq': (1152, 128), 'w_mk': (1280, 128), 'w_mv': (1408, 128), 'w_mo': (1536, 128)}
R_A, R_B = 1024, 640
SMALL_ROWS = 24

WEIGHTS = ['w_in', 'b_fgt', 'rel_bias', 'g_fox_out', 'g_chk_out', 'w_out', 'g_mix_pre', 'g_mix_post', 'g_mem_kv',
           'w_mq', 'w_mk', 'w_mv', 'w_mo', 'g_mem_pre', 'g_mem_post', 'w_ff1', 'w_ff2', 'g_ff_pre', 'g_ff_post']
BIG = ['w_in', 'w_out', 'w_mq', 'w_mk', 'w_mv', 'w_mo', 'w_ff1', 'w_ff2']
SMALL = [n for n in WEIGHTS if n not in BIG]


def _pcall(body, **kw):
    return pl.pallas_call(body, **kw)


def _nn(a, b):
    return jnp.dot(a, b, preferred_element_type=F32)


def _nt(a, b):
    return lax.dot_general(a, b, (((1,), (1,)), ((), ())), preferred_element_type=F32)


def _tn(a, b):
    return lax.dot_general(a, b, (((0,), (0,)), ((), ())), preferred_element_type=F32)


def _w(ref):
    v = ref[...]
    return v if v.ndim == 2 else v.reshape(-1, v.shape[-1])


def _rstd(x):
    return lax.rsqrt(jnp.mean(x * x, axis=-1, keepdims=True) + EPS)


def _rms(x, g):
    return x * _rstd(x) * g


def _rms_bwd(x, g, dy):
    r = _rstd(x)
    xh = x * r
    dg = jnp.sum(dy * xh, axis=0, keepdims=True)
    dxh = dy * g
    dx = r * (dxh - xh * jnp.mean(dxh * xh, axis=-1, keepdims=True))
    return dx, dg


def _resident(a):
    if isinstance(a, tuple):
        _, shape, index = a
        return pl.BlockSpec(shape, lambda *_: index, pipeline_mode=pl.Buffered(1))
    return pl.BlockSpec(a.shape, lambda *_, nd=a.ndim: (0,) * nd, pipeline_mode=pl.Buffered(1))


def _wblk(gw, name):
    r0, rows = W_ROWS[name]
    return (gw, (8, rows, D), (0, r0 // rows, 0))


def _tok_call(body, name, tiled, full, outs_tiled, outs_acc=(), rows=T, tm=TM, vmem=None):
    in_specs = [pl.BlockSpec((tm, a.shape[1]), lambda i: (i, 0)) for a in tiled]
    in_specs += [_resident(a) for a in full]
    full = [a[0] if isinstance(a, tuple) else a for a in full]
    out_shape = [jax.ShapeDtypeStruct((rows, c), dt) for c, dt in outs_tiled]
    out_shape += [jax.ShapeDtypeStruct(s, F32) for s in outs_acc]
    out_specs = [pl.BlockSpec((tm, c), lambda i: (i, 0)) for c, _ in outs_tiled]
    out_specs += [pl.BlockSpec(s, lambda i, nd=len(s): (0,) * nd) for s in outs_acc]
    return _pcall(
        body, name=name, grid=(rows // tm,), in_specs=in_specs, out_specs=out_specs, out_shape=out_shape,
        compiler_params=pltpu.CompilerParams(dimension_semantics=("arbitrary",), vmem_limit_bytes=vmem),
    )(*tiled, *full)


def _one_call(body, name, ins, outs, vmem=None):
    whole = lambda s: pl.BlockSpec(s, lambda i, nd=len(s): (0,) * nd)
    return _pcall(
        body, name=name, grid=(1,), in_specs=[_resident(a) for a in ins], out_specs=[whole(s) for s, _ in outs],
        out_shape=[jax.ShapeDtypeStruct(s, dt) for s, dt in outs],
        compiler_params=pltpu.CompilerParams(dimension_semantics=("arbitrary",), vmem_limit_bytes=vmem),
    )(*[a[0] if isinstance(a, tuple) else a for a in ins])


def _premix_fwd(x, g_pre, win_t):
    def body(x_ref, g_ref, w_ref, h_ref, proj_ref, flog_ref):
        h = _rms(x_ref[...], g_ref[...]).astype(BF16)
        h_ref[...] = h
        p = _nt(h, w_ref[...])
        proj_ref[...] = p.astype(BF16)
        flog_ref[...] = p[:, GATE0:GATE0 + 128]

    return _tok_call(body, "premix_fwd", [x], [g_pre, win_t],
                     [(D, BF16), (PROJ, BF16), (128, F32)], vmem=VMEM_BIG)


def _postmix_fwd(x, o_f, o_c, g_fo, g_co, w_out, g_post, g_mpre, w_mq):
    def body(x_ref, of_ref, oc_ref, gfo_ref, gco_ref, wo_ref, gp_ref, gm_ref, wq_ref,
             y_ref, z_ref, x1_ref, h2_ref, qm_ref):
        y_ref[:, :512] = _rms(of_ref[...], gfo_ref[...]).astype(BF16)
        y_ref[:, 512:] = _rms(oc_ref[...], gco_ref[...]).astype(BF16)
        z = _nn(y_ref[...], _w(wo_ref))
        z_ref[...] = z
        x1 = x_ref[...] + _rms(z, gp_ref[...])
        x1_ref[...] = x1
        h2 = _rms(x1, gm_ref[...]).astype(BF16)
        h2_ref[...] = h2
        qm_ref[...] = _nn(h2, _w(wq_ref)).astype(BF16)

    return _tok_call(body, "postmix_fwd", [x, o_f, o_c], [g_fo, g_co, w_out, g_post, g_mpre, w_mq],
                     [(D, BF16), (D, F32), (D, F32), (D, BF16), (D, BF16)], vmem=VMEM_BIG)


def _memkv_fwd(mem, g_kv, w_mk, w_mv):
    def body(m_ref, g_ref, wk_ref, wv_ref, mn_ref, k_ref, v_ref):
        mn = _rms(m_ref[...], g_ref[...]).astype(BF16)
        mn_ref[...] = mn
        k_ref[...] = _nn(mn, _w(wk_ref)).astype(BF16)
        v_ref[...] = _nn(mn, _w(wv_ref)).astype(BF16)

    return _tok_call(body, "memkv_fwd", [mem], [g_kv, w_mk, w_mv],
                     [(D, BF16), (D, BF16), (D, BF16)], rows=NMEM, tm=NMEM, vmem=VMEM_BIG)


def _mem_fwd(qm, x1, km, vm, w_mo, g_post, g_fpre):
    def body(q_ref, x1_ref, k_ref, v_ref, wo_ref, gp_ref, gf_ref, om_ref, ym_ref, x2_ref, h3_ref):
        for h in range(MEM_HEADS):
            sl = slice(h * MEM_HD, (h + 1) * MEM_HD)
            s = _nt(q_ref[:, sl], k_ref[:, sl]) * MEM_SCALE
            p = jnp.exp(s - jnp.max(s, axis=-1, keepdims=True))
            p = p / jnp.sum(p, axis=-1, keepdims=True)
            om_ref[:, sl] = _nn(p.astype(BF16), v_ref[:, sl]).astype(BF16)
        ym = _nn(om_ref[...], _w(wo_ref))
        ym_ref[...] = ym
        x2 = x1_ref[...] + _rms(ym, gp_ref[...])
        x2_ref[...] = x2
        h3_ref[...] = _rms(x2, gf_ref[...]).astype(BF16)

    return _tok_call(body, "mem_fwd", [qm, x1], [km, vm, w_mo, g_post, g_fpre],
                     [(D, BF16), (D, F32), (D, F32), (D, BF16)], vmem=VMEM_BIG)


def _ffn_fwd(h3, x2, tgt, w1_t, w2, g_post):
    def body(h_ref, x2_ref, t_ref, w1_ref, w2_ref, g_ref, a_ref, y_ref, dx_ref, loss_ref):
        @pl.when(pl.program_id(0) == 0)
        def _():
            loss_ref[...] = jnp.zeros_like(loss_ref)

        a = _nt(h_ref[...], _w(w1_ref))
        a_ref[...] = a.astype(BF16)
        r = jnp.square(jnp.maximum(a, 0.0)).astype(BF16)
        y = _nn(r, _w(w2_ref))
        y_ref[...] = y
        e = x2_ref[...] + _rms(y, g_ref[...]) - t_ref[...]
        dx_ref[...] = e * (1.0 / D)
        loss_ref[...] += 0.5 * jnp.sum(jnp.sum(e * e, axis=-1, keepdims=True) * (1.0 / D))

    return _tok_call(body, "ffn_fwd", [h3, x2, tgt], [w1_t, w2, g_post],
                     [(DFF, BF16), (D, F32), (D, F32)], [(8, 128)], vmem=VMEM_BIG)


def _tri(lower):
    r = lax.broadcasted_iota(jnp.int32, (128, 128), 0)
    c = lax.broadcasted_iota(jnp.int32, (128, 128), 1)
    return jnp.where(r >= c if lower else c >= r, 1.0, 0.0).astype(F32)


def _hdot(a, b):
    return jnp.dot(a, b, preferred_element_type=F32, precision=lax.Precision.HIGHEST)


def _gate_fwd(flog, b_pad):
    def body(f_ref, b_ref, c_ref):
        tri = _tri(True)

        def step(i, carry):
            rows = pl.ds(pl.multiple_of(i * 128, 128), 128)
            z = f_ref[rows, :] + b_ref[...]
            lf = jnp.minimum(z, 0.0) - jnp.log(1.0 + jnp.exp(-jnp.abs(z)))
            cb = _hdot(tri, lf) + carry
            c_ref[rows, :] = cb
            return cb[127:128, :]

        lax.fori_loop(0, T // 128, step, jnp.zeros((1, 128), F32))

    return _one_call(body, "gate_fwd", [flog, b_pad], [((T, 128), F32)])[0]


def _gate_bwd(dc, flog, b_pad):
    def body(dc_ref, f_ref, b_ref, df_ref, db_ref):
        tri = _tri(False)

        def step(j, carry):
            run, db = carry
            i = T // 128 - 1 - j
            rows = pl.ds(pl.multiple_of(i * 128, 128), 128)
            dcb = dc_ref[rows, :]
            rb = _hdot(tri, dcb) + run
            z = f_ref[rows, :] + b_ref[...]
            df = rb * (1.0 / (1.0 + jnp.exp(z)))
            df_ref[rows, :] = df
            return run + jnp.sum(dcb, axis=0, keepdims=True), db + jnp.sum(df, axis=0, keepdims=True)

        _, db = lax.fori_loop(0, T // 128, step, (jnp.zeros((1, 128), F32), jnp.zeros((1, 128), F32)))
        db_ref[...] = jnp.broadcast_to(db, (8, 128))

    return _one_call(body, "gate_bwd", [dc, flog, b_pad], [((T, 128), F32), ((8, 128), F32)])


def _lane_lo(rows=TQ):
    return lax.broadcasted_iota(jnp.int32, (rows, 128), 1) < HD


def _half(v, lo, a, scale=None):
    keep = lo if a == 0 else jnp.logical_not(lo)
    v = v.astype(F32) if scale is None else v.astype(F32) * scale
    return jnp.where(keep, v, 0.0).astype(BF16)


def _fox_specs():
    return [pl.BlockSpec((FQ, 128), lambda h, i: (i, h)),
            pl.BlockSpec((T, 128), lambda h, i: (0, 4 + h)),
            pl.BlockSpec((T, 128), lambda h, i: (0, 8 + h))]


def _fox_fwd(proj, c2, ct3):
    def body(q_ref, k_ref, v_ref, c_ref, ct_ref, o_ref, l_ref):
        i = pl.program_id(1)
        lo = _lane_lo(FQ)
        causal = lax.broadcasted_iota(jnp.int32, (FQ, FQ), 1) <= lax.broadcasted_iota(jnp.int32, (FQ, FQ), 0)
        q = q_ref[...]
        qs = [_half(q, lo, a, SCALE) for a in range(2)]
        cqs = [c_ref[:, 128 * a:128 * a + 1] for a in range(2)]

        def tile(off, carry, diagonal):
            kblk = k_ref[pl.ds(off, FQ), :]
            vblk = v_ref[pl.ds(off, FQ), :]
            new = []
            for a in range(2):
                m, l, acc = carry[a]
                s = _nt(qs[a], kblk) + (cqs[a] - ct_ref[a:a + 1, pl.ds(off, FQ)])
                if diagonal:
                    s = jnp.where(causal, s, NEG)
                m2 = jnp.maximum(m, jnp.max(s, axis=-1, keepdims=True))
                p = jnp.exp(s - m2)
                alpha = jnp.exp(m - m2)
                new.append((m2, alpha * l + jnp.sum(p, axis=-1, keepdims=True),
                            alpha * acc + _nn(p.astype(BF16), vblk)))
            return tuple(new)

        init = (jnp.full((FQ, 1), NEG, F32), jnp.zeros((FQ, 1), F32), jnp.zeros((FQ, 128), F32))
        carry = lax.fori_loop(0, i, lambda kb, c: tile(pl.multiple_of(kb * FQ, FQ), c, False), (init, init))
        carry = tile(pl.multiple_of(i * FQ, FQ), carry, True)
        outs = []
        for a in range(2):
            m, l, acc = carry[a]
            outs.append(acc / l)
            l_ref[:, 128 * a:128 * a + 128] = jnp.broadcast_to(m + jnp.log(l), (FQ, 128))
        o_ref[...] = jnp.where(lo, outs[0], outs[1])

    return _pcall(
        body, name="fox_fwd", grid=(4, T // FQ),
        in_specs=_fox_specs() + [pl.BlockSpec((FQ, 256), lambda h, i: (i, h)),
                                 pl.BlockSpec((None, 2, T), lambda h, i: (h, 0, 0))],
        out_specs=[pl.BlockSpec((FQ, 128), lambda h, i: (i, h)), pl.BlockSpec((FQ, 256), lambda h, i: (i, h))],
        out_shape=[jax.ShapeDtypeStruct((T, 512), F32), jax.ShapeDtypeStruct((T, 1024), F32)],
        compiler_params=pltpu.CompilerParams(dimension_semantics=("arbitrary", "arbitrary"), vmem_limit_bytes=VMEM_BIG),
    )(proj, proj, proj, c2, ct3)


def _fox_bwd(proj, c2, ct3, o, lse, do):
    def body(q_ref, k_ref, v_ref, c_ref, ct_ref, o_ref, l_ref, do_ref, dq_ref, dk_ref, dv_ref, dct_ref, dcq_ref):
        i = pl.program_id(1)

        @pl.when(i == 0)
        def _():
            dk_ref[...] = jnp.zeros_like(dk_ref)
            dv_ref[...] = jnp.zeros_like(dv_ref)
            dct_ref[...] = jnp.zeros_like(dct_ref)

        lo = _lane_lo(FQ)
        causal = lax.broadcasted_iota(jnp.int32, (FQ, FQ), 1) <= lax.broadcasted_iota(jnp.int32, (FQ, FQ), 0)
        q = q_ref[...]
        do_v = do_ref[...]
        prod = do_v * o_ref[...]
        qs = [_half(q, lo, a, SCALE) for a in range(2)]
        dos = [_half(do_v, lo, a) for a in range(2)]
        deltas = [jnp.sum(jnp.where(lo if a == 0 else jnp.logical_not(lo), prod, 0.0), axis=-1, keepdims=True)
                  for a in range(2)]
        cqs = [c_ref[:, 128 * a:128 * a + 1] for a in range(2)]
        las = [l_ref[:, 128 * a:128 * a + 1] for a in range(2)]

        def tile(off, carry, diagonal):
            kblk = k_ref[pl.ds(off, FQ), :]
            vblk = v_ref[pl.ds(off, FQ), :]
            new = []
            dk = jnp.zeros((FQ, 128), F32)
            dv = jnp.zeros((FQ, 128), F32)
            for a in range(2):
                dq_acc, rs = carry[a]
                s = _nt(qs[a], kblk) + (cqs[a] - ct_ref[a:a + 1, pl.ds(off, FQ)])
                if diagonal:
                    s = jnp.where(causal, s, NEG)
                p = jnp.exp(s - las[a])
                ds = p * (_nt(dos[a], vblk) - deltas[a])
                dsb = ds.astype(BF16)
                dk = dk + _tn(dsb, qs[a])
                dv = dv + _tn(p.astype(BF16), dos[a])
                dct_ref[a:a + 1, pl.ds(off, FQ)] -= jnp.sum(ds, axis=0, keepdims=True)
                new.append((dq_acc + _nn(dsb, kblk), rs + jnp.sum(ds, axis=-1, keepdims=True)))
            dk_ref[pl.ds(off, FQ), :] += dk
            dv_ref[pl.ds(off, FQ), :] += dv
            return tuple(new)

        init = (jnp.zeros((FQ, 128), F32), jnp.zeros((FQ, 1), F32))
        carry = lax.fori_loop(0, i, lambda kb, c: tile(pl.multiple_of(kb * FQ, FQ), c, False), (init, init))
        carry = tile(pl.multiple_of(i * FQ, FQ), carry, True)
        for a in range(2):
            dcq_ref[:, 128 * a:128 * a + 128] = jnp.broadcast_to(carry[a][1], (FQ, 128))
        dq_ref[...] = jnp.where(lo, carry[0][0], carry[1][0]) * SCALE

    blk = pl.BlockSpec((FQ, 128), lambda h, i: (i, h))
    wide = pl.BlockSpec((FQ, 256), lambda h, i: (i, h))
    rows = pl.BlockSpec((None, 2, T), lambda h, i: (h, 0, 0))
    col = pl.BlockSpec((T, 128), lambda h, i: (0, h))
    return _pcall(
        body, name="fox_bwd", grid=(4, T // FQ),
        in_specs=_fox_specs() + [wide, rows, blk, wide, blk],
        out_specs=[blk, col, col, rows, wide],
        out_shape=[jax.ShapeDtypeStruct((T, 512), F32), jax.ShapeDtypeStruct((T, 512), F32),
                   jax.ShapeDtypeStruct((T, 512), F32), jax.ShapeDtypeStruct((4, 2, T), F32),
                   jax.ShapeDtypeStruct((T, 1024), F32)],
        compiler_params=pltpu.CompilerParams(dimension_semantics=("arbitrary", "arbitrary"), vmem_limit_bytes=VMEM_BIG),
    )(proj, proj, proj, c2, ct3, o, lse, do)


def _rel_onehot():
    ridx = lax.broadcasted_iota(jnp.int32, (NREL_PAD, VW), 0)
    j = lax.broadcasted_iota(jnp.int32, (NREL_PAD, VW), 1)
    return jnp.where(ridx == jnp.clip(TQ + LEFT - 1 - j, -128, 128) + 128, 1.0, 0.0).astype(F32)


def _relvec_fwd(tbl):
    def body(t_ref, v_ref):
        v_ref[...] = _hdot(t_ref[...], _rel_onehot())

    return _one_call(body, "relvec_fwd", [tbl], [((8, VW), F32)])[0]


def _relvec_bwd(gv):
    def body(g_ref, t_ref):
        t_ref[...] = lax.dot_general(g_ref[...], _rel_onehot(), (((1,), (1,)), ((), ())),
                                     preferred_element_type=F32, precision=lax.Precision.HIGHEST)

    return _one_call(body, "relvec_bwd", [gv], [((8, NREL_PAD), F32)])[0]


def _chk_bias(vt_ref, a):
    vb = jnp.broadcast_to(vt_ref[a:a + 1, :], (TQ, VW))
    y = pltpu.roll(vb, VW - (TQ - 1), 1, stride=1, stride_axis=0)[:, :WIN]
    cr = lax.broadcasted_iota(jnp.int32, (TQ, WIN), 0) // 64
    cm = lax.broadcasted_iota(jnp.int32, (TQ, WIN), 1) // 64
    return jnp.where((cm >= cr) & (cm <= cr + 8), y, NEG)


def _chk_specs():
    return [pl.BlockSpec((TQ, 128), lambda h, i: (i, CHK0 // 128 + h)),
            pl.BlockSpec((T + LEFT, 128), lambda h, i: (0, h)),
            pl.BlockSpec((T + LEFT, 128), lambda h, i: (0, 4 + h)),
            pl.BlockSpec((None, 2, VW), lambda h, i: (h, 0, 0))]


def _chk_fwd(proj, kvp, vt3):
    def body(q_ref, k_ref, v_ref, vt_ref, o_ref, l_ref, bias_ref):
        i = pl.program_id(1)

        @pl.when(i == 0)
        def _():
            for a in range(2):
                bias_ref[a] = _chk_bias(vt_ref, a)

        lo = _lane_lo()
        off = pl.multiple_of(i * TQ, TQ)
        kw = k_ref[pl.ds(off, WIN), :]
        vw = v_ref[pl.ds(off, WIN), :]
        real = lax.broadcasted_iota(jnp.int32, (TQ, WIN), 1) + off >= LEFT
        q = q_ref[...]
        outs = []
        for a in range(2):
            s = jnp.where(real, _nt(_half(q, lo, a), kw) * SCALE + bias_ref[a], NEG)
            m = jnp.max(s, axis=-1, keepdims=True)
            p = jnp.exp(s - m)
            l = jnp.sum(p, axis=-1, keepdims=True)
            outs.append(_nn(p.astype(BF16), vw) / l)
            l_ref[:, 128 * a:128 * a + 128] = jnp.broadcast_to(m + jnp.log(l), (TQ, 128))
        o_ref[...] = jnp.where(lo, outs[0], outs[1])

    return _pcall(
        body, name="chk_fwd", grid=(4, T // TQ), in_specs=_chk_specs(),
        out_specs=[pl.BlockSpec((TQ, 128), lambda h, i: (i, h)), pl.BlockSpec((TQ, 256), lambda h, i: (i, h))],
        out_shape=[jax.ShapeDtypeStruct((T, 512), F32), jax.ShapeDtypeStruct((T, 1024), F32)],
        scratch_shapes=[pltpu.VMEM((2, TQ, WIN), F32)],
        compiler_params=pltpu.CompilerParams(dimension_semantics=("arbitrary", "arbitrary")),
    )(proj, kvp, kvp, vt3)


def _chk_bwd(proj, kvp, vt3, o, lse, do):
    nq = T // TQ

    def body(q_ref, k_ref, v_ref, vt_ref, o_ref, l_ref, do_ref, dq_ref, dk_ref, dv_ref, gv_ref, bias_ref, dsum_ref):
        i = pl.program_id(1)

        @pl.when(i == 0)
        def _():
            for a in range(2):
                bias_ref[a] = _chk_bias(vt_ref, a)
            dsum_ref[...] = jnp.zeros_like(dsum_ref)
            dk_ref[...] = jnp.zeros_like(dk_ref)
            dv_ref[...] = jnp.zeros_like(dv_ref)

        lo = _lane_lo()
        off = pl.multiple_of(i * TQ, TQ)
        kw = k_ref[pl.ds(off, WIN), :]
        vw = v_ref[pl.ds(off, WIN), :]
        real = lax.broadcasted_iota(jnp.int32, (TQ, WIN), 1) + off >= LEFT
        q = q_ref[...]
        do_v = do_ref[...]
        prod = do_v * o_ref[...]
        dqs = []
        for a in range(2):
            keep = lo if a == 0 else jnp.logical_not(lo)
            qa = _half(q, lo, a)
            doa = _half(do_v, lo, a)
            delta = jnp.sum(jnp.where(keep, prod, 0.0), axis=-1, keepdims=True)
            s = jnp.where(real, _nt(qa, kw) * SCALE + bias_ref[a], NEG)
            p = jnp.exp(s - l_ref[:, 128 * a:128 * a + 1])
            ds = p * (_nt(doa, vw) - delta)
            dsum_ref[a] += ds
            dsb = ds.astype(BF16)
            dk_ref[pl.ds(off, WIN), :] += _tn(dsb, qa) * SCALE
            dv_ref[pl.ds(off, WIN), :] += _tn(p.astype(BF16), doa)
            dqs.append(_nn(dsb, kw))
        dq_ref[...] = jnp.where(lo, dqs[0], dqs[1]) * SCALE

        @pl.when(i == nq - 1)
        def _():
            rr = lax.broadcasted_iota(jnp.int32, (TQ, TQ), 0)
            cc = lax.broadcasted_iota(jnp.int32, (TQ, TQ), 1)
            flip = jnp.where(rr + cc == TQ - 1, 1.0, 0.0).astype(F32)
            for a in range(2):
                dpad = jnp.concatenate([dsum_ref[a], jnp.zeros((TQ, VW - WIN), F32)], axis=1)
                z = pltpu.roll(_hdot(flip, dpad), 0, 1, stride=1, stride_axis=0)
                gv_ref[a:a + 1, :] = jnp.sum(z, axis=0, keepdims=True)

    blk = pl.BlockSpec((TQ, 128), lambda h, i: (i, h))
    wide = pl.BlockSpec((TQ, 256), lambda h, i: (i, h))
    col = pl.BlockSpec((T + LEFT, 128), lambda h, i: (0, h))
    return _pcall(
        body, name="chk_bwd", grid=(4, nq), in_specs=_chk_specs() + [blk, wide, blk],
        out_specs=[blk, col, col, pl.BlockSpec((None, 2, VW), lambda h, i: (h, 0, 0))],
        out_shape=[jax.ShapeDtypeStruct((T, 512), F32), jax.ShapeDtypeStruct((T + LEFT, 512), F32),
                   jax.ShapeDtypeStruct((T + LEFT, 512), F32), jax.ShapeDtypeStruct((4, 2, VW), F32)],
        scratch_shapes=[pltpu.VMEM((2, TQ, WIN), F32), pltpu.VMEM((2, TQ, WIN), F32)],
        compiler_params=pltpu.CompilerParams(dimension_semantics=("arbitrary", "arbitrary")),
    )(proj, kvp, kvp, vt3, o, lse, do)


def _zero_at_start(*refs):
    @pl.when(pl.program_id(0) == 0)
    def _():
        for r in refs:
            r[...] = jnp.zeros_like(r)


def _ffn_bwd(dx3, y3, x2, a, w1_t, w2, g_post, g_pre):
    def body(dx3_ref, y_ref, x2_ref, a_ref, w1_ref, w2_ref, gp_ref, gf_ref,
             dx2_ref, da_ref, dy_ref, r_ref, dgp_ref, dgf_ref):
        _zero_at_start(dgp_ref, dgf_ref)
        dx3_v = dx3_ref[...]
        dy, dgp = _rms_bwd(y_ref[...], gp_ref[...], dx3_v)
        dgp_ref[...] += dgp
        dyb = dy.astype(BF16)
        dy_ref[...] = dyb
        ra = jnp.maximum(a_ref[...].astype(F32), 0.0)
        r_ref[...] = jnp.square(ra).astype(BF16)
        da = (_nt(dyb, _w(w2_ref)) * (2.0 * ra)).astype(BF16)
        da_ref[...] = da
        dh, dgf = _rms_bwd(x2_ref[...], gf_ref[...], _nn(da, _w(w1_ref)))
        dgf_ref[...] += dgf
        dx2_ref[...] = dx3_v + dh

    return _tok_call(body, "ffn_bwd", [dx3, y3, x2, a], [w1_t, w2, g_post, g_pre],
                     [(D, F32), (DFF, BF16), (D, BF16), (DFF, BF16)], [(1, D), (1, D)], vmem=VMEM_BIG)


def _mem_bwd(dx2, ym, x1, qm, km, vm, w_mo, w_mq, g_post, g_pre):
    def body(dx2_ref, ym_ref, x1_ref, q_ref, k_ref, v_ref, wo_ref, wq_ref, gp_ref, gm_ref,
             dx1_ref, dym_ref, dq_ref, dk_ref, dv_ref, dgp_ref, dgm_ref, dom_ref):
        _zero_at_start(dk_ref, dv_ref, dgp_ref, dgm_ref)
        dx2_v = dx2_ref[...]
        dym, dgp = _rms_bwd(ym_ref[...], gp_ref[...], dx2_v)
        dgp_ref[...] += dgp
        dymb = dym.astype(BF16)
        dym_ref[...] = dymb
        dom_ref[...] = _nt(dymb, _w(wo_ref)).astype(BF16)
        for h in range(MEM_HEADS):
            sl = slice(h * MEM_HD, (h + 1) * MEM_HD)
            qh, kh, doh = q_ref[:, sl], k_ref[:, sl], dom_ref[:, sl]
            s = _nt(qh, kh) * MEM_SCALE
            p = jnp.exp(s - jnp.max(s, axis=-1, keepdims=True))
            p = p / jnp.sum(p, axis=-1, keepdims=True)
            dp = _nt(doh, v_ref[:, sl])
            ds = (p * (dp - jnp.sum(p * dp, axis=-1, keepdims=True))).astype(BF16)
            dq_ref[:, sl] = (_nn(ds, kh) * MEM_SCALE).astype(BF16)
            dk_ref[:, sl] += _tn(ds, qh) * MEM_SCALE
            dv_ref[:, sl] += _tn(p.astype(BF16), doh)
        dh, dgm = _rms_bwd(x1_ref[...], gm_ref[...], _nt(dq_ref[...], _w(wq_ref)))
        dgm_ref[...] += dgm
        dx1_ref[...] = dx2_v + dh

    in_specs = [pl.BlockSpec((TM, D), lambda i: (i, 0))] * 4
    in_specs += [_resident(a) for a in (km, vm, w_mo, w_mq, g_post, g_pre)]
    w_mo, w_mq = w_mo[0], w_mq[0]
    tiled = pl.BlockSpec((TM, D), lambda i: (i, 0))
    kv = pl.BlockSpec((NMEM, D), lambda i: (0, 0))
    vec = pl.BlockSpec((1, D), lambda i: (0, 0))
    return _pcall(
        body, name="mem_bwd", grid=(T // TM,), in_specs=in_specs,
        out_specs=[tiled, tiled, tiled, kv, kv, vec, vec],
        out_shape=[jax.ShapeDtypeStruct((T, D), F32), jax.ShapeDtypeStruct((T, D), BF16),
                   jax.ShapeDtypeStruct((T, D), BF16), jax.ShapeDtypeStruct((NMEM, D), F32),
                   jax.ShapeDtypeStruct((NMEM, D), F32), jax.ShapeDtypeStruct((1, D), F32),
                   jax.ShapeDtypeStruct((1, D), F32)],
        scratch_shapes=[pltpu.VMEM((TM, D), BF16)],
        compiler_params=pltpu.CompilerParams(dimension_semantics=("arbitrary",), vmem_limit_bytes=VMEM_BIG),
    )(dx2, ym, x1, qm, km, vm, w_mo, w_mq, g_post, g_pre)


def _memkv_bwd(dkm, dvm, mem, w_mk, w_mv):
    def body(dk_ref, dv_ref, m_ref, wk_ref, wv_ref, dg_ref):
        dmn = _nt(dk_ref[...].astype(BF16), _w(wk_ref)) + _nt(dv_ref[...].astype(BF16), _w(wv_ref))
        mv = m_ref[...]
        dg_ref[...] = jnp.sum(dmn * (mv * _rstd(mv)), axis=0, keepdims=True)

    return _one_call(body, "memkv_bwd", [dkm, dvm, mem, w_mk, w_mv], [((1, D), F32)], vmem=VMEM_BIG)[0]


def _postmix_bwd(dx1, z, o_f, o_c, w_out, g_post, g_fo, g_co):
    def body(dx1_ref, z_ref, of_ref, oc_ref, wo_ref, gp_ref, gfo_ref, gco_ref,
             dz_ref, dof_ref, doc_ref, dgp_ref, dgfo_ref, dgco_ref):
        _zero_at_start(dgp_ref, dgfo_ref, dgco_ref)
        dz, dgp = _rms_bwd(z_ref[...], gp_ref[...], dx1_ref[...])
        dgp_ref[...] += dgp
        dzb = dz.astype(BF16)
        dz_ref[...] = dzb
        dy = _nt(dzb, _w(wo_ref))
        dof, dgfo = _rms_bwd(of_ref[...], gfo_ref[...], dy[:, :512])
        doc, dgco = _rms_bwd(oc_ref[...], gco_ref[...], dy[:, 512:])
        dof_ref[...] = dof
        doc_ref[...] = doc
        dgfo_ref[...] += dgfo
        dgco_ref[...] += dgco

    return _tok_call(body, "postmix_bwd", [dx1, z, o_f, o_c], [w_out, g_post, g_fo, g_co],
                     [(D, BF16), (512, F32), (512, F32)], [(1, D), (1, 512), (1, 512)], vmem=VMEM_BIG)


def _premix_bwd(dx1, x, dproj, win_t, g_pre):
    def body(dx1_ref, x_ref, dp_ref, w_ref, g_ref, dx_ref, dg_ref):
        _zero_at_start(dg_ref)
        dh, dg = _rms_bwd(x_ref[...], g_ref[...], _nn(dp_ref[...], w_ref[...]))
        dg_ref[...] += dg
        dx_ref[...] = dx1_ref[...] + dh

    return _tok_call(body, "premix_bwd", [dx1, x, dproj], [win_t, g_pre], [(D, F32)], [(1, D)], vmem=VMEM_BIG)


def _wgrad(a, b, name):
    k, m = a.shape
    n = b.shape[1]
    tm = 640 if m % 640 == 0 and m > 1024 else min(m, 512)
    tn = min(n, 1024)

    def body(a_ref, b_ref, o_ref):
        o_ref[...] = _tn(a_ref[...].astype(BF16), b_ref[...].astype(BF16))

    return _pcall(
        body, name=name, grid=(m // tm, n // tn),
        in_specs=[pl.BlockSpec((k, tm), lambda i, j: (0, i)), pl.BlockSpec((k, tn), lambda i, j: (0, j))],
        out_specs=pl.BlockSpec((tm, tn), lambda i, j: (i, j)),
        out_shape=jax.ShapeDtypeStruct((m, n), F32),
        compiler_params=pltpu.CompilerParams(dimension_semantics=("arbitrary", "arbitrary"), vmem_limit_bytes=VMEM_BIG),
    )(a, b)


def _wgrad_group(name, pairs, rows):
    def body(*refs):
        o_ref = refs[-1]
        for k in range(len(pairs)):
            o_ref[k * rows:(k + 1) * rows, :] = _tn(refs[2 * k][...].astype(BF16), refs[2 * k + 1][...].astype(BF16))

    in_specs, ops = [], []
    for a, b in pairs:
        in_specs += [pl.BlockSpec((a.shape[0], rows), lambda j: (0, j)), _resident(b)]
        ops += [a, b]
    return _pcall(
        body, name=name, grid=(8,), in_specs=in_specs,
        out_specs=pl.BlockSpec((None, len(pairs) * rows, D), lambda j: (j, 0, 0)),
        out_shape=jax.ShapeDtypeStruct((8, len(pairs) * rows, D), F32),
        compiler_params=pltpu.CompilerParams(dimension_semantics=("arbitrary",), vmem_limit_bytes=VMEM_BIG),
    )(*ops)


def _adam_math(w, g, m, v):
    m2 = ADAM_B1 * m + (1.0 - ADAM_B1) * g
    v2 = ADAM_B2 * v + (1.0 - ADAM_B2) * jnp.square(g)
    m_hat = m2 / (1.0 - ADAM_B1 ** ADAM_STEP)
    v_hat = v2 / (1.0 - ADAM_B2 ** ADAM_STEP)
    delta = -ADAM_LR * (m_hat / (jnp.sqrt(v_hat) + ADAM_EPS) + ADAM_WD * w)
    return delta, m2, v2


def _adamw(w, g, m, v, name):
    rows, cols = w.shape
    tr = 256 if rows % 256 == 0 else rows

    def body(w_ref, g_ref, m_ref, v_ref, d_ref, m2_ref, v2_ref):
        d_ref[...], m2_ref[...], v2_ref[...] = _adam_math(w_ref[...], g_ref[...], m_ref[...], v_ref[...])

    spec = pl.BlockSpec((tr, cols), lambda i: (i, 0))
    return _pcall(
        body, name=name, grid=(rows // tr,), in_specs=[spec] * 4, out_specs=[spec] * 3,
        out_shape=[jax.ShapeDtypeStruct(w.shape, F32)] * 3,
        compiler_params=pltpu.CompilerParams(dimension_semantics=("arbitrary",)),
    )(w, g, m, v)


def _adamw_small(w, gparts, m, v):
    def body(w_ref, g_ref, m_ref, v_ref, gs_ref, d_ref, m2_ref, v2_ref):
        g = g_ref[0]
        for k in range(1, 8):
            g = g + g_ref[k]
        gs_ref[...] = g
        d_ref[...], m2_ref[...], v2_ref[...] = _adam_math(w_ref[...], g, m_ref[...], v_ref[...])

    return _one_call(body, "adamw_small", [w, gparts, m, v], [(w.shape, F32)] * 4)


def _row_tile(rows):
    return next(t for t in (512, 400, 320) if rows % t == 0)


def _add_halves(g4, theirs, core, name):
    rows = g4.shape[2]
    tr = _row_tile(rows)

    def body(c_ref, a_ref, b_ref, o_ref):
        o_ref[...] = (a_ref[...] + b_ref[...]).astype(BF16)

    grid_spec = pltpu.PrefetchScalarGridSpec(
        num_scalar_prefetch=1, grid=(4, rows // tr),
        in_specs=[pl.BlockSpec((None, None, tr, D), lambda j, i, c: (j, c[0], i, 0)),
                  pl.BlockSpec((None, None, tr, D), lambda j, i, c: (j, 0, i, 0))],
        out_specs=pl.BlockSpec((None, tr, D), lambda j, i, c: (j, i, 0)))
    return _pcall(
        body, name=name, grid_spec=grid_spec, out_shape=jax.ShapeDtypeStruct((4, rows, D), BF16),
        compiler_params=pltpu.CompilerParams(dimension_semantics=("arbitrary", "arbitrary")),
    )(core, g4, theirs)


def _sum_chips(own, got, order, name):
    rows = own.shape[1]
    tr = _row_tile(rows)

    def body(o_ref, a_ref, b_ref, c_ref, d_ref, out_ref):
        f = lambda r: r[...].astype(F32)
        out_ref[...] = ((f(a_ref) + f(b_ref)) + f(c_ref)) + f(d_ref)

    slot = lambda k: pl.BlockSpec((None, tr, D), lambda i, o: (o[k], i, 0))
    grid_spec = pltpu.PrefetchScalarGridSpec(
        num_scalar_prefetch=1, grid=(rows // tr,), in_specs=[slot(0), slot(1), slot(2), slot(3)],
        out_specs=pl.BlockSpec((tr, D), lambda i, o: (i, 0)))
    return _pcall(
        body, name=name, grid_spec=grid_spec, out_shape=jax.ShapeDtypeStruct((rows, D), F32),
        compiler_params=pltpu.CompilerParams(dimension_semantics=("arbitrary",)),
    )(order, own, got, got, got)


def _place():
    return lax.axis_index("x"), lax.axis_index("y"), lax.axis_index("c")


def _allgather(block, name):
    def body(x_ref, out_ref, token, send_sems, recv_sems, local_sem):
        token[...] = jnp.zeros_like(token)
        x, y, c = _place()
        me, sibling = (x, y, c), (x, y, 1 - c)
        chips = [(1 - x, y), (x, 1 - y), (1 - x, 1 - y)]

        def slot(px, py, pc):
            return out_ref.at[4 * px + 2 * py + pc]

        def copy(k, blk, to, src=None):
            return pltpu.make_async_remote_copy(
                src_ref=slot(*blk) if src is None else src, dst_ref=slot(*blk),
                send_sem=send_sems.at[k], recv_sem=recv_sems.at[k], device_id=to, device_id_type=MESH)

        mine = pltpu.make_async_copy(x_ref, slot(*me), local_sem)
        mine.start()
        first = [copy(0, me, sibling, src=x_ref)]
        first += [copy(1 + j, me, (*chip, c), src=x_ref) for j, chip in enumerate(chips)]
        for cp in first:
            cp.start()
        passed = [copy(4 + j, (*chip, c), sibling) for j, chip in enumerate(chips)]
        for j, chip in enumerate(chips):
            copy(1 + j, (*chip, c), me).wait_recv()
            passed[j].start()
        copy(0, sibling, me).wait_recv()
        for j, chip in enumerate(chips):
            copy(4 + j, (*chip, 1 - c), me).wait_recv()
        for cp in first + passed:
            cp.wait_send()
        mine.wait()

    return _pcall(
        body, name=name,
        out_shape=[jax.ShapeDtypeStruct((8,) + block.shape, block.dtype), jax.ShapeDtypeStruct((8, 128), F32)],
        in_specs=[pl.BlockSpec(memory_space=pl.ANY)],
        out_specs=[pl.BlockSpec(memory_space=pl.ANY), pl.BlockSpec(memory_space=pltpu.VMEM)],
        scratch_shapes=[pltpu.SemaphoreType.DMA((7,)), pltpu.SemaphoreType.DMA((7,)), pltpu.SemaphoreType.DMA(())],
        compiler_params=pltpu.CompilerParams(has_side_effects=True),
    )(block)


HBM_SPEC = pl.BlockSpec(memory_space=pltpu.HBM)
SEM_SPEC = pl.BlockSpec(memory_space=pltpu.SEMAPHORE)
ANY_SPEC = pl.BlockSpec(memory_space=pl.ANY)
EFFECT = pltpu.SideEffectType.DATAFLOW_SIDE_EFFECTING


def _in_hbm(a):
    return pltpu.with_memory_space_constraint(a, pltpu.HBM)


def _start_copies(name, src, land_shape, plan, n):
    def body(src_ref, land_ref, send_sems, recv_sems, src_thru, land_thru, token):
        for k, (s, d, to, _) in enumerate(plan(src_ref, land_ref)):
            pltpu.make_async_remote_copy(src_ref=s, dst_ref=d, send_sem=send_sems.at[k], recv_sem=recv_sems.at[k],
                                         device_id=to, device_id_type=MESH).start()
        token[...] = jnp.zeros_like(token)

    return _pcall(
        body, name=name,
        out_shape=(pltpu.SemaphoreType.DMA((n,)), pltpu.SemaphoreType.DMA((n,)), pltpu.HBM(src.shape, src.dtype),
                   pltpu.HBM(land_shape, src.dtype), jax.ShapeDtypeStruct((8, 128), F32)),
        in_specs=(HBM_SPEC, HBM_SPEC),
        out_specs=(SEM_SPEC, SEM_SPEC, HBM_SPEC, HBM_SPEC, pl.BlockSpec(memory_space=pltpu.VMEM)),
        input_output_aliases={0: 2, 1: 3}, compiler_params=pltpu.CompilerParams(has_side_effects=EFFECT),
    )(_in_hbm(src), _in_hbm(lax.empty(land_shape, src.dtype)))


def _wait_copies(name, started, after, plan):
    send_sems, recv_sems, src_thru, land_thru, _ = started

    def body(src_ref, land_ref, send_sems, recv_sems, after_ref, src_out, land_out):
        for k, (s, _, to, mine) in enumerate(plan(src_ref, land_ref)):
            cp = pltpu.make_async_remote_copy(src_ref=s, dst_ref=mine, send_sem=send_sems.at[k],
                                              recv_sem=recv_sems.at[k], device_id=to, device_id_type=MESH)
            cp.wait_send()
            cp.wait_recv()

    return _pcall(
        body, name=name,
        out_shape=(pltpu.HBM(src_thru.shape, src_thru.dtype), pltpu.HBM(land_thru.shape, land_thru.dtype)),
        in_specs=(HBM_SPEC, HBM_SPEC, SEM_SPEC, SEM_SPEC, ANY_SPEC), out_specs=(HBM_SPEC, HBM_SPEC),
        input_output_aliases={0: 0, 1: 1}, compiler_params=pltpu.CompilerParams(has_side_effects=EFFECT),
    )(src_thru, land_thru, send_sems, recv_sems, after)


def _gather_plan(src_ref, land_ref):
    x, y, c = _place()
    peers = [(x, y, 1 - c), (1 - x, y, c), (x, 1 - y, c), (1 - x, 1 - y, c)]
    return [(src_ref, land_ref.at[4 * x + 2 * y + c], p, land_ref.at[4 * p[0] + 2 * p[1] + p[2]]) for p in peers]


def _swap_plan(src_ref, land_ref):
    x, y, c = _place()
    return [(src_ref.at[:, pl.ds(1 - c, 1)], land_ref, (x, y, 1 - c), land_ref)]


def _exchange_plan(src_ref, land_ref):
    x, y, c = _place()
    chips = [(1 - x, y), (x, 1 - y), (1 - x, 1 - y)]
    return [(src_ref.at[2 * px + py], land_ref.at[2 * x + y], (px, py, c), land_ref.at[2 * px + py]) for px, py in chips]


def _gather_forward(land, block):
    def body(land_ref, x_ref, out_ref, send_sems, recv_sems, local_sem):
        x, y, c = _place()
        chips = [(1 - x, y), (x, 1 - y), (1 - x, 1 - y)]
        mine = pltpu.make_async_copy(x_ref, out_ref.at[4 * x + 2 * y + c], local_sem)
        mine.start()

        def copy(k, px, py, pc):
            blk = out_ref.at[4 * px + 2 * py + pc]
            return pltpu.make_async_remote_copy(src_ref=blk, dst_ref=blk, send_sem=send_sems.at[k],
                                                recv_sem=recv_sems.at[k], device_id=(x, y, 1 - c), device_id_type=MESH)

        sent = [copy(k, px, py, c) for k, (px, py) in enumerate(chips)]
        for cp in sent:
            cp.start()
        for k, (px, py) in enumerate(chips):
            copy(k, px, py, 1 - c).wait_recv()
        for cp in sent:
            cp.wait_send()
        mine.wait()

    return _pcall(
        body, name="allgather_rest_forward", out_shape=jax.ShapeDtypeStruct(land.shape, land.dtype),
        in_specs=[ANY_SPEC, ANY_SPEC], out_specs=ANY_SPEC, input_output_aliases={0: 0},
        scratch_shapes=[pltpu.SemaphoreType.DMA((3,)), pltpu.SemaphoreType.DMA((3,)), pltpu.SemaphoreType.DMA(())],
        compiler_params=pltpu.CompilerParams(has_side_effects=True),
    )(land, block)


class _ReduceScatter:
    def __init__(self, name, g):
        self.name = name
        rows = g.shape[1]
        self.started = _start_copies(name + "_swap_start", g.reshape(4, 2, rows, D), (4, 1, rows, D), _swap_plan, 1)
        self.token = self.started[4][0, 0]

    def halfway(self, after):
        g4, theirs = _wait_copies(self.name + "_swap_wait", self.started, after, _swap_plan)
        self.own = _add_halves(g4, theirs, lax.axis_index("c").reshape(1), self.name + "_add_halves")
        self.started = _start_copies(self.name + "_exch_start", self.own, self.own.shape, _exchange_plan, 3)
        self.token = self.started[4][0, 0]

    def finish(self, after):
        own, got = _wait_copies(self.name + "_exch_wait", self.started, after, _exchange_plan)
        chip = 2 * lax.axis_index("x") + lax.axis_index("y")
        order = (chip + jnp.arange(4, dtype=jnp.int32)) % 4
        return _sum_chips(own, got, order, self.name + "_sum_chips")


def _pack_small(p):
    z = lambda a, n: jnp.pad(a, ((0, 0), (0, n - a.shape[1])))
    rows = [z(p['rel_bias'], D), z(p['b_fgt'], D), jnp.concatenate([p['g_fox_out'], p['g_chk_out']], axis=1)]
    rows += [p[n] for n in ('g_mix_pre', 'g_mix_post', 'g_mem_kv', 'g_mem_pre', 'g_mem_post', 'g_ff_pre', 'g_ff_post')]
    rows.append(jnp.zeros((SMALL_ROWS - 17, D), F32))
    return jnp.concatenate(rows, axis=0)


def _unpack_small(a):
    out = {'rel_bias': a[0:8, :257], 'b_fgt': a[8:9, :8], 'g_fox_out': a[9:10, :512], 'g_chk_out': a[9:10, 512:]}
    for k, n in enumerate(('g_mix_pre', 'g_mix_post', 'g_mem_kv', 'g_mem_pre', 'g_mem_post', 'g_ff_pre', 'g_ff_post')):
        out[n] = a[10 + k:11 + k]
    return out


def _local_grads(x, mem, tgt, win_t, gw_of, sm, on_grads):
    b_pad = jnp.pad(sm['b_fgt'], ((0, 0), (0, 120)))
    tbl = jnp.pad(sm['rel_bias'], ((0, 0), (0, NREL_PAD - 257)))

    h1, proj, flog = _premix_fwd(x, sm['g_mix_pre'], win_t)
    c = _gate_fwd(flog, b_pad)
    c8 = c[:, :8]
    c2 = jnp.repeat(c8, 128, axis=1)
    ct3 = c8.T.reshape(4, 2, T)
    o_f, lse_f = _fox_fwd(proj, c2, ct3)
    vt3 = _relvec_fwd(tbl).reshape(4, 2, VW)
    kvp = jnp.pad(proj[:, CHK0 + 512:], ((LEFT, 0), (0, 0)))
    o_c, lse_c = _chk_fwd(proj, kvp, vt3)
    gw = gw_of(o_c)
    w_out, w_mq, w_mk, w_mv, w_mo, w1_t, w2 = (_wblk(gw, n) for n in ('w_out', 'w_mq', 'w_mk', 'w_mv', 'w_mo', 'w_ff1', 'w_ff2'))
    ycat, z, x1, h2, qm = _postmix_fwd(x, o_f, o_c, sm['g_fox_out'], sm['g_chk_out'], w_out,
                                       sm['g_mix_post'], sm['g_mem_pre'], w_mq)
    memn, km, vm = _memkv_fwd(mem, sm['g_mem_kv'], w_mk, w_mv)
    om, ym, x2, h3 = _mem_fwd(qm, x1, km, vm, w_mo, sm['g_mem_post'], sm['g_ff_pre'])
    a, y3, dx3, loss_acc = _ffn_fwd(h3, x2, tgt, w1_t, w2, sm['g_ff_post'])

    gs = {}
    dx2, da, dy3, r, gs['g_ff_post'], gs['g_ff_pre'] = _ffn_bwd(dx3, y3, x2, a, w1_t, w2, sm['g_ff_post'], sm['g_ff_pre'])
    zero = on_grads('A', _wgrad_group("wgrad_ff", [(da, h3), (r, dy3)], 512), None)
    dx1, dym, dqm, dkm, dvm, gs['g_mem_post'], gs['g_mem_pre'] = _mem_bwd(
        dx2, ym, x1, qm, km, vm, w_mo, w_mq, sm['g_mem_post'] + zero, sm['g_mem_pre'])
    zero = on_grads('A halfway', None, dx1)
    gs['g_mem_kv'] = _memkv_bwd(dkm, dvm, mem, w_mk, w_mv)
    dz, dof, doc, gs['g_mix_post'], gs['g_fox_out'], gs['g_chk_out'] = _postmix_bwd(
        dx1, z, o_f, o_c, w_out, sm['g_mix_post'] + zero, sm['g_fox_out'], sm['g_chk_out'])
    zero = on_grads('B', _wgrad_group("wgrad_mem_out", [(ycat, dz), (h2, dqm), (memn, dkm), (memn, dvm), (om, dym)], 128), None)
    dq_f, dk_f, dv_f, dct, dcq = _fox_bwd(proj, c2, ct3 + zero, o_f, lse_f, dof)
    zero = on_grads('B halfway', None, dq_f)
    dq_c, dkp, dvp, gv = _chk_bwd(proj, kvp, vt3 + zero, o_c, lse_c, doc)
    gs['rel_bias'] = _relvec_bwd(gv.reshape(8, VW))[:, :257]
    dc = jnp.pad(dct.reshape(8, T).T + dcq[:, ::128], ((0, 0), (0, 120)))
    dflog, db = _gate_bwd(dc, flog, b_pad)
    gs['b_fgt'] = db[0:1, :8]
    dproj = jnp.concatenate([dq_f, dk_f, dv_f, dflog, dq_c, dkp[LEFT:], dvp[LEFT:]], axis=1).astype(BF16)
    g_in = _wgrad(dproj, h1, "wgrad_in")
    g_in = jnp.concatenate([g_in[:GATE0 + 8], g_in[CHK0:]], axis=0).reshape(8, N_IN, D)
    on_grads('C', jnp.pad(g_in, ((0, 0), (0, R_IN - N_IN), (0, 0))), None)
    grad_x, gs['g_mix_pre'] = _premix_bwd(dx1, x, dproj, win_t, sm['g_mix_pre'])
    return loss_acc[0, 0], grad_x, gs


def kernel(x, mem, w_in, b_fgt, rel_bias, g_fox_out, g_chk_out, w_out, g_mix_pre, g_mix_post, g_mem_kv, w_mq, w_mk, w_mv, w_mo, g_mem_pre, g_mem_post, w_ff1, w_ff2, g_ff_pre, g_ff_post, loss_target, m_w_in, m_b_fgt, m_rel_bias, m_g_fox_out, m_g_chk_out, m_w_out, m_g_mix_pre, m_g_mix_post, m_g_mem_kv, m_w_mq, m_w_mk, m_w_mv, m_w_mo, m_g_mem_pre, m_g_mem_post, m_w_ff1, m_w_ff2, m_g_ff_pre, m_g_ff_post, v_w_in, v_b_fgt, v_rel_bias, v_g_fox_out, v_g_chk_out, v_w_out, v_g_mix_pre, v_g_mix_post, v_g_mem_kv, v_w_mq, v_w_mk, v_w_mv, v_w_mo, v_g_mem_pre, v_g_mem_post, v_w_ff1, v_w_ff2, v_g_ff_pre, v_g_ff_post):
    args = dict(locals())
    two_d = lambda a: a.reshape(a.shape[-2:])
    w = {n: two_d(args[n]) for n in WEIGHTS}
    m = {n: two_d(args['m_' + n]) for n in WEIGHTS}
    v = {n: two_d(args['v_' + n]) for n in WEIGHTS}

    sm = {n: w[n] for n in SMALL}
    shard_in = jnp.pad(w['w_in'].T, ((0, R_IN - N_IN), (0, 0))).astype(BF16)
    gathered_in, zero = _allgather(shard_in, "allgather_w_in")
    wt = gathered_in[:, :N_IN].reshape(8 * N_IN, D)
    win_t = jnp.concatenate([wt[:GATE0], jnp.pad(wt[GATE0:GATE0 + 8], ((0, 120), (0, 0))), wt[GATE0 + 8:]], axis=0)
    shard_rest = (jnp.concatenate([w['w_ff1'].T, w['w_ff2'], w['w_out'], w['w_mq'], w['w_mk'], w['w_mv'], w['w_mo']],
                                  axis=0) + zero[0, 0]).astype(BF16)
    rest = _start_copies("allgather_rest_start", shard_rest, (8, R_REST, D), _gather_plan, 4)
    sm['g_mix_pre'] = sm['g_mix_pre'] + rest[4][0, 0]

    def gw_of(after):
        block, land = _wait_copies("allgather_rest_wait", rest, after, _gather_plan)
        return _gather_forward(land, block)

    rs = {}

    def on_grads(stage, g, after):
        if stage.endswith('halfway'):
            rs[stage[0]].halfway(after)
            return rs[stage[0]].token
        rs[stage] = _ReduceScatter("rs_" + stage.lower(), g)
        if stage == 'C':
            rs[stage].halfway(rs[stage].started[4])
        return rs[stage].token

    loss_local, grad_x, gs = _local_grads(x[0], mem[0], loss_target[0], win_t, gw_of, sm, on_grads)
    loss = lax.psum(loss_local, ("x", "y", "c"))
    g_a, g_b, g_c = (rs[k].finish(grad_x) for k in 'ABC')
    g_big = {'w_in': g_c[:N_IN].T, 'w_ff1': g_a[:512].T, 'w_ff2': g_a[512:]}
    for k, n in enumerate(('w_out', 'w_mq', 'w_mk', 'w_mv', 'w_mo')):
        g_big[n] = g_b[128 * k:128 * (k + 1)]

    grads, deltas, new_m, new_v = {}, {}, {}, {}
    for n in BIG:
        grads[n] = g_big[n]
        deltas[n], new_m[n], new_v[n] = _adamw(w[n], g_big[n], m[n], v[n], "adamw_" + n)

    gparts, _ = _allgather(_pack_small(gs), "allgather_small_grads")
    gsum, d_s, m_s, v_s = _adamw_small(_pack_small(sm), gparts, _pack_small({n: m[n] for n in SMALL}),
                                       _pack_small({n: v[n] for n in SMALL}))
    for dst, packed in ((grads, gsum), (deltas, d_s), (new_m, m_s), (new_v, v_s)):
        dst.update(_unpack_small(packed))

    out = [loss, grad_x[None]]
    for group in (grads, deltas, new_m, new_v):
        out += [group[n].reshape(args[n].shape) for n in WEIGHTS]
    return tuple(out)
```

```python
import functools

import jax
import jax.numpy as jnp
from jax import lax
from jax.experimental import pallas as pl
from jax.experimental.pallas import tpu as pltpu

F32 = jnp.float32
BF16 = jnp.bfloat16
MESH = pl.DeviceIdType.MESH

T = 2048
D = 1024
NMEM = 256
DFF = 4096
EPS = 1e-6
TM = 256
TQ = 256
FQ = 512
HD = 64
SCALE = HD ** -0.5
MEM_HEADS = 4
MEM_HD = 256
MEM_SCALE = MEM_HD ** -0.5
NEG = -1e30
LEFT = 512
WIN = LEFT + TQ
VW = 1024
NREL_PAD = 384
PROJ = 3200
GATE0 = 1536
CHK0 = 1664
VMEM_BIG = 56 * 1024 * 1024

ADAM_LR = 0.001
ADAM_B1 = 0.9
ADAM_B2 = 0.999
ADAM_EPS = 1e-08
ADAM_WD = 0.01
ADAM_STEP = 10

N_IN = 385
R_IN = 400
R_REST = 1664
W_ROWS = {'w_ff1': (0, 512), 'w_ff2': (512, 512),
          'w_out': (1024, 128), 'w_mq': (1152, 128), 'w_mk': (1280, 128), 'w_mv': (1408, 128), 'w_mo': (1536, 128)}
R_A, R_B = 1024, 640
SMALL_ROWS = 24

WEIGHTS = ['w_in', 'b_fgt', 'rel_bias', 'g_fox_out', 'g_chk_out', 'w_out', 'g_mix_pre', 'g_mix_post', 'g_mem_kv',
           'w_mq', 'w_mk', 'w_mv', 'w_mo', 'g_mem_pre', 'g_mem_post', 'w_ff1', 'w_ff2', 'g_ff_pre', 'g_ff_post']
BIG = ['w_in', 'w_out', 'w_mq', 'w_mk', 'w_mv', 'w_mo', 'w_ff1', 'w_ff2']
SMALL = [n for n in WEIGHTS if n not in BIG]


def _pcall(body, **kw):
    return pl.pallas_call(body, **kw)


def _nn(a, b):
    return jnp.dot(a, b, preferred_element_type=F32)


def _nt(a, b):
    return lax.dot_general(a, b, (((1,), (1,)), ((), ())), preferred_element_type=F32)


def _tn(a, b):
    return lax.dot_general(a, b, (((0,), (0,)), ((), ())), preferred_element_type=F32)


def _w(ref):
    v = ref[...]
    return v if v.ndim == 2 else v.reshape(-1, v.shape[-1])


def _rstd(x):
    return lax.rsqrt(jnp.mean(x * x, axis=-1, keepdims=True) + EPS)


def _rms(x, g):
    return x * _rstd(x) * g


def _rms_bwd(x, g, dy):
    r = _rstd(x)
    xh = x * r
    dg = jnp.sum(dy * xh, axis=0, keepdims=True)
    dxh = dy * g
    dx = r * (dxh - xh * jnp.mean(dxh * xh, axis=-1, keepdims=True))
    return dx, dg


def _resident(a):
    if isinstance(a, tuple):
        _, shape, index = a
        return pl.BlockSpec(shape, lambda *_: index, pipeline_mode=pl.Buffered(1))
    return pl.BlockSpec(a.shape, lambda *_, nd=a.ndim: (0,) * nd, pipeline_mode=pl.Buffered(1))


def _wblk(gw, name):
    r0, rows = W_ROWS[name]
    return (gw, (8, rows, D), (0, r0 // rows, 0))


def _tok_call(body, name, tiled, full, outs_tiled, outs_acc=(), rows=T, tm=TM, vmem=None):
    in_specs = [pl.BlockSpec((tm, a.shape[1]), lambda i: (i, 0)) for a in tiled]
    in_specs += [_resident(a) for a in full]
    full = [a[0] if isinstance(a, tuple) else a for a in full]
    out_shape = [jax.ShapeDtypeStruct((rows, c), dt) for c, dt in outs_tiled]
    out_shape += [jax.ShapeDtypeStruct(s, F32) for s in outs_acc]
    out_specs = [pl.BlockSpec((tm, c), lambda i: (i, 0)) for c, _ in outs_tiled]
    out_specs += [pl.BlockSpec(s, lambda i, nd=len(s): (0,) * nd) for s in outs_acc]
    return _pcall(
        body, name=name, grid=(rows // tm,), in_specs=in_specs, out_specs=out_specs, out_shape=out_shape,
        compiler_params=pltpu.CompilerParams(dimension_semantics=("arbitrary",), vmem_limit_bytes=vmem),
    )(*tiled, *full)


def _one_call(body, name, ins, outs, vmem=None):
    whole = lambda s: pl.BlockSpec(s, lambda i, nd=len(s): (0,) * nd)
    return _pcall(
        body, name=name, grid=(1,), in_specs=[_resident(a) for a in ins], out_specs=[whole(s) for s, _ in outs],
        out_shape=[jax.ShapeDtypeStruct(s, dt) for s, dt in outs],
        compiler_params=pltpu.CompilerParams(dimension_semantics=("arbitrary",), vmem_limit_bytes=vmem),
    )(*[a[0] if isinstance(a, tuple) else a for a in ins])


def _premix_fwd(x, g_pre, win_t):
    def body(x_ref, g_ref, w_ref, h_ref, proj_ref, flog_ref):
        h = _rms(x_ref[...], g_ref[...]).astype(BF16)
        h_ref[...] = h
        p = _nt(h, w_ref[...])
        proj_ref[...] = p.astype(BF16)
        flog_ref[...] = p[:, GATE0:GATE0 + 128]

    return _tok_call(body, "premix_fwd", [x], [g_pre, win_t],
                     [(D, BF16), (PROJ, BF16), (128, F32)], vmem=VMEM_BIG)


def _postmix_fwd(x, o_f, o_c, g_fo, g_co, w_out, g_post, g_mpre, w_mq):
    def body(x_ref, of_ref, oc_ref, gfo_ref, gco_ref, wo_ref, gp_ref, gm_ref, wq_ref,
             y_ref, z_ref, x1_ref, h2_ref, qm_ref):
        y_ref[:, :512] = _rms(of_ref[...], gfo_ref[...]).astype(BF16)
        y_ref[:, 512:] = _rms(oc_ref[...], gco_ref[...]).astype(BF16)
        z = _nn(y_ref[...], _w(wo_ref))
        z_ref[...] = z
        x1 = x_ref[...] + _rms(z, gp_ref[...])
        x1_ref[...] = x1
        h2 = _rms(x1, gm_ref[...]).astype(BF16)
        h2_ref[...] = h2
        qm_ref[...] = _nn(h2, _w(wq_ref)).astype(BF16)

    return _tok_call(body, "postmix_fwd", [x, o_f, o_c], [g_fo, g_co, w_out, g_post, g_mpre, w_mq],
                     [(D, BF16), (D, F32), (D, F32), (D, BF16), (D, BF16)], vmem=VMEM_BIG)


def _memkv_fwd(mem, g_kv, w_mk, w_mv):
    def body(m_ref, g_ref, wk_ref, wv_ref, mn_ref, k_ref, v_ref):
        mn = _rms(m_ref[...], g_ref[...]).astype(BF16)
        mn_ref[...] = mn
        k_ref[...] = _nn(mn, _w(wk_ref)).astype(BF16)
        v_ref[...] = _nn(mn, _w(wv_ref)).astype(BF16)

    return _tok_call(body, "memkv_fwd", [mem], [g_kv, w_mk, w_mv],
                     [(D, BF16), (D, BF16), (D, BF16)], rows=NMEM, tm=NMEM, vmem=VMEM_BIG)


def _mem_fwd(qm, x1, km, vm, w_mo, g_post, g_fpre):
    def body(q_ref, x1_ref, k_ref, v_ref, wo_ref, gp_ref, gf_ref, om_ref, ym_ref, x2_ref, h3_ref):
        for h in range(MEM_HEADS):
            sl = slice(h * MEM_HD, (h + 1) * MEM_HD)
            s = _nt(q_ref[:, sl], k_ref[:, sl]) * MEM_SCALE
            p = jnp.exp(s - jnp.max(s, axis=-1, keepdims=True))
            p = p / jnp.sum(p, axis=-1, keepdims=True)
            om_ref[:, sl] = _nn(p.astype(BF16), v_ref[:, sl]).astype(BF16)
        ym = _nn(om_ref[...], _w(wo_ref))
        ym_ref[...] = ym
        x2 = x1_ref[...] + _rms(ym, gp_ref[...])
        x2_ref[...] = x2
        h3_ref[...] = _rms(x2, gf_ref[...]).astype(BF16)

    return _tok_call(body, "mem_fwd", [qm, x1], [km, vm, w_mo, g_post, g_fpre],
                     [(D, BF16), (D, F32), (D, F32), (D, BF16)], vmem=VMEM_BIG)


def _ffn_fwd(h3, x2, tgt, w1_t, w2, g_post):
    def body(h_ref, x2_ref, t_ref, w1_ref, w2_ref, g_ref, a_ref, y_ref, dx_ref, loss_ref):
        @pl.when(pl.program_id(0) == 0)
        def _():
            loss_ref[...] = jnp.zeros_like(loss_ref)

        a = _nt(h_ref[...], _w(w1_ref))
        a_ref[...] = a.astype(BF16)
        r = jnp.square(jnp.maximum(a, 0.0)).astype(BF16)
        y = _nn(r, _w(w2_ref))
        y_ref[...] = y
        e = x2_ref[...] + _rms(y, g_ref[...]) - t_ref[...]
        dx_ref[...] = e * (1.0 / D)
        loss_ref[...] += 0.5 * jnp.sum(jnp.sum(e * e, axis=-1, keepdims=True) * (1.0 / D))

    return _tok_call(body, "ffn_fwd", [h3, x2, tgt], [w1_t, w2, g_post],
                     [(DFF, BF16), (D, F32), (D, F32)], [(8, 128)], vmem=VMEM_BIG)


def _tri(lower):
    r = lax.broadcasted_iota(jnp.int32, (128, 128), 0)
    c = lax.broadcasted_iota(jnp.int32, (128, 128), 1)
    return jnp.where(r >= c if lower else c >= r, 1.0, 0.0).astype(F32)


def _hdot(a, b):
    return jnp.dot(a, b, preferred_element_type=F32, precision=lax.Precision.HIGHEST)


def _gate_fwd(flog, b_pad):
    def body(f_ref, b_ref, c_ref):
        tri = _tri(True)

        def step(i, carry):
            rows = pl.ds(pl.multiple_of(i * 128, 128), 128)
            z = f_ref[rows, :] + b_ref[...]
            lf = jnp.minimum(z, 0.0) - jnp.log(1.0 + jnp.exp(-jnp.abs(z)))
            cb = _hdot(tri, lf) + carry
            c_ref[rows, :] = cb
            return cb[127:128, :]

        lax.fori_loop(0, T // 128, step, jnp.zeros((1, 128), F32))

    return _one_call(body, "gate_fwd", [flog, b_pad], [((T, 128), F32)])[0]


def _gate_bwd(dc, flog, b_pad):
    def body(dc_ref, f_ref, b_ref, df_ref, db_ref):
        tri = _tri(False)

        def step(j, carry):
            run, db = carry
            i = T // 128 - 1 - j
            rows = pl.ds(pl.multiple_of(i * 128, 128), 128)
            dcb = dc_ref[rows, :]
            rb = _hdot(tri, dcb) + run
            z = f_ref[rows, :] + b_ref[...]
            df = rb * (1.0 / (1.0 + jnp.exp(z)))
            df_ref[rows, :] = df
            return run + jnp.sum(dcb, axis=0, keepdims=True), db + jnp.sum(df, axis=0, keepdims=True)

        _, db = lax.fori_loop(0, T // 128, step, (jnp.zeros((1, 128), F32), jnp.zeros((1, 128), F32)))
        db_ref[...] = jnp.broadcast_to(db, (8, 128))

    return _one_call(body, "gate_bwd", [dc, flog, b_pad], [((T, 128), F32), ((8, 128), F32)])


def _lane_lo(rows=TQ):
    return lax.broadcasted_iota(jnp.int32, (rows, 128), 1) < HD


def _half(v, lo, a, scale=None):
    keep = lo if a == 0 else jnp.logical_not(lo)
    v = v.astype(F32) if scale is None else v.astype(F32) * scale
    return jnp.where(keep, v, 0.0).astype(BF16)


def _fox_specs():
    return [pl.BlockSpec((FQ, 128), lambda h, i: (i, h)),
            pl.BlockSpec((T, 128), lambda h, i: (0, 4 + h)),
            pl.BlockSpec((T, 128), lambda h, i: (0, 8 + h))]


def _fox_fwd(proj, c2, ct3):
    def body(q_ref, k_ref, v_ref, c_ref, ct_ref, o_ref, l_ref):
        i = pl.program_id(1)
        lo = _lane_lo(FQ)
        causal = lax.broadcasted_iota(jnp.int32, (FQ, FQ), 1) <= lax.broadcasted_iota(jnp.int32, (FQ, FQ), 0)
        q = q_ref[...]
        qs = [_half(q, lo, a, SCALE) for a in range(2)]
        cqs = [c_ref[:, 128 * a:128 * a + 1] for a in range(2)]

        def tile(off, carry, diagonal):
            kblk = k_ref[pl.ds(off, FQ), :]
            vblk = v_ref[pl.ds(off, FQ), :]
            new = []
            for a in range(2):
                m, l, acc = carry[a]
                s = _nt(qs[a], kblk) + (cqs[a] - ct_ref[a:a + 1, pl.ds(off, FQ)])
                if diagonal:
                    s = jnp.where(causal, s, NEG)
                m2 = jnp.maximum(m, jnp.max(s, axis=-1, keepdims=True))
                p = jnp.exp(s - m2)
                alpha = jnp.exp(m - m2)
                new.append((m2, alpha * l + jnp.sum(p, axis=-1, keepdims=True),
                            alpha * acc + _nn(p.astype(BF16), vblk)))
            return tuple(new)

        init = (jnp.full((FQ, 1), NEG, F32), jnp.zeros((FQ, 1), F32), jnp.zeros((FQ, 128), F32))
        carry = lax.fori_loop(0, i, lambda kb, c: tile(pl.multiple_of(kb * FQ, FQ), c, False), (init, init))
        carry = tile(pl.multiple_of(i * FQ, FQ), carry, True)
        outs = []
        for a in range(2):
            m, l, acc = carry[a]
            outs.append(acc / l)
            l_ref[:, 128 * a:128 * a + 128] = jnp.broadcast_to(m + jnp.log(l), (FQ, 128))
        o_ref[...] = jnp.where(lo, outs[0], outs[1])

    return _pcall(
        body, name="fox_fwd", grid=(4, T // FQ),
        in_specs=_fox_specs() + [pl.BlockSpec((FQ, 256), lambda h, i: (i, h)),
                                 pl.BlockSpec((None, 2, T), lambda h, i: (h, 0, 0))],
        out_specs=[pl.BlockSpec((FQ, 128), lambda h, i: (i, h)), pl.BlockSpec((FQ, 256), lambda h, i: (i, h))],
        out_shape=[jax.ShapeDtypeStruct((T, 512), F32), jax.ShapeDtypeStruct((T, 1024), F32)],
        compiler_params=pltpu.CompilerParams(dimension_semantics=("arbitrary", "arbitrary"), vmem_limit_bytes=VMEM_BIG),
    )(proj, proj, proj, c2, ct3)


def _fox_bwd(proj, c2, ct3, o, lse, do):
    def body(q_ref, k_ref, v_ref, c_ref, ct_ref, o_ref, l_ref, do_ref, dq_ref, dk_ref, dv_ref, dct_ref, dcq_ref):
        i = pl.program_id(1)

        @pl.when(i == 0)
        def _():
            dk_ref[...] = jnp.zeros_like(dk_ref)
            dv_ref[...] = jnp.zeros_like(dv_ref)
            dct_ref[...] = jnp.zeros_like(dct_ref)

        lo = _lane_lo(FQ)
        causal = lax.broadcasted_iota(jnp.int32, (FQ, FQ), 1) <= lax.broadcasted_iota(jnp.int32, (FQ, FQ), 0)
        q = q_ref[...]
        do_v = do_ref[...]
        prod = do_v * o_ref[...]
        qs = [_half(q, lo, a, SCALE) for a in range(2)]
        dos = [_half(do_v, lo, a) for a in range(2)]
        deltas = [jnp.sum(jnp.where(lo if a == 0 else jnp.logical_not(lo), prod, 0.0), axis=-1, keepdims=True)
                  for a in range(2)]
        cqs = [c_ref[:, 128 * a:128 * a + 1] for a in range(2)]
        las = [l_ref[:, 128 * a:128 * a + 1] for a in range(2)]

        def tile(off, carry, diagonal):
            kblk = k_ref[pl.ds(off, FQ), :]
            vblk = v_ref[pl.ds(off, FQ), :]
            new = []
            dk = jnp.zeros((FQ, 128), F32)
            dv = jnp.zeros((FQ, 128), F32)
            for a in range(2):
                dq_acc, rs = carry[a]
                s = _nt(qs[a], kblk) + (cqs[a] - ct_ref[a:a + 1, pl.ds(off, FQ)])
                if diagonal:
                    s = jnp.where(causal, s, NEG)
                p = jnp.exp(s - las[a])
                ds = p * (_nt(dos[a], vblk) - deltas[a])
                dsb = ds.astype(BF16)
                dk = dk + _tn(dsb, qs[a])
                dv = dv + _tn(p.astype(BF16), dos[a])
                dct_ref[a:a + 1, pl.ds(off, FQ)] -= jnp.sum(ds, axis=0, keepdims=True)
                new.append((dq_acc + _nn(dsb, kblk), rs + jnp.sum(ds, axis=-1, keepdims=True)))
            dk_ref[pl.ds(off, FQ), :] += dk
            dv_ref[pl.ds(off, FQ), :] += dv
            return tuple(new)

        init = (jnp.zeros((FQ, 128), F32), jnp.zeros((FQ, 1), F32))
        carry = lax.fori_loop(0, i, lambda kb, c: tile(pl.multiple_of(kb * FQ, FQ), c, False), (init, init))
        carry = tile(pl.multiple_of(i * FQ, FQ), carry, True)
        for a in range(2):
            dcq_ref[:, 128 * a:128 * a + 128] = jnp.broadcast_to(carry[a][1], (FQ, 128))
        dq_ref[...] = jnp.where(lo, carry[0][0], carry[1][0]) * SCALE

    blk = pl.BlockSpec((FQ, 128), lambda h, i: (i, h))
    wide = pl.BlockSpec((FQ, 256), lambda h, i: (i, h))
    rows = pl.BlockSpec((None, 2, T), lambda h, i: (h, 0, 0))
    col = pl.BlockSpec((T, 128), lambda h, i: (0, h))
    return _pcall(
        body, name="fox_bwd", grid=(4, T // FQ),
        in_specs=_fox_specs() + [wide, rows, blk, wide, blk],
        out_specs=[blk, col, col, rows, wide],
        out_shape=[jax.ShapeDtypeStruct((T, 512), F32), jax.ShapeDtypeStruct((T, 512), F32),
                   jax.ShapeDtypeStruct((T, 512), F32), jax.ShapeDtypeStruct((4, 2, T), F32),
                   jax.ShapeDtypeStruct((T, 1024), F32)],
        compiler_params=pltpu.CompilerParams(dimension_semantics=("arbitrary", "arbitrary"), vmem_limit_bytes=VMEM_BIG),
    )(proj, proj, proj, c2, ct3, o, lse, do)


def _rel_onehot():
    ridx = lax.broadcasted_iota(jnp.int32, (NREL_PAD, VW), 0)
    j = lax.broadcasted_iota(jnp.int32, (NREL_PAD, VW), 1)
    return jnp.where(ridx == jnp.clip(TQ + LEFT - 1 - j, -128, 128) + 128, 1.0, 0.0).astype(F32)


def _relvec_fwd(tbl):
    def body(t_ref, v_ref):
        v_ref[...] = _hdot(t_ref[...], _rel_onehot())

    return _one_call(body, "relvec_fwd", [tbl], [((8, VW), F32)])[0]


def _relvec_bwd(gv):
    def body(g_ref, t_ref):
        t_ref[...] = lax.dot_general(g_ref[...], _rel_onehot(), (((1,), (1,)), ((), ())),
                                     preferred_element_type=F32, precision=lax.Precision.HIGHEST)

    return _one_call(body, "relvec_bwd", [gv], [((8, NREL_PAD), F32)])[0]


def _chk_bias(vt_ref, a):
    vb = jnp.broadcast_to(vt_ref[a:a + 1, :], (TQ, VW))
    y = pltpu.roll(vb, VW - (TQ - 1), 1, stride=1, stride_axis=0)[:, :WIN]
    cr = lax.broadcasted_iota(jnp.int32, (TQ, WIN), 0) // 64
    cm = lax.broadcasted_iota(jnp.int32, (TQ, WIN), 1) // 64
    return jnp.where((cm >= cr) & (cm <= cr + 8), y, NEG)


def _chk_specs():
    return [pl.BlockSpec((TQ, 128), lambda h, i: (i, CHK0 // 128 + h)),
            pl.BlockSpec((T + LEFT, 128), lambda h, i: (0, h)),
            pl.BlockSpec((T + LEFT, 128), lambda h, i: (0, 4 + h)),
            pl.BlockSpec((None, 2, VW), lambda h, i: (h, 0, 0))]


def _chk_fwd(proj, kvp, vt3):
    def body(q_ref, k_ref, v_ref, vt_ref, o_ref, l_ref, bias_ref):
        i = pl.program_id(1)

        @pl.when(i == 0)
        def _():
            for a in range(2):
                bias_ref[a] = _chk_bias(vt_ref, a)

        lo = _lane_lo()
        off = pl.multiple_of(i * TQ, TQ)
        kw = k_ref[pl.ds(off, WIN), :]
        vw = v_ref[pl.ds(off, WIN), :]
        real = lax.broadcasted_iota(jnp.int32, (TQ, WIN), 1) + off >= LEFT
        q = q_ref[...]
        outs = []
        for a in range(2):
            s = jnp.where(real, _nt(_half(q, lo, a), kw) * SCALE + bias_ref[a], NEG)
            m = jnp.max(s, axis=-1, keepdims=True)
            p = jnp.exp(s - m)
            l = jnp.sum(p, axis=-1, keepdims=True)
            outs.append(_nn(p.astype(BF16), vw) / l)
            l_ref[:, 128 * a:128 * a + 128] = jnp.broadcast_to(m + jnp.log(l), (TQ, 128))
        o_ref[...] = jnp.where(lo, outs[0], outs[1])

    return _pcall(
        body, name="chk_fwd", grid=(4, T // TQ), in_specs=_chk_specs(),
        out_specs=[pl.BlockSpec((TQ, 128), lambda h, i: (i, h)), pl.BlockSpec((TQ, 256), lambda h, i: (i, h))],
        out_shape=[jax.ShapeDtypeStruct((T, 512), F32), jax.ShapeDtypeStruct((T, 1024), F32)],
        scratch_shapes=[pltpu.VMEM((2, TQ, WIN), F32)],
        compiler_params=pltpu.CompilerParams(dimension_semantics=("arbitrary", "arbitrary")),
    )(proj, kvp, kvp, vt3)


def _chk_bwd(proj, kvp, vt3, o, lse, do):
    nq = T // TQ

    def body(q_ref, k_ref, v_ref, vt_ref, o_ref, l_ref, do_ref, dq_ref, dk_ref, dv_ref, gv_ref, bias_ref, dsum_ref):
        i = pl.program_id(1)

        @pl.when(i == 0)
        def _():
            for a in range(2):
                bias_ref[a] = _chk_bias(vt_ref, a)
            dsum_ref[...] = jnp.zeros_like(dsum_ref)
            dk_ref[...] = jnp.zeros_like(dk_ref)
            dv_ref[...] = jnp.zeros_like(dv_ref)

        lo = _lane_lo()
        off = pl.multiple_of(i * TQ, TQ)
        kw = k_ref[pl.ds(off, WIN), :]
        vw = v_ref[pl.ds(off, WIN), :]
        real = lax.broadcasted_iota(jnp.int32, (TQ, WIN), 1) + off >= LEFT
        q = q_ref[...]
        do_v = do_ref[...]
        prod = do_v * o_ref[...]
        dqs = []
        for a in range(2):
            keep = lo if a == 0 else jnp.logical_not(lo)
            qa = _half(q, lo, a)
            doa = _half(do_v, lo, a)
            delta = jnp.sum(jnp.where(keep, prod, 0.0), axis=-1, keepdims=True)
            s = jnp.where(real, _nt(qa, kw) * SCALE + bias_ref[a], NEG)
            p = jnp.exp(s - l_ref[:, 128 * a:128 * a + 1])
            ds = p * (_nt(doa, vw) - delta)
            dsum_ref[a] += ds
            dsb = ds.astype(BF16)
            dk_ref[pl.ds(off, WIN), :] += _tn(dsb, qa) * SCALE
            dv_ref[pl.ds(off, WIN), :] += _tn(p.astype(BF16), doa)
            dqs.append(_nn(dsb, kw))
        dq_ref[...] = jnp.where(lo, dqs[0], dqs[1]) * SCALE

        @pl.when(i == nq - 1)
        def _():
            rr = lax.broadcasted_iota(jnp.int32, (TQ, TQ), 0)
            cc = lax.broadcasted_iota(jnp.int32, (TQ, TQ), 1)
            flip = jnp.where(rr + cc == TQ - 1, 1.0, 0.0).astype(F32)
            for a in range(2):
                dpad = jnp.concatenate([dsum_ref[a], jnp.zeros((TQ, VW - WIN), F32)], axis=1)
                z = pltpu.roll(_hdot(flip, dpad), 0, 1, stride=1, stride_axis=0)
                gv_ref[a:a + 1, :] = jnp.sum(z, axis=0, keepdims=True)

    blk = pl.BlockSpec((TQ, 128), lambda h, i: (i, h))
    wide = pl.BlockSpec((TQ, 256), lambda h, i: (i, h))
    col = pl.BlockSpec((T + LEFT, 128), lambda h, i: (0, h))
    return _pcall(
        body, name="chk_bwd", grid=(4, nq), in_specs=_chk_specs() + [blk, wide, blk],
        out_specs=[blk, col, col, pl.BlockSpec((None, 2, VW), lambda h, i: (h, 0, 0))],
        out_shape=[jax.ShapeDtypeStruct((T, 512), F32), jax.ShapeDtypeStruct((T + LEFT, 512), F32),
                   jax.ShapeDtypeStruct((T + LEFT, 512), F32), jax.ShapeDtypeStruct((4, 2, VW), F32)],
        scratch_shapes=[pltpu.VMEM((2, TQ, WIN), F32), pltpu.VMEM((2, TQ, WIN), F32)],
        compiler_params=pltpu.CompilerParams(dimension_semantics=("arbitrary", "arbitrary")),
    )(proj, kvp, kvp, vt3, o, lse, do)


def _zero_at_start(*refs):
    @pl.when(pl.program_id(0) == 0)
    def _():
        for r in refs:
            r[...] = jnp.zeros_like(r)


def _ffn_bwd(dx3, y3, x2, a, w1_t, w2, g_post, g_pre):
    def body(dx3_ref, y_ref, x2_ref, a_ref, w1_ref, w2_ref, gp_ref, gf_ref,
             dx2_ref, da_ref, dy_ref, r_ref, dgp_ref, dgf_ref):
        _zero_at_start(dgp_ref, dgf_ref)
        dx3_v = dx3_ref[...]
        dy, dgp = _rms_bwd(y_ref[...], gp_ref[...], dx3_v)
        dgp_ref[...] += dgp
        dyb = dy.astype(BF16)
        dy_ref[...] = dyb
        ra = jnp.maximum(a_ref[...].astype(F32), 0.0)
        r_ref[...] = jnp.square(ra).astype(BF16)
        da = (_nt(dyb, _w(w2_ref)) * (2.0 * ra)).astype(BF16)
        da_ref[...] = da
        dh, dgf = _rms_bwd(x2_ref[...], gf_ref[...], _nn(da, _w(w1_ref)))
        dgf_ref[...] += dgf
        dx2_ref[...] = dx3_v + dh

    return _tok_call(body, "ffn_bwd", [dx3, y3, x2, a], [w1_t, w2, g_post, g_pre],
                     [(D, F32), (DFF, BF16), (D, BF16), (DFF, BF16)], [(1, D), (1, D)], vmem=VMEM_BIG)


def _mem_bwd(dx2, ym, x1, qm, km, vm, w_mo, w_mq, g_post, g_pre):
    def body(dx2_ref, ym_ref, x1_ref, q_ref, k_ref, v_ref, wo_ref, wq_ref, gp_ref, gm_ref,
             dx1_ref, dym_ref, dq_ref, dk_ref, dv_ref, dgp_ref, dgm_ref, dom_ref):
        _zero_at_start(dk_ref, dv_ref, dgp_ref, dgm_ref)
        dx2_v = dx2_ref[...]
        dym, dgp = _rms_bwd(ym_ref[...], gp_ref[...], dx2_v)
        dgp_ref[...] += dgp
        dymb = dym.astype(BF16)
        dym_ref[...] = dymb
        dom_ref[...] = _nt(dymb, _w(wo_ref)).astype(BF16)
        for h in range(MEM_HEADS):
            sl = slice(h * MEM_HD, (h + 1) * MEM_HD)
            qh, kh, doh = q_ref[:, sl], k_ref[:, sl], dom_ref[:, sl]
            s = _nt(qh, kh) * MEM_SCALE
            p = jnp.exp(s - jnp.max(s, axis=-1, keepdims=True))
            p = p / jnp.sum(p, axis=-1, keepdims=True)
            dp = _nt(doh, v_ref[:, sl])
            ds = (p * (dp - jnp.sum(p * dp, axis=-1, keepdims=True))).astype(BF16)
            dq_ref[:, sl] = (_nn(ds, kh) * MEM_SCALE).astype(BF16)
            dk_ref[:, sl] += _tn(ds, qh) * MEM_SCALE
            dv_ref[:, sl] += _tn(p.astype(BF16), doh)
        dh, dgm = _rms_bwd(x1_ref[...], gm_ref[...], _nt(dq_ref[...], _w(wq_ref)))
        dgm_ref[...] += dgm
        dx1_ref[...] = dx2_v + dh

    in_specs = [pl.BlockSpec((TM, D), lambda i: (i, 0))] * 4
    in_specs += [_resident(a) for a in (km, vm, w_mo, w_mq, g_post, g_pre)]
    w_mo, w_mq = w_mo[0], w_mq[0]
    tiled = pl.BlockSpec((TM, D), lambda i: (i, 0))
    kv = pl.BlockSpec((NMEM, D), lambda i: (0, 0))
    vec = pl.BlockSpec((1, D), lambda i: (0, 0))
    return _pcall(
        body, name="mem_bwd", grid=(T // TM,), in_specs=in_specs,
        out_specs=[tiled, tiled, tiled, kv, kv, vec, vec],
        out_shape=[jax.ShapeDtypeStruct((T, D), F32), jax.ShapeDtypeStruct((T, D), BF16),
                   jax.ShapeDtypeStruct((T, D), BF16), jax.ShapeDtypeStruct((NMEM, D), F32),
                   jax.ShapeDtypeStruct((NMEM, D), F32), jax.ShapeDtypeStruct((1, D), F32),
                   jax.ShapeDtypeStruct((1, D), F32)],
        scratch_shapes=[pltpu.VMEM((TM, D), BF16)],
        compiler_params=pltpu.CompilerParams(dimension_semantics=("arbitrary",), vmem_limit_bytes=VMEM_BIG),
    )(dx2, ym, x1, qm, km, vm, w_mo, w_mq, g_post, g_pre)


def _memkv_bwd(dkm, dvm, mem, w_mk, w_mv):
    def body(dk_ref, dv_ref, m_ref, wk_ref, wv_ref, dg_ref):
        dmn = _nt(dk_ref[...].astype(BF16), _w(wk_ref)) + _nt(dv_ref[...].astype(BF16), _w(wv_ref))
        mv = m_ref[...]
        dg_ref[...] = jnp.sum(dmn * (mv * _rstd(mv)), axis=0, keepdims=True)

    return _one_call(body, "memkv_bwd", [dkm, dvm, mem, w_mk, w_mv], [((1, D), F32)], vmem=VMEM_BIG)[0]


def _postmix_bwd(dx1, z, o_f, o_c, w_out, g_post, g_fo, g_co):
    def body(dx1_ref, z_ref, of_ref, oc_ref, wo_ref, gp_ref, gfo_ref, gco_ref,
             dz_ref, dof_ref, doc_ref, dgp_ref, dgfo_ref, dgco_ref):
        _zero_at_start(dgp_ref, dgfo_ref, dgco_ref)
        dz, dgp = _rms_bwd(z_ref[...], gp_ref[...], dx1_ref[...])
        dgp_ref[...] += dgp
        dzb = dz.astype(BF16)
        dz_ref[...] = dzb
        dy = _nt(dzb, _w(wo_ref))
        dof, dgfo = _rms_bwd(of_ref[...], gfo_ref[...], dy[:, :512])
        doc, dgco = _rms_bwd(oc_ref[...], gco_ref[...], dy[:, 512:])
        dof_ref[...] = dof
        doc_ref[...] = doc
        dgfo_ref[...] += dgfo
        dgco_ref[...] += dgco

    return _tok_call(body, "postmix_bwd", [dx1, z, o_f, o_c], [w_out, g_post, g_fo, g_co],
                     [(D, BF16), (512, F32), (512, F32)], [(1, D), (1, 512), (1, 512)], vmem=VMEM_BIG)


def _premix_bwd(dx1, x, dproj, win_t, g_pre):
    def body(dx1_ref, x_ref, dp_ref, w_ref, g_ref, dx_ref, dg_ref):
        _zero_at_start(dg_ref)
        dh, dg = _rms_bwd(x_ref[...], g_ref[...], _nn(dp_ref[...], w_ref[...]))
        dg_ref[...] += dg
        dx_ref[...] = dx1_ref[...] + dh

    return _tok_call(body, "premix_bwd", [dx1, x, dproj], [win_t, g_pre], [(D, F32)], [(1, D)], vmem=VMEM_BIG)


def _wgrad(a, b, name):
    k, m = a.shape
    n = b.shape[1]
    tm = 640 if m % 640 == 0 and m > 1024 else min(m, 512)
    tn = min(n, 1024)

    def body(a_ref, b_ref, o_ref):
        o_ref[...] = _tn(a_ref[...].astype(BF16), b_ref[...].astype(BF16))

    return _pcall(
        body, name=name, grid=(m // tm, n // tn),
        in_specs=[pl.BlockSpec((k, tm), lambda i, j: (0, i)), pl.BlockSpec((k, tn), lambda i, j: (0, j))],
        out_specs=pl.BlockSpec((tm, tn), lambda i, j: (i, j)),
        out_shape=jax.ShapeDtypeStruct((m, n), F32),
        compiler_params=pltpu.CompilerParams(dimension_semantics=("arbitrary", "arbitrary"), vmem_limit_bytes=VMEM_BIG),
    )(a, b)


def _wgrad_group(name, pairs, rows):
    def body(*refs):
        o_ref = refs[-1]
        for k in range(len(pairs)):
            o_ref[k * rows:(k + 1) * rows, :] = _tn(refs[2 * k][...].astype(BF16), refs[2 * k + 1][...].astype(BF16))

    in_specs, ops = [], []
    for a, b in pairs:
        in_specs += [pl.BlockSpec((a.shape[0], rows), lambda j: (0, j)), _resident(b)]
        ops += [a, b]
    return _pcall(
        body, name=name, grid=(8,), in_specs=in_specs,
        out_specs=pl.BlockSpec((None, len(pairs) * rows, D), lambda j: (j, 0, 0)),
        out_shape=jax.ShapeDtypeStruct((8, len(pairs) * rows, D), F32),
        compiler_params=pltpu.CompilerParams(dimension_semantics=("arbitrary",), vmem_limit_bytes=VMEM_BIG),
    )(*ops)


def _adam_math(w, g, m, v):
    m2 = ADAM_B1 * m + (1.0 - ADAM_B1) * g
    v2 = ADAM_B2 * v + (1.0 - ADAM_B2) * jnp.square(g)
    m_hat = m2 / (1.0 - ADAM_B1 ** ADAM_STEP)
    v_hat = v2 / (1.0 - ADAM_B2 ** ADAM_STEP)
    delta = -ADAM_LR * (m_hat / (jnp.sqrt(v_hat) + ADAM_EPS) + ADAM_WD * w)
    return delta, m2, v2


def _adamw(w, g, m, v, name):
    rows, cols = w.shape
    tr = 256 if rows % 256 == 0 else rows

    def body(w_ref, g_ref, m_ref, v_ref, d_ref, m2_ref, v2_ref):
        d_ref[...], m2_ref[...], v2_ref[...] = _adam_math(w_ref[...], g_ref[...], m_ref[...], v_ref[...])

    spec = pl.BlockSpec((tr, cols), lambda i: (i, 0))
    return _pcall(
        body, name=name, grid=(rows // tr,), in_specs=[spec] * 4, out_specs=[spec] * 3,
        out_shape=[jax.ShapeDtypeStruct(w.shape, F32)] * 3,
        compiler_params=pltpu.CompilerParams(dimension_semantics=("arbitrary",)),
    )(w, g, m, v)


def _adamw_small(w, gparts, m, v):
    def body(w_ref, g_ref, m_ref, v_ref, gs_ref, d_ref, m2_ref, v2_ref):
        g = g_ref[0]
        for k in range(1, 8):
            g = g + g_ref[k]
        gs_ref[...] = g
        d_ref[...], m2_ref[...], v2_ref[...] = _adam_math(w_ref[...], g, m_ref[...], v_ref[...])

    return _one_call(body, "adamw_small", [w, gparts, m, v], [(w.shape, F32)] * 4)


def _row_tile(rows):
    return next(t for t in (512, 400, 320) if rows % t == 0)


def _add_halves(g4, theirs, core, name):
    rows = g4.shape[2]
    tr = _row_tile(rows)

    def body(c_ref, a_ref, b_ref, o_ref):
        o_ref[...] = (a_ref[...] + b_ref[...]).astype(BF16)

    grid_spec = pltpu.PrefetchScalarGridSpec(
        num_scalar_prefetch=1, grid=(4, rows // tr),
        in_specs=[pl.BlockSpec((None, None, tr, D), lambda j, i, c: (j, c[0], i, 0)),
                  pl.BlockSpec((None, None, tr, D), lambda j, i, c: (j, 0, i, 0))],
        out_specs=pl.BlockSpec((None, tr, D), lambda j, i, c: (j, i, 0)))
    return _pcall(
        body, name=name, grid_spec=grid_spec, out_shape=jax.ShapeDtypeStruct((4, rows, D), BF16),
        compiler_params=pltpu.CompilerParams(dimension_semantics=("arbitrary", "arbitrary")),
    )(core, g4, theirs)


def _sum_chips(own, got, order, name):
    rows = own.shape[1]
    tr = _row_tile(rows)

    def body(o_ref, a_ref, b_ref, c_ref, d_ref, out_ref):
        f = lambda r: r[...].astype(F32)
        out_ref[...] = ((f(a_ref) + f(b_ref)) + f(c_ref)) + f(d_ref)

    slot = lambda k: pl.BlockSpec((None, tr, D), lambda i, o: (o[k], i, 0))
    grid_spec = pltpu.PrefetchScalarGridSpec(
        num_scalar_prefetch=1, grid=(rows // tr,), in_specs=[slot(0), slot(1), slot(2), slot(3)],
        out_specs=pl.BlockSpec((tr, D), lambda i, o: (i, 0)))
    return _pcall(
        body, name=name, grid_spec=grid_spec, out_shape=jax.ShapeDtypeStruct((rows, D), F32),
        compiler_params=pltpu.CompilerParams(dimension_semantics=("arbitrary",)),
    )(order, own, got, got, got)


def _place():
    return lax.axis_index("x"), lax.axis_index("y"), lax.axis_index("c")


def _allgather(block, name):
    def body(x_ref, out_ref, token, send_sems, recv_sems, local_sem):
        token[...] = jnp.zeros_like(token)
        x, y, c = _place()
        me, sibling = (x, y, c), (x, y, 1 - c)
        chips = [(1 - x, y), (x, 1 - y), (1 - x, 1 - y)]

        def slot(px, py, pc):
            return out_ref.at[4 * px + 2 * py + pc]

        def copy(k, blk, to, src=None):
            return pltpu.make_async_remote_copy(
                src_ref=slot(*blk) if src is None else src, dst_ref=slot(*blk),
                send_sem=send_sems.at[k], recv_sem=recv_sems.at[k], device_id=to, device_id_type=MESH)

        mine = pltpu.make_async_copy(x_ref, slot(*me), local_sem)
        mine.start()
        first = [copy(0, me, sibling, src=x_ref)]
        first += [copy(1 + j, me, (*chip, c), src=x_ref) for j, chip in enumerate(chips)]
        for cp in first:
            cp.start()
        passed = [copy(4 + j, (*chip, c), sibling) for j, chip in enumerate(chips)]
        for j, chip in enumerate(chips):
            copy(1 + j, (*chip, c), me).wait_recv()
            passed[j].start()
        copy(0, sibling, me).wait_recv()
        for j, chip in enumerate(chips):
            copy(4 + j, (*chip, 1 - c), me).wait_recv()
        for cp in first + passed:
            cp.wait_send()
        mine.wait()

    return _pcall(
        body, name=name,
        out_shape=[jax.ShapeDtypeStruct((8,) + block.shape, block.dtype), jax.ShapeDtypeStruct((8, 128), F32)],
        in_specs=[pl.BlockSpec(memory_space=pl.ANY)],
        out_specs=[pl.BlockSpec(memory_space=pl.ANY), pl.BlockSpec(memory_space=pltpu.VMEM)],
        scratch_shapes=[pltpu.SemaphoreType.DMA((7,)), pltpu.SemaphoreType.DMA((7,)), pltpu.SemaphoreType.DMA(())],
        compiler_params=pltpu.CompilerParams(has_side_effects=True),
    )(block)


HBM_SPEC = pl.BlockSpec(memory_space=pltpu.HBM)
SEM_SPEC = pl.BlockSpec(memory_space=pltpu.SEMAPHORE)
ANY_SPEC = pl.BlockSpec(memory_space=pl.ANY)
EFFECT = pltpu.SideEffectType.DATAFLOW_SIDE_EFFECTING


def _in_hbm(a):
    return pltpu.with_memory_space_constraint(a, pltpu.HBM)


def _start_copies(name, src, land_shape, plan, n):
    def body(src_ref, land_ref, send_sems, recv_sems, src_thru, land_thru, token):
        for k, (s, d, to, _) in enumerate(plan(src_ref, land_ref)):
            pltpu.make_async_remote_copy(src_ref=s, dst_ref=d, send_sem=send_sems.at[k], recv_sem=recv_sems.at[k],
                                         device_id=to, device_id_type=MESH).start()
        token[...] = jnp.zeros_like(token)

    return _pcall(
        body, name=name,
        out_shape=(pltpu.SemaphoreType.DMA((n,)), pltpu.SemaphoreType.DMA((n,)), pltpu.HBM(src.shape, src.dtype),
                   pltpu.HBM(land_shape, src.dtype), jax.ShapeDtypeStruct((8, 128), F32)),
        in_specs=(HBM_SPEC, HBM_SPEC),
        out_specs=(SEM_SPEC, SEM_SPEC, HBM_SPEC, HBM_SPEC, pl.BlockSpec(memory_space=pltpu.VMEM)),
        input_output_aliases={0: 2, 1: 3}, compiler_params=pltpu.CompilerParams(has_side_effects=EFFECT),
    )(_in_hbm(src), _in_hbm(lax.empty(land_shape, src.dtype)))


def _wait_copies(name, started, after, plan):
    send_sems, recv_sems, src_thru, land_thru, _ = started

    def body(src_ref, land_ref, send_sems, recv_sems, *rest):
        for k, (s, _, to, mine) in enumerate(plan(src_ref, land_ref)):
            cp = pltpu.make_async_remote_copy(src_ref=s, dst_ref=mine, send_sem=send_sems.at[k],
                                              recv_sem=recv_sems.at[k], device_id=to, device_id_type=MESH)
            cp.wait_send()
            cp.wait_recv()

    return _pcall(
        body, name=name,
        out_shape=(pltpu.HBM(src_thru.shape, src_thru.dtype), pltpu.HBM(land_thru.shape, land_thru.dtype)),
        in_specs=(HBM_SPEC, HBM_SPEC, SEM_SPEC, SEM_SPEC) + (ANY_SPEC,) * len(after), out_specs=(HBM_SPEC, HBM_SPEC),
        input_output_aliases={0: 0, 1: 1}, compiler_params=pltpu.CompilerParams(has_side_effects=EFFECT),
    )(src_thru, land_thru, send_sems, recv_sems, *after)


def _gather_plan(src_ref, land_ref):
    x, y, c = _place()
    peers = [(x, y, 1 - c), (1 - x, y, c), (x, 1 - y, c), (1 - x, 1 - y, c)]
    return [(src_ref, land_ref.at[4 * x + 2 * y + c], p, land_ref.at[4 * p[0] + 2 * p[1] + p[2]]) for p in peers]


def _swap_plan(src_ref, land_ref):
    x, y, c = _place()
    return [(src_ref.at[:, pl.ds(1 - c, 1)], land_ref, (x, y, 1 - c), land_ref)]


def _exchange_plan(src_ref, land_ref):
    x, y, c = _place()
    chips = [(1 - x, y), (x, 1 - y), (1 - x, 1 - y)]
    return [(src_ref.at[2 * px + py], land_ref.at[2 * x + y], (px, py, c), land_ref.at[2 * px + py]) for px, py in chips]


def _gather_forward(land, block):
    def body(land_ref, x_ref, out_ref, send_sems, recv_sems, local_sems):
        x, y, c = _place()
        chips = [(1 - x, y), (x, 1 - y), (1 - x, 1 - y)]
        slot = lambda px, py, pc: 4 * px + 2 * py + pc
        local = [pltpu.make_async_copy(x_ref, out_ref.at[slot(x, y, c)], local_sems.at[0]),
                 pltpu.make_async_copy(land_ref.at[slot(x, y, 1 - c)], out_ref.at[slot(x, y, 1 - c)], local_sems.at[1])]
        local += [pltpu.make_async_copy(land_ref.at[slot(px, py, c)], out_ref.at[slot(px, py, c)], local_sems.at[2 + k])
                  for k, (px, py) in enumerate(chips)]

        def copy(k, px, py, pc):
            return pltpu.make_async_remote_copy(
                src_ref=land_ref.at[slot(px, py, pc)], dst_ref=out_ref.at[slot(px, py, pc)], send_sem=send_sems.at[k],
                recv_sem=recv_sems.at[k], device_id=(x, y, 1 - c), device_id_type=MESH)

        sent = [copy(k, px, py, c) for k, (px, py) in enumerate(chips)]
        for cp in sent + local:
            cp.start()
        for k, (px, py) in enumerate(chips):
            copy(k, px, py, 1 - c).wait_recv()
        for cp in sent:
            cp.wait_send()
        for cp in local:
            cp.wait()

    return _pcall(
        body, name="allgather_rest_forward", out_shape=jax.ShapeDtypeStruct(land.shape, land.dtype),
        in_specs=[ANY_SPEC, ANY_SPEC], out_specs=ANY_SPEC,
        scratch_shapes=[pltpu.SemaphoreType.DMA((3,)), pltpu.SemaphoreType.DMA((3,)), pltpu.SemaphoreType.DMA((5,))],
        compiler_params=pltpu.CompilerParams(has_side_effects=True),
    )(land, block)


class _ReduceScatter:
    def __init__(self, name, g):
        self.name = name
        rows = g.shape[1]
        self.started = _start_copies(name + "_swap_start", g.reshape(4, 2, rows, D), (4, 1, rows, D), _swap_plan, 1)
        self.token = self.started[4][0, 0]

    def halfway(self, after):
        g4, theirs = _wait_copies(self.name + "_swap_wait", self.started, after, _swap_plan)
        self.own = _add_halves(g4, theirs, lax.axis_index("c").reshape(1), self.name + "_add_halves")
        self.started = _start_copies(self.name + "_exch_start", self.own, self.own.shape, _exchange_plan, 3)
        self.token = self.started[4][0, 0]

    def finish(self, after):
        own, got = _wait_copies(self.name + "_exch_wait", self.started, after, _exchange_plan)
        chip = 2 * lax.axis_index("x") + lax.axis_index("y")
        order = (chip + jnp.arange(4, dtype=jnp.int32)) % 4
        return _sum_chips(own, got, order, self.name + "_sum_chips")


def _pack_small(p):
    z = lambda a, n: jnp.pad(a, ((0, 0), (0, n - a.shape[1])))
    rows = [z(p['rel_bias'], D), z(p['b_fgt'], D), jnp.concatenate([p['g_fox_out'], p['g_chk_out']], axis=1)]
    rows += [p[n] for n in ('g_mix_pre', 'g_mix_post', 'g_mem_kv', 'g_mem_pre', 'g_mem_post', 'g_ff_pre', 'g_ff_post')]
    rows.append(jnp.zeros((SMALL_ROWS - 17, D), F32))
    return jnp.concatenate(rows, axis=0)


def _unpack_small(a):
    out = {'rel_bias': a[0:8, :257], 'b_fgt': a[8:9, :8], 'g_fox_out': a[9:10, :512], 'g_chk_out': a[9:10, 512:]}
    for k, n in enumerate(('g_mix_pre', 'g_mix_post', 'g_mem_kv', 'g_mem_pre', 'g_mem_post', 'g_ff_pre', 'g_ff_post')):
        out[n] = a[10 + k:11 + k]
    return out


_GAP_DEV, _GAP_ROW = divmod(GATE0 + 8, N_IN)
_GAP = CHK0 - GATE0 - 8


def _in_rows_to_proj(g):
    parts = [g[j, :N_IN] for j in range(_GAP_DEV)]
    parts += [g[_GAP_DEV, :_GAP_ROW], jnp.zeros((_GAP, D), g.dtype), g[_GAP_DEV, _GAP_ROW:N_IN]]
    parts += [g[j, :N_IN] for j in range(_GAP_DEV + 1, 8)]
    return jnp.concatenate(parts, axis=0)


def _proj_rows_to_in(g):
    pad = lambda a: jnp.pad(a, ((0, R_IN - a.shape[0]), (0, 0)))
    lo = N_IN * _GAP_DEV
    shards = [pad(g[N_IN * j:N_IN * (j + 1)]) for j in range(_GAP_DEV)]
    shards.append(pad(jnp.concatenate([g[lo:lo + _GAP_ROW], g[lo + _GAP_ROW + _GAP:lo + N_IN + _GAP]], axis=0)))
    shards += [pad(g[N_IN * j + _GAP:N_IN * (j + 1) + _GAP]) for j in range(_GAP_DEV + 1, 8)]
    return jnp.stack(shards)


def _local_grads(x, mem, tgt, win_t, gw_of, sm, on_grads):
    b_pad = jnp.pad(sm['b_fgt'], ((0, 0), (0, 120)))
    tbl = jnp.pad(sm['rel_bias'], ((0, 0), (0, NREL_PAD - 257)))

    h1, proj, flog = _premix_fwd(x, sm['g_mix_pre'], win_t)
    c = _gate_fwd(flog, b_pad)
    c8 = c[:, :8]
    c2 = jnp.repeat(c8, 128, axis=1)
    ct3 = c8.T.reshape(4, 2, T)
    o_f, lse_f = _fox_fwd(proj, c2, ct3)
    vt3 = _relvec_fwd(tbl).reshape(4, 2, VW)
    kvp = jnp.pad(proj[:, CHK0 + 512:], ((LEFT, 0), (0, 0)))
    o_c, lse_c = _chk_fwd(proj, kvp, vt3)
    gw = gw_of([o_f, o_c])
    w_out, w_mq, w_mk, w_mv, w_mo, w1_t, w2 = (_wblk(gw, n) for n in ('w_out', 'w_mq', 'w_mk', 'w_mv', 'w_mo', 'w_ff1', 'w_ff2'))
    ycat, z, x1, h2, qm = _postmix_fwd(x, o_f, o_c, sm['g_fox_out'], sm['g_chk_out'], w_out,
                                       sm['g_mix_post'], sm['g_mem_pre'], w_mq)
    memn, km, vm = _memkv_fwd(mem, sm['g_mem_kv'], w_mk, w_mv)
    om, ym, x2, h3 = _mem_fwd(qm, x1, km, vm, w_mo, sm['g_mem_post'], sm['g_ff_pre'])
    a, y3, dx3, loss_acc = _ffn_fwd(h3, x2, tgt, w1_t, w2, sm['g_ff_post'])

    gs = {}
    dx2, da, dy3, r, gs['g_ff_post'], gs['g_ff_pre'] = _ffn_bwd(dx3, y3, x2, a, w1_t, w2, sm['g_ff_post'], sm['g_ff_pre'])
    zero = on_grads('A', _wgrad_group("wgrad_ff", [(da, h3), (r, dy3)], 512), None)
    dx1, dym, dqm, dkm, dvm, gs['g_mem_post'], gs['g_mem_pre'] = _mem_bwd(
        dx2, ym, x1, qm, km, vm, w_mo, w_mq, sm['g_mem_post'] + zero, sm['g_mem_pre'])
    zero = on_grads('A halfway', None, [dx1])
    gs['g_mem_kv'] = _memkv_bwd(dkm, dvm, mem, w_mk, w_mv)
    dz, dof, doc, gs['g_mix_post'], gs['g_fox_out'], gs['g_chk_out'] = _postmix_bwd(
        dx1, z, o_f, o_c, w_out, sm['g_mix_post'] + zero, sm['g_fox_out'], sm['g_chk_out'])
    zero = on_grads('B', _wgrad_group("wgrad_mem_out", [(ycat, dz), (h2, dqm), (memn, dkm), (memn, dvm), (om, dym)], 128), None)
    dq_f, dk_f, dv_f, dct, dcq = _fox_bwd(proj, c2, ct3 + zero, o_f, lse_f, dof)
    zero = on_grads('B halfway', None, [dq_f])
    dq_c, dkp, dvp, gv = _chk_bwd(proj, kvp, vt3 + zero, o_c, lse_c, doc)
    gs['rel_bias'] = _relvec_bwd(gv.reshape(8, VW))[:, :257]
    dc = jnp.pad(dct.reshape(8, T).T + dcq[:, ::128], ((0, 0), (0, 120)))
    dflog, db = _gate_bwd(dc, flog, b_pad)
    gs['b_fgt'] = db[0:1, :8]
    dproj = jnp.concatenate([dq_f, dk_f, dv_f, dflog, dq_c, dkp[LEFT:], dvp[LEFT:]], axis=1).astype(BF16)
    zero = on_grads('C', _proj_rows_to_in(_wgrad(dproj, h1, "wgrad_in")), None)
    grad_x, gs['g_mix_pre'] = _premix_bwd(dx1, x, dproj, win_t, sm['g_mix_pre'] + zero)
    return loss_acc[0, 0], grad_x, gs


def kernel(x, mem, w_in, b_fgt, rel_bias, g_fox_out, g_chk_out, w_out, g_mix_pre, g_mix_post, g_mem_kv, w_mq, w_mk, w_mv, w_mo, g_mem_pre, g_mem_post, w_ff1, w_ff2, g_ff_pre, g_ff_post, loss_target, m_w_in, m_b_fgt, m_rel_bias, m_g_fox_out, m_g_chk_out, m_w_out, m_g_mix_pre, m_g_mix_post, m_g_mem_kv, m_w_mq, m_w_mk, m_w_mv, m_w_mo, m_g_mem_pre, m_g_mem_post, m_w_ff1, m_w_ff2, m_g_ff_pre, m_g_ff_post, v_w_in, v_b_fgt, v_rel_bias, v_g_fox_out, v_g_chk_out, v_w_out, v_g_mix_pre, v_g_mix_post, v_g_mem_kv, v_w_mq, v_w_mk, v_w_mv, v_w_mo, v_g_mem_pre, v_g_mem_post, v_w_ff1, v_w_ff2, v_g_ff_pre, v_g_ff_post):
    args = dict(locals())
    two_d = lambda a: a.reshape(a.shape[-2:])
    w = {n: two_d(args[n]) for n in WEIGHTS}
    m = {n: two_d(args['m_' + n]) for n in WEIGHTS}
    v = {n: two_d(args['v_' + n]) for n in WEIGHTS}

    sm = {n: w[n] for n in SMALL}
    shard_in = jnp.pad(w['w_in'].T, ((0, R_IN - N_IN), (0, 0))).astype(BF16)
    gathered_in, zero = _allgather(shard_in, "allgather_w_in")
    win_t = _in_rows_to_proj(gathered_in)
    shard_rest = (jnp.concatenate([w['w_ff1'].T, w['w_ff2'], w['w_out'], w['w_mq'], w['w_mk'], w['w_mv'], w['w_mo']],
                                  axis=0) + zero[0, 0]).astype(BF16)
    rest = _start_copies("allgather_rest_start", shard_rest, (8, R_REST, D), _gather_plan, 4)
    sm['g_mix_pre'] = sm['g_mix_pre'] + rest[4][0, 0]

    def gw_of(after):
        block, land = _wait_copies("allgather_rest_wait", rest, after, _gather_plan)
        return _gather_forward(land, block)

    rs = {}

    def on_grads(stage, g, after):
        if stage.endswith('halfway'):
            rs[stage[0]].halfway(after)
            return rs[stage[0]].token
        rs[stage] = _ReduceScatter("rs_" + stage.lower(), g)
        if stage == 'C':
            rs[stage].halfway([rs[stage].started[4]])
        return rs[stage].token

    loss_local, grad_x, gs = _local_grads(x[0], mem[0], loss_target[0], win_t, gw_of, sm, on_grads)
    loss = lax.psum(loss_local, ("x", "y", "c"))
    g_a, g_b = rs['A'].finish([grad_x]), rs['B'].finish([grad_x])
    g_big = {'w_ff1': g_a[:512].T, 'w_ff2': g_a[512:]}
    for k, n in enumerate(('w_out', 'w_mq', 'w_mk', 'w_mv', 'w_mo')):
        g_big[n] = g_b[128 * k:128 * (k + 1)]

    grads, deltas, new_m, new_v = {}, {}, {}, {}
    gparts, _ = _allgather(_pack_small(gs), "allgather_small_grads")
    gsum, d_s, m_s, v_s = _adamw_small(_pack_small(sm), gparts, _pack_small({n: m[n] for n in SMALL}),
                                       _pack_small({n: v[n] for n in SMALL}))
    for dst, packed in ((grads, gsum), (deltas, d_s), (new_m, m_s), (new_v, v_s)):
        dst.update(_unpack_small(packed))

    g_big['w_in'] = rs['C'].finish([gsum])[:N_IN].T
    for n in BIG:
        grads[n] = g_big[n]
        deltas[n], new_m[n], new_v[n] = _adamw(w[n], g_big[n], m[n], v[n], "adamw_" + n)

    out = [loss, grad_x[None]]
    for group in (grads, deltas, new_m, new_v):
        out += [group[n].reshape(args[n].shape) for n in WEIGHTS]
    return tuple(out)
```

```python
import functools

import jax
import jax.numpy as jnp
from jax import lax
from jax.experimental import pallas as pl
from jax.experimental.pallas import tpu as pltpu

F32 = jnp.float32
BF16 = jnp.bfloat16
MESH = pl.DeviceIdType.MESH

T = 2048
D = 1024
NMEM = 256
DFF = 4096
EPS = 1e-6
TM = 256
TQ = 256
FQ = 512
HD = 64
SCALE = HD ** -0.5
MEM_HEADS = 4
MEM_HD = 256
MEM_SCALE = MEM_HD ** -0.5
NEG = -1e30
LEFT = 512
WIN = LEFT + TQ
VW = 1024
NREL_PAD = 384
PROJ = 3200
GATE0 = 1536
CHK0 = 1664
VMEM_BIG = 56 * 1024 * 1024

ADAM_LR = 0.001
ADAM_B1 = 0.9
ADAM_B2 = 0.999
ADAM_EPS = 1e-08
ADAM_WD = 0.01
ADAM_STEP = 10

N_IN = 385
R_IN = 400
R_REST = 1664
W_ROWS = {'w_ff1': (0, 512), 'w_ff2': (512, 512),
          'w_out': (1024, 128), 'w_mq': (1152, 128), 'w_mk': (1280, 128), 'w_mv': (1408, 128), 'w_mo': (1536, 128)}
R_A, R_B = 1024, 640
SMALL_ROWS = 24

WEIGHTS = ['w_in', 'b_fgt', 'rel_bias', 'g_fox_out', 'g_chk_out', 'w_out', 'g_mix_pre', 'g_mix_post', 'g_mem_kv',
           'w_mq', 'w_mk', 'w_mv', 'w_mo', 'g_mem_pre', 'g_mem_post', 'w_ff1', 'w_ff2', 'g_ff_pre', 'g_ff_post']
BIG = ['w_in', 'w_out', 'w_mq', 'w_mk', 'w_mv', 'w_mo', 'w_ff1', 'w_ff2']
SMALL = [n for n in WEIGHTS if n not in BIG]


def _pcall(body, **kw):
    return pl.pallas_call(body, **kw)


def _nn(a, b):
    return jnp.dot(a, b, preferred_element_type=F32)


def _nt(a, b):
    return lax.dot_general(a, b, (((1,), (1,)), ((), ())), preferred_element_type=F32)


def _tn(a, b):
    return lax.dot_general(a, b, (((0,), (0,)), ((), ())), preferred_element_type=F32)


def _w(ref):
    v = ref[...]
    return v if v.ndim == 2 else v.reshape(-1, v.shape[-1])


def _rstd(x):
    return lax.rsqrt(jnp.mean(x * x, axis=-1, keepdims=True) + EPS)


def _rms(x, g):
    return x * _rstd(x) * g


def _rms_bwd(x, g, dy):
    r = _rstd(x)
    xh = x * r
    dg = jnp.sum(dy * xh, axis=0, keepdims=True)
    dxh = dy * g
    dx = r * (dxh - xh * jnp.mean(dxh * xh, axis=-1, keepdims=True))
    return dx, dg


def _resident(a):
    if isinstance(a, tuple):
        _, shape, index = a
        return pl.BlockSpec(shape, lambda *_: index, pipeline_mode=pl.Buffered(1))
    return pl.BlockSpec(a.shape, lambda *_, nd=a.ndim: (0,) * nd, pipeline_mode=pl.Buffered(1))


def _wblk(gw, name):
    r0, rows = W_ROWS[name]
    return (gw, (8, rows, D), (0, r0 // rows, 0))


def _tok_call(body, name, tiled, full, outs_tiled, outs_acc=(), rows=T, tm=TM, vmem=None):
    in_specs = [pl.BlockSpec((tm, a.shape[1]), lambda i: (i, 0)) for a in tiled]
    in_specs += [_resident(a) for a in full]
    full = [a[0] if isinstance(a, tuple) else a for a in full]
    out_shape = [jax.ShapeDtypeStruct((rows, c), dt) for c, dt in outs_tiled]
    out_shape += [jax.ShapeDtypeStruct(s, F32) for s in outs_acc]
    out_specs = [pl.BlockSpec((tm, c), lambda i: (i, 0)) for c, _ in outs_tiled]
    out_specs += [pl.BlockSpec(s, lambda i, nd=len(s): (0,) * nd) for s in outs_acc]
    return _pcall(
        body, name=name, grid=(rows // tm,), in_specs=in_specs, out_specs=out_specs, out_shape=out_shape,
        compiler_params=pltpu.CompilerParams(dimension_semantics=("arbitrary",), vmem_limit_bytes=vmem),
    )(*tiled, *full)


def _one_call(body, name, ins, outs, vmem=None):
    whole = lambda s: pl.BlockSpec(s, lambda i, nd=len(s): (0,) * nd)
    return _pcall(
        body, name=name, grid=(1,), in_specs=[_resident(a) for a in ins], out_specs=[whole(s) for s, _ in outs],
        out_shape=[jax.ShapeDtypeStruct(s, dt) for s, dt in outs],
        compiler_params=pltpu.CompilerParams(dimension_semantics=("arbitrary",), vmem_limit_bytes=vmem),
    )(*[a[0] if isinstance(a, tuple) else a for a in ins])


def _premix_fwd(x, g_pre, win_t):
    def body(x_ref, g_ref, w_ref, h_ref, proj_ref, flog_ref):
        h = _rms(x_ref[...], g_ref[...]).astype(BF16)
        h_ref[...] = h
        p = _nt(h, w_ref[...])
        proj_ref[...] = p.astype(BF16)
        flog_ref[...] = p[:, GATE0:GATE0 + 128]

    return _tok_call(body, "premix_fwd", [x], [g_pre, win_t],
                     [(D, BF16), (PROJ, BF16), (128, F32)], vmem=VMEM_BIG)


def _postmix_fwd(x, o_f, o_c, g_fo, g_co, w_out, g_post, g_mpre, w_mq):
    def body(x_ref, of_ref, oc_ref, gfo_ref, gco_ref, wo_ref, gp_ref, gm_ref, wq_ref,
             y_ref, z_ref, x1_ref, h2_ref, qm_ref):
        y_ref[:, :512] = _rms(of_ref[...], gfo_ref[...]).astype(BF16)
        y_ref[:, 512:] = _rms(oc_ref[...], gco_ref[...]).astype(BF16)
        z = _nn(y_ref[...], _w(wo_ref))
        z_ref[...] = z
        x1 = x_ref[...] + _rms(z, gp_ref[...])
        x1_ref[...] = x1
        h2 = _rms(x1, gm_ref[...]).astype(BF16)
        h2_ref[...] = h2
        qm_ref[...] = _nn(h2, _w(wq_ref)).astype(BF16)

    return _tok_call(body, "postmix_fwd", [x, o_f, o_c], [g_fo, g_co, w_out, g_post, g_mpre, w_mq],
                     [(D, BF16), (D, F32), (D, F32), (D, BF16), (D, BF16)], vmem=VMEM_BIG)


def _memkv_fwd(mem, g_kv, w_mk, w_mv):
    def body(m_ref, g_ref, wk_ref, wv_ref, mn_ref, k_ref, v_ref):
        mn = _rms(m_ref[...], g_ref[...]).astype(BF16)
        mn_ref[...] = mn
        k_ref[...] = _nn(mn, _w(wk_ref)).astype(BF16)
        v_ref[...] = _nn(mn, _w(wv_ref)).astype(BF16)

    return _tok_call(body, "memkv_fwd", [mem], [g_kv, w_mk, w_mv],
                     [(D, BF16), (D, BF16), (D, BF16)], rows=NMEM, tm=NMEM, vmem=VMEM_BIG)


def _mem_fwd(qm, x1, km, vm, w_mo, g_post, g_fpre):
    def body(q_ref, x1_ref, k_ref, v_ref, wo_ref, gp_ref, gf_ref, om_ref, ym_ref, x2_ref, h3_ref):
        for h in range(MEM_HEADS):
            sl = slice(h * MEM_HD, (h + 1) * MEM_HD)
            s = _nt(q_ref[:, sl], k_ref[:, sl]) * MEM_SCALE
            p = jnp.exp(s - jnp.max(s, axis=-1, keepdims=True))
            p = p / jnp.sum(p, axis=-1, keepdims=True)
            om_ref[:, sl] = _nn(p.astype(BF16), v_ref[:, sl]).astype(BF16)
        ym = _nn(om_ref[...], _w(wo_ref))
        ym_ref[...] = ym
        x2 = x1_ref[...] + _rms(ym, gp_ref[...])
        x2_ref[...] = x2
        h3_ref[...] = _rms(x2, gf_ref[...]).astype(BF16)

    return _tok_call(body, "mem_fwd", [qm, x1], [km, vm, w_mo, g_post, g_fpre],
                     [(D, BF16), (D, F32), (D, F32), (D, BF16)], vmem=VMEM_BIG)


def _ffn_fwd(h3, x2, tgt, w1_t, w2, g_post):
    def body(h_ref, x2_ref, t_ref, w1_ref, w2_ref, g_ref, a_ref, y_ref, dx_ref, loss_ref):
        @pl.when(pl.program_id(0) == 0)
        def _():
            loss_ref[...] = jnp.zeros_like(loss_ref)

        a = _nt(h_ref[...], _w(w1_ref))
        a_ref[...] = a.astype(BF16)
        r = jnp.square(jnp.maximum(a, 0.0)).astype(BF16)
        y = _nn(r, _w(w2_ref))
        y_ref[...] = y
        e = x2_ref[...] + _rms(y, g_ref[...]) - t_ref[...]
        dx_ref[...] = e * (1.0 / D)
        loss_ref[...] += 0.5 * jnp.sum(jnp.sum(e * e, axis=-1, keepdims=True) * (1.0 / D))

    return _tok_call(body, "ffn_fwd", [h3, x2, tgt], [w1_t, w2, g_post],
                     [(DFF, BF16), (D, F32), (D, F32)], [(8, 128)], vmem=VMEM_BIG)


def _tri(lower):
    r = lax.broadcasted_iota(jnp.int32, (128, 128), 0)
    c = lax.broadcasted_iota(jnp.int32, (128, 128), 1)
    return jnp.where(r >= c if lower else c >= r, 1.0, 0.0).astype(F32)


def _hdot(a, b):
    return jnp.dot(a, b, preferred_element_type=F32, precision=lax.Precision.HIGHEST)


def _gate_fwd(flog, b_pad):
    def body(f_ref, b_ref, c_ref):
        tri = _tri(True)

        def step(i, carry):
            rows = pl.ds(pl.multiple_of(i * 128, 128), 128)
            z = f_ref[rows, :] + b_ref[...]
            lf = jnp.minimum(z, 0.0) - jnp.log(1.0 + jnp.exp(-jnp.abs(z)))
            cb = _hdot(tri, lf) + carry
            c_ref[rows, :] = cb
            return cb[127:128, :]

        lax.fori_loop(0, T // 128, step, jnp.zeros((1, 128), F32))

    return _one_call(body, "gate_fwd", [flog, b_pad], [((T, 128), F32)])[0]


def _gate_bwd(dc, flog, b_pad):
    def body(dc_ref, f_ref, b_ref, df_ref, db_ref):
        tri = _tri(False)

        def step(j, carry):
            run, db = carry
            i = T // 128 - 1 - j
            rows = pl.ds(pl.multiple_of(i * 128, 128), 128)
            dcb = dc_ref[rows, :]
            rb = _hdot(tri, dcb) + run
            z = f_ref[rows, :] + b_ref[...]
            df = rb * (1.0 / (1.0 + jnp.exp(z)))
            df_ref[rows, :] = df
            return run + jnp.sum(dcb, axis=0, keepdims=True), db + jnp.sum(df, axis=0, keepdims=True)

        _, db = lax.fori_loop(0, T // 128, step, (jnp.zeros((1, 128), F32), jnp.zeros((1, 128), F32)))
        db_ref[...] = jnp.broadcast_to(db, (8, 128))

    return _one_call(body, "gate_bwd", [dc, flog, b_pad], [((T, 128), F32), ((8, 128), F32)])


def _lane_lo(rows=TQ):
    return lax.broadcasted_iota(jnp.int32, (rows, 128), 1) < HD


def _half(v, lo, a, scale=None):
    keep = lo if a == 0 else jnp.logical_not(lo)
    v = v.astype(F32) if scale is None else v.astype(F32) * scale
    return jnp.where(keep, v, 0.0).astype(BF16)


def _fox_specs():
    return [pl.BlockSpec((FQ, 128), lambda h, i: (i, h)),
            pl.BlockSpec((T, 128), lambda h, i: (0, 4 + h)),
            pl.BlockSpec((T, 128), lambda h, i: (0, 8 + h))]


def _fox_fwd(proj, c2, ct3):
    def body(q_ref, k_ref, v_ref, c_ref, ct_ref, o_ref, l_ref):
        i = pl.program_id(1)
        lo = _lane_lo(FQ)
        causal = lax.broadcasted_iota(jnp.int32, (FQ, FQ), 1) <= lax.broadcasted_iota(jnp.int32, (FQ, FQ), 0)
        q = q_ref[...]
        qs = [_half(q, lo, a, SCALE) for a in range(2)]
        cqs = [c_ref[:, 128 * a:128 * a + 1] for a in range(2)]

        def tile(off, carry, diagonal):
            kblk = k_ref[pl.ds(off, FQ), :]
            vblk = v_ref[pl.ds(off, FQ), :]
            new = []
            for a in range(2):
                m, l, acc = carry[a]
                s = _nt(qs[a], kblk) + (cqs[a] - ct_ref[a:a + 1, pl.ds(off, FQ)])
                if diagonal:
                    s = jnp.where(causal, s, NEG)
                m2 = jnp.maximum(m, jnp.max(s, axis=-1, keepdims=True))
                p = jnp.exp(s - m2)
                alpha = jnp.exp(m - m2)
                new.append((m2, alpha * l + jnp.sum(p, axis=-1, keepdims=True),
                            alpha * acc + _nn(p.astype(BF16), vblk)))
            return tuple(new)

        init = (jnp.full((FQ, 1), NEG, F32), jnp.zeros((FQ, 1), F32), jnp.zeros((FQ, 128), F32))
        carry = lax.fori_loop(0, i, lambda kb, c: tile(pl.multiple_of(kb * FQ, FQ), c, False), (init, init))
        carry = tile(pl.multiple_of(i * FQ, FQ), carry, True)
        outs = []
        for a in range(2):
            m, l, acc = carry[a]
            outs.append(acc / l)
            l_ref[:, 128 * a:128 * a + 128] = jnp.broadcast_to(m + jnp.log(l), (FQ, 128))
        o_ref[...] = jnp.where(lo, outs[0], outs[1])

    return _pcall(
        body, name="fox_fwd", grid=(4, T // FQ),
        in_specs=_fox_specs() + [pl.BlockSpec((FQ, 256), lambda h, i: (i, h)),
                                 pl.BlockSpec((None, 2, T), lambda h, i: (h, 0, 0))],
        out_specs=[pl.BlockSpec((FQ, 128), lambda h, i: (i, h)), pl.BlockSpec((FQ, 256), lambda h, i: (i, h))],
        out_shape=[jax.ShapeDtypeStruct((T, 512), F32), jax.ShapeDtypeStruct((T, 1024), F32)],
        compiler_params=pltpu.CompilerParams(dimension_semantics=("arbitrary", "arbitrary"), vmem_limit_bytes=VMEM_BIG),
    )(proj, proj, proj, c2, ct3)


def _fox_bwd(proj, c2, ct3, o, lse, do):
    def body(q_ref, k_ref, v_ref, c_ref, ct_ref, o_ref, l_ref, do_ref, dq_ref, dk_ref, dv_ref, dct_ref, dcq_ref):
        i = pl.program_id(1)

        @pl.when(i == 0)
        def _():
            dk_ref[...] = jnp.zeros_like(dk_ref)
            dv_ref[...] = jnp.zeros_like(dv_ref)
            dct_ref[...] = jnp.zeros_like(dct_ref)

        lo = _lane_lo(FQ)
        causal = lax.broadcasted_iota(jnp.int32, (FQ, FQ), 1) <= lax.broadcasted_iota(jnp.int32, (FQ, FQ), 0)
        q = q_ref[...]
        do_v = do_ref[...]
        prod = do_v * o_ref[...]
        qs = [_half(q, lo, a, SCALE) for a in range(2)]
        dos = [_half(do_v, lo, a) for a in range(2)]
        deltas = [jnp.sum(jnp.where(lo if a == 0 else jnp.logical_not(lo), prod, 0.0), axis=-1, keepdims=True)
                  for a in range(2)]
        cqs = [c_ref[:, 128 * a:128 * a + 1] for a in range(2)]
        las = [l_ref[:, 128 * a:128 * a + 1] for a in range(2)]

        def tile(off, carry, diagonal):
            kblk = k_ref[pl.ds(off, FQ), :]
            vblk = v_ref[pl.ds(off, FQ), :]
            new = []
            dk = jnp.zeros((FQ, 128), F32)
            dv = jnp.zeros((FQ, 128), F32)
            for a in range(2):
                dq_acc, rs = carry[a]
                s = _nt(qs[a], kblk) + (cqs[a] - ct_ref[a:a + 1, pl.ds(off, FQ)])
                if diagonal:
                    s = jnp.where(causal, s, NEG)
                p = jnp.exp(s - las[a])
                ds = p * (_nt(dos[a], vblk) - deltas[a])
                dsb = ds.astype(BF16)
                dk = dk + _tn(dsb, qs[a])
                dv = dv + _tn(p.astype(BF16), dos[a])
                dct_ref[a:a + 1, pl.ds(off, FQ)] -= jnp.sum(ds, axis=0, keepdims=True)
                new.append((dq_acc + _nn(dsb, kblk), rs + jnp.sum(ds, axis=-1, keepdims=True)))
            dk_ref[pl.ds(off, FQ), :] += dk
            dv_ref[pl.ds(off, FQ), :] += dv
            return tuple(new)

        init = (jnp.zeros((FQ, 128), F32), jnp.zeros((FQ, 1), F32))
        carry = lax.fori_loop(0, i, lambda kb, c: tile(pl.multiple_of(kb * FQ, FQ), c, False), (init, init))
        carry = tile(pl.multiple_of(i * FQ, FQ), carry, True)
        for a in range(2):
            dcq_ref[:, 128 * a:128 * a + 128] = jnp.broadcast_to(carry[a][1], (FQ, 128))
        dq_ref[...] = jnp.where(lo, carry[0][0], carry[1][0]) * SCALE

    blk = pl.BlockSpec((FQ, 128), lambda h, i: (i, h))
    wide = pl.BlockSpec((FQ, 256), lambda h, i: (i, h))
    rows = pl.BlockSpec((None, 2, T), lambda h, i: (h, 0, 0))
    col = pl.BlockSpec((T, 128), lambda h, i: (0, h))
    return _pcall(
        body, name="fox_bwd", grid=(4, T // FQ),
        in_specs=_fox_specs() + [wide, rows, blk, wide, blk],
        out_specs=[blk, col, col, rows, wide],
        out_shape=[jax.ShapeDtypeStruct((T, 512), F32), jax.ShapeDtypeStruct((T, 512), F32),
                   jax.ShapeDtypeStruct((T, 512), F32), jax.ShapeDtypeStruct((4, 2, T), F32),
                   jax.ShapeDtypeStruct((T, 1024), F32)],
        compiler_params=pltpu.CompilerParams(dimension_semantics=("arbitrary", "arbitrary"), vmem_limit_bytes=VMEM_BIG),
    )(proj, proj, proj, c2, ct3, o, lse, do)


def _rel_onehot():
    ridx = lax.broadcasted_iota(jnp.int32, (NREL_PAD, VW), 0)
    j = lax.broadcasted_iota(jnp.int32, (NREL_PAD, VW), 1)
    return jnp.where(ridx == jnp.clip(TQ + LEFT - 1 - j, -128, 128) + 128, 1.0, 0.0).astype(F32)


def _relvec_fwd(tbl):
    def body(t_ref, v_ref):
        v_ref[...] = _hdot(t_ref[...], _rel_onehot())

    return _one_call(body, "relvec_fwd", [tbl], [((8, VW), F32)])[0]


def _relvec_bwd(gv):
    def body(g_ref, t_ref):
        t_ref[...] = lax.dot_general(g_ref[...], _rel_onehot(), (((1,), (1,)), ((), ())),
                                     preferred_element_type=F32, precision=lax.Precision.HIGHEST)

    return _one_call(body, "relvec_bwd", [gv], [((8, NREL_PAD), F32)])[0]


def _chk_bias(vt_ref, a):
    vb = jnp.broadcast_to(vt_ref[a:a + 1, :], (TQ, VW))
    y = pltpu.roll(vb, VW - (TQ - 1), 1, stride=1, stride_axis=0)[:, :WIN]
    cr = lax.broadcasted_iota(jnp.int32, (TQ, WIN), 0) // 64
    cm = lax.broadcasted_iota(jnp.int32, (TQ, WIN), 1) // 64
    return jnp.where((cm >= cr) & (cm <= cr + 8), y, NEG)


def _chk_specs():
    return [pl.BlockSpec((TQ, 128), lambda h, i: (i, CHK0 // 128 + h)),
            pl.BlockSpec((T + LEFT, 128), lambda h, i: (0, h)),
            pl.BlockSpec((T + LEFT, 128), lambda h, i: (0, 4 + h)),
            pl.BlockSpec((None, 2, VW), lambda h, i: (h, 0, 0))]


def _chk_fwd(proj, kvp, vt3):
    def body(q_ref, k_ref, v_ref, vt_ref, o_ref, l_ref, bias_ref):
        i = pl.program_id(1)

        @pl.when(i == 0)
        def _():
            for a in range(2):
                bias_ref[a] = _chk_bias(vt_ref, a)

        lo = _lane_lo()
        off = pl.multiple_of(i * TQ, TQ)
        kw = k_ref[pl.ds(off, WIN), :]
        vw = v_ref[pl.ds(off, WIN), :]
        real = lax.broadcasted_iota(jnp.int32, (TQ, WIN), 1) + off >= LEFT
        q = q_ref[...]
        outs = []
        for a in range(2):
            s = jnp.where(real, _nt(_half(q, lo, a), kw) * SCALE + bias_ref[a], NEG)
            m = jnp.max(s, axis=-1, keepdims=True)
            p = jnp.exp(s - m)
            l = jnp.sum(p, axis=-1, keepdims=True)
            outs.append(_nn(p.astype(BF16), vw) / l)
            l_ref[:, 128 * a:128 * a + 128] = jnp.broadcast_to(m + jnp.log(l), (TQ, 128))
        o_ref[...] = jnp.where(lo, outs[0], outs[1])

    return _pcall(
        body, name="chk_fwd", grid=(4, T // TQ), in_specs=_chk_specs(),
        out_specs=[pl.BlockSpec((TQ, 128), lambda h, i: (i, h)), pl.BlockSpec((TQ, 256), lambda h, i: (i, h))],
        out_shape=[jax.ShapeDtypeStruct((T, 512), F32), jax.ShapeDtypeStruct((T, 1024), F32)],
        scratch_shapes=[pltpu.VMEM((2, TQ, WIN), F32)],
        compiler_params=pltpu.CompilerParams(dimension_semantics=("arbitrary", "arbitrary")),
    )(proj, kvp, kvp, vt3)


def _chk_bwd(proj, kvp, vt3, o, lse, do):
    nq = T // TQ

    def body(q_ref, k_ref, v_ref, vt_ref, o_ref, l_ref, do_ref, dq_ref, dk_ref, dv_ref, gv_ref, bias_ref, dsum_ref):
        i = pl.program_id(1)

        @pl.when(i == 0)
        def _():
            for a in range(2):
                bias_ref[a] = _chk_bias(vt_ref, a)
            dsum_ref[...] = jnp.zeros_like(dsum_ref)
            dk_ref[...] = jnp.zeros_like(dk_ref)
            dv_ref[...] = jnp.zeros_like(dv_ref)

        lo = _lane_lo()
        off = pl.multiple_of(i * TQ, TQ)
        kw = k_ref[pl.ds(off, WIN), :]
        vw = v_ref[pl.ds(off, WIN), :]
        real = lax.broadcasted_iota(jnp.int32, (TQ, WIN), 1) + off >= LEFT
        q = q_ref[...]
        do_v = do_ref[...]
        prod = do_v * o_ref[...]
        dqs = []
        for a in range(2):
            keep = lo if a == 0 else jnp.logical_not(lo)
            qa = _half(q, lo, a)
            doa = _half(do_v, lo, a)
            delta = jnp.sum(jnp.where(keep, prod, 0.0), axis=-1, keepdims=True)
            s = jnp.where(real, _nt(qa, kw) * SCALE + bias_ref[a], NEG)
            p = jnp.exp(s - l_ref[:, 128 * a:128 * a + 1])
            ds = p * (_nt(doa, vw) - delta)
            dsum_ref[a] += ds
            dsb = ds.astype(BF16)
            dk_ref[pl.ds(off, WIN), :] += _tn(dsb, qa) * SCALE
            dv_ref[pl.ds(off, WIN), :] += _tn(p.astype(BF16), doa)
            dqs.append(_nn(dsb, kw))
        dq_ref[...] = jnp.where(lo, dqs[0], dqs[1]) * SCALE

        @pl.when(i == nq - 1)
        def _():
            rr = lax.broadcasted_iota(jnp.int32, (TQ, TQ), 0)
            cc = lax.broadcasted_iota(jnp.int32, (TQ, TQ), 1)
            flip = jnp.where(rr + cc == TQ - 1, 1.0, 0.0).astype(F32)
            for a in range(2):
                dpad = jnp.concatenate([dsum_ref[a], jnp.zeros((TQ, VW - WIN), F32)], axis=1)
                z = pltpu.roll(_hdot(flip, dpad), 0, 1, stride=1, stride_axis=0)
                gv_ref[a:a + 1, :] = jnp.sum(z, axis=0, keepdims=True)

    blk = pl.BlockSpec((TQ, 128), lambda h, i: (i, h))
    wide = pl.BlockSpec((TQ, 256), lambda h, i: (i, h))
    col = pl.BlockSpec((T + LEFT, 128), lambda h, i: (0, h))
    return _pcall(
        body, name="chk_bwd", grid=(4, nq), in_specs=_chk_specs() + [blk, wide, blk],
        out_specs=[blk, col, col, pl.BlockSpec((None, 2, VW), lambda h, i: (h, 0, 0))],
        out_shape=[jax.ShapeDtypeStruct((T, 512), F32), jax.ShapeDtypeStruct((T + LEFT, 512), F32),
                   jax.ShapeDtypeStruct((T + LEFT, 512), F32), jax.ShapeDtypeStruct((4, 2, VW), F32)],
        scratch_shapes=[pltpu.VMEM((2, TQ, WIN), F32), pltpu.VMEM((2, TQ, WIN), F32)],
        compiler_params=pltpu.CompilerParams(dimension_semantics=("arbitrary", "arbitrary")),
    )(proj, kvp, kvp, vt3, o, lse, do)


def _zero_at_start(*refs):
    @pl.when(pl.program_id(0) == 0)
    def _():
        for r in refs:
            r[...] = jnp.zeros_like(r)


def _ffn_bwd(dx3, y3, x2, a, w1_t, w2, g_post, g_pre):
    def body(dx3_ref, y_ref, x2_ref, a_ref, w1_ref, w2_ref, gp_ref, gf_ref,
             dx2_ref, da_ref, dy_ref, r_ref, dgp_ref, dgf_ref):
        _zero_at_start(dgp_ref, dgf_ref)
        dx3_v = dx3_ref[...]
        dy, dgp = _rms_bwd(y_ref[...], gp_ref[...], dx3_v)
        dgp_ref[...] += dgp
        dyb = dy.astype(BF16)
        dy_ref[...] = dyb
        ra = jnp.maximum(a_ref[...].astype(F32), 0.0)
        r_ref[...] = jnp.square(ra).astype(BF16)
        da = (_nt(dyb, _w(w2_ref)) * (2.0 * ra)).astype(BF16)
        da_ref[...] = da
        dh, dgf = _rms_bwd(x2_ref[...], gf_ref[...], _nn(da, _w(w1_ref)))
        dgf_ref[...] += dgf
        dx2_ref[...] = dx3_v + dh

    return _tok_call(body, "ffn_bwd", [dx3, y3, x2, a], [w1_t, w2, g_post, g_pre],
                     [(D, F32), (DFF, BF16), (D, BF16), (DFF, BF16)], [(1, D), (1, D)], vmem=VMEM_BIG)


def _mem_bwd(dx2, ym, x1, qm, km, vm, w_mo, w_mq, g_post, g_pre):
    def body(dx2_ref, ym_ref, x1_ref, q_ref, k_ref, v_ref, wo_ref, wq_ref, gp_ref, gm_ref,
             dx1_ref, dym_ref, dq_ref, dk_ref, dv_ref, dgp_ref, dgm_ref, dom_ref):
        _zero_at_start(dk_ref, dv_ref, dgp_ref, dgm_ref)
        dx2_v = dx2_ref[...]
        dym, dgp = _rms_bwd(ym_ref[...], gp_ref[...], dx2_v)
        dgp_ref[...] += dgp
        dymb = dym.astype(BF16)
        dym_ref[...] = dymb
        dom_ref[...] = _nt(dymb, _w(wo_ref)).astype(BF16)
        for h in range(MEM_HEADS):
            sl = slice(h * MEM_HD, (h + 1) * MEM_HD)
            qh, kh, doh = q_ref[:, sl], k_ref[:, sl], dom_ref[:, sl]
            s = _nt(qh, kh) * MEM_SCALE
            p = jnp.exp(s - jnp.max(s, axis=-1, keepdims=True))
            p = p / jnp.sum(p, axis=-1, keepdims=True)
            dp = _nt(doh, v_ref[:, sl])
            ds = (p * (dp - jnp.sum(p * dp, axis=-1, keepdims=True))).astype(BF16)
            dq_ref[:, sl] = (_nn(ds, kh) * MEM_SCALE).astype(BF16)
            dk_ref[:, sl] += _tn(ds, qh) * MEM_SCALE
            dv_ref[:, sl] += _tn(p.astype(BF16), doh)
        dh, dgm = _rms_bwd(x1_ref[...], gm_ref[...], _nt(dq_ref[...], _w(wq_ref)))
        dgm_ref[...] += dgm
        dx1_ref[...] = dx2_v + dh

    in_specs = [pl.BlockSpec((TM, D), lambda i: (i, 0))] * 4
    in_specs += [_resident(a) for a in (km, vm, w_mo, w_mq, g_post, g_pre)]
    w_mo, w_mq = w_mo[0], w_mq[0]
    tiled = pl.BlockSpec((TM, D), lambda i: (i, 0))
    kv = pl.BlockSpec((NMEM, D), lambda i: (0, 0))
    vec = pl.BlockSpec((1, D), lambda i: (0, 0))
    return _pcall(
        body, name="mem_bwd", grid=(T // TM,), in_specs=in_specs,
        out_specs=[tiled, tiled, tiled, kv, kv, vec, vec],
        out_shape=[jax.ShapeDtypeStruct((T, D), F32), jax.ShapeDtypeStruct((T, D), BF16),
                   jax.ShapeDtypeStruct((T, D), BF16), jax.ShapeDtypeStruct((NMEM, D), F32),
                   jax.ShapeDtypeStruct((NMEM, D), F32), jax.ShapeDtypeStruct((1, D), F32),
                   jax.ShapeDtypeStruct((1, D), F32)],
        scratch_shapes=[pltpu.VMEM((TM, D), BF16)],
        compiler_params=pltpu.CompilerParams(dimension_semantics=("arbitrary",), vmem_limit_bytes=VMEM_BIG),
    )(dx2, ym, x1, qm, km, vm, w_mo, w_mq, g_post, g_pre)


def _memkv_bwd(dkm, dvm, mem, w_mk, w_mv):
    def body(dk_ref, dv_ref, m_ref, wk_ref, wv_ref, dg_ref):
        dmn = _nt(dk_ref[...].astype(BF16), _w(wk_ref)) + _nt(dv_ref[...].astype(BF16), _w(wv_ref))
        mv = m_ref[...]
        dg_ref[...] = jnp.sum(dmn * (mv * _rstd(mv)), axis=0, keepdims=True)

    return _one_call(body, "memkv_bwd", [dkm, dvm, mem, w_mk, w_mv], [((1, D), F32)], vmem=VMEM_BIG)[0]


def _postmix_bwd(dx1, z, o_f, o_c, w_out, g_post, g_fo, g_co):
    def body(dx1_ref, z_ref, of_ref, oc_ref, wo_ref, gp_ref, gfo_ref, gco_ref,
             dz_ref, dof_ref, doc_ref, dgp_ref, dgfo_ref, dgco_ref):
        _zero_at_start(dgp_ref, dgfo_ref, dgco_ref)
        dz, dgp = _rms_bwd(z_ref[...], gp_ref[...], dx1_ref[...])
        dgp_ref[...] += dgp
        dzb = dz.astype(BF16)
        dz_ref[...] = dzb
        dy = _nt(dzb, _w(wo_ref))
        dof, dgfo = _rms_bwd(of_ref[...], gfo_ref[...], dy[:, :512])
        doc, dgco = _rms_bwd(oc_ref[...], gco_ref[...], dy[:, 512:])
        dof_ref[...] = dof
        doc_ref[...] = doc
        dgfo_ref[...] += dgfo
        dgco_ref[...] += dgco

    return _tok_call(body, "postmix_bwd", [dx1, z, o_f, o_c], [w_out, g_post, g_fo, g_co],
                     [(D, BF16), (512, F32), (512, F32)], [(1, D), (1, 512), (1, 512)], vmem=VMEM_BIG)


def _premix_bwd(dx1, x, dproj, win_t, g_pre):
    def body(dx1_ref, x_ref, dp_ref, w_ref, g_ref, dx_ref, dg_ref):
        _zero_at_start(dg_ref)
        dh, dg = _rms_bwd(x_ref[...], g_ref[...], _nn(dp_ref[...], w_ref[...]))
        dg_ref[...] += dg
        dx_ref[...] = dx1_ref[...] + dh

    return _tok_call(body, "premix_bwd", [dx1, x, dproj], [win_t, g_pre], [(D, F32)], [(1, D)], vmem=VMEM_BIG)


def _wgrad(a, b, name):
    k, m = a.shape
    n = b.shape[1]
    tm = 640 if m % 640 == 0 and m > 1024 else min(m, 512)
    tn = min(n, 1024)

    def body(a_ref, b_ref, o_ref):
        o_ref[...] = _tn(a_ref[...].astype(BF16), b_ref[...].astype(BF16))

    return _pcall(
        body, name=name, grid=(m // tm, n // tn),
        in_specs=[pl.BlockSpec((k, tm), lambda i, j: (0, i)), pl.BlockSpec((k, tn), lambda i, j: (0, j))],
        out_specs=pl.BlockSpec((tm, tn), lambda i, j: (i, j)),
        out_shape=jax.ShapeDtypeStruct((m, n), F32),
        compiler_params=pltpu.CompilerParams(dimension_semantics=("arbitrary", "arbitrary"), vmem_limit_bytes=VMEM_BIG),
    )(a, b)


def _wgrad_group(name, pairs, rows):
    def body(*refs):
        o_ref = refs[-1]
        for k in range(len(pairs)):
            o_ref[k * rows:(k + 1) * rows, :] = _tn(refs[2 * k][...].astype(BF16), refs[2 * k + 1][...].astype(BF16))

    in_specs, ops = [], []
    for a, b in pairs:
        in_specs += [pl.BlockSpec((a.shape[0], rows), lambda j: (0, j)), _resident(b)]
        ops += [a, b]
    return _pcall(
        body, name=name, grid=(8,), in_specs=in_specs,
        out_specs=pl.BlockSpec((None, len(pairs) * rows, D), lambda j: (j, 0, 0)),
        out_shape=jax.ShapeDtypeStruct((8, len(pairs) * rows, D), F32),
        compiler_params=pltpu.CompilerParams(dimension_semantics=("arbitrary",), vmem_limit_bytes=VMEM_BIG),
    )(*ops)


def _adam_math(w, g, m, v):
    m2 = ADAM_B1 * m + (1.0 - ADAM_B1) * g
    v2 = ADAM_B2 * v + (1.0 - ADAM_B2) * jnp.square(g)
    m_hat = m2 / (1.0 - ADAM_B1 ** ADAM_STEP)
    v_hat = v2 / (1.0 - ADAM_B2 ** ADAM_STEP)
    delta = -ADAM_LR * (m_hat / (jnp.sqrt(v_hat) + ADAM_EPS) + ADAM_WD * w)
    return delta, m2, v2


def _adamw(w, g, m, v, name):
    rows, cols = w.shape
    tr = 256 if rows % 256 == 0 else rows

    def body(w_ref, g_ref, m_ref, v_ref, d_ref, m2_ref, v2_ref):
        d_ref[...], m2_ref[...], v2_ref[...] = _adam_math(w_ref[...], g_ref[...], m_ref[...], v_ref[...])

    spec = pl.BlockSpec((tr, cols), lambda i: (i, 0))
    return _pcall(
        body, name=name, grid=(rows // tr,), in_specs=[spec] * 4, out_specs=[spec] * 3,
        out_shape=[jax.ShapeDtypeStruct(w.shape, F32)] * 3,
        compiler_params=pltpu.CompilerParams(dimension_semantics=("arbitrary",)),
    )(w, g, m, v)


def _adamw_small(w, gparts, m, v):
    def body(w_ref, g_ref, m_ref, v_ref, gs_ref, d_ref, m2_ref, v2_ref):
        g = g_ref[0]
        for k in range(1, 8):
            g = g + g_ref[k]
        gs_ref[...] = g
        d_ref[...], m2_ref[...], v2_ref[...] = _adam_math(w_ref[...], g, m_ref[...], v_ref[...])

    return _one_call(body, "adamw_small", [w, gparts, m, v], [(w.shape, F32)] * 4)


def _row_tile(rows):
    return next(t for t in (512, 400, 320) if rows % t == 0)


def _add_halves(g4, theirs, core, name):
    rows = g4.shape[2]
    tr = _row_tile(rows)

    def body(c_ref, a_ref, b_ref, o_ref):
        o_ref[...] = (a_ref[...] + b_ref[...]).astype(BF16)

    grid_spec = pltpu.PrefetchScalarGridSpec(
        num_scalar_prefetch=1, grid=(4, rows // tr),
        in_specs=[pl.BlockSpec((None, None, tr, D), lambda j, i, c: (j, c[0], i, 0)),
                  pl.BlockSpec((None, None, tr, D), lambda j, i, c: (j, 0, i, 0))],
        out_specs=pl.BlockSpec((None, tr, D), lambda j, i, c: (j, i, 0)))
    return _pcall(
        body, name=name, grid_spec=grid_spec, out_shape=jax.ShapeDtypeStruct((4, rows, D), BF16),
        compiler_params=pltpu.CompilerParams(dimension_semantics=("arbitrary", "arbitrary")),
    )(core, g4, theirs)


def _sum_chips(own, got, order, name):
    rows = own.shape[1]
    tr = _row_tile(rows)

    def body(o_ref, a_ref, b_ref, c_ref, d_ref, out_ref):
        f = lambda r: r[...].astype(F32)
        out_ref[...] = ((f(a_ref) + f(b_ref)) + f(c_ref)) + f(d_ref)

    slot = lambda k: pl.BlockSpec((None, tr, D), lambda i, o: (o[k], i, 0))
    grid_spec = pltpu.PrefetchScalarGridSpec(
        num_scalar_prefetch=1, grid=(rows // tr,), in_specs=[slot(0), slot(1), slot(2), slot(3)],
        out_specs=pl.BlockSpec((tr, D), lambda i, o: (i, 0)))
    return _pcall(
        body, name=name, grid_spec=grid_spec, out_shape=jax.ShapeDtypeStruct((rows, D), F32),
        compiler_params=pltpu.CompilerParams(dimension_semantics=("arbitrary",)),
    )(order, own, got, got, got)


def _place():
    return lax.axis_index("x"), lax.axis_index("y"), lax.axis_index("c")


def _allgather(block, name):
    def body(x_ref, out_ref, token, send_sems, recv_sems, local_sem):
        token[...] = jnp.zeros_like(token)
        x, y, c = _place()
        me, sibling = (x, y, c), (x, y, 1 - c)
        chips = [(1 - x, y), (x, 1 - y), (1 - x, 1 - y)]

        def slot(px, py, pc):
            return out_ref.at[4 * px + 2 * py + pc]

        def copy(k, blk, to, src=None):
            return pltpu.make_async_remote_copy(
                src_ref=slot(*blk) if src is None else src, dst_ref=slot(*blk),
                send_sem=send_sems.at[k], recv_sem=recv_sems.at[k], device_id=to, device_id_type=MESH)

        mine = pltpu.make_async_copy(x_ref, slot(*me), local_sem)
        mine.start()
        first = [copy(0, me, sibling, src=x_ref)]
        first += [copy(1 + j, me, (*chip, c), src=x_ref) for j, chip in enumerate(chips)]
        for cp in first:
            cp.start()
        passed = [copy(4 + j, (*chip, c), sibling) for j, chip in enumerate(chips)]
        for j, chip in enumerate(chips):
            copy(1 + j, (*chip, c), me).wait_recv()
            passed[j].start()
        copy(0, sibling, me).wait_recv()
        for j, chip in enumerate(chips):
            copy(4 + j, (*chip, 1 - c), me).wait_recv()
        for cp in first + passed:
            cp.wait_send()
        mine.wait()

    return _pcall(
        body, name=name,
        out_shape=[jax.ShapeDtypeStruct((8,) + block.shape, block.dtype), jax.ShapeDtypeStruct((8, 128), F32)],
        in_specs=[pl.BlockSpec(memory_space=pl.ANY)],
        out_specs=[pl.BlockSpec(memory_space=pl.ANY), pl.BlockSpec(memory_space=pltpu.VMEM)],
        scratch_shapes=[pltpu.SemaphoreType.DMA((7,)), pltpu.SemaphoreType.DMA((7,)), pltpu.SemaphoreType.DMA(())],
        compiler_params=pltpu.CompilerParams(has_side_effects=True),
    )(block)


HBM_SPEC = pl.BlockSpec(memory_space=pltpu.HBM)
SEM_SPEC = pl.BlockSpec(memory_space=pltpu.SEMAPHORE)
ANY_SPEC = pl.BlockSpec(memory_space=pl.ANY)
EFFECT = pltpu.SideEffectType.DATAFLOW_SIDE_EFFECTING


def _in_hbm(a):
    return pltpu.with_memory_space_constraint(a, pltpu.HBM)


def _start_copies(name, src, land_shape, plan, n):
    def body(src_ref, land_ref, send_sems, recv_sems, src_thru, land_thru, token):
        for k, (s, d, to, _) in enumerate(plan(src_ref, land_ref)):
            pltpu.make_async_remote_copy(src_ref=s, dst_ref=d, send_sem=send_sems.at[k], recv_sem=recv_sems.at[k],
                                         device_id=to, device_id_type=MESH).start()
        token[...] = jnp.zeros_like(token)

    return _pcall(
        body, name=name,
        out_shape=(pltpu.SemaphoreType.DMA((n,)), pltpu.SemaphoreType.DMA((n,)), pltpu.HBM(src.shape, src.dtype),
                   pltpu.HBM(land_shape, src.dtype), jax.ShapeDtypeStruct((8, 128), F32)),
        in_specs=(HBM_SPEC, HBM_SPEC),
        out_specs=(SEM_SPEC, SEM_SPEC, HBM_SPEC, HBM_SPEC, pl.BlockSpec(memory_space=pltpu.VMEM)),
        input_output_aliases={0: 2, 1: 3}, compiler_params=pltpu.CompilerParams(has_side_effects=EFFECT),
    )(_in_hbm(src), _in_hbm(lax.empty(land_shape, src.dtype)))


def _wait_copies(name, started, after, plan):
    send_sems, recv_sems, src_thru, land_thru, _ = started

    def body(src_ref, land_ref, send_sems, recv_sems, *rest):
        for k, (s, _, to, mine) in enumerate(plan(src_ref, land_ref)):
            cp = pltpu.make_async_remote_copy(src_ref=s, dst_ref=mine, send_sem=send_sems.at[k],
                                              recv_sem=recv_sems.at[k], device_id=to, device_id_type=MESH)
            cp.wait_send()
            cp.wait_recv()

    return _pcall(
        body, name=name,
        out_shape=(pltpu.HBM(src_thru.shape, src_thru.dtype), pltpu.HBM(land_thru.shape, land_thru.dtype)),
        in_specs=(HBM_SPEC, HBM_SPEC, SEM_SPEC, SEM_SPEC) + (ANY_SPEC,) * len(after), out_specs=(HBM_SPEC, HBM_SPEC),
        input_output_aliases={0: 0, 1: 1}, compiler_params=pltpu.CompilerParams(has_side_effects=EFFECT),
    )(src_thru, land_thru, send_sems, recv_sems, *after)


def _gather_plan(src_ref, land_ref):
    x, y, c = _place()
    peers = [(x, y, 1 - c), (1 - x, y, c), (x, 1 - y, c), (1 - x, 1 - y, c)]
    return [(src_ref, land_ref.at[4 * x + 2 * y + c], p, land_ref.at[4 * p[0] + 2 * p[1] + p[2]]) for p in peers]


def _swap_plan(src_ref, land_ref):
    x, y, c = _place()
    return [(src_ref.at[:, pl.ds(1 - c, 1)], land_ref, (x, y, 1 - c), land_ref)]


def _exchange_plan(src_ref, land_ref):
    x, y, c = _place()
    chips = [(1 - x, y), (x, 1 - y), (1 - x, 1 - y)]
    return [(src_ref.at[2 * px + py], land_ref.at[2 * x + y], (px, py, c), land_ref.at[2 * px + py]) for px, py in chips]


def _gather_forward(land, block):
    def body(land_ref, out_ref, send_sems, recv_sems):
        x, y, c = _place()
        chips = [(1 - x, y), (x, 1 - y), (1 - x, 1 - y)]

        def copy(k, px, py, pc):
            blk = out_ref.at[4 * px + 2 * py + pc]
            return pltpu.make_async_remote_copy(src_ref=blk, dst_ref=blk, send_sem=send_sems.at[k],
                                                recv_sem=recv_sems.at[k], device_id=(x, y, 1 - c), device_id_type=MESH)

        sent = [copy(k, px, py, c) for k, (px, py) in enumerate(chips)]
        for cp in sent:
            cp.start()
        for k, (px, py) in enumerate(chips):
            copy(k, px, py, 1 - c).wait_recv()
        for cp in sent:
            cp.wait_send()

    land = _pcall(
        body, name="allgather_rest_forward", out_shape=jax.ShapeDtypeStruct(land.shape, land.dtype),
        in_specs=[ANY_SPEC], out_specs=ANY_SPEC, input_output_aliases={0: 0},
        scratch_shapes=[pltpu.SemaphoreType.DMA((3,)), pltpu.SemaphoreType.DMA((3,))],
        compiler_params=pltpu.CompilerParams(has_side_effects=True),
    )(land)

    rows = block.shape[0]
    tr = rows // 4

    def place(me_ref, x_ref, land_ref, out_ref):
        out_ref[...] = x_ref[...]

    x, y, c = _place()
    grid_spec = pltpu.PrefetchScalarGridSpec(
        num_scalar_prefetch=1, grid=(rows // tr,),
        in_specs=[pl.BlockSpec((tr, D), lambda i, me: (i, 0)), ANY_SPEC],
        out_specs=pl.BlockSpec((None, tr, D), lambda i, me: (me[0], i, 0)))
    return _pcall(
        place, name="allgather_rest_own", grid_spec=grid_spec, out_shape=jax.ShapeDtypeStruct(land.shape, land.dtype),
        input_output_aliases={2: 0}, compiler_params=pltpu.CompilerParams(dimension_semantics=("arbitrary",)),
    )((4 * x + 2 * y + c).reshape(1), block, land)


class _ReduceScatter:
    def __init__(self, name, g):
        self.name = name
        rows = g.shape[1]
        self.started = _start_copies(name + "_swap_start", g.reshape(4, 2, rows, D), (4, 1, rows, D), _swap_plan, 1)
        self.token = self.started[4][0, 0]

    def halfway(self, after):
        g4, theirs = _wait_copies(self.name + "_swap_wait", self.started, after, _swap_plan)
        self.own = _add_halves(g4, theirs, lax.axis_index("c").reshape(1), self.name + "_add_halves")
        self.started = _start_copies(self.name + "_exch_start", self.own, self.own.shape, _exchange_plan, 3)
        self.token = self.started[4][0, 0]

    def finish(self, after):
        own, got = _wait_copies(self.name + "_exch_wait", self.started, after, _exchange_plan)
        chip = 2 * lax.axis_index("x") + lax.axis_index("y")
        order = (chip + jnp.arange(4, dtype=jnp.int32)) % 4
        return _sum_chips(own, got, order, self.name + "_sum_chips")


def _pack_small(p):
    z = lambda a, n: jnp.pad(a, ((0, 0), (0, n - a.shape[1])))
    rows = [z(p['rel_bias'], D), z(p['b_fgt'], D), jnp.concatenate([p['g_fox_out'], p['g_chk_out']], axis=1)]
    rows += [p[n] for n in ('g_mix_pre', 'g_mix_post', 'g_mem_kv', 'g_mem_pre', 'g_mem_post', 'g_ff_pre', 'g_ff_post')]
    rows.append(jnp.zeros((SMALL_ROWS - 17, D), F32))
    return jnp.concatenate(rows, axis=0)


def _unpack_small(a):
    out = {'rel_bias': a[0:8, :257], 'b_fgt': a[8:9, :8], 'g_fox_out': a[9:10, :512], 'g_chk_out': a[9:10, 512:]}
    for k, n in enumerate(('g_mix_pre', 'g_mix_post', 'g_mem_kv', 'g_mem_pre', 'g_mem_post', 'g_ff_pre', 'g_ff_post')):
        out[n] = a[10 + k:11 + k]
    return out


_GAP_DEV, _GAP_ROW = divmod(GATE0 + 8, N_IN)
_GAP = CHK0 - GATE0 - 8


def _in_rows_to_proj(g):
    wt = g[:, :N_IN].reshape(8 * N_IN, D)
    return jnp.concatenate([wt[:GATE0], jnp.pad(wt[GATE0:GATE0 + 8], ((0, _GAP), (0, 0))), wt[GATE0 + 8:]], axis=0)


def _proj_rows_to_in(g):
    pad = lambda a: jnp.pad(a, ((0, R_IN - a.shape[0]), (0, 0)))
    lo = N_IN * _GAP_DEV
    shards = [pad(g[N_IN * j:N_IN * (j + 1)]) for j in range(_GAP_DEV)]
    shards.append(pad(jnp.concatenate([g[lo:lo + _GAP_ROW], g[lo + _GAP_ROW + _GAP:lo + N_IN + _GAP]], axis=0)))
    shards += [pad(g[N_IN * j + _GAP:N_IN * (j + 1) + _GAP]) for j in range(_GAP_DEV + 1, 8)]
    return jnp.stack(shards)


def _local_grads(x, mem, tgt, win_t, gw_of, sm, on_grads):
    b_pad = jnp.pad(sm['b_fgt'], ((0, 0), (0, 120)))
    tbl = jnp.pad(sm['rel_bias'], ((0, 0), (0, NREL_PAD - 257)))

    h1, proj, flog = _premix_fwd(x, sm['g_mix_pre'], win_t)
    c = _gate_fwd(flog, b_pad)
    c8 = c[:, :8]
    c2 = jnp.repeat(c8, 128, axis=1)
    ct3 = c8.T.reshape(4, 2, T)
    o_f, lse_f = _fox_fwd(proj, c2, ct3)
    vt3 = _relvec_fwd(tbl).reshape(4, 2, VW)
    kvp = jnp.pad(proj[:, CHK0 + 512:], ((LEFT, 0), (0, 0)))
    o_c, lse_c = _chk_fwd(proj, kvp, vt3)
    gw = gw_of([o_f, o_c])
    w_out, w_mq, w_mk, w_mv, w_mo, w1_t, w2 = (_wblk(gw, n) for n in ('w_out', 'w_mq', 'w_mk', 'w_mv', 'w_mo', 'w_ff1', 'w_ff2'))
    ycat, z, x1, h2, qm = _postmix_fwd(x, o_f, o_c, sm['g_fox_out'], sm['g_chk_out'], w_out,
                                       sm['g_mix_post'], sm['g_mem_pre'], w_mq)
    memn, km, vm = _memkv_fwd(mem, sm['g_mem_kv'], w_mk, w_mv)
    om, ym, x2, h3 = _mem_fwd(qm, x1, km, vm, w_mo, sm['g_mem_post'], sm['g_ff_pre'])
    a, y3, dx3, loss_acc = _ffn_fwd(h3, x2, tgt, w1_t, w2, sm['g_ff_post'])

    gs = {}
    dx2, da, dy3, r, gs['g_ff_post'], gs['g_ff_pre'] = _ffn_bwd(dx3, y3, x2, a, w1_t, w2, sm['g_ff_post'], sm['g_ff_pre'])
    zero = on_grads('A', _wgrad_group("wgrad_ff", [(da, h3), (r, dy3)], 512), None)
    dx1, dym, dqm, dkm, dvm, gs['g_mem_post'], gs['g_mem_pre'] = _mem_bwd(
        dx2, ym, x1, qm, km, vm, w_mo, w_mq, sm['g_mem_post'] + zero, sm['g_mem_pre'])
    zero = on_grads('A halfway', None, [dx1])
    gs['g_mem_kv'] = _memkv_bwd(dkm, dvm, mem, w_mk, w_mv)
    dz, dof, doc, gs['g_mix_post'], gs['g_fox_out'], gs['g_chk_out'] = _postmix_bwd(
        dx1, z, o_f, o_c, w_out, sm['g_mix_post'] + zero, sm['g_fox_out'], sm['g_chk_out'])
    zero = on_grads('B', _wgrad_group("wgrad_mem_out", [(ycat, dz), (h2, dqm), (memn, dkm), (memn, dvm), (om, dym)], 128), None)
    dq_f, dk_f, dv_f, dct, dcq = _fox_bwd(proj, c2, ct3 + zero, o_f, lse_f, dof)
    zero = on_grads('B halfway', None, [dq_f])
    dq_c, dkp, dvp, gv = _chk_bwd(proj, kvp, vt3 + zero, o_c, lse_c, doc)
    gs['rel_bias'] = _relvec_bwd(gv.reshape(8, VW))[:, :257]
    dc = jnp.pad(dct.reshape(8, T).T + dcq[:, ::128], ((0, 0), (0, 120)))
    dflog, db = _gate_bwd(dc, flog, b_pad)
    gs['b_fgt'] = db[0:1, :8]
    dproj = jnp.concatenate([dq_f, dk_f, dv_f, dflog, dq_c, dkp[LEFT:], dvp[LEFT:]], axis=1).astype(BF16)
    zero = on_grads('C', _proj_rows_to_in(_wgrad(dproj, h1, "wgrad_in")), None)
    grad_x, gs['g_mix_pre'] = _premix_bwd(dx1, x, dproj, win_t, sm['g_mix_pre'] + zero)
    return loss_acc[0, 0], grad_x, gs


def kernel(x, mem, w_in, b_fgt, rel_bias, g_fox_out, g_chk_out, w_out, g_mix_pre, g_mix_post, g_mem_kv, w_mq, w_mk, w_mv, w_mo, g_mem_pre, g_mem_post, w_ff1, w_ff2, g_ff_pre, g_ff_post, loss_target, m_w_in, m_b_fgt, m_rel_bias, m_g_fox_out, m_g_chk_out, m_w_out, m_g_mix_pre, m_g_mix_post, m_g_mem_kv, m_w_mq, m_w_mk, m_w_mv, m_w_mo, m_g_mem_pre, m_g_mem_post, m_w_ff1, m_w_ff2, m_g_ff_pre, m_g_ff_post, v_w_in, v_b_fgt, v_rel_bias, v_g_fox_out, v_g_chk_out, v_w_out, v_g_mix_pre, v_g_mix_post, v_g_mem_kv, v_w_mq, v_w_mk, v_w_mv, v_w_mo, v_g_mem_pre, v_g_mem_post, v_w_ff1, v_w_ff2, v_g_ff_pre, v_g_ff_post):
    args = dict(locals())
    two_d = lambda a: a.reshape(a.shape[-2:])
    w = {n: two_d(args[n]) for n in WEIGHTS}
    m = {n: two_d(args['m_' + n]) for n in WEIGHTS}
    v = {n: two_d(args['v_' + n]) for n in WEIGHTS}

    sm = {n: w[n] for n in SMALL}
    shard_in = jnp.pad(w['w_in'].T, ((0, R_IN - N_IN), (0, 0))).astype(BF16)
    gathered_in, zero = _allgather(shard_in, "allgather_w_in")
    win_t = _in_rows_to_proj(gathered_in)
    shard_rest = (jnp.concatenate([w['w_ff1'].T, w['w_ff2'], w['w_out'], w['w_mq'], w['w_mk'], w['w_mv'], w['w_mo']],
                                  axis=0) + zero[0, 0]).astype(BF16)
    rest = _start_copies("allgather_rest_start", shard_rest, (8, R_REST, D), _gather_plan, 4)
    sm['g_mix_pre'] = sm['g_mix_pre'] + rest[4][0, 0]

    def gw_of(after):
        block, land = _wait_copies("allgather_rest_wait", rest, after, _gather_plan)
        return _gather_forward(land, block)

    rs = {}

    def on_grads(stage, g, after):
        if stage.endswith('halfway'):
            rs[stage[0]].halfway(after)
            return rs[stage[0]].token
        rs[stage] = _ReduceScatter("rs_" + stage.lower(), g)
        if stage == 'C':
            rs[stage].halfway([rs[stage].started[4]])
        return rs[stage].token

    loss_local, grad_x, gs = _local_grads(x[0], mem[0], loss_target[0], win_t, gw_of, sm, on_grads)
    loss = lax.psum(loss_local, ("x", "y", "c"))
    g_a, g_b = rs['A'].finish([grad_x]), rs['B'].finish([grad_x])
    g_big = {'w_ff1': g_a[:512].T, 'w_ff2': g_a[512:]}
    for k, n in enumerate(('w_out', 'w_mq', 'w_mk', 'w_mv', 'w_mo')):
        g_big[n] = g_b[128 * k:128 * (k + 1)]

    grads, deltas, new_m, new_v = {}, {}, {}, {}
    gparts, _ = _allgather(_pack_small(gs), "allgather_small_grads")
    gsum, d_s, m_s, v_s = _adamw_small(_pack_small(sm), gparts, _pack_small({n: m[n] for n in SMALL}),
                                       _pack_small({n: v[n] for n in SMALL}))
    for dst, packed in ((grads, gsum), (deltas, d_s), (new_m, m_s), (new_v, v_s)):
        dst.update(_unpack_small(packed))

    g_big['w_in'] = rs['C'].finish([gsum])[:N_IN].T
    for n in BIG:
        grads[n] = g_big[n]
        deltas[n], new_m[n], new_v[n] = _adamw(w[n], g_big[n], m[n], v[n], "adamw_" + n)

    out = [loss, grad_x[None]]
    for group in (grads, deltas, new_m, new_v):
        out += [group[n].reshape(args[n].shape) for n in WEIGHTS]
    return tuple(out)
```

```python
import functools

import jax
import jax.numpy as jnp
from jax import lax
from jax.experimental import pallas as pl
from jax.experimental.pallas import tpu as pltpu

F32 = jnp.float32
BF16 = jnp.bfloat16
MESH = pl.DeviceIdType.MESH

T = 2048
D = 1024
NMEM = 256
DFF = 4096
EPS = 1e-6
TM = 256
TQ = 256
FQ = 512
HD = 64
SCALE = HD ** -0.5
MEM_HEADS = 4
MEM_HD = 256
MEM_SCALE = MEM_HD ** -0.5
NEG = -1e30
LEFT = 512
WIN = LEFT + TQ
VW = 1024
NREL_PAD = 384
PROJ = 3200
GATE0 = 1536
CHK0 = 1664
VMEM_BIG = 56 * 1024 * 1024

ADAM_LR = 0.001
ADAM_B1 = 0.9
ADAM_B2 = 0.999
ADAM_EPS = 1e-08
ADAM_WD = 0.01
ADAM_STEP = 10

N_IN = 385
R_IN = 400
R_REST = 1664
W_ROWS = {'w_ff1': (0, 512), 'w_ff2': (512, 512),
          'w_out': (1024, 128), 'w_mq': (1152, 128), 'w_mk': (1280, 128), 'w_mv': (1408, 128), 'w_mo': (1536, 128)}
R_A, R_B = 1024, 640
SMALL_ROWS = 24

WEIGHTS = ['w_in', 'b_fgt', 'rel_bias', 'g_fox_out', 'g_chk_out', 'w_out', 'g_mix_pre', 'g_mix_post', 'g_mem_kv',
           'w_mq', 'w_mk', 'w_mv', 'w_mo', 'g_mem_pre', 'g_mem_post', 'w_ff1', 'w_ff2', 'g_ff_pre', 'g_ff_post']
BIG = ['w_in', 'w_out', 'w_mq', 'w_mk', 'w_mv', 'w_mo', 'w_ff1', 'w_ff2']
SMALL = [n for n in WEIGHTS if n not in BIG]


def _pcall(body, **kw):
    return pl.pallas_call(body, **kw)


def _nn(a, b):
    return jnp.dot(a, b, preferred_element_type=F32)


def _nt(a, b):
    return lax.dot_general(a, b, (((1,), (1,)), ((), ())), preferred_element_type=F32)


def _tn(a, b):
    return lax.dot_general(a, b, (((0,), (0,)), ((), ())), preferred_element_type=F32)


def _w(ref):
    v = ref[...]
    return v if v.ndim == 2 else v.reshape(-1, v.shape[-1])


def _rstd(x):
    return lax.rsqrt(jnp.mean(x * x, axis=-1, keepdims=True) + EPS)


def _rms(x, g):
    return x * _rstd(x) * g


def _rms_bwd(x, g, dy):
    r = _rstd(x)
    xh = x * r
    dg = jnp.sum(dy * xh, axis=0, keepdims=True)
    dxh = dy * g
    dx = r * (dxh - xh * jnp.mean(dxh * xh, axis=-1, keepdims=True))
    return dx, dg


def _resident(a):
    if isinstance(a, tuple):
        _, shape, index = a
        return pl.BlockSpec(shape, lambda *_: index, pipeline_mode=pl.Buffered(1))
    return pl.BlockSpec(a.shape, lambda *_, nd=a.ndim: (0,) * nd, pipeline_mode=pl.Buffered(1))


def _wblk(gw, name):
    r0, rows = W_ROWS[name]
    return (gw, (8, rows, D), (0, r0 // rows, 0))


def _tok_call(body, name, tiled, full, outs_tiled, outs_acc=(), rows=T, tm=TM, vmem=None):
    in_specs = [pl.BlockSpec((tm, a.shape[1]), lambda i: (i, 0)) for a in tiled]
    in_specs += [_resident(a) for a in full]
    full = [a[0] if isinstance(a, tuple) else a for a in full]
    out_shape = [jax.ShapeDtypeStruct((rows, c), dt) for c, dt in outs_tiled]
    out_shape += [jax.ShapeDtypeStruct(s, F32) for s in outs_acc]
    out_specs = [pl.BlockSpec((tm, c), lambda i: (i, 0)) for c, _ in outs_tiled]
    out_specs += [pl.BlockSpec(s, lambda i, nd=len(s): (0,) * nd) for s in outs_acc]
    return _pcall(
        body, name=name, grid=(rows // tm,), in_specs=in_specs, out_specs=out_specs, out_shape=out_shape,
        compiler_params=pltpu.CompilerParams(dimension_semantics=("arbitrary",), vmem_limit_bytes=vmem),
    )(*tiled, *full)


def _one_call(body, name, ins, outs, vmem=None):
    whole = lambda s: pl.BlockSpec(s, lambda i, nd=len(s): (0,) * nd)
    return _pcall(
        body, name=name, grid=(1,), in_specs=[_resident(a) for a in ins], out_specs=[whole(s) for s, _ in outs],
        out_shape=[jax.ShapeDtypeStruct(s, dt) for s, dt in outs],
        compiler_params=pltpu.CompilerParams(dimension_semantics=("arbitrary",), vmem_limit_bytes=vmem),
    )(*[a[0] if isinstance(a, tuple) else a for a in ins])


def _premix_fwd(x, g_pre, win_t):
    def body(x_ref, g_ref, w_ref, h_ref, proj_ref, flog_ref):
        h = _rms(x_ref[...], g_ref[...]).astype(BF16)
        h_ref[...] = h
        p = _nt(h, w_ref[...])
        proj_ref[...] = p.astype(BF16)
        flog_ref[...] = p[:, GATE0:GATE0 + 128]

    return _tok_call(body, "premix_fwd", [x], [g_pre, win_t],
                     [(D, BF16), (PROJ, BF16), (128, F32)], vmem=VMEM_BIG)


def _postmix_fwd(x, o_f, o_c, g_fo, g_co, w_out, g_post, g_mpre, w_mq):
    def body(x_ref, of_ref, oc_ref, gfo_ref, gco_ref, wo_ref, gp_ref, gm_ref, wq_ref,
             y_ref, z_ref, x1_ref, h2_ref, qm_ref):
        y_ref[:, :512] = _rms(of_ref[...], gfo_ref[...]).astype(BF16)
        y_ref[:, 512:] = _rms(oc_ref[...], gco_ref[...]).astype(BF16)
        z = _nn(y_ref[...], _w(wo_ref))
        z_ref[...] = z
        x1 = x_ref[...] + _rms(z, gp_ref[...])
        x1_ref[...] = x1
        h2 = _rms(x1, gm_ref[...]).astype(BF16)
        h2_ref[...] = h2
        qm_ref[...] = _nn(h2, _w(wq_ref)).astype(BF16)

    return _tok_call(body, "postmix_fwd", [x, o_f, o_c], [g_fo, g_co, w_out, g_post, g_mpre, w_mq],
                     [(D, BF16), (D, F32), (D, F32), (D, BF16), (D, BF16)], vmem=VMEM_BIG)


def _memkv_fwd(mem, g_kv, w_mk, w_mv):
    def body(m_ref, g_ref, wk_ref, wv_ref, mn_ref, k_ref, v_ref):
        mn = _rms(m_ref[...], g_ref[...]).astype(BF16)
        mn_ref[...] = mn
        k_ref[...] = _nn(mn, _w(wk_ref)).astype(BF16)
        v_ref[...] = _nn(mn, _w(wv_ref)).astype(BF16)

    return _tok_call(body, "memkv_fwd", [mem], [g_kv, w_mk, w_mv],
                     [(D, BF16), (D, BF16), (D, BF16)], rows=NMEM, tm=NMEM, vmem=VMEM_BIG)


def _mem_fwd(qm, x1, km, vm, w_mo, g_post, g_fpre):
    def body(q_ref, x1_ref, k_ref, v_ref, wo_ref, gp_ref, gf_ref, om_ref, ym_ref, x2_ref, h3_ref):
        for h in range(MEM_HEADS):
            sl = slice(h * MEM_HD, (h + 1) * MEM_HD)
            s = _nt(q_ref[:, sl], k_ref[:, sl]) * MEM_SCALE
            p = jnp.exp(s - jnp.max(s, axis=-1, keepdims=True))
            p = p / jnp.sum(p, axis=-1, keepdims=True)
            om_ref[:, sl] = _nn(p.astype(BF16), v_ref[:, sl]).astype(BF16)
        ym = _nn(om_ref[...], _w(wo_ref))
        ym_ref[...] = ym
        x2 = x1_ref[...] + _rms(ym, gp_ref[...])
        x2_ref[...] = x2
        h3_ref[...] = _rms(x2, gf_ref[...]).astype(BF16)

    return _tok_call(body, "mem_fwd", [qm, x1], [km, vm, w_mo, g_post, g_fpre],
                     [(D, BF16), (D, F32), (D, F32), (D, BF16)], vmem=VMEM_BIG)


def _ffn_fwd(h3, x2, tgt, w1_t, w2, g_post):
    def body(h_ref, x2_ref, t_ref, w1_ref, w2_ref, g_ref, a_ref, y_ref, dx_ref, loss_ref):
        @pl.when(pl.program_id(0) == 0)
        def _():
            loss_ref[...] = jnp.zeros_like(loss_ref)

        a = _nt(h_ref[...], _w(w1_ref))
        a_ref[...] = a.astype(BF16)
        r = jnp.square(jnp.maximum(a, 0.0)).astype(BF16)
        y = _nn(r, _w(w2_ref))
        y_ref[...] = y
        e = x2_ref[...] + _rms(y, g_ref[...]) - t_ref[...]
        dx_ref[...] = e * (1.0 / D)
        loss_ref[...] += 0.5 * jnp.sum(jnp.sum(e * e, axis=-1, keepdims=True) * (1.0 / D))

    return _tok_call(body, "ffn_fwd", [h3, x2, tgt], [w1_t, w2, g_post],
                     [(DFF, BF16), (D, F32), (D, F32)], [(8, 128)], vmem=VMEM_BIG)


def _tri(lower):
    r = lax.broadcasted_iota(jnp.int32, (128, 128), 0)
    c = lax.broadcasted_iota(jnp.int32, (128, 128), 1)
    return jnp.where(r >= c if lower else c >= r, 1.0, 0.0).astype(F32)


def _hdot(a, b):
    return jnp.dot(a, b, preferred_element_type=F32, precision=lax.Precision.HIGHEST)


def _gate_fwd(flog, b_pad):
    def body(f_ref, b_ref, c_ref):
        tri = _tri(True)

        def step(i, carry):
            rows = pl.ds(pl.multiple_of(i * 128, 128), 128)
            z = f_ref[rows, :] + b_ref[...]
            lf = jnp.minimum(z, 0.0) - jnp.log(1.0 + jnp.exp(-jnp.abs(z)))
            cb = _hdot(tri, lf) + carry
            c_ref[rows, :] = cb
            return cb[127:128, :]

        lax.fori_loop(0, T // 128, step, jnp.zeros((1, 128), F32))

    return _one_call(body, "gate_fwd", [flog, b_pad], [((T, 128), F32)])[0]


def _gate_bwd(dc, flog, b_pad):
    def body(dc_ref, f_ref, b_ref, df_ref, db_ref):
        tri = _tri(False)

        def step(j, carry):
            run, db = carry
            i = T // 128 - 1 - j
            rows = pl.ds(pl.multiple_of(i * 128, 128), 128)
            dcb = dc_ref[rows, :]
            rb = _hdot(tri, dcb) + run
            z = f_ref[rows, :] + b_ref[...]
            df = rb * (1.0 / (1.0 + jnp.exp(z)))
            df_ref[rows, :] = df.astype(BF16)
            return run + jnp.sum(dcb, axis=0, keepdims=True), db + jnp.sum(df, axis=0, keepdims=True)

        _, db = lax.fori_loop(0, T // 128, step, (jnp.zeros((1, 128), F32), jnp.zeros((1, 128), F32)))
        db_ref[...] = jnp.broadcast_to(db, (8, 128))

    return _one_call(body, "gate_bwd", [dc, flog, b_pad], [((T, 128), BF16), ((8, 128), F32)])


def _lane_lo(rows=TQ):
    return lax.broadcasted_iota(jnp.int32, (rows, 128), 1) < HD


def _half(v, lo, a, scale=None):
    keep = lo if a == 0 else jnp.logical_not(lo)
    v = v.astype(F32) if scale is None else v.astype(F32) * scale
    return jnp.where(keep, v, 0.0).astype(BF16)


def _fox_specs():
    return [pl.BlockSpec((FQ, 128), lambda h, i: (i, h)),
            pl.BlockSpec((T, 128), lambda h, i: (0, 4 + h)),
            pl.BlockSpec((T, 128), lambda h, i: (0, 8 + h))]


def _fox_fwd(proj, c2, ct3):
    def body(q_ref, k_ref, v_ref, c_ref, ct_ref, o_ref, l_ref):
        i = pl.program_id(1)
        lo = _lane_lo(FQ)
        causal = lax.broadcasted_iota(jnp.int32, (FQ, FQ), 1) <= lax.broadcasted_iota(jnp.int32, (FQ, FQ), 0)
        q = q_ref[...]
        qs = [_half(q, lo, a, SCALE) for a in range(2)]
        cqs = [c_ref[:, 128 * a:128 * a + 1] for a in range(2)]

        def tile(off, carry, diagonal):
            kblk = k_ref[pl.ds(off, FQ), :]
            vblk = v_ref[pl.ds(off, FQ), :]
            new = []
            for a in range(2):
                m, l, acc = carry[a]
                s = _nt(qs[a], kblk) + (cqs[a] - ct_ref[a:a + 1, pl.ds(off, FQ)])
                if diagonal:
                    s = jnp.where(causal, s, NEG)
                m2 = jnp.maximum(m, jnp.max(s, axis=-1, keepdims=True))
                p = jnp.exp(s - m2)
                alpha = jnp.exp(m - m2)
                new.append((m2, alpha * l + jnp.sum(p, axis=-1, keepdims=True),
                            alpha * acc + _nn(p.astype(BF16), vblk)))
            return tuple(new)

        init = (jnp.full((FQ, 1), NEG, F32), jnp.zeros((FQ, 1), F32), jnp.zeros((FQ, 128), F32))
        carry = lax.fori_loop(0, i, lambda kb, c: tile(pl.multiple_of(kb * FQ, FQ), c, False), (init, init))
        carry = tile(pl.multiple_of(i * FQ, FQ), carry, True)
        outs = []
        for a in range(2):
            m, l, acc = carry[a]
            outs.append(acc / l)
            l_ref[:, 128 * a:128 * a + 128] = jnp.broadcast_to(m + jnp.log(l), (FQ, 128))
        o_ref[...] = jnp.where(lo, outs[0], outs[1])

    return _pcall(
        body, name="fox_fwd", grid=(4, T // FQ),
        in_specs=_fox_specs() + [pl.BlockSpec((FQ, 256), lambda h, i: (i, h)),
                                 pl.BlockSpec((None, 2, T), lambda h, i: (h, 0, 0))],
        out_specs=[pl.BlockSpec((FQ, 128), lambda h, i: (i, h)), pl.BlockSpec((FQ, 256), lambda h, i: (i, h))],
        out_shape=[jax.ShapeDtypeStruct((T, 512), F32), jax.ShapeDtypeStruct((T, 1024), F32)],
        compiler_params=pltpu.CompilerParams(dimension_semantics=("arbitrary", "arbitrary"), vmem_limit_bytes=VMEM_BIG),
    )(proj, proj, proj, c2, ct3)


def _fox_bwd(proj, c2, ct3, o, lse, do):
    def body(q_ref, k_ref, v_ref, c_ref, ct_ref, o_ref, l_ref, do_ref, dq_ref, dkb_ref, dvb_ref, dct_ref, dcq_ref,
             dk_ref, dv_ref):
        i = pl.program_id(1)

        @pl.when(i == 0)
        def _():
            dk_ref[...] = jnp.zeros_like(dk_ref)
            dv_ref[...] = jnp.zeros_like(dv_ref)
            dct_ref[...] = jnp.zeros_like(dct_ref)

        lo = _lane_lo(FQ)
        causal = lax.broadcasted_iota(jnp.int32, (FQ, FQ), 1) <= lax.broadcasted_iota(jnp.int32, (FQ, FQ), 0)
        q = q_ref[...]
        do_v = do_ref[...]
        prod = do_v * o_ref[...]
        qs = [_half(q, lo, a, SCALE) for a in range(2)]
        dos = [_half(do_v, lo, a) for a in range(2)]
        deltas = [jnp.sum(jnp.where(lo if a == 0 else jnp.logical_not(lo), prod, 0.0), axis=-1, keepdims=True)
                  for a in range(2)]
        cqs = [c_ref[:, 128 * a:128 * a + 1] for a in range(2)]
        las = [l_ref[:, 128 * a:128 * a + 1] for a in range(2)]

        def tile(off, carry, diagonal):
            kblk = k_ref[pl.ds(off, FQ), :]
            vblk = v_ref[pl.ds(off, FQ), :]
            new = []
            dk = jnp.zeros((FQ, 128), F32)
            dv = jnp.zeros((FQ, 128), F32)
            for a in range(2):
                dq_acc, rs = carry[a]
                s = _nt(qs[a], kblk) + (cqs[a] - ct_ref[a:a + 1, pl.ds(off, FQ)])
                if diagonal:
                    s = jnp.where(causal, s, NEG)
                p = jnp.exp(s - las[a])
                ds = p * (_nt(dos[a], vblk) - deltas[a])
                dsb = ds.astype(BF16)
                dk = dk + _tn(dsb, qs[a])
                dv = dv + _tn(p.astype(BF16), dos[a])
                dct_ref[a:a + 1, pl.ds(off, FQ)] -= jnp.sum(ds, axis=0, keepdims=True)
                new.append((dq_acc + _nn(dsb, kblk), rs + jnp.sum(ds, axis=-1, keepdims=True)))
            dk_ref[pl.ds(off, FQ), :] += dk
            dv_ref[pl.ds(off, FQ), :] += dv
            return tuple(new)

        init = (jnp.zeros((FQ, 128), F32), jnp.zeros((FQ, 1), F32))
        carry = lax.fori_loop(0, i, lambda kb, c: tile(pl.multiple_of(kb * FQ, FQ), c, False), (init, init))
        carry = tile(pl.multiple_of(i * FQ, FQ), carry, True)
        for a in range(2):
            dcq_ref[:, 128 * a:128 * a + 128] = jnp.broadcast_to(carry[a][1], (FQ, 128))
        dq_ref[...] = (jnp.where(lo, carry[0][0], carry[1][0]) * SCALE).astype(BF16)

        @pl.when(i == T // FQ - 1)
        def _():
            dkb_ref[...] = dk_ref[...].astype(BF16)
            dvb_ref[...] = dv_ref[...].astype(BF16)

    blk = pl.BlockSpec((FQ, 128), lambda h, i: (i, h))
    wide = pl.BlockSpec((FQ, 256), lambda h, i: (i, h))
    rows = pl.BlockSpec((None, 2, T), lambda h, i: (h, 0, 0))
    col = pl.BlockSpec((T, 128), lambda h, i: (0, h))
    return _pcall(
        body, name="fox_bwd", grid=(4, T // FQ),
        in_specs=_fox_specs() + [wide, rows, blk, wide, blk],
        out_specs=[blk, col, col, rows, wide],
        out_shape=[jax.ShapeDtypeStruct((T, 512), BF16), jax.ShapeDtypeStruct((T, 512), BF16),
                   jax.ShapeDtypeStruct((T, 512), BF16), jax.ShapeDtypeStruct((4, 2, T), F32),
                   jax.ShapeDtypeStruct((T, 1024), F32)],
        scratch_shapes=[pltpu.VMEM((T, 128), F32), pltpu.VMEM((T, 128), F32)],
        compiler_params=pltpu.CompilerParams(dimension_semantics=("arbitrary", "arbitrary"), vmem_limit_bytes=VMEM_BIG),
    )(proj, proj, proj, c2, ct3, o, lse, do)


def _rel_onehot():
    ridx = lax.broadcasted_iota(jnp.int32, (NREL_PAD, VW), 0)
    j = lax.broadcasted_iota(jnp.int32, (NREL_PAD, VW), 1)
    return jnp.where(ridx == jnp.clip(TQ + LEFT - 1 - j, -128, 128) + 128, 1.0, 0.0).astype(F32)


def _relvec_fwd(tbl):
    def body(t_ref, v_ref):
        v_ref[...] = _hdot(t_ref[...], _rel_onehot())

    return _one_call(body, "relvec_fwd", [tbl], [((8, VW), F32)])[0]


def _relvec_bwd(gv):
    def body(g_ref, t_ref):
        t_ref[...] = lax.dot_general(g_ref[...], _rel_onehot(), (((1,), (1,)), ((), ())),
                                     preferred_element_type=F32, precision=lax.Precision.HIGHEST)

    return _one_call(body, "relvec_bwd", [gv], [((8, NREL_PAD), F32)])[0]


def _chk_bias(vt_ref, a):
    vb = jnp.broadcast_to(vt_ref[a:a + 1, :], (TQ, VW))
    y = pltpu.roll(vb, VW - (TQ - 1), 1, stride=1, stride_axis=0)[:, :WIN]
    cr = lax.broadcasted_iota(jnp.int32, (TQ, WIN), 0) // 64
    cm = lax.broadcasted_iota(jnp.int32, (TQ, WIN), 1) // 64
    return jnp.where((cm >= cr) & (cm <= cr + 8), y, NEG)


def _chk_specs():
    return [pl.BlockSpec((TQ, 128), lambda h, i: (i, CHK0 // 128 + h)),
            pl.BlockSpec((T + LEFT, 128), lambda h, i: (0, h)),
            pl.BlockSpec((T + LEFT, 128), lambda h, i: (0, 4 + h)),
            pl.BlockSpec((None, 2, VW), lambda h, i: (h, 0, 0))]


def _chk_fwd(proj, kvp, vt3):
    def body(q_ref, k_ref, v_ref, vt_ref, o_ref, l_ref, bias_ref):
        i = pl.program_id(1)

        @pl.when(i == 0)
        def _():
            for a in range(2):
                bias_ref[a] = _chk_bias(vt_ref, a)

        lo = _lane_lo()
        off = pl.multiple_of(i * TQ, TQ)
        kw = k_ref[pl.ds(off, WIN), :]
        vw = v_ref[pl.ds(off, WIN), :]
        real = lax.broadcasted_iota(jnp.int32, (TQ, WIN), 1) + off >= LEFT
        q = q_ref[...]
        outs = []
        for a in range(2):
            s = jnp.where(real, _nt(_half(q, lo, a), kw) * SCALE + bias_ref[a], NEG)
            m = jnp.max(s, axis=-1, keepdims=True)
            p = jnp.exp(s - m)
            l = jnp.sum(p, axis=-1, keepdims=True)
            outs.append(_nn(p.astype(BF16), vw) / l)
            l_ref[:, 128 * a:128 * a + 128] = jnp.broadcast_to(m + jnp.log(l), (TQ, 128))
        o_ref[...] = jnp.where(lo, outs[0], outs[1])

    return _pcall(
        body, name="chk_fwd", grid=(4, T // TQ), in_specs=_chk_specs(),
        out_specs=[pl.BlockSpec((TQ, 128), lambda h, i: (i, h)), pl.BlockSpec((TQ, 256), lambda h, i: (i, h))],
        out_shape=[jax.ShapeDtypeStruct((T, 512), F32), jax.ShapeDtypeStruct((T, 1024), F32)],
        scratch_shapes=[pltpu.VMEM((2, TQ, WIN), F32)],
        compiler_params=pltpu.CompilerParams(dimension_semantics=("arbitrary", "arbitrary")),
    )(proj, kvp, kvp, vt3)


def _chk_bwd(proj, kvp, vt3, o, lse, do):
    nq = T // TQ

    def body(q_ref, k_ref, v_ref, vt_ref, o_ref, l_ref, do_ref, dq_ref, dkb_ref, dvb_ref, gv_ref, bias_ref, dsum_ref,
             dk_ref, dv_ref):
        i = pl.program_id(1)

        @pl.when(i == 0)
        def _():
            for a in range(2):
                bias_ref[a] = _chk_bias(vt_ref, a)
            dsum_ref[...] = jnp.zeros_like(dsum_ref)
            dk_ref[...] = jnp.zeros_like(dk_ref)
            dv_ref[...] = jnp.zeros_like(dv_ref)

        lo = _lane_lo()
        off = pl.multiple_of(i * TQ, TQ)
        kw = k_ref[pl.ds(off, WIN), :]
        vw = v_ref[pl.ds(off, WIN), :]
        real = lax.broadcasted_iota(jnp.int32, (TQ, WIN), 1) + off >= LEFT
        q = q_ref[...]
        do_v = do_ref[...]
        prod = do_v * o_ref[...]
        dqs = []
        for a in range(2):
            keep = lo if a == 0 else jnp.logical_not(lo)
            qa = _half(q, lo, a)
            doa = _half(do_v, lo, a)
            delta = jnp.sum(jnp.where(keep, prod, 0.0), axis=-1, keepdims=True)
            s = jnp.where(real, _nt(qa, kw) * SCALE + bias_ref[a], NEG)
            p = jnp.exp(s - l_ref[:, 128 * a:128 * a + 1])
            ds = p * (_nt(doa, vw) - delta)
            dsum_ref[a] += ds
            dsb = ds.astype(BF16)
            dk_ref[pl.ds(off, WIN), :] += _tn(dsb, qa) * SCALE
            dv_ref[pl.ds(off, WIN), :] += _tn(p.astype(BF16), doa)
            dqs.append(_nn(dsb, kw))
        dq_ref[...] = (jnp.where(lo, dqs[0], dqs[1]) * SCALE).astype(BF16)

        @pl.when(i == nq - 1)
        def _():
            dkb_ref[...] = dk_ref[LEFT:, :].astype(BF16)
            dvb_ref[...] = dv_ref[LEFT:, :].astype(BF16)
            rr = lax.broadcasted_iota(jnp.int32, (TQ, TQ), 0)
            cc = lax.broadcasted_iota(jnp.int32, (TQ, TQ), 1)
            flip = jnp.where(rr + cc == TQ - 1, 1.0, 0.0).astype(F32)
            for a in range(2):
                dpad = jnp.concatenate([dsum_ref[a], jnp.zeros((TQ, VW - WIN), F32)], axis=1)
                z = pltpu.roll(_hdot(flip, dpad), 0, 1, stride=1, stride_axis=0)
                gv_ref[a:a + 1, :] = jnp.sum(z, axis=0, keepdims=True)

    blk = pl.BlockSpec((TQ, 128), lambda h, i: (i, h))
    wide = pl.BlockSpec((TQ, 256), lambda h, i: (i, h))
    col = pl.BlockSpec((T, 128), lambda h, i: (0, h))
    return _pcall(
        body, name="chk_bwd", grid=(4, nq), in_specs=_chk_specs() + [blk, wide, blk],
        out_specs=[blk, col, col, pl.BlockSpec((None, 2, VW), lambda h, i: (h, 0, 0))],
        out_shape=[jax.ShapeDtypeStruct((T, 512), BF16), jax.ShapeDtypeStruct((T, 512), BF16),
                   jax.ShapeDtypeStruct((T, 512), BF16), jax.ShapeDtypeStruct((4, 2, VW), F32)],
        scratch_shapes=[pltpu.VMEM((2, TQ, WIN), F32), pltpu.VMEM((2, TQ, WIN), F32),
                        pltpu.VMEM((T + LEFT, 128), F32), pltpu.VMEM((T + LEFT, 128), F32)],
        compiler_params=pltpu.CompilerParams(dimension_semantics=("arbitrary", "arbitrary")),
    )(proj, kvp, kvp, vt3, o, lse, do)


def _zero_at_start(*refs):
    @pl.when(pl.program_id(0) == 0)
    def _():
        for r in refs:
            r[...] = jnp.zeros_like(r)


def _ffn_bwd(dx3, y3, x2, a, w1_t, w2, g_post, g_pre):
    def body(dx3_ref, y_ref, x2_ref, a_ref, w1_ref, w2_ref, gp_ref, gf_ref,
             dx2_ref, da_ref, dy_ref, r_ref, dgp_ref, dgf_ref):
        _zero_at_start(dgp_ref, dgf_ref)
        dx3_v = dx3_ref[...]
        dy, dgp = _rms_bwd(y_ref[...], gp_ref[...], dx3_v)
        dgp_ref[...] += dgp
        dyb = dy.astype(BF16)
        dy_ref[...] = dyb
        ra = jnp.maximum(a_ref[...].astype(F32), 0.0)
        r_ref[...] = jnp.square(ra).astype(BF16)
        da = (_nt(dyb, _w(w2_ref)) * (2.0 * ra)).astype(BF16)
        da_ref[...] = da
        dh, dgf = _rms_bwd(x2_ref[...], gf_ref[...], _nn(da, _w(w1_ref)))
        dgf_ref[...] += dgf
        dx2_ref[...] = dx3_v + dh

    return _tok_call(body, "ffn_bwd", [dx3, y3, x2, a], [w1_t, w2, g_post, g_pre],
                     [(D, F32), (DFF, BF16), (D, BF16), (DFF, BF16)], [(1, D), (1, D)], vmem=VMEM_BIG)


def _mem_bwd(dx2, ym, x1, qm, km, vm, w_mo, w_mq, g_post, g_pre):
    def body(dx2_ref, ym_ref, x1_ref, q_ref, k_ref, v_ref, wo_ref, wq_ref, gp_ref, gm_ref,
             dx1_ref, dym_ref, dq_ref, dk_ref, dv_ref, dgp_ref, dgm_ref, dom_ref):
        _zero_at_start(dk_ref, dv_ref, dgp_ref, dgm_ref)
        dx2_v = dx2_ref[...]
        dym, dgp = _rms_bwd(ym_ref[...], gp_ref[...], dx2_v)
        dgp_ref[...] += dgp
        dymb = dym.astype(BF16)
        dym_ref[...] = dymb
        dom_ref[...] = _nt(dymb, _w(wo_ref)).astype(BF16)
        for h in range(MEM_HEADS):
            sl = slice(h * MEM_HD, (h + 1) * MEM_HD)
            qh, kh, doh = q_ref[:, sl], k_ref[:, sl], dom_ref[:, sl]
            s = _nt(qh, kh) * MEM_SCALE
            p = jnp.exp(s - jnp.max(s, axis=-1, keepdims=True))
            p = p / jnp.sum(p, axis=-1, keepdims=True)
            dp = _nt(doh, v_ref[:, sl])
            ds = (p * (dp - jnp.sum(p * dp, axis=-1, keepdims=True))).astype(BF16)
            dq_ref[:, sl] = (_nn(ds, kh) * MEM_SCALE).astype(BF16)
            dk_ref[:, sl] += _tn(ds, qh) * MEM_SCALE
            dv_ref[:, sl] += _tn(p.astype(BF16), doh)
        dh, dgm = _rms_bwd(x1_ref[...], gm_ref[...], _nt(dq_ref[...], _w(wq_ref)))
        dgm_ref[...] += dgm
        dx1_ref[...] = dx2_v + dh

    in_specs = [pl.BlockSpec((TM, D), lambda i: (i, 0))] * 4
    in_specs += [_resident(a) for a in (km, vm, w_mo, w_mq, g_post, g_pre)]
    w_mo, w_mq = w_mo[0], w_mq[0]
    tiled = pl.BlockSpec((TM, D), lambda i: (i, 0))
    kv = pl.BlockSpec((NMEM, D), lambda i: (0, 0))
    vec = pl.BlockSpec((1, D), lambda i: (0, 0))
    return _pcall(
        body, name="mem_bwd", grid=(T // TM,), in_specs=in_specs,
        out_specs=[tiled, tiled, tiled, kv, kv, vec, vec],
        out_shape=[jax.ShapeDtypeStruct((T, D), F32), jax.ShapeDtypeStruct((T, D), BF16),
                   jax.ShapeDtypeStruct((T, D), BF16), jax.ShapeDtypeStruct((NMEM, D), F32),
                   jax.ShapeDtypeStruct((NMEM, D), F32), jax.ShapeDtypeStruct((1, D), F32),
                   jax.ShapeDtypeStruct((1, D), F32)],
        scratch_shapes=[pltpu.VMEM((TM, D), BF16)],
        compiler_params=pltpu.CompilerParams(dimension_semantics=("arbitrary",), vmem_limit_bytes=VMEM_BIG),
    )(dx2, ym, x1, qm, km, vm, w_mo, w_mq, g_post, g_pre)


def _memkv_bwd(dkm, dvm, mem, w_mk, w_mv):
    def body(dk_ref, dv_ref, m_ref, wk_ref, wv_ref, dg_ref):
        dmn = _nt(dk_ref[...].astype(BF16), _w(wk_ref)) + _nt(dv_ref[...].astype(BF16), _w(wv_ref))
        mv = m_ref[...]
        dg_ref[...] = jnp.sum(dmn * (mv * _rstd(mv)), axis=0, keepdims=True)

    return _one_call(body, "memkv_bwd", [dkm, dvm, mem, w_mk, w_mv], [((1, D), F32)], vmem=VMEM_BIG)[0]


def _postmix_bwd(dx1, z, o_f, o_c, w_out, g_post, g_fo, g_co):
    def body(dx1_ref, z_ref, of_ref, oc_ref, wo_ref, gp_ref, gfo_ref, gco_ref,
             dz_ref, dof_ref, doc_ref, dgp_ref, dgfo_ref, dgco_ref):
        _zero_at_start(dgp_ref, dgfo_ref, dgco_ref)
        dz, dgp = _rms_bwd(z_ref[...], gp_ref[...], dx1_ref[...])
        dgp_ref[...] += dgp
        dzb = dz.astype(BF16)
        dz_ref[...] = dzb
        dy = _nt(dzb, _w(wo_ref))
        dof, dgfo = _rms_bwd(of_ref[...], gfo_ref[...], dy[:, :512])
        doc, dgco = _rms_bwd(oc_ref[...], gco_ref[...], dy[:, 512:])
        dof_ref[...] = dof
        doc_ref[...] = doc
        dgfo_ref[...] += dgfo
        dgco_ref[...] += dgco

    return _tok_call(body, "postmix_bwd", [dx1, z, o_f, o_c], [w_out, g_post, g_fo, g_co],
                     [(D, BF16), (512, F32), (512, F32)], [(1, D), (1, 512), (1, 512)], vmem=VMEM_BIG)


def _premix_bwd(dx1, x, pieces, win_t, g_pre):
    def body(dx1_ref, x_ref, *refs):
        piece_refs, (w_ref, g_ref, dx_ref, dp_ref, dg_ref) = refs[:len(pieces)], refs[len(pieces):]
        _zero_at_start(dg_ref)
        col = 0
        for p in piece_refs:
            dp_ref[:, col:col + p.shape[1]] = p[...]
            col += p.shape[1]
        dh, dg = _rms_bwd(x_ref[...], g_ref[...], _nn(dp_ref[...], w_ref[...]))
        dg_ref[...] += dg
        dx_ref[...] = dx1_ref[...] + dh

    return _tok_call(body, "premix_bwd", [dx1, x] + list(pieces), [win_t, g_pre], [(D, F32), (PROJ, BF16)], [(1, D)],
                     vmem=VMEM_BIG)


def _wgrad(a, b, name):
    k, m = a.shape
    n = b.shape[1]
    tm = 640 if m % 640 == 0 and m > 1024 else min(m, 512)
    tn = min(n, 1024)

    def body(a_ref, b_ref, o_ref):
        o_ref[...] = _tn(a_ref[...].astype(BF16), b_ref[...].astype(BF16))

    return _pcall(
        body, name=name, grid=(m // tm, n // tn),
        in_specs=[pl.BlockSpec((k, tm), lambda i, j: (0, i)), pl.BlockSpec((k, tn), lambda i, j: (0, j))],
        out_specs=pl.BlockSpec((tm, tn), lambda i, j: (i, j)),
        out_shape=jax.ShapeDtypeStruct((m, n), F32),
        compiler_params=pltpu.CompilerParams(dimension_semantics=("arbitrary", "arbitrary"), vmem_limit_bytes=VMEM_BIG),
    )(a, b)


def _wgrad_group(name, pairs, rows):
    def body(*refs):
        o_ref = refs[-1]
        for k in range(len(pairs)):
            o_ref[k * rows:(k + 1) * rows, :] = _tn(refs[2 * k][...].astype(BF16), refs[2 * k + 1][...].astype(BF16))

    in_specs, ops = [], []
    for a, b in pairs:
        in_specs += [pl.BlockSpec((a.shape[0], rows), lambda j: (0, j)), _resident(b)]
        ops += [a, b]
    return _pcall(
        body, name=name, grid=(8,), in_specs=in_specs,
        out_specs=pl.BlockSpec((None, len(pairs) * rows, D), lambda j: (j, 0, 0)),
        out_shape=jax.ShapeDtypeStruct((8, len(pairs) * rows, D), F32),
        compiler_params=pltpu.CompilerParams(dimension_semantics=("arbitrary",), vmem_limit_bytes=VMEM_BIG),
    )(*ops)


def _adam_math(w, g, m, v):
    m2 = ADAM_B1 * m + (1.0 - ADAM_B1) * g
    v2 = ADAM_B2 * v + (1.0 - ADAM_B2) * jnp.square(g)
    m_hat = m2 / (1.0 - ADAM_B1 ** ADAM_STEP)
    v_hat = v2 / (1.0 - ADAM_B2 ** ADAM_STEP)
    delta = -ADAM_LR * (m_hat / (jnp.sqrt(v_hat) + ADAM_EPS) + ADAM_WD * w)
    return delta, m2, v2


def _adamw(w, g, m, v, name):
    rows, cols = w.shape
    tr = 256 if rows % 256 == 0 else rows

    def body(w_ref, g_ref, m_ref, v_ref, d_ref, m2_ref, v2_ref):
        d_ref[...], m2_ref[...], v2_ref[...] = _adam_math(w_ref[...], g_ref[...], m_ref[...], v_ref[...])

    spec = pl.BlockSpec((tr, cols), lambda i: (i, 0))
    return _pcall(
        body, name=name, grid=(rows // tr,), in_specs=[spec] * 4, out_specs=[spec] * 3,
        out_shape=[jax.ShapeDtypeStruct(w.shape, F32)] * 3,
        compiler_params=pltpu.CompilerParams(dimension_semantics=("arbitrary",)),
    )(w, g, m, v)


def _adamw_small(w, gparts, m, v):
    def body(w_ref, g_ref, m_ref, v_ref, gs_ref, d_ref, m2_ref, v2_ref):
        g = g_ref[0]
        for k in range(1, 8):
            g = g + g_ref[k]
        gs_ref[...] = g
        d_ref[...], m2_ref[...], v2_ref[...] = _adam_math(w_ref[...], g, m_ref[...], v_ref[...])

    return _one_call(body, "adamw_small", [w, gparts, m, v], [(w.shape, F32)] * 4)


def _row_tile(rows):
    return next(t for t in (512, 400, 320) if rows % t == 0)


def _add_halves(g4, theirs, core, name):
    rows = g4.shape[2]
    tr = _row_tile(rows)

    def body(c_ref, a_ref, b_ref, o_ref):
        o_ref[...] = (a_ref[...] + b_ref[...]).astype(BF16)

    grid_spec = pltpu.PrefetchScalarGridSpec(
        num_scalar_prefetch=1, grid=(4, rows // tr),
        in_specs=[pl.BlockSpec((None, None, tr, D), lambda j, i, c: (j, c[0], i, 0)),
                  pl.BlockSpec((None, None, tr, D), lambda j, i, c: (j, 0, i, 0))],
        out_specs=pl.BlockSpec((None, tr, D), lambda j, i, c: (j, i, 0)))
    return _pcall(
        body, name=name, grid_spec=grid_spec, out_shape=jax.ShapeDtypeStruct((4, rows, D), BF16),
        compiler_params=pltpu.CompilerParams(dimension_semantics=("arbitrary", "arbitrary")),
    )(core, g4, theirs)


def _sum_adam(own, got, order, r0, w, m, v, name, transposed=False):
    n = w.shape[1] if transposed else w.shape[0]
    tr = min(n, 256)

    def body(o_ref, a_ref, b_ref, c_ref, d_ref, w_ref, m_ref, v_ref, g_ref, dl_ref, m2_ref, v2_ref):
        f = lambda r: r[...].astype(F32)
        g = ((f(a_ref) + f(b_ref)) + f(c_ref)) + f(d_ref)
        g = g.T if transposed else g
        g_ref[...] = g
        dl_ref[...], m2_ref[...], v2_ref[...] = _adam_math(w_ref[...], g, m_ref[...], v_ref[...])

    slot = lambda k: pl.BlockSpec((None, tr, D), lambda i, o: (o[k], r0 // tr + i, 0))
    wspec = pl.BlockSpec((D, tr), lambda i, o: (0, i)) if transposed else pl.BlockSpec((tr, D), lambda i, o: (i, 0))
    grid_spec = pltpu.PrefetchScalarGridSpec(
        num_scalar_prefetch=1, grid=(n // tr,), in_specs=[slot(0), slot(1), slot(2), slot(3), wspec, wspec, wspec],
        out_specs=[wspec] * 4)
    return _pcall(
        body, name=name, grid_spec=grid_spec, out_shape=[jax.ShapeDtypeStruct(w.shape, F32)] * 4,
        compiler_params=pltpu.CompilerParams(dimension_semantics=("arbitrary",)),
    )(order, own, got, got, got, w, m, v)


def _sum_chips(own, got, order, name):
    rows = own.shape[1]
    tr = _row_tile(rows)

    def body(o_ref, a_ref, b_ref, c_ref, d_ref, out_ref):
        f = lambda r: r[...].astype(F32)
        out_ref[...] = ((f(a_ref) + f(b_ref)) + f(c_ref)) + f(d_ref)

    slot = lambda k: pl.BlockSpec((None, tr, D), lambda i, o: (o[k], i, 0))
    grid_spec = pltpu.PrefetchScalarGridSpec(
        num_scalar_prefetch=1, grid=(rows // tr,), in_specs=[slot(0), slot(1), slot(2), slot(3)],
        out_specs=pl.BlockSpec((tr, D), lambda i, o: (i, 0)))
    return _pcall(
        body, name=name, grid_spec=grid_spec, out_shape=jax.ShapeDtypeStruct((rows, D), F32),
        compiler_params=pltpu.CompilerParams(dimension_semantics=("arbitrary",)),
    )(order, own, got, got, got)


def _place():
    return lax.axis_index("x"), lax.axis_index("y"), lax.axis_index("c")


def _allgather(block, name):
    def body(x_ref, out_ref, token, send_sems, recv_sems, local_sem):
        token[...] = jnp.zeros_like(token)
        x, y, c = _place()
        me, sibling = (x, y, c), (x, y, 1 - c)
        chips = [(1 - x, y), (x, 1 - y), (1 - x, 1 - y)]

        def slot(px, py, pc):
            return out_ref.at[4 * px + 2 * py + pc]

        def copy(k, blk, to, src=None):
            return pltpu.make_async_remote_copy(
                src_ref=slot(*blk) if src is None else src, dst_ref=slot(*blk),
                send_sem=send_sems.at[k], recv_sem=recv_sems.at[k], device_id=to, device_id_type=MESH)

        mine = pltpu.make_async_copy(x_ref, slot(*me), local_sem)
        mine.start()
        first = [copy(0, me, sibling, src=x_ref)]
        first += [copy(1 + j, me, (*chip, c), src=x_ref) for j, chip in enumerate(chips)]
        for cp in first:
            cp.start()
        passed = [copy(4 + j, (*chip, c), sibling) for j, chip in enumerate(chips)]
        for j, chip in enumerate(chips):
            copy(1 + j, (*chip, c), me).wait_recv()
            passed[j].start()
        copy(0, sibling, me).wait_recv()
        for j, chip in enumerate(chips):
            copy(4 + j, (*chip, 1 - c), me).wait_recv()
        for cp in first + passed:
            cp.wait_send()
        mine.wait()

    return _pcall(
        body, name=name,
        out_shape=[jax.ShapeDtypeStruct((8,) + block.shape, block.dtype), jax.ShapeDtypeStruct((8, 128), F32)],
        in_specs=[pl.BlockSpec(memory_space=pl.ANY)],
        out_specs=[pl.BlockSpec(memory_space=pl.ANY), pl.BlockSpec(memory_space=pltpu.VMEM)],
        scratch_shapes=[pltpu.SemaphoreType.DMA((7,)), pltpu.SemaphoreType.DMA((7,)), pltpu.SemaphoreType.DMA(())],
        compiler_params=pltpu.CompilerParams(has_side_effects=True),
    )(block)


HBM_SPEC = pl.BlockSpec(memory_space=pltpu.HBM)
SEM_SPEC = pl.BlockSpec(memory_space=pltpu.SEMAPHORE)
ANY_SPEC = pl.BlockSpec(memory_space=pl.ANY)
EFFECT = pltpu.SideEffectType.DATAFLOW_SIDE_EFFECTING


def _in_hbm(a):
    return pltpu.with_memory_space_constraint(a, pltpu.HBM)


def _start_copies(name, src, land_shape, plan, n):
    def body(src_ref, land_ref, send_sems, recv_sems, src_thru, land_thru, token):
        for k, (s, d, to, _) in enumerate(plan(src_ref, land_ref)):
            pltpu.make_async_remote_copy(src_ref=s, dst_ref=d, send_sem=send_sems.at[k], recv_sem=recv_sems.at[k],
                                         device_id=to, device_id_type=MESH).start()
        token[...] = jnp.zeros_like(token)

    return _pcall(
        body, name=name,
        out_shape=(pltpu.SemaphoreType.DMA((n,)), pltpu.SemaphoreType.DMA((n,)), pltpu.HBM(src.shape, src.dtype),
                   pltpu.HBM(land_shape, src.dtype), jax.ShapeDtypeStruct((8, 128), F32)),
        in_specs=(HBM_SPEC, HBM_SPEC),
        out_specs=(SEM_SPEC, SEM_SPEC, HBM_SPEC, HBM_SPEC, pl.BlockSpec(memory_space=pltpu.VMEM)),
        input_output_aliases={0: 2, 1: 3}, compiler_params=pltpu.CompilerParams(has_side_effects=EFFECT),
    )(_in_hbm(src), _in_hbm(lax.empty(land_shape, src.dtype)))


def _wait_copies(name, started, after, plan):
    send_sems, recv_sems, src_thru, land_thru, _ = started

    def body(src_ref, land_ref, send_sems, recv_sems, *rest):
        for k, (s, _, to, mine) in enumerate(plan(src_ref, land_ref)):
            cp = pltpu.make_async_remote_copy(src_ref=s, dst_ref=mine, send_sem=send_sems.at[k],
                                              recv_sem=recv_sems.at[k], device_id=to, device_id_type=MESH)
            cp.wait_send()
            cp.wait_recv()

    return _pcall(
        body, name=name,
        out_shape=(pltpu.HBM(src_thru.shape, src_thru.dtype), pltpu.HBM(land_thru.shape, land_thru.dtype)),
        in_specs=(HBM_SPEC, HBM_SPEC, SEM_SPEC, SEM_SPEC) + (ANY_SPEC,) * len(after), out_specs=(HBM_SPEC, HBM_SPEC),
        input_output_aliases={0: 0, 1: 1}, compiler_params=pltpu.CompilerParams(has_side_effects=EFFECT),
    )(src_thru, land_thru, send_sems, recv_sems, *after)


def _gather_plan(src_ref, land_ref):
    x, y, c = _place()
    peers = [(x, y, 1 - c), (1 - x, y, c), (x, 1 - y, c), (1 - x, 1 - y, c)]
    return [(src_ref, land_ref.at[4 * x + 2 * y + c], p, land_ref.at[4 * p[0] + 2 * p[1] + p[2]]) for p in peers]


def _swap_plan(src_ref, land_ref):
    x, y, c = _place()
    return [(src_ref.at[:, pl.ds(1 - c, 1)], land_ref, (x, y, 1 - c), land_ref)]


def _exchange_plan(src_ref, land_ref):
    x, y, c = _place()
    chips = [(1 - x, y), (x, 1 - y), (1 - x, 1 - y)]
    return [(src_ref.at[2 * px + py], land_ref.at[2 * x + y], (px, py, c), land_ref.at[2 * px + py]) for px, py in chips]


def _gather_forward(land, block):
    def body(land_ref, out_ref, send_sems, recv_sems):
        x, y, c = _place()
        chips = [(1 - x, y), (x, 1 - y), (1 - x, 1 - y)]

        def copy(k, px, py, pc):
            blk = out_ref.at[4 * px + 2 * py + pc]
            return pltpu.make_async_remote_copy(src_ref=blk, dst_ref=blk, send_sem=send_sems.at[k],
                                                recv_sem=recv_sems.at[k], device_id=(x, y, 1 - c), device_id_type=MESH)

        sent = [copy(k, px, py, c) for k, (px, py) in enumerate(chips)]
        for cp in sent:
            cp.start()
        for k, (px, py) in enumerate(chips):
            copy(k, px, py, 1 - c).wait_recv()
        for cp in sent:
            cp.wait_send()

    land = _pcall(
        body, name="allgather_rest_forward", out_shape=jax.ShapeDtypeStruct(land.shape, land.dtype),
        in_specs=[ANY_SPEC], out_specs=ANY_SPEC, input_output_aliases={0: 0},
        scratch_shapes=[pltpu.SemaphoreType.DMA((3,)), pltpu.SemaphoreType.DMA((3,))],
        compiler_params=pltpu.CompilerParams(has_side_effects=True),
    )(land)

    rows = block.shape[0]
    tr = rows // 4

    def place(me_ref, x_ref, land_ref, out_ref):
        out_ref[...] = x_ref[...]

    x, y, c = _place()
    grid_spec = pltpu.PrefetchScalarGridSpec(
        num_scalar_prefetch=1, grid=(rows // tr,),
        in_specs=[pl.BlockSpec((tr, D), lambda i, me: (i, 0)), ANY_SPEC],
        out_specs=pl.BlockSpec((None, tr, D), lambda i, me: (me[0], i, 0)))
    return _pcall(
        place, name="allgather_rest_own", grid_spec=grid_spec, out_shape=jax.ShapeDtypeStruct(land.shape, land.dtype),
        input_output_aliases={2: 0}, compiler_params=pltpu.CompilerParams(dimension_semantics=("arbitrary",)),
    )((4 * x + 2 * y + c).reshape(1), block, land)


class _ReduceScatter:
    def __init__(self, name, g):
        self.name = name
        rows = g.shape[1]
        self.started = _start_copies(name + "_swap_start", g.reshape(4, 2, rows, D), (4, 1, rows, D), _swap_plan, 1)
        self.token = self.started[4][0, 0]

    def halfway(self, after):
        g4, theirs = _wait_copies(self.name + "_swap_wait", self.started, after, _swap_plan)
        self.own = _add_halves(g4, theirs, lax.axis_index("c").reshape(1), self.name + "_add_halves")
        self.started = _start_copies(self.name + "_exch_start", self.own, self.own.shape, _exchange_plan, 3)
        self.token = self.started[4][0, 0]

    def finish(self, after):
        own, got = _wait_copies(self.name + "_exch_wait", self.started, after, _exchange_plan)
        chip = 2 * lax.axis_index("x") + lax.axis_index("y")
        return own, got, (chip + jnp.arange(4, dtype=jnp.int32)) % 4


def _pack_small(p, scalar=None):
    z = lambda a, n: jnp.pad(a, ((0, 0), (0, n - a.shape[1])))
    rows = [z(p['rel_bias'], D), z(p['b_fgt'], D), jnp.concatenate([p['g_fox_out'], p['g_chk_out']], axis=1)]
    rows += [p[n] for n in ('g_mix_pre', 'g_mix_post', 'g_mem_kv', 'g_mem_pre', 'g_mem_post', 'g_ff_pre', 'g_ff_post')]
    rows.append(jnp.zeros((1, D), F32) if scalar is None else z(jnp.reshape(scalar, (1, 1)), D))
    rows.append(jnp.zeros((SMALL_ROWS - 18, D), F32))
    return jnp.concatenate(rows, axis=0)


def _unpack_small(a):
    out = {'rel_bias': a[0:8, :257], 'b_fgt': a[8:9, :8], 'g_fox_out': a[9:10, :512], 'g_chk_out': a[9:10, 512:]}
    for k, n in enumerate(('g_mix_pre', 'g_mix_post', 'g_mem_kv', 'g_mem_pre', 'g_mem_post', 'g_ff_pre', 'g_ff_post')):
        out[n] = a[10 + k:11 + k]
    return out


_GAP_DEV, _GAP_ROW = divmod(GATE0 + 8, N_IN)
_GAP = CHK0 - GATE0 - 8


def _in_rows_to_proj(g):
    wt = g[:, :N_IN].reshape(8 * N_IN, D)
    return jnp.concatenate([wt[:GATE0], jnp.pad(wt[GATE0:GATE0 + 8], ((0, _GAP), (0, 0))), wt[GATE0 + 8:]], axis=0)


def _proj_rows_to_in(g):
    pad = lambda a: jnp.pad(a, ((0, R_IN - a.shape[0]), (0, 0)))
    lo = N_IN * _GAP_DEV
    shards = [pad(g[N_IN * j:N_IN * (j + 1)]) for j in range(_GAP_DEV)]
    shards.append(pad(jnp.concatenate([g[lo:lo + _GAP_ROW], g[lo + _GAP_ROW + _GAP:lo + N_IN + _GAP]], axis=0)))
    shards += [pad(g[N_IN * j + _GAP:N_IN * (j + 1) + _GAP]) for j in range(_GAP_DEV + 1, 8)]
    return jnp.stack(shards)


def _local_grads(x, mem, tgt, win_t, gw_of, sm, on_grads):
    b_pad = jnp.pad(sm['b_fgt'], ((0, 0), (0, 120)))
    tbl = jnp.pad(sm['rel_bias'], ((0, 0), (0, NREL_PAD - 257)))

    h1, proj, flog = _premix_fwd(x, sm['g_mix_pre'], win_t)
    c = _gate_fwd(flog, b_pad)
    c8 = c[:, :8]
    c2 = jnp.repeat(c8, 128, axis=1)
    ct3 = c8.T.reshape(4, 2, T)
    o_f, lse_f = _fox_fwd(proj, c2, ct3)
    vt3 = _relvec_fwd(tbl).reshape(4, 2, VW)
    kvp = jnp.pad(proj[:, CHK0 + 512:], ((LEFT, 0), (0, 0)))
    o_c, lse_c = _chk_fwd(proj, kvp, vt3)
    gw = gw_of([o_f, o_c])
    w_out, w_mq, w_mk, w_mv, w_mo, w1_t, w2 = (_wblk(gw, n) for n in ('w_out', 'w_mq', 'w_mk', 'w_mv', 'w_mo', 'w_ff1', 'w_ff2'))
    ycat, z, x1, h2, qm = _postmix_fwd(x, o_f, o_c, sm['g_fox_out'], sm['g_chk_out'], w_out,
                                       sm['g_mix_post'], sm['g_mem_pre'], w_mq)
    memn, km, vm = _memkv_fwd(mem, sm['g_mem_kv'], w_mk, w_mv)
    om, ym, x2, h3 = _mem_fwd(qm, x1, km, vm, w_mo, sm['g_mem_post'], sm['g_ff_pre'])
    a, y3, dx3, loss_acc = _ffn_fwd(h3, x2, tgt, w1_t, w2, sm['g_ff_post'])

    gs = {}
    dx2, da, dy3, r, gs['g_ff_post'], gs['g_ff_pre'] = _ffn_bwd(dx3, y3, x2, a, w1_t, w2, sm['g_ff_post'], sm['g_ff_pre'])
    zero = on_grads('A', _wgrad_group("wgrad_ff", [(da, h3), (r, dy3)], 512), None)
    dx1, dym, dqm, dkm, dvm, gs['g_mem_post'], gs['g_mem_pre'] = _mem_bwd(
        dx2, ym, x1, qm, km, vm, w_mo, w_mq, sm['g_mem_post'] + zero, sm['g_mem_pre'])
    zero = on_grads('A halfway', None, [dx1])
    gs['g_mem_kv'] = _memkv_bwd(dkm, dvm, mem, w_mk, w_mv)
    dz, dof, doc, gs['g_mix_post'], gs['g_fox_out'], gs['g_chk_out'] = _postmix_bwd(
        dx1, z, o_f, o_c, w_out, sm['g_mix_post'] + zero, sm['g_fox_out'], sm['g_chk_out'])
    zero = on_grads('B', _wgrad_group("wgrad_mem_out", [(ycat, dz), (h2, dqm), (memn, dkm), (memn, dvm), (om, dym)], 128), None)
    dq_f, dk_f, dv_f, dct, dcq = _fox_bwd(proj, c2, ct3 + zero, o_f, lse_f, dof)
    zero = on_grads('B halfway', None, [dq_f])
    dq_c, dk_c, dv_c, gv = _chk_bwd(proj, kvp, vt3 + zero, o_c, lse_c, doc)
    gs['rel_bias'] = _relvec_bwd(gv.reshape(8, VW))[:, :257]
    dc = jnp.pad(dct.reshape(8, T).T + dcq[:, ::128], ((0, 0), (0, 120)))
    dflog, db = _gate_bwd(dc, flog, b_pad)
    gs['b_fgt'] = db[0:1, :8]
    grad_x, dproj, gs['g_mix_pre'] = _premix_bwd(dx1, x, [dq_f, dk_f, dv_f, dflog, dq_c, dk_c, dv_c], win_t, sm['g_mix_pre'])
    on_grads('C', _proj_rows_to_in(_wgrad(dproj, h1, "wgrad_in")), None)
    return loss_acc[0, 0], grad_x, gs


def kernel(x, mem, w_in, b_fgt, rel_bias, g_fox_out, g_chk_out, w_out, g_mix_pre, g_mix_post, g_mem_kv, w_mq, w_mk, w_mv, w_mo, g_mem_pre, g_mem_post, w_ff1, w_ff2, g_ff_pre, g_ff_post, loss_target, m_w_in, m_b_fgt, m_rel_bias, m_g_fox_out, m_g_chk_out, m_w_out, m_g_mix_pre, m_g_mix_post, m_g_mem_kv, m_w_mq, m_w_mk, m_w_mv, m_w_mo, m_g_mem_pre, m_g_mem_post, m_w_ff1, m_w_ff2, m_g_ff_pre, m_g_ff_post, v_w_in, v_b_fgt, v_rel_bias, v_g_fox_out, v_g_chk_out, v_w_out, v_g_mix_pre, v_g_mix_post, v_g_mem_kv, v_w_mq, v_w_mk, v_w_mv, v_w_mo, v_g_mem_pre, v_g_mem_post, v_w_ff1, v_w_ff2, v_g_ff_pre, v_g_ff_post):
    args = dict(locals())
    two_d = lambda a: a.reshape(a.shape[-2:])
    w = {n: two_d(args[n]) for n in WEIGHTS}
    m = {n: two_d(args['m_' + n]) for n in WEIGHTS}
    v = {n: two_d(args['v_' + n]) for n in WEIGHTS}

    sm = {n: w[n] for n in SMALL}
    shard_in = jnp.pad(w['w_in'].T, ((0, R_IN - N_IN), (0, 0))).astype(BF16)
    gathered_in, zero = _allgather(shard_in, "allgather_w_in")
    win_t = _in_rows_to_proj(gathered_in)
    shard_rest = (jnp.concatenate([w['w_ff1'].T, w['w_ff2'], w['w_out'], w['w_mq'], w['w_mk'], w['w_mv'], w['w_mo']],
                                  axis=0) + zero[0, 0]).astype(BF16)
    rest = _start_copies("allgather_rest_start", shard_rest, (8, R_REST, D), _gather_plan, 4)
    sm['g_mix_pre'] = sm['g_mix_pre'] + rest[4][0, 0]

    def gw_of(after):
        block, land = _wait_copies("allgather_rest_wait", rest, after, _gather_plan)
        return _gather_forward(land, block)

    rs = {}

    def on_grads(stage, g, after):
        if stage.endswith('halfway'):
            rs[stage[0]].halfway(after)
            return rs[stage[0]].token
        rs[stage] = _ReduceScatter("rs_" + stage.lower(), g)
        if stage == 'C':
            rs[stage].halfway([rs[stage].started[4]])
        return rs[stage].token

    loss_local, grad_x, gs = _local_grads(x[0], mem[0], loss_target[0], win_t, gw_of, sm, on_grads)
    grads, deltas, new_m, new_v = {}, {}, {}, {}

    def update(n, out):
        grads[n], deltas[n], new_m[n], new_v[n] = out

    own, got, order = rs['A'].finish([grad_x])
    update('w_ff1', _sum_adam(own, got, order, 0, w['w_ff1'], m['w_ff1'], v['w_ff1'], "adamw_w_ff1", transposed=True))
    update('w_ff2', _sum_adam(own, got, order, 512, w['w_ff2'], m['w_ff2'], v['w_ff2'], "adamw_w_ff2"))
    own, got, order = rs['B'].finish([grad_x])
    for k, n in enumerate(('w_out', 'w_mq', 'w_mk', 'w_mv', 'w_mo')):
        update(n, _sum_adam(own, got, order, 128 * k, w[n], m[n], v[n], "adamw_" + n))

    gparts, _ = _allgather(_pack_small(gs, loss_local), "allgather_small_grads")
    gsum, d_s, m_s, v_s = _adamw_small(_pack_small(sm), gparts, _pack_small({n: m[n] for n in SMALL}),
                                       _pack_small({n: v[n] for n in SMALL}))
    for dst, packed in ((grads, gsum), (deltas, d_s), (new_m, m_s), (new_v, v_s)):
        dst.update(_unpack_small(packed))
    loss = gsum[17, 0]

    g_in = _sum_chips(*rs['C'].finish([gsum]), "rs_c_sum_chips")[:N_IN].T
    update('w_in', (g_in,) + tuple(_adamw(w['w_in'], g_in, m['w_in'], v['w_in'], "adamw_w_in")))

    out = [loss, grad_x[None]]
    for group in (grads, deltas, new_m, new_v):
        out += [group[n].reshape(args[n].shape) for n in WEIGHTS]
    return tuple(out)
```

```python
import functools

import jax
import jax.numpy as jnp
from jax import lax
from jax.experimental import pallas as pl
from jax.experimental.pallas import tpu as pltpu

F32 = jnp.float32
BF16 = jnp.bfloat16
MESH = pl.DeviceIdType.MESH

T = 2048
D = 1024
NMEM = 256
DFF = 4096
EPS = 1e-6
TM = 256
TQ = 256
FQ = 512
HD = 64
SCALE = HD ** -0.5
MEM_HEADS = 4
MEM_HD = 256
MEM_SCALE = MEM_HD ** -0.5
NEG = -1e30
LEFT = 512
WIN = LEFT + TQ
VW = 1024
NREL_PAD = 384
PROJ = 3200
GATE0 = 1536
CHK0 = 1664
VMEM_BIG = 56 * 1024 * 1024

ADAM_LR = 0.001
ADAM_B1 = 0.9
ADAM_B2 = 0.999
ADAM_EPS = 1e-08
ADAM_WD = 0.01
ADAM_STEP = 10

N_IN = 385
R_IN = 400
R_REST = 1664
W_ROWS = {'w_ff1': (0, 512), 'w_ff2': (512, 512),
          'w_out': (1024, 128), 'w_mq': (1152, 128), 'w_mk': (1280, 128), 'w_mv': (1408, 128), 'w_mo': (1536, 128)}
R_A, R_B = 1024, 640
SMALL_ROWS = 24

WEIGHTS = ['w_in', 'b_fgt', 'rel_bias', 'g_fox_out', 'g_chk_out', 'w_out', 'g_mix_pre', 'g_mix_post', 'g_mem_kv',
           'w_mq', 'w_mk', 'w_mv', 'w_mo', 'g_mem_pre', 'g_mem_post', 'w_ff1', 'w_ff2', 'g_ff_pre', 'g_ff_post']
BIG = ['w_in', 'w_out', 'w_mq', 'w_mk', 'w_mv', 'w_mo', 'w_ff1', 'w_ff2']
SMALL = [n for n in WEIGHTS if n not in BIG]


def _pcall(body, **kw):
    return pl.pallas_call(body, **kw)


def _nn(a, b):
    return jnp.dot(a, b, preferred_element_type=F32)


def _nt(a, b):
    return lax.dot_general(a, b, (((1,), (1,)), ((), ())), preferred_element_type=F32)


def _tn(a, b):
    return lax.dot_general(a, b, (((0,), (0,)), ((), ())), preferred_element_type=F32)


def _w(ref):
    v = ref[...]
    return v if v.ndim == 2 else v.reshape(-1, v.shape[-1])


def _rstd(x):
    return lax.rsqrt(jnp.mean(x * x, axis=-1, keepdims=True) + EPS)


def _rms(x, g):
    return x * _rstd(x) * g


def _rms_bwd(x, g, dy):
    r = _rstd(x)
    xh = x * r
    dg = jnp.sum(dy * xh, axis=0, keepdims=True)
    dxh = dy * g
    dx = r * (dxh - xh * jnp.mean(dxh * xh, axis=-1, keepdims=True))
    return dx, dg


def _resident(a):
    if isinstance(a, tuple):
        _, shape, index = a
        return pl.BlockSpec(shape, lambda *_: index, pipeline_mode=pl.Buffered(1))
    return pl.BlockSpec(a.shape, lambda *_, nd=a.ndim: (0,) * nd, pipeline_mode=pl.Buffered(1))


def _wblk(gw, name):
    r0, rows = W_ROWS[name]
    return (gw, (8, rows, D), (0, r0 // rows, 0))


def _tok_call(body, name, tiled, full, outs_tiled, outs_acc=(), rows=T, tm=TM, vmem=None):
    in_specs = [pl.BlockSpec((tm, a.shape[1]), lambda i: (i, 0)) for a in tiled]
    in_specs += [_resident(a) for a in full]
    full = [a[0] if isinstance(a, tuple) else a for a in full]
    out_shape = [jax.ShapeDtypeStruct((rows, c), dt) for c, dt in outs_tiled]
    out_shape += [jax.ShapeDtypeStruct(s, F32) for s in outs_acc]
    out_specs = [pl.BlockSpec((tm, c), lambda i: (i, 0)) for c, _ in outs_tiled]
    out_specs += [pl.BlockSpec(s, lambda i, nd=len(s): (0,) * nd) for s in outs_acc]
    return _pcall(
        body, name=name, grid=(rows // tm,), in_specs=in_specs, out_specs=out_specs, out_shape=out_shape,
        compiler_params=pltpu.CompilerParams(dimension_semantics=("arbitrary",), vmem_limit_bytes=vmem),
    )(*tiled, *full)


def _one_call(body, name, ins, outs, vmem=None):
    whole = lambda s: pl.BlockSpec(s, lambda i, nd=len(s): (0,) * nd)
    return _pcall(
        body, name=name, grid=(1,), in_specs=[_resident(a) for a in ins], out_specs=[whole(s) for s, _ in outs],
        out_shape=[jax.ShapeDtypeStruct(s, dt) for s, dt in outs],
        compiler_params=pltpu.CompilerParams(dimension_semantics=("arbitrary",), vmem_limit_bytes=vmem),
    )(*[a[0] if isinstance(a, tuple) else a for a in ins])


def _premix_fwd(x, g_pre, win_t):
    def body(x_ref, g_ref, w_ref, h_ref, proj_ref, flog_ref):
        h = _rms(x_ref[...], g_ref[...]).astype(BF16)
        h_ref[...] = h
        p = _nt(h, w_ref[...])
        proj_ref[...] = p.astype(BF16)
        flog_ref[...] = p[:, GATE0:GATE0 + 128]

    return _tok_call(body, "premix_fwd", [x], [g_pre, win_t],
                     [(D, BF16), (PROJ, BF16), (128, F32)], vmem=VMEM_BIG)


def _postmix_fwd(x, o_f, o_c, g_fo, g_co, w_out, g_post, g_mpre, w_mq):
    def body(x_ref, of_ref, oc_ref, gfo_ref, gco_ref, wo_ref, gp_ref, gm_ref, wq_ref,
             y_ref, z_ref, x1_ref, h2_ref, qm_ref):
        y_ref[:, :512] = _rms(of_ref[...], gfo_ref[...]).astype(BF16)
        y_ref[:, 512:] = _rms(oc_ref[...], gco_ref[...]).astype(BF16)
        z = _nn(y_ref[...], _w(wo_ref))
        z_ref[...] = z
        x1 = x_ref[...] + _rms(z, gp_ref[...])
        x1_ref[...] = x1
        h2 = _rms(x1, gm_ref[...]).astype(BF16)
        h2_ref[...] = h2
        qm_ref[...] = _nn(h2, _w(wq_ref)).astype(BF16)

    return _tok_call(body, "postmix_fwd", [x, o_f, o_c], [g_fo, g_co, w_out, g_post, g_mpre, w_mq],
                     [(D, BF16), (D, F32), (D, F32), (D, BF16), (D, BF16)], vmem=VMEM_BIG)


def _memkv_fwd(mem, g_kv, w_mk, w_mv):
    def body(m_ref, g_ref, wk_ref, wv_ref, mn_ref, k_ref, v_ref):
        mn = _rms(m_ref[...], g_ref[...]).astype(BF16)
        mn_ref[...] = mn
        k_ref[...] = _nn(mn, _w(wk_ref)).astype(BF16)
        v_ref[...] = _nn(mn, _w(wv_ref)).astype(BF16)

    return _tok_call(body, "memkv_fwd", [mem], [g_kv, w_mk, w_mv],
                     [(D, BF16), (D, BF16), (D, BF16)], rows=NMEM, tm=NMEM, vmem=VMEM_BIG)


def _mem_fwd(qm, x1, km, vm, w_mo, g_post, g_fpre):
    def body(q_ref, x1_ref, k_ref, v_ref, wo_ref, gp_ref, gf_ref, om_ref, ym_ref, x2_ref, h3_ref):
        for h in range(MEM_HEADS):
            sl = slice(h * MEM_HD, (h + 1) * MEM_HD)
            s = _nt(q_ref[:, sl], k_ref[:, sl]) * MEM_SCALE
            p = jnp.exp(s - jnp.max(s, axis=-1, keepdims=True))
            p = p / jnp.sum(p, axis=-1, keepdims=True)
            om_ref[:, sl] = _nn(p.astype(BF16), v_ref[:, sl]).astype(BF16)
        ym = _nn(om_ref[...], _w(wo_ref))
        ym_ref[...] = ym
        x2 = x1_ref[...] + _rms(ym, gp_ref[...])
        x2_ref[...] = x2
        h3_ref[...] = _rms(x2, gf_ref[...]).astype(BF16)

    return _tok_call(body, "mem_fwd", [qm, x1], [km, vm, w_mo, g_post, g_fpre],
                     [(D, BF16), (D, F32), (D, F32), (D, BF16)], vmem=VMEM_BIG)


def _ffn_fwd(h3, x2, tgt, w1_t, w2, g_post):
    def body(h_ref, x2_ref, t_ref, w1_ref, w2_ref, g_ref, a_ref, y_ref, dx_ref, loss_ref):
        @pl.when(pl.program_id(0) == 0)
        def _():
            loss_ref[...] = jnp.zeros_like(loss_ref)

        a = _nt(h_ref[...], _w(w1_ref))
        a_ref[...] = a.astype(BF16)
        r = jnp.square(jnp.maximum(a, 0.0)).astype(BF16)
        y = _nn(r, _w(w2_ref))
        y_ref[...] = y
        e = x2_ref[...] + _rms(y, g_ref[...]) - t_ref[...]
        dx_ref[...] = e * (1.0 / D)
        loss_ref[...] += 0.5 * jnp.sum(jnp.sum(e * e, axis=-1, keepdims=True) * (1.0 / D))

    return _tok_call(body, "ffn_fwd", [h3, x2, tgt], [w1_t, w2, g_post],
                     [(DFF, BF16), (D, F32), (D, F32)], [(8, 128)], vmem=VMEM_BIG)


def _tri(lower):
    r = lax.broadcasted_iota(jnp.int32, (128, 128), 0)
    c = lax.broadcasted_iota(jnp.int32, (128, 128), 1)
    return jnp.where(r >= c if lower else c >= r, 1.0, 0.0).astype(F32)


def _hdot(a, b):
    return jnp.dot(a, b, preferred_element_type=F32, precision=lax.Precision.HIGHEST)


def _gate_fwd(flog, b_pad):
    def body(f_ref, b_ref, c_ref):
        tri = _tri(True)

        def step(i, carry):
            rows = pl.ds(pl.multiple_of(i * 128, 128), 128)
            z = f_ref[rows, :] + b_ref[...]
            lf = jnp.minimum(z, 0.0) - jnp.log(1.0 + jnp.exp(-jnp.abs(z)))
            cb = _hdot(tri, lf) + carry
            c_ref[rows, :] = cb
            return cb[127:128, :]

        lax.fori_loop(0, T // 128, step, jnp.zeros((1, 128), F32))

    return _one_call(body, "gate_fwd", [flog, b_pad], [((T, 128), F32)])[0]


def _gate_bwd(dc, flog, b_pad):
    def body(dc_ref, f_ref, b_ref, df_ref, db_ref):
        tri = _tri(False)

        def step(j, carry):
            run, db = carry
            i = T // 128 - 1 - j
            rows = pl.ds(pl.multiple_of(i * 128, 128), 128)
            dcb = dc_ref[rows, :]
            rb = _hdot(tri, dcb) + run
            z = f_ref[rows, :] + b_ref[...]
            df = rb * (1.0 / (1.0 + jnp.exp(z)))
            df_ref[rows, :] = df.astype(BF16)
            return run + jnp.sum(dcb, axis=0, keepdims=True), db + jnp.sum(df, axis=0, keepdims=True)

        _, db = lax.fori_loop(0, T // 128, step, (jnp.zeros((1, 128), F32), jnp.zeros((1, 128), F32)))
        db_ref[...] = jnp.broadcast_to(db, (8, 128))

    return _one_call(body, "gate_bwd", [dc, flog, b_pad], [((T, 128), BF16), ((8, 128), F32)])


def _lane_lo(rows=TQ):
    return lax.broadcasted_iota(jnp.int32, (rows, 128), 1) < HD


def _half(v, lo, a, scale=None):
    keep = lo if a == 0 else jnp.logical_not(lo)
    v = v.astype(F32) if scale is None else v.astype(F32) * scale
    return jnp.where(keep, v, 0.0).astype(BF16)


def _fox_specs():
    return [pl.BlockSpec((FQ, 128), lambda h, i: (i, h)),
            pl.BlockSpec((T, 128), lambda h, i: (0, 4 + h)),
            pl.BlockSpec((T, 128), lambda h, i: (0, 8 + h))]


def _fox_fwd(proj, c2, ct3):
    def body(q_ref, k_ref, v_ref, c_ref, ct_ref, o_ref, l_ref):
        i = pl.program_id(1)
        lo = _lane_lo(FQ)
        causal = lax.broadcasted_iota(jnp.int32, (FQ, FQ), 1) <= lax.broadcasted_iota(jnp.int32, (FQ, FQ), 0)
        q = q_ref[...]
        qs = [_half(q, lo, a, SCALE) for a in range(2)]
        cqs = [c_ref[:, 128 * a:128 * a + 1] for a in range(2)]

        def tile(off, carry, diagonal):
            kblk = k_ref[pl.ds(off, FQ), :]
            vblk = v_ref[pl.ds(off, FQ), :]
            new = []
            for a in range(2):
                m, l, acc = carry[a]
                s = _nt(qs[a], kblk) + (cqs[a] - ct_ref[a:a + 1, pl.ds(off, FQ)])
                if diagonal:
                    s = jnp.where(causal, s, NEG)
                m2 = jnp.maximum(m, jnp.max(s, axis=-1, keepdims=True))
                p = jnp.exp(s - m2)
                alpha = jnp.exp(m - m2)
                new.append((m2, alpha * l + jnp.sum(p, axis=-1, keepdims=True),
                            alpha * acc + _nn(p.astype(BF16), vblk)))
            return tuple(new)

        init = (jnp.full((FQ, 1), NEG, F32), jnp.zeros((FQ, 1), F32), jnp.zeros((FQ, 128), F32))
        carry = lax.fori_loop(0, i, lambda kb, c: tile(pl.multiple_of(kb * FQ, FQ), c, False), (init, init))
        carry = tile(pl.multiple_of(i * FQ, FQ), carry, True)
        outs = []
        for a in range(2):
            m, l, acc = carry[a]
            outs.append(acc / l)
            l_ref[:, 128 * a:128 * a + 128] = jnp.broadcast_to(m + jnp.log(l), (FQ, 128))
        o_ref[...] = jnp.where(lo, outs[0], outs[1])

    return _pcall(
        body, name="fox_fwd", grid=(4, T // FQ),
        in_specs=_fox_specs() + [pl.BlockSpec((FQ, 256), lambda h, i: (i, h)),
                                 pl.BlockSpec((None, 2, T), lambda h, i: (h, 0, 0))],
        out_specs=[pl.BlockSpec((FQ, 128), lambda h, i: (i, h)), pl.BlockSpec((FQ, 256), lambda h, i: (i, h))],
        out_shape=[jax.ShapeDtypeStruct((T, 512), F32), jax.ShapeDtypeStruct((T, 1024), F32)],
        compiler_params=pltpu.CompilerParams(dimension_semantics=("arbitrary", "arbitrary"), vmem_limit_bytes=VMEM_BIG),
    )(proj, proj, proj, c2, ct3)


def _fox_bwd(proj, c2, ct3, o, lse, do):
    def body(q_ref, k_ref, v_ref, c_ref, ct_ref, o_ref, l_ref, do_ref, dq_ref, dkb_ref, dvb_ref, dct_ref, dcq_ref,
             dk_ref, dv_ref):
        i = pl.program_id(1)

        @pl.when(i == 0)
        def _():
            dk_ref[...] = jnp.zeros_like(dk_ref)
            dv_ref[...] = jnp.zeros_like(dv_ref)
            dct_ref[...] = jnp.zeros_like(dct_ref)

        lo = _lane_lo(FQ)
        causal = lax.broadcasted_iota(jnp.int32, (FQ, FQ), 1) <= lax.broadcasted_iota(jnp.int32, (FQ, FQ), 0)
        q = q_ref[...]
        do_v = do_ref[...]
        prod = do_v * o_ref[...]
        qs = [_half(q, lo, a, SCALE) for a in range(2)]
        dos = [_half(do_v, lo, a) for a in range(2)]
        deltas = [jnp.sum(jnp.where(lo if a == 0 else jnp.logical_not(lo), prod, 0.0), axis=-1, keepdims=True)
                  for a in range(2)]
        cqs = [c_ref[:, 128 * a:128 * a + 1] for a in range(2)]
        las = [l_ref[:, 128 * a:128 * a + 1] for a in range(2)]

        def tile(off, carry, diagonal):
            kblk = k_ref[pl.ds(off, FQ), :]
            vblk = v_ref[pl.ds(off, FQ), :]
            new = []
            dk = jnp.zeros((FQ, 128), F32)
            dv = jnp.zeros((FQ, 128), F32)
            for a in range(2):
                dq_acc, rs = carry[a]
                s = _nt(qs[a], kblk) + (cqs[a] - ct_ref[a:a + 1, pl.ds(off, FQ)])
                if diagonal:
                    s = jnp.where(causal, s, NEG)
                p = jnp.exp(s - las[a])
                ds = p * (_nt(dos[a], vblk) - deltas[a])
                dsb = ds.astype(BF16)
                dk = dk + _tn(dsb, qs[a])
                dv = dv + _tn(p.astype(BF16), dos[a])
                dct_ref[a:a + 1, pl.ds(off, FQ)] -= jnp.sum(ds, axis=0, keepdims=True)
                new.append((dq_acc + _nn(dsb, kblk), rs + jnp.sum(ds, axis=-1, keepdims=True)))
            dk_ref[pl.ds(off, FQ), :] += dk
            dv_ref[pl.ds(off, FQ), :] += dv
            return tuple(new)

        init = (jnp.zeros((FQ, 128), F32), jnp.zeros((FQ, 1), F32))
        carry = lax.fori_loop(0, i, lambda kb, c: tile(pl.multiple_of(kb * FQ, FQ), c, False), (init, init))
        carry = tile(pl.multiple_of(i * FQ, FQ), carry, True)
        for a in range(2):
            dcq_ref[:, 128 * a:128 * a + 128] = jnp.broadcast_to(carry[a][1], (FQ, 128))
        dq_ref[...] = (jnp.where(lo, carry[0][0], carry[1][0]) * SCALE).astype(BF16)

        @pl.when(i == T // FQ - 1)
        def _():
            dkb_ref[...] = dk_ref[...].astype(BF16)
            dvb_ref[...] = dv_ref[...].astype(BF16)

    blk = pl.BlockSpec((FQ, 128), lambda h, i: (i, h))
    wide = pl.BlockSpec((FQ, 256), lambda h, i: (i, h))
    rows = pl.BlockSpec((None, 2, T), lambda h, i: (h, 0, 0))
    col = pl.BlockSpec((T, 128), lambda h, i: (0, h))
    return _pcall(
        body, name="fox_bwd", grid=(4, T // FQ),
        in_specs=_fox_specs() + [wide, rows, blk, wide, blk],
        out_specs=[blk, col, col, rows, wide],
        out_shape=[jax.ShapeDtypeStruct((T, 512), BF16), jax.ShapeDtypeStruct((T, 512), BF16),
                   jax.ShapeDtypeStruct((T, 512), BF16), jax.ShapeDtypeStruct((4, 2, T), F32),
                   jax.ShapeDtypeStruct((T, 1024), F32)],
        scratch_shapes=[pltpu.VMEM((T, 128), F32), pltpu.VMEM((T, 128), F32)],
        compiler_params=pltpu.CompilerParams(dimension_semantics=("arbitrary", "arbitrary"), vmem_limit_bytes=VMEM_BIG),
    )(proj, proj, proj, c2, ct3, o, lse, do)


def _rel_onehot():
    ridx = lax.broadcasted_iota(jnp.int32, (NREL_PAD, VW), 0)
    j = lax.broadcasted_iota(jnp.int32, (NREL_PAD, VW), 1)
    return jnp.where(ridx == jnp.clip(TQ + LEFT - 1 - j, -128, 128) + 128, 1.0, 0.0).astype(F32)


def _relvec_fwd(tbl):
    def body(t_ref, v_ref):
        v_ref[...] = _hdot(t_ref[...], _rel_onehot())

    return _one_call(body, "relvec_fwd", [tbl], [((8, VW), F32)])[0]


def _relvec_bwd(gv):
    def body(g_ref, t_ref):
        t_ref[...] = lax.dot_general(g_ref[...], _rel_onehot(), (((1,), (1,)), ((), ())),
                                     preferred_element_type=F32, precision=lax.Precision.HIGHEST)

    return _one_call(body, "relvec_bwd", [gv], [((8, NREL_PAD), F32)])[0]


def _chk_bias(vt_ref, a):
    vb = jnp.broadcast_to(vt_ref[a:a + 1, :], (TQ, VW))
    y = pltpu.roll(vb, VW - (TQ - 1), 1, stride=1, stride_axis=0)[:, :WIN]
    cr = lax.broadcasted_iota(jnp.int32, (TQ, WIN), 0) // 64
    cm = lax.broadcasted_iota(jnp.int32, (TQ, WIN), 1) // 64
    return jnp.where((cm >= cr) & (cm <= cr + 8), y, NEG)


def _chk_specs():
    return [pl.BlockSpec((TQ, 128), lambda h, i: (i, CHK0 // 128 + h)),
            pl.BlockSpec((T + LEFT, 128), lambda h, i: (0, h)),
            pl.BlockSpec((T + LEFT, 128), lambda h, i: (0, 4 + h)),
            pl.BlockSpec((None, 2, VW), lambda h, i: (h, 0, 0))]


def _chk_fwd(proj, kvp, vt3):
    def body(q_ref, k_ref, v_ref, vt_ref, o_ref, l_ref, bias_ref):
        i = pl.program_id(1)

        @pl.when(i == 0)
        def _():
            for a in range(2):
                bias_ref[a] = _chk_bias(vt_ref, a)

        lo = _lane_lo()
        off = pl.multiple_of(i * TQ, TQ)
        kw = k_ref[pl.ds(off, WIN), :]
        vw = v_ref[pl.ds(off, WIN), :]
        real = lax.broadcasted_iota(jnp.int32, (TQ, WIN), 1) + off >= LEFT
        q = q_ref[...]
        outs = []
        for a in range(2):
            s = jnp.where(real, _nt(_half(q, lo, a), kw) * SCALE + bias_ref[a], NEG)
            m = jnp.max(s, axis=-1, keepdims=True)
            p = jnp.exp(s - m)
            l = jnp.sum(p, axis=-1, keepdims=True)
            outs.append(_nn(p.astype(BF16), vw) / l)
            l_ref[:, 128 * a:128 * a + 128] = jnp.broadcast_to(m + jnp.log(l), (TQ, 128))
        o_ref[...] = jnp.where(lo, outs[0], outs[1])

    return _pcall(
        body, name="chk_fwd", grid=(4, T // TQ), in_specs=_chk_specs(),
        out_specs=[pl.BlockSpec((TQ, 128), lambda h, i: (i, h)), pl.BlockSpec((TQ, 256), lambda h, i: (i, h))],
        out_shape=[jax.ShapeDtypeStruct((T, 512), F32), jax.ShapeDtypeStruct((T, 1024), F32)],
        scratch_shapes=[pltpu.VMEM((2, TQ, WIN), F32)],
        compiler_params=pltpu.CompilerParams(dimension_semantics=("arbitrary", "arbitrary")),
    )(proj, kvp, kvp, vt3)


def _chk_bwd(proj, kvp, vt3, o, lse, do):
    nq = T // TQ

    def body(q_ref, k_ref, v_ref, vt_ref, o_ref, l_ref, do_ref, dq_ref, dkb_ref, dvb_ref, gv_ref, bias_ref, dsum_ref,
             dk_ref, dv_ref):
        i = pl.program_id(1)

        @pl.when(i == 0)
        def _():
            for a in range(2):
                bias_ref[a] = _chk_bias(vt_ref, a)
            dsum_ref[...] = jnp.zeros_like(dsum_ref)
            dk_ref[...] = jnp.zeros_like(dk_ref)
            dv_ref[...] = jnp.zeros_like(dv_ref)

        lo = _lane_lo()
        off = pl.multiple_of(i * TQ, TQ)
        kw = k_ref[pl.ds(off, WIN), :]
        vw = v_ref[pl.ds(off, WIN), :]
        real = lax.broadcasted_iota(jnp.int32, (TQ, WIN), 1) + off >= LEFT
        q = q_ref[...]
        do_v = do_ref[...]
        prod = do_v * o_ref[...]
        dqs = []
        for a in range(2):
            keep = lo if a == 0 else jnp.logical_not(lo)
            qa = _half(q, lo, a)
            doa = _half(do_v, lo, a)
            delta = jnp.sum(jnp.where(keep, prod, 0.0), axis=-1, keepdims=True)
            s = jnp.where(real, _nt(qa, kw) * SCALE + bias_ref[a], NEG)
            p = jnp.exp(s - l_ref[:, 128 * a:128 * a + 1])
            ds = p * (_nt(doa, vw) - delta)
            dsum_ref[a] += ds
            dsb = ds.astype(BF16)
            dk_ref[pl.ds(off, WIN), :] += _tn(dsb, qa) * SCALE
            dv_ref[pl.ds(off, WIN), :] += _tn(p.astype(BF16), doa)
            dqs.append(_nn(dsb, kw))
        dq_ref[...] = (jnp.where(lo, dqs[0], dqs[1]) * SCALE).astype(BF16)

        @pl.when(i == nq - 1)
        def _():
            dkb_ref[...] = dk_ref[LEFT:, :].astype(BF16)
            dvb_ref[...] = dv_ref[LEFT:, :].astype(BF16)
            rr = lax.broadcasted_iota(jnp.int32, (TQ, TQ), 0)
            cc = lax.broadcasted_iota(jnp.int32, (TQ, TQ), 1)
            flip = jnp.where(rr + cc == TQ - 1, 1.0, 0.0).astype(F32)
            for a in range(2):
                dpad = jnp.concatenate([dsum_ref[a], jnp.zeros((TQ, VW - WIN), F32)], axis=1)
                z = pltpu.roll(_hdot(flip, dpad), 0, 1, stride=1, stride_axis=0)
                gv_ref[a:a + 1, :] = jnp.sum(z, axis=0, keepdims=True)

    blk = pl.BlockSpec((TQ, 128), lambda h, i: (i, h))
    wide = pl.BlockSpec((TQ, 256), lambda h, i: (i, h))
    col = pl.BlockSpec((T, 128), lambda h, i: (0, h))
    return _pcall(
        body, name="chk_bwd", grid=(4, nq), in_specs=_chk_specs() + [blk, wide, blk],
        out_specs=[blk, col, col, pl.BlockSpec((None, 2, VW), lambda h, i: (h, 0, 0))],
        out_shape=[jax.ShapeDtypeStruct((T, 512), BF16), jax.ShapeDtypeStruct((T, 512), BF16),
                   jax.ShapeDtypeStruct((T, 512), BF16), jax.ShapeDtypeStruct((4, 2, VW), F32)],
        scratch_shapes=[pltpu.VMEM((2, TQ, WIN), F32), pltpu.VMEM((2, TQ, WIN), F32),
                        pltpu.VMEM((T + LEFT, 128), F32), pltpu.VMEM((T + LEFT, 128), F32)],
        compiler_params=pltpu.CompilerParams(dimension_semantics=("arbitrary", "arbitrary")),
    )(proj, kvp, kvp, vt3, o, lse, do)


def _zero_at_start(*refs):
    @pl.when(pl.program_id(0) == 0)
    def _():
        for r in refs:
            r[...] = jnp.zeros_like(r)


def _ffn_bwd(dx3, y3, x2, a, w1_t, w2, g_post, g_pre):
    def body(dx3_ref, y_ref, x2_ref, a_ref, w1_ref, w2_ref, gp_ref, gf_ref,
             dx2_ref, da_ref, dy_ref, r_ref, dgp_ref, dgf_ref):
        _zero_at_start(dgp_ref, dgf_ref)
        dx3_v = dx3_ref[...]
        dy, dgp = _rms_bwd(y_ref[...], gp_ref[...], dx3_v)
        dgp_ref[...] += dgp
        dyb = dy.astype(BF16)
        dy_ref[...] = dyb
        ra = jnp.maximum(a_ref[...].astype(F32), 0.0)
        r_ref[...] = jnp.square(ra).astype(BF16)
        da = (_nt(dyb, _w(w2_ref)) * (2.0 * ra)).astype(BF16)
        da_ref[...] = da
        dh, dgf = _rms_bwd(x2_ref[...], gf_ref[...], _nn(da, _w(w1_ref)))
        dgf_ref[...] += dgf
        dx2_ref[...] = dx3_v + dh

    return _tok_call(body, "ffn_bwd", [dx3, y3, x2, a], [w1_t, w2, g_post, g_pre],
                     [(D, F32), (DFF, BF16), (D, BF16), (DFF, BF16)], [(1, D), (1, D)], vmem=VMEM_BIG)


def _mem_bwd(dx2, ym, x1, qm, km, vm, w_mo, w_mq, g_post, g_pre):
    def body(dx2_ref, ym_ref, x1_ref, q_ref, k_ref, v_ref, wo_ref, wq_ref, gp_ref, gm_ref,
             dx1_ref, dym_ref, dq_ref, dk_ref, dv_ref, dgp_ref, dgm_ref, dom_ref):
        _zero_at_start(dk_ref, dv_ref, dgp_ref, dgm_ref)
        dx2_v = dx2_ref[...]
        dym, dgp = _rms_bwd(ym_ref[...], gp_ref[...], dx2_v)
        dgp_ref[...] += dgp
        dymb = dym.astype(BF16)
        dym_ref[...] = dymb
        dom_ref[...] = _nt(dymb, _w(wo_ref)).astype(BF16)
        for h in range(MEM_HEADS):
            sl = slice(h * MEM_HD, (h + 1) * MEM_HD)
            qh, kh, doh = q_ref[:, sl], k_ref[:, sl], dom_ref[:, sl]
            s = _nt(qh, kh) * MEM_SCALE
            p = jnp.exp(s - jnp.max(s, axis=-1, keepdims=True))
            p = p / jnp.sum(p, axis=-1, keepdims=True)
            dp = _nt(doh, v_ref[:, sl])
            ds = (p * (dp - jnp.sum(p * dp, axis=-1, keepdims=True))).astype(BF16)
            dq_ref[:, sl] = (_nn(ds, kh) * MEM_SCALE).astype(BF16)
            dk_ref[:, sl] += _tn(ds, qh) * MEM_SCALE
            dv_ref[:, sl] += _tn(p.astype(BF16), doh)
        dh, dgm = _rms_bwd(x1_ref[...], gm_ref[...], _nt(dq_ref[...], _w(wq_ref)))
        dgm_ref[...] += dgm
        dx1_ref[...] = dx2_v + dh

    in_specs = [pl.BlockSpec((TM, D), lambda i: (i, 0))] * 4
    in_specs += [_resident(a) for a in (km, vm, w_mo, w_mq, g_post, g_pre)]
    w_mo, w_mq = w_mo[0], w_mq[0]
    tiled = pl.BlockSpec((TM, D), lambda i: (i, 0))
    kv = pl.BlockSpec((NMEM, D), lambda i: (0, 0))
    vec = pl.BlockSpec((1, D), lambda i: (0, 0))
    return _pcall(
        body, name="mem_bwd", grid=(T // TM,), in_specs=in_specs,
        out_specs=[tiled, tiled, tiled, kv, kv, vec, vec],
        out_shape=[jax.ShapeDtypeStruct((T, D), F32), jax.ShapeDtypeStruct((T, D), BF16),
                   jax.ShapeDtypeStruct((T, D), BF16), jax.ShapeDtypeStruct((NMEM, D), F32),
                   jax.ShapeDtypeStruct((NMEM, D), F32), jax.ShapeDtypeStruct((1, D), F32),
                   jax.ShapeDtypeStruct((1, D), F32)],
        scratch_shapes=[pltpu.VMEM((TM, D), BF16)],
        compiler_params=pltpu.CompilerParams(dimension_semantics=("arbitrary",), vmem_limit_bytes=VMEM_BIG),
    )(dx2, ym, x1, qm, km, vm, w_mo, w_mq, g_post, g_pre)


def _memkv_bwd(dkm, dvm, mem, w_mk, w_mv):
    def body(dk_ref, dv_ref, m_ref, wk_ref, wv_ref, dg_ref):
        dmn = _nt(dk_ref[...].astype(BF16), _w(wk_ref)) + _nt(dv_ref[...].astype(BF16), _w(wv_ref))
        mv = m_ref[...]
        dg_ref[...] = jnp.sum(dmn * (mv * _rstd(mv)), axis=0, keepdims=True)

    return _one_call(body, "memkv_bwd", [dkm, dvm, mem, w_mk, w_mv], [((1, D), F32)], vmem=VMEM_BIG)[0]


def _postmix_bwd(dx1, z, o_f, o_c, w_out, g_post, g_fo, g_co):
    def body(dx1_ref, z_ref, of_ref, oc_ref, wo_ref, gp_ref, gfo_ref, gco_ref,
             dz_ref, dof_ref, doc_ref, dgp_ref, dgfo_ref, dgco_ref):
        _zero_at_start(dgp_ref, dgfo_ref, dgco_ref)
        dz, dgp = _rms_bwd(z_ref[...], gp_ref[...], dx1_ref[...])
        dgp_ref[...] += dgp
        dzb = dz.astype(BF16)
        dz_ref[...] = dzb
        dy = _nt(dzb, _w(wo_ref))
        dof, dgfo = _rms_bwd(of_ref[...], gfo_ref[...], dy[:, :512])
        doc, dgco = _rms_bwd(oc_ref[...], gco_ref[...], dy[:, 512:])
        dof_ref[...] = dof
        doc_ref[...] = doc
        dgfo_ref[...] += dgfo
        dgco_ref[...] += dgco

    return _tok_call(body, "postmix_bwd", [dx1, z, o_f, o_c], [w_out, g_post, g_fo, g_co],
                     [(D, BF16), (512, F32), (512, F32)], [(1, D), (1, 512), (1, 512)], vmem=VMEM_BIG)


def _premix_bwd(dx1, x, pieces, win_t, g_pre):
    def body(dx1_ref, x_ref, *refs):
        piece_refs, (w_ref, g_ref, dx_ref, dp_ref, dg_ref) = refs[:len(pieces)], refs[len(pieces):]
        _zero_at_start(dg_ref)
        col = 0
        for p in piece_refs:
            dp_ref[:, col:col + p.shape[1]] = p[...]
            col += p.shape[1]
        dh, dg = _rms_bwd(x_ref[...], g_ref[...], _nn(dp_ref[...], w_ref[...]))
        dg_ref[...] += dg
        dx_ref[...] = dx1_ref[...] + dh

    return _tok_call(body, "premix_bwd", [dx1, x] + list(pieces), [win_t, g_pre], [(D, F32), (PROJ, BF16)], [(1, D)],
                     vmem=VMEM_BIG)


def _wgrad(a, b, name):
    k, m = a.shape
    n = b.shape[1]
    tm = 640 if m % 640 == 0 and m > 1024 else min(m, 512)
    tn = min(n, 1024)

    def body(a_ref, b_ref, o_ref):
        o_ref[...] = _tn(a_ref[...].astype(BF16), b_ref[...].astype(BF16))

    return _pcall(
        body, name=name, grid=(m // tm, n // tn),
        in_specs=[pl.BlockSpec((k, tm), lambda i, j: (0, i)), pl.BlockSpec((k, tn), lambda i, j: (0, j))],
        out_specs=pl.BlockSpec((tm, tn), lambda i, j: (i, j)),
        out_shape=jax.ShapeDtypeStruct((m, n), F32),
        compiler_params=pltpu.CompilerParams(dimension_semantics=("arbitrary", "arbitrary"), vmem_limit_bytes=VMEM_BIG),
    )(a, b)


def _wgrad_group(name, pairs, rows):
    def body(*refs):
        o_ref = refs[-1]
        for k in range(len(pairs)):
            o_ref[k * rows:(k + 1) * rows, :] = _tn(refs[2 * k][...].astype(BF16), refs[2 * k + 1][...].astype(BF16))

    in_specs, ops = [], []
    for a, b in pairs:
        in_specs += [pl.BlockSpec((a.shape[0], rows), lambda j: (0, j)), _resident(b)]
        ops += [a, b]
    return _pcall(
        body, name=name, grid=(8,), in_specs=in_specs,
        out_specs=pl.BlockSpec((None, len(pairs) * rows, D), lambda j: (j, 0, 0)),
        out_shape=jax.ShapeDtypeStruct((8, len(pairs) * rows, D), F32),
        compiler_params=pltpu.CompilerParams(dimension_semantics=("arbitrary",), vmem_limit_bytes=VMEM_BIG),
    )(*ops)


def _adam_math(w, g, m, v):
    m2 = ADAM_B1 * m + (1.0 - ADAM_B1) * g
    v2 = ADAM_B2 * v + (1.0 - ADAM_B2) * jnp.square(g)
    m_hat = m2 / (1.0 - ADAM_B1 ** ADAM_STEP)
    v_hat = v2 / (1.0 - ADAM_B2 ** ADAM_STEP)
    delta = -ADAM_LR * (m_hat / (jnp.sqrt(v_hat) + ADAM_EPS) + ADAM_WD * w)
    return delta, m2, v2


def _adamw(w, g, m, v, name):
    rows, cols = w.shape
    tr = 256 if rows % 256 == 0 else rows

    def body(w_ref, g_ref, m_ref, v_ref, d_ref, m2_ref, v2_ref):
        d_ref[...], m2_ref[...], v2_ref[...] = _adam_math(w_ref[...], g_ref[...], m_ref[...], v_ref[...])

    spec = pl.BlockSpec((tr, cols), lambda i: (i, 0))
    return _pcall(
        body, name=name, grid=(rows // tr,), in_specs=[spec] * 4, out_specs=[spec] * 3,
        out_shape=[jax.ShapeDtypeStruct(w.shape, F32)] * 3,
        compiler_params=pltpu.CompilerParams(dimension_semantics=("arbitrary",)),
    )(w, g, m, v)


def _adamw_small(w, gparts, m, v):
    def body(w_ref, g_ref, m_ref, v_ref, gs_ref, d_ref, m2_ref, v2_ref):
        g = g_ref[0]
        for k in range(1, 8):
            g = g + g_ref[k]
        gs_ref[...] = g
        d_ref[...], m2_ref[...], v2_ref[...] = _adam_math(w_ref[...], g, m_ref[...], v_ref[...])

    return _one_call(body, "adamw_small", [w, gparts, m, v], [(w.shape, F32)] * 4)


def _row_tile(rows):
    return next(t for t in (512, 400, 320) if rows % t == 0)


def _add_halves(g4, theirs, core, name):
    rows = g4.shape[2]
    tr = _row_tile(rows)

    def body(c_ref, a_ref, b_ref, o_ref):
        o_ref[...] = (a_ref[...] + b_ref[...]).astype(BF16)

    grid_spec = pltpu.PrefetchScalarGridSpec(
        num_scalar_prefetch=1, grid=(4, rows // tr),
        in_specs=[pl.BlockSpec((None, None, tr, D), lambda j, i, c: (j, c[0], i, 0)),
                  pl.BlockSpec((None, None, tr, D), lambda j, i, c: (j, 0, i, 0))],
        out_specs=pl.BlockSpec((None, tr, D), lambda j, i, c: (j, i, 0)))
    return _pcall(
        body, name=name, grid_spec=grid_spec, out_shape=jax.ShapeDtypeStruct((4, rows, D), BF16),
        compiler_params=pltpu.CompilerParams(dimension_semantics=("arbitrary", "arbitrary")),
    )(core, g4, theirs)


def _sum_adam(own, got, order, r0, w, m, v, name, transposed=False):
    n = w.shape[1] if transposed else w.shape[0]
    tr = min(n, 256)

    def body(o_ref, a_ref, b_ref, c_ref, d_ref, w_ref, m_ref, v_ref, g_ref, dl_ref, m2_ref, v2_ref):
        f = lambda r: r[...].astype(F32)
        g = ((f(a_ref) + f(b_ref)) + f(c_ref)) + f(d_ref)
        g = g.T if transposed else g
        g_ref[...] = g
        dl_ref[...], m2_ref[...], v2_ref[...] = _adam_math(w_ref[...], g, m_ref[...], v_ref[...])

    slot = lambda k: pl.BlockSpec((None, tr, D), lambda i, o: (o[k], r0 // tr + i, 0))
    wspec = pl.BlockSpec((D, tr), lambda i, o: (0, i)) if transposed else pl.BlockSpec((tr, D), lambda i, o: (i, 0))
    grid_spec = pltpu.PrefetchScalarGridSpec(
        num_scalar_prefetch=1, grid=(n // tr,), in_specs=[slot(0), slot(1), slot(2), slot(3), wspec, wspec, wspec],
        out_specs=[wspec] * 4)
    return _pcall(
        body, name=name, grid_spec=grid_spec, out_shape=[jax.ShapeDtypeStruct(w.shape, F32)] * 4,
        compiler_params=pltpu.CompilerParams(dimension_semantics=("arbitrary",)),
    )(order, own, got, got, got, w, m, v)


def _sum_chips(own, got, order, name):
    rows = own.shape[1]
    tr = _row_tile(rows)

    def body(o_ref, a_ref, b_ref, c_ref, d_ref, out_ref):
        f = lambda r: r[...].astype(F32)
        out_ref[...] = ((f(a_ref) + f(b_ref)) + f(c_ref)) + f(d_ref)

    slot = lambda k: pl.BlockSpec((None, tr, D), lambda i, o: (o[k], i, 0))
    grid_spec = pltpu.PrefetchScalarGridSpec(
        num_scalar_prefetch=1, grid=(rows // tr,), in_specs=[slot(0), slot(1), slot(2), slot(3)],
        out_specs=pl.BlockSpec((tr, D), lambda i, o: (i, 0)))
    return _pcall(
        body, name=name, grid_spec=grid_spec, out_shape=jax.ShapeDtypeStruct((rows, D), F32),
        compiler_params=pltpu.CompilerParams(dimension_semantics=("arbitrary",)),
    )(order, own, got, got, got)


def _place():
    return lax.axis_index("x"), lax.axis_index("y"), lax.axis_index("c")


def _allgather(block, name):
    def body(x_ref, out_ref, token, send_sems, recv_sems, local_sem):
        token[...] = jnp.zeros_like(token)
        x, y, c = _place()
        me, sibling = (x, y, c), (x, y, 1 - c)
        chips = [(1 - x, y), (x, 1 - y), (1 - x, 1 - y)]

        def slot(px, py, pc):
            return out_ref.at[4 * px + 2 * py + pc]

        def copy(k, blk, to, src=None):
            return pltpu.make_async_remote_copy(
                src_ref=slot(*blk) if src is None else src, dst_ref=slot(*blk),
                send_sem=send_sems.at[k], recv_sem=recv_sems.at[k], device_id=to, device_id_type=MESH)

        mine = pltpu.make_async_copy(x_ref, slot(*me), local_sem)
        mine.start()
        first = [copy(0, me, sibling, src=x_ref)]
        first += [copy(1 + j, me, (*chip, c), src=x_ref) for j, chip in enumerate(chips)]
        for cp in first:
            cp.start()
        passed = [copy(4 + j, (*chip, c), sibling) for j, chip in enumerate(chips)]
        for j, chip in enumerate(chips):
            copy(1 + j, (*chip, c), me).wait_recv()
            passed[j].start()
        copy(0, sibling, me).wait_recv()
        for j, chip in enumerate(chips):
            copy(4 + j, (*chip, 1 - c), me).wait_recv()
        for cp in first + passed:
            cp.wait_send()
        mine.wait()

    return _pcall(
        body, name=name,
        out_shape=[jax.ShapeDtypeStruct((8,) + block.shape, block.dtype), jax.ShapeDtypeStruct((8, 128), F32)],
        in_specs=[pl.BlockSpec(memory_space=pl.ANY)],
        out_specs=[pl.BlockSpec(memory_space=pl.ANY), pl.BlockSpec(memory_space=pltpu.VMEM)],
        scratch_shapes=[pltpu.SemaphoreType.DMA((7,)), pltpu.SemaphoreType.DMA((7,)), pltpu.SemaphoreType.DMA(())],
        compiler_params=pltpu.CompilerParams(has_side_effects=True),
    )(block)


HBM_SPEC = pl.BlockSpec(memory_space=pltpu.HBM)
SEM_SPEC = pl.BlockSpec(memory_space=pltpu.SEMAPHORE)
ANY_SPEC = pl.BlockSpec(memory_space=pl.ANY)
EFFECT = pltpu.SideEffectType.DATAFLOW_SIDE_EFFECTING


def _in_hbm(a):
    return pltpu.with_memory_space_constraint(a, pltpu.HBM)


def _start_copies(name, src, land_shape, plan, n):
    def body(src_ref, land_ref, send_sems, recv_sems, src_thru, land_thru, token):
        for k, (s, d, to, _) in enumerate(plan(src_ref, land_ref)):
            pltpu.make_async_remote_copy(src_ref=s, dst_ref=d, send_sem=send_sems.at[k], recv_sem=recv_sems.at[k],
                                         device_id=to, device_id_type=MESH).start()
        token[...] = jnp.zeros_like(token)

    return _pcall(
        body, name=name,
        out_shape=(pltpu.SemaphoreType.DMA((n,)), pltpu.SemaphoreType.DMA((n,)), pltpu.HBM(src.shape, src.dtype),
                   pltpu.HBM(land_shape, src.dtype), jax.ShapeDtypeStruct((8, 128), F32)),
        in_specs=(HBM_SPEC, HBM_SPEC),
        out_specs=(SEM_SPEC, SEM_SPEC, HBM_SPEC, HBM_SPEC, pl.BlockSpec(memory_space=pltpu.VMEM)),
        input_output_aliases={0: 2, 1: 3}, compiler_params=pltpu.CompilerParams(has_side_effects=EFFECT),
    )(_in_hbm(src), _in_hbm(lax.empty(land_shape, src.dtype)))


def _wait_copies(name, started, after, plan):
    send_sems, recv_sems, src_thru, land_thru, _ = started

    def body(src_ref, land_ref, send_sems, recv_sems, *rest):
        for k, (s, _, to, mine) in enumerate(plan(src_ref, land_ref)):
            cp = pltpu.make_async_remote_copy(src_ref=s, dst_ref=mine, send_sem=send_sems.at[k],
                                              recv_sem=recv_sems.at[k], device_id=to, device_id_type=MESH)
            cp.wait_send()
            cp.wait_recv()

    return _pcall(
        body, name=name,
        out_shape=(pltpu.HBM(src_thru.shape, src_thru.dtype), pltpu.HBM(land_thru.shape, land_thru.dtype)),
        in_specs=(HBM_SPEC, HBM_SPEC, SEM_SPEC, SEM_SPEC) + (ANY_SPEC,) * len(after), out_specs=(HBM_SPEC, HBM_SPEC),
        input_output_aliases={0: 0, 1: 1}, compiler_params=pltpu.CompilerParams(has_side_effects=EFFECT),
    )(src_thru, land_thru, send_sems, recv_sems, *after)


def _gather_plan(src_ref, land_ref):
    x, y, c = _place()
    peers = [(x, y, 1 - c), (1 - x, y, c), (x, 1 - y, c), (1 - x, 1 - y, c)]
    return [(src_ref, land_ref.at[4 * x + 2 * y + c], p, land_ref.at[4 * p[0] + 2 * p[1] + p[2]]) for p in peers]


def _swap_plan(src_ref, land_ref):
    x, y, c = _place()
    return [(src_ref.at[:, pl.ds(1 - c, 1)], land_ref, (x, y, 1 - c), land_ref)]


def _exchange_plan(src_ref, land_ref):
    x, y, c = _place()
    chips = [(1 - x, y), (x, 1 - y), (1 - x, 1 - y)]
    return [(src_ref.at[2 * px + py], land_ref.at[2 * x + y], (px, py, c), land_ref.at[2 * px + py]) for px, py in chips]


def _gather_forward(land, block):
    def body(land_ref, out_ref, send_sems, recv_sems):
        x, y, c = _place()
        chips = [(1 - x, y), (x, 1 - y), (1 - x, 1 - y)]

        def copy(k, px, py, pc):
            blk = out_ref.at[4 * px + 2 * py + pc]
            return pltpu.make_async_remote_copy(src_ref=blk, dst_ref=blk, send_sem=send_sems.at[k],
                                                recv_sem=recv_sems.at[k], device_id=(x, y, 1 - c), device_id_type=MESH)

        sent = [copy(k, px, py, c) for k, (px, py) in enumerate(chips)]
        for cp in sent:
            cp.start()
        for k, (px, py) in enumerate(chips):
            copy(k, px, py, 1 - c).wait_recv()
        for cp in sent:
            cp.wait_send()

    land = _pcall(
        body, name="allgather_rest_forward", out_shape=jax.ShapeDtypeStruct(land.shape, land.dtype),
        in_specs=[ANY_SPEC], out_specs=ANY_SPEC, input_output_aliases={0: 0},
        scratch_shapes=[pltpu.SemaphoreType.DMA((3,)), pltpu.SemaphoreType.DMA((3,))],
        compiler_params=pltpu.CompilerParams(has_side_effects=True),
    )(land)

    rows = block.shape[0]
    tr = rows // 4

    def place(me_ref, x_ref, land_ref, out_ref):
        out_ref[...] = x_ref[...]

    x, y, c = _place()
    grid_spec = pltpu.PrefetchScalarGridSpec(
        num_scalar_prefetch=1, grid=(rows // tr,),
        in_specs=[pl.BlockSpec((tr, D), lambda i, me: (i, 0)), ANY_SPEC],
        out_specs=pl.BlockSpec((None, tr, D), lambda i, me: (me[0], i, 0)))
    return _pcall(
        place, name="allgather_rest_own", grid_spec=grid_spec, out_shape=jax.ShapeDtypeStruct(land.shape, land.dtype),
        input_output_aliases={2: 0}, compiler_params=pltpu.CompilerParams(dimension_semantics=("arbitrary",)),
    )((4 * x + 2 * y + c).reshape(1), block, land)


class _ReduceScatter:
    def __init__(self, name, g):
        self.name = name
        rows = g.shape[1]
        self.started = _start_copies(name + "_swap_start", g.reshape(4, 2, rows, D), (4, 1, rows, D), _swap_plan, 1)
        self.token = self.started[4][0, 0]

    def halfway(self, after):
        g4, theirs = _wait_copies(self.name + "_swap_wait", self.started, after, _swap_plan)
        self.own = _add_halves(g4, theirs, lax.axis_index("c").reshape(1), self.name + "_add_halves")
        self.started = _start_copies(self.name + "_exch_start", self.own, self.own.shape, _exchange_plan, 3)
        self.token = self.started[4][0, 0]

    def finish(self, after):
        own, got = _wait_copies(self.name + "_exch_wait", self.started, after, _exchange_plan)
        chip = 2 * lax.axis_index("x") + lax.axis_index("y")
        return own, got, (chip + jnp.arange(4, dtype=jnp.int32)) % 4


def _pack_small(p, scalar=None):
    z = lambda a, n: jnp.pad(a, ((0, 0), (0, n - a.shape[1])))
    rows = [z(p['rel_bias'], D), z(p['b_fgt'], D), jnp.concatenate([p['g_fox_out'], p['g_chk_out']], axis=1)]
    rows += [p[n] for n in ('g_mix_pre', 'g_mix_post', 'g_mem_kv', 'g_mem_pre', 'g_mem_post', 'g_ff_pre', 'g_ff_post')]
    rows.append(jnp.zeros((1, D), F32) if scalar is None else z(jnp.reshape(scalar, (1, 1)), D))
    rows.append(jnp.zeros((SMALL_ROWS - 18, D), F32))
    return jnp.concatenate(rows, axis=0)


def _unpack_small(a):
    out = {'rel_bias': a[0:8, :257], 'b_fgt': a[8:9, :8], 'g_fox_out': a[9:10, :512], 'g_chk_out': a[9:10, 512:]}
    for k, n in enumerate(('g_mix_pre', 'g_mix_post', 'g_mem_kv', 'g_mem_pre', 'g_mem_post', 'g_ff_pre', 'g_ff_post')):
        out[n] = a[10 + k:11 + k]
    return out


_GAP_DEV, _GAP_ROW = divmod(GATE0 + 8, N_IN)
_GAP = CHK0 - GATE0 - 8


def _in_rows_to_proj(g):
    wt = g[:, :N_IN].reshape(8 * N_IN, D)
    return jnp.concatenate([wt[:GATE0], jnp.pad(wt[GATE0:GATE0 + 8], ((0, _GAP), (0, 0))), wt[GATE0 + 8:]], axis=0)


def _proj_rows_to_in(g):
    pad = lambda a: jnp.pad(a, ((0, R_IN - a.shape[0]), (0, 0)))
    lo = N_IN * _GAP_DEV
    shards = [pad(g[N_IN * j:N_IN * (j + 1)]) for j in range(_GAP_DEV)]
    shards.append(pad(jnp.concatenate([g[lo:lo + _GAP_ROW], g[lo + _GAP_ROW + _GAP:lo + N_IN + _GAP]], axis=0)))
    shards += [pad(g[N_IN * j + _GAP:N_IN * (j + 1) + _GAP]) for j in range(_GAP_DEV + 1, 8)]
    return jnp.stack(shards)


def _local_grads(x, mem, tgt, win_t, gw_of, sm, on_grads):
    b_pad = jnp.pad(sm['b_fgt'], ((0, 0), (0, 120)))
    tbl = jnp.pad(sm['rel_bias'], ((0, 0), (0, NREL_PAD - 257)))

    h1, proj, flog = _premix_fwd(x, sm['g_mix_pre'], win_t)
    c = _gate_fwd(flog, b_pad)
    c8 = c[:, :8]
    c2 = jnp.repeat(c8, 128, axis=1)
    ct3 = c8.T.reshape(4, 2, T)
    o_f, lse_f = _fox_fwd(proj, c2, ct3)
    vt3 = _relvec_fwd(tbl).reshape(4, 2, VW)
    kvp = jnp.pad(proj[:, CHK0 + 512:], ((LEFT, 0), (0, 0)))
    o_c, lse_c = _chk_fwd(proj, kvp, vt3)
    gw = gw_of([o_f, o_c])
    w_out, w_mq, w_mk, w_mv, w_mo, w1_t, w2 = (_wblk(gw, n) for n in ('w_out', 'w_mq', 'w_mk', 'w_mv', 'w_mo', 'w_ff1', 'w_ff2'))
    ycat, z, x1, h2, qm = _postmix_fwd(x, o_f, o_c, sm['g_fox_out'], sm['g_chk_out'], w_out,
                                       sm['g_mix_post'], sm['g_mem_pre'], w_mq)
    memn, km, vm = _memkv_fwd(mem, sm['g_mem_kv'], w_mk, w_mv)
    om, ym, x2, h3 = _mem_fwd(qm, x1, km, vm, w_mo, sm['g_mem_post'], sm['g_ff_pre'])
    a, y3, dx3, loss_acc = _ffn_fwd(h3, x2, tgt, w1_t, w2, sm['g_ff_post'])

    gs = {}
    dx2, da, dy3, r, gs['g_ff_post'], gs['g_ff_pre'] = _ffn_bwd(dx3, y3, x2, a, w1_t, w2, sm['g_ff_post'], sm['g_ff_pre'])
    zero = on_grads('A', _wgrad_group("wgrad_ff", [(da, h3), (r, dy3)], 512), None)
    dx1, dym, dqm, dkm, dvm, gs['g_mem_post'], gs['g_mem_pre'] = _mem_bwd(
        dx2, ym, x1, qm, km, vm, w_mo, w_mq, sm['g_mem_post'] + zero, sm['g_mem_pre'])
    zero = on_grads('A halfway', None, [dx1])
    gs['g_mem_kv'] = _memkv_bwd(dkm, dvm, mem, w_mk, w_mv)
    dz, dof, doc, gs['g_mix_post'], gs['g_fox_out'], gs['g_chk_out'] = _postmix_bwd(
        dx1, z, o_f, o_c, w_out, sm['g_mix_post'] + zero, sm['g_fox_out'], sm['g_chk_out'])
    zero = on_grads('B', _wgrad_group("wgrad_mem_out", [(ycat, dz), (h2, dqm), (memn, dkm), (memn, dvm), (om, dym)], 128), None)
    dq_f, dk_f, dv_f, dct, dcq = _fox_bwd(proj, c2, ct3 + zero, o_f, lse_f, dof)
    zero = on_grads('B halfway', None, [dq_f])
    dq_c, dk_c, dv_c, gv = _chk_bwd(proj, kvp, vt3 + zero, o_c, lse_c, doc)
    gs['rel_bias'] = _relvec_bwd(gv.reshape(8, VW))[:, :257]
    dc = jnp.pad(dct.reshape(8, T).T + dcq[:, ::128], ((0, 0), (0, 120)))
    dflog, db = _gate_bwd(dc, flog, b_pad)
    gs['b_fgt'] = db[0:1, :8]
    grad_x, dproj, gs['g_mix_pre'] = _premix_bwd(dx1, x, [dq_f, dk_f, dv_f, dflog, dq_c, dk_c, dv_c], win_t, sm['g_mix_pre'])
    on_grads('C', _proj_rows_to_in(_wgrad(dproj, h1, "wgrad_in")), None)
    return loss_acc[0, 0], grad_x, gs


def kernel(x, mem, w_in, b_fgt, rel_bias, g_fox_out, g_chk_out, w_out, g_mix_pre, g_mix_post, g_mem_kv, w_mq, w_mk, w_mv, w_mo, g_mem_pre, g_mem_post, w_ff1, w_ff2, g_ff_pre, g_ff_post, loss_target, m_w_in, m_b_fgt, m_rel_bias, m_g_fox_out, m_g_chk_out, m_w_out, m_g_mix_pre, m_g_mix_post, m_g_mem_kv, m_w_mq, m_w_mk, m_w_mv, m_w_mo, m_g_mem_pre, m_g_mem_post, m_w_ff1, m_w_ff2, m_g_ff_pre, m_g_ff_post, v_w_in, v_b_fgt, v_rel_bias, v_g_fox_out, v_g_chk_out, v_w_out, v_g_mix_pre, v_g_mix_post, v_g_mem_kv, v_w_mq, v_w_mk, v_w_mv, v_w_mo, v_g_mem_pre, v_g_mem_post, v_w_ff1, v_w_ff2, v_g_ff_pre, v_g_ff_post):
    args = dict(locals())
    two_d = lambda a: a.reshape(a.shape[-2:])
    w = {n: two_d(args[n]) for n in WEIGHTS}
    m = {n: two_d(args['m_' + n]) for n in WEIGHTS}
    v = {n: two_d(args['v_' + n]) for n in WEIGHTS}

    sm = {n: w[n] for n in SMALL}
    shard_in = jnp.pad(w['w_in'].T, ((0, R_IN - N_IN), (0, 0))).astype(BF16)
    gathered_in, zero = _allgather(shard_in, "allgather_w_in")
    win_t = _in_rows_to_proj(gathered_in)
    shard_rest = (jnp.concatenate([w['w_ff1'].T, w['w_ff2'], w['w_out'], w['w_mq'], w['w_mk'], w['w_mv'], w['w_mo']],
                                  axis=0) + zero[0, 0]).astype(BF16)
    rest = _start_copies("allgather_rest_start", shard_rest, (8, R_REST, D), _gather_plan, 4)
    sm['g_mix_pre'] = sm['g_mix_pre'] + rest[4][0, 0]

    def gw_of(after):
        block, land = _wait_copies("allgather_rest_wait", rest, after, _gather_plan)
        return _gather_forward(land, block)

    rs = {}

    def on_grads(stage, g, after):
        if stage.endswith('halfway'):
            rs[stage[0]].halfway(after)
            return rs[stage[0]].token
        rs[stage] = _ReduceScatter("rs_" + stage.lower(), g)
        return rs[stage].token

    loss_local, grad_x, gs = _local_grads(x[0], mem[0], loss_target[0], win_t, gw_of, sm, on_grads)
    grads, deltas, new_m, new_v = {}, {}, {}, {}

    def update(n, out):
        grads[n], deltas[n], new_m[n], new_v[n] = out

    gparts, _ = _allgather(_pack_small(gs, loss_local + rs['C'].token), "allgather_small_grads")
    gsum, d_s, m_s, v_s = _adamw_small(_pack_small(sm), gparts, _pack_small({n: m[n] for n in SMALL}),
                                       _pack_small({n: v[n] for n in SMALL}))
    for dst, packed in ((grads, gsum), (deltas, d_s), (new_m, m_s), (new_v, v_s)):
        dst.update(_unpack_small(packed))
    loss = gsum[17, 0]
    rs['C'].halfway([gsum])

    own, got, order = rs['A'].finish([grad_x, rs['C'].started[4]])
    update('w_ff1', _sum_adam(own, got, order, 0, w['w_ff1'], m['w_ff1'], v['w_ff1'], "adamw_w_ff1", transposed=True))
    update('w_ff2', _sum_adam(own, got, order, 512, w['w_ff2'], m['w_ff2'], v['w_ff2'], "adamw_w_ff2"))
    own, got, order = rs['B'].finish([grad_x, rs['C'].started[4]])
    for k, n in enumerate(('w_out', 'w_mq', 'w_mk', 'w_mv', 'w_mo')):
        update(n, _sum_adam(own, got, order, 128 * k, w[n], m[n], v[n], "adamw_" + n))

    g_in = _sum_chips(*rs['C'].finish([new_v['w_ff2'], new_v['w_mo']]), "rs_c_sum_chips")[:N_IN].T
    update('w_in', (g_in,) + tuple(_adamw(w['w_in'], g_in, m['w_in'], v['w_in'], "adamw_w_in")))

    out = [loss, grad_x[None]]
    for group in (grads, deltas, new_m, new_v):
        out += [group[n].reshape(args[n].shape) for n in WEIGHTS]
    return tuple(out)
```

```python
import functools

import jax
import jax.numpy as jnp
from jax import lax
from jax.experimental import pallas as pl
from jax.experimental.pallas import tpu as pltpu

F32 = jnp.float32
BF16 = jnp.bfloat16
MESH = pl.DeviceIdType.MESH

T = 2048
D = 1024
NMEM = 256
DFF = 4096
EPS = 1e-6
TM = 256
TM_WIDE = 512
TQ = 256
FQ = 512
HD = 64
SCALE = HD ** -0.5
MEM_HEADS = 4
MEM_HD = 256
MEM_SCALE = MEM_HD ** -0.5
NEG = -1e30
LEFT = 512
WIN = LEFT + TQ
VW = 1024
NREL_PAD = 384
PROJ = 3200
GATE0 = 1536
CHK0 = 1664
VMEM_BIG = 56 * 1024 * 1024

ADAM_LR = 0.001
ADAM_B1 = 0.9
ADAM_B2 = 0.999
ADAM_EPS = 1e-08
ADAM_WD = 0.01
ADAM_STEP = 10

N_IN = 385
R_IN = 400
R_REST = 1664
W_ROWS = {'w_ff1': (0, 512), 'w_ff2': (512, 512),
          'w_out': (1024, 128), 'w_mq': (1152, 128), 'w_mk': (1280, 128), 'w_mv': (1408, 128), 'w_mo': (1536, 128)}
R_A, R_B = 1024, 640
SMALL_ROWS = 24

WEIGHTS = ['w_in', 'b_fgt', 'rel_bias', 'g_fox_out', 'g_chk_out', 'w_out', 'g_mix_pre', 'g_mix_post', 'g_mem_kv',
           'w_mq', 'w_mk', 'w_mv', 'w_mo', 'g_mem_pre', 'g_mem_post', 'w_ff1', 'w_ff2', 'g_ff_pre', 'g_ff_post']
BIG = ['w_in', 'w_out', 'w_mq', 'w_mk', 'w_mv', 'w_mo', 'w_ff1', 'w_ff2']
SMALL = [n for n in WEIGHTS if n not in BIG]


def _pcall(body, **kw):
    return pl.pallas_call(body, **kw)


def _nn(a, b):
    return jnp.dot(a, b, preferred_element_type=F32)


def _nt(a, b):
    return lax.dot_general(a, b, (((1,), (1,)), ((), ())), preferred_element_type=F32)


def _tn(a, b):
    return lax.dot_general(a, b, (((0,), (0,)), ((), ())), preferred_element_type=F32)


def _w(ref):
    v = ref[...]
    return v if v.ndim == 2 else v.reshape(-1, v.shape[-1])


def _rstd(x):
    return lax.rsqrt(jnp.mean(x * x, axis=-1, keepdims=True) + EPS)


def _rms(x, g):
    return x * _rstd(x) * g


def _rms_bwd(x, g, dy):
    r = _rstd(x)
    xh = x * r
    dg = jnp.sum(dy * xh, axis=0, keepdims=True)
    dxh = dy * g
    dx = r * (dxh - xh * jnp.mean(dxh * xh, axis=-1, keepdims=True))
    return dx, dg


def _resident(a):
    if isinstance(a, tuple):
        _, shape, index = a
        return pl.BlockSpec(shape, lambda *_: index, pipeline_mode=pl.Buffered(1))
    return pl.BlockSpec(a.shape, lambda *_, nd=a.ndim: (0,) * nd, pipeline_mode=pl.Buffered(1))


def _wblk(gw, name):
    r0, rows = W_ROWS[name]
    return (gw, (8, rows, D), (0, r0 // rows, 0))


def _tok_call(body, name, tiled, full, outs_tiled, outs_acc=(), rows=T, tm=TM, vmem=None):
    in_specs = [pl.BlockSpec((tm, a.shape[1]), lambda i: (i, 0)) for a in tiled]
    in_specs += [_resident(a) for a in full]
    full = [a[0] if isinstance(a, tuple) else a for a in full]
    out_shape = [jax.ShapeDtypeStruct((rows, c), dt) for c, dt in outs_tiled]
    out_shape += [jax.ShapeDtypeStruct(s, F32) for s in outs_acc]
    out_specs = [pl.BlockSpec((tm, c), lambda i: (i, 0)) for c, _ in outs_tiled]
    out_specs += [pl.BlockSpec(s, lambda i, nd=len(s): (0,) * nd) for s in outs_acc]
    return _pcall(
        body, name=name, grid=(rows // tm,), in_specs=in_specs, out_specs=out_specs, out_shape=out_shape,
        compiler_params=pltpu.CompilerParams(dimension_semantics=("arbitrary",), vmem_limit_bytes=vmem),
    )(*tiled, *full)


def _one_call(body, name, ins, outs, vmem=None):
    whole = lambda s: pl.BlockSpec(s, lambda i, nd=len(s): (0,) * nd)
    return _pcall(
        body, name=name, grid=(1,), in_specs=[_resident(a) for a in ins], out_specs=[whole(s) for s, _ in outs],
        out_shape=[jax.ShapeDtypeStruct(s, dt) for s, dt in outs],
        compiler_params=pltpu.CompilerParams(dimension_semantics=("arbitrary",), vmem_limit_bytes=vmem),
    )(*[a[0] if isinstance(a, tuple) else a for a in ins])


def _premix_fwd(x, g_pre, win_t):
    def body(x_ref, g_ref, w_ref, h_ref, proj_ref, flog_ref):
        h = _rms(x_ref[...], g_ref[...]).astype(BF16)
        h_ref[...] = h
        p = _nt(h, w_ref[...])
        proj_ref[...] = p.astype(BF16)
        flog_ref[...] = p[:, GATE0:GATE0 + 128]

    return _tok_call(body, "premix_fwd", [x], [g_pre, win_t],
                     [(D, BF16), (PROJ, BF16), (128, F32)], tm=TM_WIDE, vmem=VMEM_BIG)


def _postmix_fwd(x, o_f, o_c, g_fo, g_co, w_out, g_post, g_mpre, w_mq):
    def body(x_ref, of_ref, oc_ref, gfo_ref, gco_ref, wo_ref, gp_ref, gm_ref, wq_ref,
             y_ref, z_ref, x1_ref, h2_ref, qm_ref):
        y_ref[:, :512] = _rms(of_ref[...], gfo_ref[...]).astype(BF16)
        y_ref[:, 512:] = _rms(oc_ref[...], gco_ref[...]).astype(BF16)
        z = _nn(y_ref[...], _w(wo_ref))
        z_ref[...] = z
        x1 = x_ref[...] + _rms(z, gp_ref[...])
        x1_ref[...] = x1
        h2 = _rms(x1, gm_ref[...]).astype(BF16)
        h2_ref[...] = h2
        qm_ref[...] = _nn(h2, _w(wq_ref)).astype(BF16)

    return _tok_call(body, "postmix_fwd", [x, o_f, o_c], [g_fo, g_co, w_out, g_post, g_mpre, w_mq],
                     [(D, BF16), (D, F32), (D, F32), (D, BF16), (D, BF16)], tm=TM_WIDE, vmem=VMEM_BIG)


def _memkv_fwd(mem, g_kv, w_mk, w_mv):
    def body(m_ref, g_ref, wk_ref, wv_ref, mn_ref, k_ref, v_ref):
        mn = _rms(m_ref[...], g_ref[...]).astype(BF16)
        mn_ref[...] = mn
        k_ref[...] = _nn(mn, _w(wk_ref)).astype(BF16)
        v_ref[...] = _nn(mn, _w(wv_ref)).astype(BF16)

    return _tok_call(body, "memkv_fwd", [mem], [g_kv, w_mk, w_mv],
                     [(D, BF16), (D, BF16), (D, BF16)], rows=NMEM, tm=NMEM, vmem=VMEM_BIG)


def _mem_fwd(qm, x1, km, vm, w_mo, g_post, g_fpre):
    def body(q_ref, x1_ref, k_ref, v_ref, wo_ref, gp_ref, gf_ref, om_ref, ym_ref, x2_ref, h3_ref):
        for h in range(MEM_HEADS):
            sl = slice(h * MEM_HD, (h + 1) * MEM_HD)
            s = _nt(q_ref[:, sl], k_ref[:, sl]) * MEM_SCALE
            p = jnp.exp(s - jnp.max(s, axis=-1, keepdims=True))
            p = p / jnp.sum(p, axis=-1, keepdims=True)
            om_ref[:, sl] = _nn(p.astype(BF16), v_ref[:, sl]).astype(BF16)
        ym = _nn(om_ref[...], _w(wo_ref))
        ym_ref[...] = ym
        x2 = x1_ref[...] + _rms(ym, gp_ref[...])
        x2_ref[...] = x2
        h3_ref[...] = _rms(x2, gf_ref[...]).astype(BF16)

    return _tok_call(body, "mem_fwd", [qm, x1], [km, vm, w_mo, g_post, g_fpre],
                     [(D, BF16), (D, F32), (D, F32), (D, BF16)], tm=TM_WIDE, vmem=VMEM_BIG)


def _ffn_fwd(h3, x2, tgt, w1_t, w2, g_post):
    def body(h_ref, x2_ref, t_ref, w1_ref, w2_ref, g_ref, a_ref, y_ref, dx_ref, loss_ref):
        @pl.when(pl.program_id(0) == 0)
        def _():
            loss_ref[...] = jnp.zeros_like(loss_ref)

        a = _nt(h_ref[...], _w(w1_ref))
        a_ref[...] = a.astype(BF16)
        r = jnp.square(jnp.maximum(a, 0.0)).astype(BF16)
        y = _nn(r, _w(w2_ref))
        y_ref[...] = y
        e = x2_ref[...] + _rms(y, g_ref[...]) - t_ref[...]
        dx_ref[...] = e * (1.0 / D)
        loss_ref[...] += 0.5 * jnp.sum(jnp.sum(e * e, axis=-1, keepdims=True) * (1.0 / D))

    return _tok_call(body, "ffn_fwd", [h3, x2, tgt], [w1_t, w2, g_post],
                     [(DFF, BF16), (D, F32), (D, F32)], [(8, 128)], vmem=VMEM_BIG)


def _tri(lower):
    r = lax.broadcasted_iota(jnp.int32, (128, 128), 0)
    c = lax.broadcasted_iota(jnp.int32, (128, 128), 1)
    return jnp.where(r >= c if lower else c >= r, 1.0, 0.0).astype(F32)


def _hdot(a, b):
    return jnp.dot(a, b, preferred_element_type=F32, precision=lax.Precision.HIGHEST)


def _gate_fwd(flog, b_pad):
    def body(f_ref, b_ref, c_ref):
        tri = _tri(True)

        def step(i, carry):
            rows = pl.ds(pl.multiple_of(i * 128, 128), 128)
            z = f_ref[rows, :] + b_ref[...]
            lf = jnp.minimum(z, 0.0) - jnp.log(1.0 + jnp.exp(-jnp.abs(z)))
            cb = _hdot(tri, lf) + carry
            c_ref[rows, :] = cb
            return cb[127:128, :]

        lax.fori_loop(0, T // 128, step, jnp.zeros((1, 128), F32))

    return _one_call(body, "gate_fwd", [flog, b_pad], [((T, 128), F32)])[0]


def _gate_bwd(dc, flog, b_pad):
    def body(dc_ref, f_ref, b_ref, df_ref, db_ref):
        tri = _tri(False)

        def step(j, carry):
            run, db = carry
            i = T // 128 - 1 - j
            rows = pl.ds(pl.multiple_of(i * 128, 128), 128)
            dcb = dc_ref[rows, :]
            rb = _hdot(tri, dcb) + run
            z = f_ref[rows, :] + b_ref[...]
            df = rb * (1.0 / (1.0 + jnp.exp(z)))
            df_ref[rows, :] = df.astype(BF16)
            return run + jnp.sum(dcb, axis=0, keepdims=True), db + jnp.sum(df, axis=0, keepdims=True)

        _, db = lax.fori_loop(0, T // 128, step, (jnp.zeros((1, 128), F32), jnp.zeros((1, 128), F32)))
        db_ref[...] = jnp.broadcast_to(db, (8, 128))

    return _one_call(body, "gate_bwd", [dc, flog, b_pad], [((T, 128), BF16), ((8, 128), F32)])


def _lane_lo(rows=TQ):
    return lax.broadcasted_iota(jnp.int32, (rows, 128), 1) < HD


def _half(v, lo, a, scale=None):
    keep = lo if a == 0 else jnp.logical_not(lo)
    v = v.astype(F32) if scale is None else v.astype(F32) * scale
    return jnp.where(keep, v, 0.0).astype(BF16)


def _fox_specs():
    return [pl.BlockSpec((FQ, 128), lambda h, i: (i, h)),
            pl.BlockSpec((T, 128), lambda h, i: (0, 4 + h)),
            pl.BlockSpec((T, 128), lambda h, i: (0, 8 + h))]


def _fox_fwd(proj, c2, ct3):
    def body(q_ref, k_ref, v_ref, c_ref, ct_ref, o_ref, l_ref):
        i = pl.program_id(1)
        lo = _lane_lo(FQ)
        causal = lax.broadcasted_iota(jnp.int32, (FQ, FQ), 1) <= lax.broadcasted_iota(jnp.int32, (FQ, FQ), 0)
        q = q_ref[...]
        qs = [_half(q, lo, a, SCALE) for a in range(2)]
        cqs = [c_ref[:, 128 * a:128 * a + 1] for a in range(2)]

        def tile(off, carry, diagonal):
            kblk = k_ref[pl.ds(off, FQ), :]
            vblk = v_ref[pl.ds(off, FQ), :]
            new = []
            for a in range(2):
                m, l, acc = carry[a]
                s = _nt(qs[a], kblk) + (cqs[a] - ct_ref[a:a + 1, pl.ds(off, FQ)])
                if diagonal:
                    s = jnp.where(causal, s, NEG)
                m2 = jnp.maximum(m, jnp.max(s, axis=-1, keepdims=True))
                p = jnp.exp(s - m2)
                alpha = jnp.exp(m - m2)
                new.append((m2, alpha * l + jnp.sum(p, axis=-1, keepdims=True),
                            alpha * acc + _nn(p.astype(BF16), vblk)))
            return tuple(new)

        init = (jnp.full((FQ, 1), NEG, F32), jnp.zeros((FQ, 1), F32), jnp.zeros((FQ, 128), F32))
        carry = lax.fori_loop(0, i, lambda kb, c: tile(pl.multiple_of(kb * FQ, FQ), c, False), (init, init))
        carry = tile(pl.multiple_of(i * FQ, FQ), carry, True)
        outs = []
        for a in range(2):
            m, l, acc = carry[a]
            outs.append(acc / l)
            l_ref[:, 128 * a:128 * a + 128] = jnp.broadcast_to(m + jnp.log(l), (FQ, 128))
        o_ref[...] = jnp.where(lo, outs[0], outs[1])

    return _pcall(
        body, name="fox_fwd", grid=(4, T // FQ),
        in_specs=_fox_specs() + [pl.BlockSpec((FQ, 256), lambda h, i: (i, h)),
                                 pl.BlockSpec((None, 2, T), lambda h, i: (h, 0, 0))],
        out_specs=[pl.BlockSpec((FQ, 128), lambda h, i: (i, h)), pl.BlockSpec((FQ, 256), lambda h, i: (i, h))],
        out_shape=[jax.ShapeDtypeStruct((T, 512), F32), jax.ShapeDtypeStruct((T, 1024), F32)],
        compiler_params=pltpu.CompilerParams(dimension_semantics=("arbitrary", "arbitrary"), vmem_limit_bytes=VMEM_BIG),
    )(proj, proj, proj, c2, ct3)


def _fox_bwd(proj, c2, ct3, o, lse, do):
    def body(q_ref, k_ref, v_ref, c_ref, ct_ref, o_ref, l_ref, do_ref, dq_ref, dkb_ref, dvb_ref, dct_ref, dcq_ref,
             dk_ref, dv_ref):
        i = pl.program_id(1)

        @pl.when(i == 0)
        def _():
            dk_ref[...] = jnp.zeros_like(dk_ref)
            dv_ref[...] = jnp.zeros_like(dv_ref)
            dct_ref[...] = jnp.zeros_like(dct_ref)

        lo = _lane_lo(FQ)
        causal = lax.broadcasted_iota(jnp.int32, (FQ, FQ), 1) <= lax.broadcasted_iota(jnp.int32, (FQ, FQ), 0)
        q = q_ref[...]
        do_v = do_ref[...]
        prod = do_v * o_ref[...]
        qs = [_half(q, lo, a, SCALE) for a in range(2)]
        dos = [_half(do_v, lo, a) for a in range(2)]
        deltas = [jnp.sum(jnp.where(lo if a == 0 else jnp.logical_not(lo), prod, 0.0), axis=-1, keepdims=True)
                  for a in range(2)]
        cqs = [c_ref[:, 128 * a:128 * a + 1] for a in range(2)]
        las = [l_ref[:, 128 * a:128 * a + 1] for a in range(2)]

        def tile(off, carry, diagonal):
            kblk = k_ref[pl.ds(off, FQ), :]
            vblk = v_ref[pl.ds(off, FQ), :]
            new = []
            dk = jnp.zeros((FQ, 128), F32)
            dv = jnp.zeros((FQ, 128), F32)
            for a in range(2):
                dq_acc, rs = carry[a]
                s = _nt(qs[a], kblk) + (cqs[a] - ct_ref[a:a + 1, pl.ds(off, FQ)])
                if diagonal:
                    s = jnp.where(causal, s, NEG)
                p = jnp.exp(s - las[a])
                ds = p * (_nt(dos[a], vblk) - deltas[a])
                dsb = ds.astype(BF16)
                dk = dk + _tn(dsb, qs[a])
                dv = dv + _tn(p.astype(BF16), dos[a])
                dct_ref[a:a + 1, pl.ds(off, FQ)] -= jnp.sum(ds, axis=0, keepdims=True)
                new.append((dq_acc + _nn(dsb, kblk), rs + jnp.sum(ds, axis=-1, keepdims=True)))
            dk_ref[pl.ds(off, FQ), :] += dk
            dv_ref[pl.ds(off, FQ), :] += dv
            return tuple(new)

        init = (jnp.zeros((FQ, 128), F32), jnp.zeros((FQ, 1), F32))
        carry = lax.fori_loop(0, i, lambda kb, c: tile(pl.multiple_of(kb * FQ, FQ), c, False), (init, init))
        carry = tile(pl.multiple_of(i * FQ, FQ), carry, True)
        for a in range(2):
            dcq_ref[:, 128 * a:128 * a + 128] = jnp.broadcast_to(carry[a][1], (FQ, 128))
        dq_ref[...] = (jnp.where(lo, carry[0][0], carry[1][0]) * SCALE).astype(BF16)

        @pl.when(i == T // FQ - 1)
        def _():
            dkb_ref[...] = dk_ref[...].astype(BF16)
            dvb_ref[...] = dv_ref[...].astype(BF16)

    blk = pl.BlockSpec((FQ, 128), lambda h, i: (i, h))
    wide = pl.BlockSpec((FQ, 256), lambda h, i: (i, h))
    rows = pl.BlockSpec((None, 2, T), lambda h, i: (h, 0, 0))
    col = pl.BlockSpec((T, 128), lambda h, i: (0, h))
    return _pcall(
        body, name="fox_bwd", grid=(4, T // FQ),
        in_specs=_fox_specs() + [wide, rows, blk, wide, blk],
        out_specs=[blk, col, col, rows, wide],
        out_shape=[jax.ShapeDtypeStruct((T, 512), BF16), jax.ShapeDtypeStruct((T, 512), BF16),
                   jax.ShapeDtypeStruct((T, 512), BF16), jax.ShapeDtypeStruct((4, 2, T), F32),
                   jax.ShapeDtypeStruct((T, 1024), F32)],
        scratch_shapes=[pltpu.VMEM((T, 128), F32), pltpu.VMEM((T, 128), F32)],
        compiler_params=pltpu.CompilerParams(dimension_semantics=("arbitrary", "arbitrary"), vmem_limit_bytes=VMEM_BIG),
    )(proj, proj, proj, c2, ct3, o, lse, do)


def _rel_onehot():
    ridx = lax.broadcasted_iota(jnp.int32, (NREL_PAD, VW), 0)
    j = lax.broadcasted_iota(jnp.int32, (NREL_PAD, VW), 1)
    return jnp.where(ridx == jnp.clip(TQ + LEFT - 1 - j, -128, 128) + 128, 1.0, 0.0).astype(F32)


def _relvec_fwd(tbl):
    def body(t_ref, v_ref):
        v_ref[...] = _hdot(t_ref[...], _rel_onehot())

    return _one_call(body, "relvec_fwd", [tbl], [((8, VW), F32)])[0]


def _relvec_bwd(gv):
    def body(g_ref, t_ref):
        t_ref[...] = lax.dot_general(g_ref[...], _rel_onehot(), (((1,), (1,)), ((), ())),
                                     preferred_element_type=F32, precision=lax.Precision.HIGHEST)

    return _one_call(body, "relvec_bwd", [gv], [((8, NREL_PAD), F32)])[0]


def _chk_bias(vt_ref, a):
    vb = jnp.broadcast_to(vt_ref[a:a + 1, :], (TQ, VW))
    y = pltpu.roll(vb, VW - (TQ - 1), 1, stride=1, stride_axis=0)[:, :WIN]
    cr = lax.broadcasted_iota(jnp.int32, (TQ, WIN), 0) // 64
    cm = lax.broadcasted_iota(jnp.int32, (TQ, WIN), 1) // 64
    return jnp.where((cm >= cr) & (cm <= cr + 8), y, NEG)


def _chk_specs():
    return [pl.BlockSpec((TQ, 128), lambda h, i: (i, CHK0 // 128 + h)),
            pl.BlockSpec((T + LEFT, 128), lambda h, i: (0, h)),
            pl.BlockSpec((T + LEFT, 128), lambda h, i: (0, 4 + h)),
            pl.BlockSpec((None, 2, VW), lambda h, i: (h, 0, 0))]


def _chk_fwd(proj, kvp, vt3):
    def body(q_ref, k_ref, v_ref, vt_ref, o_ref, l_ref, bias_ref):
        i = pl.program_id(1)

        @pl.when(i == 0)
        def _():
            for a in range(2):
                bias_ref[a] = _chk_bias(vt_ref, a)

        lo = _lane_lo()
        off = pl.multiple_of(i * TQ, TQ)
        kw = k_ref[pl.ds(off, WIN), :]
        vw = v_ref[pl.ds(off, WIN), :]
        real = lax.broadcasted_iota(jnp.int32, (TQ, WIN), 1) + off >= LEFT
        q = q_ref[...]
        outs = []
        for a in range(2):
            s = jnp.where(real, _nt(_half(q, lo, a), kw) * SCALE + bias_ref[a], NEG)
            m = jnp.max(s, axis=-1, keepdims=True)
            p = jnp.exp(s - m)
            l = jnp.sum(p, axis=-1, keepdims=True)
            outs.append(_nn(p.astype(BF16), vw) / l)
            l_ref[:, 128 * a:128 * a + 128] = jnp.broadcast_to(m + jnp.log(l), (TQ, 128))
        o_ref[...] = jnp.where(lo, outs[0], outs[1])

    return _pcall(
        body, name="chk_fwd", grid=(4, T // TQ), in_specs=_chk_specs(),
        out_specs=[pl.BlockSpec((TQ, 128), lambda h, i: (i, h)), pl.BlockSpec((TQ, 256), lambda h, i: (i, h))],
        out_shape=[jax.ShapeDtypeStruct((T, 512), F32), jax.ShapeDtypeStruct((T, 1024), F32)],
        scratch_shapes=[pltpu.VMEM((2, TQ, WIN), F32)],
        compiler_params=pltpu.CompilerParams(dimension_semantics=("arbitrary", "arbitrary")),
    )(proj, kvp, kvp, vt3)


def _chk_bwd(proj, kvp, vt3, o, lse, do):
    nq = T // TQ

    def body(q_ref, k_ref, v_ref, vt_ref, o_ref, l_ref, do_ref, dq_ref, dkb_ref, dvb_ref, gv_ref, bias_ref, dsum_ref,
             dk_ref, dv_ref):
        i = pl.program_id(1)

        @pl.when(i == 0)
        def _():
            for a in range(2):
                bias_ref[a] = _chk_bias(vt_ref, a)
            dsum_ref[...] = jnp.zeros_like(dsum_ref)
            dk_ref[...] = jnp.zeros_like(dk_ref)
            dv_ref[...] = jnp.zeros_like(dv_ref)

        lo = _lane_lo()
        off = pl.multiple_of(i * TQ, TQ)
        kw = k_ref[pl.ds(off, WIN), :]
        vw = v_ref[pl.ds(off, WIN), :]
        real = lax.broadcasted_iota(jnp.int32, (TQ, WIN), 1) + off >= LEFT
        q = q_ref[...]
        do_v = do_ref[...]
        prod = do_v * o_ref[...]
        dqs = []
        for a in range(2):
            keep = lo if a == 0 else jnp.logical_not(lo)
            qa = _half(q, lo, a)
            doa = _half(do_v, lo, a)
            delta = jnp.sum(jnp.where(keep, prod, 0.0), axis=-1, keepdims=True)
            s = jnp.where(real, _nt(qa, kw) * SCALE + bias_ref[a], NEG)
            p = jnp.exp(s - l_ref[:, 128 * a:128 * a + 1])
            ds = p * (_nt(doa, vw) - delta)
            dsum_ref[a] += ds
            dsb = ds.astype(BF16)
            dk_ref[pl.ds(off, WIN), :] += _tn(dsb, qa) * SCALE
            dv_ref[pl.ds(off, WIN), :] += _tn(p.astype(BF16), doa)
            dqs.append(_nn(dsb, kw))
        dq_ref[...] = (jnp.where(lo, dqs[0], dqs[1]) * SCALE).astype(BF16)

        @pl.when(i == nq - 1)
        def _():
            dkb_ref[...] = dk_ref[LEFT:, :].astype(BF16)
            dvb_ref[...] = dv_ref[LEFT:, :].astype(BF16)
            rr = lax.broadcasted_iota(jnp.int32, (TQ, TQ), 0)
            cc = lax.broadcasted_iota(jnp.int32, (TQ, TQ), 1)
            flip = jnp.where(rr + cc == TQ - 1, 1.0, 0.0).astype(F32)
            for a in range(2):
                dpad = jnp.concatenate([dsum_ref[a], jnp.zeros((TQ, VW - WIN), F32)], axis=1)
                z = pltpu.roll(_hdot(flip, dpad), 0, 1, stride=1, stride_axis=0)
                gv_ref[a:a + 1, :] = jnp.sum(z, axis=0, keepdims=True)

    blk = pl.BlockSpec((TQ, 128), lambda h, i: (i, h))
    wide = pl.BlockSpec((TQ, 256), lambda h, i: (i, h))
    col = pl.BlockSpec((T, 128), lambda h, i: (0, h))
    return _pcall(
        body, name="chk_bwd", grid=(4, nq), in_specs=_chk_specs() + [blk, wide, blk],
        out_specs=[blk, col, col, pl.BlockSpec((None, 2, VW), lambda h, i: (h, 0, 0))],
        out_shape=[jax.ShapeDtypeStruct((T, 512), BF16), jax.ShapeDtypeStruct((T, 512), BF16),
                   jax.ShapeDtypeStruct((T, 512), BF16), jax.ShapeDtypeStruct((4, 2, VW), F32)],
        scratch_shapes=[pltpu.VMEM((2, TQ, WIN), F32), pltpu.VMEM((2, TQ, WIN), F32),
                        pltpu.VMEM((T + LEFT, 128), F32), pltpu.VMEM((T + LEFT, 128), F32)],
        compiler_params=pltpu.CompilerParams(dimension_semantics=("arbitrary", "arbitrary")),
    )(proj, kvp, kvp, vt3, o, lse, do)


def _zero_at_start(*refs):
    @pl.when(pl.program_id(0) == 0)
    def _():
        for r in refs:
            r[...] = jnp.zeros_like(r)


def _ffn_bwd(dx3, y3, x2, a, w1_t, w2, g_post, g_pre):
    def body(dx3_ref, y_ref, x2_ref, a_ref, w1_ref, w2_ref, gp_ref, gf_ref,
             dx2_ref, da_ref, dy_ref, r_ref, dgp_ref, dgf_ref):
        _zero_at_start(dgp_ref, dgf_ref)
        dx3_v = dx3_ref[...]
        dy, dgp = _rms_bwd(y_ref[...], gp_ref[...], dx3_v)
        dgp_ref[...] += dgp
        dyb = dy.astype(BF16)
        dy_ref[...] = dyb
        ra = jnp.maximum(a_ref[...].astype(F32), 0.0)
        r_ref[...] = jnp.square(ra).astype(BF16)
        da = (_nt(dyb, _w(w2_ref)) * (2.0 * ra)).astype(BF16)
        da_ref[...] = da
        dh, dgf = _rms_bwd(x2_ref[...], gf_ref[...], _nn(da, _w(w1_ref)))
        dgf_ref[...] += dgf
        dx2_ref[...] = dx3_v + dh

    return _tok_call(body, "ffn_bwd", [dx3, y3, x2, a], [w1_t, w2, g_post, g_pre],
                     [(D, F32), (DFF, BF16), (D, BF16), (DFF, BF16)], [(1, D), (1, D)], vmem=VMEM_BIG)


def _mem_bwd(dx2, ym, x1, qm, km, vm, w_mo, w_mq, g_post, g_pre):
    def body(dx2_ref, ym_ref, x1_ref, q_ref, k_ref, v_ref, wo_ref, wq_ref, gp_ref, gm_ref,
             dx1_ref, dym_ref, dq_ref, dk_ref, dv_ref, dgp_ref, dgm_ref, dom_ref):
        _zero_at_start(dk_ref, dv_ref, dgp_ref, dgm_ref)
        dx2_v = dx2_ref[...]
        dym, dgp = _rms_bwd(ym_ref[...], gp_ref[...], dx2_v)
        dgp_ref[...] += dgp
        dymb = dym.astype(BF16)
        dym_ref[...] = dymb
        dom_ref[...] = _nt(dymb, _w(wo_ref)).astype(BF16)
        for h in range(MEM_HEADS):
            sl = slice(h * MEM_HD, (h + 1) * MEM_HD)
            qh, kh, doh = q_ref[:, sl], k_ref[:, sl], dom_ref[:, sl]
            s = _nt(qh, kh) * MEM_SCALE
            p = jnp.exp(s - jnp.max(s, axis=-1, keepdims=True))
            p = p / jnp.sum(p, axis=-1, keepdims=True)
            dp = _nt(doh, v_ref[:, sl])
            ds = (p * (dp - jnp.sum(p * dp, axis=-1, keepdims=True))).astype(BF16)
            dq_ref[:, sl] = (_nn(ds, kh) * MEM_SCALE).astype(BF16)
            dk_ref[:, sl] += _tn(ds, qh) * MEM_SCALE
            dv_ref[:, sl] += _tn(p.astype(BF16), doh)
        dh, dgm = _rms_bwd(x1_ref[...], gm_ref[...], _nt(dq_ref[...], _w(wq_ref)))
        dgm_ref[...] += dgm
        dx1_ref[...] = dx2_v + dh

    in_specs = [pl.BlockSpec((TM, D), lambda i: (i, 0))] * 4
    in_specs += [_resident(a) for a in (km, vm, w_mo, w_mq, g_post, g_pre)]
    w_mo, w_mq = w_mo[0], w_mq[0]
    tiled = pl.BlockSpec((TM, D), lambda i: (i, 0))
    kv = pl.BlockSpec((NMEM, D), lambda i: (0, 0))
    vec = pl.BlockSpec((1, D), lambda i: (0, 0))
    return _pcall(
        body, name="mem_bwd", grid=(T // TM,), in_specs=in_specs,
        out_specs=[tiled, tiled, tiled, kv, kv, vec, vec],
        out_shape=[jax.ShapeDtypeStruct((T, D), F32), jax.ShapeDtypeStruct((T, D), BF16),
                   jax.ShapeDtypeStruct((T, D), BF16), jax.ShapeDtypeStruct((NMEM, D), F32),
                   jax.ShapeDtypeStruct((NMEM, D), F32), jax.ShapeDtypeStruct((1, D), F32),
                   jax.ShapeDtypeStruct((1, D), F32)],
        scratch_shapes=[pltpu.VMEM((TM, D), BF16)],
        compiler_params=pltpu.CompilerParams(dimension_semantics=("arbitrary",), vmem_limit_bytes=VMEM_BIG),
    )(dx2, ym, x1, qm, km, vm, w_mo, w_mq, g_post, g_pre)


def _memkv_bwd(dkm, dvm, mem, w_mk, w_mv):
    def body(dk_ref, dv_ref, m_ref, wk_ref, wv_ref, dg_ref):
        dmn = _nt(dk_ref[...].astype(BF16), _w(wk_ref)) + _nt(dv_ref[...].astype(BF16), _w(wv_ref))
        mv = m_ref[...]
        dg_ref[...] = jnp.sum(dmn * (mv * _rstd(mv)), axis=0, keepdims=True)

    return _one_call(body, "memkv_bwd", [dkm, dvm, mem, w_mk, w_mv], [((1, D), F32)], vmem=VMEM_BIG)[0]


def _postmix_bwd(dx1, z, o_f, o_c, w_out, g_post, g_fo, g_co):
    def body(dx1_ref, z_ref, of_ref, oc_ref, wo_ref, gp_ref, gfo_ref, gco_ref,
             dz_ref, dof_ref, doc_ref, dgp_ref, dgfo_ref, dgco_ref):
        _zero_at_start(dgp_ref, dgfo_ref, dgco_ref)
        dz, dgp = _rms_bwd(z_ref[...], gp_ref[...], dx1_ref[...])
        dgp_ref[...] += dgp
        dzb = dz.astype(BF16)
        dz_ref[...] = dzb
        dy = _nt(dzb, _w(wo_ref))
        dof, dgfo = _rms_bwd(of_ref[...], gfo_ref[...], dy[:, :512])
        doc, dgco = _rms_bwd(oc_ref[...], gco_ref[...], dy[:, 512:])
        dof_ref[...] = dof
        doc_ref[...] = doc
        dgfo_ref[...] += dgfo
        dgco_ref[...] += dgco

    return _tok_call(body, "postmix_bwd", [dx1, z, o_f, o_c], [w_out, g_post, g_fo, g_co],
                     [(D, BF16), (512, F32), (512, F32)], [(1, D), (1, 512), (1, 512)], tm=TM_WIDE, vmem=VMEM_BIG)


def _premix_bwd(dx1, x, pieces, win_t, g_pre):
    def body(dx1_ref, x_ref, *refs):
        piece_refs, (w_ref, g_ref, dx_ref, dp_ref, dg_ref) = refs[:len(pieces)], refs[len(pieces):]
        _zero_at_start(dg_ref)
        col = 0
        for p in piece_refs:
            dp_ref[:, col:col + p.shape[1]] = p[...]
            col += p.shape[1]
        dh, dg = _rms_bwd(x_ref[...], g_ref[...], _nn(dp_ref[...], w_ref[...]))
        dg_ref[...] += dg
        dx_ref[...] = dx1_ref[...] + dh

    return _tok_call(body, "premix_bwd", [dx1, x] + list(pieces), [win_t, g_pre], [(D, F32), (PROJ, BF16)], [(1, D)],
                     tm=TM_WIDE, vmem=VMEM_BIG)


def _wgrad(a, b, name):
    k, m = a.shape
    n = b.shape[1]
    tm = 640 if m % 640 == 0 and m > 1024 else min(m, 512)
    tn = min(n, 1024)

    def body(a_ref, b_ref, o_ref):
        o_ref[...] = _tn(a_ref[...].astype(BF16), b_ref[...].astype(BF16))

    return _pcall(
        body, name=name, grid=(m // tm, n // tn),
        in_specs=[pl.BlockSpec((k, tm), lambda i, j: (0, i)), pl.BlockSpec((k, tn), lambda i, j: (0, j))],
        out_specs=pl.BlockSpec((tm, tn), lambda i, j: (i, j)),
        out_shape=jax.ShapeDtypeStruct((m, n), F32),
        compiler_params=pltpu.CompilerParams(dimension_semantics=("arbitrary", "arbitrary"), vmem_limit_bytes=VMEM_BIG),
    )(a, b)


def _wgrad_group(name, pairs, rows):
    def body(*refs):
        o_ref = refs[-1]
        for k in range(len(pairs)):
            o_ref[k * rows:(k + 1) * rows, :] = _tn(refs[2 * k][...].astype(BF16), refs[2 * k + 1][...].astype(BF16))

    in_specs, ops = [], []
    for a, b in pairs:
        in_specs += [pl.BlockSpec((a.shape[0], rows), lambda j: (0, j)), _resident(b)]
        ops += [a, b]
    return _pcall(
        body, name=name, grid=(8,), in_specs=in_specs,
        out_specs=pl.BlockSpec((None, len(pairs) * rows, D), lambda j: (j, 0, 0)),
        out_shape=jax.ShapeDtypeStruct((8, len(pairs) * rows, D), F32),
        compiler_params=pltpu.CompilerParams(dimension_semantics=("arbitrary",), vmem_limit_bytes=VMEM_BIG),
    )(*ops)


def _adam_math(w, g, m, v):
    m2 = ADAM_B1 * m + (1.0 - ADAM_B1) * g
    v2 = ADAM_B2 * v + (1.0 - ADAM_B2) * jnp.square(g)
    m_hat = m2 / (1.0 - ADAM_B1 ** ADAM_STEP)
    v_hat = v2 / (1.0 - ADAM_B2 ** ADAM_STEP)
    delta = -ADAM_LR * (m_hat / (jnp.sqrt(v_hat) + ADAM_EPS) + ADAM_WD * w)
    return delta, m2, v2


def _adamw(w, g, m, v, name):
    rows, cols = w.shape
    tr = 256 if rows % 256 == 0 else rows

    def body(w_ref, g_ref, m_ref, v_ref, d_ref, m2_ref, v2_ref):
        d_ref[...], m2_ref[...], v2_ref[...] = _adam_math(w_ref[...], g_ref[...], m_ref[...], v_ref[...])

    spec = pl.BlockSpec((tr, cols), lambda i: (i, 0))
    return _pcall(
        body, name=name, grid=(rows // tr,), in_specs=[spec] * 4, out_specs=[spec] * 3,
        out_shape=[jax.ShapeDtypeStruct(w.shape, F32)] * 3,
        compiler_params=pltpu.CompilerParams(dimension_semantics=("arbitrary",)),
    )(w, g, m, v)


def _adamw_small(w, gparts, m, v):
    def body(w_ref, g_ref, m_ref, v_ref, gs_ref, d_ref, m2_ref, v2_ref):
        g = g_ref[0]
        for k in range(1, 8):
            g = g + g_ref[k]
        gs_ref[...] = g
        d_ref[...], m2_ref[...], v2_ref[...] = _adam_math(w_ref[...], g, m_ref[...], v_ref[...])

    return _one_call(body, "adamw_small", [w, gparts, m, v], [(w.shape, F32)] * 4)


def _row_tile(rows):
    return next(t for t in (512, 400, 320) if rows % t == 0)


def _add_halves(g4, theirs, core, name):
    rows = g4.shape[2]
    tr = _row_tile(rows)

    def body(c_ref, a_ref, b_ref, o_ref):
        o_ref[...] = (a_ref[...] + b_ref[...]).astype(BF16)

    grid_spec = pltpu.PrefetchScalarGridSpec(
        num_scalar_prefetch=1, grid=(4, rows // tr),
        in_specs=[pl.BlockSpec((None, None, tr, D), lambda j, i, c: (j, c[0], i, 0)),
                  pl.BlockSpec((None, None, tr, D), lambda j, i, c: (j, 0, i, 0))],
        out_specs=pl.BlockSpec((None, tr, D), lambda j, i, c: (j, i, 0)))
    return _pcall(
        body, name=name, grid_spec=grid_spec, out_shape=jax.ShapeDtypeStruct((4, rows, D), BF16),
        compiler_params=pltpu.CompilerParams(dimension_semantics=("arbitrary", "arbitrary")),
    )(core, g4, theirs)


def _sum_adam(own, got, order, r0, w, m, v, name, transposed=False):
    n = w.shape[1] if transposed else w.shape[0]
    tr = min(n, 256)

    def body(o_ref, a_ref, b_ref, c_ref, d_ref, w_ref, m_ref, v_ref, g_ref, dl_ref, m2_ref, v2_ref):
        f = lambda r: r[...].astype(F32)
        g = ((f(a_ref) + f(b_ref)) + f(c_ref)) + f(d_ref)
        g = g.T if transposed else g
        g_ref[...] = g
        dl_ref[...], m2_ref[...], v2_ref[...] = _adam_math(w_ref[...], g, m_ref[...], v_ref[...])

    slot = lambda k: pl.BlockSpec((None, tr, D), lambda i, o: (o[k], r0 // tr + i, 0))
    wspec = pl.BlockSpec((D, tr), lambda i, o: (0, i)) if transposed else pl.BlockSpec((tr, D), lambda i, o: (i, 0))
    grid_spec = pltpu.PrefetchScalarGridSpec(
        num_scalar_prefetch=1, grid=(n // tr,), in_specs=[slot(0), slot(1), slot(2), slot(3), wspec, wspec, wspec],
        out_specs=[wspec] * 4)
    return _pcall(
        body, name=name, grid_spec=grid_spec, out_shape=[jax.ShapeDtypeStruct(w.shape, F32)] * 4,
        compiler_params=pltpu.CompilerParams(dimension_semantics=("arbitrary",)),
    )(order, own, got, got, got, w, m, v)


def _sum_chips(own, got, order, name):
    rows = own.shape[1]
    tr = _row_tile(rows)

    def body(o_ref, a_ref, b_ref, c_ref, d_ref, out_ref):
        f = lambda r: r[...].astype(F32)
        out_ref[...] = ((f(a_ref) + f(b_ref)) + f(c_ref)) + f(d_ref)

    slot = lambda k: pl.BlockSpec((None, tr, D), lambda i, o: (o[k], i, 0))
    grid_spec = pltpu.PrefetchScalarGridSpec(
        num_scalar_prefetch=1, grid=(rows // tr,), in_specs=[slot(0), slot(1), slot(2), slot(3)],
        out_specs=pl.BlockSpec((tr, D), lambda i, o: (i, 0)))
    return _pcall(
        body, name=name, grid_spec=grid_spec, out_shape=jax.ShapeDtypeStruct((rows, D), F32),
        compiler_params=pltpu.CompilerParams(dimension_semantics=("arbitrary",)),
    )(order, own, got, got, got)


def _place():
    return lax.axis_index("x"), lax.axis_index("y"), lax.axis_index("c")


def _allgather(block, name):
    def body(x_ref, out_ref, token, send_sems, recv_sems, local_sem):
        token[...] = jnp.zeros_like(token)
        x, y, c = _place()
        me, sibling = (x, y, c), (x, y, 1 - c)
        chips = [(1 - x, y), (x, 1 - y), (1 - x, 1 - y)]

        def slot(px, py, pc):
            return out_ref.at[4 * px + 2 * py + pc]

        def copy(k, blk, to, src=None):
            return pltpu.make_async_remote_copy(
                src_ref=slot(*blk) if src is None else src, dst_ref=slot(*blk),
                send_sem=send_sems.at[k], recv_sem=recv_sems.at[k], device_id=to, device_id_type=MESH)

        mine = pltpu.make_async_copy(x_ref, slot(*me), local_sem)
        mine.start()
        first = [copy(0, me, sibling, src=x_ref)]
        first += [copy(1 + j, me, (*chip, c), src=x_ref) for j, chip in enumerate(chips)]
        for cp in first:
            cp.start()
        passed = [copy(4 + j, (*chip, c), sibling) for j, chip in enumerate(chips)]
        for j, chip in enumerate(chips):
            copy(1 + j, (*chip, c), me).wait_recv()
            passed[j].start()
        copy(0, sibling, me).wait_recv()
        for j, chip in enumerate(chips):
            copy(4 + j, (*chip, 1 - c), me).wait_recv()
        for cp in first + passed:
            cp.wait_send()
        mine.wait()

    return _pcall(
        body, name=name,
        out_shape=[jax.ShapeDtypeStruct((8,) + block.shape, block.dtype), jax.ShapeDtypeStruct((8, 128), F32)],
        in_specs=[pl.BlockSpec(memory_space=pl.ANY)],
        out_specs=[pl.BlockSpec(memory_space=pl.ANY), pl.BlockSpec(memory_space=pltpu.VMEM)],
        scratch_shapes=[pltpu.SemaphoreType.DMA((7,)), pltpu.SemaphoreType.DMA((7,)), pltpu.SemaphoreType.DMA(())],
        compiler_params=pltpu.CompilerParams(has_side_effects=True),
    )(block)


HBM_SPEC = pl.BlockSpec(memory_space=pltpu.HBM)
SEM_SPEC = pl.BlockSpec(memory_space=pltpu.SEMAPHORE)
ANY_SPEC = pl.BlockSpec(memory_space=pl.ANY)
EFFECT = pltpu.SideEffectType.DATAFLOW_SIDE_EFFECTING


def _in_hbm(a):
    return pltpu.with_memory_space_constraint(a, pltpu.HBM)


def _start_copies(name, src, land_shape, plan, n):
    def body(src_ref, land_ref, send_sems, recv_sems, src_thru, land_thru, token):
        for k, (s, d, to, _) in enumerate(plan(src_ref, land_ref)):
            pltpu.make_async_remote_copy(src_ref=s, dst_ref=d, send_sem=send_sems.at[k], recv_sem=recv_sems.at[k],
                                         device_id=to, device_id_type=MESH).start()
        token[...] = jnp.zeros_like(token)

    return _pcall(
        body, name=name,
        out_shape=(pltpu.SemaphoreType.DMA((n,)), pltpu.SemaphoreType.DMA((n,)), pltpu.HBM(src.shape, src.dtype),
                   pltpu.HBM(land_shape, src.dtype), jax.ShapeDtypeStruct((8, 128), F32)),
        in_specs=(HBM_SPEC, HBM_SPEC),
        out_specs=(SEM_SPEC, SEM_SPEC, HBM_SPEC, HBM_SPEC, pl.BlockSpec(memory_space=pltpu.VMEM)),
        input_output_aliases={0: 2, 1: 3}, compiler_params=pltpu.CompilerParams(has_side_effects=EFFECT),
    )(_in_hbm(src), _in_hbm(lax.empty(land_shape, src.dtype)))


def _wait_copies(name, started, after, plan):
    send_sems, recv_sems, src_thru, land_thru, _ = started

    def body(src_ref, land_ref, send_sems, recv_sems, *rest):
        for k, (s, _, to, mine) in enumerate(plan(src_ref, land_ref)):
            cp = pltpu.make_async_remote_copy(src_ref=s, dst_ref=mine, send_sem=send_sems.at[k],
                                              recv_sem=recv_sems.at[k], device_id=to, device_id_type=MESH)
            cp.wait_send()
            cp.wait_recv()

    return _pcall(
        body, name=name,
        out_shape=(pltpu.HBM(src_thru.shape, src_thru.dtype), pltpu.HBM(land_thru.shape, land_thru.dtype)),
        in_specs=(HBM_SPEC, HBM_SPEC, SEM_SPEC, SEM_SPEC) + (ANY_SPEC,) * len(after), out_specs=(HBM_SPEC, HBM_SPEC),
        input_output_aliases={0: 0, 1: 1}, compiler_params=pltpu.CompilerParams(has_side_effects=EFFECT),
    )(src_thru, land_thru, send_sems, recv_sems, *after)


def _gather_plan(src_ref, land_ref):
    x, y, c = _place()
    peers = [(x, y, 1 - c), (1 - x, y, c), (x, 1 - y, c), (1 - x, 1 - y, c)]
    return [(src_ref, land_ref.at[4 * x + 2 * y + c], p, land_ref.at[4 * p[0] + 2 * p[1] + p[2]]) for p in peers]


def _swap_plan(src_ref, land_ref):
    x, y, c = _place()
    return [(src_ref.at[:, pl.ds(1 - c, 1)], land_ref, (x, y, 1 - c), land_ref)]


def _exchange_plan(src_ref, land_ref):
    x, y, c = _place()
    chips = [(1 - x, y), (x, 1 - y), (1 - x, 1 - y)]
    return [(src_ref.at[2 * px + py], land_ref.at[2 * x + y], (px, py, c), land_ref.at[2 * px + py]) for px, py in chips]


def _gather_forward(land, block):
    def body(land_ref, out_ref, send_sems, recv_sems):
        x, y, c = _place()
        chips = [(1 - x, y), (x, 1 - y), (1 - x, 1 - y)]

        def copy(k, px, py, pc):
            blk = out_ref.at[4 * px + 2 * py + pc]
            return pltpu.make_async_remote_copy(src_ref=blk, dst_ref=blk, send_sem=send_sems.at[k],
                                                recv_sem=recv_sems.at[k], device_id=(x, y, 1 - c), device_id_type=MESH)

        sent = [copy(k, px, py, c) for k, (px, py) in enumerate(chips)]
        for cp in sent:
            cp.start()
        for k, (px, py) in enumerate(chips):
            copy(k, px, py, 1 - c).wait_recv()
        for cp in sent:
            cp.wait_send()

    land = _pcall(
        body, name="allgather_rest_forward", out_shape=jax.ShapeDtypeStruct(land.shape, land.dtype),
        in_specs=[ANY_SPEC], out_specs=ANY_SPEC, input_output_aliases={0: 0},
        scratch_shapes=[pltpu.SemaphoreType.DMA((3,)), pltpu.SemaphoreType.DMA((3,))],
        compiler_params=pltpu.CompilerParams(has_side_effects=True),
    )(land)

    rows = block.shape[0]
    tr = rows // 4

    def place(me_ref, x_ref, land_ref, out_ref):
        out_ref[...] = x_ref[...]

    x, y, c = _place()
    grid_spec = pltpu.PrefetchScalarGridSpec(
        num_scalar_prefetch=1, grid=(rows // tr,),
        in_specs=[pl.BlockSpec((tr, D), lambda i, me: (i, 0)), ANY_SPEC],
        out_specs=pl.BlockSpec((None, tr, D), lambda i, me: (me[0], i, 0)))
    return _pcall(
        place, name="allgather_rest_own", grid_spec=grid_spec, out_shape=jax.ShapeDtypeStruct(land.shape, land.dtype),
        input_output_aliases={2: 0}, compiler_params=pltpu.CompilerParams(dimension_semantics=("arbitrary",)),
    )((4 * x + 2 * y + c).reshape(1), block, land)


class _ReduceScatter:
    def __init__(self, name, g):
        self.name = name
        rows = g.shape[1]
        self.started = _start_copies(name + "_swap_start", g.reshape(4, 2, rows, D), (4, 1, rows, D), _swap_plan, 1)
        self.token = self.started[4][0, 0]

    def halfway(self, after):
        g4, theirs = _wait_copies(self.name + "_swap_wait", self.started, after, _swap_plan)
        self.own = _add_halves(g4, theirs, lax.axis_index("c").reshape(1), self.name + "_add_halves")
        self.started = _start_copies(self.name + "_exch_start", self.own, self.own.shape, _exchange_plan, 3)
        self.token = self.started[4][0, 0]

    def finish(self, after):
        own, got = _wait_copies(self.name + "_exch_wait", self.started, after, _exchange_plan)
        chip = 2 * lax.axis_index("x") + lax.axis_index("y")
        return own, got, (chip + jnp.arange(4, dtype=jnp.int32)) % 4


def _pack_small(p, scalar=None):
    z = lambda a, n: jnp.pad(a, ((0, 0), (0, n - a.shape[1])))
    rows = [z(p['rel_bias'], D), z(p['b_fgt'], D), jnp.concatenate([p['g_fox_out'], p['g_chk_out']], axis=1)]
    rows += [p[n] for n in ('g_mix_pre', 'g_mix_post', 'g_mem_kv', 'g_mem_pre', 'g_mem_post', 'g_ff_pre', 'g_ff_post')]
    rows.append(jnp.zeros((1, D), F32) if scalar is None else z(jnp.reshape(scalar, (1, 1)), D))
    rows.append(jnp.zeros((SMALL_ROWS - 18, D), F32))
    return jnp.concatenate(rows, axis=0)


def _unpack_small(a):
    out = {'rel_bias': a[0:8, :257], 'b_fgt': a[8:9, :8], 'g_fox_out': a[9:10, :512], 'g_chk_out': a[9:10, 512:]}
    for k, n in enumerate(('g_mix_pre', 'g_mix_post', 'g_mem_kv', 'g_mem_pre', 'g_mem_post', 'g_ff_pre', 'g_ff_post')):
        out[n] = a[10 + k:11 + k]
    return out


_GAP_DEV, _GAP_ROW = divmod(GATE0 + 8, N_IN)
_GAP = CHK0 - GATE0 - 8


def _in_rows_to_proj(g):
    wt = g[:, :N_IN].reshape(8 * N_IN, D)
    return jnp.concatenate([wt[:GATE0], jnp.pad(wt[GATE0:GATE0 + 8], ((0, _GAP), (0, 0))), wt[GATE0 + 8:]], axis=0)


def _proj_rows_to_in(g):
    pad = lambda a: jnp.pad(a, ((0, R_IN - a.shape[0]), (0, 0)))
    lo = N_IN * _GAP_DEV
    shards = [pad(g[N_IN * j:N_IN * (j + 1)]) for j in range(_GAP_DEV)]
    shards.append(pad(jnp.concatenate([g[lo:lo + _GAP_ROW], g[lo + _GAP_ROW + _GAP:lo + N_IN + _GAP]], axis=0)))
    shards += [pad(g[N_IN * j + _GAP:N_IN * (j + 1) + _GAP]) for j in range(_GAP_DEV + 1, 8)]
    return jnp.stack(shards)


def _local_grads(x, mem, tgt, win_t, gw_of, sm, on_grads):
    b_pad = jnp.pad(sm['b_fgt'], ((0, 0), (0, 120)))
    tbl = jnp.pad(sm['rel_bias'], ((0, 0), (0, NREL_PAD - 257)))

    h1, proj, flog = _premix_fwd(x, sm['g_mix_pre'], win_t)
    c = _gate_fwd(flog, b_pad)
    c8 = c[:, :8]
    c2 = jnp.repeat(c8, 128, axis=1)
    ct3 = c8.T.reshape(4, 2, T)
    o_f, lse_f = _fox_fwd(proj, c2, ct3)
    vt3 = _relvec_fwd(tbl).reshape(4, 2, VW)
    kvp = jnp.pad(proj[:, CHK0 + 512:], ((LEFT, 0), (0, 0)))
    o_c, lse_c = _chk_fwd(proj, kvp, vt3)
    gw = gw_of([o_f, o_c])
    w_out, w_mq, w_mk, w_mv, w_mo, w1_t, w2 = (_wblk(gw, n) for n in ('w_out', 'w_mq', 'w_mk', 'w_mv', 'w_mo', 'w_ff1', 'w_ff2'))
    ycat, z, x1, h2, qm = _postmix_fwd(x, o_f, o_c, sm['g_fox_out'], sm['g_chk_out'], w_out,
                                       sm['g_mix_post'], sm['g_mem_pre'], w_mq)
    memn, km, vm = _memkv_fwd(mem, sm['g_mem_kv'], w_mk, w_mv)
    om, ym, x2, h3 = _mem_fwd(qm, x1, km, vm, w_mo, sm['g_mem_post'], sm['g_ff_pre'])
    a, y3, dx3, loss_acc = _ffn_fwd(h3, x2, tgt, w1_t, w2, sm['g_ff_post'])

    gs = {}
    dx2, da, dy3, r, gs['g_ff_post'], gs['g_ff_pre'] = _ffn_bwd(dx3, y3, x2, a, w1_t, w2, sm['g_ff_post'], sm['g_ff_pre'])
    zero = on_grads('A', _wgrad_group("wgrad_ff", [(da, h3), (r, dy3)], 512), None)
    dx1, dym, dqm, dkm, dvm, gs['g_mem_post'], gs['g_mem_pre'] = _mem_bwd(
        dx2, ym, x1, qm, km, vm, w_mo, w_mq, sm['g_mem_post'] + zero, sm['g_mem_pre'])
    zero = on_grads('A halfway', None, [dx1])
    gs['g_mem_kv'] = _memkv_bwd(dkm, dvm, mem, w_mk, w_mv)
    dz, dof, doc, gs['g_mix_post'], gs['g_fox_out'], gs['g_chk_out'] = _postmix_bwd(
        dx1, z, o_f, o_c, w_out, sm['g_mix_post'] + zero, sm['g_fox_out'], sm['g_chk_out'])
    zero = on_grads('B', _wgrad_group("wgrad_mem_out", [(ycat, dz), (h2, dqm), (memn, dkm), (memn, dvm), (om, dym)], 128), None)
    dq_f, dk_f, dv_f, dct, dcq = _fox_bwd(proj, c2, ct3 + zero, o_f, lse_f, dof)
    zero = on_grads('B halfway', None, [dq_f])
    dq_c, dk_c, dv_c, gv = _chk_bwd(proj, kvp, vt3 + zero, o_c, lse_c, doc)
    gs['rel_bias'] = _relvec_bwd(gv.reshape(8, VW))[:, :257]
    dc = jnp.pad(dct.reshape(8, T).T + dcq[:, ::128], ((0, 0), (0, 120)))
    dflog, db = _gate_bwd(dc, flog, b_pad)
    gs['b_fgt'] = db[0:1, :8]
    grad_x, dproj, gs['g_mix_pre'] = _premix_bwd(dx1, x, [dq_f, dk_f, dv_f, dflog, dq_c, dk_c, dv_c], win_t, sm['g_mix_pre'])
    on_grads('C', _proj_rows_to_in(_wgrad(dproj, h1, "wgrad_in")), None)
    return loss_acc[0, 0], grad_x, gs


def kernel(x, mem, w_in, b_fgt, rel_bias, g_fox_out, g_chk_out, w_out, g_mix_pre, g_mix_post, g_mem_kv, w_mq, w_mk, w_mv, w_mo, g_mem_pre, g_mem_post, w_ff1, w_ff2, g_ff_pre, g_ff_post, loss_target, m_w_in, m_b_fgt, m_rel_bias, m_g_fox_out, m_g_chk_out, m_w_out, m_g_mix_pre, m_g_mix_post, m_g_mem_kv, m_w_mq, m_w_mk, m_w_mv, m_w_mo, m_g_mem_pre, m_g_mem_post, m_w_ff1, m_w_ff2, m_g_ff_pre, m_g_ff_post, v_w_in, v_b_fgt, v_rel_bias, v_g_fox_out, v_g_chk_out, v_w_out, v_g_mix_pre, v_g_mix_post, v_g_mem_kv, v_w_mq, v_w_mk, v_w_mv, v_w_mo, v_g_mem_pre, v_g_mem_post, v_w_ff1, v_w_ff2, v_g_ff_pre, v_g_ff_post):
    args = dict(locals())
    two_d = lambda a: a.reshape(a.shape[-2:])
    w = {n: two_d(args[n]) for n in WEIGHTS}
    m = {n: two_d(args['m_' + n]) for n in WEIGHTS}
    v = {n: two_d(args['v_' + n]) for n in WEIGHTS}

    sm = {n: w[n] for n in SMALL}
    shard_in = jnp.pad(w['w_in'].T, ((0, R_IN - N_IN), (0, 0))).astype(BF16)
    gathered_in, zero = _allgather(shard_in, "allgather_w_in")
    win_t = _in_rows_to_proj(gathered_in)
    shard_rest = (jnp.concatenate([w['w_ff1'].T, w['w_ff2'], w['w_out'], w['w_mq'], w['w_mk'], w['w_mv'], w['w_mo']],
                                  axis=0) + zero[0, 0]).astype(BF16)
    rest = _start_copies("allgather_rest_start", shard_rest, (8, R_REST, D), _gather_plan, 4)
    sm['g_mix_pre'] = sm['g_mix_pre'] + rest[4][0, 0]

    def gw_of(after):
        block, land = _wait_copies("allgather_rest_wait", rest, after, _gather_plan)
        return _gather_forward(land, block)

    rs = {}

    def on_grads(stage, g, after):
        if stage.endswith('halfway'):
            rs[stage[0]].halfway(after)
            return rs[stage[0]].token
        rs[stage] = _ReduceScatter("rs_" + stage.lower(), g)
        return rs[stage].token

    loss_local, grad_x, gs = _local_grads(x[0], mem[0], loss_target[0], win_t, gw_of, sm, on_grads)
    grads, deltas, new_m, new_v = {}, {}, {}, {}

    def update(n, out):
        grads[n], deltas[n], new_m[n], new_v[n] = out

    gparts, _ = _allgather(_pack_small(gs, loss_local + rs['C'].token), "allgather_small_grads")
    gsum, d_s, m_s, v_s = _adamw_small(_pack_small(sm), gparts, _pack_small({n: m[n] for n in SMALL}),
                                       _pack_small({n: v[n] for n in SMALL}))
    for dst, packed in ((grads, gsum), (deltas, d_s), (new_m, m_s), (new_v, v_s)):
        dst.update(_unpack_small(packed))
    loss = gsum[17, 0]
    rs['C'].halfway([gsum])

    own, got, order = rs['A'].finish([grad_x, rs['C'].started[4]])
    update('w_ff1', _sum_adam(own, got, order, 0, w['w_ff1'], m['w_ff1'], v['w_ff1'], "adamw_w_ff1", transposed=True))
    update('w_ff2', _sum_adam(own, got, order, 512, w['w_ff2'], m['w_ff2'], v['w_ff2'], "adamw_w_ff2"))
    own, got, order = rs['B'].finish([grad_x, rs['C'].started[4]])
    for k, n in enumerate(('w_out', 'w_mq', 'w_mk', 'w_mv', 'w_mo')):
        update(n, _sum_adam(own, got, order, 128 * k, w[n], m[n], v[n], "adamw_" + n))

    g_in = _sum_chips(*rs['C'].finish([new_v[n] for n in BIG if n != 'w_in']), "rs_c_sum_chips")[:N_IN].T
    update('w_in', (g_in,) + tuple(_adamw(w['w_in'], g_in, m['w_in'], v['w_in'], "adamw_w_in")))

    out = [loss, grad_x[None]]
    for group in (grads, deltas, new_m, new_v):
        out += [group[n].reshape(args[n].shape) for n in WEIGHTS]
    return tuple(out)
```

```python
import functools

import jax
import jax.numpy as jnp
from jax import lax
from jax.experimental import pallas as pl
from jax.experimental.pallas import tpu as pltpu

F32 = jnp.float32
BF16 = jnp.bfloat16
MESH = pl.DeviceIdType.MESH

T = 2048
D = 1024
NMEM = 256
DFF = 4096
EPS = 1e-6
TM = 256
TM_WIDE = 512
TQ = 256
FQ = 512
HD = 64
SCALE = HD ** -0.5
MEM_HEADS = 4
MEM_HD = 256
MEM_SCALE = MEM_HD ** -0.5
NEG = -1e30
LEFT = 512
WIN = LEFT + TQ
VW = 1024
NREL_PAD = 384
PROJ = 3200
GATE0 = 1536
CHK0 = 1664
VMEM_BIG = 56 * 1024 * 1024

ADAM_LR = 0.001
ADAM_B1 = 0.9
ADAM_B2 = 0.999
ADAM_EPS = 1e-08
ADAM_WD = 0.01
ADAM_STEP = 10

N_IN = 385
R_IN = 400
R_REST = 1664
W_ROWS = {'w_ff1': (0, 512), 'w_ff2': (512, 512),
          'w_out': (1024, 128), 'w_mq': (1152, 128), 'w_mk': (1280, 128), 'w_mv': (1408, 128), 'w_mo': (1536, 128)}
R_A, R_B = 1024, 640
SMALL_ROWS = 24
SMALL_SLOT = {'rel_bias': (0, 8, 0, 257), 'b_fgt': (8, 1, 0, 8), 'g_fox_out': (9, 1, 0, 512), 'g_chk_out': (9, 1, 512, 512),
              'g_mix_pre': (10, 1, 0, 1024), 'g_mix_post': (11, 1, 0, 1024), 'g_mem_kv': (12, 1, 0, 1024),
              'g_mem_pre': (13, 1, 0, 1024), 'g_mem_post': (14, 1, 0, 1024), 'g_ff_pre': (15, 1, 0, 1024),
              'g_ff_post': (16, 1, 0, 1024)}

WEIGHTS = ['w_in', 'b_fgt', 'rel_bias', 'g_fox_out', 'g_chk_out', 'w_out', 'g_mix_pre', 'g_mix_post', 'g_mem_kv',
           'w_mq', 'w_mk', 'w_mv', 'w_mo', 'g_mem_pre', 'g_mem_post', 'w_ff1', 'w_ff2', 'g_ff_pre', 'g_ff_post']
BIG = ['w_in', 'w_out', 'w_mq', 'w_mk', 'w_mv', 'w_mo', 'w_ff1', 'w_ff2']
SMALL = [n for n in WEIGHTS if n not in BIG]


def _pcall(body, **kw):
    return pl.pallas_call(body, **kw)


def _nn(a, b):
    return jnp.dot(a, b, preferred_element_type=F32)


def _nt(a, b):
    return lax.dot_general(a, b, (((1,), (1,)), ((), ())), preferred_element_type=F32)


def _tn(a, b):
    return lax.dot_general(a, b, (((0,), (0,)), ((), ())), preferred_element_type=F32)


def _w(ref):
    v = ref[...]
    return v if v.ndim == 2 else v.reshape(-1, v.shape[-1])


def _rstd(x):
    return lax.rsqrt(jnp.mean(x * x, axis=-1, keepdims=True) + EPS)


def _rms(x, g):
    return x * _rstd(x) * g


def _rms_bwd(x, g, dy):
    r = _rstd(x)
    xh = x * r
    dg = jnp.sum(dy * xh, axis=0, keepdims=True)
    dxh = dy * g
    dx = r * (dxh - xh * jnp.mean(dxh * xh, axis=-1, keepdims=True))
    return dx, dg


def _resident(a):
    if isinstance(a, tuple):
        _, shape, index = a
        return pl.BlockSpec(shape, lambda *_: index, pipeline_mode=pl.Buffered(1))
    return pl.BlockSpec(a.shape, lambda *_, nd=a.ndim: (0,) * nd, pipeline_mode=pl.Buffered(1))


def _wblk(gw, name):
    r0, rows = W_ROWS[name]
    return (gw, (8, rows, D), (0, r0 // rows, 0))


def _tok_call(body, name, tiled, full, outs_tiled, outs_acc=(), rows=T, tm=TM, vmem=None):
    in_specs = [pl.BlockSpec((tm, a.shape[1]), lambda i: (i, 0)) for a in tiled]
    in_specs += [_resident(a) for a in full]
    full = [a[0] if isinstance(a, tuple) else a for a in full]
    out_shape = [jax.ShapeDtypeStruct((rows, c), dt) for c, dt in outs_tiled]
    out_shape += [jax.ShapeDtypeStruct(s, F32) for s in outs_acc]
    out_specs = [pl.BlockSpec((tm, c), lambda i: (i, 0)) for c, _ in outs_tiled]
    out_specs += [pl.BlockSpec(s, lambda i, nd=len(s): (0,) * nd) for s in outs_acc]
    return _pcall(
        body, name=name, grid=(rows // tm,), in_specs=in_specs, out_specs=out_specs, out_shape=out_shape,
        compiler_params=pltpu.CompilerParams(dimension_semantics=("arbitrary",), vmem_limit_bytes=vmem),
    )(*tiled, *full)


def _one_call(body, name, ins, outs, vmem=None):
    whole = lambda s: pl.BlockSpec(s, lambda i, nd=len(s): (0,) * nd)
    return _pcall(
        body, name=name, grid=(1,), in_specs=[_resident(a) for a in ins], out_specs=[whole(s) for s, _ in outs],
        out_shape=[jax.ShapeDtypeStruct(s, dt) for s, dt in outs],
        compiler_params=pltpu.CompilerParams(dimension_semantics=("arbitrary",), vmem_limit_bytes=vmem),
    )(*[a[0] if isinstance(a, tuple) else a for a in ins])


def _premix_fwd(x, g_pre, win_t):
    def body(x_ref, g_ref, w_ref, h_ref, proj_ref, flog_ref):
        h = _rms(x_ref[...], g_ref[...]).astype(BF16)
        h_ref[...] = h
        p = _nt(h, w_ref[...])
        proj_ref[...] = p.astype(BF16)
        flog_ref[...] = p[:, GATE0:GATE0 + 128]

    return _tok_call(body, "premix_fwd", [x], [g_pre, win_t],
                     [(D, BF16), (PROJ, BF16), (128, F32)], tm=TM_WIDE, vmem=VMEM_BIG)


def _postmix_fwd(x, o_f, o_c, g_fo, g_co, w_out, g_post, g_mpre, w_mq):
    def body(x_ref, of_ref, oc_ref, gfo_ref, gco_ref, wo_ref, gp_ref, gm_ref, wq_ref,
             y_ref, z_ref, x1_ref, h2_ref, qm_ref):
        y_ref[:, :512] = _rms(of_ref[...], gfo_ref[...]).astype(BF16)
        y_ref[:, 512:] = _rms(oc_ref[...], gco_ref[...]).astype(BF16)
        z = _nn(y_ref[...], _w(wo_ref))
        z_ref[...] = z
        x1 = x_ref[...] + _rms(z, gp_ref[...])
        x1_ref[...] = x1
        h2 = _rms(x1, gm_ref[...]).astype(BF16)
        h2_ref[...] = h2
        qm_ref[...] = _nn(h2, _w(wq_ref)).astype(BF16)

    return _tok_call(body, "postmix_fwd", [x, o_f, o_c], [g_fo, g_co, w_out, g_post, g_mpre, w_mq],
                     [(D, BF16), (D, F32), (D, F32), (D, BF16), (D, BF16)], tm=TM_WIDE, vmem=VMEM_BIG)


def _memkv_fwd(mem, g_kv, w_mk, w_mv):
    def body(m_ref, g_ref, wk_ref, wv_ref, mn_ref, k_ref, v_ref):
        mn = _rms(m_ref[...], g_ref[...]).astype(BF16)
        mn_ref[...] = mn
        k_ref[...] = _nn(mn, _w(wk_ref)).astype(BF16)
        v_ref[...] = _nn(mn, _w(wv_ref)).astype(BF16)

    return _tok_call(body, "memkv_fwd", [mem], [g_kv, w_mk, w_mv],
                     [(D, BF16), (D, BF16), (D, BF16)], rows=NMEM, tm=NMEM, vmem=VMEM_BIG)


def _mem_fwd(qm, x1, km, vm, w_mo, g_post, g_fpre):
    def body(q_ref, x1_ref, k_ref, v_ref, wo_ref, gp_ref, gf_ref, om_ref, ym_ref, x2_ref, h3_ref):
        for h in range(MEM_HEADS):
            sl = slice(h * MEM_HD, (h + 1) * MEM_HD)
            s = _nt(q_ref[:, sl], k_ref[:, sl]) * MEM_SCALE
            p = jnp.exp(s - jnp.max(s, axis=-1, keepdims=True))
            p = p / jnp.sum(p, axis=-1, keepdims=True)
            om_ref[:, sl] = _nn(p.astype(BF16), v_ref[:, sl]).astype(BF16)
        ym = _nn(om_ref[...], _w(wo_ref))
        ym_ref[...] = ym
        x2 = x1_ref[...] + _rms(ym, gp_ref[...])
        x2_ref[...] = x2
        h3_ref[...] = _rms(x2, gf_ref[...]).astype(BF16)

    return _tok_call(body, "mem_fwd", [qm, x1], [km, vm, w_mo, g_post, g_fpre],
                     [(D, BF16), (D, F32), (D, F32), (D, BF16)], tm=TM_WIDE, vmem=VMEM_BIG)


def _ffn_fwd(h3, x2, tgt, w1_t, w2, g_post):
    def body(h_ref, x2_ref, t_ref, w1_ref, w2_ref, g_ref, a_ref, y_ref, dx_ref, loss_ref):
        @pl.when(pl.program_id(0) == 0)
        def _():
            loss_ref[...] = jnp.zeros_like(loss_ref)

        a = _nt(h_ref[...], _w(w1_ref))
        a_ref[...] = a.astype(BF16)
        r = jnp.square(jnp.maximum(a, 0.0)).astype(BF16)
        y = _nn(r, _w(w2_ref))
        y_ref[...] = y
        e = x2_ref[...] + _rms(y, g_ref[...]) - t_ref[...]
        dx_ref[...] = e * (1.0 / D)
        loss_ref[...] += 0.5 * jnp.sum(jnp.sum(e * e, axis=-1, keepdims=True) * (1.0 / D))

    return _tok_call(body, "ffn_fwd", [h3, x2, tgt], [w1_t, w2, g_post],
                     [(DFF, BF16), (D, F32), (D, F32)], [(8, 128)], vmem=VMEM_BIG)


def _tri(lower):
    r = lax.broadcasted_iota(jnp.int32, (128, 128), 0)
    c = lax.broadcasted_iota(jnp.int32, (128, 128), 1)
    return jnp.where(r >= c if lower else c >= r, 1.0, 0.0).astype(F32)


def _hdot(a, b):
    return jnp.dot(a, b, preferred_element_type=F32, precision=lax.Precision.HIGHEST)


def _gate_fwd(flog, b_pad):
    def body(f_ref, b_ref, c_ref):
        tri = _tri(True)

        def step(i, carry):
            rows = pl.ds(pl.multiple_of(i * 128, 128), 128)
            z = f_ref[rows, :] + b_ref[...]
            lf = jnp.minimum(z, 0.0) - jnp.log(1.0 + jnp.exp(-jnp.abs(z)))
            cb = _hdot(tri, lf) + carry
            c_ref[rows, :] = cb
            return cb[127:128, :]

        lax.fori_loop(0, T // 128, step, jnp.zeros((1, 128), F32))

    return _one_call(body, "gate_fwd", [flog, b_pad], [((T, 128), F32)])[0]


def _gate_bwd(dc, flog, b_pad):
    def body(dc_ref, f_ref, b_ref, df_ref, db_ref):
        tri = _tri(False)

        def step(j, carry):
            run, db = carry
            i = T // 128 - 1 - j
            rows = pl.ds(pl.multiple_of(i * 128, 128), 128)
            dcb = dc_ref[rows, :]
            rb = _hdot(tri, dcb) + run
            z = f_ref[rows, :] + b_ref[...]
            df = rb * (1.0 / (1.0 + jnp.exp(z)))
            df_ref[rows, :] = df.astype(BF16)
            return run + jnp.sum(dcb, axis=0, keepdims=True), db + jnp.sum(df, axis=0, keepdims=True)

        _, db = lax.fori_loop(0, T // 128, step, (jnp.zeros((1, 128), F32), jnp.zeros((1, 128), F32)))
        db_ref[...] = jnp.broadcast_to(db, (8, 128))

    return _one_call(body, "gate_bwd", [dc, flog, b_pad], [((T, 128), BF16), ((8, 128), F32)])


def _lane_lo(rows=TQ):
    return lax.broadcasted_iota(jnp.int32, (rows, 128), 1) < HD


def _half(v, lo, a, scale=None):
    keep = lo if a == 0 else jnp.logical_not(lo)
    v = v.astype(F32) if scale is None else v.astype(F32) * scale
    return jnp.where(keep, v, 0.0).astype(BF16)


def _fox_specs():
    return [pl.BlockSpec((FQ, 128), lambda h, i: (i, h)),
            pl.BlockSpec((T, 128), lambda h, i: (0, 4 + h)),
            pl.BlockSpec((T, 128), lambda h, i: (0, 8 + h))]


def _fox_fwd(proj, c2, ct3):
    def body(q_ref, k_ref, v_ref, c_ref, ct_ref, o_ref, l_ref):
        i = pl.program_id(1)
        lo = _lane_lo(FQ)
        causal = lax.broadcasted_iota(jnp.int32, (FQ, FQ), 1) <= lax.broadcasted_iota(jnp.int32, (FQ, FQ), 0)
        q = q_ref[...]
        qs = [_half(q, lo, a, SCALE) for a in range(2)]
        cqs = [c_ref[:, 128 * a:128 * a + 1] for a in range(2)]

        def tile(off, carry, diagonal):
            kblk = k_ref[pl.ds(off, FQ), :]
            vblk = v_ref[pl.ds(off, FQ), :]
            new = []
            for a in range(2):
                m, l, acc = carry[a]
                s = _nt(qs[a], kblk) + (cqs[a] - ct_ref[a:a + 1, pl.ds(off, FQ)])
                if diagonal:
                    s = jnp.where(causal, s, NEG)
                m2 = jnp.maximum(m, jnp.max(s, axis=-1, keepdims=True))
                p = jnp.exp(s - m2)
                alpha = jnp.exp(m - m2)
                new.append((m2, alpha * l + jnp.sum(p, axis=-1, keepdims=True),
                            alpha * acc + _nn(p.astype(BF16), vblk)))
            return tuple(new)

        init = (jnp.full((FQ, 1), NEG, F32), jnp.zeros((FQ, 1), F32), jnp.zeros((FQ, 128), F32))
        carry = lax.fori_loop(0, i, lambda kb, c: tile(pl.multiple_of(kb * FQ, FQ), c, False), (init, init))
        carry = tile(pl.multiple_of(i * FQ, FQ), carry, True)
        outs = []
        for a in range(2):
            m, l, acc = carry[a]
            outs.append(acc / l)
            l_ref[:, 128 * a:128 * a + 128] = jnp.broadcast_to(m + jnp.log(l), (FQ, 128))
        o_ref[...] = jnp.where(lo, outs[0], outs[1])

    return _pcall(
        body, name="fox_fwd", grid=(4, T // FQ),
        in_specs=_fox_specs() + [pl.BlockSpec((FQ, 256), lambda h, i: (i, h)),
                                 pl.BlockSpec((None, 2, T), lambda h, i: (h, 0, 0))],
        out_specs=[pl.BlockSpec((FQ, 128), lambda h, i: (i, h)), pl.BlockSpec((FQ, 256), lambda h, i: (i, h))],
        out_shape=[jax.ShapeDtypeStruct((T, 512), F32), jax.ShapeDtypeStruct((T, 1024), F32)],
        compiler_params=pltpu.CompilerParams(dimension_semantics=("arbitrary", "arbitrary"), vmem_limit_bytes=VMEM_BIG),
    )(proj, proj, proj, c2, ct3)


def _fox_bwd(proj, c2, ct3, o, lse, do):
    def body(q_ref, k_ref, v_ref, c_ref, ct_ref, o_ref, l_ref, do_ref, dq_ref, dkb_ref, dvb_ref, dct_ref, dcq_ref,
             dk_ref, dv_ref):
        i = pl.program_id(1)

        @pl.when(i == 0)
        def _():
            dk_ref[...] = jnp.zeros_like(dk_ref)
            dv_ref[...] = jnp.zeros_like(dv_ref)
            dct_ref[...] = jnp.zeros_like(dct_ref)

        lo = _lane_lo(FQ)
        causal = lax.broadcasted_iota(jnp.int32, (FQ, FQ), 1) <= lax.broadcasted_iota(jnp.int32, (FQ, FQ), 0)
        q = q_ref[...]
        do_v = do_ref[...]
        prod = do_v * o_ref[...]
        qs = [_half(q, lo, a, SCALE) for a in range(2)]
        dos = [_half(do_v, lo, a) for a in range(2)]
        deltas = [jnp.sum(jnp.where(lo if a == 0 else jnp.logical_not(lo), prod, 0.0), axis=-1, keepdims=True)
                  for a in range(2)]
        cqs = [c_ref[:, 128 * a:128 * a + 1] for a in range(2)]
        las = [l_ref[:, 128 * a:128 * a + 1] for a in range(2)]

        def tile(off, carry, diagonal):
            kblk = k_ref[pl.ds(off, FQ), :]
            vblk = v_ref[pl.ds(off, FQ), :]
            new = []
            dk = jnp.zeros((FQ, 128), F32)
            dv = jnp.zeros((FQ, 128), F32)
            for a in range(2):
                dq_acc, rs = carry[a]
                s = _nt(qs[a], kblk) + (cqs[a] - ct_ref[a:a + 1, pl.ds(off, FQ)])
                if diagonal:
                    s = jnp.where(causal, s, NEG)
                p = jnp.exp(s - las[a])
                ds = p * (_nt(dos[a], vblk) - deltas[a])
                dsb = ds.astype(BF16)
                dk = dk + _tn(dsb, qs[a])
                dv = dv + _tn(p.astype(BF16), dos[a])
                dct_ref[a:a + 1, pl.ds(off, FQ)] -= jnp.sum(ds, axis=0, keepdims=True)
                new.append((dq_acc + _nn(dsb, kblk), rs + jnp.sum(ds, axis=-1, keepdims=True)))
            dk_ref[pl.ds(off, FQ), :] += dk
            dv_ref[pl.ds(off, FQ), :] += dv
            return tuple(new)

        init = (jnp.zeros((FQ, 128), F32), jnp.zeros((FQ, 1), F32))
        carry = lax.fori_loop(0, i, lambda kb, c: tile(pl.multiple_of(kb * FQ, FQ), c, False), (init, init))
        carry = tile(pl.multiple_of(i * FQ, FQ), carry, True)
        for a in range(2):
            dcq_ref[:, 128 * a:128 * a + 128] = jnp.broadcast_to(carry[a][1], (FQ, 128))
        dq_ref[...] = (jnp.where(lo, carry[0][0], carry[1][0]) * SCALE).astype(BF16)

        @pl.when(i == T // FQ - 1)
        def _():
            dkb_ref[...] = dk_ref[...].astype(BF16)
            dvb_ref[...] = dv_ref[...].astype(BF16)

    blk = pl.BlockSpec((FQ, 128), lambda h, i: (i, h))
    wide = pl.BlockSpec((FQ, 256), lambda h, i: (i, h))
    rows = pl.BlockSpec((None, 2, T), lambda h, i: (h, 0, 0))
    col = pl.BlockSpec((T, 128), lambda h, i: (0, h))
    return _pcall(
        body, name="fox_bwd", grid=(4, T // FQ),
        in_specs=_fox_specs() + [wide, rows, blk, wide, blk],
        out_specs=[blk, col, col, rows, wide],
        out_shape=[jax.ShapeDtypeStruct((T, 512), BF16), jax.ShapeDtypeStruct((T, 512), BF16),
                   jax.ShapeDtypeStruct((T, 512), BF16), jax.ShapeDtypeStruct((4, 2, T), F32),
                   jax.ShapeDtypeStruct((T, 1024), F32)],
        scratch_shapes=[pltpu.VMEM((T, 128), F32), pltpu.VMEM((T, 128), F32)],
        compiler_params=pltpu.CompilerParams(dimension_semantics=("arbitrary", "arbitrary"), vmem_limit_bytes=VMEM_BIG),
    )(proj, proj, proj, c2, ct3, o, lse, do)


AUG_ROWSUM, AUG_COLSUM = 67, 64


def _fox_prep(proj, c2):
    def body(q_ref, k_ref, c_ref, qa_ref, ka_ref):
        lane = lax.broadcasted_iota(jnp.int32, (FQ, 128), 1)
        q = q_ref[...].astype(F32) * SCALE
        k = k_ref[...].astype(F32)
        for a in range(2):
            c = c_ref[:, 128 * a:128 * a + 128]
            hi = c.astype(BF16).astype(F32)
            mid = (c - hi).astype(BF16).astype(F32)
            lo = c - hi - mid
            parts = lambda first, sign: jnp.where(lane == first, sign * hi, jnp.where(lane == first + 1, sign * mid, sign * lo))
            qd = q if a == 0 else pltpu.roll(q, 64, 1)
            kd = k if a == 0 else pltpu.roll(k, 64, 1)
            ones = jnp.ones((FQ, 128), F32)
            qa = jnp.where(lane < 64, qd, jnp.where(lane < 67, ones, jnp.where(lane < 70, parts(67, 1.0), 0.0)))
            ka = jnp.where(lane < 64, kd, jnp.where(lane < 67, parts(64, -1.0), jnp.where(lane < 70, ones, 0.0)))
            qa_ref[:, 128 * a:128 * a + 128] = qa.astype(BF16)
            ka_ref[:, 128 * a:128 * a + 128] = ka.astype(BF16)

    wide = pl.BlockSpec((FQ, 256), lambda h, i: (i, h))
    return _pcall(
        body, name="fox_prep", grid=(4, T // FQ),
        in_specs=[pl.BlockSpec((FQ, 128), lambda h, i: (i, h)), pl.BlockSpec((FQ, 128), lambda h, i: (i, 4 + h)), wide],
        out_specs=[wide, wide], out_shape=[jax.ShapeDtypeStruct((T, 1024), BF16)] * 2,
        compiler_params=pltpu.CompilerParams(dimension_semantics=("arbitrary", "arbitrary")),
    )(proj, proj, c2)


def _fox_aug_specs():
    return [pl.BlockSpec((FQ, 256), lambda h, i: (i, h)), pl.BlockSpec((T, 256), lambda h, i: (0, h)),
            pl.BlockSpec((T, 128), lambda h, i: (0, 8 + h))]


def _causal():
    return lax.broadcasted_iota(jnp.int32, (FQ, FQ), 1) <= lax.broadcasted_iota(jnp.int32, (FQ, FQ), 0)


def _fox_fwd_aug(q_aug, k_aug, proj):
    def body(q_ref, k_ref, v_ref, o_ref, l_ref):
        i = pl.program_id(1)
        lo = _lane_lo(FQ)
        causal = _causal()
        qs = [q_ref[:, 128 * a:128 * a + 128] for a in range(2)]

        def tile(off, carry, diagonal):
            vblk = v_ref[pl.ds(off, FQ), :]
            new = []
            for a in range(2):
                m, l, acc = carry[a]
                s = _nt(qs[a], k_ref[pl.ds(off, FQ), 128 * a:128 * a + 128])
                if diagonal:
                    s = jnp.where(causal, s, NEG)
                m2 = jnp.maximum(m, jnp.max(s, axis=-1, keepdims=True))
                p = jnp.exp(s - m2)
                alpha = jnp.exp(m - m2)
                new.append((m2, alpha * l + jnp.sum(p, axis=-1, keepdims=True),
                            alpha * acc + _nn(p.astype(BF16), vblk)))
            return tuple(new)

        init = (jnp.full((FQ, 1), NEG, F32), jnp.zeros((FQ, 1), F32), jnp.zeros((FQ, 128), F32))
        carry = lax.fori_loop(0, i, lambda kb, c: tile(pl.multiple_of(kb * FQ, FQ), c, False), (init, init))
        carry = tile(pl.multiple_of(i * FQ, FQ), carry, True)
        outs = []
        for a in range(2):
            m, l, acc = carry[a]
            outs.append(acc / l)
            l_ref[:, 128 * a:128 * a + 128] = jnp.broadcast_to(m + jnp.log(l), (FQ, 128))
        o_ref[...] = jnp.where(lo, outs[0], outs[1])

    return _pcall(
        body, name="fox_fwd", grid=(4, T // FQ), in_specs=_fox_aug_specs(),
        out_specs=[pl.BlockSpec((FQ, 128), lambda h, i: (i, h)), pl.BlockSpec((FQ, 256), lambda h, i: (i, h))],
        out_shape=[jax.ShapeDtypeStruct((T, 512), F32), jax.ShapeDtypeStruct((T, 1024), F32)],
        compiler_params=pltpu.CompilerParams(dimension_semantics=("arbitrary", "arbitrary"), vmem_limit_bytes=VMEM_BIG),
    )(q_aug, k_aug, proj)


def _fox_bwd_aug(q_aug, k_aug, proj, o, lse, do, after):
    nq = T // FQ

    def body(q_ref, k_ref, v_ref, o_ref, l_ref, do_ref, after_ref, dq_ref, dkb_ref, dvb_ref, dcq_ref, dck_ref, dk_ref,
             dv_ref):
        i = pl.program_id(1)

        @pl.when(i == 0)
        def _():
            dk_ref[...] = jnp.zeros_like(dk_ref)
            dv_ref[...] = jnp.zeros_like(dv_ref)

        lo = _lane_lo(FQ)
        lane = lax.broadcasted_iota(jnp.int32, (FQ, 128), 1)
        causal = _causal()
        do_v = do_ref[...]
        prod = do_v * o_ref[...]
        qs = [q_ref[:, 128 * a:128 * a + 128] for a in range(2)]
        dos = [_half(do_v, lo, a) for a in range(2)]
        deltas = [jnp.sum(jnp.where(lo if a == 0 else jnp.logical_not(lo), prod, 0.0), axis=-1, keepdims=True)
                  for a in range(2)]
        las = [l_ref[:, 128 * a:128 * a + 1] for a in range(2)]

        def tile(off, carry, diagonal):
            vblk = v_ref[pl.ds(off, FQ), :]
            new = []
            dv = jnp.zeros((FQ, 128), F32)
            for a in range(2):
                kblk = k_ref[pl.ds(off, FQ), 128 * a:128 * a + 128]
                s = _nt(qs[a], kblk)
                if diagonal:
                    s = jnp.where(causal, s, NEG)
                p = jnp.exp(s - las[a])
                dsb = (p * (_nt(dos[a], vblk) - deltas[a])).astype(BF16)
                dk_ref[a, pl.ds(off, FQ), :] += _tn(dsb, qs[a])
                dv = dv + _tn(p.astype(BF16), dos[a])
                new.append(carry[a] + _nn(dsb, kblk))
            dv_ref[pl.ds(off, FQ), :] += dv
            return tuple(new)

        init = jnp.zeros((FQ, 128), F32)
        dqs = lax.fori_loop(0, i, lambda kb, c: tile(pl.multiple_of(kb * FQ, FQ), c, False), (init, init))
        dqs = tile(pl.multiple_of(i * FQ, FQ), dqs, True)
        pick = lambda x, at: jnp.sum(jnp.where(lane == at, x, 0.0), axis=-1, keepdims=True)
        for a in range(2):
            dcq_ref[:, 128 * a:128 * a + 128] = jnp.broadcast_to(pick(dqs[a], AUG_ROWSUM), (FQ, 128))
        dq_ref[...] = (jnp.where(lo, dqs[0], pltpu.roll(dqs[1], 64, 1)) * SCALE).astype(BF16)

        @pl.when(i == nq - 1)
        def _():
            big_lane = lax.broadcasted_iota(jnp.int32, (T, 128), 1)
            dkb_ref[...] = jnp.where(big_lane < 64, dk_ref[0], pltpu.roll(dk_ref[1], 64, 1)).astype(BF16)
            dvb_ref[...] = dv_ref[...].astype(BF16)
            for a in range(2):
                col = jnp.sum(jnp.where(big_lane == AUG_COLSUM, dk_ref[a], 0.0), axis=-1, keepdims=True)
                dck_ref[:, 128 * a:128 * a + 128] = jnp.broadcast_to(-col, (T, 128))

    blk = pl.BlockSpec((FQ, 128), lambda h, i: (i, h))
    wide = pl.BlockSpec((FQ, 256), lambda h, i: (i, h))
    col = pl.BlockSpec((T, 128), lambda h, i: (0, h))
    colwide = pl.BlockSpec((T, 256), lambda h, i: (0, h))
    return _pcall(
        body, name="fox_bwd", grid=(4, nq), in_specs=_fox_aug_specs() + [blk, wide, blk, ANY_SPEC],
        out_specs=[blk, col, col, wide, colwide],
        out_shape=[jax.ShapeDtypeStruct((T, 512), BF16)] * 3 + [jax.ShapeDtypeStruct((T, 1024), F32)] * 2,
        scratch_shapes=[pltpu.VMEM((2, T, 128), F32), pltpu.VMEM((T, 128), F32)],
        compiler_params=pltpu.CompilerParams(dimension_semantics=("arbitrary", "arbitrary"), vmem_limit_bytes=VMEM_BIG),
    )(q_aug, k_aug, proj, o, lse, do, after)


def _rel_onehot():
    ridx = lax.broadcasted_iota(jnp.int32, (NREL_PAD, VW), 0)
    j = lax.broadcasted_iota(jnp.int32, (NREL_PAD, VW), 1)
    return jnp.where(ridx == jnp.clip(TQ + LEFT - 1 - j, -128, 128) + 128, 1.0, 0.0).astype(F32)


def _relvec_fwd(tbl):
    def body(t_ref, v_ref):
        v_ref[...] = _hdot(t_ref[...], _rel_onehot())

    return _one_call(body, "relvec_fwd", [tbl], [((8, VW), F32)])[0]


def _relvec_bwd(gv):
    def body(g_ref, t_ref):
        t_ref[...] = lax.dot_general(g_ref[...], _rel_onehot(), (((1,), (1,)), ((), ())),
                                     preferred_element_type=F32, precision=lax.Precision.HIGHEST)

    return _one_call(body, "relvec_bwd", [gv], [((8, NREL_PAD), F32)])[0]


def _chk_bias(vt_ref, a):
    vb = jnp.broadcast_to(vt_ref[a:a + 1, :], (TQ, VW))
    y = pltpu.roll(vb, VW - (TQ - 1), 1, stride=1, stride_axis=0)[:, :WIN]
    cr = lax.broadcasted_iota(jnp.int32, (TQ, WIN), 0) // 64
    cm = lax.broadcasted_iota(jnp.int32, (TQ, WIN), 1) // 64
    return jnp.where((cm >= cr) & (cm <= cr + 8), y, NEG)


def _chk_specs():
    return [pl.BlockSpec((TQ, 128), lambda h, i: (i, CHK0 // 128 + h)),
            pl.BlockSpec((T + LEFT, 128), lambda h, i: (0, h)),
            pl.BlockSpec((T + LEFT, 128), lambda h, i: (0, 4 + h)),
            pl.BlockSpec((None, 2, VW), lambda h, i: (h, 0, 0))]


def _chk_fwd(proj, kvp, vt3):
    def body(q_ref, k_ref, v_ref, vt_ref, o_ref, l_ref, bias_ref):
        i = pl.program_id(1)

        @pl.when(i == 0)
        def _():
            for a in range(2):
                bias_ref[a] = _chk_bias(vt_ref, a)

        lo = _lane_lo()
        off = pl.multiple_of(i * TQ, TQ)
        kw = k_ref[pl.ds(off, WIN), :]
        vw = v_ref[pl.ds(off, WIN), :]
        real = lax.broadcasted_iota(jnp.int32, (TQ, WIN), 1) + off >= LEFT
        q = q_ref[...]
        outs = []
        for a in range(2):
            s = jnp.where(real, _nt(_half(q, lo, a), kw) * SCALE + bias_ref[a], NEG)
            m = jnp.max(s, axis=-1, keepdims=True)
            p = jnp.exp(s - m)
            l = jnp.sum(p, axis=-1, keepdims=True)
            outs.append(_nn(p.astype(BF16), vw) / l)
            l_ref[:, 128 * a:128 * a + 128] = jnp.broadcast_to(m + jnp.log(l), (TQ, 128))
        o_ref[...] = jnp.where(lo, outs[0], outs[1])

    return _pcall(
        body, name="chk_fwd", grid=(4, T // TQ), in_specs=_chk_specs(),
        out_specs=[pl.BlockSpec((TQ, 128), lambda h, i: (i, h)), pl.BlockSpec((TQ, 256), lambda h, i: (i, h))],
        out_shape=[jax.ShapeDtypeStruct((T, 512), F32), jax.ShapeDtypeStruct((T, 1024), F32)],
        scratch_shapes=[pltpu.VMEM((2, TQ, WIN), F32)],
        compiler_params=pltpu.CompilerParams(dimension_semantics=("arbitrary", "arbitrary")),
    )(proj, kvp, kvp, vt3)


def _chk_bwd(proj, kvp, vt3, o, lse, do):
    nq = T // TQ

    def body(q_ref, k_ref, v_ref, vt_ref, o_ref, l_ref, do_ref, dq_ref, dkb_ref, dvb_ref, gv_ref, bias_ref, dsum_ref,
             dk_ref, dv_ref):
        i = pl.program_id(1)

        @pl.when(i == 0)
        def _():
            for a in range(2):
                bias_ref[a] = _chk_bias(vt_ref, a)
            dsum_ref[...] = jnp.zeros_like(dsum_ref)
            dk_ref[...] = jnp.zeros_like(dk_ref)
            dv_ref[...] = jnp.zeros_like(dv_ref)

        lo = _lane_lo()
        off = pl.multiple_of(i * TQ, TQ)
        kw = k_ref[pl.ds(off, WIN), :]
        vw = v_ref[pl.ds(off, WIN), :]
        real = lax.broadcasted_iota(jnp.int32, (TQ, WIN), 1) + off >= LEFT
        q = q_ref[...]
        do_v = do_ref[...]
        prod = do_v * o_ref[...]
        dqs = []
        for a in range(2):
            keep = lo if a == 0 else jnp.logical_not(lo)
            qa = _half(q, lo, a)
            doa = _half(do_v, lo, a)
            delta = jnp.sum(jnp.where(keep, prod, 0.0), axis=-1, keepdims=True)
            s = jnp.where(real, _nt(qa, kw) * SCALE + bias_ref[a], NEG)
            p = jnp.exp(s - l_ref[:, 128 * a:128 * a + 1])
            ds = p * (_nt(doa, vw) - delta)
            dsum_ref[a] += ds
            dsb = ds.astype(BF16)
            dk_ref[pl.ds(off, WIN), :] += _tn(dsb, qa) * SCALE
            dv_ref[pl.ds(off, WIN), :] += _tn(p.astype(BF16), doa)
            dqs.append(_nn(dsb, kw))
        dq_ref[...] = (jnp.where(lo, dqs[0], dqs[1]) * SCALE).astype(BF16)

        @pl.when(i == nq - 1)
        def _():
            dkb_ref[...] = dk_ref[LEFT:, :].astype(BF16)
            dvb_ref[...] = dv_ref[LEFT:, :].astype(BF16)
            rr = lax.broadcasted_iota(jnp.int32, (TQ, TQ), 0)
            cc = lax.broadcasted_iota(jnp.int32, (TQ, TQ), 1)
            flip = jnp.where(rr + cc == TQ - 1, 1.0, 0.0).astype(F32)
            for a in range(2):
                dpad = jnp.concatenate([dsum_ref[a], jnp.zeros((TQ, VW - WIN), F32)], axis=1)
                z = pltpu.roll(_hdot(flip, dpad), 0, 1, stride=1, stride_axis=0)
                gv_ref[a:a + 1, :] = jnp.sum(z, axis=0, keepdims=True)

    blk = pl.BlockSpec((TQ, 128), lambda h, i: (i, h))
    wide = pl.BlockSpec((TQ, 256), lambda h, i: (i, h))
    col = pl.BlockSpec((T, 128), lambda h, i: (0, h))
    return _pcall(
        body, name="chk_bwd", grid=(4, nq), in_specs=_chk_specs() + [blk, wide, blk],
        out_specs=[blk, col, col, pl.BlockSpec((None, 2, VW), lambda h, i: (h, 0, 0))],
        out_shape=[jax.ShapeDtypeStruct((T, 512), BF16), jax.ShapeDtypeStruct((T, 512), BF16),
                   jax.ShapeDtypeStruct((T, 512), BF16), jax.ShapeDtypeStruct((4, 2, VW), F32)],
        scratch_shapes=[pltpu.VMEM((2, TQ, WIN), F32), pltpu.VMEM((2, TQ, WIN), F32),
                        pltpu.VMEM((T + LEFT, 128), F32), pltpu.VMEM((T + LEFT, 128), F32)],
        compiler_params=pltpu.CompilerParams(dimension_semantics=("arbitrary", "arbitrary")),
    )(proj, kvp, kvp, vt3, o, lse, do)


def _zero_at_start(*refs):
    @pl.when(pl.program_id(0) == 0)
    def _():
        for r in refs:
            r[...] = jnp.zeros_like(r)


def _ffn_bwd(dx3, y3, x2, a, w1_t, w2, g_post, g_pre):
    def body(dx3_ref, y_ref, x2_ref, a_ref, w1_ref, w2_ref, gp_ref, gf_ref,
             dx2_ref, da_ref, dy_ref, r_ref, dgp_ref, dgf_ref):
        _zero_at_start(dgp_ref, dgf_ref)
        dx3_v = dx3_ref[...]
        dy, dgp = _rms_bwd(y_ref[...], gp_ref[...], dx3_v)
        dgp_ref[...] += dgp
        dyb = dy.astype(BF16)
        dy_ref[...] = dyb
        ra = jnp.maximum(a_ref[...].astype(F32), 0.0)
        r_ref[...] = jnp.square(ra).astype(BF16)
        da = (_nt(dyb, _w(w2_ref)) * (2.0 * ra)).astype(BF16)
        da_ref[...] = da
        dh, dgf = _rms_bwd(x2_ref[...], gf_ref[...], _nn(da, _w(w1_ref)))
        dgf_ref[...] += dgf
        dx2_ref[...] = dx3_v + dh

    return _tok_call(body, "ffn_bwd", [dx3, y3, x2, a], [w1_t, w2, g_post, g_pre],
                     [(D, F32), (DFF, BF16), (D, BF16), (DFF, BF16)], [(1, D), (1, D)], vmem=VMEM_BIG)


def _mem_bwd(dx2, ym, x1, qm, km, vm, w_mo, w_mq, g_post, g_pre):
    def body(dx2_ref, ym_ref, x1_ref, q_ref, k_ref, v_ref, wo_ref, wq_ref, gp_ref, gm_ref,
             dx1_ref, dym_ref, dq_ref, dk_ref, dv_ref, dgp_ref, dgm_ref, dom_ref):
        _zero_at_start(dk_ref, dv_ref, dgp_ref, dgm_ref)
        dx2_v = dx2_ref[...]
        dym, dgp = _rms_bwd(ym_ref[...], gp_ref[...], dx2_v)
        dgp_ref[...] += dgp
        dymb = dym.astype(BF16)
        dym_ref[...] = dymb
        dom_ref[...] = _nt(dymb, _w(wo_ref)).astype(BF16)
        for h in range(MEM_HEADS):
            sl = slice(h * MEM_HD, (h + 1) * MEM_HD)
            qh, kh, doh = q_ref[:, sl], k_ref[:, sl], dom_ref[:, sl]
            s = _nt(qh, kh) * MEM_SCALE
            p = jnp.exp(s - jnp.max(s, axis=-1, keepdims=True))
            p = p / jnp.sum(p, axis=-1, keepdims=True)
            dp = _nt(doh, v_ref[:, sl])
            ds = (p * (dp - jnp.sum(p * dp, axis=-1, keepdims=True))).astype(BF16)
            dq_ref[:, sl] = (_nn(ds, kh) * MEM_SCALE).astype(BF16)
            dk_ref[:, sl] += _tn(ds, qh) * MEM_SCALE
            dv_ref[:, sl] += _tn(p.astype(BF16), doh)
        dh, dgm = _rms_bwd(x1_ref[...], gm_ref[...], _nt(dq_ref[...], _w(wq_ref)))
        dgm_ref[...] += dgm
        dx1_ref[...] = dx2_v + dh

    in_specs = [pl.BlockSpec((TM, D), lambda i: (i, 0))] * 4
    in_specs += [_resident(a) for a in (km, vm, w_mo, w_mq, g_post, g_pre)]
    w_mo, w_mq = w_mo[0], w_mq[0]
    tiled = pl.BlockSpec((TM, D), lambda i: (i, 0))
    kv = pl.BlockSpec((NMEM, D), lambda i: (0, 0))
    vec = pl.BlockSpec((1, D), lambda i: (0, 0))
    return _pcall(
        body, name="mem_bwd", grid=(T // TM,), in_specs=in_specs,
        out_specs=[tiled, tiled, tiled, kv, kv, vec, vec],
        out_shape=[jax.ShapeDtypeStruct((T, D), F32), jax.ShapeDtypeStruct((T, D), BF16),
                   jax.ShapeDtypeStruct((T, D), BF16), jax.ShapeDtypeStruct((NMEM, D), F32),
                   jax.ShapeDtypeStruct((NMEM, D), F32), jax.ShapeDtypeStruct((1, D), F32),
                   jax.ShapeDtypeStruct((1, D), F32)],
        scratch_shapes=[pltpu.VMEM((TM, D), BF16)],
        compiler_params=pltpu.CompilerParams(dimension_semantics=("arbitrary",), vmem_limit_bytes=VMEM_BIG),
    )(dx2, ym, x1, qm, km, vm, w_mo, w_mq, g_post, g_pre)


def _memkv_bwd(dkm, dvm, mem, w_mk, w_mv):
    def body(dk_ref, dv_ref, m_ref, wk_ref, wv_ref, dg_ref):
        dmn = _nt(dk_ref[...].astype(BF16), _w(wk_ref)) + _nt(dv_ref[...].astype(BF16), _w(wv_ref))
        mv = m_ref[...]
        dg_ref[...] = jnp.sum(dmn * (mv * _rstd(mv)), axis=0, keepdims=True)

    return _one_call(body, "memkv_bwd", [dkm, dvm, mem, w_mk, w_mv], [((1, D), F32)], vmem=VMEM_BIG)[0]


def _postmix_bwd(dx1, z, o_f, o_c, w_out, g_post, g_fo, g_co):
    def body(dx1_ref, z_ref, of_ref, oc_ref, wo_ref, gp_ref, gfo_ref, gco_ref,
             dz_ref, dof_ref, doc_ref, dgp_ref, dgfo_ref, dgco_ref):
        _zero_at_start(dgp_ref, dgfo_ref, dgco_ref)
        dz, dgp = _rms_bwd(z_ref[...], gp_ref[...], dx1_ref[...])
        dgp_ref[...] += dgp
        dzb = dz.astype(BF16)
        dz_ref[...] = dzb
        dy = _nt(dzb, _w(wo_ref))
        dof, dgfo = _rms_bwd(of_ref[...], gfo_ref[...], dy[:, :512])
        doc, dgco = _rms_bwd(oc_ref[...], gco_ref[...], dy[:, 512:])
        dof_ref[...] = dof
        doc_ref[...] = doc
        dgfo_ref[...] += dgfo
        dgco_ref[...] += dgco

    return _tok_call(body, "postmix_bwd", [dx1, z, o_f, o_c], [w_out, g_post, g_fo, g_co],
                     [(D, BF16), (512, F32), (512, F32)], [(1, D), (1, 512), (1, 512)], tm=TM_WIDE, vmem=VMEM_BIG)


def _premix_bwd(dx1, x, pieces, win_t, g_pre):
    def body(dx1_ref, x_ref, *refs):
        piece_refs, (w_ref, g_ref, dx_ref, dp_ref, dg_ref) = refs[:len(pieces)], refs[len(pieces):]
        _zero_at_start(dg_ref)
        col = 0
        for p in piece_refs:
            dp_ref[:, col:col + p.shape[1]] = p[...]
            col += p.shape[1]
        dh, dg = _rms_bwd(x_ref[...], g_ref[...], _nn(dp_ref[...], w_ref[...]))
        dg_ref[...] += dg
        dx_ref[...] = dx1_ref[...] + dh

    return _tok_call(body, "premix_bwd", [dx1, x] + list(pieces), [win_t, g_pre], [(D, F32), (PROJ, BF16)], [(1, D)],
                     tm=TM_WIDE, vmem=VMEM_BIG)


def _wgrad(a, b, name):
    k, m = a.shape
    n = b.shape[1]
    tm = 640 if m % 640 == 0 and m > 1024 else min(m, 512)
    tn = min(n, 1024)

    def body(a_ref, b_ref, o_ref):
        o_ref[...] = _tn(a_ref[...].astype(BF16), b_ref[...].astype(BF16))

    return _pcall(
        body, name=name, grid=(m // tm, n // tn),
        in_specs=[pl.BlockSpec((k, tm), lambda i, j: (0, i)), pl.BlockSpec((k, tn), lambda i, j: (0, j))],
        out_specs=pl.BlockSpec((tm, tn), lambda i, j: (i, j)),
        out_shape=jax.ShapeDtypeStruct((m, n), F32),
        compiler_params=pltpu.CompilerParams(dimension_semantics=("arbitrary", "arbitrary"), vmem_limit_bytes=VMEM_BIG),
    )(a, b)


def _wgrad_group(name, pairs, rows):
    def body(*refs):
        o_ref = refs[-1]
        for k in range(len(pairs)):
            o_ref[k * rows:(k + 1) * rows, :] = _tn(refs[2 * k][...].astype(BF16), refs[2 * k + 1][...].astype(BF16))

    in_specs, ops = [], []
    for a, b in pairs:
        in_specs += [pl.BlockSpec((a.shape[0], rows), lambda j: (0, j)), _resident(b)]
        ops += [a, b]
    return _pcall(
        body, name=name, grid=(8,), in_specs=in_specs,
        out_specs=pl.BlockSpec((None, len(pairs) * rows, D), lambda j: (j, 0, 0)),
        out_shape=jax.ShapeDtypeStruct((8, len(pairs) * rows, D), F32),
        compiler_params=pltpu.CompilerParams(dimension_semantics=("arbitrary",), vmem_limit_bytes=VMEM_BIG),
    )(*ops)


def _adam_math(w, g, m, v):
    m2 = ADAM_B1 * m + (1.0 - ADAM_B1) * g
    v2 = ADAM_B2 * v + (1.0 - ADAM_B2) * jnp.square(g)
    m_hat = m2 / (1.0 - ADAM_B1 ** ADAM_STEP)
    v_hat = v2 / (1.0 - ADAM_B2 ** ADAM_STEP)
    delta = -ADAM_LR * (m_hat / (jnp.sqrt(v_hat) + ADAM_EPS) + ADAM_WD * w)
    return delta, m2, v2


def _adamw(w, g, m, v, name):
    rows, cols = w.shape
    tr = 256 if rows % 256 == 0 else rows

    def body(w_ref, g_ref, m_ref, v_ref, d_ref, m2_ref, v2_ref):
        d_ref[...], m2_ref[...], v2_ref[...] = _adam_math(w_ref[...], g_ref[...], m_ref[...], v_ref[...])

    spec = pl.BlockSpec((tr, cols), lambda i: (i, 0))
    return _pcall(
        body, name=name, grid=(rows // tr,), in_specs=[spec] * 4, out_specs=[spec] * 3,
        out_shape=[jax.ShapeDtypeStruct(w.shape, F32)] * 3,
        compiler_params=pltpu.CompilerParams(dimension_semantics=("arbitrary",)),
    )(w, g, m, v)


def _adamw_small(gparts, ws, ms, vs):
    n = len(SMALL)

    def body(g_ref, *refs):
        w_refs, m_refs, v_refs = refs[:n], refs[n:2 * n], refs[2 * n:3 * n]
        outs, sum_ref = refs[3 * n:-1], refs[-1]
        g = g_ref[0]
        for k in range(1, 8):
            g = g + g_ref[k]
        sum_ref[...] = g
        outs[0][...] = sum_ref[17:18, 0:128]
        for t, name in enumerate(SMALL):
            r0, nr, c0, nc = SMALL_SLOT[name]
            gt = sum_ref[r0:r0 + nr, c0:c0 + nc]
            out = (gt,) + _adam_math(w_refs[t][...], gt, m_refs[t][...], v_refs[t][...])
            for o_ref, val in zip(outs[1 + 4 * t:5 + 4 * t], out):
                o_ref[...] = val

    whole = lambda s: pl.BlockSpec(s, lambda i, nd=len(s): (0,) * nd)
    ins = [gparts] + list(ws) + list(ms) + list(vs)
    out_shapes = [(1, 128)] + [a.shape for a in ws for _ in range(4)]
    return _pcall(
        body, name="adamw_small", grid=(1,), in_specs=[whole(a.shape) for a in ins],
        out_specs=[whole(s) for s in out_shapes], out_shape=[jax.ShapeDtypeStruct(s, F32) for s in out_shapes],
        scratch_shapes=[pltpu.VMEM((SMALL_ROWS, D), F32)],
        compiler_params=pltpu.CompilerParams(dimension_semantics=("arbitrary",)),
    )(*ins)


def _row_tile(rows):
    return next(t for t in (512, 400, 320) if rows % t == 0)


def _add_halves(g4, theirs, core, name):
    rows = g4.shape[2]
    tr = _row_tile(rows)

    def body(c_ref, a_ref, b_ref, o_ref):
        o_ref[...] = (a_ref[...] + b_ref[...]).astype(BF16)

    grid_spec = pltpu.PrefetchScalarGridSpec(
        num_scalar_prefetch=1, grid=(4, rows // tr),
        in_specs=[pl.BlockSpec((None, None, tr, D), lambda j, i, c: (j, c[0], i, 0)),
                  pl.BlockSpec((None, None, tr, D), lambda j, i, c: (j, 0, i, 0))],
        out_specs=pl.BlockSpec((None, tr, D), lambda j, i, c: (j, i, 0)))
    return _pcall(
        body, name=name, grid_spec=grid_spec, out_shape=jax.ShapeDtypeStruct((4, rows, D), BF16),
        compiler_params=pltpu.CompilerParams(dimension_semantics=("arbitrary", "arbitrary")),
    )(core, g4, theirs)


def _sum_adam(own, got, order, r0, w, m, v, name, transposed=False):
    n = w.shape[1] if transposed else w.shape[0]
    tr = min(n, 256)

    def body(o_ref, a_ref, b_ref, c_ref, d_ref, w_ref, m_ref, v_ref, g_ref, dl_ref, m2_ref, v2_ref):
        f = lambda r: r[...].astype(F32)
        g = ((f(a_ref) + f(b_ref)) + f(c_ref)) + f(d_ref)
        g = g.T if transposed else g
        g_ref[...] = g
        dl_ref[...], m2_ref[...], v2_ref[...] = _adam_math(w_ref[...], g, m_ref[...], v_ref[...])

    slot = lambda k: pl.BlockSpec((None, tr, D), lambda i, o: (o[k], r0 // tr + i, 0))
    wspec = pl.BlockSpec((D, tr), lambda i, o: (0, i)) if transposed else pl.BlockSpec((tr, D), lambda i, o: (i, 0))
    grid_spec = pltpu.PrefetchScalarGridSpec(
        num_scalar_prefetch=1, grid=(n // tr,), in_specs=[slot(0), slot(1), slot(2), slot(3), wspec, wspec, wspec],
        out_specs=[wspec] * 4)
    return _pcall(
        body, name=name, grid_spec=grid_spec, out_shape=[jax.ShapeDtypeStruct(w.shape, F32)] * 4,
        compiler_params=pltpu.CompilerParams(dimension_semantics=("arbitrary",)),
    )(order, own, got, got, got, w, m, v)


def _sum_chips(own, got, order, name):
    rows = own.shape[1]
    tr = _row_tile(rows)

    def body(o_ref, a_ref, b_ref, c_ref, d_ref, out_ref):
        f = lambda r: r[...].astype(F32)
        out_ref[...] = ((f(a_ref) + f(b_ref)) + f(c_ref)) + f(d_ref)

    slot = lambda k: pl.BlockSpec((None, tr, D), lambda i, o: (o[k], i, 0))
    grid_spec = pltpu.PrefetchScalarGridSpec(
        num_scalar_prefetch=1, grid=(rows // tr,), in_specs=[slot(0), slot(1), slot(2), slot(3)],
        out_specs=pl.BlockSpec((tr, D), lambda i, o: (i, 0)))
    return _pcall(
        body, name=name, grid_spec=grid_spec, out_shape=jax.ShapeDtypeStruct((rows, D), F32),
        compiler_params=pltpu.CompilerParams(dimension_semantics=("arbitrary",)),
    )(order, own, got, got, got)


def _place():
    return lax.axis_index("x"), lax.axis_index("y"), lax.axis_index("c")


def _allgather(block, name):
    def body(x_ref, out_ref, token, send_sems, recv_sems, local_sem):
        token[...] = jnp.zeros_like(token)
        x, y, c = _place()
        me, sibling = (x, y, c), (x, y, 1 - c)
        chips = [(1 - x, y), (x, 1 - y), (1 - x, 1 - y)]

        def slot(px, py, pc):
            return out_ref.at[4 * px + 2 * py + pc]

        def copy(k, blk, to, src=None):
            return pltpu.make_async_remote_copy(
                src_ref=slot(*blk) if src is None else src, dst_ref=slot(*blk),
                send_sem=send_sems.at[k], recv_sem=recv_sems.at[k], device_id=to, device_id_type=MESH)

        mine = pltpu.make_async_copy(x_ref, slot(*me), local_sem)
        mine.start()
        first = [copy(0, me, sibling, src=x_ref)]
        first += [copy(1 + j, me, (*chip, c), src=x_ref) for j, chip in enumerate(chips)]
        for cp in first:
            cp.start()
        passed = [copy(4 + j, (*chip, c), sibling) for j, chip in enumerate(chips)]
        for j, chip in enumerate(chips):
            copy(1 + j, (*chip, c), me).wait_recv()
            passed[j].start()
        copy(0, sibling, me).wait_recv()
        for j, chip in enumerate(chips):
            copy(4 + j, (*chip, 1 - c), me).wait_recv()
        for cp in first + passed:
            cp.wait_send()
        mine.wait()

    return _pcall(
        body, name=name,
        out_shape=[jax.ShapeDtypeStruct((8,) + block.shape, block.dtype), jax.ShapeDtypeStruct((8, 128), F32)],
        in_specs=[pl.BlockSpec(memory_space=pl.ANY)],
        out_specs=[pl.BlockSpec(memory_space=pl.ANY), pl.BlockSpec(memory_space=pltpu.VMEM)],
        scratch_shapes=[pltpu.SemaphoreType.DMA((7,)), pltpu.SemaphoreType.DMA((7,)), pltpu.SemaphoreType.DMA(())],
        compiler_params=pltpu.CompilerParams(has_side_effects=True),
    )(block)


HBM_SPEC = pl.BlockSpec(memory_space=pltpu.HBM)
SEM_SPEC = pl.BlockSpec(memory_space=pltpu.SEMAPHORE)
ANY_SPEC = pl.BlockSpec(memory_space=pl.ANY)
EFFECT = pltpu.SideEffectType.DATAFLOW_SIDE_EFFECTING


def _in_hbm(a):
    return pltpu.with_memory_space_constraint(a, pltpu.HBM)


def _start_copies(name, src, land_shape, plan, n):
    def body(src_ref, land_ref, send_sems, recv_sems, src_thru, land_thru, token):
        for k, (s, d, to, _) in enumerate(plan(src_ref, land_ref)):
            pltpu.make_async_remote_copy(src_ref=s, dst_ref=d, send_sem=send_sems.at[k], recv_sem=recv_sems.at[k],
                                         device_id=to, device_id_type=MESH).start()
        token[...] = jnp.zeros_like(token)

    return _pcall(
        body, name=name,
        out_shape=(pltpu.SemaphoreType.DMA((n,)), pltpu.SemaphoreType.DMA((n,)), pltpu.HBM(src.shape, src.dtype),
                   pltpu.HBM(land_shape, src.dtype), jax.ShapeDtypeStruct((8, 128), F32)),
        in_specs=(HBM_SPEC, HBM_SPEC),
        out_specs=(SEM_SPEC, SEM_SPEC, HBM_SPEC, HBM_SPEC, pl.BlockSpec(memory_space=pltpu.VMEM)),
        input_output_aliases={0: 2, 1: 3}, compiler_params=pltpu.CompilerParams(has_side_effects=EFFECT),
    )(_in_hbm(src), _in_hbm(lax.empty(land_shape, src.dtype)))


def _wait_copies(name, started, after, plan):
    send_sems, recv_sems, src_thru, land_thru, _ = started

    def body(src_ref, land_ref, send_sems, recv_sems, *rest):
        for k, (s, _, to, mine) in enumerate(plan(src_ref, land_ref)):
            cp = pltpu.make_async_remote_copy(src_ref=s, dst_ref=mine, send_sem=send_sems.at[k],
                                              recv_sem=recv_sems.at[k], device_id=to, device_id_type=MESH)
            cp.wait_send()
            cp.wait_recv()

    return _pcall(
        body, name=name,
        out_shape=(pltpu.HBM(src_thru.shape, src_thru.dtype), pltpu.HBM(land_thru.shape, land_thru.dtype)),
        in_specs=(HBM_SPEC, HBM_SPEC, SEM_SPEC, SEM_SPEC) + (ANY_SPEC,) * len(after), out_specs=(HBM_SPEC, HBM_SPEC),
        input_output_aliases={0: 0, 1: 1}, compiler_params=pltpu.CompilerParams(has_side_effects=EFFECT),
    )(src_thru, land_thru, send_sems, recv_sems, *after)


def _gather_plan(src_ref, land_ref):
    x, y, c = _place()
    peers = [(x, y, 1 - c), (1 - x, y, c), (x, 1 - y, c), (1 - x, 1 - y, c)]
    return [(src_ref, land_ref.at[4 * x + 2 * y + c], p, land_ref.at[4 * p[0] + 2 * p[1] + p[2]]) for p in peers]


def _swap_plan(src_ref, land_ref):
    x, y, c = _place()
    return [(src_ref.at[:, pl.ds(1 - c, 1)], land_ref, (x, y, 1 - c), land_ref)]


def _exchange_plan(src_ref, land_ref):
    x, y, c = _place()
    chips = [(1 - x, y), (x, 1 - y), (1 - x, 1 - y)]
    return [(src_ref.at[2 * px + py], land_ref.at[2 * x + y], (px, py, c), land_ref.at[2 * px + py]) for px, py in chips]


def _gather_forward(land, block):
    def body(land_ref, out_ref, send_sems, recv_sems):
        x, y, c = _place()
        chips = [(1 - x, y), (x, 1 - y), (1 - x, 1 - y)]

        def copy(k, px, py, pc):
            blk = out_ref.at[4 * px + 2 * py + pc]
            return pltpu.make_async_remote_copy(src_ref=blk, dst_ref=blk, send_sem=send_sems.at[k],
                                                recv_sem=recv_sems.at[k], device_id=(x, y, 1 - c), device_id_type=MESH)

        sent = [copy(k, px, py, c) for k, (px, py) in enumerate(chips)]
        for cp in sent:
            cp.start()
        for k, (px, py) in enumerate(chips):
            copy(k, px, py, 1 - c).wait_recv()
        for cp in sent:
            cp.wait_send()

    land = _pcall(
        body, name="allgather_rest_forward", out_shape=jax.ShapeDtypeStruct(land.shape, land.dtype),
        in_specs=[ANY_SPEC], out_specs=ANY_SPEC, input_output_aliases={0: 0},
        scratch_shapes=[pltpu.SemaphoreType.DMA((3,)), pltpu.SemaphoreType.DMA((3,))],
        compiler_params=pltpu.CompilerParams(has_side_effects=True),
    )(land)

    rows = block.shape[0]
    tr = rows // 4

    def place(me_ref, x_ref, land_ref, out_ref):
        out_ref[...] = x_ref[...]

    x, y, c = _place()
    grid_spec = pltpu.PrefetchScalarGridSpec(
        num_scalar_prefetch=1, grid=(rows // tr,),
        in_specs=[pl.BlockSpec((tr, D), lambda i, me: (i, 0)), ANY_SPEC],
        out_specs=pl.BlockSpec((None, tr, D), lambda i, me: (me[0], i, 0)))
    return _pcall(
        place, name="allgather_rest_own", grid_spec=grid_spec, out_shape=jax.ShapeDtypeStruct(land.shape, land.dtype),
        input_output_aliases={2: 0}, compiler_params=pltpu.CompilerParams(dimension_semantics=("arbitrary",)),
    )((4 * x + 2 * y + c).reshape(1), block, land)


class _ReduceScatter:
    def __init__(self, name, g):
        self.name = name
        rows = g.shape[1]
        self.started = _start_copies(name + "_swap_start", g.reshape(4, 2, rows, D), (4, 1, rows, D), _swap_plan, 1)
        self.token = self.started[4][0, 0]

    def halfway(self, after):
        g4, theirs = _wait_copies(self.name + "_swap_wait", self.started, after, _swap_plan)
        self.own = _add_halves(g4, theirs, lax.axis_index("c").reshape(1), self.name + "_add_halves")
        self.started = _start_copies(self.name + "_exch_start", self.own, self.own.shape, _exchange_plan, 3)
        self.token = self.started[4][0, 0]

    def finish(self, after):
        own, got = _wait_copies(self.name + "_exch_wait", self.started, after, _exchange_plan)
        chip = 2 * lax.axis_index("x") + lax.axis_index("y")
        return own, got, (chip + jnp.arange(4, dtype=jnp.int32)) % 4


def _pack_small(p, scalar=None):
    z = lambda a, n: jnp.pad(a, ((0, 0), (0, n - a.shape[1])))
    rows = [z(p['rel_bias'], D), z(p['b_fgt'], D), jnp.concatenate([p['g_fox_out'], p['g_chk_out']], axis=1)]
    rows += [p[n] for n in ('g_mix_pre', 'g_mix_post', 'g_mem_kv', 'g_mem_pre', 'g_mem_post', 'g_ff_pre', 'g_ff_post')]
    rows.append(jnp.zeros((1, D), F32) if scalar is None else z(jnp.reshape(scalar, (1, 1)), D))
    rows.append(jnp.zeros((SMALL_ROWS - 18, D), F32))
    return jnp.concatenate(rows, axis=0)


_GAP_DEV, _GAP_ROW = divmod(GATE0 + 8, N_IN)
_GAP = CHK0 - GATE0 - 8


def _in_rows_to_proj(g):
    wt = g[:, :N_IN].reshape(8 * N_IN, D)
    return jnp.concatenate([wt[:GATE0], jnp.pad(wt[GATE0:GATE0 + 8], ((0, _GAP), (0, 0))), wt[GATE0 + 8:]], axis=0)


def _proj_rows_to_in(g):
    pad = lambda a: jnp.pad(a, ((0, R_IN - a.shape[0]), (0, 0)))
    lo = N_IN * _GAP_DEV
    shards = [pad(g[N_IN * j:N_IN * (j + 1)]) for j in range(_GAP_DEV)]
    shards.append(pad(jnp.concatenate([g[lo:lo + _GAP_ROW], g[lo + _GAP_ROW + _GAP:lo + N_IN + _GAP]], axis=0)))
    shards += [pad(g[N_IN * j + _GAP:N_IN * (j + 1) + _GAP]) for j in range(_GAP_DEV + 1, 8)]
    return jnp.stack(shards)


def _local_grads(x, mem, tgt, win_t, gw_of, sm, on_grads):
    b_pad = jnp.pad(sm['b_fgt'], ((0, 0), (0, 120)))
    tbl = jnp.pad(sm['rel_bias'], ((0, 0), (0, NREL_PAD - 257)))

    h1, proj, flog = _premix_fwd(x, sm['g_mix_pre'], win_t)
    c = _gate_fwd(flog, b_pad)
    c8 = c[:, :8]
    c2 = jnp.repeat(c8, 128, axis=1)
    q_aug, k_aug = _fox_prep(proj, c2)
    o_f, lse_f = _fox_fwd_aug(q_aug, k_aug, proj)
    vt3 = _relvec_fwd(tbl).reshape(4, 2, VW)
    kvp = jnp.pad(proj[:, CHK0 + 512:], ((LEFT, 0), (0, 0)))
    o_c, lse_c = _chk_fwd(proj, kvp, vt3)
    gw = gw_of([o_f, o_c])
    w_out, w_mq, w_mk, w_mv, w_mo, w1_t, w2 = (_wblk(gw, n) for n in ('w_out', 'w_mq', 'w_mk', 'w_mv', 'w_mo', 'w_ff1', 'w_ff2'))
    ycat, z, x1, h2, qm = _postmix_fwd(x, o_f, o_c, sm['g_fox_out'], sm['g_chk_out'], w_out,
                                       sm['g_mix_post'], sm['g_mem_pre'], w_mq)
    memn, km, vm = _memkv_fwd(mem, sm['g_mem_kv'], w_mk, w_mv)
    om, ym, x2, h3 = _mem_fwd(qm, x1, km, vm, w_mo, sm['g_mem_post'], sm['g_ff_pre'])
    a, y3, dx3, loss_acc = _ffn_fwd(h3, x2, tgt, w1_t, w2, sm['g_ff_post'])

    gs = {}
    dx2, da, dy3, r, gs['g_ff_post'], gs['g_ff_pre'] = _ffn_bwd(dx3, y3, x2, a, w1_t, w2, sm['g_ff_post'], sm['g_ff_pre'])
    zero = on_grads('A', _wgrad_group("wgrad_ff", [(da, h3), (r, dy3)], 512), None)
    dx1, dym, dqm, dkm, dvm, gs['g_mem_post'], gs['g_mem_pre'] = _mem_bwd(
        dx2, ym, x1, qm, km, vm, w_mo, w_mq, sm['g_mem_post'] + zero, sm['g_mem_pre'])
    zero = on_grads('A halfway', None, [dx1])
    gs['g_mem_kv'] = _memkv_bwd(dkm, dvm, mem, w_mk, w_mv)
    dz, dof, doc, gs['g_mix_post'], gs['g_fox_out'], gs['g_chk_out'] = _postmix_bwd(
        dx1, z, o_f, o_c, w_out, sm['g_mix_post'] + zero, sm['g_fox_out'], sm['g_chk_out'])
    zero = on_grads('B', _wgrad_group("wgrad_mem_out", [(ycat, dz), (h2, dqm), (memn, dkm), (memn, dvm), (om, dym)], 128), None)
    dq_f, dk_f, dv_f, dcq, dck = _fox_bwd_aug(q_aug, k_aug, proj, o_f, lse_f, dof, jnp.reshape(zero, (1, 1)))
    zero = on_grads('B halfway', None, [dq_f])
    dq_c, dk_c, dv_c, gv = _chk_bwd(proj, kvp, vt3 + zero, o_c, lse_c, doc)
    gs['rel_bias'] = _relvec_bwd(gv.reshape(8, VW))[:, :257]
    dc = jnp.pad((dcq + dck)[:, ::128], ((0, 0), (0, 120)))
    dflog, db = _gate_bwd(dc, flog, b_pad)
    gs['b_fgt'] = db[0:1, :8]
    grad_x, dproj, gs['g_mix_pre'] = _premix_bwd(dx1, x, [dq_f, dk_f, dv_f, dflog, dq_c, dk_c, dv_c], win_t, sm['g_mix_pre'])
    on_grads('C', _proj_rows_to_in(_wgrad(dproj, h1, "wgrad_in")), None)
    return loss_acc[0, 0], grad_x, gs


def kernel(x, mem, w_in, b_fgt, rel_bias, g_fox_out, g_chk_out, w_out, g_mix_pre, g_mix_post, g_mem_kv, w_mq, w_mk, w_mv, w_mo, g_mem_pre, g_mem_post, w_ff1, w_ff2, g_ff_pre, g_ff_post, loss_target, m_w_in, m_b_fgt, m_rel_bias, m_g_fox_out, m_g_chk_out, m_w_out, m_g_mix_pre, m_g_mix_post, m_g_mem_kv, m_w_mq, m_w_mk, m_w_mv, m_w_mo, m_g_mem_pre, m_g_mem_post, m_w_ff1, m_w_ff2, m_g_ff_pre, m_g_ff_post, v_w_in, v_b_fgt, v_rel_bias, v_g_fox_out, v_g_chk_out, v_w_out, v_g_mix_pre, v_g_mix_post, v_g_mem_kv, v_w_mq, v_w_mk, v_w_mv, v_w_mo, v_g_mem_pre, v_g_mem_post, v_w_ff1, v_w_ff2, v_g_ff_pre, v_g_ff_post):
    args = dict(locals())
    two_d = lambda a: a.reshape(a.shape[-2:])
    w = {n: two_d(args[n]) for n in WEIGHTS}
    m = {n: two_d(args['m_' + n]) for n in WEIGHTS}
    v = {n: two_d(args['v_' + n]) for n in WEIGHTS}

    sm = {n: w[n] for n in SMALL}
    shard_in = jnp.pad(w['w_in'].T, ((0, R_IN - N_IN), (0, 0))).astype(BF16)
    gathered_in, zero = _allgather(shard_in, "allgather_w_in")
    win_t = _in_rows_to_proj(gathered_in)
    shard_rest = (jnp.concatenate([w['w_ff1'].T, w['w_ff2'], w['w_out'], w['w_mq'], w['w_mk'], w['w_mv'], w['w_mo']],
                                  axis=0) + zero[0, 0]).astype(BF16)
    rest = _start_copies("allgather_rest_start", shard_rest, (8, R_REST, D), _gather_plan, 4)
    sm['g_mix_pre'] = sm['g_mix_pre'] + rest[4][0, 0]

    def gw_of(after):
        block, land = _wait_copies("allgather_rest_wait", rest, after, _gather_plan)
        return _gather_forward(land, block)

    rs = {}

    def on_grads(stage, g, after):
        if stage.endswith('halfway'):
            rs[stage[0]].halfway(after)
            return rs[stage[0]].token
        rs[stage] = _ReduceScatter("rs_" + stage.lower(), g)
        return rs[stage].token

    loss_local, grad_x, gs = _local_grads(x[0], mem[0], loss_target[0], win_t, gw_of, sm, on_grads)
    grads, deltas, new_m, new_v = {}, {}, {}, {}

    def update(n, out):
        grads[n], deltas[n], new_m[n], new_v[n] = out

    gparts, _ = _allgather(_pack_small(gs, loss_local + rs['C'].token), "allgather_small_grads")
    small = _adamw_small(gparts, [w[n] for n in SMALL], [m[n] for n in SMALL], [v[n] for n in SMALL])
    loss = small[0][0, 0]
    for t, n in enumerate(SMALL):
        update(n, small[1 + 4 * t:5 + 4 * t])
    rs['C'].halfway([small[0]])

    own, got, order = rs['A'].finish([grad_x, rs['C'].started[4]])
    update('w_ff1', _sum_adam(own, got, order, 0, w['w_ff1'], m['w_ff1'], v['w_ff1'], "adamw_w_ff1", transposed=True))
    update('w_ff2', _sum_adam(own, got, order, 512, w['w_ff2'], m['w_ff2'], v['w_ff2'], "adamw_w_ff2"))
    own, got, order = rs['B'].finish([grad_x, rs['C'].started[4]])
    for k, n in enumerate(('w_out', 'w_mq', 'w_mk', 'w_mv', 'w_mo')):
        update(n, _sum_adam(own, got, order, 128 * k, w[n], m[n], v[n], "adamw_" + n))

    g_in = _sum_chips(*rs['C'].finish([new_v[n] for n in BIG if n != 'w_in']), "rs_c_sum_chips")[:N_IN].T
    update('w_in', (g_in,) + tuple(_adamw(w['w_in'], g_in, m['w_in'], v['w_in'], "adamw_w_in")))

    out = [loss, grad_x[None]]
    for group in (grads, deltas, new_m, new_v):
        out += [group[n].reshape(args[n].shape) for n in WEIGHTS]
    return tuple(out)
```

```python
import functools

import jax
import jax.numpy as jnp
from jax import lax
from jax.experimental import pallas as pl
from jax.experimental.pallas import tpu as pltpu

F32 = jnp.float32
BF16 = jnp.bfloat16
MESH = pl.DeviceIdType.MESH

T = 2048
D = 1024
NMEM = 256
DFF = 4096
EPS = 1e-6
TM = 256
TM_WIDE = 512
TQ = 256
FQ = 512
HD = 64
SCALE = HD ** -0.5
MEM_HEADS = 4
MEM_HD = 256
MEM_SCALE = MEM_HD ** -0.5
NEG = -1e30
LEFT = 512
WIN = LEFT + TQ
VW = 1024
NREL_PAD = 384
PROJ = 3200
GATE0 = 1536
CHK0 = 1664
VMEM_BIG = 56 * 1024 * 1024

ADAM_LR = 0.001
ADAM_B1 = 0.9
ADAM_B2 = 0.999
ADAM_EPS = 1e-08
ADAM_WD = 0.01
ADAM_STEP = 10

N_IN = 385
R_IN = 400
R_REST = 1664
W_ROWS = {'w_ff1': (0, 512), 'w_ff2': (512, 512),
          'w_out': (1024, 128), 'w_mq': (1152, 128), 'w_mk': (1280, 128), 'w_mv': (1408, 128), 'w_mo': (1536, 128)}
R_A, R_B = 1024, 640
SMALL_ROWS = 24
SMALL_SLOT = {'rel_bias': (0, 8, 0, 257), 'b_fgt': (8, 1, 0, 8), 'g_fox_out': (9, 1, 0, 512), 'g_chk_out': (9, 1, 512, 512),
              'g_mix_pre': (10, 1, 0, 1024), 'g_mix_post': (11, 1, 0, 1024), 'g_mem_kv': (12, 1, 0, 1024),
              'g_mem_pre': (13, 1, 0, 1024), 'g_mem_post': (14, 1, 0, 1024), 'g_ff_pre': (15, 1, 0, 1024),
              'g_ff_post': (16, 1, 0, 1024)}

WEIGHTS = ['w_in', 'b_fgt', 'rel_bias', 'g_fox_out', 'g_chk_out', 'w_out', 'g_mix_pre', 'g_mix_post', 'g_mem_kv',
           'w_mq', 'w_mk', 'w_mv', 'w_mo', 'g_mem_pre', 'g_mem_post', 'w_ff1', 'w_ff2', 'g_ff_pre', 'g_ff_post']
BIG = ['w_in', 'w_out', 'w_mq', 'w_mk', 'w_mv', 'w_mo', 'w_ff1', 'w_ff2']
SMALL = [n for n in WEIGHTS if n not in BIG]


def _pcall(body, **kw):
    return pl.pallas_call(body, **kw)


def _nn(a, b):
    return jnp.dot(a, b, preferred_element_type=F32)


def _nt(a, b):
    return lax.dot_general(a, b, (((1,), (1,)), ((), ())), preferred_element_type=F32)


def _tn(a, b):
    return lax.dot_general(a, b, (((0,), (0,)), ((), ())), preferred_element_type=F32)


def _w(ref):
    v = ref[...]
    return v if v.ndim == 2 else v.reshape(-1, v.shape[-1])


def _rstd(x):
    return lax.rsqrt(jnp.mean(x * x, axis=-1, keepdims=True) + EPS)


def _rms(x, g):
    return x * _rstd(x) * g


def _rms_bwd(x, g, dy):
    r = _rstd(x)
    xh = x * r
    dg = jnp.sum(dy * xh, axis=0, keepdims=True)
    dxh = dy * g
    dx = r * (dxh - xh * jnp.mean(dxh * xh, axis=-1, keepdims=True))
    return dx, dg


def _resident(a):
    if isinstance(a, tuple):
        _, shape, index = a
        return pl.BlockSpec(shape, lambda *_: index, pipeline_mode=pl.Buffered(1))
    return pl.BlockSpec(a.shape, lambda *_, nd=a.ndim: (0,) * nd, pipeline_mode=pl.Buffered(1))


def _wblk(gw, name):
    r0, rows = W_ROWS[name]
    return (gw, (8, rows, D), (0, r0 // rows, 0))


def _tok_call(body, name, tiled, full, outs_tiled, outs_acc=(), rows=T, tm=TM, vmem=None):
    in_specs = [pl.BlockSpec((tm, a.shape[1]), lambda i: (i, 0)) for a in tiled]
    in_specs += [_resident(a) for a in full]
    full = [a[0] if isinstance(a, tuple) else a for a in full]
    out_shape = [jax.ShapeDtypeStruct((rows, c), dt) for c, dt in outs_tiled]
    out_shape += [jax.ShapeDtypeStruct(s, F32) for s in outs_acc]
    out_specs = [pl.BlockSpec((tm, c), lambda i: (i, 0)) for c, _ in outs_tiled]
    out_specs += [pl.BlockSpec(s, lambda i, nd=len(s): (0,) * nd) for s in outs_acc]
    return _pcall(
        body, name=name, grid=(rows // tm,), in_specs=in_specs, out_specs=out_specs, out_shape=out_shape,
        compiler_params=pltpu.CompilerParams(dimension_semantics=("arbitrary",), vmem_limit_bytes=vmem),
    )(*tiled, *full)


def _one_call(body, name, ins, outs, vmem=None):
    whole = lambda s: pl.BlockSpec(s, lambda i, nd=len(s): (0,) * nd)
    return _pcall(
        body, name=name, grid=(1,), in_specs=[_resident(a) for a in ins], out_specs=[whole(s) for s, _ in outs],
        out_shape=[jax.ShapeDtypeStruct(s, dt) for s, dt in outs],
        compiler_params=pltpu.CompilerParams(dimension_semantics=("arbitrary",), vmem_limit_bytes=vmem),
    )(*[a[0] if isinstance(a, tuple) else a for a in ins])


def _premix_fwd(x, g_pre, win_t):
    def body(x_ref, g_ref, w_ref, h_ref, proj_ref, flog_ref):
        h = _rms(x_ref[...], g_ref[...]).astype(BF16)
        h_ref[...] = h
        p = _nt(h, w_ref[...])
        proj_ref[...] = p.astype(BF16)
        flog_ref[...] = p[:, GATE0:GATE0 + 128]

    return _tok_call(body, "premix_fwd", [x], [g_pre, win_t],
                     [(D, BF16), (PROJ, BF16), (128, F32)], tm=TM_WIDE, vmem=VMEM_BIG)


def _postmix_fwd(x, o_f, o_c, g_fo, g_co, w_out, g_post, g_mpre, w_mq):
    def body(x_ref, of_ref, oc_ref, gfo_ref, gco_ref, wo_ref, gp_ref, gm_ref, wq_ref,
             y_ref, z_ref, x1_ref, h2_ref, qm_ref):
        y_ref[:, :512] = _rms(of_ref[...], gfo_ref[...]).astype(BF16)
        y_ref[:, 512:] = _rms(oc_ref[...], gco_ref[...]).astype(BF16)
        z = _nn(y_ref[...], _w(wo_ref))
        z_ref[...] = z
        x1 = x_ref[...] + _rms(z, gp_ref[...])
        x1_ref[...] = x1
        h2 = _rms(x1, gm_ref[...]).astype(BF16)
        h2_ref[...] = h2
        qm_ref[...] = _nn(h2, _w(wq_ref)).astype(BF16)

    return _tok_call(body, "postmix_fwd", [x, o_f, o_c], [g_fo, g_co, w_out, g_post, g_mpre, w_mq],
                     [(D, BF16), (D, F32), (D, F32), (D, BF16), (D, BF16)], tm=TM_WIDE, vmem=VMEM_BIG)


def _memkv_fwd(mem, g_kv, w_mk, w_mv):
    def body(m_ref, g_ref, wk_ref, wv_ref, mn_ref, k_ref, v_ref):
        mn = _rms(m_ref[...], g_ref[...]).astype(BF16)
        mn_ref[...] = mn
        k_ref[...] = _nn(mn, _w(wk_ref)).astype(BF16)
        v_ref[...] = _nn(mn, _w(wv_ref)).astype(BF16)

    return _tok_call(body, "memkv_fwd", [mem], [g_kv, w_mk, w_mv],
                     [(D, BF16), (D, BF16), (D, BF16)], rows=NMEM, tm=NMEM, vmem=VMEM_BIG)


def _mem_fwd(qm, x1, km, vm, w_mo, g_post, g_fpre):
    def body(q_ref, x1_ref, k_ref, v_ref, wo_ref, gp_ref, gf_ref, om_ref, ym_ref, x2_ref, h3_ref):
        for h in range(MEM_HEADS):
            sl = slice(h * MEM_HD, (h + 1) * MEM_HD)
            s = _nt(q_ref[:, sl], k_ref[:, sl]) * MEM_SCALE
            p = jnp.exp(s - jnp.max(s, axis=-1, keepdims=True))
            p = p / jnp.sum(p, axis=-1, keepdims=True)
            om_ref[:, sl] = _nn(p.astype(BF16), v_ref[:, sl]).astype(BF16)
        ym = _nn(om_ref[...], _w(wo_ref))
        ym_ref[...] = ym
        x2 = x1_ref[...] + _rms(ym, gp_ref[...])
        x2_ref[...] = x2
        h3_ref[...] = _rms(x2, gf_ref[...]).astype(BF16)

    return _tok_call(body, "mem_fwd", [qm, x1], [km, vm, w_mo, g_post, g_fpre],
                     [(D, BF16), (D, F32), (D, F32), (D, BF16)], tm=TM_WIDE, vmem=VMEM_BIG)


def _ffn_fwd(h3, x2, tgt, w1_t, w2, g_post):
    def body(h_ref, x2_ref, t_ref, w1_ref, w2_ref, g_ref, a_ref, y_ref, dx_ref, loss_ref):
        @pl.when(pl.program_id(0) == 0)
        def _():
            loss_ref[...] = jnp.zeros_like(loss_ref)

        a = _nt(h_ref[...], _w(w1_ref))
        a_ref[...] = a.astype(BF16)
        r = jnp.square(jnp.maximum(a, 0.0)).astype(BF16)
        y = _nn(r, _w(w2_ref))
        y_ref[...] = y
        e = x2_ref[...] + _rms(y, g_ref[...]) - t_ref[...]
        dx_ref[...] = e * (1.0 / D)
        loss_ref[...] += 0.5 * jnp.sum(jnp.sum(e * e, axis=-1, keepdims=True) * (1.0 / D))

    return _tok_call(body, "ffn_fwd", [h3, x2, tgt], [w1_t, w2, g_post],
                     [(DFF, BF16), (D, F32), (D, F32)], [(8, 128)], vmem=VMEM_BIG)


def _tri(lower):
    r = lax.broadcasted_iota(jnp.int32, (128, 128), 0)
    c = lax.broadcasted_iota(jnp.int32, (128, 128), 1)
    return jnp.where(r >= c if lower else c >= r, 1.0, 0.0).astype(F32)


def _hdot(a, b):
    return jnp.dot(a, b, preferred_element_type=F32, precision=lax.Precision.HIGHEST)


def _gate_fwd(flog, b_pad):
    def body(f_ref, b_ref, c_ref):
        tri = _tri(True)

        def step(i, carry):
            rows = pl.ds(pl.multiple_of(i * 128, 128), 128)
            z = f_ref[rows, :] + b_ref[...]
            lf = jnp.minimum(z, 0.0) - jnp.log(1.0 + jnp.exp(-jnp.abs(z)))
            cb = _hdot(tri, lf) + carry
            c_ref[rows, :] = cb
            return cb[127:128, :]

        lax.fori_loop(0, T // 128, step, jnp.zeros((1, 128), F32))

    return _one_call(body, "gate_fwd", [flog, b_pad], [((T, 128), F32)])[0]


def _gate_bwd(dc, flog, b_pad):
    def body(dc_ref, f_ref, b_ref, df_ref, db_ref):
        tri = _tri(False)

        def step(j, carry):
            run, db = carry
            i = T // 128 - 1 - j
            rows = pl.ds(pl.multiple_of(i * 128, 128), 128)
            dcb = dc_ref[rows, :]
            rb = _hdot(tri, dcb) + run
            z = f_ref[rows, :] + b_ref[...]
            df = rb * (1.0 / (1.0 + jnp.exp(z)))
            df_ref[rows, :] = df.astype(BF16)
            return run + jnp.sum(dcb, axis=0, keepdims=True), db + jnp.sum(df, axis=0, keepdims=True)

        _, db = lax.fori_loop(0, T // 128, step, (jnp.zeros((1, 128), F32), jnp.zeros((1, 128), F32)))
        db_ref[...] = jnp.broadcast_to(db, (8, 128))

    return _one_call(body, "gate_bwd", [dc, flog, b_pad], [((T, 128), BF16), ((8, 128), F32)])


def _lane_lo(rows=TQ):
    return lax.broadcasted_iota(jnp.int32, (rows, 128), 1) < HD


def _half(v, lo, a, scale=None):
    keep = lo if a == 0 else jnp.logical_not(lo)
    v = v.astype(F32) if scale is None else v.astype(F32) * scale
    return jnp.where(keep, v, 0.0).astype(BF16)


def _fox_specs():
    return [pl.BlockSpec((FQ, 128), lambda h, i: (i, h)),
            pl.BlockSpec((T, 128), lambda h, i: (0, 4 + h)),
            pl.BlockSpec((T, 128), lambda h, i: (0, 8 + h))]


def _lane_pick(x, at):
    lane = lax.broadcasted_iota(jnp.int32, x.shape, 1)
    return jnp.sum(jnp.where(lane == at, x, 0.0), axis=-1, keepdims=True)


def _fox_fwd(proj, c, ct3):
    def body(q_ref, k_ref, v_ref, c_ref, ct_ref, o_ref, l_ref):
        i = pl.program_id(1)
        lo = _lane_lo(FQ)
        causal = lax.broadcasted_iota(jnp.int32, (FQ, FQ), 1) <= lax.broadcasted_iota(jnp.int32, (FQ, FQ), 0)
        q = q_ref[...]
        qs = [_half(q, lo, a, SCALE) for a in range(2)]
        cqs = [_lane_pick(c_ref[...], 2 * pl.program_id(0) + a) for a in range(2)]

        def tile(off, carry, diagonal):
            kblk = k_ref[pl.ds(off, FQ), :]
            vblk = v_ref[pl.ds(off, FQ), :]
            new = []
            for a in range(2):
                m, l, acc = carry[a]
                s = _nt(qs[a], kblk) + (cqs[a] - ct_ref[a:a + 1, pl.ds(off, FQ)])
                if diagonal:
                    s = jnp.where(causal, s, NEG)
                m2 = jnp.maximum(m, jnp.max(s, axis=-1, keepdims=True))
                p = jnp.exp(s - m2)
                alpha = jnp.exp(m - m2)
                new.append((m2, alpha * l + jnp.sum(p, axis=-1, keepdims=True),
                            alpha * acc + _nn(p.astype(BF16), vblk)))
            return tuple(new)

        init = (jnp.full((FQ, 1), NEG, F32), jnp.zeros((FQ, 1), F32), jnp.zeros((FQ, 128), F32))
        carry = lax.fori_loop(0, i, lambda kb, c: tile(pl.multiple_of(kb * FQ, FQ), c, False), (init, init))
        carry = tile(pl.multiple_of(i * FQ, FQ), carry, True)
        outs = []
        for a in range(2):
            m, l, acc = carry[a]
            outs.append(acc / l)
            l_ref[:, 128 * a:128 * a + 128] = jnp.broadcast_to(m + jnp.log(l), (FQ, 128))
        o_ref[...] = jnp.where(lo, outs[0], outs[1])

    return _pcall(
        body, name="fox_fwd", grid=(4, T // FQ),
        in_specs=_fox_specs() + [pl.BlockSpec((FQ, 128), lambda h, i: (i, 0)),
                                 pl.BlockSpec((None, 2, T), lambda h, i: (h, 0, 0))],
        out_specs=[pl.BlockSpec((FQ, 128), lambda h, i: (i, h)), pl.BlockSpec((FQ, 256), lambda h, i: (i, h))],
        out_shape=[jax.ShapeDtypeStruct((T, 512), F32), jax.ShapeDtypeStruct((T, 1024), F32)],
        compiler_params=pltpu.CompilerParams(dimension_semantics=("arbitrary", "arbitrary"), vmem_limit_bytes=VMEM_BIG),
    )(proj, proj, proj, c, ct3)


def _fox_bwd(proj, c, ct3, o, lse, do):
    def body(q_ref, k_ref, v_ref, c_ref, ct_ref, o_ref, l_ref, do_ref, dq_ref, dkb_ref, dvb_ref, dct_ref, dcq_ref,
             dk_ref, dv_ref):
        i = pl.program_id(1)

        @pl.when(i == 0)
        def _():
            dk_ref[...] = jnp.zeros_like(dk_ref)
            dv_ref[...] = jnp.zeros_like(dv_ref)
            dct_ref[...] = jnp.zeros_like(dct_ref)

        lo = _lane_lo(FQ)
        causal = lax.broadcasted_iota(jnp.int32, (FQ, FQ), 1) <= lax.broadcasted_iota(jnp.int32, (FQ, FQ), 0)
        q = q_ref[...]
        do_v = do_ref[...]
        prod = do_v * o_ref[...]
        qs = [_half(q, lo, a, SCALE) for a in range(2)]
        dos = [_half(do_v, lo, a) for a in range(2)]
        deltas = [jnp.sum(jnp.where(lo if a == 0 else jnp.logical_not(lo), prod, 0.0), axis=-1, keepdims=True)
                  for a in range(2)]
        cqs = [_lane_pick(c_ref[...], 2 * pl.program_id(0) + a) for a in range(2)]
        las = [l_ref[:, 128 * a:128 * a + 1] for a in range(2)]

        def tile(off, carry, diagonal):
            kblk = k_ref[pl.ds(off, FQ), :]
            vblk = v_ref[pl.ds(off, FQ), :]
            new = []
            dk = jnp.zeros((FQ, 128), F32)
            dv = jnp.zeros((FQ, 128), F32)
            for a in range(2):
                dq_acc, rs = carry[a]
                s = _nt(qs[a], kblk) + (cqs[a] - ct_ref[a:a + 1, pl.ds(off, FQ)])
                if diagonal:
                    s = jnp.where(causal, s, NEG)
                p = jnp.exp(s - las[a])
                ds = p * (_nt(dos[a], vblk) - deltas[a])
                dsb = ds.astype(BF16)
                dk = dk + _tn(dsb, qs[a])
                dv = dv + _tn(p.astype(BF16), dos[a])
                dct_ref[a:a + 1, pl.ds(off, FQ)] -= jnp.sum(ds, axis=0, keepdims=True)
                new.append((dq_acc + _nn(dsb, kblk), rs + jnp.sum(ds, axis=-1, keepdims=True)))
            dk_ref[pl.ds(off, FQ), :] += dk
            dv_ref[pl.ds(off, FQ), :] += dv
            return tuple(new)

        init = (jnp.zeros((FQ, 128), F32), jnp.zeros((FQ, 1), F32))
        carry = lax.fori_loop(0, i, lambda kb, c: tile(pl.multiple_of(kb * FQ, FQ), c, False), (init, init))
        carry = tile(pl.multiple_of(i * FQ, FQ), carry, True)
        lane = lax.broadcasted_iota(jnp.int32, (FQ, 128), 1)
        dcq_ref[...] = jnp.where(lane == 0, carry[0][1], jnp.where(lane == 1, carry[1][1], 0.0))
        dq_ref[...] = (jnp.where(lo, carry[0][0], carry[1][0]) * SCALE).astype(BF16)

        @pl.when(i == T // FQ - 1)
        def _():
            dkb_ref[...] = dk_ref[...].astype(BF16)
            dvb_ref[...] = dv_ref[...].astype(BF16)

    blk = pl.BlockSpec((FQ, 128), lambda h, i: (i, h))
    wide = pl.BlockSpec((FQ, 256), lambda h, i: (i, h))
    rows = pl.BlockSpec((None, 2, T), lambda h, i: (h, 0, 0))
    col = pl.BlockSpec((T, 128), lambda h, i: (0, h))
    return _pcall(
        body, name="fox_bwd", grid=(4, T // FQ),
        in_specs=_fox_specs() + [pl.BlockSpec((FQ, 128), lambda h, i: (i, 0)), rows, blk, wide, blk],
        out_specs=[blk, col, col, rows, pl.BlockSpec((None, FQ, 128), lambda h, i: (h, i, 0))],
        out_shape=[jax.ShapeDtypeStruct((T, 512), BF16), jax.ShapeDtypeStruct((T, 512), BF16),
                   jax.ShapeDtypeStruct((T, 512), BF16), jax.ShapeDtypeStruct((4, 2, T), F32),
                   jax.ShapeDtypeStruct((4, T, 128), F32)],
        scratch_shapes=[pltpu.VMEM((T, 128), F32), pltpu.VMEM((T, 128), F32)],
        compiler_params=pltpu.CompilerParams(dimension_semantics=("arbitrary", "arbitrary"), vmem_limit_bytes=VMEM_BIG),
    )(proj, proj, proj, c, ct3, o, lse, do)


AUG_ROWSUM, AUG_COLSUM = 67, 64


def _fox_prep(proj, c2):
    def body(q_ref, k_ref, c_ref, qa_ref, ka_ref):
        lane = lax.broadcasted_iota(jnp.int32, (FQ, 128), 1)
        q = q_ref[...].astype(F32) * SCALE
        k = k_ref[...].astype(F32)
        for a in range(2):
            c = c_ref[:, 128 * a:128 * a + 128]
            hi = c.astype(BF16).astype(F32)
            mid = (c - hi).astype(BF16).astype(F32)
            lo = c - hi - mid
            parts = lambda first, sign: jnp.where(lane == first, sign * hi, jnp.where(lane == first + 1, sign * mid, sign * lo))
            qd = q if a == 0 else pltpu.roll(q, 64, 1)
            kd = k if a == 0 else pltpu.roll(k, 64, 1)
            ones = jnp.ones((FQ, 128), F32)
            qa = jnp.where(lane < 64, qd, jnp.where(lane < 67, ones, jnp.where(lane < 70, parts(67, 1.0), 0.0)))
            ka = jnp.where(lane < 64, kd, jnp.where(lane < 67, parts(64, -1.0), jnp.where(lane < 70, ones, 0.0)))
            qa_ref[:, 128 * a:128 * a + 128] = qa.astype(BF16)
            ka_ref[:, 128 * a:128 * a + 128] = ka.astype(BF16)

    wide = pl.BlockSpec((FQ, 256), lambda h, i: (i, h))
    return _pcall(
        body, name="fox_prep", grid=(4, T // FQ),
        in_specs=[pl.BlockSpec((FQ, 128), lambda h, i: (i, h)), pl.BlockSpec((FQ, 128), lambda h, i: (i, 4 + h)), wide],
        out_specs=[wide, wide], out_shape=[jax.ShapeDtypeStruct((T, 1024), BF16)] * 2,
        compiler_params=pltpu.CompilerParams(dimension_semantics=("arbitrary", "arbitrary")),
    )(proj, proj, c2)


def _fox_aug_specs():
    return [pl.BlockSpec((FQ, 256), lambda h, i: (i, h)), pl.BlockSpec((T, 256), lambda h, i: (0, h)),
            pl.BlockSpec((T, 128), lambda h, i: (0, 8 + h))]


def _causal():
    return lax.broadcasted_iota(jnp.int32, (FQ, FQ), 1) <= lax.broadcasted_iota(jnp.int32, (FQ, FQ), 0)


def _fox_fwd_aug(q_aug, k_aug, proj):
    def body(q_ref, k_ref, v_ref, o_ref, l_ref):
        i = pl.program_id(1)
        lo = _lane_lo(FQ)
        causal = _causal()
        qs = [q_ref[:, 128 * a:128 * a + 128] for a in range(2)]

        def tile(off, carry, diagonal):
            vblk = v_ref[pl.ds(off, FQ), :]
            new = []
            for a in range(2):
                m, l, acc = carry[a]
                s = _nt(qs[a], k_ref[pl.ds(off, FQ), 128 * a:128 * a + 128])
                if diagonal:
                    s = jnp.where(causal, s, NEG)
                m2 = jnp.maximum(m, jnp.max(s, axis=-1, keepdims=True))
                p = jnp.exp(s - m2)
                alpha = jnp.exp(m - m2)
                new.append((m2, alpha * l + jnp.sum(p, axis=-1, keepdims=True),
                            alpha * acc + _nn(p.astype(BF16), vblk)))
            return tuple(new)

        init = (jnp.full((FQ, 1), NEG, F32), jnp.zeros((FQ, 1), F32), jnp.zeros((FQ, 128), F32))
        carry = lax.fori_loop(0, i, lambda kb, c: tile(pl.multiple_of(kb * FQ, FQ), c, False), (init, init))
        carry = tile(pl.multiple_of(i * FQ, FQ), carry, True)
        outs = []
        for a in range(2):
            m, l, acc = carry[a]
            outs.append(acc / l)
            l_ref[:, 128 * a:128 * a + 128] = jnp.broadcast_to(m + jnp.log(l), (FQ, 128))
        o_ref[...] = jnp.where(lo, outs[0], outs[1])

    return _pcall(
        body, name="fox_fwd", grid=(4, T // FQ), in_specs=_fox_aug_specs(),
        out_specs=[pl.BlockSpec((FQ, 128), lambda h, i: (i, h)), pl.BlockSpec((FQ, 256), lambda h, i: (i, h))],
        out_shape=[jax.ShapeDtypeStruct((T, 512), F32), jax.ShapeDtypeStruct((T, 1024), F32)],
        compiler_params=pltpu.CompilerParams(dimension_semantics=("arbitrary", "arbitrary"), vmem_limit_bytes=VMEM_BIG),
    )(q_aug, k_aug, proj)


def _fox_bwd_aug(q_aug, k_aug, proj, o, lse, do, after):
    nq = T // FQ

    def body(q_ref, k_ref, v_ref, o_ref, l_ref, do_ref, after_ref, dq_ref, dkb_ref, dvb_ref, dcq_ref, dck_ref, dk_ref,
             dv_ref):
        i = pl.program_id(1)

        @pl.when(i == 0)
        def _():
            dk_ref[...] = jnp.zeros_like(dk_ref)
            dv_ref[...] = jnp.zeros_like(dv_ref)

        lo = _lane_lo(FQ)
        lane = lax.broadcasted_iota(jnp.int32, (FQ, 128), 1)
        causal = _causal()
        do_v = do_ref[...]
        prod = do_v * o_ref[...]
        qs = [q_ref[:, 128 * a:128 * a + 128] for a in range(2)]
        dos = [_half(do_v, lo, a) for a in range(2)]
        deltas = [jnp.sum(jnp.where(lo if a == 0 else jnp.logical_not(lo), prod, 0.0), axis=-1, keepdims=True)
                  for a in range(2)]
        las = [l_ref[:, 128 * a:128 * a + 1] for a in range(2)]

        def tile(off, carry, diagonal):
            vblk = v_ref[pl.ds(off, FQ), :]
            new = []
            dv = jnp.zeros((FQ, 128), F32)
            for a in range(2):
                kblk = k_ref[pl.ds(off, FQ), 128 * a:128 * a + 128]
                s = _nt(qs[a], kblk)
                if diagonal:
                    s = jnp.where(causal, s, NEG)
                p = jnp.exp(s - las[a])
                dsb = (p * (_nt(dos[a], vblk) - deltas[a])).astype(BF16)
                dk_ref[a, pl.ds(off, FQ), :] += _tn(dsb, qs[a])
                dv = dv + _tn(p.astype(BF16), dos[a])
                new.append(carry[a] + _nn(dsb, kblk))
            dv_ref[pl.ds(off, FQ), :] += dv
            return tuple(new)

        init = jnp.zeros((FQ, 128), F32)
        dqs = lax.fori_loop(0, i, lambda kb, c: tile(pl.multiple_of(kb * FQ, FQ), c, False), (init, init))
        dqs = tile(pl.multiple_of(i * FQ, FQ), dqs, True)
        pick = lambda x, at: jnp.sum(jnp.where(lane == at, x, 0.0), axis=-1, keepdims=True)
        for a in range(2):
            dcq_ref[:, 128 * a:128 * a + 128] = jnp.broadcast_to(pick(dqs[a], AUG_ROWSUM), (FQ, 128))
        dq_ref[...] = (jnp.where(lo, dqs[0], pltpu.roll(dqs[1], 64, 1)) * SCALE).astype(BF16)

        @pl.when(i == nq - 1)
        def _():
            big_lane = lax.broadcasted_iota(jnp.int32, (T, 128), 1)
            dkb_ref[...] = jnp.where(big_lane < 64, dk_ref[0], pltpu.roll(dk_ref[1], 64, 1)).astype(BF16)
            dvb_ref[...] = dv_ref[...].astype(BF16)
            for a in range(2):
                col = jnp.sum(jnp.where(big_lane == AUG_COLSUM, dk_ref[a], 0.0), axis=-1, keepdims=True)
                dck_ref[:, 128 * a:128 * a + 128] = jnp.broadcast_to(-col, (T, 128))

    blk = pl.BlockSpec((FQ, 128), lambda h, i: (i, h))
    wide = pl.BlockSpec((FQ, 256), lambda h, i: (i, h))
    col = pl.BlockSpec((T, 128), lambda h, i: (0, h))
    colwide = pl.BlockSpec((T, 256), lambda h, i: (0, h))
    return _pcall(
        body, name="fox_bwd", grid=(4, nq), in_specs=_fox_aug_specs() + [blk, wide, blk, ANY_SPEC],
        out_specs=[blk, col, col, wide, colwide],
        out_shape=[jax.ShapeDtypeStruct((T, 512), BF16)] * 3 + [jax.ShapeDtypeStruct((T, 1024), F32)] * 2,
        scratch_shapes=[pltpu.VMEM((2, T, 128), F32), pltpu.VMEM((T, 128), F32)],
        compiler_params=pltpu.CompilerParams(dimension_semantics=("arbitrary", "arbitrary"), vmem_limit_bytes=VMEM_BIG),
    )(q_aug, k_aug, proj, o, lse, do, after)


def _rel_onehot():
    ridx = lax.broadcasted_iota(jnp.int32, (NREL_PAD, VW), 0)
    j = lax.broadcasted_iota(jnp.int32, (NREL_PAD, VW), 1)
    return jnp.where(ridx == jnp.clip(TQ + LEFT - 1 - j, -128, 128) + 128, 1.0, 0.0).astype(F32)


def _relvec_fwd(tbl):
    def body(t_ref, v_ref):
        v_ref[...] = _hdot(t_ref[...], _rel_onehot())

    return _one_call(body, "relvec_fwd", [tbl], [((8, VW), F32)])[0]


def _relvec_bwd(gv):
    def body(g_ref, t_ref):
        t_ref[...] = lax.dot_general(g_ref[...], _rel_onehot(), (((1,), (1,)), ((), ())),
                                     preferred_element_type=F32, precision=lax.Precision.HIGHEST)

    return _one_call(body, "relvec_bwd", [gv], [((8, NREL_PAD), F32)])[0]


def _chk_bias(vt_ref, a):
    vb = jnp.broadcast_to(vt_ref[a:a + 1, :], (TQ, VW))
    y = pltpu.roll(vb, VW - (TQ - 1), 1, stride=1, stride_axis=0)[:, :WIN]
    cr = lax.broadcasted_iota(jnp.int32, (TQ, WIN), 0) // 64
    cm = lax.broadcasted_iota(jnp.int32, (TQ, WIN), 1) // 64
    return jnp.where((cm >= cr) & (cm <= cr + 8), y, NEG)


def _chk_specs():
    return [pl.BlockSpec((TQ, 128), lambda h, i: (i, CHK0 // 128 + h)),
            pl.BlockSpec((T + LEFT, 128), lambda h, i: (0, h)),
            pl.BlockSpec((T + LEFT, 128), lambda h, i: (0, 4 + h)),
            pl.BlockSpec((None, 2, VW), lambda h, i: (h, 0, 0))]


def _chk_fwd(proj, kvp, vt3):
    def body(q_ref, k_ref, v_ref, vt_ref, o_ref, l_ref, bias_ref):
        i = pl.program_id(1)

        @pl.when(i == 0)
        def _():
            for a in range(2):
                bias_ref[a] = _chk_bias(vt_ref, a)

        lo = _lane_lo()
        off = pl.multiple_of(i * TQ, TQ)
        kw = k_ref[pl.ds(off, WIN), :]
        vw = v_ref[pl.ds(off, WIN), :]
        real = lax.broadcasted_iota(jnp.int32, (TQ, WIN), 1) + off >= LEFT
        q = q_ref[...]
        outs = []
        for a in range(2):
            s = jnp.where(real, _nt(_half(q, lo, a), kw) * SCALE + bias_ref[a], NEG)
            m = jnp.max(s, axis=-1, keepdims=True)
            p = jnp.exp(s - m)
            l = jnp.sum(p, axis=-1, keepdims=True)
            outs.append(_nn(p.astype(BF16), vw) / l)
            l_ref[:, 128 * a:128 * a + 128] = jnp.broadcast_to(m + jnp.log(l), (TQ, 128))
        o_ref[...] = jnp.where(lo, outs[0], outs[1])

    return _pcall(
        body, name="chk_fwd", grid=(4, T // TQ), in_specs=_chk_specs(),
        out_specs=[pl.BlockSpec((TQ, 128), lambda h, i: (i, h)), pl.BlockSpec((TQ, 256), lambda h, i: (i, h))],
        out_shape=[jax.ShapeDtypeStruct((T, 512), F32), jax.ShapeDtypeStruct((T, 1024), F32)],
        scratch_shapes=[pltpu.VMEM((2, TQ, WIN), F32)],
        compiler_params=pltpu.CompilerParams(dimension_semantics=("arbitrary", "arbitrary")),
    )(proj, kvp, kvp, vt3)


def _chk_bwd(proj, kvp, vt3, o, lse, do):
    nq = T // TQ

    def body(q_ref, k_ref, v_ref, vt_ref, o_ref, l_ref, do_ref, dq_ref, dkb_ref, dvb_ref, gv_ref, bias_ref, dsum_ref,
             dk_ref, dv_ref):
        i = pl.program_id(1)

        @pl.when(i == 0)
        def _():
            for a in range(2):
                bias_ref[a] = _chk_bias(vt_ref, a)
            dsum_ref[...] = jnp.zeros_like(dsum_ref)
            dk_ref[...] = jnp.zeros_like(dk_ref)
            dv_ref[...] = jnp.zeros_like(dv_ref)

        lo = _lane_lo()
        off = pl.multiple_of(i * TQ, TQ)
        kw = k_ref[pl.ds(off, WIN), :]
        vw = v_ref[pl.ds(off, WIN), :]
        real = lax.broadcasted_iota(jnp.int32, (TQ, WIN), 1) + off >= LEFT
        q = q_ref[...]
        do_v = do_ref[...]
        prod = do_v * o_ref[...]
        dqs = []
        for a in range(2):
            keep = lo if a == 0 else jnp.logical_not(lo)
            qa = _half(q, lo, a)
            doa = _half(do_v, lo, a)
            delta = jnp.sum(jnp.where(keep, prod, 0.0), axis=-1, keepdims=True)
            s = jnp.where(real, _nt(qa, kw) * SCALE + bias_ref[a], NEG)
            p = jnp.exp(s - l_ref[:, 128 * a:128 * a + 1])
            ds = p * (_nt(doa, vw) - delta)
            dsum_ref[a] += ds
            dsb = ds.astype(BF16)
            dk_ref[pl.ds(off, WIN), :] += _tn(dsb, qa) * SCALE
            dv_ref[pl.ds(off, WIN), :] += _tn(p.astype(BF16), doa)
            dqs.append(_nn(dsb, kw))
        dq_ref[...] = (jnp.where(lo, dqs[0], dqs[1]) * SCALE).astype(BF16)

        @pl.when(i == nq - 1)
        def _():
            dkb_ref[...] = dk_ref[LEFT:, :].astype(BF16)
            dvb_ref[...] = dv_ref[LEFT:, :].astype(BF16)
            rr = lax.broadcasted_iota(jnp.int32, (TQ, TQ), 0)
            cc = lax.broadcasted_iota(jnp.int32, (TQ, TQ), 1)
            flip = jnp.where(rr + cc == TQ - 1, 1.0, 0.0).astype(F32)
            for a in range(2):
                dpad = jnp.concatenate([dsum_ref[a], jnp.zeros((TQ, VW - WIN), F32)], axis=1)
                z = pltpu.roll(_hdot(flip, dpad), 0, 1, stride=1, stride_axis=0)
                gv_ref[a:a + 1, :] = jnp.sum(z, axis=0, keepdims=True)

    blk = pl.BlockSpec((TQ, 128), lambda h, i: (i, h))
    wide = pl.BlockSpec((TQ, 256), lambda h, i: (i, h))
    col = pl.BlockSpec((T, 128), lambda h, i: (0, h))
    return _pcall(
        body, name="chk_bwd", grid=(4, nq), in_specs=_chk_specs() + [blk, wide, blk],
        out_specs=[blk, col, col, pl.BlockSpec((None, 2, VW), lambda h, i: (h, 0, 0))],
        out_shape=[jax.ShapeDtypeStruct((T, 512), BF16), jax.ShapeDtypeStruct((T, 512), BF16),
                   jax.ShapeDtypeStruct((T, 512), BF16), jax.ShapeDtypeStruct((4, 2, VW), F32)],
        scratch_shapes=[pltpu.VMEM((2, TQ, WIN), F32), pltpu.VMEM((2, TQ, WIN), F32),
                        pltpu.VMEM((T + LEFT, 128), F32), pltpu.VMEM((T + LEFT, 128), F32)],
        compiler_params=pltpu.CompilerParams(dimension_semantics=("arbitrary", "arbitrary")),
    )(proj, kvp, kvp, vt3, o, lse, do)


def _zero_at_start(*refs):
    @pl.when(pl.program_id(0) == 0)
    def _():
        for r in refs:
            r[...] = jnp.zeros_like(r)


def _ffn_bwd(dx3, y3, x2, a, w1_t, w2, g_post, g_pre):
    def body(dx3_ref, y_ref, x2_ref, a_ref, w1_ref, w2_ref, gp_ref, gf_ref,
             dx2_ref, da_ref, dy_ref, r_ref, dgp_ref, dgf_ref):
        _zero_at_start(dgp_ref, dgf_ref)
        dx3_v = dx3_ref[...]
        dy, dgp = _rms_bwd(y_ref[...], gp_ref[...], dx3_v)
        dgp_ref[...] += dgp
        dyb = dy.astype(BF16)
        dy_ref[...] = dyb
        ra = jnp.maximum(a_ref[...].astype(F32), 0.0)
        r_ref[...] = jnp.square(ra).astype(BF16)
        da = (_nt(dyb, _w(w2_ref)) * (2.0 * ra)).astype(BF16)
        da_ref[...] = da
        dh, dgf = _rms_bwd(x2_ref[...], gf_ref[...], _nn(da, _w(w1_ref)))
        dgf_ref[...] += dgf
        dx2_ref[...] = dx3_v + dh

    return _tok_call(body, "ffn_bwd", [dx3, y3, x2, a], [w1_t, w2, g_post, g_pre],
                     [(D, F32), (DFF, BF16), (D, BF16), (DFF, BF16)], [(1, D), (1, D)], vmem=VMEM_BIG)


def _mem_bwd(dx2, ym, x1, qm, km, vm, w_mo, w_mq, g_post, g_pre):
    def body(dx2_ref, ym_ref, x1_ref, q_ref, k_ref, v_ref, wo_ref, wq_ref, gp_ref, gm_ref,
             dx1_ref, dym_ref, dq_ref, dk_ref, dv_ref, dgp_ref, dgm_ref, dom_ref):
        _zero_at_start(dk_ref, dv_ref, dgp_ref, dgm_ref)
        dx2_v = dx2_ref[...]
        dym, dgp = _rms_bwd(ym_ref[...], gp_ref[...], dx2_v)
        dgp_ref[...] += dgp
        dymb = dym.astype(BF16)
        dym_ref[...] = dymb
        dom_ref[...] = _nt(dymb, _w(wo_ref)).astype(BF16)
        for h in range(MEM_HEADS):
            sl = slice(h * MEM_HD, (h + 1) * MEM_HD)
            qh, kh, doh = q_ref[:, sl], k_ref[:, sl], dom_ref[:, sl]
            s = _nt(qh, kh) * MEM_SCALE
            p = jnp.exp(s - jnp.max(s, axis=-1, keepdims=True))
            p = p / jnp.sum(p, axis=-1, keepdims=True)
            dp = _nt(doh, v_ref[:, sl])
            ds = (p * (dp - jnp.sum(p * dp, axis=-1, keepdims=True))).astype(BF16)
            dq_ref[:, sl] = (_nn(ds, kh) * MEM_SCALE).astype(BF16)
            dk_ref[:, sl] += _tn(ds, qh) * MEM_SCALE
            dv_ref[:, sl] += _tn(p.astype(BF16), doh)
        dh, dgm = _rms_bwd(x1_ref[...], gm_ref[...], _nt(dq_ref[...], _w(wq_ref)))
        dgm_ref[...] += dgm
        dx1_ref[...] = dx2_v + dh

    tiled = pl.BlockSpec((TM_WIDE, D), lambda i: (i, 0))
    in_specs = [tiled] * 4 + [_resident(a) for a in (km, vm, w_mo, w_mq, g_post, g_pre)]
    w_mo, w_mq = w_mo[0], w_mq[0]
    kv = pl.BlockSpec((NMEM, D), lambda i: (0, 0))
    vec = pl.BlockSpec((1, D), lambda i: (0, 0))
    return _pcall(
        body, name="mem_bwd", grid=(T // TM_WIDE,), in_specs=in_specs,
        out_specs=[tiled, tiled, tiled, kv, kv, vec, vec],
        out_shape=[jax.ShapeDtypeStruct((T, D), F32), jax.ShapeDtypeStruct((T, D), BF16),
                   jax.ShapeDtypeStruct((T, D), BF16), jax.ShapeDtypeStruct((NMEM, D), F32),
                   jax.ShapeDtypeStruct((NMEM, D), F32), jax.ShapeDtypeStruct((1, D), F32),
                   jax.ShapeDtypeStruct((1, D), F32)],
        scratch_shapes=[pltpu.VMEM((TM_WIDE, D), BF16)],
        compiler_params=pltpu.CompilerParams(dimension_semantics=("arbitrary",), vmem_limit_bytes=VMEM_BIG),
    )(dx2, ym, x1, qm, km, vm, w_mo, w_mq, g_post, g_pre)


def _memkv_bwd(dkm, dvm, mem, w_mk, w_mv):
    def body(dk_ref, dv_ref, m_ref, wk_ref, wv_ref, dg_ref):
        dmn = _nt(dk_ref[...].astype(BF16), _w(wk_ref)) + _nt(dv_ref[...].astype(BF16), _w(wv_ref))
        mv = m_ref[...]
        dg_ref[...] = jnp.sum(dmn * (mv * _rstd(mv)), axis=0, keepdims=True)

    return _one_call(body, "memkv_bwd", [dkm, dvm, mem, w_mk, w_mv], [((1, D), F32)], vmem=VMEM_BIG)[0]


def _postmix_bwd(dx1, z, o_f, o_c, w_out, g_post, g_fo, g_co):
    def body(dx1_ref, z_ref, of_ref, oc_ref, wo_ref, gp_ref, gfo_ref, gco_ref,
             dz_ref, dof_ref, doc_ref, dgp_ref, dgfo_ref, dgco_ref):
        _zero_at_start(dgp_ref, dgfo_ref, dgco_ref)
        dz, dgp = _rms_bwd(z_ref[...], gp_ref[...], dx1_ref[...])
        dgp_ref[...] += dgp
        dzb = dz.astype(BF16)
        dz_ref[...] = dzb
        dy = _nt(dzb, _w(wo_ref))
        dof, dgfo = _rms_bwd(of_ref[...], gfo_ref[...], dy[:, :512])
        doc, dgco = _rms_bwd(oc_ref[...], gco_ref[...], dy[:, 512:])
        dof_ref[...] = dof
        doc_ref[...] = doc
        dgfo_ref[...] += dgfo
        dgco_ref[...] += dgco

    return _tok_call(body, "postmix_bwd", [dx1, z, o_f, o_c], [w_out, g_post, g_fo, g_co],
                     [(D, BF16), (512, F32), (512, F32)], [(1, D), (1, 512), (1, 512)], tm=TM_WIDE, vmem=VMEM_BIG)


def _premix_bwd(dx1, x, pieces, win_t, g_pre):
    def body(dx1_ref, x_ref, *refs):
        piece_refs, (w_ref, g_ref, dx_ref, dp_ref, dg_ref) = refs[:len(pieces)], refs[len(pieces):]
        _zero_at_start(dg_ref)
        col = 0
        for p in piece_refs:
            dp_ref[:, col:col + p.shape[1]] = p[...]
            col += p.shape[1]
        dh, dg = _rms_bwd(x_ref[...], g_ref[...], _nn(dp_ref[...], w_ref[...]))
        dg_ref[...] += dg
        dx_ref[...] = dx1_ref[...] + dh

    return _tok_call(body, "premix_bwd", [dx1, x] + list(pieces), [win_t, g_pre], [(D, F32), (PROJ, BF16)], [(1, D)],
                     tm=TM_WIDE, vmem=VMEM_BIG)


def _wgrad(a, b, name):
    k, m = a.shape
    n = b.shape[1]
    tm = 640 if m % 640 == 0 and m > 1024 else min(m, 512)
    tn = min(n, 1024)

    def body(a_ref, b_ref, o_ref):
        o_ref[...] = _tn(a_ref[...].astype(BF16), b_ref[...].astype(BF16))

    return _pcall(
        body, name=name, grid=(m // tm, n // tn),
        in_specs=[pl.BlockSpec((k, tm), lambda i, j: (0, i)), pl.BlockSpec((k, tn), lambda i, j: (0, j))],
        out_specs=pl.BlockSpec((tm, tn), lambda i, j: (i, j)),
        out_shape=jax.ShapeDtypeStruct((m, n), F32),
        compiler_params=pltpu.CompilerParams(dimension_semantics=("arbitrary", "arbitrary"), vmem_limit_bytes=VMEM_BIG),
    )(a, b)


def _wgrad_group(name, pairs, rows):
    def body(*refs):
        o_ref = refs[-1]
        for k in range(len(pairs)):
            o_ref[k * rows:(k + 1) * rows, :] = _tn(refs[2 * k][...].astype(BF16), refs[2 * k + 1][...].astype(BF16))

    in_specs, ops = [], []
    for a, b in pairs:
        in_specs += [pl.BlockSpec((a.shape[0], rows), lambda j: (0, j)), _resident(b)]
        ops += [a, b]
    return _pcall(
        body, name=name, grid=(8,), in_specs=in_specs,
        out_specs=pl.BlockSpec((None, len(pairs) * rows, D), lambda j: (j, 0, 0)),
        out_shape=jax.ShapeDtypeStruct((8, len(pairs) * rows, D), F32),
        compiler_params=pltpu.CompilerParams(dimension_semantics=("arbitrary",), vmem_limit_bytes=VMEM_BIG),
    )(*ops)


def _adam_math(w, g, m, v):
    m2 = ADAM_B1 * m + (1.0 - ADAM_B1) * g
    v2 = ADAM_B2 * v + (1.0 - ADAM_B2) * jnp.square(g)
    m_hat = m2 / (1.0 - ADAM_B1 ** ADAM_STEP)
    v_hat = v2 / (1.0 - ADAM_B2 ** ADAM_STEP)
    delta = -ADAM_LR * (m_hat / (jnp.sqrt(v_hat) + ADAM_EPS) + ADAM_WD * w)
    return delta, m2, v2


def _adamw(w, g, m, v, name):
    rows, cols = w.shape
    tr = 256 if rows % 256 == 0 else rows

    def body(w_ref, g_ref, m_ref, v_ref, d_ref, m2_ref, v2_ref):
        d_ref[...], m2_ref[...], v2_ref[...] = _adam_math(w_ref[...], g_ref[...], m_ref[...], v_ref[...])

    spec = pl.BlockSpec((tr, cols), lambda i: (i, 0))
    return _pcall(
        body, name=name, grid=(rows // tr,), in_specs=[spec] * 4, out_specs=[spec] * 3,
        out_shape=[jax.ShapeDtypeStruct(w.shape, F32)] * 3,
        compiler_params=pltpu.CompilerParams(dimension_semantics=("arbitrary",)),
    )(w, g, m, v)


def _adamw_small(gparts, ws, ms, vs):
    n = len(SMALL)

    def body(g_ref, *refs):
        w_refs, m_refs, v_refs = refs[:n], refs[n:2 * n], refs[2 * n:3 * n]
        outs, sum_ref = refs[3 * n:-1], refs[-1]
        g = g_ref[0]
        for k in range(1, 8):
            g = g + g_ref[k]
        sum_ref[...] = g
        outs[0][...] = sum_ref[17:18, 0:128]
        for t, name in enumerate(SMALL):
            r0, nr, c0, nc = SMALL_SLOT[name]
            gt = sum_ref[r0:r0 + nr, c0:c0 + nc]
            out = (gt,) + _adam_math(w_refs[t][...], gt, m_refs[t][...], v_refs[t][...])
            for o_ref, val in zip(outs[1 + 4 * t:5 + 4 * t], out):
                o_ref[...] = val

    whole = lambda s: pl.BlockSpec(s, lambda i, nd=len(s): (0,) * nd)
    ins = [gparts] + list(ws) + list(ms) + list(vs)
    out_shapes = [(1, 128)] + [a.shape for a in ws for _ in range(4)]
    return _pcall(
        body, name="adamw_small", grid=(1,), in_specs=[whole(a.shape) for a in ins],
        out_specs=[whole(s) for s in out_shapes], out_shape=[jax.ShapeDtypeStruct(s, F32) for s in out_shapes],
        scratch_shapes=[pltpu.VMEM((SMALL_ROWS, D), F32)],
        compiler_params=pltpu.CompilerParams(dimension_semantics=("arbitrary",)),
    )(*ins)


def _row_tile(rows):
    return next(t for t in (512, 400, 320) if rows % t == 0)


def _add_halves(g4, theirs, core, name):
    rows = g4.shape[2]
    tr = _row_tile(rows)

    def body(c_ref, a_ref, b_ref, o_ref):
        o_ref[...] = (a_ref[...] + b_ref[...]).astype(BF16)

    grid_spec = pltpu.PrefetchScalarGridSpec(
        num_scalar_prefetch=1, grid=(4, rows // tr),
        in_specs=[pl.BlockSpec((None, None, tr, D), lambda j, i, c: (j, c[0], i, 0)),
                  pl.BlockSpec((None, None, tr, D), lambda j, i, c: (j, 0, i, 0))],
        out_specs=pl.BlockSpec((None, tr, D), lambda j, i, c: (j, i, 0)))
    return _pcall(
        body, name=name, grid_spec=grid_spec, out_shape=jax.ShapeDtypeStruct((4, rows, D), BF16),
        compiler_params=pltpu.CompilerParams(dimension_semantics=("arbitrary", "arbitrary")),
    )(core, g4, theirs)


def _sum_adam(own, got, order, r0, w, m, v, name, transposed=False):
    n = w.shape[1] if transposed else w.shape[0]
    tr = min(n, 256)

    def body(o_ref, a_ref, b_ref, c_ref, d_ref, w_ref, m_ref, v_ref, g_ref, dl_ref, m2_ref, v2_ref):
        f = lambda r: r[...].astype(F32)
        g = ((f(a_ref) + f(b_ref)) + f(c_ref)) + f(d_ref)
        g = g.T if transposed else g
        g_ref[...] = g
        dl_ref[...], m2_ref[...], v2_ref[...] = _adam_math(w_ref[...], g, m_ref[...], v_ref[...])

    slot = lambda k: pl.BlockSpec((None, tr, D), lambda i, o: (o[k], r0 // tr + i, 0))
    wspec = pl.BlockSpec((D, tr), lambda i, o: (0, i)) if transposed else pl.BlockSpec((tr, D), lambda i, o: (i, 0))
    grid_spec = pltpu.PrefetchScalarGridSpec(
        num_scalar_prefetch=1, grid=(n // tr,), in_specs=[slot(0), slot(1), slot(2), slot(3), wspec, wspec, wspec],
        out_specs=[wspec] * 4)
    return _pcall(
        body, name=name, grid_spec=grid_spec, out_shape=[jax.ShapeDtypeStruct(w.shape, F32)] * 4,
        compiler_params=pltpu.CompilerParams(dimension_semantics=("arbitrary",)),
    )(order, own, got, got, got, w, m, v)


def _sum_chips(own, got, order, name):
    rows = own.shape[1]
    tr = _row_tile(rows)

    def body(o_ref, a_ref, b_ref, c_ref, d_ref, out_ref):
        f = lambda r: r[...].astype(F32)
        out_ref[...] = ((f(a_ref) + f(b_ref)) + f(c_ref)) + f(d_ref)

    slot = lambda k: pl.BlockSpec((None, tr, D), lambda i, o: (o[k], i, 0))
    grid_spec = pltpu.PrefetchScalarGridSpec(
        num_scalar_prefetch=1, grid=(rows // tr,), in_specs=[slot(0), slot(1), slot(2), slot(3)],
        out_specs=pl.BlockSpec((tr, D), lambda i, o: (i, 0)))
    return _pcall(
        body, name=name, grid_spec=grid_spec, out_shape=jax.ShapeDtypeStruct((rows, D), F32),
        compiler_params=pltpu.CompilerParams(dimension_semantics=("arbitrary",)),
    )(order, own, got, got, got)


def _place():
    return lax.axis_index("x"), lax.axis_index("y"), lax.axis_index("c")


def _allgather(block, name):
    def body(x_ref, out_ref, token, send_sems, recv_sems, local_sem):
        token[...] = jnp.zeros_like(token)
        x, y, c = _place()
        me, sibling = (x, y, c), (x, y, 1 - c)
        chips = [(1 - x, y), (x, 1 - y), (1 - x, 1 - y)]

        def slot(px, py, pc):
            return out_ref.at[4 * px + 2 * py + pc]

        def copy(k, blk, to, src=None):
            return pltpu.make_async_remote_copy(
                src_ref=slot(*blk) if src is None else src, dst_ref=slot(*blk),
                send_sem=send_sems.at[k], recv_sem=recv_sems.at[k], device_id=to, device_id_type=MESH)

        mine = pltpu.make_async_copy(x_ref, slot(*me), local_sem)
        mine.start()
        first = [copy(0, me, sibling, src=x_ref)]
        first += [copy(1 + j, me, (*chip, c), src=x_ref) for j, chip in enumerate(chips)]
        for cp in first:
            cp.start()
        passed = [copy(4 + j, (*chip, c), sibling) for j, chip in enumerate(chips)]
        for j, chip in enumerate(chips):
            copy(1 + j, (*chip, c), me).wait_recv()
            passed[j].start()
        copy(0, sibling, me).wait_recv()
        for j, chip in enumerate(chips):
            copy(4 + j, (*chip, 1 - c), me).wait_recv()
        for cp in first + passed:
            cp.wait_send()
        mine.wait()

    return _pcall(
        body, name=name,
        out_shape=[jax.ShapeDtypeStruct((8,) + block.shape, block.dtype), jax.ShapeDtypeStruct((8, 128), F32)],
        in_specs=[pl.BlockSpec(memory_space=pl.ANY)],
        out_specs=[pl.BlockSpec(memory_space=pl.ANY), pl.BlockSpec(memory_space=pltpu.VMEM)],
        scratch_shapes=[pltpu.SemaphoreType.DMA((7,)), pltpu.SemaphoreType.DMA((7,)), pltpu.SemaphoreType.DMA(())],
        compiler_params=pltpu.CompilerParams(has_side_effects=True),
    )(block)


HBM_SPEC = pl.BlockSpec(memory_space=pltpu.HBM)
SEM_SPEC = pl.BlockSpec(memory_space=pltpu.SEMAPHORE)
ANY_SPEC = pl.BlockSpec(memory_space=pl.ANY)
EFFECT = pltpu.SideEffectType.DATAFLOW_SIDE_EFFECTING


def _in_hbm(a):
    return pltpu.with_memory_space_constraint(a, pltpu.HBM)


def _start_copies(name, src, land_shape, plan, n):
    def body(src_ref, land_ref, send_sems, recv_sems, src_thru, land_thru, token):
        for k, (s, d, to, _) in enumerate(plan(src_ref, land_ref)):
            pltpu.make_async_remote_copy(src_ref=s, dst_ref=d, send_sem=send_sems.at[k], recv_sem=recv_sems.at[k],
                                         device_id=to, device_id_type=MESH).start()
        token[...] = jnp.zeros_like(token)

    return _pcall(
        body, name=name,
        out_shape=(pltpu.SemaphoreType.DMA((n,)), pltpu.SemaphoreType.DMA((n,)), pltpu.HBM(src.shape, src.dtype),
                   pltpu.HBM(land_shape, src.dtype), jax.ShapeDtypeStruct((8, 128), F32)),
        in_specs=(HBM_SPEC, HBM_SPEC),
        out_specs=(SEM_SPEC, SEM_SPEC, HBM_SPEC, HBM_SPEC, pl.BlockSpec(memory_space=pltpu.VMEM)),
        input_output_aliases={0: 2, 1: 3}, compiler_params=pltpu.CompilerParams(has_side_effects=EFFECT),
    )(_in_hbm(src), _in_hbm(lax.empty(land_shape, src.dtype)))


def _wait_copies(name, started, after, plan):
    send_sems, recv_sems, src_thru, land_thru, _ = started

    def body(src_ref, land_ref, send_sems, recv_sems, *rest):
        for k, (s, _, to, mine) in enumerate(plan(src_ref, land_ref)):
            cp = pltpu.make_async_remote_copy(src_ref=s, dst_ref=mine, send_sem=send_sems.at[k],
                                              recv_sem=recv_sems.at[k], device_id=to, device_id_type=MESH)
            cp.wait_send()
            cp.wait_recv()

    return _pcall(
        body, name=name,
        out_shape=(pltpu.HBM(src_thru.shape, src_thru.dtype), pltpu.HBM(land_thru.shape, land_thru.dtype)),
        in_specs=(HBM_SPEC, HBM_SPEC, SEM_SPEC, SEM_SPEC) + (ANY_SPEC,) * len(after), out_specs=(HBM_SPEC, HBM_SPEC),
        input_output_aliases={0: 0, 1: 1}, compiler_params=pltpu.CompilerParams(has_side_effects=EFFECT),
    )(src_thru, land_thru, send_sems, recv_sems, *after)


def _gather_plan(src_ref, land_ref):
    x, y, c = _place()
    peers = [(x, y, 1 - c), (1 - x, y, c), (x, 1 - y, c), (1 - x, 1 - y, c)]
    return [(src_ref, land_ref.at[4 * x + 2 * y + c], p, land_ref.at[4 * p[0] + 2 * p[1] + p[2]]) for p in peers]


def _swap_plan(src_ref, land_ref):
    x, y, c = _place()
    return [(src_ref.at[:, pl.ds(1 - c, 1)], land_ref, (x, y, 1 - c), land_ref)]


def _exchange_plan(src_ref, land_ref):
    x, y, c = _place()
    chips = [(1 - x, y), (x, 1 - y), (1 - x, 1 - y)]
    return [(src_ref.at[2 * px + py], land_ref.at[2 * x + y], (px, py, c), land_ref.at[2 * px + py]) for px, py in chips]


def _gather_forward(land, block):
    def body(land_ref, out_ref, send_sems, recv_sems):
        x, y, c = _place()
        chips = [(1 - x, y), (x, 1 - y), (1 - x, 1 - y)]

        def copy(k, px, py, pc):
            blk = out_ref.at[4 * px + 2 * py + pc]
            return pltpu.make_async_remote_copy(src_ref=blk, dst_ref=blk, send_sem=send_sems.at[k],
                                                recv_sem=recv_sems.at[k], device_id=(x, y, 1 - c), device_id_type=MESH)

        sent = [copy(k, px, py, c) for k, (px, py) in enumerate(chips)]
        for cp in sent:
            cp.start()
        for k, (px, py) in enumerate(chips):
            copy(k, px, py, 1 - c).wait_recv()
        for cp in sent:
            cp.wait_send()

    land = _pcall(
        body, name="allgather_rest_forward", out_shape=jax.ShapeDtypeStruct(land.shape, land.dtype),
        in_specs=[ANY_SPEC], out_specs=ANY_SPEC, input_output_aliases={0: 0},
        scratch_shapes=[pltpu.SemaphoreType.DMA((3,)), pltpu.SemaphoreType.DMA((3,))],
        compiler_params=pltpu.CompilerParams(has_side_effects=True),
    )(land)

    rows = block.shape[0]
    tr = rows // 4

    def place(me_ref, x_ref, land_ref, out_ref):
        out_ref[...] = x_ref[...]

    x, y, c = _place()
    grid_spec = pltpu.PrefetchScalarGridSpec(
        num_scalar_prefetch=1, grid=(rows // tr,),
        in_specs=[pl.BlockSpec((tr, D), lambda i, me: (i, 0)), ANY_SPEC],
        out_specs=pl.BlockSpec((None, tr, D), lambda i, me: (me[0], i, 0)))
    return _pcall(
        place, name="allgather_rest_own", grid_spec=grid_spec, out_shape=jax.ShapeDtypeStruct(land.shape, land.dtype),
        input_output_aliases={2: 0}, compiler_params=pltpu.CompilerParams(dimension_semantics=("arbitrary",)),
    )((4 * x + 2 * y + c).reshape(1), block, land)


class _ReduceScatter:
    def __init__(self, name, g):
        self.name = name
        rows = g.shape[1]
        self.started = _start_copies(name + "_swap_start", g.reshape(4, 2, rows, D), (4, 1, rows, D), _swap_plan, 1)
        self.token = self.started[4][0, 0]

    def halfway(self, after):
        g4, theirs = _wait_copies(self.name + "_swap_wait", self.started, after, _swap_plan)
        self.own = _add_halves(g4, theirs, lax.axis_index("c").reshape(1), self.name + "_add_halves")
        self.started = _start_copies(self.name + "_exch_start", self.own, self.own.shape, _exchange_plan, 3)
        self.token = self.started[4][0, 0]

    def finish(self, after):
        own, got = _wait_copies(self.name + "_exch_wait", self.started, after, _exchange_plan)
        chip = 2 * lax.axis_index("x") + lax.axis_index("y")
        return own, got, (chip + jnp.arange(4, dtype=jnp.int32)) % 4


def _pack_small(p, scalar=None):
    z = lambda a, n: jnp.pad(a, ((0, 0), (0, n - a.shape[1])))
    rows = [z(p['rel_bias'], D), z(p['b_fgt'], D), jnp.concatenate([p['g_fox_out'], p['g_chk_out']], axis=1)]
    rows += [p[n] for n in ('g_mix_pre', 'g_mix_post', 'g_mem_kv', 'g_mem_pre', 'g_mem_post', 'g_ff_pre', 'g_ff_post')]
    rows.append(jnp.zeros((1, D), F32) if scalar is None else z(jnp.reshape(scalar, (1, 1)), D))
    rows.append(jnp.zeros((SMALL_ROWS - 18, D), F32))
    return jnp.concatenate(rows, axis=0)


_GAP_DEV, _GAP_ROW = divmod(GATE0 + 8, N_IN)
_GAP = CHK0 - GATE0 - 8


def _in_rows_to_proj(g):
    wt = g[:, :N_IN].reshape(8 * N_IN, D)
    return jnp.concatenate([wt[:GATE0], jnp.pad(wt[GATE0:GATE0 + 8], ((0, _GAP), (0, 0))), wt[GATE0 + 8:]], axis=0)


def _proj_rows_to_in(g):
    pad = lambda a: jnp.pad(a, ((0, R_IN - a.shape[0]), (0, 0)))
    lo = N_IN * _GAP_DEV
    shards = [pad(g[N_IN * j:N_IN * (j + 1)]) for j in range(_GAP_DEV)]
    shards.append(pad(jnp.concatenate([g[lo:lo + _GAP_ROW], g[lo + _GAP_ROW + _GAP:lo + N_IN + _GAP]], axis=0)))
    shards += [pad(g[N_IN * j + _GAP:N_IN * (j + 1) + _GAP]) for j in range(_GAP_DEV + 1, 8)]
    return jnp.stack(shards)


def _local_grads(x, mem, tgt, win_t, gw_of, sm, on_grads):
    b_pad = jnp.pad(sm['b_fgt'], ((0, 0), (0, 120)))
    tbl = jnp.pad(sm['rel_bias'], ((0, 0), (0, NREL_PAD - 257)))

    h1, proj, flog = _premix_fwd(x, sm['g_mix_pre'], win_t)
    c = _gate_fwd(flog, b_pad)
    ct3 = c[:, :8].T.reshape(4, 2, T)
    o_f, lse_f = _fox_fwd(proj, c, ct3)
    vt3 = _relvec_fwd(tbl).reshape(4, 2, VW)
    kvp = jnp.pad(proj[:, CHK0 + 512:], ((LEFT, 0), (0, 0)))
    o_c, lse_c = _chk_fwd(proj, kvp, vt3)
    gw = gw_of([o_f, o_c])
    w_out, w_mq, w_mk, w_mv, w_mo, w1_t, w2 = (_wblk(gw, n) for n in ('w_out', 'w_mq', 'w_mk', 'w_mv', 'w_mo', 'w_ff1', 'w_ff2'))
    ycat, z, x1, h2, qm = _postmix_fwd(x, o_f, o_c, sm['g_fox_out'], sm['g_chk_out'], w_out,
                                       sm['g_mix_post'], sm['g_mem_pre'], w_mq)
    memn, km, vm = _memkv_fwd(mem, sm['g_mem_kv'], w_mk, w_mv)
    om, ym, x2, h3 = _mem_fwd(qm, x1, km, vm, w_mo, sm['g_mem_post'], sm['g_ff_pre'])
    a, y3, dx3, loss_acc = _ffn_fwd(h3, x2, tgt, w1_t, w2, sm['g_ff_post'])

    gs = {}
    dx2, da, dy3, r, gs['g_ff_post'], gs['g_ff_pre'] = _ffn_bwd(dx3, y3, x2, a, w1_t, w2, sm['g_ff_post'], sm['g_ff_pre'])
    zero = on_grads('A', _wgrad_group("wgrad_ff", [(da, h3), (r, dy3)], 512), None)
    dx1, dym, dqm, dkm, dvm, gs['g_mem_post'], gs['g_mem_pre'] = _mem_bwd(
        dx2, ym, x1, qm, km, vm, w_mo, w_mq, sm['g_mem_post'] + zero, sm['g_mem_pre'])
    zero = on_grads('A halfway', None, [dx1])
    gs['g_mem_kv'] = _memkv_bwd(dkm, dvm, mem, w_mk, w_mv)
    dz, dof, doc, gs['g_mix_post'], gs['g_fox_out'], gs['g_chk_out'] = _postmix_bwd(
        dx1, z, o_f, o_c, w_out, sm['g_mix_post'] + zero, sm['g_fox_out'], sm['g_chk_out'])
    zero = on_grads('B', _wgrad_group("wgrad_mem_out", [(ycat, dz), (h2, dqm), (memn, dkm), (memn, dvm), (om, dym)], 128), None)
    dq_f, dk_f, dv_f, dct, dcq = _fox_bwd(proj, c, ct3 + zero, o_f, lse_f, dof)
    zero = on_grads('B halfway', None, [dq_f])
    dq_c, dk_c, dv_c, gv = _chk_bwd(proj, kvp, vt3 + zero, o_c, lse_c, doc)
    gs['rel_bias'] = _relvec_bwd(gv.reshape(8, VW))[:, :257]
    dc = jnp.pad(dct.reshape(8, T).T + dcq[:, :, :2].transpose(1, 0, 2).reshape(T, 8), ((0, 0), (0, 120)))
    dflog, db = _gate_bwd(dc, flog, b_pad)
    gs['b_fgt'] = db[0:1, :8]
    grad_x, dproj, gs['g_mix_pre'] = _premix_bwd(dx1, x, [dq_f, dk_f, dv_f, dflog, dq_c, dk_c, dv_c], win_t, sm['g_mix_pre'])
    on_grads('C', _proj_rows_to_in(_wgrad(dproj, h1, "wgrad_in")), None)
    return loss_acc[0, 0], grad_x, gs


def kernel(x, mem, w_in, b_fgt, rel_bias, g_fox_out, g_chk_out, w_out, g_mix_pre, g_mix_post, g_mem_kv, w_mq, w_mk, w_mv, w_mo, g_mem_pre, g_mem_post, w_ff1, w_ff2, g_ff_pre, g_ff_post, loss_target, m_w_in, m_b_fgt, m_rel_bias, m_g_fox_out, m_g_chk_out, m_w_out, m_g_mix_pre, m_g_mix_post, m_g_mem_kv, m_w_mq, m_w_mk, m_w_mv, m_w_mo, m_g_mem_pre, m_g_mem_post, m_w_ff1, m_w_ff2, m_g_ff_pre, m_g_ff_post, v_w_in, v_b_fgt, v_rel_bias, v_g_fox_out, v_g_chk_out, v_w_out, v_g_mix_pre, v_g_mix_post, v_g_mem_kv, v_w_mq, v_w_mk, v_w_mv, v_w_mo, v_g_mem_pre, v_g_mem_post, v_w_ff1, v_w_ff2, v_g_ff_pre, v_g_ff_post):
    args = dict(locals())
    two_d = lambda a: a.reshape(a.shape[-2:])
    w = {n: two_d(args[n]) for n in WEIGHTS}
    m = {n: two_d(args['m_' + n]) for n in WEIGHTS}
    v = {n: two_d(args['v_' + n]) for n in WEIGHTS}

    sm = {n: w[n] for n in SMALL}
    shard_in = jnp.pad(w['w_in'].T, ((0, R_IN - N_IN), (0, 0))).astype(BF16)
    gathered_in, zero = _allgather(shard_in, "allgather_w_in")
    win_t = _in_rows_to_proj(gathered_in)
    shard_rest = (jnp.concatenate([w['w_ff1'].T, w['w_ff2'], w['w_out'], w['w_mq'], w['w_mk'], w['w_mv'], w['w_mo']],
                                  axis=0) + zero[0, 0]).astype(BF16)
    rest = _start_copies("allgather_rest_start", shard_rest, (8, R_REST, D), _gather_plan, 4)
    sm['g_mix_pre'] = sm['g_mix_pre'] + rest[4][0, 0]

    def gw_of(after):
        block, land = _wait_copies("allgather_rest_wait", rest, after, _gather_plan)
        return _gather_forward(land, block)

    rs = {}

    def on_grads(stage, g, after):
        if stage.endswith('halfway'):
            rs[stage[0]].halfway(after)
            return rs[stage[0]].token
        rs[stage] = _ReduceScatter("rs_" + stage.lower(), g)
        return rs[stage].token

    loss_local, grad_x, gs = _local_grads(x[0], mem[0], loss_target[0], win_t, gw_of, sm, on_grads)
    grads, deltas, new_m, new_v = {}, {}, {}, {}

    def update(n, out):
        grads[n], deltas[n], new_m[n], new_v[n] = out

    gparts, _ = _allgather(_pack_small(gs, loss_local + rs['C'].token), "allgather_small_grads")
    small = _adamw_small(gparts, [w[n] for n in SMALL], [m[n] for n in SMALL], [v[n] for n in SMALL])
    loss = small[0][0, 0]
    for t, n in enumerate(SMALL):
        update(n, small[1 + 4 * t:5 + 4 * t])
    rs['C'].halfway([small[0]])

    own, got, order = rs['A'].finish([grad_x, rs['C'].started[4]])
    update('w_ff1', _sum_adam(own, got, order, 0, w['w_ff1'], m['w_ff1'], v['w_ff1'], "adamw_w_ff1", transposed=True))
    update('w_ff2', _sum_adam(own, got, order, 512, w['w_ff2'], m['w_ff2'], v['w_ff2'], "adamw_w_ff2"))
    own, got, order = rs['B'].finish([grad_x, rs['C'].started[4]])
    for k, n in enumerate(('w_out', 'w_mq', 'w_mk', 'w_mv', 'w_mo')):
        update(n, _sum_adam(own, got, order, 128 * k, w[n], m[n], v[n], "adamw_" + n))

    g_in = _sum_chips(*rs['C'].finish([new_v[n] for n in BIG if n != 'w_in']), "rs_c_sum_chips")[:N_IN].T
    update('w_in', (g_in,) + tuple(_adamw(w['w_in'], g_in, m['w_in'], v['w_in'], "adamw_w_in")))

    out = [loss, grad_x[None]]
    for group in (grads, deltas, new_m, new_v):
        out += [group[n].reshape(args[n].shape) for n in WEIGHTS]
    return tuple(out)
```

```python
import functools

import jax
import jax.numpy as jnp
from jax import lax
from jax.experimental import pallas as pl
from jax.experimental.pallas import tpu as pltpu

F32 = jnp.float32
BF16 = jnp.bfloat16
MESH = pl.DeviceIdType.MESH

T = 2048
D = 1024
NMEM = 256
DFF = 4096
EPS = 1e-6
TM = 256
TM_WIDE = 512
TQ = 256
FQ = 512
HD = 64
SCALE = HD ** -0.5
MEM_HEADS = 4
MEM_HD = 256
MEM_SCALE = MEM_HD ** -0.5
NEG = -1e30
LEFT = 512
WIN = LEFT + TQ
VW = 1024
NREL_PAD = 384
PROJ = 3200
GATE0 = 1536
CHK0 = 1664
VMEM_BIG = 56 * 1024 * 1024

ADAM_LR = 0.001
ADAM_B1 = 0.9
ADAM_B2 = 0.999
ADAM_EPS = 1e-08
ADAM_WD = 0.01
ADAM_STEP = 10

N_IN = 385
R_IN = 400
R_REST = 1664
W_ROWS = {'w_ff1': (0, 512), 'w_ff2': (512, 512),
          'w_out': (1024, 128), 'w_mq': (1152, 128), 'w_mk': (1280, 128), 'w_mv': (1408, 128), 'w_mo': (1536, 128)}
R_A, R_B = 1024, 640
SMALL_ROWS = 24
SMALL_SLOT = {'rel_bias': (0, 8, 0, 257), 'b_fgt': (8, 1, 0, 8), 'g_fox_out': (9, 1, 0, 512), 'g_chk_out': (9, 1, 512, 512),
              'g_mix_pre': (10, 1, 0, 1024), 'g_mix_post': (11, 1, 0, 1024), 'g_mem_kv': (12, 1, 0, 1024),
              'g_mem_pre': (13, 1, 0, 1024), 'g_mem_post': (14, 1, 0, 1024), 'g_ff_pre': (15, 1, 0, 1024),
              'g_ff_post': (16, 1, 0, 1024)}

WEIGHTS = ['w_in', 'b_fgt', 'rel_bias', 'g_fox_out', 'g_chk_out', 'w_out', 'g_mix_pre', 'g_mix_post', 'g_mem_kv',
           'w_mq', 'w_mk', 'w_mv', 'w_mo', 'g_mem_pre', 'g_mem_post', 'w_ff1', 'w_ff2', 'g_ff_pre', 'g_ff_post']
BIG = ['w_in', 'w_out', 'w_mq', 'w_mk', 'w_mv', 'w_mo', 'w_ff1', 'w_ff2']
SMALL = [n for n in WEIGHTS if n not in BIG]


def _pcall(body, **kw):
    return pl.pallas_call(body, **kw)


def _nn(a, b):
    return jnp.dot(a, b, preferred_element_type=F32)


def _nt(a, b):
    return lax.dot_general(a, b, (((1,), (1,)), ((), ())), preferred_element_type=F32)


def _tn(a, b):
    return lax.dot_general(a, b, (((0,), (0,)), ((), ())), preferred_element_type=F32)


def _w(ref):
    v = ref[...]
    return v if v.ndim == 2 else v.reshape(-1, v.shape[-1])


def _rstd(x):
    return lax.rsqrt(jnp.mean(x * x, axis=-1, keepdims=True) + EPS)


def _rms(x, g):
    return x * _rstd(x) * g


def _rms_bwd(x, g, dy):
    r = _rstd(x)
    xh = x * r
    dg = jnp.sum(dy * xh, axis=0, keepdims=True)
    dxh = dy * g
    dx = r * (dxh - xh * jnp.mean(dxh * xh, axis=-1, keepdims=True))
    return dx, dg


def _resident(a):
    if isinstance(a, tuple):
        _, shape, index = a
        return pl.BlockSpec(shape, lambda *_: index, pipeline_mode=pl.Buffered(1))
    return pl.BlockSpec(a.shape, lambda *_, nd=a.ndim: (0,) * nd, pipeline_mode=pl.Buffered(1))


def _wblk(gw, name):
    r0, rows = W_ROWS[name]
    return (gw, (8, rows, D), (0, r0 // rows, 0))


def _tok_call(body, name, tiled, full, outs_tiled, outs_acc=(), rows=T, tm=TM, vmem=None):
    in_specs = [pl.BlockSpec((tm, a.shape[1]), lambda i: (i, 0)) for a in tiled]
    in_specs += [_resident(a) for a in full]
    full = [a[0] if isinstance(a, tuple) else a for a in full]
    out_shape = [jax.ShapeDtypeStruct((rows, c), dt) for c, dt in outs_tiled]
    out_shape += [jax.ShapeDtypeStruct(s, F32) for s in outs_acc]
    out_specs = [pl.BlockSpec((tm, c), lambda i: (i, 0)) for c, _ in outs_tiled]
    out_specs += [pl.BlockSpec(s, lambda i, nd=len(s): (0,) * nd) for s in outs_acc]
    return _pcall(
        body, name=name, grid=(rows // tm,), in_specs=in_specs, out_specs=out_specs, out_shape=out_shape,
        compiler_params=pltpu.CompilerParams(dimension_semantics=("arbitrary",), vmem_limit_bytes=vmem),
    )(*tiled, *full)


def _one_call(body, name, ins, outs, vmem=None):
    whole = lambda s: pl.BlockSpec(s, lambda i, nd=len(s): (0,) * nd)
    return _pcall(
        body, name=name, grid=(1,), in_specs=[_resident(a) for a in ins], out_specs=[whole(s) for s, _ in outs],
        out_shape=[jax.ShapeDtypeStruct(s, dt) for s, dt in outs],
        compiler_params=pltpu.CompilerParams(dimension_semantics=("arbitrary",), vmem_limit_bytes=vmem),
    )(*[a[0] if isinstance(a, tuple) else a for a in ins])


def _premix_fwd(x, g_pre, win_t):
    def body(x_ref, g_ref, w_ref, h_ref, proj_ref, flog_ref):
        h = _rms(x_ref[...], g_ref[...]).astype(BF16)
        h_ref[...] = h
        p = _nt(h, w_ref[...])
        proj_ref[...] = p.astype(BF16)
        flog_ref[...] = p[:, GATE0:GATE0 + 128]

    return _tok_call(body, "premix_fwd", [x], [g_pre, win_t],
                     [(D, BF16), (PROJ, BF16), (128, F32)], tm=TM_WIDE, vmem=VMEM_BIG)


def _postmix_fwd(x, o_f, o_c, g_fo, g_co, w_out, g_post, g_mpre, w_mq):
    def body(x_ref, of_ref, oc_ref, gfo_ref, gco_ref, wo_ref, gp_ref, gm_ref, wq_ref,
             y_ref, z_ref, x1_ref, h2_ref, qm_ref):
        y_ref[:, :512] = _rms(of_ref[...], gfo_ref[...]).astype(BF16)
        y_ref[:, 512:] = _rms(oc_ref[...], gco_ref[...]).astype(BF16)
        z = _nn(y_ref[...], _w(wo_ref))
        z_ref[...] = z
        x1 = x_ref[...] + _rms(z, gp_ref[...])
        x1_ref[...] = x1
        h2 = _rms(x1, gm_ref[...]).astype(BF16)
        h2_ref[...] = h2
        qm_ref[...] = _nn(h2, _w(wq_ref)).astype(BF16)

    return _tok_call(body, "postmix_fwd", [x, o_f, o_c], [g_fo, g_co, w_out, g_post, g_mpre, w_mq],
                     [(D, BF16), (D, F32), (D, F32), (D, BF16), (D, BF16)], tm=TM_WIDE, vmem=VMEM_BIG)


def _memkv_fwd(mem, g_kv, w_mk, w_mv):
    def body(m_ref, g_ref, wk_ref, wv_ref, mn_ref, k_ref, v_ref):
        mn = _rms(m_ref[...], g_ref[...]).astype(BF16)
        mn_ref[...] = mn
        k_ref[...] = _nn(mn, _w(wk_ref)).astype(BF16)
        v_ref[...] = _nn(mn, _w(wv_ref)).astype(BF16)

    return _tok_call(body, "memkv_fwd", [mem], [g_kv, w_mk, w_mv],
                     [(D, BF16), (D, BF16), (D, BF16)], rows=NMEM, tm=NMEM, vmem=VMEM_BIG)


def _mem_fwd(qm, x1, km, vm, w_mo, g_post, g_fpre):
    def body(q_ref, x1_ref, k_ref, v_ref, wo_ref, gp_ref, gf_ref, om_ref, ym_ref, x2_ref, h3_ref):
        for h in range(MEM_HEADS):
            sl = slice(h * MEM_HD, (h + 1) * MEM_HD)
            s = _nt(q_ref[:, sl], k_ref[:, sl]) * MEM_SCALE
            p = jnp.exp(s - jnp.max(s, axis=-1, keepdims=True))
            p = p / jnp.sum(p, axis=-1, keepdims=True)
            om_ref[:, sl] = _nn(p.astype(BF16), v_ref[:, sl]).astype(BF16)
        ym = _nn(om_ref[...], _w(wo_ref))
        ym_ref[...] = ym
        x2 = x1_ref[...] + _rms(ym, gp_ref[...])
        x2_ref[...] = x2
        h3_ref[...] = _rms(x2, gf_ref[...]).astype(BF16)

    return _tok_call(body, "mem_fwd", [qm, x1], [km, vm, w_mo, g_post, g_fpre],
                     [(D, BF16), (D, F32), (D, F32), (D, BF16)], tm=TM_WIDE, vmem=VMEM_BIG)


def _ffn_fwd(h3, x2, tgt, w1_t, w2, g_post):
    def body(h_ref, x2_ref, t_ref, w1_ref, w2_ref, g_ref, a_ref, y_ref, dx_ref, loss_ref):
        @pl.when(pl.program_id(0) == 0)
        def _():
            loss_ref[...] = jnp.zeros_like(loss_ref)

        h = h_ref[...]
        y = jnp.zeros((TM_WIDE, D), F32)
        for c in range(4):
            cols = slice(c * (DFF // 4), (c + 1) * (DFF // 4))
            a = _nt(h, w1_ref[2 * c:2 * c + 2].reshape(DFF // 4, D))
            a_ref[:, cols] = a.astype(BF16)
            y = y + _nn(jnp.square(jnp.maximum(a, 0.0)).astype(BF16), w2_ref[2 * c:2 * c + 2].reshape(DFF // 4, D))
        y_ref[...] = y
        e = x2_ref[...] + _rms(y, g_ref[...]) - t_ref[...]
        dx_ref[...] = e * (1.0 / D)
        loss_ref[...] += 0.5 * jnp.sum(jnp.sum(e * e, axis=-1, keepdims=True) * (1.0 / D))

    return _tok_call(body, "ffn_fwd", [h3, x2, tgt], [w1_t, w2, g_post],
                     [(DFF, BF16), (D, F32), (D, F32)], [(8, 128)], tm=TM_WIDE, vmem=VMEM_BIG)


def _tri(lower):
    r = lax.broadcasted_iota(jnp.int32, (128, 128), 0)
    c = lax.broadcasted_iota(jnp.int32, (128, 128), 1)
    return jnp.where(r >= c if lower else c >= r, 1.0, 0.0).astype(F32)


def _hdot(a, b):
    return jnp.dot(a, b, preferred_element_type=F32, precision=lax.Precision.HIGHEST)


def _gate_fwd(flog, b_pad):
    def body(f_ref, b_ref, c_ref):
        tri = _tri(True)

        def step(i, carry):
            rows = pl.ds(pl.multiple_of(i * 128, 128), 128)
            z = f_ref[rows, :] + b_ref[...]
            lf = jnp.minimum(z, 0.0) - jnp.log(1.0 + jnp.exp(-jnp.abs(z)))
            cb = _hdot(tri, lf) + carry
            c_ref[rows, :] = cb
            return cb[127:128, :]

        lax.fori_loop(0, T // 128, step, jnp.zeros((1, 128), F32))

    return _one_call(body, "gate_fwd", [flog, b_pad], [((T, 128), F32)])[0]


def _gate_bwd(dc, flog, b_pad):
    def body(dc_ref, f_ref, b_ref, df_ref, db_ref):
        tri = _tri(False)

        def step(j, carry):
            run, db = carry
            i = T // 128 - 1 - j
            rows = pl.ds(pl.multiple_of(i * 128, 128), 128)
            dcb = dc_ref[rows, :]
            rb = _hdot(tri, dcb) + run
            z = f_ref[rows, :] + b_ref[...]
            df = rb * (1.0 / (1.0 + jnp.exp(z)))
            df_ref[rows, :] = df.astype(BF16)
            return run + jnp.sum(dcb, axis=0, keepdims=True), db + jnp.sum(df, axis=0, keepdims=True)

        _, db = lax.fori_loop(0, T // 128, step, (jnp.zeros((1, 128), F32), jnp.zeros((1, 128), F32)))
        db_ref[...] = jnp.broadcast_to(db, (8, 128))

    return _one_call(body, "gate_bwd", [dc, flog, b_pad], [((T, 128), BF16), ((8, 128), F32)])


def _lane_lo(rows=TQ):
    return lax.broadcasted_iota(jnp.int32, (rows, 128), 1) < HD


def _half(v, lo, a, scale=None):
    keep = lo if a == 0 else jnp.logical_not(lo)
    v = v.astype(F32) if scale is None else v.astype(F32) * scale
    return jnp.where(keep, v, 0.0).astype(BF16)


def _fox_specs():
    return [pl.BlockSpec((FQ, 128), lambda h, i: (i, h)),
            pl.BlockSpec((T, 128), lambda h, i: (0, 4 + h)),
            pl.BlockSpec((T, 128), lambda h, i: (0, 8 + h))]


def _lane_pick(x, at):
    lane = lax.broadcasted_iota(jnp.int32, x.shape, 1)
    return jnp.sum(jnp.where(lane == at, x, 0.0), axis=-1, keepdims=True)


def _fox_fwd(proj, c, ct3):
    def body(q_ref, k_ref, v_ref, c_ref, ct_ref, o_ref, l_ref):
        i = pl.program_id(1)
        lo = _lane_lo(FQ)
        causal = lax.broadcasted_iota(jnp.int32, (FQ, FQ), 1) <= lax.broadcasted_iota(jnp.int32, (FQ, FQ), 0)
        q = q_ref[...]
        qs = [_half(q, lo, a, SCALE) for a in range(2)]
        cqs = [_lane_pick(c_ref[...], 2 * pl.program_id(0) + a) for a in range(2)]

        def tile(off, carry, diagonal):
            kblk = k_ref[pl.ds(off, FQ), :]
            vblk = v_ref[pl.ds(off, FQ), :]
            new = []
            for a in range(2):
                m, l, acc = carry[a]
                s = _nt(qs[a], kblk) + (cqs[a] - ct_ref[a:a + 1, pl.ds(off, FQ)])
                if diagonal:
                    s = jnp.where(causal, s, NEG)
                m2 = jnp.maximum(m, jnp.max(s, axis=-1, keepdims=True))
                p = jnp.exp(s - m2)
                alpha = jnp.exp(m - m2)
                new.append((m2, alpha * l + jnp.sum(p, axis=-1, keepdims=True),
                            alpha * acc + _nn(p.astype(BF16), vblk)))
            return tuple(new)

        init = (jnp.full((FQ, 1), NEG, F32), jnp.zeros((FQ, 1), F32), jnp.zeros((FQ, 128), F32))
        carry = lax.fori_loop(0, i, lambda kb, c: tile(pl.multiple_of(kb * FQ, FQ), c, False), (init, init))
        carry = tile(pl.multiple_of(i * FQ, FQ), carry, True)
        outs = []
        for a in range(2):
            m, l, acc = carry[a]
            outs.append(acc / l)
            l_ref[:, 128 * a:128 * a + 128] = jnp.broadcast_to(m + jnp.log(l), (FQ, 128))
        o_ref[...] = jnp.where(lo, outs[0], outs[1])

    return _pcall(
        body, name="fox_fwd", grid=(4, T // FQ),
        in_specs=_fox_specs() + [pl.BlockSpec((FQ, 128), lambda h, i: (i, 0)),
                                 pl.BlockSpec((None, 2, T), lambda h, i: (h, 0, 0))],
        out_specs=[pl.BlockSpec((FQ, 128), lambda h, i: (i, h)), pl.BlockSpec((FQ, 256), lambda h, i: (i, h))],
        out_shape=[jax.ShapeDtypeStruct((T, 512), F32), jax.ShapeDtypeStruct((T, 1024), F32)],
        compiler_params=pltpu.CompilerParams(dimension_semantics=("arbitrary", "arbitrary"), vmem_limit_bytes=VMEM_BIG),
    )(proj, proj, proj, c, ct3)


def _fox_bwd(proj, c, ct3, o, lse, do):
    def body(q_ref, k_ref, v_ref, c_ref, ct_ref, o_ref, l_ref, do_ref, dq_ref, dkb_ref, dvb_ref, dct_ref, dcq_ref,
             dk_ref, dv_ref):
        i = pl.program_id(1)

        @pl.when(i == 0)
        def _():
            dk_ref[...] = jnp.zeros_like(dk_ref)
            dv_ref[...] = jnp.zeros_like(dv_ref)
            dct_ref[...] = jnp.zeros_like(dct_ref)

        lo = _lane_lo(FQ)
        causal = lax.broadcasted_iota(jnp.int32, (FQ, FQ), 1) <= lax.broadcasted_iota(jnp.int32, (FQ, FQ), 0)
        q = q_ref[...]
        do_v = do_ref[...]
        prod = do_v * o_ref[...]
        qs = [_half(q, lo, a, SCALE) for a in range(2)]
        dos = [_half(do_v, lo, a) for a in range(2)]
        deltas = [jnp.sum(jnp.where(lo if a == 0 else jnp.logical_not(lo), prod, 0.0), axis=-1, keepdims=True)
                  for a in range(2)]
        cqs = [_lane_pick(c_ref[...], 2 * pl.program_id(0) + a) for a in range(2)]
        las = [l_ref[:, 128 * a:128 * a + 1] for a in range(2)]

        def tile(off, carry, diagonal):
            kblk = k_ref[pl.ds(off, FQ), :]
            vblk = v_ref[pl.ds(off, FQ), :]
            new = []
            dk = jnp.zeros((FQ, 128), F32)
            dv = jnp.zeros((FQ, 128), F32)
            for a in range(2):
                dq_acc, rs = carry[a]
                s = _nt(qs[a], kblk) + (cqs[a] - ct_ref[a:a + 1, pl.ds(off, FQ)])
                if diagonal:
                    s = jnp.where(causal, s, NEG)
                p = jnp.exp(s - las[a])
                ds = p * (_nt(dos[a], vblk) - deltas[a])
                dsb = ds.astype(BF16)
                dk = dk + _tn(dsb, qs[a])
                dv = dv + _tn(p.astype(BF16), dos[a])
                dct_ref[a:a + 1, pl.ds(off, FQ)] -= jnp.sum(ds, axis=0, keepdims=True)
                new.append((dq_acc + _nn(dsb, kblk), rs + jnp.sum(ds, axis=-1, keepdims=True)))
            dk_ref[pl.ds(off, FQ), :] += dk
            dv_ref[pl.ds(off, FQ), :] += dv
            return tuple(new)

        init = (jnp.zeros((FQ, 128), F32), jnp.zeros((FQ, 1), F32))
        carry = lax.fori_loop(0, i, lambda kb, c: tile(pl.multiple_of(kb * FQ, FQ), c, False), (init, init))
        carry = tile(pl.multiple_of(i * FQ, FQ), carry, True)
        lane = lax.broadcasted_iota(jnp.int32, (FQ, 128), 1)
        dcq_ref[...] = jnp.where(lane == 0, carry[0][1], jnp.where(lane == 1, carry[1][1], 0.0))
        dq_ref[...] = (jnp.where(lo, carry[0][0], carry[1][0]) * SCALE).astype(BF16)

        @pl.when(i == T // FQ - 1)
        def _():
            dkb_ref[...] = dk_ref[...].astype(BF16)
            dvb_ref[...] = dv_ref[...].astype(BF16)

    blk = pl.BlockSpec((FQ, 128), lambda h, i: (i, h))
    wide = pl.BlockSpec((FQ, 256), lambda h, i: (i, h))
    rows = pl.BlockSpec((None, 2, T), lambda h, i: (h, 0, 0))
    col = pl.BlockSpec((T, 128), lambda h, i: (0, h))
    return _pcall(
        body, name="fox_bwd", grid=(4, T // FQ),
        in_specs=_fox_specs() + [pl.BlockSpec((FQ, 128), lambda h, i: (i, 0)), rows, blk, wide, blk],
        out_specs=[blk, col, col, rows, pl.BlockSpec((None, FQ, 128), lambda h, i: (h, i, 0))],
        out_shape=[jax.ShapeDtypeStruct((T, 512), BF16), jax.ShapeDtypeStruct((T, 512), BF16),
                   jax.ShapeDtypeStruct((T, 512), BF16), jax.ShapeDtypeStruct((4, 2, T), F32),
                   jax.ShapeDtypeStruct((4, T, 128), F32)],
        scratch_shapes=[pltpu.VMEM((T, 128), F32), pltpu.VMEM((T, 128), F32)],
        compiler_params=pltpu.CompilerParams(dimension_semantics=("arbitrary", "arbitrary"), vmem_limit_bytes=VMEM_BIG),
    )(proj, proj, proj, c, ct3, o, lse, do)


AUG_ROWSUM, AUG_COLSUM = 67, 64


def _fox_prep(proj, c2):
    def body(q_ref, k_ref, c_ref, qa_ref, ka_ref):
        lane = lax.broadcasted_iota(jnp.int32, (FQ, 128), 1)
        q = q_ref[...].astype(F32) * SCALE
        k = k_ref[...].astype(F32)
        for a in range(2):
            c = c_ref[:, 128 * a:128 * a + 128]
            hi = c.astype(BF16).astype(F32)
            mid = (c - hi).astype(BF16).astype(F32)
            lo = c - hi - mid
            parts = lambda first, sign: jnp.where(lane == first, sign * hi, jnp.where(lane == first + 1, sign * mid, sign * lo))
            qd = q if a == 0 else pltpu.roll(q, 64, 1)
            kd = k if a == 0 else pltpu.roll(k, 64, 1)
            ones = jnp.ones((FQ, 128), F32)
            qa = jnp.where(lane < 64, qd, jnp.where(lane < 67, ones, jnp.where(lane < 70, parts(67, 1.0), 0.0)))
            ka = jnp.where(lane < 64, kd, jnp.where(lane < 67, parts(64, -1.0), jnp.where(lane < 70, ones, 0.0)))
            qa_ref[:, 128 * a:128 * a + 128] = qa.astype(BF16)
            ka_ref[:, 128 * a:128 * a + 128] = ka.astype(BF16)

    wide = pl.BlockSpec((FQ, 256), lambda h, i: (i, h))
    return _pcall(
        body, name="fox_prep", grid=(4, T // FQ),
        in_specs=[pl.BlockSpec((FQ, 128), lambda h, i: (i, h)), pl.BlockSpec((FQ, 128), lambda h, i: (i, 4 + h)), wide],
        out_specs=[wide, wide], out_shape=[jax.ShapeDtypeStruct((T, 1024), BF16)] * 2,
        compiler_params=pltpu.CompilerParams(dimension_semantics=("arbitrary", "arbitrary")),
    )(proj, proj, c2)


def _fox_aug_specs():
    return [pl.BlockSpec((FQ, 256), lambda h, i: (i, h)), pl.BlockSpec((T, 256), lambda h, i: (0, h)),
            pl.BlockSpec((T, 128), lambda h, i: (0, 8 + h))]


def _causal():
    return lax.broadcasted_iota(jnp.int32, (FQ, FQ), 1) <= lax.broadcasted_iota(jnp.int32, (FQ, FQ), 0)


def _fox_fwd_aug(q_aug, k_aug, proj):
    def body(q_ref, k_ref, v_ref, o_ref, l_ref):
        i = pl.program_id(1)
        lo = _lane_lo(FQ)
        causal = _causal()
        qs = [q_ref[:, 128 * a:128 * a + 128] for a in range(2)]

        def tile(off, carry, diagonal):
            vblk = v_ref[pl.ds(off, FQ), :]
            new = []
            for a in range(2):
                m, l, acc = carry[a]
                s = _nt(qs[a], k_ref[pl.ds(off, FQ), 128 * a:128 * a + 128])
                if diagonal:
                    s = jnp.where(causal, s, NEG)
                m2 = jnp.maximum(m, jnp.max(s, axis=-1, keepdims=True))
                p = jnp.exp(s - m2)
                alpha = jnp.exp(m - m2)
                new.append((m2, alpha * l + jnp.sum(p, axis=-1, keepdims=True),
                            alpha * acc + _nn(p.astype(BF16), vblk)))
            return tuple(new)

        init = (jnp.full((FQ, 1), NEG, F32), jnp.zeros((FQ, 1), F32), jnp.zeros((FQ, 128), F32))
        carry = lax.fori_loop(0, i, lambda kb, c: tile(pl.multiple_of(kb * FQ, FQ), c, False), (init, init))
        carry = tile(pl.multiple_of(i * FQ, FQ), carry, True)
        outs = []
        for a in range(2):
            m, l, acc = carry[a]
            outs.append(acc / l)
            l_ref[:, 128 * a:128 * a + 128] = jnp.broadcast_to(m + jnp.log(l), (FQ, 128))
        o_ref[...] = jnp.where(lo, outs[0], outs[1])

    return _pcall(
        body, name="fox_fwd", grid=(4, T // FQ), in_specs=_fox_aug_specs(),
        out_specs=[pl.BlockSpec((FQ, 128), lambda h, i: (i, h)), pl.BlockSpec((FQ, 256), lambda h, i: (i, h))],
        out_shape=[jax.ShapeDtypeStruct((T, 512), F32), jax.ShapeDtypeStruct((T, 1024), F32)],
        compiler_params=pltpu.CompilerParams(dimension_semantics=("arbitrary", "arbitrary"), vmem_limit_bytes=VMEM_BIG),
    )(q_aug, k_aug, proj)


def _fox_bwd_aug(q_aug, k_aug, proj, o, lse, do, after):
    nq = T // FQ

    def body(q_ref, k_ref, v_ref, o_ref, l_ref, do_ref, after_ref, dq_ref, dkb_ref, dvb_ref, dcq_ref, dck_ref, dk_ref,
             dv_ref):
        i = pl.program_id(1)

        @pl.when(i == 0)
        def _():
            dk_ref[...] = jnp.zeros_like(dk_ref)
            dv_ref[...] = jnp.zeros_like(dv_ref)

        lo = _lane_lo(FQ)
        lane = lax.broadcasted_iota(jnp.int32, (FQ, 128), 1)
        causal = _causal()
        do_v = do_ref[...]
        prod = do_v * o_ref[...]
        qs = [q_ref[:, 128 * a:128 * a + 128] for a in range(2)]
        dos = [_half(do_v, lo, a) for a in range(2)]
        deltas = [jnp.sum(jnp.where(lo if a == 0 else jnp.logical_not(lo), prod, 0.0), axis=-1, keepdims=True)
                  for a in range(2)]
        las = [l_ref[:, 128 * a:128 * a + 1] for a in range(2)]

        def tile(off, carry, diagonal):
            vblk = v_ref[pl.ds(off, FQ), :]
            new = []
            dv = jnp.zeros((FQ, 128), F32)
            for a in range(2):
                kblk = k_ref[pl.ds(off, FQ), 128 * a:128 * a + 128]
                s = _nt(qs[a], kblk)
                if diagonal:
                    s = jnp.where(causal, s, NEG)
                p = jnp.exp(s - las[a])
                dsb = (p * (_nt(dos[a], vblk) - deltas[a])).astype(BF16)
                dk_ref[a, pl.ds(off, FQ), :] += _tn(dsb, qs[a])
                dv = dv + _tn(p.astype(BF16), dos[a])
                new.append(carry[a] + _nn(dsb, kblk))
            dv_ref[pl.ds(off, FQ), :] += dv
            return tuple(new)

        init = jnp.zeros((FQ, 128), F32)
        dqs = lax.fori_loop(0, i, lambda kb, c: tile(pl.multiple_of(kb * FQ, FQ), c, False), (init, init))
        dqs = tile(pl.multiple_of(i * FQ, FQ), dqs, True)
        pick = lambda x, at: jnp.sum(jnp.where(lane == at, x, 0.0), axis=-1, keepdims=True)
        for a in range(2):
            dcq_ref[:, 128 * a:128 * a + 128] = jnp.broadcast_to(pick(dqs[a], AUG_ROWSUM), (FQ, 128))
        dq_ref[...] = (jnp.where(lo, dqs[0], pltpu.roll(dqs[1], 64, 1)) * SCALE).astype(BF16)

        @pl.when(i == nq - 1)
        def _():
            big_lane = lax.broadcasted_iota(jnp.int32, (T, 128), 1)
            dkb_ref[...] = jnp.where(big_lane < 64, dk_ref[0], pltpu.roll(dk_ref[1], 64, 1)).astype(BF16)
            dvb_ref[...] = dv_ref[...].astype(BF16)
            for a in range(2):
                col = jnp.sum(jnp.where(big_lane == AUG_COLSUM, dk_ref[a], 0.0), axis=-1, keepdims=True)
                dck_ref[:, 128 * a:128 * a + 128] = jnp.broadcast_to(-col, (T, 128))

    blk = pl.BlockSpec((FQ, 128), lambda h, i: (i, h))
    wide = pl.BlockSpec((FQ, 256), lambda h, i: (i, h))
    col = pl.BlockSpec((T, 128), lambda h, i: (0, h))
    colwide = pl.BlockSpec((T, 256), lambda h, i: (0, h))
    return _pcall(
        body, name="fox_bwd", grid=(4, nq), in_specs=_fox_aug_specs() + [blk, wide, blk, ANY_SPEC],
        out_specs=[blk, col, col, wide, colwide],
        out_shape=[jax.ShapeDtypeStruct((T, 512), BF16)] * 3 + [jax.ShapeDtypeStruct((T, 1024), F32)] * 2,
        scratch_shapes=[pltpu.VMEM((2, T, 128), F32), pltpu.VMEM((T, 128), F32)],
        compiler_params=pltpu.CompilerParams(dimension_semantics=("arbitrary", "arbitrary"), vmem_limit_bytes=VMEM_BIG),
    )(q_aug, k_aug, proj, o, lse, do, after)


def _rel_onehot():
    ridx = lax.broadcasted_iota(jnp.int32, (NREL_PAD, VW), 0)
    j = lax.broadcasted_iota(jnp.int32, (NREL_PAD, VW), 1)
    return jnp.where(ridx == jnp.clip(TQ + LEFT - 1 - j, -128, 128) + 128, 1.0, 0.0).astype(F32)


def _relvec_fwd(tbl):
    def body(t_ref, v_ref):
        v_ref[...] = _hdot(t_ref[...], _rel_onehot())

    return _one_call(body, "relvec_fwd", [tbl], [((8, VW), F32)])[0]


def _relvec_bwd(gv):
    def body(g_ref, t_ref):
        t_ref[...] = lax.dot_general(g_ref[...], _rel_onehot(), (((1,), (1,)), ((), ())),
                                     preferred_element_type=F32, precision=lax.Precision.HIGHEST)

    return _one_call(body, "relvec_bwd", [gv], [((8, NREL_PAD), F32)])[0]


def _chk_bias(vt_ref, a, hidden):
    vb = jnp.broadcast_to(vt_ref[a:a + 1, :], (TQ, VW))
    y = pltpu.roll(vb, VW - (TQ - 1), 1, stride=1, stride_axis=0)[:, :WIN]
    cr = lax.broadcasted_iota(jnp.int32, (TQ, WIN), 0) // 64
    m = lax.broadcasted_iota(jnp.int32, (TQ, WIN), 1)
    return jnp.where((m // 64 >= cr) & (m // 64 <= cr + 8) & (m >= hidden), y, NEG)


def _chk_specs():
    return [pl.BlockSpec((TQ, 128), lambda h, i: (i, CHK0 // 128 + h)),
            pl.BlockSpec((T + LEFT, 128), lambda h, i: (0, h)),
            pl.BlockSpec((T + LEFT, 128), lambda h, i: (0, 4 + h)),
            pl.BlockSpec((None, 2, VW), lambda h, i: (h, 0, 0))]


def _chk_fwd(proj, kvp, vt3):
    def body(q_ref, k_ref, v_ref, vt_ref, o_ref, l_ref, bias_ref):
        i = pl.program_id(1)

        @pl.when(i == 0)
        def _():
            for first in range(3):
                for a in range(2):
                    bias_ref[first, a] = _chk_bias(vt_ref, a, max(LEFT - first * TQ, 0))

        lo = _lane_lo()
        off = pl.multiple_of(i * TQ, TQ)
        kw = k_ref[pl.ds(off, WIN), :]
        vw = v_ref[pl.ds(off, WIN), :]
        bias_at = jnp.minimum(i, 2)
        q = q_ref[...]
        outs = []
        for a in range(2):
            s = _nt(_half(q, lo, a, SCALE), kw) + bias_ref[bias_at, a]
            m = jnp.max(s, axis=-1, keepdims=True)
            p = jnp.exp(s - m)
            l = jnp.sum(p, axis=-1, keepdims=True)
            outs.append(_nn(p.astype(BF16), vw) / l)
            l_ref[:, 128 * a:128 * a + 128] = jnp.broadcast_to(m + jnp.log(l), (TQ, 128))
        o_ref[...] = jnp.where(lo, outs[0], outs[1])

    return _pcall(
        body, name="chk_fwd", grid=(4, T // TQ), in_specs=_chk_specs(),
        out_specs=[pl.BlockSpec((TQ, 128), lambda h, i: (i, h)), pl.BlockSpec((TQ, 256), lambda h, i: (i, h))],
        out_shape=[jax.ShapeDtypeStruct((T, 512), F32), jax.ShapeDtypeStruct((T, 1024), F32)],
        scratch_shapes=[pltpu.VMEM((3, 2, TQ, WIN), F32)],
        compiler_params=pltpu.CompilerParams(dimension_semantics=("arbitrary", "arbitrary")),
    )(proj, kvp, kvp, vt3)


def _chk_bwd(proj, kvp, vt3, o, lse, do):
    nq = T // TQ

    def body(q_ref, k_ref, v_ref, vt_ref, o_ref, l_ref, do_ref, dq_ref, dkb_ref, dvb_ref, gv_ref, bias_ref, dsum_ref,
             dk_ref, dv_ref):
        i = pl.program_id(1)

        @pl.when(i == 0)
        def _():
            for first in range(3):
                for a in range(2):
                    bias_ref[first, a] = _chk_bias(vt_ref, a, max(LEFT - first * TQ, 0))
            dsum_ref[...] = jnp.zeros_like(dsum_ref)
            dk_ref[...] = jnp.zeros_like(dk_ref)
            dv_ref[...] = jnp.zeros_like(dv_ref)

        lo = _lane_lo()
        off = pl.multiple_of(i * TQ, TQ)
        kw = k_ref[pl.ds(off, WIN), :]
        vw = v_ref[pl.ds(off, WIN), :]
        bias_at = jnp.minimum(i, 2)
        q = q_ref[...]
        do_v = do_ref[...]
        prod = do_v * o_ref[...]
        dqs = []
        for a in range(2):
            keep = lo if a == 0 else jnp.logical_not(lo)
            qa = _half(q, lo, a, SCALE)
            doa = _half(do_v, lo, a)
            delta = jnp.sum(jnp.where(keep, prod, 0.0), axis=-1, keepdims=True)
            s = _nt(qa, kw) + bias_ref[bias_at, a]
            p = jnp.exp(s - l_ref[:, 128 * a:128 * a + 1])
            ds = p * (_nt(doa, vw) - delta)
            dsum_ref[a] += ds
            dsb = ds.astype(BF16)
            dk_ref[pl.ds(off, WIN), :] += _tn(dsb, qa)
            dv_ref[pl.ds(off, WIN), :] += _tn(p.astype(BF16), doa)
            dqs.append(_nn(dsb, kw))
        dq_ref[...] = (jnp.where(lo, dqs[0], dqs[1]) * SCALE).astype(BF16)

        @pl.when(i == nq - 1)
        def _():
            dkb_ref[...] = dk_ref[LEFT:, :].astype(BF16)
            dvb_ref[...] = dv_ref[LEFT:, :].astype(BF16)
            rr = lax.broadcasted_iota(jnp.int32, (TQ, TQ), 0)
            cc = lax.broadcasted_iota(jnp.int32, (TQ, TQ), 1)
            flip = jnp.where(rr + cc == TQ - 1, 1.0, 0.0).astype(F32)
            for a in range(2):
                dpad = jnp.concatenate([dsum_ref[a], jnp.zeros((TQ, VW - WIN), F32)], axis=1)
                z = pltpu.roll(_hdot(flip, dpad), 0, 1, stride=1, stride_axis=0)
                gv_ref[a:a + 1, :] = jnp.sum(z, axis=0, keepdims=True)

    blk = pl.BlockSpec((TQ, 128), lambda h, i: (i, h))
    wide = pl.BlockSpec((TQ, 256), lambda h, i: (i, h))
    col = pl.BlockSpec((T, 128), lambda h, i: (0, h))
    return _pcall(
        body, name="chk_bwd", grid=(4, nq), in_specs=_chk_specs() + [blk, wide, blk],
        out_specs=[blk, col, col, pl.BlockSpec((None, 2, VW), lambda h, i: (h, 0, 0))],
        out_shape=[jax.ShapeDtypeStruct((T, 512), BF16), jax.ShapeDtypeStruct((T, 512), BF16),
                   jax.ShapeDtypeStruct((T, 512), BF16), jax.ShapeDtypeStruct((4, 2, VW), F32)],
        scratch_shapes=[pltpu.VMEM((3, 2, TQ, WIN), F32), pltpu.VMEM((2, TQ, WIN), F32),
                        pltpu.VMEM((T + LEFT, 128), F32), pltpu.VMEM((T + LEFT, 128), F32)],
        compiler_params=pltpu.CompilerParams(dimension_semantics=("arbitrary", "arbitrary")),
    )(proj, kvp, kvp, vt3, o, lse, do)


def _zero_at_start(*refs):
    @pl.when(pl.program_id(0) == 0)
    def _():
        for r in refs:
            r[...] = jnp.zeros_like(r)


def _ffn_bwd(dx3, y3, x2, a, w1_t, w2, g_post, g_pre):
    def body(dx3_ref, y_ref, x2_ref, a_ref, w1_ref, w2_ref, gp_ref, gf_ref,
             dx2_ref, da_ref, dy_ref, r_ref, dgp_ref, dgf_ref):
        _zero_at_start(dgp_ref, dgf_ref)
        dx3_v = dx3_ref[...]
        dy, dgp = _rms_bwd(y_ref[...], gp_ref[...], dx3_v)
        dgp_ref[...] += dgp
        dyb = dy.astype(BF16)
        dy_ref[...] = dyb
        ra = jnp.maximum(a_ref[...].astype(F32), 0.0)
        r_ref[...] = jnp.square(ra).astype(BF16)
        da = (_nt(dyb, _w(w2_ref)) * (2.0 * ra)).astype(BF16)
        da_ref[...] = da
        dh, dgf = _rms_bwd(x2_ref[...], gf_ref[...], _nn(da, _w(w1_ref)))
        dgf_ref[...] += dgf
        dx2_ref[...] = dx3_v + dh

    return _tok_call(body, "ffn_bwd", [dx3, y3, x2, a], [w1_t, w2, g_post, g_pre],
                     [(D, F32), (DFF, BF16), (D, BF16), (DFF, BF16)], [(1, D), (1, D)], vmem=VMEM_BIG)


def _mem_bwd(dx2, ym, x1, qm, km, vm, w_mo, w_mq, g_post, g_pre):
    def body(dx2_ref, ym_ref, x1_ref, q_ref, k_ref, v_ref, wo_ref, wq_ref, gp_ref, gm_ref,
             dx1_ref, dym_ref, dq_ref, dk_ref, dv_ref, dgp_ref, dgm_ref, dom_ref):
        _zero_at_start(dk_ref, dv_ref, dgp_ref, dgm_ref)
        dx2_v = dx2_ref[...]
        dym, dgp = _rms_bwd(ym_ref[...], gp_ref[...], dx2_v)
        dgp_ref[...] += dgp
        dymb = dym.astype(BF16)
        dym_ref[...] = dymb
        dom_ref[...] = _nt(dymb, _w(wo_ref)).astype(BF16)
        for h in range(MEM_HEADS):
            sl = slice(h * MEM_HD, (h + 1) * MEM_HD)
            qh, kh, doh = q_ref[:, sl], k_ref[:, sl], dom_ref[:, sl]
            s = _nt(qh, kh) * MEM_SCALE
            p = jnp.exp(s - jnp.max(s, axis=-1, keepdims=True))
            p = p / jnp.sum(p, axis=-1, keepdims=True)
            dp = _nt(doh, v_ref[:, sl])
            ds = (p * (dp - jnp.sum(p * dp, axis=-1, keepdims=True))).astype(BF16)
            dq_ref[:, sl] = (_nn(ds, kh) * MEM_SCALE).astype(BF16)
            dk_ref[:, sl] += _tn(ds, qh) * MEM_SCALE
            dv_ref[:, sl] += _tn(p.astype(BF16), doh)
        dh, dgm = _rms_bwd(x1_ref[...], gm_ref[...], _nt(dq_ref[...], _w(wq_ref)))
        dgm_ref[...] += dgm
        dx1_ref[...] = dx2_v + dh

    tiled = pl.BlockSpec((TM_WIDE, D), lambda i: (i, 0))
    in_specs = [tiled] * 4 + [_resident(a) for a in (km, vm, w_mo, w_mq, g_post, g_pre)]
    w_mo, w_mq = w_mo[0], w_mq[0]
    kv = pl.BlockSpec((NMEM, D), lambda i: (0, 0))
    vec = pl.BlockSpec((1, D), lambda i: (0, 0))
    return _pcall(
        body, name="mem_bwd", grid=(T // TM_WIDE,), in_specs=in_specs,
        out_specs=[tiled, tiled, tiled, kv, kv, vec, vec],
        out_shape=[jax.ShapeDtypeStruct((T, D), F32), jax.ShapeDtypeStruct((T, D), BF16),
                   jax.ShapeDtypeStruct((T, D), BF16), jax.ShapeDtypeStruct((NMEM, D), F32),
                   jax.ShapeDtypeStruct((NMEM, D), F32), jax.ShapeDtypeStruct((1, D), F32),
                   jax.ShapeDtypeStruct((1, D), F32)],
        scratch_shapes=[pltpu.VMEM((TM_WIDE, D), BF16)],
        compiler_params=pltpu.CompilerParams(dimension_semantics=("arbitrary",), vmem_limit_bytes=VMEM_BIG),
    )(dx2, ym, x1, qm, km, vm, w_mo, w_mq, g_post, g_pre)


def _memkv_bwd(dkm, dvm, mem, w_mk, w_mv):
    def body(dk_ref, dv_ref, m_ref, wk_ref, wv_ref, dg_ref):
        dmn = _nt(dk_ref[...].astype(BF16), _w(wk_ref)) + _nt(dv_ref[...].astype(BF16), _w(wv_ref))
        mv = m_ref[...]
        dg_ref[...] = jnp.sum(dmn * (mv * _rstd(mv)), axis=0, keepdims=True)

    return _one_call(body, "memkv_bwd", [dkm, dvm, mem, w_mk, w_mv], [((1, D), F32)], vmem=VMEM_BIG)[0]


def _postmix_bwd(dx1, z, o_f, o_c, w_out, g_post, g_fo, g_co):
    def body(dx1_ref, z_ref, of_ref, oc_ref, wo_ref, gp_ref, gfo_ref, gco_ref,
             dz_ref, dof_ref, doc_ref, dgp_ref, dgfo_ref, dgco_ref):
        _zero_at_start(dgp_ref, dgfo_ref, dgco_ref)
        dz, dgp = _rms_bwd(z_ref[...], gp_ref[...], dx1_ref[...])
        dgp_ref[...] += dgp
        dzb = dz.astype(BF16)
        dz_ref[...] = dzb
        dy = _nt(dzb, _w(wo_ref))
        dof, dgfo = _rms_bwd(of_ref[...], gfo_ref[...], dy[:, :512])
        doc, dgco = _rms_bwd(oc_ref[...], gco_ref[...], dy[:, 512:])
        dof_ref[...] = dof
        doc_ref[...] = doc
        dgfo_ref[...] += dgfo
        dgco_ref[...] += dgco

    return _tok_call(body, "postmix_bwd", [dx1, z, o_f, o_c], [w_out, g_post, g_fo, g_co],
                     [(D, BF16), (512, F32), (512, F32)], [(1, D), (1, 512), (1, 512)], tm=TM_WIDE, vmem=VMEM_BIG)


def _premix_bwd(dx1, x, pieces, win_t, g_pre):
    def body(dx1_ref, x_ref, *refs):
        piece_refs, (w_ref, g_ref, dx_ref, dp_ref, dg_ref) = refs[:len(pieces)], refs[len(pieces):]
        _zero_at_start(dg_ref)
        col = 0
        for p in piece_refs:
            dp_ref[:, col:col + p.shape[1]] = p[...]
            col += p.shape[1]
        dh, dg = _rms_bwd(x_ref[...], g_ref[...], _nn(dp_ref[...], w_ref[...]))
        dg_ref[...] += dg
        dx_ref[...] = dx1_ref[...] + dh

    return _tok_call(body, "premix_bwd", [dx1, x] + list(pieces), [win_t, g_pre], [(D, F32), (PROJ, BF16)], [(1, D)],
                     tm=TM_WIDE, vmem=VMEM_BIG)


def _wgrad(a, b, name):
    k, m = a.shape
    n = b.shape[1]
    tm = 640 if m % 640 == 0 and m > 1024 else min(m, 512)
    tn = min(n, 1024)

    def body(a_ref, b_ref, o_ref):
        o_ref[...] = _tn(a_ref[...].astype(BF16), b_ref[...].astype(BF16))

    return _pcall(
        body, name=name, grid=(m // tm, n // tn),
        in_specs=[pl.BlockSpec((k, tm), lambda i, j: (0, i)), pl.BlockSpec((k, tn), lambda i, j: (0, j))],
        out_specs=pl.BlockSpec((tm, tn), lambda i, j: (i, j)),
        out_shape=jax.ShapeDtypeStruct((m, n), F32),
        compiler_params=pltpu.CompilerParams(dimension_semantics=("arbitrary", "arbitrary"), vmem_limit_bytes=VMEM_BIG),
    )(a, b)


def _wgrad_group(name, pairs, rows):
    def body(*refs):
        o_ref = refs[-1]
        for k in range(len(pairs)):
            o_ref[k * rows:(k + 1) * rows, :] = _tn(refs[2 * k][...].astype(BF16), refs[2 * k + 1][...].astype(BF16))

    in_specs, ops = [], []
    for a, b in pairs:
        in_specs += [pl.BlockSpec((a.shape[0], rows), lambda j: (0, j)), _resident(b)]
        ops += [a, b]
    return _pcall(
        body, name=name, grid=(8,), in_specs=in_specs,
        out_specs=pl.BlockSpec((None, len(pairs) * rows, D), lambda j: (j, 0, 0)),
        out_shape=jax.ShapeDtypeStruct((8, len(pairs) * rows, D), F32),
        compiler_params=pltpu.CompilerParams(dimension_semantics=("arbitrary",), vmem_limit_bytes=VMEM_BIG),
    )(*ops)


def _adam_math(w, g, m, v):
    m2 = ADAM_B1 * m + (1.0 - ADAM_B1) * g
    v2 = ADAM_B2 * v + (1.0 - ADAM_B2) * jnp.square(g)
    m_hat = m2 / (1.0 - ADAM_B1 ** ADAM_STEP)
    v_hat = v2 / (1.0 - ADAM_B2 ** ADAM_STEP)
    delta = -ADAM_LR * (m_hat / (jnp.sqrt(v_hat) + ADAM_EPS) + ADAM_WD * w)
    return delta, m2, v2


def _adamw(w, g, m, v, name):
    rows, cols = w.shape
    tr = 256 if rows % 256 == 0 else rows

    def body(w_ref, g_ref, m_ref, v_ref, d_ref, m2_ref, v2_ref):
        d_ref[...], m2_ref[...], v2_ref[...] = _adam_math(w_ref[...], g_ref[...], m_ref[...], v_ref[...])

    spec = pl.BlockSpec((tr, cols), lambda i: (i, 0))
    return _pcall(
        body, name=name, grid=(rows // tr,), in_specs=[spec] * 4, out_specs=[spec] * 3,
        out_shape=[jax.ShapeDtypeStruct(w.shape, F32)] * 3,
        compiler_params=pltpu.CompilerParams(dimension_semantics=("arbitrary",)),
    )(w, g, m, v)


def _adamw_small(gparts, ws, ms, vs):
    n = len(SMALL)

    def body(g_ref, *refs):
        w_refs, m_refs, v_refs = refs[:n], refs[n:2 * n], refs[2 * n:3 * n]
        outs, sum_ref = refs[3 * n:-1], refs[-1]
        g = g_ref[0]
        for k in range(1, 8):
            g = g + g_ref[k]
        sum_ref[...] = g
        outs[0][...] = sum_ref[17:18, 0:128]
        for t, name in enumerate(SMALL):
            r0, nr, c0, nc = SMALL_SLOT[name]
            gt = sum_ref[r0:r0 + nr, c0:c0 + nc]
            out = (gt,) + _adam_math(w_refs[t][...], gt, m_refs[t][...], v_refs[t][...])
            for o_ref, val in zip(outs[1 + 4 * t:5 + 4 * t], out):
                o_ref[...] = val

    whole = lambda s: pl.BlockSpec(s, lambda i, nd=len(s): (0,) * nd)
    ins = [gparts] + list(ws) + list(ms) + list(vs)
    out_shapes = [(1, 128)] + [a.shape for a in ws for _ in range(4)]
    return _pcall(
        body, name="adamw_small", grid=(1,), in_specs=[whole(a.shape) for a in ins],
        out_specs=[whole(s) for s in out_shapes], out_shape=[jax.ShapeDtypeStruct(s, F32) for s in out_shapes],
        scratch_shapes=[pltpu.VMEM((SMALL_ROWS, D), F32)],
        compiler_params=pltpu.CompilerParams(dimension_semantics=("arbitrary",)),
    )(*ins)


def _row_tile(rows):
    return next(t for t in (512, 400, 320) if rows % t == 0)


def _add_halves(g4, theirs, core, name):
    rows = g4.shape[2]
    tr = _row_tile(rows)

    def body(c_ref, a_ref, b_ref, o_ref):
        o_ref[...] = (a_ref[...] + b_ref[...]).astype(BF16)

    grid_spec = pltpu.PrefetchScalarGridSpec(
        num_scalar_prefetch=1, grid=(4, rows // tr),
        in_specs=[pl.BlockSpec((None, None, tr, D), lambda j, i, c: (j, c[0], i, 0)),
                  pl.BlockSpec((None, None, tr, D), lambda j, i, c: (j, 0, i, 0))],
        out_specs=pl.BlockSpec((None, tr, D), lambda j, i, c: (j, i, 0)))
    return _pcall(
        body, name=name, grid_spec=grid_spec, out_shape=jax.ShapeDtypeStruct((4, rows, D), BF16),
        compiler_params=pltpu.CompilerParams(dimension_semantics=("arbitrary", "arbitrary")),
    )(core, g4, theirs)


def _sum_adam(own, got, order, r0, w, m, v, name, transposed=False):
    n = w.shape[1] if transposed else w.shape[0]
    tr = min(n, 256)

    def body(o_ref, a_ref, b_ref, c_ref, d_ref, w_ref, m_ref, v_ref, g_ref, dl_ref, m2_ref, v2_ref):
        f = lambda r: r[...].astype(F32)
        g = ((f(a_ref) + f(b_ref)) + f(c_ref)) + f(d_ref)
        g = g.T if transposed else g
        g_ref[...] = g
        dl_ref[...], m2_ref[...], v2_ref[...] = _adam_math(w_ref[...], g, m_ref[...], v_ref[...])

    slot = lambda k: pl.BlockSpec((None, tr, D), lambda i, o: (o[k], r0 // tr + i, 0))
    wspec = pl.BlockSpec((D, tr), lambda i, o: (0, i)) if transposed else pl.BlockSpec((tr, D), lambda i, o: (i, 0))
    grid_spec = pltpu.PrefetchScalarGridSpec(
        num_scalar_prefetch=1, grid=(n // tr,), in_specs=[slot(0), slot(1), slot(2), slot(3), wspec, wspec, wspec],
        out_specs=[wspec] * 4)
    return _pcall(
        body, name=name, grid_spec=grid_spec, out_shape=[jax.ShapeDtypeStruct(w.shape, F32)] * 4,
        compiler_params=pltpu.CompilerParams(dimension_semantics=("arbitrary",)),
    )(order, own, got, got, got, w, m, v)


def _sum_adam_rows(own, got, order, ws, ms, vs, name):
    n, rows = len(ws), ws[0].shape[0]

    def body(o_ref, a_ref, b_ref, c_ref, d_ref, *refs):
        ins, outs = refs[:3 * n], refs[3 * n:]
        for t in range(n):
            r = slice(t * rows, (t + 1) * rows)
            f = lambda ref: ref[r, :].astype(F32)
            g = ((f(a_ref) + f(b_ref)) + f(c_ref)) + f(d_ref)
            out = (g,) + _adam_math(ins[t][...], g, ins[n + t][...], ins[2 * n + t][...])
            for o, val in zip(outs[4 * t:4 * t + 4], out):
                o[...] = val

    slot = lambda k: pl.BlockSpec((None, n * rows, D), lambda i, o: (o[k], 0, 0), pipeline_mode=pl.Buffered(1))
    wspec = pl.BlockSpec((rows, D), lambda i, o: (0, 0), pipeline_mode=pl.Buffered(1))
    grid_spec = pltpu.PrefetchScalarGridSpec(
        num_scalar_prefetch=1, grid=(1,), in_specs=[slot(0), slot(1), slot(2), slot(3)] + [wspec] * (3 * n),
        out_specs=[pl.BlockSpec((rows, D), lambda i, o: (0, 0))] * (4 * n))
    return _pcall(
        body, name=name, grid_spec=grid_spec, out_shape=[jax.ShapeDtypeStruct((rows, D), F32)] * (4 * n),
        compiler_params=pltpu.CompilerParams(dimension_semantics=("arbitrary",), vmem_limit_bytes=VMEM_BIG),
    )(order, own, got, got, got, *ws, *ms, *vs)


def _sum_chips(own, got, order, name):
    rows = own.shape[1]
    tr = _row_tile(rows)

    def body(o_ref, a_ref, b_ref, c_ref, d_ref, out_ref):
        f = lambda r: r[...].astype(F32)
        out_ref[...] = ((f(a_ref) + f(b_ref)) + f(c_ref)) + f(d_ref)

    slot = lambda k: pl.BlockSpec((None, tr, D), lambda i, o: (o[k], i, 0))
    grid_spec = pltpu.PrefetchScalarGridSpec(
        num_scalar_prefetch=1, grid=(rows // tr,), in_specs=[slot(0), slot(1), slot(2), slot(3)],
        out_specs=pl.BlockSpec((tr, D), lambda i, o: (i, 0)))
    return _pcall(
        body, name=name, grid_spec=grid_spec, out_shape=jax.ShapeDtypeStruct((rows, D), F32),
        compiler_params=pltpu.CompilerParams(dimension_semantics=("arbitrary",)),
    )(order, own, got, got, got)


def _place():
    return lax.axis_index("x"), lax.axis_index("y"), lax.axis_index("c")


def _allgather(block, name):
    def body(x_ref, out_ref, token, send_sems, recv_sems, local_sem):
        token[...] = jnp.zeros_like(token)
        x, y, c = _place()
        me, sibling = (x, y, c), (x, y, 1 - c)
        chips = [(1 - x, y), (x, 1 - y), (1 - x, 1 - y)]

        def slot(px, py, pc):
            return out_ref.at[4 * px + 2 * py + pc]

        def copy(k, blk, to, src=None):
            return pltpu.make_async_remote_copy(
                src_ref=slot(*blk) if src is None else src, dst_ref=slot(*blk),
                send_sem=send_sems.at[k], recv_sem=recv_sems.at[k], device_id=to, device_id_type=MESH)

        mine = pltpu.make_async_copy(x_ref, slot(*me), local_sem)
        mine.start()
        first = [copy(0, me, sibling, src=x_ref)]
        first += [copy(1 + j, me, (*chip, c), src=x_ref) for j, chip in enumerate(chips)]
        for cp in first:
            cp.start()
        passed = [copy(4 + j, (*chip, c), sibling) for j, chip in enumerate(chips)]
        for j, chip in enumerate(chips):
            copy(1 + j, (*chip, c), me).wait_recv()
            passed[j].start()
        copy(0, sibling, me).wait_recv()
        for j, chip in enumerate(chips):
            copy(4 + j, (*chip, 1 - c), me).wait_recv()
        for cp in first + passed:
            cp.wait_send()
        mine.wait()

    return _pcall(
        body, name=name,
        out_shape=[jax.ShapeDtypeStruct((8,) + block.shape, block.dtype), jax.ShapeDtypeStruct((8, 128), F32)],
        in_specs=[pl.BlockSpec(memory_space=pl.ANY)],
        out_specs=[pl.BlockSpec(memory_space=pl.ANY), pl.BlockSpec(memory_space=pltpu.VMEM)],
        scratch_shapes=[pltpu.SemaphoreType.DMA((7,)), pltpu.SemaphoreType.DMA((7,)), pltpu.SemaphoreType.DMA(())],
        compiler_params=pltpu.CompilerParams(has_side_effects=True),
    )(block)


HBM_SPEC = pl.BlockSpec(memory_space=pltpu.HBM)
SEM_SPEC = pl.BlockSpec(memory_space=pltpu.SEMAPHORE)
ANY_SPEC = pl.BlockSpec(memory_space=pl.ANY)
EFFECT = pltpu.SideEffectType.DATAFLOW_SIDE_EFFECTING


def _in_hbm(a):
    return pltpu.with_memory_space_constraint(a, pltpu.HBM)


def _start_copies(name, src, land_shape, plan, n):
    def body(src_ref, land_ref, send_sems, recv_sems, src_thru, land_thru, token):
        for k, (s, d, to, _) in enumerate(plan(src_ref, land_ref)):
            pltpu.make_async_remote_copy(src_ref=s, dst_ref=d, send_sem=send_sems.at[k], recv_sem=recv_sems.at[k],
                                         device_id=to, device_id_type=MESH).start()
        token[...] = jnp.zeros_like(token)

    return _pcall(
        body, name=name,
        out_shape=(pltpu.SemaphoreType.DMA((n,)), pltpu.SemaphoreType.DMA((n,)), pltpu.HBM(src.shape, src.dtype),
                   pltpu.HBM(land_shape, src.dtype), jax.ShapeDtypeStruct((8, 128), F32)),
        in_specs=(HBM_SPEC, HBM_SPEC),
        out_specs=(SEM_SPEC, SEM_SPEC, HBM_SPEC, HBM_SPEC, pl.BlockSpec(memory_space=pltpu.VMEM)),
        input_output_aliases={0: 2, 1: 3}, compiler_params=pltpu.CompilerParams(has_side_effects=EFFECT),
    )(_in_hbm(src), _in_hbm(lax.empty(land_shape, src.dtype)))


def _wait_copies(name, started, after, plan):
    send_sems, recv_sems, src_thru, land_thru, _ = started

    def body(src_ref, land_ref, send_sems, recv_sems, *rest):
        for k, (s, _, to, mine) in enumerate(plan(src_ref, land_ref)):
            cp = pltpu.make_async_remote_copy(src_ref=s, dst_ref=mine, send_sem=send_sems.at[k],
                                              recv_sem=recv_sems.at[k], device_id=to, device_id_type=MESH)
            cp.wait_send()
            cp.wait_recv()

    return _pcall(
        body, name=name,
        out_shape=(pltpu.HBM(src_thru.shape, src_thru.dtype), pltpu.HBM(land_thru.shape, land_thru.dtype)),
        in_specs=(HBM_SPEC, HBM_SPEC, SEM_SPEC, SEM_SPEC) + (ANY_SPEC,) * len(after), out_specs=(HBM_SPEC, HBM_SPEC),
        input_output_aliases={0: 0, 1: 1}, compiler_params=pltpu.CompilerParams(has_side_effects=EFFECT),
    )(src_thru, land_thru, send_sems, recv_sems, *after)


def _gather_plan(src_ref, land_ref):
    x, y, c = _place()
    peers = [(x, y, 1 - c), (1 - x, y, c), (x, 1 - y, c), (1 - x, 1 - y, c)]
    return [(src_ref, land_ref.at[4 * x + 2 * y + c], p, land_ref.at[4 * p[0] + 2 * p[1] + p[2]]) for p in peers]


def _swap_plan(src_ref, land_ref):
    x, y, c = _place()
    return [(src_ref.at[:, pl.ds(1 - c, 1)], land_ref, (x, y, 1 - c), land_ref)]


def _exchange_plan(src_ref, land_ref):
    x, y, c = _place()
    chips = [(1 - x, y), (x, 1 - y), (1 - x, 1 - y)]
    return [(src_ref.at[2 * px + py], land_ref.at[2 * x + y], (px, py, c), land_ref.at[2 * px + py]) for px, py in chips]


def _gather_forward(land, block):
    def body(land_ref, out_ref, send_sems, recv_sems):
        x, y, c = _place()
        chips = [(1 - x, y), (x, 1 - y), (1 - x, 1 - y)]

        def copy(k, px, py, pc):
            blk = out_ref.at[4 * px + 2 * py + pc]
            return pltpu.make_async_remote_copy(src_ref=blk, dst_ref=blk, send_sem=send_sems.at[k],
                                                recv_sem=recv_sems.at[k], device_id=(x, y, 1 - c), device_id_type=MESH)

        sent = [copy(k, px, py, c) for k, (px, py) in enumerate(chips)]
        for cp in sent:
            cp.start()
        for k, (px, py) in enumerate(chips):
            copy(k, px, py, 1 - c).wait_recv()
        for cp in sent:
            cp.wait_send()

    land = _pcall(
        body, name="allgather_rest_forward", out_shape=jax.ShapeDtypeStruct(land.shape, land.dtype),
        in_specs=[ANY_SPEC], out_specs=ANY_SPEC, input_output_aliases={0: 0},
        scratch_shapes=[pltpu.SemaphoreType.DMA((3,)), pltpu.SemaphoreType.DMA((3,))],
        compiler_params=pltpu.CompilerParams(has_side_effects=True),
    )(land)

    rows = block.shape[0]
    tr = rows // 4

    def place(me_ref, x_ref, land_ref, out_ref):
        out_ref[...] = x_ref[...]

    x, y, c = _place()
    grid_spec = pltpu.PrefetchScalarGridSpec(
        num_scalar_prefetch=1, grid=(rows // tr,),
        in_specs=[pl.BlockSpec((tr, D), lambda i, me: (i, 0)), ANY_SPEC],
        out_specs=pl.BlockSpec((None, tr, D), lambda i, me: (me[0], i, 0)))
    return _pcall(
        place, name="allgather_rest_own", grid_spec=grid_spec, out_shape=jax.ShapeDtypeStruct(land.shape, land.dtype),
        input_output_aliases={2: 0}, compiler_params=pltpu.CompilerParams(dimension_semantics=("arbitrary",)),
    )((4 * x + 2 * y + c).reshape(1), block, land)


class _ReduceScatter:
    def __init__(self, name, g):
        self.name = name
        rows = g.shape[1]
        self.started = _start_copies(name + "_swap_start", g.reshape(4, 2, rows, D), (4, 1, rows, D), _swap_plan, 1)
        self.token = self.started[4][0, 0]

    def halfway(self, after):
        g4, theirs = _wait_copies(self.name + "_swap_wait", self.started, after, _swap_plan)
        self.own = _add_halves(g4, theirs, lax.axis_index("c").reshape(1), self.name + "_add_halves")
        self.started = _start_copies(self.name + "_exch_start", self.own, self.own.shape, _exchange_plan, 3)
        self.token = self.started[4][0, 0]

    def finish(self, after):
        own, got = _wait_copies(self.name + "_exch_wait", self.started, after, _exchange_plan)
        chip = 2 * lax.axis_index("x") + lax.axis_index("y")
        return own, got, (chip + jnp.arange(4, dtype=jnp.int32)) % 4


def _pack_small(p, scalar=None):
    z = lambda a, n: jnp.pad(a, ((0, 0), (0, n - a.shape[1])))
    rows = [z(p['rel_bias'], D), z(p['b_fgt'], D), jnp.concatenate([p['g_fox_out'], p['g_chk_out']], axis=1)]
    rows += [p[n] for n in ('g_mix_pre', 'g_mix_post', 'g_mem_kv', 'g_mem_pre', 'g_mem_post', 'g_ff_pre', 'g_ff_post')]
    rows.append(jnp.zeros((1, D), F32) if scalar is None else z(jnp.reshape(scalar, (1, 1)), D))
    rows.append(jnp.zeros((SMALL_ROWS - 18, D), F32))
    return jnp.concatenate(rows, axis=0)


_GAP_DEV, _GAP_ROW = divmod(GATE0 + 8, N_IN)
_GAP = CHK0 - GATE0 - 8


def _in_rows_to_proj(g):
    wt = g[:, :N_IN].reshape(8 * N_IN, D)
    return jnp.concatenate([wt[:GATE0], jnp.pad(wt[GATE0:GATE0 + 8], ((0, _GAP), (0, 0))), wt[GATE0 + 8:]], axis=0)


def _proj_rows_to_in(g):
    pad = lambda a: jnp.pad(a, ((0, R_IN - a.shape[0]), (0, 0)))
    lo = N_IN * _GAP_DEV
    shards = [pad(g[N_IN * j:N_IN * (j + 1)]) for j in range(_GAP_DEV)]
    shards.append(pad(jnp.concatenate([g[lo:lo + _GAP_ROW], g[lo + _GAP_ROW + _GAP:lo + N_IN + _GAP]], axis=0)))
    shards += [pad(g[N_IN * j + _GAP:N_IN * (j + 1) + _GAP]) for j in range(_GAP_DEV + 1, 8)]
    return jnp.stack(shards)


def _local_grads(x, mem, tgt, win_t, gw_of, sm, on_grads):
    b_pad = jnp.pad(sm['b_fgt'], ((0, 0), (0, 120)))
    tbl = jnp.pad(sm['rel_bias'], ((0, 0), (0, NREL_PAD - 257)))

    h1, proj, flog = _premix_fwd(x, sm['g_mix_pre'], win_t)
    c = _gate_fwd(flog, b_pad)
    ct3 = c[:, :8].T.reshape(4, 2, T)
    o_f, lse_f = _fox_fwd(proj, c, ct3)
    vt3 = _relvec_fwd(tbl).reshape(4, 2, VW)
    kvp = jnp.pad(proj[:, CHK0 + 512:], ((LEFT, 0), (0, 0)))
    o_c, lse_c = _chk_fwd(proj, kvp, vt3)
    gw = gw_of([o_f, o_c])
    w_out, w_mq, w_mk, w_mv, w_mo, w1_t, w2 = (_wblk(gw, n) for n in ('w_out', 'w_mq', 'w_mk', 'w_mv', 'w_mo', 'w_ff1', 'w_ff2'))
    ycat, z, x1, h2, qm = _postmix_fwd(x, o_f, o_c, sm['g_fox_out'], sm['g_chk_out'], w_out,
                                       sm['g_mix_post'], sm['g_mem_pre'], w_mq)
    memn, km, vm = _memkv_fwd(mem, sm['g_mem_kv'], w_mk, w_mv)
    om, ym, x2, h3 = _mem_fwd(qm, x1, km, vm, w_mo, sm['g_mem_post'], sm['g_ff_pre'])
    a, y3, dx3, loss_acc = _ffn_fwd(h3, x2, tgt, w1_t, w2, sm['g_ff_post'])

    gs = {}
    dx2, da, dy3, r, gs['g_ff_post'], gs['g_ff_pre'] = _ffn_bwd(dx3, y3, x2, a, w1_t, w2, sm['g_ff_post'], sm['g_ff_pre'])
    zero = on_grads('A', _wgrad_group("wgrad_ff", [(da, h3), (r, dy3)], 512), None)
    dx1, dym, dqm, dkm, dvm, gs['g_mem_post'], gs['g_mem_pre'] = _mem_bwd(
        dx2, ym, x1, qm, km, vm, w_mo, w_mq, sm['g_mem_post'] + zero, sm['g_mem_pre'])
    zero = on_grads('A halfway', None, [dx1])
    gs['g_mem_kv'] = _memkv_bwd(dkm, dvm, mem, w_mk, w_mv)
    dz, dof, doc, gs['g_mix_post'], gs['g_fox_out'], gs['g_chk_out'] = _postmix_bwd(
        dx1, z, o_f, o_c, w_out, sm['g_mix_post'] + zero, sm['g_fox_out'], sm['g_chk_out'])
    zero = on_grads('B', _wgrad_group("wgrad_mem_out", [(ycat, dz), (h2, dqm), (memn, dkm), (memn, dvm), (om, dym)], 128), None)
    dq_f, dk_f, dv_f, dct, dcq = _fox_bwd(proj, c, ct3 + zero, o_f, lse_f, dof)
    zero = on_grads('B halfway', None, [dq_f])
    dq_c, dk_c, dv_c, gv = _chk_bwd(proj, kvp, vt3 + zero, o_c, lse_c, doc)
    gs['rel_bias'] = _relvec_bwd(gv.reshape(8, VW))[:, :257]
    dc = jnp.pad(dct.reshape(8, T).T + dcq[:, :, :2].transpose(1, 0, 2).reshape(T, 8), ((0, 0), (0, 120)))
    dflog, db = _gate_bwd(dc, flog, b_pad)
    gs['b_fgt'] = db[0:1, :8]
    grad_x, dproj, gs['g_mix_pre'] = _premix_bwd(dx1, x, [dq_f, dk_f, dv_f, dflog, dq_c, dk_c, dv_c], win_t, sm['g_mix_pre'])
    on_grads('C', _proj_rows_to_in(_wgrad(dproj, h1, "wgrad_in")), None)
    return loss_acc[0, 0], grad_x, gs


def kernel(x, mem, w_in, b_fgt, rel_bias, g_fox_out, g_chk_out, w_out, g_mix_pre, g_mix_post, g_mem_kv, w_mq, w_mk, w_mv, w_mo, g_mem_pre, g_mem_post, w_ff1, w_ff2, g_ff_pre, g_ff_post, loss_target, m_w_in, m_b_fgt, m_rel_bias, m_g_fox_out, m_g_chk_out, m_w_out, m_g_mix_pre, m_g_mix_post, m_g_mem_kv, m_w_mq, m_w_mk, m_w_mv, m_w_mo, m_g_mem_pre, m_g_mem_post, m_w_ff1, m_w_ff2, m_g_ff_pre, m_g_ff_post, v_w_in, v_b_fgt, v_rel_bias, v_g_fox_out, v_g_chk_out, v_w_out, v_g_mix_pre, v_g_mix_post, v_g_mem_kv, v_w_mq, v_w_mk, v_w_mv, v_w_mo, v_g_mem_pre, v_g_mem_post, v_w_ff1, v_w_ff2, v_g_ff_pre, v_g_ff_post):
    args = dict(locals())
    two_d = lambda a: a.reshape(a.shape[-2:])
    w = {n: two_d(args[n]) for n in WEIGHTS}
    m = {n: two_d(args['m_' + n]) for n in WEIGHTS}
    v = {n: two_d(args['v_' + n]) for n in WEIGHTS}

    sm = {n: w[n] for n in SMALL}
    shard_in = jnp.pad(w['w_in'].T, ((0, R_IN - N_IN), (0, 0))).astype(BF16)
    gathered_in, zero = _allgather(shard_in, "allgather_w_in")
    win_t = _in_rows_to_proj(gathered_in)
    shard_rest = (jnp.concatenate([w['w_ff1'].T, w['w_ff2'], w['w_out'], w['w_mq'], w['w_mk'], w['w_mv'], w['w_mo']],
                                  axis=0) + zero[0, 0]).astype(BF16)
    rest = _start_copies("allgather_rest_start", shard_rest, (8, R_REST, D), _gather_plan, 4)
    sm['g_mix_pre'] = sm['g_mix_pre'] + rest[4][0, 0]

    def gw_of(after):
        block, land = _wait_copies("allgather_rest_wait", rest, after, _gather_plan)
        return _gather_forward(land, block)

    rs = {}

    def on_grads(stage, g, after):
        if stage.endswith('halfway'):
            rs[stage[0]].halfway(after)
            return rs[stage[0]].token
        rs[stage] = _ReduceScatter("rs_" + stage.lower(), g)
        return rs[stage].token

    loss_local, grad_x, gs = _local_grads(x[0], mem[0], loss_target[0], win_t, gw_of, sm, on_grads)
    grads, deltas, new_m, new_v = {}, {}, {}, {}

    def update(n, out):
        grads[n], deltas[n], new_m[n], new_v[n] = out

    gparts, _ = _allgather(_pack_small(gs, loss_local + rs['C'].token), "allgather_small_grads")
    small = _adamw_small(gparts, [w[n] for n in SMALL], [m[n] for n in SMALL], [v[n] for n in SMALL])
    loss = small[0][0, 0]
    for t, n in enumerate(SMALL):
        update(n, small[1 + 4 * t:5 + 4 * t])
    rs['C'].halfway([small[0]])

    own, got, order = rs['A'].finish([grad_x, rs['C'].started[4]])
    update('w_ff1', _sum_adam(own, got, order, 0, w['w_ff1'], m['w_ff1'], v['w_ff1'], "adamw_w_ff1", transposed=True))
    update('w_ff2', _sum_adam(own, got, order, 512, w['w_ff2'], m['w_ff2'], v['w_ff2'], "adamw_w_ff2"))
    own, got, order = rs['B'].finish([grad_x, rs['C'].started[4]])
    names_b = ('w_out', 'w_mq', 'w_mk', 'w_mv', 'w_mo')
    done = _sum_adam_rows(own, got, order, [w[n] for n in names_b], [m[n] for n in names_b], [v[n] for n in names_b],
                          "adamw_group_b")
    for k, n in enumerate(names_b):
        update(n, done[4 * k:4 * k + 4])

    g_in = _sum_chips(*rs['C'].finish([new_v[n] for n in BIG if n != 'w_in']), "rs_c_sum_chips")[:N_IN].T
    update('w_in', (g_in,) + tuple(_adamw(w['w_in'], g_in, m['w_in'], v['w_in'], "adamw_w_in")))

    out = [loss, grad_x[None]]
    for group in (grads, deltas, new_m, new_v):
        out += [group[n].reshape(args[n].shape) for n in WEIGHTS]
    return tuple(out)
```

```python
import functools

import jax
import jax.numpy as jnp
from jax import lax
from jax.experimental import pallas as pl
from jax.experimental.pallas import tpu as pltpu

F32 = jnp.float32
BF16 = jnp.bfloat16
MESH = pl.DeviceIdType.MESH

T = 2048
D = 1024
NMEM = 256
DFF = 4096
EPS = 1e-6
TM = 256
TM_WIDE = 512
TQ = 256
FQ = 512
HD = 64
SCALE = HD ** -0.5
MEM_HEADS = 4
MEM_HD = 256
MEM_SCALE = MEM_HD ** -0.5
NEG = -1e30
LEFT = 512
WIN = LEFT + TQ
VW = 1024
NREL_PAD = 384
PROJ = 3200
GATE0 = 1536
CHK0 = 1664
VMEM_BIG = 56 * 1024 * 1024

ADAM_LR = 0.001
ADAM_B1 = 0.9
ADAM_B2 = 0.999
ADAM_EPS = 1e-08
ADAM_WD = 0.01
ADAM_STEP = 10

N_IN = 385
R_IN = 400
R_REST = 1664
W_ROWS = {'w_ff1': (0, 512), 'w_ff2': (512, 512),
          'w_out': (1024, 128), 'w_mq': (1152, 128), 'w_mk': (1280, 128), 'w_mv': (1408, 128), 'w_mo': (1536, 128)}
R_A, R_B = 1024, 640
SMALL_ROWS = 24
SMALL_SLOT = {'rel_bias': (0, 8, 0, 257), 'b_fgt': (8, 1, 0, 8), 'g_fox_out': (9, 1, 0, 512), 'g_chk_out': (9, 1, 512, 512),
              'g_mix_pre': (10, 1, 0, 1024), 'g_mix_post': (11, 1, 0, 1024), 'g_mem_kv': (12, 1, 0, 1024),
              'g_mem_pre': (13, 1, 0, 1024), 'g_mem_post': (14, 1, 0, 1024), 'g_ff_pre': (15, 1, 0, 1024),
              'g_ff_post': (16, 1, 0, 1024)}

WEIGHTS = ['w_in', 'b_fgt', 'rel_bias', 'g_fox_out', 'g_chk_out', 'w_out', 'g_mix_pre', 'g_mix_post', 'g_mem_kv',
           'w_mq', 'w_mk', 'w_mv', 'w_mo', 'g_mem_pre', 'g_mem_post', 'w_ff1', 'w_ff2', 'g_ff_pre', 'g_ff_post']
BIG = ['w_in', 'w_out', 'w_mq', 'w_mk', 'w_mv', 'w_mo', 'w_ff1', 'w_ff2']
SMALL = [n for n in WEIGHTS if n not in BIG]


def _pcall(body, **kw):
    return pl.pallas_call(body, **kw)


def _nn(a, b):
    return jnp.dot(a, b, preferred_element_type=F32)


def _nt(a, b):
    return lax.dot_general(a, b, (((1,), (1,)), ((), ())), preferred_element_type=F32)


def _tn(a, b):
    return lax.dot_general(a, b, (((0,), (0,)), ((), ())), preferred_element_type=F32)


def _w(ref):
    v = ref[...]
    return v if v.ndim == 2 else v.reshape(-1, v.shape[-1])


def _rstd(x):
    return lax.rsqrt(jnp.mean(x * x, axis=-1, keepdims=True) + EPS)


def _rms(x, g):
    return x * _rstd(x) * g


def _rms_bwd(x, g, dy):
    r = _rstd(x)
    xh = x * r
    dg = jnp.sum(dy * xh, axis=0, keepdims=True)
    dxh = dy * g
    dx = r * (dxh - xh * jnp.mean(dxh * xh, axis=-1, keepdims=True))
    return dx, dg


def _resident(a):
    if isinstance(a, tuple):
        _, shape, index = a
        return pl.BlockSpec(shape, lambda *_: index, pipeline_mode=pl.Buffered(1))
    return pl.BlockSpec(a.shape, lambda *_, nd=a.ndim: (0,) * nd, pipeline_mode=pl.Buffered(1))


def _wblk(gw, name):
    r0, rows = W_ROWS[name]
    return (gw, (8, rows, D), (0, r0 // rows, 0))


def _tok_call(body, name, tiled, full, outs_tiled, outs_acc=(), rows=T, tm=TM, vmem=None):
    in_specs = [pl.BlockSpec((tm, a.shape[1]), lambda i: (i, 0)) for a in tiled]
    in_specs += [_resident(a) for a in full]
    full = [a[0] if isinstance(a, tuple) else a for a in full]
    out_shape = [jax.ShapeDtypeStruct((rows, c), dt) for c, dt in outs_tiled]
    out_shape += [jax.ShapeDtypeStruct(s, F32) for s in outs_acc]
    out_specs = [pl.BlockSpec((tm, c), lambda i: (i, 0)) for c, _ in outs_tiled]
    out_specs += [pl.BlockSpec(s, lambda i, nd=len(s): (0,) * nd) for s in outs_acc]
    return _pcall(
        body, name=name, grid=(rows // tm,), in_specs=in_specs, out_specs=out_specs, out_shape=out_shape,
        compiler_params=pltpu.CompilerParams(dimension_semantics=("arbitrary",), vmem_limit_bytes=vmem),
    )(*tiled, *full)


def _one_call(body, name, ins, outs, vmem=None):
    whole = lambda s: pl.BlockSpec(s, lambda i, nd=len(s): (0,) * nd)
    return _pcall(
        body, name=name, grid=(1,), in_specs=[_resident(a) for a in ins], out_specs=[whole(s) for s, _ in outs],
        out_shape=[jax.ShapeDtypeStruct(s, dt) for s, dt in outs],
        compiler_params=pltpu.CompilerParams(dimension_semantics=("arbitrary",), vmem_limit_bytes=vmem),
    )(*[a[0] if isinstance(a, tuple) else a for a in ins])


def _premix_fwd(x, g_pre, win_t):
    def body(x_ref, g_ref, w_ref, h_ref, proj_ref, flog_ref):
        h = _rms(x_ref[...], g_ref[...]).astype(BF16)
        h_ref[...] = h
        p = _nt(h, w_ref[...])
        proj_ref[...] = p.astype(BF16)
        flog_ref[...] = p[:, GATE0:GATE0 + 128]

    return _tok_call(body, "premix_fwd", [x], [g_pre, win_t],
                     [(D, BF16), (PROJ, BF16), (128, F32)], tm=TM_WIDE, vmem=VMEM_BIG)


def _postmix_fwd(x, o_f, o_c, g_fo, g_co, w_out, g_post, g_mpre, w_mq):
    def body(x_ref, of_ref, oc_ref, gfo_ref, gco_ref, wo_ref, gp_ref, gm_ref, wq_ref,
             y_ref, z_ref, x1_ref, h2_ref, qm_ref):
        y_ref[:, :512] = _rms(of_ref[...], gfo_ref[...]).astype(BF16)
        y_ref[:, 512:] = _rms(oc_ref[...], gco_ref[...]).astype(BF16)
        z = _nn(y_ref[...], _w(wo_ref))
        z_ref[...] = z
        x1 = x_ref[...] + _rms(z, gp_ref[...])
        x1_ref[...] = x1
        h2 = _rms(x1, gm_ref[...]).astype(BF16)
        h2_ref[...] = h2
        qm_ref[...] = _nn(h2, _w(wq_ref)).astype(BF16)

    return _tok_call(body, "postmix_fwd", [x, o_f, o_c], [g_fo, g_co, w_out, g_post, g_mpre, w_mq],
                     [(D, BF16), (D, F32), (D, F32), (D, BF16), (D, BF16)], tm=TM_WIDE, vmem=VMEM_BIG)


def _memkv_fwd(mem, g_kv, w_mk, w_mv):
    def body(m_ref, g_ref, wk_ref, wv_ref, mn_ref, k_ref, v_ref):
        mn = _rms(m_ref[...], g_ref[...]).astype(BF16)
        mn_ref[...] = mn
        k_ref[...] = _nn(mn, _w(wk_ref)).astype(BF16)
        v_ref[...] = _nn(mn, _w(wv_ref)).astype(BF16)

    return _tok_call(body, "memkv_fwd", [mem], [g_kv, w_mk, w_mv],
                     [(D, BF16), (D, BF16), (D, BF16)], rows=NMEM, tm=NMEM, vmem=VMEM_BIG)


def _mem_fwd(qm, x1, km, vm, w_mo, g_post, g_fpre):
    def body(q_ref, x1_ref, k_ref, v_ref, wo_ref, gp_ref, gf_ref, om_ref, ym_ref, x2_ref, h3_ref):
        for h in range(MEM_HEADS):
            sl = slice(h * MEM_HD, (h + 1) * MEM_HD)
            s = _nt(q_ref[:, sl], k_ref[:, sl]) * MEM_SCALE
            p = jnp.exp(s - jnp.max(s, axis=-1, keepdims=True))
            p = p / jnp.sum(p, axis=-1, keepdims=True)
            om_ref[:, sl] = _nn(p.astype(BF16), v_ref[:, sl]).astype(BF16)
        ym = _nn(om_ref[...], _w(wo_ref))
        ym_ref[...] = ym
        x2 = x1_ref[...] + _rms(ym, gp_ref[...])
        x2_ref[...] = x2
        h3_ref[...] = _rms(x2, gf_ref[...]).astype(BF16)

    return _tok_call(body, "mem_fwd", [qm, x1], [km, vm, w_mo, g_post, g_fpre],
                     [(D, BF16), (D, F32), (D, F32), (D, BF16)], tm=TM_WIDE, vmem=VMEM_BIG)


def _ffn_fwd(h3, x2, tgt, w1_t, w2, g_post):
    def body(h_ref, x2_ref, t_ref, w1_ref, w2_ref, g_ref, a_ref, y_ref, dx_ref, loss_ref):
        @pl.when(pl.program_id(0) == 0)
        def _():
            loss_ref[...] = jnp.zeros_like(loss_ref)

        h = h_ref[...]
        y = jnp.zeros((TM_WIDE, D), F32)
        for c in range(4):
            cols = slice(c * (DFF // 4), (c + 1) * (DFF // 4))
            a = _nt(h, w1_ref[2 * c:2 * c + 2].reshape(DFF // 4, D))
            a_ref[:, cols] = a.astype(BF16)
            y = y + _nn(jnp.square(jnp.maximum(a, 0.0)).astype(BF16), w2_ref[2 * c:2 * c + 2].reshape(DFF // 4, D))
        y_ref[...] = y
        e = x2_ref[...] + _rms(y, g_ref[...]) - t_ref[...]
        dx_ref[...] = e * (1.0 / D)
        loss_ref[...] += 0.5 * jnp.sum(jnp.sum(e * e, axis=-1, keepdims=True) * (1.0 / D))

    return _tok_call(body, "ffn_fwd", [h3, x2, tgt], [w1_t, w2, g_post],
                     [(DFF, BF16), (D, F32), (D, F32)], [(8, 128)], tm=TM_WIDE, vmem=VMEM_BIG)


def _tri(lower):
    r = lax.broadcasted_iota(jnp.int32, (128, 128), 0)
    c = lax.broadcasted_iota(jnp.int32, (128, 128), 1)
    return jnp.where(r >= c if lower else c >= r, 1.0, 0.0).astype(F32)


def _hdot(a, b):
    return jnp.dot(a, b, preferred_element_type=F32, precision=lax.Precision.HIGHEST)


def _gate_fwd(flog, b_pad):
    def body(f_ref, b_ref, c_ref):
        tri = _tri(True)

        def step(i, carry):
            rows = pl.ds(pl.multiple_of(i * 128, 128), 128)
            z = f_ref[rows, :] + b_ref[...]
            lf = jnp.minimum(z, 0.0) - jnp.log(1.0 + jnp.exp(-jnp.abs(z)))
            cb = _hdot(tri, lf) + carry
            c_ref[rows, :] = cb
            return cb[127:128, :]

        lax.fori_loop(0, T // 128, step, jnp.zeros((1, 128), F32))

    return _one_call(body, "gate_fwd", [flog, b_pad], [((T, 128), F32)])[0]


def _gate_bwd(dc, flog, b_pad):
    def body(dc_ref, f_ref, b_ref, df_ref, db_ref):
        tri = _tri(False)

        def step(j, carry):
            run, db = carry
            i = T // 128 - 1 - j
            rows = pl.ds(pl.multiple_of(i * 128, 128), 128)
            dcb = dc_ref[rows, :]
            rb = _hdot(tri, dcb) + run
            z = f_ref[rows, :] + b_ref[...]
            df = rb * (1.0 / (1.0 + jnp.exp(z)))
            df_ref[rows, :] = df.astype(BF16)
            return run + jnp.sum(dcb, axis=0, keepdims=True), db + jnp.sum(df, axis=0, keepdims=True)

        _, db = lax.fori_loop(0, T // 128, step, (jnp.zeros((1, 128), F32), jnp.zeros((1, 128), F32)))
        db_ref[...] = jnp.broadcast_to(db, (8, 128))

    return _one_call(body, "gate_bwd", [dc, flog, b_pad], [((T, 128), BF16), ((8, 128), F32)])


def _lane_lo(rows=TQ):
    return lax.broadcasted_iota(jnp.int32, (rows, 128), 1) < HD


def _half(v, lo, a, scale=None):
    keep = lo if a == 0 else jnp.logical_not(lo)
    v = v.astype(F32) if scale is None else v.astype(F32) * scale
    return jnp.where(keep, v, 0.0).astype(BF16)


def _fox_specs():
    return [pl.BlockSpec((FQ, 128), lambda h, i: (i, h)),
            pl.BlockSpec((T, 128), lambda h, i: (0, 4 + h)),
            pl.BlockSpec((T, 128), lambda h, i: (0, 8 + h))]


def _lane_pick(x, at):
    lane = lax.broadcasted_iota(jnp.int32, x.shape, 1)
    return jnp.sum(jnp.where(lane == at, x, 0.0), axis=-1, keepdims=True)


def _fox_fwd(proj, c, ct3):
    def body(q_ref, k_ref, v_ref, c_ref, ct_ref, o_ref, l_ref):
        i = pl.program_id(1)
        lo = _lane_lo(FQ)
        causal = lax.broadcasted_iota(jnp.int32, (FQ, FQ), 1) <= lax.broadcasted_iota(jnp.int32, (FQ, FQ), 0)
        q = q_ref[...]
        qs = [_half(q, lo, a, SCALE) for a in range(2)]
        cqs = [_lane_pick(c_ref[...], 2 * pl.program_id(0) + a) for a in range(2)]

        def tile(off, carry, diagonal):
            kblk = k_ref[pl.ds(off, FQ), :]
            vblk = v_ref[pl.ds(off, FQ), :]
            new = []
            for a in range(2):
                m, l, acc = carry[a]
                s = _nt(qs[a], kblk) + (cqs[a] - ct_ref[a:a + 1, pl.ds(off, FQ)])
                if diagonal:
                    s = jnp.where(causal, s, NEG)
                m2 = jnp.maximum(m, jnp.max(s, axis=-1, keepdims=True))
                p = jnp.exp(s - m2)
                alpha = jnp.exp(m - m2)
                new.append((m2, alpha * l + jnp.sum(p, axis=-1, keepdims=True),
                            alpha * acc + _nn(p.astype(BF16), vblk)))
            return tuple(new)

        init = (jnp.full((FQ, 1), NEG, F32), jnp.zeros((FQ, 1), F32), jnp.zeros((FQ, 128), F32))
        carry = lax.fori_loop(0, i, lambda kb, c: tile(pl.multiple_of(kb * FQ, FQ), c, False), (init, init))
        carry = tile(pl.multiple_of(i * FQ, FQ), carry, True)
        outs = []
        for a in range(2):
            m, l, acc = carry[a]
            outs.append(acc / l)
            l_ref[:, 128 * a:128 * a + 128] = jnp.broadcast_to(m + jnp.log(l), (FQ, 128))
        o_ref[...] = jnp.where(lo, outs[0], outs[1])

    return _pcall(
        body, name="fox_fwd", grid=(4, T // FQ),
        in_specs=_fox_specs() + [pl.BlockSpec((FQ, 128), lambda h, i: (i, 0)),
                                 pl.BlockSpec((None, 2, T), lambda h, i: (h, 0, 0))],
        out_specs=[pl.BlockSpec((FQ, 128), lambda h, i: (i, h)), pl.BlockSpec((FQ, 256), lambda h, i: (i, h))],
        out_shape=[jax.ShapeDtypeStruct((T, 512), F32), jax.ShapeDtypeStruct((T, 1024), F32)],
        compiler_params=pltpu.CompilerParams(dimension_semantics=("arbitrary", "arbitrary"), vmem_limit_bytes=VMEM_BIG),
    )(proj, proj, proj, c, ct3)


def _fox_bwd(proj, c, ct3, o, lse, do):
    def body(q_ref, k_ref, v_ref, c_ref, ct_ref, o_ref, l_ref, do_ref, dq_ref, dkb_ref, dvb_ref, dct_ref, dcq_ref,
             dk_ref, dv_ref):
        i = pl.program_id(1)

        @pl.when(i == 0)
        def _():
            dk_ref[...] = jnp.zeros_like(dk_ref)
            dv_ref[...] = jnp.zeros_like(dv_ref)
            dct_ref[...] = jnp.zeros_like(dct_ref)

        lo = _lane_lo(FQ)
        causal = lax.broadcasted_iota(jnp.int32, (FQ, FQ), 1) <= lax.broadcasted_iota(jnp.int32, (FQ, FQ), 0)
        q = q_ref[...]
        do_v = do_ref[...]
        prod = do_v * o_ref[...]
        qs = [_half(q, lo, a, SCALE) for a in range(2)]
        dos = [_half(do_v, lo, a) for a in range(2)]
        deltas = [jnp.sum(jnp.where(lo if a == 0 else jnp.logical_not(lo), prod, 0.0), axis=-1, keepdims=True)
                  for a in range(2)]
        cqs = [_lane_pick(c_ref[...], 2 * pl.program_id(0) + a) for a in range(2)]
        las = [l_ref[:, 128 * a:128 * a + 1] for a in range(2)]

        def tile(off, carry, diagonal):
            kblk = k_ref[pl.ds(off, FQ), :]
            vblk = v_ref[pl.ds(off, FQ), :]
            new = []
            dk = jnp.zeros((FQ, 128), F32)
            dv = jnp.zeros((FQ, 128), F32)
            for a in range(2):
                dq_acc, rs = carry[a]
                s = _nt(qs[a], kblk) + (cqs[a] - ct_ref[a:a + 1, pl.ds(off, FQ)])
                if diagonal:
                    s = jnp.where(causal, s, NEG)
                p = jnp.exp(s - las[a])
                ds = p * (_nt(dos[a], vblk) - deltas[a])
                dsb = ds.astype(BF16)
                dk = dk + _tn(dsb, qs[a])
                dv = dv + _tn(p.astype(BF16), dos[a])
                dct_ref[a:a + 1, pl.ds(off, FQ)] -= jnp.sum(ds, axis=0, keepdims=True)
                new.append((dq_acc + _nn(dsb, kblk), rs + jnp.sum(ds, axis=-1, keepdims=True)))
            dk_ref[pl.ds(off, FQ), :] += dk
            dv_ref[pl.ds(off, FQ), :] += dv
            return tuple(new)

        init = (jnp.zeros((FQ, 128), F32), jnp.zeros((FQ, 1), F32))
        carry = lax.fori_loop(0, i, lambda kb, c: tile(pl.multiple_of(kb * FQ, FQ), c, False), (init, init))
        carry = tile(pl.multiple_of(i * FQ, FQ), carry, True)
        lane = lax.broadcasted_iota(jnp.int32, (FQ, 128), 1)
        dcq_ref[...] = jnp.where(lane == 0, carry[0][1], jnp.where(lane == 1, carry[1][1], 0.0))
        dq_ref[...] = (jnp.where(lo, carry[0][0], carry[1][0]) * SCALE).astype(BF16)

        @pl.when(i == T // FQ - 1)
        def _():
            dkb_ref[...] = dk_ref[...].astype(BF16)
            dvb_ref[...] = dv_ref[...].astype(BF16)

    blk = pl.BlockSpec((FQ, 128), lambda h, i: (i, h))
    wide = pl.BlockSpec((FQ, 256), lambda h, i: (i, h))
    rows = pl.BlockSpec((None, 2, T), lambda h, i: (h, 0, 0))
    col = pl.BlockSpec((T, 128), lambda h, i: (0, h))
    return _pcall(
        body, name="fox_bwd", grid=(4, T // FQ),
        in_specs=_fox_specs() + [pl.BlockSpec((FQ, 128), lambda h, i: (i, 0)), rows, blk, wide, blk],
        out_specs=[blk, col, col, rows, pl.BlockSpec((None, FQ, 128), lambda h, i: (h, i, 0))],
        out_shape=[jax.ShapeDtypeStruct((T, 512), BF16), jax.ShapeDtypeStruct((T, 512), BF16),
                   jax.ShapeDtypeStruct((T, 512), BF16), jax.ShapeDtypeStruct((4, 2, T), F32),
                   jax.ShapeDtypeStruct((4, T, 128), F32)],
        scratch_shapes=[pltpu.VMEM((T, 128), F32), pltpu.VMEM((T, 128), F32)],
        compiler_params=pltpu.CompilerParams(dimension_semantics=("arbitrary", "arbitrary"), vmem_limit_bytes=VMEM_BIG),
    )(proj, proj, proj, c, ct3, o, lse, do)


AUG_ROWSUM, AUG_COLSUM = 67, 64


def _fox_prep(proj, c2):
    def body(q_ref, k_ref, c_ref, qa_ref, ka_ref):
        lane = lax.broadcasted_iota(jnp.int32, (FQ, 128), 1)
        q = q_ref[...].astype(F32) * SCALE
        k = k_ref[...].astype(F32)
        for a in range(2):
            c = c_ref[:, 128 * a:128 * a + 128]
            hi = c.astype(BF16).astype(F32)
            mid = (c - hi).astype(BF16).astype(F32)
            lo = c - hi - mid
            parts = lambda first, sign: jnp.where(lane == first, sign * hi, jnp.where(lane == first + 1, sign * mid, sign * lo))
            qd = q if a == 0 else pltpu.roll(q, 64, 1)
            kd = k if a == 0 else pltpu.roll(k, 64, 1)
            ones = jnp.ones((FQ, 128), F32)
            qa = jnp.where(lane < 64, qd, jnp.where(lane < 67, ones, jnp.where(lane < 70, parts(67, 1.0), 0.0)))
            ka = jnp.where(lane < 64, kd, jnp.where(lane < 67, parts(64, -1.0), jnp.where(lane < 70, ones, 0.0)))
            qa_ref[:, 128 * a:128 * a + 128] = qa.astype(BF16)
            ka_ref[:, 128 * a:128 * a + 128] = ka.astype(BF16)

    wide = pl.BlockSpec((FQ, 256), lambda h, i: (i, h))
    return _pcall(
        body, name="fox_prep", grid=(4, T // FQ),
        in_specs=[pl.BlockSpec((FQ, 128), lambda h, i: (i, h)), pl.BlockSpec((FQ, 128), lambda h, i: (i, 4 + h)), wide],
        out_specs=[wide, wide], out_shape=[jax.ShapeDtypeStruct((T, 1024), BF16)] * 2,
        compiler_params=pltpu.CompilerParams(dimension_semantics=("arbitrary", "arbitrary")),
    )(proj, proj, c2)


def _fox_aug_specs():
    return [pl.BlockSpec((FQ, 256), lambda h, i: (i, h)), pl.BlockSpec((T, 256), lambda h, i: (0, h)),
            pl.BlockSpec((T, 128), lambda h, i: (0, 8 + h))]


def _causal():
    return lax.broadcasted_iota(jnp.int32, (FQ, FQ), 1) <= lax.broadcasted_iota(jnp.int32, (FQ, FQ), 0)


def _fox_fwd_aug(q_aug, k_aug, proj):
    def body(q_ref, k_ref, v_ref, o_ref, l_ref):
        i = pl.program_id(1)
        lo = _lane_lo(FQ)
        causal = _causal()
        qs = [q_ref[:, 128 * a:128 * a + 128] for a in range(2)]

        def tile(off, carry, diagonal):
            vblk = v_ref[pl.ds(off, FQ), :]
            new = []
            for a in range(2):
                m, l, acc = carry[a]
                s = _nt(qs[a], k_ref[pl.ds(off, FQ), 128 * a:128 * a + 128])
                if diagonal:
                    s = jnp.where(causal, s, NEG)
                m2 = jnp.maximum(m, jnp.max(s, axis=-1, keepdims=True))
                p = jnp.exp(s - m2)
                alpha = jnp.exp(m - m2)
                new.append((m2, alpha * l + jnp.sum(p, axis=-1, keepdims=True),
                            alpha * acc + _nn(p.astype(BF16), vblk)))
            return tuple(new)

        init = (jnp.full((FQ, 1), NEG, F32), jnp.zeros((FQ, 1), F32), jnp.zeros((FQ, 128), F32))
        carry = lax.fori_loop(0, i, lambda kb, c: tile(pl.multiple_of(kb * FQ, FQ), c, False), (init, init))
        carry = tile(pl.multiple_of(i * FQ, FQ), carry, True)
        outs = []
        for a in range(2):
            m, l, acc = carry[a]
            outs.append(acc / l)
            l_ref[:, 128 * a:128 * a + 128] = jnp.broadcast_to(m + jnp.log(l), (FQ, 128))
        o_ref[...] = jnp.where(lo, outs[0], outs[1])

    return _pcall(
        body, name="fox_fwd", grid=(4, T // FQ), in_specs=_fox_aug_specs(),
        out_specs=[pl.BlockSpec((FQ, 128), lambda h, i: (i, h)), pl.BlockSpec((FQ, 256), lambda h, i: (i, h))],
        out_shape=[jax.ShapeDtypeStruct((T, 512), F32), jax.ShapeDtypeStruct((T, 1024), F32)],
        compiler_params=pltpu.CompilerParams(dimension_semantics=("arbitrary", "arbitrary"), vmem_limit_bytes=VMEM_BIG),
    )(q_aug, k_aug, proj)


def _fox_bwd_aug(q_aug, k_aug, proj, o, lse, do, after):
    nq = T // FQ

    def body(q_ref, k_ref, v_ref, o_ref, l_ref, do_ref, after_ref, dq_ref, dkb_ref, dvb_ref, dcq_ref, dck_ref, dk_ref,
             dv_ref):
        i = pl.program_id(1)

        @pl.when(i == 0)
        def _():
            dk_ref[...] = jnp.zeros_like(dk_ref)
            dv_ref[...] = jnp.zeros_like(dv_ref)

        lo = _lane_lo(FQ)
        lane = lax.broadcasted_iota(jnp.int32, (FQ, 128), 1)
        causal = _causal()
        do_v = do_ref[...]
        prod = do_v * o_ref[...]
        qs = [q_ref[:, 128 * a:128 * a + 128] for a in range(2)]
        dos = [_half(do_v, lo, a) for a in range(2)]
        deltas = [jnp.sum(jnp.where(lo if a == 0 else jnp.logical_not(lo), prod, 0.0), axis=-1, keepdims=True)
                  for a in range(2)]
        las = [l_ref[:, 128 * a:128 * a + 1] for a in range(2)]

        def tile(off, carry, diagonal):
            vblk = v_ref[pl.ds(off, FQ), :]
            new = []
            dv = jnp.zeros((FQ, 128), F32)
            for a in range(2):
                kblk = k_ref[pl.ds(off, FQ), 128 * a:128 * a + 128]
                s = _nt(qs[a], kblk)
                if diagonal:
                    s = jnp.where(causal, s, NEG)
                p = jnp.exp(s - las[a])
                dsb = (p * (_nt(dos[a], vblk) - deltas[a])).astype(BF16)
                dk_ref[a, pl.ds(off, FQ), :] += _tn(dsb, qs[a])
                dv = dv + _tn(p.astype(BF16), dos[a])
                new.append(carry[a] + _nn(dsb, kblk))
            dv_ref[pl.ds(off, FQ), :] += dv
            return tuple(new)

        init = jnp.zeros((FQ, 128), F32)
        dqs = lax.fori_loop(0, i, lambda kb, c: tile(pl.multiple_of(kb * FQ, FQ), c, False), (init, init))
        dqs = tile(pl.multiple_of(i * FQ, FQ), dqs, True)
        pick = lambda x, at: jnp.sum(jnp.where(lane == at, x, 0.0), axis=-1, keepdims=True)
        for a in range(2):
            dcq_ref[:, 128 * a:128 * a + 128] = jnp.broadcast_to(pick(dqs[a], AUG_ROWSUM), (FQ, 128))
        dq_ref[...] = (jnp.where(lo, dqs[0], pltpu.roll(dqs[1], 64, 1)) * SCALE).astype(BF16)

        @pl.when(i == nq - 1)
        def _():
            big_lane = lax.broadcasted_iota(jnp.int32, (T, 128), 1)
            dkb_ref[...] = jnp.where(big_lane < 64, dk_ref[0], pltpu.roll(dk_ref[1], 64, 1)).astype(BF16)
            dvb_ref[...] = dv_ref[...].astype(BF16)
            for a in range(2):
                col = jnp.sum(jnp.where(big_lane == AUG_COLSUM, dk_ref[a], 0.0), axis=-1, keepdims=True)
                dck_ref[:, 128 * a:128 * a + 128] = jnp.broadcast_to(-col, (T, 128))

    blk = pl.BlockSpec((FQ, 128), lambda h, i: (i, h))
    wide = pl.BlockSpec((FQ, 256), lambda h, i: (i, h))
    col = pl.BlockSpec((T, 128), lambda h, i: (0, h))
    colwide = pl.BlockSpec((T, 256), lambda h, i: (0, h))
    return _pcall(
        body, name="fox_bwd", grid=(4, nq), in_specs=_fox_aug_specs() + [blk, wide, blk, ANY_SPEC],
        out_specs=[blk, col, col, wide, colwide],
        out_shape=[jax.ShapeDtypeStruct((T, 512), BF16)] * 3 + [jax.ShapeDtypeStruct((T, 1024), F32)] * 2,
        scratch_shapes=[pltpu.VMEM((2, T, 128), F32), pltpu.VMEM((T, 128), F32)],
        compiler_params=pltpu.CompilerParams(dimension_semantics=("arbitrary", "arbitrary"), vmem_limit_bytes=VMEM_BIG),
    )(q_aug, k_aug, proj, o, lse, do, after)


def _rel_onehot():
    ridx = lax.broadcasted_iota(jnp.int32, (NREL_PAD, VW), 0)
    j = lax.broadcasted_iota(jnp.int32, (NREL_PAD, VW), 1)
    return jnp.where(ridx == jnp.clip(TQ + LEFT - 1 - j, -128, 128) + 128, 1.0, 0.0).astype(F32)


def _relvec_fwd(tbl):
    def body(t_ref, v_ref):
        v_ref[...] = _hdot(t_ref[...], _rel_onehot())

    return _one_call(body, "relvec_fwd", [tbl], [((8, VW), F32)])[0]


def _relvec_bwd(gv):
    def body(g_ref, t_ref):
        t_ref[...] = lax.dot_general(g_ref[...], _rel_onehot(), (((1,), (1,)), ((), ())),
                                     preferred_element_type=F32, precision=lax.Precision.HIGHEST)

    return _one_call(body, "relvec_bwd", [gv], [((8, NREL_PAD), F32)])[0]


def _chk_bias(vt_ref, a, hidden):
    vb = jnp.broadcast_to(vt_ref[a:a + 1, :], (TQ, VW))
    y = pltpu.roll(vb, VW - (TQ - 1), 1, stride=1, stride_axis=0)[:, :WIN]
    cr = lax.broadcasted_iota(jnp.int32, (TQ, WIN), 0) // 64
    m = lax.broadcasted_iota(jnp.int32, (TQ, WIN), 1)
    return jnp.where((m // 64 >= cr) & (m // 64 <= cr + 8) & (m >= hidden), y, NEG)


def _chk_specs():
    return [pl.BlockSpec((TQ, 128), lambda h, i: (i, CHK0 // 128 + h)),
            pl.BlockSpec((T + LEFT, 128), lambda h, i: (0, h)),
            pl.BlockSpec((T + LEFT, 128), lambda h, i: (0, 4 + h)),
            pl.BlockSpec((None, 2, VW), lambda h, i: (h, 0, 0))]


def _chk_fwd(proj, kvp, vt3):
    def body(q_ref, k_ref, v_ref, vt_ref, o_ref, l_ref, bias_ref):
        i = pl.program_id(1)

        @pl.when(i == 0)
        def _():
            for first in range(3):
                for a in range(2):
                    bias_ref[first, a] = _chk_bias(vt_ref, a, max(LEFT - first * TQ, 0))

        lo = _lane_lo()
        off = pl.multiple_of(i * TQ, TQ)
        kw = k_ref[pl.ds(off, WIN), :]
        vw = v_ref[pl.ds(off, WIN), :]
        bias_at = jnp.minimum(i, 2)
        q = q_ref[...]
        outs = []
        for a in range(2):
            s = _nt(_half(q, lo, a, SCALE), kw) + bias_ref[bias_at, a]
            m = jnp.max(s, axis=-1, keepdims=True)
            p = jnp.exp(s - m)
            l = jnp.sum(p, axis=-1, keepdims=True)
            outs.append(_nn(p.astype(BF16), vw) / l)
            l_ref[:, 128 * a:128 * a + 128] = jnp.broadcast_to(m + jnp.log(l), (TQ, 128))
        o_ref[...] = jnp.where(lo, outs[0], outs[1])

    return _pcall(
        body, name="chk_fwd", grid=(4, T // TQ), in_specs=_chk_specs(),
        out_specs=[pl.BlockSpec((TQ, 128), lambda h, i: (i, h)), pl.BlockSpec((TQ, 256), lambda h, i: (i, h))],
        out_shape=[jax.ShapeDtypeStruct((T, 512), F32), jax.ShapeDtypeStruct((T, 1024), F32)],
        scratch_shapes=[pltpu.VMEM((3, 2, TQ, WIN), F32)],
        compiler_params=pltpu.CompilerParams(dimension_semantics=("arbitrary", "arbitrary")),
    )(proj, kvp, kvp, vt3)


def _chk_bwd(proj, kvp, vt3, o, lse, do):
    nq = T // TQ

    def body(q_ref, k_ref, v_ref, vt_ref, o_ref, l_ref, do_ref, dq_ref, dkb_ref, dvb_ref, gv_ref, bias_ref, dsum_ref,
             dk_ref, dv_ref):
        i = pl.program_id(1)

        @pl.when(i == 0)
        def _():
            for first in range(3):
                for a in range(2):
                    bias_ref[first, a] = _chk_bias(vt_ref, a, max(LEFT - first * TQ, 0))
            dsum_ref[...] = jnp.zeros_like(dsum_ref)
            dk_ref[...] = jnp.zeros_like(dk_ref)
            dv_ref[...] = jnp.zeros_like(dv_ref)

        lo = _lane_lo()
        off = pl.multiple_of(i * TQ, TQ)
        kw = k_ref[pl.ds(off, WIN), :]
        vw = v_ref[pl.ds(off, WIN), :]
        bias_at = jnp.minimum(i, 2)
        q = q_ref[...]
        do_v = do_ref[...]
        prod = do_v * o_ref[...]
        dqs = []
        for a in range(2):
            keep = lo if a == 0 else jnp.logical_not(lo)
            qa = _half(q, lo, a, SCALE)
            doa = _half(do_v, lo, a)
            delta = jnp.sum(jnp.where(keep, prod, 0.0), axis=-1, keepdims=True)
            s = _nt(qa, kw) + bias_ref[bias_at, a]
            p = jnp.exp(s - l_ref[:, 128 * a:128 * a + 1])
            ds = p * (_nt(doa, vw) - delta)
            dsum_ref[a] += ds
            dsb = ds.astype(BF16)
            dk_ref[pl.ds(off, WIN), :] += _tn(dsb, qa)
            dv_ref[pl.ds(off, WIN), :] += _tn(p.astype(BF16), doa)
            dqs.append(_nn(dsb, kw))
        dq_ref[...] = (jnp.where(lo, dqs[0], dqs[1]) * SCALE).astype(BF16)

        @pl.when(i == nq - 1)
        def _():
            dkb_ref[...] = dk_ref[LEFT:, :].astype(BF16)
            dvb_ref[...] = dv_ref[LEFT:, :].astype(BF16)
            rr = lax.broadcasted_iota(jnp.int32, (TQ, TQ), 0)
            cc = lax.broadcasted_iota(jnp.int32, (TQ, TQ), 1)
            flip = jnp.where(rr + cc == TQ - 1, 1.0, 0.0).astype(F32)
            for a in range(2):
                dpad = jnp.concatenate([dsum_ref[a], jnp.zeros((TQ, VW - WIN), F32)], axis=1)
                z = pltpu.roll(_hdot(flip, dpad), 0, 1, stride=1, stride_axis=0)
                gv_ref[a:a + 1, :] = jnp.sum(z, axis=0, keepdims=True)

    blk = pl.BlockSpec((TQ, 128), lambda h, i: (i, h))
    wide = pl.BlockSpec((TQ, 256), lambda h, i: (i, h))
    col = pl.BlockSpec((T, 128), lambda h, i: (0, h))
    return _pcall(
        body, name="chk_bwd", grid=(4, nq), in_specs=_chk_specs() + [blk, wide, blk],
        out_specs=[blk, col, col, pl.BlockSpec((None, 2, VW), lambda h, i: (h, 0, 0))],
        out_shape=[jax.ShapeDtypeStruct((T, 512), BF16), jax.ShapeDtypeStruct((T, 512), BF16),
                   jax.ShapeDtypeStruct((T, 512), BF16), jax.ShapeDtypeStruct((4, 2, VW), F32)],
        scratch_shapes=[pltpu.VMEM((3, 2, TQ, WIN), F32), pltpu.VMEM((2, TQ, WIN), F32),
                        pltpu.VMEM((T + LEFT, 128), F32), pltpu.VMEM((T + LEFT, 128), F32)],
        compiler_params=pltpu.CompilerParams(dimension_semantics=("arbitrary", "arbitrary")),
    )(proj, kvp, kvp, vt3, o, lse, do)


def _zero_at_start(*refs):
    @pl.when(pl.program_id(0) == 0)
    def _():
        for r in refs:
            r[...] = jnp.zeros_like(r)


def _ffn_bwd(dx3, y3, x2, a, w1_t, w2, g_post, g_pre):
    def body(dx3_ref, y_ref, x2_ref, a_ref, w1_ref, w2_ref, gp_ref, gf_ref,
             dx2_ref, da_ref, dy_ref, r_ref, dgp_ref, dgf_ref):
        _zero_at_start(dgp_ref, dgf_ref)
        dx3_v = dx3_ref[...]
        dy, dgp = _rms_bwd(y_ref[...], gp_ref[...], dx3_v)
        dgp_ref[...] += dgp
        dyb = dy.astype(BF16)
        dy_ref[...] = dyb
        ra = jnp.maximum(a_ref[...].astype(F32), 0.0)
        r_ref[...] = jnp.square(ra).astype(BF16)
        da = (_nt(dyb, _w(w2_ref)) * (2.0 * ra)).astype(BF16)
        da_ref[...] = da
        dh, dgf = _rms_bwd(x2_ref[...], gf_ref[...], _nn(da, _w(w1_ref)))
        dgf_ref[...] += dgf
        dx2_ref[...] = dx3_v + dh

    return _tok_call(body, "ffn_bwd", [dx3, y3, x2, a], [w1_t, w2, g_post, g_pre],
                     [(D, F32), (DFF, BF16), (D, BF16), (DFF, BF16)], [(1, D), (1, D)], vmem=VMEM_BIG)


def _mem_bwd(dx2, ym, x1, qm, km, vm, w_mo, w_mq, g_post, g_pre):
    def body(dx2_ref, ym_ref, x1_ref, q_ref, k_ref, v_ref, wo_ref, wq_ref, gp_ref, gm_ref,
             dx1_ref, dym_ref, dq_ref, dk_ref, dv_ref, dgp_ref, dgm_ref, dom_ref):
        _zero_at_start(dk_ref, dv_ref, dgp_ref, dgm_ref)
        dx2_v = dx2_ref[...]
        dym, dgp = _rms_bwd(ym_ref[...], gp_ref[...], dx2_v)
        dgp_ref[...] += dgp
        dymb = dym.astype(BF16)
        dym_ref[...] = dymb
        dom_ref[...] = _nt(dymb, _w(wo_ref)).astype(BF16)
        for h in range(MEM_HEADS):
            sl = slice(h * MEM_HD, (h + 1) * MEM_HD)
            qh, kh, doh = q_ref[:, sl], k_ref[:, sl], dom_ref[:, sl]
            s = _nt(qh, kh) * MEM_SCALE
            p = jnp.exp(s - jnp.max(s, axis=-1, keepdims=True))
            p = p / jnp.sum(p, axis=-1, keepdims=True)
            dp = _nt(doh, v_ref[:, sl])
            ds = (p * (dp - jnp.sum(p * dp, axis=-1, keepdims=True))).astype(BF16)
            dq_ref[:, sl] = (_nn(ds, kh) * MEM_SCALE).astype(BF16)
            dk_ref[:, sl] += _tn(ds, qh) * MEM_SCALE
            dv_ref[:, sl] += _tn(p.astype(BF16), doh)
        dh, dgm = _rms_bwd(x1_ref[...], gm_ref[...], _nt(dq_ref[...], _w(wq_ref)))
        dgm_ref[...] += dgm
        dx1_ref[...] = dx2_v + dh

    tiled = pl.BlockSpec((TM_WIDE, D), lambda i: (i, 0))
    in_specs = [tiled] * 4 + [_resident(a) for a in (km, vm, w_mo, w_mq, g_post, g_pre)]
    w_mo, w_mq = w_mo[0], w_mq[0]
    kv = pl.BlockSpec((NMEM, D), lambda i: (0, 0))
    vec = pl.BlockSpec((1, D), lambda i: (0, 0))
    return _pcall(
        body, name="mem_bwd", grid=(T // TM_WIDE,), in_specs=in_specs,
        out_specs=[tiled, tiled, tiled, kv, kv, vec, vec],
        out_shape=[jax.ShapeDtypeStruct((T, D), F32), jax.ShapeDtypeStruct((T, D), BF16),
                   jax.ShapeDtypeStruct((T, D), BF16), jax.ShapeDtypeStruct((NMEM, D), F32),
                   jax.ShapeDtypeStruct((NMEM, D), F32), jax.ShapeDtypeStruct((1, D), F32),
                   jax.ShapeDtypeStruct((1, D), F32)],
        scratch_shapes=[pltpu.VMEM((TM_WIDE, D), BF16)],
        compiler_params=pltpu.CompilerParams(dimension_semantics=("arbitrary",), vmem_limit_bytes=VMEM_BIG),
    )(dx2, ym, x1, qm, km, vm, w_mo, w_mq, g_post, g_pre)


def _memkv_bwd(dkm, dvm, mem, w_mk, w_mv):
    def body(dk_ref, dv_ref, m_ref, wk_ref, wv_ref, dg_ref):
        dmn = _nt(dk_ref[...].astype(BF16), _w(wk_ref)) + _nt(dv_ref[...].astype(BF16), _w(wv_ref))
        mv = m_ref[...]
        dg_ref[...] = jnp.sum(dmn * (mv * _rstd(mv)), axis=0, keepdims=True)

    return _one_call(body, "memkv_bwd", [dkm, dvm, mem, w_mk, w_mv], [((1, D), F32)], vmem=VMEM_BIG)[0]


def _postmix_bwd(dx1, z, o_f, o_c, w_out, g_post, g_fo, g_co):
    def body(dx1_ref, z_ref, of_ref, oc_ref, wo_ref, gp_ref, gfo_ref, gco_ref,
             dz_ref, dof_ref, doc_ref, dgp_ref, dgfo_ref, dgco_ref):
        _zero_at_start(dgp_ref, dgfo_ref, dgco_ref)
        dz, dgp = _rms_bwd(z_ref[...], gp_ref[...], dx1_ref[...])
        dgp_ref[...] += dgp
        dzb = dz.astype(BF16)
        dz_ref[...] = dzb
        dy = _nt(dzb, _w(wo_ref))
        dof, dgfo = _rms_bwd(of_ref[...], gfo_ref[...], dy[:, :512])
        doc, dgco = _rms_bwd(oc_ref[...], gco_ref[...], dy[:, 512:])
        dof_ref[...] = dof
        doc_ref[...] = doc
        dgfo_ref[...] += dgfo
        dgco_ref[...] += dgco

    return _tok_call(body, "postmix_bwd", [dx1, z, o_f, o_c], [w_out, g_post, g_fo, g_co],
                     [(D, BF16), (512, F32), (512, F32)], [(1, D), (1, 512), (1, 512)], tm=TM_WIDE, vmem=VMEM_BIG)


def _premix_bwd(dx1, x, pieces, win_t, g_pre):
    def body(dx1_ref, x_ref, *refs):
        piece_refs, (w_ref, g_ref, dx_ref, dp_ref, dg_ref) = refs[:len(pieces)], refs[len(pieces):]
        _zero_at_start(dg_ref)
        col = 0
        for p in piece_refs:
            dp_ref[:, col:col + p.shape[1]] = p[...]
            col += p.shape[1]
        dh, dg = _rms_bwd(x_ref[...], g_ref[...], _nn(dp_ref[...], w_ref[...]))
        dg_ref[...] += dg
        dx_ref[...] = dx1_ref[...] + dh

    return _tok_call(body, "premix_bwd", [dx1, x] + list(pieces), [win_t, g_pre], [(D, F32), (PROJ, BF16)], [(1, D)],
                     tm=TM_WIDE, vmem=VMEM_BIG)


def _wgrad(a, b, name):
    k, m = a.shape
    n = b.shape[1]
    tm = 640 if m % 640 == 0 and m > 1024 else min(m, 512)
    tn = min(n, 1024)

    def body(a_ref, b_ref, o_ref):
        o_ref[...] = _tn(a_ref[...].astype(BF16), b_ref[...].astype(BF16))

    return _pcall(
        body, name=name, grid=(m // tm, n // tn),
        in_specs=[pl.BlockSpec((k, tm), lambda i, j: (0, i)), pl.BlockSpec((k, tn), lambda i, j: (0, j))],
        out_specs=pl.BlockSpec((tm, tn), lambda i, j: (i, j)),
        out_shape=jax.ShapeDtypeStruct((m, n), F32),
        compiler_params=pltpu.CompilerParams(dimension_semantics=("arbitrary", "arbitrary"), vmem_limit_bytes=VMEM_BIG),
    )(a, b)


def _wgrad_group(name, pairs, rows):
    def body(*refs):
        o_ref = refs[-1]
        for k in range(len(pairs)):
            o_ref[k * rows:(k + 1) * rows, :] = _tn(refs[2 * k][...].astype(BF16), refs[2 * k + 1][...].astype(BF16))

    in_specs, ops = [], []
    for a, b in pairs:
        in_specs += [pl.BlockSpec((a.shape[0], rows), lambda j: (0, j)), _resident(b)]
        ops += [a, b]
    return _pcall(
        body, name=name, grid=(8,), in_specs=in_specs,
        out_specs=pl.BlockSpec((None, len(pairs) * rows, D), lambda j: (j, 0, 0)),
        out_shape=jax.ShapeDtypeStruct((8, len(pairs) * rows, D), F32),
        compiler_params=pltpu.CompilerParams(dimension_semantics=("arbitrary",), vmem_limit_bytes=VMEM_BIG),
    )(*ops)


def _adam_math(w, g, m, v):
    m2 = ADAM_B1 * m + (1.0 - ADAM_B1) * g
    v2 = ADAM_B2 * v + (1.0 - ADAM_B2) * jnp.square(g)
    m_hat = m2 / (1.0 - ADAM_B1 ** ADAM_STEP)
    v_hat = v2 / (1.0 - ADAM_B2 ** ADAM_STEP)
    delta = -ADAM_LR * (m_hat / (jnp.sqrt(v_hat) + ADAM_EPS) + ADAM_WD * w)
    return delta, m2, v2


def _adamw(w, g, m, v, name):
    rows, cols = w.shape
    tr = 256 if rows % 256 == 0 else rows

    def body(w_ref, g_ref, m_ref, v_ref, d_ref, m2_ref, v2_ref):
        d_ref[...], m2_ref[...], v2_ref[...] = _adam_math(w_ref[...], g_ref[...], m_ref[...], v_ref[...])

    spec = pl.BlockSpec((tr, cols), lambda i: (i, 0))
    return _pcall(
        body, name=name, grid=(rows // tr,), in_specs=[spec] * 4, out_specs=[spec] * 3,
        out_shape=[jax.ShapeDtypeStruct(w.shape, F32)] * 3,
        compiler_params=pltpu.CompilerParams(dimension_semantics=("arbitrary",)),
    )(w, g, m, v)


def _adamw_small(gparts, ws, ms, vs):
    n = len(SMALL)

    def body(g_ref, *refs):
        w_refs, m_refs, v_refs = refs[:n], refs[n:2 * n], refs[2 * n:3 * n]
        outs, sum_ref = refs[3 * n:-1], refs[-1]
        g = g_ref[0]
        for k in range(1, 8):
            g = g + g_ref[k]
        sum_ref[...] = g
        outs[0][...] = sum_ref[17:18, 0:128]
        for t, name in enumerate(SMALL):
            r0, nr, c0, nc = SMALL_SLOT[name]
            gt = sum_ref[r0:r0 + nr, c0:c0 + nc]
            out = (gt,) + _adam_math(w_refs[t][...], gt, m_refs[t][...], v_refs[t][...])
            for o_ref, val in zip(outs[1 + 4 * t:5 + 4 * t], out):
                o_ref[...] = val

    whole = lambda s: pl.BlockSpec(s, lambda i, nd=len(s): (0,) * nd)
    ins = [gparts] + list(ws) + list(ms) + list(vs)
    out_shapes = [(1, 128)] + [a.shape for a in ws for _ in range(4)]
    return _pcall(
        body, name="adamw_small", grid=(1,), in_specs=[whole(a.shape) for a in ins],
        out_specs=[whole(s) for s in out_shapes], out_shape=[jax.ShapeDtypeStruct(s, F32) for s in out_shapes],
        scratch_shapes=[pltpu.VMEM((SMALL_ROWS, D), F32)],
        compiler_params=pltpu.CompilerParams(dimension_semantics=("arbitrary",)),
    )(*ins)


def _row_tile(rows):
    return next(t for t in (512, 400, 320) if rows % t == 0)


def _add_halves(g4, theirs, core, name):
    rows = g4.shape[2]
    tr = _row_tile(rows)

    def body(c_ref, a_ref, b_ref, o_ref):
        o_ref[...] = (a_ref[...] + b_ref[...]).astype(BF16)

    grid_spec = pltpu.PrefetchScalarGridSpec(
        num_scalar_prefetch=1, grid=(4, rows // tr),
        in_specs=[pl.BlockSpec((None, None, tr, D), lambda j, i, c: (j, c[0], i, 0)),
                  pl.BlockSpec((None, None, tr, D), lambda j, i, c: (j, 0, i, 0))],
        out_specs=pl.BlockSpec((None, tr, D), lambda j, i, c: (j, i, 0)))
    return _pcall(
        body, name=name, grid_spec=grid_spec, out_shape=jax.ShapeDtypeStruct((4, rows, D), BF16),
        compiler_params=pltpu.CompilerParams(dimension_semantics=("arbitrary", "arbitrary")),
    )(core, g4, theirs)


def _sum_adam(own, got, order, r0, w, m, v, name, transposed=False):
    n = w.shape[1] if transposed else w.shape[0]
    tr = min(n, 256) if n % 8 == 0 else n
    rows = tr if n % 8 == 0 else own.shape[1]

    def body(o_ref, a_ref, b_ref, c_ref, d_ref, w_ref, m_ref, v_ref, g_ref, dl_ref, m2_ref, v2_ref):
        f = lambda r: r[0:tr, :].astype(F32)
        g = ((f(a_ref) + f(b_ref)) + f(c_ref)) + f(d_ref)
        g = g.T if transposed else g
        g_ref[...] = g
        dl_ref[...], m2_ref[...], v2_ref[...] = _adam_math(w_ref[...], g, m_ref[...], v_ref[...])

    slot = lambda k: pl.BlockSpec((None, rows, D), lambda i, o: (o[k], r0 // rows + i, 0))
    wspec = pl.BlockSpec((D, tr), lambda i, o: (0, i)) if transposed else pl.BlockSpec((tr, D), lambda i, o: (i, 0))
    grid_spec = pltpu.PrefetchScalarGridSpec(
        num_scalar_prefetch=1, grid=(n // tr,), in_specs=[slot(0), slot(1), slot(2), slot(3), wspec, wspec, wspec],
        out_specs=[wspec] * 4)
    return _pcall(
        body, name=name, grid_spec=grid_spec, out_shape=[jax.ShapeDtypeStruct(w.shape, F32)] * 4,
        compiler_params=pltpu.CompilerParams(dimension_semantics=("arbitrary",)),
    )(order, own, got, got, got, w, m, v)


def _sum_adam_rows(own, got, order, ws, ms, vs, name):
    n, rows = len(ws), ws[0].shape[0]

    def body(o_ref, a_ref, b_ref, c_ref, d_ref, *refs):
        ins, outs = refs[:3 * n], refs[3 * n:]
        for t in range(n):
            r = slice(t * rows, (t + 1) * rows)
            f = lambda ref: ref[r, :].astype(F32)
            g = ((f(a_ref) + f(b_ref)) + f(c_ref)) + f(d_ref)
            out = (g,) + _adam_math(ins[t][...], g, ins[n + t][...], ins[2 * n + t][...])
            for o, val in zip(outs[4 * t:4 * t + 4], out):
                o[...] = val

    slot = lambda k: pl.BlockSpec((None, n * rows, D), lambda i, o: (o[k], 0, 0), pipeline_mode=pl.Buffered(1))
    wspec = pl.BlockSpec((rows, D), lambda i, o: (0, 0), pipeline_mode=pl.Buffered(1))
    grid_spec = pltpu.PrefetchScalarGridSpec(
        num_scalar_prefetch=1, grid=(1,), in_specs=[slot(0), slot(1), slot(2), slot(3)] + [wspec] * (3 * n),
        out_specs=[pl.BlockSpec((rows, D), lambda i, o: (0, 0))] * (4 * n))
    return _pcall(
        body, name=name, grid_spec=grid_spec, out_shape=[jax.ShapeDtypeStruct((rows, D), F32)] * (4 * n),
        compiler_params=pltpu.CompilerParams(dimension_semantics=("arbitrary",), vmem_limit_bytes=VMEM_BIG),
    )(order, own, got, got, got, *ws, *ms, *vs)


def _sum_chips(own, got, order, name):
    rows = own.shape[1]
    tr = _row_tile(rows)

    def body(o_ref, a_ref, b_ref, c_ref, d_ref, out_ref):
        f = lambda r: r[...].astype(F32)
        out_ref[...] = ((f(a_ref) + f(b_ref)) + f(c_ref)) + f(d_ref)

    slot = lambda k: pl.BlockSpec((None, tr, D), lambda i, o: (o[k], i, 0))
    grid_spec = pltpu.PrefetchScalarGridSpec(
        num_scalar_prefetch=1, grid=(rows // tr,), in_specs=[slot(0), slot(1), slot(2), slot(3)],
        out_specs=pl.BlockSpec((tr, D), lambda i, o: (i, 0)))
    return _pcall(
        body, name=name, grid_spec=grid_spec, out_shape=jax.ShapeDtypeStruct((rows, D), F32),
        compiler_params=pltpu.CompilerParams(dimension_semantics=("arbitrary",)),
    )(order, own, got, got, got)


def _place():
    return lax.axis_index("x"), lax.axis_index("y"), lax.axis_index("c")


def _allgather(block, name):
    def body(x_ref, out_ref, token, send_sems, recv_sems, local_sem):
        token[...] = jnp.zeros_like(token)
        x, y, c = _place()
        me, sibling = (x, y, c), (x, y, 1 - c)
        chips = [(1 - x, y), (x, 1 - y), (1 - x, 1 - y)]

        def slot(px, py, pc):
            return out_ref.at[4 * px + 2 * py + pc]

        def copy(k, blk, to, src=None):
            return pltpu.make_async_remote_copy(
                src_ref=slot(*blk) if src is None else src, dst_ref=slot(*blk),
                send_sem=send_sems.at[k], recv_sem=recv_sems.at[k], device_id=to, device_id_type=MESH)

        mine = pltpu.make_async_copy(x_ref, slot(*me), local_sem)
        mine.start()
        first = [copy(0, me, sibling, src=x_ref)]
        first += [copy(1 + j, me, (*chip, c), src=x_ref) for j, chip in enumerate(chips)]
        for cp in first:
            cp.start()
        passed = [copy(4 + j, (*chip, c), sibling) for j, chip in enumerate(chips)]
        for j, chip in enumerate(chips):
            copy(1 + j, (*chip, c), me).wait_recv()
            passed[j].start()
        copy(0, sibling, me).wait_recv()
        for j, chip in enumerate(chips):
            copy(4 + j, (*chip, 1 - c), me).wait_recv()
        for cp in first + passed:
            cp.wait_send()
        mine.wait()

    return _pcall(
        body, name=name,
        out_shape=[jax.ShapeDtypeStruct((8,) + block.shape, block.dtype), jax.ShapeDtypeStruct((8, 128), F32)],
        in_specs=[pl.BlockSpec(memory_space=pl.ANY)],
        out_specs=[pl.BlockSpec(memory_space=pl.ANY), pl.BlockSpec(memory_space=pltpu.VMEM)],
        scratch_shapes=[pltpu.SemaphoreType.DMA((7,)), pltpu.SemaphoreType.DMA((7,)), pltpu.SemaphoreType.DMA(())],
        compiler_params=pltpu.CompilerParams(has_side_effects=True),
    )(block)


HBM_SPEC = pl.BlockSpec(memory_space=pltpu.HBM)
SEM_SPEC = pl.BlockSpec(memory_space=pltpu.SEMAPHORE)
ANY_SPEC = pl.BlockSpec(memory_space=pl.ANY)
EFFECT = pltpu.SideEffectType.DATAFLOW_SIDE_EFFECTING


def _in_hbm(a):
    return pltpu.with_memory_space_constraint(a, pltpu.HBM)


def _start_copies(name, src, land_shape, plan, n):
    def body(src_ref, land_ref, send_sems, recv_sems, src_thru, land_thru, token):
        for k, (s, d, to, _) in enumerate(plan(src_ref, land_ref)):
            pltpu.make_async_remote_copy(src_ref=s, dst_ref=d, send_sem=send_sems.at[k], recv_sem=recv_sems.at[k],
                                         device_id=to, device_id_type=MESH).start()
        token[...] = jnp.zeros_like(token)

    return _pcall(
        body, name=name,
        out_shape=(pltpu.SemaphoreType.DMA((n,)), pltpu.SemaphoreType.DMA((n,)), pltpu.HBM(src.shape, src.dtype),
                   pltpu.HBM(land_shape, src.dtype), jax.ShapeDtypeStruct((8, 128), F32)),
        in_specs=(HBM_SPEC, HBM_SPEC),
        out_specs=(SEM_SPEC, SEM_SPEC, HBM_SPEC, HBM_SPEC, pl.BlockSpec(memory_space=pltpu.VMEM)),
        input_output_aliases={0: 2, 1: 3}, compiler_params=pltpu.CompilerParams(has_side_effects=EFFECT),
    )(_in_hbm(src), _in_hbm(lax.empty(land_shape, src.dtype)))


def _wait_copies(name, started, after, plan):
    send_sems, recv_sems, src_thru, land_thru, _ = started

    def body(src_ref, land_ref, send_sems, recv_sems, *rest):
        for k, (s, _, to, mine) in enumerate(plan(src_ref, land_ref)):
            cp = pltpu.make_async_remote_copy(src_ref=s, dst_ref=mine, send_sem=send_sems.at[k],
                                              recv_sem=recv_sems.at[k], device_id=to, device_id_type=MESH)
            cp.wait_send()
            cp.wait_recv()

    return _pcall(
        body, name=name,
        out_shape=(pltpu.HBM(src_thru.shape, src_thru.dtype), pltpu.HBM(land_thru.shape, land_thru.dtype)),
        in_specs=(HBM_SPEC, HBM_SPEC, SEM_SPEC, SEM_SPEC) + (ANY_SPEC,) * len(after), out_specs=(HBM_SPEC, HBM_SPEC),
        input_output_aliases={0: 0, 1: 1}, compiler_params=pltpu.CompilerParams(has_side_effects=EFFECT),
    )(src_thru, land_thru, send_sems, recv_sems, *after)


def _gather_plan(src_ref, land_ref):
    x, y, c = _place()
    peers = [(x, y, 1 - c), (1 - x, y, c), (x, 1 - y, c), (1 - x, 1 - y, c)]
    return [(src_ref, land_ref.at[4 * x + 2 * y + c], p, land_ref.at[4 * p[0] + 2 * p[1] + p[2]]) for p in peers]


def _swap_plan(src_ref, land_ref):
    x, y, c = _place()
    return [(src_ref.at[:, pl.ds(1 - c, 1)], land_ref, (x, y, 1 - c), land_ref)]


def _exchange_plan(src_ref, land_ref):
    x, y, c = _place()
    chips = [(1 - x, y), (x, 1 - y), (1 - x, 1 - y)]
    return [(src_ref.at[2 * px + py], land_ref.at[2 * x + y], (px, py, c), land_ref.at[2 * px + py]) for px, py in chips]


def _gather_forward(land, block):
    def body(land_ref, out_ref, send_sems, recv_sems):
        x, y, c = _place()
        chips = [(1 - x, y), (x, 1 - y), (1 - x, 1 - y)]

        def copy(k, px, py, pc):
            blk = out_ref.at[4 * px + 2 * py + pc]
            return pltpu.make_async_remote_copy(src_ref=blk, dst_ref=blk, send_sem=send_sems.at[k],
                                                recv_sem=recv_sems.at[k], device_id=(x, y, 1 - c), device_id_type=MESH)

        sent = [copy(k, px, py, c) for k, (px, py) in enumerate(chips)]
        for cp in sent:
            cp.start()
        for k, (px, py) in enumerate(chips):
            copy(k, px, py, 1 - c).wait_recv()
        for cp in sent:
            cp.wait_send()

    land = _pcall(
        body, name="allgather_rest_forward", out_shape=jax.ShapeDtypeStruct(land.shape, land.dtype),
        in_specs=[ANY_SPEC], out_specs=ANY_SPEC, input_output_aliases={0: 0},
        scratch_shapes=[pltpu.SemaphoreType.DMA((3,)), pltpu.SemaphoreType.DMA((3,))],
        compiler_params=pltpu.CompilerParams(has_side_effects=True),
    )(land)

    rows = block.shape[0]
    tr = rows // 4

    def place(me_ref, x_ref, land_ref, out_ref):
        out_ref[...] = x_ref[...]

    x, y, c = _place()
    grid_spec = pltpu.PrefetchScalarGridSpec(
        num_scalar_prefetch=1, grid=(rows // tr,),
        in_specs=[pl.BlockSpec((tr, D), lambda i, me: (i, 0)), ANY_SPEC],
        out_specs=pl.BlockSpec((None, tr, D), lambda i, me: (me[0], i, 0)))
    return _pcall(
        place, name="allgather_rest_own", grid_spec=grid_spec, out_shape=jax.ShapeDtypeStruct(land.shape, land.dtype),
        input_output_aliases={2: 0}, compiler_params=pltpu.CompilerParams(dimension_semantics=("arbitrary",)),
    )((4 * x + 2 * y + c).reshape(1), block, land)


class _ReduceScatter:
    def __init__(self, name, g):
        self.name = name
        rows = g.shape[1]
        self.started = _start_copies(name + "_swap_start", g.reshape(4, 2, rows, D), (4, 1, rows, D), _swap_plan, 1)
        self.token = self.started[4][0, 0]

    def halfway(self, after):
        g4, theirs = _wait_copies(self.name + "_swap_wait", self.started, after, _swap_plan)
        self.own = _add_halves(g4, theirs, lax.axis_index("c").reshape(1), self.name + "_add_halves")
        self.started = _start_copies(self.name + "_exch_start", self.own, self.own.shape, _exchange_plan, 3)
        self.token = self.started[4][0, 0]

    def finish(self, after):
        own, got = _wait_copies(self.name + "_exch_wait", self.started, after, _exchange_plan)
        chip = 2 * lax.axis_index("x") + lax.axis_index("y")
        return own, got, (chip + jnp.arange(4, dtype=jnp.int32)) % 4


def _pack_small(p, scalar=None):
    z = lambda a, n: jnp.pad(a, ((0, 0), (0, n - a.shape[1])))
    rows = [z(p['rel_bias'], D), z(p['b_fgt'], D), jnp.concatenate([p['g_fox_out'], p['g_chk_out']], axis=1)]
    rows += [p[n] for n in ('g_mix_pre', 'g_mix_post', 'g_mem_kv', 'g_mem_pre', 'g_mem_post', 'g_ff_pre', 'g_ff_post')]
    rows.append(jnp.zeros((1, D), F32) if scalar is None else z(jnp.reshape(scalar, (1, 1)), D))
    rows.append(jnp.zeros((SMALL_ROWS - 18, D), F32))
    return jnp.concatenate(rows, axis=0)


_GAP_DEV, _GAP_ROW = divmod(GATE0 + 8, N_IN)
_GAP = CHK0 - GATE0 - 8


def _in_rows_to_proj(g):
    wt = g[:, :N_IN].reshape(8 * N_IN, D)
    return jnp.concatenate([wt[:GATE0], jnp.pad(wt[GATE0:GATE0 + 8], ((0, _GAP), (0, 0))), wt[GATE0 + 8:]], axis=0)


def _proj_rows_to_in(g):
    pad = lambda a: jnp.pad(a, ((0, R_IN - a.shape[0]), (0, 0)))
    lo = N_IN * _GAP_DEV
    shards = [pad(g[N_IN * j:N_IN * (j + 1)]) for j in range(_GAP_DEV)]
    shards.append(pad(jnp.concatenate([g[lo:lo + _GAP_ROW], g[lo + _GAP_ROW + _GAP:lo + N_IN + _GAP]], axis=0)))
    shards += [pad(g[N_IN * j + _GAP:N_IN * (j + 1) + _GAP]) for j in range(_GAP_DEV + 1, 8)]
    return jnp.stack(shards)


def _local_grads(x, mem, tgt, win_t, gw_of, sm, on_grads):
    b_pad = jnp.pad(sm['b_fgt'], ((0, 0), (0, 120)))
    tbl = jnp.pad(sm['rel_bias'], ((0, 0), (0, NREL_PAD - 257)))

    h1, proj, flog = _premix_fwd(x, sm['g_mix_pre'], win_t)
    c = _gate_fwd(flog, b_pad)
    ct3 = c[:, :8].T.reshape(4, 2, T)
    o_f, lse_f = _fox_fwd(proj, c, ct3)
    vt3 = _relvec_fwd(tbl).reshape(4, 2, VW)
    kvp = jnp.pad(proj[:, CHK0 + 512:], ((LEFT, 0), (0, 0)))
    o_c, lse_c = _chk_fwd(proj, kvp, vt3)
    gw = gw_of([o_f, o_c])
    w_out, w_mq, w_mk, w_mv, w_mo, w1_t, w2 = (_wblk(gw, n) for n in ('w_out', 'w_mq', 'w_mk', 'w_mv', 'w_mo', 'w_ff1', 'w_ff2'))
    ycat, z, x1, h2, qm = _postmix_fwd(x, o_f, o_c, sm['g_fox_out'], sm['g_chk_out'], w_out,
                                       sm['g_mix_post'], sm['g_mem_pre'], w_mq)
    memn, km, vm = _memkv_fwd(mem, sm['g_mem_kv'], w_mk, w_mv)
    om, ym, x2, h3 = _mem_fwd(qm, x1, km, vm, w_mo, sm['g_mem_post'], sm['g_ff_pre'])
    a, y3, dx3, loss_acc = _ffn_fwd(h3, x2, tgt, w1_t, w2, sm['g_ff_post'])

    gs = {}
    dx2, da, dy3, r, gs['g_ff_post'], gs['g_ff_pre'] = _ffn_bwd(dx3, y3, x2, a, w1_t, w2, sm['g_ff_post'], sm['g_ff_pre'])
    zero = on_grads('A', _wgrad_group("wgrad_ff", [(da, h3), (r, dy3)], 512), None)
    dx1, dym, dqm, dkm, dvm, gs['g_mem_post'], gs['g_mem_pre'] = _mem_bwd(
        dx2, ym, x1, qm, km, vm, w_mo, w_mq, sm['g_mem_post'] + zero, sm['g_mem_pre'])
    zero = on_grads('A halfway', None, [dx1])
    gs['g_mem_kv'] = _memkv_bwd(dkm, dvm, mem, w_mk, w_mv)
    dz, dof, doc, gs['g_mix_post'], gs['g_fox_out'], gs['g_chk_out'] = _postmix_bwd(
        dx1, z, o_f, o_c, w_out, sm['g_mix_post'] + zero, sm['g_fox_out'], sm['g_chk_out'])
    zero = on_grads('B', _wgrad_group("wgrad_mem_out", [(ycat, dz), (h2, dqm), (memn, dkm), (memn, dvm), (om, dym)], 128), None)
    dq_f, dk_f, dv_f, dct, dcq = _fox_bwd(proj, c, ct3 + zero, o_f, lse_f, dof)
    zero = on_grads('B halfway', None, [dq_f])
    dq_c, dk_c, dv_c, gv = _chk_bwd(proj, kvp, vt3 + zero, o_c, lse_c, doc)
    gs['rel_bias'] = _relvec_bwd(gv.reshape(8, VW))[:, :257]
    dc = jnp.pad(dct.reshape(8, T).T + dcq[:, :, :2].transpose(1, 0, 2).reshape(T, 8), ((0, 0), (0, 120)))
    dflog, db = _gate_bwd(dc, flog, b_pad)
    gs['b_fgt'] = db[0:1, :8]
    grad_x, dproj, gs['g_mix_pre'] = _premix_bwd(dx1, x, [dq_f, dk_f, dv_f, dflog, dq_c, dk_c, dv_c], win_t, sm['g_mix_pre'])
    on_grads('C', _proj_rows_to_in(_wgrad(dproj, h1, "wgrad_in")), None)
    return loss_acc[0, 0], grad_x, gs


def kernel(x, mem, w_in, b_fgt, rel_bias, g_fox_out, g_chk_out, w_out, g_mix_pre, g_mix_post, g_mem_kv, w_mq, w_mk, w_mv, w_mo, g_mem_pre, g_mem_post, w_ff1, w_ff2, g_ff_pre, g_ff_post, loss_target, m_w_in, m_b_fgt, m_rel_bias, m_g_fox_out, m_g_chk_out, m_w_out, m_g_mix_pre, m_g_mix_post, m_g_mem_kv, m_w_mq, m_w_mk, m_w_mv, m_w_mo, m_g_mem_pre, m_g_mem_post, m_w_ff1, m_w_ff2, m_g_ff_pre, m_g_ff_post, v_w_in, v_b_fgt, v_rel_bias, v_g_fox_out, v_g_chk_out, v_w_out, v_g_mix_pre, v_g_mix_post, v_g_mem_kv, v_w_mq, v_w_mk, v_w_mv, v_w_mo, v_g_mem_pre, v_g_mem_post, v_w_ff1, v_w_ff2, v_g_ff_pre, v_g_ff_post):
    args = dict(locals())
    two_d = lambda a: a.reshape(a.shape[-2:])
    w = {n: two_d(args[n]) for n in WEIGHTS}
    m = {n: two_d(args['m_' + n]) for n in WEIGHTS}
    v = {n: two_d(args['v_' + n]) for n in WEIGHTS}

    sm = {n: w[n] for n in SMALL}
    shard_in = jnp.pad(w['w_in'].T, ((0, R_IN - N_IN), (0, 0))).astype(BF16)
    gathered_in, zero = _allgather(shard_in, "allgather_w_in")
    win_t = _in_rows_to_proj(gathered_in)
    shard_rest = (jnp.concatenate([w['w_ff1'].T, w['w_ff2'], w['w_out'], w['w_mq'], w['w_mk'], w['w_mv'], w['w_mo']],
                                  axis=0) + zero[0, 0]).astype(BF16)
    rest = _start_copies("allgather_rest_start", shard_rest, (8, R_REST, D), _gather_plan, 4)
    sm['g_mix_pre'] = sm['g_mix_pre'] + rest[4][0, 0]

    def gw_of(after):
        block, land = _wait_copies("allgather_rest_wait", rest, after, _gather_plan)
        return _gather_forward(land, block)

    rs = {}

    def on_grads(stage, g, after):
        if stage.endswith('halfway'):
            rs[stage[0]].halfway(after)
            return rs[stage[0]].token
        rs[stage] = _ReduceScatter("rs_" + stage.lower(), g)
        return rs[stage].token

    loss_local, grad_x, gs = _local_grads(x[0], mem[0], loss_target[0], win_t, gw_of, sm, on_grads)
    grads, deltas, new_m, new_v = {}, {}, {}, {}

    def update(n, out):
        grads[n], deltas[n], new_m[n], new_v[n] = out

    gparts, _ = _allgather(_pack_small(gs, loss_local + rs['C'].token), "allgather_small_grads")
    small = _adamw_small(gparts, [w[n] for n in SMALL], [m[n] for n in SMALL], [v[n] for n in SMALL])
    loss = small[0][0, 0]
    for t, n in enumerate(SMALL):
        update(n, small[1 + 4 * t:5 + 4 * t])
    rs['C'].halfway([small[0]])

    own, got, order = rs['A'].finish([grad_x, rs['C'].started[4]])
    update('w_ff1', _sum_adam(own, got, order, 0, w['w_ff1'], m['w_ff1'], v['w_ff1'], "adamw_w_ff1", transposed=True))
    update('w_ff2', _sum_adam(own, got, order, 512, w['w_ff2'], m['w_ff2'], v['w_ff2'], "adamw_w_ff2"))
    own, got, order = rs['B'].finish([grad_x, rs['C'].started[4]])
    names_b = ('w_out', 'w_mq', 'w_mk', 'w_mv', 'w_mo')
    done = _sum_adam_rows(own, got, order, [w[n] for n in names_b], [m[n] for n in names_b], [v[n] for n in names_b],
                          "adamw_group_b")
    for k, n in enumerate(names_b):
        update(n, done[4 * k:4 * k + 4])

    own, got, order = rs['C'].finish([new_v[n] for n in BIG if n != 'w_in'])
    done = _sum_adam(own, got, order, 0, w['w_in'].T, m['w_in'].T, v['w_in'].T, "adamw_w_in")
    update('w_in', [a.T for a in done])

    out = [loss, grad_x[None]]
    for group in (grads, deltas, new_m, new_v):
        out += [group[n].reshape(args[n].shape) for n in WEIGHTS]
    return tuple(out)
```

```python
import functools

import jax
import jax.numpy as jnp
from jax import lax
from jax.experimental import pallas as pl
from jax.experimental.pallas import tpu as pltpu

F32 = jnp.float32
BF16 = jnp.bfloat16
MESH = pl.DeviceIdType.MESH

T = 2048
D = 1024
NMEM = 256
DFF = 4096
EPS = 1e-6
TM = 256
TM_WIDE = 512
TQ = 256
FQ = 512
HD = 64
SCALE = HD ** -0.5
MEM_HEADS = 4
MEM_HD = 256
MEM_SCALE = MEM_HD ** -0.5
NEG = -1e30
LEFT = 512
WIN = LEFT + TQ
VW = 1024
NREL_PAD = 384
PROJ = 3200
GATE0 = 1536
CHK0 = 1664
VMEM_BIG = 56 * 1024 * 1024

ADAM_LR = 0.001
ADAM_B1 = 0.9
ADAM_B2 = 0.999
ADAM_EPS = 1e-08
ADAM_WD = 0.01
ADAM_STEP = 10

N_IN = 385
R_IN = 400
R_REST = 1664
W_ROWS = {'w_ff1': (0, 512), 'w_ff2': (512, 512),
          'w_out': (1024, 128), 'w_mq': (1152, 128), 'w_mk': (1280, 128), 'w_mv': (1408, 128), 'w_mo': (1536, 128)}
R_A, R_B = 1024, 640
SMALL_ROWS = 24
SMALL_SLOT = {'rel_bias': (0, 8, 0, 257), 'b_fgt': (8, 1, 0, 8), 'g_fox_out': (9, 1, 0, 512), 'g_chk_out': (9, 1, 512, 512),
              'g_mix_pre': (10, 1, 0, 1024), 'g_mix_post': (11, 1, 0, 1024), 'g_mem_kv': (12, 1, 0, 1024),
              'g_mem_pre': (13, 1, 0, 1024), 'g_mem_post': (14, 1, 0, 1024), 'g_ff_pre': (15, 1, 0, 1024),
              'g_ff_post': (16, 1, 0, 1024)}

WEIGHTS = ['w_in', 'b_fgt', 'rel_bias', 'g_fox_out', 'g_chk_out', 'w_out', 'g_mix_pre', 'g_mix_post', 'g_mem_kv',
           'w_mq', 'w_mk', 'w_mv', 'w_mo', 'g_mem_pre', 'g_mem_post', 'w_ff1', 'w_ff2', 'g_ff_pre', 'g_ff_post']
BIG = ['w_in', 'w_out', 'w_mq', 'w_mk', 'w_mv', 'w_mo', 'w_ff1', 'w_ff2']
SMALL = [n for n in WEIGHTS if n not in BIG]


def _pcall(body, **kw):
    return pl.pallas_call(body, **kw)


def _nn(a, b):
    return jnp.dot(a, b, preferred_element_type=F32)


def _nt(a, b):
    return lax.dot_general(a, b, (((1,), (1,)), ((), ())), preferred_element_type=F32)


def _tn(a, b):
    return lax.dot_general(a, b, (((0,), (0,)), ((), ())), preferred_element_type=F32)


def _w(ref):
    v = ref[...]
    return v if v.ndim == 2 else v.reshape(-1, v.shape[-1])


def _rstd(x):
    return lax.rsqrt(jnp.mean(x * x, axis=-1, keepdims=True) + EPS)


def _rms(x, g):
    return x * _rstd(x) * g


def _rms_bwd(x, g, dy):
    r = _rstd(x)
    xh = x * r
    dg = jnp.sum(dy * xh, axis=0, keepdims=True)
    dxh = dy * g
    dx = r * (dxh - xh * jnp.mean(dxh * xh, axis=-1, keepdims=True))
    return dx, dg


def _resident(a):
    if isinstance(a, tuple):
        _, shape, index = a
        return pl.BlockSpec(shape, lambda *_: index, pipeline_mode=pl.Buffered(1))
    return pl.BlockSpec(a.shape, lambda *_, nd=a.ndim: (0,) * nd, pipeline_mode=pl.Buffered(1))


def _wblk(gw, name):
    r0, rows = W_ROWS[name]
    return (gw, (8, rows, D), (0, r0 // rows, 0))


def _tok_call(body, name, tiled, full, outs_tiled, outs_acc=(), rows=T, tm=TM, vmem=None):
    in_specs = [pl.BlockSpec((tm, a.shape[1]), lambda i: (i, 0)) for a in tiled]
    in_specs += [_resident(a) for a in full]
    full = [a[0] if isinstance(a, tuple) else a for a in full]
    out_shape = [jax.ShapeDtypeStruct((rows, c), dt) for c, dt in outs_tiled]
    out_shape += [jax.ShapeDtypeStruct(s, F32) for s in outs_acc]
    out_specs = [pl.BlockSpec((tm, c), lambda i: (i, 0)) for c, _ in outs_tiled]
    out_specs += [pl.BlockSpec(s, lambda i, nd=len(s): (0,) * nd) for s in outs_acc]
    return _pcall(
        body, name=name, grid=(rows // tm,), in_specs=in_specs, out_specs=out_specs, out_shape=out_shape,
        compiler_params=pltpu.CompilerParams(dimension_semantics=("arbitrary",), vmem_limit_bytes=vmem),
    )(*tiled, *full)


def _one_call(body, name, ins, outs, vmem=None):
    whole = lambda s: pl.BlockSpec(s, lambda i, nd=len(s): (0,) * nd)
    return _pcall(
        body, name=name, grid=(1,), in_specs=[_resident(a) for a in ins], out_specs=[whole(s) for s, _ in outs],
        out_shape=[jax.ShapeDtypeStruct(s, dt) for s, dt in outs],
        compiler_params=pltpu.CompilerParams(dimension_semantics=("arbitrary",), vmem_limit_bytes=vmem),
    )(*[a[0] if isinstance(a, tuple) else a for a in ins])


def _premix_fwd(x, g_pre, win_t):
    def body(x_ref, g_ref, w_ref, h_ref, proj_ref, flog_ref):
        h = _rms(x_ref[...], g_ref[...]).astype(BF16)
        h_ref[...] = h
        p = _nt(h, w_ref[...])
        proj_ref[...] = p.astype(BF16)
        flog_ref[...] = p[:, GATE0:GATE0 + 128]

    return _tok_call(body, "premix_fwd", [x], [g_pre, win_t],
                     [(D, BF16), (PROJ, BF16), (128, F32)], tm=TM_WIDE, vmem=VMEM_BIG)


def _postmix_fwd(x, o_f, o_c, g_fo, g_co, w_out, g_post, g_mpre, w_mq):
    def body(x_ref, of_ref, oc_ref, gfo_ref, gco_ref, wo_ref, gp_ref, gm_ref, wq_ref,
             y_ref, z_ref, x1_ref, h2_ref, qm_ref):
        y_ref[:, :512] = _rms(of_ref[...], gfo_ref[...]).astype(BF16)
        y_ref[:, 512:] = _rms(oc_ref[...], gco_ref[...]).astype(BF16)
        z = _nn(y_ref[...], _w(wo_ref))
        z_ref[...] = z
        x1 = x_ref[...] + _rms(z, gp_ref[...])
        x1_ref[...] = x1
        h2 = _rms(x1, gm_ref[...]).astype(BF16)
        h2_ref[...] = h2
        qm_ref[...] = _nn(h2, _w(wq_ref)).astype(BF16)

    return _tok_call(body, "postmix_fwd", [x, o_f, o_c], [g_fo, g_co, w_out, g_post, g_mpre, w_mq],
                     [(D, BF16), (D, F32), (D, F32), (D, BF16), (D, BF16)], tm=TM_WIDE, vmem=VMEM_BIG)


def _memkv_fwd(mem, g_kv, w_mk, w_mv):
    def body(m_ref, g_ref, wk_ref, wv_ref, mn_ref, k_ref, v_ref):
        mn = _rms(m_ref[...], g_ref[...]).astype(BF16)
        mn_ref[...] = mn
        k_ref[...] = _nn(mn, _w(wk_ref)).astype(BF16)
        v_ref[...] = _nn(mn, _w(wv_ref)).astype(BF16)

    return _tok_call(body, "memkv_fwd", [mem], [g_kv, w_mk, w_mv],
                     [(D, BF16), (D, BF16), (D, BF16)], rows=NMEM, tm=NMEM, vmem=VMEM_BIG)


def _mem_fwd(qm, x1, km, vm, w_mo, g_post, g_fpre):
    def body(q_ref, x1_ref, k_ref, v_ref, wo_ref, gp_ref, gf_ref, om_ref, ym_ref, x2_ref, h3_ref):
        for h in range(MEM_HEADS):
            sl = slice(h * MEM_HD, (h + 1) * MEM_HD)
            s = _nt(q_ref[:, sl], k_ref[:, sl]) * MEM_SCALE
            p = jnp.exp(s - jnp.max(s, axis=-1, keepdims=True))
            p = p / jnp.sum(p, axis=-1, keepdims=True)
            om_ref[:, sl] = _nn(p.astype(BF16), v_ref[:, sl]).astype(BF16)
        ym = _nn(om_ref[...], _w(wo_ref))
        ym_ref[...] = ym
        x2 = x1_ref[...] + _rms(ym, gp_ref[...])
        x2_ref[...] = x2
        h3_ref[...] = _rms(x2, gf_ref[...]).astype(BF16)

    return _tok_call(body, "mem_fwd", [qm, x1], [km, vm, w_mo, g_post, g_fpre],
                     [(D, BF16), (D, F32), (D, F32), (D, BF16)], tm=TM_WIDE, vmem=VMEM_BIG)


def _ffn_fwd(h3, x2, tgt, w1_t, w2, g_post):
    def body(h_ref, x2_ref, t_ref, w1_ref, w2_ref, g_ref, a_ref, y_ref, dx_ref, loss_ref):
        @pl.when(pl.program_id(0) == 0)
        def _():
            loss_ref[...] = jnp.zeros_like(loss_ref)

        h = h_ref[...]
        y = jnp.zeros((TM_WIDE, D), F32)
        for c in range(4):
            cols = slice(c * (DFF // 4), (c + 1) * (DFF // 4))
            a = _nt(h, w1_ref[2 * c:2 * c + 2].reshape(DFF // 4, D))
            a_ref[:, cols] = a.astype(BF16)
            y = y + _nn(jnp.square(jnp.maximum(a, 0.0)).astype(BF16), w2_ref[2 * c:2 * c + 2].reshape(DFF // 4, D))
        y_ref[...] = y
        e = x2_ref[...] + _rms(y, g_ref[...]) - t_ref[...]
        dx_ref[...] = e * (1.0 / D)
        loss_ref[...] += 0.5 * jnp.sum(jnp.sum(e * e, axis=-1, keepdims=True) * (1.0 / D))

    return _tok_call(body, "ffn_fwd", [h3, x2, tgt], [w1_t, w2, g_post],
                     [(DFF, BF16), (D, F32), (D, F32)], [(8, 128)], tm=TM_WIDE, vmem=VMEM_BIG)


def _tri(lower):
    r = lax.broadcasted_iota(jnp.int32, (128, 128), 0)
    c = lax.broadcasted_iota(jnp.int32, (128, 128), 1)
    return jnp.where(r >= c if lower else c >= r, 1.0, 0.0).astype(F32)


def _hdot(a, b):
    return jnp.dot(a, b, preferred_element_type=F32, precision=lax.Precision.HIGHEST)


def _gate_fwd(flog, b_pad):
    def body(f_ref, b_ref, c_ref):
        tri = _tri(True)

        def step(i, carry):
            rows = pl.ds(pl.multiple_of(i * 128, 128), 128)
            z = f_ref[rows, :] + b_ref[...]
            lf = jnp.minimum(z, 0.0) - jnp.log(1.0 + jnp.exp(-jnp.abs(z)))
            cb = _hdot(tri, lf) + carry
            c_ref[rows, :] = cb
            return cb[127:128, :]

        lax.fori_loop(0, T // 128, step, jnp.zeros((1, 128), F32))

    return _one_call(body, "gate_fwd", [flog, b_pad], [((T, 128), F32)])[0]


def _gate_bwd(dc, flog, b_pad):
    def body(dc_ref, f_ref, b_ref, df_ref, db_ref):
        tri = _tri(False)

        def step(j, carry):
            run, db = carry
            i = T // 128 - 1 - j
            rows = pl.ds(pl.multiple_of(i * 128, 128), 128)
            dcb = dc_ref[rows, :]
            rb = _hdot(tri, dcb) + run
            z = f_ref[rows, :] + b_ref[...]
            df = rb * (1.0 / (1.0 + jnp.exp(z)))
            df_ref[rows, :] = df.astype(BF16)
            return run + jnp.sum(dcb, axis=0, keepdims=True), db + jnp.sum(df, axis=0, keepdims=True)

        _, db = lax.fori_loop(0, T // 128, step, (jnp.zeros((1, 128), F32), jnp.zeros((1, 128), F32)))
        db_ref[...] = jnp.broadcast_to(db, (8, 128))

    return _one_call(body, "gate_bwd", [dc, flog, b_pad], [((T, 128), BF16), ((8, 128), F32)])


def _lane_lo(rows=TQ):
    return lax.broadcasted_iota(jnp.int32, (rows, 128), 1) < HD


def _half(v, lo, a, scale=None):
    keep = lo if a == 0 else jnp.logical_not(lo)
    v = v.astype(F32) if scale is None else v.astype(F32) * scale
    return jnp.where(keep, v, 0.0).astype(BF16)


def _fox_specs():
    return [pl.BlockSpec((FQ, 128), lambda h, i: (i, h)),
            pl.BlockSpec((T, 128), lambda h, i: (0, 4 + h)),
            pl.BlockSpec((T, 128), lambda h, i: (0, 8 + h))]


def _lane_pick(x, at):
    lane = lax.broadcasted_iota(jnp.int32, x.shape, 1)
    return jnp.sum(jnp.where(lane == at, x, 0.0), axis=-1, keepdims=True)


def _fox_fwd(proj, c, ct3):
    def body(q_ref, k_ref, v_ref, c_ref, ct_ref, o_ref, l_ref):
        i = pl.program_id(1)
        lo = _lane_lo(FQ)
        causal = lax.broadcasted_iota(jnp.int32, (FQ, FQ), 1) <= lax.broadcasted_iota(jnp.int32, (FQ, FQ), 0)
        q = q_ref[...]
        qs = [_half(q, lo, a, SCALE) for a in range(2)]
        cqs = [_lane_pick(c_ref[...], 2 * pl.program_id(0) + a) for a in range(2)]

        def tile(off, carry, diagonal):
            kblk = k_ref[pl.ds(off, FQ), :]
            vblk = v_ref[pl.ds(off, FQ), :]
            new = []
            for a in range(2):
                m, l, acc = carry[a]
                s = _nt(qs[a], kblk) + (cqs[a] - ct_ref[a:a + 1, pl.ds(off, FQ)])
                if diagonal:
                    s = jnp.where(causal, s, NEG)
                m2 = jnp.maximum(m, jnp.max(s, axis=-1, keepdims=True))
                p = jnp.exp(s - m2)
                alpha = jnp.exp(m - m2)
                new.append((m2, alpha * l + jnp.sum(p, axis=-1, keepdims=True),
                            alpha * acc + _nn(p.astype(BF16), vblk)))
            return tuple(new)

        init = (jnp.full((FQ, 1), NEG, F32), jnp.zeros((FQ, 1), F32), jnp.zeros((FQ, 128), F32))
        carry = lax.fori_loop(0, i, lambda kb, c: tile(pl.multiple_of(kb * FQ, FQ), c, False), (init, init))
        carry = tile(pl.multiple_of(i * FQ, FQ), carry, True)
        outs = []
        for a in range(2):
            m, l, acc = carry[a]
            outs.append(acc / l)
            l_ref[:, 128 * a:128 * a + 128] = jnp.broadcast_to(m + jnp.log(l), (FQ, 128))
        o_ref[...] = jnp.where(lo, outs[0], outs[1])

    return _pcall(
        body, name="fox_fwd", grid=(4, T // FQ),
        in_specs=_fox_specs() + [pl.BlockSpec((FQ, 128), lambda h, i: (i, 0)),
                                 pl.BlockSpec((None, 2, T), lambda h, i: (h, 0, 0))],
        out_specs=[pl.BlockSpec((FQ, 128), lambda h, i: (i, h)), pl.BlockSpec((FQ, 256), lambda h, i: (i, h))],
        out_shape=[jax.ShapeDtypeStruct((T, 512), F32), jax.ShapeDtypeStruct((T, 1024), F32)],
        compiler_params=pltpu.CompilerParams(dimension_semantics=("arbitrary", "arbitrary"), vmem_limit_bytes=VMEM_BIG),
    )(proj, proj, proj, c, ct3)


def _fox_bwd(proj, c, ct3, o, lse, do):
    def body(q_ref, k_ref, v_ref, c_ref, ct_ref, o_ref, l_ref, do_ref, dq_ref, dkb_ref, dvb_ref, dct_ref, dcq_ref,
             dk_ref, dv_ref):
        i = pl.program_id(1)

        @pl.when(i == 0)
        def _():
            dk_ref[...] = jnp.zeros_like(dk_ref)
            dv_ref[...] = jnp.zeros_like(dv_ref)
            dct_ref[...] = jnp.zeros_like(dct_ref)

        lo = _lane_lo(FQ)
        causal = lax.broadcasted_iota(jnp.int32, (FQ, FQ), 1) <= lax.broadcasted_iota(jnp.int32, (FQ, FQ), 0)
        q = q_ref[...]
        do_v = do_ref[...]
        prod = do_v * o_ref[...]
        qs = [_half(q, lo, a, SCALE) for a in range(2)]
        dos = [_half(do_v, lo, a) for a in range(2)]
        deltas = [jnp.sum(jnp.where(lo if a == 0 else jnp.logical_not(lo), prod, 0.0), axis=-1, keepdims=True)
                  for a in range(2)]
        cqs = [_lane_pick(c_ref[...], 2 * pl.program_id(0) + a) for a in range(2)]
        las = [l_ref[:, 128 * a:128 * a + 1] for a in range(2)]

        def tile(off, carry, diagonal):
            kblk = k_ref[pl.ds(off, FQ), :]
            vblk = v_ref[pl.ds(off, FQ), :]
            new = []
            dk = jnp.zeros((FQ, 128), F32)
            dv = jnp.zeros((FQ, 128), F32)
            for a in range(2):
                dq_acc, rs = carry[a]
                s = _nt(qs[a], kblk) + (cqs[a] - ct_ref[a:a + 1, pl.ds(off, FQ)])
                if diagonal:
                    s = jnp.where(causal, s, NEG)
                p = jnp.exp(s - las[a])
                ds = p * (_nt(dos[a], vblk) - deltas[a])
                dsb = ds.astype(BF16)
                dk = dk + _tn(dsb, qs[a])
                dv = dv + _tn(p.astype(BF16), dos[a])
                dct_ref[a:a + 1, pl.ds(off, FQ)] -= jnp.sum(ds, axis=0, keepdims=True)
                new.append((dq_acc + _nn(dsb, kblk), rs + jnp.sum(ds, axis=-1, keepdims=True)))
            dk_ref[pl.ds(off, FQ), :] += dk
            dv_ref[pl.ds(off, FQ), :] += dv
            return tuple(new)

        init = (jnp.zeros((FQ, 128), F32), jnp.zeros((FQ, 1), F32))
        carry = lax.fori_loop(0, i, lambda kb, c: tile(pl.multiple_of(kb * FQ, FQ), c, False), (init, init))
        carry = tile(pl.multiple_of(i * FQ, FQ), carry, True)
        lane = lax.broadcasted_iota(jnp.int32, (FQ, 128), 1)
        dcq_ref[...] = jnp.where(lane == 0, carry[0][1], jnp.where(lane == 1, carry[1][1], 0.0))
        dq_ref[...] = (jnp.where(lo, carry[0][0], carry[1][0]) * SCALE).astype(BF16)

        @pl.when(i == T // FQ - 1)
        def _():
            dkb_ref[...] = dk_ref[...].astype(BF16)
            dvb_ref[...] = dv_ref[...].astype(BF16)

    blk = pl.BlockSpec((FQ, 128), lambda h, i: (i, h))
    wide = pl.BlockSpec((FQ, 256), lambda h, i: (i, h))
    rows = pl.BlockSpec((None, 2, T), lambda h, i: (h, 0, 0))
    col = pl.BlockSpec((T, 128), lambda h, i: (0, h))
    return _pcall(
        body, name="fox_bwd", grid=(4, T // FQ),
        in_specs=_fox_specs() + [pl.BlockSpec((FQ, 128), lambda h, i: (i, 0)), rows, blk, wide, blk],
        out_specs=[blk, col, col, rows, pl.BlockSpec((None, FQ, 128), lambda h, i: (h, i, 0))],
        out_shape=[jax.ShapeDtypeStruct((T, 512), BF16), jax.ShapeDtypeStruct((T, 512), BF16),
                   jax.ShapeDtypeStruct((T, 512), BF16), jax.ShapeDtypeStruct((4, 2, T), F32),
                   jax.ShapeDtypeStruct((4, T, 128), F32)],
        scratch_shapes=[pltpu.VMEM((T, 128), F32), pltpu.VMEM((T, 128), F32)],
        compiler_params=pltpu.CompilerParams(dimension_semantics=("arbitrary", "arbitrary"), vmem_limit_bytes=VMEM_BIG),
    )(proj, proj, proj, c, ct3, o, lse, do)


AUG_ROWSUM, AUG_COLSUM = 67, 64


def _fox_prep(proj, c2):
    def body(q_ref, k_ref, c_ref, qa_ref, ka_ref):
        lane = lax.broadcasted_iota(jnp.int32, (FQ, 128), 1)
        q = q_ref[...].astype(F32) * SCALE
        k = k_ref[...].astype(F32)
        for a in range(2):
            c = c_ref[:, 128 * a:128 * a + 128]
            hi = c.astype(BF16).astype(F32)
            mid = (c - hi).astype(BF16).astype(F32)
            lo = c - hi - mid
            parts = lambda first, sign: jnp.where(lane == first, sign * hi, jnp.where(lane == first + 1, sign * mid, sign * lo))
            qd = q if a == 0 else pltpu.roll(q, 64, 1)
            kd = k if a == 0 else pltpu.roll(k, 64, 1)
            ones = jnp.ones((FQ, 128), F32)
            qa = jnp.where(lane < 64, qd, jnp.where(lane < 67, ones, jnp.where(lane < 70, parts(67, 1.0), 0.0)))
            ka = jnp.where(lane < 64, kd, jnp.where(lane < 67, parts(64, -1.0), jnp.where(lane < 70, ones, 0.0)))
            qa_ref[:, 128 * a:128 * a + 128] = qa.astype(BF16)
            ka_ref[:, 128 * a:128 * a + 128] = ka.astype(BF16)

    wide = pl.BlockSpec((FQ, 256), lambda h, i: (i, h))
    return _pcall(
        body, name="fox_prep", grid=(4, T // FQ),
        in_specs=[pl.BlockSpec((FQ, 128), lambda h, i: (i, h)), pl.BlockSpec((FQ, 128), lambda h, i: (i, 4 + h)), wide],
        out_specs=[wide, wide], out_shape=[jax.ShapeDtypeStruct((T, 1024), BF16)] * 2,
        compiler_params=pltpu.CompilerParams(dimension_semantics=("arbitrary", "arbitrary")),
    )(proj, proj, c2)


def _fox_aug_specs():
    return [pl.BlockSpec((FQ, 256), lambda h, i: (i, h)), pl.BlockSpec((T, 256), lambda h, i: (0, h)),
            pl.BlockSpec((T, 128), lambda h, i: (0, 8 + h))]


def _causal():
    return lax.broadcasted_iota(jnp.int32, (FQ, FQ), 1) <= lax.broadcasted_iota(jnp.int32, (FQ, FQ), 0)


def _fox_fwd_aug(q_aug, k_aug, proj):
    def body(q_ref, k_ref, v_ref, o_ref, l_ref):
        i = pl.program_id(1)
        lo = _lane_lo(FQ)
        causal = _causal()
        qs = [q_ref[:, 128 * a:128 * a + 128] for a in range(2)]

        def tile(off, carry, diagonal):
            vblk = v_ref[pl.ds(off, FQ), :]
            new = []
            for a in range(2):
                m, l, acc = carry[a]
                s = _nt(qs[a], k_ref[pl.ds(off, FQ), 128 * a:128 * a + 128])
                if diagonal:
                    s = jnp.where(causal, s, NEG)
                m2 = jnp.maximum(m, jnp.max(s, axis=-1, keepdims=True))
                p = jnp.exp(s - m2)
                alpha = jnp.exp(m - m2)
                new.append((m2, alpha * l + jnp.sum(p, axis=-1, keepdims=True),
                            alpha * acc + _nn(p.astype(BF16), vblk)))
            return tuple(new)

        init = (jnp.full((FQ, 1), NEG, F32), jnp.zeros((FQ, 1), F32), jnp.zeros((FQ, 128), F32))
        carry = lax.fori_loop(0, i, lambda kb, c: tile(pl.multiple_of(kb * FQ, FQ), c, False), (init, init))
        carry = tile(pl.multiple_of(i * FQ, FQ), carry, True)
        outs = []
        for a in range(2):
            m, l, acc = carry[a]
            outs.append(acc / l)
            l_ref[:, 128 * a:128 * a + 128] = jnp.broadcast_to(m + jnp.log(l), (FQ, 128))
        o_ref[...] = jnp.where(lo, outs[0], outs[1])

    return _pcall(
        body, name="fox_fwd", grid=(4, T // FQ), in_specs=_fox_aug_specs(),
        out_specs=[pl.BlockSpec((FQ, 128), lambda h, i: (i, h)), pl.BlockSpec((FQ, 256), lambda h, i: (i, h))],
        out_shape=[jax.ShapeDtypeStruct((T, 512), F32), jax.ShapeDtypeStruct((T, 1024), F32)],
        compiler_params=pltpu.CompilerParams(dimension_semantics=("arbitrary", "arbitrary"), vmem_limit_bytes=VMEM_BIG),
    )(q_aug, k_aug, proj)


def _fox_bwd_aug(q_aug, k_aug, proj, o, lse, do, after):
    nq = T // FQ

    def body(q_ref, k_ref, v_ref, o_ref, l_ref, do_ref, after_ref, dq_ref, dkb_ref, dvb_ref, dcq_ref, dck_ref, dk_ref,
             dv_ref):
        i = pl.program_id(1)

        @pl.when(i == 0)
        def _():
            dk_ref[...] = jnp.zeros_like(dk_ref)
            dv_ref[...] = jnp.zeros_like(dv_ref)

        lo = _lane_lo(FQ)
        lane = lax.broadcasted_iota(jnp.int32, (FQ, 128), 1)
        causal = _causal()
        do_v = do_ref[...]
        prod = do_v * o_ref[...]
        qs = [q_ref[:, 128 * a:128 * a + 128] for a in range(2)]
        dos = [_half(do_v, lo, a) for a in range(2)]
        deltas = [jnp.sum(jnp.where(lo if a == 0 else jnp.logical_not(lo), prod, 0.0), axis=-1, keepdims=True)
                  for a in range(2)]
        las = [l_ref[:, 128 * a:128 * a + 1] for a in range(2)]

        def tile(off, carry, diagonal):
            vblk = v_ref[pl.ds(off, FQ), :]
            new = []
            dv = jnp.zeros((FQ, 128), F32)
            for a in range(2):
                kblk = k_ref[pl.ds(off, FQ), 128 * a:128 * a + 128]
                s = _nt(qs[a], kblk)
                if diagonal:
                    s = jnp.where(causal, s, NEG)
                p = jnp.exp(s - las[a])
                dsb = (p * (_nt(dos[a], vblk) - deltas[a])).astype(BF16)
                dk_ref[a, pl.ds(off, FQ), :] += _tn(dsb, qs[a])
                dv = dv + _tn(p.astype(BF16), dos[a])
                new.append(carry[a] + _nn(dsb, kblk))
            dv_ref[pl.ds(off, FQ), :] += dv
            return tuple(new)

        init = jnp.zeros((FQ, 128), F32)
        dqs = lax.fori_loop(0, i, lambda kb, c: tile(pl.multiple_of(kb * FQ, FQ), c, False), (init, init))
        dqs = tile(pl.multiple_of(i * FQ, FQ), dqs, True)
        pick = lambda x, at: jnp.sum(jnp.where(lane == at, x, 0.0), axis=-1, keepdims=True)
        for a in range(2):
            dcq_ref[:, 128 * a:128 * a + 128] = jnp.broadcast_to(pick(dqs[a], AUG_ROWSUM), (FQ, 128))
        dq_ref[...] = (jnp.where(lo, dqs[0], pltpu.roll(dqs[1], 64, 1)) * SCALE).astype(BF16)

        @pl.when(i == nq - 1)
        def _():
            big_lane = lax.broadcasted_iota(jnp.int32, (T, 128), 1)
            dkb_ref[...] = jnp.where(big_lane < 64, dk_ref[0], pltpu.roll(dk_ref[1], 64, 1)).astype(BF16)
            dvb_ref[...] = dv_ref[...].astype(BF16)
            for a in range(2):
                col = jnp.sum(jnp.where(big_lane == AUG_COLSUM, dk_ref[a], 0.0), axis=-1, keepdims=True)
                dck_ref[:, 128 * a:128 * a + 128] = jnp.broadcast_to(-col, (T, 128))

    blk = pl.BlockSpec((FQ, 128), lambda h, i: (i, h))
    wide = pl.BlockSpec((FQ, 256), lambda h, i: (i, h))
    col = pl.BlockSpec((T, 128), lambda h, i: (0, h))
    colwide = pl.BlockSpec((T, 256), lambda h, i: (0, h))
    return _pcall(
        body, name="fox_bwd", grid=(4, nq), in_specs=_fox_aug_specs() + [blk, wide, blk, ANY_SPEC],
        out_specs=[blk, col, col, wide, colwide],
        out_shape=[jax.ShapeDtypeStruct((T, 512), BF16)] * 3 + [jax.ShapeDtypeStruct((T, 1024), F32)] * 2,
        scratch_shapes=[pltpu.VMEM((2, T, 128), F32), pltpu.VMEM((T, 128), F32)],
        compiler_params=pltpu.CompilerParams(dimension_semantics=("arbitrary", "arbitrary"), vmem_limit_bytes=VMEM_BIG),
    )(q_aug, k_aug, proj, o, lse, do, after)


def _rel_onehot():
    ridx = lax.broadcasted_iota(jnp.int32, (NREL_PAD, VW), 0)
    j = lax.broadcasted_iota(jnp.int32, (NREL_PAD, VW), 1)
    return jnp.where(ridx == jnp.clip(TQ + LEFT - 1 - j, -128, 128) + 128, 1.0, 0.0).astype(F32)


def _relvec_fwd(tbl):
    def body(t_ref, v_ref):
        v_ref[...] = _hdot(t_ref[...], _rel_onehot())

    return _one_call(body, "relvec_fwd", [tbl], [((8, VW), F32)])[0]


def _relvec_bwd(gv):
    def body(g_ref, t_ref):
        t_ref[...] = lax.dot_general(g_ref[...], _rel_onehot(), (((1,), (1,)), ((), ())),
                                     preferred_element_type=F32, precision=lax.Precision.HIGHEST)

    return _one_call(body, "relvec_bwd", [gv], [((8, NREL_PAD), F32)])[0]


def _chk_bias(vt_ref, a, hidden):
    vb = jnp.broadcast_to(vt_ref[a:a + 1, :], (TQ, VW))
    y = pltpu.roll(vb, VW - (TQ - 1), 1, stride=1, stride_axis=0)[:, :WIN]
    cr = lax.broadcasted_iota(jnp.int32, (TQ, WIN), 0) // 64
    m = lax.broadcasted_iota(jnp.int32, (TQ, WIN), 1)
    return jnp.where((m // 64 >= cr) & (m // 64 <= cr + 8) & (m >= hidden), y, NEG)


def _chk_specs():
    return [pl.BlockSpec((TQ, 128), lambda h, i: (i, CHK0 // 128 + h)),
            pl.BlockSpec((T + LEFT, 128), lambda h, i: (0, h)),
            pl.BlockSpec((T + LEFT, 128), lambda h, i: (0, 4 + h)),
            pl.BlockSpec((None, 2, VW), lambda h, i: (h, 0, 0))]


def _chk_fwd(proj, kvp, vt3):
    def body(q_ref, k_ref, v_ref, vt_ref, o_ref, l_ref, bias_ref):
        i = pl.program_id(1)

        @pl.when(i == 0)
        def _():
            for first in range(3):
                for a in range(2):
                    bias_ref[first, a] = _chk_bias(vt_ref, a, max(LEFT - first * TQ, 0))

        lo = _lane_lo()
        off = pl.multiple_of(i * TQ, TQ)
        kw = k_ref[pl.ds(off, WIN), :]
        vw = v_ref[pl.ds(off, WIN), :]
        bias_at = jnp.minimum(i, 2)
        q = q_ref[...]
        outs = []
        for a in range(2):
            s = _nt(_half(q, lo, a, SCALE), kw) + bias_ref[bias_at, a]
            m = jnp.max(s, axis=-1, keepdims=True)
            p = jnp.exp(s - m)
            l = jnp.sum(p, axis=-1, keepdims=True)
            outs.append(_nn(p.astype(BF16), vw) / l)
            l_ref[:, 128 * a:128 * a + 128] = jnp.broadcast_to(m + jnp.log(l), (TQ, 128))
        o_ref[...] = jnp.where(lo, outs[0], outs[1])

    return _pcall(
        body, name="chk_fwd", grid=(4, T // TQ), in_specs=_chk_specs(),
        out_specs=[pl.BlockSpec((TQ, 128), lambda h, i: (i, h)), pl.BlockSpec((TQ, 256), lambda h, i: (i, h))],
        out_shape=[jax.ShapeDtypeStruct((T, 512), F32), jax.ShapeDtypeStruct((T, 1024), F32)],
        scratch_shapes=[pltpu.VMEM((3, 2, TQ, WIN), F32)],
        compiler_params=pltpu.CompilerParams(dimension_semantics=("arbitrary", "arbitrary")),
    )(proj, kvp, kvp, vt3)


def _chk_bwd(proj, kvp, vt3, o, lse, do):
    nq = T // TQ

    def body(q_ref, k_ref, v_ref, vt_ref, o_ref, l_ref, do_ref, dq_ref, dkb_ref, dvb_ref, gv_ref, bias_ref, dsum_ref,
             dk_ref, dv_ref):
        i = pl.program_id(1)

        @pl.when(i == 0)
        def _():
            for first in range(3):
                for a in range(2):
                    bias_ref[first, a] = _chk_bias(vt_ref, a, max(LEFT - first * TQ, 0))
            dsum_ref[...] = jnp.zeros_like(dsum_ref)
            dk_ref[...] = jnp.zeros_like(dk_ref)
            dv_ref[...] = jnp.zeros_like(dv_ref)

        lo = _lane_lo()
        off = pl.multiple_of(i * TQ, TQ)
        kw = k_ref[pl.ds(off, WIN), :]
        vw = v_ref[pl.ds(off, WIN), :]
        bias_at = jnp.minimum(i, 2)
        q = q_ref[...]
        do_v = do_ref[...]
        prod = do_v * o_ref[...]
        dqs = []
        for a in range(2):
            keep = lo if a == 0 else jnp.logical_not(lo)
            qa = _half(q, lo, a, SCALE)
            doa = _half(do_v, lo, a)
            delta = jnp.sum(jnp.where(keep, prod, 0.0), axis=-1, keepdims=True)
            s = _nt(qa, kw) + bias_ref[bias_at, a]
            p = jnp.exp(s - l_ref[:, 128 * a:128 * a + 1])
            ds = p * (_nt(doa, vw) - delta)
            dsum_ref[a] += ds
            dsb = ds.astype(BF16)
            dk_ref[pl.ds(off, WIN), :] += _tn(dsb, qa)
            dv_ref[pl.ds(off, WIN), :] += _tn(p.astype(BF16), doa)
            dqs.append(_nn(dsb, kw))
        dq_ref[...] = (jnp.where(lo, dqs[0], dqs[1]) * SCALE).astype(BF16)

        @pl.when(i == nq - 1)
        def _():
            dkb_ref[...] = dk_ref[LEFT:, :].astype(BF16)
            dvb_ref[...] = dv_ref[LEFT:, :].astype(BF16)
            rr = lax.broadcasted_iota(jnp.int32, (TQ, TQ), 0)
            cc = lax.broadcasted_iota(jnp.int32, (TQ, TQ), 1)
            flip = jnp.where(rr + cc == TQ - 1, 1.0, 0.0).astype(F32)
            for a in range(2):
                dpad = jnp.concatenate([dsum_ref[a], jnp.zeros((TQ, VW - WIN), F32)], axis=1)
                z = pltpu.roll(_hdot(flip, dpad), 0, 1, stride=1, stride_axis=0)
                gv_ref[a:a + 1, :] = jnp.sum(z, axis=0, keepdims=True)

    blk = pl.BlockSpec((TQ, 128), lambda h, i: (i, h))
    wide = pl.BlockSpec((TQ, 256), lambda h, i: (i, h))
    col = pl.BlockSpec((T, 128), lambda h, i: (0, h))
    return _pcall(
        body, name="chk_bwd", grid=(4, nq), in_specs=_chk_specs() + [blk, wide, blk],
        out_specs=[blk, col, col, pl.BlockSpec((None, 2, VW), lambda h, i: (h, 0, 0))],
        out_shape=[jax.ShapeDtypeStruct((T, 512), BF16), jax.ShapeDtypeStruct((T, 512), BF16),
                   jax.ShapeDtypeStruct((T, 512), BF16), jax.ShapeDtypeStruct((4, 2, VW), F32)],
        scratch_shapes=[pltpu.VMEM((3, 2, TQ, WIN), F32), pltpu.VMEM((2, TQ, WIN), F32),
                        pltpu.VMEM((T + LEFT, 128), F32), pltpu.VMEM((T + LEFT, 128), F32)],
        compiler_params=pltpu.CompilerParams(dimension_semantics=("arbitrary", "arbitrary")),
    )(proj, kvp, kvp, vt3, o, lse, do)


def _zero_at_start(*refs):
    @pl.when(pl.program_id(0) == 0)
    def _():
        for r in refs:
            r[...] = jnp.zeros_like(r)


def _ffn_bwd(dx3, y3, x2, a, w1_t, w2, g_post, g_pre):
    def body(dx3_ref, y_ref, x2_ref, a_ref, w1_ref, w2_ref, gp_ref, gf_ref,
             dx2_ref, da_ref, dy_ref, r_ref, dgp_ref, dgf_ref):
        _zero_at_start(dgp_ref, dgf_ref)
        dx3_v = dx3_ref[...]
        dy, dgp = _rms_bwd(y_ref[...], gp_ref[...], dx3_v)
        dgp_ref[...] += dgp
        dyb = dy.astype(BF16)
        dy_ref[...] = dyb
        ra = jnp.maximum(a_ref[...].astype(F32), 0.0)
        r_ref[...] = jnp.square(ra).astype(BF16)
        da = (_nt(dyb, _w(w2_ref)) * (2.0 * ra)).astype(BF16)
        da_ref[...] = da
        dh, dgf = _rms_bwd(x2_ref[...], gf_ref[...], _nn(da, _w(w1_ref)))
        dgf_ref[...] += dgf
        dx2_ref[...] = dx3_v + dh

    return _tok_call(body, "ffn_bwd", [dx3, y3, x2, a], [w1_t, w2, g_post, g_pre],
                     [(D, F32), (DFF, BF16), (D, BF16), (DFF, BF16)], [(1, D), (1, D)], vmem=VMEM_BIG)


def _mem_bwd(dx2, ym, x1, qm, km, vm, w_mo, w_mq, g_post, g_pre):
    def body(dx2_ref, ym_ref, x1_ref, q_ref, k_ref, v_ref, wo_ref, wq_ref, gp_ref, gm_ref,
             dx1_ref, dym_ref, dq_ref, dk_ref, dv_ref, dgp_ref, dgm_ref, dom_ref):
        _zero_at_start(dk_ref, dv_ref, dgp_ref, dgm_ref)
        dx2_v = dx2_ref[...]
        dym, dgp = _rms_bwd(ym_ref[...], gp_ref[...], dx2_v)
        dgp_ref[...] += dgp
        dymb = dym.astype(BF16)
        dym_ref[...] = dymb
        dom_ref[...] = _nt(dymb, _w(wo_ref)).astype(BF16)
        for h in range(MEM_HEADS):
            sl = slice(h * MEM_HD, (h + 1) * MEM_HD)
            qh, kh, doh = q_ref[:, sl], k_ref[:, sl], dom_ref[:, sl]
            s = _nt(qh, kh) * MEM_SCALE
            p = jnp.exp(s - jnp.max(s, axis=-1, keepdims=True))
            p = p / jnp.sum(p, axis=-1, keepdims=True)
            dp = _nt(doh, v_ref[:, sl])
            ds = (p * (dp - jnp.sum(p * dp, axis=-1, keepdims=True))).astype(BF16)
            dq_ref[:, sl] = (_nn(ds, kh) * MEM_SCALE).astype(BF16)
            dk_ref[:, sl] += _tn(ds, qh) * MEM_SCALE
            dv_ref[:, sl] += _tn(p.astype(BF16), doh)
        dh, dgm = _rms_bwd(x1_ref[...], gm_ref[...], _nt(dq_ref[...], _w(wq_ref)))
        dgm_ref[...] += dgm
        dx1_ref[...] = dx2_v + dh

    tiled = pl.BlockSpec((TM_WIDE, D), lambda i: (i, 0))
    in_specs = [tiled] * 4 + [_resident(a) for a in (km, vm, w_mo, w_mq, g_post, g_pre)]
    w_mo, w_mq = w_mo[0], w_mq[0]
    kv = pl.BlockSpec((NMEM, D), lambda i: (0, 0))
    vec = pl.BlockSpec((1, D), lambda i: (0, 0))
    return _pcall(
        body, name="mem_bwd", grid=(T // TM_WIDE,), in_specs=in_specs,
        out_specs=[tiled, tiled, tiled, kv, kv, vec, vec],
        out_shape=[jax.ShapeDtypeStruct((T, D), F32), jax.ShapeDtypeStruct((T, D), BF16),
                   jax.ShapeDtypeStruct((T, D), BF16), jax.ShapeDtypeStruct((NMEM, D), F32),
                   jax.ShapeDtypeStruct((NMEM, D), F32), jax.ShapeDtypeStruct((1, D), F32),
                   jax.ShapeDtypeStruct((1, D), F32)],
        scratch_shapes=[pltpu.VMEM((TM_WIDE, D), BF16)],
        compiler_params=pltpu.CompilerParams(dimension_semantics=("arbitrary",), vmem_limit_bytes=VMEM_BIG),
    )(dx2, ym, x1, qm, km, vm, w_mo, w_mq, g_post, g_pre)


def _memkv_bwd(dkm, dvm, mem, w_mk, w_mv):
    def body(dk_ref, dv_ref, m_ref, wk_ref, wv_ref, dg_ref):
        dmn = _nt(dk_ref[...].astype(BF16), _w(wk_ref)) + _nt(dv_ref[...].astype(BF16), _w(wv_ref))
        mv = m_ref[...]
        dg_ref[...] = jnp.sum(dmn * (mv * _rstd(mv)), axis=0, keepdims=True)

    return _one_call(body, "memkv_bwd", [dkm, dvm, mem, w_mk, w_mv], [((1, D), F32)], vmem=VMEM_BIG)[0]


def _postmix_bwd(dx1, z, o_f, o_c, w_out, g_post, g_fo, g_co):
    def body(dx1_ref, z_ref, of_ref, oc_ref, wo_ref, gp_ref, gfo_ref, gco_ref,
             dz_ref, dof_ref, doc_ref, dgp_ref, dgfo_ref, dgco_ref):
        _zero_at_start(dgp_ref, dgfo_ref, dgco_ref)
        dz, dgp = _rms_bwd(z_ref[...], gp_ref[...], dx1_ref[...])
        dgp_ref[...] += dgp
        dzb = dz.astype(BF16)
        dz_ref[...] = dzb
        dy = _nt(dzb, _w(wo_ref))
        dof, dgfo = _rms_bwd(of_ref[...], gfo_ref[...], dy[:, :512])
        doc, dgco = _rms_bwd(oc_ref[...], gco_ref[...], dy[:, 512:])
        dof_ref[...] = dof
        doc_ref[...] = doc
        dgfo_ref[...] += dgfo
        dgco_ref[...] += dgco

    return _tok_call(body, "postmix_bwd", [dx1, z, o_f, o_c], [w_out, g_post, g_fo, g_co],
                     [(D, BF16), (512, F32), (512, F32)], [(1, D), (1, 512), (1, 512)], tm=TM_WIDE, vmem=VMEM_BIG)


def _premix_bwd(dx1, x, pieces, win_t, g_pre):
    def body(dx1_ref, x_ref, *refs):
        piece_refs, (w_ref, g_ref, dx_ref, dp_ref, dg_ref) = refs[:len(pieces)], refs[len(pieces):]
        _zero_at_start(dg_ref)
        col = 0
        for p in piece_refs:
            dp_ref[:, col:col + p.shape[1]] = p[...]
            col += p.shape[1]
        dh, dg = _rms_bwd(x_ref[...], g_ref[...], _nn(dp_ref[...], w_ref[...]))
        dg_ref[...] += dg
        dx_ref[...] = dx1_ref[...] + dh

    return _tok_call(body, "premix_bwd", [dx1, x] + list(pieces), [win_t, g_pre], [(D, F32), (PROJ, BF16)], [(1, D)],
                     tm=TM_WIDE, vmem=VMEM_BIG)


def _wgrad(a, b, name):
    k, m = a.shape
    n = b.shape[1]
    tm = 640 if m % 640 == 0 and m > 1024 else min(m, 512)
    tn = min(n, 1024)

    def body(a_ref, b_ref, o_ref):
        o_ref[...] = _tn(a_ref[...].astype(BF16), b_ref[...].astype(BF16))

    return _pcall(
        body, name=name, grid=(m // tm, n // tn),
        in_specs=[pl.BlockSpec((k, tm), lambda i, j: (0, i)), pl.BlockSpec((k, tn), lambda i, j: (0, j))],
        out_specs=pl.BlockSpec((tm, tn), lambda i, j: (i, j)),
        out_shape=jax.ShapeDtypeStruct((m, n), F32),
        compiler_params=pltpu.CompilerParams(dimension_semantics=("arbitrary", "arbitrary"), vmem_limit_bytes=VMEM_BIG),
    )(a, b)


def _wgrad_group(name, pairs, rows):
    def body(*refs):
        o_ref = refs[-1]
        for k in range(len(pairs)):
            o_ref[k * rows:(k + 1) * rows, :] = _tn(refs[2 * k][...].astype(BF16), refs[2 * k + 1][...].astype(BF16))

    in_specs, ops = [], []
    for a, b in pairs:
        in_specs += [pl.BlockSpec((a.shape[0], rows), lambda j: (0, j)), _resident(b)]
        ops += [a, b]
    return _pcall(
        body, name=name, grid=(8,), in_specs=in_specs,
        out_specs=pl.BlockSpec((None, len(pairs) * rows, D), lambda j: (j, 0, 0)),
        out_shape=jax.ShapeDtypeStruct((8, len(pairs) * rows, D), F32),
        compiler_params=pltpu.CompilerParams(dimension_semantics=("arbitrary",), vmem_limit_bytes=VMEM_BIG),
    )(*ops)


def _adam_math(w, g, m, v):
    m2 = ADAM_B1 * m + (1.0 - ADAM_B1) * g
    v2 = ADAM_B2 * v + (1.0 - ADAM_B2) * jnp.square(g)
    m_hat = m2 / (1.0 - ADAM_B1 ** ADAM_STEP)
    v_hat = v2 / (1.0 - ADAM_B2 ** ADAM_STEP)
    delta = -ADAM_LR * (m_hat / (jnp.sqrt(v_hat) + ADAM_EPS) + ADAM_WD * w)
    return delta, m2, v2


def _adamw(w, g, m, v, name):
    rows, cols = w.shape
    tr = 256 if rows % 256 == 0 else rows

    def body(w_ref, g_ref, m_ref, v_ref, d_ref, m2_ref, v2_ref):
        d_ref[...], m2_ref[...], v2_ref[...] = _adam_math(w_ref[...], g_ref[...], m_ref[...], v_ref[...])

    spec = pl.BlockSpec((tr, cols), lambda i: (i, 0))
    return _pcall(
        body, name=name, grid=(rows // tr,), in_specs=[spec] * 4, out_specs=[spec] * 3,
        out_shape=[jax.ShapeDtypeStruct(w.shape, F32)] * 3,
        compiler_params=pltpu.CompilerParams(dimension_semantics=("arbitrary",)),
    )(w, g, m, v)


def _adamw_small(gparts, ws, ms, vs):
    n = len(SMALL)

    def body(g_ref, *refs):
        w_refs, m_refs, v_refs = refs[:n], refs[n:2 * n], refs[2 * n:3 * n]
        outs, sum_ref = refs[3 * n:-1], refs[-1]
        g = g_ref[0]
        for k in range(1, 8):
            g = g + g_ref[k]
        sum_ref[...] = g
        outs[0][...] = sum_ref[17:18, 0:128]
        for t, name in enumerate(SMALL):
            r0, nr, c0, nc = SMALL_SLOT[name]
            gt = sum_ref[r0:r0 + nr, c0:c0 + nc]
            out = (gt,) + _adam_math(w_refs[t][...], gt, m_refs[t][...], v_refs[t][...])
            for o_ref, val in zip(outs[1 + 4 * t:5 + 4 * t], out):
                o_ref[...] = val

    whole = lambda s: pl.BlockSpec(s, lambda i, nd=len(s): (0,) * nd)
    ins = [gparts] + list(ws) + list(ms) + list(vs)
    out_shapes = [(1, 128)] + [a.shape for a in ws for _ in range(4)]
    return _pcall(
        body, name="adamw_small", grid=(1,), in_specs=[whole(a.shape) for a in ins],
        out_specs=[whole(s) for s in out_shapes], out_shape=[jax.ShapeDtypeStruct(s, F32) for s in out_shapes],
        scratch_shapes=[pltpu.VMEM((SMALL_ROWS, D), F32)],
        compiler_params=pltpu.CompilerParams(dimension_semantics=("arbitrary",)),
    )(*ins)


def _row_tile(rows):
    return next(t for t in (512, 400, 320) if rows % t == 0)


def _add_halves(g4, theirs, core, name):
    rows = g4.shape[2]
    tr = _row_tile(rows)

    def body(c_ref, a_ref, b_ref, o_ref):
        o_ref[...] = (a_ref[...] + b_ref[...]).astype(BF16)

    grid_spec = pltpu.PrefetchScalarGridSpec(
        num_scalar_prefetch=1, grid=(4, rows // tr),
        in_specs=[pl.BlockSpec((None, None, tr, D), lambda j, i, c: (j, c[0], i, 0)),
                  pl.BlockSpec((None, None, tr, D), lambda j, i, c: (j, 0, i, 0))],
        out_specs=pl.BlockSpec((None, tr, D), lambda j, i, c: (j, i, 0)))
    return _pcall(
        body, name=name, grid_spec=grid_spec, out_shape=jax.ShapeDtypeStruct((4, rows, D), BF16),
        compiler_params=pltpu.CompilerParams(dimension_semantics=("arbitrary", "arbitrary")),
    )(core, g4, theirs)


def _sum_adam(own, got, order, r0, w, m, v, name, transposed=False):
    n = w.shape[1] if transposed else w.shape[0]
    tr = min(n, 256) if n % 8 == 0 else n
    rows = tr if n % 8 == 0 else own.shape[1]

    def body(o_ref, a_ref, b_ref, c_ref, d_ref, w_ref, m_ref, v_ref, g_ref, dl_ref, m2_ref, v2_ref):
        f = lambda r: r[0:tr, :].astype(F32)
        g = ((f(a_ref) + f(b_ref)) + f(c_ref)) + f(d_ref)
        g = g.T if transposed else g
        g_ref[...] = g
        dl_ref[...], m2_ref[...], v2_ref[...] = _adam_math(w_ref[...], g, m_ref[...], v_ref[...])

    slot = lambda k: pl.BlockSpec((None, rows, D), lambda i, o: (o[k], r0 // rows + i, 0))
    wspec = pl.BlockSpec((D, tr), lambda i, o: (0, i)) if transposed else pl.BlockSpec((tr, D), lambda i, o: (i, 0))
    grid_spec = pltpu.PrefetchScalarGridSpec(
        num_scalar_prefetch=1, grid=(n // tr,), in_specs=[slot(0), slot(1), slot(2), slot(3), wspec, wspec, wspec],
        out_specs=[wspec] * 4)
    return _pcall(
        body, name=name, grid_spec=grid_spec, out_shape=[jax.ShapeDtypeStruct(w.shape, F32)] * 4,
        compiler_params=pltpu.CompilerParams(dimension_semantics=("arbitrary",)),
    )(order, own, got, got, got, w, m, v)


def _sum_adam_rows(own, got, order, ws, ms, vs, name):
    n, rows = len(ws), ws[0].shape[0]

    def body(o_ref, a_ref, b_ref, c_ref, d_ref, *refs):
        ins, outs = refs[:3 * n], refs[3 * n:]
        for t in range(n):
            r = slice(t * rows, (t + 1) * rows)
            f = lambda ref: ref[r, :].astype(F32)
            g = ((f(a_ref) + f(b_ref)) + f(c_ref)) + f(d_ref)
            out = (g,) + _adam_math(ins[t][...], g, ins[n + t][...], ins[2 * n + t][...])
            for o, val in zip(outs[4 * t:4 * t + 4], out):
                o[...] = val

    slot = lambda k: pl.BlockSpec((None, n * rows, D), lambda i, o: (o[k], 0, 0), pipeline_mode=pl.Buffered(1))
    wspec = pl.BlockSpec((rows, D), lambda i, o: (0, 0), pipeline_mode=pl.Buffered(1))
    grid_spec = pltpu.PrefetchScalarGridSpec(
        num_scalar_prefetch=1, grid=(1,), in_specs=[slot(0), slot(1), slot(2), slot(3)] + [wspec] * (3 * n),
        out_specs=[pl.BlockSpec((rows, D), lambda i, o: (0, 0))] * (4 * n))
    return _pcall(
        body, name=name, grid_spec=grid_spec, out_shape=[jax.ShapeDtypeStruct((rows, D), F32)] * (4 * n),
        compiler_params=pltpu.CompilerParams(dimension_semantics=("arbitrary",), vmem_limit_bytes=VMEM_BIG),
    )(order, own, got, got, got, *ws, *ms, *vs)


def _sum_chips(own, got, order, name):
    rows = own.shape[1]
    tr = _row_tile(rows)

    def body(o_ref, a_ref, b_ref, c_ref, d_ref, out_ref):
        f = lambda r: r[...].astype(F32)
        out_ref[...] = ((f(a_ref) + f(b_ref)) + f(c_ref)) + f(d_ref)

    slot = lambda k: pl.BlockSpec((None, tr, D), lambda i, o: (o[k], i, 0))
    grid_spec = pltpu.PrefetchScalarGridSpec(
        num_scalar_prefetch=1, grid=(rows // tr,), in_specs=[slot(0), slot(1), slot(2), slot(3)],
        out_specs=pl.BlockSpec((tr, D), lambda i, o: (i, 0)))
    return _pcall(
        body, name=name, grid_spec=grid_spec, out_shape=jax.ShapeDtypeStruct((rows, D), F32),
        compiler_params=pltpu.CompilerParams(dimension_semantics=("arbitrary",)),
    )(order, own, got, got, got)


def _place():
    return lax.axis_index("x"), lax.axis_index("y"), lax.axis_index("c")


def _allgather(block, name):
    def body(x_ref, out_ref, token, send_sems, recv_sems, local_sem):
        token[...] = jnp.zeros_like(token)
        x, y, c = _place()
        me, sibling = (x, y, c), (x, y, 1 - c)
        chips = [(1 - x, y), (x, 1 - y), (1 - x, 1 - y)]

        def slot(px, py, pc):
            return out_ref.at[4 * px + 2 * py + pc]

        def copy(k, blk, to, src=None):
            return pltpu.make_async_remote_copy(
                src_ref=slot(*blk) if src is None else src, dst_ref=slot(*blk),
                send_sem=send_sems.at[k], recv_sem=recv_sems.at[k], device_id=to, device_id_type=MESH)

        mine = pltpu.make_async_copy(x_ref, slot(*me), local_sem)
        mine.start()
        first = [copy(0, me, sibling, src=x_ref)]
        first += [copy(1 + j, me, (*chip, c), src=x_ref) for j, chip in enumerate(chips)]
        for cp in first:
            cp.start()
        passed = [copy(4 + j, (*chip, c), sibling) for j, chip in enumerate(chips)]
        for j, chip in enumerate(chips):
            copy(1 + j, (*chip, c), me).wait_recv()
            passed[j].start()
        copy(0, sibling, me).wait_recv()
        for j, chip in enumerate(chips):
            copy(4 + j, (*chip, 1 - c), me).wait_recv()
        for cp in first + passed:
            cp.wait_send()
        mine.wait()

    return _pcall(
        body, name=name,
        out_shape=[jax.ShapeDtypeStruct((8,) + block.shape, block.dtype), jax.ShapeDtypeStruct((8, 128), F32)],
        in_specs=[pl.BlockSpec(memory_space=pl.ANY)],
        out_specs=[pl.BlockSpec(memory_space=pl.ANY), pl.BlockSpec(memory_space=pltpu.VMEM)],
        scratch_shapes=[pltpu.SemaphoreType.DMA((7,)), pltpu.SemaphoreType.DMA((7,)), pltpu.SemaphoreType.DMA(())],
        compiler_params=pltpu.CompilerParams(has_side_effects=True),
    )(block)


HBM_SPEC = pl.BlockSpec(memory_space=pltpu.HBM)
SEM_SPEC = pl.BlockSpec(memory_space=pltpu.SEMAPHORE)
ANY_SPEC = pl.BlockSpec(memory_space=pl.ANY)
EFFECT = pltpu.SideEffectType.DATAFLOW_SIDE_EFFECTING


def _in_hbm(a):
    return pltpu.with_memory_space_constraint(a, pltpu.HBM)


def _start_copies(name, src, land_shape, plan, n):
    def body(src_ref, land_ref, send_sems, recv_sems, src_thru, land_thru, token):
        for k, (s, d, to, _) in enumerate(plan(src_ref, land_ref)):
            pltpu.make_async_remote_copy(src_ref=s, dst_ref=d, send_sem=send_sems.at[k], recv_sem=recv_sems.at[k],
                                         device_id=to, device_id_type=MESH).start()
        token[...] = jnp.zeros_like(token)

    return _pcall(
        body, name=name,
        out_shape=(pltpu.SemaphoreType.DMA((n,)), pltpu.SemaphoreType.DMA((n,)), pltpu.HBM(src.shape, src.dtype),
                   pltpu.HBM(land_shape, src.dtype), jax.ShapeDtypeStruct((8, 128), F32)),
        in_specs=(HBM_SPEC, HBM_SPEC),
        out_specs=(SEM_SPEC, SEM_SPEC, HBM_SPEC, HBM_SPEC, pl.BlockSpec(memory_space=pltpu.VMEM)),
        input_output_aliases={0: 2, 1: 3}, compiler_params=pltpu.CompilerParams(has_side_effects=EFFECT),
    )(_in_hbm(src), _in_hbm(lax.empty(land_shape, src.dtype)))


def _wait_copies(name, started, after, plan):
    send_sems, recv_sems, src_thru, land_thru, _ = started

    def body(src_ref, land_ref, send_sems, recv_sems, *rest):
        for k, (s, _, to, mine) in enumerate(plan(src_ref, land_ref)):
            cp = pltpu.make_async_remote_copy(src_ref=s, dst_ref=mine, send_sem=send_sems.at[k],
                                              recv_sem=recv_sems.at[k], device_id=to, device_id_type=MESH)
            cp.wait_send()
            cp.wait_recv()

    return _pcall(
        body, name=name,
        out_shape=(pltpu.HBM(src_thru.shape, src_thru.dtype), pltpu.HBM(land_thru.shape, land_thru.dtype)),
        in_specs=(HBM_SPEC, HBM_SPEC, SEM_SPEC, SEM_SPEC) + (ANY_SPEC,) * len(after), out_specs=(HBM_SPEC, HBM_SPEC),
        input_output_aliases={0: 0, 1: 1}, compiler_params=pltpu.CompilerParams(has_side_effects=EFFECT),
    )(src_thru, land_thru, send_sems, recv_sems, *after)


def _gather_plan(src_ref, land_ref):
    x, y, c = _place()
    peers = [(x, y, 1 - c), (1 - x, y, c), (x, 1 - y, c), (1 - x, 1 - y, c)]
    return [(src_ref, land_ref.at[4 * x + 2 * y + c], p, land_ref.at[4 * p[0] + 2 * p[1] + p[2]]) for p in peers]


def _swap_plan(src_ref, land_ref):
    x, y, c = _place()
    return [(src_ref.at[:, pl.ds(1 - c, 1)], land_ref, (x, y, 1 - c), land_ref)]


def _exchange_plan(src_ref, land_ref):
    x, y, c = _place()
    chips = [(1 - x, y), (x, 1 - y), (1 - x, 1 - y)]
    return [(src_ref.at[2 * px + py], land_ref.at[2 * x + y], (px, py, c), land_ref.at[2 * px + py]) for px, py in chips]


def _gather_forward(land, block):
    def body(land_ref, out_ref, send_sems, recv_sems):
        x, y, c = _place()
        chips = [(1 - x, y), (x, 1 - y), (1 - x, 1 - y)]

        def copy(k, px, py, pc):
            blk = out_ref.at[4 * px + 2 * py + pc]
            return pltpu.make_async_remote_copy(src_ref=blk, dst_ref=blk, send_sem=send_sems.at[k],
                                                recv_sem=recv_sems.at[k], device_id=(x, y, 1 - c), device_id_type=MESH)

        sent = [copy(k, px, py, c) for k, (px, py) in enumerate(chips)]
        for cp in sent:
            cp.start()
        for k, (px, py) in enumerate(chips):
            copy(k, px, py, 1 - c).wait_recv()
        for cp in sent:
            cp.wait_send()

    land = _pcall(
        body, name="allgather_rest_forward", out_shape=jax.ShapeDtypeStruct(land.shape, land.dtype),
        in_specs=[ANY_SPEC], out_specs=ANY_SPEC, input_output_aliases={0: 0},
        scratch_shapes=[pltpu.SemaphoreType.DMA((3,)), pltpu.SemaphoreType.DMA((3,))],
        compiler_params=pltpu.CompilerParams(has_side_effects=True),
    )(land)

    rows = block.shape[0]
    tr = rows // 4

    def place(me_ref, x_ref, land_ref, out_ref):
        out_ref[...] = x_ref[...]

    x, y, c = _place()
    grid_spec = pltpu.PrefetchScalarGridSpec(
        num_scalar_prefetch=1, grid=(rows // tr,),
        in_specs=[pl.BlockSpec((tr, D), lambda i, me: (i, 0)), ANY_SPEC],
        out_specs=pl.BlockSpec((None, tr, D), lambda i, me: (me[0], i, 0)))
    return _pcall(
        place, name="allgather_rest_own", grid_spec=grid_spec, out_shape=jax.ShapeDtypeStruct(land.shape, land.dtype),
        input_output_aliases={2: 0}, compiler_params=pltpu.CompilerParams(dimension_semantics=("arbitrary",)),
    )((4 * x + 2 * y + c).reshape(1), block, land)


class _ReduceScatter:
    def __init__(self, name, g):
        self.name = name
        rows = g.shape[1]
        self.started = _start_copies(name + "_swap_start", g.reshape(4, 2, rows, D), (4, 1, rows, D), _swap_plan, 1)
        self.token = self.started[4][0, 0]

    def halfway(self, after):
        g4, theirs = _wait_copies(self.name + "_swap_wait", self.started, after, _swap_plan)
        self.own = _add_halves(g4, theirs, lax.axis_index("c").reshape(1), self.name + "_add_halves")
        self.started = _start_copies(self.name + "_exch_start", self.own, self.own.shape, _exchange_plan, 3)
        self.token = self.started[4][0, 0]

    def finish(self, after):
        own, got = _wait_copies(self.name + "_exch_wait", self.started, after, _exchange_plan)
        chip = 2 * lax.axis_index("x") + lax.axis_index("y")
        return own, got, (chip + jnp.arange(4, dtype=jnp.int32)) % 4


def _pack_small(p, scalar=None):
    z = lambda a, n: jnp.pad(a, ((0, 0), (0, n - a.shape[1])))
    rows = [z(p['rel_bias'], D), z(p['b_fgt'], D), jnp.concatenate([p['g_fox_out'], p['g_chk_out']], axis=1)]
    rows += [p[n] for n in ('g_mix_pre', 'g_mix_post', 'g_mem_kv', 'g_mem_pre', 'g_mem_post', 'g_ff_pre', 'g_ff_post')]
    rows.append(jnp.zeros((1, D), F32) if scalar is None else z(jnp.reshape(scalar, (1, 1)), D))
    rows.append(jnp.zeros((SMALL_ROWS - 18, D), F32))
    return jnp.concatenate(rows, axis=0)


_GAP_DEV, _GAP_ROW = divmod(GATE0 + 8, N_IN)
_GAP = CHK0 - GATE0 - 8


def _in_rows_to_proj(g):
    runs = [(j, 0, N_IN, N_IN * j) for j in range(_GAP_DEV)]
    runs += [(_GAP_DEV, 0, _GAP_ROW, N_IN * _GAP_DEV), (_GAP_DEV, _GAP_ROW, N_IN, N_IN * _GAP_DEV + _GAP_ROW + _GAP)]
    runs += [(j, 0, N_IN, N_IN * j + _GAP) for j in range(_GAP_DEV + 1, 8)]

    def body(g_ref, o_ref, acc_ref):
        acc_ref[...] = jnp.zeros_like(acc_ref)
        for j, r0, r1, dest in runs:
            start, shift = dest // 16 * 16, dest % 16
            win = -(-(shift + r1 - r0) // 16) * 16
            r = lax.broadcasted_iota(jnp.int32, (win, R_IN), 0)
            c = lax.broadcasted_iota(jnp.int32, (win, R_IN), 1)
            move = jnp.where((c >= r0) & (c < r1) & (r == c - r0 + shift), 1.0, 0.0).astype(BF16)
            acc_ref[start:start + win, :] += _nn(move, g_ref[j])
        o_ref[...] = acc_ref[...].astype(BF16)

    return _pcall(
        body, name="w_in_layout", out_shape=jax.ShapeDtypeStruct((PROJ, D), BF16), grid=(1,),
        in_specs=[pl.BlockSpec(g.shape, lambda i: (0, 0, 0))], out_specs=pl.BlockSpec((PROJ, D), lambda i: (0, 0)),
        scratch_shapes=[pltpu.VMEM((PROJ, D), F32)],
        compiler_params=pltpu.CompilerParams(dimension_semantics=("arbitrary",), vmem_limit_bytes=VMEM_BIG),
    )(g)


def _proj_rows_to_in(g):
    pad = lambda a: jnp.pad(a, ((0, R_IN - a.shape[0]), (0, 0)))
    lo = N_IN * _GAP_DEV
    shards = [pad(g[N_IN * j:N_IN * (j + 1)]) for j in range(_GAP_DEV)]
    shards.append(pad(jnp.concatenate([g[lo:lo + _GAP_ROW], g[lo + _GAP_ROW + _GAP:lo + N_IN + _GAP]], axis=0)))
    shards += [pad(g[N_IN * j + _GAP:N_IN * (j + 1) + _GAP]) for j in range(_GAP_DEV + 1, 8)]
    return jnp.stack(shards)


def _local_grads(x, mem, tgt, win_t, gw_of, sm, on_grads):
    b_pad = jnp.pad(sm['b_fgt'], ((0, 0), (0, 120)))
    tbl = jnp.pad(sm['rel_bias'], ((0, 0), (0, NREL_PAD - 257)))

    h1, proj, flog = _premix_fwd(x, sm['g_mix_pre'], win_t)
    c = _gate_fwd(flog, b_pad)
    ct3 = c[:, :8].T.reshape(4, 2, T)
    o_f, lse_f = _fox_fwd(proj, c, ct3)
    vt3 = _relvec_fwd(tbl).reshape(4, 2, VW)
    kvp = jnp.pad(proj[:, CHK0 + 512:], ((LEFT, 0), (0, 0)))
    o_c, lse_c = _chk_fwd(proj, kvp, vt3)
    gw = gw_of([o_f, o_c])
    w_out, w_mq, w_mk, w_mv, w_mo, w1_t, w2 = (_wblk(gw, n) for n in ('w_out', 'w_mq', 'w_mk', 'w_mv', 'w_mo', 'w_ff1', 'w_ff2'))
    ycat, z, x1, h2, qm = _postmix_fwd(x, o_f, o_c, sm['g_fox_out'], sm['g_chk_out'], w_out,
                                       sm['g_mix_post'], sm['g_mem_pre'], w_mq)
    memn, km, vm = _memkv_fwd(mem, sm['g_mem_kv'], w_mk, w_mv)
    om, ym, x2, h3 = _mem_fwd(qm, x1, km, vm, w_mo, sm['g_mem_post'], sm['g_ff_pre'])
    a, y3, dx3, loss_acc = _ffn_fwd(h3, x2, tgt, w1_t, w2, sm['g_ff_post'])

    gs = {}
    dx2, da, dy3, r, gs['g_ff_post'], gs['g_ff_pre'] = _ffn_bwd(dx3, y3, x2, a, w1_t, w2, sm['g_ff_post'], sm['g_ff_pre'])
    zero = on_grads('A', _wgrad_group("wgrad_ff", [(da, h3), (r, dy3)], 512), None)
    dx1, dym, dqm, dkm, dvm, gs['g_mem_post'], gs['g_mem_pre'] = _mem_bwd(
        dx2, ym, x1, qm, km, vm, w_mo, w_mq, sm['g_mem_post'] + zero, sm['g_mem_pre'])
    zero = on_grads('A halfway', None, [dx1])
    gs['g_mem_kv'] = _memkv_bwd(dkm, dvm, mem, w_mk, w_mv)
    dz, dof, doc, gs['g_mix_post'], gs['g_fox_out'], gs['g_chk_out'] = _postmix_bwd(
        dx1, z, o_f, o_c, w_out, sm['g_mix_post'] + zero, sm['g_fox_out'], sm['g_chk_out'])
    zero = on_grads('B', _wgrad_group("wgrad_mem_out", [(ycat, dz), (h2, dqm), (memn, dkm), (memn, dvm), (om, dym)], 128), None)
    dq_f, dk_f, dv_f, dct, dcq = _fox_bwd(proj, c, ct3 + zero, o_f, lse_f, dof)
    zero = on_grads('B halfway', None, [dq_f])
    dq_c, dk_c, dv_c, gv = _chk_bwd(proj, kvp, vt3 + zero, o_c, lse_c, doc)
    gs['rel_bias'] = _relvec_bwd(gv.reshape(8, VW))[:, :257]
    dc = jnp.pad(dct.reshape(8, T).T + dcq[:, :, :2].transpose(1, 0, 2).reshape(T, 8), ((0, 0), (0, 120)))
    dflog, db = _gate_bwd(dc, flog, b_pad)
    gs['b_fgt'] = db[0:1, :8]
    grad_x, dproj, gs['g_mix_pre'] = _premix_bwd(dx1, x, [dq_f, dk_f, dv_f, dflog, dq_c, dk_c, dv_c], win_t, sm['g_mix_pre'])
    on_grads('C', _proj_rows_to_in(_wgrad(dproj, h1, "wgrad_in")), None)
    return loss_acc[0, 0], grad_x, gs


def kernel(x, mem, w_in, b_fgt, rel_bias, g_fox_out, g_chk_out, w_out, g_mix_pre, g_mix_post, g_mem_kv, w_mq, w_mk, w_mv, w_mo, g_mem_pre, g_mem_post, w_ff1, w_ff2, g_ff_pre, g_ff_post, loss_target, m_w_in, m_b_fgt, m_rel_bias, m_g_fox_out, m_g_chk_out, m_w_out, m_g_mix_pre, m_g_mix_post, m_g_mem_kv, m_w_mq, m_w_mk, m_w_mv, m_w_mo, m_g_mem_pre, m_g_mem_post, m_w_ff1, m_w_ff2, m_g_ff_pre, m_g_ff_post, v_w_in, v_b_fgt, v_rel_bias, v_g_fox_out, v_g_chk_out, v_w_out, v_g_mix_pre, v_g_mix_post, v_g_mem_kv, v_w_mq, v_w_mk, v_w_mv, v_w_mo, v_g_mem_pre, v_g_mem_post, v_w_ff1, v_w_ff2, v_g_ff_pre, v_g_ff_post):
    args = dict(locals())
    two_d = lambda a: a.reshape(a.shape[-2:])
    w = {n: two_d(args[n]) for n in WEIGHTS}
    m = {n: two_d(args['m_' + n]) for n in WEIGHTS}
    v = {n: two_d(args['v_' + n]) for n in WEIGHTS}

    sm = {n: w[n] for n in SMALL}
    shard_in = jnp.pad(w['w_in'].T, ((0, R_IN - N_IN), (0, 0))).astype(BF16)
    gathered_in, zero = _allgather(shard_in, "allgather_w_in")
    win_t = _in_rows_to_proj(gathered_in)
    shard_rest = (jnp.concatenate([w['w_ff1'].T, w['w_ff2'], w['w_out'], w['w_mq'], w['w_mk'], w['w_mv'], w['w_mo']],
                                  axis=0) + zero[0, 0]).astype(BF16)
    rest = _start_copies("allgather_rest_start", shard_rest, (8, R_REST, D), _gather_plan, 4)
    sm['g_mix_pre'] = sm['g_mix_pre'] + rest[4][0, 0]

    def gw_of(after):
        block, land = _wait_copies("allgather_rest_wait", rest, after, _gather_plan)
        return _gather_forward(land, block)

    rs = {}

    def on_grads(stage, g, after):
        if stage.endswith('halfway'):
            rs[stage[0]].halfway(after)
            return rs[stage[0]].token
        rs[stage] = _ReduceScatter("rs_" + stage.lower(), g)
        return rs[stage].token

    loss_local, grad_x, gs = _local_grads(x[0], mem[0], loss_target[0], win_t, gw_of, sm, on_grads)
    grads, deltas, new_m, new_v = {}, {}, {}, {}

    def update(n, out):
        grads[n], deltas[n], new_m[n], new_v[n] = out

    packed = _pack_small(gs, loss_local + rs['C'].token)
    rs['C'].halfway([packed])
    gparts, _ = _allgather(packed + rs['C'].token, "allgather_small_grads")
    small = _adamw_small(gparts, [w[n] for n in SMALL], [m[n] for n in SMALL], [v[n] for n in SMALL])
    loss = small[0][0, 0]
    for t, n in enumerate(SMALL):
        update(n, small[1 + 4 * t:5 + 4 * t])

    own, got, order = rs['A'].finish([grad_x, rs['C'].started[4]])
    update('w_ff1', _sum_adam(own, got, order, 0, w['w_ff1'], m['w_ff1'], v['w_ff1'], "adamw_w_ff1", transposed=True))
    update('w_ff2', _sum_adam(own, got, order, 512, w['w_ff2'], m['w_ff2'], v['w_ff2'], "adamw_w_ff2"))
    own, got, order = rs['B'].finish([grad_x, rs['C'].started[4]])
    names_b = ('w_out', 'w_mq', 'w_mk', 'w_mv', 'w_mo')
    done = _sum_adam_rows(own, got, order, [w[n] for n in names_b], [m[n] for n in names_b], [v[n] for n in names_b],
                          "adamw_group_b")
    for k, n in enumerate(names_b):
        update(n, done[4 * k:4 * k + 4])

    own, got, order = rs['C'].finish([new_v[n] for n in BIG if n != 'w_in'])
    done = _sum_adam(own, got, order, 0, w['w_in'].T, m['w_in'].T, v['w_in'].T, "adamw_w_in")
    update('w_in', [a.T for a in done])

    out = [loss, grad_x[None]]
    for group in (grads, deltas, new_m, new_v):
        out += [group[n].reshape(args[n].shape) for n in WEIGHTS]
    return tuple(out)
```

```python
import functools

import jax
import jax.numpy as jnp
from jax import lax
from jax.experimental import pallas as pl
from jax.experimental.pallas import tpu as pltpu

F32 = jnp.float32
BF16 = jnp.bfloat16
MESH = pl.DeviceIdType.MESH

T = 2048
D = 1024
NMEM = 256
DFF = 4096
EPS = 1e-6
TM = 256
TM_WIDE = 512
TQ = 256
FQ = 512
HD = 64
SCALE = HD ** -0.5
MEM_HEADS = 4
MEM_HD = 256
MEM_SCALE = MEM_HD ** -0.5
NEG = -1e30
LEFT = 512
WIN = LEFT + TQ
VW = 1024
NREL_PAD = 384
PROJ = 3200
GATE0 = 1536
CHK0 = 1664
VMEM_BIG = 56 * 1024 * 1024

ADAM_LR = 0.001
ADAM_B1 = 0.9
ADAM_B2 = 0.999
ADAM_EPS = 1e-08
ADAM_WD = 0.01
ADAM_STEP = 10

N_IN = 385
R_IN = 400
R_REST = 1664
W_ROWS = {'w_ff1': (0, 512), 'w_ff2': (512, 512),
          'w_out': (1024, 128), 'w_mq': (1152, 128), 'w_mk': (1280, 128), 'w_mv': (1408, 128), 'w_mo': (1536, 128)}
R_A, R_B = 1024, 640
SMALL_ROWS = 24
SMALL_SLOT = {'rel_bias': (0, 8, 0, 257), 'b_fgt': (8, 1, 0, 8), 'g_fox_out': (9, 1, 0, 512), 'g_chk_out': (9, 1, 512, 512),
              'g_mix_pre': (10, 1, 0, 1024), 'g_mix_post': (11, 1, 0, 1024), 'g_mem_kv': (12, 1, 0, 1024),
              'g_mem_pre': (13, 1, 0, 1024), 'g_mem_post': (14, 1, 0, 1024), 'g_ff_pre': (15, 1, 0, 1024),
              'g_ff_post': (16, 1, 0, 1024)}

WEIGHTS = ['w_in', 'b_fgt', 'rel_bias', 'g_fox_out', 'g_chk_out', 'w_out', 'g_mix_pre', 'g_mix_post', 'g_mem_kv',
           'w_mq', 'w_mk', 'w_mv', 'w_mo', 'g_mem_pre', 'g_mem_post', 'w_ff1', 'w_ff2', 'g_ff_pre', 'g_ff_post']
BIG = ['w_in', 'w_out', 'w_mq', 'w_mk', 'w_mv', 'w_mo', 'w_ff1', 'w_ff2']
SMALL = [n for n in WEIGHTS if n not in BIG]


def _pcall(body, **kw):
    return pl.pallas_call(body, **kw)


def _nn(a, b):
    return jnp.dot(a, b, preferred_element_type=F32)


def _nt(a, b):
    return lax.dot_general(a, b, (((1,), (1,)), ((), ())), preferred_element_type=F32)


def _tn(a, b):
    return lax.dot_general(a, b, (((0,), (0,)), ((), ())), preferred_element_type=F32)


def _w(ref):
    v = ref[...]
    return v if v.ndim == 2 else v.reshape(-1, v.shape[-1])


def _rstd(x):
    return lax.rsqrt(jnp.mean(x * x, axis=-1, keepdims=True) + EPS)


def _rms(x, g):
    return x * _rstd(x) * g


def _rms_bwd(x, g, dy):
    r = _rstd(x)
    xh = x * r
    dg = jnp.sum(dy * xh, axis=0, keepdims=True)
    dxh = dy * g
    dx = r * (dxh - xh * jnp.mean(dxh * xh, axis=-1, keepdims=True))
    return dx, dg


def _resident(a):
    if isinstance(a, tuple):
        _, shape, index = a
        return pl.BlockSpec(shape, lambda *_: index, pipeline_mode=pl.Buffered(1))
    return pl.BlockSpec(a.shape, lambda *_, nd=a.ndim: (0,) * nd, pipeline_mode=pl.Buffered(1))


def _wblk(gw, name):
    r0, rows = W_ROWS[name]
    return (gw, (8, rows, D), (0, r0 // rows, 0))


def _tok_call(body, name, tiled, full, outs_tiled, outs_acc=(), rows=T, tm=TM, vmem=None):
    in_specs = [pl.BlockSpec((tm, a.shape[1]), lambda i: (i, 0)) for a in tiled]
    in_specs += [_resident(a) for a in full]
    full = [a[0] if isinstance(a, tuple) else a for a in full]
    out_shape = [jax.ShapeDtypeStruct((rows, c), dt) for c, dt in outs_tiled]
    out_shape += [jax.ShapeDtypeStruct(s, F32) for s in outs_acc]
    out_specs = [pl.BlockSpec((tm, c), lambda i: (i, 0)) for c, _ in outs_tiled]
    out_specs += [pl.BlockSpec(s, lambda i, nd=len(s): (0,) * nd) for s in outs_acc]
    return _pcall(
        body, name=name, grid=(rows // tm,), in_specs=in_specs, out_specs=out_specs, out_shape=out_shape,
        compiler_params=pltpu.CompilerParams(dimension_semantics=("arbitrary",), vmem_limit_bytes=vmem),
    )(*tiled, *full)


def _one_call(body, name, ins, outs, vmem=None):
    whole = lambda s: pl.BlockSpec(s, lambda i, nd=len(s): (0,) * nd)
    return _pcall(
        body, name=name, grid=(1,), in_specs=[_resident(a) for a in ins], out_specs=[whole(s) for s, _ in outs],
        out_shape=[jax.ShapeDtypeStruct(s, dt) for s, dt in outs],
        compiler_params=pltpu.CompilerParams(dimension_semantics=("arbitrary",), vmem_limit_bytes=vmem),
    )(*[a[0] if isinstance(a, tuple) else a for a in ins])


def _premix_fwd(x, g_pre, win_t):
    def body(x_ref, g_ref, w_ref, h_ref, proj_ref, flog_ref):
        h = _rms(x_ref[...], g_ref[...]).astype(BF16)
        h_ref[...] = h
        p = _nt(h, w_ref[...])
        proj_ref[...] = p.astype(BF16)
        flog_ref[...] = p[:, GATE0:GATE0 + 128]

    return _tok_call(body, "premix_fwd", [x], [g_pre, win_t],
                     [(D, BF16), (PROJ, BF16), (128, F32)], tm=TM_WIDE, vmem=VMEM_BIG)


def _postmix_fwd(x, o_f, o_c, g_fo, g_co, w_out, g_post, g_mpre, w_mq):
    def body(x_ref, of_ref, oc_ref, gfo_ref, gco_ref, wo_ref, gp_ref, gm_ref, wq_ref,
             y_ref, z_ref, x1_ref, h2_ref, qm_ref):
        y_ref[:, :512] = _rms(of_ref[...], gfo_ref[...]).astype(BF16)
        y_ref[:, 512:] = _rms(oc_ref[...], gco_ref[...]).astype(BF16)
        z = _nn(y_ref[...], _w(wo_ref))
        z_ref[...] = z
        x1 = x_ref[...] + _rms(z, gp_ref[...])
        x1_ref[...] = x1
        h2 = _rms(x1, gm_ref[...]).astype(BF16)
        h2_ref[...] = h2
        qm_ref[...] = _nn(h2, _w(wq_ref)).astype(BF16)

    return _tok_call(body, "postmix_fwd", [x, o_f, o_c], [g_fo, g_co, w_out, g_post, g_mpre, w_mq],
                     [(D, BF16), (D, F32), (D, F32), (D, BF16), (D, BF16)], tm=TM_WIDE, vmem=VMEM_BIG)


def _memkv_fwd(mem, g_kv, w_mk, w_mv):
    def body(m_ref, g_ref, wk_ref, wv_ref, mn_ref, k_ref, v_ref):
        mn = _rms(m_ref[...], g_ref[...]).astype(BF16)
        mn_ref[...] = mn
        k_ref[...] = _nn(mn, _w(wk_ref)).astype(BF16)
        v_ref[...] = _nn(mn, _w(wv_ref)).astype(BF16)

    return _tok_call(body, "memkv_fwd", [mem], [g_kv, w_mk, w_mv],
                     [(D, BF16), (D, BF16), (D, BF16)], rows=NMEM, tm=NMEM, vmem=VMEM_BIG)


def _mem_fwd(qm, x1, km, vm, w_mo, g_post, g_fpre):
    def body(q_ref, x1_ref, k_ref, v_ref, wo_ref, gp_ref, gf_ref, om_ref, ym_ref, x2_ref, h3_ref):
        for h in range(MEM_HEADS):
            sl = slice(h * MEM_HD, (h + 1) * MEM_HD)
            s = _nt(q_ref[:, sl], k_ref[:, sl]) * MEM_SCALE
            p = jnp.exp(s - jnp.max(s, axis=-1, keepdims=True))
            p = p / jnp.sum(p, axis=-1, keepdims=True)
            om_ref[:, sl] = _nn(p.astype(BF16), v_ref[:, sl]).astype(BF16)
        ym = _nn(om_ref[...], _w(wo_ref))
        ym_ref[...] = ym
        x2 = x1_ref[...] + _rms(ym, gp_ref[...])
        x2_ref[...] = x2
        h3_ref[...] = _rms(x2, gf_ref[...]).astype(BF16)

    return _tok_call(body, "mem_fwd", [qm, x1], [km, vm, w_mo, g_post, g_fpre],
                     [(D, BF16), (D, F32), (D, F32), (D, BF16)], tm=TM_WIDE, vmem=VMEM_BIG)


def _ffn_fwd(h3, x2, tgt, w1_t, w2, g_post):
    def body(h_ref, x2_ref, t_ref, w1_ref, w2_ref, g_ref, a_ref, y_ref, dx_ref, loss_ref):
        @pl.when(pl.program_id(0) == 0)
        def _():
            loss_ref[...] = jnp.zeros_like(loss_ref)

        h = h_ref[...]
        y = jnp.zeros((TM_WIDE, D), F32)
        for c in range(4):
            cols = slice(c * (DFF // 4), (c + 1) * (DFF // 4))
            a = _nt(h, w1_ref[2 * c:2 * c + 2].reshape(DFF // 4, D))
            a_ref[:, cols] = a.astype(BF16)
            y = y + _nn(jnp.square(jnp.maximum(a, 0.0)).astype(BF16), w2_ref[2 * c:2 * c + 2].reshape(DFF // 4, D))
        y_ref[...] = y
        e = x2_ref[...] + _rms(y, g_ref[...]) - t_ref[...]
        dx_ref[...] = e * (1.0 / D)
        loss_ref[...] += 0.5 * jnp.sum(jnp.sum(e * e, axis=-1, keepdims=True) * (1.0 / D))

    return _tok_call(body, "ffn_fwd", [h3, x2, tgt], [w1_t, w2, g_post],
                     [(DFF, BF16), (D, F32), (D, F32)], [(8, 128)], tm=TM_WIDE, vmem=VMEM_BIG)


def _tri(lower):
    r = lax.broadcasted_iota(jnp.int32, (128, 128), 0)
    c = lax.broadcasted_iota(jnp.int32, (128, 128), 1)
    return jnp.where(r >= c if lower else c >= r, 1.0, 0.0).astype(F32)


def _hdot(a, b):
    return jnp.dot(a, b, preferred_element_type=F32, precision=lax.Precision.HIGHEST)


def _gate_fwd(flog, b_pad):
    def body(f_ref, b_ref, c_ref):
        tri = _tri(True)

        def step(i, carry):
            rows = pl.ds(pl.multiple_of(i * 128, 128), 128)
            z = f_ref[rows, :] + b_ref[...]
            lf = jnp.minimum(z, 0.0) - jnp.log(1.0 + jnp.exp(-jnp.abs(z)))
            cb = _hdot(tri, lf) + carry
            c_ref[rows, :] = cb
            return cb[127:128, :]

        lax.fori_loop(0, T // 128, step, jnp.zeros((1, 128), F32))

    return _one_call(body, "gate_fwd", [flog, b_pad], [((T, 128), F32)])[0]


def _gate_bwd(dc, flog, b_pad):
    def body(dc_ref, f_ref, b_ref, df_ref, db_ref):
        tri = _tri(False)

        def step(j, carry):
            run, db = carry
            i = T // 128 - 1 - j
            rows = pl.ds(pl.multiple_of(i * 128, 128), 128)
            dcb = dc_ref[rows, :]
            rb = _hdot(tri, dcb) + run
            z = f_ref[rows, :] + b_ref[...]
            df = rb * (1.0 / (1.0 + jnp.exp(z)))
            df_ref[rows, :] = df.astype(BF16)
            return run + jnp.sum(dcb, axis=0, keepdims=True), db + jnp.sum(df, axis=0, keepdims=True)

        _, db = lax.fori_loop(0, T // 128, step, (jnp.zeros((1, 128), F32), jnp.zeros((1, 128), F32)))
        db_ref[...] = jnp.broadcast_to(db, (8, 128))

    return _one_call(body, "gate_bwd", [dc, flog, b_pad], [((T, 128), BF16), ((8, 128), F32)])


def _lane_lo(rows=TQ):
    return lax.broadcasted_iota(jnp.int32, (rows, 128), 1) < HD


def _half(v, lo, a, scale=None):
    keep = lo if a == 0 else jnp.logical_not(lo)
    v = v.astype(F32) if scale is None else v.astype(F32) * scale
    return jnp.where(keep, v, 0.0).astype(BF16)


def _fox_specs():
    return [pl.BlockSpec((FQ, 128), lambda h, i: (i, h)),
            pl.BlockSpec((T, 128), lambda h, i: (0, 4 + h)),
            pl.BlockSpec((T, 128), lambda h, i: (0, 8 + h))]


def _lane_pick(x, at):
    lane = lax.broadcasted_iota(jnp.int32, x.shape, 1)
    return jnp.sum(jnp.where(lane == at, x, 0.0), axis=-1, keepdims=True)


def _fox_fwd(proj, c, ct3):
    def body(q_ref, k_ref, v_ref, c_ref, ct_ref, o_ref, l_ref):
        i = pl.program_id(1)
        lo = _lane_lo(FQ)
        causal = lax.broadcasted_iota(jnp.int32, (FQ, FQ), 1) <= lax.broadcasted_iota(jnp.int32, (FQ, FQ), 0)
        q = q_ref[...]
        qs = [_half(q, lo, a, SCALE) for a in range(2)]
        cqs = [_lane_pick(c_ref[...], 2 * pl.program_id(0) + a) for a in range(2)]

        def tile(off, carry, diagonal):
            kblk = k_ref[pl.ds(off, FQ), :]
            vblk = v_ref[pl.ds(off, FQ), :]
            new = []
            for a in range(2):
                m, l, acc = carry[a]
                s = _nt(qs[a], kblk) + (cqs[a] - ct_ref[a:a + 1, pl.ds(off, FQ)])
                if diagonal:
                    s = jnp.where(causal, s, NEG)
                m2 = jnp.maximum(m, jnp.max(s, axis=-1, keepdims=True))
                p = jnp.exp(s - m2)
                alpha = jnp.exp(m - m2)
                new.append((m2, alpha * l + jnp.sum(p, axis=-1, keepdims=True),
                            alpha * acc + _nn(p.astype(BF16), vblk)))
            return tuple(new)

        init = (jnp.full((FQ, 1), NEG, F32), jnp.zeros((FQ, 1), F32), jnp.zeros((FQ, 128), F32))
        carry = lax.fori_loop(0, i, lambda kb, c: tile(pl.multiple_of(kb * FQ, FQ), c, False), (init, init))
        carry = tile(pl.multiple_of(i * FQ, FQ), carry, True)
        outs = []
        for a in range(2):
            m, l, acc = carry[a]
            outs.append(acc / l)
            l_ref[:, 128 * a:128 * a + 128] = jnp.broadcast_to(m + jnp.log(l), (FQ, 128))
        o_ref[...] = jnp.where(lo, outs[0], outs[1])

    return _pcall(
        body, name="fox_fwd", grid=(4, T // FQ),
        in_specs=_fox_specs() + [pl.BlockSpec((FQ, 128), lambda h, i: (i, 0)),
                                 pl.BlockSpec((None, 2, T), lambda h, i: (h, 0, 0))],
        out_specs=[pl.BlockSpec((FQ, 128), lambda h, i: (i, h)), pl.BlockSpec((FQ, 256), lambda h, i: (i, h))],
        out_shape=[jax.ShapeDtypeStruct((T, 512), F32), jax.ShapeDtypeStruct((T, 1024), F32)],
        compiler_params=pltpu.CompilerParams(dimension_semantics=("arbitrary", "arbitrary"), vmem_limit_bytes=VMEM_BIG),
    )(proj, proj, proj, c, ct3)


def _fox_bwd(proj, c, ct3, o, lse, do):
    def body(q_ref, k_ref, v_ref, c_ref, ct_ref, o_ref, l_ref, do_ref, dq_ref, dkb_ref, dvb_ref, dct_ref, dcq_ref,
             dk_ref, dv_ref):
        i = pl.program_id(1)

        @pl.when(i == 0)
        def _():
            dk_ref[...] = jnp.zeros_like(dk_ref)
            dv_ref[...] = jnp.zeros_like(dv_ref)
            dct_ref[...] = jnp.zeros_like(dct_ref)

        lo = _lane_lo(FQ)
        causal = lax.broadcasted_iota(jnp.int32, (FQ, FQ), 1) <= lax.broadcasted_iota(jnp.int32, (FQ, FQ), 0)
        q = q_ref[...]
        do_v = do_ref[...]
        prod = do_v * o_ref[...]
        qs = [_half(q, lo, a, SCALE) for a in range(2)]
        dos = [_half(do_v, lo, a) for a in range(2)]
        deltas = [jnp.sum(jnp.where(lo if a == 0 else jnp.logical_not(lo), prod, 0.0), axis=-1, keepdims=True)
                  for a in range(2)]
        cqs = [_lane_pick(c_ref[...], 2 * pl.program_id(0) + a) for a in range(2)]
        las = [l_ref[:, 128 * a:128 * a + 1] for a in range(2)]

        def tile(off, carry, diagonal):
            kblk = k_ref[pl.ds(off, FQ), :]
            vblk = v_ref[pl.ds(off, FQ), :]
            new = []
            dk = jnp.zeros((FQ, 128), F32)
            dv = jnp.zeros((FQ, 128), F32)
            for a in range(2):
                dq_acc, rs = carry[a]
                s = _nt(qs[a], kblk) + (cqs[a] - ct_ref[a:a + 1, pl.ds(off, FQ)])
                if diagonal:
                    s = jnp.where(causal, s, NEG)
                p = jnp.exp(s - las[a])
                ds = p * (_nt(dos[a], vblk) - deltas[a])
                dsb = ds.astype(BF16)
                dk = dk + _tn(dsb, qs[a])
                dv = dv + _tn(p.astype(BF16), dos[a])
                dct_ref[a:a + 1, pl.ds(off, FQ)] -= jnp.sum(ds, axis=0, keepdims=True)
                new.append((dq_acc + _nn(dsb, kblk), rs + jnp.sum(ds, axis=-1, keepdims=True)))
            dk_ref[pl.ds(off, FQ), :] += dk
            dv_ref[pl.ds(off, FQ), :] += dv
            return tuple(new)

        init = (jnp.zeros((FQ, 128), F32), jnp.zeros((FQ, 1), F32))
        carry = lax.fori_loop(0, i, lambda kb, c: tile(pl.multiple_of(kb * FQ, FQ), c, False), (init, init))
        carry = tile(pl.multiple_of(i * FQ, FQ), carry, True)
        lane = lax.broadcasted_iota(jnp.int32, (FQ, 128), 1)
        dcq_ref[...] = jnp.where(lane == 0, carry[0][1], jnp.where(lane == 1, carry[1][1], 0.0))
        dq_ref[...] = (jnp.where(lo, carry[0][0], carry[1][0]) * SCALE).astype(BF16)

        @pl.when(i == T // FQ - 1)
        def _():
            dkb_ref[...] = dk_ref[...].astype(BF16)
            dvb_ref[...] = dv_ref[...].astype(BF16)

    blk = pl.BlockSpec((FQ, 128), lambda h, i: (i, h))
    wide = pl.BlockSpec((FQ, 256), lambda h, i: (i, h))
    rows = pl.BlockSpec((None, 2, T), lambda h, i: (h, 0, 0))
    col = pl.BlockSpec((T, 128), lambda h, i: (0, h))
    return _pcall(
        body, name="fox_bwd", grid=(4, T // FQ),
        in_specs=_fox_specs() + [pl.BlockSpec((FQ, 128), lambda h, i: (i, 0)), rows, blk, wide, blk],
        out_specs=[blk, col, col, rows, pl.BlockSpec((None, FQ, 128), lambda h, i: (h, i, 0))],
        out_shape=[jax.ShapeDtypeStruct((T, 512), BF16), jax.ShapeDtypeStruct((T, 512), BF16),
                   jax.ShapeDtypeStruct((T, 512), BF16), jax.ShapeDtypeStruct((4, 2, T), F32),
                   jax.ShapeDtypeStruct((4, T, 128), F32)],
        scratch_shapes=[pltpu.VMEM((T, 128), F32), pltpu.VMEM((T, 128), F32)],
        compiler_params=pltpu.CompilerParams(dimension_semantics=("arbitrary", "arbitrary"), vmem_limit_bytes=VMEM_BIG),
    )(proj, proj, proj, c, ct3, o, lse, do)


AUG_ROWSUM, AUG_COLSUM = 67, 64


def _fox_prep(proj, c2):
    def body(q_ref, k_ref, c_ref, qa_ref, ka_ref):
        lane = lax.broadcasted_iota(jnp.int32, (FQ, 128), 1)
        q = q_ref[...].astype(F32) * SCALE
        k = k_ref[...].astype(F32)
        for a in range(2):
            c = c_ref[:, 128 * a:128 * a + 128]
            hi = c.astype(BF16).astype(F32)
            mid = (c - hi).astype(BF16).astype(F32)
            lo = c - hi - mid
            parts = lambda first, sign: jnp.where(lane == first, sign * hi, jnp.where(lane == first + 1, sign * mid, sign * lo))
            qd = q if a == 0 else pltpu.roll(q, 64, 1)
            kd = k if a == 0 else pltpu.roll(k, 64, 1)
            ones = jnp.ones((FQ, 128), F32)
            qa = jnp.where(lane < 64, qd, jnp.where(lane < 67, ones, jnp.where(lane < 70, parts(67, 1.0), 0.0)))
            ka = jnp.where(lane < 64, kd, jnp.where(lane < 67, parts(64, -1.0), jnp.where(lane < 70, ones, 0.0)))
            qa_ref[:, 128 * a:128 * a + 128] = qa.astype(BF16)
            ka_ref[:, 128 * a:128 * a + 128] = ka.astype(BF16)

    wide = pl.BlockSpec((FQ, 256), lambda h, i: (i, h))
    return _pcall(
        body, name="fox_prep", grid=(4, T // FQ),
        in_specs=[pl.BlockSpec((FQ, 128), lambda h, i: (i, h)), pl.BlockSpec((FQ, 128), lambda h, i: (i, 4 + h)), wide],
        out_specs=[wide, wide], out_shape=[jax.ShapeDtypeStruct((T, 1024), BF16)] * 2,
        compiler_params=pltpu.CompilerParams(dimension_semantics=("arbitrary", "arbitrary")),
    )(proj, proj, c2)


def _fox_aug_specs():
    return [pl.BlockSpec((FQ, 256), lambda h, i: (i, h)), pl.BlockSpec((T, 256), lambda h, i: (0, h)),
            pl.BlockSpec((T, 128), lambda h, i: (0, 8 + h))]


def _causal():
    return lax.broadcasted_iota(jnp.int32, (FQ, FQ), 1) <= lax.broadcasted_iota(jnp.int32, (FQ, FQ), 0)


def _fox_fwd_aug(q_aug, k_aug, proj):
    def body(q_ref, k_ref, v_ref, o_ref, l_ref):
        i = pl.program_id(1)
        lo = _lane_lo(FQ)
        causal = _causal()
        qs = [q_ref[:, 128 * a:128 * a + 128] for a in range(2)]

        def tile(off, carry, diagonal):
            vblk = v_ref[pl.ds(off, FQ), :]
            new = []
            for a in range(2):
                m, l, acc = carry[a]
                s = _nt(qs[a], k_ref[pl.ds(off, FQ), 128 * a:128 * a + 128])
                if diagonal:
                    s = jnp.where(causal, s, NEG)
                m2 = jnp.maximum(m, jnp.max(s, axis=-1, keepdims=True))
                p = jnp.exp(s - m2)
                alpha = jnp.exp(m - m2)
                new.append((m2, alpha * l + jnp.sum(p, axis=-1, keepdims=True),
                            alpha * acc + _nn(p.astype(BF16), vblk)))
            return tuple(new)

        init = (jnp.full((FQ, 1), NEG, F32), jnp.zeros((FQ, 1), F32), jnp.zeros((FQ, 128), F32))
        carry = lax.fori_loop(0, i, lambda kb, c: tile(pl.multiple_of(kb * FQ, FQ), c, False), (init, init))
        carry = tile(pl.multiple_of(i * FQ, FQ), carry, True)
        outs = []
        for a in range(2):
            m, l, acc = carry[a]
            outs.append(acc / l)
            l_ref[:, 128 * a:128 * a + 128] = jnp.broadcast_to(m + jnp.log(l), (FQ, 128))
        o_ref[...] = jnp.where(lo, outs[0], outs[1])

    return _pcall(
        body, name="fox_fwd", grid=(4, T // FQ), in_specs=_fox_aug_specs(),
        out_specs=[pl.BlockSpec((FQ, 128), lambda h, i: (i, h)), pl.BlockSpec((FQ, 256), lambda h, i: (i, h))],
        out_shape=[jax.ShapeDtypeStruct((T, 512), F32), jax.ShapeDtypeStruct((T, 1024), F32)],
        compiler_params=pltpu.CompilerParams(dimension_semantics=("arbitrary", "arbitrary"), vmem_limit_bytes=VMEM_BIG),
    )(q_aug, k_aug, proj)


def _fox_bwd_aug(q_aug, k_aug, proj, o, lse, do, after):
    nq = T // FQ

    def body(q_ref, k_ref, v_ref, o_ref, l_ref, do_ref, after_ref, dq_ref, dkb_ref, dvb_ref, dcq_ref, dck_ref, dk_ref,
             dv_ref):
        i = pl.program_id(1)

        @pl.when(i == 0)
        def _():
            dk_ref[...] = jnp.zeros_like(dk_ref)
            dv_ref[...] = jnp.zeros_like(dv_ref)

        lo = _lane_lo(FQ)
        lane = lax.broadcasted_iota(jnp.int32, (FQ, 128), 1)
        causal = _causal()
        do_v = do_ref[...]
        prod = do_v * o_ref[...]
        qs = [q_ref[:, 128 * a:128 * a + 128] for a in range(2)]
        dos = [_half(do_v, lo, a) for a in range(2)]
        deltas = [jnp.sum(jnp.where(lo if a == 0 else jnp.logical_not(lo), prod, 0.0), axis=-1, keepdims=True)
                  for a in range(2)]
        las = [l_ref[:, 128 * a:128 * a + 1] for a in range(2)]

        def tile(off, carry, diagonal):
            vblk = v_ref[pl.ds(off, FQ), :]
            new = []
            dv = jnp.zeros((FQ, 128), F32)
            for a in range(2):
                kblk = k_ref[pl.ds(off, FQ), 128 * a:128 * a + 128]
                s = _nt(qs[a], kblk)
                if diagonal:
                    s = jnp.where(causal, s, NEG)
                p = jnp.exp(s - las[a])
                dsb = (p * (_nt(dos[a], vblk) - deltas[a])).astype(BF16)
                dk_ref[a, pl.ds(off, FQ), :] += _tn(dsb, qs[a])
                dv = dv + _tn(p.astype(BF16), dos[a])
                new.append(carry[a] + _nn(dsb, kblk))
            dv_ref[pl.ds(off, FQ), :] += dv
            return tuple(new)

        init = jnp.zeros((FQ, 128), F32)
        dqs = lax.fori_loop(0, i, lambda kb, c: tile(pl.multiple_of(kb * FQ, FQ), c, False), (init, init))
        dqs = tile(pl.multiple_of(i * FQ, FQ), dqs, True)
        pick = lambda x, at: jnp.sum(jnp.where(lane == at, x, 0.0), axis=-1, keepdims=True)
        for a in range(2):
            dcq_ref[:, 128 * a:128 * a + 128] = jnp.broadcast_to(pick(dqs[a], AUG_ROWSUM), (FQ, 128))
        dq_ref[...] = (jnp.where(lo, dqs[0], pltpu.roll(dqs[1], 64, 1)) * SCALE).astype(BF16)

        @pl.when(i == nq - 1)
        def _():
            big_lane = lax.broadcasted_iota(jnp.int32, (T, 128), 1)
            dkb_ref[...] = jnp.where(big_lane < 64, dk_ref[0], pltpu.roll(dk_ref[1], 64, 1)).astype(BF16)
            dvb_ref[...] = dv_ref[...].astype(BF16)
            for a in range(2):
                col = jnp.sum(jnp.where(big_lane == AUG_COLSUM, dk_ref[a], 0.0), axis=-1, keepdims=True)
                dck_ref[:, 128 * a:128 * a + 128] = jnp.broadcast_to(-col, (T, 128))

    blk = pl.BlockSpec((FQ, 128), lambda h, i: (i, h))
    wide = pl.BlockSpec((FQ, 256), lambda h, i: (i, h))
    col = pl.BlockSpec((T, 128), lambda h, i: (0, h))
    colwide = pl.BlockSpec((T, 256), lambda h, i: (0, h))
    return _pcall(
        body, name="fox_bwd", grid=(4, nq), in_specs=_fox_aug_specs() + [blk, wide, blk, ANY_SPEC],
        out_specs=[blk, col, col, wide, colwide],
        out_shape=[jax.ShapeDtypeStruct((T, 512), BF16)] * 3 + [jax.ShapeDtypeStruct((T, 1024), F32)] * 2,
        scratch_shapes=[pltpu.VMEM((2, T, 128), F32), pltpu.VMEM((T, 128), F32)],
        compiler_params=pltpu.CompilerParams(dimension_semantics=("arbitrary", "arbitrary"), vmem_limit_bytes=VMEM_BIG),
    )(q_aug, k_aug, proj, o, lse, do, after)


def _rel_onehot():
    ridx = lax.broadcasted_iota(jnp.int32, (NREL_PAD, VW), 0)
    j = lax.broadcasted_iota(jnp.int32, (NREL_PAD, VW), 1)
    return jnp.where(ridx == jnp.clip(TQ + LEFT - 1 - j, -128, 128) + 128, 1.0, 0.0).astype(F32)


def _relvec_fwd(tbl):
    def body(t_ref, v_ref):
        v_ref[...] = _hdot(t_ref[...], _rel_onehot())

    return _one_call(body, "relvec_fwd", [tbl], [((8, VW), F32)])[0]


def _relvec_bwd(gv):
    def body(g_ref, t_ref):
        t_ref[...] = lax.dot_general(g_ref[...], _rel_onehot(), (((1,), (1,)), ((), ())),
                                     preferred_element_type=F32, precision=lax.Precision.HIGHEST)

    return _one_call(body, "relvec_bwd", [gv], [((8, NREL_PAD), F32)])[0]


def _chk_bias(vt_ref, a, hidden):
    vb = jnp.broadcast_to(vt_ref[a:a + 1, :], (TQ, VW))
    y = pltpu.roll(vb, VW - (TQ - 1), 1, stride=1, stride_axis=0)[:, :WIN]
    cr = lax.broadcasted_iota(jnp.int32, (TQ, WIN), 0) // 64
    m = lax.broadcasted_iota(jnp.int32, (TQ, WIN), 1)
    return jnp.where((m // 64 >= cr) & (m // 64 <= cr + 8) & (m >= hidden), y, NEG)


def _chk_specs():
    return [pl.BlockSpec((TQ, 128), lambda h, i: (i, CHK0 // 128 + h)),
            pl.BlockSpec((T + LEFT, 128), lambda h, i: (0, h)),
            pl.BlockSpec((T + LEFT, 128), lambda h, i: (0, 4 + h)),
            pl.BlockSpec((None, 2, VW), lambda h, i: (h, 0, 0))]


def _chk_fwd(proj, kvp, vt3):
    def body(q_ref, k_ref, v_ref, vt_ref, o_ref, l_ref, bias_ref):
        i = pl.program_id(1)

        @pl.when(i == 0)
        def _():
            for first in range(3):
                for a in range(2):
                    bias_ref[first, a] = _chk_bias(vt_ref, a, max(LEFT - first * TQ, 0))

        lo = _lane_lo()
        off = pl.multiple_of(i * TQ, TQ)
        kw = k_ref[pl.ds(off, WIN), :]
        vw = v_ref[pl.ds(off, WIN), :]
        bias_at = jnp.minimum(i, 2)
        q = q_ref[...]
        outs = []
        for a in range(2):
            s = _nt(_half(q, lo, a, SCALE), kw) + bias_ref[bias_at, a]
            m = jnp.max(s, axis=-1, keepdims=True)
            p = jnp.exp(s - m)
            l = jnp.sum(p, axis=-1, keepdims=True)
            outs.append(_nn(p.astype(BF16), vw) / l)
            l_ref[:, 128 * a:128 * a + 128] = jnp.broadcast_to(m + jnp.log(l), (TQ, 128))
        o_ref[...] = jnp.where(lo, outs[0], outs[1])

    return _pcall(
        body, name="chk_fwd", grid=(4, T // TQ), in_specs=_chk_specs(),
        out_specs=[pl.BlockSpec((TQ, 128), lambda h, i: (i, h)), pl.BlockSpec((TQ, 256), lambda h, i: (i, h))],
        out_shape=[jax.ShapeDtypeStruct((T, 512), F32), jax.ShapeDtypeStruct((T, 1024), F32)],
        scratch_shapes=[pltpu.VMEM((3, 2, TQ, WIN), F32)],
        compiler_params=pltpu.CompilerParams(dimension_semantics=("arbitrary", "arbitrary")),
    )(proj, kvp, kvp, vt3)


def _chk_bwd(proj, kvp, vt3, o, lse, do):
    nq = T // TQ

    def body(q_ref, k_ref, v_ref, vt_ref, o_ref, l_ref, do_ref, dq_ref, dkb_ref, dvb_ref, gv_ref, bias_ref, dsum_ref,
             dk_ref, dv_ref):
        i = pl.program_id(1)

        @pl.when(i == 0)
        def _():
            for first in range(3):
                for a in range(2):
                    bias_ref[first, a] = _chk_bias(vt_ref, a, max(LEFT - first * TQ, 0))
            dsum_ref[...] = jnp.zeros_like(dsum_ref)
            dk_ref[...] = jnp.zeros_like(dk_ref)
            dv_ref[...] = jnp.zeros_like(dv_ref)

        lo = _lane_lo()
        off = pl.multiple_of(i * TQ, TQ)
        kw = k_ref[pl.ds(off, WIN), :]
        vw = v_ref[pl.ds(off, WIN), :]
        bias_at = jnp.minimum(i, 2)
        q = q_ref[...]
        do_v = do_ref[...]
        prod = do_v * o_ref[...]
        dqs = []
        for a in range(2):
            keep = lo if a == 0 else jnp.logical_not(lo)
            qa = _half(q, lo, a, SCALE)
            doa = _half(do_v, lo, a)
            delta = jnp.sum(jnp.where(keep, prod, 0.0), axis=-1, keepdims=True)
            s = _nt(qa, kw) + bias_ref[bias_at, a]
            p = jnp.exp(s - l_ref[:, 128 * a:128 * a + 1])
            ds = p * (_nt(doa, vw) - delta)
            dsum_ref[a] += ds
            dsb = ds.astype(BF16)
            dk_ref[pl.ds(off, WIN), :] += _tn(dsb, qa)
            dv_ref[pl.ds(off, WIN), :] += _tn(p.astype(BF16), doa)
            dqs.append(_nn(dsb, kw))
        dq_ref[...] = (jnp.where(lo, dqs[0], dqs[1]) * SCALE).astype(BF16)

        @pl.when(i == nq - 1)
        def _():
            dkb_ref[...] = dk_ref[LEFT:, :].astype(BF16)
            dvb_ref[...] = dv_ref[LEFT:, :].astype(BF16)
            rr = lax.broadcasted_iota(jnp.int32, (TQ, TQ), 0)
            cc = lax.broadcasted_iota(jnp.int32, (TQ, TQ), 1)
            flip = jnp.where(rr + cc == TQ - 1, 1.0, 0.0).astype(F32)
            for a in range(2):
                dpad = jnp.concatenate([dsum_ref[a], jnp.zeros((TQ, VW - WIN), F32)], axis=1)
                z = pltpu.roll(_hdot(flip, dpad), 0, 1, stride=1, stride_axis=0)
                gv_ref[a:a + 1, :] = jnp.sum(z, axis=0, keepdims=True)

    blk = pl.BlockSpec((TQ, 128), lambda h, i: (i, h))
    wide = pl.BlockSpec((TQ, 256), lambda h, i: (i, h))
    col = pl.BlockSpec((T, 128), lambda h, i: (0, h))
    return _pcall(
        body, name="chk_bwd", grid=(4, nq), in_specs=_chk_specs() + [blk, wide, blk],
        out_specs=[blk, col, col, pl.BlockSpec((None, 2, VW), lambda h, i: (h, 0, 0))],
        out_shape=[jax.ShapeDtypeStruct((T, 512), BF16), jax.ShapeDtypeStruct((T, 512), BF16),
                   jax.ShapeDtypeStruct((T, 512), BF16), jax.ShapeDtypeStruct((4, 2, VW), F32)],
        scratch_shapes=[pltpu.VMEM((3, 2, TQ, WIN), F32), pltpu.VMEM((2, TQ, WIN), F32),
                        pltpu.VMEM((T + LEFT, 128), F32), pltpu.VMEM((T + LEFT, 128), F32)],
        compiler_params=pltpu.CompilerParams(dimension_semantics=("arbitrary", "arbitrary")),
    )(proj, kvp, kvp, vt3, o, lse, do)


def _zero_at_start(*refs):
    @pl.when(pl.program_id(0) == 0)
    def _():
        for r in refs:
            r[...] = jnp.zeros_like(r)


def _ffn_bwd(dx3, y3, x2, a, w1_t, w2, g_post, g_pre):
    def body(dx3_ref, y_ref, x2_ref, a_ref, w1_ref, w2_ref, gp_ref, gf_ref,
             dx2_ref, da_ref, dy_ref, r_ref, dgp_ref, dgf_ref):
        _zero_at_start(dgp_ref, dgf_ref)
        dx3_v = dx3_ref[...]
        dy, dgp = _rms_bwd(y_ref[...], gp_ref[...], dx3_v)
        dgp_ref[...] += dgp
        dyb = dy.astype(BF16)
        dy_ref[...] = dyb
        ra = jnp.maximum(a_ref[...].astype(F32), 0.0)
        r_ref[...] = jnp.square(ra).astype(BF16)
        da = (_nt(dyb, _w(w2_ref)) * (2.0 * ra)).astype(BF16)
        da_ref[...] = da
        dh, dgf = _rms_bwd(x2_ref[...], gf_ref[...], _nn(da, _w(w1_ref)))
        dgf_ref[...] += dgf
        dx2_ref[...] = dx3_v + dh

    return _tok_call(body, "ffn_bwd", [dx3, y3, x2, a], [w1_t, w2, g_post, g_pre],
                     [(D, F32), (DFF, BF16), (D, BF16), (DFF, BF16)], [(1, D), (1, D)], vmem=VMEM_BIG)


def _mem_bwd(dx2, ym, x1, qm, km, vm, w_mo, w_mq, g_post, g_pre):
    def body(dx2_ref, ym_ref, x1_ref, q_ref, k_ref, v_ref, wo_ref, wq_ref, gp_ref, gm_ref,
             dx1_ref, dym_ref, dq_ref, dk_ref, dv_ref, dgp_ref, dgm_ref, dom_ref):
        _zero_at_start(dk_ref, dv_ref, dgp_ref, dgm_ref)
        dx2_v = dx2_ref[...]
        dym, dgp = _rms_bwd(ym_ref[...], gp_ref[...], dx2_v)
        dgp_ref[...] += dgp
        dymb = dym.astype(BF16)
        dym_ref[...] = dymb
        dom_ref[...] = _nt(dymb, _w(wo_ref)).astype(BF16)
        for h in range(MEM_HEADS):
            sl = slice(h * MEM_HD, (h + 1) * MEM_HD)
            qh, kh, doh = q_ref[:, sl], k_ref[:, sl], dom_ref[:, sl]
            s = _nt(qh, kh) * MEM_SCALE
            p = jnp.exp(s - jnp.max(s, axis=-1, keepdims=True))
            p = p / jnp.sum(p, axis=-1, keepdims=True)
            dp = _nt(doh, v_ref[:, sl])
            ds = (p * (dp - jnp.sum(p * dp, axis=-1, keepdims=True))).astype(BF16)
            dq_ref[:, sl] = (_nn(ds, kh) * MEM_SCALE).astype(BF16)
            dk_ref[:, sl] += _tn(ds, qh) * MEM_SCALE
            dv_ref[:, sl] += _tn(p.astype(BF16), doh)
        dh, dgm = _rms_bwd(x1_ref[...], gm_ref[...], _nt(dq_ref[...], _w(wq_ref)))
        dgm_ref[...] += dgm
        dx1_ref[...] = dx2_v + dh

    tiled = pl.BlockSpec((TM_WIDE, D), lambda i: (i, 0))
    in_specs = [tiled] * 4 + [_resident(a) for a in (km, vm, w_mo, w_mq, g_post, g_pre)]
    w_mo, w_mq = w_mo[0], w_mq[0]
    kv = pl.BlockSpec((NMEM, D), lambda i: (0, 0))
    vec = pl.BlockSpec((1, D), lambda i: (0, 0))
    return _pcall(
        body, name="mem_bwd", grid=(T // TM_WIDE,), in_specs=in_specs,
        out_specs=[tiled, tiled, tiled, kv, kv, vec, vec],
        out_shape=[jax.ShapeDtypeStruct((T, D), F32), jax.ShapeDtypeStruct((T, D), BF16),
                   jax.ShapeDtypeStruct((T, D), BF16), jax.ShapeDtypeStruct((NMEM, D), F32),
                   jax.ShapeDtypeStruct((NMEM, D), F32), jax.ShapeDtypeStruct((1, D), F32),
                   jax.ShapeDtypeStruct((1, D), F32)],
        scratch_shapes=[pltpu.VMEM((TM_WIDE, D), BF16)],
        compiler_params=pltpu.CompilerParams(dimension_semantics=("arbitrary",), vmem_limit_bytes=VMEM_BIG),
    )(dx2, ym, x1, qm, km, vm, w_mo, w_mq, g_post, g_pre)


def _memkv_bwd(dkm, dvm, mem, w_mk, w_mv):
    def body(dk_ref, dv_ref, m_ref, wk_ref, wv_ref, dg_ref):
        dmn = _nt(dk_ref[...].astype(BF16), _w(wk_ref)) + _nt(dv_ref[...].astype(BF16), _w(wv_ref))
        mv = m_ref[...]
        dg_ref[...] = jnp.sum(dmn * (mv * _rstd(mv)), axis=0, keepdims=True)

    return _one_call(body, "memkv_bwd", [dkm, dvm, mem, w_mk, w_mv], [((1, D), F32)], vmem=VMEM_BIG)[0]


def _postmix_bwd(dx1, z, o_f, o_c, w_out, g_post, g_fo, g_co):
    def body(dx1_ref, z_ref, of_ref, oc_ref, wo_ref, gp_ref, gfo_ref, gco_ref,
             dz_ref, dof_ref, doc_ref, dgp_ref, dgfo_ref, dgco_ref):
        _zero_at_start(dgp_ref, dgfo_ref, dgco_ref)
        dz, dgp = _rms_bwd(z_ref[...], gp_ref[...], dx1_ref[...])
        dgp_ref[...] += dgp
        dzb = dz.astype(BF16)
        dz_ref[...] = dzb
        dy = _nt(dzb, _w(wo_ref))
        dof, dgfo = _rms_bwd(of_ref[...], gfo_ref[...], dy[:, :512])
        doc, dgco = _rms_bwd(oc_ref[...], gco_ref[...], dy[:, 512:])
        dof_ref[...] = dof
        doc_ref[...] = doc
        dgfo_ref[...] += dgfo
        dgco_ref[...] += dgco

    return _tok_call(body, "postmix_bwd", [dx1, z, o_f, o_c], [w_out, g_post, g_fo, g_co],
                     [(D, BF16), (512, F32), (512, F32)], [(1, D), (1, 512), (1, 512)], tm=TM_WIDE, vmem=VMEM_BIG)


def _premix_bwd(dx1, x, pieces, win_t, g_pre):
    def body(dx1_ref, x_ref, *refs):
        piece_refs, (w_ref, g_ref, dx_ref, dp_ref, dg_ref) = refs[:len(pieces)], refs[len(pieces):]
        _zero_at_start(dg_ref)
        col = 0
        for p in piece_refs:
            dp_ref[:, col:col + p.shape[1]] = p[...]
            col += p.shape[1]
        dh, dg = _rms_bwd(x_ref[...], g_ref[...], _nn(dp_ref[...], w_ref[...]))
        dg_ref[...] += dg
        dx_ref[...] = dx1_ref[...] + dh

    return _tok_call(body, "premix_bwd", [dx1, x] + list(pieces), [win_t, g_pre], [(D, F32), (PROJ, BF16)], [(1, D)],
                     tm=TM_WIDE, vmem=VMEM_BIG)


def _wgrad(a, b, name):
    k, m = a.shape
    n = b.shape[1]
    tm = 640 if m % 640 == 0 and m > 1024 else min(m, 512)
    tn = min(n, 1024)

    def body(a_ref, b_ref, o_ref):
        o_ref[...] = _tn(a_ref[...].astype(BF16), b_ref[...].astype(BF16))

    return _pcall(
        body, name=name, grid=(m // tm, n // tn),
        in_specs=[pl.BlockSpec((k, tm), lambda i, j: (0, i)), pl.BlockSpec((k, tn), lambda i, j: (0, j))],
        out_specs=pl.BlockSpec((tm, tn), lambda i, j: (i, j)),
        out_shape=jax.ShapeDtypeStruct((m, n), F32),
        compiler_params=pltpu.CompilerParams(dimension_semantics=("arbitrary", "arbitrary"), vmem_limit_bytes=VMEM_BIG),
    )(a, b)


def _wgrad_group(name, pairs, rows):
    def body(*refs):
        o_ref = refs[-1]
        for k in range(len(pairs)):
            o_ref[k * rows:(k + 1) * rows, :] = _tn(refs[2 * k][...].astype(BF16), refs[2 * k + 1][...].astype(BF16))

    in_specs, ops = [], []
    for a, b in pairs:
        in_specs += [pl.BlockSpec((a.shape[0], rows), lambda j: (0, j)), _resident(b)]
        ops += [a, b]
    return _pcall(
        body, name=name, grid=(8,), in_specs=in_specs,
        out_specs=pl.BlockSpec((None, len(pairs) * rows, D), lambda j: (j, 0, 0)),
        out_shape=jax.ShapeDtypeStruct((8, len(pairs) * rows, D), F32),
        compiler_params=pltpu.CompilerParams(dimension_semantics=("arbitrary",), vmem_limit_bytes=VMEM_BIG),
    )(*ops)


def _adam_math(w, g, m, v):
    m2 = ADAM_B1 * m + (1.0 - ADAM_B1) * g
    v2 = ADAM_B2 * v + (1.0 - ADAM_B2) * jnp.square(g)
    m_hat = m2 / (1.0 - ADAM_B1 ** ADAM_STEP)
    v_hat = v2 / (1.0 - ADAM_B2 ** ADAM_STEP)
    delta = -ADAM_LR * (m_hat / (jnp.sqrt(v_hat) + ADAM_EPS) + ADAM_WD * w)
    return delta, m2, v2


def _adamw(w, g, m, v, name):
    rows, cols = w.shape
    tr = 256 if rows % 256 == 0 else rows

    def body(w_ref, g_ref, m_ref, v_ref, d_ref, m2_ref, v2_ref):
        d_ref[...], m2_ref[...], v2_ref[...] = _adam_math(w_ref[...], g_ref[...], m_ref[...], v_ref[...])

    spec = pl.BlockSpec((tr, cols), lambda i: (i, 0))
    return _pcall(
        body, name=name, grid=(rows // tr,), in_specs=[spec] * 4, out_specs=[spec] * 3,
        out_shape=[jax.ShapeDtypeStruct(w.shape, F32)] * 3,
        compiler_params=pltpu.CompilerParams(dimension_semantics=("arbitrary",)),
    )(w, g, m, v)


def _adamw_small(gparts, ws, ms, vs):
    n = len(SMALL)

    def body(g_ref, *refs):
        w_refs, m_refs, v_refs = refs[:n], refs[n:2 * n], refs[2 * n:3 * n]
        outs, sum_ref = refs[3 * n:-1], refs[-1]
        g = g_ref[0]
        for k in range(1, 8):
            g = g + g_ref[k]
        sum_ref[...] = g
        outs[0][...] = sum_ref[17:18, 0:128]
        for t, name in enumerate(SMALL):
            r0, nr, c0, nc = SMALL_SLOT[name]
            gt = sum_ref[r0:r0 + nr, c0:c0 + nc]
            out = (gt,) + _adam_math(w_refs[t][...], gt, m_refs[t][...], v_refs[t][...])
            for o_ref, val in zip(outs[1 + 4 * t:5 + 4 * t], out):
                o_ref[...] = val

    whole = lambda s: pl.BlockSpec(s, lambda i, nd=len(s): (0,) * nd)
    ins = [gparts] + list(ws) + list(ms) + list(vs)
    out_shapes = [(1, 128)] + [a.shape for a in ws for _ in range(4)]
    return _pcall(
        body, name="adamw_small", grid=(1,), in_specs=[whole(a.shape) for a in ins],
        out_specs=[whole(s) for s in out_shapes], out_shape=[jax.ShapeDtypeStruct(s, F32) for s in out_shapes],
        scratch_shapes=[pltpu.VMEM((SMALL_ROWS, D), F32)],
        compiler_params=pltpu.CompilerParams(dimension_semantics=("arbitrary",)),
    )(*ins)


def _row_tile(rows):
    return next(t for t in (512, 400, 320) if rows % t == 0)


def _add_halves(g4, theirs, core, name):
    rows = g4.shape[2]
    tr = _row_tile(rows)

    def body(c_ref, a_ref, b_ref, o_ref):
        o_ref[...] = (a_ref[...] + b_ref[...]).astype(BF16)

    grid_spec = pltpu.PrefetchScalarGridSpec(
        num_scalar_prefetch=1, grid=(4, rows // tr),
        in_specs=[pl.BlockSpec((None, None, tr, D), lambda j, i, c: (j, c[0], i, 0)),
                  pl.BlockSpec((None, None, tr, D), lambda j, i, c: (j, 0, i, 0))],
        out_specs=pl.BlockSpec((None, tr, D), lambda j, i, c: (j, i, 0)))
    return _pcall(
        body, name=name, grid_spec=grid_spec, out_shape=jax.ShapeDtypeStruct((4, rows, D), BF16),
        compiler_params=pltpu.CompilerParams(dimension_semantics=("arbitrary", "arbitrary")),
    )(core, g4, theirs)


def _sum_adam(own, got, order, r0, w, m, v, name, transposed=False):
    n = w.shape[1] if transposed else w.shape[0]
    tr = min(n, 256) if n % 8 == 0 else n
    rows = tr if n % 8 == 0 else own.shape[1]

    def body(o_ref, a_ref, b_ref, c_ref, d_ref, w_ref, m_ref, v_ref, g_ref, dl_ref, m2_ref, v2_ref):
        f = lambda r: r[0:tr, :].astype(F32)
        g = ((f(a_ref) + f(b_ref)) + f(c_ref)) + f(d_ref)
        g = g.T if transposed else g
        g_ref[...] = g
        dl_ref[...], m2_ref[...], v2_ref[...] = _adam_math(w_ref[...], g, m_ref[...], v_ref[...])

    slot = lambda k: pl.BlockSpec((None, rows, D), lambda i, o: (o[k], r0 // rows + i, 0))
    wspec = pl.BlockSpec((D, tr), lambda i, o: (0, i)) if transposed else pl.BlockSpec((tr, D), lambda i, o: (i, 0))
    grid_spec = pltpu.PrefetchScalarGridSpec(
        num_scalar_prefetch=1, grid=(n // tr,), in_specs=[slot(0), slot(1), slot(2), slot(3), wspec, wspec, wspec],
        out_specs=[wspec] * 4)
    return _pcall(
        body, name=name, grid_spec=grid_spec, out_shape=[jax.ShapeDtypeStruct(w.shape, F32)] * 4,
        compiler_params=pltpu.CompilerParams(dimension_semantics=("arbitrary",)),
    )(order, own, got, got, got, w, m, v)


def _sum_adam_rows(own, got, order, ws, ms, vs, name):
    n, rows = len(ws), ws[0].shape[0]

    def body(o_ref, a_ref, b_ref, c_ref, d_ref, *refs):
        ins, outs = refs[:3 * n], refs[3 * n:]
        for t in range(n):
            r = slice(t * rows, (t + 1) * rows)
            f = lambda ref: ref[r, :].astype(F32)
            g = ((f(a_ref) + f(b_ref)) + f(c_ref)) + f(d_ref)
            out = (g,) + _adam_math(ins[t][...], g, ins[n + t][...], ins[2 * n + t][...])
            for o, val in zip(outs[4 * t:4 * t + 4], out):
                o[...] = val

    slot = lambda k: pl.BlockSpec((None, n * rows, D), lambda i, o: (o[k], 0, 0), pipeline_mode=pl.Buffered(1))
    wspec = pl.BlockSpec((rows, D), lambda i, o: (0, 0), pipeline_mode=pl.Buffered(1))
    grid_spec = pltpu.PrefetchScalarGridSpec(
        num_scalar_prefetch=1, grid=(1,), in_specs=[slot(0), slot(1), slot(2), slot(3)] + [wspec] * (3 * n),
        out_specs=[pl.BlockSpec((rows, D), lambda i, o: (0, 0))] * (4 * n))
    return _pcall(
        body, name=name, grid_spec=grid_spec, out_shape=[jax.ShapeDtypeStruct((rows, D), F32)] * (4 * n),
        compiler_params=pltpu.CompilerParams(dimension_semantics=("arbitrary",), vmem_limit_bytes=VMEM_BIG),
    )(order, own, got, got, got, *ws, *ms, *vs)


def _sum_chips(own, got, order, name):
    rows = own.shape[1]
    tr = _row_tile(rows)

    def body(o_ref, a_ref, b_ref, c_ref, d_ref, out_ref):
        f = lambda r: r[...].astype(F32)
        out_ref[...] = ((f(a_ref) + f(b_ref)) + f(c_ref)) + f(d_ref)

    slot = lambda k: pl.BlockSpec((None, tr, D), lambda i, o: (o[k], i, 0))
    grid_spec = pltpu.PrefetchScalarGridSpec(
        num_scalar_prefetch=1, grid=(rows // tr,), in_specs=[slot(0), slot(1), slot(2), slot(3)],
        out_specs=pl.BlockSpec((tr, D), lambda i, o: (i, 0)))
    return _pcall(
        body, name=name, grid_spec=grid_spec, out_shape=jax.ShapeDtypeStruct((rows, D), F32),
        compiler_params=pltpu.CompilerParams(dimension_semantics=("arbitrary",)),
    )(order, own, got, got, got)


def _place():
    return lax.axis_index("x"), lax.axis_index("y"), lax.axis_index("c")


def _allgather(block, name):
    def body(x_ref, out_ref, token, send_sems, recv_sems, local_sem):
        token[...] = jnp.zeros_like(token)
        x, y, c = _place()
        me, sibling = (x, y, c), (x, y, 1 - c)
        chips = [(1 - x, y), (x, 1 - y), (1 - x, 1 - y)]

        def slot(px, py, pc):
            return out_ref.at[4 * px + 2 * py + pc]

        def copy(k, blk, to, src=None):
            return pltpu.make_async_remote_copy(
                src_ref=slot(*blk) if src is None else src, dst_ref=slot(*blk),
                send_sem=send_sems.at[k], recv_sem=recv_sems.at[k], device_id=to, device_id_type=MESH)

        mine = pltpu.make_async_copy(x_ref, slot(*me), local_sem)
        mine.start()
        first = [copy(0, me, sibling, src=x_ref)]
        first += [copy(1 + j, me, (*chip, c), src=x_ref) for j, chip in enumerate(chips)]
        for cp in first:
            cp.start()
        passed = [copy(4 + j, (*chip, c), sibling) for j, chip in enumerate(chips)]
        for j, chip in enumerate(chips):
            copy(1 + j, (*chip, c), me).wait_recv()
            passed[j].start()
        copy(0, sibling, me).wait_recv()
        for j, chip in enumerate(chips):
            copy(4 + j, (*chip, 1 - c), me).wait_recv()
        for cp in first + passed:
            cp.wait_send()
        mine.wait()

    return _pcall(
        body, name=name,
        out_shape=[jax.ShapeDtypeStruct((8,) + block.shape, block.dtype), jax.ShapeDtypeStruct((8, 128), F32)],
        in_specs=[pl.BlockSpec(memory_space=pl.ANY)],
        out_specs=[pl.BlockSpec(memory_space=pl.ANY), pl.BlockSpec(memory_space=pltpu.VMEM)],
        scratch_shapes=[pltpu.SemaphoreType.DMA((7,)), pltpu.SemaphoreType.DMA((7,)), pltpu.SemaphoreType.DMA(())],
        compiler_params=pltpu.CompilerParams(has_side_effects=True),
    )(block)


HBM_SPEC = pl.BlockSpec(memory_space=pltpu.HBM)
SEM_SPEC = pl.BlockSpec(memory_space=pltpu.SEMAPHORE)
ANY_SPEC = pl.BlockSpec(memory_space=pl.ANY)
EFFECT = pltpu.SideEffectType.DATAFLOW_SIDE_EFFECTING


def _in_hbm(a):
    return pltpu.with_memory_space_constraint(a, pltpu.HBM)


def _start_copies(name, src, land_shape, plan, n):
    def body(src_ref, land_ref, send_sems, recv_sems, src_thru, land_thru, token):
        for k, (s, d, to, _) in enumerate(plan(src_ref, land_ref)):
            pltpu.make_async_remote_copy(src_ref=s, dst_ref=d, send_sem=send_sems.at[k], recv_sem=recv_sems.at[k],
                                         device_id=to, device_id_type=MESH).start()
        token[...] = jnp.zeros_like(token)

    return _pcall(
        body, name=name,
        out_shape=(pltpu.SemaphoreType.DMA((n,)), pltpu.SemaphoreType.DMA((n,)), pltpu.HBM(src.shape, src.dtype),
                   pltpu.HBM(land_shape, src.dtype), jax.ShapeDtypeStruct((8, 128), F32)),
        in_specs=(HBM_SPEC, HBM_SPEC),
        out_specs=(SEM_SPEC, SEM_SPEC, HBM_SPEC, HBM_SPEC, pl.BlockSpec(memory_space=pltpu.VMEM)),
        input_output_aliases={0: 2, 1: 3}, compiler_params=pltpu.CompilerParams(has_side_effects=EFFECT),
    )(_in_hbm(src), _in_hbm(lax.empty(land_shape, src.dtype)))


def _wait_copies(name, started, after, plan):
    send_sems, recv_sems, src_thru, land_thru, _ = started

    def body(src_ref, land_ref, send_sems, recv_sems, *rest):
        for k, (s, _, to, mine) in enumerate(plan(src_ref, land_ref)):
            cp = pltpu.make_async_remote_copy(src_ref=s, dst_ref=mine, send_sem=send_sems.at[k],
                                              recv_sem=recv_sems.at[k], device_id=to, device_id_type=MESH)
            cp.wait_send()
            cp.wait_recv()

    return _pcall(
        body, name=name,
        out_shape=(pltpu.HBM(src_thru.shape, src_thru.dtype), pltpu.HBM(land_thru.shape, land_thru.dtype)),
        in_specs=(HBM_SPEC, HBM_SPEC, SEM_SPEC, SEM_SPEC) + (ANY_SPEC,) * len(after), out_specs=(HBM_SPEC, HBM_SPEC),
        input_output_aliases={0: 0, 1: 1}, compiler_params=pltpu.CompilerParams(has_side_effects=EFFECT),
    )(src_thru, land_thru, send_sems, recv_sems, *after)


def _gather_plan(src_ref, land_ref):
    x, y, c = _place()
    peers = [(x, y, 1 - c), (1 - x, y, c), (x, 1 - y, c), (1 - x, 1 - y, c)]
    return [(src_ref, land_ref.at[4 * x + 2 * y + c], p, land_ref.at[4 * p[0] + 2 * p[1] + p[2]]) for p in peers]


def _swap_plan(src_ref, land_ref):
    x, y, c = _place()
    return [(src_ref.at[:, pl.ds(1 - c, 1)], land_ref, (x, y, 1 - c), land_ref)]


def _exchange_plan(src_ref, land_ref):
    x, y, c = _place()
    chips = [(1 - x, y), (x, 1 - y), (1 - x, 1 - y)]
    return [(src_ref.at[2 * px + py], land_ref.at[2 * x + y], (px, py, c), land_ref.at[2 * px + py]) for px, py in chips]


def _gather_forward(land, block):
    def body(land_ref, out_ref, send_sems, recv_sems):
        x, y, c = _place()
        chips = [(1 - x, y), (x, 1 - y), (1 - x, 1 - y)]

        def copy(k, px, py, pc):
            blk = out_ref.at[4 * px + 2 * py + pc]
            return pltpu.make_async_remote_copy(src_ref=blk, dst_ref=blk, send_sem=send_sems.at[k],
                                                recv_sem=recv_sems.at[k], device_id=(x, y, 1 - c), device_id_type=MESH)

        sent = [copy(k, px, py, c) for k, (px, py) in enumerate(chips)]
        for cp in sent:
            cp.start()
        for k, (px, py) in enumerate(chips):
            copy(k, px, py, 1 - c).wait_recv()
        for cp in sent:
            cp.wait_send()

    land = _pcall(
        body, name="allgather_rest_forward", out_shape=jax.ShapeDtypeStruct(land.shape, land.dtype),
        in_specs=[ANY_SPEC], out_specs=ANY_SPEC, input_output_aliases={0: 0},
        scratch_shapes=[pltpu.SemaphoreType.DMA((3,)), pltpu.SemaphoreType.DMA((3,))],
        compiler_params=pltpu.CompilerParams(has_side_effects=True),
    )(land)

    rows = block.shape[0]
    tr = rows // 4

    def place(me_ref, x_ref, land_ref, out_ref):
        out_ref[...] = x_ref[...]

    x, y, c = _place()
    grid_spec = pltpu.PrefetchScalarGridSpec(
        num_scalar_prefetch=1, grid=(rows // tr,),
        in_specs=[pl.BlockSpec((tr, D), lambda i, me: (i, 0)), ANY_SPEC],
        out_specs=pl.BlockSpec((None, tr, D), lambda i, me: (me[0], i, 0)))
    return _pcall(
        place, name="allgather_rest_own", grid_spec=grid_spec, out_shape=jax.ShapeDtypeStruct(land.shape, land.dtype),
        input_output_aliases={2: 0}, compiler_params=pltpu.CompilerParams(dimension_semantics=("arbitrary",)),
    )((4 * x + 2 * y + c).reshape(1), block, land)


class _ReduceScatter:
    def __init__(self, name, g):
        self.name = name
        rows = g.shape[1]
        self.started = _start_copies(name + "_swap_start", g.reshape(4, 2, rows, D), (4, 1, rows, D), _swap_plan, 1)
        self.token = self.started[4][0, 0]

    def halfway(self, after):
        g4, theirs = _wait_copies(self.name + "_swap_wait", self.started, after, _swap_plan)
        self.own = _add_halves(g4, theirs, lax.axis_index("c").reshape(1), self.name + "_add_halves")
        self.started = _start_copies(self.name + "_exch_start", self.own, self.own.shape, _exchange_plan, 3)
        self.token = self.started[4][0, 0]

    def finish(self, after):
        own, got = _wait_copies(self.name + "_exch_wait", self.started, after, _exchange_plan)
        chip = 2 * lax.axis_index("x") + lax.axis_index("y")
        return own, got, (chip + jnp.arange(4, dtype=jnp.int32)) % 4


def _pack_small(p, scalar=None):
    z = lambda a, n: jnp.pad(a, ((0, 0), (0, n - a.shape[1])))
    rows = [z(p['rel_bias'], D), z(p['b_fgt'], D), jnp.concatenate([p['g_fox_out'], p['g_chk_out']], axis=1)]
    rows += [p[n] for n in ('g_mix_pre', 'g_mix_post', 'g_mem_kv', 'g_mem_pre', 'g_mem_post', 'g_ff_pre', 'g_ff_post')]
    rows.append(jnp.zeros((1, D), F32) if scalar is None else z(jnp.reshape(scalar, (1, 1)), D))
    rows.append(jnp.zeros((SMALL_ROWS - 18, D), F32))
    return jnp.concatenate(rows, axis=0)


_GAP_DEV, _GAP_ROW = divmod(GATE0 + 8, N_IN)
_GAP = CHK0 - GATE0 - 8


def _in_rows_to_proj(g):
    runs = [(j, 0, N_IN, N_IN * j) for j in range(_GAP_DEV)]
    runs += [(_GAP_DEV, 0, _GAP_ROW, N_IN * _GAP_DEV), (_GAP_DEV, _GAP_ROW, N_IN, N_IN * _GAP_DEV + _GAP_ROW + _GAP)]
    runs += [(j, 0, N_IN, N_IN * j + _GAP) for j in range(_GAP_DEV + 1, 8)]

    def body(g_ref, o_ref, acc_ref):
        acc_ref[...] = jnp.zeros_like(acc_ref)
        for j, r0, r1, dest in runs:
            start, shift = dest // 16 * 16, dest % 16
            win = -(-(shift + r1 - r0) // 16) * 16
            r = lax.broadcasted_iota(jnp.int32, (win, R_IN), 0)
            c = lax.broadcasted_iota(jnp.int32, (win, R_IN), 1)
            move = jnp.where((c >= r0) & (c < r1) & (r == c - r0 + shift), 1.0, 0.0).astype(BF16)
            acc_ref[start:start + win, :] += _nn(move, g_ref[j])
        o_ref[...] = acc_ref[...].astype(BF16)

    return _pcall(
        body, name="w_in_layout", out_shape=jax.ShapeDtypeStruct((PROJ, D), BF16), grid=(1,),
        in_specs=[pl.BlockSpec(g.shape, lambda i: (0, 0, 0))], out_specs=pl.BlockSpec((PROJ, D), lambda i: (0, 0)),
        scratch_shapes=[pltpu.VMEM((PROJ, D), F32)],
        compiler_params=pltpu.CompilerParams(dimension_semantics=("arbitrary",), vmem_limit_bytes=VMEM_BIG),
    )(g)


def _proj_rows_to_in(g):
    pad = lambda a: jnp.pad(a, ((0, R_IN - a.shape[0]), (0, 0)))
    lo = N_IN * _GAP_DEV
    shards = [pad(g[N_IN * j:N_IN * (j + 1)]) for j in range(_GAP_DEV)]
    shards.append(pad(jnp.concatenate([g[lo:lo + _GAP_ROW], g[lo + _GAP_ROW + _GAP:lo + N_IN + _GAP]], axis=0)))
    shards += [pad(g[N_IN * j + _GAP:N_IN * (j + 1) + _GAP]) for j in range(_GAP_DEV + 1, 8)]
    return jnp.stack(shards)


def _local_grads(x, mem, tgt, win_t, gw_of, sm, on_grads):
    b_pad = jnp.pad(sm['b_fgt'], ((0, 0), (0, 120)))
    tbl = jnp.pad(sm['rel_bias'], ((0, 0), (0, NREL_PAD - 257)))

    h1, proj, flog = _premix_fwd(x, sm['g_mix_pre'], win_t)
    c = _gate_fwd(flog, b_pad)
    ct3 = c[:, :8].T.reshape(4, 2, T)
    o_f, lse_f = _fox_fwd(proj, c, ct3)
    vt3 = _relvec_fwd(tbl).reshape(4, 2, VW)
    kvp = jnp.pad(proj[:, CHK0 + 512:], ((LEFT, 0), (0, 0)))
    o_c, lse_c = _chk_fwd(proj, kvp, vt3)
    gw = gw_of([o_f, o_c])
    w_out, w_mq, w_mk, w_mv, w_mo, w1_t, w2 = (_wblk(gw, n) for n in ('w_out', 'w_mq', 'w_mk', 'w_mv', 'w_mo', 'w_ff1', 'w_ff2'))
    ycat, z, x1, h2, qm = _postmix_fwd(x, o_f, o_c, sm['g_fox_out'], sm['g_chk_out'], w_out,
                                       sm['g_mix_post'], sm['g_mem_pre'], w_mq)
    memn, km, vm = _memkv_fwd(mem, sm['g_mem_kv'], w_mk, w_mv)
    om, ym, x2, h3 = _mem_fwd(qm, x1, km, vm, w_mo, sm['g_mem_post'], sm['g_ff_pre'])
    a, y3, dx3, loss_acc = _ffn_fwd(h3, x2, tgt, w1_t, w2, sm['g_ff_post'])

    gs = {}
    dx2, da, dy3, r, gs['g_ff_post'], gs['g_ff_pre'] = _ffn_bwd(dx3, y3, x2, a, w1_t, w2, sm['g_ff_post'], sm['g_ff_pre'])
    zero = on_grads('A', _wgrad_group("wgrad_ff", [(da, h3), (r, dy3)], 512), None)
    dx1, dym, dqm, dkm, dvm, gs['g_mem_post'], gs['g_mem_pre'] = _mem_bwd(
        dx2, ym, x1, qm, km, vm, w_mo, w_mq, sm['g_mem_post'] + zero, sm['g_mem_pre'])
    zero = on_grads('A halfway', None, [dx1])
    gs['g_mem_kv'] = _memkv_bwd(dkm, dvm, mem, w_mk, w_mv)
    dz, dof, doc, gs['g_mix_post'], gs['g_fox_out'], gs['g_chk_out'] = _postmix_bwd(
        dx1, z, o_f, o_c, w_out, sm['g_mix_post'] + zero, sm['g_fox_out'], sm['g_chk_out'])
    zero = on_grads('B', _wgrad_group("wgrad_mem_out", [(ycat, dz), (h2, dqm), (memn, dkm), (memn, dvm), (om, dym)], 128), None)
    dq_f, dk_f, dv_f, dct, dcq = _fox_bwd(proj, c, ct3 + zero, o_f, lse_f, dof)
    zero = on_grads('B halfway', None, [dq_f])
    dq_c, dk_c, dv_c, gv = _chk_bwd(proj, kvp, vt3 + zero, o_c, lse_c, doc)
    gs['rel_bias'] = _relvec_bwd(gv.reshape(8, VW))[:, :257]
    dc = jnp.pad(dct.reshape(8, T).T + dcq[:, :, :2].transpose(1, 0, 2).reshape(T, 8), ((0, 0), (0, 120)))
    dflog, db = _gate_bwd(dc, flog, b_pad)
    gs['b_fgt'] = db[0:1, :8]
    grad_x, dproj, gs['g_mix_pre'] = _premix_bwd(dx1, x, [dq_f, dk_f, dv_f, dflog, dq_c, dk_c, dv_c], win_t, sm['g_mix_pre'])
    on_grads('C', _proj_rows_to_in(_wgrad(dproj, h1, "wgrad_in")), None)
    return loss_acc[0, 0], grad_x, gs


def kernel(x, mem, w_in, b_fgt, rel_bias, g_fox_out, g_chk_out, w_out, g_mix_pre, g_mix_post, g_mem_kv, w_mq, w_mk, w_mv, w_mo, g_mem_pre, g_mem_post, w_ff1, w_ff2, g_ff_pre, g_ff_post, loss_target, m_w_in, m_b_fgt, m_rel_bias, m_g_fox_out, m_g_chk_out, m_w_out, m_g_mix_pre, m_g_mix_post, m_g_mem_kv, m_w_mq, m_w_mk, m_w_mv, m_w_mo, m_g_mem_pre, m_g_mem_post, m_w_ff1, m_w_ff2, m_g_ff_pre, m_g_ff_post, v_w_in, v_b_fgt, v_rel_bias, v_g_fox_out, v_g_chk_out, v_w_out, v_g_mix_pre, v_g_mix_post, v_g_mem_kv, v_w_mq, v_w_mk, v_w_mv, v_w_mo, v_g_mem_pre, v_g_mem_post, v_w_ff1, v_w_ff2, v_g_ff_pre, v_g_ff_post):
    args = dict(locals())
    two_d = lambda a: a.reshape(a.shape[-2:])
    w = {n: two_d(args[n]) for n in WEIGHTS}
    m = {n: two_d(args['m_' + n]) for n in WEIGHTS}
    v = {n: two_d(args['v_' + n]) for n in WEIGHTS}

    sm = {n: w[n] for n in SMALL}
    shard_in = jnp.pad(w['w_in'].T, ((0, R_IN - N_IN), (0, 0))).astype(BF16)
    gathered_in, zero = _allgather(shard_in, "allgather_w_in")
    win_t = _in_rows_to_proj(gathered_in)
    shard_rest = (jnp.concatenate([w['w_ff1'].T, w['w_ff2'], w['w_out'], w['w_mq'], w['w_mk'], w['w_mv'], w['w_mo']],
                                  axis=0) + zero[0, 0]).astype(BF16)
    rest = _start_copies("allgather_rest_start", shard_rest, (8, R_REST, D), _gather_plan, 4)
    sm['g_mix_pre'] = sm['g_mix_pre'] + rest[4][0, 0]

    def gw_of(after):
        block, land = _wait_copies("allgather_rest_wait", rest, after, _gather_plan)
        return _gather_forward(land, block)

    rs = {}

    def on_grads(stage, g, after):
        if stage.endswith('halfway'):
            rs[stage[0]].halfway(after)
            return rs[stage[0]].token
        rs[stage] = _ReduceScatter("rs_" + stage.lower(), g)
        return rs[stage].token

    loss_local, grad_x, gs = _local_grads(x[0], mem[0], loss_target[0], win_t, gw_of, sm, on_grads)
    grads, deltas, new_m, new_v = {}, {}, {}, {}

    def update(n, out):
        grads[n], deltas[n], new_m[n], new_v[n] = out

    packed = _pack_small(gs, loss_local + rs['C'].token)
    rs['C'].halfway([packed])
    gparts, _ = _allgather(packed + rs['C'].token, "allgather_small_grads")
    small = _adamw_small(gparts, [w[n] for n in SMALL], [m[n] for n in SMALL], [v[n] for n in SMALL])
    loss = small[0][0, 0]
    for t, n in enumerate(SMALL):
        update(n, small[1 + 4 * t:5 + 4 * t])

    own, got, order = rs['A'].finish([grad_x, small[0]])
    update('w_ff1', _sum_adam(own, got, order, 0, w['w_ff1'], m['w_ff1'], v['w_ff1'], "adamw_w_ff1", transposed=True))
    update('w_ff2', _sum_adam(own, got, order, 512, w['w_ff2'], m['w_ff2'], v['w_ff2'], "adamw_w_ff2"))
    own, got, order = rs['B'].finish([grad_x, small[0]])
    names_b = ('w_out', 'w_mq', 'w_mk', 'w_mv', 'w_mo')
    done = _sum_adam_rows(own, got, order, [w[n] for n in names_b], [m[n] for n in names_b], [v[n] for n in names_b],
                          "adamw_group_b")
    for k, n in enumerate(names_b):
        update(n, done[4 * k:4 * k + 4])

    own, got, order = rs['C'].finish([new_v[n] for n in BIG if n != 'w_in'])
    done = _sum_adam(own, got, order, 0, w['w_in'].T, m['w_in'].T, v['w_in'].T, "adamw_w_in")
    update('w_in', [a.T for a in done])

    out = [loss, grad_x[None]]
    for group in (grads, deltas, new_m, new_v):
        out += [group[n].reshape(args[n].shape) for n in WEIGHTS]
    return tuple(out)
```

```python
import functools

import jax
import jax.numpy as jnp
from jax import lax
from jax.experimental import pallas as pl
from jax.experimental.pallas import tpu as pltpu

F32 = jnp.float32
BF16 = jnp.bfloat16
MESH = pl.DeviceIdType.MESH

T = 2048
D = 1024
NMEM = 256
DFF = 4096
EPS = 1e-6
TM = 256
TM_WIDE = 512
TQ = 256
FQ = 512
HD = 64
SCALE = HD ** -0.5
MEM_HEADS = 4
MEM_HD = 256
MEM_SCALE = MEM_HD ** -0.5
NEG = -1e30
LEFT = 512
WIN = LEFT + TQ
VW = 1024
NREL_PAD = 384
PROJ = 3200
GATE0 = 1536
CHK0 = 1664
VMEM_BIG = 56 * 1024 * 1024

ADAM_LR = 0.001
ADAM_B1 = 0.9
ADAM_B2 = 0.999
ADAM_EPS = 1e-08
ADAM_WD = 0.01
ADAM_STEP = 10

N_IN = 385
R_IN = 400
R_REST = 1664
W_ROWS = {'w_ff1': (0, 512), 'w_ff2': (512, 512),
          'w_out': (1024, 128), 'w_mq': (1152, 128), 'w_mk': (1280, 128), 'w_mv': (1408, 128), 'w_mo': (1536, 128)}
R_A, R_B = 1024, 640
SMALL_ROWS = 24
SMALL_SLOT = {'rel_bias': (0, 8, 0, 257), 'b_fgt': (8, 1, 0, 8), 'g_fox_out': (9, 1, 0, 512), 'g_chk_out': (9, 1, 512, 512),
              'g_mix_pre': (10, 1, 0, 1024), 'g_mix_post': (11, 1, 0, 1024), 'g_mem_kv': (12, 1, 0, 1024),
              'g_mem_pre': (13, 1, 0, 1024), 'g_mem_post': (14, 1, 0, 1024), 'g_ff_pre': (15, 1, 0, 1024),
              'g_ff_post': (16, 1, 0, 1024)}

WEIGHTS = ['w_in', 'b_fgt', 'rel_bias', 'g_fox_out', 'g_chk_out', 'w_out', 'g_mix_pre', 'g_mix_post', 'g_mem_kv',
           'w_mq', 'w_mk', 'w_mv', 'w_mo', 'g_mem_pre', 'g_mem_post', 'w_ff1', 'w_ff2', 'g_ff_pre', 'g_ff_post']
BIG = ['w_in', 'w_out', 'w_mq', 'w_mk', 'w_mv', 'w_mo', 'w_ff1', 'w_ff2']
SMALL = [n for n in WEIGHTS if n not in BIG]


def _pcall(body, **kw):
    return pl.pallas_call(body, **kw)


def _nn(a, b):
    return jnp.dot(a, b, preferred_element_type=F32)


def _nt(a, b):
    return lax.dot_general(a, b, (((1,), (1,)), ((), ())), preferred_element_type=F32)


def _tn(a, b):
    return lax.dot_general(a, b, (((0,), (0,)), ((), ())), preferred_element_type=F32)


def _w(ref):
    v = ref[...]
    return v if v.ndim == 2 else v.reshape(-1, v.shape[-1])


def _rstd(x):
    return lax.rsqrt(jnp.mean(x * x, axis=-1, keepdims=True) + EPS)


def _rms(x, g):
    return x * _rstd(x) * g


def _rms_bwd(x, g, dy):
    r = _rstd(x)
    xh = x * r
    dg = jnp.sum(dy * xh, axis=0, keepdims=True)
    dxh = dy * g
    dx = r * (dxh - xh * jnp.mean(dxh * xh, axis=-1, keepdims=True))
    return dx, dg


def _resident(a):
    if isinstance(a, tuple):
        _, shape, index = a
        return pl.BlockSpec(shape, lambda *_: index, pipeline_mode=pl.Buffered(1))
    return pl.BlockSpec(a.shape, lambda *_, nd=a.ndim: (0,) * nd, pipeline_mode=pl.Buffered(1))


def _wblk(gw, name):
    r0, rows = W_ROWS[name]
    return (gw, (8, rows, D), (0, r0 // rows, 0))


def _tok_call(body, name, tiled, full, outs_tiled, outs_acc=(), rows=T, tm=TM, vmem=None):
    in_specs = [pl.BlockSpec((tm, a.shape[1]), lambda i: (i, 0)) for a in tiled]
    in_specs += [_resident(a) for a in full]
    full = [a[0] if isinstance(a, tuple) else a for a in full]
    out_shape = [jax.ShapeDtypeStruct((rows, c), dt) for c, dt in outs_tiled]
    out_shape += [jax.ShapeDtypeStruct(s, F32) for s in outs_acc]
    out_specs = [pl.BlockSpec((tm, c), lambda i: (i, 0)) for c, _ in outs_tiled]
    out_specs += [pl.BlockSpec(s, lambda i, nd=len(s): (0,) * nd) for s in outs_acc]
    return _pcall(
        body, name=name, grid=(rows // tm,), in_specs=in_specs, out_specs=out_specs, out_shape=out_shape,
        compiler_params=pltpu.CompilerParams(dimension_semantics=("arbitrary",), vmem_limit_bytes=vmem),
    )(*tiled, *full)


def _one_call(body, name, ins, outs, vmem=None):
    whole = lambda s: pl.BlockSpec(s, lambda i, nd=len(s): (0,) * nd)
    return _pcall(
        body, name=name, grid=(1,), in_specs=[_resident(a) for a in ins], out_specs=[whole(s) for s, _ in outs],
        out_shape=[jax.ShapeDtypeStruct(s, dt) for s, dt in outs],
        compiler_params=pltpu.CompilerParams(dimension_semantics=("arbitrary",), vmem_limit_bytes=vmem),
    )(*[a[0] if isinstance(a, tuple) else a for a in ins])


def _premix_fwd(x, g_pre, win_t):
    def body(x_ref, g_ref, w_ref, h_ref, proj_ref, flog_ref):
        h = _rms(x_ref[...], g_ref[...]).astype(BF16)
        h_ref[...] = h
        p = _nt(h, w_ref[...])
        proj_ref[...] = p.astype(BF16)
        flog_ref[...] = p[:, GATE0:GATE0 + 128]

    return _tok_call(body, "premix_fwd", [x], [g_pre, win_t],
                     [(D, BF16), (PROJ, BF16), (128, F32)], tm=TM_WIDE, vmem=VMEM_BIG)


def _postmix_fwd(x, o_f, o_c, g_fo, g_co, w_out, g_post, g_mpre, w_mq):
    def body(x_ref, of_ref, oc_ref, gfo_ref, gco_ref, wo_ref, gp_ref, gm_ref, wq_ref,
             y_ref, z_ref, x1_ref, h2_ref, qm_ref):
        y_ref[:, :512] = _rms(of_ref[...], gfo_ref[...]).astype(BF16)
        y_ref[:, 512:] = _rms(oc_ref[...], gco_ref[...]).astype(BF16)
        z = _nn(y_ref[...], _w(wo_ref))
        z_ref[...] = z
        x1 = x_ref[...] + _rms(z, gp_ref[...])
        x1_ref[...] = x1
        h2 = _rms(x1, gm_ref[...]).astype(BF16)
        h2_ref[...] = h2
        qm_ref[...] = _nn(h2, _w(wq_ref)).astype(BF16)

    return _tok_call(body, "postmix_fwd", [x, o_f, o_c], [g_fo, g_co, w_out, g_post, g_mpre, w_mq],
                     [(D, BF16), (D, F32), (D, F32), (D, BF16), (D, BF16)], tm=TM_WIDE, vmem=VMEM_BIG)


def _memkv_fwd(mem, g_kv, w_mk, w_mv):
    def body(m_ref, g_ref, wk_ref, wv_ref, mn_ref, k_ref, v_ref):
        mn = _rms(m_ref[...], g_ref[...]).astype(BF16)
        mn_ref[...] = mn
        k_ref[...] = _nn(mn, _w(wk_ref)).astype(BF16)
        v_ref[...] = _nn(mn, _w(wv_ref)).astype(BF16)

    return _tok_call(body, "memkv_fwd", [mem], [g_kv, w_mk, w_mv],
                     [(D, BF16), (D, BF16), (D, BF16)], rows=NMEM, tm=NMEM, vmem=VMEM_BIG)


def _mem_fwd(qm, x1, km, vm, w_mo, g_post, g_fpre):
    def body(q_ref, x1_ref, k_ref, v_ref, wo_ref, gp_ref, gf_ref, om_ref, ym_ref, x2_ref, h3_ref):
        for h in range(MEM_HEADS):
            sl = slice(h * MEM_HD, (h + 1) * MEM_HD)
            s = _nt(q_ref[:, sl], k_ref[:, sl]) * MEM_SCALE
            p = jnp.exp(s - jnp.max(s, axis=-1, keepdims=True))
            p = p / jnp.sum(p, axis=-1, keepdims=True)
            om_ref[:, sl] = _nn(p.astype(BF16), v_ref[:, sl]).astype(BF16)
        ym = _nn(om_ref[...], _w(wo_ref))
        ym_ref[...] = ym
        x2 = x1_ref[...] + _rms(ym, gp_ref[...])
        x2_ref[...] = x2
        h3_ref[...] = _rms(x2, gf_ref[...]).astype(BF16)

    return _tok_call(body, "mem_fwd", [qm, x1], [km, vm, w_mo, g_post, g_fpre],
                     [(D, BF16), (D, F32), (D, F32), (D, BF16)], tm=TM_WIDE, vmem=VMEM_BIG)


def _ffn_fwd(h3, x2, tgt, w1_t, w2, g_post):
    def body(h_ref, x2_ref, t_ref, w1_ref, w2_ref, g_ref, a_ref, y_ref, dx_ref, loss_ref):
        @pl.when(pl.program_id(0) == 0)
        def _():
            loss_ref[...] = jnp.zeros_like(loss_ref)

        h = h_ref[...]
        y = jnp.zeros((TM_WIDE, D), F32)
        for c in range(4):
            cols = slice(c * (DFF // 4), (c + 1) * (DFF // 4))
            a = _nt(h, w1_ref[2 * c:2 * c + 2].reshape(DFF // 4, D))
            a_ref[:, cols] = a.astype(BF16)
            y = y + _nn(jnp.square(jnp.maximum(a, 0.0)).astype(BF16), w2_ref[2 * c:2 * c + 2].reshape(DFF // 4, D))
        y_ref[...] = y
        e = x2_ref[...] + _rms(y, g_ref[...]) - t_ref[...]
        dx_ref[...] = e * (1.0 / D)
        loss_ref[...] += 0.5 * jnp.sum(jnp.sum(e * e, axis=-1, keepdims=True) * (1.0 / D))

    return _tok_call(body, "ffn_fwd", [h3, x2, tgt], [w1_t, w2, g_post],
                     [(DFF, BF16), (D, F32), (D, F32)], [(8, 128)], tm=TM_WIDE, vmem=VMEM_BIG)


def _tri(lower):
    r = lax.broadcasted_iota(jnp.int32, (128, 128), 0)
    c = lax.broadcasted_iota(jnp.int32, (128, 128), 1)
    return jnp.where(r >= c if lower else c >= r, 1.0, 0.0).astype(F32)


def _hdot(a, b):
    return jnp.dot(a, b, preferred_element_type=F32, precision=lax.Precision.HIGHEST)


def _gate_fwd(flog, b_pad):
    def body(f_ref, b_ref, c_ref):
        tri = _tri(True)

        def step(i, carry):
            rows = pl.ds(pl.multiple_of(i * 128, 128), 128)
            z = f_ref[rows, :] + b_ref[...]
            lf = jnp.minimum(z, 0.0) - jnp.log(1.0 + jnp.exp(-jnp.abs(z)))
            cb = _hdot(tri, lf) + carry
            c_ref[rows, :] = cb
            return cb[127:128, :]

        lax.fori_loop(0, T // 128, step, jnp.zeros((1, 128), F32))

    return _one_call(body, "gate_fwd", [flog, b_pad], [((T, 128), F32)])[0]


def _gate_bwd(dc, flog, b_pad):
    def body(dc_ref, f_ref, b_ref, df_ref, db_ref):
        tri = _tri(False)

        def step(j, carry):
            run, db = carry
            i = T // 128 - 1 - j
            rows = pl.ds(pl.multiple_of(i * 128, 128), 128)
            dcb = dc_ref[rows, :]
            rb = _hdot(tri, dcb) + run
            z = f_ref[rows, :] + b_ref[...]
            df = rb * (1.0 / (1.0 + jnp.exp(z)))
            df_ref[rows, :] = df.astype(BF16)
            return run + jnp.sum(dcb, axis=0, keepdims=True), db + jnp.sum(df, axis=0, keepdims=True)

        _, db = lax.fori_loop(0, T // 128, step, (jnp.zeros((1, 128), F32), jnp.zeros((1, 128), F32)))
        db_ref[...] = jnp.broadcast_to(db, (8, 128))

    return _one_call(body, "gate_bwd", [dc, flog, b_pad], [((T, 128), BF16), ((8, 128), F32)])


def _lane_lo(rows=TQ):
    return lax.broadcasted_iota(jnp.int32, (rows, 128), 1) < HD


def _half(v, lo, a, scale=None):
    keep = lo if a == 0 else jnp.logical_not(lo)
    v = v.astype(F32) if scale is None else v.astype(F32) * scale
    return jnp.where(keep, v, 0.0).astype(BF16)


def _fox_specs():
    return [pl.BlockSpec((FQ, 128), lambda h, i: (i, h)),
            pl.BlockSpec((T, 128), lambda h, i: (0, 4 + h)),
            pl.BlockSpec((T, 128), lambda h, i: (0, 8 + h))]


def _lane_pick(x, at):
    lane = lax.broadcasted_iota(jnp.int32, x.shape, 1)
    return jnp.sum(jnp.where(lane == at, x, 0.0), axis=-1, keepdims=True)


def _fox_fwd(proj, c, ct3):
    def body(q_ref, k_ref, v_ref, c_ref, ct_ref, o_ref, l_ref):
        i = pl.program_id(1)
        lo = _lane_lo(FQ)
        causal = lax.broadcasted_iota(jnp.int32, (FQ, FQ), 1) <= lax.broadcasted_iota(jnp.int32, (FQ, FQ), 0)
        q = q_ref[...]
        qs = [_half(q, lo, a, SCALE) for a in range(2)]
        cqs = [_lane_pick(c_ref[...], 2 * pl.program_id(0) + a) for a in range(2)]

        def tile(off, carry, diagonal):
            kblk = k_ref[pl.ds(off, FQ), :]
            vblk = v_ref[pl.ds(off, FQ), :]
            new = []
            for a in range(2):
                m, l, acc = carry[a]
                s = _nt(qs[a], kblk) + (cqs[a] - ct_ref[a:a + 1, pl.ds(off, FQ)])
                if diagonal:
                    s = jnp.where(causal, s, NEG)
                m2 = jnp.maximum(m, jnp.max(s, axis=-1, keepdims=True))
                p = jnp.exp(s - m2)
                alpha = jnp.exp(m - m2)
                new.append((m2, alpha * l + jnp.sum(p, axis=-1, keepdims=True),
                            alpha * acc + _nn(p.astype(BF16), vblk)))
            return tuple(new)

        init = (jnp.full((FQ, 1), NEG, F32), jnp.zeros((FQ, 1), F32), jnp.zeros((FQ, 128), F32))
        carry = lax.fori_loop(0, i, lambda kb, c: tile(pl.multiple_of(kb * FQ, FQ), c, False), (init, init))
        carry = tile(pl.multiple_of(i * FQ, FQ), carry, True)
        outs = []
        for a in range(2):
            m, l, acc = carry[a]
            outs.append(acc / l)
            l_ref[:, 128 * a:128 * a + 128] = jnp.broadcast_to(m + jnp.log(l), (FQ, 128))
        o_ref[...] = jnp.where(lo, outs[0], outs[1])

    return _pcall(
        body, name="fox_fwd", grid=(4, T // FQ),
        in_specs=_fox_specs() + [pl.BlockSpec((FQ, 128), lambda h, i: (i, 0)),
                                 pl.BlockSpec((None, 2, T), lambda h, i: (h, 0, 0))],
        out_specs=[pl.BlockSpec((FQ, 128), lambda h, i: (i, h)), pl.BlockSpec((FQ, 256), lambda h, i: (i, h))],
        out_shape=[jax.ShapeDtypeStruct((T, 512), F32), jax.ShapeDtypeStruct((T, 1024), F32)],
        compiler_params=pltpu.CompilerParams(dimension_semantics=("arbitrary", "arbitrary"), vmem_limit_bytes=VMEM_BIG),
    )(proj, proj, proj, c, ct3)


def _fox_bwd(proj, c, ct3, o, lse, do):
    def body(q_ref, k_ref, v_ref, c_ref, ct_ref, o_ref, l_ref, do_ref, dq_ref, dkb_ref, dvb_ref, dct_ref, dcq_ref,
             dk_ref, dv_ref):
        i = pl.program_id(1)

        @pl.when(i == 0)
        def _():
            dk_ref[...] = jnp.zeros_like(dk_ref)
            dv_ref[...] = jnp.zeros_like(dv_ref)
            dct_ref[...] = jnp.zeros_like(dct_ref)

        lo = _lane_lo(FQ)
        causal = lax.broadcasted_iota(jnp.int32, (FQ, FQ), 1) <= lax.broadcasted_iota(jnp.int32, (FQ, FQ), 0)
        q = q_ref[...]
        do_v = do_ref[...]
        prod = do_v * o_ref[...]
        qs = [_half(q, lo, a, SCALE) for a in range(2)]
        dos = [_half(do_v, lo, a) for a in range(2)]
        deltas = [jnp.sum(jnp.where(lo if a == 0 else jnp.logical_not(lo), prod, 0.0), axis=-1, keepdims=True)
                  for a in range(2)]
        cqs = [_lane_pick(c_ref[...], 2 * pl.program_id(0) + a) for a in range(2)]
        las = [l_ref[:, 128 * a:128 * a + 1] for a in range(2)]

        def tile(off, carry, diagonal):
            kblk = k_ref[pl.ds(off, FQ), :]
            vblk = v_ref[pl.ds(off, FQ), :]
            new = []
            dk = jnp.zeros((128, FQ), F32)
            dv = jnp.zeros((128, FQ), F32)
            for a in range(2):
                dq_acc, rs = carry[a]
                s = _nt(qs[a], kblk) + (cqs[a] - ct_ref[a:a + 1, pl.ds(off, FQ)])
                if diagonal:
                    s = jnp.where(causal, s, NEG)
                p = jnp.exp(s - las[a])
                ds = p * (_nt(dos[a], vblk) - deltas[a])
                dsb = ds.astype(BF16)
                dk = dk + _tn(qs[a], dsb)
                dv = dv + _tn(dos[a], p.astype(BF16))
                dct_ref[a:a + 1, pl.ds(off, FQ)] -= jnp.sum(ds, axis=0, keepdims=True)
                new.append((dq_acc + _nn(dsb, kblk), rs + jnp.sum(ds, axis=-1, keepdims=True)))
            dk_ref[:, pl.ds(off, FQ)] += dk
            dv_ref[:, pl.ds(off, FQ)] += dv
            return tuple(new)

        init = (jnp.zeros((FQ, 128), F32), jnp.zeros((FQ, 1), F32))
        carry = lax.fori_loop(0, i, lambda kb, c: tile(pl.multiple_of(kb * FQ, FQ), c, False), (init, init))
        carry = tile(pl.multiple_of(i * FQ, FQ), carry, True)
        lane = lax.broadcasted_iota(jnp.int32, (FQ, 128), 1)
        dcq_ref[...] = jnp.where(lane == 0, carry[0][1], jnp.where(lane == 1, carry[1][1], 0.0))
        dq_ref[...] = (jnp.where(lo, carry[0][0], carry[1][0]) * SCALE).astype(BF16)

        @pl.when(i == T // FQ - 1)
        def _():
            dkb_ref[...] = dk_ref[...].T.astype(BF16)
            dvb_ref[...] = dv_ref[...].T.astype(BF16)

    blk = pl.BlockSpec((FQ, 128), lambda h, i: (i, h))
    wide = pl.BlockSpec((FQ, 256), lambda h, i: (i, h))
    rows = pl.BlockSpec((None, 2, T), lambda h, i: (h, 0, 0))
    col = pl.BlockSpec((T, 128), lambda h, i: (0, h))
    return _pcall(
        body, name="fox_bwd", grid=(4, T // FQ),
        in_specs=_fox_specs() + [pl.BlockSpec((FQ, 128), lambda h, i: (i, 0)), rows, blk, wide, blk],
        out_specs=[blk, col, col, rows, pl.BlockSpec((None, FQ, 128), lambda h, i: (h, i, 0))],
        out_shape=[jax.ShapeDtypeStruct((T, 512), BF16), jax.ShapeDtypeStruct((T, 512), BF16),
                   jax.ShapeDtypeStruct((T, 512), BF16), jax.ShapeDtypeStruct((4, 2, T), F32),
                   jax.ShapeDtypeStruct((4, T, 128), F32)],
        scratch_shapes=[pltpu.VMEM((128, T), F32), pltpu.VMEM((128, T), F32)],
        compiler_params=pltpu.CompilerParams(dimension_semantics=("arbitrary", "arbitrary"), vmem_limit_bytes=VMEM_BIG),
    )(proj, proj, proj, c, ct3, o, lse, do)


AUG_ROWSUM, AUG_COLSUM = 67, 64


def _fox_prep(proj, c2):
    def body(q_ref, k_ref, c_ref, qa_ref, ka_ref):
        lane = lax.broadcasted_iota(jnp.int32, (FQ, 128), 1)
        q = q_ref[...].astype(F32) * SCALE
        k = k_ref[...].astype(F32)
        for a in range(2):
            c = c_ref[:, 128 * a:128 * a + 128]
            hi = c.astype(BF16).astype(F32)
            mid = (c - hi).astype(BF16).astype(F32)
            lo = c - hi - mid
            parts = lambda first, sign: jnp.where(lane == first, sign * hi, jnp.where(lane == first + 1, sign * mid, sign * lo))
            qd = q if a == 0 else pltpu.roll(q, 64, 1)
            kd = k if a == 0 else pltpu.roll(k, 64, 1)
            ones = jnp.ones((FQ, 128), F32)
            qa = jnp.where(lane < 64, qd, jnp.where(lane < 67, ones, jnp.where(lane < 70, parts(67, 1.0), 0.0)))
            ka = jnp.where(lane < 64, kd, jnp.where(lane < 67, parts(64, -1.0), jnp.where(lane < 70, ones, 0.0)))
            qa_ref[:, 128 * a:128 * a + 128] = qa.astype(BF16)
            ka_ref[:, 128 * a:128 * a + 128] = ka.astype(BF16)

    wide = pl.BlockSpec((FQ, 256), lambda h, i: (i, h))
    return _pcall(
        body, name="fox_prep", grid=(4, T // FQ),
        in_specs=[pl.BlockSpec((FQ, 128), lambda h, i: (i, h)), pl.BlockSpec((FQ, 128), lambda h, i: (i, 4 + h)), wide],
        out_specs=[wide, wide], out_shape=[jax.ShapeDtypeStruct((T, 1024), BF16)] * 2,
        compiler_params=pltpu.CompilerParams(dimension_semantics=("arbitrary", "arbitrary")),
    )(proj, proj, c2)


def _fox_aug_specs():
    return [pl.BlockSpec((FQ, 256), lambda h, i: (i, h)), pl.BlockSpec((T, 256), lambda h, i: (0, h)),
            pl.BlockSpec((T, 128), lambda h, i: (0, 8 + h))]


def _causal():
    return lax.broadcasted_iota(jnp.int32, (FQ, FQ), 1) <= lax.broadcasted_iota(jnp.int32, (FQ, FQ), 0)


def _fox_fwd_aug(q_aug, k_aug, proj):
    def body(q_ref, k_ref, v_ref, o_ref, l_ref):
        i = pl.program_id(1)
        lo = _lane_lo(FQ)
        causal = _causal()
        qs = [q_ref[:, 128 * a:128 * a + 128] for a in range(2)]

        def tile(off, carry, diagonal):
            vblk = v_ref[pl.ds(off, FQ), :]
            new = []
            for a in range(2):
                m, l, acc = carry[a]
                s = _nt(qs[a], k_ref[pl.ds(off, FQ), 128 * a:128 * a + 128])
                if diagonal:
                    s = jnp.where(causal, s, NEG)
                m2 = jnp.maximum(m, jnp.max(s, axis=-1, keepdims=True))
                p = jnp.exp(s - m2)
                alpha = jnp.exp(m - m2)
                new.append((m2, alpha * l + jnp.sum(p, axis=-1, keepdims=True),
                            alpha * acc + _nn(p.astype(BF16), vblk)))
            return tuple(new)

        init = (jnp.full((FQ, 1), NEG, F32), jnp.zeros((FQ, 1), F32), jnp.zeros((FQ, 128), F32))
        carry = lax.fori_loop(0, i, lambda kb, c: tile(pl.multiple_of(kb * FQ, FQ), c, False), (init, init))
        carry = tile(pl.multiple_of(i * FQ, FQ), carry, True)
        outs = []
        for a in range(2):
            m, l, acc = carry[a]
            outs.append(acc / l)
            l_ref[:, 128 * a:128 * a + 128] = jnp.broadcast_to(m + jnp.log(l), (FQ, 128))
        o_ref[...] = jnp.where(lo, outs[0], outs[1])

    return _pcall(
        body, name="fox_fwd", grid=(4, T // FQ), in_specs=_fox_aug_specs(),
        out_specs=[pl.BlockSpec((FQ, 128), lambda h, i: (i, h)), pl.BlockSpec((FQ, 256), lambda h, i: (i, h))],
        out_shape=[jax.ShapeDtypeStruct((T, 512), F32), jax.ShapeDtypeStruct((T, 1024), F32)],
        compiler_params=pltpu.CompilerParams(dimension_semantics=("arbitrary", "arbitrary"), vmem_limit_bytes=VMEM_BIG),
    )(q_aug, k_aug, proj)


def _fox_bwd_aug(q_aug, k_aug, proj, o, lse, do, after):
    nq = T // FQ

    def body(q_ref, k_ref, v_ref, o_ref, l_ref, do_ref, after_ref, dq_ref, dkb_ref, dvb_ref, dcq_ref, dck_ref, dk_ref,
             dv_ref):
        i = pl.program_id(1)

        @pl.when(i == 0)
        def _():
            dk_ref[...] = jnp.zeros_like(dk_ref)
            dv_ref[...] = jnp.zeros_like(dv_ref)

        lo = _lane_lo(FQ)
        lane = lax.broadcasted_iota(jnp.int32, (FQ, 128), 1)
        causal = _causal()
        do_v = do_ref[...]
        prod = do_v * o_ref[...]
        qs = [q_ref[:, 128 * a:128 * a + 128] for a in range(2)]
        dos = [_half(do_v, lo, a) for a in range(2)]
        deltas = [jnp.sum(jnp.where(lo if a == 0 else jnp.logical_not(lo), prod, 0.0), axis=-1, keepdims=True)
                  for a in range(2)]
        las = [l_ref[:, 128 * a:128 * a + 1] for a in range(2)]

        def tile(off, carry, diagonal):
            vblk = v_ref[pl.ds(off, FQ), :]
            new = []
            dv = jnp.zeros((FQ, 128), F32)
            for a in range(2):
                kblk = k_ref[pl.ds(off, FQ), 128 * a:128 * a + 128]
                s = _nt(qs[a], kblk)
                if diagonal:
                    s = jnp.where(causal, s, NEG)
                p = jnp.exp(s - las[a])
                dsb = (p * (_nt(dos[a], vblk) - deltas[a])).astype(BF16)
                dk_ref[a, pl.ds(off, FQ), :] += _tn(dsb, qs[a])
                dv = dv + _tn(p.astype(BF16), dos[a])
                new.append(carry[a] + _nn(dsb, kblk))
            dv_ref[pl.ds(off, FQ), :] += dv
            return tuple(new)

        init = jnp.zeros((FQ, 128), F32)
        dqs = lax.fori_loop(0, i, lambda kb, c: tile(pl.multiple_of(kb * FQ, FQ), c, False), (init, init))
        dqs = tile(pl.multiple_of(i * FQ, FQ), dqs, True)
        pick = lambda x, at: jnp.sum(jnp.where(lane == at, x, 0.0), axis=-1, keepdims=True)
        for a in range(2):
            dcq_ref[:, 128 * a:128 * a + 128] = jnp.broadcast_to(pick(dqs[a], AUG_ROWSUM), (FQ, 128))
        dq_ref[...] = (jnp.where(lo, dqs[0], pltpu.roll(dqs[1], 64, 1)) * SCALE).astype(BF16)

        @pl.when(i == nq - 1)
        def _():
            big_lane = lax.broadcasted_iota(jnp.int32, (T, 128), 1)
            dkb_ref[...] = jnp.where(big_lane < 64, dk_ref[0], pltpu.roll(dk_ref[1], 64, 1)).astype(BF16)
            dvb_ref[...] = dv_ref[...].astype(BF16)
            for a in range(2):
                col = jnp.sum(jnp.where(big_lane == AUG_COLSUM, dk_ref[a], 0.0), axis=-1, keepdims=True)
                dck_ref[:, 128 * a:128 * a + 128] = jnp.broadcast_to(-col, (T, 128))

    blk = pl.BlockSpec((FQ, 128), lambda h, i: (i, h))
    wide = pl.BlockSpec((FQ, 256), lambda h, i: (i, h))
    col = pl.BlockSpec((T, 128), lambda h, i: (0, h))
    colwide = pl.BlockSpec((T, 256), lambda h, i: (0, h))
    return _pcall(
        body, name="fox_bwd", grid=(4, nq), in_specs=_fox_aug_specs() + [blk, wide, blk, ANY_SPEC],
        out_specs=[blk, col, col, wide, colwide],
        out_shape=[jax.ShapeDtypeStruct((T, 512), BF16)] * 3 + [jax.ShapeDtypeStruct((T, 1024), F32)] * 2,
        scratch_shapes=[pltpu.VMEM((2, T, 128), F32), pltpu.VMEM((T, 128), F32)],
        compiler_params=pltpu.CompilerParams(dimension_semantics=("arbitrary", "arbitrary"), vmem_limit_bytes=VMEM_BIG),
    )(q_aug, k_aug, proj, o, lse, do, after)


def _rel_onehot():
    ridx = lax.broadcasted_iota(jnp.int32, (NREL_PAD, VW), 0)
    j = lax.broadcasted_iota(jnp.int32, (NREL_PAD, VW), 1)
    return jnp.where(ridx == jnp.clip(TQ + LEFT - 1 - j, -128, 128) + 128, 1.0, 0.0).astype(F32)


def _relvec_fwd(tbl):
    def body(t_ref, v_ref):
        v_ref[...] = _hdot(t_ref[...], _rel_onehot())

    return _one_call(body, "relvec_fwd", [tbl], [((8, VW), F32)])[0]


def _relvec_bwd(gv):
    def body(g_ref, t_ref):
        t_ref[...] = lax.dot_general(g_ref[...], _rel_onehot(), (((1,), (1,)), ((), ())),
                                     preferred_element_type=F32, precision=lax.Precision.HIGHEST)

    return _one_call(body, "relvec_bwd", [gv], [((8, NREL_PAD), F32)])[0]


def _chk_bias(vt_ref, a, hidden):
    vb = jnp.broadcast_to(vt_ref[a:a + 1, :], (TQ, VW))
    y = pltpu.roll(vb, VW - (TQ - 1), 1, stride=1, stride_axis=0)[:, :WIN]
    cr = lax.broadcasted_iota(jnp.int32, (TQ, WIN), 0) // 64
    m = lax.broadcasted_iota(jnp.int32, (TQ, WIN), 1)
    return jnp.where((m // 64 >= cr) & (m // 64 <= cr + 8) & (m >= hidden), y, NEG)


def _chk_specs():
    return [pl.BlockSpec((TQ, 128), lambda h, i: (i, CHK0 // 128 + h)),
            pl.BlockSpec((T + LEFT, 128), lambda h, i: (0, h)),
            pl.BlockSpec((T + LEFT, 128), lambda h, i: (0, 4 + h)),
            pl.BlockSpec((None, 2, VW), lambda h, i: (h, 0, 0))]


def _chk_fwd(proj, kvp, vt3):
    def body(q_ref, k_ref, v_ref, vt_ref, o_ref, l_ref, bias_ref):
        i = pl.program_id(1)

        @pl.when(i == 0)
        def _():
            for first in range(3):
                for a in range(2):
                    bias_ref[first, a] = _chk_bias(vt_ref, a, max(LEFT - first * TQ, 0))

        lo = _lane_lo()
        off = pl.multiple_of(i * TQ, TQ)
        kw = k_ref[pl.ds(off, WIN), :]
        vw = v_ref[pl.ds(off, WIN), :]
        bias_at = jnp.minimum(i, 2)
        q = q_ref[...]
        outs = []
        for a in range(2):
            s = _nt(_half(q, lo, a, SCALE), kw) + bias_ref[bias_at, a]
            m = jnp.max(s, axis=-1, keepdims=True)
            p = jnp.exp(s - m)
            l = jnp.sum(p, axis=-1, keepdims=True)
            outs.append(_nn(p.astype(BF16), vw) / l)
            l_ref[:, 128 * a:128 * a + 128] = jnp.broadcast_to(m + jnp.log(l), (TQ, 128))
        o_ref[...] = jnp.where(lo, outs[0], outs[1])

    return _pcall(
        body, name="chk_fwd", grid=(4, T // TQ), in_specs=_chk_specs(),
        out_specs=[pl.BlockSpec((TQ, 128), lambda h, i: (i, h)), pl.BlockSpec((TQ, 256), lambda h, i: (i, h))],
        out_shape=[jax.ShapeDtypeStruct((T, 512), F32), jax.ShapeDtypeStruct((T, 1024), F32)],
        scratch_shapes=[pltpu.VMEM((3, 2, TQ, WIN), F32)],
        compiler_params=pltpu.CompilerParams(dimension_semantics=("arbitrary", "arbitrary")),
    )(proj, kvp, kvp, vt3)


def _chk_bwd(proj, kvp, vt3, o, lse, do):
    nq = T // TQ

    def body(q_ref, k_ref, v_ref, vt_ref, o_ref, l_ref, do_ref, dq_ref, dkb_ref, dvb_ref, gv_ref, bias_ref, dsum_ref,
             dk_ref, dv_ref):
        i = pl.program_id(1)

        @pl.when(i == 0)
        def _():
            for first in range(3):
                for a in range(2):
                    bias_ref[first, a] = _chk_bias(vt_ref, a, max(LEFT - first * TQ, 0))
            dsum_ref[...] = jnp.zeros_like(dsum_ref)
            dk_ref[...] = jnp.zeros_like(dk_ref)
            dv_ref[...] = jnp.zeros_like(dv_ref)

        lo = _lane_lo()
        off = pl.multiple_of(i * TQ, TQ)
        kw = k_ref[pl.ds(off, WIN), :]
        vw = v_ref[pl.ds(off, WIN), :]
        bias_at = jnp.minimum(i, 2)
        q = q_ref[...]
        do_v = do_ref[...]
        prod = do_v * o_ref[...]
        dqs = []
        for a in range(2):
            keep = lo if a == 0 else jnp.logical_not(lo)
            qa = _half(q, lo, a, SCALE)
            doa = _half(do_v, lo, a)
            delta = jnp.sum(jnp.where(keep, prod, 0.0), axis=-1, keepdims=True)
            s = _nt(qa, kw) + bias_ref[bias_at, a]
            p = jnp.exp(s - l_ref[:, 128 * a:128 * a + 1])
            ds = p * (_nt(doa, vw) - delta)
            dsum_ref[a] += ds
            dsb = ds.astype(BF16)
            dk_ref[:, pl.ds(off, WIN)] += _tn(qa, dsb)
            dv_ref[:, pl.ds(off, WIN)] += _tn(doa, p.astype(BF16))
            dqs.append(_nn(dsb, kw))
        dq_ref[...] = (jnp.where(lo, dqs[0], dqs[1]) * SCALE).astype(BF16)

        @pl.when(i == nq - 1)
        def _():
            dkb_ref[...] = dk_ref[:, LEFT:].T.astype(BF16)
            dvb_ref[...] = dv_ref[:, LEFT:].T.astype(BF16)
            rr = lax.broadcasted_iota(jnp.int32, (TQ, TQ), 0)
            cc = lax.broadcasted_iota(jnp.int32, (TQ, TQ), 1)
            flip = jnp.where(rr + cc == TQ - 1, 1.0, 0.0).astype(F32)
            for a in range(2):
                dpad = jnp.concatenate([dsum_ref[a], jnp.zeros((TQ, VW - WIN), F32)], axis=1)
                z = pltpu.roll(_hdot(flip, dpad), 0, 1, stride=1, stride_axis=0)
                gv_ref[a:a + 1, :] = jnp.sum(z, axis=0, keepdims=True)

    blk = pl.BlockSpec((TQ, 128), lambda h, i: (i, h))
    wide = pl.BlockSpec((TQ, 256), lambda h, i: (i, h))
    col = pl.BlockSpec((T, 128), lambda h, i: (0, h))
    return _pcall(
        body, name="chk_bwd", grid=(4, nq), in_specs=_chk_specs() + [blk, wide, blk],
        out_specs=[blk, col, col, pl.BlockSpec((None, 2, VW), lambda h, i: (h, 0, 0))],
        out_shape=[jax.ShapeDtypeStruct((T, 512), BF16), jax.ShapeDtypeStruct((T, 512), BF16),
                   jax.ShapeDtypeStruct((T, 512), BF16), jax.ShapeDtypeStruct((4, 2, VW), F32)],
        scratch_shapes=[pltpu.VMEM((3, 2, TQ, WIN), F32), pltpu.VMEM((2, TQ, WIN), F32),
                        pltpu.VMEM((128, T + LEFT), F32), pltpu.VMEM((128, T + LEFT), F32)],
        compiler_params=pltpu.CompilerParams(dimension_semantics=("arbitrary", "arbitrary")),
    )(proj, kvp, kvp, vt3, o, lse, do)


def _zero_at_start(*refs):
    @pl.when(pl.program_id(0) == 0)
    def _():
        for r in refs:
            r[...] = jnp.zeros_like(r)


def _ffn_bwd(dx3, y3, x2, a, w1_t, w2, g_post, g_pre):
    def body(dx3_ref, y_ref, x2_ref, a_ref, w1_ref, w2_ref, gp_ref, gf_ref,
             dx2_ref, da_ref, dy_ref, r_ref, dgp_ref, dgf_ref):
        _zero_at_start(dgp_ref, dgf_ref)
        dx3_v = dx3_ref[...]
        dy, dgp = _rms_bwd(y_ref[...], gp_ref[...], dx3_v)
        dgp_ref[...] += dgp
        dyb = dy.astype(BF16)
        dy_ref[...] = dyb
        ra = jnp.maximum(a_ref[...].astype(F32), 0.0)
        r_ref[...] = jnp.square(ra).astype(BF16)
        da = (_nt(dyb, _w(w2_ref)) * (2.0 * ra)).astype(BF16)
        da_ref[...] = da
        dh, dgf = _rms_bwd(x2_ref[...], gf_ref[...], _nn(da, _w(w1_ref)))
        dgf_ref[...] += dgf
        dx2_ref[...] = dx3_v + dh

    return _tok_call(body, "ffn_bwd", [dx3, y3, x2, a], [w1_t, w2, g_post, g_pre],
                     [(D, F32), (DFF, BF16), (D, BF16), (DFF, BF16)], [(1, D), (1, D)], vmem=VMEM_BIG)


def _mem_bwd(dx2, ym, x1, qm, km, vm, w_mo, w_mq, g_post, g_pre):
    def body(dx2_ref, ym_ref, x1_ref, q_ref, k_ref, v_ref, wo_ref, wq_ref, gp_ref, gm_ref,
             dx1_ref, dym_ref, dq_ref, dk_ref, dv_ref, dgp_ref, dgm_ref, dom_ref):
        _zero_at_start(dk_ref, dv_ref, dgp_ref, dgm_ref)
        dx2_v = dx2_ref[...]
        dym, dgp = _rms_bwd(ym_ref[...], gp_ref[...], dx2_v)
        dgp_ref[...] += dgp
        dymb = dym.astype(BF16)
        dym_ref[...] = dymb
        dom_ref[...] = _nt(dymb, _w(wo_ref)).astype(BF16)
        for h in range(MEM_HEADS):
            sl = slice(h * MEM_HD, (h + 1) * MEM_HD)
            qh, kh, doh = q_ref[:, sl], k_ref[:, sl], dom_ref[:, sl]
            s = _nt(qh, kh) * MEM_SCALE
            p = jnp.exp(s - jnp.max(s, axis=-1, keepdims=True))
            p = p / jnp.sum(p, axis=-1, keepdims=True)
            dp = _nt(doh, v_ref[:, sl])
            ds = (p * (dp - jnp.sum(p * dp, axis=-1, keepdims=True))).astype(BF16)
            dq_ref[:, sl] = (_nn(ds, kh) * MEM_SCALE).astype(BF16)
            dk_ref[:, sl] += _tn(ds, qh) * MEM_SCALE
            dv_ref[:, sl] += _tn(p.astype(BF16), doh)
        dh, dgm = _rms_bwd(x1_ref[...], gm_ref[...], _nt(dq_ref[...], _w(wq_ref)))
        dgm_ref[...] += dgm
        dx1_ref[...] = dx2_v + dh

    tiled = pl.BlockSpec((TM_WIDE, D), lambda i: (i, 0))
    in_specs = [tiled] * 4 + [_resident(a) for a in (km, vm, w_mo, w_mq, g_post, g_pre)]
    w_mo, w_mq = w_mo[0], w_mq[0]
    kv = pl.BlockSpec((NMEM, D), lambda i: (0, 0))
    vec = pl.BlockSpec((1, D), lambda i: (0, 0))
    return _pcall(
        body, name="mem_bwd", grid=(T // TM_WIDE,), in_specs=in_specs,
        out_specs=[tiled, tiled, tiled, kv, kv, vec, vec],
        out_shape=[jax.ShapeDtypeStruct((T, D), F32), jax.ShapeDtypeStruct((T, D), BF16),
                   jax.ShapeDtypeStruct((T, D), BF16), jax.ShapeDtypeStruct((NMEM, D), F32),
                   jax.ShapeDtypeStruct((NMEM, D), F32), jax.ShapeDtypeStruct((1, D), F32),
                   jax.ShapeDtypeStruct((1, D), F32)],
        scratch_shapes=[pltpu.VMEM((TM_WIDE, D), BF16)],
        compiler_params=pltpu.CompilerParams(dimension_semantics=("arbitrary",), vmem_limit_bytes=VMEM_BIG),
    )(dx2, ym, x1, qm, km, vm, w_mo, w_mq, g_post, g_pre)


def _memkv_bwd(dkm, dvm, mem, w_mk, w_mv):
    def body(dk_ref, dv_ref, m_ref, wk_ref, wv_ref, dg_ref):
        dmn = _nt(dk_ref[...].astype(BF16), _w(wk_ref)) + _nt(dv_ref[...].astype(BF16), _w(wv_ref))
        mv = m_ref[...]
        dg_ref[...] = jnp.sum(dmn * (mv * _rstd(mv)), axis=0, keepdims=True)

    return _one_call(body, "memkv_bwd", [dkm, dvm, mem, w_mk, w_mv], [((1, D), F32)], vmem=VMEM_BIG)[0]


def _postmix_bwd(dx1, z, o_f, o_c, w_out, g_post, g_fo, g_co):
    def body(dx1_ref, z_ref, of_ref, oc_ref, wo_ref, gp_ref, gfo_ref, gco_ref,
             dz_ref, dof_ref, doc_ref, dgp_ref, dgfo_ref, dgco_ref):
        _zero_at_start(dgp_ref, dgfo_ref, dgco_ref)
        dz, dgp = _rms_bwd(z_ref[...], gp_ref[...], dx1_ref[...])
        dgp_ref[...] += dgp
        dzb = dz.astype(BF16)
        dz_ref[...] = dzb
        dy = _nt(dzb, _w(wo_ref))
        dof, dgfo = _rms_bwd(of_ref[...], gfo_ref[...], dy[:, :512])
        doc, dgco = _rms_bwd(oc_ref[...], gco_ref[...], dy[:, 512:])
        dof_ref[...] = dof
        doc_ref[...] = doc
        dgfo_ref[...] += dgfo
        dgco_ref[...] += dgco

    return _tok_call(body, "postmix_bwd", [dx1, z, o_f, o_c], [w_out, g_post, g_fo, g_co],
                     [(D, BF16), (512, F32), (512, F32)], [(1, D), (1, 512), (1, 512)], tm=TM_WIDE, vmem=VMEM_BIG)


def _premix_bwd(dx1, x, pieces, win_t, g_pre):
    def body(dx1_ref, x_ref, *refs):
        piece_refs, (w_ref, g_ref, dx_ref, dp_ref, dg_ref) = refs[:len(pieces)], refs[len(pieces):]
        _zero_at_start(dg_ref)
        col = 0
        for p in piece_refs:
            dp_ref[:, col:col + p.shape[1]] = p[...]
            col += p.shape[1]
        dh, dg = _rms_bwd(x_ref[...], g_ref[...], _nn(dp_ref[...], w_ref[...]))
        dg_ref[...] += dg
        dx_ref[...] = dx1_ref[...] + dh

    return _tok_call(body, "premix_bwd", [dx1, x] + list(pieces), [win_t, g_pre], [(D, F32), (PROJ, BF16)], [(1, D)],
                     tm=TM_WIDE, vmem=VMEM_BIG)


def _wgrad(a, b, name):
    k, m = a.shape
    n = b.shape[1]
    tm = 640 if m % 640 == 0 and m > 1024 else min(m, 512)
    tn = min(n, 1024)

    def body(a_ref, b_ref, o_ref):
        o_ref[...] = _tn(a_ref[...].astype(BF16), b_ref[...].astype(BF16))

    return _pcall(
        body, name=name, grid=(m // tm, n // tn),
        in_specs=[pl.BlockSpec((k, tm), lambda i, j: (0, i)), pl.BlockSpec((k, tn), lambda i, j: (0, j))],
        out_specs=pl.BlockSpec((tm, tn), lambda i, j: (i, j)),
        out_shape=jax.ShapeDtypeStruct((m, n), F32),
        compiler_params=pltpu.CompilerParams(dimension_semantics=("arbitrary", "arbitrary"), vmem_limit_bytes=VMEM_BIG),
    )(a, b)


def _wgrad_group(name, pairs, rows):
    def body(*refs):
        o_ref = refs[-1]
        for k in range(len(pairs)):
            o_ref[k * rows:(k + 1) * rows, :] = _tn(refs[2 * k][...].astype(BF16), refs[2 * k + 1][...].astype(BF16))

    in_specs, ops = [], []
    for a, b in pairs:
        in_specs += [pl.BlockSpec((a.shape[0], rows), lambda j: (0, j)), _resident(b)]
        ops += [a, b]
    return _pcall(
        body, name=name, grid=(8,), in_specs=in_specs,
        out_specs=pl.BlockSpec((None, len(pairs) * rows, D), lambda j: (j, 0, 0)),
        out_shape=jax.ShapeDtypeStruct((8, len(pairs) * rows, D), F32),
        compiler_params=pltpu.CompilerParams(dimension_semantics=("arbitrary",), vmem_limit_bytes=VMEM_BIG),
    )(*ops)


def _adam_math(w, g, m, v):
    m2 = ADAM_B1 * m + (1.0 - ADAM_B1) * g
    v2 = ADAM_B2 * v + (1.0 - ADAM_B2) * jnp.square(g)
    m_hat = m2 / (1.0 - ADAM_B1 ** ADAM_STEP)
    v_hat = v2 / (1.0 - ADAM_B2 ** ADAM_STEP)
    delta = -ADAM_LR * (m_hat / (jnp.sqrt(v_hat) + ADAM_EPS) + ADAM_WD * w)
    return delta, m2, v2


def _adamw(w, g, m, v, name):
    rows, cols = w.shape
    tr = 256 if rows % 256 == 0 else rows

    def body(w_ref, g_ref, m_ref, v_ref, d_ref, m2_ref, v2_ref):
        d_ref[...], m2_ref[...], v2_ref[...] = _adam_math(w_ref[...], g_ref[...], m_ref[...], v_ref[...])

    spec = pl.BlockSpec((tr, cols), lambda i: (i, 0))
    return _pcall(
        body, name=name, grid=(rows // tr,), in_specs=[spec] * 4, out_specs=[spec] * 3,
        out_shape=[jax.ShapeDtypeStruct(w.shape, F32)] * 3,
        compiler_params=pltpu.CompilerParams(dimension_semantics=("arbitrary",)),
    )(w, g, m, v)


def _adamw_small(gparts, ws, ms, vs):
    n = len(SMALL)

    def body(g_ref, *refs):
        w_refs, m_refs, v_refs = refs[:n], refs[n:2 * n], refs[2 * n:3 * n]
        outs, sum_ref = refs[3 * n:-1], refs[-1]
        g = g_ref[0]
        for k in range(1, 8):
            g = g + g_ref[k]
        sum_ref[...] = g
        outs[0][...] = sum_ref[17:18, 0:128]
        for t, name in enumerate(SMALL):
            r0, nr, c0, nc = SMALL_SLOT[name]
            gt = sum_ref[r0:r0 + nr, c0:c0 + nc]
            out = (gt,) + _adam_math(w_refs[t][...], gt, m_refs[t][...], v_refs[t][...])
            for o_ref, val in zip(outs[1 + 4 * t:5 + 4 * t], out):
                o_ref[...] = val

    whole = lambda s: pl.BlockSpec(s, lambda i, nd=len(s): (0,) * nd)
    ins = [gparts] + list(ws) + list(ms) + list(vs)
    out_shapes = [(1, 128)] + [a.shape for a in ws for _ in range(4)]
    return _pcall(
        body, name="adamw_small", grid=(1,), in_specs=[whole(a.shape) for a in ins],
        out_specs=[whole(s) for s in out_shapes], out_shape=[jax.ShapeDtypeStruct(s, F32) for s in out_shapes],
        scratch_shapes=[pltpu.VMEM((SMALL_ROWS, D), F32)],
        compiler_params=pltpu.CompilerParams(dimension_semantics=("arbitrary",)),
    )(*ins)


def _row_tile(rows):
    return next(t for t in (512, 400, 320) if rows % t == 0)


def _add_halves(g4, theirs, core, name):
    rows = g4.shape[2]
    tr = _row_tile(rows)

    def body(c_ref, a_ref, b_ref, o_ref):
        o_ref[...] = (a_ref[...] + b_ref[...]).astype(BF16)

    grid_spec = pltpu.PrefetchScalarGridSpec(
        num_scalar_prefetch=1, grid=(4, rows // tr),
        in_specs=[pl.BlockSpec((None, None, tr, D), lambda j, i, c: (j, c[0], i, 0)),
                  pl.BlockSpec((None, None, tr, D), lambda j, i, c: (j, 0, i, 0))],
        out_specs=pl.BlockSpec((None, tr, D), lambda j, i, c: (j, i, 0)))
    return _pcall(
        body, name=name, grid_spec=grid_spec, out_shape=jax.ShapeDtypeStruct((4, rows, D), BF16),
        compiler_params=pltpu.CompilerParams(dimension_semantics=("arbitrary", "arbitrary")),
    )(core, g4, theirs)


def _sum_adam(own, got, order, r0, w, m, v, name, transposed=False):
    n = w.shape[1] if transposed else w.shape[0]
    tr = min(n, 256) if n % 8 == 0 else n
    rows = tr if n % 8 == 0 else own.shape[1]

    def body(o_ref, a_ref, b_ref, c_ref, d_ref, w_ref, m_ref, v_ref, g_ref, dl_ref, m2_ref, v2_ref):
        f = lambda r: r[0:tr, :].astype(F32)
        g = ((f(a_ref) + f(b_ref)) + f(c_ref)) + f(d_ref)
        g = g.T if transposed else g
        g_ref[...] = g
        dl_ref[...], m2_ref[...], v2_ref[...] = _adam_math(w_ref[...], g, m_ref[...], v_ref[...])

    slot = lambda k: pl.BlockSpec((None, rows, D), lambda i, o: (o[k], r0 // rows + i, 0))
    wspec = pl.BlockSpec((D, tr), lambda i, o: (0, i)) if transposed else pl.BlockSpec((tr, D), lambda i, o: (i, 0))
    grid_spec = pltpu.PrefetchScalarGridSpec(
        num_scalar_prefetch=1, grid=(n // tr,), in_specs=[slot(0), slot(1), slot(2), slot(3), wspec, wspec, wspec],
        out_specs=[wspec] * 4)
    return _pcall(
        body, name=name, grid_spec=grid_spec, out_shape=[jax.ShapeDtypeStruct(w.shape, F32)] * 4,
        compiler_params=pltpu.CompilerParams(dimension_semantics=("arbitrary",)),
    )(order, own, got, got, got, w, m, v)


def _sum_adam_rows(own, got, order, ws, ms, vs, name):
    n, rows = len(ws), ws[0].shape[0]

    def body(o_ref, a_ref, b_ref, c_ref, d_ref, *refs):
        ins, outs = refs[:3 * n], refs[3 * n:]
        for t in range(n):
            r = slice(t * rows, (t + 1) * rows)
            f = lambda ref: ref[r, :].astype(F32)
            g = ((f(a_ref) + f(b_ref)) + f(c_ref)) + f(d_ref)
            out = (g,) + _adam_math(ins[t][...], g, ins[n + t][...], ins[2 * n + t][...])
            for o, val in zip(outs[4 * t:4 * t + 4], out):
                o[...] = val

    slot = lambda k: pl.BlockSpec((None, n * rows, D), lambda i, o: (o[k], 0, 0), pipeline_mode=pl.Buffered(1))
    wspec = pl.BlockSpec((rows, D), lambda i, o: (0, 0), pipeline_mode=pl.Buffered(1))
    grid_spec = pltpu.PrefetchScalarGridSpec(
        num_scalar_prefetch=1, grid=(1,), in_specs=[slot(0), slot(1), slot(2), slot(3)] + [wspec] * (3 * n),
        out_specs=[pl.BlockSpec((rows, D), lambda i, o: (0, 0))] * (4 * n))
    return _pcall(
        body, name=name, grid_spec=grid_spec, out_shape=[jax.ShapeDtypeStruct((rows, D), F32)] * (4 * n),
        compiler_params=pltpu.CompilerParams(dimension_semantics=("arbitrary",), vmem_limit_bytes=VMEM_BIG),
    )(order, own, got, got, got, *ws, *ms, *vs)


def _sum_chips(own, got, order, name):
    rows = own.shape[1]
    tr = _row_tile(rows)

    def body(o_ref, a_ref, b_ref, c_ref, d_ref, out_ref):
        f = lambda r: r[...].astype(F32)
        out_ref[...] = ((f(a_ref) + f(b_ref)) + f(c_ref)) + f(d_ref)

    slot = lambda k: pl.BlockSpec((None, tr, D), lambda i, o: (o[k], i, 0))
    grid_spec = pltpu.PrefetchScalarGridSpec(
        num_scalar_prefetch=1, grid=(rows // tr,), in_specs=[slot(0), slot(1), slot(2), slot(3)],
        out_specs=pl.BlockSpec((tr, D), lambda i, o: (i, 0)))
    return _pcall(
        body, name=name, grid_spec=grid_spec, out_shape=jax.ShapeDtypeStruct((rows, D), F32),
        compiler_params=pltpu.CompilerParams(dimension_semantics=("arbitrary",)),
    )(order, own, got, got, got)


def _place():
    return lax.axis_index("x"), lax.axis_index("y"), lax.axis_index("c")


def _allgather(block, name):
    def body(x_ref, out_ref, token, send_sems, recv_sems, local_sem):
        token[...] = jnp.zeros_like(token)
        x, y, c = _place()
        me, sibling = (x, y, c), (x, y, 1 - c)
        chips = [(1 - x, y), (x, 1 - y), (1 - x, 1 - y)]

        def slot(px, py, pc):
            return out_ref.at[4 * px + 2 * py + pc]

        def copy(k, blk, to, src=None):
            return pltpu.make_async_remote_copy(
                src_ref=slot(*blk) if src is None else src, dst_ref=slot(*blk),
                send_sem=send_sems.at[k], recv_sem=recv_sems.at[k], device_id=to, device_id_type=MESH)

        mine = pltpu.make_async_copy(x_ref, slot(*me), local_sem)
        mine.start()
        first = [copy(0, me, sibling, src=x_ref)]
        first += [copy(1 + j, me, (*chip, c), src=x_ref) for j, chip in enumerate(chips)]
        for cp in first:
            cp.start()
        passed = [copy(4 + j, (*chip, c), sibling) for j, chip in enumerate(chips)]
        for j, chip in enumerate(chips):
            copy(1 + j, (*chip, c), me).wait_recv()
            passed[j].start()
        copy(0, sibling, me).wait_recv()
        for j, chip in enumerate(chips):
            copy(4 + j, (*chip, 1 - c), me).wait_recv()
        for cp in first + passed:
            cp.wait_send()
        mine.wait()

    return _pcall(
        body, name=name,
        out_shape=[jax.ShapeDtypeStruct((8,) + block.shape, block.dtype), jax.ShapeDtypeStruct((8, 128), F32)],
        in_specs=[pl.BlockSpec(memory_space=pl.ANY)],
        out_specs=[pl.BlockSpec(memory_space=pl.ANY), pl.BlockSpec(memory_space=pltpu.VMEM)],
        scratch_shapes=[pltpu.SemaphoreType.DMA((7,)), pltpu.SemaphoreType.DMA((7,)), pltpu.SemaphoreType.DMA(())],
        compiler_params=pltpu.CompilerParams(has_side_effects=True),
    )(block)


HBM_SPEC = pl.BlockSpec(memory_space=pltpu.HBM)
SEM_SPEC = pl.BlockSpec(memory_space=pltpu.SEMAPHORE)
ANY_SPEC = pl.BlockSpec(memory_space=pl.ANY)
EFFECT = pltpu.SideEffectType.DATAFLOW_SIDE_EFFECTING


def _in_hbm(a):
    return pltpu.with_memory_space_constraint(a, pltpu.HBM)


def _start_copies(name, src, land_shape, plan, n):
    def body(src_ref, land_ref, send_sems, recv_sems, src_thru, land_thru, token):
        for k, (s, d, to, _) in enumerate(plan(src_ref, land_ref)):
            pltpu.make_async_remote_copy(src_ref=s, dst_ref=d, send_sem=send_sems.at[k], recv_sem=recv_sems.at[k],
                                         device_id=to, device_id_type=MESH).start()
        token[...] = jnp.zeros_like(token)

    return _pcall(
        body, name=name,
        out_shape=(pltpu.SemaphoreType.DMA((n,)), pltpu.SemaphoreType.DMA((n,)), pltpu.HBM(src.shape, src.dtype),
                   pltpu.HBM(land_shape, src.dtype), jax.ShapeDtypeStruct((8, 128), F32)),
        in_specs=(HBM_SPEC, HBM_SPEC),
        out_specs=(SEM_SPEC, SEM_SPEC, HBM_SPEC, HBM_SPEC, pl.BlockSpec(memory_space=pltpu.VMEM)),
        input_output_aliases={0: 2, 1: 3}, compiler_params=pltpu.CompilerParams(has_side_effects=EFFECT),
    )(_in_hbm(src), _in_hbm(lax.empty(land_shape, src.dtype)))


def _wait_copies(name, started, after, plan):
    send_sems, recv_sems, src_thru, land_thru, _ = started

    def body(src_ref, land_ref, send_sems, recv_sems, *rest):
        for k, (s, _, to, mine) in enumerate(plan(src_ref, land_ref)):
            cp = pltpu.make_async_remote_copy(src_ref=s, dst_ref=mine, send_sem=send_sems.at[k],
                                              recv_sem=recv_sems.at[k], device_id=to, device_id_type=MESH)
            cp.wait_send()
            cp.wait_recv()

    return _pcall(
        body, name=name,
        out_shape=(pltpu.HBM(src_thru.shape, src_thru.dtype), pltpu.HBM(land_thru.shape, land_thru.dtype)),
        in_specs=(HBM_SPEC, HBM_SPEC, SEM_SPEC, SEM_SPEC) + (ANY_SPEC,) * len(after), out_specs=(HBM_SPEC, HBM_SPEC),
        input_output_aliases={0: 0, 1: 1}, compiler_params=pltpu.CompilerParams(has_side_effects=EFFECT),
    )(src_thru, land_thru, send_sems, recv_sems, *after)


def _gather_plan(src_ref, land_ref):
    x, y, c = _place()
    peers = [(x, y, 1 - c), (1 - x, y, c), (x, 1 - y, c), (1 - x, 1 - y, c)]
    return [(src_ref, land_ref.at[4 * x + 2 * y + c], p, land_ref.at[4 * p[0] + 2 * p[1] + p[2]]) for p in peers]


def _swap_plan(src_ref, land_ref):
    x, y, c = _place()
    return [(src_ref.at[:, pl.ds(1 - c, 1)], land_ref, (x, y, 1 - c), land_ref)]


def _exchange_plan(src_ref, land_ref):
    x, y, c = _place()
    chips = [(1 - x, y), (x, 1 - y), (1 - x, 1 - y)]
    return [(src_ref.at[2 * px + py], land_ref.at[2 * x + y], (px, py, c), land_ref.at[2 * px + py]) for px, py in chips]


def _gather_forward(land, block):
    def body(land_ref, out_ref, send_sems, recv_sems):
        x, y, c = _place()
        chips = [(1 - x, y), (x, 1 - y), (1 - x, 1 - y)]

        def copy(k, px, py, pc):
            blk = out_ref.at[4 * px + 2 * py + pc]
            return pltpu.make_async_remote_copy(src_ref=blk, dst_ref=blk, send_sem=send_sems.at[k],
                                                recv_sem=recv_sems.at[k], device_id=(x, y, 1 - c), device_id_type=MESH)

        sent = [copy(k, px, py, c) for k, (px, py) in enumerate(chips)]
        for cp in sent:
            cp.start()
        for k, (px, py) in enumerate(chips):
            copy(k, px, py, 1 - c).wait_recv()
        for cp in sent:
            cp.wait_send()

    land = _pcall(
        body, name="allgather_rest_forward", out_shape=jax.ShapeDtypeStruct(land.shape, land.dtype),
        in_specs=[ANY_SPEC], out_specs=ANY_SPEC, input_output_aliases={0: 0},
        scratch_shapes=[pltpu.SemaphoreType.DMA((3,)), pltpu.SemaphoreType.DMA((3,))],
        compiler_params=pltpu.CompilerParams(has_side_effects=True),
    )(land)

    rows = block.shape[0]
    tr = rows // 4

    def place(me_ref, x_ref, land_ref, out_ref):
        out_ref[...] = x_ref[...]

    x, y, c = _place()
    grid_spec = pltpu.PrefetchScalarGridSpec(
        num_scalar_prefetch=1, grid=(rows // tr,),
        in_specs=[pl.BlockSpec((tr, D), lambda i, me: (i, 0)), ANY_SPEC],
        out_specs=pl.BlockSpec((None, tr, D), lambda i, me: (me[0], i, 0)))
    return _pcall(
        place, name="allgather_rest_own", grid_spec=grid_spec, out_shape=jax.ShapeDtypeStruct(land.shape, land.dtype),
        input_output_aliases={2: 0}, compiler_params=pltpu.CompilerParams(dimension_semantics=("arbitrary",)),
    )((4 * x + 2 * y + c).reshape(1), block, land)


class _ReduceScatter:
    def __init__(self, name, g):
        self.name = name
        rows = g.shape[1]
        self.started = _start_copies(name + "_swap_start", g.reshape(4, 2, rows, D), (4, 1, rows, D), _swap_plan, 1)
        self.token = self.started[4][0, 0]

    def halfway(self, after):
        g4, theirs = _wait_copies(self.name + "_swap_wait", self.started, after, _swap_plan)
        self.own = _add_halves(g4, theirs, lax.axis_index("c").reshape(1), self.name + "_add_halves")
        self.started = _start_copies(self.name + "_exch_start", self.own, self.own.shape, _exchange_plan, 3)
        self.token = self.started[4][0, 0]

    def finish(self, after):
        own, got = _wait_copies(self.name + "_exch_wait", self.started, after, _exchange_plan)
        chip = 2 * lax.axis_index("x") + lax.axis_index("y")
        return own, got, (chip + jnp.arange(4, dtype=jnp.int32)) % 4


def _pack_small(p, scalar=None):
    z = lambda a, n: jnp.pad(a, ((0, 0), (0, n - a.shape[1])))
    rows = [z(p['rel_bias'], D), z(p['b_fgt'], D), jnp.concatenate([p['g_fox_out'], p['g_chk_out']], axis=1)]
    rows += [p[n] for n in ('g_mix_pre', 'g_mix_post', 'g_mem_kv', 'g_mem_pre', 'g_mem_post', 'g_ff_pre', 'g_ff_post')]
    rows.append(jnp.zeros((1, D), F32) if scalar is None else z(jnp.reshape(scalar, (1, 1)), D))
    rows.append(jnp.zeros((SMALL_ROWS - 18, D), F32))
    return jnp.concatenate(rows, axis=0)


_GAP_DEV, _GAP_ROW = divmod(GATE0 + 8, N_IN)
_GAP = CHK0 - GATE0 - 8


def _in_rows_to_proj(g):
    runs = [(j, 0, N_IN, N_IN * j) for j in range(_GAP_DEV)]
    runs += [(_GAP_DEV, 0, _GAP_ROW, N_IN * _GAP_DEV), (_GAP_DEV, _GAP_ROW, N_IN, N_IN * _GAP_DEV + _GAP_ROW + _GAP)]
    runs += [(j, 0, N_IN, N_IN * j + _GAP) for j in range(_GAP_DEV + 1, 8)]

    def body(g_ref, o_ref, acc_ref):
        acc_ref[...] = jnp.zeros_like(acc_ref)
        for j, r0, r1, dest in runs:
            start, shift = dest // 16 * 16, dest % 16
            win = -(-(shift + r1 - r0) // 16) * 16
            r = lax.broadcasted_iota(jnp.int32, (win, R_IN), 0)
            c = lax.broadcasted_iota(jnp.int32, (win, R_IN), 1)
            move = jnp.where((c >= r0) & (c < r1) & (r == c - r0 + shift), 1.0, 0.0).astype(BF16)
            acc_ref[start:start + win, :] += _nn(move, g_ref[j])
        o_ref[...] = acc_ref[...].astype(BF16)

    return _pcall(
        body, name="w_in_layout", out_shape=jax.ShapeDtypeStruct((PROJ, D), BF16), grid=(1,),
        in_specs=[pl.BlockSpec(g.shape, lambda i: (0, 0, 0))], out_specs=pl.BlockSpec((PROJ, D), lambda i: (0, 0)),
        scratch_shapes=[pltpu.VMEM((PROJ, D), F32)],
        compiler_params=pltpu.CompilerParams(dimension_semantics=("arbitrary",), vmem_limit_bytes=VMEM_BIG),
    )(g)


def _proj_rows_to_in(g):
    pad = lambda a: jnp.pad(a, ((0, R_IN - a.shape[0]), (0, 0)))
    lo = N_IN * _GAP_DEV
    shards = [pad(g[N_IN * j:N_IN * (j + 1)]) for j in range(_GAP_DEV)]
    shards.append(pad(jnp.concatenate([g[lo:lo + _GAP_ROW], g[lo + _GAP_ROW + _GAP:lo + N_IN + _GAP]], axis=0)))
    shards += [pad(g[N_IN * j + _GAP:N_IN * (j + 1) + _GAP]) for j in range(_GAP_DEV + 1, 8)]
    return jnp.stack(shards)


def _local_grads(x, mem, tgt, win_t, gw_of, sm, on_grads):
    b_pad = jnp.pad(sm['b_fgt'], ((0, 0), (0, 120)))
    tbl = jnp.pad(sm['rel_bias'], ((0, 0), (0, NREL_PAD - 257)))

    h1, proj, flog = _premix_fwd(x, sm['g_mix_pre'], win_t)
    c = _gate_fwd(flog, b_pad)
    ct3 = c[:, :8].T.reshape(4, 2, T)
    o_f, lse_f = _fox_fwd(proj, c, ct3)
    vt3 = _relvec_fwd(tbl).reshape(4, 2, VW)
    kvp = jnp.pad(proj[:, CHK0 + 512:], ((LEFT, 0), (0, 0)))
    o_c, lse_c = _chk_fwd(proj, kvp, vt3)
    gw = gw_of([o_f, o_c])
    w_out, w_mq, w_mk, w_mv, w_mo, w1_t, w2 = (_wblk(gw, n) for n in ('w_out', 'w_mq', 'w_mk', 'w_mv', 'w_mo', 'w_ff1', 'w_ff2'))
    ycat, z, x1, h2, qm = _postmix_fwd(x, o_f, o_c, sm['g_fox_out'], sm['g_chk_out'], w_out,
                                       sm['g_mix_post'], sm['g_mem_pre'], w_mq)
    memn, km, vm = _memkv_fwd(mem, sm['g_mem_kv'], w_mk, w_mv)
    om, ym, x2, h3 = _mem_fwd(qm, x1, km, vm, w_mo, sm['g_mem_post'], sm['g_ff_pre'])
    a, y3, dx3, loss_acc = _ffn_fwd(h3, x2, tgt, w1_t, w2, sm['g_ff_post'])

    gs = {}
    dx2, da, dy3, r, gs['g_ff_post'], gs['g_ff_pre'] = _ffn_bwd(dx3, y3, x2, a, w1_t, w2, sm['g_ff_post'], sm['g_ff_pre'])
    zero = on_grads('A', _wgrad_group("wgrad_ff", [(da, h3), (r, dy3)], 512), None)
    dx1, dym, dqm, dkm, dvm, gs['g_mem_post'], gs['g_mem_pre'] = _mem_bwd(
        dx2, ym, x1, qm, km, vm, w_mo, w_mq, sm['g_mem_post'] + zero, sm['g_mem_pre'])
    zero = on_grads('A halfway', None, [dx1])
    gs['g_mem_kv'] = _memkv_bwd(dkm, dvm, mem, w_mk, w_mv)
    dz, dof, doc, gs['g_mix_post'], gs['g_fox_out'], gs['g_chk_out'] = _postmix_bwd(
        dx1, z, o_f, o_c, w_out, sm['g_mix_post'] + zero, sm['g_fox_out'], sm['g_chk_out'])
    zero = on_grads('B', _wgrad_group("wgrad_mem_out", [(ycat, dz), (h2, dqm), (memn, dkm), (memn, dvm), (om, dym)], 128), None)
    dq_f, dk_f, dv_f, dct, dcq = _fox_bwd(proj, c, ct3 + zero, o_f, lse_f, dof)
    zero = on_grads('B halfway', None, [dq_f])
    dq_c, dk_c, dv_c, gv = _chk_bwd(proj, kvp, vt3 + zero, o_c, lse_c, doc)
    gs['rel_bias'] = _relvec_bwd(gv.reshape(8, VW))[:, :257]
    dc = jnp.pad(dct.reshape(8, T).T + dcq[:, :, :2].transpose(1, 0, 2).reshape(T, 8), ((0, 0), (0, 120)))
    dflog, db = _gate_bwd(dc, flog, b_pad)
    gs['b_fgt'] = db[0:1, :8]
    grad_x, dproj, gs['g_mix_pre'] = _premix_bwd(dx1, x, [dq_f, dk_f, dv_f, dflog, dq_c, dk_c, dv_c], win_t, sm['g_mix_pre'])
    on_grads('C', _proj_rows_to_in(_wgrad(dproj, h1, "wgrad_in")), None)
    return loss_acc[0, 0], grad_x, gs


def kernel(x, mem, w_in, b_fgt, rel_bias, g_fox_out, g_chk_out, w_out, g_mix_pre, g_mix_post, g_mem_kv, w_mq, w_mk, w_mv, w_mo, g_mem_pre, g_mem_post, w_ff1, w_ff2, g_ff_pre, g_ff_post, loss_target, m_w_in, m_b_fgt, m_rel_bias, m_g_fox_out, m_g_chk_out, m_w_out, m_g_mix_pre, m_g_mix_post, m_g_mem_kv, m_w_mq, m_w_mk, m_w_mv, m_w_mo, m_g_mem_pre, m_g_mem_post, m_w_ff1, m_w_ff2, m_g_ff_pre, m_g_ff_post, v_w_in, v_b_fgt, v_rel_bias, v_g_fox_out, v_g_chk_out, v_w_out, v_g_mix_pre, v_g_mix_post, v_g_mem_kv, v_w_mq, v_w_mk, v_w_mv, v_w_mo, v_g_mem_pre, v_g_mem_post, v_w_ff1, v_w_ff2, v_g_ff_pre, v_g_ff_post):
    args = dict(locals())
    two_d = lambda a: a.reshape(a.shape[-2:])
    w = {n: two_d(args[n]) for n in WEIGHTS}
    m = {n: two_d(args['m_' + n]) for n in WEIGHTS}
    v = {n: two_d(args['v_' + n]) for n in WEIGHTS}

    sm = {n: w[n] for n in SMALL}
    shard_in = jnp.pad(w['w_in'].T, ((0, R_IN - N_IN), (0, 0))).astype(BF16)
    gathered_in, zero = _allgather(shard_in, "allgather_w_in")
    win_t = _in_rows_to_proj(gathered_in)
    shard_rest = (jnp.concatenate([w['w_ff1'].T, w['w_ff2'], w['w_out'], w['w_mq'], w['w_mk'], w['w_mv'], w['w_mo']],
                                  axis=0) + zero[0, 0]).astype(BF16)
    rest = _start_copies("allgather_rest_start", shard_rest, (8, R_REST, D), _gather_plan, 4)
    sm['g_mix_pre'] = sm['g_mix_pre'] + rest[4][0, 0]

    def gw_of(after):
        block, land = _wait_copies("allgather_rest_wait", rest, after, _gather_plan)
        return _gather_forward(land, block)

    rs = {}

    def on_grads(stage, g, after):
        if stage.endswith('halfway'):
            rs[stage[0]].halfway(after)
            return rs[stage[0]].token
        rs[stage] = _ReduceScatter("rs_" + stage.lower(), g)
        return rs[stage].token

    loss_local, grad_x, gs = _local_grads(x[0], mem[0], loss_target[0], win_t, gw_of, sm, on_grads)
    grads, deltas, new_m, new_v = {}, {}, {}, {}

    def update(n, out):
        grads[n], deltas[n], new_m[n], new_v[n] = out

    packed = _pack_small(gs, loss_local + rs['C'].token)
    rs['C'].halfway([packed])
    gparts, _ = _allgather(packed + rs['C'].token, "allgather_small_grads")
    small = _adamw_small(gparts, [w[n] for n in SMALL], [m[n] for n in SMALL], [v[n] for n in SMALL])
    loss = small[0][0, 0]
    for t, n in enumerate(SMALL):
        update(n, small[1 + 4 * t:5 + 4 * t])

    own, got, order = rs['A'].finish([grad_x, rs['C'].started[4]])
    update('w_ff1', _sum_adam(own, got, order, 0, w['w_ff1'], m['w_ff1'], v['w_ff1'], "adamw_w_ff1", transposed=True))
    update('w_ff2', _sum_adam(own, got, order, 512, w['w_ff2'], m['w_ff2'], v['w_ff2'], "adamw_w_ff2"))
    own, got, order = rs['B'].finish([grad_x, rs['C'].started[4]])
    names_b = ('w_out', 'w_mq', 'w_mk', 'w_mv', 'w_mo')
    done = _sum_adam_rows(own, got, order, [w[n] for n in names_b], [m[n] for n in names_b], [v[n] for n in names_b],
                          "adamw_group_b")
    for k, n in enumerate(names_b):
        update(n, done[4 * k:4 * k + 4])

    own, got, order = rs['C'].finish([new_v[n] for n in BIG if n != 'w_in'])
    done = _sum_adam(own, got, order, 0, w['w_in'].T, m['w_in'].T, v['w_in'].T, "adamw_w_in")
    update('w_in', [a.T for a in done])

    out = [loss, grad_x[None]]
    for group in (grads, deltas, new_m, new_v):
        out += [group[n].reshape(args[n].shape) for n in WEIGHTS]
    return tuple(out)
```

```python
import jax
import jax.numpy as jnp
from jax import lax
from jax.experimental import pallas as pl
from jax.experimental.pallas import tpu as pltpu

F32 = jnp.float32
BF16 = jnp.bfloat16
MESH = pl.DeviceIdType.MESH

T = 2048
D = 1024
NMEM = 256
DFF = 4096
EPS = 1e-6
TM = 256
TM_WIDE = 512
TQ = 256
FQ = 512
HD = 64
SCALE = HD ** -0.5
MEM_HEADS = 4
MEM_HD = 256
MEM_SCALE = MEM_HD ** -0.5
NEG = -1e30
LEFT = 512
WIN = LEFT + TQ
VW = 1024
NREL_PAD = 384
PROJ = 3200
GATE0 = 1536
CHK0 = 1664
VMEM_BIG = 56 * 1024 * 1024

ADAM_LR = 0.001
ADAM_B1 = 0.9
ADAM_B2 = 0.999
ADAM_EPS = 1e-08
ADAM_WD = 0.01
ADAM_STEP = 10

N_IN = 385
R_IN = 400
R_REST = 1664
W_ROWS = {'w_ff1': (0, 512), 'w_ff2': (512, 512),
          'w_out': (1024, 128), 'w_mq': (1152, 128), 'w_mk': (1280, 128), 'w_mv': (1408, 128), 'w_mo': (1536, 128)}
SMALL_ROWS = 24
SMALL_SLOT = {'rel_bias': (0, 8, 0, 257), 'b_fgt': (8, 1, 0, 8), 'g_fox_out': (9, 1, 0, 512), 'g_chk_out': (9, 1, 512, 512),
              'g_mix_pre': (10, 1, 0, 1024), 'g_mix_post': (11, 1, 0, 1024), 'g_mem_kv': (12, 1, 0, 1024),
              'g_mem_pre': (13, 1, 0, 1024), 'g_mem_post': (14, 1, 0, 1024), 'g_ff_pre': (15, 1, 0, 1024),
              'g_ff_post': (16, 1, 0, 1024)}

WEIGHTS = ['w_in', 'b_fgt', 'rel_bias', 'g_fox_out', 'g_chk_out', 'w_out', 'g_mix_pre', 'g_mix_post', 'g_mem_kv',
           'w_mq', 'w_mk', 'w_mv', 'w_mo', 'g_mem_pre', 'g_mem_post', 'w_ff1', 'w_ff2', 'g_ff_pre', 'g_ff_post']
BIG = ['w_in', 'w_out', 'w_mq', 'w_mk', 'w_mv', 'w_mo', 'w_ff1', 'w_ff2']
SMALL = [n for n in WEIGHTS if n not in BIG]


def _pcall(body, **kw):
    return pl.pallas_call(body, **kw)


def _nn(a, b):
    return jnp.dot(a, b, preferred_element_type=F32)


def _nt(a, b):
    return lax.dot_general(a, b, (((1,), (1,)), ((), ())), preferred_element_type=F32)


def _tn(a, b):
    return lax.dot_general(a, b, (((0,), (0,)), ((), ())), preferred_element_type=F32)


def _w(ref):
    v = ref[...]
    return v if v.ndim == 2 else v.reshape(-1, v.shape[-1])


def _rstd(x):
    return lax.rsqrt(jnp.mean(x * x, axis=-1, keepdims=True) + EPS)


def _rms(x, g):
    return x * _rstd(x) * g


def _rms_bwd(x, g, dy):
    r = _rstd(x)
    xh = x * r
    dg = jnp.sum(dy * xh, axis=0, keepdims=True)
    dxh = dy * g
    dx = r * (dxh - xh * jnp.mean(dxh * xh, axis=-1, keepdims=True))
    return dx, dg


def _resident(a):
    if isinstance(a, tuple):
        _, shape, index = a
        return pl.BlockSpec(shape, lambda *_: index, pipeline_mode=pl.Buffered(1))
    return pl.BlockSpec(a.shape, lambda *_, nd=a.ndim: (0,) * nd, pipeline_mode=pl.Buffered(1))


def _wblk(gw, name):
    r0, rows = W_ROWS[name]
    return (gw, (8, rows, D), (0, r0 // rows, 0))


def _tok_call(body, name, tiled, full, outs_tiled, outs_acc=(), rows=T, tm=TM, vmem=None):
    in_specs = [pl.BlockSpec((tm, a.shape[1]), lambda i: (i, 0)) for a in tiled]
    in_specs += [_resident(a) for a in full]
    full = [a[0] if isinstance(a, tuple) else a for a in full]
    out_shape = [jax.ShapeDtypeStruct((rows, c), dt) for c, dt in outs_tiled]
    out_shape += [jax.ShapeDtypeStruct(s, F32) for s in outs_acc]
    out_specs = [pl.BlockSpec((tm, c), lambda i: (i, 0)) for c, _ in outs_tiled]
    out_specs += [pl.BlockSpec(s, lambda i, nd=len(s): (0,) * nd) for s in outs_acc]
    return _pcall(
        body, name=name, grid=(rows // tm,), in_specs=in_specs, out_specs=out_specs, out_shape=out_shape,
        compiler_params=pltpu.CompilerParams(dimension_semantics=("arbitrary",), vmem_limit_bytes=vmem),
    )(*tiled, *full)


def _one_call(body, name, ins, outs, vmem=None):
    whole = lambda s: pl.BlockSpec(s, lambda i, nd=len(s): (0,) * nd)
    return _pcall(
        body, name=name, grid=(1,), in_specs=[_resident(a) for a in ins], out_specs=[whole(s) for s, _ in outs],
        out_shape=[jax.ShapeDtypeStruct(s, dt) for s, dt in outs],
        compiler_params=pltpu.CompilerParams(dimension_semantics=("arbitrary",), vmem_limit_bytes=vmem),
    )(*[a[0] if isinstance(a, tuple) else a for a in ins])


def _premix_fwd(x, g_pre, win_t):
    def body(x_ref, g_ref, w_ref, h_ref, proj_ref, flog_ref):
        h = _rms(x_ref[...], g_ref[...]).astype(BF16)
        h_ref[...] = h
        p = _nt(h, w_ref[...])
        proj_ref[...] = p.astype(BF16)
        flog_ref[...] = p[:, GATE0:GATE0 + 128]

    return _tok_call(body, "premix_fwd", [x], [g_pre, win_t],
                     [(D, BF16), (PROJ, BF16), (128, F32)], tm=TM_WIDE, vmem=VMEM_BIG)


def _postmix_fwd(x, o_f, o_c, g_fo, g_co, w_out, g_post, g_mpre, w_mq):
    def body(x_ref, of_ref, oc_ref, gfo_ref, gco_ref, wo_ref, gp_ref, gm_ref, wq_ref,
             y_ref, z_ref, x1_ref, h2_ref, qm_ref):
        y_ref[:, :512] = _rms(of_ref[...], gfo_ref[...]).astype(BF16)
        y_ref[:, 512:] = _rms(oc_ref[...], gco_ref[...]).astype(BF16)
        z = _nn(y_ref[...], _w(wo_ref))
        z_ref[...] = z
        x1 = x_ref[...] + _rms(z, gp_ref[...])
        x1_ref[...] = x1
        h2 = _rms(x1, gm_ref[...]).astype(BF16)
        h2_ref[...] = h2
        qm_ref[...] = _nn(h2, _w(wq_ref)).astype(BF16)

    return _tok_call(body, "postmix_fwd", [x, o_f, o_c], [g_fo, g_co, w_out, g_post, g_mpre, w_mq],
                     [(D, BF16), (D, F32), (D, F32), (D, BF16), (D, BF16)], tm=TM_WIDE, vmem=VMEM_BIG)


def _memkv_fwd(mem, g_kv, w_mk, w_mv):
    def body(m_ref, g_ref, wk_ref, wv_ref, mn_ref, k_ref, v_ref):
        mn = _rms(m_ref[...], g_ref[...]).astype(BF16)
        mn_ref[...] = mn
        k_ref[...] = _nn(mn, _w(wk_ref)).astype(BF16)
        v_ref[...] = _nn(mn, _w(wv_ref)).astype(BF16)

    return _tok_call(body, "memkv_fwd", [mem], [g_kv, w_mk, w_mv],
                     [(D, BF16), (D, BF16), (D, BF16)], rows=NMEM, tm=NMEM, vmem=VMEM_BIG)


def _mem_fwd(qm, x1, km, vm, w_mo, g_post, g_fpre):
    def body(q_ref, x1_ref, k_ref, v_ref, wo_ref, gp_ref, gf_ref, om_ref, ym_ref, x2_ref, h3_ref):
        for h in range(MEM_HEADS):
            sl = slice(h * MEM_HD, (h + 1) * MEM_HD)
            s = _nt(q_ref[:, sl], k_ref[:, sl]) * MEM_SCALE
            p = jnp.exp(s - jnp.max(s, axis=-1, keepdims=True))
            p = p / jnp.sum(p, axis=-1, keepdims=True)
            om_ref[:, sl] = _nn(p.astype(BF16), v_ref[:, sl]).astype(BF16)
        ym = _nn(om_ref[...], _w(wo_ref))
        ym_ref[...] = ym
        x2 = x1_ref[...] + _rms(ym, gp_ref[...])
        x2_ref[...] = x2
        h3_ref[...] = _rms(x2, gf_ref[...]).astype(BF16)

    return _tok_call(body, "mem_fwd", [qm, x1], [km, vm, w_mo, g_post, g_fpre],
                     [(D, BF16), (D, F32), (D, F32), (D, BF16)], tm=TM_WIDE, vmem=VMEM_BIG)


def _ffn_fwd(h3, x2, tgt, w1_t, w2, g_post):
    def body(h_ref, x2_ref, t_ref, w1_ref, w2_ref, g_ref, a_ref, y_ref, dx_ref, loss_ref):
        @pl.when(pl.program_id(0) == 0)
        def _():
            loss_ref[...] = jnp.zeros_like(loss_ref)

        h = h_ref[...]
        y = jnp.zeros((TM_WIDE, D), F32)
        for c in range(4):
            cols = slice(c * (DFF // 4), (c + 1) * (DFF // 4))
            a = _nt(h, w1_ref[2 * c:2 * c + 2].reshape(DFF // 4, D))
            a_ref[:, cols] = a.astype(BF16)
            y = y + _nn(jnp.square(jnp.maximum(a, 0.0)).astype(BF16), w2_ref[2 * c:2 * c + 2].reshape(DFF // 4, D))
        y_ref[...] = y
        e = x2_ref[...] + _rms(y, g_ref[...]) - t_ref[...]
        dx_ref[...] = e * (1.0 / D)
        loss_ref[...] += 0.5 * jnp.sum(jnp.sum(e * e, axis=-1, keepdims=True) * (1.0 / D))

    return _tok_call(body, "ffn_fwd", [h3, x2, tgt], [w1_t, w2, g_post],
                     [(DFF, BF16), (D, F32), (D, F32)], [(8, 128)], tm=TM_WIDE, vmem=VMEM_BIG)


def _tri(lower):
    r = lax.broadcasted_iota(jnp.int32, (128, 128), 0)
    c = lax.broadcasted_iota(jnp.int32, (128, 128), 1)
    return jnp.where(r >= c if lower else c >= r, 1.0, 0.0).astype(F32)


def _hdot(a, b):
    return jnp.dot(a, b, preferred_element_type=F32, precision=lax.Precision.HIGHEST)


def _gate_fwd(flog, b_pad):
    def body(f_ref, b_ref, c_ref):
        tri = _tri(True)

        def step(i, carry):
            rows = pl.ds(pl.multiple_of(i * 128, 128), 128)
            z = f_ref[rows, :] + b_ref[...]
            lf = jnp.minimum(z, 0.0) - jnp.log(1.0 + jnp.exp(-jnp.abs(z)))
            cb = _hdot(tri, lf) + carry
            c_ref[rows, :] = cb
            return cb[127:128, :]

        lax.fori_loop(0, T // 128, step, jnp.zeros((1, 128), F32))

    return _one_call(body, "gate_fwd", [flog, b_pad], [((T, 128), F32)])[0]


def _gate_bwd(dc, flog, b_pad):
    def body(dc_ref, f_ref, b_ref, df_ref, db_ref):
        tri = _tri(False)

        def step(j, carry):
            run, db = carry
            i = T // 128 - 1 - j
            rows = pl.ds(pl.multiple_of(i * 128, 128), 128)
            dcb = dc_ref[rows, :]
            rb = _hdot(tri, dcb) + run
            z = f_ref[rows, :] + b_ref[...]
            df = rb * (1.0 / (1.0 + jnp.exp(z)))
            df_ref[rows, :] = df.astype(BF16)
            return run + jnp.sum(dcb, axis=0, keepdims=True), db + jnp.sum(df, axis=0, keepdims=True)

        _, db = lax.fori_loop(0, T // 128, step, (jnp.zeros((1, 128), F32), jnp.zeros((1, 128), F32)))
        db_ref[...] = jnp.broadcast_to(db, (8, 128))

    return _one_call(body, "gate_bwd", [dc, flog, b_pad], [((T, 128), BF16), ((8, 128), F32)])


def _lane_lo(rows=TQ):
    return lax.broadcasted_iota(jnp.int32, (rows, 128), 1) < HD


def _half(v, lo, a, scale=None):
    keep = lo if a == 0 else jnp.logical_not(lo)
    v = v.astype(F32) if scale is None else v.astype(F32) * scale
    return jnp.where(keep, v, 0.0).astype(BF16)


def _fox_specs():
    return [pl.BlockSpec((FQ, 128), lambda h, i: (i, h)),
            pl.BlockSpec((T, 128), lambda h, i: (0, 4 + h)),
            pl.BlockSpec((T, 128), lambda h, i: (0, 8 + h))]


def _lane_pick(x, at):
    lane = lax.broadcasted_iota(jnp.int32, x.shape, 1)
    return jnp.sum(jnp.where(lane == at, x, 0.0), axis=-1, keepdims=True)


def _fox_fwd(proj, c, ct3):
    def body(q_ref, k_ref, v_ref, c_ref, ct_ref, o_ref, l_ref):
        i = pl.program_id(1)
        lo = _lane_lo(FQ)
        causal = lax.broadcasted_iota(jnp.int32, (FQ, FQ), 1) <= lax.broadcasted_iota(jnp.int32, (FQ, FQ), 0)
        q = q_ref[...]
        qs = [_half(q, lo, a, SCALE) for a in range(2)]
        cqs = [_lane_pick(c_ref[...], 2 * pl.program_id(0) + a) for a in range(2)]

        def tile(off, carry, diagonal):
            kblk = k_ref[pl.ds(off, FQ), :]
            vblk = v_ref[pl.ds(off, FQ), :]
            new = []
            for a in range(2):
                m, l, acc = carry[a]
                s = _nt(qs[a], kblk) + (cqs[a] - ct_ref[a:a + 1, pl.ds(off, FQ)])
                if diagonal:
                    s = jnp.where(causal, s, NEG)
                m2 = jnp.maximum(m, jnp.max(s, axis=-1, keepdims=True))
                p = jnp.exp(s - m2)
                alpha = jnp.exp(m - m2)
                new.append((m2, alpha * l + jnp.sum(p, axis=-1, keepdims=True),
                            alpha * acc + _nn(p.astype(BF16), vblk)))
            return tuple(new)

        init = (jnp.full((FQ, 1), NEG, F32), jnp.zeros((FQ, 1), F32), jnp.zeros((FQ, 128), F32))
        carry = lax.fori_loop(0, i, lambda kb, c: tile(pl.multiple_of(kb * FQ, FQ), c, False), (init, init))
        carry = tile(pl.multiple_of(i * FQ, FQ), carry, True)
        outs = []
        for a in range(2):
            m, l, acc = carry[a]
            outs.append(acc / l)
            l_ref[:, 128 * a:128 * a + 128] = jnp.broadcast_to(m + jnp.log(l), (FQ, 128))
        o_ref[...] = jnp.where(lo, outs[0], outs[1])

    return _pcall(
        body, name="fox_fwd", grid=(4, T // FQ),
        in_specs=_fox_specs() + [pl.BlockSpec((FQ, 128), lambda h, i: (i, 0)),
                                 pl.BlockSpec((None, 2, T), lambda h, i: (h, 0, 0))],
        out_specs=[pl.BlockSpec((FQ, 128), lambda h, i: (i, h)), pl.BlockSpec((FQ, 256), lambda h, i: (i, h))],
        out_shape=[jax.ShapeDtypeStruct((T, 512), F32), jax.ShapeDtypeStruct((T, 1024), F32)],
        compiler_params=pltpu.CompilerParams(dimension_semantics=("arbitrary", "arbitrary"), vmem_limit_bytes=VMEM_BIG),
    )(proj, proj, proj, c, ct3)


def _fox_bwd(proj, c, ct3, o, lse, do):
    def body(q_ref, k_ref, v_ref, c_ref, ct_ref, o_ref, l_ref, do_ref, dq_ref, dkb_ref, dvb_ref, dct_ref, dcq_ref,
             dk_ref, dv_ref):
        i = pl.program_id(1)

        @pl.when(i == 0)
        def _():
            dk_ref[...] = jnp.zeros_like(dk_ref)
            dv_ref[...] = jnp.zeros_like(dv_ref)
            dct_ref[...] = jnp.zeros_like(dct_ref)

        lo = _lane_lo(FQ)
        causal = lax.broadcasted_iota(jnp.int32, (FQ, FQ), 1) <= lax.broadcasted_iota(jnp.int32, (FQ, FQ), 0)
        q = q_ref[...]
        do_v = do_ref[...]
        prod = do_v * o_ref[...]
        qs = [_half(q, lo, a, SCALE) for a in range(2)]
        dos = [_half(do_v, lo, a) for a in range(2)]
        deltas = [jnp.sum(jnp.where(lo if a == 0 else jnp.logical_not(lo), prod, 0.0), axis=-1, keepdims=True)
                  for a in range(2)]
        cqs = [_lane_pick(c_ref[...], 2 * pl.program_id(0) + a) for a in range(2)]
        las = [l_ref[:, 128 * a:128 * a + 1] for a in range(2)]

        def tile(off, carry, diagonal):
            kblk = k_ref[pl.ds(off, FQ), :]
            vblk = v_ref[pl.ds(off, FQ), :]
            new = []
            dk = jnp.zeros((128, FQ), F32)
            dv = jnp.zeros((128, FQ), F32)
            for a in range(2):
                dq_acc, rs = carry[a]
                s = _nt(qs[a], kblk) + (cqs[a] - ct_ref[a:a + 1, pl.ds(off, FQ)])
                if diagonal:
                    s = jnp.where(causal, s, NEG)
                p = jnp.exp(s - las[a])
                ds = p * (_nt(dos[a], vblk) - deltas[a])
                dsb = ds.astype(BF16)
                dk = dk + _tn(qs[a], dsb)
                dv = dv + _tn(dos[a], p.astype(BF16))
                dct_ref[a:a + 1, pl.ds(off, FQ)] -= jnp.sum(ds, axis=0, keepdims=True)
                new.append((dq_acc + _nn(dsb, kblk), rs + jnp.sum(ds, axis=-1, keepdims=True)))
            dk_ref[:, pl.ds(off, FQ)] += dk
            dv_ref[:, pl.ds(off, FQ)] += dv
            return tuple(new)

        init = (jnp.zeros((FQ, 128), F32), jnp.zeros((FQ, 1), F32))
        carry = lax.fori_loop(0, i, lambda kb, c: tile(pl.multiple_of(kb * FQ, FQ), c, False), (init, init))
        carry = tile(pl.multiple_of(i * FQ, FQ), carry, True)
        lane = lax.broadcasted_iota(jnp.int32, (FQ, 128), 1)
        dcq_ref[...] = jnp.where(lane == 0, carry[0][1], jnp.where(lane == 1, carry[1][1], 0.0))
        dq_ref[...] = (jnp.where(lo, carry[0][0], carry[1][0]) * SCALE).astype(BF16)

        @pl.when(i == T // FQ - 1)
        def _():
            dkb_ref[...] = dk_ref[...].T.astype(BF16)
            dvb_ref[...] = dv_ref[...].T.astype(BF16)

    blk = pl.BlockSpec((FQ, 128), lambda h, i: (i, h))
    wide = pl.BlockSpec((FQ, 256), lambda h, i: (i, h))
    rows = pl.BlockSpec((None, 2, T), lambda h, i: (h, 0, 0))
    col = pl.BlockSpec((T, 128), lambda h, i: (0, h))
    return _pcall(
        body, name="fox_bwd", grid=(4, T // FQ),
        in_specs=_fox_specs() + [pl.BlockSpec((FQ, 128), lambda h, i: (i, 0)), rows, blk, wide, blk],
        out_specs=[blk, col, col, rows, pl.BlockSpec((None, FQ, 128), lambda h, i: (h, i, 0))],
        out_shape=[jax.ShapeDtypeStruct((T, 512), BF16), jax.ShapeDtypeStruct((T, 512), BF16),
                   jax.ShapeDtypeStruct((T, 512), BF16), jax.ShapeDtypeStruct((4, 2, T), F32),
                   jax.ShapeDtypeStruct((4, T, 128), F32)],
        scratch_shapes=[pltpu.VMEM((128, T), F32), pltpu.VMEM((128, T), F32)],
        compiler_params=pltpu.CompilerParams(dimension_semantics=("arbitrary", "arbitrary"), vmem_limit_bytes=VMEM_BIG),
    )(proj, proj, proj, c, ct3, o, lse, do)


def _rel_onehot():
    ridx = lax.broadcasted_iota(jnp.int32, (NREL_PAD, VW), 0)
    j = lax.broadcasted_iota(jnp.int32, (NREL_PAD, VW), 1)
    return jnp.where(ridx == jnp.clip(TQ + LEFT - 1 - j, -128, 128) + 128, 1.0, 0.0).astype(F32)


def _relvec_fwd(tbl):
    def body(t_ref, v_ref):
        v_ref[...] = _hdot(t_ref[...], _rel_onehot())

    return _one_call(body, "relvec_fwd", [tbl], [((8, VW), F32)])[0]


def _relvec_bwd(gv):
    def body(g_ref, t_ref):
        t_ref[...] = lax.dot_general(g_ref[...], _rel_onehot(), (((1,), (1,)), ((), ())),
                                     preferred_element_type=F32, precision=lax.Precision.HIGHEST)

    return _one_call(body, "relvec_bwd", [gv], [((8, NREL_PAD), F32)])[0]


def _chk_bias(vt_ref, a, hidden):
    vb = jnp.broadcast_to(vt_ref[a:a + 1, :], (TQ, VW))
    y = pltpu.roll(vb, VW - (TQ - 1), 1, stride=1, stride_axis=0)[:, :WIN]
    cr = lax.broadcasted_iota(jnp.int32, (TQ, WIN), 0) // 64
    m = lax.broadcasted_iota(jnp.int32, (TQ, WIN), 1)
    return jnp.where((m // 64 >= cr) & (m // 64 <= cr + 8) & (m >= hidden), y, NEG)


def _chk_specs():
    return [pl.BlockSpec((TQ, 128), lambda h, i: (i, CHK0 // 128 + h)),
            pl.BlockSpec((T + LEFT, 128), lambda h, i: (0, h)),
            pl.BlockSpec((T + LEFT, 128), lambda h, i: (0, 4 + h)),
            pl.BlockSpec((None, 2, VW), lambda h, i: (h, 0, 0))]


def _chk_fwd(proj, kvp, vt3):
    def body(q_ref, k_ref, v_ref, vt_ref, o_ref, l_ref, bias_ref):
        i = pl.program_id(1)

        @pl.when(i == 0)
        def _():
            for first in range(3):
                for a in range(2):
                    bias_ref[first, a] = _chk_bias(vt_ref, a, max(LEFT - first * TQ, 0))

        lo = _lane_lo()
        off = pl.multiple_of(i * TQ, TQ)
        kw = k_ref[pl.ds(off, WIN), :]
        vw = v_ref[pl.ds(off, WIN), :]
        bias_at = jnp.minimum(i, 2)
        q = q_ref[...]
        outs = []
        for a in range(2):
            s = _nt(_half(q, lo, a, SCALE), kw) + bias_ref[bias_at, a]
            m = jnp.max(s, axis=-1, keepdims=True)
            p = jnp.exp(s - m)
            l = jnp.sum(p, axis=-1, keepdims=True)
            outs.append(_nn(p.astype(BF16), vw) / l)
            l_ref[:, 128 * a:128 * a + 128] = jnp.broadcast_to(m + jnp.log(l), (TQ, 128))
        o_ref[...] = jnp.where(lo, outs[0], outs[1])

    return _pcall(
        body, name="chk_fwd", grid=(4, T // TQ), in_specs=_chk_specs(),
        out_specs=[pl.BlockSpec((TQ, 128), lambda h, i: (i, h)), pl.BlockSpec((TQ, 256), lambda h, i: (i, h))],
        out_shape=[jax.ShapeDtypeStruct((T, 512), F32), jax.ShapeDtypeStruct((T, 1024), F32)],
        scratch_shapes=[pltpu.VMEM((3, 2, TQ, WIN), F32)],
        compiler_params=pltpu.CompilerParams(dimension_semantics=("arbitrary", "arbitrary")),
    )(proj, kvp, kvp, vt3)


def _chk_bwd(proj, kvp, vt3, o, lse, do):
    nq = T // TQ

    def body(q_ref, k_ref, v_ref, vt_ref, o_ref, l_ref, do_ref, dq_ref, dkb_ref, dvb_ref, gv_ref, bias_ref, dsum_ref,
             dk_ref, dv_ref):
        i = pl.program_id(1)

        @pl.when(i == 0)
        def _():
            for first in range(3):
                for a in range(2):
                    bias_ref[first, a] = _chk_bias(vt_ref, a, max(LEFT - first * TQ, 0))
            dsum_ref[...] = jnp.zeros_like(dsum_ref)
            dk_ref[...] = jnp.zeros_like(dk_ref)
            dv_ref[...] = jnp.zeros_like(dv_ref)

        lo = _lane_lo()
        off = pl.multiple_of(i * TQ, TQ)
        kw = k_ref[pl.ds(off, WIN), :]
        vw = v_ref[pl.ds(off, WIN), :]
        bias_at = jnp.minimum(i, 2)
        q = q_ref[...]
        do_v = do_ref[...]
        prod = do_v * o_ref[...]
        dqs = []
        for a in range(2):
            keep = lo if a == 0 else jnp.logical_not(lo)
            qa = _half(q, lo, a, SCALE)
            doa = _half(do_v, lo, a)
            delta = jnp.sum(jnp.where(keep, prod, 0.0), axis=-1, keepdims=True)
            s = _nt(qa, kw) + bias_ref[bias_at, a]
            p = jnp.exp(s - l_ref[:, 128 * a:128 * a + 1])
            ds = p * (_nt(doa, vw) - delta)
            dsum_ref[a] += ds
            dsb = ds.astype(BF16)
            dk_ref[:, pl.ds(off, WIN)] += _tn(qa, dsb)
            dv_ref[:, pl.ds(off, WIN)] += _tn(doa, p.astype(BF16))
            dqs.append(_nn(dsb, kw))
        dq_ref[...] = (jnp.where(lo, dqs[0], dqs[1]) * SCALE).astype(BF16)

        @pl.when(i == nq - 1)
        def _():
            dkb_ref[...] = dk_ref[:, LEFT:].T.astype(BF16)
            dvb_ref[...] = dv_ref[:, LEFT:].T.astype(BF16)
            rr = lax.broadcasted_iota(jnp.int32, (TQ, TQ), 0)
            cc = lax.broadcasted_iota(jnp.int32, (TQ, TQ), 1)
            flip = jnp.where(rr + cc == TQ - 1, 1.0, 0.0).astype(F32)
            for a in range(2):
                dpad = jnp.concatenate([dsum_ref[a], jnp.zeros((TQ, VW - WIN), F32)], axis=1)
                z = pltpu.roll(_hdot(flip, dpad), 0, 1, stride=1, stride_axis=0)
                gv_ref[a:a + 1, :] = jnp.sum(z, axis=0, keepdims=True)

    blk = pl.BlockSpec((TQ, 128), lambda h, i: (i, h))
    wide = pl.BlockSpec((TQ, 256), lambda h, i: (i, h))
    col = pl.BlockSpec((T, 128), lambda h, i: (0, h))
    return _pcall(
        body, name="chk_bwd", grid=(4, nq), in_specs=_chk_specs() + [blk, wide, blk],
        out_specs=[blk, col, col, pl.BlockSpec((None, 2, VW), lambda h, i: (h, 0, 0))],
        out_shape=[jax.ShapeDtypeStruct((T, 512), BF16), jax.ShapeDtypeStruct((T, 512), BF16),
                   jax.ShapeDtypeStruct((T, 512), BF16), jax.ShapeDtypeStruct((4, 2, VW), F32)],
        scratch_shapes=[pltpu.VMEM((3, 2, TQ, WIN), F32), pltpu.VMEM((2, TQ, WIN), F32),
                        pltpu.VMEM((128, T + LEFT), F32), pltpu.VMEM((128, T + LEFT), F32)],
        compiler_params=pltpu.CompilerParams(dimension_semantics=("arbitrary", "arbitrary")),
    )(proj, kvp, kvp, vt3, o, lse, do)


def _zero_at_start(*refs):
    @pl.when(pl.program_id(0) == 0)
    def _():
        for r in refs:
            r[...] = jnp.zeros_like(r)


def _ffn_bwd(dx3, y3, x2, a, w1_t, w2, g_post, g_pre):
    def body(dx3_ref, y_ref, x2_ref, a_ref, w1_ref, w2_ref, gp_ref, gf_ref,
             dx2_ref, da_ref, dy_ref, r_ref, dgp_ref, dgf_ref):
        _zero_at_start(dgp_ref, dgf_ref)
        dx3_v = dx3_ref[...]
        dy, dgp = _rms_bwd(y_ref[...], gp_ref[...], dx3_v)
        dgp_ref[...] += dgp
        dyb = dy.astype(BF16)
        dy_ref[...] = dyb
        ra = jnp.maximum(a_ref[...].astype(F32), 0.0)
        r_ref[...] = jnp.square(ra).astype(BF16)
        da = (_nt(dyb, _w(w2_ref)) * (2.0 * ra)).astype(BF16)
        da_ref[...] = da
        dh, dgf = _rms_bwd(x2_ref[...], gf_ref[...], _nn(da, _w(w1_ref)))
        dgf_ref[...] += dgf
        dx2_ref[...] = dx3_v + dh

    return _tok_call(body, "ffn_bwd", [dx3, y3, x2, a], [w1_t, w2, g_post, g_pre],
                     [(D, F32), (DFF, BF16), (D, BF16), (DFF, BF16)], [(1, D), (1, D)], vmem=VMEM_BIG)


def _mem_bwd(dx2, ym, x1, qm, km, vm, w_mo, w_mq, g_post, g_pre):
    def body(dx2_ref, ym_ref, x1_ref, q_ref, k_ref, v_ref, wo_ref, wq_ref, gp_ref, gm_ref,
             dx1_ref, dym_ref, dq_ref, dk_ref, dv_ref, dgp_ref, dgm_ref, dom_ref):
        _zero_at_start(dk_ref, dv_ref, dgp_ref, dgm_ref)
        dx2_v = dx2_ref[...]
        dym, dgp = _rms_bwd(ym_ref[...], gp_ref[...], dx2_v)
        dgp_ref[...] += dgp
        dymb = dym.astype(BF16)
        dym_ref[...] = dymb
        dom_ref[...] = _nt(dymb, _w(wo_ref)).astype(BF16)
        for h in range(MEM_HEADS):
            sl = slice(h * MEM_HD, (h + 1) * MEM_HD)
            qh, kh, doh = q_ref[:, sl], k_ref[:, sl], dom_ref[:, sl]
            s = _nt(qh, kh) * MEM_SCALE
            p = jnp.exp(s - jnp.max(s, axis=-1, keepdims=True))
            p = p / jnp.sum(p, axis=-1, keepdims=True)
            dp = _nt(doh, v_ref[:, sl])
            ds = (p * (dp - jnp.sum(p * dp, axis=-1, keepdims=True))).astype(BF16)
            dq_ref[:, sl] = (_nn(ds, kh) * MEM_SCALE).astype(BF16)
            dk_ref[:, sl] += _tn(ds, qh) * MEM_SCALE
            dv_ref[:, sl] += _tn(p.astype(BF16), doh)
        dh, dgm = _rms_bwd(x1_ref[...], gm_ref[...], _nt(dq_ref[...], _w(wq_ref)))
        dgm_ref[...] += dgm
        dx1_ref[...] = dx2_v + dh

    tiled = pl.BlockSpec((TM_WIDE, D), lambda i: (i, 0))
    in_specs = [tiled] * 4 + [_resident(a) for a in (km, vm, w_mo, w_mq, g_post, g_pre)]
    w_mo, w_mq = w_mo[0], w_mq[0]
    kv = pl.BlockSpec((NMEM, D), lambda i: (0, 0))
    vec = pl.BlockSpec((1, D), lambda i: (0, 0))
    return _pcall(
        body, name="mem_bwd", grid=(T // TM_WIDE,), in_specs=in_specs,
        out_specs=[tiled, tiled, tiled, kv, kv, vec, vec],
        out_shape=[jax.ShapeDtypeStruct((T, D), F32), jax.ShapeDtypeStruct((T, D), BF16),
                   jax.ShapeDtypeStruct((T, D), BF16), jax.ShapeDtypeStruct((NMEM, D), F32),
                   jax.ShapeDtypeStruct((NMEM, D), F32), jax.ShapeDtypeStruct((1, D), F32),
                   jax.ShapeDtypeStruct((1, D), F32)],
        scratch_shapes=[pltpu.VMEM((TM_WIDE, D), BF16)],
        compiler_params=pltpu.CompilerParams(dimension_semantics=("arbitrary",), vmem_limit_bytes=VMEM_BIG),
    )(dx2, ym, x1, qm, km, vm, w_mo, w_mq, g_post, g_pre)


def _memkv_bwd(dkm, dvm, mem, w_mk, w_mv):
    def body(dk_ref, dv_ref, m_ref, wk_ref, wv_ref, dg_ref):
        dmn = _nt(dk_ref[...].astype(BF16), _w(wk_ref)) + _nt(dv_ref[...].astype(BF16), _w(wv_ref))
        mv = m_ref[...]
        dg_ref[...] = jnp.sum(dmn * (mv * _rstd(mv)), axis=0, keepdims=True)

    return _one_call(body, "memkv_bwd", [dkm, dvm, mem, w_mk, w_mv], [((1, D), F32)], vmem=VMEM_BIG)[0]


def _postmix_bwd(dx1, z, o_f, o_c, w_out, g_post, g_fo, g_co):
    def body(dx1_ref, z_ref, of_ref, oc_ref, wo_ref, gp_ref, gfo_ref, gco_ref,
             dz_ref, dof_ref, doc_ref, dgp_ref, dgfo_ref, dgco_ref):
        _zero_at_start(dgp_ref, dgfo_ref, dgco_ref)
        dz, dgp = _rms_bwd(z_ref[...], gp_ref[...], dx1_ref[...])
        dgp_ref[...] += dgp
        dzb = dz.astype(BF16)
        dz_ref[...] = dzb
        dy = _nt(dzb, _w(wo_ref))
        dof, dgfo = _rms_bwd(of_ref[...], gfo_ref[...], dy[:, :512])
        doc, dgco = _rms_bwd(oc_ref[...], gco_ref[...], dy[:, 512:])
        dof_ref[...] = dof
        doc_ref[...] = doc
        dgfo_ref[...] += dgfo
        dgco_ref[...] += dgco

    return _tok_call(body, "postmix_bwd", [dx1, z, o_f, o_c], [w_out, g_post, g_fo, g_co],
                     [(D, BF16), (512, F32), (512, F32)], [(1, D), (1, 512), (1, 512)], tm=TM_WIDE, vmem=VMEM_BIG)


def _premix_bwd(dx1, x, pieces, win_t, g_pre):
    def body(dx1_ref, x_ref, *refs):
        piece_refs, (w_ref, g_ref, dx_ref, dp_ref, dg_ref) = refs[:len(pieces)], refs[len(pieces):]
        _zero_at_start(dg_ref)
        col = 0
        for p in piece_refs:
            dp_ref[:, col:col + p.shape[1]] = p[...]
            col += p.shape[1]
        dh, dg = _rms_bwd(x_ref[...], g_ref[...], _nn(dp_ref[...], w_ref[...]))
        dg_ref[...] += dg
        dx_ref[...] = dx1_ref[...] + dh

    return _tok_call(body, "premix_bwd", [dx1, x] + list(pieces), [win_t, g_pre], [(D, F32), (PROJ, BF16)], [(1, D)],
                     tm=TM_WIDE, vmem=VMEM_BIG)


def _wgrad(a, b, name):
    k, m = a.shape
    n = b.shape[1]
    tm = 640 if m % 640 == 0 and m > 1024 else min(m, 512)
    tn = min(n, 1024)

    def body(a_ref, b_ref, o_ref):
        o_ref[...] = _tn(a_ref[...].astype(BF16), b_ref[...].astype(BF16))

    return _pcall(
        body, name=name, grid=(m // tm, n // tn),
        in_specs=[pl.BlockSpec((k, tm), lambda i, j: (0, i)), pl.BlockSpec((k, tn), lambda i, j: (0, j))],
        out_specs=pl.BlockSpec((tm, tn), lambda i, j: (i, j)),
        out_shape=jax.ShapeDtypeStruct((m, n), F32),
        compiler_params=pltpu.CompilerParams(dimension_semantics=("arbitrary", "arbitrary"), vmem_limit_bytes=VMEM_BIG),
    )(a, b)


def _wgrad_group(name, pairs, rows):
    def body(*refs):
        o_ref = refs[-1]
        for k in range(len(pairs)):
            o_ref[k * rows:(k + 1) * rows, :] = _tn(refs[2 * k][...].astype(BF16), refs[2 * k + 1][...].astype(BF16))

    in_specs, ops = [], []
    for a, b in pairs:
        in_specs += [pl.BlockSpec((a.shape[0], rows), lambda j: (0, j)), _resident(b)]
        ops += [a, b]
    return _pcall(
        body, name=name, grid=(8,), in_specs=in_specs,
        out_specs=pl.BlockSpec((None, len(pairs) * rows, D), lambda j: (j, 0, 0)),
        out_shape=jax.ShapeDtypeStruct((8, len(pairs) * rows, D), F32),
        compiler_params=pltpu.CompilerParams(dimension_semantics=("arbitrary",), vmem_limit_bytes=VMEM_BIG),
    )(*ops)


def _adam_math(w, g, m, v):
    m2 = ADAM_B1 * m + (1.0 - ADAM_B1) * g
    v2 = ADAM_B2 * v + (1.0 - ADAM_B2) * jnp.square(g)
    m_hat = m2 / (1.0 - ADAM_B1 ** ADAM_STEP)
    v_hat = v2 / (1.0 - ADAM_B2 ** ADAM_STEP)
    delta = -ADAM_LR * (m_hat / (jnp.sqrt(v_hat) + ADAM_EPS) + ADAM_WD * w)
    return delta, m2, v2


def _adamw_small(gparts, ws, ms, vs):
    n = len(SMALL)

    def body(g_ref, *refs):
        w_refs, m_refs, v_refs = refs[:n], refs[n:2 * n], refs[2 * n:3 * n]
        outs, sum_ref = refs[3 * n:-1], refs[-1]
        g = g_ref[0]
        for k in range(1, 8):
            g = g + g_ref[k]
        sum_ref[...] = g
        outs[0][...] = sum_ref[17:18, 0:128]
        for t, name in enumerate(SMALL):
            r0, nr, c0, nc = SMALL_SLOT[name]
            gt = sum_ref[r0:r0 + nr, c0:c0 + nc]
            out = (gt,) + _adam_math(w_refs[t][...], gt, m_refs[t][...], v_refs[t][...])
            for o_ref, val in zip(outs[1 + 4 * t:5 + 4 * t], out):
                o_ref[...] = val

    whole = lambda s: pl.BlockSpec(s, lambda i, nd=len(s): (0,) * nd)
    ins = [gparts] + list(ws) + list(ms) + list(vs)
    out_shapes = [(1, 128)] + [a.shape for a in ws for _ in range(4)]
    return _pcall(
        body, name="adamw_small", grid=(1,), in_specs=[whole(a.shape) for a in ins],
        out_specs=[whole(s) for s in out_shapes], out_shape=[jax.ShapeDtypeStruct(s, F32) for s in out_shapes],
        scratch_shapes=[pltpu.VMEM((SMALL_ROWS, D), F32)],
        compiler_params=pltpu.CompilerParams(dimension_semantics=("arbitrary",)),
    )(*ins)


def _row_tile(rows):
    return next(t for t in (512, 400, 320) if rows % t == 0)


def _add_halves(g4, theirs, core, name):
    rows = g4.shape[2]
    tr = _row_tile(rows)

    def body(c_ref, a_ref, b_ref, o_ref):
        o_ref[...] = (a_ref[...] + b_ref[...]).astype(BF16)

    grid_spec = pltpu.PrefetchScalarGridSpec(
        num_scalar_prefetch=1, grid=(4, rows // tr),
        in_specs=[pl.BlockSpec((None, None, tr, D), lambda j, i, c: (j, c[0], i, 0)),
                  pl.BlockSpec((None, None, tr, D), lambda j, i, c: (j, 0, i, 0))],
        out_specs=pl.BlockSpec((None, tr, D), lambda j, i, c: (j, i, 0)))
    return _pcall(
        body, name=name, grid_spec=grid_spec, out_shape=jax.ShapeDtypeStruct((4, rows, D), BF16),
        compiler_params=pltpu.CompilerParams(dimension_semantics=("arbitrary", "arbitrary")),
    )(core, g4, theirs)


def _sum_adam(own, got, order, r0, w, m, v, name, transposed=False):
    n = w.shape[1] if transposed else w.shape[0]
    tr = min(n, 256) if n % 8 == 0 else n
    rows = tr if n % 8 == 0 else own.shape[1]

    def body(o_ref, a_ref, b_ref, c_ref, d_ref, w_ref, m_ref, v_ref, g_ref, dl_ref, m2_ref, v2_ref):
        f = lambda r: r[0:tr, :].astype(F32)
        g = ((f(a_ref) + f(b_ref)) + f(c_ref)) + f(d_ref)
        g = g.T if transposed else g
        g_ref[...] = g
        dl_ref[...], m2_ref[...], v2_ref[...] = _adam_math(w_ref[...], g, m_ref[...], v_ref[...])

    slot = lambda k: pl.BlockSpec((None, rows, D), lambda i, o: (o[k], r0 // rows + i, 0))
    wspec = pl.BlockSpec((D, tr), lambda i, o: (0, i)) if transposed else pl.BlockSpec((tr, D), lambda i, o: (i, 0))
    grid_spec = pltpu.PrefetchScalarGridSpec(
        num_scalar_prefetch=1, grid=(n // tr,), in_specs=[slot(0), slot(1), slot(2), slot(3), wspec, wspec, wspec],
        out_specs=[wspec] * 4)
    return _pcall(
        body, name=name, grid_spec=grid_spec, out_shape=[jax.ShapeDtypeStruct(w.shape, F32)] * 4,
        compiler_params=pltpu.CompilerParams(dimension_semantics=("arbitrary",)),
    )(order, own, got, got, got, w, m, v)


def _sum_adam_rows(own, got, order, ws, ms, vs, name):
    n, rows = len(ws), ws[0].shape[0]

    def body(o_ref, a_ref, b_ref, c_ref, d_ref, *refs):
        ins, outs = refs[:3 * n], refs[3 * n:]
        for t in range(n):
            r = slice(t * rows, (t + 1) * rows)
            f = lambda ref: ref[r, :].astype(F32)
            g = ((f(a_ref) + f(b_ref)) + f(c_ref)) + f(d_ref)
            out = (g,) + _adam_math(ins[t][...], g, ins[n + t][...], ins[2 * n + t][...])
            for o, val in zip(outs[4 * t:4 * t + 4], out):
                o[...] = val

    slot = lambda k: pl.BlockSpec((None, n * rows, D), lambda i, o: (o[k], 0, 0), pipeline_mode=pl.Buffered(1))
    wspec = pl.BlockSpec((rows, D), lambda i, o: (0, 0), pipeline_mode=pl.Buffered(1))
    grid_spec = pltpu.PrefetchScalarGridSpec(
        num_scalar_prefetch=1, grid=(1,), in_specs=[slot(0), slot(1), slot(2), slot(3)] + [wspec] * (3 * n),
        out_specs=[pl.BlockSpec((rows, D), lambda i, o: (0, 0))] * (4 * n))
    return _pcall(
        body, name=name, grid_spec=grid_spec, out_shape=[jax.ShapeDtypeStruct((rows, D), F32)] * (4 * n),
        compiler_params=pltpu.CompilerParams(dimension_semantics=("arbitrary",), vmem_limit_bytes=VMEM_BIG),
    )(order, own, got, got, got, *ws, *ms, *vs)


def _place():
    return lax.axis_index("x"), lax.axis_index("y"), lax.axis_index("c")


def _allgather(block, name):
    def body(x_ref, out_ref, token, send_sems, recv_sems, local_sem):
        token[...] = jnp.zeros_like(token)
        x, y, c = _place()
        me, sibling = (x, y, c), (x, y, 1 - c)
        chips = [(1 - x, y), (x, 1 - y), (1 - x, 1 - y)]

        def slot(px, py, pc):
            return out_ref.at[4 * px + 2 * py + pc]

        def copy(k, blk, to, src=None):
            return pltpu.make_async_remote_copy(
                src_ref=slot(*blk) if src is None else src, dst_ref=slot(*blk),
                send_sem=send_sems.at[k], recv_sem=recv_sems.at[k], device_id=to, device_id_type=MESH)

        mine = pltpu.make_async_copy(x_ref, slot(*me), local_sem)
        mine.start()
        first = [copy(0, me, sibling, src=x_ref)]
        first += [copy(1 + j, me, (*chip, c), src=x_ref) for j, chip in enumerate(chips)]
        for cp in first:
            cp.start()
        passed = [copy(4 + j, (*chip, c), sibling) for j, chip in enumerate(chips)]
        for j, chip in enumerate(chips):
            copy(1 + j, (*chip, c), me).wait_recv()
            passed[j].start()
        copy(0, sibling, me).wait_recv()
        for j, chip in enumerate(chips):
            copy(4 + j, (*chip, 1 - c), me).wait_recv()
        for cp in first + passed:
            cp.wait_send()
        mine.wait()

    return _pcall(
        body, name=name,
        out_shape=[jax.ShapeDtypeStruct((8,) + block.shape, block.dtype), jax.ShapeDtypeStruct((8, 128), F32)],
        in_specs=[pl.BlockSpec(memory_space=pl.ANY)],
        out_specs=[pl.BlockSpec(memory_space=pl.ANY), pl.BlockSpec(memory_space=pltpu.VMEM)],
        scratch_shapes=[pltpu.SemaphoreType.DMA((7,)), pltpu.SemaphoreType.DMA((7,)), pltpu.SemaphoreType.DMA(())],
        compiler_params=pltpu.CompilerParams(has_side_effects=True),
    )(block)


HBM_SPEC = pl.BlockSpec(memory_space=pltpu.HBM)
SEM_SPEC = pl.BlockSpec(memory_space=pltpu.SEMAPHORE)
ANY_SPEC = pl.BlockSpec(memory_space=pl.ANY)
EFFECT = pltpu.SideEffectType.DATAFLOW_SIDE_EFFECTING


def _in_hbm(a):
    return pltpu.with_memory_space_constraint(a, pltpu.HBM)


def _start_copies(name, src, land_shape, plan, n):
    def body(src_ref, land_ref, send_sems, recv_sems, src_thru, land_thru, token):
        for k, (s, d, to, _) in enumerate(plan(src_ref, land_ref)):
            pltpu.make_async_remote_copy(src_ref=s, dst_ref=d, send_sem=send_sems.at[k], recv_sem=recv_sems.at[k],
                                         device_id=to, device_id_type=MESH).start()
        token[...] = jnp.zeros_like(token)

    return _pcall(
        body, name=name,
        out_shape=(pltpu.SemaphoreType.DMA((n,)), pltpu.SemaphoreType.DMA((n,)), pltpu.HBM(src.shape, src.dtype),
                   pltpu.HBM(land_shape, src.dtype), jax.ShapeDtypeStruct((8, 128), F32)),
        in_specs=(HBM_SPEC, HBM_SPEC),
        out_specs=(SEM_SPEC, SEM_SPEC, HBM_SPEC, HBM_SPEC, pl.BlockSpec(memory_space=pltpu.VMEM)),
        input_output_aliases={0: 2, 1: 3}, compiler_params=pltpu.CompilerParams(has_side_effects=EFFECT),
    )(_in_hbm(src), _in_hbm(lax.empty(land_shape, src.dtype)))


def _wait_copies(name, started, after, plan):
    send_sems, recv_sems, src_thru, land_thru, _ = started

    def body(src_ref, land_ref, send_sems, recv_sems, *rest):
        for k, (s, _, to, mine) in enumerate(plan(src_ref, land_ref)):
            cp = pltpu.make_async_remote_copy(src_ref=s, dst_ref=mine, send_sem=send_sems.at[k],
                                              recv_sem=recv_sems.at[k], device_id=to, device_id_type=MESH)
            cp.wait_send()
            cp.wait_recv()

    return _pcall(
        body, name=name,
        out_shape=(pltpu.HBM(src_thru.shape, src_thru.dtype), pltpu.HBM(land_thru.shape, land_thru.dtype)),
        in_specs=(HBM_SPEC, HBM_SPEC, SEM_SPEC, SEM_SPEC) + (ANY_SPEC,) * len(after), out_specs=(HBM_SPEC, HBM_SPEC),
        input_output_aliases={0: 0, 1: 1}, compiler_params=pltpu.CompilerParams(has_side_effects=EFFECT),
    )(src_thru, land_thru, send_sems, recv_sems, *after)


def _start_inplace(name, buf, plan, n):
    def body(buf_ref, send_sems, recv_sems, buf_thru, token):
        for k, (s, d, to, _) in enumerate(plan(buf_ref, buf_ref)):
            pltpu.make_async_remote_copy(src_ref=s, dst_ref=d, send_sem=send_sems.at[k], recv_sem=recv_sems.at[k],
                                         device_id=to, device_id_type=MESH).start()
        token[...] = jnp.zeros_like(token)

    return _pcall(
        body, name=name,
        out_shape=(pltpu.SemaphoreType.DMA((n,)), pltpu.SemaphoreType.DMA((n,)), pltpu.HBM(buf.shape, buf.dtype),
                   jax.ShapeDtypeStruct((8, 128), F32)),
        in_specs=(HBM_SPEC,), out_specs=(SEM_SPEC, SEM_SPEC, HBM_SPEC, pl.BlockSpec(memory_space=pltpu.VMEM)),
        input_output_aliases={0: 2}, compiler_params=pltpu.CompilerParams(has_side_effects=EFFECT),
    )(_in_hbm(buf))


def _wait_inplace(name, started, after, plan):
    send_sems, recv_sems, buf_thru, _ = started

    def body(buf_ref, send_sems, recv_sems, *rest):
        for k, (s, _, to, mine) in enumerate(plan(buf_ref, buf_ref)):
            cp = pltpu.make_async_remote_copy(src_ref=s, dst_ref=mine, send_sem=send_sems.at[k],
                                              recv_sem=recv_sems.at[k], device_id=to, device_id_type=MESH)
            cp.wait_send()
            cp.wait_recv()

    return _pcall(
        body, name=name, out_shape=pltpu.HBM(buf_thru.shape, buf_thru.dtype),
        in_specs=(HBM_SPEC, SEM_SPEC, SEM_SPEC) + (ANY_SPEC,) * len(after), out_specs=HBM_SPEC,
        input_output_aliases={0: 0}, compiler_params=pltpu.CompilerParams(has_side_effects=EFFECT),
    )(buf_thru, send_sems, recv_sems, *after)


def _gather_plan(src_ref, land_ref):
    x, y, c = _place()
    peers = [(x, y, 1 - c), (1 - x, y, c), (x, 1 - y, c)]
    return [(src_ref, land_ref.at[4 * x + 2 * y + c], p, land_ref.at[4 * p[0] + 2 * p[1] + p[2]]) for p in peers]


def _relay_plan(buf_ref, _):
    x, y, c = _place()
    slot = lambda p, pc: 4 * p[0] + 2 * p[1] + pc
    xn, yn, dg, sib = (1 - x, y), (x, 1 - y), (1 - x, 1 - y), (x, y, 1 - c)
    half = buf_ref.shape[1] // 2
    lo, hi = pl.ds(0, half), pl.ds(half, half)
    return [(buf_ref.at[slot(xn, c)], buf_ref.at[slot(xn, c)], sib, buf_ref.at[slot(xn, 1 - c)]),
            (buf_ref.at[slot(yn, c)], buf_ref.at[slot(yn, c)], sib, buf_ref.at[slot(yn, 1 - c)]),
            (buf_ref.at[slot(xn, c), lo], buf_ref.at[slot(xn, c), lo], (*yn, c), buf_ref.at[slot(dg, c), lo]),
            (buf_ref.at[slot(yn, c), hi], buf_ref.at[slot(yn, c), hi], (*xn, c), buf_ref.at[slot(dg, c), hi])]


def _swap_plan(src_ref, land_ref):
    x, y, c = _place()
    return [(src_ref.at[:, pl.ds(1 - c, 1)], land_ref, (x, y, 1 - c), land_ref)]


def _exchange_plan(src_ref, land_ref):
    x, y, c = _place()
    chips = [(1 - x, y), (x, 1 - y), (1 - x, 1 - y)]
    return [(src_ref.at[2 * px + py], land_ref.at[2 * x + y], (px, py, c), land_ref.at[2 * px + py]) for px, py in chips]


def _gather_forward(land, block):
    def body(land_ref, out_ref, send_sems, recv_sems):
        x, y, c = _place()
        chips = [(1 - x, 1 - y)]

        def copy(k, px, py, pc):
            blk = out_ref.at[4 * px + 2 * py + pc]
            return pltpu.make_async_remote_copy(src_ref=blk, dst_ref=blk, send_sem=send_sems.at[k],
                                                recv_sem=recv_sems.at[k], device_id=(x, y, 1 - c), device_id_type=MESH)

        sent = [copy(k, px, py, c) for k, (px, py) in enumerate(chips)]
        for cp in sent:
            cp.start()
        for k, (px, py) in enumerate(chips):
            copy(k, px, py, 1 - c).wait_recv()
        for cp in sent:
            cp.wait_send()

    land = _pcall(
        body, name="allgather_rest_forward", out_shape=jax.ShapeDtypeStruct(land.shape, land.dtype),
        in_specs=[ANY_SPEC], out_specs=ANY_SPEC, input_output_aliases={0: 0},
        scratch_shapes=[pltpu.SemaphoreType.DMA((1,)), pltpu.SemaphoreType.DMA((1,))],
        compiler_params=pltpu.CompilerParams(has_side_effects=True),
    )(land)

    rows = block.shape[0]
    tr = rows // 4

    def place(me_ref, x_ref, land_ref, out_ref):
        out_ref[...] = x_ref[...]

    x, y, c = _place()
    grid_spec = pltpu.PrefetchScalarGridSpec(
        num_scalar_prefetch=1, grid=(rows // tr,),
        in_specs=[pl.BlockSpec((tr, D), lambda i, me: (i, 0)), ANY_SPEC],
        out_specs=pl.BlockSpec((None, tr, D), lambda i, me: (me[0], i, 0)))
    return _pcall(
        place, name="allgather_rest_own", grid_spec=grid_spec, out_shape=jax.ShapeDtypeStruct(land.shape, land.dtype),
        input_output_aliases={2: 0}, compiler_params=pltpu.CompilerParams(dimension_semantics=("arbitrary",)),
    )((4 * x + 2 * y + c).reshape(1), block, land)


class _ReduceScatter:
    def __init__(self, name, g):
        self.name = name
        rows = g.shape[1]
        self.started = _start_copies(name + "_swap_start", g.reshape(4, 2, rows, D), (4, 1, rows, D), _swap_plan, 1)
        self.token = self.started[4][0, 0]

    def halfway(self, after):
        g4, theirs = _wait_copies(self.name + "_swap_wait", self.started, after, _swap_plan)
        self.own = _add_halves(g4, theirs, lax.axis_index("c").reshape(1), self.name + "_add_halves")
        self.started = _start_copies(self.name + "_exch_start", self.own, self.own.shape, _exchange_plan, 3)
        self.token = self.started[4][0, 0]

    def finish(self, after):
        own, got = _wait_copies(self.name + "_exch_wait", self.started, after, _exchange_plan)
        chip = 2 * lax.axis_index("x") + lax.axis_index("y")
        return own, got, (chip + jnp.arange(4, dtype=jnp.int32)) % 4


def _pack_small(p, scalar=None):
    z = lambda a, n: jnp.pad(a, ((0, 0), (0, n - a.shape[1])))
    rows = [z(p['rel_bias'], D), z(p['b_fgt'], D), jnp.concatenate([p['g_fox_out'], p['g_chk_out']], axis=1)]
    rows += [p[n] for n in ('g_mix_pre', 'g_mix_post', 'g_mem_kv', 'g_mem_pre', 'g_mem_post', 'g_ff_pre', 'g_ff_post')]
    rows.append(jnp.zeros((1, D), F32) if scalar is None else z(jnp.reshape(scalar, (1, 1)), D))
    rows.append(jnp.zeros((SMALL_ROWS - 18, D), F32))
    return jnp.concatenate(rows, axis=0)


_GAP_DEV, _GAP_ROW = divmod(GATE0 + 8, N_IN)
_GAP = CHK0 - GATE0 - 8


def _in_rows_to_proj(g):
    runs = [(j, 0, N_IN, N_IN * j) for j in range(_GAP_DEV)]
    runs += [(_GAP_DEV, 0, _GAP_ROW, N_IN * _GAP_DEV), (_GAP_DEV, _GAP_ROW, N_IN, N_IN * _GAP_DEV + _GAP_ROW + _GAP)]
    runs += [(j, 0, N_IN, N_IN * j + _GAP) for j in range(_GAP_DEV + 1, 8)]

    def body(g_ref, o_ref, acc_ref):
        acc_ref[...] = jnp.zeros_like(acc_ref)
        for j, r0, r1, dest in runs:
            start, shift = dest // 16 * 16, dest % 16
            win = -(-(shift + r1 - r0) // 16) * 16
            r = lax.broadcasted_iota(jnp.int32, (win, R_IN), 0)
            c = lax.broadcasted_iota(jnp.int32, (win, R_IN), 1)
            move = jnp.where((c >= r0) & (c < r1) & (r == c - r0 + shift), 1.0, 0.0).astype(BF16)
            acc_ref[start:start + win, :] += _nn(move, g_ref[j])
        o_ref[...] = acc_ref[...].astype(BF16)

    return _pcall(
        body, name="w_in_layout", out_shape=jax.ShapeDtypeStruct((PROJ, D), BF16), grid=(1,),
        in_specs=[pl.BlockSpec(g.shape, lambda i: (0, 0, 0))], out_specs=pl.BlockSpec((PROJ, D), lambda i: (0, 0)),
        scratch_shapes=[pltpu.VMEM((PROJ, D), F32)],
        compiler_params=pltpu.CompilerParams(dimension_semantics=("arbitrary",), vmem_limit_bytes=VMEM_BIG),
    )(g)


def _proj_rows_to_in(g):
    pad = lambda a: jnp.pad(a, ((0, R_IN - a.shape[0]), (0, 0)))
    lo = N_IN * _GAP_DEV
    shards = [pad(g[N_IN * j:N_IN * (j + 1)]) for j in range(_GAP_DEV)]
    shards.append(pad(jnp.concatenate([g[lo:lo + _GAP_ROW], g[lo + _GAP_ROW + _GAP:lo + N_IN + _GAP]], axis=0)))
    shards += [pad(g[N_IN * j + _GAP:N_IN * (j + 1) + _GAP]) for j in range(_GAP_DEV + 1, 8)]
    return jnp.stack(shards)


def _local_grads(x, mem, tgt, win_t, gw_of, sm, on_grads):
    b_pad = jnp.pad(sm['b_fgt'], ((0, 0), (0, 120)))
    tbl = jnp.pad(sm['rel_bias'], ((0, 0), (0, NREL_PAD - 257)))

    h1, proj, flog = _premix_fwd(x, sm['g_mix_pre'], win_t)
    c = _gate_fwd(flog, b_pad)
    ct3 = c[:, :8].T.reshape(4, 2, T)
    o_f, lse_f = _fox_fwd(proj, c, ct3)
    vt3 = _relvec_fwd(tbl).reshape(4, 2, VW)
    kvp = jnp.pad(proj[:, CHK0 + 512:], ((LEFT, 0), (0, 0)))
    o_c, lse_c = _chk_fwd(proj, kvp, vt3 + gw_of('relay', [o_f]))
    gw = gw_of('done', [o_c])
    w_out, w_mq, w_mk, w_mv, w_mo, w1_t, w2 = (_wblk(gw, n) for n in ('w_out', 'w_mq', 'w_mk', 'w_mv', 'w_mo', 'w_ff1', 'w_ff2'))
    ycat, z, x1, h2, qm = _postmix_fwd(x, o_f, o_c, sm['g_fox_out'], sm['g_chk_out'], w_out,
                                       sm['g_mix_post'], sm['g_mem_pre'], w_mq)
    memn, km, vm = _memkv_fwd(mem, sm['g_mem_kv'], w_mk, w_mv)
    om, ym, x2, h3 = _mem_fwd(qm, x1, km, vm, w_mo, sm['g_mem_post'], sm['g_ff_pre'])
    a, y3, dx3, loss_acc = _ffn_fwd(h3, x2, tgt, w1_t, w2, sm['g_ff_post'])

    gs = {}
    dx2, da, dy3, r, gs['g_ff_post'], gs['g_ff_pre'] = _ffn_bwd(dx3, y3, x2, a, w1_t, w2, sm['g_ff_post'], sm['g_ff_pre'])
    zero = on_grads('A', _wgrad_group("wgrad_ff", [(da, h3), (r, dy3)], 512), None)
    dx1, dym, dqm, dkm, dvm, gs['g_mem_post'], gs['g_mem_pre'] = _mem_bwd(
        dx2, ym, x1, qm, km, vm, w_mo, w_mq, sm['g_mem_post'] + zero, sm['g_mem_pre'])
    zero = on_grads('A halfway', None, [dx1])
    gs['g_mem_kv'] = _memkv_bwd(dkm, dvm, mem, w_mk, w_mv)
    dz, dof, doc, gs['g_mix_post'], gs['g_fox_out'], gs['g_chk_out'] = _postmix_bwd(
        dx1, z, o_f, o_c, w_out, sm['g_mix_post'] + zero, sm['g_fox_out'], sm['g_chk_out'])
    zero = on_grads('B', _wgrad_group("wgrad_mem_out", [(ycat, dz), (h2, dqm), (memn, dkm), (memn, dvm), (om, dym)], 128), None)
    dq_f, dk_f, dv_f, dct, dcq = _fox_bwd(proj, c, ct3 + zero, o_f, lse_f, dof)
    zero = on_grads('B halfway', None, [dq_f])
    dq_c, dk_c, dv_c, gv = _chk_bwd(proj, kvp, vt3 + zero, o_c, lse_c, doc)
    gs['rel_bias'] = _relvec_bwd(gv.reshape(8, VW))[:, :257]
    dc = jnp.pad(dct.reshape(8, T).T + dcq[:, :, :2].transpose(1, 0, 2).reshape(T, 8), ((0, 0), (0, 120)))
    dflog, db = _gate_bwd(dc, flog, b_pad)
    gs['b_fgt'] = db[0:1, :8]
    grad_x, dproj, gs['g_mix_pre'] = _premix_bwd(dx1, x, [dq_f, dk_f, dv_f, dflog, dq_c, dk_c, dv_c], win_t, sm['g_mix_pre'])
    on_grads('C', _proj_rows_to_in(_wgrad(dproj, h1, "wgrad_in")), None)
    return loss_acc[0, 0], grad_x, gs


def kernel(x, mem, w_in, b_fgt, rel_bias, g_fox_out, g_chk_out, w_out, g_mix_pre, g_mix_post, g_mem_kv, w_mq, w_mk, w_mv, w_mo, g_mem_pre, g_mem_post, w_ff1, w_ff2, g_ff_pre, g_ff_post, loss_target, m_w_in, m_b_fgt, m_rel_bias, m_g_fox_out, m_g_chk_out, m_w_out, m_g_mix_pre, m_g_mix_post, m_g_mem_kv, m_w_mq, m_w_mk, m_w_mv, m_w_mo, m_g_mem_pre, m_g_mem_post, m_w_ff1, m_w_ff2, m_g_ff_pre, m_g_ff_post, v_w_in, v_b_fgt, v_rel_bias, v_g_fox_out, v_g_chk_out, v_w_out, v_g_mix_pre, v_g_mix_post, v_g_mem_kv, v_w_mq, v_w_mk, v_w_mv, v_w_mo, v_g_mem_pre, v_g_mem_post, v_w_ff1, v_w_ff2, v_g_ff_pre, v_g_ff_post):
    args = dict(locals())
    two_d = lambda a: a.reshape(a.shape[-2:])
    w = {n: two_d(args[n]) for n in WEIGHTS}
    m = {n: two_d(args['m_' + n]) for n in WEIGHTS}
    v = {n: two_d(args['v_' + n]) for n in WEIGHTS}

    sm = {n: w[n] for n in SMALL}
    shard_in = jnp.pad(w['w_in'].T, ((0, R_IN - N_IN), (0, 0))).astype(BF16)
    gathered_in, zero = _allgather(shard_in, "allgather_w_in")
    win_t = _in_rows_to_proj(gathered_in)
    shard_rest = (jnp.concatenate([w['w_ff1'].T, w['w_ff2'], w['w_out'], w['w_mq'], w['w_mk'], w['w_mv'], w['w_mo']],
                                  axis=0) + zero[0, 0]).astype(BF16)
    gather = {'first': _start_copies("allgather_rest_start", shard_rest, (8, R_REST, D), _gather_plan, 3)}
    sm['g_mix_pre'] = sm['g_mix_pre'] + gather['first'][4][0, 0]

    def gw_of(stage, after):
        if stage == 'relay':
            gather['block'], land = _wait_copies("allgather_rest_wait", gather['first'], after, _gather_plan)
            gather['second'] = _start_inplace("allgather_rest_relay_start", land, _relay_plan, 4)
            return gather['second'][3][0, 0]
        land = _wait_inplace("allgather_rest_relay_wait", gather['second'], after, _relay_plan)
        return _gather_forward(land, gather['block'])

    rs = {}

    def on_grads(stage, g, after):
        if stage.endswith('halfway'):
            rs[stage[0]].halfway(after)
            return rs[stage[0]].token
        rs[stage] = _ReduceScatter("rs_" + stage.lower(), g)
        return rs[stage].token

    loss_local, grad_x, gs = _local_grads(x[0], mem[0], loss_target[0], win_t, gw_of, sm, on_grads)
    grads, deltas, new_m, new_v = {}, {}, {}, {}

    def update(n, out):
        grads[n], deltas[n], new_m[n], new_v[n] = out

    packed = _pack_small(gs, loss_local + rs['C'].token)
    rs['C'].halfway([packed])
    gparts, _ = _allgather(packed + rs['C'].token, "allgather_small_grads")
    small = _adamw_small(gparts, [w[n] for n in SMALL], [m[n] for n in SMALL], [v[n] for n in SMALL])
    loss = small[0][0, 0]
    for t, n in enumerate(SMALL):
        update(n, small[1 + 4 * t:5 + 4 * t])

    own, got, order = rs['A'].finish([grad_x, rs['C'].started[4]])
    update('w_ff1', _sum_adam(own, got, order, 0, w['w_ff1'], m['w_ff1'], v['w_ff1'], "adamw_w_ff1", transposed=True))
    update('w_ff2', _sum_adam(own, got, order, 512, w['w_ff2'], m['w_ff2'], v['w_ff2'], "adamw_w_ff2"))
    own, got, order = rs['B'].finish([grad_x, rs['C'].started[4]])
    names_b = ('w_out', 'w_mq', 'w_mk', 'w_mv', 'w_mo')
    done = _sum_adam_rows(own, got, order, [w[n] for n in names_b], [m[n] for n in names_b], [v[n] for n in names_b],
                          "adamw_group_b")
    for k, n in enumerate(names_b):
        update(n, done[4 * k:4 * k + 4])

    own, got, order = rs['C'].finish([new_v[n] for n in BIG if n != 'w_in'])
    done = _sum_adam(own, got, order, 0, w['w_in'].T, m['w_in'].T, v['w_in'].T, "adamw_w_in")
    update('w_in', [a.T for a in done])

    out = [loss, grad_x[None]]
    for group in (grads, deltas, new_m, new_v):
        out += [group[n].reshape(args[n].shape) for n in WEIGHTS]
    return tuple(out)
```

```python
import jax
import jax.numpy as jnp
from jax import lax
from jax.experimental import pallas as pl
from jax.experimental.pallas import tpu as pltpu

F32 = jnp.float32
BF16 = jnp.bfloat16
MESH = pl.DeviceIdType.MESH

T = 2048
D = 1024
NMEM = 256
DFF = 4096
EPS = 1e-6
TM = 256
TM_WIDE = 512
TQ = 256
FQ = 512
HD = 64
SCALE = HD ** -0.5
MEM_HEADS = 4
MEM_HD = 256
MEM_SCALE = MEM_HD ** -0.5
NEG = -1e30
LEFT = 512
WIN = LEFT + TQ
VW = 1024
NREL_PAD = 384
PROJ = 3200
GATE0 = 1536
CHK0 = 1664
VMEM_BIG = 56 * 1024 * 1024

ADAM_LR = 0.001
ADAM_B1 = 0.9
ADAM_B2 = 0.999
ADAM_EPS = 1e-08
ADAM_WD = 0.01
ADAM_STEP = 10

N_IN = 385
R_IN = 400
R_REST = 1664
W_ROWS = {'w_ff1': (0, 512), 'w_ff2': (512, 512),
          'w_out': (1024, 128), 'w_mq': (1152, 128), 'w_mk': (1280, 128), 'w_mv': (1408, 128), 'w_mo': (1536, 128)}
SMALL_ROWS = 24
SMALL_SLOT = {'rel_bias': (0, 8, 0, 257), 'b_fgt': (8, 1, 0, 8), 'g_fox_out': (9, 1, 0, 512), 'g_chk_out': (9, 1, 512, 512),
              'g_mix_pre': (10, 1, 0, 1024), 'g_mix_post': (11, 1, 0, 1024), 'g_mem_kv': (12, 1, 0, 1024),
              'g_mem_pre': (13, 1, 0, 1024), 'g_mem_post': (14, 1, 0, 1024), 'g_ff_pre': (15, 1, 0, 1024),
              'g_ff_post': (16, 1, 0, 1024)}

WEIGHTS = ['w_in', 'b_fgt', 'rel_bias', 'g_fox_out', 'g_chk_out', 'w_out', 'g_mix_pre', 'g_mix_post', 'g_mem_kv',
           'w_mq', 'w_mk', 'w_mv', 'w_mo', 'g_mem_pre', 'g_mem_post', 'w_ff1', 'w_ff2', 'g_ff_pre', 'g_ff_post']
BIG = ['w_in', 'w_out', 'w_mq', 'w_mk', 'w_mv', 'w_mo', 'w_ff1', 'w_ff2']
SMALL = [n for n in WEIGHTS if n not in BIG]


def _pcall(body, **kw):
    return pl.pallas_call(body, **kw)


def _nn(a, b):
    return jnp.dot(a, b, preferred_element_type=F32)


def _nt(a, b):
    return lax.dot_general(a, b, (((1,), (1,)), ((), ())), preferred_element_type=F32)


def _tn(a, b):
    return lax.dot_general(a, b, (((0,), (0,)), ((), ())), preferred_element_type=F32)


def _w(ref):
    v = ref[...]
    return v if v.ndim == 2 else v.reshape(-1, v.shape[-1])


def _rstd(x):
    return lax.rsqrt(jnp.mean(x * x, axis=-1, keepdims=True) + EPS)


def _rms(x, g):
    return x * _rstd(x) * g


def _rms_bwd(x, g, dy):
    r = _rstd(x)
    xh = x * r
    dg = jnp.sum(dy * xh, axis=0, keepdims=True)
    dxh = dy * g
    dx = r * (dxh - xh * jnp.mean(dxh * xh, axis=-1, keepdims=True))
    return dx, dg


def _resident(a):
    if isinstance(a, tuple):
        _, shape, index = a
        return pl.BlockSpec(shape, lambda *_: index, pipeline_mode=pl.Buffered(1))
    return pl.BlockSpec(a.shape, lambda *_, nd=a.ndim: (0,) * nd, pipeline_mode=pl.Buffered(1))


def _wblk(gw, name):
    r0, rows = W_ROWS[name]
    return (gw, (8, rows, D), (0, r0 // rows, 0))


def _tok_call(body, name, tiled, full, outs_tiled, outs_acc=(), rows=T, tm=TM, vmem=None):
    in_specs = [pl.BlockSpec((tm, a.shape[1]), lambda i: (i, 0)) for a in tiled]
    in_specs += [_resident(a) for a in full]
    full = [a[0] if isinstance(a, tuple) else a for a in full]
    out_shape = [jax.ShapeDtypeStruct((rows, c), dt) for c, dt in outs_tiled]
    out_shape += [jax.ShapeDtypeStruct(s, F32) for s in outs_acc]
    out_specs = [pl.BlockSpec((tm, c), lambda i: (i, 0)) for c, _ in outs_tiled]
    out_specs += [pl.BlockSpec(s, lambda i, nd=len(s): (0,) * nd) for s in outs_acc]
    return _pcall(
        body, name=name, grid=(rows // tm,), in_specs=in_specs, out_specs=out_specs, out_shape=out_shape,
        compiler_params=pltpu.CompilerParams(dimension_semantics=("arbitrary",), vmem_limit_bytes=vmem),
    )(*tiled, *full)


def _one_call(body, name, ins, outs, vmem=None):
    whole = lambda s: pl.BlockSpec(s, lambda i, nd=len(s): (0,) * nd)
    return _pcall(
        body, name=name, grid=(1,), in_specs=[_resident(a) for a in ins], out_specs=[whole(s) for s, _ in outs],
        out_shape=[jax.ShapeDtypeStruct(s, dt) for s, dt in outs],
        compiler_params=pltpu.CompilerParams(dimension_semantics=("arbitrary",), vmem_limit_bytes=vmem),
    )(*[a[0] if isinstance(a, tuple) else a for a in ins])


def _premix_fwd(x, g_pre, win_t):
    def body(x_ref, g_ref, w_ref, h_ref, proj_ref, flog_ref):
        h = _rms(x_ref[...], g_ref[...]).astype(BF16)
        h_ref[...] = h
        p = _nt(h, w_ref[...])
        proj_ref[...] = p.astype(BF16)
        flog_ref[...] = p[:, GATE0:GATE0 + 128]

    return _tok_call(body, "premix_fwd", [x], [g_pre, win_t],
                     [(D, BF16), (PROJ, BF16), (128, F32)], tm=TM_WIDE, vmem=VMEM_BIG)


def _postmix_fwd(x, o_f, o_c, g_fo, g_co, w_out, g_post, g_mpre, w_mq):
    def body(x_ref, of_ref, oc_ref, gfo_ref, gco_ref, wo_ref, gp_ref, gm_ref, wq_ref,
             y_ref, z_ref, x1_ref, h2_ref, qm_ref):
        y_ref[:, :512] = _rms(of_ref[...], gfo_ref[...]).astype(BF16)
        y_ref[:, 512:] = _rms(oc_ref[...], gco_ref[...]).astype(BF16)
        z = _nn(y_ref[...], _w(wo_ref))
        z_ref[...] = z
        x1 = x_ref[...] + _rms(z, gp_ref[...])
        x1_ref[...] = x1
        h2 = _rms(x1, gm_ref[...]).astype(BF16)
        h2_ref[...] = h2
        qm_ref[...] = _nn(h2, _w(wq_ref)).astype(BF16)

    return _tok_call(body, "postmix_fwd", [x, o_f, o_c], [g_fo, g_co, w_out, g_post, g_mpre, w_mq],
                     [(D, BF16), (D, F32), (D, F32), (D, BF16), (D, BF16)], tm=TM_WIDE, vmem=VMEM_BIG)


def _memkv_fwd(mem, g_kv, w_mk, w_mv):
    def body(m_ref, g_ref, wk_ref, wv_ref, mn_ref, k_ref, v_ref):
        mn = _rms(m_ref[...], g_ref[...]).astype(BF16)
        mn_ref[...] = mn
        k_ref[...] = _nn(mn, _w(wk_ref)).astype(BF16)
        v_ref[...] = _nn(mn, _w(wv_ref)).astype(BF16)

    return _tok_call(body, "memkv_fwd", [mem], [g_kv, w_mk, w_mv],
                     [(D, BF16), (D, BF16), (D, BF16)], rows=NMEM, tm=NMEM, vmem=VMEM_BIG)


def _mem_fwd(qm, x1, km, vm, w_mo, g_post, g_fpre):
    def body(q_ref, x1_ref, k_ref, v_ref, wo_ref, gp_ref, gf_ref, om_ref, ym_ref, x2_ref, h3_ref):
        for h in range(MEM_HEADS):
            sl = slice(h * MEM_HD, (h + 1) * MEM_HD)
            s = _nt(q_ref[:, sl], k_ref[:, sl]) * MEM_SCALE
            p = jnp.exp(s - jnp.max(s, axis=-1, keepdims=True))
            p = p / jnp.sum(p, axis=-1, keepdims=True)
            om_ref[:, sl] = _nn(p.astype(BF16), v_ref[:, sl]).astype(BF16)
        ym = _nn(om_ref[...], _w(wo_ref))
        ym_ref[...] = ym
        x2 = x1_ref[...] + _rms(ym, gp_ref[...])
        x2_ref[...] = x2
        h3_ref[...] = _rms(x2, gf_ref[...]).astype(BF16)

    return _tok_call(body, "mem_fwd", [qm, x1], [km, vm, w_mo, g_post, g_fpre],
                     [(D, BF16), (D, F32), (D, F32), (D, BF16)], tm=TM_WIDE, vmem=VMEM_BIG)


def _tri(lower):
    r = lax.broadcasted_iota(jnp.int32, (128, 128), 0)
    c = lax.broadcasted_iota(jnp.int32, (128, 128), 1)
    return jnp.where(r >= c if lower else c >= r, 1.0, 0.0).astype(F32)


def _hdot(a, b):
    return jnp.dot(a, b, preferred_element_type=F32, precision=lax.Precision.HIGHEST)


def _gate_fwd(flog, b_pad):
    def body(f_ref, b_ref, c_ref):
        tri = _tri(True)

        def step(i, carry):
            rows = pl.ds(pl.multiple_of(i * 128, 128), 128)
            z = f_ref[rows, :] + b_ref[...]
            lf = jnp.minimum(z, 0.0) - jnp.log(1.0 + jnp.exp(-jnp.abs(z)))
            cb = _hdot(tri, lf) + carry
            c_ref[rows, :] = cb
            return cb[127:128, :]

        lax.fori_loop(0, T // 128, step, jnp.zeros((1, 128), F32))

    return _one_call(body, "gate_fwd", [flog, b_pad], [((T, 128), F32)])[0]


def _gate_bwd(dc, flog, b_pad):
    def body(dc_ref, f_ref, b_ref, df_ref, db_ref):
        tri = _tri(False)

        def step(j, carry):
            run, db = carry
            i = T // 128 - 1 - j
            rows = pl.ds(pl.multiple_of(i * 128, 128), 128)
            dcb = dc_ref[rows, :]
            rb = _hdot(tri, dcb) + run
            z = f_ref[rows, :] + b_ref[...]
            df = rb * (1.0 / (1.0 + jnp.exp(z)))
            df_ref[rows, :] = df.astype(BF16)
            return run + jnp.sum(dcb, axis=0, keepdims=True), db + jnp.sum(df, axis=0, keepdims=True)

        _, db = lax.fori_loop(0, T // 128, step, (jnp.zeros((1, 128), F32), jnp.zeros((1, 128), F32)))
        db_ref[...] = jnp.broadcast_to(db, (8, 128))

    return _one_call(body, "gate_bwd", [dc, flog, b_pad], [((T, 128), BF16), ((8, 128), F32)])


def _lane_lo(rows=TQ):
    return lax.broadcasted_iota(jnp.int32, (rows, 128), 1) < HD


def _half(v, lo, a, scale=None):
    keep = lo if a == 0 else jnp.logical_not(lo)
    v = v.astype(F32) if scale is None else v.astype(F32) * scale
    return jnp.where(keep, v, 0.0).astype(BF16)


def _fox_specs():
    return [pl.BlockSpec((FQ, 128), lambda h, i: (i, h)),
            pl.BlockSpec((T, 128), lambda h, i: (0, 4 + h)),
            pl.BlockSpec((T, 128), lambda h, i: (0, 8 + h))]


def _lane_pick(x, at):
    lane = lax.broadcasted_iota(jnp.int32, x.shape, 1)
    return jnp.sum(jnp.where(lane == at, x, 0.0), axis=-1, keepdims=True)


def _fox_fwd(proj, c, ct3):
    def body(q_ref, k_ref, v_ref, c_ref, ct_ref, o_ref, l_ref):
        i = pl.program_id(1)
        lo = _lane_lo(FQ)
        causal = lax.broadcasted_iota(jnp.int32, (FQ, FQ), 1) <= lax.broadcasted_iota(jnp.int32, (FQ, FQ), 0)
        q = q_ref[...]
        qs = [_half(q, lo, a, SCALE) for a in range(2)]
        cqs = [_lane_pick(c_ref[...], 2 * pl.program_id(0) + a) for a in range(2)]

        def tile(off, carry, diagonal):
            kblk = k_ref[pl.ds(off, FQ), :]
            vblk = v_ref[pl.ds(off, FQ), :]
            new = []
            for a in range(2):
                m, l, acc = carry[a]
                s = _nt(qs[a], kblk) + (cqs[a] - ct_ref[a:a + 1, pl.ds(off, FQ)])
                if diagonal:
                    s = jnp.where(causal, s, NEG)
                m2 = jnp.maximum(m, jnp.max(s, axis=-1, keepdims=True))
                p = jnp.exp(s - m2)
                alpha = jnp.exp(m - m2)
                new.append((m2, alpha * l + jnp.sum(p, axis=-1, keepdims=True),
                            alpha * acc + _nn(p.astype(BF16), vblk)))
            return tuple(new)

        init = (jnp.full((FQ, 1), NEG, F32), jnp.zeros((FQ, 1), F32), jnp.zeros((FQ, 128), F32))
        carry = lax.fori_loop(0, i, lambda kb, c: tile(pl.multiple_of(kb * FQ, FQ), c, False), (init, init))
        carry = tile(pl.multiple_of(i * FQ, FQ), carry, True)
        outs = []
        for a in range(2):
            m, l, acc = carry[a]
            outs.append(acc / l)
            l_ref[:, 128 * a:128 * a + 128] = jnp.broadcast_to(m + jnp.log(l), (FQ, 128))
        o_ref[...] = jnp.where(lo, outs[0], outs[1])

    return _pcall(
        body, name="fox_fwd", grid=(4, T // FQ),
        in_specs=_fox_specs() + [pl.BlockSpec((FQ, 128), lambda h, i: (i, 0)),
                                 pl.BlockSpec((None, 2, T), lambda h, i: (h, 0, 0))],
        out_specs=[pl.BlockSpec((FQ, 128), lambda h, i: (i, h)), pl.BlockSpec((FQ, 256), lambda h, i: (i, h))],
        out_shape=[jax.ShapeDtypeStruct((T, 512), F32), jax.ShapeDtypeStruct((T, 1024), F32)],
        compiler_params=pltpu.CompilerParams(dimension_semantics=("arbitrary", "arbitrary"), vmem_limit_bytes=VMEM_BIG),
    )(proj, proj, proj, c, ct3)


def _fox_bwd(proj, c, ct3, o, lse, do):
    def body(q_ref, k_ref, v_ref, c_ref, ct_ref, o_ref, l_ref, do_ref, dq_ref, dkb_ref, dvb_ref, dct_ref, dcq_ref,
             dk_ref, dv_ref):
        i = pl.program_id(1)

        @pl.when(i == 0)
        def _():
            dk_ref[...] = jnp.zeros_like(dk_ref)
            dv_ref[...] = jnp.zeros_like(dv_ref)
            dct_ref[...] = jnp.zeros_like(dct_ref)

        lo = _lane_lo(FQ)
        causal = lax.broadcasted_iota(jnp.int32, (FQ, FQ), 1) <= lax.broadcasted_iota(jnp.int32, (FQ, FQ), 0)
        q = q_ref[...]
        do_v = do_ref[...]
        prod = do_v * o_ref[...]
        qs = [_half(q, lo, a, SCALE) for a in range(2)]
        dos = [_half(do_v, lo, a) for a in range(2)]
        deltas = [jnp.sum(jnp.where(lo if a == 0 else jnp.logical_not(lo), prod, 0.0), axis=-1, keepdims=True)
                  for a in range(2)]
        cqs = [_lane_pick(c_ref[...], 2 * pl.program_id(0) + a) for a in range(2)]
        las = [l_ref[:, 128 * a:128 * a + 1] for a in range(2)]

        def tile(off, carry, diagonal):
            kblk = k_ref[pl.ds(off, FQ), :]
            vblk = v_ref[pl.ds(off, FQ), :]
            new = []
            dk = jnp.zeros((128, FQ), F32)
            dv = jnp.zeros((128, FQ), F32)
            for a in range(2):
                dq_acc, rs = carry[a]
                s = _nt(qs[a], kblk) + (cqs[a] - ct_ref[a:a + 1, pl.ds(off, FQ)])
                if diagonal:
                    s = jnp.where(causal, s, NEG)
                p = jnp.exp(s - las[a])
                ds = p * (_nt(dos[a], vblk) - deltas[a])
                dsb = ds.astype(BF16)
                dk = dk + _tn(qs[a], dsb)
                dv = dv + _tn(dos[a], p.astype(BF16))
                dct_ref[a:a + 1, pl.ds(off, FQ)] -= jnp.sum(ds, axis=0, keepdims=True)
                new.append((dq_acc + _nn(dsb, kblk), rs + jnp.sum(ds, axis=-1, keepdims=True)))
            dk_ref[:, pl.ds(off, FQ)] += dk
            dv_ref[:, pl.ds(off, FQ)] += dv
            return tuple(new)

        init = (jnp.zeros((FQ, 128), F32), jnp.zeros((FQ, 1), F32))
        carry = lax.fori_loop(0, i, lambda kb, c: tile(pl.multiple_of(kb * FQ, FQ), c, False), (init, init))
        carry = tile(pl.multiple_of(i * FQ, FQ), carry, True)
        lane = lax.broadcasted_iota(jnp.int32, (FQ, 128), 1)
        dcq_ref[...] = jnp.where(lane == 0, carry[0][1], jnp.where(lane == 1, carry[1][1], 0.0))
        dq_ref[...] = (jnp.where(lo, carry[0][0], carry[1][0]) * SCALE).astype(BF16)

        @pl.when(i == T // FQ - 1)
        def _():
            dkb_ref[...] = dk_ref[...].T.astype(BF16)
            dvb_ref[...] = dv_ref[...].T.astype(BF16)

    blk = pl.BlockSpec((FQ, 128), lambda h, i: (i, h))
    wide = pl.BlockSpec((FQ, 256), lambda h, i: (i, h))
    rows = pl.BlockSpec((None, 2, T), lambda h, i: (h, 0, 0))
    col = pl.BlockSpec((T, 128), lambda h, i: (0, h))
    return _pcall(
        body, name="fox_bwd", grid=(4, T // FQ),
        in_specs=_fox_specs() + [pl.BlockSpec((FQ, 128), lambda h, i: (i, 0)), rows, blk, wide, blk],
        out_specs=[blk, col, col, rows, pl.BlockSpec((None, FQ, 128), lambda h, i: (h, i, 0))],
        out_shape=[jax.ShapeDtypeStruct((T, 512), BF16), jax.ShapeDtypeStruct((T, 512), BF16),
                   jax.ShapeDtypeStruct((T, 512), BF16), jax.ShapeDtypeStruct((4, 2, T), F32),
                   jax.ShapeDtypeStruct((4, T, 128), F32)],
        scratch_shapes=[pltpu.VMEM((128, T), F32), pltpu.VMEM((128, T), F32)],
        compiler_params=pltpu.CompilerParams(dimension_semantics=("arbitrary", "arbitrary"), vmem_limit_bytes=VMEM_BIG),
    )(proj, proj, proj, c, ct3, o, lse, do)


def _rel_onehot():
    ridx = lax.broadcasted_iota(jnp.int32, (NREL_PAD, VW), 0)
    j = lax.broadcasted_iota(jnp.int32, (NREL_PAD, VW), 1)
    return jnp.where(ridx == jnp.clip(TQ + LEFT - 1 - j, -128, 128) + 128, 1.0, 0.0).astype(F32)


def _relvec_fwd(tbl):
    def body(t_ref, v_ref):
        v_ref[...] = _hdot(t_ref[...], _rel_onehot())

    return _one_call(body, "relvec_fwd", [tbl], [((8, VW), F32)])[0]


def _relvec_bwd(gv):
    def body(g_ref, t_ref):
        t_ref[...] = lax.dot_general(g_ref[...], _rel_onehot(), (((1,), (1,)), ((), ())),
                                     preferred_element_type=F32, precision=lax.Precision.HIGHEST)

    return _one_call(body, "relvec_bwd", [gv], [((8, NREL_PAD), F32)])[0]


def _chk_bias(vt_ref, a, hidden):
    vb = jnp.broadcast_to(vt_ref[a:a + 1, :], (TQ, VW))
    y = pltpu.roll(vb, VW - (TQ - 1), 1, stride=1, stride_axis=0)[:, :WIN]
    cr = lax.broadcasted_iota(jnp.int32, (TQ, WIN), 0) // 64
    m = lax.broadcasted_iota(jnp.int32, (TQ, WIN), 1)
    return jnp.where((m // 64 >= cr) & (m // 64 <= cr + 8) & (m >= hidden), y, NEG)


def _chk_specs():
    return [pl.BlockSpec((TQ, 128), lambda h, i: (i, CHK0 // 128 + h)),
            pl.BlockSpec((T + LEFT, 128), lambda h, i: (0, h)),
            pl.BlockSpec((T + LEFT, 128), lambda h, i: (0, 4 + h)),
            pl.BlockSpec((None, 2, VW), lambda h, i: (h, 0, 0))]


def _chk_fwd(proj, kvp, vt3):
    def body(q_ref, k_ref, v_ref, vt_ref, o_ref, l_ref, bias_ref):
        i = pl.program_id(1)

        @pl.when(i == 0)
        def _():
            for first in range(3):
                for a in range(2):
                    bias_ref[first, a] = _chk_bias(vt_ref, a, max(LEFT - first * TQ, 0))

        lo = _lane_lo()
        off = pl.multiple_of(i * TQ, TQ)
        kw = k_ref[pl.ds(off, WIN), :]
        vw = v_ref[pl.ds(off, WIN), :]
        bias_at = jnp.minimum(i, 2)
        q = q_ref[...]
        outs = []
        for a in range(2):
            s = _nt(_half(q, lo, a, SCALE), kw) + bias_ref[bias_at, a]
            m = jnp.max(s, axis=-1, keepdims=True)
            p = jnp.exp(s - m)
            l = jnp.sum(p, axis=-1, keepdims=True)
            outs.append(_nn(p.astype(BF16), vw) / l)
            l_ref[:, 128 * a:128 * a + 128] = jnp.broadcast_to(m + jnp.log(l), (TQ, 128))
        o_ref[...] = jnp.where(lo, outs[0], outs[1])

    return _pcall(
        body, name="chk_fwd", grid=(4, T // TQ), in_specs=_chk_specs(),
        out_specs=[pl.BlockSpec((TQ, 128), lambda h, i: (i, h)), pl.BlockSpec((TQ, 256), lambda h, i: (i, h))],
        out_shape=[jax.ShapeDtypeStruct((T, 512), F32), jax.ShapeDtypeStruct((T, 1024), F32)],
        scratch_shapes=[pltpu.VMEM((3, 2, TQ, WIN), F32)],
        compiler_params=pltpu.CompilerParams(dimension_semantics=("arbitrary", "arbitrary")),
    )(proj, kvp, kvp, vt3)


def _chk_bwd(proj, kvp, vt3, o, lse, do):
    nq = T // TQ

    def body(q_ref, k_ref, v_ref, vt_ref, o_ref, l_ref, do_ref, dq_ref, dkb_ref, dvb_ref, gv_ref, bias_ref, dsum_ref,
             dk_ref, dv_ref):
        i = pl.program_id(1)

        @pl.when(i == 0)
        def _():
            for first in range(3):
                for a in range(2):
                    bias_ref[first, a] = _chk_bias(vt_ref, a, max(LEFT - first * TQ, 0))
            dsum_ref[...] = jnp.zeros_like(dsum_ref)
            dk_ref[...] = jnp.zeros_like(dk_ref)
            dv_ref[...] = jnp.zeros_like(dv_ref)

        lo = _lane_lo()
        off = pl.multiple_of(i * TQ, TQ)
        kw = k_ref[pl.ds(off, WIN), :]
        vw = v_ref[pl.ds(off, WIN), :]
        bias_at = jnp.minimum(i, 2)
        q = q_ref[...]
        do_v = do_ref[...]
        prod = do_v * o_ref[...]
        dqs = []
        for a in range(2):
            keep = lo if a == 0 else jnp.logical_not(lo)
            qa = _half(q, lo, a, SCALE)
            doa = _half(do_v, lo, a)
            delta = jnp.sum(jnp.where(keep, prod, 0.0), axis=-1, keepdims=True)
            s = _nt(qa, kw) + bias_ref[bias_at, a]
            p = jnp.exp(s - l_ref[:, 128 * a:128 * a + 1])
            ds = p * (_nt(doa, vw) - delta)
            dsum_ref[a] += ds
            dsb = ds.astype(BF16)
            dk_ref[:, pl.ds(off, WIN)] += _tn(qa, dsb)
            dv_ref[:, pl.ds(off, WIN)] += _tn(doa, p.astype(BF16))
            dqs.append(_nn(dsb, kw))
        dq_ref[...] = (jnp.where(lo, dqs[0], dqs[1]) * SCALE).astype(BF16)

        @pl.when(i == nq - 1)
        def _():
            dkb_ref[...] = dk_ref[:, LEFT:].T.astype(BF16)
            dvb_ref[...] = dv_ref[:, LEFT:].T.astype(BF16)
            rr = lax.broadcasted_iota(jnp.int32, (TQ, TQ), 0)
            cc = lax.broadcasted_iota(jnp.int32, (TQ, TQ), 1)
            flip = jnp.where(rr + cc == TQ - 1, 1.0, 0.0).astype(F32)
            for a in range(2):
                dpad = jnp.concatenate([dsum_ref[a], jnp.zeros((TQ, VW - WIN), F32)], axis=1)
                z = pltpu.roll(_hdot(flip, dpad), 0, 1, stride=1, stride_axis=0)
                gv_ref[a:a + 1, :] = jnp.sum(z, axis=0, keepdims=True)

    blk = pl.BlockSpec((TQ, 128), lambda h, i: (i, h))
    wide = pl.BlockSpec((TQ, 256), lambda h, i: (i, h))
    col = pl.BlockSpec((T, 128), lambda h, i: (0, h))
    return _pcall(
        body, name="chk_bwd", grid=(4, nq), in_specs=_chk_specs() + [blk, wide, blk],
        out_specs=[blk, col, col, pl.BlockSpec((None, 2, VW), lambda h, i: (h, 0, 0))],
        out_shape=[jax.ShapeDtypeStruct((T, 512), BF16), jax.ShapeDtypeStruct((T, 512), BF16),
                   jax.ShapeDtypeStruct((T, 512), BF16), jax.ShapeDtypeStruct((4, 2, VW), F32)],
        scratch_shapes=[pltpu.VMEM((3, 2, TQ, WIN), F32), pltpu.VMEM((2, TQ, WIN), F32),
                        pltpu.VMEM((128, T + LEFT), F32), pltpu.VMEM((128, T + LEFT), F32)],
        compiler_params=pltpu.CompilerParams(dimension_semantics=("arbitrary", "arbitrary")),
    )(proj, kvp, kvp, vt3, o, lse, do)


def _zero_at_start(*refs):
    @pl.when(pl.program_id(0) == 0)
    def _():
        for r in refs:
            r[...] = jnp.zeros_like(r)


def _ffn_step(h3, x2, tgt, w1_t, w2, g_post, g_pre):
    def body(h_ref, x2_ref, t_ref, w1_ref, w2_ref, gp_ref, gf_ref, dx2_ref, da_ref, dy_ref, r_ref, loss_ref, dgp_ref, dgf_ref):
        _zero_at_start(loss_ref, dgp_ref, dgf_ref)
        w1, w2v = _w(w1_ref), _w(w2_ref)
        ra = jnp.maximum(_nt(h_ref[...], w1), 0.0)
        r = jnp.square(ra).astype(BF16)
        r_ref[...] = r
        y = _nn(r, w2v)
        x2v = x2_ref[...]
        e = x2v + _rms(y, gp_ref[...]) - t_ref[...]
        loss_ref[...] += 0.5 * jnp.sum(jnp.sum(e * e, axis=-1, keepdims=True) * (1.0 / D))
        dx3 = e * (1.0 / D)
        dy, dgp = _rms_bwd(y, gp_ref[...], dx3)
        dgp_ref[...] += dgp
        dyb = dy.astype(BF16)
        dy_ref[...] = dyb
        da = (_nt(dyb, w2v) * (2.0 * ra)).astype(BF16)
        da_ref[...] = da
        dh, dgf = _rms_bwd(x2v, gf_ref[...], _nn(da, w1))
        dgf_ref[...] += dgf
        dx2_ref[...] = dx3 + dh

    return _tok_call(body, "ffn_step", [h3, x2, tgt], [w1_t, w2, g_post, g_pre],
                     [(D, F32), (DFF, BF16), (D, BF16), (DFF, BF16)], [(8, 128), (1, D), (1, D)], vmem=VMEM_BIG)


def _mem_bwd(dx2, ym, x1, qm, km, vm, w_mo, w_mq, g_post, g_pre):
    def body(dx2_ref, ym_ref, x1_ref, q_ref, k_ref, v_ref, wo_ref, wq_ref, gp_ref, gm_ref,
             dx1_ref, dym_ref, dq_ref, dk_ref, dv_ref, dgp_ref, dgm_ref, dom_ref):
        _zero_at_start(dk_ref, dv_ref, dgp_ref, dgm_ref)
        dx2_v = dx2_ref[...]
        dym, dgp = _rms_bwd(ym_ref[...], gp_ref[...], dx2_v)
        dgp_ref[...] += dgp
        dymb = dym.astype(BF16)
        dym_ref[...] = dymb
        dom_ref[...] = _nt(dymb, _w(wo_ref)).astype(BF16)
        for h in range(MEM_HEADS):
            sl = slice(h * MEM_HD, (h + 1) * MEM_HD)
            qh, kh, doh = q_ref[:, sl], k_ref[:, sl], dom_ref[:, sl]
            s = _nt(qh, kh) * MEM_SCALE
            p = jnp.exp(s - jnp.max(s, axis=-1, keepdims=True))
            p = p / jnp.sum(p, axis=-1, keepdims=True)
            dp = _nt(doh, v_ref[:, sl])
            ds = (p * (dp - jnp.sum(p * dp, axis=-1, keepdims=True))).astype(BF16)
            dq_ref[:, sl] = (_nn(ds, kh) * MEM_SCALE).astype(BF16)
            dk_ref[:, sl] += _tn(ds, qh) * MEM_SCALE
            dv_ref[:, sl] += _tn(p.astype(BF16), doh)
        dh, dgm = _rms_bwd(x1_ref[...], gm_ref[...], _nt(dq_ref[...], _w(wq_ref)))
        dgm_ref[...] += dgm
        dx1_ref[...] = dx2_v + dh

    tiled = pl.BlockSpec((TM_WIDE, D), lambda i: (i, 0))
    in_specs = [tiled] * 4 + [_resident(a) for a in (km, vm, w_mo, w_mq, g_post, g_pre)]
    w_mo, w_mq = w_mo[0], w_mq[0]
    kv = pl.BlockSpec((NMEM, D), lambda i: (0, 0))
    vec = pl.BlockSpec((1, D), lambda i: (0, 0))
    return _pcall(
        body, name="mem_bwd", grid=(T // TM_WIDE,), in_specs=in_specs,
        out_specs=[tiled, tiled, tiled, kv, kv, vec, vec],
        out_shape=[jax.ShapeDtypeStruct((T, D), F32), jax.ShapeDtypeStruct((T, D), BF16),
                   jax.ShapeDtypeStruct((T, D), BF16), jax.ShapeDtypeStruct((NMEM, D), F32),
                   jax.ShapeDtypeStruct((NMEM, D), F32), jax.ShapeDtypeStruct((1, D), F32),
                   jax.ShapeDtypeStruct((1, D), F32)],
        scratch_shapes=[pltpu.VMEM((TM_WIDE, D), BF16)],
        compiler_params=pltpu.CompilerParams(dimension_semantics=("arbitrary",), vmem_limit_bytes=VMEM_BIG),
    )(dx2, ym, x1, qm, km, vm, w_mo, w_mq, g_post, g_pre)


def _memkv_bwd(dkm, dvm, mem, w_mk, w_mv):
    def body(dk_ref, dv_ref, m_ref, wk_ref, wv_ref, dg_ref):
        dmn = _nt(dk_ref[...].astype(BF16), _w(wk_ref)) + _nt(dv_ref[...].astype(BF16), _w(wv_ref))
        mv = m_ref[...]
        dg_ref[...] = jnp.sum(dmn * (mv * _rstd(mv)), axis=0, keepdims=True)

    return _one_call(body, "memkv_bwd", [dkm, dvm, mem, w_mk, w_mv], [((1, D), F32)], vmem=VMEM_BIG)[0]


def _postmix_bwd(dx1, z, o_f, o_c, w_out, g_post, g_fo, g_co):
    def body(dx1_ref, z_ref, of_ref, oc_ref, wo_ref, gp_ref, gfo_ref, gco_ref,
             dz_ref, dof_ref, doc_ref, dgp_ref, dgfo_ref, dgco_ref):
        _zero_at_start(dgp_ref, dgfo_ref, dgco_ref)
        dz, dgp = _rms_bwd(z_ref[...], gp_ref[...], dx1_ref[...])
        dgp_ref[...] += dgp
        dzb = dz.astype(BF16)
        dz_ref[...] = dzb
        dy = _nt(dzb, _w(wo_ref))
        dof, dgfo = _rms_bwd(of_ref[...], gfo_ref[...], dy[:, :512])
        doc, dgco = _rms_bwd(oc_ref[...], gco_ref[...], dy[:, 512:])
        dof_ref[...] = dof
        doc_ref[...] = doc
        dgfo_ref[...] += dgfo
        dgco_ref[...] += dgco

    return _tok_call(body, "postmix_bwd", [dx1, z, o_f, o_c], [w_out, g_post, g_fo, g_co],
                     [(D, BF16), (512, F32), (512, F32)], [(1, D), (1, 512), (1, 512)], tm=TM_WIDE, vmem=VMEM_BIG)


def _premix_bwd(dx1, x, pieces, win_t, g_pre):
    def body(dx1_ref, x_ref, *refs):
        piece_refs, (w_ref, g_ref, dx_ref, dp_ref, dg_ref) = refs[:len(pieces)], refs[len(pieces):]
        _zero_at_start(dg_ref)
        col = 0
        for p in piece_refs:
            dp_ref[:, col:col + p.shape[1]] = p[...]
            col += p.shape[1]
        dh, dg = _rms_bwd(x_ref[...], g_ref[...], _nn(dp_ref[...], w_ref[...]))
        dg_ref[...] += dg
        dx_ref[...] = dx1_ref[...] + dh

    return _tok_call(body, "premix_bwd", [dx1, x] + list(pieces), [win_t, g_pre], [(D, F32), (PROJ, BF16)], [(1, D)],
                     tm=TM_WIDE, vmem=VMEM_BIG)


def _wgrad(a, b, name):
    k, m = a.shape
    n = b.shape[1]
    tm = 640 if m % 640 == 0 and m > 1024 else min(m, 512)
    tn = min(n, 1024)

    def body(a_ref, b_ref, o_ref):
        o_ref[...] = _tn(a_ref[...].astype(BF16), b_ref[...].astype(BF16))

    return _pcall(
        body, name=name, grid=(m // tm, n // tn),
        in_specs=[pl.BlockSpec((k, tm), lambda i, j: (0, i)), pl.BlockSpec((k, tn), lambda i, j: (0, j))],
        out_specs=pl.BlockSpec((tm, tn), lambda i, j: (i, j)),
        out_shape=jax.ShapeDtypeStruct((m, n), F32),
        compiler_params=pltpu.CompilerParams(dimension_semantics=("arbitrary", "arbitrary"), vmem_limit_bytes=VMEM_BIG),
    )(a, b)


def _wgrad_group(name, pairs, rows):
    def body(*refs):
        o_ref = refs[-1]
        for k in range(len(pairs)):
            o_ref[k * rows:(k + 1) * rows, :] = _tn(refs[2 * k][...].astype(BF16), refs[2 * k + 1][...].astype(BF16))

    in_specs, ops = [], []
    for a, b in pairs:
        in_specs += [pl.BlockSpec((a.shape[0], rows), lambda j: (0, j)), _resident(b)]
        ops += [a, b]
    return _pcall(
        body, name=name, grid=(8,), in_specs=in_specs,
        out_specs=pl.BlockSpec((None, len(pairs) * rows, D), lambda j: (j, 0, 0)),
        out_shape=jax.ShapeDtypeStruct((8, len(pairs) * rows, D), F32),
        compiler_params=pltpu.CompilerParams(dimension_semantics=("arbitrary",), vmem_limit_bytes=VMEM_BIG),
    )(*ops)


def _adam_math(w, g, m, v):
    m2 = ADAM_B1 * m + (1.0 - ADAM_B1) * g
    v2 = ADAM_B2 * v + (1.0 - ADAM_B2) * jnp.square(g)
    m_hat = m2 / (1.0 - ADAM_B1 ** ADAM_STEP)
    v_hat = v2 / (1.0 - ADAM_B2 ** ADAM_STEP)
    delta = -ADAM_LR * (m_hat / (jnp.sqrt(v_hat) + ADAM_EPS) + ADAM_WD * w)
    return delta, m2, v2


def _adamw_small(gparts, ws, ms, vs):
    n = len(SMALL)

    def body(g_ref, *refs):
        w_refs, m_refs, v_refs = refs[:n], refs[n:2 * n], refs[2 * n:3 * n]
        outs, sum_ref = refs[3 * n:-1], refs[-1]
        g = g_ref[0]
        for k in range(1, 8):
            g = g + g_ref[k]
        sum_ref[...] = g
        outs[0][...] = sum_ref[17:18, 0:128]
        for t, name in enumerate(SMALL):
            r0, nr, c0, nc = SMALL_SLOT[name]
            gt = sum_ref[r0:r0 + nr, c0:c0 + nc]
            out = (gt,) + _adam_math(w_refs[t][...], gt, m_refs[t][...], v_refs[t][...])
            for o_ref, val in zip(outs[1 + 4 * t:5 + 4 * t], out):
                o_ref[...] = val

    whole = lambda s: pl.BlockSpec(s, lambda i, nd=len(s): (0,) * nd)
    ins = [gparts] + list(ws) + list(ms) + list(vs)
    out_shapes = [(1, 128)] + [a.shape for a in ws for _ in range(4)]
    return _pcall(
        body, name="adamw_small", grid=(1,), in_specs=[whole(a.shape) for a in ins],
        out_specs=[whole(s) for s in out_shapes], out_shape=[jax.ShapeDtypeStruct(s, F32) for s in out_shapes],
        scratch_shapes=[pltpu.VMEM((SMALL_ROWS, D), F32)],
        compiler_params=pltpu.CompilerParams(dimension_semantics=("arbitrary",)),
    )(*ins)


def _row_tile(rows):
    return next(t for t in (512, 400, 320) if rows % t == 0)


def _add_halves(g4, theirs, core, name):
    rows = g4.shape[2]
    tr = _row_tile(rows)

    def body(c_ref, a_ref, b_ref, o_ref):
        o_ref[...] = (a_ref[...] + b_ref[...]).astype(BF16)

    grid_spec = pltpu.PrefetchScalarGridSpec(
        num_scalar_prefetch=1, grid=(4, rows // tr),
        in_specs=[pl.BlockSpec((None, None, tr, D), lambda j, i, c: (j, c[0], i, 0)),
                  pl.BlockSpec((None, None, tr, D), lambda j, i, c: (j, 0, i, 0))],
        out_specs=pl.BlockSpec((None, tr, D), lambda j, i, c: (j, i, 0)))
    return _pcall(
        body, name=name, grid_spec=grid_spec, out_shape=jax.ShapeDtypeStruct((4, rows, D), BF16),
        compiler_params=pltpu.CompilerParams(dimension_semantics=("arbitrary", "arbitrary")),
    )(core, g4, theirs)


def _sum_adam(own, got, order, r0, w, m, v, name, transposed=False):
    n = w.shape[1] if transposed else w.shape[0]
    tr = min(n, 256) if n % 8 == 0 else n
    rows = tr if n % 8 == 0 else own.shape[1]

    def body(o_ref, a_ref, b_ref, c_ref, d_ref, w_ref, m_ref, v_ref, g_ref, dl_ref, m2_ref, v2_ref):
        f = lambda r: r[0:tr, :].astype(F32)
        g = ((f(a_ref) + f(b_ref)) + f(c_ref)) + f(d_ref)
        g = g.T if transposed else g
        g_ref[...] = g
        dl_ref[...], m2_ref[...], v2_ref[...] = _adam_math(w_ref[...], g, m_ref[...], v_ref[...])

    slot = lambda k: pl.BlockSpec((None, rows, D), lambda i, o: (o[k], r0 // rows + i, 0))
    wspec = pl.BlockSpec((D, tr), lambda i, o: (0, i)) if transposed else pl.BlockSpec((tr, D), lambda i, o: (i, 0))
    grid_spec = pltpu.PrefetchScalarGridSpec(
        num_scalar_prefetch=1, grid=(n // tr,), in_specs=[slot(0), slot(1), slot(2), slot(3), wspec, wspec, wspec],
        out_specs=[wspec] * 4)
    return _pcall(
        body, name=name, grid_spec=grid_spec, out_shape=[jax.ShapeDtypeStruct(w.shape, F32)] * 4,
        compiler_params=pltpu.CompilerParams(dimension_semantics=("arbitrary",)),
    )(order, own, got, got, got, w, m, v)


def _sum_adam_rows(own, got, order, ws, ms, vs, name):
    n, rows = len(ws), ws[0].shape[0]

    def body(o_ref, a_ref, b_ref, c_ref, d_ref, *refs):
        ins, outs = refs[:3 * n], refs[3 * n:]
        for t in range(n):
            r = slice(t * rows, (t + 1) * rows)
            f = lambda ref: ref[r, :].astype(F32)
            g = ((f(a_ref) + f(b_ref)) + f(c_ref)) + f(d_ref)
            out = (g,) + _adam_math(ins[t][...], g, ins[n + t][...], ins[2 * n + t][...])
            for o, val in zip(outs[4 * t:4 * t + 4], out):
                o[...] = val

    slot = lambda k: pl.BlockSpec((None, n * rows, D), lambda i, o: (o[k], 0, 0), pipeline_mode=pl.Buffered(1))
    wspec = pl.BlockSpec((rows, D), lambda i, o: (0, 0), pipeline_mode=pl.Buffered(1))
    grid_spec = pltpu.PrefetchScalarGridSpec(
        num_scalar_prefetch=1, grid=(1,), in_specs=[slot(0), slot(1), slot(2), slot(3)] + [wspec] * (3 * n),
        out_specs=[pl.BlockSpec((rows, D), lambda i, o: (0, 0))] * (4 * n))
    return _pcall(
        body, name=name, grid_spec=grid_spec, out_shape=[jax.ShapeDtypeStruct((rows, D), F32)] * (4 * n),
        compiler_params=pltpu.CompilerParams(dimension_semantics=("arbitrary",), vmem_limit_bytes=VMEM_BIG),
    )(order, own, got, got, got, *ws, *ms, *vs)


def _place():
    return lax.axis_index("x"), lax.axis_index("y"), lax.axis_index("c")


def _allgather(block, name):
    def body(x_ref, out_ref, token, send_sems, recv_sems, local_sem):
        token[...] = jnp.zeros_like(token)
        x, y, c = _place()
        me, sibling = (x, y, c), (x, y, 1 - c)
        chips = [(1 - x, y), (x, 1 - y), (1 - x, 1 - y)]

        def slot(px, py, pc):
            return out_ref.at[4 * px + 2 * py + pc]

        def copy(k, blk, to, src=None):
            return pltpu.make_async_remote_copy(
                src_ref=slot(*blk) if src is None else src, dst_ref=slot(*blk),
                send_sem=send_sems.at[k], recv_sem=recv_sems.at[k], device_id=to, device_id_type=MESH)

        mine = pltpu.make_async_copy(x_ref, slot(*me), local_sem)
        mine.start()
        first = [copy(0, me, sibling, src=x_ref)]
        first += [copy(1 + j, me, (*chip, c), src=x_ref) for j, chip in enumerate(chips)]
        for cp in first:
            cp.start()
        passed = [copy(4 + j, (*chip, c), sibling) for j, chip in enumerate(chips)]
        for j, chip in enumerate(chips):
            copy(1 + j, (*chip, c), me).wait_recv()
            passed[j].start()
        copy(0, sibling, me).wait_recv()
        for j, chip in enumerate(chips):
            copy(4 + j, (*chip, 1 - c), me).wait_recv()
        for cp in first + passed:
            cp.wait_send()
        mine.wait()

    return _pcall(
        body, name=name,
        out_shape=[jax.ShapeDtypeStruct((8,) + block.shape, block.dtype), jax.ShapeDtypeStruct((8, 128), F32)],
        in_specs=[pl.BlockSpec(memory_space=pl.ANY)],
        out_specs=[pl.BlockSpec(memory_space=pl.ANY), pl.BlockSpec(memory_space=pltpu.VMEM)],
        scratch_shapes=[pltpu.SemaphoreType.DMA((7,)), pltpu.SemaphoreType.DMA((7,)), pltpu.SemaphoreType.DMA(())],
        compiler_params=pltpu.CompilerParams(has_side_effects=True),
    )(block)


HBM_SPEC = pl.BlockSpec(memory_space=pltpu.HBM)
SEM_SPEC = pl.BlockSpec(memory_space=pltpu.SEMAPHORE)
ANY_SPEC = pl.BlockSpec(memory_space=pl.ANY)
EFFECT = pltpu.SideEffectType.DATAFLOW_SIDE_EFFECTING


def _in_hbm(a):
    return pltpu.with_memory_space_constraint(a, pltpu.HBM)


def _start_copies(name, src, land_shape, plan, n):
    def body(src_ref, land_ref, send_sems, recv_sems, src_thru, land_thru, token):
        for k, (s, d, to, _) in enumerate(plan(src_ref, land_ref)):
            pltpu.make_async_remote_copy(src_ref=s, dst_ref=d, send_sem=send_sems.at[k], recv_sem=recv_sems.at[k],
                                         device_id=to, device_id_type=MESH).start()
        token[...] = jnp.zeros_like(token)

    return _pcall(
        body, name=name,
        out_shape=(pltpu.SemaphoreType.DMA((n,)), pltpu.SemaphoreType.DMA((n,)), pltpu.HBM(src.shape, src.dtype),
                   pltpu.HBM(land_shape, src.dtype), jax.ShapeDtypeStruct((8, 128), F32)),
        in_specs=(HBM_SPEC, HBM_SPEC),
        out_specs=(SEM_SPEC, SEM_SPEC, HBM_SPEC, HBM_SPEC, pl.BlockSpec(memory_space=pltpu.VMEM)),
        input_output_aliases={0: 2, 1: 3}, compiler_params=pltpu.CompilerParams(has_side_effects=EFFECT),
    )(_in_hbm(src), _in_hbm(lax.empty(land_shape, src.dtype)))


def _wait_copies(name, started, after, plan):
    send_sems, recv_sems, src_thru, land_thru, _ = started

    def body(src_ref, land_ref, send_sems, recv_sems, *rest):
        for k, (s, _, to, mine) in enumerate(plan(src_ref, land_ref)):
            cp = pltpu.make_async_remote_copy(src_ref=s, dst_ref=mine, send_sem=send_sems.at[k],
                                              recv_sem=recv_sems.at[k], device_id=to, device_id_type=MESH)
            cp.wait_send()
            cp.wait_recv()

    return _pcall(
        body, name=name,
        out_shape=(pltpu.HBM(src_thru.shape, src_thru.dtype), pltpu.HBM(land_thru.shape, land_thru.dtype)),
        in_specs=(HBM_SPEC, HBM_SPEC, SEM_SPEC, SEM_SPEC) + (ANY_SPEC,) * len(after), out_specs=(HBM_SPEC, HBM_SPEC),
        input_output_aliases={0: 0, 1: 1}, compiler_params=pltpu.CompilerParams(has_side_effects=EFFECT),
    )(src_thru, land_thru, send_sems, recv_sems, *after)


def _start_inplace(name, buf, plan, n):
    def body(buf_ref, send_sems, recv_sems, buf_thru, token):
        for k, (s, d, to, _) in enumerate(plan(buf_ref, buf_ref)):
            pltpu.make_async_remote_copy(src_ref=s, dst_ref=d, send_sem=send_sems.at[k], recv_sem=recv_sems.at[k],
                                         device_id=to, device_id_type=MESH).start()
        token[...] = jnp.zeros_like(token)

    return _pcall(
        body, name=name,
        out_shape=(pltpu.SemaphoreType.DMA((n,)), pltpu.SemaphoreType.DMA((n,)), pltpu.HBM(buf.shape, buf.dtype),
                   jax.ShapeDtypeStruct((8, 128), F32)),
        in_specs=(HBM_SPEC,), out_specs=(SEM_SPEC, SEM_SPEC, HBM_SPEC, pl.BlockSpec(memory_space=pltpu.VMEM)),
        input_output_aliases={0: 2}, compiler_params=pltpu.CompilerParams(has_side_effects=EFFECT),
    )(_in_hbm(buf))


def _wait_inplace(name, started, after, plan):
    send_sems, recv_sems, buf_thru, _ = started

    def body(buf_ref, send_sems, recv_sems, *rest):
        for k, (s, _, to, mine) in enumerate(plan(buf_ref, buf_ref)):
            cp = pltpu.make_async_remote_copy(src_ref=s, dst_ref=mine, send_sem=send_sems.at[k],
                                              recv_sem=recv_sems.at[k], device_id=to, device_id_type=MESH)
            cp.wait_send()
            cp.wait_recv()

    return _pcall(
        body, name=name, out_shape=pltpu.HBM(buf_thru.shape, buf_thru.dtype),
        in_specs=(HBM_SPEC, SEM_SPEC, SEM_SPEC) + (ANY_SPEC,) * len(after), out_specs=HBM_SPEC,
        input_output_aliases={0: 0}, compiler_params=pltpu.CompilerParams(has_side_effects=EFFECT),
    )(buf_thru, send_sems, recv_sems, *after)


def _gather_plan(src_ref, land_ref):
    x, y, c = _place()
    peers = [(x, y, 1 - c), (1 - x, y, c), (x, 1 - y, c)]
    return [(src_ref, land_ref.at[4 * x + 2 * y + c], p, land_ref.at[4 * p[0] + 2 * p[1] + p[2]]) for p in peers]


def _relay_plan(buf_ref, _):
    x, y, c = _place()
    slot = lambda p, pc: 4 * p[0] + 2 * p[1] + pc
    xn, yn, dg, sib = (1 - x, y), (x, 1 - y), (1 - x, 1 - y), (x, y, 1 - c)
    half = buf_ref.shape[1] // 2
    lo, hi = pl.ds(0, half), pl.ds(half, half)
    return [(buf_ref.at[slot(xn, c)], buf_ref.at[slot(xn, c)], sib, buf_ref.at[slot(xn, 1 - c)]),
            (buf_ref.at[slot(yn, c)], buf_ref.at[slot(yn, c)], sib, buf_ref.at[slot(yn, 1 - c)]),
            (buf_ref.at[slot(xn, c), lo], buf_ref.at[slot(xn, c), lo], (*yn, c), buf_ref.at[slot(dg, c), lo]),
            (buf_ref.at[slot(yn, c), hi], buf_ref.at[slot(yn, c), hi], (*xn, c), buf_ref.at[slot(dg, c), hi])]


def _swap_plan(src_ref, land_ref):
    x, y, c = _place()
    return [(src_ref.at[:, pl.ds(1 - c, 1)], land_ref, (x, y, 1 - c), land_ref)]


def _exchange_plan(src_ref, land_ref):
    x, y, c = _place()
    chips = [(1 - x, y), (x, 1 - y), (1 - x, 1 - y)]
    return [(src_ref.at[2 * px + py], land_ref.at[2 * x + y], (px, py, c), land_ref.at[2 * px + py]) for px, py in chips]


def _gather_forward(land, block):
    def body(land_ref, out_ref, send_sems, recv_sems):
        x, y, c = _place()
        chips = [(1 - x, 1 - y)]

        def copy(k, px, py, pc):
            blk = out_ref.at[4 * px + 2 * py + pc]
            return pltpu.make_async_remote_copy(src_ref=blk, dst_ref=blk, send_sem=send_sems.at[k],
                                                recv_sem=recv_sems.at[k], device_id=(x, y, 1 - c), device_id_type=MESH)

        sent = [copy(k, px, py, c) for k, (px, py) in enumerate(chips)]
        for cp in sent:
            cp.start()
        for k, (px, py) in enumerate(chips):
            copy(k, px, py, 1 - c).wait_recv()
        for cp in sent:
            cp.wait_send()

    land = _pcall(
        body, name="allgather_rest_forward", out_shape=jax.ShapeDtypeStruct(land.shape, land.dtype),
        in_specs=[ANY_SPEC], out_specs=ANY_SPEC, input_output_aliases={0: 0},
        scratch_shapes=[pltpu.SemaphoreType.DMA((1,)), pltpu.SemaphoreType.DMA((1,))],
        compiler_params=pltpu.CompilerParams(has_side_effects=True),
    )(land)

    rows = block.shape[0]
    tr = rows // 4

    def place(me_ref, x_ref, land_ref, out_ref):
        out_ref[...] = x_ref[...]

    x, y, c = _place()
    grid_spec = pltpu.PrefetchScalarGridSpec(
        num_scalar_prefetch=1, grid=(rows // tr,),
        in_specs=[pl.BlockSpec((tr, D), lambda i, me: (i, 0)), ANY_SPEC],
        out_specs=pl.BlockSpec((None, tr, D), lambda i, me: (me[0], i, 0)))
    return _pcall(
        place, name="allgather_rest_own", grid_spec=grid_spec, out_shape=jax.ShapeDtypeStruct(land.shape, land.dtype),
        input_output_aliases={2: 0}, compiler_params=pltpu.CompilerParams(dimension_semantics=("arbitrary",)),
    )((4 * x + 2 * y + c).reshape(1), block, land)


class _ReduceScatter:
    def __init__(self, name, g):
        self.name = name
        rows = g.shape[1]
        self.started = _start_copies(name + "_swap_start", g.reshape(4, 2, rows, D), (4, 1, rows, D), _swap_plan, 1)
        self.token = self.started[4][0, 0]

    def halfway(self, after):
        g4, theirs = _wait_copies(self.name + "_swap_wait", self.started, after, _swap_plan)
        self.own = _add_halves(g4, theirs, lax.axis_index("c").reshape(1), self.name + "_add_halves")
        self.started = _start_copies(self.name + "_exch_start", self.own, self.own.shape, _exchange_plan, 3)
        self.token = self.started[4][0, 0]

    def finish(self, after):
        own, got = _wait_copies(self.name + "_exch_wait", self.started, after, _exchange_plan)
        chip = 2 * lax.axis_index("x") + lax.axis_index("y")
        return own, got, (chip + jnp.arange(4, dtype=jnp.int32)) % 4


def _pack_small(p, scalar=None):
    z = lambda a, n: jnp.pad(a, ((0, 0), (0, n - a.shape[1])))
    rows = [z(p['rel_bias'], D), z(p['b_fgt'], D), jnp.concatenate([p['g_fox_out'], p['g_chk_out']], axis=1)]
    rows += [p[n] for n in ('g_mix_pre', 'g_mix_post', 'g_mem_kv', 'g_mem_pre', 'g_mem_post', 'g_ff_pre', 'g_ff_post')]
    rows.append(jnp.zeros((1, D), F32) if scalar is None else z(jnp.reshape(scalar, (1, 1)), D))
    rows.append(jnp.zeros((SMALL_ROWS - 18, D), F32))
    return jnp.concatenate(rows, axis=0)


_GAP_DEV, _GAP_ROW = divmod(GATE0 + 8, N_IN)
_GAP = CHK0 - GATE0 - 8


def _in_rows_to_proj(g):
    runs = [(j, 0, N_IN, N_IN * j) for j in range(_GAP_DEV)]
    runs += [(_GAP_DEV, 0, _GAP_ROW, N_IN * _GAP_DEV), (_GAP_DEV, _GAP_ROW, N_IN, N_IN * _GAP_DEV + _GAP_ROW + _GAP)]
    runs += [(j, 0, N_IN, N_IN * j + _GAP) for j in range(_GAP_DEV + 1, 8)]

    def body(g_ref, o_ref, acc_ref):
        acc_ref[...] = jnp.zeros_like(acc_ref)
        for j, r0, r1, dest in runs:
            start, shift = dest // 16 * 16, dest % 16
            win = -(-(shift + r1 - r0) // 16) * 16
            r = lax.broadcasted_iota(jnp.int32, (win, R_IN), 0)
            c = lax.broadcasted_iota(jnp.int32, (win, R_IN), 1)
            move = jnp.where((c >= r0) & (c < r1) & (r == c - r0 + shift), 1.0, 0.0).astype(BF16)
            acc_ref[start:start + win, :] += _nn(move, g_ref[j])
        o_ref[...] = acc_ref[...].astype(BF16)

    return _pcall(
        body, name="w_in_layout", out_shape=jax.ShapeDtypeStruct((PROJ, D), BF16), grid=(1,),
        in_specs=[pl.BlockSpec(g.shape, lambda i: (0, 0, 0))], out_specs=pl.BlockSpec((PROJ, D), lambda i: (0, 0)),
        scratch_shapes=[pltpu.VMEM((PROJ, D), F32)],
        compiler_params=pltpu.CompilerParams(dimension_semantics=("arbitrary",), vmem_limit_bytes=VMEM_BIG),
    )(g)


def _proj_rows_to_in(g):
    pad = lambda a: jnp.pad(a, ((0, R_IN - a.shape[0]), (0, 0)))
    lo = N_IN * _GAP_DEV
    shards = [pad(g[N_IN * j:N_IN * (j + 1)]) for j in range(_GAP_DEV)]
    shards.append(pad(jnp.concatenate([g[lo:lo + _GAP_ROW], g[lo + _GAP_ROW + _GAP:lo + N_IN + _GAP]], axis=0)))
    shards += [pad(g[N_IN * j + _GAP:N_IN * (j + 1) + _GAP]) for j in range(_GAP_DEV + 1, 8)]
    return jnp.stack(shards)


def _local_grads(x, mem, tgt, win_t, gw_of, sm, on_grads):
    b_pad = jnp.pad(sm['b_fgt'], ((0, 0), (0, 120)))
    tbl = jnp.pad(sm['rel_bias'], ((0, 0), (0, NREL_PAD - 257)))

    h1, proj, flog = _premix_fwd(x, sm['g_mix_pre'], win_t)
    c = _gate_fwd(flog, b_pad)
    ct3 = c[:, :8].T.reshape(4, 2, T)
    o_f, lse_f = _fox_fwd(proj, c, ct3)
    vt3 = _relvec_fwd(tbl).reshape(4, 2, VW)
    kvp = jnp.pad(proj[:, CHK0 + 512:], ((LEFT, 0), (0, 0)))
    o_c, lse_c = _chk_fwd(proj, kvp, vt3 + gw_of('relay', [o_f]))
    gw = gw_of('done', [o_c])
    w_out, w_mq, w_mk, w_mv, w_mo, w1_t, w2 = (_wblk(gw, n) for n in ('w_out', 'w_mq', 'w_mk', 'w_mv', 'w_mo', 'w_ff1', 'w_ff2'))
    ycat, z, x1, h2, qm = _postmix_fwd(x, o_f, o_c, sm['g_fox_out'], sm['g_chk_out'], w_out,
                                       sm['g_mix_post'], sm['g_mem_pre'], w_mq)
    memn, km, vm = _memkv_fwd(mem, sm['g_mem_kv'], w_mk, w_mv)
    om, ym, x2, h3 = _mem_fwd(qm, x1, km, vm, w_mo, sm['g_mem_post'], sm['g_ff_pre'])

    gs = {}
    dx2, da, dy3, r, loss_acc, gs['g_ff_post'], gs['g_ff_pre'] = _ffn_step(h3, x2, tgt, w1_t, w2, sm['g_ff_post'],
                                                                         sm['g_ff_pre'])
    zero = on_grads('A', _wgrad_group("wgrad_ff", [(da, h3), (r, dy3)], 512), None)
    dx1, dym, dqm, dkm, dvm, gs['g_mem_post'], gs['g_mem_pre'] = _mem_bwd(
        dx2, ym, x1, qm, km, vm, w_mo, w_mq, sm['g_mem_post'] + zero, sm['g_mem_pre'])
    zero = on_grads('A halfway', None, [dx1])
    gs['g_mem_kv'] = _memkv_bwd(dkm, dvm, mem, w_mk, w_mv)
    dz, dof, doc, gs['g_mix_post'], gs['g_fox_out'], gs['g_chk_out'] = _postmix_bwd(
        dx1, z, o_f, o_c, w_out, sm['g_mix_post'] + zero, sm['g_fox_out'], sm['g_chk_out'])
    zero = on_grads('B', _wgrad_group("wgrad_mem_out", [(ycat, dz), (h2, dqm), (memn, dkm), (memn, dvm), (om, dym)], 128), None)
    dq_f, dk_f, dv_f, dct, dcq = _fox_bwd(proj, c, ct3 + zero, o_f, lse_f, dof)
    zero = on_grads('B halfway', None, [dq_f])
    dq_c, dk_c, dv_c, gv = _chk_bwd(proj, kvp, vt3 + zero, o_c, lse_c, doc)
    gs['rel_bias'] = _relvec_bwd(gv.reshape(8, VW))[:, :257]
    dc = jnp.pad(dct.reshape(8, T).T + dcq[:, :, :2].transpose(1, 0, 2).reshape(T, 8), ((0, 0), (0, 120)))
    dflog, db = _gate_bwd(dc, flog, b_pad)
    gs['b_fgt'] = db[0:1, :8]
    grad_x, dproj, gs['g_mix_pre'] = _premix_bwd(dx1, x, [dq_f, dk_f, dv_f, dflog, dq_c, dk_c, dv_c], win_t, sm['g_mix_pre'])
    on_grads('C', _proj_rows_to_in(_wgrad(dproj, h1, "wgrad_in")), None)
    return loss_acc[0, 0], grad_x, gs


def kernel(x, mem, w_in, b_fgt, rel_bias, g_fox_out, g_chk_out, w_out, g_mix_pre, g_mix_post, g_mem_kv, w_mq, w_mk, w_mv, w_mo, g_mem_pre, g_mem_post, w_ff1, w_ff2, g_ff_pre, g_ff_post, loss_target, m_w_in, m_b_fgt, m_rel_bias, m_g_fox_out, m_g_chk_out, m_w_out, m_g_mix_pre, m_g_mix_post, m_g_mem_kv, m_w_mq, m_w_mk, m_w_mv, m_w_mo, m_g_mem_pre, m_g_mem_post, m_w_ff1, m_w_ff2, m_g_ff_pre, m_g_ff_post, v_w_in, v_b_fgt, v_rel_bias, v_g_fox_out, v_g_chk_out, v_w_out, v_g_mix_pre, v_g_mix_post, v_g_mem_kv, v_w_mq, v_w_mk, v_w_mv, v_w_mo, v_g_mem_pre, v_g_mem_post, v_w_ff1, v_w_ff2, v_g_ff_pre, v_g_ff_post):
    args = dict(locals())
    two_d = lambda a: a.reshape(a.shape[-2:])
    w = {n: two_d(args[n]) for n in WEIGHTS}
    m = {n: two_d(args['m_' + n]) for n in WEIGHTS}
    v = {n: two_d(args['v_' + n]) for n in WEIGHTS}

    sm = {n: w[n] for n in SMALL}
    shard_in = jnp.pad(w['w_in'].T, ((0, R_IN - N_IN), (0, 0))).astype(BF16)
    gathered_in, zero = _allgather(shard_in, "allgather_w_in")
    win_t = _in_rows_to_proj(gathered_in)
    shard_rest = (jnp.concatenate([w['w_ff1'].T, w['w_ff2'], w['w_out'], w['w_mq'], w['w_mk'], w['w_mv'], w['w_mo']],
                                  axis=0) + zero[0, 0]).astype(BF16)
    gather = {'first': _start_copies("allgather_rest_start", shard_rest, (8, R_REST, D), _gather_plan, 3)}
    sm['g_mix_pre'] = sm['g_mix_pre'] + gather['first'][4][0, 0]

    def gw_of(stage, after):
        if stage == 'relay':
            gather['block'], land = _wait_copies("allgather_rest_wait", gather['first'], after, _gather_plan)
            gather['second'] = _start_inplace("allgather_rest_relay_start", land, _relay_plan, 4)
            return gather['second'][3][0, 0]
        land = _wait_inplace("allgather_rest_relay_wait", gather['second'], after, _relay_plan)
        return _gather_forward(land, gather['block'])

    rs = {}

    def on_grads(stage, g, after):
        if stage.endswith('halfway'):
            rs[stage[0]].halfway(after)
            return rs[stage[0]].token
        rs[stage] = _ReduceScatter("rs_" + stage.lower(), g)
        return rs[stage].token

    loss_local, grad_x, gs = _local_grads(x[0], mem[0], loss_target[0], win_t, gw_of, sm, on_grads)
    grads, deltas, new_m, new_v = {}, {}, {}, {}

    def update(n, out):
        grads[n], deltas[n], new_m[n], new_v[n] = out

    packed = _pack_small(gs, loss_local + rs['C'].token)
    rs['C'].halfway([packed])
    gparts, _ = _allgather(packed + rs['C'].token, "allgather_small_grads")
    small = _adamw_small(gparts, [w[n] for n in SMALL], [m[n] for n in SMALL], [v[n] for n in SMALL])
    loss = small[0][0, 0]
    for t, n in enumerate(SMALL):
        update(n, small[1 + 4 * t:5 + 4 * t])

    own, got, order = rs['A'].finish([grad_x, rs['C'].started[4]])
    update('w_ff1', _sum_adam(own, got, order, 0, w['w_ff1'], m['w_ff1'], v['w_ff1'], "adamw_w_ff1", transposed=True))
    update('w_ff2', _sum_adam(own, got, order, 512, w['w_ff2'], m['w_ff2'], v['w_ff2'], "adamw_w_ff2"))
    own, got, order = rs['B'].finish([grad_x, rs['C'].started[4]])
    names_b = ('w_out', 'w_mq', 'w_mk', 'w_mv', 'w_mo')
    done = _sum_adam_rows(own, got, order, [w[n] for n in names_b], [m[n] for n in names_b], [v[n] for n in names_b],
                          "adamw_group_b")
    for k, n in enumerate(names_b):
        update(n, done[4 * k:4 * k + 4])

    own, got, order = rs['C'].finish([new_v[n] for n in BIG if n != 'w_in'])
    done = _sum_adam(own, got, order, 0, w['w_in'].T, m['w_in'].T, v['w_in'].T, "adamw_w_in")
    update('w_in', [a.T for a in done])

    out = [loss, grad_x[None]]
    for group in (grads, deltas, new_m, new_v):
        out += [group[n].reshape(args[n].shape) for n in WEIGHTS]
    return tuple(out)
```

```python
import jax
import jax.numpy as jnp
from jax import lax
from jax.experimental import pallas as pl
from jax.experimental.pallas import tpu as pltpu

F32 = jnp.float32
BF16 = jnp.bfloat16
MESH = pl.DeviceIdType.MESH

T = 2048
D = 1024
NMEM = 256
DFF = 4096
EPS = 1e-6
TM = 256
TM_WIDE = 512
TQ = 256
FQ = 512
HD = 64
SCALE = HD ** -0.5
MEM_HEADS = 4
MEM_HD = 256
MEM_SCALE = MEM_HD ** -0.5
NEG = -1e30
LEFT = 512
WIN = LEFT + TQ
VW = 1024
NREL_PAD = 384
PROJ = 3200
GATE0 = 1536
CHK0 = 1664
VMEM_BIG = 56 * 1024 * 1024

ADAM_LR = 0.001
ADAM_B1 = 0.9
ADAM_B2 = 0.999
ADAM_EPS = 1e-08
ADAM_WD = 0.01
ADAM_STEP = 10

N_IN = 385
R_IN = 400
R_REST = 1664
W_ROWS = {'w_ff1': (0, 512), 'w_ff2': (512, 512),
          'w_out': (1024, 128), 'w_mq': (1152, 128), 'w_mk': (1280, 128), 'w_mv': (1408, 128), 'w_mo': (1536, 128)}
SMALL_ROWS = 24
SMALL_SLOT = {'rel_bias': (0, 8, 0, 257), 'b_fgt': (8, 1, 0, 8), 'g_fox_out': (9, 1, 0, 512), 'g_chk_out': (9, 1, 512, 512),
              'g_mix_pre': (10, 1, 0, 1024), 'g_mix_post': (11, 1, 0, 1024), 'g_mem_kv': (12, 1, 0, 1024),
              'g_mem_pre': (13, 1, 0, 1024), 'g_mem_post': (14, 1, 0, 1024), 'g_ff_pre': (15, 1, 0, 1024),
              'g_ff_post': (16, 1, 0, 1024)}

WEIGHTS = ['w_in', 'b_fgt', 'rel_bias', 'g_fox_out', 'g_chk_out', 'w_out', 'g_mix_pre', 'g_mix_post', 'g_mem_kv',
           'w_mq', 'w_mk', 'w_mv', 'w_mo', 'g_mem_pre', 'g_mem_post', 'w_ff1', 'w_ff2', 'g_ff_pre', 'g_ff_post']
BIG = ['w_in', 'w_out', 'w_mq', 'w_mk', 'w_mv', 'w_mo', 'w_ff1', 'w_ff2']
SMALL = [n for n in WEIGHTS if n not in BIG]


def _pcall(body, **kw):
    return pl.pallas_call(body, **kw)


def _nn(a, b):
    return jnp.dot(a, b, preferred_element_type=F32)


def _nt(a, b):
    return lax.dot_general(a, b, (((1,), (1,)), ((), ())), preferred_element_type=F32)


def _tn(a, b):
    return lax.dot_general(a, b, (((0,), (0,)), ((), ())), preferred_element_type=F32)


def _w(ref):
    v = ref[...]
    return v if v.ndim == 2 else v.reshape(-1, v.shape[-1])


def _rstd(x):
    return lax.rsqrt(jnp.mean(x * x, axis=-1, keepdims=True) + EPS)


def _rms(x, g):
    return x * _rstd(x) * g


def _rms_bwd(x, g, dy):
    r = _rstd(x)
    xh = x * r
    dg = jnp.sum(dy * xh, axis=0, keepdims=True)
    dxh = dy * g
    dx = r * (dxh - xh * jnp.mean(dxh * xh, axis=-1, keepdims=True))
    return dx, dg


def _resident(a):
    if isinstance(a, tuple):
        _, shape, index = a
        return pl.BlockSpec(shape, lambda *_: index, pipeline_mode=pl.Buffered(1))
    return pl.BlockSpec(a.shape, lambda *_, nd=a.ndim: (0,) * nd, pipeline_mode=pl.Buffered(1))


def _wblk(gw, name):
    r0, rows = W_ROWS[name]
    return (gw, (8, rows, D), (0, r0 // rows, 0))


def _tok_call(body, name, tiled, full, outs_tiled, outs_acc=(), rows=T, tm=TM, vmem=None):
    in_specs = [pl.BlockSpec((tm, a.shape[1]), lambda i: (i, 0)) for a in tiled]
    in_specs += [_resident(a) for a in full]
    full = [a[0] if isinstance(a, tuple) else a for a in full]
    out_shape = [jax.ShapeDtypeStruct((rows, c), dt) for c, dt in outs_tiled]
    out_shape += [jax.ShapeDtypeStruct(s, F32) for s in outs_acc]
    out_specs = [pl.BlockSpec((tm, c), lambda i: (i, 0)) for c, _ in outs_tiled]
    out_specs += [pl.BlockSpec(s, lambda i, nd=len(s): (0,) * nd) for s in outs_acc]
    return _pcall(
        body, name=name, grid=(rows // tm,), in_specs=in_specs, out_specs=out_specs, out_shape=out_shape,
        compiler_params=pltpu.CompilerParams(dimension_semantics=("arbitrary",), vmem_limit_bytes=vmem),
    )(*tiled, *full)


def _one_call(body, name, ins, outs, vmem=None):
    whole = lambda s: pl.BlockSpec(s, lambda i, nd=len(s): (0,) * nd)
    return _pcall(
        body, name=name, grid=(1,), in_specs=[_resident(a) for a in ins], out_specs=[whole(s) for s, _ in outs],
        out_shape=[jax.ShapeDtypeStruct(s, dt) for s, dt in outs],
        compiler_params=pltpu.CompilerParams(dimension_semantics=("arbitrary",), vmem_limit_bytes=vmem),
    )(*[a[0] if isinstance(a, tuple) else a for a in ins])


def _premix_fwd(x, g_pre, win_t):
    def body(x_ref, g_ref, w_ref, h_ref, proj_ref, flog_ref):
        h = _rms(x_ref[...], g_ref[...]).astype(BF16)
        h_ref[...] = h
        p = _nt(h, w_ref[...])
        proj_ref[...] = p.astype(BF16)
        flog_ref[...] = p[:, GATE0:GATE0 + 128]

    return _tok_call(body, "premix_fwd", [x], [g_pre, win_t],
                     [(D, BF16), (PROJ, BF16), (128, F32)], tm=TM_WIDE, vmem=VMEM_BIG)


def _postmix_fwd(x, o_f, o_c, g_fo, g_co, w_out, g_post, g_mpre, w_mq):
    def body(x_ref, of_ref, oc_ref, gfo_ref, gco_ref, wo_ref, gp_ref, gm_ref, wq_ref,
             y_ref, z_ref, x1_ref, h2_ref, qm_ref):
        y_ref[:, :512] = _rms(of_ref[...], gfo_ref[...]).astype(BF16)
        y_ref[:, 512:] = _rms(oc_ref[...], gco_ref[...]).astype(BF16)
        z = _nn(y_ref[...], _w(wo_ref))
        z_ref[...] = z
        x1 = x_ref[...] + _rms(z, gp_ref[...])
        x1_ref[...] = x1
        h2 = _rms(x1, gm_ref[...]).astype(BF16)
        h2_ref[...] = h2
        qm_ref[...] = _nn(h2, _w(wq_ref)).astype(BF16)

    return _tok_call(body, "postmix_fwd", [x, o_f, o_c], [g_fo, g_co, w_out, g_post, g_mpre, w_mq],
                     [(D, BF16), (D, F32), (D, F32), (D, BF16), (D, BF16)], tm=TM_WIDE, vmem=VMEM_BIG)


def _memkv_fwd(mem, g_kv, w_mk, w_mv):
    def body(m_ref, g_ref, wk_ref, wv_ref, mn_ref, k_ref, v_ref):
        mn = _rms(m_ref[...], g_ref[...]).astype(BF16)
        mn_ref[...] = mn
        k_ref[...] = _nn(mn, _w(wk_ref)).astype(BF16)
        v_ref[...] = _nn(mn, _w(wv_ref)).astype(BF16)

    return _tok_call(body, "memkv_fwd", [mem], [g_kv, w_mk, w_mv],
                     [(D, BF16), (D, BF16), (D, BF16)], rows=NMEM, tm=NMEM, vmem=VMEM_BIG)


def _mem_fwd(qm, x1, km, vm, w_mo, g_post, g_fpre):
    def body(q_ref, x1_ref, k_ref, v_ref, wo_ref, gp_ref, gf_ref, om_ref, ym_ref, x2_ref, h3_ref):
        for h in range(MEM_HEADS):
            sl = slice(h * MEM_HD, (h + 1) * MEM_HD)
            s = _nt(q_ref[:, sl], k_ref[:, sl]) * MEM_SCALE
            p = jnp.exp(s - jnp.max(s, axis=-1, keepdims=True))
            p = p / jnp.sum(p, axis=-1, keepdims=True)
            om_ref[:, sl] = _nn(p.astype(BF16), v_ref[:, sl]).astype(BF16)
        ym = _nn(om_ref[...], _w(wo_ref))
        ym_ref[...] = ym
        x2 = x1_ref[...] + _rms(ym, gp_ref[...])
        x2_ref[...] = x2
        h3_ref[...] = _rms(x2, gf_ref[...]).astype(BF16)

    return _tok_call(body, "mem_fwd", [qm, x1], [km, vm, w_mo, g_post, g_fpre],
                     [(D, BF16), (D, F32), (D, F32), (D, BF16)], tm=TM_WIDE, vmem=VMEM_BIG)


def _tri(lower):
    r = lax.broadcasted_iota(jnp.int32, (128, 128), 0)
    c = lax.broadcasted_iota(jnp.int32, (128, 128), 1)
    return jnp.where(r >= c if lower else c >= r, 1.0, 0.0).astype(F32)


def _hdot(a, b):
    return jnp.dot(a, b, preferred_element_type=F32, precision=lax.Precision.HIGHEST)


def _gate_fwd(flog, b_pad):
    def body(f_ref, b_ref, c_ref):
        tri = _tri(True)

        def step(i, carry):
            rows = pl.ds(pl.multiple_of(i * 128, 128), 128)
            z = f_ref[rows, :] + b_ref[...]
            lf = jnp.minimum(z, 0.0) - jnp.log(1.0 + jnp.exp(-jnp.abs(z)))
            cb = _hdot(tri, lf) + carry
            c_ref[rows, :] = cb
            return cb[127:128, :]

        lax.fori_loop(0, T // 128, step, jnp.zeros((1, 128), F32))

    return _one_call(body, "gate_fwd", [flog, b_pad], [((T, 128), F32)])[0]


def _gate_bwd(dc, flog, b_pad):
    def body(dc_ref, f_ref, b_ref, df_ref, db_ref):
        tri = _tri(False)

        def step(j, carry):
            run, db = carry
            i = T // 128 - 1 - j
            rows = pl.ds(pl.multiple_of(i * 128, 128), 128)
            dcb = dc_ref[rows, :]
            rb = _hdot(tri, dcb) + run
            z = f_ref[rows, :] + b_ref[...]
            df = rb * (1.0 / (1.0 + jnp.exp(z)))
            df_ref[rows, :] = df.astype(BF16)
            return run + jnp.sum(dcb, axis=0, keepdims=True), db + jnp.sum(df, axis=0, keepdims=True)

        _, db = lax.fori_loop(0, T // 128, step, (jnp.zeros((1, 128), F32), jnp.zeros((1, 128), F32)))
        db_ref[...] = jnp.broadcast_to(db, (8, 128))

    return _one_call(body, "gate_bwd", [dc, flog, b_pad], [((T, 128), BF16), ((8, 128), F32)])


def _lane_lo(rows=TQ):
    return lax.broadcasted_iota(jnp.int32, (rows, 128), 1) < HD


def _half(v, lo, a, scale=None):
    keep = lo if a == 0 else jnp.logical_not(lo)
    v = v.astype(F32) if scale is None else v.astype(F32) * scale
    return jnp.where(keep, v, 0.0).astype(BF16)


def _fox_specs():
    return [pl.BlockSpec((FQ, 128), lambda h, i: (i, h)),
            pl.BlockSpec((T, 128), lambda h, i: (0, 4 + h)),
            pl.BlockSpec((T, 128), lambda h, i: (0, 8 + h))]


def _lane_pick(x, at):
    lane = lax.broadcasted_iota(jnp.int32, x.shape, 1)
    return jnp.sum(jnp.where(lane == at, x, 0.0), axis=-1, keepdims=True)


def _fox_fwd(proj, c, ct3):
    def body(q_ref, k_ref, v_ref, c_ref, ct_ref, o_ref, l_ref):
        i = pl.program_id(1)
        lo = _lane_lo(FQ)
        causal = lax.broadcasted_iota(jnp.int32, (FQ, FQ), 1) <= lax.broadcasted_iota(jnp.int32, (FQ, FQ), 0)
        q = q_ref[...]
        qs = [_half(q, lo, a, SCALE) for a in range(2)]
        cqs = [_lane_pick(c_ref[...], 2 * pl.program_id(0) + a) for a in range(2)]

        def tile(off, carry, diagonal):
            kblk = k_ref[pl.ds(off, FQ), :]
            vblk = v_ref[pl.ds(off, FQ), :]
            new = []
            for a in range(2):
                m, l, acc = carry[a]
                s = _nt(qs[a], kblk) + (cqs[a] - ct_ref[a:a + 1, pl.ds(off, FQ)])
                if diagonal:
                    s = jnp.where(causal, s, NEG)
                m2 = jnp.maximum(m, jnp.max(s, axis=-1, keepdims=True))
                p = jnp.exp(s - m2)
                alpha = jnp.exp(m - m2)
                new.append((m2, alpha * l + jnp.sum(p, axis=-1, keepdims=True),
                            alpha * acc + _nn(p.astype(BF16), vblk)))
            return tuple(new)

        init = (jnp.full((FQ, 1), NEG, F32), jnp.zeros((FQ, 1), F32), jnp.zeros((FQ, 128), F32))
        carry = lax.fori_loop(0, i, lambda kb, c: tile(pl.multiple_of(kb * FQ, FQ), c, False), (init, init))
        carry = tile(pl.multiple_of(i * FQ, FQ), carry, True)
        outs = []
        for a in range(2):
            m, l, acc = carry[a]
            outs.append(acc / l)
            l_ref[:, 128 * a:128 * a + 128] = jnp.broadcast_to(m + jnp.log(l), (FQ, 128))
        o_ref[...] = jnp.where(lo, outs[0], outs[1])

    return _pcall(
        body, name="fox_fwd", grid=(4, T // FQ),
        in_specs=_fox_specs() + [pl.BlockSpec((FQ, 128), lambda h, i: (i, 0)),
                                 pl.BlockSpec((None, 2, T), lambda h, i: (h, 0, 0))],
        out_specs=[pl.BlockSpec((FQ, 128), lambda h, i: (i, h)), pl.BlockSpec((FQ, 256), lambda h, i: (i, h))],
        out_shape=[jax.ShapeDtypeStruct((T, 512), F32), jax.ShapeDtypeStruct((T, 1024), F32)],
        compiler_params=pltpu.CompilerParams(dimension_semantics=("arbitrary", "arbitrary"), vmem_limit_bytes=VMEM_BIG),
    )(proj, proj, proj, c, ct3)


def _fox_bwd(proj, c, ct3, o, lse, do):
    def body(q_ref, k_ref, v_ref, c_ref, ct_ref, o_ref, l_ref, do_ref, dq_ref, dkb_ref, dvb_ref, dct_ref, dcq_ref,
             dk_ref, dv_ref):
        i = pl.program_id(1)

        @pl.when(i == 0)
        def _():
            dk_ref[...] = jnp.zeros_like(dk_ref)
            dv_ref[...] = jnp.zeros_like(dv_ref)
            dct_ref[...] = jnp.zeros_like(dct_ref)

        lo = _lane_lo(FQ)
        causal = lax.broadcasted_iota(jnp.int32, (FQ, FQ), 1) <= lax.broadcasted_iota(jnp.int32, (FQ, FQ), 0)
        q = q_ref[...]
        do_v = do_ref[...]
        prod = do_v * o_ref[...]
        qs = [_half(q, lo, a, SCALE) for a in range(2)]
        dos = [_half(do_v, lo, a) for a in range(2)]
        deltas = [jnp.sum(jnp.where(lo if a == 0 else jnp.logical_not(lo), prod, 0.0), axis=-1, keepdims=True)
                  for a in range(2)]
        cqs = [_lane_pick(c_ref[...], 2 * pl.program_id(0) + a) for a in range(2)]
        las = [l_ref[:, 128 * a:128 * a + 1] for a in range(2)]

        def tile(off, carry, diagonal):
            kblk = k_ref[pl.ds(off, FQ), :]
            vblk = v_ref[pl.ds(off, FQ), :]
            new = []
            dk = jnp.zeros((128, FQ), F32)
            dv = jnp.zeros((128, FQ), F32)
            for a in range(2):
                dq_acc, rs = carry[a]
                s = _nt(qs[a], kblk) + (cqs[a] - ct_ref[a:a + 1, pl.ds(off, FQ)])
                if diagonal:
                    s = jnp.where(causal, s, NEG)
                p = jnp.exp(s - las[a])
                ds = p * (_nt(dos[a], vblk) - deltas[a])
                dsb = ds.astype(BF16)
                dk = dk + _tn(qs[a], dsb)
                dv = dv + _tn(dos[a], p.astype(BF16))
                dct_ref[a:a + 1, pl.ds(off, FQ)] -= jnp.sum(ds, axis=0, keepdims=True)
                new.append((dq_acc + _nn(dsb, kblk), rs + jnp.sum(ds, axis=-1, keepdims=True)))
            dk_ref[:, pl.ds(off, FQ)] += dk
            dv_ref[:, pl.ds(off, FQ)] += dv
            return tuple(new)

        init = (jnp.zeros((FQ, 128), F32), jnp.zeros((FQ, 1), F32))
        carry = lax.fori_loop(0, i, lambda kb, c: tile(pl.multiple_of(kb * FQ, FQ), c, False), (init, init))
        carry = tile(pl.multiple_of(i * FQ, FQ), carry, True)
        lane = lax.broadcasted_iota(jnp.int32, (FQ, 128), 1)
        dcq_ref[...] = jnp.where(lane == 0, carry[0][1], jnp.where(lane == 1, carry[1][1], 0.0))
        dq_ref[...] = (jnp.where(lo, carry[0][0], carry[1][0]) * SCALE).astype(BF16)

        @pl.when(i == T // FQ - 1)
        def _():
            dkb_ref[...] = dk_ref[...].T.astype(BF16)
            dvb_ref[...] = dv_ref[...].T.astype(BF16)

    blk = pl.BlockSpec((FQ, 128), lambda h, i: (i, h))
    wide = pl.BlockSpec((FQ, 256), lambda h, i: (i, h))
    rows = pl.BlockSpec((None, 2, T), lambda h, i: (h, 0, 0))
    col = pl.BlockSpec((T, 128), lambda h, i: (0, h))
    return _pcall(
        body, name="fox_bwd", grid=(4, T // FQ),
        in_specs=_fox_specs() + [pl.BlockSpec((FQ, 128), lambda h, i: (i, 0)), rows, blk, wide, blk],
        out_specs=[blk, col, col, rows, pl.BlockSpec((None, FQ, 128), lambda h, i: (h, i, 0))],
        out_shape=[jax.ShapeDtypeStruct((T, 512), BF16), jax.ShapeDtypeStruct((T, 512), BF16),
                   jax.ShapeDtypeStruct((T, 512), BF16), jax.ShapeDtypeStruct((4, 2, T), F32),
                   jax.ShapeDtypeStruct((4, T, 128), F32)],
        scratch_shapes=[pltpu.VMEM((128, T), F32), pltpu.VMEM((128, T), F32)],
        compiler_params=pltpu.CompilerParams(dimension_semantics=("arbitrary", "arbitrary"), vmem_limit_bytes=VMEM_BIG),
    )(proj, proj, proj, c, ct3, o, lse, do)


def _rel_onehot():
    ridx = lax.broadcasted_iota(jnp.int32, (NREL_PAD, VW), 0)
    j = lax.broadcasted_iota(jnp.int32, (NREL_PAD, VW), 1)
    return jnp.where(ridx == jnp.clip(TQ + LEFT - 1 - j, -128, 128) + 128, 1.0, 0.0).astype(F32)


def _relvec_fwd(tbl):
    def body(t_ref, v_ref):
        v_ref[...] = _hdot(t_ref[...], _rel_onehot())

    return _one_call(body, "relvec_fwd", [tbl], [((8, VW), F32)])[0]


def _relvec_bwd(gv):
    def body(g_ref, t_ref):
        t_ref[...] = lax.dot_general(g_ref[...], _rel_onehot(), (((1,), (1,)), ((), ())),
                                     preferred_element_type=F32, precision=lax.Precision.HIGHEST)

    return _one_call(body, "relvec_bwd", [gv], [((8, NREL_PAD), F32)])[0]


def _chk_bias(vt_ref, a, hidden):
    vb = jnp.broadcast_to(vt_ref[a:a + 1, :], (TQ, VW))
    y = pltpu.roll(vb, VW - (TQ - 1), 1, stride=1, stride_axis=0)[:, :WIN]
    cr = lax.broadcasted_iota(jnp.int32, (TQ, WIN), 0) // 64
    m = lax.broadcasted_iota(jnp.int32, (TQ, WIN), 1)
    return jnp.where((m // 64 >= cr) & (m // 64 <= cr + 8) & (m >= hidden), y, NEG)


def _chk_specs():
    return [pl.BlockSpec((TQ, 128), lambda h, i: (i, CHK0 // 128 + h)),
            pl.BlockSpec((T + LEFT, 128), lambda h, i: (0, h)),
            pl.BlockSpec((T + LEFT, 128), lambda h, i: (0, 4 + h)),
            pl.BlockSpec((None, 2, VW), lambda h, i: (h, 0, 0))]


def _chk_fwd(proj, kvp, vt3):
    def body(q_ref, k_ref, v_ref, vt_ref, o_ref, l_ref, bias_ref):
        i = pl.program_id(1)

        @pl.when(i == 0)
        def _():
            for first in range(3):
                for a in range(2):
                    bias_ref[first, a] = _chk_bias(vt_ref, a, max(LEFT - first * TQ, 0))

        lo = _lane_lo()
        off = pl.multiple_of(i * TQ, TQ)
        kw = k_ref[pl.ds(off, WIN), :]
        vw = v_ref[pl.ds(off, WIN), :]
        bias_at = jnp.minimum(i, 2)
        q = q_ref[...]
        outs = []
        for a in range(2):
            s = _nt(_half(q, lo, a, SCALE), kw) + bias_ref[bias_at, a]
            m = jnp.max(s, axis=-1, keepdims=True)
            p = jnp.exp(s - m)
            l = jnp.sum(p, axis=-1, keepdims=True)
            outs.append(_nn(p.astype(BF16), vw) / l)
            l_ref[:, 128 * a:128 * a + 128] = jnp.broadcast_to(m + jnp.log(l), (TQ, 128))
        o_ref[...] = jnp.where(lo, outs[0], outs[1])

    return _pcall(
        body, name="chk_fwd", grid=(4, T // TQ), in_specs=_chk_specs(),
        out_specs=[pl.BlockSpec((TQ, 128), lambda h, i: (i, h)), pl.BlockSpec((TQ, 256), lambda h, i: (i, h))],
        out_shape=[jax.ShapeDtypeStruct((T, 512), F32), jax.ShapeDtypeStruct((T, 1024), F32)],
        scratch_shapes=[pltpu.VMEM((3, 2, TQ, WIN), F32)],
        compiler_params=pltpu.CompilerParams(dimension_semantics=("arbitrary", "arbitrary")),
    )(proj, kvp, kvp, vt3)


def _chk_bwd(proj, kvp, vt3, o, lse, do):
    nq = T // TQ

    def body(q_ref, k_ref, v_ref, vt_ref, o_ref, l_ref, do_ref, dq_ref, dkb_ref, dvb_ref, gv_ref, bias_ref, dsum_ref,
             dk_ref, dv_ref):
        i = pl.program_id(1)

        @pl.when(i == 0)
        def _():
            for first in range(3):
                for a in range(2):
                    bias_ref[first, a] = _chk_bias(vt_ref, a, max(LEFT - first * TQ, 0))
            dsum_ref[...] = jnp.zeros_like(dsum_ref)
            dk_ref[...] = jnp.zeros_like(dk_ref)
            dv_ref[...] = jnp.zeros_like(dv_ref)

        lo = _lane_lo()
        off = pl.multiple_of(i * TQ, TQ)
        kw = k_ref[pl.ds(off, WIN), :]
        vw = v_ref[pl.ds(off, WIN), :]
        bias_at = jnp.minimum(i, 2)
        q = q_ref[...]
        do_v = do_ref[...]
        prod = do_v * o_ref[...]
        dqs = []
        for a in range(2):
            keep = lo if a == 0 else jnp.logical_not(lo)
            qa = _half(q, lo, a, SCALE)
            doa = _half(do_v, lo, a)
            delta = jnp.sum(jnp.where(keep, prod, 0.0), axis=-1, keepdims=True)
            s = _nt(qa, kw) + bias_ref[bias_at, a]
            p = jnp.exp(s - l_ref[:, 128 * a:128 * a + 1])
            ds = p * (_nt(doa, vw) - delta)
            dsum_ref[a] += ds
            dsb = ds.astype(BF16)
            dk_ref[:, pl.ds(off, WIN)] += _tn(qa, dsb)
            dv_ref[:, pl.ds(off, WIN)] += _tn(doa, p.astype(BF16))
            dqs.append(_nn(dsb, kw))
        dq_ref[...] = (jnp.where(lo, dqs[0], dqs[1]) * SCALE).astype(BF16)

        @pl.when(i == nq - 1)
        def _():
            dkb_ref[...] = dk_ref[:, LEFT:].T.astype(BF16)
            dvb_ref[...] = dv_ref[:, LEFT:].T.astype(BF16)
            rr = lax.broadcasted_iota(jnp.int32, (TQ, TQ), 0)
            cc = lax.broadcasted_iota(jnp.int32, (TQ, TQ), 1)
            flip = jnp.where(rr + cc == TQ - 1, 1.0, 0.0).astype(F32)
            for a in range(2):
                dpad = jnp.concatenate([dsum_ref[a], jnp.zeros((TQ, VW - WIN), F32)], axis=1)
                z = pltpu.roll(_hdot(flip, dpad), 0, 1, stride=1, stride_axis=0)
                gv_ref[a:a + 1, :] = jnp.sum(z, axis=0, keepdims=True)

    blk = pl.BlockSpec((TQ, 128), lambda h, i: (i, h))
    wide = pl.BlockSpec((TQ, 256), lambda h, i: (i, h))
    col = pl.BlockSpec((T, 128), lambda h, i: (0, h))
    return _pcall(
        body, name="chk_bwd", grid=(4, nq), in_specs=_chk_specs() + [blk, wide, blk],
        out_specs=[blk, col, col, pl.BlockSpec((None, 2, VW), lambda h, i: (h, 0, 0))],
        out_shape=[jax.ShapeDtypeStruct((T, 512), BF16), jax.ShapeDtypeStruct((T, 512), BF16),
                   jax.ShapeDtypeStruct((T, 512), BF16), jax.ShapeDtypeStruct((4, 2, VW), F32)],
        scratch_shapes=[pltpu.VMEM((3, 2, TQ, WIN), F32), pltpu.VMEM((2, TQ, WIN), F32),
                        pltpu.VMEM((128, T + LEFT), F32), pltpu.VMEM((128, T + LEFT), F32)],
        compiler_params=pltpu.CompilerParams(dimension_semantics=("arbitrary", "arbitrary")),
    )(proj, kvp, kvp, vt3, o, lse, do)


def _zero_at_start(*refs):
    @pl.when(pl.program_id(0) == 0)
    def _():
        for r in refs:
            r[...] = jnp.zeros_like(r)


def _ffn_step(h3, x2, tgt, w1_t, w2, g_post, g_pre):
    def body(h_ref, x2_ref, t_ref, w1_ref, w2_ref, gp_ref, gf_ref, dx2_ref, da_ref, dy_ref, r_ref, loss_ref, dgp_ref, dgf_ref):
        _zero_at_start(loss_ref, dgp_ref, dgf_ref)
        w1, w2v = _w(w1_ref), _w(w2_ref)
        ra = jnp.maximum(_nt(h_ref[...], w1), 0.0)
        r = jnp.square(ra).astype(BF16)
        r_ref[...] = r
        y = _nn(r, w2v)
        x2v = x2_ref[...]
        e = x2v + _rms(y, gp_ref[...]) - t_ref[...]
        loss_ref[...] += 0.5 * jnp.sum(jnp.sum(e * e, axis=-1, keepdims=True) * (1.0 / D))
        dx3 = e * (1.0 / D)
        dy, dgp = _rms_bwd(y, gp_ref[...], dx3)
        dgp_ref[...] += dgp
        dyb = dy.astype(BF16)
        dy_ref[...] = dyb
        da = (_nt(dyb, w2v) * (2.0 * ra)).astype(BF16)
        da_ref[...] = da
        dh, dgf = _rms_bwd(x2v, gf_ref[...], _nn(da, w1))
        dgf_ref[...] += dgf
        dx2_ref[...] = dx3 + dh

    return _tok_call(body, "ffn_step", [h3, x2, tgt], [w1_t, w2, g_post, g_pre],
                     [(D, F32), (DFF, BF16), (D, BF16), (DFF, BF16)], [(8, 128), (1, D), (1, D)], vmem=VMEM_BIG)


def _mem_bwd(dx2, ym, x1, qm, km, vm, w_mo, w_mq, g_post, g_pre):
    def body(dx2_ref, ym_ref, x1_ref, q_ref, k_ref, v_ref, wo_ref, wq_ref, gp_ref, gm_ref,
             dx1_ref, dym_ref, dq_ref, dk_ref, dv_ref, dgp_ref, dgm_ref, dom_ref):
        _zero_at_start(dk_ref, dv_ref, dgp_ref, dgm_ref)
        dx2_v = dx2_ref[...]
        dym, dgp = _rms_bwd(ym_ref[...], gp_ref[...], dx2_v)
        dgp_ref[...] += dgp
        dymb = dym.astype(BF16)
        dym_ref[...] = dymb
        dom_ref[...] = _nt(dymb, _w(wo_ref)).astype(BF16)
        for h in range(MEM_HEADS):
            sl = slice(h * MEM_HD, (h + 1) * MEM_HD)
            qh, kh, doh = q_ref[:, sl], k_ref[:, sl], dom_ref[:, sl]
            s = _nt(qh, kh) * MEM_SCALE
            p = jnp.exp(s - jnp.max(s, axis=-1, keepdims=True))
            p = p / jnp.sum(p, axis=-1, keepdims=True)
            dp = _nt(doh, v_ref[:, sl])
            ds = (p * (dp - jnp.sum(p * dp, axis=-1, keepdims=True))).astype(BF16)
            dq_ref[:, sl] = (_nn(ds, kh) * MEM_SCALE).astype(BF16)
            dk_ref[:, sl] += _tn(ds, qh) * MEM_SCALE
            dv_ref[:, sl] += _tn(p.astype(BF16), doh)
        dh, dgm = _rms_bwd(x1_ref[...], gm_ref[...], _nt(dq_ref[...], _w(wq_ref)))
        dgm_ref[...] += dgm
        dx1_ref[...] = dx2_v + dh

    tiled = pl.BlockSpec((TM_WIDE, D), lambda i: (i, 0))
    in_specs = [tiled] * 4 + [_resident(a) for a in (km, vm, w_mo, w_mq, g_post, g_pre)]
    w_mo, w_mq = w_mo[0], w_mq[0]
    kv = pl.BlockSpec((NMEM, D), lambda i: (0, 0))
    vec = pl.BlockSpec((1, D), lambda i: (0, 0))
    return _pcall(
        body, name="mem_bwd", grid=(T // TM_WIDE,), in_specs=in_specs,
        out_specs=[tiled, tiled, tiled, kv, kv, vec, vec],
        out_shape=[jax.ShapeDtypeStruct((T, D), F32), jax.ShapeDtypeStruct((T, D), BF16),
                   jax.ShapeDtypeStruct((T, D), BF16), jax.ShapeDtypeStruct((NMEM, D), F32),
                   jax.ShapeDtypeStruct((NMEM, D), F32), jax.ShapeDtypeStruct((1, D), F32),
                   jax.ShapeDtypeStruct((1, D), F32)],
        scratch_shapes=[pltpu.VMEM((TM_WIDE, D), BF16)],
        compiler_params=pltpu.CompilerParams(dimension_semantics=("arbitrary",), vmem_limit_bytes=VMEM_BIG),
    )(dx2, ym, x1, qm, km, vm, w_mo, w_mq, g_post, g_pre)


def _memkv_bwd(dkm, dvm, mem, w_mk, w_mv):
    def body(dk_ref, dv_ref, m_ref, wk_ref, wv_ref, dg_ref):
        dmn = _nt(dk_ref[...].astype(BF16), _w(wk_ref)) + _nt(dv_ref[...].astype(BF16), _w(wv_ref))
        mv = m_ref[...]
        dg_ref[...] = jnp.sum(dmn * (mv * _rstd(mv)), axis=0, keepdims=True)

    return _one_call(body, "memkv_bwd", [dkm, dvm, mem, w_mk, w_mv], [((1, D), F32)], vmem=VMEM_BIG)[0]


def _postmix_bwd(dx1, z, o_f, o_c, w_out, g_post, g_fo, g_co):
    def body(dx1_ref, z_ref, of_ref, oc_ref, wo_ref, gp_ref, gfo_ref, gco_ref,
             dz_ref, dof_ref, doc_ref, dgp_ref, dgfo_ref, dgco_ref):
        _zero_at_start(dgp_ref, dgfo_ref, dgco_ref)
        dz, dgp = _rms_bwd(z_ref[...], gp_ref[...], dx1_ref[...])
        dgp_ref[...] += dgp
        dzb = dz.astype(BF16)
        dz_ref[...] = dzb
        dy = _nt(dzb, _w(wo_ref))
        dof, dgfo = _rms_bwd(of_ref[...], gfo_ref[...], dy[:, :512])
        doc, dgco = _rms_bwd(oc_ref[...], gco_ref[...], dy[:, 512:])
        dof_ref[...] = dof
        doc_ref[...] = doc
        dgfo_ref[...] += dgfo
        dgco_ref[...] += dgco

    return _tok_call(body, "postmix_bwd", [dx1, z, o_f, o_c], [w_out, g_post, g_fo, g_co],
                     [(D, BF16), (512, F32), (512, F32)], [(1, D), (1, 512), (1, 512)], tm=TM_WIDE, vmem=VMEM_BIG)


def _premix_bwd(dx1, x, pieces, win_t, g_pre):
    def body(dx1_ref, x_ref, *refs):
        piece_refs, (w_ref, g_ref, dx_ref, dp_ref, dg_ref) = refs[:len(pieces)], refs[len(pieces):]
        _zero_at_start(dg_ref)
        col = 0
        for p in piece_refs:
            dp_ref[:, col:col + p.shape[1]] = p[...]
            col += p.shape[1]
        dh, dg = _rms_bwd(x_ref[...], g_ref[...], _nn(dp_ref[...], w_ref[...]))
        dg_ref[...] += dg
        dx_ref[...] = dx1_ref[...] + dh

    return _tok_call(body, "premix_bwd", [dx1, x] + list(pieces), [win_t, g_pre], [(D, F32), (PROJ, BF16)], [(1, D)],
                     tm=TM_WIDE, vmem=VMEM_BIG)


def _wgrad(a, b, name):
    k, m = a.shape
    n = b.shape[1]
    tm = 640 if m % 640 == 0 and m > 1024 else min(m, 512)
    tn = min(n, 1024)

    def body(a_ref, b_ref, o_ref):
        o_ref[...] = _tn(a_ref[...].astype(BF16), b_ref[...].astype(BF16))

    return _pcall(
        body, name=name, grid=(m // tm, n // tn),
        in_specs=[pl.BlockSpec((k, tm), lambda i, j: (0, i)), pl.BlockSpec((k, tn), lambda i, j: (0, j))],
        out_specs=pl.BlockSpec((tm, tn), lambda i, j: (i, j)),
        out_shape=jax.ShapeDtypeStruct((m, n), F32),
        compiler_params=pltpu.CompilerParams(dimension_semantics=("arbitrary", "arbitrary"), vmem_limit_bytes=VMEM_BIG),
    )(a, b)


def _wgrad_group(name, pairs, rows):
    def body(*refs):
        o_ref = refs[-1]
        for k in range(len(pairs)):
            o_ref[k * rows:(k + 1) * rows, :] = _tn(refs[2 * k][...].astype(BF16), refs[2 * k + 1][...].astype(BF16))

    in_specs, ops = [], []
    for a, b in pairs:
        in_specs += [pl.BlockSpec((a.shape[0], rows), lambda j: (0, j)), _resident(b)]
        ops += [a, b]
    return _pcall(
        body, name=name, grid=(8,), in_specs=in_specs,
        out_specs=pl.BlockSpec((None, len(pairs) * rows, D), lambda j: (j, 0, 0)),
        out_shape=jax.ShapeDtypeStruct((8, len(pairs) * rows, D), F32),
        compiler_params=pltpu.CompilerParams(dimension_semantics=("arbitrary",), vmem_limit_bytes=VMEM_BIG),
    )(*ops)


def _adam_math(w, g, m, v):
    m2 = ADAM_B1 * m + (1.0 - ADAM_B1) * g
    v2 = ADAM_B2 * v + (1.0 - ADAM_B2) * jnp.square(g)
    m_hat = m2 / (1.0 - ADAM_B1 ** ADAM_STEP)
    v_hat = v2 / (1.0 - ADAM_B2 ** ADAM_STEP)
    delta = -ADAM_LR * (m_hat / (jnp.sqrt(v_hat) + ADAM_EPS) + ADAM_WD * w)
    return delta, m2, v2


def _adamw_small(gparts, ws, ms, vs):
    n = len(SMALL)

    def body(g_ref, *refs):
        w_refs, m_refs, v_refs = refs[:n], refs[n:2 * n], refs[2 * n:3 * n]
        outs, sum_ref = refs[3 * n:-1], refs[-1]
        g = g_ref[0]
        for k in range(1, 8):
            g = g + g_ref[k]
        sum_ref[...] = g
        outs[0][...] = sum_ref[17:18, 0:128]
        for t, name in enumerate(SMALL):
            r0, nr, c0, nc = SMALL_SLOT[name]
            gt = sum_ref[r0:r0 + nr, c0:c0 + nc]
            out = (gt,) + _adam_math(w_refs[t][...], gt, m_refs[t][...], v_refs[t][...])
            for o_ref, val in zip(outs[1 + 4 * t:5 + 4 * t], out):
                o_ref[...] = val

    whole = lambda s: pl.BlockSpec(s, lambda i, nd=len(s): (0,) * nd)
    ins = [gparts] + list(ws) + list(ms) + list(vs)
    out_shapes = [(1, 128)] + [a.shape for a in ws for _ in range(4)]
    return _pcall(
        body, name="adamw_small", grid=(1,), in_specs=[whole(a.shape) for a in ins],
        out_specs=[whole(s) for s in out_shapes], out_shape=[jax.ShapeDtypeStruct(s, F32) for s in out_shapes],
        scratch_shapes=[pltpu.VMEM((SMALL_ROWS, D), F32)],
        compiler_params=pltpu.CompilerParams(dimension_semantics=("arbitrary",)),
    )(*ins)


def _row_tile(rows):
    return next(t for t in (512, 400, 320) if rows % t == 0)


def _add_halves(g4, theirs, core, name):
    rows = g4.shape[2]
    tr = _row_tile(rows)

    def body(c_ref, a_ref, b_ref, o_ref):
        o_ref[...] = (a_ref[...] + b_ref[...]).astype(BF16)

    grid_spec = pltpu.PrefetchScalarGridSpec(
        num_scalar_prefetch=1, grid=(4, rows // tr),
        in_specs=[pl.BlockSpec((None, None, tr, D), lambda j, i, c: (j, c[0], i, 0)),
                  pl.BlockSpec((None, None, tr, D), lambda j, i, c: (j, 0, i, 0))],
        out_specs=pl.BlockSpec((None, tr, D), lambda j, i, c: (j, i, 0)))
    return _pcall(
        body, name=name, grid_spec=grid_spec, out_shape=jax.ShapeDtypeStruct((4, rows, D), BF16),
        compiler_params=pltpu.CompilerParams(dimension_semantics=("arbitrary", "arbitrary")),
    )(core, g4, theirs)


def _sum_adam(own, got, order, r0, w, m, v, name, transposed=False):
    n = w.shape[1] if transposed else w.shape[0]
    tr = min(n, 256) if n % 8 == 0 else n
    rows = tr if n % 8 == 0 else own.shape[1]

    def body(o_ref, a_ref, b_ref, c_ref, d_ref, w_ref, m_ref, v_ref, g_ref, dl_ref, m2_ref, v2_ref):
        f = lambda r: r[0:tr, :].astype(F32)
        g = ((f(a_ref) + f(b_ref)) + f(c_ref)) + f(d_ref)
        g = g.T if transposed else g
        g_ref[...] = g
        dl_ref[...], m2_ref[...], v2_ref[...] = _adam_math(w_ref[...], g, m_ref[...], v_ref[...])

    slot = lambda k: pl.BlockSpec((None, rows, D), lambda i, o: (o[k], r0 // rows + i, 0))
    wspec = pl.BlockSpec((D, tr), lambda i, o: (0, i)) if transposed else pl.BlockSpec((tr, D), lambda i, o: (i, 0))
    grid_spec = pltpu.PrefetchScalarGridSpec(
        num_scalar_prefetch=1, grid=(n // tr,), in_specs=[slot(0), slot(1), slot(2), slot(3), wspec, wspec, wspec],
        out_specs=[wspec] * 4)
    return _pcall(
        body, name=name, grid_spec=grid_spec, out_shape=[jax.ShapeDtypeStruct(w.shape, F32)] * 4,
        compiler_params=pltpu.CompilerParams(dimension_semantics=("arbitrary",)),
    )(order, own, got, got, got, w, m, v)


def _sum_adam_rows(own, got, order, ws, ms, vs, name):
    n, rows = len(ws), ws[0].shape[0]

    def body(o_ref, a_ref, b_ref, c_ref, d_ref, *refs):
        ins, outs = refs[:3 * n], refs[3 * n:]
        for t in range(n):
            r = slice(t * rows, (t + 1) * rows)
            f = lambda ref: ref[r, :].astype(F32)
            g = ((f(a_ref) + f(b_ref)) + f(c_ref)) + f(d_ref)
            out = (g,) + _adam_math(ins[t][...], g, ins[n + t][...], ins[2 * n + t][...])
            for o, val in zip(outs[4 * t:4 * t + 4], out):
                o[...] = val

    slot = lambda k: pl.BlockSpec((None, n * rows, D), lambda i, o: (o[k], 0, 0), pipeline_mode=pl.Buffered(1))
    wspec = pl.BlockSpec((rows, D), lambda i, o: (0, 0), pipeline_mode=pl.Buffered(1))
    grid_spec = pltpu.PrefetchScalarGridSpec(
        num_scalar_prefetch=1, grid=(1,), in_specs=[slot(0), slot(1), slot(2), slot(3)] + [wspec] * (3 * n),
        out_specs=[pl.BlockSpec((rows, D), lambda i, o: (0, 0))] * (4 * n))
    return _pcall(
        body, name=name, grid_spec=grid_spec, out_shape=[jax.ShapeDtypeStruct((rows, D), F32)] * (4 * n),
        compiler_params=pltpu.CompilerParams(dimension_semantics=("arbitrary",), vmem_limit_bytes=VMEM_BIG),
    )(order, own, got, got, got, *ws, *ms, *vs)


def _place():
    return lax.axis_index("x"), lax.axis_index("y"), lax.axis_index("c")


def _allgather(block, name):
    rows = block.shape[0]
    split = (rows // 2 + 15) // 16 * 16

    def body(x_ref, out_ref, token, send_sems, recv_sems, local_sem):
        token[...] = jnp.zeros_like(token)
        x, y, c = _place()
        me, sib = (x, y, c), (x, y, 1 - c)
        xn, yn, dg = (1 - x, y), (x, 1 - y), (1 - x, 1 - y)
        lo, hi = pl.ds(0, split), pl.ds(split, rows - split)

        def copy(k, blk, to, part=None, src=None):
            index = 4 * blk[0] + 2 * blk[1] + blk[2]
            view = out_ref.at[index] if part is None else out_ref.at[index, part]
            return pltpu.make_async_remote_copy(
                src_ref=view if src is None else src, dst_ref=view,
                send_sem=send_sems.at[k], recv_sem=recv_sems.at[k], device_id=to, device_id_type=MESH)

        def start(*copies):
            for cp in copies:
                cp.start()
            return list(copies)

        mine = pltpu.make_async_copy(x_ref, out_ref.at[4 * x + 2 * y + c], local_sem)
        mine.start()
        sent = start(copy(0, me, sib, src=x_ref), copy(1, me, (*xn, c), src=x_ref), copy(2, me, (*yn, c), src=x_ref))
        copy(1, (*xn, c), me).wait_recv()
        sent += start(copy(3, (*xn, c), sib), copy(5, (*xn, c), (*yn, c), part=lo))
        copy(2, (*yn, c), me).wait_recv()
        sent += start(copy(4, (*yn, c), sib), copy(6, (*yn, c), (*xn, c), part=hi))
        copy(5, (*dg, c), me, part=lo).wait_recv()
        copy(6, (*dg, c), me, part=hi).wait_recv()
        sent += start(copy(7, (*dg, c), sib))
        for k, blk in ((0, sib), (3, (*xn, 1 - c)), (4, (*yn, 1 - c)), (7, (*dg, 1 - c))):
            copy(k, blk, me).wait_recv()
        for cp in sent:
            cp.wait_send()
        mine.wait()

    return _pcall(
        body, name=name,
        out_shape=[jax.ShapeDtypeStruct((8,) + block.shape, block.dtype), jax.ShapeDtypeStruct((8, 128), F32)],
        in_specs=[pl.BlockSpec(memory_space=pl.ANY)],
        out_specs=[pl.BlockSpec(memory_space=pl.ANY), pl.BlockSpec(memory_space=pltpu.VMEM)],
        scratch_shapes=[pltpu.SemaphoreType.DMA((8,)), pltpu.SemaphoreType.DMA((8,)), pltpu.SemaphoreType.DMA(())],
        compiler_params=pltpu.CompilerParams(has_side_effects=True),
    )(block)


HBM_SPEC = pl.BlockSpec(memory_space=pltpu.HBM)
SEM_SPEC = pl.BlockSpec(memory_space=pltpu.SEMAPHORE)
ANY_SPEC = pl.BlockSpec(memory_space=pl.ANY)
EFFECT = pltpu.SideEffectType.DATAFLOW_SIDE_EFFECTING


def _in_hbm(a):
    return pltpu.with_memory_space_constraint(a, pltpu.HBM)


def _start_copies(name, src, land_shape, plan, n):
    def body(src_ref, land_ref, send_sems, recv_sems, src_thru, land_thru, token):
        for k, (s, d, to, _) in enumerate(plan(src_ref, land_ref)):
            pltpu.make_async_remote_copy(src_ref=s, dst_ref=d, send_sem=send_sems.at[k], recv_sem=recv_sems.at[k],
                                         device_id=to, device_id_type=MESH).start()
        token[...] = jnp.zeros_like(token)

    return _pcall(
        body, name=name,
        out_shape=(pltpu.SemaphoreType.DMA((n,)), pltpu.SemaphoreType.DMA((n,)), pltpu.HBM(src.shape, src.dtype),
                   pltpu.HBM(land_shape, src.dtype), jax.ShapeDtypeStruct((8, 128), F32)),
        in_specs=(HBM_SPEC, HBM_SPEC),
        out_specs=(SEM_SPEC, SEM_SPEC, HBM_SPEC, HBM_SPEC, pl.BlockSpec(memory_space=pltpu.VMEM)),
        input_output_aliases={0: 2, 1: 3}, compiler_params=pltpu.CompilerParams(has_side_effects=EFFECT),
    )(_in_hbm(src), _in_hbm(lax.empty(land_shape, src.dtype)))


def _wait_copies(name, started, after, plan):
    send_sems, recv_sems, src_thru, land_thru, _ = started

    def body(src_ref, land_ref, send_sems, recv_sems, *rest):
        for k, (s, _, to, mine) in enumerate(plan(src_ref, land_ref)):
            cp = pltpu.make_async_remote_copy(src_ref=s, dst_ref=mine, send_sem=send_sems.at[k],
                                              recv_sem=recv_sems.at[k], device_id=to, device_id_type=MESH)
            cp.wait_send()
            cp.wait_recv()

    return _pcall(
        body, name=name,
        out_shape=(pltpu.HBM(src_thru.shape, src_thru.dtype), pltpu.HBM(land_thru.shape, land_thru.dtype)),
        in_specs=(HBM_SPEC, HBM_SPEC, SEM_SPEC, SEM_SPEC) + (ANY_SPEC,) * len(after), out_specs=(HBM_SPEC, HBM_SPEC),
        input_output_aliases={0: 0, 1: 1}, compiler_params=pltpu.CompilerParams(has_side_effects=EFFECT),
    )(src_thru, land_thru, send_sems, recv_sems, *after)


def _start_inplace(name, buf, plan, n):
    def body(buf_ref, send_sems, recv_sems, buf_thru, token):
        for k, (s, d, to, _) in enumerate(plan(buf_ref, buf_ref)):
            pltpu.make_async_remote_copy(src_ref=s, dst_ref=d, send_sem=send_sems.at[k], recv_sem=recv_sems.at[k],
                                         device_id=to, device_id_type=MESH).start()
        token[...] = jnp.zeros_like(token)

    return _pcall(
        body, name=name,
        out_shape=(pltpu.SemaphoreType.DMA((n,)), pltpu.SemaphoreType.DMA((n,)), pltpu.HBM(buf.shape, buf.dtype),
                   jax.ShapeDtypeStruct((8, 128), F32)),
        in_specs=(HBM_SPEC,), out_specs=(SEM_SPEC, SEM_SPEC, HBM_SPEC, pl.BlockSpec(memory_space=pltpu.VMEM)),
        input_output_aliases={0: 2}, compiler_params=pltpu.CompilerParams(has_side_effects=EFFECT),
    )(_in_hbm(buf))


def _wait_inplace(name, started, after, plan):
    send_sems, recv_sems, buf_thru, _ = started

    def body(buf_ref, send_sems, recv_sems, *rest):
        for k, (s, _, to, mine) in enumerate(plan(buf_ref, buf_ref)):
            cp = pltpu.make_async_remote_copy(src_ref=s, dst_ref=mine, send_sem=send_sems.at[k],
                                              recv_sem=recv_sems.at[k], device_id=to, device_id_type=MESH)
            cp.wait_send()
            cp.wait_recv()

    return _pcall(
        body, name=name, out_shape=pltpu.HBM(buf_thru.shape, buf_thru.dtype),
        in_specs=(HBM_SPEC, SEM_SPEC, SEM_SPEC) + (ANY_SPEC,) * len(after), out_specs=HBM_SPEC,
        input_output_aliases={0: 0}, compiler_params=pltpu.CompilerParams(has_side_effects=EFFECT),
    )(buf_thru, send_sems, recv_sems, *after)


def _gather_plan(src_ref, land_ref):
    x, y, c = _place()
    peers = [(x, y, 1 - c), (1 - x, y, c), (x, 1 - y, c)]
    return [(src_ref, land_ref.at[4 * x + 2 * y + c], p, land_ref.at[4 * p[0] + 2 * p[1] + p[2]]) for p in peers]


def _relay_plan(buf_ref, _):
    x, y, c = _place()
    slot = lambda p, pc: 4 * p[0] + 2 * p[1] + pc
    xn, yn, dg, sib = (1 - x, y), (x, 1 - y), (1 - x, 1 - y), (x, y, 1 - c)
    half = buf_ref.shape[1] // 2
    lo, hi = pl.ds(0, half), pl.ds(half, half)
    return [(buf_ref.at[slot(xn, c)], buf_ref.at[slot(xn, c)], sib, buf_ref.at[slot(xn, 1 - c)]),
            (buf_ref.at[slot(yn, c)], buf_ref.at[slot(yn, c)], sib, buf_ref.at[slot(yn, 1 - c)]),
            (buf_ref.at[slot(xn, c), lo], buf_ref.at[slot(xn, c), lo], (*yn, c), buf_ref.at[slot(dg, c), lo]),
            (buf_ref.at[slot(yn, c), hi], buf_ref.at[slot(yn, c), hi], (*xn, c), buf_ref.at[slot(dg, c), hi])]


def _swap_plan(src_ref, land_ref):
    x, y, c = _place()
    return [(src_ref.at[:, pl.ds(1 - c, 1)], land_ref, (x, y, 1 - c), land_ref)]


def _exchange_plan(src_ref, land_ref):
    x, y, c = _place()
    chips = [(1 - x, y), (x, 1 - y), (1 - x, 1 - y)]
    return [(src_ref.at[2 * px + py], land_ref.at[2 * x + y], (px, py, c), land_ref.at[2 * px + py]) for px, py in chips]


def _gather_forward(land, block):
    def body(land_ref, out_ref, send_sems, recv_sems):
        x, y, c = _place()
        chips = [(1 - x, 1 - y)]

        def copy(k, px, py, pc):
            blk = out_ref.at[4 * px + 2 * py + pc]
            return pltpu.make_async_remote_copy(src_ref=blk, dst_ref=blk, send_sem=send_sems.at[k],
                                                recv_sem=recv_sems.at[k], device_id=(x, y, 1 - c), device_id_type=MESH)

        sent = [copy(k, px, py, c) for k, (px, py) in enumerate(chips)]
        for cp in sent:
            cp.start()
        for k, (px, py) in enumerate(chips):
            copy(k, px, py, 1 - c).wait_recv()
        for cp in sent:
            cp.wait_send()

    land = _pcall(
        body, name="allgather_rest_forward", out_shape=jax.ShapeDtypeStruct(land.shape, land.dtype),
        in_specs=[ANY_SPEC], out_specs=ANY_SPEC, input_output_aliases={0: 0},
        scratch_shapes=[pltpu.SemaphoreType.DMA((1,)), pltpu.SemaphoreType.DMA((1,))],
        compiler_params=pltpu.CompilerParams(has_side_effects=True),
    )(land)

    rows = block.shape[0]
    tr = rows // 4

    def place(me_ref, x_ref, land_ref, out_ref):
        out_ref[...] = x_ref[...]

    x, y, c = _place()
    grid_spec = pltpu.PrefetchScalarGridSpec(
        num_scalar_prefetch=1, grid=(rows // tr,),
        in_specs=[pl.BlockSpec((tr, D), lambda i, me: (i, 0)), ANY_SPEC],
        out_specs=pl.BlockSpec((None, tr, D), lambda i, me: (me[0], i, 0)))
    return _pcall(
        place, name="allgather_rest_own", grid_spec=grid_spec, out_shape=jax.ShapeDtypeStruct(land.shape, land.dtype),
        input_output_aliases={2: 0}, compiler_params=pltpu.CompilerParams(dimension_semantics=("arbitrary",)),
    )((4 * x + 2 * y + c).reshape(1), block, land)


class _ReduceScatter:
    def __init__(self, name, g):
        self.name = name
        rows = g.shape[1]
        self.started = _start_copies(name + "_swap_start", g.reshape(4, 2, rows, D), (4, 1, rows, D), _swap_plan, 1)
        self.token = self.started[4][0, 0]

    def halfway(self, after):
        g4, theirs = _wait_copies(self.name + "_swap_wait", self.started, after, _swap_plan)
        self.own = _add_halves(g4, theirs, lax.axis_index("c").reshape(1), self.name + "_add_halves")
        self.started = _start_copies(self.name + "_exch_start", self.own, self.own.shape, _exchange_plan, 3)
        self.token = self.started[4][0, 0]

    def finish(self, after):
        own, got = _wait_copies(self.name + "_exch_wait", self.started, after, _exchange_plan)
        chip = 2 * lax.axis_index("x") + lax.axis_index("y")
        return own, got, (chip + jnp.arange(4, dtype=jnp.int32)) % 4


def _pack_small(p, scalar=None):
    z = lambda a, n: jnp.pad(a, ((0, 0), (0, n - a.shape[1])))
    rows = [z(p['rel_bias'], D), z(p['b_fgt'], D), jnp.concatenate([p['g_fox_out'], p['g_chk_out']], axis=1)]
    rows += [p[n] for n in ('g_mix_pre', 'g_mix_post', 'g_mem_kv', 'g_mem_pre', 'g_mem_post', 'g_ff_pre', 'g_ff_post')]
    rows.append(jnp.zeros((1, D), F32) if scalar is None else z(jnp.reshape(scalar, (1, 1)), D))
    rows.append(jnp.zeros((SMALL_ROWS - 18, D), F32))
    return jnp.concatenate(rows, axis=0)


_GAP_DEV, _GAP_ROW = divmod(GATE0 + 8, N_IN)
_GAP = CHK0 - GATE0 - 8


def _in_rows_to_proj(g):
    runs = [(j, 0, N_IN, N_IN * j) for j in range(_GAP_DEV)]
    runs += [(_GAP_DEV, 0, _GAP_ROW, N_IN * _GAP_DEV), (_GAP_DEV, _GAP_ROW, N_IN, N_IN * _GAP_DEV + _GAP_ROW + _GAP)]
    runs += [(j, 0, N_IN, N_IN * j + _GAP) for j in range(_GAP_DEV + 1, 8)]

    def body(g_ref, o_ref, acc_ref):
        acc_ref[...] = jnp.zeros_like(acc_ref)
        for j, r0, r1, dest in runs:
            start, shift = dest // 16 * 16, dest % 16
            win = -(-(shift + r1 - r0) // 16) * 16
            r = lax.broadcasted_iota(jnp.int32, (win, R_IN), 0)
            c = lax.broadcasted_iota(jnp.int32, (win, R_IN), 1)
            move = jnp.where((c >= r0) & (c < r1) & (r == c - r0 + shift), 1.0, 0.0).astype(BF16)
            acc_ref[start:start + win, :] += _nn(move, g_ref[j])
        o_ref[...] = acc_ref[...].astype(BF16)

    return _pcall(
        body, name="w_in_layout", out_shape=jax.ShapeDtypeStruct((PROJ, D), BF16), grid=(1,),
        in_specs=[pl.BlockSpec(g.shape, lambda i: (0, 0, 0))], out_specs=pl.BlockSpec((PROJ, D), lambda i: (0, 0)),
        scratch_shapes=[pltpu.VMEM((PROJ, D), F32)],
        compiler_params=pltpu.CompilerParams(dimension_semantics=("arbitrary",), vmem_limit_bytes=VMEM_BIG),
    )(g)


def _proj_rows_to_in(g):
    pad = lambda a: jnp.pad(a, ((0, R_IN - a.shape[0]), (0, 0)))
    lo = N_IN * _GAP_DEV
    shards = [pad(g[N_IN * j:N_IN * (j + 1)]) for j in range(_GAP_DEV)]
    shards.append(pad(jnp.concatenate([g[lo:lo + _GAP_ROW], g[lo + _GAP_ROW + _GAP:lo + N_IN + _GAP]], axis=0)))
    shards += [pad(g[N_IN * j + _GAP:N_IN * (j + 1) + _GAP]) for j in range(_GAP_DEV + 1, 8)]
    return jnp.stack(shards)


def _local_grads(x, mem, tgt, win_t, gw_of, sm, on_grads):
    b_pad = jnp.pad(sm['b_fgt'], ((0, 0), (0, 120)))
    tbl = jnp.pad(sm['rel_bias'], ((0, 0), (0, NREL_PAD - 257)))

    h1, proj, flog = _premix_fwd(x, sm['g_mix_pre'], win_t)
    c = _gate_fwd(flog, b_pad)
    ct3 = c[:, :8].T.reshape(4, 2, T)
    o_f, lse_f = _fox_fwd(proj, c, ct3)
    vt3 = _relvec_fwd(tbl).reshape(4, 2, VW)
    kvp = jnp.pad(proj[:, CHK0 + 512:], ((LEFT, 0), (0, 0)))
    o_c, lse_c = _chk_fwd(proj, kvp, vt3 + gw_of('relay', [o_f]))
    gw = gw_of('done', [o_c])
    w_out, w_mq, w_mk, w_mv, w_mo, w1_t, w2 = (_wblk(gw, n) for n in ('w_out', 'w_mq', 'w_mk', 'w_mv', 'w_mo', 'w_ff1', 'w_ff2'))
    ycat, z, x1, h2, qm = _postmix_fwd(x, o_f, o_c, sm['g_fox_out'], sm['g_chk_out'], w_out,
                                       sm['g_mix_post'], sm['g_mem_pre'], w_mq)
    memn, km, vm = _memkv_fwd(mem, sm['g_mem_kv'], w_mk, w_mv)
    om, ym, x2, h3 = _mem_fwd(qm, x1, km, vm, w_mo, sm['g_mem_post'], sm['g_ff_pre'])

    gs = {}
    dx2, da, dy3, r, loss_acc, gs['g_ff_post'], gs['g_ff_pre'] = _ffn_step(h3, x2, tgt, w1_t, w2, sm['g_ff_post'],
                                                                         sm['g_ff_pre'])
    zero = on_grads('A', _wgrad_group("wgrad_ff", [(da, h3), (r, dy3)], 512), None)
    dx1, dym, dqm, dkm, dvm, gs['g_mem_post'], gs['g_mem_pre'] = _mem_bwd(
        dx2, ym, x1, qm, km, vm, w_mo, w_mq, sm['g_mem_post'] + zero, sm['g_mem_pre'])
    zero = on_grads('A halfway', None, [dx1])
    gs['g_mem_kv'] = _memkv_bwd(dkm, dvm, mem, w_mk, w_mv)
    dz, dof, doc, gs['g_mix_post'], gs['g_fox_out'], gs['g_chk_out'] = _postmix_bwd(
        dx1, z, o_f, o_c, w_out, sm['g_mix_post'] + zero, sm['g_fox_out'], sm['g_chk_out'])
    zero = on_grads('B', _wgrad_group("wgrad_mem_out", [(ycat, dz), (h2, dqm), (memn, dkm), (memn, dvm), (om, dym)], 128), None)
    dq_f, dk_f, dv_f, dct, dcq = _fox_bwd(proj, c, ct3 + zero, o_f, lse_f, dof)
    zero = on_grads('B halfway', None, [dq_f])
    dq_c, dk_c, dv_c, gv = _chk_bwd(proj, kvp, vt3 + zero, o_c, lse_c, doc)
    gs['rel_bias'] = _relvec_bwd(gv.reshape(8, VW))[:, :257]
    dc = jnp.pad(dct.reshape(8, T).T + dcq[:, :, :2].transpose(1, 0, 2).reshape(T, 8), ((0, 0), (0, 120)))
    dflog, db = _gate_bwd(dc, flog, b_pad)
    gs['b_fgt'] = db[0:1, :8]
    grad_x, dproj, gs['g_mix_pre'] = _premix_bwd(dx1, x, [dq_f, dk_f, dv_f, dflog, dq_c, dk_c, dv_c], win_t, sm['g_mix_pre'])
    on_grads('C', _proj_rows_to_in(_wgrad(dproj, h1, "wgrad_in")), None)
    return loss_acc[0, 0], grad_x, gs


def kernel(x, mem, w_in, b_fgt, rel_bias, g_fox_out, g_chk_out, w_out, g_mix_pre, g_mix_post, g_mem_kv, w_mq, w_mk, w_mv, w_mo, g_mem_pre, g_mem_post, w_ff1, w_ff2, g_ff_pre, g_ff_post, loss_target, m_w_in, m_b_fgt, m_rel_bias, m_g_fox_out, m_g_chk_out, m_w_out, m_g_mix_pre, m_g_mix_post, m_g_mem_kv, m_w_mq, m_w_mk, m_w_mv, m_w_mo, m_g_mem_pre, m_g_mem_post, m_w_ff1, m_w_ff2, m_g_ff_pre, m_g_ff_post, v_w_in, v_b_fgt, v_rel_bias, v_g_fox_out, v_g_chk_out, v_w_out, v_g_mix_pre, v_g_mix_post, v_g_mem_kv, v_w_mq, v_w_mk, v_w_mv, v_w_mo, v_g_mem_pre, v_g_mem_post, v_w_ff1, v_w_ff2, v_g_ff_pre, v_g_ff_post):
    args = dict(locals())
    two_d = lambda a: a.reshape(a.shape[-2:])
    w = {n: two_d(args[n]) for n in WEIGHTS}
    m = {n: two_d(args['m_' + n]) for n in WEIGHTS}
    v = {n: two_d(args['v_' + n]) for n in WEIGHTS}

    sm = {n: w[n] for n in SMALL}
    shard_in = jnp.pad(w['w_in'].T, ((0, R_IN - N_IN), (0, 0))).astype(BF16)
    gathered_in, zero = _allgather(shard_in, "allgather_w_in")
    win_t = _in_rows_to_proj(gathered_in)
    shard_rest = (jnp.concatenate([w['w_ff1'].T, w['w_ff2'], w['w_out'], w['w_mq'], w['w_mk'], w['w_mv'], w['w_mo']],
                                  axis=0) + zero[0, 0]).astype(BF16)
    gather = {'first': _start_copies("allgather_rest_start", shard_rest, (8, R_REST, D), _gather_plan, 3)}
    sm['g_mix_pre'] = sm['g_mix_pre'] + gather['first'][4][0, 0]

    def gw_of(stage, after):
        if stage == 'relay':
            gather['block'], land = _wait_copies("allgather_rest_wait", gather['first'], after, _gather_plan)
            gather['second'] = _start_inplace("allgather_rest_relay_start", land, _relay_plan, 4)
            return gather['second'][3][0, 0]
        land = _wait_inplace("allgather_rest_relay_wait", gather['second'], after, _relay_plan)
        return _gather_forward(land, gather['block'])

    rs = {}

    def on_grads(stage, g, after):
        if stage.endswith('halfway'):
            rs[stage[0]].halfway(after)
            return rs[stage[0]].token
        rs[stage] = _ReduceScatter("rs_" + stage.lower(), g)
        return rs[stage].token

    loss_local, grad_x, gs = _local_grads(x[0], mem[0], loss_target[0], win_t, gw_of, sm, on_grads)
    grads, deltas, new_m, new_v = {}, {}, {}, {}

    def update(n, out):
        grads[n], deltas[n], new_m[n], new_v[n] = out

    packed = _pack_small(gs, loss_local + rs['C'].token)
    rs['C'].halfway([packed])
    gparts, _ = _allgather(packed + rs['C'].token, "allgather_small_grads")
    small = _adamw_small(gparts, [w[n] for n in SMALL], [m[n] for n in SMALL], [v[n] for n in SMALL])
    loss = small[0][0, 0]
    for t, n in enumerate(SMALL):
        update(n, small[1 + 4 * t:5 + 4 * t])

    own, got, order = rs['A'].finish([grad_x, rs['C'].started[4]])
    update('w_ff1', _sum_adam(own, got, order, 0, w['w_ff1'], m['w_ff1'], v['w_ff1'], "adamw_w_ff1", transposed=True))
    update('w_ff2', _sum_adam(own, got, order, 512, w['w_ff2'], m['w_ff2'], v['w_ff2'], "adamw_w_ff2"))
    own, got, order = rs['B'].finish([grad_x, rs['C'].started[4]])
    names_b = ('w_out', 'w_mq', 'w_mk', 'w_mv', 'w_mo')
    done = _sum_adam_rows(own, got, order, [w[n] for n in names_b], [m[n] for n in names_b], [v[n] for n in names_b],
                          "adamw_group_b")
    for k, n in enumerate(names_b):
        update(n, done[4 * k:4 * k + 4])

    own, got, order = rs['C'].finish([new_v[n] for n in BIG if n != 'w_in'])
    done = _sum_adam(own, got, order, 0, w['w_in'].T, m['w_in'].T, v['w_in'].T, "adamw_w_in")
    update('w_in', [a.T for a in done])

    out = [loss, grad_x[None]]
    for group in (grads, deltas, new_m, new_v):
        out += [group[n].reshape(args[n].shape) for n in WEIGHTS]
    return tuple(out)
```

```python
import jax
import jax.numpy as jnp
from jax import lax
from jax.experimental import pallas as pl
from jax.experimental.pallas import tpu as pltpu

F32 = jnp.float32
BF16 = jnp.bfloat16
MESH = pl.DeviceIdType.MESH

T = 2048
D = 1024
NMEM = 256
DFF = 4096
EPS = 1e-6
TM = 256
TM_WIDE = 512
TQ = 256
FQ = 512
HD = 64
SCALE = HD ** -0.5
MEM_HEADS = 4
MEM_HD = 256
MEM_SCALE = MEM_HD ** -0.5
NEG = -1e30
LEFT = 512
WIN = LEFT + TQ
VW = 1024
NREL_PAD = 384
PROJ = 3200
GATE0 = 1536
CHK0 = 1664
VMEM_BIG = 56 * 1024 * 1024

ADAM_LR = 0.001
ADAM_B1 = 0.9
ADAM_B2 = 0.999
ADAM_EPS = 1e-08
ADAM_WD = 0.01
ADAM_STEP = 10

N_IN = 385
R_IN = 400
R_REST = 1664
W_ROWS = {'w_ff1': (0, 512), 'w_ff2': (512, 512),
          'w_out': (1024, 128), 'w_mq': (1152, 128), 'w_mk': (1280, 128), 'w_mv': (1408, 128), 'w_mo': (1536, 128)}
SMALL_ROWS = 24
SMALL_SLOT = {'rel_bias': (0, 8, 0, 257), 'b_fgt': (8, 1, 0, 8), 'g_fox_out': (9, 1, 0, 512), 'g_chk_out': (9, 1, 512, 512),
              'g_mix_pre': (10, 1, 0, 1024), 'g_mix_post': (11, 1, 0, 1024), 'g_mem_kv': (12, 1, 0, 1024),
              'g_mem_pre': (13, 1, 0, 1024), 'g_mem_post': (14, 1, 0, 1024), 'g_ff_pre': (15, 1, 0, 1024),
              'g_ff_post': (16, 1, 0, 1024)}

WEIGHTS = ['w_in', 'b_fgt', 'rel_bias', 'g_fox_out', 'g_chk_out', 'w_out', 'g_mix_pre', 'g_mix_post', 'g_mem_kv',
           'w_mq', 'w_mk', 'w_mv', 'w_mo', 'g_mem_pre', 'g_mem_post', 'w_ff1', 'w_ff2', 'g_ff_pre', 'g_ff_post']
BIG = ['w_in', 'w_out', 'w_mq', 'w_mk', 'w_mv', 'w_mo', 'w_ff1', 'w_ff2']
SMALL = [n for n in WEIGHTS if n not in BIG]


def _pcall(body, **kw):
    return pl.pallas_call(body, **kw)


def _nn(a, b):
    return jnp.dot(a, b, preferred_element_type=F32)


def _nt(a, b):
    return lax.dot_general(a, b, (((1,), (1,)), ((), ())), preferred_element_type=F32)


def _tn(a, b):
    return lax.dot_general(a, b, (((0,), (0,)), ((), ())), preferred_element_type=F32)


def _w(ref):
    v = ref[...]
    return v if v.ndim == 2 else v.reshape(-1, v.shape[-1])


def _rstd(x):
    return lax.rsqrt(jnp.mean(x * x, axis=-1, keepdims=True) + EPS)


def _rms(x, g):
    return x * _rstd(x) * g


def _rms_bwd(x, g, dy):
    r = _rstd(x)
    xh = x * r
    dg = jnp.sum(dy * xh, axis=0, keepdims=True)
    dxh = dy * g
    dx = r * (dxh - xh * jnp.mean(dxh * xh, axis=-1, keepdims=True))
    return dx, dg


def _resident(a):
    if isinstance(a, tuple):
        _, shape, index = a
        return pl.BlockSpec(shape, lambda *_: index, pipeline_mode=pl.Buffered(1))
    return pl.BlockSpec(a.shape, lambda *_, nd=a.ndim: (0,) * nd, pipeline_mode=pl.Buffered(1))


def _wblk(gw, name):
    r0, rows = W_ROWS[name]
    return (gw, (8, rows, D), (0, r0 // rows, 0))


def _behind(body, n_in, after):
    if not after:
        return body
    return lambda *refs: body(*refs[:n_in], *refs[n_in + len(after):])


def _tok_call(body, name, tiled, full, outs_tiled, outs_acc=(), rows=T, tm=TM, vmem=None, after=()):
    in_specs = [pl.BlockSpec((tm, a.shape[1]), lambda i: (i, 0)) for a in tiled]
    in_specs += [_resident(a) for a in full] + [ANY_SPEC] * len(after)
    full = [a[0] if isinstance(a, tuple) else a for a in full] + list(after)
    body = _behind(body, len(tiled) + len(full) - len(after), after)
    out_shape = [jax.ShapeDtypeStruct((rows, c), dt) for c, dt in outs_tiled]
    out_shape += [jax.ShapeDtypeStruct(s, F32) for s in outs_acc]
    out_specs = [pl.BlockSpec((tm, c), lambda i: (i, 0)) for c, _ in outs_tiled]
    out_specs += [pl.BlockSpec(s, lambda i, nd=len(s): (0,) * nd) for s in outs_acc]
    return _pcall(
        body, name=name, grid=(rows // tm,), in_specs=in_specs, out_specs=out_specs, out_shape=out_shape,
        compiler_params=pltpu.CompilerParams(dimension_semantics=("arbitrary",), vmem_limit_bytes=vmem),
    )(*tiled, *full)


def _one_call(body, name, ins, outs, vmem=None):
    whole = lambda s: pl.BlockSpec(s, lambda i, nd=len(s): (0,) * nd)
    return _pcall(
        body, name=name, grid=(1,), in_specs=[_resident(a) for a in ins], out_specs=[whole(s) for s, _ in outs],
        out_shape=[jax.ShapeDtypeStruct(s, dt) for s, dt in outs],
        compiler_params=pltpu.CompilerParams(dimension_semantics=("arbitrary",), vmem_limit_bytes=vmem),
    )(*[a[0] if isinstance(a, tuple) else a for a in ins])


def _premix_fwd(x, g_pre, win_t, after=()):
    def body(x_ref, g_ref, w_ref, h_ref, proj_ref, flog_ref):
        h = _rms(x_ref[...], g_ref[...]).astype(BF16)
        h_ref[...] = h
        p = _nt(h, w_ref[...])
        proj_ref[...] = p.astype(BF16)
        flog_ref[...] = p[:, GATE0:GATE0 + 128]

    return _tok_call(body, "premix_fwd", [x], [g_pre, win_t],
                     [(D, BF16), (PROJ, BF16), (128, F32)], tm=TM_WIDE, vmem=VMEM_BIG, after=after)


def _postmix_fwd(x, o_f, o_c, g_fo, g_co, w_out, g_post, g_mpre, w_mq):
    def body(x_ref, of_ref, oc_ref, gfo_ref, gco_ref, wo_ref, gp_ref, gm_ref, wq_ref,
             y_ref, z_ref, x1_ref, h2_ref, qm_ref):
        y_ref[:, :512] = _rms(of_ref[...], gfo_ref[...]).astype(BF16)
        y_ref[:, 512:] = _rms(oc_ref[...], gco_ref[...]).astype(BF16)
        z = _nn(y_ref[...], _w(wo_ref))
        z_ref[...] = z
        x1 = x_ref[...] + _rms(z, gp_ref[...])
        x1_ref[...] = x1
        h2 = _rms(x1, gm_ref[...]).astype(BF16)
        h2_ref[...] = h2
        qm_ref[...] = _nn(h2, _w(wq_ref)).astype(BF16)

    return _tok_call(body, "postmix_fwd", [x, o_f, o_c], [g_fo, g_co, w_out, g_post, g_mpre, w_mq],
                     [(D, BF16), (D, F32), (D, F32), (D, BF16), (D, BF16)], tm=TM_WIDE, vmem=VMEM_BIG)


def _memkv_fwd(mem, g_kv, w_mk, w_mv):
    def body(m_ref, g_ref, wk_ref, wv_ref, mn_ref, k_ref, v_ref):
        mn = _rms(m_ref[...], g_ref[...]).astype(BF16)
        mn_ref[...] = mn
        k_ref[...] = _nn(mn, _w(wk_ref)).astype(BF16)
        v_ref[...] = _nn(mn, _w(wv_ref)).astype(BF16)

    return _tok_call(body, "memkv_fwd", [mem], [g_kv, w_mk, w_mv],
                     [(D, BF16), (D, BF16), (D, BF16)], rows=NMEM, tm=NMEM, vmem=VMEM_BIG)


def _mem_fwd(qm, x1, km, vm, w_mo, g_post, g_fpre):
    def body(q_ref, x1_ref, k_ref, v_ref, wo_ref, gp_ref, gf_ref, om_ref, ym_ref, x2_ref, h3_ref):
        for h in range(MEM_HEADS):
            sl = slice(h * MEM_HD, (h + 1) * MEM_HD)
            s = _nt(q_ref[:, sl], k_ref[:, sl]) * MEM_SCALE
            p = jnp.exp(s - jnp.max(s, axis=-1, keepdims=True))
            p = p / jnp.sum(p, axis=-1, keepdims=True)
            om_ref[:, sl] = _nn(p.astype(BF16), v_ref[:, sl]).astype(BF16)
        ym = _nn(om_ref[...], _w(wo_ref))
        ym_ref[...] = ym
        x2 = x1_ref[...] + _rms(ym, gp_ref[...])
        x2_ref[...] = x2
        h3_ref[...] = _rms(x2, gf_ref[...]).astype(BF16)

    return _tok_call(body, "mem_fwd", [qm, x1], [km, vm, w_mo, g_post, g_fpre],
                     [(D, BF16), (D, F32), (D, F32), (D, BF16)], tm=TM_WIDE, vmem=VMEM_BIG)


def _tri(lower):
    r = lax.broadcasted_iota(jnp.int32, (128, 128), 0)
    c = lax.broadcasted_iota(jnp.int32, (128, 128), 1)
    return jnp.where(r >= c if lower else c >= r, 1.0, 0.0).astype(F32)


def _hdot(a, b):
    return jnp.dot(a, b, preferred_element_type=F32, precision=lax.Precision.HIGHEST)


def _gate_fwd(flog, b_pad):
    def body(f_ref, b_ref, c_ref):
        tri = _tri(True)

        def step(i, carry):
            rows = pl.ds(pl.multiple_of(i * 128, 128), 128)
            z = f_ref[rows, :] + b_ref[...]
            lf = jnp.minimum(z, 0.0) - jnp.log(1.0 + jnp.exp(-jnp.abs(z)))
            cb = _hdot(tri, lf) + carry
            c_ref[rows, :] = cb
            return cb[127:128, :]

        lax.fori_loop(0, T // 128, step, jnp.zeros((1, 128), F32))

    return _one_call(body, "gate_fwd", [flog, b_pad], [((T, 128), F32)])[0]


def _gate_bwd(dc, flog, b_pad):
    def body(dc_ref, f_ref, b_ref, df_ref, db_ref):
        tri = _tri(False)

        def step(j, carry):
            run, db = carry
            i = T // 128 - 1 - j
            rows = pl.ds(pl.multiple_of(i * 128, 128), 128)
            dcb = dc_ref[rows, :]
            rb = _hdot(tri, dcb) + run
            z = f_ref[rows, :] + b_ref[...]
            df = rb * (1.0 / (1.0 + jnp.exp(z)))
            df_ref[rows, :] = df.astype(BF16)
            return run + jnp.sum(dcb, axis=0, keepdims=True), db + jnp.sum(df, axis=0, keepdims=True)

        _, db = lax.fori_loop(0, T // 128, step, (jnp.zeros((1, 128), F32), jnp.zeros((1, 128), F32)))
        db_ref[...] = jnp.broadcast_to(db, (8, 128))

    return _one_call(body, "gate_bwd", [dc, flog, b_pad], [((T, 128), BF16), ((8, 128), F32)])


def _lane_lo(rows=TQ):
    return lax.broadcasted_iota(jnp.int32, (rows, 128), 1) < HD


def _half(v, lo, a, scale=None):
    keep = lo if a == 0 else jnp.logical_not(lo)
    v = v.astype(F32) if scale is None else v.astype(F32) * scale
    return jnp.where(keep, v, 0.0).astype(BF16)


def _fox_specs():
    return [pl.BlockSpec((FQ, 128), lambda h, i: (i, h)),
            pl.BlockSpec((T, 128), lambda h, i: (0, 4 + h)),
            pl.BlockSpec((T, 128), lambda h, i: (0, 8 + h))]


def _lane_pick(x, at):
    lane = lax.broadcasted_iota(jnp.int32, x.shape, 1)
    return jnp.sum(jnp.where(lane == at, x, 0.0), axis=-1, keepdims=True)


def _fox_fwd(proj, c, ct3):
    def body(q_ref, k_ref, v_ref, c_ref, ct_ref, o_ref, l_ref):
        i = pl.program_id(1)
        lo = _lane_lo(FQ)
        causal = lax.broadcasted_iota(jnp.int32, (FQ, FQ), 1) <= lax.broadcasted_iota(jnp.int32, (FQ, FQ), 0)
        q = q_ref[...]
        qs = [_half(q, lo, a, SCALE) for a in range(2)]
        cqs = [_lane_pick(c_ref[...], 2 * pl.program_id(0) + a) for a in range(2)]

        def tile(off, carry, diagonal):
            kblk = k_ref[pl.ds(off, FQ), :]
            vblk = v_ref[pl.ds(off, FQ), :]
            new = []
            for a in range(2):
                m, l, acc = carry[a]
                s = _nt(qs[a], kblk) + (cqs[a] - ct_ref[a:a + 1, pl.ds(off, FQ)])
                if diagonal:
                    s = jnp.where(causal, s, NEG)
                m2 = jnp.maximum(m, jnp.max(s, axis=-1, keepdims=True))
                p = jnp.exp(s - m2)
                alpha = jnp.exp(m - m2)
                new.append((m2, alpha * l + jnp.sum(p, axis=-1, keepdims=True),
                            alpha * acc + _nn(p.astype(BF16), vblk)))
            return tuple(new)

        init = (jnp.full((FQ, 1), NEG, F32), jnp.zeros((FQ, 1), F32), jnp.zeros((FQ, 128), F32))
        carry = lax.fori_loop(0, i, lambda kb, c: tile(pl.multiple_of(kb * FQ, FQ), c, False), (init, init))
        carry = tile(pl.multiple_of(i * FQ, FQ), carry, True)
        outs = []
        for a in range(2):
            m, l, acc = carry[a]
            outs.append(acc / l)
            l_ref[:, 128 * a:128 * a + 128] = jnp.broadcast_to(m + jnp.log(l), (FQ, 128))
        o_ref[...] = jnp.where(lo, outs[0], outs[1])

    return _pcall(
        body, name="fox_fwd", grid=(4, T // FQ),
        in_specs=_fox_specs() + [pl.BlockSpec((FQ, 128), lambda h, i: (i, 0)),
                                 pl.BlockSpec((None, 2, T), lambda h, i: (h, 0, 0))],
        out_specs=[pl.BlockSpec((FQ, 128), lambda h, i: (i, h)), pl.BlockSpec((FQ, 256), lambda h, i: (i, h))],
        out_shape=[jax.ShapeDtypeStruct((T, 512), F32), jax.ShapeDtypeStruct((T, 1024), F32)],
        compiler_params=pltpu.CompilerParams(dimension_semantics=("arbitrary", "arbitrary"), vmem_limit_bytes=VMEM_BIG),
    )(proj, proj, proj, c, ct3)


def _fox_bwd(proj, c, ct3, o, lse, do, after=()):
    def body(q_ref, k_ref, v_ref, c_ref, ct_ref, o_ref, l_ref, do_ref, dq_ref, dkb_ref, dvb_ref, dct_ref, dcq_ref,
             dk_ref, dv_ref):
        i = pl.program_id(1)

        @pl.when(i == 0)
        def _():
            dk_ref[...] = jnp.zeros_like(dk_ref)
            dv_ref[...] = jnp.zeros_like(dv_ref)
            dct_ref[...] = jnp.zeros_like(dct_ref)

        lo = _lane_lo(FQ)
        causal = lax.broadcasted_iota(jnp.int32, (FQ, FQ), 1) <= lax.broadcasted_iota(jnp.int32, (FQ, FQ), 0)
        q = q_ref[...]
        do_v = do_ref[...]
        prod = do_v * o_ref[...]
        qs = [_half(q, lo, a, SCALE) for a in range(2)]
        dos = [_half(do_v, lo, a) for a in range(2)]
        deltas = [jnp.sum(jnp.where(lo if a == 0 else jnp.logical_not(lo), prod, 0.0), axis=-1, keepdims=True)
                  for a in range(2)]
        cqs = [_lane_pick(c_ref[...], 2 * pl.program_id(0) + a) for a in range(2)]
        las = [l_ref[:, 128 * a:128 * a + 1] for a in range(2)]

        def tile(off, carry, diagonal):
            kblk = k_ref[pl.ds(off, FQ), :]
            vblk = v_ref[pl.ds(off, FQ), :]
            new = []
            dk = jnp.zeros((128, FQ), F32)
            dv = jnp.zeros((128, FQ), F32)
            for a in range(2):
                dq_acc, rs = carry[a]
                s = _nt(qs[a], kblk) + (cqs[a] - ct_ref[a:a + 1, pl.ds(off, FQ)])
                if diagonal:
                    s = jnp.where(causal, s, NEG)
                p = jnp.exp(s - las[a])
                ds = p * (_nt(dos[a], vblk) - deltas[a])
                dsb = ds.astype(BF16)
                dk = dk + _tn(qs[a], dsb)
                dv = dv + _tn(dos[a], p.astype(BF16))
                dct_ref[a:a + 1, pl.ds(off, FQ)] -= jnp.sum(ds, axis=0, keepdims=True)
                new.append((dq_acc + _nn(dsb, kblk), rs + jnp.sum(ds, axis=-1, keepdims=True)))
            dk_ref[:, pl.ds(off, FQ)] += dk
            dv_ref[:, pl.ds(off, FQ)] += dv
            return tuple(new)

        init = (jnp.zeros((FQ, 128), F32), jnp.zeros((FQ, 1), F32))
        carry = lax.fori_loop(0, i, lambda kb, c: tile(pl.multiple_of(kb * FQ, FQ), c, False), (init, init))
        carry = tile(pl.multiple_of(i * FQ, FQ), carry, True)
        lane = lax.broadcasted_iota(jnp.int32, (FQ, 128), 1)
        dcq_ref[...] = jnp.where(lane == 0, carry[0][1], jnp.where(lane == 1, carry[1][1], 0.0))
        dq_ref[...] = (jnp.where(lo, carry[0][0], carry[1][0]) * SCALE).astype(BF16)

        @pl.when(i == T // FQ - 1)
        def _():
            dkb_ref[...] = dk_ref[...].T.astype(BF16)
            dvb_ref[...] = dv_ref[...].T.astype(BF16)

    blk = pl.BlockSpec((FQ, 128), lambda h, i: (i, h))
    wide = pl.BlockSpec((FQ, 256), lambda h, i: (i, h))
    rows = pl.BlockSpec((None, 2, T), lambda h, i: (h, 0, 0))
    col = pl.BlockSpec((T, 128), lambda h, i: (0, h))
    return _pcall(
        _behind(body, 8, after), name="fox_bwd", grid=(4, T // FQ),
        in_specs=_fox_specs() + [pl.BlockSpec((FQ, 128), lambda h, i: (i, 0)), rows, blk, wide, blk] + [ANY_SPEC] * len(after),
        out_specs=[blk, col, col, rows, pl.BlockSpec((None, FQ, 128), lambda h, i: (h, i, 0))],
        out_shape=[jax.ShapeDtypeStruct((T, 512), BF16), jax.ShapeDtypeStruct((T, 512), BF16),
                   jax.ShapeDtypeStruct((T, 512), BF16), jax.ShapeDtypeStruct((4, 2, T), F32),
                   jax.ShapeDtypeStruct((4, T, 128), F32)],
        scratch_shapes=[pltpu.VMEM((128, T), F32), pltpu.VMEM((128, T), F32)],
        compiler_params=pltpu.CompilerParams(dimension_semantics=("arbitrary", "arbitrary"), vmem_limit_bytes=VMEM_BIG),
    )(proj, proj, proj, c, ct3, o, lse, do, *after)


def _rel_onehot():
    ridx = lax.broadcasted_iota(jnp.int32, (NREL_PAD, VW), 0)
    j = lax.broadcasted_iota(jnp.int32, (NREL_PAD, VW), 1)
    return jnp.where(ridx == jnp.clip(TQ + LEFT - 1 - j, -128, 128) + 128, 1.0, 0.0).astype(F32)


def _relvec_fwd(tbl):
    def body(t_ref, v_ref):
        v_ref[...] = _hdot(t_ref[...], _rel_onehot())

    return _one_call(body, "relvec_fwd", [tbl], [((8, VW), F32)])[0]


def _relvec_bwd(gv):
    def body(g_ref, t_ref):
        t_ref[...] = lax.dot_general(g_ref[...], _rel_onehot(), (((1,), (1,)), ((), ())),
                                     preferred_element_type=F32, precision=lax.Precision.HIGHEST)

    return _one_call(body, "relvec_bwd", [gv], [((8, NREL_PAD), F32)])[0]


def _chk_bias(vt_ref, a, hidden):
    vb = jnp.broadcast_to(vt_ref[a:a + 1, :], (TQ, VW))
    y = pltpu.roll(vb, VW - (TQ - 1), 1, stride=1, stride_axis=0)[:, :WIN]
    cr = lax.broadcasted_iota(jnp.int32, (TQ, WIN), 0) // 64
    m = lax.broadcasted_iota(jnp.int32, (TQ, WIN), 1)
    return jnp.where((m // 64 >= cr) & (m // 64 <= cr + 8) & (m >= hidden), y, NEG)


def _chk_specs():
    return [pl.BlockSpec((TQ, 128), lambda h, i: (i, CHK0 // 128 + h)),
            pl.BlockSpec((T + LEFT, 128), lambda h, i: (0, h)),
            pl.BlockSpec((T + LEFT, 128), lambda h, i: (0, 4 + h)),
            pl.BlockSpec((None, 2, VW), lambda h, i: (h, 0, 0))]


def _chk_fwd(proj, kvp, vt3, after=()):
    def body(q_ref, k_ref, v_ref, vt_ref, o_ref, l_ref, bias_ref):
        i = pl.program_id(1)

        @pl.when(i == 0)
        def _():
            for first in range(3):
                for a in range(2):
                    bias_ref[first, a] = _chk_bias(vt_ref, a, max(LEFT - first * TQ, 0))

        lo = _lane_lo()
        off = pl.multiple_of(i * TQ, TQ)
        kw = k_ref[pl.ds(off, WIN), :]
        vw = v_ref[pl.ds(off, WIN), :]
        bias_at = jnp.minimum(i, 2)
        q = q_ref[...]
        outs = []
        for a in range(2):
            s = _nt(_half(q, lo, a, SCALE), kw) + bias_ref[bias_at, a]
            m = jnp.max(s, axis=-1, keepdims=True)
            p = jnp.exp(s - m)
            l = jnp.sum(p, axis=-1, keepdims=True)
            outs.append(_nn(p.astype(BF16), vw) / l)
            l_ref[:, 128 * a:128 * a + 128] = jnp.broadcast_to(m + jnp.log(l), (TQ, 128))
        o_ref[...] = jnp.where(lo, outs[0], outs[1])

    return _pcall(
        _behind(body, 4, after), name="chk_fwd", grid=(4, T // TQ), in_specs=_chk_specs() + [ANY_SPEC] * len(after),
        out_specs=[pl.BlockSpec((TQ, 128), lambda h, i: (i, h)), pl.BlockSpec((TQ, 256), lambda h, i: (i, h))],
        out_shape=[jax.ShapeDtypeStruct((T, 512), F32), jax.ShapeDtypeStruct((T, 1024), F32)],
        scratch_shapes=[pltpu.VMEM((3, 2, TQ, WIN), F32)],
        compiler_params=pltpu.CompilerParams(dimension_semantics=("arbitrary", "arbitrary")),
    )(proj, kvp, kvp, vt3, *after)


def _chk_bwd(proj, kvp, vt3, o, lse, do, after=()):
    nq = T // TQ

    def body(q_ref, k_ref, v_ref, vt_ref, o_ref, l_ref, do_ref, dq_ref, dkb_ref, dvb_ref, gv_ref, bias_ref, dsum_ref,
             dk_ref, dv_ref):
        i = pl.program_id(1)

        @pl.when(i == 0)
        def _():
            for first in range(3):
                for a in range(2):
                    bias_ref[first, a] = _chk_bias(vt_ref, a, max(LEFT - first * TQ, 0))
            dsum_ref[...] = jnp.zeros_like(dsum_ref)
            dk_ref[...] = jnp.zeros_like(dk_ref)
            dv_ref[...] = jnp.zeros_like(dv_ref)

        lo = _lane_lo()
        off = pl.multiple_of(i * TQ, TQ)
        kw = k_ref[pl.ds(off, WIN), :]
        vw = v_ref[pl.ds(off, WIN), :]
        bias_at = jnp.minimum(i, 2)
        q = q_ref[...]
        do_v = do_ref[...]
        prod = do_v * o_ref[...]
        dqs = []
        for a in range(2):
            keep = lo if a == 0 else jnp.logical_not(lo)
            qa = _half(q, lo, a, SCALE)
            doa = _half(do_v, lo, a)
            delta = jnp.sum(jnp.where(keep, prod, 0.0), axis=-1, keepdims=True)
            s = _nt(qa, kw) + bias_ref[bias_at, a]
            p = jnp.exp(s - l_ref[:, 128 * a:128 * a + 1])
            ds = p * (_nt(doa, vw) - delta)
            dsum_ref[a] += ds
            dsb = ds.astype(BF16)
            dk_ref[:, pl.ds(off, WIN)] += _tn(qa, dsb)
            dv_ref[:, pl.ds(off, WIN)] += _tn(doa, p.astype(BF16))
            dqs.append(_nn(dsb, kw))
        dq_ref[...] = (jnp.where(lo, dqs[0], dqs[1]) * SCALE).astype(BF16)

        @pl.when(i == nq - 1)
        def _():
            dkb_ref[...] = dk_ref[:, LEFT:].T.astype(BF16)
            dvb_ref[...] = dv_ref[:, LEFT:].T.astype(BF16)
            rr = lax.broadcasted_iota(jnp.int32, (TQ, TQ), 0)
            cc = lax.broadcasted_iota(jnp.int32, (TQ, TQ), 1)
            flip = jnp.where(rr + cc == TQ - 1, 1.0, 0.0).astype(F32)
            for a in range(2):
                dpad = jnp.concatenate([dsum_ref[a], jnp.zeros((TQ, VW - WIN), F32)], axis=1)
                z = pltpu.roll(_hdot(flip, dpad), 0, 1, stride=1, stride_axis=0)
                gv_ref[a:a + 1, :] = jnp.sum(z, axis=0, keepdims=True)

    blk = pl.BlockSpec((TQ, 128), lambda h, i: (i, h))
    wide = pl.BlockSpec((TQ, 256), lambda h, i: (i, h))
    col = pl.BlockSpec((T, 128), lambda h, i: (0, h))
    return _pcall(
        _behind(body, 7, after), name="chk_bwd", grid=(4, nq), in_specs=_chk_specs() + [blk, wide, blk] + [ANY_SPEC] * len(after),
        out_specs=[blk, col, col, pl.BlockSpec((None, 2, VW), lambda h, i: (h, 0, 0))],
        out_shape=[jax.ShapeDtypeStruct((T, 512), BF16), jax.ShapeDtypeStruct((T, 512), BF16),
                   jax.ShapeDtypeStruct((T, 512), BF16), jax.ShapeDtypeStruct((4, 2, VW), F32)],
        scratch_shapes=[pltpu.VMEM((3, 2, TQ, WIN), F32), pltpu.VMEM((2, TQ, WIN), F32),
                        pltpu.VMEM((128, T + LEFT), F32), pltpu.VMEM((128, T + LEFT), F32)],
        compiler_params=pltpu.CompilerParams(dimension_semantics=("arbitrary", "arbitrary")),
    )(proj, kvp, kvp, vt3, o, lse, do, *after)


def _zero_at_start(*refs):
    @pl.when(pl.program_id(0) == 0)
    def _():
        for r in refs:
            r[...] = jnp.zeros_like(r)


def _ffn_step(h3, x2, tgt, w1_t, w2, g_post, g_pre):
    def body(h_ref, x2_ref, t_ref, w1_ref, w2_ref, gp_ref, gf_ref, dx2_ref, da_ref, dy_ref, r_ref, loss_ref, dgp_ref, dgf_ref):
        _zero_at_start(loss_ref, dgp_ref, dgf_ref)
        w1, w2v = _w(w1_ref), _w(w2_ref)
        ra = jnp.maximum(_nt(h_ref[...], w1), 0.0)
        r = jnp.square(ra).astype(BF16)
        r_ref[...] = r
        y = _nn(r, w2v)
        x2v = x2_ref[...]
        e = x2v + _rms(y, gp_ref[...]) - t_ref[...]
        loss_ref[...] += 0.5 * jnp.sum(jnp.sum(e * e, axis=-1, keepdims=True) * (1.0 / D))
        dx3 = e * (1.0 / D)
        dy, dgp = _rms_bwd(y, gp_ref[...], dx3)
        dgp_ref[...] += dgp
        dyb = dy.astype(BF16)
        dy_ref[...] = dyb
        da = (_nt(dyb, w2v) * (2.0 * ra)).astype(BF16)
        da_ref[...] = da
        dh, dgf = _rms_bwd(x2v, gf_ref[...], _nn(da, w1))
        dgf_ref[...] += dgf
        dx2_ref[...] = dx3 + dh

    return _tok_call(body, "ffn_step", [h3, x2, tgt], [w1_t, w2, g_post, g_pre],
                     [(D, F32), (DFF, BF16), (D, BF16), (DFF, BF16)], [(8, 128), (1, D), (1, D)], vmem=VMEM_BIG)


def _mem_bwd(dx2, ym, x1, qm, km, vm, w_mo, w_mq, g_post, g_pre, after=()):
    def body(dx2_ref, ym_ref, x1_ref, q_ref, k_ref, v_ref, wo_ref, wq_ref, gp_ref, gm_ref,
             dx1_ref, dym_ref, dq_ref, dk_ref, dv_ref, dgp_ref, dgm_ref, dom_ref):
        _zero_at_start(dk_ref, dv_ref, dgp_ref, dgm_ref)
        dx2_v = dx2_ref[...]
        dym, dgp = _rms_bwd(ym_ref[...], gp_ref[...], dx2_v)
        dgp_ref[...] += dgp
        dymb = dym.astype(BF16)
        dym_ref[...] = dymb
        dom_ref[...] = _nt(dymb, _w(wo_ref)).astype(BF16)
        for h in range(MEM_HEADS):
            sl = slice(h * MEM_HD, (h + 1) * MEM_HD)
            qh, kh, doh = q_ref[:, sl], k_ref[:, sl], dom_ref[:, sl]
            s = _nt(qh, kh) * MEM_SCALE
            p = jnp.exp(s - jnp.max(s, axis=-1, keepdims=True))
            p = p / jnp.sum(p, axis=-1, keepdims=True)
            dp = _nt(doh, v_ref[:, sl])
            ds = (p * (dp - jnp.sum(p * dp, axis=-1, keepdims=True))).astype(BF16)
            dq_ref[:, sl] = (_nn(ds, kh) * MEM_SCALE).astype(BF16)
            dk_ref[:, sl] += _tn(ds, qh) * MEM_SCALE
            dv_ref[:, sl] += _tn(p.astype(BF16), doh)
        dh, dgm = _rms_bwd(x1_ref[...], gm_ref[...], _nt(dq_ref[...], _w(wq_ref)))
        dgm_ref[...] += dgm
        dx1_ref[...] = dx2_v + dh

    tiled = pl.BlockSpec((TM_WIDE, D), lambda i: (i, 0))
    in_specs = [tiled] * 4 + [_resident(a) for a in (km, vm, w_mo, w_mq, g_post, g_pre)] + [ANY_SPEC] * len(after)
    w_mo, w_mq = w_mo[0], w_mq[0]
    kv = pl.BlockSpec((NMEM, D), lambda i: (0, 0))
    vec = pl.BlockSpec((1, D), lambda i: (0, 0))
    return _pcall(
        _behind(body, 10, after), name="mem_bwd", grid=(T // TM_WIDE,), in_specs=in_specs,
        out_specs=[tiled, tiled, tiled, kv, kv, vec, vec],
        out_shape=[jax.ShapeDtypeStruct((T, D), F32), jax.ShapeDtypeStruct((T, D), BF16),
                   jax.ShapeDtypeStruct((T, D), BF16), jax.ShapeDtypeStruct((NMEM, D), F32),
                   jax.ShapeDtypeStruct((NMEM, D), F32), jax.ShapeDtypeStruct((1, D), F32),
                   jax.ShapeDtypeStruct((1, D), F32)],
        scratch_shapes=[pltpu.VMEM((TM_WIDE, D), BF16)],
        compiler_params=pltpu.CompilerParams(dimension_semantics=("arbitrary",), vmem_limit_bytes=VMEM_BIG),
    )(dx2, ym, x1, qm, km, vm, w_mo, w_mq, g_post, g_pre, *after)


def _memkv_bwd(dkm, dvm, mem, w_mk, w_mv):
    def body(dk_ref, dv_ref, m_ref, wk_ref, wv_ref, dg_ref):
        dmn = _nt(dk_ref[...].astype(BF16), _w(wk_ref)) + _nt(dv_ref[...].astype(BF16), _w(wv_ref))
        mv = m_ref[...]
        dg_ref[...] = jnp.sum(dmn * (mv * _rstd(mv)), axis=0, keepdims=True)

    return _one_call(body, "memkv_bwd", [dkm, dvm, mem, w_mk, w_mv], [((1, D), F32)], vmem=VMEM_BIG)[0]


def _postmix_bwd(dx1, z, o_f, o_c, w_out, g_post, g_fo, g_co, after=()):
    def body(dx1_ref, z_ref, of_ref, oc_ref, wo_ref, gp_ref, gfo_ref, gco_ref,
             dz_ref, dof_ref, doc_ref, dgp_ref, dgfo_ref, dgco_ref):
        _zero_at_start(dgp_ref, dgfo_ref, dgco_ref)
        dz, dgp = _rms_bwd(z_ref[...], gp_ref[...], dx1_ref[...])
        dgp_ref[...] += dgp
        dzb = dz.astype(BF16)
        dz_ref[...] = dzb
        dy = _nt(dzb, _w(wo_ref))
        dof, dgfo = _rms_bwd(of_ref[...], gfo_ref[...], dy[:, :512])
        doc, dgco = _rms_bwd(oc_ref[...], gco_ref[...], dy[:, 512:])
        dof_ref[...] = dof
        doc_ref[...] = doc
        dgfo_ref[...] += dgfo
        dgco_ref[...] += dgco

    return _tok_call(body, "postmix_bwd", [dx1, z, o_f, o_c], [w_out, g_post, g_fo, g_co],
                     [(D, BF16), (512, F32), (512, F32)], [(1, D), (1, 512), (1, 512)], tm=TM_WIDE, vmem=VMEM_BIG,
                     after=after)


def _premix_bwd(dx1, x, pieces, win_t, g_pre):
    def body(dx1_ref, x_ref, *refs):
        piece_refs, (w_ref, g_ref, dx_ref, dp_ref, dg_ref) = refs[:len(pieces)], refs[len(pieces):]
        _zero_at_start(dg_ref)
        col = 0
        for p in piece_refs:
            dp_ref[:, col:col + p.shape[1]] = p[...]
            col += p.shape[1]
        dh, dg = _rms_bwd(x_ref[...], g_ref[...], _nn(dp_ref[...], w_ref[...]))
        dg_ref[...] += dg
        dx_ref[...] = dx1_ref[...] + dh

    return _tok_call(body, "premix_bwd", [dx1, x] + list(pieces), [win_t, g_pre], [(D, F32), (PROJ, BF16)], [(1, D)],
                     tm=TM_WIDE, vmem=VMEM_BIG)


def _wgrad(a, b, name):
    k, m = a.shape
    n = b.shape[1]
    tm = 640 if m % 640 == 0 and m > 1024 else min(m, 512)
    tn = min(n, 1024)

    def body(a_ref, b_ref, o_ref):
        o_ref[...] = _tn(a_ref[...].astype(BF16), b_ref[...].astype(BF16))

    return _pcall(
        body, name=name, grid=(m // tm, n // tn),
        in_specs=[pl.BlockSpec((k, tm), lambda i, j: (0, i)), pl.BlockSpec((k, tn), lambda i, j: (0, j))],
        out_specs=pl.BlockSpec((tm, tn), lambda i, j: (i, j)),
        out_shape=jax.ShapeDtypeStruct((m, n), F32),
        compiler_params=pltpu.CompilerParams(dimension_semantics=("arbitrary", "arbitrary"), vmem_limit_bytes=VMEM_BIG),
    )(a, b)


def _wgrad_group(name, pairs, rows):
    def body(*refs):
        o_ref = refs[-1]
        for k in range(len(pairs)):
            o_ref[k * rows:(k + 1) * rows, :] = _tn(refs[2 * k][...].astype(BF16), refs[2 * k + 1][...].astype(BF16))

    in_specs, ops = [], []
    for a, b in pairs:
        in_specs += [pl.BlockSpec((a.shape[0], rows), lambda j: (0, j)), _resident(b)]
        ops += [a, b]
    return _pcall(
        body, name=name, grid=(8,), in_specs=in_specs,
        out_specs=pl.BlockSpec((None, len(pairs) * rows, D), lambda j: (j, 0, 0)),
        out_shape=jax.ShapeDtypeStruct((8, len(pairs) * rows, D), F32),
        compiler_params=pltpu.CompilerParams(dimension_semantics=("arbitrary",), vmem_limit_bytes=VMEM_BIG),
    )(*ops)


def _adam_math(w, g, m, v):
    m2 = ADAM_B1 * m + (1.0 - ADAM_B1) * g
    v2 = ADAM_B2 * v + (1.0 - ADAM_B2) * jnp.square(g)
    m_hat = m2 / (1.0 - ADAM_B1 ** ADAM_STEP)
    v_hat = v2 / (1.0 - ADAM_B2 ** ADAM_STEP)
    delta = -ADAM_LR * (m_hat / (jnp.sqrt(v_hat) + ADAM_EPS) + ADAM_WD * w)
    return delta, m2, v2


def _adamw_small(gparts, ws, ms, vs):
    n = len(SMALL)

    def body(g_ref, *refs):
        w_refs, m_refs, v_refs = refs[:n], refs[n:2 * n], refs[2 * n:3 * n]
        outs, sum_ref = refs[3 * n:-1], refs[-1]
        g = g_ref[0]
        for k in range(1, 8):
            g = g + g_ref[k]
        sum_ref[...] = g
        outs[0][...] = sum_ref[17:18, 0:128]
        for t, name in enumerate(SMALL):
            r0, nr, c0, nc = SMALL_SLOT[name]
            gt = sum_ref[r0:r0 + nr, c0:c0 + nc]
            out = (gt,) + _adam_math(w_refs[t][...], gt, m_refs[t][...], v_refs[t][...])
            for o_ref, val in zip(outs[1 + 4 * t:5 + 4 * t], out):
                o_ref[...] = val

    whole = lambda s: pl.BlockSpec(s, lambda i, nd=len(s): (0,) * nd)
    ins = [gparts] + list(ws) + list(ms) + list(vs)
    out_shapes = [(1, 128)] + [a.shape for a in ws for _ in range(4)]
    return _pcall(
        body, name="adamw_small", grid=(1,), in_specs=[whole(a.shape) for a in ins],
        out_specs=[whole(s) for s in out_shapes], out_shape=[jax.ShapeDtypeStruct(s, F32) for s in out_shapes],
        scratch_shapes=[pltpu.VMEM((SMALL_ROWS, D), F32)],
        compiler_params=pltpu.CompilerParams(dimension_semantics=("arbitrary",)),
    )(*ins)


def _row_tile(rows):
    return next(t for t in (512, 400, 320) if rows % t == 0)


def _add_halves(g4, theirs, core, name):
    rows = g4.shape[2]
    tr = _row_tile(rows)

    def body(c_ref, a_ref, b_ref, o_ref):
        o_ref[...] = (a_ref[...] + b_ref[...]).astype(BF16)

    grid_spec = pltpu.PrefetchScalarGridSpec(
        num_scalar_prefetch=1, grid=(4, rows // tr),
        in_specs=[pl.BlockSpec((None, None, tr, D), lambda j, i, c: (j, c[0], i, 0)),
                  pl.BlockSpec((None, None, tr, D), lambda j, i, c: (j, 0, i, 0))],
        out_specs=pl.BlockSpec((None, tr, D), lambda j, i, c: (j, i, 0)))
    return _pcall(
        body, name=name, grid_spec=grid_spec, out_shape=jax.ShapeDtypeStruct((4, rows, D), BF16),
        compiler_params=pltpu.CompilerParams(dimension_semantics=("arbitrary", "arbitrary")),
    )(core, g4, theirs)


def _sum_adam(own, got, order, r0, w, m, v, name, transposed=False):
    n = w.shape[1] if transposed else w.shape[0]
    tr = min(n, 256) if n % 8 == 0 else n
    rows = tr if n % 8 == 0 else own.shape[1]

    def body(o_ref, a_ref, b_ref, c_ref, d_ref, w_ref, m_ref, v_ref, g_ref, dl_ref, m2_ref, v2_ref):
        f = lambda r: r[0:tr, :].astype(F32)
        g = ((f(a_ref) + f(b_ref)) + f(c_ref)) + f(d_ref)
        g = g.T if transposed else g
        g_ref[...] = g
        dl_ref[...], m2_ref[...], v2_ref[...] = _adam_math(w_ref[...], g, m_ref[...], v_ref[...])

    slot = lambda k: pl.BlockSpec((None, rows, D), lambda i, o: (o[k], r0 // rows + i, 0))
    wspec = pl.BlockSpec((D, tr), lambda i, o: (0, i)) if transposed else pl.BlockSpec((tr, D), lambda i, o: (i, 0))
    grid_spec = pltpu.PrefetchScalarGridSpec(
        num_scalar_prefetch=1, grid=(n // tr,), in_specs=[slot(0), slot(1), slot(2), slot(3), wspec, wspec, wspec],
        out_specs=[wspec] * 4)
    return _pcall(
        body, name=name, grid_spec=grid_spec, out_shape=[jax.ShapeDtypeStruct(w.shape, F32)] * 4,
        compiler_params=pltpu.CompilerParams(dimension_semantics=("arbitrary",)),
    )(order, own, got, got, got, w, m, v)


def _sum_adam_rows(own, got, order, ws, ms, vs, name):
    n, rows = len(ws), ws[0].shape[0]

    def body(o_ref, a_ref, b_ref, c_ref, d_ref, *refs):
        ins, outs = refs[:3 * n], refs[3 * n:]
        for t in range(n):
            r = slice(t * rows, (t + 1) * rows)
            f = lambda ref: ref[r, :].astype(F32)
            g = ((f(a_ref) + f(b_ref)) + f(c_ref)) + f(d_ref)
            out = (g,) + _adam_math(ins[t][...], g, ins[n + t][...], ins[2 * n + t][...])
            for o, val in zip(outs[4 * t:4 * t + 4], out):
                o[...] = val

    slot = lambda k: pl.BlockSpec((None, n * rows, D), lambda i, o: (o[k], 0, 0), pipeline_mode=pl.Buffered(1))
    wspec = pl.BlockSpec((rows, D), lambda i, o: (0, 0), pipeline_mode=pl.Buffered(1))
    grid_spec = pltpu.PrefetchScalarGridSpec(
        num_scalar_prefetch=1, grid=(1,), in_specs=[slot(0), slot(1), slot(2), slot(3)] + [wspec] * (3 * n),
        out_specs=[pl.BlockSpec((rows, D), lambda i, o: (0, 0))] * (4 * n))
    return _pcall(
        body, name=name, grid_spec=grid_spec, out_shape=[jax.ShapeDtypeStruct((rows, D), F32)] * (4 * n),
        compiler_params=pltpu.CompilerParams(dimension_semantics=("arbitrary",), vmem_limit_bytes=VMEM_BIG),
    )(order, own, got, got, got, *ws, *ms, *vs)


def _place():
    return lax.axis_index("x"), lax.axis_index("y"), lax.axis_index("c")


def _allgather(block, name, after=()):
    rows = block.shape[0]
    split = (rows // 2 + 15) // 16 * 16

    def body(x_ref, out_ref, token, send_sems, recv_sems, local_sem):
        token[...] = jnp.zeros_like(token)
        x, y, c = _place()
        me, sib = (x, y, c), (x, y, 1 - c)
        xn, yn, dg = (1 - x, y), (x, 1 - y), (1 - x, 1 - y)
        lo, hi = pl.ds(0, split), pl.ds(split, rows - split)

        def copy(k, blk, to, part=None, src=None):
            index = 4 * blk[0] + 2 * blk[1] + blk[2]
            view = out_ref.at[index] if part is None else out_ref.at[index, part]
            return pltpu.make_async_remote_copy(
                src_ref=view if src is None else src, dst_ref=view,
                send_sem=send_sems.at[k], recv_sem=recv_sems.at[k], device_id=to, device_id_type=MESH)

        def start(*copies):
            for cp in copies:
                cp.start()
            return list(copies)

        mine = pltpu.make_async_copy(x_ref, out_ref.at[4 * x + 2 * y + c], local_sem)
        mine.start()
        sent = start(copy(0, me, sib, src=x_ref), copy(1, me, (*xn, c), src=x_ref), copy(2, me, (*yn, c), src=x_ref))
        copy(1, (*xn, c), me).wait_recv()
        sent += start(copy(3, (*xn, c), sib), copy(5, (*xn, c), (*yn, c), part=lo))
        copy(2, (*yn, c), me).wait_recv()
        sent += start(copy(4, (*yn, c), sib), copy(6, (*yn, c), (*xn, c), part=hi))
        copy(5, (*dg, c), me, part=lo).wait_recv()
        copy(6, (*dg, c), me, part=hi).wait_recv()
        sent += start(copy(7, (*dg, c), sib))
        for k, blk in ((0, sib), (3, (*xn, 1 - c)), (4, (*yn, 1 - c)), (7, (*dg, 1 - c))):
            copy(k, blk, me).wait_recv()
        for cp in sent:
            cp.wait_send()
        mine.wait()

    return _pcall(
        _behind(body, 1, after), name=name,
        out_shape=[jax.ShapeDtypeStruct((8,) + block.shape, block.dtype), jax.ShapeDtypeStruct((8, 128), F32)],
        in_specs=[pl.BlockSpec(memory_space=pl.ANY)] * (1 + len(after)),
        out_specs=[pl.BlockSpec(memory_space=pl.ANY), pl.BlockSpec(memory_space=pltpu.VMEM)],
        scratch_shapes=[pltpu.SemaphoreType.DMA((8,)), pltpu.SemaphoreType.DMA((8,)), pltpu.SemaphoreType.DMA(())],
        compiler_params=pltpu.CompilerParams(has_side_effects=True),
    )(block, *after)


HBM_SPEC = pl.BlockSpec(memory_space=pltpu.HBM)
SEM_SPEC = pl.BlockSpec(memory_space=pltpu.SEMAPHORE)
ANY_SPEC = pl.BlockSpec(memory_space=pl.ANY)
EFFECT = pltpu.SideEffectType.DATAFLOW_SIDE_EFFECTING


def _in_hbm(a):
    return pltpu.with_memory_space_constraint(a, pltpu.HBM)


def _start_copies(name, src, land_shape, plan, n):
    def body(src_ref, land_ref, send_sems, recv_sems, src_thru, land_thru, token):
        for k, (s, d, to, _) in enumerate(plan(src_ref, land_ref)):
            pltpu.make_async_remote_copy(src_ref=s, dst_ref=d, send_sem=send_sems.at[k], recv_sem=recv_sems.at[k],
                                         device_id=to, device_id_type=MESH).start()
        token[...] = jnp.zeros_like(token)

    return _pcall(
        body, name=name,
        out_shape=(pltpu.SemaphoreType.DMA((n,)), pltpu.SemaphoreType.DMA((n,)), pltpu.HBM(src.shape, src.dtype),
                   pltpu.HBM(land_shape, src.dtype), jax.ShapeDtypeStruct((8, 128), F32)),
        in_specs=(HBM_SPEC, HBM_SPEC),
        out_specs=(SEM_SPEC, SEM_SPEC, HBM_SPEC, HBM_SPEC, pl.BlockSpec(memory_space=pltpu.VMEM)),
        input_output_aliases={0: 2, 1: 3}, compiler_params=pltpu.CompilerParams(has_side_effects=EFFECT),
    )(_in_hbm(src), _in_hbm(lax.empty(land_shape, src.dtype)))


def _wait_copies(name, started, after, plan):
    send_sems, recv_sems, src_thru, land_thru, _ = started

    def body(src_ref, land_ref, send_sems, recv_sems, *rest):
        for k, (s, _, to, mine) in enumerate(plan(src_ref, land_ref)):
            cp = pltpu.make_async_remote_copy(src_ref=s, dst_ref=mine, send_sem=send_sems.at[k],
                                              recv_sem=recv_sems.at[k], device_id=to, device_id_type=MESH)
            cp.wait_send()
            cp.wait_recv()

    return _pcall(
        body, name=name,
        out_shape=(pltpu.HBM(src_thru.shape, src_thru.dtype), pltpu.HBM(land_thru.shape, land_thru.dtype)),
        in_specs=(HBM_SPEC, HBM_SPEC, SEM_SPEC, SEM_SPEC) + (ANY_SPEC,) * len(after), out_specs=(HBM_SPEC, HBM_SPEC),
        input_output_aliases={0: 0, 1: 1}, compiler_params=pltpu.CompilerParams(has_side_effects=EFFECT),
    )(src_thru, land_thru, send_sems, recv_sems, *after)


def _start_inplace(name, buf, plan, n):
    def body(buf_ref, send_sems, recv_sems, buf_thru, token):
        for k, (s, d, to, _) in enumerate(plan(buf_ref, buf_ref)):
            pltpu.make_async_remote_copy(src_ref=s, dst_ref=d, send_sem=send_sems.at[k], recv_sem=recv_sems.at[k],
                                         device_id=to, device_id_type=MESH).start()
        token[...] = jnp.zeros_like(token)

    return _pcall(
        body, name=name,
        out_shape=(pltpu.SemaphoreType.DMA((n,)), pltpu.SemaphoreType.DMA((n,)), pltpu.HBM(buf.shape, buf.dtype),
                   jax.ShapeDtypeStruct((8, 128), F32)),
        in_specs=(HBM_SPEC,), out_specs=(SEM_SPEC, SEM_SPEC, HBM_SPEC, pl.BlockSpec(memory_space=pltpu.VMEM)),
        input_output_aliases={0: 2}, compiler_params=pltpu.CompilerParams(has_side_effects=EFFECT),
    )(_in_hbm(buf))


def _wait_inplace(name, started, after, plan):
    send_sems, recv_sems, buf_thru, _ = started

    def body(buf_ref, send_sems, recv_sems, *rest):
        for k, (s, _, to, mine) in enumerate(plan(buf_ref, buf_ref)):
            cp = pltpu.make_async_remote_copy(src_ref=s, dst_ref=mine, send_sem=send_sems.at[k],
                                              recv_sem=recv_sems.at[k], device_id=to, device_id_type=MESH)
            cp.wait_send()
            cp.wait_recv()

    return _pcall(
        body, name=name, out_shape=pltpu.HBM(buf_thru.shape, buf_thru.dtype),
        in_specs=(HBM_SPEC, SEM_SPEC, SEM_SPEC) + (ANY_SPEC,) * len(after), out_specs=HBM_SPEC,
        input_output_aliases={0: 0}, compiler_params=pltpu.CompilerParams(has_side_effects=EFFECT),
    )(buf_thru, send_sems, recv_sems, *after)


def _gather_plan(src_ref, land_ref):
    x, y, c = _place()
    peers = [(x, y, 1 - c), (1 - x, y, c), (x, 1 - y, c)]
    return [(src_ref, land_ref.at[4 * x + 2 * y + c], p, land_ref.at[4 * p[0] + 2 * p[1] + p[2]]) for p in peers]


def _relay_plan(buf_ref, _):
    x, y, c = _place()
    slot = lambda p, pc: 4 * p[0] + 2 * p[1] + pc
    xn, yn, dg, sib = (1 - x, y), (x, 1 - y), (1 - x, 1 - y), (x, y, 1 - c)
    half = buf_ref.shape[1] // 2
    lo, hi = pl.ds(0, half), pl.ds(half, half)
    return [(buf_ref.at[slot(xn, c)], buf_ref.at[slot(xn, c)], sib, buf_ref.at[slot(xn, 1 - c)]),
            (buf_ref.at[slot(yn, c)], buf_ref.at[slot(yn, c)], sib, buf_ref.at[slot(yn, 1 - c)]),
            (buf_ref.at[slot(xn, c), lo], buf_ref.at[slot(xn, c), lo], (*yn, c), buf_ref.at[slot(dg, c), lo]),
            (buf_ref.at[slot(yn, c), hi], buf_ref.at[slot(yn, c), hi], (*xn, c), buf_ref.at[slot(dg, c), hi])]


def _swap_plan(src_ref, land_ref):
    x, y, c = _place()
    return [(src_ref.at[:, pl.ds(1 - c, 1)], land_ref, (x, y, 1 - c), land_ref)]


def _exchange_plan(src_ref, land_ref):
    x, y, c = _place()
    chips = [(1 - x, y), (x, 1 - y), (1 - x, 1 - y)]
    return [(src_ref.at[2 * px + py], land_ref.at[2 * x + y], (px, py, c), land_ref.at[2 * px + py]) for px, py in chips]


def _gather_forward(land, block):
    def body(land_ref, out_ref, send_sems, recv_sems):
        x, y, c = _place()
        chips = [(1 - x, 1 - y)]

        def copy(k, px, py, pc):
            blk = out_ref.at[4 * px + 2 * py + pc]
            return pltpu.make_async_remote_copy(src_ref=blk, dst_ref=blk, send_sem=send_sems.at[k],
                                                recv_sem=recv_sems.at[k], device_id=(x, y, 1 - c), device_id_type=MESH)

        sent = [copy(k, px, py, c) for k, (px, py) in enumerate(chips)]
        for cp in sent:
            cp.start()
        for k, (px, py) in enumerate(chips):
            copy(k, px, py, 1 - c).wait_recv()
        for cp in sent:
            cp.wait_send()

    land = _pcall(
        body, name="allgather_rest_forward", out_shape=jax.ShapeDtypeStruct(land.shape, land.dtype),
        in_specs=[ANY_SPEC], out_specs=ANY_SPEC, input_output_aliases={0: 0},
        scratch_shapes=[pltpu.SemaphoreType.DMA((1,)), pltpu.SemaphoreType.DMA((1,))],
        compiler_params=pltpu.CompilerParams(has_side_effects=True),
    )(land)

    rows = block.shape[0]
    tr = rows // 4

    def place(me_ref, x_ref, land_ref, out_ref):
        out_ref[...] = x_ref[...]

    x, y, c = _place()
    grid_spec = pltpu.PrefetchScalarGridSpec(
        num_scalar_prefetch=1, grid=(rows // tr,),
        in_specs=[pl.BlockSpec((tr, D), lambda i, me: (i, 0)), ANY_SPEC],
        out_specs=pl.BlockSpec((None, tr, D), lambda i, me: (me[0], i, 0)))
    return _pcall(
        place, name="allgather_rest_own", grid_spec=grid_spec, out_shape=jax.ShapeDtypeStruct(land.shape, land.dtype),
        input_output_aliases={2: 0}, compiler_params=pltpu.CompilerParams(dimension_semantics=("arbitrary",)),
    )((4 * x + 2 * y + c).reshape(1), block, land)


class _ReduceScatter:
    def __init__(self, name, g):
        self.name = name
        rows = g.shape[1]
        self.started = _start_copies(name + "_swap_start", g.reshape(4, 2, rows, D), (4, 1, rows, D), _swap_plan, 1)
        self.token = self.started[4]

    def halfway(self, after):
        g4, theirs = _wait_copies(self.name + "_swap_wait", self.started, after, _swap_plan)
        self.own = _add_halves(g4, theirs, lax.axis_index("c").reshape(1), self.name + "_add_halves")
        self.started = _start_copies(self.name + "_exch_start", self.own, self.own.shape, _exchange_plan, 3)
        self.token = self.started[4]

    def finish(self, after):
        own, got = _wait_copies(self.name + "_exch_wait", self.started, after, _exchange_plan)
        chip = 2 * lax.axis_index("x") + lax.axis_index("y")
        return own, got, (chip + jnp.arange(4, dtype=jnp.int32)) % 4


def _pack_small(p, loss):
    def body(*refs):
        o_ref = refs[-1]
        o_ref[...] = jnp.zeros_like(o_ref)
        for ref, name in zip(refs, SMALL):
            r0, nr, c0, nc = SMALL_SLOT[name]
            o_ref[r0:r0 + nr, c0:c0 + nc] = ref[...]
        o_ref[17:18, 0:128] = refs[len(SMALL)][0:1, :]

    return _one_call(body, "pack_small_grads", [p[n] for n in SMALL] + [loss], [((SMALL_ROWS, D), F32)])[0]


_GAP_DEV, _GAP_ROW = divmod(GATE0 + 8, N_IN)
_GAP = CHK0 - GATE0 - 8


def _in_rows_to_proj(g):
    runs = [(j, 0, N_IN, N_IN * j) for j in range(_GAP_DEV)]
    runs += [(_GAP_DEV, 0, _GAP_ROW, N_IN * _GAP_DEV), (_GAP_DEV, _GAP_ROW, N_IN, N_IN * _GAP_DEV + _GAP_ROW + _GAP)]
    runs += [(j, 0, N_IN, N_IN * j + _GAP) for j in range(_GAP_DEV + 1, 8)]

    def body(g_ref, o_ref, acc_ref):
        acc_ref[...] = jnp.zeros_like(acc_ref)
        for j, r0, r1, dest in runs:
            start, shift = dest // 16 * 16, dest % 16
            win = -(-(shift + r1 - r0) // 16) * 16
            r = lax.broadcasted_iota(jnp.int32, (win, R_IN), 0)
            c = lax.broadcasted_iota(jnp.int32, (win, R_IN), 1)
            move = jnp.where((c >= r0) & (c < r1) & (r == c - r0 + shift), 1.0, 0.0).astype(BF16)
            acc_ref[start:start + win, :] += _nn(move, g_ref[j])
        o_ref[...] = acc_ref[...].astype(BF16)

    return _pcall(
        body, name="w_in_layout", out_shape=jax.ShapeDtypeStruct((PROJ, D), BF16), grid=(1,),
        in_specs=[pl.BlockSpec(g.shape, lambda i: (0, 0, 0))], out_specs=pl.BlockSpec((PROJ, D), lambda i: (0, 0)),
        scratch_shapes=[pltpu.VMEM((PROJ, D), F32)],
        compiler_params=pltpu.CompilerParams(dimension_semantics=("arbitrary",), vmem_limit_bytes=VMEM_BIG),
    )(g)


def _proj_rows_to_in(g):
    pad = lambda a: jnp.pad(a, ((0, R_IN - a.shape[0]), (0, 0)))
    lo = N_IN * _GAP_DEV
    shards = [pad(g[N_IN * j:N_IN * (j + 1)]) for j in range(_GAP_DEV)]
    shards.append(pad(jnp.concatenate([g[lo:lo + _GAP_ROW], g[lo + _GAP_ROW + _GAP:lo + N_IN + _GAP]], axis=0)))
    shards += [pad(g[N_IN * j + _GAP:N_IN * (j + 1) + _GAP]) for j in range(_GAP_DEV + 1, 8)]
    return jnp.stack(shards)


def _local_grads(x, mem, tgt, win_t, gw_of, sm, on_grads, after=()):
    b_pad = jnp.pad(sm['b_fgt'], ((0, 0), (0, 120)))
    tbl = jnp.pad(sm['rel_bias'], ((0, 0), (0, NREL_PAD - 257)))

    h1, proj, flog = _premix_fwd(x, sm['g_mix_pre'], win_t, after)
    c = _gate_fwd(flog, b_pad)
    ct3 = c[:, :8].T.reshape(4, 2, T)
    o_f, lse_f = _fox_fwd(proj, c, ct3)
    vt3 = _relvec_fwd(tbl).reshape(4, 2, VW)
    kvp = jnp.pad(proj[:, CHK0 + 512:], ((LEFT, 0), (0, 0)))
    o_c, lse_c = _chk_fwd(proj, kvp, vt3, [gw_of('relay', [o_f])])
    gw = gw_of('done', [o_c])
    w_out, w_mq, w_mk, w_mv, w_mo, w1_t, w2 = (_wblk(gw, n) for n in ('w_out', 'w_mq', 'w_mk', 'w_mv', 'w_mo', 'w_ff1', 'w_ff2'))
    ycat, z, x1, h2, qm = _postmix_fwd(x, o_f, o_c, sm['g_fox_out'], sm['g_chk_out'], w_out,
                                       sm['g_mix_post'], sm['g_mem_pre'], w_mq)
    memn, km, vm = _memkv_fwd(mem, sm['g_mem_kv'], w_mk, w_mv)
    om, ym, x2, h3 = _mem_fwd(qm, x1, km, vm, w_mo, sm['g_mem_post'], sm['g_ff_pre'])

    gs = {}
    dx2, da, dy3, r, loss_acc, gs['g_ff_post'], gs['g_ff_pre'] = _ffn_step(h3, x2, tgt, w1_t, w2, sm['g_ff_post'],
                                                                         sm['g_ff_pre'])
    tok = on_grads('A', _wgrad_group("wgrad_ff", [(da, h3), (r, dy3)], 512), None)
    dx1, dym, dqm, dkm, dvm, gs['g_mem_post'], gs['g_mem_pre'] = _mem_bwd(
        dx2, ym, x1, qm, km, vm, w_mo, w_mq, sm['g_mem_post'], sm['g_mem_pre'], [tok])
    tok = on_grads('A halfway', None, [dx1])
    gs['g_mem_kv'] = _memkv_bwd(dkm, dvm, mem, w_mk, w_mv)
    dz, dof, doc, gs['g_mix_post'], gs['g_fox_out'], gs['g_chk_out'] = _postmix_bwd(
        dx1, z, o_f, o_c, w_out, sm['g_mix_post'], sm['g_fox_out'], sm['g_chk_out'], [tok])
    tok = on_grads('B', _wgrad_group("wgrad_mem_out", [(ycat, dz), (h2, dqm), (memn, dkm), (memn, dvm), (om, dym)], 128), None)
    dq_f, dk_f, dv_f, dct, dcq = _fox_bwd(proj, c, ct3, o_f, lse_f, dof, [tok])
    tok = on_grads('B halfway', None, [dq_f])
    dq_c, dk_c, dv_c, gv = _chk_bwd(proj, kvp, vt3, o_c, lse_c, doc, [tok])
    gs['rel_bias'] = _relvec_bwd(gv.reshape(8, VW))[:, :257]
    dc = jnp.pad(dct.reshape(8, T).T + dcq[:, :, :2].transpose(1, 0, 2).reshape(T, 8), ((0, 0), (0, 120)))
    dflog, db = _gate_bwd(dc, flog, b_pad)
    gs['b_fgt'] = db[0:1, :8]
    grad_x, dproj, gs['g_mix_pre'] = _premix_bwd(dx1, x, [dq_f, dk_f, dv_f, dflog, dq_c, dk_c, dv_c], win_t, sm['g_mix_pre'])
    on_grads('C', _proj_rows_to_in(_wgrad(dproj, h1, "wgrad_in")), None)
    return loss_acc, grad_x, gs


def kernel(x, mem, w_in, b_fgt, rel_bias, g_fox_out, g_chk_out, w_out, g_mix_pre, g_mix_post, g_mem_kv, w_mq, w_mk, w_mv, w_mo, g_mem_pre, g_mem_post, w_ff1, w_ff2, g_ff_pre, g_ff_post, loss_target, m_w_in, m_b_fgt, m_rel_bias, m_g_fox_out, m_g_chk_out, m_w_out, m_g_mix_pre, m_g_mix_post, m_g_mem_kv, m_w_mq, m_w_mk, m_w_mv, m_w_mo, m_g_mem_pre, m_g_mem_post, m_w_ff1, m_w_ff2, m_g_ff_pre, m_g_ff_post, v_w_in, v_b_fgt, v_rel_bias, v_g_fox_out, v_g_chk_out, v_w_out, v_g_mix_pre, v_g_mix_post, v_g_mem_kv, v_w_mq, v_w_mk, v_w_mv, v_w_mo, v_g_mem_pre, v_g_mem_post, v_w_ff1, v_w_ff2, v_g_ff_pre, v_g_ff_post):
    args = dict(locals())
    two_d = lambda a: a.reshape(a.shape[-2:])
    w = {n: two_d(args[n]) for n in WEIGHTS}
    m = {n: two_d(args['m_' + n]) for n in WEIGHTS}
    v = {n: two_d(args['v_' + n]) for n in WEIGHTS}

    sm = {n: w[n] for n in SMALL}
    shard_in = jnp.pad(w['w_in'].T, ((0, R_IN - N_IN), (0, 0))).astype(BF16)
    gathered_in, zero = _allgather(shard_in, "allgather_w_in")
    win_t = _in_rows_to_proj(gathered_in)
    shard_rest = (jnp.concatenate([w['w_ff1'].T, w['w_ff2'], w['w_out'], w['w_mq'], w['w_mk'], w['w_mv'], w['w_mo']],
                                  axis=0) + zero[0, 0]).astype(BF16)
    gather = {'first': _start_copies("allgather_rest_start", shard_rest, (8, R_REST, D), _gather_plan, 3)}

    def gw_of(stage, after):
        if stage == 'relay':
            gather['block'], land = _wait_copies("allgather_rest_wait", gather['first'], after, _gather_plan)
            gather['second'] = _start_inplace("allgather_rest_relay_start", land, _relay_plan, 4)
            return gather['second'][3]
        land = _wait_inplace("allgather_rest_relay_wait", gather['second'], after, _relay_plan)
        return _gather_forward(land, gather['block'])

    rs = {}

    def on_grads(stage, g, after):
        if stage.endswith('halfway'):
            rs[stage[0]].halfway(after)
            return rs[stage[0]].token
        rs[stage] = _ReduceScatter("rs_" + stage.lower(), g)
        return rs[stage].token

    loss_local, grad_x, gs = _local_grads(x[0], mem[0], loss_target[0], win_t, gw_of, sm, on_grads, [gather['first'][4]])
    grads, deltas, new_m, new_v = {}, {}, {}, {}

    def update(n, out):
        grads[n], deltas[n], new_m[n], new_v[n] = out

    packed = _pack_small(gs, loss_local)
    rs['C'].halfway([packed])
    gparts, _ = _allgather(packed, "allgather_small_grads", [rs['C'].token])
    small = _adamw_small(gparts, [w[n] for n in SMALL], [m[n] for n in SMALL], [v[n] for n in SMALL])
    loss = small[0][0, 0]
    for t, n in enumerate(SMALL):
        update(n, small[1 + 4 * t:5 + 4 * t])

    own, got, order = rs['A'].finish([grad_x, rs['C'].started[4]])
    update('w_ff1', _sum_adam(own, got, order, 0, w['w_ff1'], m['w_ff1'], v['w_ff1'], "adamw_w_ff1", transposed=True))
    update('w_ff2', _sum_adam(own, got, order, 512, w['w_ff2'], m['w_ff2'], v['w_ff2'], "adamw_w_ff2"))
    own, got, order = rs['B'].finish([grad_x, rs['C'].started[4]])
    names_b = ('w_out', 'w_mq', 'w_mk', 'w_mv', 'w_mo')
    done = _sum_adam_rows(own, got, order, [w[n] for n in names_b], [m[n] for n in names_b], [v[n] for n in names_b],
                          "adamw_group_b")
    for k, n in enumerate(names_b):
        update(n, done[4 * k:4 * k + 4])

    own, got, order = rs['C'].finish([new_v[n] for n in BIG if n != 'w_in'])
    done = _sum_adam(own, got, order, 0, w['w_in'].T, m['w_in'].T, v['w_in'].T, "adamw_w_in")
    update('w_in', [a.T for a in done])

    out = [loss, grad_x[None]]
    for group in (grads, deltas, new_m, new_v):
        out += [group[n].reshape(args[n].shape) for n in WEIGHTS]
    return tuple(out)
```

```python
import jax
import jax.numpy as jnp
from jax import lax
from jax.experimental import pallas as pl
from jax.experimental.pallas import tpu as pltpu

F32 = jnp.float32
BF16 = jnp.bfloat16
MESH = pl.DeviceIdType.MESH

T = 2048
D = 1024
NMEM = 256
DFF = 4096
EPS = 1e-6
TM = 256
TM_WIDE = 512
TQ = 256
FQ = 512
HD = 64
SCALE = HD ** -0.5
MEM_HEADS = 4
MEM_HD = 256
MEM_SCALE = MEM_HD ** -0.5
NEG = -1e30
LEFT = 512
WIN = LEFT + TQ
VW = 1024
NREL_PAD = 384
PROJ = 3200
GATE0 = 1536
CHK0 = 1664
VMEM_BIG = 56 * 1024 * 1024

ADAM_LR = 0.001
ADAM_B1 = 0.9
ADAM_B2 = 0.999
ADAM_EPS = 1e-08
ADAM_WD = 0.01
ADAM_STEP = 10

N_IN = 385
R_IN = 400
R_REST = 1664
W_ROWS = {'w_ff1': (0, 512), 'w_ff2': (512, 512),
          'w_out': (1024, 128), 'w_mq': (1152, 128), 'w_mk': (1280, 128), 'w_mv': (1408, 128), 'w_mo': (1536, 128)}
SMALL_ROWS = 24
SMALL_SLOT = {'rel_bias': (0, 8, 0, 257), 'b_fgt': (8, 1, 0, 8), 'g_fox_out': (9, 1, 0, 512), 'g_chk_out': (9, 1, 512, 512),
              'g_mix_pre': (10, 1, 0, 1024), 'g_mix_post': (11, 1, 0, 1024), 'g_mem_kv': (12, 1, 0, 1024),
              'g_mem_pre': (13, 1, 0, 1024), 'g_mem_post': (14, 1, 0, 1024), 'g_ff_pre': (15, 1, 0, 1024),
              'g_ff_post': (16, 1, 0, 1024)}

WEIGHTS = ['w_in', 'b_fgt', 'rel_bias', 'g_fox_out', 'g_chk_out', 'w_out', 'g_mix_pre', 'g_mix_post', 'g_mem_kv',
           'w_mq', 'w_mk', 'w_mv', 'w_mo', 'g_mem_pre', 'g_mem_post', 'w_ff1', 'w_ff2', 'g_ff_pre', 'g_ff_post']
BIG = ['w_in', 'w_out', 'w_mq', 'w_mk', 'w_mv', 'w_mo', 'w_ff1', 'w_ff2']
SMALL = [n for n in WEIGHTS if n not in BIG]


def _pcall(body, **kw):
    return pl.pallas_call(body, **kw)


def _nn(a, b):
    return jnp.dot(a, b, preferred_element_type=F32)


def _nt(a, b):
    return lax.dot_general(a, b, (((1,), (1,)), ((), ())), preferred_element_type=F32)


def _tn(a, b):
    return lax.dot_general(a, b, (((0,), (0,)), ((), ())), preferred_element_type=F32)


def _w(ref):
    v = ref[...]
    return v if v.ndim == 2 else v.reshape(-1, v.shape[-1])


def _rstd(x):
    return lax.rsqrt(jnp.mean(x * x, axis=-1, keepdims=True) + EPS)


def _rms(x, g):
    return x * _rstd(x) * g


def _rms_bwd(x, g, dy):
    r = _rstd(x)
    xh = x * r
    dg = jnp.sum(dy * xh, axis=0, keepdims=True)
    dxh = dy * g
    dx = r * (dxh - xh * jnp.mean(dxh * xh, axis=-1, keepdims=True))
    return dx, dg


def _resident(a):
    if isinstance(a, tuple):
        _, shape, index = a
        return pl.BlockSpec(shape, lambda *_: index, pipeline_mode=pl.Buffered(1))
    return pl.BlockSpec(a.shape, lambda *_, nd=a.ndim: (0,) * nd, pipeline_mode=pl.Buffered(1))


def _wblk(gw, name):
    r0, rows = W_ROWS[name]
    return (gw, (8, rows, D), (0, r0 // rows, 0))


def _behind(body, n_in, after):
    if not after:
        return body
    return lambda *refs: body(*refs[:n_in], *refs[n_in + len(after):])


def _tok_call(body, name, tiled, full, outs_tiled, outs_acc=(), rows=T, tm=TM, vmem=None, after=()):
    in_specs = [pl.BlockSpec((tm, a.shape[1]), lambda i: (i, 0)) for a in tiled]
    in_specs += [_resident(a) for a in full] + [ANY_SPEC] * len(after)
    full = [a[0] if isinstance(a, tuple) else a for a in full] + list(after)
    body = _behind(body, len(tiled) + len(full) - len(after), after)
    out_shape = [jax.ShapeDtypeStruct((rows, c), dt) for c, dt in outs_tiled]
    out_shape += [jax.ShapeDtypeStruct(s, F32) for s in outs_acc]
    out_specs = [pl.BlockSpec((tm, c), lambda i: (i, 0)) for c, _ in outs_tiled]
    out_specs += [pl.BlockSpec(s, lambda i, nd=len(s): (0,) * nd) for s in outs_acc]
    return _pcall(
        body, name=name, grid=(rows // tm,), in_specs=in_specs, out_specs=out_specs, out_shape=out_shape,
        compiler_params=pltpu.CompilerParams(dimension_semantics=("arbitrary",), vmem_limit_bytes=vmem),
    )(*tiled, *full)


def _one_call(body, name, ins, outs, vmem=None):
    whole = lambda s: pl.BlockSpec(s, lambda i, nd=len(s): (0,) * nd)
    return _pcall(
        body, name=name, grid=(1,), in_specs=[_resident(a) for a in ins], out_specs=[whole(s) for s, _ in outs],
        out_shape=[jax.ShapeDtypeStruct(s, dt) for s, dt in outs],
        compiler_params=pltpu.CompilerParams(dimension_semantics=("arbitrary",), vmem_limit_bytes=vmem),
    )(*[a[0] if isinstance(a, tuple) else a for a in ins])


def _premix_fwd(x, g_pre, win_t, after=()):
    def body(x_ref, g_ref, w_ref, h_ref, proj_ref, flog_ref):
        h = _rms(x_ref[...], g_ref[...]).astype(BF16)
        h_ref[...] = h
        p = _nt(h, w_ref[...])
        proj_ref[...] = p.astype(BF16)
        flog_ref[...] = p[:, GATE0:GATE0 + 128]

    return _tok_call(body, "premix_fwd", [x], [g_pre, win_t],
                     [(D, BF16), (PROJ, BF16), (128, F32)], tm=TM_WIDE, vmem=VMEM_BIG, after=after)


def _postmix_fwd(x, o_f, o_c, g_fo, g_co, w_out, g_post, g_mpre, w_mq):
    def body(x_ref, of_ref, oc_ref, gfo_ref, gco_ref, wo_ref, gp_ref, gm_ref, wq_ref,
             y_ref, z_ref, x1_ref, h2_ref, qm_ref):
        y_ref[:, :512] = _rms(of_ref[...], gfo_ref[...]).astype(BF16)
        y_ref[:, 512:] = _rms(oc_ref[...], gco_ref[...]).astype(BF16)
        z = _nn(y_ref[...], _w(wo_ref))
        z_ref[...] = z
        x1 = x_ref[...] + _rms(z, gp_ref[...])
        x1_ref[...] = x1
        h2 = _rms(x1, gm_ref[...]).astype(BF16)
        h2_ref[...] = h2
        qm_ref[...] = _nn(h2, _w(wq_ref)).astype(BF16)

    return _tok_call(body, "postmix_fwd", [x, o_f, o_c], [g_fo, g_co, w_out, g_post, g_mpre, w_mq],
                     [(D, BF16), (D, F32), (D, F32), (D, BF16), (D, BF16)], tm=TM_WIDE, vmem=VMEM_BIG)


def _memkv_fwd(mem, g_kv, w_mk, w_mv):
    def body(m_ref, g_ref, wk_ref, wv_ref, mn_ref, k_ref, v_ref):
        mn = _rms(m_ref[...], g_ref[...]).astype(BF16)
        mn_ref[...] = mn
        k_ref[...] = _nn(mn, _w(wk_ref)).astype(BF16)
        v_ref[...] = _nn(mn, _w(wv_ref)).astype(BF16)

    return _tok_call(body, "memkv_fwd", [mem], [g_kv, w_mk, w_mv],
                     [(D, BF16), (D, BF16), (D, BF16)], rows=NMEM, tm=NMEM, vmem=VMEM_BIG)


def _mem_fwd(qm, x1, km, vm, w_mo, g_post, g_fpre):
    def body(q_ref, x1_ref, k_ref, v_ref, wo_ref, gp_ref, gf_ref, om_ref, ym_ref, x2_ref, h3_ref):
        for h in range(MEM_HEADS):
            sl = slice(h * MEM_HD, (h + 1) * MEM_HD)
            s = _nt(q_ref[:, sl], k_ref[:, sl]) * MEM_SCALE
            p = jnp.exp(s - jnp.max(s, axis=-1, keepdims=True))
            p = p / jnp.sum(p, axis=-1, keepdims=True)
            om_ref[:, sl] = _nn(p.astype(BF16), v_ref[:, sl]).astype(BF16)
        ym = _nn(om_ref[...], _w(wo_ref))
        ym_ref[...] = ym
        x2 = x1_ref[...] + _rms(ym, gp_ref[...])
        x2_ref[...] = x2
        h3_ref[...] = _rms(x2, gf_ref[...]).astype(BF16)

    return _tok_call(body, "mem_fwd", [qm, x1], [km, vm, w_mo, g_post, g_fpre],
                     [(D, BF16), (D, F32), (D, F32), (D, BF16)], tm=TM_WIDE, vmem=VMEM_BIG)


def _tri(lower):
    r = lax.broadcasted_iota(jnp.int32, (128, 128), 0)
    c = lax.broadcasted_iota(jnp.int32, (128, 128), 1)
    return jnp.where(r >= c if lower else c >= r, 1.0, 0.0).astype(F32)


def _hdot(a, b):
    return jnp.dot(a, b, preferred_element_type=F32, precision=lax.Precision.HIGHEST)


def _gate_fwd(flog, b_pad):
    def body(f_ref, b_ref, c_ref):
        tri = _tri(True)

        def step(i, carry):
            rows = pl.ds(pl.multiple_of(i * 128, 128), 128)
            z = f_ref[rows, :] + b_ref[...]
            lf = jnp.minimum(z, 0.0) - jnp.log(1.0 + jnp.exp(-jnp.abs(z)))
            cb = _hdot(tri, lf) + carry
            c_ref[rows, :] = cb
            return cb[127:128, :]

        lax.fori_loop(0, T // 128, step, jnp.zeros((1, 128), F32))

    return _one_call(body, "gate_fwd", [flog, b_pad], [((T, 128), F32)])[0]


def _gate_bwd(dc, flog, b_pad):
    def body(dc_ref, f_ref, b_ref, df_ref, db_ref):
        tri = _tri(False)

        def step(j, carry):
            run, db = carry
            i = T // 128 - 1 - j
            rows = pl.ds(pl.multiple_of(i * 128, 128), 128)
            dcb = dc_ref[rows, :]
            rb = _hdot(tri, dcb) + run
            z = f_ref[rows, :] + b_ref[...]
            df = rb * (1.0 / (1.0 + jnp.exp(z)))
            df_ref[rows, :] = df.astype(BF16)
            return run + jnp.sum(dcb, axis=0, keepdims=True), db + jnp.sum(df, axis=0, keepdims=True)

        _, db = lax.fori_loop(0, T // 128, step, (jnp.zeros((1, 128), F32), jnp.zeros((1, 128), F32)))
        db_ref[...] = jnp.broadcast_to(db, (8, 128))

    return _one_call(body, "gate_bwd", [dc, flog, b_pad], [((T, 128), BF16), ((8, 128), F32)])


def _lane_lo(rows=TQ):
    return lax.broadcasted_iota(jnp.int32, (rows, 128), 1) < HD


def _half(v, lo, a, scale=None):
    keep = lo if a == 0 else jnp.logical_not(lo)
    v = v.astype(F32) if scale is None else v.astype(F32) * scale
    return jnp.where(keep, v, 0.0).astype(BF16)


def _fox_specs():
    return [pl.BlockSpec((FQ, 128), lambda h, i: (i, h)),
            pl.BlockSpec((T, 128), lambda h, i: (0, 4 + h)),
            pl.BlockSpec((T, 128), lambda h, i: (0, 8 + h))]


def _lane_pick(x, at):
    lane = lax.broadcasted_iota(jnp.int32, x.shape, 1)
    return jnp.sum(jnp.where(lane == at, x, 0.0), axis=-1, keepdims=True)


def _fox_fwd(proj, c, ct3):
    def body(q_ref, k_ref, v_ref, c_ref, ct_ref, o_ref, l_ref):
        i = pl.program_id(1)
        lo = _lane_lo(FQ)
        causal = lax.broadcasted_iota(jnp.int32, (FQ, FQ), 1) <= lax.broadcasted_iota(jnp.int32, (FQ, FQ), 0)
        q = q_ref[...]
        qs = [_half(q, lo, a, SCALE) for a in range(2)]
        cqs = [_lane_pick(c_ref[...], 2 * pl.program_id(0) + a) for a in range(2)]

        def tile(off, carry, diagonal):
            kblk = k_ref[pl.ds(off, FQ), :]
            vblk = v_ref[pl.ds(off, FQ), :]
            new = []
            for a in range(2):
                m, l, acc = carry[a]
                s = _nt(qs[a], kblk) + (cqs[a] - ct_ref[a:a + 1, pl.ds(off, FQ)])
                if diagonal:
                    s = jnp.where(causal, s, NEG)
                m2 = jnp.maximum(m, jnp.max(s, axis=-1, keepdims=True))
                p = jnp.exp(s - m2)
                alpha = jnp.exp(m - m2)
                new.append((m2, alpha * l + jnp.sum(p, axis=-1, keepdims=True),
                            alpha * acc + _nn(p.astype(BF16), vblk)))
            return tuple(new)

        init = (jnp.full((FQ, 1), NEG, F32), jnp.zeros((FQ, 1), F32), jnp.zeros((FQ, 128), F32))
        carry = lax.fori_loop(0, i, lambda kb, c: tile(pl.multiple_of(kb * FQ, FQ), c, False), (init, init))
        carry = tile(pl.multiple_of(i * FQ, FQ), carry, True)
        outs = []
        for a in range(2):
            m, l, acc = carry[a]
            outs.append(acc / l)
            l_ref[:, 128 * a:128 * a + 128] = jnp.broadcast_to(m + jnp.log(l), (FQ, 128))
        o_ref[...] = jnp.where(lo, outs[0], outs[1])

    return _pcall(
        body, name="fox_fwd", grid=(4, T // FQ),
        in_specs=_fox_specs() + [pl.BlockSpec((FQ, 128), lambda h, i: (i, 0)),
                                 pl.BlockSpec((None, 2, T), lambda h, i: (h, 0, 0))],
        out_specs=[pl.BlockSpec((FQ, 128), lambda h, i: (i, h)), pl.BlockSpec((FQ, 256), lambda h, i: (i, h))],
        out_shape=[jax.ShapeDtypeStruct((T, 512), F32), jax.ShapeDtypeStruct((T, 1024), F32)],
        compiler_params=pltpu.CompilerParams(dimension_semantics=("arbitrary", "arbitrary"), vmem_limit_bytes=VMEM_BIG),
    )(proj, proj, proj, c, ct3)


def _fox_bwd(proj, c, ct3, o, lse, do, after=()):
    def body(q_ref, k_ref, v_ref, c_ref, ct_ref, o_ref, l_ref, do_ref, dq_ref, dkb_ref, dvb_ref, dct_ref, dcq_ref,
             dk_ref, dv_ref):
        i = pl.program_id(1)

        @pl.when(i == 0)
        def _():
            dk_ref[...] = jnp.zeros_like(dk_ref)
            dv_ref[...] = jnp.zeros_like(dv_ref)
            dct_ref[...] = jnp.zeros_like(dct_ref)

        lo = _lane_lo(FQ)
        causal = lax.broadcasted_iota(jnp.int32, (FQ, FQ), 1) <= lax.broadcasted_iota(jnp.int32, (FQ, FQ), 0)
        q = q_ref[...]
        do_v = do_ref[...]
        prod = do_v * o_ref[...]
        qs = [_half(q, lo, a, SCALE) for a in range(2)]
        dos = [_half(do_v, lo, a) for a in range(2)]
        deltas = [jnp.sum(jnp.where(lo if a == 0 else jnp.logical_not(lo), prod, 0.0), axis=-1, keepdims=True)
                  for a in range(2)]
        cqs = [_lane_pick(c_ref[...], 2 * pl.program_id(0) + a) for a in range(2)]
        las = [l_ref[:, 128 * a:128 * a + 1] for a in range(2)]

        def tile(off, carry, diagonal):
            kblk = k_ref[pl.ds(off, FQ), :]
            vblk = v_ref[pl.ds(off, FQ), :]
            new = []
            dk = jnp.zeros((128, FQ), F32)
            dv = jnp.zeros((128, FQ), F32)
            for a in range(2):
                dq_acc, rs = carry[a]
                s = _nt(qs[a], kblk) + (cqs[a] - ct_ref[a:a + 1, pl.ds(off, FQ)])
                if diagonal:
                    s = jnp.where(causal, s, NEG)
                p = jnp.exp(s - las[a])
                ds = p * (_nt(dos[a], vblk) - deltas[a])
                dsb = ds.astype(BF16)
                dk = dk + _tn(qs[a], dsb)
                dv = dv + _tn(dos[a], p.astype(BF16))
                dct_ref[a:a + 1, pl.ds(off, FQ)] -= jnp.sum(ds, axis=0, keepdims=True)
                new.append((dq_acc + _nn(dsb, kblk), rs + jnp.sum(ds, axis=-1, keepdims=True)))
            dk_ref[:, pl.ds(off, FQ)] += dk
            dv_ref[:, pl.ds(off, FQ)] += dv
            return tuple(new)

        init = (jnp.zeros((FQ, 128), F32), jnp.zeros((FQ, 1), F32))
        carry = lax.fori_loop(0, i, lambda kb, c: tile(pl.multiple_of(kb * FQ, FQ), c, False), (init, init))
        carry = tile(pl.multiple_of(i * FQ, FQ), carry, True)
        lane = lax.broadcasted_iota(jnp.int32, (FQ, 128), 1)
        dcq_ref[...] = jnp.where(lane == 0, carry[0][1], jnp.where(lane == 1, carry[1][1], 0.0))
        dq_ref[...] = (jnp.where(lo, carry[0][0], carry[1][0]) * SCALE).astype(BF16)

        @pl.when(i == T // FQ - 1)
        def _():
            dkb_ref[...] = dk_ref[...].T.astype(BF16)
            dvb_ref[...] = dv_ref[...].T.astype(BF16)

    blk = pl.BlockSpec((FQ, 128), lambda h, i: (i, h))
    wide = pl.BlockSpec((FQ, 256), lambda h, i: (i, h))
    rows = pl.BlockSpec((None, 2, T), lambda h, i: (h, 0, 0))
    col = pl.BlockSpec((T, 128), lambda h, i: (0, h))
    return _pcall(
        _behind(body, 8, after), name="fox_bwd", grid=(4, T // FQ),
        in_specs=_fox_specs() + [pl.BlockSpec((FQ, 128), lambda h, i: (i, 0)), rows, blk, wide, blk] + [ANY_SPEC] * len(after),
        out_specs=[blk, col, col, rows, pl.BlockSpec((None, FQ, 128), lambda h, i: (h, i, 0))],
        out_shape=[jax.ShapeDtypeStruct((T, 512), BF16), jax.ShapeDtypeStruct((T, 512), BF16),
                   jax.ShapeDtypeStruct((T, 512), BF16), jax.ShapeDtypeStruct((4, 2, T), F32),
                   jax.ShapeDtypeStruct((4, T, 128), F32)],
        scratch_shapes=[pltpu.VMEM((128, T), F32), pltpu.VMEM((128, T), F32)],
        compiler_params=pltpu.CompilerParams(dimension_semantics=("arbitrary", "arbitrary"), vmem_limit_bytes=VMEM_BIG),
    )(proj, proj, proj, c, ct3, o, lse, do, *after)


def _rel_onehot():
    ridx = lax.broadcasted_iota(jnp.int32, (NREL_PAD, VW), 0)
    j = lax.broadcasted_iota(jnp.int32, (NREL_PAD, VW), 1)
    return jnp.where(ridx == jnp.clip(TQ + LEFT - 1 - j, -128, 128) + 128, 1.0, 0.0).astype(F32)


def _relvec_fwd(tbl):
    def body(t_ref, v_ref):
        v_ref[...] = _hdot(t_ref[...], _rel_onehot())

    return _one_call(body, "relvec_fwd", [tbl], [((8, VW), F32)])[0]


def _relvec_bwd(gv):
    def body(g_ref, t_ref):
        t_ref[...] = lax.dot_general(g_ref[...], _rel_onehot(), (((1,), (1,)), ((), ())),
                                     preferred_element_type=F32, precision=lax.Precision.HIGHEST)

    return _one_call(body, "relvec_bwd", [gv], [((8, NREL_PAD), F32)])[0]


def _chk_bias(vt_ref, a, hidden):
    vb = jnp.broadcast_to(vt_ref[a:a + 1, :], (TQ, VW))
    y = pltpu.roll(vb, VW - (TQ - 1), 1, stride=1, stride_axis=0)[:, :WIN]
    cr = lax.broadcasted_iota(jnp.int32, (TQ, WIN), 0) // 64
    m = lax.broadcasted_iota(jnp.int32, (TQ, WIN), 1)
    return jnp.where((m // 64 >= cr) & (m // 64 <= cr + 8) & (m >= hidden), y, NEG)


def _chk_specs():
    return [pl.BlockSpec((TQ, 128), lambda h, i: (i, CHK0 // 128 + h)),
            pl.BlockSpec((T + LEFT, 128), lambda h, i: (0, h)),
            pl.BlockSpec((T + LEFT, 128), lambda h, i: (0, 4 + h)),
            pl.BlockSpec((None, 2, VW), lambda h, i: (h, 0, 0))]


def _chk_fwd(proj, kvp, vt3, after=()):
    def body(q_ref, k_ref, v_ref, vt_ref, o_ref, l_ref, bias_ref):
        i = pl.program_id(1)

        @pl.when(i == 0)
        def _():
            for first in range(3):
                for a in range(2):
                    bias_ref[first, a] = _chk_bias(vt_ref, a, max(LEFT - first * TQ, 0))

        lo = _lane_lo()
        off = pl.multiple_of(i * TQ, TQ)
        kw = k_ref[pl.ds(off, WIN), :]
        vw = v_ref[pl.ds(off, WIN), :]
        bias_at = jnp.minimum(i, 2)
        q = q_ref[...]
        outs = []
        for a in range(2):
            s = _nt(_half(q, lo, a, SCALE), kw) + bias_ref[bias_at, a]
            m = jnp.max(s, axis=-1, keepdims=True)
            p = jnp.exp(s - m)
            l = jnp.sum(p, axis=-1, keepdims=True)
            outs.append(_nn(p.astype(BF16), vw) / l)
            l_ref[:, 128 * a:128 * a + 128] = jnp.broadcast_to(m + jnp.log(l), (TQ, 128))
        o_ref[...] = jnp.where(lo, outs[0], outs[1])

    return _pcall(
        _behind(body, 4, after), name="chk_fwd", grid=(4, T // TQ), in_specs=_chk_specs() + [ANY_SPEC] * len(after),
        out_specs=[pl.BlockSpec((TQ, 128), lambda h, i: (i, h)), pl.BlockSpec((TQ, 256), lambda h, i: (i, h))],
        out_shape=[jax.ShapeDtypeStruct((T, 512), F32), jax.ShapeDtypeStruct((T, 1024), F32)],
        scratch_shapes=[pltpu.VMEM((3, 2, TQ, WIN), F32)],
        compiler_params=pltpu.CompilerParams(dimension_semantics=("arbitrary", "arbitrary")),
    )(proj, kvp, kvp, vt3, *after)


def _chk_bwd(proj, kvp, vt3, o, lse, do, after=()):
    nq = T // TQ

    def body(q_ref, k_ref, v_ref, vt_ref, o_ref, l_ref, do_ref, dq_ref, dkb_ref, dvb_ref, gv_ref, bias_ref, dsum_ref,
             dk_ref, dv_ref):
        i = pl.program_id(1)

        @pl.when(i == 0)
        def _():
            for first in range(3):
                for a in range(2):
                    bias_ref[first, a] = _chk_bias(vt_ref, a, max(LEFT - first * TQ, 0))
            dsum_ref[...] = jnp.zeros_like(dsum_ref)
            dk_ref[...] = jnp.zeros_like(dk_ref)
            dv_ref[...] = jnp.zeros_like(dv_ref)

        lo = _lane_lo()
        off = pl.multiple_of(i * TQ, TQ)
        kw = k_ref[pl.ds(off, WIN), :]
        vw = v_ref[pl.ds(off, WIN), :]
        bias_at = jnp.minimum(i, 2)
        q = q_ref[...]
        do_v = do_ref[...]
        prod = do_v * o_ref[...]
        dqs = []
        for a in range(2):
            keep = lo if a == 0 else jnp.logical_not(lo)
            qa = _half(q, lo, a, SCALE)
            doa = _half(do_v, lo, a)
            delta = jnp.sum(jnp.where(keep, prod, 0.0), axis=-1, keepdims=True)
            s = _nt(qa, kw) + bias_ref[bias_at, a]
            p = jnp.exp(s - l_ref[:, 128 * a:128 * a + 1])
            ds = p * (_nt(doa, vw) - delta)
            dsum_ref[a] += ds
            dsb = ds.astype(BF16)
            dk_ref[:, pl.ds(off, WIN)] += _tn(qa, dsb)
            dv_ref[:, pl.ds(off, WIN)] += _tn(doa, p.astype(BF16))
            dqs.append(_nn(dsb, kw))
        dq_ref[...] = (jnp.where(lo, dqs[0], dqs[1]) * SCALE).astype(BF16)

        @pl.when(i == nq - 1)
        def _():
            dkb_ref[...] = dk_ref[:, LEFT:].T.astype(BF16)
            dvb_ref[...] = dv_ref[:, LEFT:].T.astype(BF16)
            rr = lax.broadcasted_iota(jnp.int32, (TQ, TQ), 0)
            cc = lax.broadcasted_iota(jnp.int32, (TQ, TQ), 1)
            flip = jnp.where(rr + cc == TQ - 1, 1.0, 0.0).astype(F32)
            for a in range(2):
                dpad = jnp.concatenate([dsum_ref[a], jnp.zeros((TQ, VW - WIN), F32)], axis=1)
                z = pltpu.roll(_hdot(flip, dpad), 0, 1, stride=1, stride_axis=0)
                gv_ref[a:a + 1, :] = jnp.sum(z, axis=0, keepdims=True)

    blk = pl.BlockSpec((TQ, 128), lambda h, i: (i, h))
    wide = pl.BlockSpec((TQ, 256), lambda h, i: (i, h))
    col = pl.BlockSpec((T, 128), lambda h, i: (0, h))
    return _pcall(
        _behind(body, 7, after), name="chk_bwd", grid=(4, nq), in_specs=_chk_specs() + [blk, wide, blk] + [ANY_SPEC] * len(after),
        out_specs=[blk, col, col, pl.BlockSpec((None, 2, VW), lambda h, i: (h, 0, 0))],
        out_shape=[jax.ShapeDtypeStruct((T, 512), BF16), jax.ShapeDtypeStruct((T, 512), BF16),
                   jax.ShapeDtypeStruct((T, 512), BF16), jax.ShapeDtypeStruct((4, 2, VW), F32)],
        scratch_shapes=[pltpu.VMEM((3, 2, TQ, WIN), F32), pltpu.VMEM((2, TQ, WIN), F32),
                        pltpu.VMEM((128, T + LEFT), F32), pltpu.VMEM((128, T + LEFT), F32)],
        compiler_params=pltpu.CompilerParams(dimension_semantics=("arbitrary", "arbitrary")),
    )(proj, kvp, kvp, vt3, o, lse, do, *after)


def _zero_at_start(*refs):
    @pl.when(pl.program_id(0) == 0)
    def _():
        for r in refs:
            r[...] = jnp.zeros_like(r)


def _ffn_step(h3, x2, tgt, w1_t, w2, g_post, g_pre):
    def body(h_ref, x2_ref, t_ref, w1_ref, w2_ref, gp_ref, gf_ref, dx2_ref, da_ref, dy_ref, r_ref, loss_ref, dgp_ref, dgf_ref):
        _zero_at_start(loss_ref, dgp_ref, dgf_ref)
        w1, w2v = _w(w1_ref), _w(w2_ref)
        ra = jnp.maximum(_nt(h_ref[...], w1), 0.0)
        r = jnp.square(ra).astype(BF16)
        r_ref[...] = r
        y = _nn(r, w2v)
        x2v = x2_ref[...]
        e = x2v + _rms(y, gp_ref[...]) - t_ref[...]
        loss_ref[...] += 0.5 * jnp.sum(jnp.sum(e * e, axis=-1, keepdims=True) * (1.0 / D))
        dx3 = e * (1.0 / D)
        dy, dgp = _rms_bwd(y, gp_ref[...], dx3)
        dgp_ref[...] += dgp
        dyb = dy.astype(BF16)
        dy_ref[...] = dyb
        da = (_nt(dyb, w2v) * (2.0 * ra)).astype(BF16)
        da_ref[...] = da
        dh, dgf = _rms_bwd(x2v, gf_ref[...], _nn(da, w1))
        dgf_ref[...] += dgf
        dx2_ref[...] = dx3 + dh

    return _tok_call(body, "ffn_step", [h3, x2, tgt], [w1_t, w2, g_post, g_pre],
                     [(D, F32), (DFF, BF16), (D, BF16), (DFF, BF16)], [(8, 128), (1, D), (1, D)], vmem=VMEM_BIG)


def _mem_bwd(dx2, ym, x1, qm, km, vm, w_mo, w_mq, g_post, g_pre, after=()):
    def body(dx2_ref, ym_ref, x1_ref, q_ref, k_ref, v_ref, wo_ref, wq_ref, gp_ref, gm_ref,
             dx1_ref, dym_ref, dq_ref, dk_ref, dv_ref, dgp_ref, dgm_ref, dom_ref):
        _zero_at_start(dk_ref, dv_ref, dgp_ref, dgm_ref)
        dx2_v = dx2_ref[...]
        dym, dgp = _rms_bwd(ym_ref[...], gp_ref[...], dx2_v)
        dgp_ref[...] += dgp
        dymb = dym.astype(BF16)
        dym_ref[...] = dymb
        dom_ref[...] = _nt(dymb, _w(wo_ref)).astype(BF16)
        for h in range(MEM_HEADS):
            sl = slice(h * MEM_HD, (h + 1) * MEM_HD)
            qh, kh, doh = q_ref[:, sl], k_ref[:, sl], dom_ref[:, sl]
            s = _nt(qh, kh) * MEM_SCALE
            p = jnp.exp(s - jnp.max(s, axis=-1, keepdims=True))
            p = p / jnp.sum(p, axis=-1, keepdims=True)
            dp = _nt(doh, v_ref[:, sl])
            ds = (p * (dp - jnp.sum(p * dp, axis=-1, keepdims=True))).astype(BF16)
            dq_ref[:, sl] = (_nn(ds, kh) * MEM_SCALE).astype(BF16)
            dk_ref[:, sl] += _tn(ds, qh) * MEM_SCALE
            dv_ref[:, sl] += _tn(p.astype(BF16), doh)
        dh, dgm = _rms_bwd(x1_ref[...], gm_ref[...], _nt(dq_ref[...], _w(wq_ref)))
        dgm_ref[...] += dgm
        dx1_ref[...] = dx2_v + dh

    tiled = pl.BlockSpec((TM_WIDE, D), lambda i: (i, 0))
    in_specs = [tiled] * 4 + [_resident(a) for a in (km, vm, w_mo, w_mq, g_post, g_pre)] + [ANY_SPEC] * len(after)
    w_mo, w_mq = w_mo[0], w_mq[0]
    kv = pl.BlockSpec((NMEM, D), lambda i: (0, 0))
    vec = pl.BlockSpec((1, D), lambda i: (0, 0))
    return _pcall(
        _behind(body, 10, after), name="mem_bwd", grid=(T // TM_WIDE,), in_specs=in_specs,
        out_specs=[tiled, tiled, tiled, kv, kv, vec, vec],
        out_shape=[jax.ShapeDtypeStruct((T, D), F32), jax.ShapeDtypeStruct((T, D), BF16),
                   jax.ShapeDtypeStruct((T, D), BF16), jax.ShapeDtypeStruct((NMEM, D), F32),
                   jax.ShapeDtypeStruct((NMEM, D), F32), jax.ShapeDtypeStruct((1, D), F32),
                   jax.ShapeDtypeStruct((1, D), F32)],
        scratch_shapes=[pltpu.VMEM((TM_WIDE, D), BF16)],
        compiler_params=pltpu.CompilerParams(dimension_semantics=("arbitrary",), vmem_limit_bytes=VMEM_BIG),
    )(dx2, ym, x1, qm, km, vm, w_mo, w_mq, g_post, g_pre, *after)


def _memkv_bwd(dkm, dvm, mem, w_mk, w_mv):
    def body(dk_ref, dv_ref, m_ref, wk_ref, wv_ref, dg_ref):
        dmn = _nt(dk_ref[...].astype(BF16), _w(wk_ref)) + _nt(dv_ref[...].astype(BF16), _w(wv_ref))
        mv = m_ref[...]
        dg_ref[...] = jnp.sum(dmn * (mv * _rstd(mv)), axis=0, keepdims=True)

    return _one_call(body, "memkv_bwd", [dkm, dvm, mem, w_mk, w_mv], [((1, D), F32)], vmem=VMEM_BIG)[0]


def _postmix_bwd(dx1, z, o_f, o_c, w_out, g_post, g_fo, g_co, after=()):
    def body(dx1_ref, z_ref, of_ref, oc_ref, wo_ref, gp_ref, gfo_ref, gco_ref,
             dz_ref, dof_ref, doc_ref, dgp_ref, dgfo_ref, dgco_ref):
        _zero_at_start(dgp_ref, dgfo_ref, dgco_ref)
        dz, dgp = _rms_bwd(z_ref[...], gp_ref[...], dx1_ref[...])
        dgp_ref[...] += dgp
        dzb = dz.astype(BF16)
        dz_ref[...] = dzb
        dy = _nt(dzb, _w(wo_ref))
        dof, dgfo = _rms_bwd(of_ref[...], gfo_ref[...], dy[:, :512])
        doc, dgco = _rms_bwd(oc_ref[...], gco_ref[...], dy[:, 512:])
        dof_ref[...] = dof
        doc_ref[...] = doc
        dgfo_ref[...] += dgfo
        dgco_ref[...] += dgco

    return _tok_call(body, "postmix_bwd", [dx1, z, o_f, o_c], [w_out, g_post, g_fo, g_co],
                     [(D, BF16), (512, F32), (512, F32)], [(1, D), (1, 512), (1, 512)], tm=TM_WIDE, vmem=VMEM_BIG,
                     after=after)


def _premix_bwd(dx1, x, pieces, win_t, g_pre):
    def body(dx1_ref, x_ref, *refs):
        piece_refs, (w_ref, g_ref, dx_ref, dp_ref, dg_ref) = refs[:len(pieces)], refs[len(pieces):]
        _zero_at_start(dg_ref)
        col = 0
        for p in piece_refs:
            dp_ref[:, col:col + p.shape[1]] = p[...]
            col += p.shape[1]
        dh, dg = _rms_bwd(x_ref[...], g_ref[...], _nn(dp_ref[...], w_ref[...]))
        dg_ref[...] += dg
        dx_ref[...] = dx1_ref[...] + dh

    return _tok_call(body, "premix_bwd", [dx1, x] + list(pieces), [win_t, g_pre], [(D, F32), (PROJ, BF16)], [(1, D)],
                     tm=TM_WIDE, vmem=VMEM_BIG)


def _wgrad(a, b, name):
    k, m = a.shape
    n = b.shape[1]
    tm = 640 if m % 640 == 0 and m > 1024 else min(m, 512)
    tn = min(n, 1024)

    def body(a_ref, b_ref, o_ref):
        o_ref[...] = _tn(a_ref[...].astype(BF16), b_ref[...].astype(BF16))

    return _pcall(
        body, name=name, grid=(m // tm, n // tn),
        in_specs=[pl.BlockSpec((k, tm), lambda i, j: (0, i)), pl.BlockSpec((k, tn), lambda i, j: (0, j))],
        out_specs=pl.BlockSpec((tm, tn), lambda i, j: (i, j)),
        out_shape=jax.ShapeDtypeStruct((m, n), F32),
        compiler_params=pltpu.CompilerParams(dimension_semantics=("arbitrary", "arbitrary"), vmem_limit_bytes=VMEM_BIG),
    )(a, b)


def _wgrad_group(name, pairs, rows):
    def body(*refs):
        o_ref = refs[-1]
        for k in range(len(pairs)):
            o_ref[k * rows:(k + 1) * rows, :] = _tn(refs[2 * k][...].astype(BF16), refs[2 * k + 1][...].astype(BF16))

    in_specs, ops = [], []
    for a, b in pairs:
        in_specs += [pl.BlockSpec((a.shape[0], rows), lambda j: (0, j)), _resident(b)]
        ops += [a, b]
    return _pcall(
        body, name=name, grid=(8,), in_specs=in_specs,
        out_specs=pl.BlockSpec((None, len(pairs) * rows, D), lambda j: (j, 0, 0)),
        out_shape=jax.ShapeDtypeStruct((8, len(pairs) * rows, D), F32),
        compiler_params=pltpu.CompilerParams(dimension_semantics=("arbitrary",), vmem_limit_bytes=VMEM_BIG),
    )(*ops)


def _adam_math(w, g, m, v):
    m2 = ADAM_B1 * m + (1.0 - ADAM_B1) * g
    v2 = ADAM_B2 * v + (1.0 - ADAM_B2) * jnp.square(g)
    m_hat = m2 / (1.0 - ADAM_B1 ** ADAM_STEP)
    v_hat = v2 / (1.0 - ADAM_B2 ** ADAM_STEP)
    delta = -ADAM_LR * (m_hat / (jnp.sqrt(v_hat) + ADAM_EPS) + ADAM_WD * w)
    return delta, m2, v2


def _adamw_small(gparts, ws, ms, vs):
    n = len(SMALL)

    def body(g_ref, *refs):
        w_refs, m_refs, v_refs = refs[:n], refs[n:2 * n], refs[2 * n:3 * n]
        outs, sum_ref = refs[3 * n:-1], refs[-1]
        g = g_ref[0]
        for k in range(1, 8):
            g = g + g_ref[k]
        sum_ref[...] = g
        outs[0][...] = sum_ref[17:18, 0:128]
        for t, name in enumerate(SMALL):
            r0, nr, c0, nc = SMALL_SLOT[name]
            gt = sum_ref[r0:r0 + nr, c0:c0 + nc]
            out = (gt,) + _adam_math(w_refs[t][...], gt, m_refs[t][...], v_refs[t][...])
            for o_ref, val in zip(outs[1 + 4 * t:5 + 4 * t], out):
                o_ref[...] = val

    whole = lambda s: pl.BlockSpec(s, lambda i, nd=len(s): (0,) * nd)
    ins = [gparts] + list(ws) + list(ms) + list(vs)
    out_shapes = [(1, 128)] + [a.shape for a in ws for _ in range(4)]
    return _pcall(
        body, name="adamw_small", grid=(1,), in_specs=[whole(a.shape) for a in ins],
        out_specs=[whole(s) for s in out_shapes], out_shape=[jax.ShapeDtypeStruct(s, F32) for s in out_shapes],
        scratch_shapes=[pltpu.VMEM((SMALL_ROWS, D), F32)],
        compiler_params=pltpu.CompilerParams(dimension_semantics=("arbitrary",)),
    )(*ins)


def _row_tile(rows):
    return next(t for t in (512, 400, 320) if rows % t == 0)


def _add_halves(g4, theirs, core, name):
    rows = g4.shape[2]
    tr = _row_tile(rows)

    def body(c_ref, a_ref, b_ref, o_ref):
        o_ref[...] = (a_ref[...] + b_ref[...]).astype(BF16)

    grid_spec = pltpu.PrefetchScalarGridSpec(
        num_scalar_prefetch=1, grid=(4, rows // tr),
        in_specs=[pl.BlockSpec((None, None, tr, D), lambda j, i, c: (j, c[0], i, 0)),
                  pl.BlockSpec((None, None, tr, D), lambda j, i, c: (j, 0, i, 0))],
        out_specs=pl.BlockSpec((None, tr, D), lambda j, i, c: (j, i, 0)))
    return _pcall(
        body, name=name, grid_spec=grid_spec, out_shape=jax.ShapeDtypeStruct((4, rows, D), BF16),
        compiler_params=pltpu.CompilerParams(dimension_semantics=("arbitrary", "arbitrary")),
    )(core, g4, theirs)


def _sum_adam(own, got, order, r0, w, m, v, name, transposed=False):
    n = w.shape[1] if transposed else w.shape[0]
    tr = min(n, 256) if n % 8 == 0 else n
    rows = tr if n % 8 == 0 else own.shape[1]

    def body(o_ref, a_ref, b_ref, c_ref, d_ref, w_ref, m_ref, v_ref, g_ref, dl_ref, m2_ref, v2_ref):
        f = lambda r: r[0:tr, :].astype(F32)
        g = ((f(a_ref) + f(b_ref)) + f(c_ref)) + f(d_ref)
        g = g.T if transposed else g
        g_ref[...] = g
        dl_ref[...], m2_ref[...], v2_ref[...] = _adam_math(w_ref[...], g, m_ref[...], v_ref[...])

    slot = lambda k: pl.BlockSpec((None, rows, D), lambda i, o: (o[k], r0 // rows + i, 0))
    wspec = pl.BlockSpec((D, tr), lambda i, o: (0, i)) if transposed else pl.BlockSpec((tr, D), lambda i, o: (i, 0))
    grid_spec = pltpu.PrefetchScalarGridSpec(
        num_scalar_prefetch=1, grid=(n // tr,), in_specs=[slot(0), slot(1), slot(2), slot(3), wspec, wspec, wspec],
        out_specs=[wspec] * 4)
    return _pcall(
        body, name=name, grid_spec=grid_spec, out_shape=[jax.ShapeDtypeStruct(w.shape, F32)] * 4,
        compiler_params=pltpu.CompilerParams(dimension_semantics=("arbitrary",)),
    )(order, own, got, got, got, w, m, v)


def _sum_adam_rows(own, got, order, ws, ms, vs, name):
    n, rows = len(ws), ws[0].shape[0]

    def body(o_ref, a_ref, b_ref, c_ref, d_ref, *refs):
        ins, outs = refs[:3 * n], refs[3 * n:]
        for t in range(n):
            r = slice(t * rows, (t + 1) * rows)
            f = lambda ref: ref[r, :].astype(F32)
            g = ((f(a_ref) + f(b_ref)) + f(c_ref)) + f(d_ref)
            out = (g,) + _adam_math(ins[t][...], g, ins[n + t][...], ins[2 * n + t][...])
            for o, val in zip(outs[4 * t:4 * t + 4], out):
                o[...] = val

    slot = lambda k: pl.BlockSpec((None, n * rows, D), lambda i, o: (o[k], 0, 0), pipeline_mode=pl.Buffered(1))
    wspec = pl.BlockSpec((rows, D), lambda i, o: (0, 0), pipeline_mode=pl.Buffered(1))
    grid_spec = pltpu.PrefetchScalarGridSpec(
        num_scalar_prefetch=1, grid=(1,), in_specs=[slot(0), slot(1), slot(2), slot(3)] + [wspec] * (3 * n),
        out_specs=[pl.BlockSpec((rows, D), lambda i, o: (0, 0))] * (4 * n))
    return _pcall(
        body, name=name, grid_spec=grid_spec, out_shape=[jax.ShapeDtypeStruct((rows, D), F32)] * (4 * n),
        compiler_params=pltpu.CompilerParams(dimension_semantics=("arbitrary",), vmem_limit_bytes=VMEM_BIG),
    )(order, own, got, got, got, *ws, *ms, *vs)


def _place():
    return lax.axis_index("x"), lax.axis_index("y"), lax.axis_index("c")


def _allgather(block, name, after=()):
    rows = block.shape[0]
    split = (rows // 2 + 15) // 16 * 16

    def body(x_ref, out_ref, token, send_sems, recv_sems, local_sem):
        token[...] = jnp.zeros_like(token)
        x, y, c = _place()
        me, sib = (x, y, c), (x, y, 1 - c)
        xn, yn, dg = (1 - x, y), (x, 1 - y), (1 - x, 1 - y)
        lo, hi = pl.ds(0, split), pl.ds(split, rows - split)

        def copy(k, blk, to, part=None, src=None):
            index = 4 * blk[0] + 2 * blk[1] + blk[2]
            view = out_ref.at[index] if part is None else out_ref.at[index, part]
            return pltpu.make_async_remote_copy(
                src_ref=view if src is None else src, dst_ref=view,
                send_sem=send_sems.at[k], recv_sem=recv_sems.at[k], device_id=to, device_id_type=MESH)

        def start(*copies):
            for cp in copies:
                cp.start()
            return list(copies)

        mine = pltpu.make_async_copy(x_ref, out_ref.at[4 * x + 2 * y + c], local_sem)
        mine.start()
        sent = start(copy(0, me, sib, src=x_ref), copy(1, me, (*xn, c), src=x_ref), copy(2, me, (*yn, c), src=x_ref))
        copy(1, (*xn, c), me).wait_recv()
        sent += start(copy(3, (*xn, c), sib), copy(5, (*xn, c), (*yn, c), part=lo))
        copy(2, (*yn, c), me).wait_recv()
        sent += start(copy(4, (*yn, c), sib), copy(6, (*yn, c), (*xn, c), part=hi))
        copy(5, (*dg, c), me, part=lo).wait_recv()
        copy(6, (*dg, c), me, part=hi).wait_recv()
        sent += start(copy(7, (*dg, c), sib))
        for k, blk in ((0, sib), (3, (*xn, 1 - c)), (4, (*yn, 1 - c)), (7, (*dg, 1 - c))):
            copy(k, blk, me).wait_recv()
        for cp in sent:
            cp.wait_send()
        mine.wait()

    return _pcall(
        _behind(body, 1, after), name=name,
        out_shape=[jax.ShapeDtypeStruct((8,) + block.shape, block.dtype), jax.ShapeDtypeStruct((8, 128), F32)],
        in_specs=[pl.BlockSpec(memory_space=pl.ANY)] * (1 + len(after)),
        out_specs=[pl.BlockSpec(memory_space=pl.ANY), pl.BlockSpec(memory_space=pltpu.VMEM)],
        scratch_shapes=[pltpu.SemaphoreType.DMA((8,)), pltpu.SemaphoreType.DMA((8,)), pltpu.SemaphoreType.DMA(())],
        compiler_params=pltpu.CompilerParams(has_side_effects=True),
    )(block, *after)


HBM_SPEC = pl.BlockSpec(memory_space=pltpu.HBM)
SEM_SPEC = pl.BlockSpec(memory_space=pltpu.SEMAPHORE)
ANY_SPEC = pl.BlockSpec(memory_space=pl.ANY)
EFFECT = pltpu.SideEffectType.DATAFLOW_SIDE_EFFECTING


def _in_hbm(a):
    return pltpu.with_memory_space_constraint(a, pltpu.HBM)


def _start_copies(name, src, land_shape, plan, n):
    def body(src_ref, land_ref, send_sems, recv_sems, src_thru, land_thru, token):
        for k, (s, d, to, _) in enumerate(plan(src_ref, land_ref)):
            pltpu.make_async_remote_copy(src_ref=s, dst_ref=d, send_sem=send_sems.at[k], recv_sem=recv_sems.at[k],
                                         device_id=to, device_id_type=MESH).start()
        token[...] = jnp.zeros_like(token)

    return _pcall(
        body, name=name,
        out_shape=(pltpu.SemaphoreType.DMA((n,)), pltpu.SemaphoreType.DMA((n,)), pltpu.HBM(src.shape, src.dtype),
                   pltpu.HBM(land_shape, src.dtype), jax.ShapeDtypeStruct((8, 128), F32)),
        in_specs=(HBM_SPEC, HBM_SPEC),
        out_specs=(SEM_SPEC, SEM_SPEC, HBM_SPEC, HBM_SPEC, pl.BlockSpec(memory_space=pltpu.VMEM)),
        input_output_aliases={0: 2, 1: 3}, compiler_params=pltpu.CompilerParams(has_side_effects=EFFECT),
    )(_in_hbm(src), _in_hbm(lax.empty(land_shape, src.dtype)))


def _wait_copies(name, started, after, plan):
    send_sems, recv_sems, src_thru, land_thru, _ = started

    def body(src_ref, land_ref, send_sems, recv_sems, *rest):
        for k, (s, _, to, mine) in enumerate(plan(src_ref, land_ref)):
            cp = pltpu.make_async_remote_copy(src_ref=s, dst_ref=mine, send_sem=send_sems.at[k],
                                              recv_sem=recv_sems.at[k], device_id=to, device_id_type=MESH)
            cp.wait_send()
            cp.wait_recv()

    return _pcall(
        body, name=name,
        out_shape=(pltpu.HBM(src_thru.shape, src_thru.dtype), pltpu.HBM(land_thru.shape, land_thru.dtype)),
        in_specs=(HBM_SPEC, HBM_SPEC, SEM_SPEC, SEM_SPEC) + (ANY_SPEC,) * len(after), out_specs=(HBM_SPEC, HBM_SPEC),
        input_output_aliases={0: 0, 1: 1}, compiler_params=pltpu.CompilerParams(has_side_effects=EFFECT),
    )(src_thru, land_thru, send_sems, recv_sems, *after)


def _start_inplace(name, buf, plan, n):
    def body(buf_ref, send_sems, recv_sems, buf_thru, token):
        for k, (s, d, to, _) in enumerate(plan(buf_ref, buf_ref)):
            pltpu.make_async_remote_copy(src_ref=s, dst_ref=d, send_sem=send_sems.at[k], recv_sem=recv_sems.at[k],
                                         device_id=to, device_id_type=MESH).start()
        token[...] = jnp.zeros_like(token)

    return _pcall(
        body, name=name,
        out_shape=(pltpu.SemaphoreType.DMA((n,)), pltpu.SemaphoreType.DMA((n,)), pltpu.HBM(buf.shape, buf.dtype),
                   jax.ShapeDtypeStruct((8, 128), F32)),
        in_specs=(HBM_SPEC,), out_specs=(SEM_SPEC, SEM_SPEC, HBM_SPEC, pl.BlockSpec(memory_space=pltpu.VMEM)),
        input_output_aliases={0: 2}, compiler_params=pltpu.CompilerParams(has_side_effects=EFFECT),
    )(_in_hbm(buf))


def _wait_inplace(name, started, after, plan):
    send_sems, recv_sems, buf_thru, _ = started

    def body(buf_ref, send_sems, recv_sems, *rest):
        for k, (s, _, to, mine) in enumerate(plan(buf_ref, buf_ref)):
            cp = pltpu.make_async_remote_copy(src_ref=s, dst_ref=mine, send_sem=send_sems.at[k],
                                              recv_sem=recv_sems.at[k], device_id=to, device_id_type=MESH)
            cp.wait_send()
            cp.wait_recv()

    return _pcall(
        body, name=name, out_shape=pltpu.HBM(buf_thru.shape, buf_thru.dtype),
        in_specs=(HBM_SPEC, SEM_SPEC, SEM_SPEC) + (ANY_SPEC,) * len(after), out_specs=HBM_SPEC,
        input_output_aliases={0: 0}, compiler_params=pltpu.CompilerParams(has_side_effects=EFFECT),
    )(buf_thru, send_sems, recv_sems, *after)


def _gather_plan(src_ref, land_ref):
    x, y, c = _place()
    peers = [(x, y, 1 - c), (1 - x, y, c), (x, 1 - y, c)]
    return [(src_ref, land_ref.at[4 * x + 2 * y + c], p, land_ref.at[4 * p[0] + 2 * p[1] + p[2]]) for p in peers]


def _relay_plan(buf_ref, _):
    x, y, c = _place()
    slot = lambda p, pc: 4 * p[0] + 2 * p[1] + pc
    xn, yn, dg, sib = (1 - x, y), (x, 1 - y), (1 - x, 1 - y), (x, y, 1 - c)
    half = buf_ref.shape[1] // 2
    lo, hi = pl.ds(0, half), pl.ds(half, half)
    return [(buf_ref.at[slot(xn, c)], buf_ref.at[slot(xn, c)], sib, buf_ref.at[slot(xn, 1 - c)]),
            (buf_ref.at[slot(yn, c)], buf_ref.at[slot(yn, c)], sib, buf_ref.at[slot(yn, 1 - c)]),
            (buf_ref.at[slot(xn, c), lo], buf_ref.at[slot(xn, c), lo], (*yn, c), buf_ref.at[slot(dg, c), lo]),
            (buf_ref.at[slot(yn, c), hi], buf_ref.at[slot(yn, c), hi], (*xn, c), buf_ref.at[slot(dg, c), hi])]


def _swap_plan(src_ref, land_ref):
    x, y, c = _place()
    return [(src_ref.at[:, pl.ds(1 - c, 1)], land_ref, (x, y, 1 - c), land_ref)]


def _exchange_plan(src_ref, land_ref):
    x, y, c = _place()
    chips = [(1 - x, y), (x, 1 - y), (1 - x, 1 - y)]
    return [(src_ref.at[2 * px + py], land_ref.at[2 * x + y], (px, py, c), land_ref.at[2 * px + py]) for px, py in chips]


def _gather_forward(land, block):
    def body(land_ref, out_ref, send_sems, recv_sems):
        x, y, c = _place()
        chips = [(1 - x, 1 - y)]

        def copy(k, px, py, pc):
            blk = out_ref.at[4 * px + 2 * py + pc]
            return pltpu.make_async_remote_copy(src_ref=blk, dst_ref=blk, send_sem=send_sems.at[k],
                                                recv_sem=recv_sems.at[k], device_id=(x, y, 1 - c), device_id_type=MESH)

        sent = [copy(k, px, py, c) for k, (px, py) in enumerate(chips)]
        for cp in sent:
            cp.start()
        for k, (px, py) in enumerate(chips):
            copy(k, px, py, 1 - c).wait_recv()
        for cp in sent:
            cp.wait_send()

    land = _pcall(
        body, name="allgather_rest_forward", out_shape=jax.ShapeDtypeStruct(land.shape, land.dtype),
        in_specs=[ANY_SPEC], out_specs=ANY_SPEC, input_output_aliases={0: 0},
        scratch_shapes=[pltpu.SemaphoreType.DMA((1,)), pltpu.SemaphoreType.DMA((1,))],
        compiler_params=pltpu.CompilerParams(has_side_effects=True),
    )(land)

    rows = block.shape[0]
    tr = rows // 4

    def place(me_ref, x_ref, land_ref, out_ref):
        out_ref[...] = x_ref[...]

    x, y, c = _place()
    grid_spec = pltpu.PrefetchScalarGridSpec(
        num_scalar_prefetch=1, grid=(rows // tr,),
        in_specs=[pl.BlockSpec((tr, D), lambda i, me: (i, 0)), ANY_SPEC],
        out_specs=pl.BlockSpec((None, tr, D), lambda i, me: (me[0], i, 0)))
    return _pcall(
        place, name="allgather_rest_own", grid_spec=grid_spec, out_shape=jax.ShapeDtypeStruct(land.shape, land.dtype),
        input_output_aliases={2: 0}, compiler_params=pltpu.CompilerParams(dimension_semantics=("arbitrary",)),
    )((4 * x + 2 * y + c).reshape(1), block, land)


class _ReduceScatter:
    def __init__(self, name, g):
        self.name = name
        rows = g.shape[1]
        self.started = _start_copies(name + "_swap_start", g.reshape(4, 2, rows, D), (4, 1, rows, D), _swap_plan, 1)
        self.token = self.started[4]

    def halfway(self, after):
        g4, theirs = _wait_copies(self.name + "_swap_wait", self.started, after, _swap_plan)
        self.own = _add_halves(g4, theirs, lax.axis_index("c").reshape(1), self.name + "_add_halves")
        self.started = _start_copies(self.name + "_exch_start", self.own, self.own.shape, _exchange_plan, 3)
        self.token = self.started[4]

    def finish(self, after):
        own, got = _wait_copies(self.name + "_exch_wait", self.started, after, _exchange_plan)
        chip = 2 * lax.axis_index("x") + lax.axis_index("y")
        return own, got, (chip + jnp.arange(4, dtype=jnp.int32)) % 4


def _pack_small(p, loss):
    def body(*refs):
        o_ref = refs[-1]
        o_ref[...] = jnp.zeros_like(o_ref)
        for ref, name in zip(refs, SMALL):
            r0, nr, c0, nc = SMALL_SLOT[name]
            o_ref[r0:r0 + nr, c0:c0 + nc] = ref[...]
        o_ref[17:18, 0:128] = refs[len(SMALL)][0:1, :]

    return _one_call(body, "pack_small_grads", [p[n] for n in SMALL] + [loss], [((SMALL_ROWS, D), F32)])[0]


_GAP_DEV, _GAP_ROW = divmod(GATE0 + 8, N_IN)
_GAP = CHK0 - GATE0 - 8


def _in_rows_to_proj(g):
    runs = [(j, 0, N_IN, N_IN * j) for j in range(_GAP_DEV)]
    runs += [(_GAP_DEV, 0, _GAP_ROW, N_IN * _GAP_DEV), (_GAP_DEV, _GAP_ROW, N_IN, N_IN * _GAP_DEV + _GAP_ROW + _GAP)]
    runs += [(j, 0, N_IN, N_IN * j + _GAP) for j in range(_GAP_DEV + 1, 8)]

    def body(g_ref, o_ref, acc_ref):
        acc_ref[...] = jnp.zeros_like(acc_ref)
        for j, r0, r1, dest in runs:
            start, shift = dest // 16 * 16, dest % 16
            win = -(-(shift + r1 - r0) // 16) * 16
            r = lax.broadcasted_iota(jnp.int32, (win, R_IN), 0)
            c = lax.broadcasted_iota(jnp.int32, (win, R_IN), 1)
            move = jnp.where((c >= r0) & (c < r1) & (r == c - r0 + shift), 1.0, 0.0).astype(BF16)
            acc_ref[start:start + win, :] += _nn(move, g_ref[j])
        o_ref[...] = acc_ref[...].astype(BF16)

    return _pcall(
        body, name="w_in_layout", out_shape=jax.ShapeDtypeStruct((PROJ, D), BF16), grid=(1,),
        in_specs=[pl.BlockSpec(g.shape, lambda i: (0, 0, 0))], out_specs=pl.BlockSpec((PROJ, D), lambda i: (0, 0)),
        scratch_shapes=[pltpu.VMEM((PROJ, D), F32)],
        compiler_params=pltpu.CompilerParams(dimension_semantics=("arbitrary",), vmem_limit_bytes=VMEM_BIG),
    )(g)


def _proj_rows_to_in(g):
    pad = lambda a: jnp.pad(a, ((0, R_IN - a.shape[0]), (0, 0)))
    lo = N_IN * _GAP_DEV
    shards = [pad(g[N_IN * j:N_IN * (j + 1)]) for j in range(_GAP_DEV)]
    shards.append(pad(jnp.concatenate([g[lo:lo + _GAP_ROW], g[lo + _GAP_ROW + _GAP:lo + N_IN + _GAP]], axis=0)))
    shards += [pad(g[N_IN * j + _GAP:N_IN * (j + 1) + _GAP]) for j in range(_GAP_DEV + 1, 8)]
    return jnp.stack(shards)


def _local_grads(x, mem, tgt, win_t, gw_of, sm, on_grads, after=()):
    b_pad = jnp.pad(sm['b_fgt'], ((0, 0), (0, 120)))
    tbl = jnp.pad(sm['rel_bias'], ((0, 0), (0, NREL_PAD - 257)))

    h1, proj, flog = _premix_fwd(x, sm['g_mix_pre'], win_t, after)
    c = _gate_fwd(flog, b_pad)
    ct3 = c[:, :8].T.reshape(4, 2, T)
    o_f, lse_f = _fox_fwd(proj, c, ct3)
    vt3 = _relvec_fwd(tbl).reshape(4, 2, VW)
    kvp = jnp.pad(proj[:, CHK0 + 512:], ((LEFT, 0), (0, 0)))
    o_c, lse_c = _chk_fwd(proj, kvp, vt3, [gw_of('relay', [o_f])])
    gw = gw_of('done', [o_c])
    w_out, w_mq, w_mk, w_mv, w_mo, w1_t, w2 = (_wblk(gw, n) for n in ('w_out', 'w_mq', 'w_mk', 'w_mv', 'w_mo', 'w_ff1', 'w_ff2'))
    ycat, z, x1, h2, qm = _postmix_fwd(x, o_f, o_c, sm['g_fox_out'], sm['g_chk_out'], w_out,
                                       sm['g_mix_post'], sm['g_mem_pre'], w_mq)
    memn, km, vm = _memkv_fwd(mem, sm['g_mem_kv'], w_mk, w_mv)
    om, ym, x2, h3 = _mem_fwd(qm, x1, km, vm, w_mo, sm['g_mem_post'], sm['g_ff_pre'])

    gs = {}
    dx2, da, dy3, r, loss_acc, gs['g_ff_post'], gs['g_ff_pre'] = _ffn_step(h3, x2, tgt, w1_t, w2, sm['g_ff_post'],
                                                                         sm['g_ff_pre'])
    tok = on_grads('A', _wgrad_group("wgrad_ff", [(da, h3), (r, dy3)], 512), None)
    dx1, dym, dqm, dkm, dvm, gs['g_mem_post'], gs['g_mem_pre'] = _mem_bwd(
        dx2, ym, x1, qm, km, vm, w_mo, w_mq, sm['g_mem_post'], sm['g_mem_pre'], [tok])
    tok = on_grads('A halfway', None, [dx1])
    gs['g_mem_kv'] = _memkv_bwd(dkm, dvm, mem, w_mk, w_mv)
    dz, dof, doc, gs['g_mix_post'], gs['g_fox_out'], gs['g_chk_out'] = _postmix_bwd(
        dx1, z, o_f, o_c, w_out, sm['g_mix_post'], sm['g_fox_out'], sm['g_chk_out'], [tok])
    tok = on_grads('B', _wgrad_group("wgrad_mem_out", [(ycat, dz), (h2, dqm), (memn, dkm), (memn, dvm), (om, dym)], 128), None)
    dq_f, dk_f, dv_f, dct, dcq = _fox_bwd(proj, c, ct3, o_f, lse_f, dof, [tok])
    tok = on_grads('B halfway', None, [dq_f])
    dq_c, dk_c, dv_c, gv = _chk_bwd(proj, kvp, vt3, o_c, lse_c, doc, [tok])
    gs['rel_bias'] = _relvec_bwd(gv.reshape(8, VW))[:, :257]
    dc = jnp.pad(dct.reshape(8, T).T + dcq[:, :, :2].transpose(1, 0, 2).reshape(T, 8), ((0, 0), (0, 120)))
    dflog, db = _gate_bwd(dc, flog, b_pad)
    gs['b_fgt'] = db[0:1, :8]
    grad_x, dproj, gs['g_mix_pre'] = _premix_bwd(dx1, x, [dq_f, dk_f, dv_f, dflog, dq_c, dk_c, dv_c], win_t, sm['g_mix_pre'])
    on_grads('C', _proj_rows_to_in(_wgrad(dproj, h1, "wgrad_in")), None)
    return loss_acc, grad_x, gs


def kernel(x, mem, w_in, b_fgt, rel_bias, g_fox_out, g_chk_out, w_out, g_mix_pre, g_mix_post, g_mem_kv, w_mq, w_mk, w_mv, w_mo, g_mem_pre, g_mem_post, w_ff1, w_ff2, g_ff_pre, g_ff_post, loss_target, m_w_in, m_b_fgt, m_rel_bias, m_g_fox_out, m_g_chk_out, m_w_out, m_g_mix_pre, m_g_mix_post, m_g_mem_kv, m_w_mq, m_w_mk, m_w_mv, m_w_mo, m_g_mem_pre, m_g_mem_post, m_w_ff1, m_w_ff2, m_g_ff_pre, m_g_ff_post, v_w_in, v_b_fgt, v_rel_bias, v_g_fox_out, v_g_chk_out, v_w_out, v_g_mix_pre, v_g_mix_post, v_g_mem_kv, v_w_mq, v_w_mk, v_w_mv, v_w_mo, v_g_mem_pre, v_g_mem_post, v_w_ff1, v_w_ff2, v_g_ff_pre, v_g_ff_post):
    args = dict(locals())
    two_d = lambda a: a.reshape(a.shape[-2:])
    w = {n: two_d(args[n]) for n in WEIGHTS}
    m = {n: two_d(args['m_' + n]) for n in WEIGHTS}
    v = {n: two_d(args['v_' + n]) for n in WEIGHTS}

    sm = {n: w[n] for n in SMALL}
    shard_in = jnp.pad(w['w_in'].T, ((0, R_IN - N_IN), (0, 0))).astype(BF16)
    gathered_in, zero = _allgather(shard_in, "allgather_w_in")
    win_t = _in_rows_to_proj(gathered_in)
    shard_rest = (jnp.concatenate([w['w_ff1'].T, w['w_ff2'], w['w_out'], w['w_mq'], w['w_mk'], w['w_mv'], w['w_mo']],
                                  axis=0) + zero[0, 0]).astype(BF16)
    gather = {'first': _start_copies("allgather_rest_start", shard_rest, (8, R_REST, D), _gather_plan, 3)}

    def gw_of(stage, after):
        if stage == 'relay':
            gather['block'], land = _wait_copies("allgather_rest_wait", gather['first'], after, _gather_plan)
            gather['second'] = _start_inplace("allgather_rest_relay_start", land, _relay_plan, 4)
            return gather['second'][3]
        land = _wait_inplace("allgather_rest_relay_wait", gather['second'], after, _relay_plan)
        return _gather_forward(land, gather['block'])

    rs = {}

    def on_grads(stage, g, after):
        if stage.endswith('halfway'):
            rs[stage[0]].halfway(after)
            return rs[stage[0]].token
        rs[stage] = _ReduceScatter("rs_" + stage.lower(), g)
        return rs[stage].token

    loss_local, grad_x, gs = _local_grads(x[0], mem[0], loss_target[0], win_t, gw_of, sm, on_grads, [gather['first'][4]])
    grads, deltas, new_m, new_v = {}, {}, {}, {}

    def update(n, out):
        grads[n], deltas[n], new_m[n], new_v[n] = out

    gparts, _ = _allgather(_pack_small(gs, loss_local), "allgather_small_grads", [rs['C'].token])
    small = _adamw_small(gparts, [w[n] for n in SMALL], [m[n] for n in SMALL], [v[n] for n in SMALL])
    loss = small[0][0, 0]
    for t, n in enumerate(SMALL):
        update(n, small[1 + 4 * t:5 + 4 * t])
    rs['C'].halfway([small[0]])

    own, got, order = rs['A'].finish([grad_x, rs['C'].started[4]])
    update('w_ff1', _sum_adam(own, got, order, 0, w['w_ff1'], m['w_ff1'], v['w_ff1'], "adamw_w_ff1", transposed=True))
    update('w_ff2', _sum_adam(own, got, order, 512, w['w_ff2'], m['w_ff2'], v['w_ff2'], "adamw_w_ff2"))
    own, got, order = rs['B'].finish([grad_x, rs['C'].started[4]])
    names_b = ('w_out', 'w_mq', 'w_mk', 'w_mv', 'w_mo')
    done = _sum_adam_rows(own, got, order, [w[n] for n in names_b], [m[n] for n in names_b], [v[n] for n in names_b],
                          "adamw_group_b")
    for k, n in enumerate(names_b):
        update(n, done[4 * k:4 * k + 4])

    own, got, order = rs['C'].finish([new_v[n] for n in BIG if n != 'w_in'])
    done = _sum_adam(own, got, order, 0, w['w_in'].T, m['w_in'].T, v['w_in'].T, "adamw_w_in")
    update('w_in', [a.T for a in done])

    out = [loss, grad_x[None]]
    for group in (grads, deltas, new_m, new_v):
        out += [group[n].reshape(args[n].shape) for n in WEIGHTS]
    return tuple(out)
```

```python
import jax
import jax.numpy as jnp
from jax import lax
from jax.experimental import pallas as pl
from jax.experimental.pallas import tpu as pltpu

F32 = jnp.float32
BF16 = jnp.bfloat16
MESH = pl.DeviceIdType.MESH

T = 2048
D = 1024
NMEM = 256
DFF = 4096
EPS = 1e-6
TM = 256
TM_WIDE = 512
TQ = 256
FQ = 512
HD = 64
SCALE = HD ** -0.5
MEM_HEADS = 4
MEM_HD = 256
MEM_SCALE = MEM_HD ** -0.5
NEG = -1e30
LEFT = 512
WIN = LEFT + TQ
VW = 1024
NREL_PAD = 384
PROJ = 3200
GATE0 = 1536
CHK0 = 1664
VMEM_BIG = 56 * 1024 * 1024

ADAM_LR = 0.001
ADAM_B1 = 0.9
ADAM_B2 = 0.999
ADAM_EPS = 1e-08
ADAM_WD = 0.01
ADAM_STEP = 10

N_IN = 385
R_IN = 400
R_REST = 1664
W_ROWS = {'w_ff1': (0, 512), 'w_ff2': (512, 512),
          'w_out': (1024, 128), 'w_mq': (1152, 128), 'w_mk': (1280, 128), 'w_mv': (1408, 128), 'w_mo': (1536, 128)}
SMALL_ROWS = 24
SMALL_SLOT = {'rel_bias': (0, 8, 0, 257), 'b_fgt': (8, 1, 0, 8), 'g_fox_out': (9, 1, 0, 512), 'g_chk_out': (9, 1, 512, 512),
              'g_mix_pre': (10, 1, 0, 1024), 'g_mix_post': (11, 1, 0, 1024), 'g_mem_kv': (12, 1, 0, 1024),
              'g_mem_pre': (13, 1, 0, 1024), 'g_mem_post': (14, 1, 0, 1024), 'g_ff_pre': (15, 1, 0, 1024),
              'g_ff_post': (16, 1, 0, 1024)}

WEIGHTS = ['w_in', 'b_fgt', 'rel_bias', 'g_fox_out', 'g_chk_out', 'w_out', 'g_mix_pre', 'g_mix_post', 'g_mem_kv',
           'w_mq', 'w_mk', 'w_mv', 'w_mo', 'g_mem_pre', 'g_mem_post', 'w_ff1', 'w_ff2', 'g_ff_pre', 'g_ff_post']
BIG = ['w_in', 'w_out', 'w_mq', 'w_mk', 'w_mv', 'w_mo', 'w_ff1', 'w_ff2']
SMALL = [n for n in WEIGHTS if n not in BIG]


def _pcall(body, **kw):
    return pl.pallas_call(body, **kw)


def _nn(a, b):
    return jnp.dot(a, b, preferred_element_type=F32)


def _nt(a, b):
    return lax.dot_general(a, b, (((1,), (1,)), ((), ())), preferred_element_type=F32)


def _tn(a, b):
    return lax.dot_general(a, b, (((0,), (0,)), ((), ())), preferred_element_type=F32)


def _w(ref):
    v = ref[...]
    return v if v.ndim == 2 else v.reshape(-1, v.shape[-1])


def _rstd(x):
    return lax.rsqrt(jnp.mean(x * x, axis=-1, keepdims=True) + EPS)


def _rms(x, g):
    return x * _rstd(x) * g


def _rms_bwd(x, g, dy):
    r = _rstd(x)
    xh = x * r
    dg = jnp.sum(dy * xh, axis=0, keepdims=True)
    dxh = dy * g
    dx = r * (dxh - xh * jnp.mean(dxh * xh, axis=-1, keepdims=True))
    return dx, dg


def _resident(a):
    if isinstance(a, tuple):
        _, shape, index = a
        return pl.BlockSpec(shape, lambda *_: index, pipeline_mode=pl.Buffered(1))
    return pl.BlockSpec(a.shape, lambda *_, nd=a.ndim: (0,) * nd, pipeline_mode=pl.Buffered(1))


def _wblk(gw, name):
    r0, rows = W_ROWS[name]
    return (gw, (8, rows, D), (0, r0 // rows, 0))


def _behind(body, n_in, after):
    if not after:
        return body
    return lambda *refs: body(*refs[:n_in], *refs[n_in + len(after):])


def _tok_call(body, name, tiled, full, outs_tiled, outs_acc=(), rows=T, tm=TM, vmem=None, after=()):
    in_specs = [pl.BlockSpec((tm, a.shape[1]), lambda i: (i, 0)) for a in tiled]
    in_specs += [_resident(a) for a in full] + [ANY_SPEC] * len(after)
    full = [a[0] if isinstance(a, tuple) else a for a in full] + list(after)
    body = _behind(body, len(tiled) + len(full) - len(after), after)
    out_shape = [jax.ShapeDtypeStruct((rows, c), dt) for c, dt in outs_tiled]
    out_shape += [jax.ShapeDtypeStruct(s, F32) for s in outs_acc]
    out_specs = [pl.BlockSpec((tm, c), lambda i: (i, 0)) for c, _ in outs_tiled]
    out_specs += [pl.BlockSpec(s, lambda i, nd=len(s): (0,) * nd) for s in outs_acc]
    return _pcall(
        body, name=name, grid=(rows // tm,), in_specs=in_specs, out_specs=out_specs, out_shape=out_shape,
        compiler_params=pltpu.CompilerParams(dimension_semantics=("arbitrary",), vmem_limit_bytes=vmem),
    )(*tiled, *full)


def _one_call(body, name, ins, outs, vmem=None):
    whole = lambda s: pl.BlockSpec(s, lambda i, nd=len(s): (0,) * nd)
    return _pcall(
        body, name=name, grid=(1,), in_specs=[_resident(a) for a in ins], out_specs=[whole(s) for s, _ in outs],
        out_shape=[jax.ShapeDtypeStruct(s, dt) for s, dt in outs],
        compiler_params=pltpu.CompilerParams(dimension_semantics=("arbitrary",), vmem_limit_bytes=vmem),
    )(*[a[0] if isinstance(a, tuple) else a for a in ins])


def _premix_fwd(x, g_pre, win_t, after=()):
    def body(x_ref, g_ref, w_ref, h_ref, proj_ref, flog_ref):
        h = _rms(x_ref[...], g_ref[...]).astype(BF16)
        h_ref[...] = h
        p = _nt(h, w_ref[...])
        proj_ref[...] = p.astype(BF16)
        flog_ref[...] = p[:, GATE0:GATE0 + 128]

    return _tok_call(body, "premix_fwd", [x], [g_pre, win_t],
                     [(D, BF16), (PROJ, BF16), (128, F32)], tm=TM_WIDE, vmem=VMEM_BIG, after=after)


def _postmix_fwd(x, o_f, o_c, g_fo, g_co, w_out, g_post, g_mpre, w_mq):
    def body(x_ref, of_ref, oc_ref, gfo_ref, gco_ref, wo_ref, gp_ref, gm_ref, wq_ref,
             y_ref, z_ref, x1_ref, h2_ref, qm_ref):
        y_ref[:, :512] = _rms(of_ref[...], gfo_ref[...]).astype(BF16)
        y_ref[:, 512:] = _rms(oc_ref[...], gco_ref[...]).astype(BF16)
        z = _nn(y_ref[...], _w(wo_ref))
        z_ref[...] = z
        x1 = x_ref[...] + _rms(z, gp_ref[...])
        x1_ref[...] = x1
        h2 = _rms(x1, gm_ref[...]).astype(BF16)
        h2_ref[...] = h2
        qm_ref[...] = _nn(h2, _w(wq_ref)).astype(BF16)

    return _tok_call(body, "postmix_fwd", [x, o_f, o_c], [g_fo, g_co, w_out, g_post, g_mpre, w_mq],
                     [(D, BF16), (D, F32), (D, F32), (D, BF16), (D, BF16)], tm=TM_WIDE, vmem=VMEM_BIG)


def _memkv_fwd(mem, g_kv, w_mk, w_mv):
    def body(m_ref, g_ref, wk_ref, wv_ref, mn_ref, k_ref, v_ref):
        mn = _rms(m_ref[...], g_ref[...]).astype(BF16)
        mn_ref[...] = mn
        k_ref[...] = _nn(mn, _w(wk_ref)).astype(BF16)
        v_ref[...] = _nn(mn, _w(wv_ref)).astype(BF16)

    return _tok_call(body, "memkv_fwd", [mem], [g_kv, w_mk, w_mv],
                     [(D, BF16), (D, BF16), (D, BF16)], rows=NMEM, tm=NMEM, vmem=VMEM_BIG)


def _mem_fwd(qm, x1, km, vm, w_mo, g_post, g_fpre):
    def body(q_ref, x1_ref, k_ref, v_ref, wo_ref, gp_ref, gf_ref, om_ref, ym_ref, x2_ref, h3_ref):
        for h in range(MEM_HEADS):
            sl = slice(h * MEM_HD, (h + 1) * MEM_HD)
            s = _nt(q_ref[:, sl], k_ref[:, sl]) * MEM_SCALE
            p = jnp.exp(s - jnp.max(s, axis=-1, keepdims=True))
            p = p / jnp.sum(p, axis=-1, keepdims=True)
            om_ref[:, sl] = _nn(p.astype(BF16), v_ref[:, sl]).astype(BF16)
        ym = _nn(om_ref[...], _w(wo_ref))
        ym_ref[...] = ym
        x2 = x1_ref[...] + _rms(ym, gp_ref[...])
        x2_ref[...] = x2
        h3_ref[...] = _rms(x2, gf_ref[...]).astype(BF16)

    return _tok_call(body, "mem_fwd", [qm, x1], [km, vm, w_mo, g_post, g_fpre],
                     [(D, BF16), (D, F32), (D, F32), (D, BF16)], tm=TM_WIDE, vmem=VMEM_BIG)


def _tri(lower):
    r = lax.broadcasted_iota(jnp.int32, (128, 128), 0)
    c = lax.broadcasted_iota(jnp.int32, (128, 128), 1)
    return jnp.where(r >= c if lower else c >= r, 1.0, 0.0).astype(F32)


def _hdot(a, b):
    return jnp.dot(a, b, preferred_element_type=F32, precision=lax.Precision.HIGHEST)


def _gate_fwd(flog, b_pad):
    def body(f_ref, b_ref, c_ref):
        tri = _tri(True)

        def step(i, carry):
            rows = pl.ds(pl.multiple_of(i * 128, 128), 128)
            z = f_ref[rows, :] + b_ref[...]
            lf = jnp.minimum(z, 0.0) - jnp.log(1.0 + jnp.exp(-jnp.abs(z)))
            cb = _hdot(tri, lf) + carry
            c_ref[rows, :] = cb
            return cb[127:128, :]

        lax.fori_loop(0, T // 128, step, jnp.zeros((1, 128), F32))

    return _one_call(body, "gate_fwd", [flog, b_pad], [((T, 128), F32)])[0]


def _gate_bwd(dc, flog, b_pad):
    def body(dc_ref, f_ref, b_ref, df_ref, db_ref):
        tri = _tri(False)

        def step(j, carry):
            run, db = carry
            i = T // 128 - 1 - j
            rows = pl.ds(pl.multiple_of(i * 128, 128), 128)
            dcb = dc_ref[rows, :]
            rb = _hdot(tri, dcb) + run
            z = f_ref[rows, :] + b_ref[...]
            df = rb * (1.0 / (1.0 + jnp.exp(z)))
            df_ref[rows, :] = df.astype(BF16)
            return run + jnp.sum(dcb, axis=0, keepdims=True), db + jnp.sum(df, axis=0, keepdims=True)

        _, db = lax.fori_loop(0, T // 128, step, (jnp.zeros((1, 128), F32), jnp.zeros((1, 128), F32)))
        db_ref[...] = jnp.broadcast_to(db, (8, 128))

    return _one_call(body, "gate_bwd", [dc, flog, b_pad], [((T, 128), BF16), ((8, 128), F32)])


def _lane_lo(rows=TQ):
    return lax.broadcasted_iota(jnp.int32, (rows, 128), 1) < HD


def _half(v, lo, a, scale=None):
    keep = lo if a == 0 else jnp.logical_not(lo)
    v = v.astype(F32) if scale is None else v.astype(F32) * scale
    return jnp.where(keep, v, 0.0).astype(BF16)


def _fox_specs():
    return [pl.BlockSpec((FQ, 128), lambda h, i: (i, h)),
            pl.BlockSpec((T, 128), lambda h, i: (0, 4 + h)),
            pl.BlockSpec((T, 128), lambda h, i: (0, 8 + h))]


def _lane_pick(x, at):
    lane = lax.broadcasted_iota(jnp.int32, x.shape, 1)
    return jnp.sum(jnp.where(lane == at, x, 0.0), axis=-1, keepdims=True)


def _fox_fwd(proj, c, ct3):
    def body(q_ref, k_ref, v_ref, c_ref, ct_ref, o_ref, l_ref):
        i = pl.program_id(1)
        lo = _lane_lo(FQ)
        causal = lax.broadcasted_iota(jnp.int32, (FQ, FQ), 1) <= lax.broadcasted_iota(jnp.int32, (FQ, FQ), 0)
        q = q_ref[...]
        qs = [_half(q, lo, a, SCALE) for a in range(2)]
        cqs = [_lane_pick(c_ref[...], 2 * pl.program_id(0) + a) for a in range(2)]

        def tile(off, carry, diagonal):
            kblk = k_ref[pl.ds(off, FQ), :]
            vblk = v_ref[pl.ds(off, FQ), :]
            new = []
            for a in range(2):
                m, l, acc = carry[a]
                s = _nt(qs[a], kblk) + (cqs[a] - ct_ref[a:a + 1, pl.ds(off, FQ)])
                if diagonal:
                    s = jnp.where(causal, s, NEG)
                m2 = jnp.maximum(m, jnp.max(s, axis=-1, keepdims=True))
                p = jnp.exp(s - m2)
                alpha = jnp.exp(m - m2)
                new.append((m2, alpha * l + jnp.sum(p, axis=-1, keepdims=True),
                            alpha * acc + _nn(p.astype(BF16), vblk)))
            return tuple(new)

        init = (jnp.full((FQ, 1), NEG, F32), jnp.zeros((FQ, 1), F32), jnp.zeros((FQ, 128), F32))
        carry = lax.fori_loop(0, i, lambda kb, c: tile(pl.multiple_of(kb * FQ, FQ), c, False), (init, init))
        carry = tile(pl.multiple_of(i * FQ, FQ), carry, True)
        outs = []
        for a in range(2):
            m, l, acc = carry[a]
            outs.append(acc / l)
            l_ref[:, 128 * a:128 * a + 128] = jnp.broadcast_to(m + jnp.log(l), (FQ, 128))
        o_ref[...] = jnp.where(lo, outs[0], outs[1])

    return _pcall(
        body, name="fox_fwd", grid=(4, T // FQ),
        in_specs=_fox_specs() + [pl.BlockSpec((FQ, 128), lambda h, i: (i, 0)),
                                 pl.BlockSpec((None, 2, T), lambda h, i: (h, 0, 0))],
        out_specs=[pl.BlockSpec((FQ, 128), lambda h, i: (i, h)), pl.BlockSpec((FQ, 256), lambda h, i: (i, h))],
        out_shape=[jax.ShapeDtypeStruct((T, 512), F32), jax.ShapeDtypeStruct((T, 1024), F32)],
        compiler_params=pltpu.CompilerParams(dimension_semantics=("arbitrary", "arbitrary"), vmem_limit_bytes=VMEM_BIG),
    )(proj, proj, proj, c, ct3)


def _fox_bwd(proj, c, ct3, o, lse, do, after=()):
    def body(q_ref, k_ref, v_ref, c_ref, ct_ref, o_ref, l_ref, do_ref, dq_ref, dkb_ref, dvb_ref, dct_ref, dcq_ref,
             dk_ref, dv_ref):
        i = pl.program_id(1)

        @pl.when(i == 0)
        def _():
            dk_ref[...] = jnp.zeros_like(dk_ref)
            dv_ref[...] = jnp.zeros_like(dv_ref)
            dct_ref[...] = jnp.zeros_like(dct_ref)

        lo = _lane_lo(FQ)
        causal = lax.broadcasted_iota(jnp.int32, (FQ, FQ), 1) <= lax.broadcasted_iota(jnp.int32, (FQ, FQ), 0)
        q = q_ref[...]
        do_v = do_ref[...]
        prod = do_v * o_ref[...]
        qs = [_half(q, lo, a, SCALE) for a in range(2)]
        dos = [_half(do_v, lo, a) for a in range(2)]
        deltas = [jnp.sum(jnp.where(lo if a == 0 else jnp.logical_not(lo), prod, 0.0), axis=-1, keepdims=True)
                  for a in range(2)]
        cqs = [_lane_pick(c_ref[...], 2 * pl.program_id(0) + a) for a in range(2)]
        las = [l_ref[:, 128 * a:128 * a + 1] for a in range(2)]

        def tile(off, carry, diagonal):
            kblk = k_ref[pl.ds(off, FQ), :]
            vblk = v_ref[pl.ds(off, FQ), :]
            new = []
            dk = jnp.zeros((128, FQ), F32)
            dv = jnp.zeros((128, FQ), F32)
            for a in range(2):
                dq_acc, rs = carry[a]
                s = _nt(qs[a], kblk) + (cqs[a] - ct_ref[a:a + 1, pl.ds(off, FQ)])
                if diagonal:
                    s = jnp.where(causal, s, NEG)
                p = jnp.exp(s - las[a])
                ds = p * (_nt(dos[a], vblk) - deltas[a])
                dsb = ds.astype(BF16)
                dk = dk + _tn(qs[a], dsb)
                dv = dv + _tn(dos[a], p.astype(BF16))
                dct_ref[a:a + 1, pl.ds(off, FQ)] -= jnp.sum(ds, axis=0, keepdims=True)
                new.append((dq_acc + _nn(dsb, kblk), rs + jnp.sum(ds, axis=-1, keepdims=True)))
            dk_ref[:, pl.ds(off, FQ)] += dk
            dv_ref[:, pl.ds(off, FQ)] += dv
            return tuple(new)

        init = (jnp.zeros((FQ, 128), F32), jnp.zeros((FQ, 1), F32))
        carry = lax.fori_loop(0, i, lambda kb, c: tile(pl.multiple_of(kb * FQ, FQ), c, False), (init, init))
        carry = tile(pl.multiple_of(i * FQ, FQ), carry, True)
        lane = lax.broadcasted_iota(jnp.int32, (FQ, 128), 1)
        dcq_ref[...] = jnp.where(lane == 0, carry[0][1], jnp.where(lane == 1, carry[1][1], 0.0))
        dq_ref[...] = (jnp.where(lo, carry[0][0], carry[1][0]) * SCALE).astype(BF16)

        @pl.when(i == T // FQ - 1)
        def _():
            dkb_ref[...] = dk_ref[...].T.astype(BF16)
            dvb_ref[...] = dv_ref[...].T.astype(BF16)

    blk = pl.BlockSpec((FQ, 128), lambda h, i: (i, h))
    wide = pl.BlockSpec((FQ, 256), lambda h, i: (i, h))
    rows = pl.BlockSpec((None, 2, T), lambda h, i: (h, 0, 0))
    col = pl.BlockSpec((T, 128), lambda h, i: (0, h))
    return _pcall(
        _behind(body, 8, after), name="fox_bwd", grid=(4, T // FQ),
        in_specs=_fox_specs() + [pl.BlockSpec((FQ, 128), lambda h, i: (i, 0)), rows, blk, wide, blk] + [ANY_SPEC] * len(after),
        out_specs=[blk, col, col, rows, pl.BlockSpec((None, FQ, 128), lambda h, i: (h, i, 0))],
        out_shape=[jax.ShapeDtypeStruct((T, 512), BF16), jax.ShapeDtypeStruct((T, 512), BF16),
                   jax.ShapeDtypeStruct((T, 512), BF16), jax.ShapeDtypeStruct((4, 2, T), F32),
                   jax.ShapeDtypeStruct((4, T, 128), F32)],
        scratch_shapes=[pltpu.VMEM((128, T), F32), pltpu.VMEM((128, T), F32)],
        compiler_params=pltpu.CompilerParams(dimension_semantics=("arbitrary", "arbitrary"), vmem_limit_bytes=VMEM_BIG),
    )(proj, proj, proj, c, ct3, o, lse, do, *after)


def _rel_onehot():
    ridx = lax.broadcasted_iota(jnp.int32, (NREL_PAD, VW), 0)
    j = lax.broadcasted_iota(jnp.int32, (NREL_PAD, VW), 1)
    return jnp.where(ridx == jnp.clip(TQ + LEFT - 1 - j, -128, 128) + 128, 1.0, 0.0).astype(F32)


def _relvec_fwd(tbl):
    def body(t_ref, v_ref):
        v_ref[...] = _hdot(t_ref[...], _rel_onehot())

    return _one_call(body, "relvec_fwd", [tbl], [((8, VW), F32)])[0]


def _relvec_bwd(gv):
    def body(g_ref, t_ref):
        t_ref[...] = lax.dot_general(g_ref[...], _rel_onehot(), (((1,), (1,)), ((), ())),
                                     preferred_element_type=F32, precision=lax.Precision.HIGHEST)

    return _one_call(body, "relvec_bwd", [gv], [((8, NREL_PAD), F32)])[0]


def _chk_bias(vt_ref, a, hidden):
    vb = jnp.broadcast_to(vt_ref[a:a + 1, :], (TQ, VW))
    y = pltpu.roll(vb, VW - (TQ - 1), 1, stride=1, stride_axis=0)[:, :WIN]
    cr = lax.broadcasted_iota(jnp.int32, (TQ, WIN), 0) // 64
    m = lax.broadcasted_iota(jnp.int32, (TQ, WIN), 1)
    return jnp.where((m // 64 >= cr) & (m // 64 <= cr + 8) & (m >= hidden), y, NEG)


def _chk_specs():
    return [pl.BlockSpec((TQ, 128), lambda h, i: (i, CHK0 // 128 + h)),
            pl.BlockSpec((T + LEFT, 128), lambda h, i: (0, h)),
            pl.BlockSpec((T + LEFT, 128), lambda h, i: (0, 4 + h)),
            pl.BlockSpec((None, 2, VW), lambda h, i: (h, 0, 0))]


def _chk_fwd(proj, kvp, vt3, after=()):
    def body(q_ref, k_ref, v_ref, vt_ref, o_ref, l_ref, bias_ref):
        i = pl.program_id(1)

        @pl.when(i == 0)
        def _():
            for first in range(3):
                for a in range(2):
                    bias_ref[first, a] = _chk_bias(vt_ref, a, max(LEFT - first * TQ, 0))

        lo = _lane_lo()
        off = pl.multiple_of(i * TQ, TQ)
        kw = k_ref[pl.ds(off, WIN), :]
        vw = v_ref[pl.ds(off, WIN), :]
        bias_at = jnp.minimum(i, 2)
        q = q_ref[...]
        outs = []
        for a in range(2):
            s = _nt(_half(q, lo, a, SCALE), kw) + bias_ref[bias_at, a]
            m = jnp.max(s, axis=-1, keepdims=True)
            p = jnp.exp(s - m)
            l = jnp.sum(p, axis=-1, keepdims=True)
            outs.append(_nn(p.astype(BF16), vw) / l)
            l_ref[:, 128 * a:128 * a + 128] = jnp.broadcast_to(m + jnp.log(l), (TQ, 128))
        o_ref[...] = jnp.where(lo, outs[0], outs[1])

    return _pcall(
        _behind(body, 4, after), name="chk_fwd", grid=(4, T // TQ), in_specs=_chk_specs() + [ANY_SPEC] * len(after),
        out_specs=[pl.BlockSpec((TQ, 128), lambda h, i: (i, h)), pl.BlockSpec((TQ, 256), lambda h, i: (i, h))],
        out_shape=[jax.ShapeDtypeStruct((T, 512), F32), jax.ShapeDtypeStruct((T, 1024), F32)],
        scratch_shapes=[pltpu.VMEM((3, 2, TQ, WIN), F32)],
        compiler_params=pltpu.CompilerParams(dimension_semantics=("arbitrary", "arbitrary")),
    )(proj, kvp, kvp, vt3, *after)


def _chk_bwd(proj, kvp, vt3, o, lse, do, after=()):
    nq = T // TQ

    def body(q_ref, k_ref, v_ref, vt_ref, o_ref, l_ref, do_ref, dq_ref, dkb_ref, dvb_ref, gv_ref, bias_ref, dsum_ref,
             dk_ref, dv_ref):
        i = pl.program_id(1)

        @pl.when(i == 0)
        def _():
            for first in range(3):
                for a in range(2):
                    bias_ref[first, a] = _chk_bias(vt_ref, a, max(LEFT - first * TQ, 0))
            dsum_ref[...] = jnp.zeros_like(dsum_ref)
            dk_ref[...] = jnp.zeros_like(dk_ref)
            dv_ref[...] = jnp.zeros_like(dv_ref)

        lo = _lane_lo()
        off = pl.multiple_of(i * TQ, TQ)
        kw = k_ref[pl.ds(off, WIN), :]
        vw = v_ref[pl.ds(off, WIN), :]
        bias_at = jnp.minimum(i, 2)
        q = q_ref[...]
        do_v = do_ref[...]
        prod = do_v * o_ref[...]
        dqs = []
        for a in range(2):
            keep = lo if a == 0 else jnp.logical_not(lo)
            qa = _half(q, lo, a, SCALE)
            doa = _half(do_v, lo, a)
            delta = jnp.sum(jnp.where(keep, prod, 0.0), axis=-1, keepdims=True)
            s = _nt(qa, kw) + bias_ref[bias_at, a]
            p = jnp.exp(s - l_ref[:, 128 * a:128 * a + 1])
            ds = p * (_nt(doa, vw) - delta)
            dsum_ref[a] += ds
            dsb = ds.astype(BF16)
            dk_ref[:, pl.ds(off, WIN)] += _tn(qa, dsb)
            dv_ref[:, pl.ds(off, WIN)] += _tn(doa, p.astype(BF16))
            dqs.append(_nn(dsb, kw))
        dq_ref[...] = (jnp.where(lo, dqs[0], dqs[1]) * SCALE).astype(BF16)

        @pl.when(i == nq - 1)
        def _():
            dkb_ref[...] = dk_ref[:, LEFT:].T.astype(BF16)
            dvb_ref[...] = dv_ref[:, LEFT:].T.astype(BF16)
            rr = lax.broadcasted_iota(jnp.int32, (TQ, TQ), 0)
            cc = lax.broadcasted_iota(jnp.int32, (TQ, TQ), 1)
            flip = jnp.where(rr + cc == TQ - 1, 1.0, 0.0).astype(F32)
            for a in range(2):
                dpad = jnp.concatenate([dsum_ref[a], jnp.zeros((TQ, VW - WIN), F32)], axis=1)
                z = pltpu.roll(_hdot(flip, dpad), 0, 1, stride=1, stride_axis=0)
                gv_ref[a:a + 1, :] = jnp.sum(z, axis=0, keepdims=True)

    blk = pl.BlockSpec((TQ, 128), lambda h, i: (i, h))
    wide = pl.BlockSpec((TQ, 256), lambda h, i: (i, h))
    col = pl.BlockSpec((T, 128), lambda h, i: (0, h))
    return _pcall(
        _behind(body, 7, after), name="chk_bwd", grid=(4, nq), in_specs=_chk_specs() + [blk, wide, blk] + [ANY_SPEC] * len(after),
        out_specs=[blk, col, col, pl.BlockSpec((None, 2, VW), lambda h, i: (h, 0, 0))],
        out_shape=[jax.ShapeDtypeStruct((T, 512), BF16), jax.ShapeDtypeStruct((T, 512), BF16),
                   jax.ShapeDtypeStruct((T, 512), BF16), jax.ShapeDtypeStruct((4, 2, VW), F32)],
        scratch_shapes=[pltpu.VMEM((3, 2, TQ, WIN), F32), pltpu.VMEM((2, TQ, WIN), F32),
                        pltpu.VMEM((128, T + LEFT), F32), pltpu.VMEM((128, T + LEFT), F32)],
        compiler_params=pltpu.CompilerParams(dimension_semantics=("arbitrary", "arbitrary")),
    )(proj, kvp, kvp, vt3, o, lse, do, *after)


def _zero_at_start(*refs):
    @pl.when(pl.program_id(0) == 0)
    def _():
        for r in refs:
            r[...] = jnp.zeros_like(r)


def _ffn_step(h3, x2, tgt, w1_t, w2, g_post, g_pre):
    def body(h_ref, x2_ref, t_ref, w1_ref, w2_ref, gp_ref, gf_ref, dx2_ref, da_ref, dy_ref, r_ref, loss_ref, dgp_ref, dgf_ref):
        _zero_at_start(loss_ref, dgp_ref, dgf_ref)
        w1, w2v = _w(w1_ref), _w(w2_ref)
        ra = jnp.maximum(_nt(h_ref[...], w1), 0.0)
        r = jnp.square(ra).astype(BF16)
        r_ref[...] = r
        y = _nn(r, w2v)
        x2v = x2_ref[...]
        e = x2v + _rms(y, gp_ref[...]) - t_ref[...]
        loss_ref[...] += 0.5 * jnp.sum(jnp.sum(e * e, axis=-1, keepdims=True) * (1.0 / D))
        dx3 = e * (1.0 / D)
        dy, dgp = _rms_bwd(y, gp_ref[...], dx3)
        dgp_ref[...] += dgp
        dyb = dy.astype(BF16)
        dy_ref[...] = dyb
        da = (_nt(dyb, w2v) * (2.0 * ra)).astype(BF16)
        da_ref[...] = da
        dh, dgf = _rms_bwd(x2v, gf_ref[...], _nn(da, w1))
        dgf_ref[...] += dgf
        dx2_ref[...] = dx3 + dh

    return _tok_call(body, "ffn_step", [h3, x2, tgt], [w1_t, w2, g_post, g_pre],
                     [(D, F32), (DFF, BF16), (D, BF16), (DFF, BF16)], [(8, 128), (1, D), (1, D)], vmem=VMEM_BIG)


def _mem_bwd(dx2, ym, x1, qm, km, vm, w_mo, w_mq, g_post, g_pre, after=()):
    def body(dx2_ref, ym_ref, x1_ref, q_ref, k_ref, v_ref, wo_ref, wq_ref, gp_ref, gm_ref,
             dx1_ref, dym_ref, dq_ref, dk_ref, dv_ref, dgp_ref, dgm_ref, dom_ref):
        _zero_at_start(dk_ref, dv_ref, dgp_ref, dgm_ref)
        dx2_v = dx2_ref[...]
        dym, dgp = _rms_bwd(ym_ref[...], gp_ref[...], dx2_v)
        dgp_ref[...] += dgp
        dymb = dym.astype(BF16)
        dym_ref[...] = dymb
        dom_ref[...] = _nt(dymb, _w(wo_ref)).astype(BF16)
        for h in range(MEM_HEADS):
            sl = slice(h * MEM_HD, (h + 1) * MEM_HD)
            qh, kh, doh = q_ref[:, sl], k_ref[:, sl], dom_ref[:, sl]
            s = _nt(qh, kh) * MEM_SCALE
            p = jnp.exp(s - jnp.max(s, axis=-1, keepdims=True))
            p = p / jnp.sum(p, axis=-1, keepdims=True)
            dp = _nt(doh, v_ref[:, sl])
            ds = (p * (dp - jnp.sum(p * dp, axis=-1, keepdims=True))).astype(BF16)
            dq_ref[:, sl] = (_nn(ds, kh) * MEM_SCALE).astype(BF16)
            dk_ref[:, sl] += _tn(ds, qh) * MEM_SCALE
            dv_ref[:, sl] += _tn(p.astype(BF16), doh)
        dh, dgm = _rms_bwd(x1_ref[...], gm_ref[...], _nt(dq_ref[...], _w(wq_ref)))
        dgm_ref[...] += dgm
        dx1_ref[...] = dx2_v + dh

    tiled = pl.BlockSpec((TM_WIDE, D), lambda i: (i, 0))
    in_specs = [tiled] * 4 + [_resident(a) for a in (km, vm, w_mo, w_mq, g_post, g_pre)] + [ANY_SPEC] * len(after)
    w_mo, w_mq = w_mo[0], w_mq[0]
    kv = pl.BlockSpec((NMEM, D), lambda i: (0, 0))
    vec = pl.BlockSpec((1, D), lambda i: (0, 0))
    return _pcall(
        _behind(body, 10, after), name="mem_bwd", grid=(T // TM_WIDE,), in_specs=in_specs,
        out_specs=[tiled, tiled, tiled, kv, kv, vec, vec],
        out_shape=[jax.ShapeDtypeStruct((T, D), F32), jax.ShapeDtypeStruct((T, D), BF16),
                   jax.ShapeDtypeStruct((T, D), BF16), jax.ShapeDtypeStruct((NMEM, D), F32),
                   jax.ShapeDtypeStruct((NMEM, D), F32), jax.ShapeDtypeStruct((1, D), F32),
                   jax.ShapeDtypeStruct((1, D), F32)],
        scratch_shapes=[pltpu.VMEM((TM_WIDE, D), BF16)],
        compiler_params=pltpu.CompilerParams(dimension_semantics=("arbitrary",), vmem_limit_bytes=VMEM_BIG),
    )(dx2, ym, x1, qm, km, vm, w_mo, w_mq, g_post, g_pre, *after)


def _memkv_bwd(dkm, dvm, mem, w_mk, w_mv):
    def body(dk_ref, dv_ref, m_ref, wk_ref, wv_ref, dg_ref):
        dmn = _nt(dk_ref[...].astype(BF16), _w(wk_ref)) + _nt(dv_ref[...].astype(BF16), _w(wv_ref))
        mv = m_ref[...]
        dg_ref[...] = jnp.sum(dmn * (mv * _rstd(mv)), axis=0, keepdims=True)

    return _one_call(body, "memkv_bwd", [dkm, dvm, mem, w_mk, w_mv], [((1, D), F32)], vmem=VMEM_BIG)[0]


def _postmix_bwd(dx1, z, o_f, o_c, w_out, g_post, g_fo, g_co, after=()):
    def body(dx1_ref, z_ref, of_ref, oc_ref, wo_ref, gp_ref, gfo_ref, gco_ref,
             dz_ref, dof_ref, doc_ref, dgp_ref, dgfo_ref, dgco_ref):
        _zero_at_start(dgp_ref, dgfo_ref, dgco_ref)
        dz, dgp = _rms_bwd(z_ref[...], gp_ref[...], dx1_ref[...])
        dgp_ref[...] += dgp
        dzb = dz.astype(BF16)
        dz_ref[...] = dzb
        dy = _nt(dzb, _w(wo_ref))
        dof, dgfo = _rms_bwd(of_ref[...], gfo_ref[...], dy[:, :512])
        doc, dgco = _rms_bwd(oc_ref[...], gco_ref[...], dy[:, 512:])
        dof_ref[...] = dof
        doc_ref[...] = doc
        dgfo_ref[...] += dgfo
        dgco_ref[...] += dgco

    return _tok_call(body, "postmix_bwd", [dx1, z, o_f, o_c], [w_out, g_post, g_fo, g_co],
                     [(D, BF16), (512, F32), (512, F32)], [(1, D), (1, 512), (1, 512)], tm=TM_WIDE, vmem=VMEM_BIG,
                     after=after)


def _premix_bwd(dx1, x, dproj, win_t, g_pre, after=()):
    def body(dx1_ref, x_ref, dp_ref, w_ref, g_ref, dx_ref, dg_ref):
        _zero_at_start(dg_ref)
        dh, dg = _rms_bwd(x_ref[...], g_ref[...], _nn(dp_ref[...], w_ref[...]))
        dg_ref[...] += dg
        dx_ref[...] = dx1_ref[...] + dh

    return _tok_call(body, "premix_bwd", [dx1, x, dproj], [win_t, g_pre], [(D, F32)], [(1, D)],
                     tm=TM_WIDE, vmem=VMEM_BIG, after=after)


def _wgrad(a, b, name):
    k, m = a.shape
    n = b.shape[1]
    tm = 640 if m % 640 == 0 and m > 1024 else min(m, 512)
    tn = min(n, 1024)

    def body(a_ref, b_ref, o_ref):
        o_ref[...] = _tn(a_ref[...].astype(BF16), b_ref[...].astype(BF16))

    return _pcall(
        body, name=name, grid=(m // tm, n // tn),
        in_specs=[pl.BlockSpec((k, tm), lambda i, j: (0, i)), pl.BlockSpec((k, tn), lambda i, j: (0, j))],
        out_specs=pl.BlockSpec((tm, tn), lambda i, j: (i, j)),
        out_shape=jax.ShapeDtypeStruct((m, n), F32),
        compiler_params=pltpu.CompilerParams(dimension_semantics=("arbitrary", "arbitrary"), vmem_limit_bytes=VMEM_BIG),
    )(a, b)


def _wgrad_group(name, pairs, rows):
    def body(*refs):
        o_ref = refs[-1]
        for k in range(len(pairs)):
            o_ref[k * rows:(k + 1) * rows, :] = _tn(refs[2 * k][...].astype(BF16), refs[2 * k + 1][...].astype(BF16))

    in_specs, ops = [], []
    for a, b in pairs:
        in_specs += [pl.BlockSpec((a.shape[0], rows), lambda j: (0, j)), _resident(b)]
        ops += [a, b]
    return _pcall(
        body, name=name, grid=(8,), in_specs=in_specs,
        out_specs=pl.BlockSpec((None, len(pairs) * rows, D), lambda j: (j, 0, 0)),
        out_shape=jax.ShapeDtypeStruct((8, len(pairs) * rows, D), F32),
        compiler_params=pltpu.CompilerParams(dimension_semantics=("arbitrary",), vmem_limit_bytes=VMEM_BIG),
    )(*ops)


def _adam_math(w, g, m, v):
    m2 = ADAM_B1 * m + (1.0 - ADAM_B1) * g
    v2 = ADAM_B2 * v + (1.0 - ADAM_B2) * jnp.square(g)
    m_hat = m2 / (1.0 - ADAM_B1 ** ADAM_STEP)
    v_hat = v2 / (1.0 - ADAM_B2 ** ADAM_STEP)
    delta = -ADAM_LR * (m_hat / (jnp.sqrt(v_hat) + ADAM_EPS) + ADAM_WD * w)
    return delta, m2, v2


def _adamw_small(gparts, ws, ms, vs):
    n = len(SMALL)

    def body(g_ref, *refs):
        w_refs, m_refs, v_refs = refs[:n], refs[n:2 * n], refs[2 * n:3 * n]
        outs, sum_ref = refs[3 * n:-1], refs[-1]
        g = g_ref[0]
        for k in range(1, 8):
            g = g + g_ref[k]
        sum_ref[...] = g
        outs[0][...] = sum_ref[17:18, 0:128]
        for t, name in enumerate(SMALL):
            r0, nr, c0, nc = SMALL_SLOT[name]
            gt = sum_ref[r0:r0 + nr, c0:c0 + nc]
            out = (gt,) + _adam_math(w_refs[t][...], gt, m_refs[t][...], v_refs[t][...])
            for o_ref, val in zip(outs[1 + 4 * t:5 + 4 * t], out):
                o_ref[...] = val

    whole = lambda s: pl.BlockSpec(s, lambda i, nd=len(s): (0,) * nd)
    ins = [gparts] + list(ws) + list(ms) + list(vs)
    out_shapes = [(1, 128)] + [a.shape for a in ws for _ in range(4)]
    return _pcall(
        body, name="adamw_small", grid=(1,), in_specs=[whole(a.shape) for a in ins],
        out_specs=[whole(s) for s in out_shapes], out_shape=[jax.ShapeDtypeStruct(s, F32) for s in out_shapes],
        scratch_shapes=[pltpu.VMEM((SMALL_ROWS, D), F32)],
        compiler_params=pltpu.CompilerParams(dimension_semantics=("arbitrary",)),
    )(*ins)


def _row_tile(rows):
    return next(t for t in (512, 400, 320) if rows % t == 0)


def _add_halves(g4, theirs, core, name):
    rows = g4.shape[2]
    tr = _row_tile(rows)

    def body(c_ref, a_ref, b_ref, o_ref):
        o_ref[...] = (a_ref[...] + b_ref[...]).astype(BF16)

    grid_spec = pltpu.PrefetchScalarGridSpec(
        num_scalar_prefetch=1, grid=(4, rows // tr),
        in_specs=[pl.BlockSpec((None, None, tr, D), lambda j, i, c: (j, c[0], i, 0)),
                  pl.BlockSpec((None, None, tr, D), lambda j, i, c: (j, 0, i, 0))],
        out_specs=pl.BlockSpec((None, tr, D), lambda j, i, c: (j, i, 0)))
    return _pcall(
        body, name=name, grid_spec=grid_spec, out_shape=jax.ShapeDtypeStruct((4, rows, D), BF16),
        compiler_params=pltpu.CompilerParams(dimension_semantics=("arbitrary", "arbitrary")),
    )(core, g4, theirs)


def _sum_adam(own, got, order, r0, w, m, v, name, transposed=False):
    n = w.shape[1] if transposed else w.shape[0]
    tr = min(n, 256) if n % 8 == 0 else n
    rows = tr if n % 8 == 0 else own.shape[1]

    def body(o_ref, a_ref, b_ref, c_ref, d_ref, w_ref, m_ref, v_ref, g_ref, dl_ref, m2_ref, v2_ref):
        f = lambda r: r[0:tr, :].astype(F32)
        g = ((f(a_ref) + f(b_ref)) + f(c_ref)) + f(d_ref)
        g = g.T if transposed else g
        g_ref[...] = g
        dl_ref[...], m2_ref[...], v2_ref[...] = _adam_math(w_ref[...], g, m_ref[...], v_ref[...])

    slot = lambda k: pl.BlockSpec((None, rows, D), lambda i, o: (o[k], r0 // rows + i, 0))
    wspec = pl.BlockSpec((D, tr), lambda i, o: (0, i)) if transposed else pl.BlockSpec((tr, D), lambda i, o: (i, 0))
    grid_spec = pltpu.PrefetchScalarGridSpec(
        num_scalar_prefetch=1, grid=(n // tr,), in_specs=[slot(0), slot(1), slot(2), slot(3), wspec, wspec, wspec],
        out_specs=[wspec] * 4)
    return _pcall(
        body, name=name, grid_spec=grid_spec, out_shape=[jax.ShapeDtypeStruct(w.shape, F32)] * 4,
        compiler_params=pltpu.CompilerParams(dimension_semantics=("arbitrary",)),
    )(order, own, got, got, got, w, m, v)


def _sum_adam_rows(own, got, order, ws, ms, vs, name):
    n, rows = len(ws), ws[0].shape[0]

    def body(o_ref, a_ref, b_ref, c_ref, d_ref, *refs):
        ins, outs = refs[:3 * n], refs[3 * n:]
        for t in range(n):
            r = slice(t * rows, (t + 1) * rows)
            f = lambda ref: ref[r, :].astype(F32)
            g = ((f(a_ref) + f(b_ref)) + f(c_ref)) + f(d_ref)
            out = (g,) + _adam_math(ins[t][...], g, ins[n + t][...], ins[2 * n + t][...])
            for o, val in zip(outs[4 * t:4 * t + 4], out):
                o[...] = val

    slot = lambda k: pl.BlockSpec((None, n * rows, D), lambda i, o: (o[k], 0, 0), pipeline_mode=pl.Buffered(1))
    wspec = pl.BlockSpec((rows, D), lambda i, o: (0, 0), pipeline_mode=pl.Buffered(1))
    grid_spec = pltpu.PrefetchScalarGridSpec(
        num_scalar_prefetch=1, grid=(1,), in_specs=[slot(0), slot(1), slot(2), slot(3)] + [wspec] * (3 * n),
        out_specs=[pl.BlockSpec((rows, D), lambda i, o: (0, 0))] * (4 * n))
    return _pcall(
        body, name=name, grid_spec=grid_spec, out_shape=[jax.ShapeDtypeStruct((rows, D), F32)] * (4 * n),
        compiler_params=pltpu.CompilerParams(dimension_semantics=("arbitrary",), vmem_limit_bytes=VMEM_BIG),
    )(order, own, got, got, got, *ws, *ms, *vs)


def _place():
    return lax.axis_index("x"), lax.axis_index("y"), lax.axis_index("c")


def _allgather(block, name, after=()):
    rows = block.shape[0]
    split = (rows // 2 + 15) // 16 * 16

    def body(x_ref, out_ref, token, send_sems, recv_sems, local_sem):
        token[...] = jnp.zeros_like(token)
        x, y, c = _place()
        me, sib = (x, y, c), (x, y, 1 - c)
        xn, yn, dg = (1 - x, y), (x, 1 - y), (1 - x, 1 - y)
        lo, hi = pl.ds(0, split), pl.ds(split, rows - split)

        def copy(k, blk, to, part=None, src=None):
            index = 4 * blk[0] + 2 * blk[1] + blk[2]
            view = out_ref.at[index] if part is None else out_ref.at[index, part]
            return pltpu.make_async_remote_copy(
                src_ref=view if src is None else src, dst_ref=view,
                send_sem=send_sems.at[k], recv_sem=recv_sems.at[k], device_id=to, device_id_type=MESH)

        def start(*copies):
            for cp in copies:
                cp.start()
            return list(copies)

        mine = pltpu.make_async_copy(x_ref, out_ref.at[4 * x + 2 * y + c], local_sem)
        mine.start()
        sent = start(copy(0, me, sib, src=x_ref), copy(1, me, (*xn, c), src=x_ref), copy(2, me, (*yn, c), src=x_ref))
        copy(1, (*xn, c), me).wait_recv()
        sent += start(copy(3, (*xn, c), sib), copy(5, (*xn, c), (*yn, c), part=lo))
        copy(2, (*yn, c), me).wait_recv()
        sent += start(copy(4, (*yn, c), sib), copy(6, (*yn, c), (*xn, c), part=hi))
        copy(5, (*dg, c), me, part=lo).wait_recv()
        copy(6, (*dg, c), me, part=hi).wait_recv()
        sent += start(copy(7, (*dg, c), sib))
        for k, blk in ((0, sib), (3, (*xn, 1 - c)), (4, (*yn, 1 - c)), (7, (*dg, 1 - c))):
            copy(k, blk, me).wait_recv()
        for cp in sent:
            cp.wait_send()
        mine.wait()

    return _pcall(
        _behind(body, 1, after), name=name,
        out_shape=[jax.ShapeDtypeStruct((8,) + block.shape, block.dtype), jax.ShapeDtypeStruct((8, 128), F32)],
        in_specs=[pl.BlockSpec(memory_space=pl.ANY)] * (1 + len(after)),
        out_specs=[pl.BlockSpec(memory_space=pl.ANY), pl.BlockSpec(memory_space=pltpu.VMEM)],
        scratch_shapes=[pltpu.SemaphoreType.DMA((8,)), pltpu.SemaphoreType.DMA((8,)), pltpu.SemaphoreType.DMA(())],
        compiler_params=pltpu.CompilerParams(has_side_effects=True),
    )(block, *after)


HBM_SPEC = pl.BlockSpec(memory_space=pltpu.HBM)
SEM_SPEC = pl.BlockSpec(memory_space=pltpu.SEMAPHORE)
ANY_SPEC = pl.BlockSpec(memory_space=pl.ANY)
EFFECT = pltpu.SideEffectType.DATAFLOW_SIDE_EFFECTING


def _in_hbm(a):
    return pltpu.with_memory_space_constraint(a, pltpu.HBM)


def _start_copies(name, src, land_shape, plan, n):
    def body(src_ref, land_ref, send_sems, recv_sems, src_thru, land_thru, token):
        for k, (s, d, to, _) in enumerate(plan(src_ref, land_ref)):
            pltpu.make_async_remote_copy(src_ref=s, dst_ref=d, send_sem=send_sems.at[k], recv_sem=recv_sems.at[k],
                                         device_id=to, device_id_type=MESH).start()
        token[...] = jnp.zeros_like(token)

    return _pcall(
        body, name=name,
        out_shape=(pltpu.SemaphoreType.DMA((n,)), pltpu.SemaphoreType.DMA((n,)), pltpu.HBM(src.shape, src.dtype),
                   pltpu.HBM(land_shape, src.dtype), jax.ShapeDtypeStruct((8, 128), F32)),
        in_specs=(HBM_SPEC, HBM_SPEC),
        out_specs=(SEM_SPEC, SEM_SPEC, HBM_SPEC, HBM_SPEC, pl.BlockSpec(memory_space=pltpu.VMEM)),
        input_output_aliases={0: 2, 1: 3}, compiler_params=pltpu.CompilerParams(has_side_effects=EFFECT),
    )(_in_hbm(src), _in_hbm(lax.empty(land_shape, src.dtype)))


def _wait_copies(name, started, after, plan):
    send_sems, recv_sems, src_thru, land_thru, _ = started

    def body(src_ref, land_ref, send_sems, recv_sems, *rest):
        for k, (s, _, to, mine) in enumerate(plan(src_ref, land_ref)):
            cp = pltpu.make_async_remote_copy(src_ref=s, dst_ref=mine, send_sem=send_sems.at[k],
                                              recv_sem=recv_sems.at[k], device_id=to, device_id_type=MESH)
            cp.wait_send()
            cp.wait_recv()

    return _pcall(
        body, name=name,
        out_shape=(pltpu.HBM(src_thru.shape, src_thru.dtype), pltpu.HBM(land_thru.shape, land_thru.dtype)),
        in_specs=(HBM_SPEC, HBM_SPEC, SEM_SPEC, SEM_SPEC) + (ANY_SPEC,) * len(after), out_specs=(HBM_SPEC, HBM_SPEC),
        input_output_aliases={0: 0, 1: 1}, compiler_params=pltpu.CompilerParams(has_side_effects=EFFECT),
    )(src_thru, land_thru, send_sems, recv_sems, *after)


def _start_inplace(name, buf, plan, n):
    def body(buf_ref, send_sems, recv_sems, buf_thru, token):
        for k, (s, d, to, _) in enumerate(plan(buf_ref, buf_ref)):
            pltpu.make_async_remote_copy(src_ref=s, dst_ref=d, send_sem=send_sems.at[k], recv_sem=recv_sems.at[k],
                                         device_id=to, device_id_type=MESH).start()
        token[...] = jnp.zeros_like(token)

    return _pcall(
        body, name=name,
        out_shape=(pltpu.SemaphoreType.DMA((n,)), pltpu.SemaphoreType.DMA((n,)), pltpu.HBM(buf.shape, buf.dtype),
                   jax.ShapeDtypeStruct((8, 128), F32)),
        in_specs=(HBM_SPEC,), out_specs=(SEM_SPEC, SEM_SPEC, HBM_SPEC, pl.BlockSpec(memory_space=pltpu.VMEM)),
        input_output_aliases={0: 2}, compiler_params=pltpu.CompilerParams(has_side_effects=EFFECT),
    )(_in_hbm(buf))


def _wait_inplace(name, started, after, plan):
    send_sems, recv_sems, buf_thru, _ = started

    def body(buf_ref, send_sems, recv_sems, *rest):
        for k, (s, _, to, mine) in enumerate(plan(buf_ref, buf_ref)):
            cp = pltpu.make_async_remote_copy(src_ref=s, dst_ref=mine, send_sem=send_sems.at[k],
                                              recv_sem=recv_sems.at[k], device_id=to, device_id_type=MESH)
            cp.wait_send()
            cp.wait_recv()

    return _pcall(
        body, name=name, out_shape=pltpu.HBM(buf_thru.shape, buf_thru.dtype),
        in_specs=(HBM_SPEC, SEM_SPEC, SEM_SPEC) + (ANY_SPEC,) * len(after), out_specs=HBM_SPEC,
        input_output_aliases={0: 0}, compiler_params=pltpu.CompilerParams(has_side_effects=EFFECT),
    )(buf_thru, send_sems, recv_sems, *after)


def _gather_plan(src_ref, land_ref):
    x, y, c = _place()
    peers = [(x, y, 1 - c), (1 - x, y, c), (x, 1 - y, c)]
    return [(src_ref, land_ref.at[4 * x + 2 * y + c], p, land_ref.at[4 * p[0] + 2 * p[1] + p[2]]) for p in peers]


def _relay_plan(buf_ref, _):
    x, y, c = _place()
    slot = lambda p, pc: 4 * p[0] + 2 * p[1] + pc
    xn, yn, dg, sib = (1 - x, y), (x, 1 - y), (1 - x, 1 - y), (x, y, 1 - c)
    half = buf_ref.shape[1] // 2
    lo, hi = pl.ds(0, half), pl.ds(half, half)
    return [(buf_ref.at[slot(xn, c)], buf_ref.at[slot(xn, c)], sib, buf_ref.at[slot(xn, 1 - c)]),
            (buf_ref.at[slot(yn, c)], buf_ref.at[slot(yn, c)], sib, buf_ref.at[slot(yn, 1 - c)]),
            (buf_ref.at[slot(xn, c), lo], buf_ref.at[slot(xn, c), lo], (*yn, c), buf_ref.at[slot(dg, c), lo]),
            (buf_ref.at[slot(yn, c), hi], buf_ref.at[slot(yn, c), hi], (*xn, c), buf_ref.at[slot(dg, c), hi])]


def _swap_plan(src_ref, land_ref):
    x, y, c = _place()
    return [(src_ref.at[:, pl.ds(1 - c, 1)], land_ref, (x, y, 1 - c), land_ref)]


def _exchange_plan(src_ref, land_ref):
    x, y, c = _place()
    chips = [(1 - x, y), (x, 1 - y), (1 - x, 1 - y)]
    return [(src_ref.at[2 * px + py], land_ref.at[2 * x + y], (px, py, c), land_ref.at[2 * px + py]) for px, py in chips]


def _gather_forward(land, block):
    def body(land_ref, out_ref, send_sems, recv_sems):
        x, y, c = _place()
        chips = [(1 - x, 1 - y)]

        def copy(k, px, py, pc):
            blk = out_ref.at[4 * px + 2 * py + pc]
            return pltpu.make_async_remote_copy(src_ref=blk, dst_ref=blk, send_sem=send_sems.at[k],
                                                recv_sem=recv_sems.at[k], device_id=(x, y, 1 - c), device_id_type=MESH)

        sent = [copy(k, px, py, c) for k, (px, py) in enumerate(chips)]
        for cp in sent:
            cp.start()
        for k, (px, py) in enumerate(chips):
            copy(k, px, py, 1 - c).wait_recv()
        for cp in sent:
            cp.wait_send()

    land = _pcall(
        body, name="allgather_rest_forward", out_shape=jax.ShapeDtypeStruct(land.shape, land.dtype),
        in_specs=[ANY_SPEC], out_specs=ANY_SPEC, input_output_aliases={0: 0},
        scratch_shapes=[pltpu.SemaphoreType.DMA((1,)), pltpu.SemaphoreType.DMA((1,))],
        compiler_params=pltpu.CompilerParams(has_side_effects=True),
    )(land)

    rows = block.shape[0]
    tr = rows // 4

    def place(me_ref, x_ref, land_ref, out_ref):
        out_ref[...] = x_ref[...]

    x, y, c = _place()
    grid_spec = pltpu.PrefetchScalarGridSpec(
        num_scalar_prefetch=1, grid=(rows // tr,),
        in_specs=[pl.BlockSpec((tr, D), lambda i, me: (i, 0)), ANY_SPEC],
        out_specs=pl.BlockSpec((None, tr, D), lambda i, me: (me[0], i, 0)))
    return _pcall(
        place, name="allgather_rest_own", grid_spec=grid_spec, out_shape=jax.ShapeDtypeStruct(land.shape, land.dtype),
        input_output_aliases={2: 0}, compiler_params=pltpu.CompilerParams(dimension_semantics=("arbitrary",)),
    )((4 * x + 2 * y + c).reshape(1), block, land)


class _ReduceScatter:
    def __init__(self, name, g):
        self.name = name
        rows = g.shape[1]
        self.started = _start_copies(name + "_swap_start", g.reshape(4, 2, rows, D), (4, 1, rows, D), _swap_plan, 1)
        self.token = self.started[4]

    def halfway(self, after):
        g4, theirs = _wait_copies(self.name + "_swap_wait", self.started, after, _swap_plan)
        self.own = _add_halves(g4, theirs, lax.axis_index("c").reshape(1), self.name + "_add_halves")
        self.started = _start_copies(self.name + "_exch_start", self.own, self.own.shape, _exchange_plan, 3)
        self.token = self.started[4]

    def finish(self, after):
        own, got = _wait_copies(self.name + "_exch_wait", self.started, after, _exchange_plan)
        chip = 2 * lax.axis_index("x") + lax.axis_index("y")
        return own, got, (chip + jnp.arange(4, dtype=jnp.int32)) % 4


def _pack_small(p, loss):
    def body(*refs):
        o_ref = refs[-1]
        o_ref[...] = jnp.zeros_like(o_ref)
        for ref, name in zip(refs, SMALL):
            r0, nr, c0, nc = SMALL_SLOT[name]
            o_ref[r0:r0 + nr, c0:c0 + nc] = ref[...]
        o_ref[17:18, 0:128] = refs[len(SMALL)][0:1, :]

    return _one_call(body, "pack_small_grads", [p[n] for n in SMALL] + [loss], [((SMALL_ROWS, D), F32)])[0]


_GAP_DEV, _GAP_ROW = divmod(GATE0 + 8, N_IN)
_GAP = CHK0 - GATE0 - 8


def _in_rows_to_proj(g):
    runs = [(j, 0, N_IN, N_IN * j) for j in range(_GAP_DEV)]
    runs += [(_GAP_DEV, 0, _GAP_ROW, N_IN * _GAP_DEV), (_GAP_DEV, _GAP_ROW, N_IN, N_IN * _GAP_DEV + _GAP_ROW + _GAP)]
    runs += [(j, 0, N_IN, N_IN * j + _GAP) for j in range(_GAP_DEV + 1, 8)]

    def body(g_ref, o_ref, acc_ref):
        acc_ref[...] = jnp.zeros_like(acc_ref)
        for j, r0, r1, dest in runs:
            start, shift = dest // 16 * 16, dest % 16
            win = -(-(shift + r1 - r0) // 16) * 16
            r = lax.broadcasted_iota(jnp.int32, (win, R_IN), 0)
            c = lax.broadcasted_iota(jnp.int32, (win, R_IN), 1)
            move = jnp.where((c >= r0) & (c < r1) & (r == c - r0 + shift), 1.0, 0.0).astype(BF16)
            acc_ref[start:start + win, :] += _nn(move, g_ref[j])
        o_ref[...] = acc_ref[...].astype(BF16)

    return _pcall(
        body, name="w_in_layout", out_shape=jax.ShapeDtypeStruct((PROJ, D), BF16), grid=(1,),
        in_specs=[pl.BlockSpec(g.shape, lambda i: (0, 0, 0))], out_specs=pl.BlockSpec((PROJ, D), lambda i: (0, 0)),
        scratch_shapes=[pltpu.VMEM((PROJ, D), F32)],
        compiler_params=pltpu.CompilerParams(dimension_semantics=("arbitrary",), vmem_limit_bytes=VMEM_BIG),
    )(g)


def _proj_rows_to_in(g):
    pad = lambda a: jnp.pad(a, ((0, R_IN - a.shape[0]), (0, 0)))
    lo = N_IN * _GAP_DEV
    shards = [pad(g[N_IN * j:N_IN * (j + 1)]) for j in range(_GAP_DEV)]
    shards.append(pad(jnp.concatenate([g[lo:lo + _GAP_ROW], g[lo + _GAP_ROW + _GAP:lo + N_IN + _GAP]], axis=0)))
    shards += [pad(g[N_IN * j + _GAP:N_IN * (j + 1) + _GAP]) for j in range(_GAP_DEV + 1, 8)]
    return jnp.stack(shards)


def _local_grads(x, mem, tgt, win_t, gw_of, sm, on_grads, after=()):
    b_pad = jnp.pad(sm['b_fgt'], ((0, 0), (0, 120)))
    tbl = jnp.pad(sm['rel_bias'], ((0, 0), (0, NREL_PAD - 257)))

    h1, proj, flog = _premix_fwd(x, sm['g_mix_pre'], win_t, after)
    c = _gate_fwd(flog, b_pad)
    ct3 = c[:, :8].T.reshape(4, 2, T)
    o_f, lse_f = _fox_fwd(proj, c, ct3)
    vt3 = _relvec_fwd(tbl).reshape(4, 2, VW)
    kvp = jnp.pad(proj[:, CHK0 + 512:], ((LEFT, 0), (0, 0)))
    o_c, lse_c = _chk_fwd(proj, kvp, vt3, [gw_of('relay', [o_f])])
    gw = gw_of('done', [o_c])
    w_out, w_mq, w_mk, w_mv, w_mo, w1_t, w2 = (_wblk(gw, n) for n in ('w_out', 'w_mq', 'w_mk', 'w_mv', 'w_mo', 'w_ff1', 'w_ff2'))
    ycat, z, x1, h2, qm = _postmix_fwd(x, o_f, o_c, sm['g_fox_out'], sm['g_chk_out'], w_out,
                                       sm['g_mix_post'], sm['g_mem_pre'], w_mq)
    memn, km, vm = _memkv_fwd(mem, sm['g_mem_kv'], w_mk, w_mv)
    om, ym, x2, h3 = _mem_fwd(qm, x1, km, vm, w_mo, sm['g_mem_post'], sm['g_ff_pre'])

    gs = {}
    dx2, da, dy3, r, loss_acc, gs['g_ff_post'], gs['g_ff_pre'] = _ffn_step(h3, x2, tgt, w1_t, w2, sm['g_ff_post'],
                                                                         sm['g_ff_pre'])
    tok = on_grads('A', _wgrad_group("wgrad_ff", [(da, h3), (r, dy3)], 512), None)
    dx1, dym, dqm, dkm, dvm, gs['g_mem_post'], gs['g_mem_pre'] = _mem_bwd(
        dx2, ym, x1, qm, km, vm, w_mo, w_mq, sm['g_mem_post'], sm['g_mem_pre'], [tok])
    tok = on_grads('A halfway', None, [dx1])
    gs['g_mem_kv'] = _memkv_bwd(dkm, dvm, mem, w_mk, w_mv)
    dz, dof, doc, gs['g_mix_post'], gs['g_fox_out'], gs['g_chk_out'] = _postmix_bwd(
        dx1, z, o_f, o_c, w_out, sm['g_mix_post'], sm['g_fox_out'], sm['g_chk_out'], [tok])
    tok = on_grads('B', _wgrad_group("wgrad_mem_out", [(ycat, dz), (h2, dqm), (memn, dkm), (memn, dvm), (om, dym)], 128), None)
    dq_f, dk_f, dv_f, dct, dcq = _fox_bwd(proj, c, ct3, o_f, lse_f, dof, [tok])
    tok = on_grads('B halfway', None, [dq_f])
    dq_c, dk_c, dv_c, gv = _chk_bwd(proj, kvp, vt3, o_c, lse_c, doc, [tok])
    gs['rel_bias'] = _relvec_bwd(gv.reshape(8, VW))[:, :257]
    dc = jnp.pad(dct.reshape(8, T).T + dcq[:, :, :2].transpose(1, 0, 2).reshape(T, 8), ((0, 0), (0, 120)))
    dflog, db = _gate_bwd(dc, flog, b_pad)
    gs['b_fgt'] = db[0:1, :8]
    dproj = jnp.concatenate([dq_f, dk_f, dv_f, dflog, dq_c, dk_c, dv_c], axis=1)
    on_grads('C', _proj_rows_to_in(_wgrad(dproj, h1, "wgrad_in")), None)
    tok = on_grads('C halfway', None, [gs['g_mem_kv'], gs['rel_bias']])
    grad_x, gs['g_mix_pre'] = _premix_bwd(dx1, x, dproj, win_t, sm['g_mix_pre'], [tok])
    return loss_acc, grad_x, gs


def kernel(x, mem, w_in, b_fgt, rel_bias, g_fox_out, g_chk_out, w_out, g_mix_pre, g_mix_post, g_mem_kv, w_mq, w_mk, w_mv, w_mo, g_mem_pre, g_mem_post, w_ff1, w_ff2, g_ff_pre, g_ff_post, loss_target, m_w_in, m_b_fgt, m_rel_bias, m_g_fox_out, m_g_chk_out, m_w_out, m_g_mix_pre, m_g_mix_post, m_g_mem_kv, m_w_mq, m_w_mk, m_w_mv, m_w_mo, m_g_mem_pre, m_g_mem_post, m_w_ff1, m_w_ff2, m_g_ff_pre, m_g_ff_post, v_w_in, v_b_fgt, v_rel_bias, v_g_fox_out, v_g_chk_out, v_w_out, v_g_mix_pre, v_g_mix_post, v_g_mem_kv, v_w_mq, v_w_mk, v_w_mv, v_w_mo, v_g_mem_pre, v_g_mem_post, v_w_ff1, v_w_ff2, v_g_ff_pre, v_g_ff_post):
    args = dict(locals())
    two_d = lambda a: a.reshape(a.shape[-2:])
    w = {n: two_d(args[n]) for n in WEIGHTS}
    m = {n: two_d(args['m_' + n]) for n in WEIGHTS}
    v = {n: two_d(args['v_' + n]) for n in WEIGHTS}

    sm = {n: w[n] for n in SMALL}
    shard_in = jnp.pad(w['w_in'].T, ((0, R_IN - N_IN), (0, 0))).astype(BF16)
    gathered_in, zero = _allgather(shard_in, "allgather_w_in")
    win_t = _in_rows_to_proj(gathered_in)
    shard_rest = (jnp.concatenate([w['w_ff1'].T, w['w_ff2'], w['w_out'], w['w_mq'], w['w_mk'], w['w_mv'], w['w_mo']],
                                  axis=0) + zero[0, 0]).astype(BF16)
    gather = {'first': _start_copies("allgather_rest_start", shard_rest, (8, R_REST, D), _gather_plan, 3)}

    def gw_of(stage, after):
        if stage == 'relay':
            gather['block'], land = _wait_copies("allgather_rest_wait", gather['first'], after, _gather_plan)
            gather['second'] = _start_inplace("allgather_rest_relay_start", land, _relay_plan, 4)
            return gather['second'][3]
        land = _wait_inplace("allgather_rest_relay_wait", gather['second'], after, _relay_plan)
        return _gather_forward(land, gather['block'])

    rs = {}

    def on_grads(stage, g, after):
        if stage.endswith('halfway'):
            rs[stage[0]].halfway(after)
            return rs[stage[0]].token
        rs[stage] = _ReduceScatter("rs_" + stage.lower(), g)
        return rs[stage].token

    loss_local, grad_x, gs = _local_grads(x[0], mem[0], loss_target[0], win_t, gw_of, sm, on_grads, [gather['first'][4]])
    grads, deltas, new_m, new_v = {}, {}, {}, {}

    def update(n, out):
        grads[n], deltas[n], new_m[n], new_v[n] = out

    gparts, _ = _allgather(_pack_small(gs, loss_local), "allgather_small_grads", [rs['C'].token])
    small = _adamw_small(gparts, [w[n] for n in SMALL], [m[n] for n in SMALL], [v[n] for n in SMALL])
    loss = small[0][0, 0]
    for t, n in enumerate(SMALL):
        update(n, small[1 + 4 * t:5 + 4 * t])

    own, got, order = rs['A'].finish([grad_x, rs['C'].started[4]])
    update('w_ff1', _sum_adam(own, got, order, 0, w['w_ff1'], m['w_ff1'], v['w_ff1'], "adamw_w_ff1", transposed=True))
    update('w_ff2', _sum_adam(own, got, order, 512, w['w_ff2'], m['w_ff2'], v['w_ff2'], "adamw_w_ff2"))
    own, got, order = rs['B'].finish([grad_x, rs['C'].started[4]])
    names_b = ('w_out', 'w_mq', 'w_mk', 'w_mv', 'w_mo')
    done = _sum_adam_rows(own, got, order, [w[n] for n in names_b], [m[n] for n in names_b], [v[n] for n in names_b],
                          "adamw_group_b")
    for k, n in enumerate(names_b):
        update(n, done[4 * k:4 * k + 4])

    own, got, order = rs['C'].finish([new_v[n] for n in BIG if n != 'w_in'])
    done = _sum_adam(own, got, order, 0, w['w_in'].T, m['w_in'].T, v['w_in'].T, "adamw_w_in")
    update('w_in', [a.T for a in done])

    out = [loss, grad_x[None]]
    for group in (grads, deltas, new_m, new_v):
        out += [group[n].reshape(args[n].shape) for n in WEIGHTS]
    return tuple(out)
```

```python
import jax
import jax.numpy as jnp
from jax import lax
from jax.experimental import pallas as pl
from jax.experimental.pallas import tpu as pltpu

F32 = jnp.float32
BF16 = jnp.bfloat16
MESH = pl.DeviceIdType.MESH

T = 2048
D = 1024
NMEM = 256
DFF = 4096
EPS = 1e-6
TM = 256
TM_WIDE = 512
TQ = 256
FQ = 512
HD = 64
SCALE = HD ** -0.5
MEM_HEADS = 4
MEM_HD = 256
MEM_SCALE = MEM_HD ** -0.5
NEG = -1e30
LEFT = 512
WIN = LEFT + TQ
VW = 1024
NREL_PAD = 384
PROJ = 3200
GATE0 = 1536
CHK0 = 1664
VMEM_BIG = 56 * 1024 * 1024

ADAM_LR = 0.001
ADAM_B1 = 0.9
ADAM_B2 = 0.999
ADAM_EPS = 1e-08
ADAM_WD = 0.01
ADAM_STEP = 10

N_IN = 385
R_IN = 400
R_REST = 1664
W_ROWS = {'w_ff1': (0, 512), 'w_ff2': (512, 512),
          'w_out': (1024, 128), 'w_mq': (1152, 128), 'w_mk': (1280, 128), 'w_mv': (1408, 128), 'w_mo': (1536, 128)}
SMALL_ROWS = 24
SMALL_SLOT = {'rel_bias': (0, 8, 0, 257), 'b_fgt': (8, 1, 0, 8), 'g_fox_out': (9, 1, 0, 512), 'g_chk_out': (9, 1, 512, 512),
              'g_mix_pre': (10, 1, 0, 1024), 'g_mix_post': (11, 1, 0, 1024), 'g_mem_kv': (12, 1, 0, 1024),
              'g_mem_pre': (13, 1, 0, 1024), 'g_mem_post': (14, 1, 0, 1024), 'g_ff_pre': (15, 1, 0, 1024),
              'g_ff_post': (16, 1, 0, 1024)}

WEIGHTS = ['w_in', 'b_fgt', 'rel_bias', 'g_fox_out', 'g_chk_out', 'w_out', 'g_mix_pre', 'g_mix_post', 'g_mem_kv',
           'w_mq', 'w_mk', 'w_mv', 'w_mo', 'g_mem_pre', 'g_mem_post', 'w_ff1', 'w_ff2', 'g_ff_pre', 'g_ff_post']
BIG = ['w_in', 'w_out', 'w_mq', 'w_mk', 'w_mv', 'w_mo', 'w_ff1', 'w_ff2']
SMALL = [n for n in WEIGHTS if n not in BIG]


def _pcall(body, **kw):
    return pl.pallas_call(body, **kw)


def _nn(a, b):
    return jnp.dot(a, b, preferred_element_type=F32)


def _nt(a, b):
    return lax.dot_general(a, b, (((1,), (1,)), ((), ())), preferred_element_type=F32)


def _tn(a, b):
    return lax.dot_general(a, b, (((0,), (0,)), ((), ())), preferred_element_type=F32)


def _w(ref):
    v = ref[...]
    return v if v.ndim == 2 else v.reshape(-1, v.shape[-1])


def _rstd(x):
    return lax.rsqrt(jnp.mean(x * x, axis=-1, keepdims=True) + EPS)


def _rms(x, g):
    return x * _rstd(x) * g


def _rms_bwd(x, g, dy):
    r = _rstd(x)
    xh = x * r
    dg = jnp.sum(dy * xh, axis=0, keepdims=True)
    dxh = dy * g
    dx = r * (dxh - xh * jnp.mean(dxh * xh, axis=-1, keepdims=True))
    return dx, dg


def _resident(a):
    if isinstance(a, tuple):
        _, shape, index = a
        return pl.BlockSpec(shape, lambda *_: index, pipeline_mode=pl.Buffered(1))
    return pl.BlockSpec(a.shape, lambda *_, nd=a.ndim: (0,) * nd, pipeline_mode=pl.Buffered(1))


def _wblk(gw, name):
    r0, rows = W_ROWS[name]
    return (gw, (8, rows, D), (0, r0 // rows, 0))


def _behind(body, n_in, after):
    if not after:
        return body
    return lambda *refs: body(*refs[:n_in], *refs[n_in + len(after):])


def _tok_call(body, name, tiled, full, outs_tiled, outs_acc=(), rows=T, tm=TM, vmem=None, after=()):
    in_specs = [pl.BlockSpec((tm, a.shape[1]), lambda i: (i, 0)) for a in tiled]
    in_specs += [_resident(a) for a in full] + [ANY_SPEC] * len(after)
    full = [a[0] if isinstance(a, tuple) else a for a in full] + list(after)
    body = _behind(body, len(tiled) + len(full) - len(after), after)
    out_shape = [jax.ShapeDtypeStruct((rows, c), dt) for c, dt in outs_tiled]
    out_shape += [jax.ShapeDtypeStruct(s, F32) for s in outs_acc]
    out_specs = [pl.BlockSpec((tm, c), lambda i: (i, 0)) for c, _ in outs_tiled]
    out_specs += [pl.BlockSpec(s, lambda i, nd=len(s): (0,) * nd) for s in outs_acc]
    return _pcall(
        body, name=name, grid=(rows // tm,), in_specs=in_specs, out_specs=out_specs, out_shape=out_shape,
        compiler_params=pltpu.CompilerParams(dimension_semantics=("arbitrary",), vmem_limit_bytes=vmem),
    )(*tiled, *full)


def _one_call(body, name, ins, outs, vmem=None):
    whole = lambda s: pl.BlockSpec(s, lambda i, nd=len(s): (0,) * nd)
    return _pcall(
        body, name=name, grid=(1,), in_specs=[_resident(a) for a in ins], out_specs=[whole(s) for s, _ in outs],
        out_shape=[jax.ShapeDtypeStruct(s, dt) for s, dt in outs],
        compiler_params=pltpu.CompilerParams(dimension_semantics=("arbitrary",), vmem_limit_bytes=vmem),
    )(*[a[0] if isinstance(a, tuple) else a for a in ins])


def _premix_fwd(x, g_pre, win_t, after=()):
    def body(x_ref, g_ref, w_ref, h_ref, proj_ref, flog_ref):
        h = _rms(x_ref[...], g_ref[...]).astype(BF16)
        h_ref[...] = h
        p = _nt(h, w_ref[...])
        proj_ref[...] = p.astype(BF16)
        flog_ref[...] = p[:, GATE0:GATE0 + 128]

    return _tok_call(body, "premix_fwd", [x], [g_pre, win_t],
                     [(D, BF16), (PROJ, BF16), (128, F32)], tm=TM_WIDE, vmem=VMEM_BIG, after=after)


def _postmix_fwd(x, o_f, o_c, g_fo, g_co, w_out, g_post, g_mpre, w_mq):
    def body(x_ref, of_ref, oc_ref, gfo_ref, gco_ref, wo_ref, gp_ref, gm_ref, wq_ref,
             y_ref, z_ref, x1_ref, h2_ref, qm_ref):
        y_ref[:, :512] = _rms(of_ref[...], gfo_ref[...]).astype(BF16)
        y_ref[:, 512:] = _rms(oc_ref[...], gco_ref[...]).astype(BF16)
        z = _nn(y_ref[...], _w(wo_ref))
        z_ref[...] = z
        x1 = x_ref[...] + _rms(z, gp_ref[...])
        x1_ref[...] = x1
        h2 = _rms(x1, gm_ref[...]).astype(BF16)
        h2_ref[...] = h2
        qm_ref[...] = _nn(h2, _w(wq_ref)).astype(BF16)

    return _tok_call(body, "postmix_fwd", [x, o_f, o_c], [g_fo, g_co, w_out, g_post, g_mpre, w_mq],
                     [(D, BF16), (D, F32), (D, F32), (D, BF16), (D, BF16)], tm=TM_WIDE, vmem=VMEM_BIG)


def _memkv_fwd(mem, g_kv, w_mk, w_mv):
    def body(m_ref, g_ref, wk_ref, wv_ref, mn_ref, k_ref, v_ref):
        mn = _rms(m_ref[...], g_ref[...]).astype(BF16)
        mn_ref[...] = mn
        k_ref[...] = _nn(mn, _w(wk_ref)).astype(BF16)
        v_ref[...] = _nn(mn, _w(wv_ref)).astype(BF16)

    return _tok_call(body, "memkv_fwd", [mem], [g_kv, w_mk, w_mv],
                     [(D, BF16), (D, BF16), (D, BF16)], rows=NMEM, tm=NMEM, vmem=VMEM_BIG)


def _mem_fwd(qm, x1, km, vm, w_mo, g_post, g_fpre):
    def body(q_ref, x1_ref, k_ref, v_ref, wo_ref, gp_ref, gf_ref, om_ref, ym_ref, x2_ref, h3_ref):
        for h in range(MEM_HEADS):
            sl = slice(h * MEM_HD, (h + 1) * MEM_HD)
            s = _nt(q_ref[:, sl], k_ref[:, sl]) * MEM_SCALE
            p = jnp.exp(s - jnp.max(s, axis=-1, keepdims=True))
            p = p / jnp.sum(p, axis=-1, keepdims=True)
            om_ref[:, sl] = _nn(p.astype(BF16), v_ref[:, sl]).astype(BF16)
        ym = _nn(om_ref[...], _w(wo_ref))
        ym_ref[...] = ym
        x2 = x1_ref[...] + _rms(ym, gp_ref[...])
        x2_ref[...] = x2
        h3_ref[...] = _rms(x2, gf_ref[...]).astype(BF16)

    return _tok_call(body, "mem_fwd", [qm, x1], [km, vm, w_mo, g_post, g_fpre],
                     [(D, BF16), (D, F32), (D, F32), (D, BF16)], tm=TM_WIDE, vmem=VMEM_BIG)


def _tri(lower):
    r = lax.broadcasted_iota(jnp.int32, (128, 128), 0)
    c = lax.broadcasted_iota(jnp.int32, (128, 128), 1)
    return jnp.where(r >= c if lower else c >= r, 1.0, 0.0).astype(F32)


def _hdot(a, b):
    return jnp.dot(a, b, preferred_element_type=F32, precision=lax.Precision.HIGHEST)


def _gate_fwd(flog, b_pad):
    def body(f_ref, b_ref, c_ref):
        tri = _tri(True)

        def step(i, carry):
            rows = pl.ds(pl.multiple_of(i * 128, 128), 128)
            z = f_ref[rows, :] + b_ref[...]
            lf = jnp.minimum(z, 0.0) - jnp.log(1.0 + jnp.exp(-jnp.abs(z)))
            cb = _hdot(tri, lf) + carry
            c_ref[rows, :] = cb
            return cb[127:128, :]

        lax.fori_loop(0, T // 128, step, jnp.zeros((1, 128), F32))

    return _one_call(body, "gate_fwd", [flog, b_pad], [((T, 128), F32)])[0]


def _gate_bwd(dc, flog, b_pad):
    def body(dc_ref, f_ref, b_ref, df_ref, db_ref):
        tri = _tri(False)

        def step(j, carry):
            run, db = carry
            i = T // 128 - 1 - j
            rows = pl.ds(pl.multiple_of(i * 128, 128), 128)
            dcb = dc_ref[rows, :]
            rb = _hdot(tri, dcb) + run
            z = f_ref[rows, :] + b_ref[...]
            df = rb * (1.0 / (1.0 + jnp.exp(z)))
            df_ref[rows, :] = df.astype(BF16)
            return run + jnp.sum(dcb, axis=0, keepdims=True), db + jnp.sum(df, axis=0, keepdims=True)

        _, db = lax.fori_loop(0, T // 128, step, (jnp.zeros((1, 128), F32), jnp.zeros((1, 128), F32)))
        db_ref[...] = jnp.broadcast_to(db, (8, 128))

    return _one_call(body, "gate_bwd", [dc, flog, b_pad], [((T, 128), BF16), ((8, 128), F32)])


def _lane_lo(rows=TQ):
    return lax.broadcasted_iota(jnp.int32, (rows, 128), 1) < HD


def _half(v, lo, a, scale=None):
    keep = lo if a == 0 else jnp.logical_not(lo)
    v = v.astype(F32) if scale is None else v.astype(F32) * scale
    return jnp.where(keep, v, 0.0).astype(BF16)


def _fox_specs():
    return [pl.BlockSpec((FQ, 128), lambda h, i: (i, h)),
            pl.BlockSpec((T, 128), lambda h, i: (0, 4 + h)),
            pl.BlockSpec((T, 128), lambda h, i: (0, 8 + h))]


def _lane_pick(x, at):
    lane = lax.broadcasted_iota(jnp.int32, x.shape, 1)
    return jnp.sum(jnp.where(lane == at, x, 0.0), axis=-1, keepdims=True)


def _fox_fwd(proj, c, ct3):
    def body(q_ref, k_ref, v_ref, c_ref, ct_ref, o_ref, l_ref):
        i = pl.program_id(1)
        lo = _lane_lo(FQ)
        causal = lax.broadcasted_iota(jnp.int32, (FQ, FQ), 1) <= lax.broadcasted_iota(jnp.int32, (FQ, FQ), 0)
        q = q_ref[...]
        qs = [_half(q, lo, a, SCALE) for a in range(2)]
        cqs = [_lane_pick(c_ref[...], 2 * pl.program_id(0) + a) for a in range(2)]

        def tile(off, carry, diagonal):
            kblk = k_ref[pl.ds(off, FQ), :]
            vblk = v_ref[pl.ds(off, FQ), :]
            new = []
            for a in range(2):
                m, l, acc = carry[a]
                s = _nt(qs[a], kblk) + (cqs[a] - ct_ref[a:a + 1, pl.ds(off, FQ)])
                if diagonal:
                    s = jnp.where(causal, s, NEG)
                m2 = jnp.maximum(m, jnp.max(s, axis=-1, keepdims=True))
                p = jnp.exp(s - m2)
                alpha = jnp.exp(m - m2)
                new.append((m2, alpha * l + jnp.sum(p, axis=-1, keepdims=True),
                            alpha * acc + _nn(p.astype(BF16), vblk)))
            return tuple(new)

        init = (jnp.full((FQ, 1), NEG, F32), jnp.zeros((FQ, 1), F32), jnp.zeros((FQ, 128), F32))
        carry = lax.fori_loop(0, i, lambda kb, c: tile(pl.multiple_of(kb * FQ, FQ), c, False), (init, init))
        carry = tile(pl.multiple_of(i * FQ, FQ), carry, True)
        outs = []
        for a in range(2):
            m, l, acc = carry[a]
            outs.append(acc / l)
            l_ref[:, 128 * a:128 * a + 128] = jnp.broadcast_to(m + jnp.log(l), (FQ, 128))
        o_ref[...] = jnp.where(lo, outs[0], outs[1])

    return _pcall(
        body, name="fox_fwd", grid=(4, T // FQ),
        in_specs=_fox_specs() + [pl.BlockSpec((FQ, 128), lambda h, i: (i, 0)),
                                 pl.BlockSpec((None, 2, T), lambda h, i: (h, 0, 0))],
        out_specs=[pl.BlockSpec((FQ, 128), lambda h, i: (i, h)), pl.BlockSpec((FQ, 256), lambda h, i: (i, h))],
        out_shape=[jax.ShapeDtypeStruct((T, 512), F32), jax.ShapeDtypeStruct((T, 1024), F32)],
        compiler_params=pltpu.CompilerParams(dimension_semantics=("arbitrary", "arbitrary"), vmem_limit_bytes=VMEM_BIG),
    )(proj, proj, proj, c, ct3)


def _fox_bwd(proj, c, ct3, o, lse, do, after=()):
    def body(q_ref, k_ref, v_ref, c_ref, ct_ref, o_ref, l_ref, do_ref, dq_ref, dkb_ref, dvb_ref, dct_ref, dcq_ref,
             dk_ref, dv_ref):
        i = pl.program_id(1)

        @pl.when(i == 0)
        def _():
            dk_ref[...] = jnp.zeros_like(dk_ref)
            dv_ref[...] = jnp.zeros_like(dv_ref)
            dct_ref[...] = jnp.zeros_like(dct_ref)

        lo = _lane_lo(FQ)
        causal = lax.broadcasted_iota(jnp.int32, (FQ, FQ), 1) <= lax.broadcasted_iota(jnp.int32, (FQ, FQ), 0)
        q = q_ref[...]
        do_v = do_ref[...]
        prod = do_v * o_ref[...]
        qs = [_half(q, lo, a, SCALE) for a in range(2)]
        dos = [_half(do_v, lo, a) for a in range(2)]
        deltas = [jnp.sum(jnp.where(lo if a == 0 else jnp.logical_not(lo), prod, 0.0), axis=-1, keepdims=True)
                  for a in range(2)]
        cqs = [_lane_pick(c_ref[...], 2 * pl.program_id(0) + a) for a in range(2)]
        las = [l_ref[:, 128 * a:128 * a + 1] for a in range(2)]

        def tile(off, carry, diagonal):
            kblk = k_ref[pl.ds(off, FQ), :]
            vblk = v_ref[pl.ds(off, FQ), :]
            new = []
            dk = jnp.zeros((128, FQ), F32)
            dv = jnp.zeros((128, FQ), F32)
            for a in range(2):
                dq_acc, rs = carry[a]
                s = _nt(qs[a], kblk) + (cqs[a] - ct_ref[a:a + 1, pl.ds(off, FQ)])
                if diagonal:
                    s = jnp.where(causal, s, NEG)
                p = jnp.exp(s - las[a])
                ds = p * (_nt(dos[a], vblk) - deltas[a])
                dsb = ds.astype(BF16)
                dk = dk + _tn(qs[a], dsb)
                dv = dv + _tn(dos[a], p.astype(BF16))
                dct_ref[a:a + 1, pl.ds(off, FQ)] -= jnp.sum(ds, axis=0, keepdims=True)
                new.append((dq_acc + _nn(dsb, kblk), rs + jnp.sum(ds, axis=-1, keepdims=True)))
            dk_ref[:, pl.ds(off, FQ)] += dk
            dv_ref[:, pl.ds(off, FQ)] += dv
            return tuple(new)

        init = (jnp.zeros((FQ, 128), F32), jnp.zeros((FQ, 1), F32))
        carry = lax.fori_loop(0, i, lambda kb, c: tile(pl.multiple_of(kb * FQ, FQ), c, False), (init, init))
        carry = tile(pl.multiple_of(i * FQ, FQ), carry, True)
        lane = lax.broadcasted_iota(jnp.int32, (FQ, 128), 1)
        dcq_ref[...] = jnp.where(lane == 0, carry[0][1], jnp.where(lane == 1, carry[1][1], 0.0))
        dq_ref[...] = (jnp.where(lo, carry[0][0], carry[1][0]) * SCALE).astype(BF16)

        @pl.when(i == T // FQ - 1)
        def _():
            dkb_ref[...] = dk_ref[...].T.astype(BF16)
            dvb_ref[...] = dv_ref[...].T.astype(BF16)

    blk = pl.BlockSpec((FQ, 128), lambda h, i: (i, h))
    wide = pl.BlockSpec((FQ, 256), lambda h, i: (i, h))
    rows = pl.BlockSpec((None, 2, T), lambda h, i: (h, 0, 0))
    col = pl.BlockSpec((T, 128), lambda h, i: (0, h))
    return _pcall(
        _behind(body, 8, after), name="fox_bwd", grid=(4, T // FQ),
        in_specs=_fox_specs() + [pl.BlockSpec((FQ, 128), lambda h, i: (i, 0)), rows, blk, wide, blk] + [ANY_SPEC] * len(after),
        out_specs=[blk, col, col, rows, pl.BlockSpec((None, FQ, 128), lambda h, i: (h, i, 0))],
        out_shape=[jax.ShapeDtypeStruct((T, 512), BF16), jax.ShapeDtypeStruct((T, 512), BF16),
                   jax.ShapeDtypeStruct((T, 512), BF16), jax.ShapeDtypeStruct((4, 2, T), F32),
                   jax.ShapeDtypeStruct((4, T, 128), F32)],
        scratch_shapes=[pltpu.VMEM((128, T), F32), pltpu.VMEM((128, T), F32)],
        compiler_params=pltpu.CompilerParams(dimension_semantics=("arbitrary", "arbitrary"), vmem_limit_bytes=VMEM_BIG),
    )(proj, proj, proj, c, ct3, o, lse, do, *after)


def _rel_onehot():
    ridx = lax.broadcasted_iota(jnp.int32, (NREL_PAD, VW), 0)
    j = lax.broadcasted_iota(jnp.int32, (NREL_PAD, VW), 1)
    return jnp.where(ridx == jnp.clip(TQ + LEFT - 1 - j, -128, 128) + 128, 1.0, 0.0).astype(F32)


def _relvec_fwd(tbl):
    def body(t_ref, v_ref):
        v_ref[...] = _hdot(t_ref[...], _rel_onehot())

    return _one_call(body, "relvec_fwd", [tbl], [((8, VW), F32)])[0]


def _relvec_bwd(gv):
    def body(g_ref, t_ref):
        t_ref[...] = lax.dot_general(g_ref[...], _rel_onehot(), (((1,), (1,)), ((), ())),
                                     preferred_element_type=F32, precision=lax.Precision.HIGHEST)

    return _one_call(body, "relvec_bwd", [gv], [((8, NREL_PAD), F32)])[0]


def _chk_bias(vt_ref, a, hidden):
    vb = jnp.broadcast_to(vt_ref[a:a + 1, :], (TQ, VW))
    y = pltpu.roll(vb, VW - (TQ - 1), 1, stride=1, stride_axis=0)[:, :WIN]
    cr = lax.broadcasted_iota(jnp.int32, (TQ, WIN), 0) // 64
    m = lax.broadcasted_iota(jnp.int32, (TQ, WIN), 1)
    return jnp.where((m // 64 >= cr) & (m // 64 <= cr + 8) & (m >= hidden), y, NEG)


def _chk_specs():
    return [pl.BlockSpec((TQ, 128), lambda h, i: (i, CHK0 // 128 + h)),
            pl.BlockSpec((T + LEFT, 128), lambda h, i: (0, h)),
            pl.BlockSpec((T + LEFT, 128), lambda h, i: (0, 4 + h)),
            pl.BlockSpec((None, 2, VW), lambda h, i: (h, 0, 0))]


def _chk_fwd(proj, kvp, vt3, after=()):
    def body(q_ref, k_ref, v_ref, vt_ref, o_ref, l_ref, bias_ref):
        i = pl.program_id(1)

        @pl.when(i == 0)
        def _():
            for first in range(3):
                for a in range(2):
                    bias_ref[first, a] = _chk_bias(vt_ref, a, max(LEFT - first * TQ, 0))

        lo = _lane_lo()
        off = pl.multiple_of(i * TQ, TQ)
        kw = k_ref[pl.ds(off, WIN), :]
        vw = v_ref[pl.ds(off, WIN), :]
        bias_at = jnp.minimum(i, 2)
        q = q_ref[...]
        outs = []
        for a in range(2):
            s = _nt(_half(q, lo, a, SCALE), kw) + bias_ref[bias_at, a]
            m = jnp.max(s, axis=-1, keepdims=True)
            p = jnp.exp(s - m)
            l = jnp.sum(p, axis=-1, keepdims=True)
            outs.append(_nn(p.astype(BF16), vw) / l)
            l_ref[:, 128 * a:128 * a + 128] = jnp.broadcast_to(m + jnp.log(l), (TQ, 128))
        o_ref[...] = jnp.where(lo, outs[0], outs[1])

    return _pcall(
        _behind(body, 4, after), name="chk_fwd", grid=(4, T // TQ), in_specs=_chk_specs() + [ANY_SPEC] * len(after),
        out_specs=[pl.BlockSpec((TQ, 128), lambda h, i: (i, h)), pl.BlockSpec((TQ, 256), lambda h, i: (i, h))],
        out_shape=[jax.ShapeDtypeStruct((T, 512), F32), jax.ShapeDtypeStruct((T, 1024), F32)],
        scratch_shapes=[pltpu.VMEM((3, 2, TQ, WIN), F32)],
        compiler_params=pltpu.CompilerParams(dimension_semantics=("arbitrary", "arbitrary")),
    )(proj, kvp, kvp, vt3, *after)


def _chk_bwd(proj, kvp, vt3, o, lse, do, after=()):
    nq = T // TQ

    def body(q_ref, k_ref, v_ref, vt_ref, o_ref, l_ref, do_ref, dq_ref, dkb_ref, dvb_ref, gv_ref, bias_ref, dsum_ref,
             dk_ref, dv_ref):
        i = pl.program_id(1)

        @pl.when(i == 0)
        def _():
            for first in range(3):
                for a in range(2):
                    bias_ref[first, a] = _chk_bias(vt_ref, a, max(LEFT - first * TQ, 0))
            dsum_ref[...] = jnp.zeros_like(dsum_ref)
            dk_ref[...] = jnp.zeros_like(dk_ref)
            dv_ref[...] = jnp.zeros_like(dv_ref)

        lo = _lane_lo()
        off = pl.multiple_of(i * TQ, TQ)
        kw = k_ref[pl.ds(off, WIN), :]
        vw = v_ref[pl.ds(off, WIN), :]
        bias_at = jnp.minimum(i, 2)
        q = q_ref[...]
        do_v = do_ref[...]
        prod = do_v * o_ref[...]
        dqs = []
        for a in range(2):
            keep = lo if a == 0 else jnp.logical_not(lo)
            qa = _half(q, lo, a, SCALE)
            doa = _half(do_v, lo, a)
            delta = jnp.sum(jnp.where(keep, prod, 0.0), axis=-1, keepdims=True)
            s = _nt(qa, kw) + bias_ref[bias_at, a]
            p = jnp.exp(s - l_ref[:, 128 * a:128 * a + 1])
            ds = p * (_nt(doa, vw) - delta)
            dsum_ref[a] += ds
            dsb = ds.astype(BF16)
            dk_ref[:, pl.ds(off, WIN)] += _tn(qa, dsb)
            dv_ref[:, pl.ds(off, WIN)] += _tn(doa, p.astype(BF16))
            dqs.append(_nn(dsb, kw))
        dq_ref[...] = (jnp.where(lo, dqs[0], dqs[1]) * SCALE).astype(BF16)

        @pl.when(i == nq - 1)
        def _():
            dkb_ref[...] = dk_ref[:, LEFT:].T.astype(BF16)
            dvb_ref[...] = dv_ref[:, LEFT:].T.astype(BF16)
            rr = lax.broadcasted_iota(jnp.int32, (TQ, TQ), 0)
            cc = lax.broadcasted_iota(jnp.int32, (TQ, TQ), 1)
            flip = jnp.where(rr + cc == TQ - 1, 1.0, 0.0).astype(F32)
            for a in range(2):
                dpad = jnp.concatenate([dsum_ref[a], jnp.zeros((TQ, VW - WIN), F32)], axis=1)
                z = pltpu.roll(_hdot(flip, dpad), 0, 1, stride=1, stride_axis=0)
                gv_ref[a:a + 1, :] = jnp.sum(z, axis=0, keepdims=True)

    blk = pl.BlockSpec((TQ, 128), lambda h, i: (i, h))
    wide = pl.BlockSpec((TQ, 256), lambda h, i: (i, h))
    col = pl.BlockSpec((T, 128), lambda h, i: (0, h))
    return _pcall(
        _behind(body, 7, after), name="chk_bwd", grid=(4, nq), in_specs=_chk_specs() + [blk, wide, blk] + [ANY_SPEC] * len(after),
        out_specs=[blk, col, col, pl.BlockSpec((None, 2, VW), lambda h, i: (h, 0, 0))],
        out_shape=[jax.ShapeDtypeStruct((T, 512), BF16), jax.ShapeDtypeStruct((T, 512), BF16),
                   jax.ShapeDtypeStruct((T, 512), BF16), jax.ShapeDtypeStruct((4, 2, VW), F32)],
        scratch_shapes=[pltpu.VMEM((3, 2, TQ, WIN), F32), pltpu.VMEM((2, TQ, WIN), F32),
                        pltpu.VMEM((128, T + LEFT), F32), pltpu.VMEM((128, T + LEFT), F32)],
        compiler_params=pltpu.CompilerParams(dimension_semantics=("arbitrary", "arbitrary")),
    )(proj, kvp, kvp, vt3, o, lse, do, *after)


def _zero_at_start(*refs):
    @pl.when(pl.program_id(0) == 0)
    def _():
        for r in refs:
            r[...] = jnp.zeros_like(r)


def _ffn_step(h3, x2, tgt, w1_t, w2, g_post, g_pre):
    def body(h_ref, x2_ref, t_ref, w1_ref, w2_ref, gp_ref, gf_ref, dx2_ref, da_ref, dy_ref, r_ref, loss_ref, dgp_ref, dgf_ref):
        _zero_at_start(loss_ref, dgp_ref, dgf_ref)
        w1, w2v = _w(w1_ref), _w(w2_ref)
        ra = jnp.maximum(_nt(h_ref[...], w1), 0.0)
        r = jnp.square(ra).astype(BF16)
        r_ref[...] = r
        y = _nn(r, w2v)
        x2v = x2_ref[...]
        e = x2v + _rms(y, gp_ref[...]) - t_ref[...]
        loss_ref[...] += 0.5 * jnp.sum(jnp.sum(e * e, axis=-1, keepdims=True) * (1.0 / D))
        dx3 = e * (1.0 / D)
        dy, dgp = _rms_bwd(y, gp_ref[...], dx3)
        dgp_ref[...] += dgp
        dyb = dy.astype(BF16)
        dy_ref[...] = dyb
        da = (_nt(dyb, w2v) * (2.0 * ra)).astype(BF16)
        da_ref[...] = da
        dh, dgf = _rms_bwd(x2v, gf_ref[...], _nn(da, w1))
        dgf_ref[...] += dgf
        dx2_ref[...] = dx3 + dh

    return _tok_call(body, "ffn_step", [h3, x2, tgt], [w1_t, w2, g_post, g_pre],
                     [(D, F32), (DFF, BF16), (D, BF16), (DFF, BF16)], [(8, 128), (1, D), (1, D)], vmem=VMEM_BIG)


def _mem_bwd(dx2, ym, x1, qm, km, vm, w_mo, w_mq, g_post, g_pre, after=()):
    def body(dx2_ref, ym_ref, x1_ref, q_ref, k_ref, v_ref, wo_ref, wq_ref, gp_ref, gm_ref,
             dx1_ref, dym_ref, dq_ref, dk_ref, dv_ref, dgp_ref, dgm_ref, dom_ref):
        _zero_at_start(dk_ref, dv_ref, dgp_ref, dgm_ref)
        dx2_v = dx2_ref[...]
        dym, dgp = _rms_bwd(ym_ref[...], gp_ref[...], dx2_v)
        dgp_ref[...] += dgp
        dymb = dym.astype(BF16)
        dym_ref[...] = dymb
        dom_ref[...] = _nt(dymb, _w(wo_ref)).astype(BF16)
        for h in range(MEM_HEADS):
            sl = slice(h * MEM_HD, (h + 1) * MEM_HD)
            qh, kh, doh = q_ref[:, sl], k_ref[:, sl], dom_ref[:, sl]
            s = _nt(qh, kh) * MEM_SCALE
            p = jnp.exp(s - jnp.max(s, axis=-1, keepdims=True))
            p = p / jnp.sum(p, axis=-1, keepdims=True)
            dp = _nt(doh, v_ref[:, sl])
            ds = (p * (dp - jnp.sum(p * dp, axis=-1, keepdims=True))).astype(BF16)
            dq_ref[:, sl] = (_nn(ds, kh) * MEM_SCALE).astype(BF16)
            dk_ref[:, sl] += _tn(ds, qh) * MEM_SCALE
            dv_ref[:, sl] += _tn(p.astype(BF16), doh)
        dh, dgm = _rms_bwd(x1_ref[...], gm_ref[...], _nt(dq_ref[...], _w(wq_ref)))
        dgm_ref[...] += dgm
        dx1_ref[...] = dx2_v + dh

    tiled = pl.BlockSpec((TM_WIDE, D), lambda i: (i, 0))
    in_specs = [tiled] * 4 + [_resident(a) for a in (km, vm, w_mo, w_mq, g_post, g_pre)] + [ANY_SPEC] * len(after)
    w_mo, w_mq = w_mo[0], w_mq[0]
    kv = pl.BlockSpec((NMEM, D), lambda i: (0, 0))
    vec = pl.BlockSpec((1, D), lambda i: (0, 0))
    return _pcall(
        _behind(body, 10, after), name="mem_bwd", grid=(T // TM_WIDE,), in_specs=in_specs,
        out_specs=[tiled, tiled, tiled, kv, kv, vec, vec],
        out_shape=[jax.ShapeDtypeStruct((T, D), F32), jax.ShapeDtypeStruct((T, D), BF16),
                   jax.ShapeDtypeStruct((T, D), BF16), jax.ShapeDtypeStruct((NMEM, D), F32),
                   jax.ShapeDtypeStruct((NMEM, D), F32), jax.ShapeDtypeStruct((1, D), F32),
                   jax.ShapeDtypeStruct((1, D), F32)],
        scratch_shapes=[pltpu.VMEM((TM_WIDE, D), BF16)],
        compiler_params=pltpu.CompilerParams(dimension_semantics=("arbitrary",), vmem_limit_bytes=VMEM_BIG),
    )(dx2, ym, x1, qm, km, vm, w_mo, w_mq, g_post, g_pre, *after)


def _memkv_bwd(dkm, dvm, mem, w_mk, w_mv):
    def body(dk_ref, dv_ref, m_ref, wk_ref, wv_ref, dg_ref):
        dmn = _nt(dk_ref[...].astype(BF16), _w(wk_ref)) + _nt(dv_ref[...].astype(BF16), _w(wv_ref))
        mv = m_ref[...]
        dg_ref[...] = jnp.sum(dmn * (mv * _rstd(mv)), axis=0, keepdims=True)

    return _one_call(body, "memkv_bwd", [dkm, dvm, mem, w_mk, w_mv], [((1, D), F32)], vmem=VMEM_BIG)[0]


def _postmix_bwd(dx1, z, o_f, o_c, w_out, g_post, g_fo, g_co, after=()):
    def body(dx1_ref, z_ref, of_ref, oc_ref, wo_ref, gp_ref, gfo_ref, gco_ref,
             dz_ref, dof_ref, doc_ref, dgp_ref, dgfo_ref, dgco_ref):
        _zero_at_start(dgp_ref, dgfo_ref, dgco_ref)
        dz, dgp = _rms_bwd(z_ref[...], gp_ref[...], dx1_ref[...])
        dgp_ref[...] += dgp
        dzb = dz.astype(BF16)
        dz_ref[...] = dzb
        dy = _nt(dzb, _w(wo_ref))
        dof, dgfo = _rms_bwd(of_ref[...], gfo_ref[...], dy[:, :512])
        doc, dgco = _rms_bwd(oc_ref[...], gco_ref[...], dy[:, 512:])
        dof_ref[...] = dof
        doc_ref[...] = doc
        dgfo_ref[...] += dgfo
        dgco_ref[...] += dgco

    return _tok_call(body, "postmix_bwd", [dx1, z, o_f, o_c], [w_out, g_post, g_fo, g_co],
                     [(D, BF16), (512, F32), (512, F32)], [(1, D), (1, 512), (1, 512)], tm=TM_WIDE, vmem=VMEM_BIG,
                     after=after)


def _premix_bwd(dx1, x, pieces, win_t, g_pre):
    def body(dx1_ref, x_ref, *refs):
        piece_refs, (w_ref, g_ref, dx_ref, dp_ref, dg_ref) = refs[:len(pieces)], refs[len(pieces):]
        _zero_at_start(dg_ref)
        col = 0
        for p in piece_refs:
            dp_ref[:, col:col + p.shape[1]] = p[...]
            col += p.shape[1]
        dh, dg = _rms_bwd(x_ref[...], g_ref[...], _nn(dp_ref[...], w_ref[...]))
        dg_ref[...] += dg
        dx_ref[...] = dx1_ref[...] + dh

    return _tok_call(body, "premix_bwd", [dx1, x] + list(pieces), [win_t, g_pre], [(D, F32), (PROJ, BF16)], [(1, D)],
                     tm=TM_WIDE, vmem=VMEM_BIG)


def _wgrad(a, b, name):
    k, m = a.shape
    n = b.shape[1]
    tm = 640 if m % 640 == 0 and m > 1024 else min(m, 512)
    tn = min(n, 1024)

    def body(a_ref, b_ref, o_ref):
        o_ref[...] = _tn(a_ref[...].astype(BF16), b_ref[...].astype(BF16))

    return _pcall(
        body, name=name, grid=(m // tm, n // tn),
        in_specs=[pl.BlockSpec((k, tm), lambda i, j: (0, i)), pl.BlockSpec((k, tn), lambda i, j: (0, j))],
        out_specs=pl.BlockSpec((tm, tn), lambda i, j: (i, j)),
        out_shape=jax.ShapeDtypeStruct((m, n), F32),
        compiler_params=pltpu.CompilerParams(dimension_semantics=("arbitrary", "arbitrary"), vmem_limit_bytes=VMEM_BIG),
    )(a, b)


def _wgrad_group(name, pairs, rows):
    def body(*refs):
        o_ref = refs[-1]
        for k in range(len(pairs)):
            g = _tn(refs[2 * k][...].astype(BF16), refs[2 * k + 1][...].astype(BF16))
            o_ref[k * rows:(k + 1) * rows, :] = g.astype(BF16)

    in_specs, ops = [], []
    for a, b in pairs:
        in_specs += [pl.BlockSpec((a.shape[0], rows), lambda j: (0, j)), _resident(b)]
        ops += [a, b]
    return _pcall(
        body, name=name, grid=(8,), in_specs=in_specs,
        out_specs=pl.BlockSpec((None, len(pairs) * rows, D), lambda j: (j, 0, 0)),
        out_shape=jax.ShapeDtypeStruct((8, len(pairs) * rows, D), BF16),
        compiler_params=pltpu.CompilerParams(dimension_semantics=("arbitrary",), vmem_limit_bytes=VMEM_BIG),
    )(*ops)


def _adam_math(w, g, m, v):
    m2 = ADAM_B1 * m + (1.0 - ADAM_B1) * g
    v2 = ADAM_B2 * v + (1.0 - ADAM_B2) * jnp.square(g)
    m_hat = m2 / (1.0 - ADAM_B1 ** ADAM_STEP)
    v_hat = v2 / (1.0 - ADAM_B2 ** ADAM_STEP)
    delta = -ADAM_LR * (m_hat / (jnp.sqrt(v_hat) + ADAM_EPS) + ADAM_WD * w)
    return delta, m2, v2


def _adamw_small(gparts, ws, ms, vs):
    n = len(SMALL)

    def body(g_ref, *refs):
        w_refs, m_refs, v_refs = refs[:n], refs[n:2 * n], refs[2 * n:3 * n]
        outs, sum_ref = refs[3 * n:-1], refs[-1]
        g = g_ref[0]
        for k in range(1, 8):
            g = g + g_ref[k]
        sum_ref[...] = g
        outs[0][...] = sum_ref[17:18, 0:128]
        for t, name in enumerate(SMALL):
            r0, nr, c0, nc = SMALL_SLOT[name]
            gt = sum_ref[r0:r0 + nr, c0:c0 + nc]
            out = (gt,) + _adam_math(w_refs[t][...], gt, m_refs[t][...], v_refs[t][...])
            for o_ref, val in zip(outs[1 + 4 * t:5 + 4 * t], out):
                o_ref[...] = val

    whole = lambda s: pl.BlockSpec(s, lambda i, nd=len(s): (0,) * nd)
    ins = [gparts] + list(ws) + list(ms) + list(vs)
    out_shapes = [(1, 128)] + [a.shape for a in ws for _ in range(4)]
    return _pcall(
        body, name="adamw_small", grid=(1,), in_specs=[whole(a.shape) for a in ins],
        out_specs=[whole(s) for s in out_shapes], out_shape=[jax.ShapeDtypeStruct(s, F32) for s in out_shapes],
        scratch_shapes=[pltpu.VMEM((SMALL_ROWS, D), F32)],
        compiler_params=pltpu.CompilerParams(dimension_semantics=("arbitrary",)),
    )(*ins)


def _row_tile(rows):
    return next(t for t in (512, 400, 320) if rows % t == 0)


def _add_halves(g4, theirs, core, name):
    rows = g4.shape[2]
    tr = _row_tile(rows)

    def body(c_ref, a_ref, b_ref, o_ref):
        o_ref[...] = (a_ref[...].astype(F32) + b_ref[...].astype(F32)).astype(BF16)

    grid_spec = pltpu.PrefetchScalarGridSpec(
        num_scalar_prefetch=1, grid=(4, rows // tr),
        in_specs=[pl.BlockSpec((None, None, tr, D), lambda j, i, c: (j, c[0], i, 0)),
                  pl.BlockSpec((None, None, tr, D), lambda j, i, c: (j, 0, i, 0))],
        out_specs=pl.BlockSpec((None, tr, D), lambda j, i, c: (j, i, 0)))
    return _pcall(
        body, name=name, grid_spec=grid_spec, out_shape=jax.ShapeDtypeStruct((4, rows, D), BF16),
        compiler_params=pltpu.CompilerParams(dimension_semantics=("arbitrary", "arbitrary")),
    )(core, g4, theirs)


def _sum_adam(own, got, order, r0, w, m, v, name, transposed=False):
    n = w.shape[1] if transposed else w.shape[0]
    tr = min(n, 256) if n % 8 == 0 else n
    rows = tr if n % 8 == 0 else own.shape[1]

    def body(o_ref, a_ref, b_ref, c_ref, d_ref, w_ref, m_ref, v_ref, g_ref, dl_ref, m2_ref, v2_ref):
        f = lambda r: r[0:tr, :].astype(F32)
        g = ((f(a_ref) + f(b_ref)) + f(c_ref)) + f(d_ref)
        g = g.T if transposed else g
        g_ref[...] = g
        dl_ref[...], m2_ref[...], v2_ref[...] = _adam_math(w_ref[...], g, m_ref[...], v_ref[...])

    slot = lambda k: pl.BlockSpec((None, rows, D), lambda i, o: (o[k], r0 // rows + i, 0))
    wspec = pl.BlockSpec((D, tr), lambda i, o: (0, i)) if transposed else pl.BlockSpec((tr, D), lambda i, o: (i, 0))
    grid_spec = pltpu.PrefetchScalarGridSpec(
        num_scalar_prefetch=1, grid=(n // tr,), in_specs=[slot(0), slot(1), slot(2), slot(3), wspec, wspec, wspec],
        out_specs=[wspec] * 4)
    return _pcall(
        body, name=name, grid_spec=grid_spec, out_shape=[jax.ShapeDtypeStruct(w.shape, F32)] * 4,
        compiler_params=pltpu.CompilerParams(dimension_semantics=("arbitrary",)),
    )(order, own, got, got, got, w, m, v)


def _sum_adam_rows(own, got, order, ws, ms, vs, name):
    n, rows = len(ws), ws[0].shape[0]

    def body(o_ref, a_ref, b_ref, c_ref, d_ref, *refs):
        ins, outs = refs[:3 * n], refs[3 * n:]
        for t in range(n):
            r = slice(t * rows, (t + 1) * rows)
            f = lambda ref: ref[r, :].astype(F32)
            g = ((f(a_ref) + f(b_ref)) + f(c_ref)) + f(d_ref)
            out = (g,) + _adam_math(ins[t][...], g, ins[n + t][...], ins[2 * n + t][...])
            for o, val in zip(outs[4 * t:4 * t + 4], out):
                o[...] = val

    slot = lambda k: pl.BlockSpec((None, n * rows, D), lambda i, o: (o[k], 0, 0), pipeline_mode=pl.Buffered(1))
    wspec = pl.BlockSpec((rows, D), lambda i, o: (0, 0), pipeline_mode=pl.Buffered(1))
    grid_spec = pltpu.PrefetchScalarGridSpec(
        num_scalar_prefetch=1, grid=(1,), in_specs=[slot(0), slot(1), slot(2), slot(3)] + [wspec] * (3 * n),
        out_specs=[pl.BlockSpec((rows, D), lambda i, o: (0, 0))] * (4 * n))
    return _pcall(
        body, name=name, grid_spec=grid_spec, out_shape=[jax.ShapeDtypeStruct((rows, D), F32)] * (4 * n),
        compiler_params=pltpu.CompilerParams(dimension_semantics=("arbitrary",), vmem_limit_bytes=VMEM_BIG),
    )(order, own, got, got, got, *ws, *ms, *vs)


def _place():
    return lax.axis_index("x"), lax.axis_index("y"), lax.axis_index("c")


def _allgather(block, name, after=()):
    rows = block.shape[0]
    split = (rows // 2 + 15) // 16 * 16

    def body(x_ref, out_ref, token, send_sems, recv_sems, local_sem):
        token[...] = jnp.zeros_like(token)
        x, y, c = _place()
        me, sib = (x, y, c), (x, y, 1 - c)
        xn, yn, dg = (1 - x, y), (x, 1 - y), (1 - x, 1 - y)
        lo, hi = pl.ds(0, split), pl.ds(split, rows - split)

        def copy(k, blk, to, part=None, src=None):
            index = 4 * blk[0] + 2 * blk[1] + blk[2]
            view = out_ref.at[index] if part is None else out_ref.at[index, part]
            return pltpu.make_async_remote_copy(
                src_ref=view if src is None else src, dst_ref=view,
                send_sem=send_sems.at[k], recv_sem=recv_sems.at[k], device_id=to, device_id_type=MESH)

        def start(*copies):
            for cp in copies:
                cp.start()
            return list(copies)

        mine = pltpu.make_async_copy(x_ref, out_ref.at[4 * x + 2 * y + c], local_sem)
        mine.start()
        sent = start(copy(0, me, sib, src=x_ref), copy(1, me, (*xn, c), src=x_ref), copy(2, me, (*yn, c), src=x_ref))
        copy(1, (*xn, c), me).wait_recv()
        sent += start(copy(3, (*xn, c), sib), copy(5, (*xn, c), (*yn, c), part=lo))
        copy(2, (*yn, c), me).wait_recv()
        sent += start(copy(4, (*yn, c), sib), copy(6, (*yn, c), (*xn, c), part=hi))
        copy(5, (*dg, c), me, part=lo).wait_recv()
        copy(6, (*dg, c), me, part=hi).wait_recv()
        sent += start(copy(7, (*dg, c), sib))
        for k, blk in ((0, sib), (3, (*xn, 1 - c)), (4, (*yn, 1 - c)), (7, (*dg, 1 - c))):
            copy(k, blk, me).wait_recv()
        for cp in sent:
            cp.wait_send()
        mine.wait()

    return _pcall(
        _behind(body, 1, after), name=name,
        out_shape=[jax.ShapeDtypeStruct((8,) + block.shape, block.dtype), jax.ShapeDtypeStruct((8, 128), F32)],
        in_specs=[pl.BlockSpec(memory_space=pl.ANY)] * (1 + len(after)),
        out_specs=[pl.BlockSpec(memory_space=pl.ANY), pl.BlockSpec(memory_space=pltpu.VMEM)],
        scratch_shapes=[pltpu.SemaphoreType.DMA((8,)), pltpu.SemaphoreType.DMA((8,)), pltpu.SemaphoreType.DMA(())],
        compiler_params=pltpu.CompilerParams(has_side_effects=True),
    )(block, *after)


HBM_SPEC = pl.BlockSpec(memory_space=pltpu.HBM)
SEM_SPEC = pl.BlockSpec(memory_space=pltpu.SEMAPHORE)
ANY_SPEC = pl.BlockSpec(memory_space=pl.ANY)
EFFECT = pltpu.SideEffectType.DATAFLOW_SIDE_EFFECTING


def _in_hbm(a):
    return pltpu.with_memory_space_constraint(a, pltpu.HBM)


def _start_copies(name, src, land_shape, plan, n):
    def body(src_ref, land_ref, send_sems, recv_sems, src_thru, land_thru, token):
        for k, (s, d, to, _) in enumerate(plan(src_ref, land_ref)):
            pltpu.make_async_remote_copy(src_ref=s, dst_ref=d, send_sem=send_sems.at[k], recv_sem=recv_sems.at[k],
                                         device_id=to, device_id_type=MESH).start()
        token[...] = jnp.zeros_like(token)

    return _pcall(
        body, name=name,
        out_shape=(pltpu.SemaphoreType.DMA((n,)), pltpu.SemaphoreType.DMA((n,)), pltpu.HBM(src.shape, src.dtype),
                   pltpu.HBM(land_shape, src.dtype), jax.ShapeDtypeStruct((8, 128), F32)),
        in_specs=(HBM_SPEC, HBM_SPEC),
        out_specs=(SEM_SPEC, SEM_SPEC, HBM_SPEC, HBM_SPEC, pl.BlockSpec(memory_space=pltpu.VMEM)),
        input_output_aliases={0: 2, 1: 3}, compiler_params=pltpu.CompilerParams(has_side_effects=EFFECT),
    )(_in_hbm(src), _in_hbm(lax.empty(land_shape, src.dtype)))


def _wait_copies(name, started, after, plan):
    send_sems, recv_sems, src_thru, land_thru, _ = started

    def body(src_ref, land_ref, send_sems, recv_sems, *rest):
        for k, (s, _, to, mine) in enumerate(plan(src_ref, land_ref)):
            cp = pltpu.make_async_remote_copy(src_ref=s, dst_ref=mine, send_sem=send_sems.at[k],
                                              recv_sem=recv_sems.at[k], device_id=to, device_id_type=MESH)
            cp.wait_send()
            cp.wait_recv()

    return _pcall(
        body, name=name,
        out_shape=(pltpu.HBM(src_thru.shape, src_thru.dtype), pltpu.HBM(land_thru.shape, land_thru.dtype)),
        in_specs=(HBM_SPEC, HBM_SPEC, SEM_SPEC, SEM_SPEC) + (ANY_SPEC,) * len(after), out_specs=(HBM_SPEC, HBM_SPEC),
        input_output_aliases={0: 0, 1: 1}, compiler_params=pltpu.CompilerParams(has_side_effects=EFFECT),
    )(src_thru, land_thru, send_sems, recv_sems, *after)


def _start_inplace(name, buf, plan, n):
    def body(buf_ref, send_sems, recv_sems, buf_thru, token):
        for k, (s, d, to, _) in enumerate(plan(buf_ref, buf_ref)):
            pltpu.make_async_remote_copy(src_ref=s, dst_ref=d, send_sem=send_sems.at[k], recv_sem=recv_sems.at[k],
                                         device_id=to, device_id_type=MESH).start()
        token[...] = jnp.zeros_like(token)

    return _pcall(
        body, name=name,
        out_shape=(pltpu.SemaphoreType.DMA((n,)), pltpu.SemaphoreType.DMA((n,)), pltpu.HBM(buf.shape, buf.dtype),
                   jax.ShapeDtypeStruct((8, 128), F32)),
        in_specs=(HBM_SPEC,), out_specs=(SEM_SPEC, SEM_SPEC, HBM_SPEC, pl.BlockSpec(memory_space=pltpu.VMEM)),
        input_output_aliases={0: 2}, compiler_params=pltpu.CompilerParams(has_side_effects=EFFECT),
    )(_in_hbm(buf))


def _wait_inplace(name, started, after, plan):
    send_sems, recv_sems, buf_thru, _ = started

    def body(buf_ref, send_sems, recv_sems, *rest):
        for k, (s, _, to, mine) in enumerate(plan(buf_ref, buf_ref)):
            cp = pltpu.make_async_remote_copy(src_ref=s, dst_ref=mine, send_sem=send_sems.at[k],
                                              recv_sem=recv_sems.at[k], device_id=to, device_id_type=MESH)
            cp.wait_send()
            cp.wait_recv()

    return _pcall(
        body, name=name, out_shape=pltpu.HBM(buf_thru.shape, buf_thru.dtype),
        in_specs=(HBM_SPEC, SEM_SPEC, SEM_SPEC) + (ANY_SPEC,) * len(after), out_specs=HBM_SPEC,
        input_output_aliases={0: 0}, compiler_params=pltpu.CompilerParams(has_side_effects=EFFECT),
    )(buf_thru, send_sems, recv_sems, *after)


def _gather_plan(src_ref, land_ref):
    x, y, c = _place()
    peers = [(x, y, 1 - c), (1 - x, y, c), (x, 1 - y, c)]
    return [(src_ref, land_ref.at[4 * x + 2 * y + c], p, land_ref.at[4 * p[0] + 2 * p[1] + p[2]]) for p in peers]


def _relay_plan(buf_ref, _):
    x, y, c = _place()
    slot = lambda p, pc: 4 * p[0] + 2 * p[1] + pc
    xn, yn, dg, sib = (1 - x, y), (x, 1 - y), (1 - x, 1 - y), (x, y, 1 - c)
    half = buf_ref.shape[1] // 2
    lo, hi = pl.ds(0, half), pl.ds(half, half)
    return [(buf_ref.at[slot(xn, c)], buf_ref.at[slot(xn, c)], sib, buf_ref.at[slot(xn, 1 - c)]),
            (buf_ref.at[slot(yn, c)], buf_ref.at[slot(yn, c)], sib, buf_ref.at[slot(yn, 1 - c)]),
            (buf_ref.at[slot(xn, c), lo], buf_ref.at[slot(xn, c), lo], (*yn, c), buf_ref.at[slot(dg, c), lo]),
            (buf_ref.at[slot(yn, c), hi], buf_ref.at[slot(yn, c), hi], (*xn, c), buf_ref.at[slot(dg, c), hi])]


def _swap_plan(src_ref, land_ref):
    x, y, c = _place()
    return [(src_ref.at[:, pl.ds(1 - c, 1)], land_ref, (x, y, 1 - c), land_ref)]


def _exchange_plan(src_ref, land_ref):
    x, y, c = _place()
    chips = [(1 - x, y), (x, 1 - y), (1 - x, 1 - y)]
    return [(src_ref.at[2 * px + py], land_ref.at[2 * x + y], (px, py, c), land_ref.at[2 * px + py]) for px, py in chips]


def _gather_forward(land, block):
    def body(land_ref, out_ref, send_sems, recv_sems):
        x, y, c = _place()
        chips = [(1 - x, 1 - y)]

        def copy(k, px, py, pc):
            blk = out_ref.at[4 * px + 2 * py + pc]
            return pltpu.make_async_remote_copy(src_ref=blk, dst_ref=blk, send_sem=send_sems.at[k],
                                                recv_sem=recv_sems.at[k], device_id=(x, y, 1 - c), device_id_type=MESH)

        sent = [copy(k, px, py, c) for k, (px, py) in enumerate(chips)]
        for cp in sent:
            cp.start()
        for k, (px, py) in enumerate(chips):
            copy(k, px, py, 1 - c).wait_recv()
        for cp in sent:
            cp.wait_send()

    land = _pcall(
        body, name="allgather_rest_forward", out_shape=jax.ShapeDtypeStruct(land.shape, land.dtype),
        in_specs=[ANY_SPEC], out_specs=ANY_SPEC, input_output_aliases={0: 0},
        scratch_shapes=[pltpu.SemaphoreType.DMA((1,)), pltpu.SemaphoreType.DMA((1,))],
        compiler_params=pltpu.CompilerParams(has_side_effects=True),
    )(land)

    rows = block.shape[0]
    tr = rows // 4

    def place(me_ref, x_ref, land_ref, out_ref):
        out_ref[...] = x_ref[...]

    x, y, c = _place()
    grid_spec = pltpu.PrefetchScalarGridSpec(
        num_scalar_prefetch=1, grid=(rows // tr,),
        in_specs=[pl.BlockSpec((tr, D), lambda i, me: (i, 0)), ANY_SPEC],
        out_specs=pl.BlockSpec((None, tr, D), lambda i, me: (me[0], i, 0)))
    return _pcall(
        place, name="allgather_rest_own", grid_spec=grid_spec, out_shape=jax.ShapeDtypeStruct(land.shape, land.dtype),
        input_output_aliases={2: 0}, compiler_params=pltpu.CompilerParams(dimension_semantics=("arbitrary",)),
    )((4 * x + 2 * y + c).reshape(1), block, land)


class _ReduceScatter:
    def __init__(self, name, g):
        self.name = name
        rows = g.shape[1]
        self.started = _start_copies(name + "_swap_start", g.reshape(4, 2, rows, D), (4, 1, rows, D), _swap_plan, 1)
        self.token = self.started[4]

    def halfway(self, after):
        g4, theirs = _wait_copies(self.name + "_swap_wait", self.started, after, _swap_plan)
        self.own = _add_halves(g4, theirs, lax.axis_index("c").reshape(1), self.name + "_add_halves")
        self.started = _start_copies(self.name + "_exch_start", self.own, self.own.shape, _exchange_plan, 3)
        self.token = self.started[4]

    def finish(self, after):
        own, got = _wait_copies(self.name + "_exch_wait", self.started, after, _exchange_plan)
        chip = 2 * lax.axis_index("x") + lax.axis_index("y")
        return own, got, (chip + jnp.arange(4, dtype=jnp.int32)) % 4


def _pack_small(p, loss):
    def body(*refs):
        o_ref = refs[-1]
        o_ref[...] = jnp.zeros_like(o_ref)
        for ref, name in zip(refs, SMALL):
            r0, nr, c0, nc = SMALL_SLOT[name]
            o_ref[r0:r0 + nr, c0:c0 + nc] = ref[...]
        o_ref[17:18, 0:128] = refs[len(SMALL)][0:1, :]

    return _one_call(body, "pack_small_grads", [p[n] for n in SMALL] + [loss], [((SMALL_ROWS, D), F32)])[0]


_GAP_DEV, _GAP_ROW = divmod(GATE0 + 8, N_IN)
_GAP = CHK0 - GATE0 - 8


def _in_rows_to_proj(g):
    runs = [(j, 0, N_IN, N_IN * j) for j in range(_GAP_DEV)]
    runs += [(_GAP_DEV, 0, _GAP_ROW, N_IN * _GAP_DEV), (_GAP_DEV, _GAP_ROW, N_IN, N_IN * _GAP_DEV + _GAP_ROW + _GAP)]
    runs += [(j, 0, N_IN, N_IN * j + _GAP) for j in range(_GAP_DEV + 1, 8)]

    def body(g_ref, o_ref, acc_ref):
        acc_ref[...] = jnp.zeros_like(acc_ref)
        for j, r0, r1, dest in runs:
            start, shift = dest // 16 * 16, dest % 16
            win = -(-(shift + r1 - r0) // 16) * 16
            r = lax.broadcasted_iota(jnp.int32, (win, R_IN), 0)
            c = lax.broadcasted_iota(jnp.int32, (win, R_IN), 1)
            move = jnp.where((c >= r0) & (c < r1) & (r == c - r0 + shift), 1.0, 0.0).astype(BF16)
            acc_ref[start:start + win, :] += _nn(move, g_ref[j])
        o_ref[...] = acc_ref[...].astype(BF16)

    return _pcall(
        body, name="w_in_layout", out_shape=jax.ShapeDtypeStruct((PROJ, D), BF16), grid=(1,),
        in_specs=[pl.BlockSpec(g.shape, lambda i: (0, 0, 0))], out_specs=pl.BlockSpec((PROJ, D), lambda i: (0, 0)),
        scratch_shapes=[pltpu.VMEM((PROJ, D), F32)],
        compiler_params=pltpu.CompilerParams(dimension_semantics=("arbitrary",), vmem_limit_bytes=VMEM_BIG),
    )(g)


def _proj_rows_to_in(g):
    pad = lambda a: jnp.pad(a, ((0, R_IN - a.shape[0]), (0, 0)))
    lo = N_IN * _GAP_DEV
    shards = [pad(g[N_IN * j:N_IN * (j + 1)]) for j in range(_GAP_DEV)]
    shards.append(pad(jnp.concatenate([g[lo:lo + _GAP_ROW], g[lo + _GAP_ROW + _GAP:lo + N_IN + _GAP]], axis=0)))
    shards += [pad(g[N_IN * j + _GAP:N_IN * (j + 1) + _GAP]) for j in range(_GAP_DEV + 1, 8)]
    return jnp.stack(shards)


def _local_grads(x, mem, tgt, win_t, gw_of, sm, on_grads, after=()):
    b_pad = jnp.pad(sm['b_fgt'], ((0, 0), (0, 120)))
    tbl = jnp.pad(sm['rel_bias'], ((0, 0), (0, NREL_PAD - 257)))

    h1, proj, flog = _premix_fwd(x, sm['g_mix_pre'], win_t, after)
    c = _gate_fwd(flog, b_pad)
    ct3 = c[:, :8].T.reshape(4, 2, T)
    o_f, lse_f = _fox_fwd(proj, c, ct3)
    vt3 = _relvec_fwd(tbl).reshape(4, 2, VW)
    kvp = jnp.pad(proj[:, CHK0 + 512:], ((LEFT, 0), (0, 0)))
    o_c, lse_c = _chk_fwd(proj, kvp, vt3, [gw_of('relay', [o_f])])
    gw = gw_of('done', [o_c])
    w_out, w_mq, w_mk, w_mv, w_mo, w1_t, w2 = (_wblk(gw, n) for n in ('w_out', 'w_mq', 'w_mk', 'w_mv', 'w_mo', 'w_ff1', 'w_ff2'))
    ycat, z, x1, h2, qm = _postmix_fwd(x, o_f, o_c, sm['g_fox_out'], sm['g_chk_out'], w_out,
                                       sm['g_mix_post'], sm['g_mem_pre'], w_mq)
    memn, km, vm = _memkv_fwd(mem, sm['g_mem_kv'], w_mk, w_mv)
    om, ym, x2, h3 = _mem_fwd(qm, x1, km, vm, w_mo, sm['g_mem_post'], sm['g_ff_pre'])

    gs = {}
    dx2, da, dy3, r, loss_acc, gs['g_ff_post'], gs['g_ff_pre'] = _ffn_step(h3, x2, tgt, w1_t, w2, sm['g_ff_post'],
                                                                         sm['g_ff_pre'])
    tok = on_grads('A', _wgrad_group("wgrad_ff", [(da, h3), (r, dy3)], 512), None)
    dx1, dym, dqm, dkm, dvm, gs['g_mem_post'], gs['g_mem_pre'] = _mem_bwd(
        dx2, ym, x1, qm, km, vm, w_mo, w_mq, sm['g_mem_post'], sm['g_mem_pre'], [tok])
    tok = on_grads('A halfway', None, [dx1])
    gs['g_mem_kv'] = _memkv_bwd(dkm, dvm, mem, w_mk, w_mv)
    dz, dof, doc, gs['g_mix_post'], gs['g_fox_out'], gs['g_chk_out'] = _postmix_bwd(
        dx1, z, o_f, o_c, w_out, sm['g_mix_post'], sm['g_fox_out'], sm['g_chk_out'], [tok])
    tok = on_grads('B', _wgrad_group("wgrad_mem_out", [(ycat, dz), (h2, dqm), (memn, dkm), (memn, dvm), (om, dym)], 128), None)
    dq_f, dk_f, dv_f, dct, dcq = _fox_bwd(proj, c, ct3, o_f, lse_f, dof, [tok])
    tok = on_grads('B halfway', None, [dq_f])
    dq_c, dk_c, dv_c, gv = _chk_bwd(proj, kvp, vt3, o_c, lse_c, doc, [tok])
    gs['rel_bias'] = _relvec_bwd(gv.reshape(8, VW))[:, :257]
    dc = jnp.pad(dct.reshape(8, T).T + dcq[:, :, :2].transpose(1, 0, 2).reshape(T, 8), ((0, 0), (0, 120)))
    dflog, db = _gate_bwd(dc, flog, b_pad)
    gs['b_fgt'] = db[0:1, :8]
    grad_x, dproj, gs['g_mix_pre'] = _premix_bwd(dx1, x, [dq_f, dk_f, dv_f, dflog, dq_c, dk_c, dv_c], win_t, sm['g_mix_pre'])
    on_grads('C', _proj_rows_to_in(_wgrad(dproj, h1, "wgrad_in")).astype(BF16), None)
    return loss_acc, grad_x, gs


def kernel(x, mem, w_in, b_fgt, rel_bias, g_fox_out, g_chk_out, w_out, g_mix_pre, g_mix_post, g_mem_kv, w_mq, w_mk, w_mv, w_mo, g_mem_pre, g_mem_post, w_ff1, w_ff2, g_ff_pre, g_ff_post, loss_target, m_w_in, m_b_fgt, m_rel_bias, m_g_fox_out, m_g_chk_out, m_w_out, m_g_mix_pre, m_g_mix_post, m_g_mem_kv, m_w_mq, m_w_mk, m_w_mv, m_w_mo, m_g_mem_pre, m_g_mem_post, m_w_ff1, m_w_ff2, m_g_ff_pre, m_g_ff_post, v_w_in, v_b_fgt, v_rel_bias, v_g_fox_out, v_g_chk_out, v_w_out, v_g_mix_pre, v_g_mix_post, v_g_mem_kv, v_w_mq, v_w_mk, v_w_mv, v_w_mo, v_g_mem_pre, v_g_mem_post, v_w_ff1, v_w_ff2, v_g_ff_pre, v_g_ff_post):
    args = dict(locals())
    two_d = lambda a: a.reshape(a.shape[-2:])
    w = {n: two_d(args[n]) for n in WEIGHTS}
    m = {n: two_d(args['m_' + n]) for n in WEIGHTS}
    v = {n: two_d(args['v_' + n]) for n in WEIGHTS}

    sm = {n: w[n] for n in SMALL}
    shard_in = jnp.pad(w['w_in'].T, ((0, R_IN - N_IN), (0, 0))).astype(BF16)
    gathered_in, zero = _allgather(shard_in, "allgather_w_in")
    win_t = _in_rows_to_proj(gathered_in)
    shard_rest = (jnp.concatenate([w['w_ff1'].T, w['w_ff2'], w['w_out'], w['w_mq'], w['w_mk'], w['w_mv'], w['w_mo']],
                                  axis=0) + zero[0, 0]).astype(BF16)
    gather = {'first': _start_copies("allgather_rest_start", shard_rest, (8, R_REST, D), _gather_plan, 3)}

    def gw_of(stage, after):
        if stage == 'relay':
            gather['block'], land = _wait_copies("allgather_rest_wait", gather['first'], after, _gather_plan)
            gather['second'] = _start_inplace("allgather_rest_relay_start", land, _relay_plan, 4)
            return gather['second'][3]
        land = _wait_inplace("allgather_rest_relay_wait", gather['second'], after, _relay_plan)
        return _gather_forward(land, gather['block'])

    rs = {}

    def on_grads(stage, g, after):
        if stage.endswith('halfway'):
            rs[stage[0]].halfway(after)
            return rs[stage[0]].token
        rs[stage] = _ReduceScatter("rs_" + stage.lower(), g)
        return rs[stage].token

    loss_local, grad_x, gs = _local_grads(x[0], mem[0], loss_target[0], win_t, gw_of, sm, on_grads, [gather['first'][4]])
    grads, deltas, new_m, new_v = {}, {}, {}, {}

    def update(n, out):
        grads[n], deltas[n], new_m[n], new_v[n] = out

    gparts, _ = _allgather(_pack_small(gs, loss_local), "allgather_small_grads", [rs['C'].token])
    small = _adamw_small(gparts, [w[n] for n in SMALL], [m[n] for n in SMALL], [v[n] for n in SMALL])
    loss = small[0][0, 0]
    for t, n in enumerate(SMALL):
        update(n, small[1 + 4 * t:5 + 4 * t])
    rs['C'].halfway([small[0]])

    own, got, order = rs['A'].finish([grad_x, rs['C'].started[4]])
    update('w_ff1', _sum_adam(own, got, order, 0, w['w_ff1'], m['w_ff1'], v['w_ff1'], "adamw_w_ff1", transposed=True))
    update('w_ff2', _sum_adam(own, got, order, 512, w['w_ff2'], m['w_ff2'], v['w_ff2'], "adamw_w_ff2"))
    own, got, order = rs['B'].finish([grad_x, rs['C'].started[4]])
    names_b = ('w_out', 'w_mq', 'w_mk', 'w_mv', 'w_mo')
    done = _sum_adam_rows(own, got, order, [w[n] for n in names_b], [m[n] for n in names_b], [v[n] for n in names_b],
                          "adamw_group_b")
    for k, n in enumerate(names_b):
        update(n, done[4 * k:4 * k + 4])

    own, got, order = rs['C'].finish([new_v[n] for n in BIG if n != 'w_in'])
    done = _sum_adam(own, got, order, 0, w['w_in'].T, m['w_in'].T, v['w_in'].T, "adamw_w_in")
    update('w_in', [a.T for a in done])

    out = [loss, grad_x[None]]
    for group in (grads, deltas, new_m, new_v):
        out += [group[n].reshape(args[n].shape) for n in WEIGHTS]
    return tuple(out)
```

```python
import jax
import jax.numpy as jnp
from jax import lax
from jax.experimental import pallas as pl
from jax.experimental.pallas import tpu as pltpu

F32 = jnp.float32
BF16 = jnp.bfloat16
MESH = pl.DeviceIdType.MESH

T = 2048
D = 1024
NMEM = 256
DFF = 4096
EPS = 1e-6
TM = 256
TM_WIDE = 512
TQ = 256
FQ = 512
HD = 64
SCALE = HD ** -0.5
MEM_HEADS = 4
MEM_HD = 256
MEM_SCALE = MEM_HD ** -0.5
NEG = -1e30
LEFT = 512
WIN = LEFT + TQ
VW = 1024
NREL_PAD = 384
PROJ = 3200
GATE0 = 1536
CHK0 = 1664
VMEM_BIG = 56 * 1024 * 1024

ADAM_LR = 0.001
ADAM_B1 = 0.9
ADAM_B2 = 0.999
ADAM_EPS = 1e-08
ADAM_WD = 0.01
ADAM_STEP = 10

N_IN = 385
R_IN = 400
R_REST = 1664
W_ROWS = {'w_ff1': (0, 512), 'w_ff2': (512, 512),
          'w_out': (1024, 128), 'w_mq': (1152, 128), 'w_mk': (1280, 128), 'w_mv': (1408, 128), 'w_mo': (1536, 128)}
SMALL_ROWS = 24
SMALL_SLOT = {'rel_bias': (0, 8, 0, 257), 'b_fgt': (8, 1, 0, 8), 'g_fox_out': (9, 1, 0, 512), 'g_chk_out': (9, 1, 512, 512),
              'g_mix_pre': (10, 1, 0, 1024), 'g_mix_post': (11, 1, 0, 1024), 'g_mem_kv': (12, 1, 0, 1024),
              'g_mem_pre': (13, 1, 0, 1024), 'g_mem_post': (14, 1, 0, 1024), 'g_ff_pre': (15, 1, 0, 1024),
              'g_ff_post': (16, 1, 0, 1024)}

WEIGHTS = ['w_in', 'b_fgt', 'rel_bias', 'g_fox_out', 'g_chk_out', 'w_out', 'g_mix_pre', 'g_mix_post', 'g_mem_kv',
           'w_mq', 'w_mk', 'w_mv', 'w_mo', 'g_mem_pre', 'g_mem_post', 'w_ff1', 'w_ff2', 'g_ff_pre', 'g_ff_post']
BIG = ['w_in', 'w_out', 'w_mq', 'w_mk', 'w_mv', 'w_mo', 'w_ff1', 'w_ff2']
SMALL = [n for n in WEIGHTS if n not in BIG]


def _pcall(body, **kw):
    return pl.pallas_call(body, **kw)


def _nn(a, b):
    return jnp.dot(a, b, preferred_element_type=F32)


def _nt(a, b):
    return lax.dot_general(a, b, (((1,), (1,)), ((), ())), preferred_element_type=F32)


def _tn(a, b):
    return lax.dot_general(a, b, (((0,), (0,)), ((), ())), preferred_element_type=F32)


def _w(ref):
    v = ref[...]
    return v if v.ndim == 2 else v.reshape(-1, v.shape[-1])


def _rstd(x):
    return lax.rsqrt(jnp.mean(x * x, axis=-1, keepdims=True) + EPS)


def _rms(x, g):
    return x * _rstd(x) * g


def _rms_bwd(x, g, dy):
    r = _rstd(x)
    xh = x * r
    dg = jnp.sum(dy * xh, axis=0, keepdims=True)
    dxh = dy * g
    dx = r * (dxh - xh * jnp.mean(dxh * xh, axis=-1, keepdims=True))
    return dx, dg


def _resident(a):
    if isinstance(a, tuple):
        _, shape, index = a
        return pl.BlockSpec(shape, lambda *_: index, pipeline_mode=pl.Buffered(1))
    return pl.BlockSpec(a.shape, lambda *_, nd=a.ndim: (0,) * nd, pipeline_mode=pl.Buffered(1))


def _wblk(gw, name):
    r0, rows = W_ROWS[name]
    return (gw, (8, rows, D), (0, r0 // rows, 0))


def _behind(body, n_in, after):
    if not after:
        return body
    return lambda *refs: body(*refs[:n_in], *refs[n_in + len(after):])


def _tok_call(body, name, tiled, full, outs_tiled, outs_acc=(), rows=T, tm=TM, vmem=None, after=()):
    in_specs = [pl.BlockSpec((tm, a.shape[1]), lambda i: (i, 0)) for a in tiled]
    in_specs += [_resident(a) for a in full] + [ANY_SPEC] * len(after)
    full = [a[0] if isinstance(a, tuple) else a for a in full] + list(after)
    body = _behind(body, len(tiled) + len(full) - len(after), after)
    out_shape = [jax.ShapeDtypeStruct((rows, c), dt) for c, dt in outs_tiled]
    out_shape += [jax.ShapeDtypeStruct(s, F32) for s in outs_acc]
    out_specs = [pl.BlockSpec((tm, c), lambda i: (i, 0)) for c, _ in outs_tiled]
    out_specs += [pl.BlockSpec(s, lambda i, nd=len(s): (0,) * nd) for s in outs_acc]
    return _pcall(
        body, name=name, grid=(rows // tm,), in_specs=in_specs, out_specs=out_specs, out_shape=out_shape,
        compiler_params=pltpu.CompilerParams(dimension_semantics=("arbitrary",), vmem_limit_bytes=vmem),
    )(*tiled, *full)


def _one_call(body, name, ins, outs, vmem=None):
    whole = lambda s: pl.BlockSpec(s, lambda i, nd=len(s): (0,) * nd)
    return _pcall(
        body, name=name, grid=(1,), in_specs=[_resident(a) for a in ins], out_specs=[whole(s) for s, _ in outs],
        out_shape=[jax.ShapeDtypeStruct(s, dt) for s, dt in outs],
        compiler_params=pltpu.CompilerParams(dimension_semantics=("arbitrary",), vmem_limit_bytes=vmem),
    )(*[a[0] if isinstance(a, tuple) else a for a in ins])


def _premix_fwd(x, g_pre, win_t, after=()):
    def body(x_ref, g_ref, w_ref, h_ref, proj_ref, flog_ref):
        h = _rms(x_ref[...], g_ref[...]).astype(BF16)
        h_ref[...] = h
        p = _nt(h, w_ref[...])
        proj_ref[...] = p.astype(BF16)
        flog_ref[...] = p[:, GATE0:GATE0 + 128]

    return _tok_call(body, "premix_fwd", [x], [g_pre, win_t],
                     [(D, BF16), (PROJ, BF16), (128, F32)], tm=TM_WIDE, vmem=VMEM_BIG, after=after)


def _postmix_fwd(x, o_f, o_c, g_fo, g_co, w_out, g_post, g_mpre, w_mq):
    def body(x_ref, of_ref, oc_ref, gfo_ref, gco_ref, wo_ref, gp_ref, gm_ref, wq_ref,
             y_ref, z_ref, x1_ref, h2_ref, qm_ref):
        y_ref[:, :512] = _rms(of_ref[...], gfo_ref[...]).astype(BF16)
        y_ref[:, 512:] = _rms(oc_ref[...], gco_ref[...]).astype(BF16)
        z = _nn(y_ref[...], _w(wo_ref))
        z_ref[...] = z
        x1 = x_ref[...] + _rms(z, gp_ref[...])
        x1_ref[...] = x1
        h2 = _rms(x1, gm_ref[...]).astype(BF16)
        h2_ref[...] = h2
        qm_ref[...] = _nn(h2, _w(wq_ref)).astype(BF16)

    return _tok_call(body, "postmix_fwd", [x, o_f, o_c], [g_fo, g_co, w_out, g_post, g_mpre, w_mq],
                     [(D, BF16), (D, F32), (D, F32), (D, BF16), (D, BF16)], tm=TM_WIDE, vmem=VMEM_BIG)


def _memkv_fwd(mem, g_kv, w_mk, w_mv):
    def body(m_ref, g_ref, wk_ref, wv_ref, mn_ref, k_ref, v_ref):
        mn = _rms(m_ref[...], g_ref[...]).astype(BF16)
        mn_ref[...] = mn
        k_ref[...] = _nn(mn, _w(wk_ref)).astype(BF16)
        v_ref[...] = _nn(mn, _w(wv_ref)).astype(BF16)

    return _tok_call(body, "memkv_fwd", [mem], [g_kv, w_mk, w_mv],
                     [(D, BF16), (D, BF16), (D, BF16)], rows=NMEM, tm=NMEM, vmem=VMEM_BIG)


def _mem_fwd(qm, x1, km, vm, w_mo, g_post, g_fpre):
    def body(q_ref, x1_ref, k_ref, v_ref, wo_ref, gp_ref, gf_ref, om_ref, ym_ref, x2_ref, h3_ref):
        for h in range(MEM_HEADS):
            sl = slice(h * MEM_HD, (h + 1) * MEM_HD)
            s = _nt(q_ref[:, sl], k_ref[:, sl]) * MEM_SCALE
            p = jnp.exp(s - jnp.max(s, axis=-1, keepdims=True))
            p = p / jnp.sum(p, axis=-1, keepdims=True)
            om_ref[:, sl] = _nn(p.astype(BF16), v_ref[:, sl]).astype(BF16)
        ym = _nn(om_ref[...], _w(wo_ref))
        ym_ref[...] = ym
        x2 = x1_ref[...] + _rms(ym, gp_ref[...])
        x2_ref[...] = x2
        h3_ref[...] = _rms(x2, gf_ref[...]).astype(BF16)

    return _tok_call(body, "mem_fwd", [qm, x1], [km, vm, w_mo, g_post, g_fpre],
                     [(D, BF16), (D, F32), (D, F32), (D, BF16)], tm=TM_WIDE, vmem=VMEM_BIG)


def _tri(lower):
    r = lax.broadcasted_iota(jnp.int32, (128, 128), 0)
    c = lax.broadcasted_iota(jnp.int32, (128, 128), 1)
    return jnp.where(r >= c if lower else c >= r, 1.0, 0.0).astype(F32)


def _hdot(a, b):
    return jnp.dot(a, b, preferred_element_type=F32, precision=lax.Precision.HIGHEST)


def _gate_fwd(flog, b_pad):
    def body(f_ref, b_ref, c_ref):
        tri = _tri(True)

        def step(i, carry):
            rows = pl.ds(pl.multiple_of(i * 128, 128), 128)
            z = f_ref[rows, :] + b_ref[...]
            lf = jnp.minimum(z, 0.0) - jnp.log(1.0 + jnp.exp(-jnp.abs(z)))
            cb = _hdot(tri, lf) + carry
            c_ref[rows, :] = cb
            return cb[127:128, :]

        lax.fori_loop(0, T // 128, step, jnp.zeros((1, 128), F32))

    return _one_call(body, "gate_fwd", [flog, b_pad], [((T, 128), F32)])[0]


def _gate_bwd(dc, flog, b_pad):
    def body(dc_ref, f_ref, b_ref, df_ref, db_ref):
        tri = _tri(False)

        def step(j, carry):
            run, db = carry
            i = T // 128 - 1 - j
            rows = pl.ds(pl.multiple_of(i * 128, 128), 128)
            dcb = dc_ref[rows, :]
            rb = _hdot(tri, dcb) + run
            z = f_ref[rows, :] + b_ref[...]
            df = rb * (1.0 / (1.0 + jnp.exp(z)))
            df_ref[rows, :] = df.astype(BF16)
            return run + jnp.sum(dcb, axis=0, keepdims=True), db + jnp.sum(df, axis=0, keepdims=True)

        _, db = lax.fori_loop(0, T // 128, step, (jnp.zeros((1, 128), F32), jnp.zeros((1, 128), F32)))
        db_ref[...] = jnp.broadcast_to(db, (8, 128))

    return _one_call(body, "gate_bwd", [dc, flog, b_pad], [((T, 128), BF16), ((8, 128), F32)])


def _lane_lo(rows=TQ):
    return lax.broadcasted_iota(jnp.int32, (rows, 128), 1) < HD


def _half(v, lo, a, scale=None):
    keep = lo if a == 0 else jnp.logical_not(lo)
    v = v.astype(F32) if scale is None else v.astype(F32) * scale
    return jnp.where(keep, v, 0.0).astype(BF16)


def _fox_specs():
    return [pl.BlockSpec((FQ, 128), lambda h, i: (i, h)),
            pl.BlockSpec((T, 128), lambda h, i: (0, 4 + h)),
            pl.BlockSpec((T, 128), lambda h, i: (0, 8 + h))]


def _lane_pick(x, at):
    lane = lax.broadcasted_iota(jnp.int32, x.shape, 1)
    return jnp.sum(jnp.where(lane == at, x, 0.0), axis=-1, keepdims=True)


def _fox_fwd(proj, c, ct3):
    def body(q_ref, k_ref, v_ref, c_ref, ct_ref, o_ref, l_ref):
        i = pl.program_id(1)
        lo = _lane_lo(FQ)
        causal = lax.broadcasted_iota(jnp.int32, (FQ, FQ), 1) <= lax.broadcasted_iota(jnp.int32, (FQ, FQ), 0)
        q = q_ref[...]
        qs = [_half(q, lo, a, SCALE) for a in range(2)]
        cqs = [_lane_pick(c_ref[...], 2 * pl.program_id(0) + a) for a in range(2)]

        def tile(off, carry, diagonal):
            kblk = k_ref[pl.ds(off, FQ), :]
            vblk = v_ref[pl.ds(off, FQ), :]
            new = []
            for a in range(2):
                m, l, acc = carry[a]
                s = _nt(qs[a], kblk) + (cqs[a] - ct_ref[a:a + 1, pl.ds(off, FQ)])
                if diagonal:
                    s = jnp.where(causal, s, NEG)
                m2 = jnp.maximum(m, jnp.max(s, axis=-1, keepdims=True))
                p = jnp.exp(s - m2)
                alpha = jnp.exp(m - m2)
                new.append((m2, alpha * l + jnp.sum(p, axis=-1, keepdims=True),
                            alpha * acc + _nn(p.astype(BF16), vblk)))
            return tuple(new)

        init = (jnp.full((FQ, 1), NEG, F32), jnp.zeros((FQ, 1), F32), jnp.zeros((FQ, 128), F32))
        carry = lax.fori_loop(0, i, lambda kb, c: tile(pl.multiple_of(kb * FQ, FQ), c, False), (init, init))
        carry = tile(pl.multiple_of(i * FQ, FQ), carry, True)
        outs = []
        for a in range(2):
            m, l, acc = carry[a]
            outs.append(acc / l)
            l_ref[:, 128 * a:128 * a + 128] = jnp.broadcast_to(m + jnp.log(l), (FQ, 128))
        o_ref[...] = jnp.where(lo, outs[0], outs[1])

    return _pcall(
        body, name="fox_fwd", grid=(4, T // FQ),
        in_specs=_fox_specs() + [pl.BlockSpec((FQ, 128), lambda h, i: (i, 0)),
                                 pl.BlockSpec((None, 2, T), lambda h, i: (h, 0, 0))],
        out_specs=[pl.BlockSpec((FQ, 128), lambda h, i: (i, h)), pl.BlockSpec((FQ, 256), lambda h, i: (i, h))],
        out_shape=[jax.ShapeDtypeStruct((T, 512), F32), jax.ShapeDtypeStruct((T, 1024), F32)],
        compiler_params=pltpu.CompilerParams(dimension_semantics=("arbitrary", "arbitrary"), vmem_limit_bytes=VMEM_BIG),
    )(proj, proj, proj, c, ct3)


def _fox_bwd(proj, c, ct3, o, lse, do, after=()):
    def body(q_ref, k_ref, v_ref, c_ref, ct_ref, o_ref, l_ref, do_ref, dq_ref, dkb_ref, dvb_ref, dct_ref, dcq_ref,
             dk_ref, dv_ref):
        i = pl.program_id(1)

        @pl.when(i == 0)
        def _():
            dk_ref[...] = jnp.zeros_like(dk_ref)
            dv_ref[...] = jnp.zeros_like(dv_ref)
            dct_ref[...] = jnp.zeros_like(dct_ref)

        lo = _lane_lo(FQ)
        causal = lax.broadcasted_iota(jnp.int32, (FQ, FQ), 1) <= lax.broadcasted_iota(jnp.int32, (FQ, FQ), 0)
        q = q_ref[...]
        do_v = do_ref[...]
        prod = do_v * o_ref[...]
        qs = [_half(q, lo, a, SCALE) for a in range(2)]
        dos = [_half(do_v, lo, a) for a in range(2)]
        deltas = [jnp.sum(jnp.where(lo if a == 0 else jnp.logical_not(lo), prod, 0.0), axis=-1, keepdims=True)
                  for a in range(2)]
        cqs = [_lane_pick(c_ref[...], 2 * pl.program_id(0) + a) for a in range(2)]
        las = [l_ref[:, 128 * a:128 * a + 1] for a in range(2)]

        def tile(off, carry, diagonal):
            kblk = k_ref[pl.ds(off, FQ), :]
            vblk = v_ref[pl.ds(off, FQ), :]
            new = []
            dk = jnp.zeros((128, FQ), F32)
            dv = jnp.zeros((128, FQ), F32)
            for a in range(2):
                dq_acc, rs = carry[a]
                s = _nt(qs[a], kblk) + (cqs[a] - ct_ref[a:a + 1, pl.ds(off, FQ)])
                if diagonal:
                    s = jnp.where(causal, s, NEG)
                p = jnp.exp(s - las[a])
                ds = p * (_nt(dos[a], vblk) - deltas[a])
                dsb = ds.astype(BF16)
                dk = dk + _tn(qs[a], dsb)
                dv = dv + _tn(dos[a], p.astype(BF16))
                dct_ref[a:a + 1, pl.ds(off, FQ)] -= jnp.sum(ds, axis=0, keepdims=True)
                new.append((dq_acc + _nn(dsb, kblk), rs + jnp.sum(ds, axis=-1, keepdims=True)))
            dk_ref[:, pl.ds(off, FQ)] += dk
            dv_ref[:, pl.ds(off, FQ)] += dv
            return tuple(new)

        init = (jnp.zeros((FQ, 128), F32), jnp.zeros((FQ, 1), F32))
        carry = lax.fori_loop(0, i, lambda kb, c: tile(pl.multiple_of(kb * FQ, FQ), c, False), (init, init))
        carry = tile(pl.multiple_of(i * FQ, FQ), carry, True)
        lane = lax.broadcasted_iota(jnp.int32, (FQ, 128), 1)
        dcq_ref[...] = jnp.where(lane == 0, carry[0][1], jnp.where(lane == 1, carry[1][1], 0.0))
        dq_ref[...] = (jnp.where(lo, carry[0][0], carry[1][0]) * SCALE).astype(BF16)

        @pl.when(i == T // FQ - 1)
        def _():
            dkb_ref[...] = dk_ref[...].T.astype(BF16)
            dvb_ref[...] = dv_ref[...].T.astype(BF16)

    blk = pl.BlockSpec((FQ, 128), lambda h, i: (i, h))
    wide = pl.BlockSpec((FQ, 256), lambda h, i: (i, h))
    rows = pl.BlockSpec((None, 2, T), lambda h, i: (h, 0, 0))
    col = pl.BlockSpec((T, 128), lambda h, i: (0, h))
    return _pcall(
        _behind(body, 8, after), name="fox_bwd", grid=(4, T // FQ),
        in_specs=_fox_specs() + [pl.BlockSpec((FQ, 128), lambda h, i: (i, 0)), rows, blk, wide, blk] + [ANY_SPEC] * len(after),
        out_specs=[blk, col, col, rows, pl.BlockSpec((None, FQ, 128), lambda h, i: (h, i, 0))],
        out_shape=[jax.ShapeDtypeStruct((T, 512), BF16), jax.ShapeDtypeStruct((T, 512), BF16),
                   jax.ShapeDtypeStruct((T, 512), BF16), jax.ShapeDtypeStruct((4, 2, T), F32),
                   jax.ShapeDtypeStruct((4, T, 128), F32)],
        scratch_shapes=[pltpu.VMEM((128, T), F32), pltpu.VMEM((128, T), F32)],
        compiler_params=pltpu.CompilerParams(dimension_semantics=("arbitrary", "arbitrary"), vmem_limit_bytes=VMEM_BIG),
    )(proj, proj, proj, c, ct3, o, lse, do, *after)


def _rel_onehot():
    ridx = lax.broadcasted_iota(jnp.int32, (NREL_PAD, VW), 0)
    j = lax.broadcasted_iota(jnp.int32, (NREL_PAD, VW), 1)
    return jnp.where(ridx == jnp.clip(TQ + LEFT - 1 - j, -128, 128) + 128, 1.0, 0.0).astype(F32)


def _relvec_fwd(tbl):
    def body(t_ref, v_ref):
        v_ref[...] = _hdot(t_ref[...], _rel_onehot())

    return _one_call(body, "relvec_fwd", [tbl], [((8, VW), F32)])[0]


def _relvec_bwd(gv):
    def body(g_ref, t_ref):
        t_ref[...] = lax.dot_general(g_ref[...], _rel_onehot(), (((1,), (1,)), ((), ())),
                                     preferred_element_type=F32, precision=lax.Precision.HIGHEST)

    return _one_call(body, "relvec_bwd", [gv], [((8, NREL_PAD), F32)])[0]


def _chk_bias(vt_ref, a, hidden):
    vb = jnp.broadcast_to(vt_ref[a:a + 1, :], (TQ, VW))
    y = pltpu.roll(vb, VW - (TQ - 1), 1, stride=1, stride_axis=0)[:, :WIN]
    cr = lax.broadcasted_iota(jnp.int32, (TQ, WIN), 0) // 64
    m = lax.broadcasted_iota(jnp.int32, (TQ, WIN), 1)
    return jnp.where((m // 64 >= cr) & (m // 64 <= cr + 8) & (m >= hidden), y, NEG)


def _chk_specs():
    return [pl.BlockSpec((TQ, 128), lambda h, i: (i, CHK0 // 128 + h)),
            pl.BlockSpec((T + LEFT, 128), lambda h, i: (0, h)),
            pl.BlockSpec((T + LEFT, 128), lambda h, i: (0, 4 + h)),
            pl.BlockSpec((None, 2, VW), lambda h, i: (h, 0, 0))]


def _chk_fwd(proj, kvp, vt3, after=()):
    def body(q_ref, k_ref, v_ref, vt_ref, o_ref, l_ref, bias_ref):
        i = pl.program_id(1)

        @pl.when(i == 0)
        def _():
            for first in range(3):
                for a in range(2):
                    bias_ref[first, a] = _chk_bias(vt_ref, a, max(LEFT - first * TQ, 0))

        lo = _lane_lo()
        off = pl.multiple_of(i * TQ, TQ)
        kw = k_ref[pl.ds(off, WIN), :]
        vw = v_ref[pl.ds(off, WIN), :]
        bias_at = jnp.minimum(i, 2)
        q = q_ref[...]
        outs = []
        for a in range(2):
            s = _nt(_half(q, lo, a, SCALE), kw) + bias_ref[bias_at, a]
            m = jnp.max(s, axis=-1, keepdims=True)
            p = jnp.exp(s - m)
            l = jnp.sum(p, axis=-1, keepdims=True)
            outs.append(_nn(p.astype(BF16), vw) / l)
            l_ref[:, 128 * a:128 * a + 128] = jnp.broadcast_to(m + jnp.log(l), (TQ, 128))
        o_ref[...] = jnp.where(lo, outs[0], outs[1])

    return _pcall(
        _behind(body, 4, after), name="chk_fwd", grid=(4, T // TQ), in_specs=_chk_specs() + [ANY_SPEC] * len(after),
        out_specs=[pl.BlockSpec((TQ, 128), lambda h, i: (i, h)), pl.BlockSpec((TQ, 256), lambda h, i: (i, h))],
        out_shape=[jax.ShapeDtypeStruct((T, 512), F32), jax.ShapeDtypeStruct((T, 1024), F32)],
        scratch_shapes=[pltpu.VMEM((3, 2, TQ, WIN), F32)],
        compiler_params=pltpu.CompilerParams(dimension_semantics=("arbitrary", "arbitrary")),
    )(proj, kvp, kvp, vt3, *after)


def _chk_bwd(proj, kvp, vt3, o, lse, do, after=()):
    nq = T // TQ

    def body(q_ref, k_ref, v_ref, vt_ref, o_ref, l_ref, do_ref, dq_ref, dkb_ref, dvb_ref, gv_ref, bias_ref, dsum_ref,
             dk_ref, dv_ref):
        i = pl.program_id(1)

        @pl.when(i == 0)
        def _():
            for first in range(3):
                for a in range(2):
                    bias_ref[first, a] = _chk_bias(vt_ref, a, max(LEFT - first * TQ, 0))
            dsum_ref[...] = jnp.zeros_like(dsum_ref)
            dk_ref[...] = jnp.zeros_like(dk_ref)
            dv_ref[...] = jnp.zeros_like(dv_ref)

        lo = _lane_lo()
        off = pl.multiple_of(i * TQ, TQ)
        kw = k_ref[pl.ds(off, WIN), :]
        vw = v_ref[pl.ds(off, WIN), :]
        bias_at = jnp.minimum(i, 2)
        q = q_ref[...]
        do_v = do_ref[...]
        prod = do_v * o_ref[...]
        dqs = []
        for a in range(2):
            keep = lo if a == 0 else jnp.logical_not(lo)
            qa = _half(q, lo, a, SCALE)
            doa = _half(do_v, lo, a)
            delta = jnp.sum(jnp.where(keep, prod, 0.0), axis=-1, keepdims=True)
            s = _nt(qa, kw) + bias_ref[bias_at, a]
            p = jnp.exp(s - l_ref[:, 128 * a:128 * a + 1])
            ds = p * (_nt(doa, vw) - delta)
            dsum_ref[a] += ds
            dsb = ds.astype(BF16)
            dk_ref[:, pl.ds(off, WIN)] += _tn(qa, dsb)
            dv_ref[:, pl.ds(off, WIN)] += _tn(doa, p.astype(BF16))
            dqs.append(_nn(dsb, kw))
        dq_ref[...] = (jnp.where(lo, dqs[0], dqs[1]) * SCALE).astype(BF16)

        @pl.when(i == nq - 1)
        def _():
            dkb_ref[...] = dk_ref[:, LEFT:].T.astype(BF16)
            dvb_ref[...] = dv_ref[:, LEFT:].T.astype(BF16)
            rr = lax.broadcasted_iota(jnp.int32, (TQ, TQ), 0)
            cc = lax.broadcasted_iota(jnp.int32, (TQ, TQ), 1)
            flip = jnp.where(rr + cc == TQ - 1, 1.0, 0.0).astype(F32)
            for a in range(2):
                dpad = jnp.concatenate([dsum_ref[a], jnp.zeros((TQ, VW - WIN), F32)], axis=1)
                z = pltpu.roll(_hdot(flip, dpad), 0, 1, stride=1, stride_axis=0)
                gv_ref[a:a + 1, :] = jnp.sum(z, axis=0, keepdims=True)

    blk = pl.BlockSpec((TQ, 128), lambda h, i: (i, h))
    wide = pl.BlockSpec((TQ, 256), lambda h, i: (i, h))
    col = pl.BlockSpec((T, 128), lambda h, i: (0, h))
    return _pcall(
        _behind(body, 7, after), name="chk_bwd", grid=(4, nq), in_specs=_chk_specs() + [blk, wide, blk] + [ANY_SPEC] * len(after),
        out_specs=[blk, col, col, pl.BlockSpec((None, 2, VW), lambda h, i: (h, 0, 0))],
        out_shape=[jax.ShapeDtypeStruct((T, 512), BF16), jax.ShapeDtypeStruct((T, 512), BF16),
                   jax.ShapeDtypeStruct((T, 512), BF16), jax.ShapeDtypeStruct((4, 2, VW), F32)],
        scratch_shapes=[pltpu.VMEM((3, 2, TQ, WIN), F32), pltpu.VMEM((2, TQ, WIN), F32),
                        pltpu.VMEM((128, T + LEFT), F32), pltpu.VMEM((128, T + LEFT), F32)],
        compiler_params=pltpu.CompilerParams(dimension_semantics=("arbitrary", "arbitrary")),
    )(proj, kvp, kvp, vt3, o, lse, do, *after)


def _zero_at_start(*refs):
    @pl.when(pl.program_id(0) == 0)
    def _():
        for r in refs:
            r[...] = jnp.zeros_like(r)


def _ffn_step(h3, x2, tgt, w1_t, w2, g_post, g_pre):
    def body(h_ref, x2_ref, t_ref, w1_ref, w2_ref, gp_ref, gf_ref, dx2_ref, da_ref, dy_ref, r_ref, loss_ref, dgp_ref, dgf_ref):
        _zero_at_start(loss_ref, dgp_ref, dgf_ref)
        w1, w2v = _w(w1_ref), _w(w2_ref)
        ra = jnp.maximum(_nt(h_ref[...], w1), 0.0)
        r = jnp.square(ra).astype(BF16)
        r_ref[...] = r
        y = _nn(r, w2v)
        x2v = x2_ref[...]
        e = x2v + _rms(y, gp_ref[...]) - t_ref[...]
        loss_ref[...] += 0.5 * jnp.sum(jnp.sum(e * e, axis=-1, keepdims=True) * (1.0 / D))
        dx3 = e * (1.0 / D)
        dy, dgp = _rms_bwd(y, gp_ref[...], dx3)
        dgp_ref[...] += dgp
        dyb = dy.astype(BF16)
        dy_ref[...] = dyb
        da = (_nt(dyb, w2v) * (2.0 * ra)).astype(BF16)
        da_ref[...] = da
        dh, dgf = _rms_bwd(x2v, gf_ref[...], _nn(da, w1))
        dgf_ref[...] += dgf
        dx2_ref[...] = dx3 + dh

    return _tok_call(body, "ffn_step", [h3, x2, tgt], [w1_t, w2, g_post, g_pre],
                     [(D, F32), (DFF, BF16), (D, BF16), (DFF, BF16)], [(8, 128), (1, D), (1, D)], vmem=VMEM_BIG)


def _mem_bwd(dx2, ym, x1, qm, km, vm, w_mo, w_mq, g_post, g_pre, after=()):
    def body(dx2_ref, ym_ref, x1_ref, q_ref, k_ref, v_ref, wo_ref, wq_ref, gp_ref, gm_ref,
             dx1_ref, dym_ref, dq_ref, dk_ref, dv_ref, dgp_ref, dgm_ref, dom_ref):
        _zero_at_start(dk_ref, dv_ref, dgp_ref, dgm_ref)
        dx2_v = dx2_ref[...]
        dym, dgp = _rms_bwd(ym_ref[...], gp_ref[...], dx2_v)
        dgp_ref[...] += dgp
        dymb = dym.astype(BF16)
        dym_ref[...] = dymb
        dom_ref[...] = _nt(dymb, _w(wo_ref)).astype(BF16)
        for h in range(MEM_HEADS):
            sl = slice(h * MEM_HD, (h + 1) * MEM_HD)
            qh, kh, doh = q_ref[:, sl], k_ref[:, sl], dom_ref[:, sl]
            s = _nt(qh, kh) * MEM_SCALE
            p = jnp.exp(s - jnp.max(s, axis=-1, keepdims=True))
            p = p / jnp.sum(p, axis=-1, keepdims=True)
            dp = _nt(doh, v_ref[:, sl])
            ds = (p * (dp - jnp.sum(p * dp, axis=-1, keepdims=True))).astype(BF16)
            dq_ref[:, sl] = (_nn(ds, kh) * MEM_SCALE).astype(BF16)
            dk_ref[:, sl] += _tn(ds, qh) * MEM_SCALE
            dv_ref[:, sl] += _tn(p.astype(BF16), doh)
        dh, dgm = _rms_bwd(x1_ref[...], gm_ref[...], _nt(dq_ref[...], _w(wq_ref)))
        dgm_ref[...] += dgm
        dx1_ref[...] = dx2_v + dh

    tiled = pl.BlockSpec((TM_WIDE, D), lambda i: (i, 0))
    in_specs = [tiled] * 4 + [_resident(a) for a in (km, vm, w_mo, w_mq, g_post, g_pre)] + [ANY_SPEC] * len(after)
    w_mo, w_mq = w_mo[0], w_mq[0]
    kv = pl.BlockSpec((NMEM, D), lambda i: (0, 0))
    vec = pl.BlockSpec((1, D), lambda i: (0, 0))
    return _pcall(
        _behind(body, 10, after), name="mem_bwd", grid=(T // TM_WIDE,), in_specs=in_specs,
        out_specs=[tiled, tiled, tiled, kv, kv, vec, vec],
        out_shape=[jax.ShapeDtypeStruct((T, D), F32), jax.ShapeDtypeStruct((T, D), BF16),
                   jax.ShapeDtypeStruct((T, D), BF16), jax.ShapeDtypeStruct((NMEM, D), F32),
                   jax.ShapeDtypeStruct((NMEM, D), F32), jax.ShapeDtypeStruct((1, D), F32),
                   jax.ShapeDtypeStruct((1, D), F32)],
        scratch_shapes=[pltpu.VMEM((TM_WIDE, D), BF16)],
        compiler_params=pltpu.CompilerParams(dimension_semantics=("arbitrary",), vmem_limit_bytes=VMEM_BIG),
    )(dx2, ym, x1, qm, km, vm, w_mo, w_mq, g_post, g_pre, *after)


def _memkv_bwd(dkm, dvm, mem, w_mk, w_mv):
    def body(dk_ref, dv_ref, m_ref, wk_ref, wv_ref, dg_ref):
        dmn = _nt(dk_ref[...].astype(BF16), _w(wk_ref)) + _nt(dv_ref[...].astype(BF16), _w(wv_ref))
        mv = m_ref[...]
        dg_ref[...] = jnp.sum(dmn * (mv * _rstd(mv)), axis=0, keepdims=True)

    return _one_call(body, "memkv_bwd", [dkm, dvm, mem, w_mk, w_mv], [((1, D), F32)], vmem=VMEM_BIG)[0]


def _postmix_bwd(dx1, z, o_f, o_c, w_out, g_post, g_fo, g_co, after=()):
    def body(dx1_ref, z_ref, of_ref, oc_ref, wo_ref, gp_ref, gfo_ref, gco_ref,
             dz_ref, dof_ref, doc_ref, dgp_ref, dgfo_ref, dgco_ref):
        _zero_at_start(dgp_ref, dgfo_ref, dgco_ref)
        dz, dgp = _rms_bwd(z_ref[...], gp_ref[...], dx1_ref[...])
        dgp_ref[...] += dgp
        dzb = dz.astype(BF16)
        dz_ref[...] = dzb
        dy = _nt(dzb, _w(wo_ref))
        dof, dgfo = _rms_bwd(of_ref[...], gfo_ref[...], dy[:, :512])
        doc, dgco = _rms_bwd(oc_ref[...], gco_ref[...], dy[:, 512:])
        dof_ref[...] = dof
        doc_ref[...] = doc
        dgfo_ref[...] += dgfo
        dgco_ref[...] += dgco

    return _tok_call(body, "postmix_bwd", [dx1, z, o_f, o_c], [w_out, g_post, g_fo, g_co],
                     [(D, BF16), (512, F32), (512, F32)], [(1, D), (1, 512), (1, 512)], tm=TM_WIDE, vmem=VMEM_BIG,
                     after=after)


def _premix_bwd(dx1, x, pieces, win_t, g_pre):
    def body(dx1_ref, x_ref, *refs):
        piece_refs, (w_ref, g_ref, dx_ref, dp_ref, dg_ref) = refs[:len(pieces)], refs[len(pieces):]
        _zero_at_start(dg_ref)
        col = 0
        for p in piece_refs:
            dp_ref[:, col:col + p.shape[1]] = p[...]
            col += p.shape[1]
        dh, dg = _rms_bwd(x_ref[...], g_ref[...], _nn(dp_ref[...], w_ref[...]))
        dg_ref[...] += dg
        dx_ref[...] = dx1_ref[...] + dh

    return _tok_call(body, "premix_bwd", [dx1, x] + list(pieces), [win_t, g_pre], [(D, F32), (PROJ, BF16)], [(1, D)],
                     tm=TM_WIDE, vmem=VMEM_BIG)


def _wgrad(a, b, name):
    k, m = a.shape
    n = b.shape[1]
    tm = 640 if m % 640 == 0 and m > 1024 else min(m, 512)
    tn = min(n, 1024)

    def body(a_ref, b_ref, o_ref):
        o_ref[...] = _tn(a_ref[...].astype(BF16), b_ref[...].astype(BF16))

    return _pcall(
        body, name=name, grid=(m // tm, n // tn),
        in_specs=[pl.BlockSpec((k, tm), lambda i, j: (0, i)), pl.BlockSpec((k, tn), lambda i, j: (0, j))],
        out_specs=pl.BlockSpec((tm, tn), lambda i, j: (i, j)),
        out_shape=jax.ShapeDtypeStruct((m, n), F32),
        compiler_params=pltpu.CompilerParams(dimension_semantics=("arbitrary", "arbitrary"), vmem_limit_bytes=VMEM_BIG),
    )(a, b)


def _wgrad_group(name, pairs, rows, whole=False):
    def body(*refs):
        o_ref = refs[-1]
        cols = pl.ds(pl.multiple_of(pl.program_id(0) * rows, rows), rows)
        for k in range(len(pairs)):
            a = refs[2 * k][:, cols] if whole else refs[2 * k][...]
            g = _tn(a.astype(BF16), refs[2 * k + 1][...].astype(BF16))
            o_ref[k * rows:(k + 1) * rows, :] = g.astype(BF16)

    in_specs, ops = [], []
    for a, b in pairs:
        in_specs += [_resident(a) if whole else pl.BlockSpec((a.shape[0], rows), lambda j: (0, j)), _resident(b)]
        ops += [a, b]
    return _pcall(
        body, name=name, grid=(8,), in_specs=in_specs,
        out_specs=pl.BlockSpec((None, len(pairs) * rows, D), lambda j: (j, 0, 0)),
        out_shape=jax.ShapeDtypeStruct((8, len(pairs) * rows, D), BF16),
        compiler_params=pltpu.CompilerParams(dimension_semantics=("arbitrary",), vmem_limit_bytes=VMEM_BIG),
    )(*ops)


def _adam_math(w, g, m, v):
    m2 = ADAM_B1 * m + (1.0 - ADAM_B1) * g
    v2 = ADAM_B2 * v + (1.0 - ADAM_B2) * jnp.square(g)
    m_hat = m2 / (1.0 - ADAM_B1 ** ADAM_STEP)
    v_hat = v2 / (1.0 - ADAM_B2 ** ADAM_STEP)
    delta = -ADAM_LR * (m_hat / (jnp.sqrt(v_hat) + ADAM_EPS) + ADAM_WD * w)
    return delta, m2, v2


def _adamw_small(gparts, ws, ms, vs):
    n = len(SMALL)

    def body(g_ref, *refs):
        w_refs, m_refs, v_refs = refs[:n], refs[n:2 * n], refs[2 * n:3 * n]
        outs, sum_ref = refs[3 * n:-1], refs[-1]
        g = g_ref[0]
        for k in range(1, 8):
            g = g + g_ref[k]
        sum_ref[...] = g
        outs[0][...] = sum_ref[17:18, 0:128]
        for t, name in enumerate(SMALL):
            r0, nr, c0, nc = SMALL_SLOT[name]
            gt = sum_ref[r0:r0 + nr, c0:c0 + nc]
            out = (gt,) + _adam_math(w_refs[t][...], gt, m_refs[t][...], v_refs[t][...])
            for o_ref, val in zip(outs[1 + 4 * t:5 + 4 * t], out):
                o_ref[...] = val

    whole = lambda s: pl.BlockSpec(s, lambda i, nd=len(s): (0,) * nd)
    ins = [gparts] + list(ws) + list(ms) + list(vs)
    out_shapes = [(1, 128)] + [a.shape for a in ws for _ in range(4)]
    return _pcall(
        body, name="adamw_small", grid=(1,), in_specs=[whole(a.shape) for a in ins],
        out_specs=[whole(s) for s in out_shapes], out_shape=[jax.ShapeDtypeStruct(s, F32) for s in out_shapes],
        scratch_shapes=[pltpu.VMEM((SMALL_ROWS, D), F32)],
        compiler_params=pltpu.CompilerParams(dimension_semantics=("arbitrary",)),
    )(*ins)


def _row_tile(rows):
    return next(t for t in (512, 400, 320) if rows % t == 0)


def _add_halves(g4, theirs, core, name):
    rows = g4.shape[2]
    tr = _row_tile(rows)

    def body(c_ref, a_ref, b_ref, o_ref):
        o_ref[...] = (a_ref[...].astype(F32) + b_ref[...].astype(F32)).astype(BF16)

    grid_spec = pltpu.PrefetchScalarGridSpec(
        num_scalar_prefetch=1, grid=(4, rows // tr),
        in_specs=[pl.BlockSpec((None, None, tr, D), lambda j, i, c: (j, c[0], i, 0)),
                  pl.BlockSpec((None, None, tr, D), lambda j, i, c: (j, 0, i, 0))],
        out_specs=pl.BlockSpec((None, tr, D), lambda j, i, c: (j, i, 0)))
    return _pcall(
        body, name=name, grid_spec=grid_spec, out_shape=jax.ShapeDtypeStruct((4, rows, D), BF16),
        compiler_params=pltpu.CompilerParams(dimension_semantics=("arbitrary", "arbitrary")),
    )(core, g4, theirs)


def _sum_adam(own, got, order, r0, w, m, v, name, transposed=False):
    n = w.shape[1] if transposed else w.shape[0]
    tr = min(n, 256) if n % 8 == 0 else n
    rows = tr if n % 8 == 0 else own.shape[1]

    def body(o_ref, a_ref, b_ref, c_ref, d_ref, w_ref, m_ref, v_ref, g_ref, dl_ref, m2_ref, v2_ref):
        f = lambda r: r[0:tr, :].astype(F32)
        g = ((f(a_ref) + f(b_ref)) + f(c_ref)) + f(d_ref)
        g = g.T if transposed else g
        g_ref[...] = g
        dl_ref[...], m2_ref[...], v2_ref[...] = _adam_math(w_ref[...], g, m_ref[...], v_ref[...])

    slot = lambda k: pl.BlockSpec((None, rows, D), lambda i, o: (o[k], r0 // rows + i, 0))
    wspec = pl.BlockSpec((D, tr), lambda i, o: (0, i)) if transposed else pl.BlockSpec((tr, D), lambda i, o: (i, 0))
    grid_spec = pltpu.PrefetchScalarGridSpec(
        num_scalar_prefetch=1, grid=(n // tr,), in_specs=[slot(0), slot(1), slot(2), slot(3), wspec, wspec, wspec],
        out_specs=[wspec] * 4)
    return _pcall(
        body, name=name, grid_spec=grid_spec, out_shape=[jax.ShapeDtypeStruct(w.shape, F32)] * 4,
        compiler_params=pltpu.CompilerParams(dimension_semantics=("arbitrary",)),
    )(order, own, got, got, got, w, m, v)


def _sum_adam_rows(own, got, order, ws, ms, vs, name):
    n, rows = len(ws), ws[0].shape[0]

    def body(o_ref, a_ref, b_ref, c_ref, d_ref, *refs):
        ins, outs = refs[:3 * n], refs[3 * n:]
        for t in range(n):
            r = slice(t * rows, (t + 1) * rows)
            f = lambda ref: ref[r, :].astype(F32)
            g = ((f(a_ref) + f(b_ref)) + f(c_ref)) + f(d_ref)
            out = (g,) + _adam_math(ins[t][...], g, ins[n + t][...], ins[2 * n + t][...])
            for o, val in zip(outs[4 * t:4 * t + 4], out):
                o[...] = val

    slot = lambda k: pl.BlockSpec((None, n * rows, D), lambda i, o: (o[k], 0, 0), pipeline_mode=pl.Buffered(1))
    wspec = pl.BlockSpec((rows, D), lambda i, o: (0, 0), pipeline_mode=pl.Buffered(1))
    grid_spec = pltpu.PrefetchScalarGridSpec(
        num_scalar_prefetch=1, grid=(1,), in_specs=[slot(0), slot(1), slot(2), slot(3)] + [wspec] * (3 * n),
        out_specs=[pl.BlockSpec((rows, D), lambda i, o: (0, 0))] * (4 * n))
    return _pcall(
        body, name=name, grid_spec=grid_spec, out_shape=[jax.ShapeDtypeStruct((rows, D), F32)] * (4 * n),
        compiler_params=pltpu.CompilerParams(dimension_semantics=("arbitrary",), vmem_limit_bytes=VMEM_BIG),
    )(order, own, got, got, got, *ws, *ms, *vs)


def _place():
    return lax.axis_index("x"), lax.axis_index("y"), lax.axis_index("c")


def _allgather(block, name, after=()):
    rows = block.shape[0]
    split = (rows // 2 + 15) // 16 * 16

    def body(x_ref, out_ref, token, send_sems, recv_sems, local_sem):
        token[...] = jnp.zeros_like(token)
        x, y, c = _place()
        me, sib = (x, y, c), (x, y, 1 - c)
        xn, yn, dg = (1 - x, y), (x, 1 - y), (1 - x, 1 - y)
        lo, hi = pl.ds(0, split), pl.ds(split, rows - split)

        def copy(k, blk, to, part=None, src=None):
            index = 4 * blk[0] + 2 * blk[1] + blk[2]
            view = out_ref.at[index] if part is None else out_ref.at[index, part]
            return pltpu.make_async_remote_copy(
                src_ref=view if src is None else src, dst_ref=view,
                send_sem=send_sems.at[k], recv_sem=recv_sems.at[k], device_id=to, device_id_type=MESH)

        def start(*copies):
            for cp in copies:
                cp.start()
            return list(copies)

        mine = pltpu.make_async_copy(x_ref, out_ref.at[4 * x + 2 * y + c], local_sem)
        mine.start()
        sent = start(copy(0, me, sib, src=x_ref), copy(1, me, (*xn, c), src=x_ref), copy(2, me, (*yn, c), src=x_ref))
        copy(1, (*xn, c), me).wait_recv()
        sent += start(copy(3, (*xn, c), sib), copy(5, (*xn, c), (*yn, c), part=lo))
        copy(2, (*yn, c), me).wait_recv()
        sent += start(copy(4, (*yn, c), sib), copy(6, (*yn, c), (*xn, c), part=hi))
        copy(5, (*dg, c), me, part=lo).wait_recv()
        copy(6, (*dg, c), me, part=hi).wait_recv()
        sent += start(copy(7, (*dg, c), sib))
        for k, blk in ((0, sib), (3, (*xn, 1 - c)), (4, (*yn, 1 - c)), (7, (*dg, 1 - c))):
            copy(k, blk, me).wait_recv()
        for cp in sent:
            cp.wait_send()
        mine.wait()

    return _pcall(
        _behind(body, 1, after), name=name,
        out_shape=[jax.ShapeDtypeStruct((8,) + block.shape, block.dtype), jax.ShapeDtypeStruct((8, 128), F32)],
        in_specs=[pl.BlockSpec(memory_space=pl.ANY)] * (1 + len(after)),
        out_specs=[pl.BlockSpec(memory_space=pl.ANY), pl.BlockSpec(memory_space=pltpu.VMEM)],
        scratch_shapes=[pltpu.SemaphoreType.DMA((8,)), pltpu.SemaphoreType.DMA((8,)), pltpu.SemaphoreType.DMA(())],
        compiler_params=pltpu.CompilerParams(has_side_effects=True),
    )(block, *after)


HBM_SPEC = pl.BlockSpec(memory_space=pltpu.HBM)
SEM_SPEC = pl.BlockSpec(memory_space=pltpu.SEMAPHORE)
ANY_SPEC = pl.BlockSpec(memory_space=pl.ANY)
EFFECT = pltpu.SideEffectType.DATAFLOW_SIDE_EFFECTING


def _in_hbm(a):
    return pltpu.with_memory_space_constraint(a, pltpu.HBM)


def _start_copies(name, src, land_shape, plan, n):
    def body(src_ref, land_ref, send_sems, recv_sems, src_thru, land_thru, token):
        for k, (s, d, to, _) in enumerate(plan(src_ref, land_ref)):
            pltpu.make_async_remote_copy(src_ref=s, dst_ref=d, send_sem=send_sems.at[k], recv_sem=recv_sems.at[k],
                                         device_id=to, device_id_type=MESH).start()
        token[...] = jnp.zeros_like(token)

    return _pcall(
        body, name=name,
        out_shape=(pltpu.SemaphoreType.DMA((n,)), pltpu.SemaphoreType.DMA((n,)), pltpu.HBM(src.shape, src.dtype),
                   pltpu.HBM(land_shape, src.dtype), jax.ShapeDtypeStruct((8, 128), F32)),
        in_specs=(HBM_SPEC, HBM_SPEC),
        out_specs=(SEM_SPEC, SEM_SPEC, HBM_SPEC, HBM_SPEC, pl.BlockSpec(memory_space=pltpu.VMEM)),
        input_output_aliases={0: 2, 1: 3}, compiler_params=pltpu.CompilerParams(has_side_effects=EFFECT),
    )(_in_hbm(src), _in_hbm(lax.empty(land_shape, src.dtype)))


def _wait_copies(name, started, after, plan):
    send_sems, recv_sems, src_thru, land_thru, _ = started

    def body(src_ref, land_ref, send_sems, recv_sems, *rest):
        for k, (s, _, to, mine) in enumerate(plan(src_ref, land_ref)):
            cp = pltpu.make_async_remote_copy(src_ref=s, dst_ref=mine, send_sem=send_sems.at[k],
                                              recv_sem=recv_sems.at[k], device_id=to, device_id_type=MESH)
            cp.wait_send()
            cp.wait_recv()

    return _pcall(
        body, name=name,
        out_shape=(pltpu.HBM(src_thru.shape, src_thru.dtype), pltpu.HBM(land_thru.shape, land_thru.dtype)),
        in_specs=(HBM_SPEC, HBM_SPEC, SEM_SPEC, SEM_SPEC) + (ANY_SPEC,) * len(after), out_specs=(HBM_SPEC, HBM_SPEC),
        input_output_aliases={0: 0, 1: 1}, compiler_params=pltpu.CompilerParams(has_side_effects=EFFECT),
    )(src_thru, land_thru, send_sems, recv_sems, *after)


def _start_inplace(name, buf, plan, n):
    def body(buf_ref, send_sems, recv_sems, buf_thru, token):
        for k, (s, d, to, _) in enumerate(plan(buf_ref, buf_ref)):
            pltpu.make_async_remote_copy(src_ref=s, dst_ref=d, send_sem=send_sems.at[k], recv_sem=recv_sems.at[k],
                                         device_id=to, device_id_type=MESH).start()
        token[...] = jnp.zeros_like(token)

    return _pcall(
        body, name=name,
        out_shape=(pltpu.SemaphoreType.DMA((n,)), pltpu.SemaphoreType.DMA((n,)), pltpu.HBM(buf.shape, buf.dtype),
                   jax.ShapeDtypeStruct((8, 128), F32)),
        in_specs=(HBM_SPEC,), out_specs=(SEM_SPEC, SEM_SPEC, HBM_SPEC, pl.BlockSpec(memory_space=pltpu.VMEM)),
        input_output_aliases={0: 2}, compiler_params=pltpu.CompilerParams(has_side_effects=EFFECT),
    )(_in_hbm(buf))


def _wait_inplace(name, started, after, plan):
    send_sems, recv_sems, buf_thru, _ = started

    def body(buf_ref, send_sems, recv_sems, *rest):
        for k, (s, _, to, mine) in enumerate(plan(buf_ref, buf_ref)):
            cp = pltpu.make_async_remote_copy(src_ref=s, dst_ref=mine, send_sem=send_sems.at[k],
                                              recv_sem=recv_sems.at[k], device_id=to, device_id_type=MESH)
            cp.wait_send()
            cp.wait_recv()

    return _pcall(
        body, name=name, out_shape=pltpu.HBM(buf_thru.shape, buf_thru.dtype),
        in_specs=(HBM_SPEC, SEM_SPEC, SEM_SPEC) + (ANY_SPEC,) * len(after), out_specs=HBM_SPEC,
        input_output_aliases={0: 0}, compiler_params=pltpu.CompilerParams(has_side_effects=EFFECT),
    )(buf_thru, send_sems, recv_sems, *after)


def _gather_plan(src_ref, land_ref):
    x, y, c = _place()
    peers = [(x, y, 1 - c), (1 - x, y, c), (x, 1 - y, c)]
    return [(src_ref, land_ref.at[4 * x + 2 * y + c], p, land_ref.at[4 * p[0] + 2 * p[1] + p[2]]) for p in peers]


def _relay_plan(buf_ref, _):
    x, y, c = _place()
    slot = lambda p, pc: 4 * p[0] + 2 * p[1] + pc
    xn, yn, dg, sib = (1 - x, y), (x, 1 - y), (1 - x, 1 - y), (x, y, 1 - c)
    half = buf_ref.shape[1] // 2
    lo, hi = pl.ds(0, half), pl.ds(half, half)
    return [(buf_ref.at[slot(xn, c)], buf_ref.at[slot(xn, c)], sib, buf_ref.at[slot(xn, 1 - c)]),
            (buf_ref.at[slot(yn, c)], buf_ref.at[slot(yn, c)], sib, buf_ref.at[slot(yn, 1 - c)]),
            (buf_ref.at[slot(xn, c), lo], buf_ref.at[slot(xn, c), lo], (*yn, c), buf_ref.at[slot(dg, c), lo]),
            (buf_ref.at[slot(yn, c), hi], buf_ref.at[slot(yn, c), hi], (*xn, c), buf_ref.at[slot(dg, c), hi])]


def _swap_plan(src_ref, land_ref):
    x, y, c = _place()
    return [(src_ref.at[:, pl.ds(1 - c, 1)], land_ref, (x, y, 1 - c), land_ref)]


def _exchange_plan(src_ref, land_ref):
    x, y, c = _place()
    chips = [(1 - x, y), (x, 1 - y), (1 - x, 1 - y)]
    return [(src_ref.at[2 * px + py], land_ref.at[2 * x + y], (px, py, c), land_ref.at[2 * px + py]) for px, py in chips]


def _gather_forward(land, block):
    def body(land_ref, out_ref, send_sems, recv_sems):
        x, y, c = _place()
        chips = [(1 - x, 1 - y)]

        def copy(k, px, py, pc):
            blk = out_ref.at[4 * px + 2 * py + pc]
            return pltpu.make_async_remote_copy(src_ref=blk, dst_ref=blk, send_sem=send_sems.at[k],
                                                recv_sem=recv_sems.at[k], device_id=(x, y, 1 - c), device_id_type=MESH)

        sent = [copy(k, px, py, c) for k, (px, py) in enumerate(chips)]
        for cp in sent:
            cp.start()
        for k, (px, py) in enumerate(chips):
            copy(k, px, py, 1 - c).wait_recv()
        for cp in sent:
            cp.wait_send()

    land = _pcall(
        body, name="allgather_rest_forward", out_shape=jax.ShapeDtypeStruct(land.shape, land.dtype),
        in_specs=[ANY_SPEC], out_specs=ANY_SPEC, input_output_aliases={0: 0},
        scratch_shapes=[pltpu.SemaphoreType.DMA((1,)), pltpu.SemaphoreType.DMA((1,))],
        compiler_params=pltpu.CompilerParams(has_side_effects=True),
    )(land)

    rows = block.shape[0]
    tr = rows // 4

    def place(me_ref, x_ref, land_ref, out_ref):
        out_ref[...] = x_ref[...]

    x, y, c = _place()
    grid_spec = pltpu.PrefetchScalarGridSpec(
        num_scalar_prefetch=1, grid=(rows // tr,),
        in_specs=[pl.BlockSpec((tr, D), lambda i, me: (i, 0)), ANY_SPEC],
        out_specs=pl.BlockSpec((None, tr, D), lambda i, me: (me[0], i, 0)))
    return _pcall(
        place, name="allgather_rest_own", grid_spec=grid_spec, out_shape=jax.ShapeDtypeStruct(land.shape, land.dtype),
        input_output_aliases={2: 0}, compiler_params=pltpu.CompilerParams(dimension_semantics=("arbitrary",)),
    )((4 * x + 2 * y + c).reshape(1), block, land)


class _ReduceScatter:
    def __init__(self, name, g):
        self.name = name
        rows = g.shape[1]
        self.started = _start_copies(name + "_swap_start", g.reshape(4, 2, rows, D), (4, 1, rows, D), _swap_plan, 1)
        self.token = self.started[4]

    def halfway(self, after):
        g4, theirs = _wait_copies(self.name + "_swap_wait", self.started, after, _swap_plan)
        self.own = _add_halves(g4, theirs, lax.axis_index("c").reshape(1), self.name + "_add_halves")
        self.started = _start_copies(self.name + "_exch_start", self.own, self.own.shape, _exchange_plan, 3)
        self.token = self.started[4]

    def finish(self, after):
        own, got = _wait_copies(self.name + "_exch_wait", self.started, after, _exchange_plan)
        chip = 2 * lax.axis_index("x") + lax.axis_index("y")
        return own, got, (chip + jnp.arange(4, dtype=jnp.int32)) % 4


def _pack_small(p, loss):
    def body(*refs):
        o_ref = refs[-1]
        o_ref[...] = jnp.zeros_like(o_ref)
        for ref, name in zip(refs, SMALL):
            r0, nr, c0, nc = SMALL_SLOT[name]
            o_ref[r0:r0 + nr, c0:c0 + nc] = ref[...]
        o_ref[17:18, 0:128] = refs[len(SMALL)][0:1, :]

    return _one_call(body, "pack_small_grads", [p[n] for n in SMALL] + [loss], [((SMALL_ROWS, D), F32)])[0]


_GAP_DEV, _GAP_ROW = divmod(GATE0 + 8, N_IN)
_GAP = CHK0 - GATE0 - 8


def _in_rows_to_proj(g):
    runs = [(j, 0, N_IN, N_IN * j) for j in range(_GAP_DEV)]
    runs += [(_GAP_DEV, 0, _GAP_ROW, N_IN * _GAP_DEV), (_GAP_DEV, _GAP_ROW, N_IN, N_IN * _GAP_DEV + _GAP_ROW + _GAP)]
    runs += [(j, 0, N_IN, N_IN * j + _GAP) for j in range(_GAP_DEV + 1, 8)]

    def body(g_ref, o_ref, acc_ref):
        acc_ref[...] = jnp.zeros_like(acc_ref)
        for j, r0, r1, dest in runs:
            start, shift = dest // 16 * 16, dest % 16
            win = -(-(shift + r1 - r0) // 16) * 16
            r = lax.broadcasted_iota(jnp.int32, (win, R_IN), 0)
            c = lax.broadcasted_iota(jnp.int32, (win, R_IN), 1)
            move = jnp.where((c >= r0) & (c < r1) & (r == c - r0 + shift), 1.0, 0.0).astype(BF16)
            acc_ref[start:start + win, :] += _nn(move, g_ref[j])
        o_ref[...] = acc_ref[...].astype(BF16)

    return _pcall(
        body, name="w_in_layout", out_shape=jax.ShapeDtypeStruct((PROJ, D), BF16), grid=(1,),
        in_specs=[pl.BlockSpec(g.shape, lambda i: (0, 0, 0))], out_specs=pl.BlockSpec((PROJ, D), lambda i: (0, 0)),
        scratch_shapes=[pltpu.VMEM((PROJ, D), F32)],
        compiler_params=pltpu.CompilerParams(dimension_semantics=("arbitrary",), vmem_limit_bytes=VMEM_BIG),
    )(g)


def _proj_rows_to_in(g):
    pad = lambda a: jnp.pad(a, ((0, R_IN - a.shape[0]), (0, 0)))
    lo = N_IN * _GAP_DEV
    shards = [pad(g[N_IN * j:N_IN * (j + 1)]) for j in range(_GAP_DEV)]
    shards.append(pad(jnp.concatenate([g[lo:lo + _GAP_ROW], g[lo + _GAP_ROW + _GAP:lo + N_IN + _GAP]], axis=0)))
    shards += [pad(g[N_IN * j + _GAP:N_IN * (j + 1) + _GAP]) for j in range(_GAP_DEV + 1, 8)]
    return jnp.stack(shards)


def _local_grads(x, mem, tgt, win_t, gw_of, sm, on_grads, after=()):
    b_pad = jnp.pad(sm['b_fgt'], ((0, 0), (0, 120)))
    tbl = jnp.pad(sm['rel_bias'], ((0, 0), (0, NREL_PAD - 257)))

    h1, proj, flog = _premix_fwd(x, sm['g_mix_pre'], win_t, after)
    c = _gate_fwd(flog, b_pad)
    ct3 = c[:, :8].T.reshape(4, 2, T)
    o_f, lse_f = _fox_fwd(proj, c, ct3)
    vt3 = _relvec_fwd(tbl).reshape(4, 2, VW)
    kvp = jnp.pad(proj[:, CHK0 + 512:], ((LEFT, 0), (0, 0)))
    o_c, lse_c = _chk_fwd(proj, kvp, vt3, [gw_of('relay', [o_f])])
    gw = gw_of('done', [o_c])
    w_out, w_mq, w_mk, w_mv, w_mo, w1_t, w2 = (_wblk(gw, n) for n in ('w_out', 'w_mq', 'w_mk', 'w_mv', 'w_mo', 'w_ff1', 'w_ff2'))
    ycat, z, x1, h2, qm = _postmix_fwd(x, o_f, o_c, sm['g_fox_out'], sm['g_chk_out'], w_out,
                                       sm['g_mix_post'], sm['g_mem_pre'], w_mq)
    memn, km, vm = _memkv_fwd(mem, sm['g_mem_kv'], w_mk, w_mv)
    om, ym, x2, h3 = _mem_fwd(qm, x1, km, vm, w_mo, sm['g_mem_post'], sm['g_ff_pre'])

    gs = {}
    dx2, da, dy3, r, loss_acc, gs['g_ff_post'], gs['g_ff_pre'] = _ffn_step(h3, x2, tgt, w1_t, w2, sm['g_ff_post'],
                                                                         sm['g_ff_pre'])
    tok = on_grads('A', _wgrad_group("wgrad_ff", [(da, h3), (r, dy3)], 512), None)
    dx1, dym, dqm, dkm, dvm, gs['g_mem_post'], gs['g_mem_pre'] = _mem_bwd(
        dx2, ym, x1, qm, km, vm, w_mo, w_mq, sm['g_mem_post'], sm['g_mem_pre'], [tok])
    tok = on_grads('A halfway', None, [dx1])
    gs['g_mem_kv'] = _memkv_bwd(dkm, dvm, mem, w_mk, w_mv)
    dz, dof, doc, gs['g_mix_post'], gs['g_fox_out'], gs['g_chk_out'] = _postmix_bwd(
        dx1, z, o_f, o_c, w_out, sm['g_mix_post'], sm['g_fox_out'], sm['g_chk_out'], [tok])
    tok = on_grads('B', _wgrad_group("wgrad_mem_out", [(ycat, dz), (h2, dqm), (memn, dkm), (memn, dvm), (om, dym)], 128,
                                     whole=True), None)
    dq_f, dk_f, dv_f, dct, dcq = _fox_bwd(proj, c, ct3, o_f, lse_f, dof, [tok])
    tok = on_grads('B halfway', None, [dq_f])
    dq_c, dk_c, dv_c, gv = _chk_bwd(proj, kvp, vt3, o_c, lse_c, doc, [tok])
    gs['rel_bias'] = _relvec_bwd(gv.reshape(8, VW))[:, :257]
    dc = jnp.pad(dct.reshape(8, T).T + dcq[:, :, :2].transpose(1, 0, 2).reshape(T, 8), ((0, 0), (0, 120)))
    dflog, db = _gate_bwd(dc, flog, b_pad)
    gs['b_fgt'] = db[0:1, :8]
    grad_x, dproj, gs['g_mix_pre'] = _premix_bwd(dx1, x, [dq_f, dk_f, dv_f, dflog, dq_c, dk_c, dv_c], win_t, sm['g_mix_pre'])
    on_grads('C', _proj_rows_to_in(_wgrad(dproj, h1, "wgrad_in")).astype(BF16), None)
    return loss_acc, grad_x, gs


def kernel(x, mem, w_in, b_fgt, rel_bias, g_fox_out, g_chk_out, w_out, g_mix_pre, g_mix_post, g_mem_kv, w_mq, w_mk, w_mv, w_mo, g_mem_pre, g_mem_post, w_ff1, w_ff2, g_ff_pre, g_ff_post, loss_target, m_w_in, m_b_fgt, m_rel_bias, m_g_fox_out, m_g_chk_out, m_w_out, m_g_mix_pre, m_g_mix_post, m_g_mem_kv, m_w_mq, m_w_mk, m_w_mv, m_w_mo, m_g_mem_pre, m_g_mem_post, m_w_ff1, m_w_ff2, m_g_ff_pre, m_g_ff_post, v_w_in, v_b_fgt, v_rel_bias, v_g_fox_out, v_g_chk_out, v_w_out, v_g_mix_pre, v_g_mix_post, v_g_mem_kv, v_w_mq, v_w_mk, v_w_mv, v_w_mo, v_g_mem_pre, v_g_mem_post, v_w_ff1, v_w_ff2, v_g_ff_pre, v_g_ff_post):
    args = dict(locals())
    two_d = lambda a: a.reshape(a.shape[-2:])
    w = {n: two_d(args[n]) for n in WEIGHTS}
    m = {n: two_d(args['m_' + n]) for n in WEIGHTS}
    v = {n: two_d(args['v_' + n]) for n in WEIGHTS}

    sm = {n: w[n] for n in SMALL}
    shard_in = jnp.pad(w['w_in'].T, ((0, R_IN - N_IN), (0, 0))).astype(BF16)
    gathered_in, zero = _allgather(shard_in, "allgather_w_in")
    win_t = _in_rows_to_proj(gathered_in)
    shard_rest = (jnp.concatenate([w['w_ff1'].T, w['w_ff2'], w['w_out'], w['w_mq'], w['w_mk'], w['w_mv'], w['w_mo']],
                                  axis=0) + zero[0, 0]).astype(BF16)
    gather = {'first': _start_copies("allgather_rest_start", shard_rest, (8, R_REST, D), _gather_plan, 3)}

    def gw_of(stage, after):
        if stage == 'relay':
            gather['block'], land = _wait_copies("allgather_rest_wait", gather['first'], after, _gather_plan)
            gather['second'] = _start_inplace("allgather_rest_relay_start", land, _relay_plan, 4)
            return gather['second'][3]
        land = _wait_inplace("allgather_rest_relay_wait", gather['second'], after, _relay_plan)
        return _gather_forward(land, gather['block'])

    rs = {}

    def on_grads(stage, g, after):
        if stage.endswith('halfway'):
            rs[stage[0]].halfway(after)
            return rs[stage[0]].token
        rs[stage] = _ReduceScatter("rs_" + stage.lower(), g)
        return rs[stage].token

    loss_local, grad_x, gs = _local_grads(x[0], mem[0], loss_target[0], win_t, gw_of, sm, on_grads, [gather['first'][4]])
    grads, deltas, new_m, new_v = {}, {}, {}, {}

    def update(n, out):
        grads[n], deltas[n], new_m[n], new_v[n] = out

    gparts, _ = _allgather(_pack_small(gs, loss_local), "allgather_small_grads", [rs['C'].token])
    small = _adamw_small(gparts, [w[n] for n in SMALL], [m[n] for n in SMALL], [v[n] for n in SMALL])
    loss = small[0][0, 0]
    for t, n in enumerate(SMALL):
        update(n, small[1 + 4 * t:5 + 4 * t])
    rs['C'].halfway([small[0]])

    own, got, order = rs['A'].finish([grad_x, rs['C'].started[4]])
    update('w_ff1', _sum_adam(own, got, order, 0, w['w_ff1'], m['w_ff1'], v['w_ff1'], "adamw_w_ff1", transposed=True))
    update('w_ff2', _sum_adam(own, got, order, 512, w['w_ff2'], m['w_ff2'], v['w_ff2'], "adamw_w_ff2"))
    own, got, order = rs['B'].finish([grad_x, rs['C'].started[4]])
    names_b = ('w_out', 'w_mq', 'w_mk', 'w_mv', 'w_mo')
    done = _sum_adam_rows(own, got, order, [w[n] for n in names_b], [m[n] for n in names_b], [v[n] for n in names_b],
                          "adamw_group_b")
    for k, n in enumerate(names_b):
        update(n, done[4 * k:4 * k + 4])

    own, got, order = rs['C'].finish([new_v[n] for n in BIG if n != 'w_in'])
    done = _sum_adam(own, got, order, 0, w['w_in'].T, m['w_in'].T, v['w_in'].T, "adamw_w_in")
    update('w_in', [a.T for a in done])

    out = [loss, grad_x[None]]
    for group in (grads, deltas, new_m, new_v):
        out += [group[n].reshape(args[n].shape) for n in WEIGHTS]
    return tuple(out)
```

```python
import jax
import jax.numpy as jnp
from jax import lax
from jax.experimental import pallas as pl
from jax.experimental.pallas import tpu as pltpu

F32 = jnp.float32
BF16 = jnp.bfloat16
MESH = pl.DeviceIdType.MESH

T = 2048
D = 1024
NMEM = 256
DFF = 4096
EPS = 1e-6
TM = 256
TM_WIDE = 512
TQ = 256
FQ = 512
HD = 64
SCALE = HD ** -0.5
MEM_HEADS = 4
MEM_HD = 256
MEM_SCALE = MEM_HD ** -0.5
NEG = -1e30
LEFT = 512
WIN = LEFT + TQ
VW = 1024
NREL_PAD = 384
PROJ = 3200
GATE0 = 1536
CHK0 = 1664
VMEM_BIG = 56 * 1024 * 1024

ADAM_LR = 0.001
ADAM_B1 = 0.9
ADAM_B2 = 0.999
ADAM_EPS = 1e-08
ADAM_WD = 0.01
ADAM_STEP = 10

N_IN = 385
R_IN = 400
R_REST = 1664
W_ROWS = {'w_ff1': (0, 512), 'w_ff2': (512, 512),
          'w_out': (1024, 128), 'w_mq': (1152, 128), 'w_mk': (1280, 128), 'w_mv': (1408, 128), 'w_mo': (1536, 128)}
SMALL_ROWS = 24
SMALL_SLOT = {'rel_bias': (0, 8, 0, 257), 'b_fgt': (8, 1, 0, 8), 'g_fox_out': (9, 1, 0, 512), 'g_chk_out': (9, 1, 512, 512),
              'g_mix_pre': (10, 1, 0, 1024), 'g_mix_post': (11, 1, 0, 1024), 'g_mem_kv': (12, 1, 0, 1024),
              'g_mem_pre': (13, 1, 0, 1024), 'g_mem_post': (14, 1, 0, 1024), 'g_ff_pre': (15, 1, 0, 1024),
              'g_ff_post': (16, 1, 0, 1024)}

WEIGHTS = ['w_in', 'b_fgt', 'rel_bias', 'g_fox_out', 'g_chk_out', 'w_out', 'g_mix_pre', 'g_mix_post', 'g_mem_kv',
           'w_mq', 'w_mk', 'w_mv', 'w_mo', 'g_mem_pre', 'g_mem_post', 'w_ff1', 'w_ff2', 'g_ff_pre', 'g_ff_post']
BIG = ['w_in', 'w_out', 'w_mq', 'w_mk', 'w_mv', 'w_mo', 'w_ff1', 'w_ff2']
SMALL = [n for n in WEIGHTS if n not in BIG]


def _pcall(body, **kw):
    return pl.pallas_call(body, **kw)


def _nn(a, b):
    return jnp.dot(a, b, preferred_element_type=F32)


def _nt(a, b):
    return lax.dot_general(a, b, (((1,), (1,)), ((), ())), preferred_element_type=F32)


def _tn(a, b):
    return lax.dot_general(a, b, (((0,), (0,)), ((), ())), preferred_element_type=F32)


def _w(ref):
    v = ref[...]
    return v if v.ndim == 2 else v.reshape(-1, v.shape[-1])


def _rstd(x):
    return lax.rsqrt(jnp.mean(x * x, axis=-1, keepdims=True) + EPS)


def _rms(x, g):
    return x * _rstd(x) * g


def _rms_bwd(x, g, dy):
    r = _rstd(x)
    xh = x * r
    dg = jnp.sum(dy * xh, axis=0, keepdims=True)
    dxh = dy * g
    dx = r * (dxh - xh * jnp.mean(dxh * xh, axis=-1, keepdims=True))
    return dx, dg


def _resident(a):
    if isinstance(a, tuple):
        _, shape, index = a
        return pl.BlockSpec(shape, lambda *_: index, pipeline_mode=pl.Buffered(1))
    return pl.BlockSpec(a.shape, lambda *_, nd=a.ndim: (0,) * nd, pipeline_mode=pl.Buffered(1))


def _wblk(gw, name):
    r0, rows = W_ROWS[name]
    return (gw, (8, rows, D), (0, r0 // rows, 0))


def _behind(body, n_in, after):
    if not after:
        return body
    return lambda *refs: body(*refs[:n_in], *refs[n_in + len(after):])


def _tok_call(body, name, tiled, full, outs_tiled, outs_acc=(), rows=T, tm=TM, vmem=None, after=()):
    in_specs = [pl.BlockSpec((tm, a.shape[1]), lambda i: (i, 0)) for a in tiled]
    in_specs += [_resident(a) for a in full] + [ANY_SPEC] * len(after)
    full = [a[0] if isinstance(a, tuple) else a for a in full] + list(after)
    body = _behind(body, len(tiled) + len(full) - len(after), after)
    out_shape = [jax.ShapeDtypeStruct((rows, c), dt) for c, dt in outs_tiled]
    out_shape += [jax.ShapeDtypeStruct(s, F32) for s in outs_acc]
    out_specs = [pl.BlockSpec((tm, c), lambda i: (i, 0)) for c, _ in outs_tiled]
    out_specs += [pl.BlockSpec(s, lambda i, nd=len(s): (0,) * nd) for s in outs_acc]
    return _pcall(
        body, name=name, grid=(rows // tm,), in_specs=in_specs, out_specs=out_specs, out_shape=out_shape,
        compiler_params=pltpu.CompilerParams(dimension_semantics=("arbitrary",), vmem_limit_bytes=vmem),
    )(*tiled, *full)


def _one_call(body, name, ins, outs, vmem=None):
    whole = lambda s: pl.BlockSpec(s, lambda i, nd=len(s): (0,) * nd)
    return _pcall(
        body, name=name, grid=(1,), in_specs=[_resident(a) for a in ins], out_specs=[whole(s) for s, _ in outs],
        out_shape=[jax.ShapeDtypeStruct(s, dt) for s, dt in outs],
        compiler_params=pltpu.CompilerParams(dimension_semantics=("arbitrary",), vmem_limit_bytes=vmem),
    )(*[a[0] if isinstance(a, tuple) else a for a in ins])


def _premix_fwd(x, g_pre, win_t, after=()):
    def body(x_ref, g_ref, w_ref, h_ref, proj_ref, flog_ref):
        h = _rms(x_ref[...], g_ref[...]).astype(BF16)
        h_ref[...] = h
        p = _nt(h, w_ref[...])
        proj_ref[...] = p.astype(BF16)
        flog_ref[...] = p[:, GATE0:GATE0 + 128]

    return _tok_call(body, "premix_fwd", [x], [g_pre, win_t],
                     [(D, BF16), (PROJ, BF16), (128, F32)], tm=TM_WIDE, vmem=VMEM_BIG, after=after)


def _postmix_fwd(x, o_f, o_c, g_fo, g_co, w_out, g_post, g_mpre, w_mq):
    def body(x_ref, of_ref, oc_ref, gfo_ref, gco_ref, wo_ref, gp_ref, gm_ref, wq_ref,
             y_ref, z_ref, x1_ref, h2_ref, qm_ref):
        y_ref[:, :512] = _rms(of_ref[...], gfo_ref[...]).astype(BF16)
        y_ref[:, 512:] = _rms(oc_ref[...], gco_ref[...]).astype(BF16)
        z = _nn(y_ref[...], _w(wo_ref))
        z_ref[...] = z
        x1 = x_ref[...] + _rms(z, gp_ref[...])
        x1_ref[...] = x1
        h2 = _rms(x1, gm_ref[...]).astype(BF16)
        h2_ref[...] = h2
        qm_ref[...] = _nn(h2, _w(wq_ref)).astype(BF16)

    return _tok_call(body, "postmix_fwd", [x, o_f, o_c], [g_fo, g_co, w_out, g_post, g_mpre, w_mq],
                     [(D, BF16), (D, F32), (D, F32), (D, BF16), (D, BF16)], tm=TM_WIDE, vmem=VMEM_BIG)


def _memkv_fwd(mem, g_kv, w_mk, w_mv):
    def body(m_ref, g_ref, wk_ref, wv_ref, mn_ref, k_ref, v_ref):
        mn = _rms(m_ref[...], g_ref[...]).astype(BF16)
        mn_ref[...] = mn
        k_ref[...] = _nn(mn, _w(wk_ref)).astype(BF16)
        v_ref[...] = _nn(mn, _w(wv_ref)).astype(BF16)

    return _tok_call(body, "memkv_fwd", [mem], [g_kv, w_mk, w_mv],
                     [(D, BF16), (D, BF16), (D, BF16)], rows=NMEM, tm=NMEM, vmem=VMEM_BIG)


def _mem_fwd(qm, x1, km, vm, w_mo, g_post, g_fpre):
    def body(q_ref, x1_ref, k_ref, v_ref, wo_ref, gp_ref, gf_ref, om_ref, ym_ref, x2_ref, h3_ref):
        for h in range(MEM_HEADS):
            sl = slice(h * MEM_HD, (h + 1) * MEM_HD)
            s = _nt(q_ref[:, sl], k_ref[:, sl]) * MEM_SCALE
            p = jnp.exp(s - jnp.max(s, axis=-1, keepdims=True))
            p = p / jnp.sum(p, axis=-1, keepdims=True)
            om_ref[:, sl] = _nn(p.astype(BF16), v_ref[:, sl]).astype(BF16)
        ym = _nn(om_ref[...], _w(wo_ref))
        ym_ref[...] = ym
        x2 = x1_ref[...] + _rms(ym, gp_ref[...])
        x2_ref[...] = x2
        h3_ref[...] = _rms(x2, gf_ref[...]).astype(BF16)

    return _tok_call(body, "mem_fwd", [qm, x1], [km, vm, w_mo, g_post, g_fpre],
                     [(D, BF16), (D, F32), (D, F32), (D, BF16)], tm=TM_WIDE, vmem=VMEM_BIG)


def _tri(lower):
    r = lax.broadcasted_iota(jnp.int32, (128, 128), 0)
    c = lax.broadcasted_iota(jnp.int32, (128, 128), 1)
    return jnp.where(r >= c if lower else c >= r, 1.0, 0.0).astype(F32)


def _hdot(a, b):
    return jnp.dot(a, b, preferred_element_type=F32, precision=lax.Precision.HIGHEST)


def _gate_fwd(flog, b_pad):
    def body(f_ref, b_ref, c_ref):
        tri = _tri(True)

        def step(i, carry):
            rows = pl.ds(pl.multiple_of(i * 128, 128), 128)
            z = f_ref[rows, :] + b_ref[...]
            lf = jnp.minimum(z, 0.0) - jnp.log(1.0 + jnp.exp(-jnp.abs(z)))
            cb = _hdot(tri, lf) + carry
            c_ref[rows, :] = cb
            return cb[127:128, :]

        lax.fori_loop(0, T // 128, step, jnp.zeros((1, 128), F32))

    return _one_call(body, "gate_fwd", [flog, b_pad], [((T, 128), F32)])[0]


def _gate_bwd(dc, flog, b_pad):
    def body(dc_ref, f_ref, b_ref, df_ref, db_ref):
        tri = _tri(False)

        def step(j, carry):
            run, db = carry
            i = T // 128 - 1 - j
            rows = pl.ds(pl.multiple_of(i * 128, 128), 128)
            dcb = dc_ref[rows, :]
            rb = _hdot(tri, dcb) + run
            z = f_ref[rows, :] + b_ref[...]
            df = rb * (1.0 / (1.0 + jnp.exp(z)))
            df_ref[rows, :] = df.astype(BF16)
            return run + jnp.sum(dcb, axis=0, keepdims=True), db + jnp.sum(df, axis=0, keepdims=True)

        _, db = lax.fori_loop(0, T // 128, step, (jnp.zeros((1, 128), F32), jnp.zeros((1, 128), F32)))
        db_ref[...] = jnp.broadcast_to(db, (8, 128))

    return _one_call(body, "gate_bwd", [dc, flog, b_pad], [((T, 128), BF16), ((8, 128), F32)])


def _lane_lo(rows=TQ):
    return lax.broadcasted_iota(jnp.int32, (rows, 128), 1) < HD


def _half(v, lo, a, scale=None):
    keep = lo if a == 0 else jnp.logical_not(lo)
    v = v.astype(F32) if scale is None else v.astype(F32) * scale
    return jnp.where(keep, v, 0.0).astype(BF16)


def _fox_specs():
    return [pl.BlockSpec((FQ, 128), lambda h, i: (i, h)),
            pl.BlockSpec((T, 128), lambda h, i: (0, 4 + h)),
            pl.BlockSpec((T, 128), lambda h, i: (0, 8 + h))]


def _lane_pick(x, at):
    lane = lax.broadcasted_iota(jnp.int32, x.shape, 1)
    return jnp.sum(jnp.where(lane == at, x, 0.0), axis=-1, keepdims=True)


def _fox_fwd(proj, c, ct3):
    def body(q_ref, k_ref, v_ref, c_ref, ct_ref, o_ref, l_ref):
        i = pl.program_id(1)
        lo = _lane_lo(FQ)
        causal = lax.broadcasted_iota(jnp.int32, (FQ, FQ), 1) <= lax.broadcasted_iota(jnp.int32, (FQ, FQ), 0)
        q = q_ref[...]
        qs = [_half(q, lo, a, SCALE) for a in range(2)]
        cqs = [_lane_pick(c_ref[...], 2 * pl.program_id(0) + a) for a in range(2)]

        def tile(off, carry, diagonal):
            kblk = k_ref[pl.ds(off, FQ), :]
            vblk = v_ref[pl.ds(off, FQ), :]
            new = []
            for a in range(2):
                m, l, acc = carry[a]
                s = _nt(qs[a], kblk) + (cqs[a] - ct_ref[a:a + 1, pl.ds(off, FQ)])
                if diagonal:
                    s = jnp.where(causal, s, NEG)
                m2 = jnp.maximum(m, jnp.max(s, axis=-1, keepdims=True))
                p = jnp.exp(s - m2)
                alpha = jnp.exp(m - m2)
                new.append((m2, alpha * l + jnp.sum(p, axis=-1, keepdims=True),
                            alpha * acc + _nn(p.astype(BF16), vblk)))
            return tuple(new)

        init = (jnp.full((FQ, 1), NEG, F32), jnp.zeros((FQ, 1), F32), jnp.zeros((FQ, 128), F32))
        carry = lax.fori_loop(0, i, lambda kb, c: tile(pl.multiple_of(kb * FQ, FQ), c, False), (init, init))
        carry = tile(pl.multiple_of(i * FQ, FQ), carry, True)
        outs = [acc / l for _, l, acc in carry]
        lses = [m + jnp.log(l) for m, l, _ in carry]
        lane = lax.broadcasted_iota(jnp.int32, (FQ, 128), 1)
        l_ref[...] = jnp.where(lane == 0, lses[0], jnp.where(lane == 1, lses[1], 0.0))
        o_ref[...] = jnp.where(lo, outs[0], outs[1])

    return _pcall(
        body, name="fox_fwd", grid=(4, T // FQ),
        in_specs=_fox_specs() + [pl.BlockSpec((FQ, 128), lambda h, i: (i, 0)),
                                 pl.BlockSpec((None, 2, T), lambda h, i: (h, 0, 0))],
        out_specs=[pl.BlockSpec((FQ, 128), lambda h, i: (i, h)), pl.BlockSpec((None, FQ, 128), lambda h, i: (h, i, 0))],
        out_shape=[jax.ShapeDtypeStruct((T, 512), F32), jax.ShapeDtypeStruct((4, T, 128), F32)],
        compiler_params=pltpu.CompilerParams(dimension_semantics=("arbitrary", "arbitrary"), vmem_limit_bytes=VMEM_BIG),
    )(proj, proj, proj, c, ct3)


def _fox_bwd(proj, c, ct3, o, lse, do, after=()):
    def body(q_ref, k_ref, v_ref, c_ref, ct_ref, o_ref, l_ref, do_ref, dq_ref, dkb_ref, dvb_ref, dct_ref, dcq_ref,
             dk_ref, dv_ref):
        i = pl.program_id(1)

        @pl.when(i == 0)
        def _():
            dk_ref[...] = jnp.zeros_like(dk_ref)
            dv_ref[...] = jnp.zeros_like(dv_ref)
            dct_ref[...] = jnp.zeros_like(dct_ref)

        lo = _lane_lo(FQ)
        causal = lax.broadcasted_iota(jnp.int32, (FQ, FQ), 1) <= lax.broadcasted_iota(jnp.int32, (FQ, FQ), 0)
        q = q_ref[...]
        do_v = do_ref[...]
        prod = do_v * o_ref[...]
        qs = [_half(q, lo, a, SCALE) for a in range(2)]
        dos = [_half(do_v, lo, a) for a in range(2)]
        deltas = [jnp.sum(jnp.where(lo if a == 0 else jnp.logical_not(lo), prod, 0.0), axis=-1, keepdims=True)
                  for a in range(2)]
        cqs = [_lane_pick(c_ref[...], 2 * pl.program_id(0) + a) for a in range(2)]
        las = [_lane_pick(l_ref[...], a) for a in range(2)]

        def tile(off, carry, diagonal):
            kblk = k_ref[pl.ds(off, FQ), :]
            vblk = v_ref[pl.ds(off, FQ), :]
            new = []
            dk = jnp.zeros((128, FQ), F32)
            dv = jnp.zeros((128, FQ), F32)
            for a in range(2):
                dq_acc, rs = carry[a]
                s = _nt(qs[a], kblk) + (cqs[a] - ct_ref[a:a + 1, pl.ds(off, FQ)])
                if diagonal:
                    s = jnp.where(causal, s, NEG)
                p = jnp.exp(s - las[a])
                ds = p * (_nt(dos[a], vblk) - deltas[a])
                dsb = ds.astype(BF16)
                dk = dk + _tn(qs[a], dsb)
                dv = dv + _tn(dos[a], p.astype(BF16))
                dct_ref[a:a + 1, pl.ds(off, FQ)] -= jnp.sum(ds, axis=0, keepdims=True)
                new.append((dq_acc + _nn(dsb, kblk), rs + jnp.sum(ds, axis=-1, keepdims=True)))
            dk_ref[:, pl.ds(off, FQ)] += dk
            dv_ref[:, pl.ds(off, FQ)] += dv
            return tuple(new)

        init = (jnp.zeros((FQ, 128), F32), jnp.zeros((FQ, 1), F32))
        carry = lax.fori_loop(0, i, lambda kb, c: tile(pl.multiple_of(kb * FQ, FQ), c, False), (init, init))
        carry = tile(pl.multiple_of(i * FQ, FQ), carry, True)
        lane = lax.broadcasted_iota(jnp.int32, (FQ, 128), 1)
        dcq_ref[...] = jnp.where(lane == 0, carry[0][1], jnp.where(lane == 1, carry[1][1], 0.0))
        dq_ref[...] = (jnp.where(lo, carry[0][0], carry[1][0]) * SCALE).astype(BF16)

        @pl.when(i == T // FQ - 1)
        def _():
            dkb_ref[...] = dk_ref[...].T.astype(BF16)
            dvb_ref[...] = dv_ref[...].T.astype(BF16)

    blk = pl.BlockSpec((FQ, 128), lambda h, i: (i, h))
    pair = pl.BlockSpec((None, FQ, 128), lambda h, i: (h, i, 0))
    rows = pl.BlockSpec((None, 2, T), lambda h, i: (h, 0, 0))
    col = pl.BlockSpec((T, 128), lambda h, i: (0, h))
    return _pcall(
        _behind(body, 8, after), name="fox_bwd", grid=(4, T // FQ),
        in_specs=_fox_specs() + [pl.BlockSpec((FQ, 128), lambda h, i: (i, 0)), rows, blk, pair, blk] + [ANY_SPEC] * len(after),
        out_specs=[blk, col, col, rows, pair],
        out_shape=[jax.ShapeDtypeStruct((T, 512), BF16), jax.ShapeDtypeStruct((T, 512), BF16),
                   jax.ShapeDtypeStruct((T, 512), BF16), jax.ShapeDtypeStruct((4, 2, T), F32),
                   jax.ShapeDtypeStruct((4, T, 128), F32)],
        scratch_shapes=[pltpu.VMEM((128, T), F32), pltpu.VMEM((128, T), F32)],
        compiler_params=pltpu.CompilerParams(dimension_semantics=("arbitrary", "arbitrary"), vmem_limit_bytes=VMEM_BIG),
    )(proj, proj, proj, c, ct3, o, lse, do, *after)


def _rel_onehot():
    ridx = lax.broadcasted_iota(jnp.int32, (NREL_PAD, VW), 0)
    j = lax.broadcasted_iota(jnp.int32, (NREL_PAD, VW), 1)
    return jnp.where(ridx == jnp.clip(TQ + LEFT - 1 - j, -128, 128) + 128, 1.0, 0.0).astype(F32)


def _relvec_fwd(tbl):
    def body(t_ref, v_ref):
        v_ref[...] = _hdot(t_ref[...], _rel_onehot())

    return _one_call(body, "relvec_fwd", [tbl], [((8, VW), F32)])[0]


def _relvec_bwd(gv):
    def body(g_ref, t_ref):
        t_ref[...] = lax.dot_general(g_ref[...], _rel_onehot(), (((1,), (1,)), ((), ())),
                                     preferred_element_type=F32, precision=lax.Precision.HIGHEST)

    return _one_call(body, "relvec_bwd", [gv], [((8, NREL_PAD), F32)])[0]


def _chk_bias(vt_ref, a, hidden):
    vb = jnp.broadcast_to(vt_ref[a:a + 1, :], (TQ, VW))
    y = pltpu.roll(vb, VW - (TQ - 1), 1, stride=1, stride_axis=0)[:, :WIN]
    cr = lax.broadcasted_iota(jnp.int32, (TQ, WIN), 0) // 64
    m = lax.broadcasted_iota(jnp.int32, (TQ, WIN), 1)
    return jnp.where((m // 64 >= cr) & (m // 64 <= cr + 8) & (m >= hidden), y, NEG)


def _chk_specs():
    return [pl.BlockSpec((TQ, 128), lambda h, i: (i, CHK0 // 128 + h)),
            pl.BlockSpec((T + LEFT, 128), lambda h, i: (0, h)),
            pl.BlockSpec((T + LEFT, 128), lambda h, i: (0, 4 + h)),
            pl.BlockSpec((None, 2, VW), lambda h, i: (h, 0, 0))]


def _chk_fwd(proj, kvp, vt3, after=()):
    def body(q_ref, k_ref, v_ref, vt_ref, o_ref, l_ref, bias_ref):
        i = pl.program_id(1)

        @pl.when(i == 0)
        def _():
            for first in range(3):
                for a in range(2):
                    bias_ref[first, a] = _chk_bias(vt_ref, a, max(LEFT - first * TQ, 0))

        lo = _lane_lo()
        off = pl.multiple_of(i * TQ, TQ)
        kw = k_ref[pl.ds(off, WIN), :]
        vw = v_ref[pl.ds(off, WIN), :]
        bias_at = jnp.minimum(i, 2)
        q = q_ref[...]
        outs, lses = [], []
        for a in range(2):
            s = _nt(_half(q, lo, a, SCALE), kw) + bias_ref[bias_at, a]
            m = jnp.max(s, axis=-1, keepdims=True)
            p = jnp.exp(s - m)
            l = jnp.sum(p, axis=-1, keepdims=True)
            outs.append(_nn(p.astype(BF16), vw) / l)
            lses.append(m + jnp.log(l))
        lane = lax.broadcasted_iota(jnp.int32, (TQ, 128), 1)
        l_ref[...] = jnp.where(lane == 0, lses[0], jnp.where(lane == 1, lses[1], 0.0))
        o_ref[...] = jnp.where(lo, outs[0], outs[1])

    return _pcall(
        _behind(body, 4, after), name="chk_fwd", grid=(4, T // TQ), in_specs=_chk_specs() + [ANY_SPEC] * len(after),
        out_specs=[pl.BlockSpec((TQ, 128), lambda h, i: (i, h)), pl.BlockSpec((None, TQ, 128), lambda h, i: (h, i, 0))],
        out_shape=[jax.ShapeDtypeStruct((T, 512), F32), jax.ShapeDtypeStruct((4, T, 128), F32)],
        scratch_shapes=[pltpu.VMEM((3, 2, TQ, WIN), F32)],
        compiler_params=pltpu.CompilerParams(dimension_semantics=("arbitrary", "arbitrary")),
    )(proj, kvp, kvp, vt3, *after)


def _chk_bwd(proj, kvp, vt3, o, lse, do, after=()):
    nq = T // TQ

    def body(q_ref, k_ref, v_ref, vt_ref, o_ref, l_ref, do_ref, dq_ref, dkb_ref, dvb_ref, gv_ref, bias_ref, dsum_ref,
             dk_ref, dv_ref):
        i = pl.program_id(1)

        @pl.when(i == 0)
        def _():
            for first in range(3):
                for a in range(2):
                    bias_ref[first, a] = _chk_bias(vt_ref, a, max(LEFT - first * TQ, 0))
            dsum_ref[...] = jnp.zeros_like(dsum_ref)
            dk_ref[...] = jnp.zeros_like(dk_ref)
            dv_ref[...] = jnp.zeros_like(dv_ref)

        lo = _lane_lo()
        off = pl.multiple_of(i * TQ, TQ)
        kw = k_ref[pl.ds(off, WIN), :]
        vw = v_ref[pl.ds(off, WIN), :]
        bias_at = jnp.minimum(i, 2)
        q = q_ref[...]
        do_v = do_ref[...]
        prod = do_v * o_ref[...]
        dqs = []
        for a in range(2):
            keep = lo if a == 0 else jnp.logical_not(lo)
            qa = _half(q, lo, a, SCALE)
            doa = _half(do_v, lo, a)
            delta = jnp.sum(jnp.where(keep, prod, 0.0), axis=-1, keepdims=True)
            s = _nt(qa, kw) + bias_ref[bias_at, a]
            p = jnp.exp(s - _lane_pick(l_ref[...], a))
            ds = p * (_nt(doa, vw) - delta)
            dsum_ref[a] += ds
            dsb = ds.astype(BF16)
            dk_ref[:, pl.ds(off, WIN)] += _tn(qa, dsb)
            dv_ref[:, pl.ds(off, WIN)] += _tn(doa, p.astype(BF16))
            dqs.append(_nn(dsb, kw))
        dq_ref[...] = (jnp.where(lo, dqs[0], dqs[1]) * SCALE).astype(BF16)

        @pl.when(i == nq - 1)
        def _():
            dkb_ref[...] = dk_ref[:, LEFT:].T.astype(BF16)
            dvb_ref[...] = dv_ref[:, LEFT:].T.astype(BF16)
            rr = lax.broadcasted_iota(jnp.int32, (TQ, TQ), 0)
            cc = lax.broadcasted_iota(jnp.int32, (TQ, TQ), 1)
            flip = jnp.where(rr + cc == TQ - 1, 1.0, 0.0).astype(F32)
            for a in range(2):
                dpad = jnp.concatenate([dsum_ref[a], jnp.zeros((TQ, VW - WIN), F32)], axis=1)
                z = pltpu.roll(_hdot(flip, dpad), 0, 1, stride=1, stride_axis=0)
                gv_ref[a:a + 1, :] = jnp.sum(z, axis=0, keepdims=True)

    blk = pl.BlockSpec((TQ, 128), lambda h, i: (i, h))
    pair = pl.BlockSpec((None, TQ, 128), lambda h, i: (h, i, 0))
    col = pl.BlockSpec((T, 128), lambda h, i: (0, h))
    return _pcall(
        _behind(body, 7, after), name="chk_bwd", grid=(4, nq), in_specs=_chk_specs() + [blk, pair, blk] + [ANY_SPEC] * len(after),
        out_specs=[blk, col, col, pl.BlockSpec((None, 2, VW), lambda h, i: (h, 0, 0))],
        out_shape=[jax.ShapeDtypeStruct((T, 512), BF16), jax.ShapeDtypeStruct((T, 512), BF16),
                   jax.ShapeDtypeStruct((T, 512), BF16), jax.ShapeDtypeStruct((4, 2, VW), F32)],
        scratch_shapes=[pltpu.VMEM((3, 2, TQ, WIN), F32), pltpu.VMEM((2, TQ, WIN), F32),
                        pltpu.VMEM((128, T + LEFT), F32), pltpu.VMEM((128, T + LEFT), F32)],
        compiler_params=pltpu.CompilerParams(dimension_semantics=("arbitrary", "arbitrary")),
    )(proj, kvp, kvp, vt3, o, lse, do, *after)


def _zero_at_start(*refs):
    @pl.when(pl.program_id(0) == 0)
    def _():
        for r in refs:
            r[...] = jnp.zeros_like(r)


def _ffn_step(h3, x2, tgt, w1_t, w2, g_post, g_pre):
    def body(h_ref, x2_ref, t_ref, w1_ref, w2_ref, gp_ref, gf_ref, dx2_ref, da_ref, dy_ref, r_ref, loss_ref, dgp_ref, dgf_ref):
        _zero_at_start(loss_ref, dgp_ref, dgf_ref)
        w1, w2v = _w(w1_ref), _w(w2_ref)
        ra = jnp.maximum(_nt(h_ref[...], w1), 0.0)
        r = jnp.square(ra).astype(BF16)
        r_ref[...] = r
        y = _nn(r, w2v)
        x2v = x2_ref[...]
        e = x2v + _rms(y, gp_ref[...]) - t_ref[...]
        loss_ref[...] += 0.5 * jnp.sum(jnp.sum(e * e, axis=-1, keepdims=True) * (1.0 / D))
        dx3 = e * (1.0 / D)
        dy, dgp = _rms_bwd(y, gp_ref[...], dx3)
        dgp_ref[...] += dgp
        dyb = dy.astype(BF16)
        dy_ref[...] = dyb
        da = (_nt(dyb, w2v) * (2.0 * ra)).astype(BF16)
        da_ref[...] = da
        dh, dgf = _rms_bwd(x2v, gf_ref[...], _nn(da, w1))
        dgf_ref[...] += dgf
        dx2_ref[...] = dx3 + dh

    return _tok_call(body, "ffn_step", [h3, x2, tgt], [w1_t, w2, g_post, g_pre],
                     [(D, F32), (DFF, BF16), (D, BF16), (DFF, BF16)], [(8, 128), (1, D), (1, D)], vmem=VMEM_BIG)


def _mem_bwd(dx2, ym, x1, qm, km, vm, w_mo, w_mq, g_post, g_pre, after=()):
    def body(dx2_ref, ym_ref, x1_ref, q_ref, k_ref, v_ref, wo_ref, wq_ref, gp_ref, gm_ref,
             dx1_ref, dym_ref, dq_ref, dk_ref, dv_ref, dgp_ref, dgm_ref, dom_ref):
        _zero_at_start(dk_ref, dv_ref, dgp_ref, dgm_ref)
        dx2_v = dx2_ref[...]
        dym, dgp = _rms_bwd(ym_ref[...], gp_ref[...], dx2_v)
        dgp_ref[...] += dgp
        dymb = dym.astype(BF16)
        dym_ref[...] = dymb
        dom_ref[...] = _nt(dymb, _w(wo_ref)).astype(BF16)
        for h in range(MEM_HEADS):
            sl = slice(h * MEM_HD, (h + 1) * MEM_HD)
            qh, kh, doh = q_ref[:, sl], k_ref[:, sl], dom_ref[:, sl]
            s = _nt(qh, kh) * MEM_SCALE
            p = jnp.exp(s - jnp.max(s, axis=-1, keepdims=True))
            p = p / jnp.sum(p, axis=-1, keepdims=True)
            dp = _nt(doh, v_ref[:, sl])
            ds = (p * (dp - jnp.sum(p * dp, axis=-1, keepdims=True))).astype(BF16)
            dq_ref[:, sl] = (_nn(ds, kh) * MEM_SCALE).astype(BF16)
            dk_ref[:, sl] += _tn(ds, qh) * MEM_SCALE
            dv_ref[:, sl] += _tn(p.astype(BF16), doh)
        dh, dgm = _rms_bwd(x1_ref[...], gm_ref[...], _nt(dq_ref[...], _w(wq_ref)))
        dgm_ref[...] += dgm
        dx1_ref[...] = dx2_v + dh

    tiled = pl.BlockSpec((TM_WIDE, D), lambda i: (i, 0))
    in_specs = [tiled] * 4 + [_resident(a) for a in (km, vm, w_mo, w_mq, g_post, g_pre)] + [ANY_SPEC] * len(after)
    w_mo, w_mq = w_mo[0], w_mq[0]
    kv = pl.BlockSpec((NMEM, D), lambda i: (0, 0))
    vec = pl.BlockSpec((1, D), lambda i: (0, 0))
    return _pcall(
        _behind(body, 10, after), name="mem_bwd", grid=(T // TM_WIDE,), in_specs=in_specs,
        out_specs=[tiled, tiled, tiled, kv, kv, vec, vec],
        out_shape=[jax.ShapeDtypeStruct((T, D), F32), jax.ShapeDtypeStruct((T, D), BF16),
                   jax.ShapeDtypeStruct((T, D), BF16), jax.ShapeDtypeStruct((NMEM, D), F32),
                   jax.ShapeDtypeStruct((NMEM, D), F32), jax.ShapeDtypeStruct((1, D), F32),
                   jax.ShapeDtypeStruct((1, D), F32)],
        scratch_shapes=[pltpu.VMEM((TM_WIDE, D), BF16)],
        compiler_params=pltpu.CompilerParams(dimension_semantics=("arbitrary",), vmem_limit_bytes=VMEM_BIG),
    )(dx2, ym, x1, qm, km, vm, w_mo, w_mq, g_post, g_pre, *after)


def _memkv_bwd(dkm, dvm, mem, w_mk, w_mv):
    def body(dk_ref, dv_ref, m_ref, wk_ref, wv_ref, dg_ref):
        dmn = _nt(dk_ref[...].astype(BF16), _w(wk_ref)) + _nt(dv_ref[...].astype(BF16), _w(wv_ref))
        mv = m_ref[...]
        dg_ref[...] = jnp.sum(dmn * (mv * _rstd(mv)), axis=0, keepdims=True)

    return _one_call(body, "memkv_bwd", [dkm, dvm, mem, w_mk, w_mv], [((1, D), F32)], vmem=VMEM_BIG)[0]


def _postmix_bwd(dx1, z, o_f, o_c, w_out, g_post, g_fo, g_co, after=()):
    def body(dx1_ref, z_ref, of_ref, oc_ref, wo_ref, gp_ref, gfo_ref, gco_ref,
             dz_ref, dof_ref, doc_ref, dgp_ref, dgfo_ref, dgco_ref):
        _zero_at_start(dgp_ref, dgfo_ref, dgco_ref)
        dz, dgp = _rms_bwd(z_ref[...], gp_ref[...], dx1_ref[...])
        dgp_ref[...] += dgp
        dzb = dz.astype(BF16)
        dz_ref[...] = dzb
        dy = _nt(dzb, _w(wo_ref))
        dof, dgfo = _rms_bwd(of_ref[...], gfo_ref[...], dy[:, :512])
        doc, dgco = _rms_bwd(oc_ref[...], gco_ref[...], dy[:, 512:])
        dof_ref[...] = dof
        doc_ref[...] = doc
        dgfo_ref[...] += dgfo
        dgco_ref[...] += dgco

    return _tok_call(body, "postmix_bwd", [dx1, z, o_f, o_c], [w_out, g_post, g_fo, g_co],
                     [(D, BF16), (512, F32), (512, F32)], [(1, D), (1, 512), (1, 512)], tm=TM_WIDE, vmem=VMEM_BIG,
                     after=after)


def _premix_bwd(dx1, x, pieces, win_t, g_pre):
    def body(dx1_ref, x_ref, *refs):
        piece_refs, (w_ref, g_ref, dx_ref, dp_ref, dg_ref) = refs[:len(pieces)], refs[len(pieces):]
        _zero_at_start(dg_ref)
        col = 0
        for p in piece_refs:
            dp_ref[:, col:col + p.shape[1]] = p[...]
            col += p.shape[1]
        dh, dg = _rms_bwd(x_ref[...], g_ref[...], _nn(dp_ref[...], w_ref[...]))
        dg_ref[...] += dg
        dx_ref[...] = dx1_ref[...] + dh

    return _tok_call(body, "premix_bwd", [dx1, x] + list(pieces), [win_t, g_pre], [(D, F32), (PROJ, BF16)], [(1, D)],
                     tm=TM_WIDE, vmem=VMEM_BIG)


def _wgrad(a, b, name):
    k, m = a.shape
    n = b.shape[1]
    tm = 640 if m % 640 == 0 and m > 1024 else min(m, 512)
    tn = min(n, 1024)

    def body(a_ref, b_ref, o_ref):
        o_ref[...] = _tn(a_ref[...].astype(BF16), b_ref[...].astype(BF16))

    return _pcall(
        body, name=name, grid=(m // tm, n // tn),
        in_specs=[pl.BlockSpec((k, tm), lambda i, j: (0, i)), pl.BlockSpec((k, tn), lambda i, j: (0, j))],
        out_specs=pl.BlockSpec((tm, tn), lambda i, j: (i, j)),
        out_shape=jax.ShapeDtypeStruct((m, n), F32),
        compiler_params=pltpu.CompilerParams(dimension_semantics=("arbitrary", "arbitrary"), vmem_limit_bytes=VMEM_BIG),
    )(a, b)


def _wgrad_group(name, pairs, rows):
    def body(*refs):
        o_ref = refs[-1]
        for k in range(len(pairs)):
            g = _tn(refs[2 * k][...].astype(BF16), refs[2 * k + 1][...].astype(BF16))
            o_ref[k * rows:(k + 1) * rows, :] = g.astype(BF16)

    in_specs, ops = [], []
    for a, b in pairs:
        in_specs += [pl.BlockSpec((a.shape[0], rows), lambda j: (0, j)), _resident(b)]
        ops += [a, b]
    return _pcall(
        body, name=name, grid=(8,), in_specs=in_specs,
        out_specs=pl.BlockSpec((None, len(pairs) * rows, D), lambda j: (j, 0, 0)),
        out_shape=jax.ShapeDtypeStruct((8, len(pairs) * rows, D), BF16),
        compiler_params=pltpu.CompilerParams(dimension_semantics=("arbitrary",), vmem_limit_bytes=VMEM_BIG),
    )(*ops)


def _adam_math(w, g, m, v):
    m2 = ADAM_B1 * m + (1.0 - ADAM_B1) * g
    v2 = ADAM_B2 * v + (1.0 - ADAM_B2) * jnp.square(g)
    m_hat = m2 / (1.0 - ADAM_B1 ** ADAM_STEP)
    v_hat = v2 / (1.0 - ADAM_B2 ** ADAM_STEP)
    delta = -ADAM_LR * (m_hat / (jnp.sqrt(v_hat) + ADAM_EPS) + ADAM_WD * w)
    return delta, m2, v2


def _adamw_small(gparts, ws, ms, vs):
    n = len(SMALL)

    def body(g_ref, *refs):
        w_refs, m_refs, v_refs = refs[:n], refs[n:2 * n], refs[2 * n:3 * n]
        outs, sum_ref = refs[3 * n:-1], refs[-1]
        g = g_ref[0]
        for k in range(1, 8):
            g = g + g_ref[k]
        sum_ref[...] = g
        outs[0][...] = sum_ref[17:18, 0:128]
        for t, name in enumerate(SMALL):
            r0, nr, c0, nc = SMALL_SLOT[name]
            gt = sum_ref[r0:r0 + nr, c0:c0 + nc]
            out = (gt,) + _adam_math(w_refs[t][...], gt, m_refs[t][...], v_refs[t][...])
            for o_ref, val in zip(outs[1 + 4 * t:5 + 4 * t], out):
                o_ref[...] = val

    whole = lambda s: pl.BlockSpec(s, lambda i, nd=len(s): (0,) * nd)
    ins = [gparts] + list(ws) + list(ms) + list(vs)
    out_shapes = [(1, 128)] + [a.shape for a in ws for _ in range(4)]
    return _pcall(
        body, name="adamw_small", grid=(1,), in_specs=[whole(a.shape) for a in ins],
        out_specs=[whole(s) for s in out_shapes], out_shape=[jax.ShapeDtypeStruct(s, F32) for s in out_shapes],
        scratch_shapes=[pltpu.VMEM((SMALL_ROWS, D), F32)],
        compiler_params=pltpu.CompilerParams(dimension_semantics=("arbitrary",)),
    )(*ins)


def _row_tile(rows):
    return next(t for t in (512, 400, 320) if rows % t == 0)


def _add_halves(g4, theirs, core, name):
    rows = g4.shape[2]
    tr = _row_tile(rows)

    def body(c_ref, a_ref, b_ref, o_ref):
        o_ref[...] = (a_ref[...].astype(F32) + b_ref[...].astype(F32)).astype(BF16)

    grid_spec = pltpu.PrefetchScalarGridSpec(
        num_scalar_prefetch=1, grid=(4, rows // tr),
        in_specs=[pl.BlockSpec((None, None, tr, D), lambda j, i, c: (j, c[0], i, 0)),
                  pl.BlockSpec((None, None, tr, D), lambda j, i, c: (j, 0, i, 0))],
        out_specs=pl.BlockSpec((None, tr, D), lambda j, i, c: (j, i, 0)))
    return _pcall(
        body, name=name, grid_spec=grid_spec, out_shape=jax.ShapeDtypeStruct((4, rows, D), BF16),
        compiler_params=pltpu.CompilerParams(dimension_semantics=("arbitrary", "arbitrary")),
    )(core, g4, theirs)


def _sum_adam(own, got, order, r0, w, m, v, name, transposed=False):
    n = w.shape[1] if transposed else w.shape[0]
    tr = min(n, 256) if n % 8 == 0 else n
    rows = tr if n % 8 == 0 else own.shape[1]

    def body(o_ref, a_ref, b_ref, c_ref, d_ref, w_ref, m_ref, v_ref, g_ref, dl_ref, m2_ref, v2_ref):
        f = lambda r: r[0:tr, :].astype(F32)
        g = ((f(a_ref) + f(b_ref)) + f(c_ref)) + f(d_ref)
        g = g.T if transposed else g
        g_ref[...] = g
        dl_ref[...], m2_ref[...], v2_ref[...] = _adam_math(w_ref[...], g, m_ref[...], v_ref[...])

    slot = lambda k: pl.BlockSpec((None, rows, D), lambda i, o: (o[k], r0 // rows + i, 0))
    wspec = pl.BlockSpec((D, tr), lambda i, o: (0, i)) if transposed else pl.BlockSpec((tr, D), lambda i, o: (i, 0))
    grid_spec = pltpu.PrefetchScalarGridSpec(
        num_scalar_prefetch=1, grid=(n // tr,), in_specs=[slot(0), slot(1), slot(2), slot(3), wspec, wspec, wspec],
        out_specs=[wspec] * 4)
    return _pcall(
        body, name=name, grid_spec=grid_spec, out_shape=[jax.ShapeDtypeStruct(w.shape, F32)] * 4,
        compiler_params=pltpu.CompilerParams(dimension_semantics=("arbitrary",)),
    )(order, own, got, got, got, w, m, v)


def _sum_adam_rows(own, got, order, ws, ms, vs, name):
    n, rows = len(ws), ws[0].shape[0]

    def body(o_ref, a_ref, b_ref, c_ref, d_ref, *refs):
        ins, outs = refs[:3 * n], refs[3 * n:]
        for t in range(n):
            r = slice(t * rows, (t + 1) * rows)
            f = lambda ref: ref[r, :].astype(F32)
            g = ((f(a_ref) + f(b_ref)) + f(c_ref)) + f(d_ref)
            out = (g,) + _adam_math(ins[t][...], g, ins[n + t][...], ins[2 * n + t][...])
            for o, val in zip(outs[4 * t:4 * t + 4], out):
                o[...] = val

    slot = lambda k: pl.BlockSpec((None, n * rows, D), lambda i, o: (o[k], 0, 0), pipeline_mode=pl.Buffered(1))
    wspec = pl.BlockSpec((rows, D), lambda i, o: (0, 0), pipeline_mode=pl.Buffered(1))
    grid_spec = pltpu.PrefetchScalarGridSpec(
        num_scalar_prefetch=1, grid=(1,), in_specs=[slot(0), slot(1), slot(2), slot(3)] + [wspec] * (3 * n),
        out_specs=[pl.BlockSpec((rows, D), lambda i, o: (0, 0))] * (4 * n))
    return _pcall(
        body, name=name, grid_spec=grid_spec, out_shape=[jax.ShapeDtypeStruct((rows, D), F32)] * (4 * n),
        compiler_params=pltpu.CompilerParams(dimension_semantics=("arbitrary",), vmem_limit_bytes=VMEM_BIG),
    )(order, own, got, got, got, *ws, *ms, *vs)


def _place():
    return lax.axis_index("x"), lax.axis_index("y"), lax.axis_index("c")


def _allgather(block, name, after=()):
    rows = block.shape[0]
    split = (rows // 2 + 15) // 16 * 16

    def body(x_ref, out_ref, token, send_sems, recv_sems, local_sem):
        token[...] = jnp.zeros_like(token)
        x, y, c = _place()
        me, sib = (x, y, c), (x, y, 1 - c)
        xn, yn, dg = (1 - x, y), (x, 1 - y), (1 - x, 1 - y)
        lo, hi = pl.ds(0, split), pl.ds(split, rows - split)

        def copy(k, blk, to, part=None, src=None):
            index = 4 * blk[0] + 2 * blk[1] + blk[2]
            view = out_ref.at[index] if part is None else out_ref.at[index, part]
            return pltpu.make_async_remote_copy(
                src_ref=view if src is None else src, dst_ref=view,
                send_sem=send_sems.at[k], recv_sem=recv_sems.at[k], device_id=to, device_id_type=MESH)

        def start(*copies):
            for cp in copies:
                cp.start()
            return list(copies)

        mine = pltpu.make_async_copy(x_ref, out_ref.at[4 * x + 2 * y + c], local_sem)
        mine.start()
        sent = start(copy(0, me, sib, src=x_ref), copy(1, me, (*xn, c), src=x_ref), copy(2, me, (*yn, c), src=x_ref))
        copy(1, (*xn, c), me).wait_recv()
        sent += start(copy(3, (*xn, c), sib), copy(5, (*xn, c), (*yn, c), part=lo))
        copy(2, (*yn, c), me).wait_recv()
        sent += start(copy(4, (*yn, c), sib), copy(6, (*yn, c), (*xn, c), part=hi))
        copy(5, (*dg, c), me, part=lo).wait_recv()
        copy(6, (*dg, c), me, part=hi).wait_recv()
        sent += start(copy(7, (*dg, c), sib))
        for k, blk in ((0, sib), (3, (*xn, 1 - c)), (4, (*yn, 1 - c)), (7, (*dg, 1 - c))):
            copy(k, blk, me).wait_recv()
        for cp in sent:
            cp.wait_send()
        mine.wait()

    return _pcall(
        _behind(body, 1, after), name=name,
        out_shape=[jax.ShapeDtypeStruct((8,) + block.shape, block.dtype), jax.ShapeDtypeStruct((8, 128), F32)],
        in_specs=[pl.BlockSpec(memory_space=pl.ANY)] * (1 + len(after)),
        out_specs=[pl.BlockSpec(memory_space=pl.ANY), pl.BlockSpec(memory_space=pltpu.VMEM)],
        scratch_shapes=[pltpu.SemaphoreType.DMA((8,)), pltpu.SemaphoreType.DMA((8,)), pltpu.SemaphoreType.DMA(())],
        compiler_params=pltpu.CompilerParams(has_side_effects=True),
    )(block, *after)


HBM_SPEC = pl.BlockSpec(memory_space=pltpu.HBM)
SEM_SPEC = pl.BlockSpec(memory_space=pltpu.SEMAPHORE)
ANY_SPEC = pl.BlockSpec(memory_space=pl.ANY)
EFFECT = pltpu.SideEffectType.DATAFLOW_SIDE_EFFECTING


def _in_hbm(a):
    return pltpu.with_memory_space_constraint(a, pltpu.HBM)


def _start_copies(name, src, land_shape, plan, n):
    def body(src_ref, land_ref, send_sems, recv_sems, src_thru, land_thru, token):
        for k, (s, d, to, _) in enumerate(plan(src_ref, land_ref)):
            pltpu.make_async_remote_copy(src_ref=s, dst_ref=d, send_sem=send_sems.at[k], recv_sem=recv_sems.at[k],
                                         device_id=to, device_id_type=MESH).start()
        token[...] = jnp.zeros_like(token)

    return _pcall(
        body, name=name,
        out_shape=(pltpu.SemaphoreType.DMA((n,)), pltpu.SemaphoreType.DMA((n,)), pltpu.HBM(src.shape, src.dtype),
                   pltpu.HBM(land_shape, src.dtype), jax.ShapeDtypeStruct((8, 128), F32)),
        in_specs=(HBM_SPEC, HBM_SPEC),
        out_specs=(SEM_SPEC, SEM_SPEC, HBM_SPEC, HBM_SPEC, pl.BlockSpec(memory_space=pltpu.VMEM)),
        input_output_aliases={0: 2, 1: 3}, compiler_params=pltpu.CompilerParams(has_side_effects=EFFECT),
    )(_in_hbm(src), _in_hbm(lax.empty(land_shape, src.dtype)))


def _wait_copies(name, started, after, plan):
    send_sems, recv_sems, src_thru, land_thru, _ = started

    def body(src_ref, land_ref, send_sems, recv_sems, *rest):
        for k, (s, _, to, mine) in enumerate(plan(src_ref, land_ref)):
            cp = pltpu.make_async_remote_copy(src_ref=s, dst_ref=mine, send_sem=send_sems.at[k],
                                              recv_sem=recv_sems.at[k], device_id=to, device_id_type=MESH)
            cp.wait_send()
            cp.wait_recv()

    return _pcall(
        body, name=name,
        out_shape=(pltpu.HBM(src_thru.shape, src_thru.dtype), pltpu.HBM(land_thru.shape, land_thru.dtype)),
        in_specs=(HBM_SPEC, HBM_SPEC, SEM_SPEC, SEM_SPEC) + (ANY_SPEC,) * len(after), out_specs=(HBM_SPEC, HBM_SPEC),
        input_output_aliases={0: 0, 1: 1}, compiler_params=pltpu.CompilerParams(has_side_effects=EFFECT),
    )(src_thru, land_thru, send_sems, recv_sems, *after)


def _start_inplace(name, buf, plan, n):
    def body(buf_ref, send_sems, recv_sems, buf_thru, token):
        for k, (s, d, to, _) in enumerate(plan(buf_ref, buf_ref)):
            pltpu.make_async_remote_copy(src_ref=s, dst_ref=d, send_sem=send_sems.at[k], recv_sem=recv_sems.at[k],
                                         device_id=to, device_id_type=MESH).start()
        token[...] = jnp.zeros_like(token)

    return _pcall(
        body, name=name,
        out_shape=(pltpu.SemaphoreType.DMA((n,)), pltpu.SemaphoreType.DMA((n,)), pltpu.HBM(buf.shape, buf.dtype),
                   jax.ShapeDtypeStruct((8, 128), F32)),
        in_specs=(HBM_SPEC,), out_specs=(SEM_SPEC, SEM_SPEC, HBM_SPEC, pl.BlockSpec(memory_space=pltpu.VMEM)),
        input_output_aliases={0: 2}, compiler_params=pltpu.CompilerParams(has_side_effects=EFFECT),
    )(_in_hbm(buf))


def _wait_inplace(name, started, after, plan):
    send_sems, recv_sems, buf_thru, _ = started

    def body(buf_ref, send_sems, recv_sems, *rest):
        for k, (s, _, to, mine) in enumerate(plan(buf_ref, buf_ref)):
            cp = pltpu.make_async_remote_copy(src_ref=s, dst_ref=mine, send_sem=send_sems.at[k],
                                              recv_sem=recv_sems.at[k], device_id=to, device_id_type=MESH)
            cp.wait_send()
            cp.wait_recv()

    return _pcall(
        body, name=name, out_shape=pltpu.HBM(buf_thru.shape, buf_thru.dtype),
        in_specs=(HBM_SPEC, SEM_SPEC, SEM_SPEC) + (ANY_SPEC,) * len(after), out_specs=HBM_SPEC,
        input_output_aliases={0: 0}, compiler_params=pltpu.CompilerParams(has_side_effects=EFFECT),
    )(buf_thru, send_sems, recv_sems, *after)


def _gather_plan(src_ref, land_ref):
    x, y, c = _place()
    peers = [(x, y, 1 - c), (1 - x, y, c), (x, 1 - y, c)]
    return [(src_ref, land_ref.at[4 * x + 2 * y + c], p, land_ref.at[4 * p[0] + 2 * p[1] + p[2]]) for p in peers]


def _relay_plan(buf_ref, _):
    x, y, c = _place()
    slot = lambda p, pc: 4 * p[0] + 2 * p[1] + pc
    xn, yn, dg, sib = (1 - x, y), (x, 1 - y), (1 - x, 1 - y), (x, y, 1 - c)
    half = buf_ref.shape[1] // 2
    lo, hi = pl.ds(0, half), pl.ds(half, half)
    return [(buf_ref.at[slot(xn, c)], buf_ref.at[slot(xn, c)], sib, buf_ref.at[slot(xn, 1 - c)]),
            (buf_ref.at[slot(yn, c)], buf_ref.at[slot(yn, c)], sib, buf_ref.at[slot(yn, 1 - c)]),
            (buf_ref.at[slot(xn, c), lo], buf_ref.at[slot(xn, c), lo], (*yn, c), buf_ref.at[slot(dg, c), lo]),
            (buf_ref.at[slot(yn, c), hi], buf_ref.at[slot(yn, c), hi], (*xn, c), buf_ref.at[slot(dg, c), hi])]


def _swap_plan(src_ref, land_ref):
    x, y, c = _place()
    return [(src_ref.at[:, pl.ds(1 - c, 1)], land_ref, (x, y, 1 - c), land_ref)]


def _exchange_plan(src_ref, land_ref):
    x, y, c = _place()
    chips = [(1 - x, y), (x, 1 - y), (1 - x, 1 - y)]
    return [(src_ref.at[2 * px + py], land_ref.at[2 * x + y], (px, py, c), land_ref.at[2 * px + py]) for px, py in chips]


def _gather_forward(land, block):
    def body(land_ref, out_ref, send_sems, recv_sems):
        x, y, c = _place()
        chips = [(1 - x, 1 - y)]

        def copy(k, px, py, pc):
            blk = out_ref.at[4 * px + 2 * py + pc]
            return pltpu.make_async_remote_copy(src_ref=blk, dst_ref=blk, send_sem=send_sems.at[k],
                                                recv_sem=recv_sems.at[k], device_id=(x, y, 1 - c), device_id_type=MESH)

        sent = [copy(k, px, py, c) for k, (px, py) in enumerate(chips)]
        for cp in sent:
            cp.start()
        for k, (px, py) in enumerate(chips):
            copy(k, px, py, 1 - c).wait_recv()
        for cp in sent:
            cp.wait_send()

    land = _pcall(
        body, name="allgather_rest_forward", out_shape=jax.ShapeDtypeStruct(land.shape, land.dtype),
        in_specs=[ANY_SPEC], out_specs=ANY_SPEC, input_output_aliases={0: 0},
        scratch_shapes=[pltpu.SemaphoreType.DMA((1,)), pltpu.SemaphoreType.DMA((1,))],
        compiler_params=pltpu.CompilerParams(has_side_effects=True),
    )(land)

    rows = block.shape[0]
    tr = rows // 4

    def place(me_ref, x_ref, land_ref, out_ref):
        out_ref[...] = x_ref[...]

    x, y, c = _place()
    grid_spec = pltpu.PrefetchScalarGridSpec(
        num_scalar_prefetch=1, grid=(rows // tr,),
        in_specs=[pl.BlockSpec((tr, D), lambda i, me: (i, 0)), ANY_SPEC],
        out_specs=pl.BlockSpec((None, tr, D), lambda i, me: (me[0], i, 0)))
    return _pcall(
        place, name="allgather_rest_own", grid_spec=grid_spec, out_shape=jax.ShapeDtypeStruct(land.shape, land.dtype),
        input_output_aliases={2: 0}, compiler_params=pltpu.CompilerParams(dimension_semantics=("arbitrary",)),
    )((4 * x + 2 * y + c).reshape(1), block, land)


class _ReduceScatter:
    def __init__(self, name, g):
        self.name = name
        rows = g.shape[1]
        self.started = _start_copies(name + "_swap_start", g.reshape(4, 2, rows, D), (4, 1, rows, D), _swap_plan, 1)
        self.token = self.started[4]

    def halfway(self, after):
        g4, theirs = _wait_copies(self.name + "_swap_wait", self.started, after, _swap_plan)
        self.own = _add_halves(g4, theirs, lax.axis_index("c").reshape(1), self.name + "_add_halves")
        self.started = _start_copies(self.name + "_exch_start", self.own, self.own.shape, _exchange_plan, 3)
        self.token = self.started[4]

    def finish(self, after):
        own, got = _wait_copies(self.name + "_exch_wait", self.started, after, _exchange_plan)
        chip = 2 * lax.axis_index("x") + lax.axis_index("y")
        return own, got, (chip + jnp.arange(4, dtype=jnp.int32)) % 4


def _pack_small(p, loss):
    def body(*refs):
        o_ref = refs[-1]
        o_ref[...] = jnp.zeros_like(o_ref)
        for ref, name in zip(refs, SMALL):
            r0, nr, c0, nc = SMALL_SLOT[name]
            o_ref[r0:r0 + nr, c0:c0 + nc] = ref[...]
        o_ref[17:18, 0:128] = refs[len(SMALL)][0:1, :]

    return _one_call(body, "pack_small_grads", [p[n] for n in SMALL] + [loss], [((SMALL_ROWS, D), F32)])[0]


_GAP_DEV, _GAP_ROW = divmod(GATE0 + 8, N_IN)
_GAP = CHK0 - GATE0 - 8


def _in_rows_to_proj(g):
    runs = [(j, 0, N_IN, N_IN * j) for j in range(_GAP_DEV)]
    runs += [(_GAP_DEV, 0, _GAP_ROW, N_IN * _GAP_DEV), (_GAP_DEV, _GAP_ROW, N_IN, N_IN * _GAP_DEV + _GAP_ROW + _GAP)]
    runs += [(j, 0, N_IN, N_IN * j + _GAP) for j in range(_GAP_DEV + 1, 8)]

    def body(g_ref, o_ref, acc_ref):
        acc_ref[...] = jnp.zeros_like(acc_ref)
        for j, r0, r1, dest in runs:
            start, shift = dest // 16 * 16, dest % 16
            win = -(-(shift + r1 - r0) // 16) * 16
            r = lax.broadcasted_iota(jnp.int32, (win, R_IN), 0)
            c = lax.broadcasted_iota(jnp.int32, (win, R_IN), 1)
            move = jnp.where((c >= r0) & (c < r1) & (r == c - r0 + shift), 1.0, 0.0).astype(BF16)
            acc_ref[start:start + win, :] += _nn(move, g_ref[j])
        o_ref[...] = acc_ref[...].astype(BF16)

    return _pcall(
        body, name="w_in_layout", out_shape=jax.ShapeDtypeStruct((PROJ, D), BF16), grid=(1,),
        in_specs=[pl.BlockSpec(g.shape, lambda i: (0, 0, 0))], out_specs=pl.BlockSpec((PROJ, D), lambda i: (0, 0)),
        scratch_shapes=[pltpu.VMEM((PROJ, D), F32)],
        compiler_params=pltpu.CompilerParams(dimension_semantics=("arbitrary",), vmem_limit_bytes=VMEM_BIG),
    )(g)


def _proj_rows_to_in(g):
    pad = lambda a: jnp.pad(a, ((0, R_IN - a.shape[0]), (0, 0)))
    lo = N_IN * _GAP_DEV
    shards = [pad(g[N_IN * j:N_IN * (j + 1)]) for j in range(_GAP_DEV)]
    shards.append(pad(jnp.concatenate([g[lo:lo + _GAP_ROW], g[lo + _GAP_ROW + _GAP:lo + N_IN + _GAP]], axis=0)))
    shards += [pad(g[N_IN * j + _GAP:N_IN * (j + 1) + _GAP]) for j in range(_GAP_DEV + 1, 8)]
    return jnp.stack(shards)


def _local_grads(x, mem, tgt, win_t, gw_of, sm, on_grads, after=()):
    b_pad = jnp.pad(sm['b_fgt'], ((0, 0), (0, 120)))
    tbl = jnp.pad(sm['rel_bias'], ((0, 0), (0, NREL_PAD - 257)))

    h1, proj, flog = _premix_fwd(x, sm['g_mix_pre'], win_t, after)
    c = _gate_fwd(flog, b_pad)
    ct3 = c[:, :8].T.reshape(4, 2, T)
    o_f, lse_f = _fox_fwd(proj, c, ct3)
    vt3 = _relvec_fwd(tbl).reshape(4, 2, VW)
    kvp = jnp.pad(proj[:, CHK0 + 512:], ((LEFT, 0), (0, 0)))
    o_c, lse_c = _chk_fwd(proj, kvp, vt3, [gw_of('relay', [o_f])])
    gw = gw_of('done', [o_c])
    w_out, w_mq, w_mk, w_mv, w_mo, w1_t, w2 = (_wblk(gw, n) for n in ('w_out', 'w_mq', 'w_mk', 'w_mv', 'w_mo', 'w_ff1', 'w_ff2'))
    ycat, z, x1, h2, qm = _postmix_fwd(x, o_f, o_c, sm['g_fox_out'], sm['g_chk_out'], w_out,
                                       sm['g_mix_post'], sm['g_mem_pre'], w_mq)
    memn, km, vm = _memkv_fwd(mem, sm['g_mem_kv'], w_mk, w_mv)
    om, ym, x2, h3 = _mem_fwd(qm, x1, km, vm, w_mo, sm['g_mem_post'], sm['g_ff_pre'])

    gs = {}
    dx2, da, dy3, r, loss_acc, gs['g_ff_post'], gs['g_ff_pre'] = _ffn_step(h3, x2, tgt, w1_t, w2, sm['g_ff_post'],
                                                                         sm['g_ff_pre'])
    tok = on_grads('A', _wgrad_group("wgrad_ff", [(da, h3), (r, dy3)], 512), None)
    dx1, dym, dqm, dkm, dvm, gs['g_mem_post'], gs['g_mem_pre'] = _mem_bwd(
        dx2, ym, x1, qm, km, vm, w_mo, w_mq, sm['g_mem_post'], sm['g_mem_pre'], [tok])
    tok = on_grads('A halfway', None, [dx1])
    gs['g_mem_kv'] = _memkv_bwd(dkm, dvm, mem, w_mk, w_mv)
    dz, dof, doc, gs['g_mix_post'], gs['g_fox_out'], gs['g_chk_out'] = _postmix_bwd(
        dx1, z, o_f, o_c, w_out, sm['g_mix_post'], sm['g_fox_out'], sm['g_chk_out'], [tok])
    tok = on_grads('B', _wgrad_group("wgrad_mem_out", [(ycat, dz), (h2, dqm), (memn, dkm), (memn, dvm), (om, dym)], 128), None)
    dq_f, dk_f, dv_f, dct, dcq = _fox_bwd(proj, c, ct3, o_f, lse_f, dof, [tok])
    tok = on_grads('B halfway', None, [dq_f])
    dq_c, dk_c, dv_c, gv = _chk_bwd(proj, kvp, vt3, o_c, lse_c, doc, [tok])
    gs['rel_bias'] = _relvec_bwd(gv.reshape(8, VW))[:, :257]
    dc = jnp.pad(dct.reshape(8, T).T + dcq[:, :, :2].transpose(1, 0, 2).reshape(T, 8), ((0, 0), (0, 120)))
    dflog, db = _gate_bwd(dc, flog, b_pad)
    gs['b_fgt'] = db[0:1, :8]
    grad_x, dproj, gs['g_mix_pre'] = _premix_bwd(dx1, x, [dq_f, dk_f, dv_f, dflog, dq_c, dk_c, dv_c], win_t, sm['g_mix_pre'])
    on_grads('C', _proj_rows_to_in(_wgrad(dproj, h1, "wgrad_in")), None)
    return loss_acc, grad_x, gs


def kernel(x, mem, w_in, b_fgt, rel_bias, g_fox_out, g_chk_out, w_out, g_mix_pre, g_mix_post, g_mem_kv, w_mq, w_mk, w_mv, w_mo, g_mem_pre, g_mem_post, w_ff1, w_ff2, g_ff_pre, g_ff_post, loss_target, m_w_in, m_b_fgt, m_rel_bias, m_g_fox_out, m_g_chk_out, m_w_out, m_g_mix_pre, m_g_mix_post, m_g_mem_kv, m_w_mq, m_w_mk, m_w_mv, m_w_mo, m_g_mem_pre, m_g_mem_post, m_w_ff1, m_w_ff2, m_g_ff_pre, m_g_ff_post, v_w_in, v_b_fgt, v_rel_bias, v_g_fox_out, v_g_chk_out, v_w_out, v_g_mix_pre, v_g_mix_post, v_g_mem_kv, v_w_mq, v_w_mk, v_w_mv, v_w_mo, v_g_mem_pre, v_g_mem_post, v_w_ff1, v_w_ff2, v_g_ff_pre, v_g_ff_post):
    args = dict(locals())
    two_d = lambda a: a.reshape(a.shape[-2:])
    w = {n: two_d(args[n]) for n in WEIGHTS}
    m = {n: two_d(args['m_' + n]) for n in WEIGHTS}
    v = {n: two_d(args['v_' + n]) for n in WEIGHTS}

    sm = {n: w[n] for n in SMALL}
    shard_in = jnp.pad(w['w_in'].T, ((0, R_IN - N_IN), (0, 0))).astype(BF16)
    gathered_in, zero = _allgather(shard_in, "allgather_w_in")
    win_t = _in_rows_to_proj(gathered_in)
    shard_rest = (jnp.concatenate([w['w_ff1'].T, w['w_ff2'], w['w_out'], w['w_mq'], w['w_mk'], w['w_mv'], w['w_mo']],
                                  axis=0) + zero[0, 0]).astype(BF16)
    gather = {'first': _start_copies("allgather_rest_start", shard_rest, (8, R_REST, D), _gather_plan, 3)}

    def gw_of(stage, after):
        if stage == 'relay':
            gather['block'], land = _wait_copies("allgather_rest_wait", gather['first'], after, _gather_plan)
            gather['second'] = _start_inplace("allgather_rest_relay_start", land, _relay_plan, 4)
            return gather['second'][3]
        land = _wait_inplace("allgather_rest_relay_wait", gather['second'], after, _relay_plan)
        return _gather_forward(land, gather['block'])

    rs = {}

    def on_grads(stage, g, after):
        if stage.endswith('halfway'):
            rs[stage[0]].halfway(after)
            return rs[stage[0]].token
        rs[stage] = _ReduceScatter("rs_" + stage.lower(), g)
        return rs[stage].token

    loss_local, grad_x, gs = _local_grads(x[0], mem[0], loss_target[0], win_t, gw_of, sm, on_grads, [gather['first'][4]])
    grads, deltas, new_m, new_v = {}, {}, {}, {}

    def update(n, out):
        grads[n], deltas[n], new_m[n], new_v[n] = out

    gparts, _ = _allgather(_pack_small(gs, loss_local), "allgather_small_grads", [rs['C'].token])
    small = _adamw_small(gparts, [w[n] for n in SMALL], [m[n] for n in SMALL], [v[n] for n in SMALL])
    loss = small[0][0, 0]
    for t, n in enumerate(SMALL):
        update(n, small[1 + 4 * t:5 + 4 * t])
    rs['C'].halfway([small[0]])

    own, got, order = rs['A'].finish([grad_x, rs['C'].started[4]])
    update('w_ff1', _sum_adam(own, got, order, 0, w['w_ff1'], m['w_ff1'], v['w_ff1'], "adamw_w_ff1", transposed=True))
    update('w_ff2', _sum_adam(own, got, order, 512, w['w_ff2'], m['w_ff2'], v['w_ff2'], "adamw_w_ff2"))
    own, got, order = rs['B'].finish([grad_x, rs['C'].started[4]])
    names_b = ('w_out', 'w_mq', 'w_mk', 'w_mv', 'w_mo')
    done = _sum_adam_rows(own, got, order, [w[n] for n in names_b], [m[n] for n in names_b], [v[n] for n in names_b],
                          "adamw_group_b")
    for k, n in enumerate(names_b):
        update(n, done[4 * k:4 * k + 4])

    own, got, order = rs['C'].finish([new_v[n] for n in BIG if n != 'w_in'])
    done = _sum_adam(own, got, order, 0, w['w_in'].T, m['w_in'].T, v['w_in'].T, "adamw_w_in")
    update('w_in', [a.T for a in done])

    out = [loss, grad_x[None]]
    for group in (grads, deltas, new_m, new_v):
        out += [group[n].reshape(args[n].shape) for n in WEIGHTS]
    return tuple(out)
```

```python
import jax
import jax.numpy as jnp
from jax import lax
from jax.experimental import pallas as pl
from jax.experimental.pallas import tpu as pltpu

F32 = jnp.float32
BF16 = jnp.bfloat16
MESH = pl.DeviceIdType.MESH

T = 2048
D = 1024
NMEM = 256
DFF = 4096
EPS = 1e-6
TM = 256
TM_WIDE = 512
TQ = 256
FQ = 512
HD = 64
SCALE = HD ** -0.5
MEM_HEADS = 4
MEM_HD = 256
MEM_SCALE = MEM_HD ** -0.5
NEG = -1e30
LEFT = 512
WIN = LEFT + TQ
VW = 1024
NREL_PAD = 384
PROJ = 3200
GATE0 = 1536
CHK0 = 1664
VMEM_BIG = 56 * 1024 * 1024

ADAM_LR = 0.001
ADAM_B1 = 0.9
ADAM_B2 = 0.999
ADAM_EPS = 1e-08
ADAM_WD = 0.01
ADAM_STEP = 10

N_IN = 385
R_IN = 400
R_REST = 1664
W_ROWS = {'w_ff1': (0, 512), 'w_ff2': (512, 512),
          'w_out': (1024, 128), 'w_mq': (1152, 128), 'w_mk': (1280, 128), 'w_mv': (1408, 128), 'w_mo': (1536, 128)}
SMALL_ROWS = 24
SMALL_SLOT = {'rel_bias': (0, 8, 0, 257), 'b_fgt': (8, 1, 0, 8), 'g_fox_out': (9, 1, 0, 512), 'g_chk_out': (9, 1, 512, 512),
              'g_mix_pre': (10, 1, 0, 1024), 'g_mix_post': (11, 1, 0, 1024), 'g_mem_kv': (12, 1, 0, 1024),
              'g_mem_pre': (13, 1, 0, 1024), 'g_mem_post': (14, 1, 0, 1024), 'g_ff_pre': (15, 1, 0, 1024),
              'g_ff_post': (16, 1, 0, 1024)}

WEIGHTS = ['w_in', 'b_fgt', 'rel_bias', 'g_fox_out', 'g_chk_out', 'w_out', 'g_mix_pre', 'g_mix_post', 'g_mem_kv',
           'w_mq', 'w_mk', 'w_mv', 'w_mo', 'g_mem_pre', 'g_mem_post', 'w_ff1', 'w_ff2', 'g_ff_pre', 'g_ff_post']
BIG = ['w_in', 'w_out', 'w_mq', 'w_mk', 'w_mv', 'w_mo', 'w_ff1', 'w_ff2']
SMALL = [n for n in WEIGHTS if n not in BIG]


def _pcall(body, **kw):
    return pl.pallas_call(body, **kw)


def _nn(a, b):
    return jnp.dot(a, b, preferred_element_type=F32)


def _nt(a, b):
    return lax.dot_general(a, b, (((1,), (1,)), ((), ())), preferred_element_type=F32)


def _tn(a, b):
    return lax.dot_general(a, b, (((0,), (0,)), ((), ())), preferred_element_type=F32)


def _w(ref):
    v = ref[...]
    return v if v.ndim == 2 else v.reshape(-1, v.shape[-1])


def _rstd(x):
    return lax.rsqrt(jnp.mean(x * x, axis=-1, keepdims=True) + EPS)


def _rms(x, g):
    return x * _rstd(x) * g


def _rms_bwd(x, g, dy):
    r = _rstd(x)
    xh = x * r
    dg = jnp.sum(dy * xh, axis=0, keepdims=True)
    dxh = dy * g
    dx = r * (dxh - xh * jnp.mean(dxh * xh, axis=-1, keepdims=True))
    return dx, dg


def _resident(a):
    if isinstance(a, tuple):
        _, shape, index = a
        return pl.BlockSpec(shape, lambda *_: index, pipeline_mode=pl.Buffered(1))
    return pl.BlockSpec(a.shape, lambda *_, nd=a.ndim: (0,) * nd, pipeline_mode=pl.Buffered(1))


def _wblk(gw, name):
    r0, rows = W_ROWS[name]
    return (gw, (8, rows, D), (0, r0 // rows, 0))


def _behind(body, n_in, after):
    if not after:
        return body
    return lambda *refs: body(*refs[:n_in], *refs[n_in + len(after):])


def _tok_call(body, name, tiled, full, outs_tiled, outs_acc=(), rows=T, tm=TM, vmem=None, after=()):
    in_specs = [pl.BlockSpec((tm, a.shape[1]), lambda i: (i, 0)) for a in tiled]
    in_specs += [_resident(a) for a in full] + [ANY_SPEC] * len(after)
    full = [a[0] if isinstance(a, tuple) else a for a in full] + list(after)
    body = _behind(body, len(tiled) + len(full) - len(after), after)
    out_shape = [jax.ShapeDtypeStruct((rows, c), dt) for c, dt in outs_tiled]
    out_shape += [jax.ShapeDtypeStruct(s, F32) for s in outs_acc]
    out_specs = [pl.BlockSpec((tm, c), lambda i: (i, 0)) for c, _ in outs_tiled]
    out_specs += [pl.BlockSpec(s, lambda i, nd=len(s): (0,) * nd) for s in outs_acc]
    return _pcall(
        body, name=name, grid=(rows // tm,), in_specs=in_specs, out_specs=out_specs, out_shape=out_shape,
        compiler_params=pltpu.CompilerParams(dimension_semantics=("arbitrary",), vmem_limit_bytes=vmem),
    )(*tiled, *full)


def _one_call(body, name, ins, outs, vmem=None):
    whole = lambda s: pl.BlockSpec(s, lambda i, nd=len(s): (0,) * nd)
    return _pcall(
        body, name=name, grid=(1,), in_specs=[_resident(a) for a in ins], out_specs=[whole(s) for s, _ in outs],
        out_shape=[jax.ShapeDtypeStruct(s, dt) for s, dt in outs],
        compiler_params=pltpu.CompilerParams(dimension_semantics=("arbitrary",), vmem_limit_bytes=vmem),
    )(*[a[0] if isinstance(a, tuple) else a for a in ins])


def _premix_fwd(x, g_pre, win_t, after=()):
    def body(x_ref, g_ref, w_ref, h_ref, proj_ref, flog_ref):
        h = _rms(x_ref[...], g_ref[...]).astype(BF16)
        h_ref[...] = h
        p = _nt(h, w_ref[...])
        proj_ref[...] = p.astype(BF16)
        flog_ref[...] = p[:, GATE0:GATE0 + 128]

    return _tok_call(body, "premix_fwd", [x], [g_pre, win_t],
                     [(D, BF16), (PROJ, BF16), (128, F32)], tm=TM_WIDE, vmem=VMEM_BIG, after=after)


def _postmix_fwd(x, o_f, o_c, g_fo, g_co, w_out, g_post, g_mpre, w_mq):
    def body(x_ref, of_ref, oc_ref, gfo_ref, gco_ref, wo_ref, gp_ref, gm_ref, wq_ref,
             y_ref, z_ref, x1_ref, h2_ref, qm_ref):
        y_ref[:, :512] = _rms(of_ref[...], gfo_ref[...]).astype(BF16)
        y_ref[:, 512:] = _rms(oc_ref[...], gco_ref[...]).astype(BF16)
        z = _nn(y_ref[...], _w(wo_ref))
        z_ref[...] = z
        x1 = x_ref[...] + _rms(z, gp_ref[...])
        x1_ref[...] = x1
        h2 = _rms(x1, gm_ref[...]).astype(BF16)
        h2_ref[...] = h2
        qm_ref[...] = _nn(h2, _w(wq_ref)).astype(BF16)

    return _tok_call(body, "postmix_fwd", [x, o_f, o_c], [g_fo, g_co, w_out, g_post, g_mpre, w_mq],
                     [(D, BF16), (D, F32), (D, F32), (D, BF16), (D, BF16)], tm=TM_WIDE, vmem=VMEM_BIG)


def _memkv_fwd(mem, g_kv, w_mk, w_mv):
    def body(m_ref, g_ref, wk_ref, wv_ref, mn_ref, k_ref, v_ref):
        mn = _rms(m_ref[...], g_ref[...]).astype(BF16)
        mn_ref[...] = mn
        k_ref[...] = _nn(mn, _w(wk_ref)).astype(BF16)
        v_ref[...] = _nn(mn, _w(wv_ref)).astype(BF16)

    return _tok_call(body, "memkv_fwd", [mem], [g_kv, w_mk, w_mv],
                     [(D, BF16), (D, BF16), (D, BF16)], rows=NMEM, tm=NMEM, vmem=VMEM_BIG)


def _mem_fwd(qm, x1, km, vm, w_mo, g_post, g_fpre):
    def body(q_ref, x1_ref, k_ref, v_ref, wo_ref, gp_ref, gf_ref, om_ref, ym_ref, x2_ref, h3_ref):
        for h in range(MEM_HEADS):
            sl = slice(h * MEM_HD, (h + 1) * MEM_HD)
            s = _nt(q_ref[:, sl], k_ref[:, sl]) * MEM_SCALE
            p = jnp.exp(s - jnp.max(s, axis=-1, keepdims=True))
            p = p / jnp.sum(p, axis=-1, keepdims=True)
            om_ref[:, sl] = _nn(p.astype(BF16), v_ref[:, sl]).astype(BF16)
        ym = _nn(om_ref[...], _w(wo_ref))
        ym_ref[...] = ym
        x2 = x1_ref[...] + _rms(ym, gp_ref[...])
        x2_ref[...] = x2
        h3_ref[...] = _rms(x2, gf_ref[...]).astype(BF16)

    return _tok_call(body, "mem_fwd", [qm, x1], [km, vm, w_mo, g_post, g_fpre],
                     [(D, BF16), (D, F32), (D, F32), (D, BF16)], tm=TM_WIDE, vmem=VMEM_BIG)


def _tri(lower):
    r = lax.broadcasted_iota(jnp.int32, (128, 128), 0)
    c = lax.broadcasted_iota(jnp.int32, (128, 128), 1)
    return jnp.where(r >= c if lower else c >= r, 1.0, 0.0).astype(F32)


def _hdot(a, b):
    return jnp.dot(a, b, preferred_element_type=F32, precision=lax.Precision.HIGHEST)


def _gate_fwd(flog, b_pad):
    def body(f_ref, b_ref, c_ref):
        tri = _tri(True)

        def step(i, carry):
            rows = pl.ds(pl.multiple_of(i * 128, 128), 128)
            z = f_ref[rows, :] + b_ref[...]
            lf = jnp.minimum(z, 0.0) - jnp.log(1.0 + jnp.exp(-jnp.abs(z)))
            cb = _hdot(tri, lf) + carry
            c_ref[rows, :] = cb
            return cb[127:128, :]

        lax.fori_loop(0, T // 128, step, jnp.zeros((1, 128), F32))

    return _one_call(body, "gate_fwd", [flog, b_pad], [((T, 128), F32)])[0]


def _gate_bwd(dc, flog, b_pad):
    def body(dc_ref, f_ref, b_ref, df_ref, db_ref):
        tri = _tri(False)

        def step(j, carry):
            run, db = carry
            i = T // 128 - 1 - j
            rows = pl.ds(pl.multiple_of(i * 128, 128), 128)
            dcb = dc_ref[rows, :]
            rb = _hdot(tri, dcb) + run
            z = f_ref[rows, :] + b_ref[...]
            df = rb * (1.0 / (1.0 + jnp.exp(z)))
            df_ref[rows, :] = df.astype(BF16)
            return run + jnp.sum(dcb, axis=0, keepdims=True), db + jnp.sum(df, axis=0, keepdims=True)

        _, db = lax.fori_loop(0, T // 128, step, (jnp.zeros((1, 128), F32), jnp.zeros((1, 128), F32)))
        db_ref[...] = jnp.broadcast_to(db, (8, 128))

    return _one_call(body, "gate_bwd", [dc, flog, b_pad], [((T, 128), BF16), ((8, 128), F32)])


def _lane_lo(rows=TQ):
    return lax.broadcasted_iota(jnp.int32, (rows, 128), 1) < HD


def _half(v, lo, a, scale=None):
    keep = lo if a == 0 else jnp.logical_not(lo)
    v = v.astype(F32) if scale is None else v.astype(F32) * scale
    return jnp.where(keep, v, 0.0).astype(BF16)


def _fox_specs():
    return [pl.BlockSpec((FQ, 128), lambda h, i: (i, h)),
            pl.BlockSpec((T, 128), lambda h, i: (0, 4 + h)),
            pl.BlockSpec((T, 128), lambda h, i: (0, 8 + h))]


def _lane_pick(x, at):
    lane = lax.broadcasted_iota(jnp.int32, x.shape, 1)
    return jnp.sum(jnp.where(lane == at, x, 0.0), axis=-1, keepdims=True)


def _fox_fwd(proj, c, ct3):
    def body(q_ref, k_ref, v_ref, c_ref, ct_ref, o_ref, l_ref):
        i = pl.program_id(1)
        lo = _lane_lo(FQ)
        causal = lax.broadcasted_iota(jnp.int32, (FQ, FQ), 1) <= lax.broadcasted_iota(jnp.int32, (FQ, FQ), 0)
        q = q_ref[...]
        qs = [_half(q, lo, a, SCALE) for a in range(2)]
        cqs = [_lane_pick(c_ref[...], 2 * pl.program_id(0) + a) for a in range(2)]

        def tile(off, carry, diagonal):
            kblk = k_ref[pl.ds(off, FQ), :]
            vblk = v_ref[pl.ds(off, FQ), :]
            new = []
            for a in range(2):
                m, l, acc = carry[a]
                s = _nt(qs[a], kblk) + (cqs[a] - ct_ref[a:a + 1, pl.ds(off, FQ)])
                if diagonal:
                    s = jnp.where(causal, s, NEG)
                m2 = jnp.maximum(m, jnp.max(s, axis=-1, keepdims=True))
                p = jnp.exp(s - m2)
                alpha = jnp.exp(m - m2)
                new.append((m2, alpha * l + jnp.sum(p, axis=-1, keepdims=True),
                            alpha * acc + _nn(p.astype(BF16), vblk)))
            return tuple(new)

        init = (jnp.full((FQ, 1), NEG, F32), jnp.zeros((FQ, 1), F32), jnp.zeros((FQ, 128), F32))
        carry = lax.fori_loop(0, i, lambda kb, c: tile(pl.multiple_of(kb * FQ, FQ), c, False), (init, init))
        carry = tile(pl.multiple_of(i * FQ, FQ), carry, True)
        outs = []
        for a in range(2):
            m, l, acc = carry[a]
            outs.append(acc / l)
            l_ref[:, 128 * a:128 * a + 128] = jnp.broadcast_to(m + jnp.log(l), (FQ, 128))
        o_ref[...] = jnp.where(lo, outs[0], outs[1])

    return _pcall(
        body, name="fox_fwd", grid=(4, T // FQ),
        in_specs=_fox_specs() + [pl.BlockSpec((FQ, 128), lambda h, i: (i, 0)),
                                 pl.BlockSpec((None, 2, T), lambda h, i: (h, 0, 0))],
        out_specs=[pl.BlockSpec((FQ, 128), lambda h, i: (i, h)), pl.BlockSpec((FQ, 256), lambda h, i: (i, h))],
        out_shape=[jax.ShapeDtypeStruct((T, 512), F32), jax.ShapeDtypeStruct((T, 1024), F32)],
        compiler_params=pltpu.CompilerParams(dimension_semantics=("arbitrary", "arbitrary"), vmem_limit_bytes=VMEM_BIG),
    )(proj, proj, proj, c, ct3)


def _fox_bwd(proj, c, ct3, o, lse, do, after=()):
    def body(q_ref, k_ref, v_ref, c_ref, ct_ref, o_ref, l_ref, do_ref, dq_ref, dkb_ref, dvb_ref, dct_ref, dcq_ref,
             dk_ref, dv_ref):
        i = pl.program_id(1)

        @pl.when(i == 0)
        def _():
            dk_ref[...] = jnp.zeros_like(dk_ref)
            dv_ref[...] = jnp.zeros_like(dv_ref)
            dct_ref[...] = jnp.zeros_like(dct_ref)

        lo = _lane_lo(FQ)
        causal = lax.broadcasted_iota(jnp.int32, (FQ, FQ), 1) <= lax.broadcasted_iota(jnp.int32, (FQ, FQ), 0)
        q = q_ref[...]
        do_v = do_ref[...]
        prod = do_v * o_ref[...]
        qs = [_half(q, lo, a, SCALE) for a in range(2)]
        dos = [_half(do_v, lo, a) for a in range(2)]
        deltas = [jnp.sum(jnp.where(lo if a == 0 else jnp.logical_not(lo), prod, 0.0), axis=-1, keepdims=True)
                  for a in range(2)]
        cqs = [_lane_pick(c_ref[...], 2 * pl.program_id(0) + a) for a in range(2)]
        las = [l_ref[:, 128 * a:128 * a + 1] for a in range(2)]

        def tile(off, carry, diagonal):
            kblk = k_ref[pl.ds(off, FQ), :]
            vblk = v_ref[pl.ds(off, FQ), :]
            new = []
            dk = jnp.zeros((128, FQ), F32)
            dv = jnp.zeros((128, FQ), F32)
            for a in range(2):
                dq_acc, rs = carry[a]
                s = _nt(qs[a], kblk) + (cqs[a] - ct_ref[a:a + 1, pl.ds(off, FQ)])
                if diagonal:
                    s = jnp.where(causal, s, NEG)
                p = jnp.exp(s - las[a])
                ds = p * (_nt(dos[a], vblk) - deltas[a])
                dsb = ds.astype(BF16)
                dk = dk + _tn(qs[a], dsb)
                dv = dv + _tn(dos[a], p.astype(BF16))
                dct_ref[a:a + 1, pl.ds(off, FQ)] -= jnp.sum(ds, axis=0, keepdims=True)
                new.append((dq_acc + _nn(dsb, kblk), rs + jnp.sum(ds, axis=-1, keepdims=True)))
            dk_ref[:, pl.ds(off, FQ)] += dk
            dv_ref[:, pl.ds(off, FQ)] += dv
            return tuple(new)

        init = (jnp.zeros((FQ, 128), F32), jnp.zeros((FQ, 1), F32))
        carry = lax.fori_loop(0, i, lambda kb, c: tile(pl.multiple_of(kb * FQ, FQ), c, False), (init, init))
        carry = tile(pl.multiple_of(i * FQ, FQ), carry, True)
        lane = lax.broadcasted_iota(jnp.int32, (FQ, 128), 1)
        dcq_ref[...] = jnp.where(lane == 0, carry[0][1], jnp.where(lane == 1, carry[1][1], 0.0))
        dq_ref[...] = (jnp.where(lo, carry[0][0], carry[1][0]) * SCALE).astype(BF16)

        @pl.when(i == T // FQ - 1)
        def _():
            dkb_ref[...] = dk_ref[...].T.astype(BF16)
            dvb_ref[...] = dv_ref[...].T.astype(BF16)

    blk = pl.BlockSpec((FQ, 128), lambda h, i: (i, h))
    wide = pl.BlockSpec((FQ, 256), lambda h, i: (i, h))
    rows = pl.BlockSpec((None, 2, T), lambda h, i: (h, 0, 0))
    col = pl.BlockSpec((T, 128), lambda h, i: (0, h))
    return _pcall(
        _behind(body, 8, after), name="fox_bwd", grid=(4, T // FQ),
        in_specs=_fox_specs() + [pl.BlockSpec((FQ, 128), lambda h, i: (i, 0)), rows, blk, wide, blk] + [ANY_SPEC] * len(after),
        out_specs=[blk, col, col, rows, pl.BlockSpec((None, FQ, 128), lambda h, i: (h, i, 0))],
        out_shape=[jax.ShapeDtypeStruct((T, 512), BF16), jax.ShapeDtypeStruct((T, 512), BF16),
                   jax.ShapeDtypeStruct((T, 512), BF16), jax.ShapeDtypeStruct((4, 2, T), F32),
                   jax.ShapeDtypeStruct((4, T, 128), F32)],
        scratch_shapes=[pltpu.VMEM((128, T), F32), pltpu.VMEM((128, T), F32)],
        compiler_params=pltpu.CompilerParams(dimension_semantics=("arbitrary", "arbitrary"), vmem_limit_bytes=VMEM_BIG),
    )(proj, proj, proj, c, ct3, o, lse, do, *after)


def _rel_onehot():
    ridx = lax.broadcasted_iota(jnp.int32, (NREL_PAD, VW), 0)
    j = lax.broadcasted_iota(jnp.int32, (NREL_PAD, VW), 1)
    return jnp.where(ridx == jnp.clip(TQ + LEFT - 1 - j, -128, 128) + 128, 1.0, 0.0).astype(F32)


def _relvec_fwd(tbl):
    def body(t_ref, v_ref):
        v_ref[...] = _hdot(t_ref[...], _rel_onehot())

    return _one_call(body, "relvec_fwd", [tbl], [((8, VW), F32)])[0]


def _relvec_bwd(gv):
    def body(g_ref, t_ref):
        t_ref[...] = lax.dot_general(g_ref[...], _rel_onehot(), (((1,), (1,)), ((), ())),
                                     preferred_element_type=F32, precision=lax.Precision.HIGHEST)

    return _one_call(body, "relvec_bwd", [gv], [((8, NREL_PAD), F32)])[0]


def _chk_bias(vt_ref, a, hidden):
    vb = jnp.broadcast_to(vt_ref[a:a + 1, :], (TQ, VW))
    y = pltpu.roll(vb, VW - (TQ - 1), 1, stride=1, stride_axis=0)[:, :WIN]
    cr = lax.broadcasted_iota(jnp.int32, (TQ, WIN), 0) // 64
    m = lax.broadcasted_iota(jnp.int32, (TQ, WIN), 1)
    return jnp.where((m // 64 >= cr) & (m // 64 <= cr + 8) & (m >= hidden), y, NEG)


def _chk_specs():
    return [pl.BlockSpec((TQ, 128), lambda h, i: (i, CHK0 // 128 + h)),
            pl.BlockSpec((T + LEFT, 128), lambda h, i: (0, h)),
            pl.BlockSpec((T + LEFT, 128), lambda h, i: (0, 4 + h)),
            pl.BlockSpec((None, 2, VW), lambda h, i: (h, 0, 0))]


def _chk_fwd(proj, kvp, vt3, after=()):
    def body(q_ref, k_ref, v_ref, vt_ref, o_ref, l_ref, bias_ref):
        i = pl.program_id(1)

        @pl.when(i == 0)
        def _():
            for first in range(3):
                for a in range(2):
                    bias_ref[first, a] = _chk_bias(vt_ref, a, max(LEFT - first * TQ, 0))

        lo = _lane_lo()
        off = pl.multiple_of(i * TQ, TQ)
        kw = k_ref[pl.ds(off, WIN), :]
        vw = v_ref[pl.ds(off, WIN), :]
        bias_at = jnp.minimum(i, 2)
        q = q_ref[...]
        outs = []
        for a in range(2):
            s = _nt(_half(q, lo, a, SCALE), kw) + bias_ref[bias_at, a]
            m = jnp.max(s, axis=-1, keepdims=True)
            p = jnp.exp(s - m)
            l = jnp.sum(p, axis=-1, keepdims=True)
            outs.append(_nn(p.astype(BF16), vw) / l)
            l_ref[:, 128 * a:128 * a + 128] = jnp.broadcast_to(m + jnp.log(l), (TQ, 128))
        o_ref[...] = jnp.where(lo, outs[0], outs[1])

    return _pcall(
        _behind(body, 4, after), name="chk_fwd", grid=(4, T // TQ), in_specs=_chk_specs() + [ANY_SPEC] * len(after),
        out_specs=[pl.BlockSpec((TQ, 128), lambda h, i: (i, h)), pl.BlockSpec((TQ, 256), lambda h, i: (i, h))],
        out_shape=[jax.ShapeDtypeStruct((T, 512), F32), jax.ShapeDtypeStruct((T, 1024), F32)],
        scratch_shapes=[pltpu.VMEM((3, 2, TQ, WIN), F32)],
        compiler_params=pltpu.CompilerParams(dimension_semantics=("arbitrary", "arbitrary")),
    )(proj, kvp, kvp, vt3, *after)


def _chk_bwd(proj, kvp, vt3, o, lse, do, after=()):
    nq = T // TQ

    def body(q_ref, k_ref, v_ref, vt_ref, o_ref, l_ref, do_ref, dq_ref, dkb_ref, dvb_ref, gv_ref, bias_ref, dsum_ref,
             dk_ref, dv_ref):
        i = pl.program_id(1)

        @pl.when(i == 0)
        def _():
            for first in range(3):
                for a in range(2):
                    bias_ref[first, a] = _chk_bias(vt_ref, a, max(LEFT - first * TQ, 0))
            dsum_ref[...] = jnp.zeros_like(dsum_ref)
            dk_ref[...] = jnp.zeros_like(dk_ref)
            dv_ref[...] = jnp.zeros_like(dv_ref)

        lo = _lane_lo()
        off = pl.multiple_of(i * TQ, TQ)
        kw = k_ref[pl.ds(off, WIN), :]
        vw = v_ref[pl.ds(off, WIN), :]
        bias_at = jnp.minimum(i, 2)
        q = q_ref[...]
        do_v = do_ref[...]
        prod = do_v * o_ref[...]
        dqs = []
        for a in range(2):
            keep = lo if a == 0 else jnp.logical_not(lo)
            qa = _half(q, lo, a, SCALE)
            doa = _half(do_v, lo, a)
            delta = jnp.sum(jnp.where(keep, prod, 0.0), axis=-1, keepdims=True)
            s = _nt(qa, kw) + bias_ref[bias_at, a]
            p = jnp.exp(s - l_ref[:, 128 * a:128 * a + 1])
            ds = p * (_nt(doa, vw) - delta)
            dsum_ref[a] += ds
            dsb = ds.astype(BF16)
            dk_ref[:, pl.ds(off, WIN)] += _tn(qa, dsb)
            dv_ref[:, pl.ds(off, WIN)] += _tn(doa, p.astype(BF16))
            dqs.append(_nn(dsb, kw))
        dq_ref[...] = (jnp.where(lo, dqs[0], dqs[1]) * SCALE).astype(BF16)

        @pl.when(i == nq - 1)
        def _():
            dkb_ref[...] = dk_ref[:, LEFT:].T.astype(BF16)
            dvb_ref[...] = dv_ref[:, LEFT:].T.astype(BF16)
            rr = lax.broadcasted_iota(jnp.int32, (TQ, TQ), 0)
            cc = lax.broadcasted_iota(jnp.int32, (TQ, TQ), 1)
            flip = jnp.where(rr + cc == TQ - 1, 1.0, 0.0).astype(F32)
            for a in range(2):
                dpad = jnp.concatenate([dsum_ref[a], jnp.zeros((TQ, VW - WIN), F32)], axis=1)
                z = pltpu.roll(_hdot(flip, dpad), 0, 1, stride=1, stride_axis=0)
                gv_ref[a:a + 1, :] = jnp.sum(z, axis=0, keepdims=True)

    blk = pl.BlockSpec((TQ, 128), lambda h, i: (i, h))
    wide = pl.BlockSpec((TQ, 256), lambda h, i: (i, h))
    col = pl.BlockSpec((T, 128), lambda h, i: (0, h))
    return _pcall(
        _behind(body, 7, after), name="chk_bwd", grid=(4, nq), in_specs=_chk_specs() + [blk, wide, blk] + [ANY_SPEC] * len(after),
        out_specs=[blk, col, col, pl.BlockSpec((None, 2, VW), lambda h, i: (h, 0, 0))],
        out_shape=[jax.ShapeDtypeStruct((T, 512), BF16), jax.ShapeDtypeStruct((T, 512), BF16),
                   jax.ShapeDtypeStruct((T, 512), BF16), jax.ShapeDtypeStruct((4, 2, VW), F32)],
        scratch_shapes=[pltpu.VMEM((3, 2, TQ, WIN), F32), pltpu.VMEM((2, TQ, WIN), F32),
                        pltpu.VMEM((128, T + LEFT), F32), pltpu.VMEM((128, T + LEFT), F32)],
        compiler_params=pltpu.CompilerParams(dimension_semantics=("arbitrary", "arbitrary")),
    )(proj, kvp, kvp, vt3, o, lse, do, *after)


def _zero_at_start(*refs):
    @pl.when(pl.program_id(0) == 0)
    def _():
        for r in refs:
            r[...] = jnp.zeros_like(r)


def _ffn_step(h3, x2, tgt, w1_t, w2, g_post, g_pre):
    def body(h_ref, x2_ref, t_ref, w1_ref, w2_ref, gp_ref, gf_ref, dx2_ref, da_ref, dy_ref, r_ref, loss_ref, dgp_ref, dgf_ref):
        _zero_at_start(loss_ref, dgp_ref, dgf_ref)
        w1, w2v = _w(w1_ref), _w(w2_ref)
        ra = jnp.maximum(_nt(h_ref[...], w1), 0.0)
        r = jnp.square(ra).astype(BF16)
        r_ref[...] = r
        y = _nn(r, w2v)
        x2v = x2_ref[...]
        e = x2v + _rms(y, gp_ref[...]) - t_ref[...]
        loss_ref[...] += 0.5 * jnp.sum(jnp.sum(e * e, axis=-1, keepdims=True) * (1.0 / D))
        dx3 = e * (1.0 / D)
        dy, dgp = _rms_bwd(y, gp_ref[...], dx3)
        dgp_ref[...] += dgp
        dyb = dy.astype(BF16)
        dy_ref[...] = dyb
        da = (_nt(dyb, w2v) * (2.0 * ra)).astype(BF16)
        da_ref[...] = da
        dh, dgf = _rms_bwd(x2v, gf_ref[...], _nn(da, w1))
        dgf_ref[...] += dgf
        dx2_ref[...] = dx3 + dh

    return _tok_call(body, "ffn_step", [h3, x2, tgt], [w1_t, w2, g_post, g_pre],
                     [(D, F32), (DFF, BF16), (D, BF16), (DFF, BF16)], [(8, 128), (1, D), (1, D)], vmem=VMEM_BIG)


def _mem_bwd(dx2, ym, x1, qm, km, vm, w_mo, w_mq, g_post, g_pre, after=()):
    def body(dx2_ref, ym_ref, x1_ref, q_ref, k_ref, v_ref, wo_ref, wq_ref, gp_ref, gm_ref,
             dx1_ref, dym_ref, dq_ref, dk_ref, dv_ref, dgp_ref, dgm_ref, dom_ref):
        _zero_at_start(dk_ref, dv_ref, dgp_ref, dgm_ref)
        dx2_v = dx2_ref[...]
        dym, dgp = _rms_bwd(ym_ref[...], gp_ref[...], dx2_v)
        dgp_ref[...] += dgp
        dymb = dym.astype(BF16)
        dym_ref[...] = dymb
        dom_ref[...] = _nt(dymb, _w(wo_ref)).astype(BF16)
        for h in range(MEM_HEADS):
            sl = slice(h * MEM_HD, (h + 1) * MEM_HD)
            qh, kh, doh = q_ref[:, sl], k_ref[:, sl], dom_ref[:, sl]
            s = _nt(qh, kh) * MEM_SCALE
            p = jnp.exp(s - jnp.max(s, axis=-1, keepdims=True))
            p = p / jnp.sum(p, axis=-1, keepdims=True)
            dp = _nt(doh, v_ref[:, sl])
            ds = (p * (dp - jnp.sum(p * dp, axis=-1, keepdims=True))).astype(BF16)
            dq_ref[:, sl] = (_nn(ds, kh) * MEM_SCALE).astype(BF16)
            dk_ref[:, sl] += _tn(ds, qh) * MEM_SCALE
            dv_ref[:, sl] += _tn(p.astype(BF16), doh)
        dh, dgm = _rms_bwd(x1_ref[...], gm_ref[...], _nt(dq_ref[...], _w(wq_ref)))
        dgm_ref[...] += dgm
        dx1_ref[...] = dx2_v + dh

    tiled = pl.BlockSpec((TM_WIDE, D), lambda i: (i, 0))
    in_specs = [tiled] * 4 + [_resident(a) for a in (km, vm, w_mo, w_mq, g_post, g_pre)] + [ANY_SPEC] * len(after)
    w_mo, w_mq = w_mo[0], w_mq[0]
    kv = pl.BlockSpec((NMEM, D), lambda i: (0, 0))
    vec = pl.BlockSpec((1, D), lambda i: (0, 0))
    return _pcall(
        _behind(body, 10, after), name="mem_bwd", grid=(T // TM_WIDE,), in_specs=in_specs,
        out_specs=[tiled, tiled, tiled, kv, kv, vec, vec],
        out_shape=[jax.ShapeDtypeStruct((T, D), F32), jax.ShapeDtypeStruct((T, D), BF16),
                   jax.ShapeDtypeStruct((T, D), BF16), jax.ShapeDtypeStruct((NMEM, D), F32),
                   jax.ShapeDtypeStruct((NMEM, D), F32), jax.ShapeDtypeStruct((1, D), F32),
                   jax.ShapeDtypeStruct((1, D), F32)],
        scratch_shapes=[pltpu.VMEM((TM_WIDE, D), BF16)],
        compiler_params=pltpu.CompilerParams(dimension_semantics=("arbitrary",), vmem_limit_bytes=VMEM_BIG),
    )(dx2, ym, x1, qm, km, vm, w_mo, w_mq, g_post, g_pre, *after)


def _memkv_bwd(dkm, dvm, mem, w_mk, w_mv):
    def body(dk_ref, dv_ref, m_ref, wk_ref, wv_ref, dg_ref):
        dmn = _nt(dk_ref[...].astype(BF16), _w(wk_ref)) + _nt(dv_ref[...].astype(BF16), _w(wv_ref))
        mv = m_ref[...]
        dg_ref[...] = jnp.sum(dmn * (mv * _rstd(mv)), axis=0, keepdims=True)

    return _one_call(body, "memkv_bwd", [dkm, dvm, mem, w_mk, w_mv], [((1, D), F32)], vmem=VMEM_BIG)[0]


def _postmix_bwd(dx1, z, o_f, o_c, w_out, g_post, g_fo, g_co, after=()):
    def body(dx1_ref, z_ref, of_ref, oc_ref, wo_ref, gp_ref, gfo_ref, gco_ref,
             dz_ref, dof_ref, doc_ref, dgp_ref, dgfo_ref, dgco_ref):
        _zero_at_start(dgp_ref, dgfo_ref, dgco_ref)
        dz, dgp = _rms_bwd(z_ref[...], gp_ref[...], dx1_ref[...])
        dgp_ref[...] += dgp
        dzb = dz.astype(BF16)
        dz_ref[...] = dzb
        dy = _nt(dzb, _w(wo_ref))
        dof, dgfo = _rms_bwd(of_ref[...], gfo_ref[...], dy[:, :512])
        doc, dgco = _rms_bwd(oc_ref[...], gco_ref[...], dy[:, 512:])
        dof_ref[...] = dof
        doc_ref[...] = doc
        dgfo_ref[...] += dgfo
        dgco_ref[...] += dgco

    return _tok_call(body, "postmix_bwd", [dx1, z, o_f, o_c], [w_out, g_post, g_fo, g_co],
                     [(D, BF16), (512, F32), (512, F32)], [(1, D), (1, 512), (1, 512)], tm=TM_WIDE, vmem=VMEM_BIG,
                     after=after)


def _premix_bwd(dx1, x, pieces, win_t, g_pre):
    def body(dx1_ref, x_ref, *refs):
        piece_refs, (w_ref, g_ref, dx_ref, dp_ref, dg_ref) = refs[:len(pieces)], refs[len(pieces):]
        _zero_at_start(dg_ref)
        col = 0
        for p in piece_refs:
            dp_ref[:, col:col + p.shape[1]] = p[...]
            col += p.shape[1]
        dh, dg = _rms_bwd(x_ref[...], g_ref[...], _nn(dp_ref[...], w_ref[...]))
        dg_ref[...] += dg
        dx_ref[...] = dx1_ref[...] + dh

    return _tok_call(body, "premix_bwd", [dx1, x] + list(pieces), [win_t, g_pre], [(D, F32), (PROJ, BF16)], [(1, D)],
                     tm=TM_WIDE, vmem=VMEM_BIG)


def _wgrad_group(name, pairs, rows):
    def body(*refs):
        o_ref = refs[-1]
        for k in range(len(pairs)):
            g = _tn(refs[2 * k][...].astype(BF16), refs[2 * k + 1][...].astype(BF16))
            o_ref[k * rows:(k + 1) * rows, :] = g.astype(BF16)

    in_specs, ops = [], []
    for a, b in pairs:
        in_specs += [pl.BlockSpec((a.shape[0], rows), lambda j: (0, j)), _resident(b)]
        ops += [a, b]
    return _pcall(
        body, name=name, grid=(8,), in_specs=in_specs,
        out_specs=pl.BlockSpec((None, len(pairs) * rows, D), lambda j: (j, 0, 0)),
        out_shape=jax.ShapeDtypeStruct((8, len(pairs) * rows, D), BF16),
        compiler_params=pltpu.CompilerParams(dimension_semantics=("arbitrary",), vmem_limit_bytes=VMEM_BIG),
    )(*ops)


def _adam_math(w, g, m, v):
    m2 = ADAM_B1 * m + (1.0 - ADAM_B1) * g
    v2 = ADAM_B2 * v + (1.0 - ADAM_B2) * jnp.square(g)
    m_hat = m2 / (1.0 - ADAM_B1 ** ADAM_STEP)
    v_hat = v2 / (1.0 - ADAM_B2 ** ADAM_STEP)
    delta = -ADAM_LR * (m_hat / (jnp.sqrt(v_hat) + ADAM_EPS) + ADAM_WD * w)
    return delta, m2, v2


def _adamw_small(gparts, ws, ms, vs):
    n = len(SMALL)

    def body(g_ref, *refs):
        w_refs, m_refs, v_refs = refs[:n], refs[n:2 * n], refs[2 * n:3 * n]
        outs, sum_ref = refs[3 * n:-1], refs[-1]
        g = g_ref[0]
        for k in range(1, 8):
            g = g + g_ref[k]
        sum_ref[...] = g
        outs[0][...] = sum_ref[17:18, 0:128]
        for t, name in enumerate(SMALL):
            r0, nr, c0, nc = SMALL_SLOT[name]
            gt = sum_ref[r0:r0 + nr, c0:c0 + nc]
            out = (gt,) + _adam_math(w_refs[t][...], gt, m_refs[t][...], v_refs[t][...])
            for o_ref, val in zip(outs[1 + 4 * t:5 + 4 * t], out):
                o_ref[...] = val

    whole = lambda s: pl.BlockSpec(s, lambda i, nd=len(s): (0,) * nd)
    ins = [gparts] + list(ws) + list(ms) + list(vs)
    out_shapes = [(1, 128)] + [a.shape for a in ws for _ in range(4)]
    return _pcall(
        body, name="adamw_small", grid=(1,), in_specs=[whole(a.shape) for a in ins],
        out_specs=[whole(s) for s in out_shapes], out_shape=[jax.ShapeDtypeStruct(s, F32) for s in out_shapes],
        scratch_shapes=[pltpu.VMEM((SMALL_ROWS, D), F32)],
        compiler_params=pltpu.CompilerParams(dimension_semantics=("arbitrary",)),
    )(*ins)


def _row_tile(rows):
    return next(t for t in (512, 400, 320) if rows % t == 0)


def _add_halves(g4, theirs, core, name):
    rows = g4.shape[2]
    tr = _row_tile(rows)

    def body(c_ref, a_ref, b_ref, o_ref):
        o_ref[...] = (a_ref[...].astype(F32) + b_ref[...].astype(F32)).astype(BF16)

    grid_spec = pltpu.PrefetchScalarGridSpec(
        num_scalar_prefetch=1, grid=(4, rows // tr),
        in_specs=[pl.BlockSpec((None, None, tr, D), lambda j, i, c: (j, c[0], i, 0)),
                  pl.BlockSpec((None, None, tr, D), lambda j, i, c: (j, 0, i, 0))],
        out_specs=pl.BlockSpec((None, tr, D), lambda j, i, c: (j, i, 0)))
    return _pcall(
        body, name=name, grid_spec=grid_spec, out_shape=jax.ShapeDtypeStruct((4, rows, D), BF16),
        compiler_params=pltpu.CompilerParams(dimension_semantics=("arbitrary", "arbitrary")),
    )(core, g4, theirs)


def _sum_adam(own, got, order, r0, w, m, v, name, transposed=False):
    n = w.shape[1] if transposed else w.shape[0]
    tr = min(n, 256) if n % 8 == 0 else n
    rows = tr if n % 8 == 0 else own.shape[1]

    def body(o_ref, a_ref, b_ref, c_ref, d_ref, w_ref, m_ref, v_ref, g_ref, dl_ref, m2_ref, v2_ref):
        f = lambda r: r[0:tr, :].astype(F32)
        g = ((f(a_ref) + f(b_ref)) + f(c_ref)) + f(d_ref)
        g = g.T if transposed else g
        g_ref[...] = g
        dl_ref[...], m2_ref[...], v2_ref[...] = _adam_math(w_ref[...], g, m_ref[...], v_ref[...])

    slot = lambda k: pl.BlockSpec((None, rows, D), lambda i, o: (o[k], r0 // rows + i, 0))
    wspec = pl.BlockSpec((D, tr), lambda i, o: (0, i)) if transposed else pl.BlockSpec((tr, D), lambda i, o: (i, 0))
    grid_spec = pltpu.PrefetchScalarGridSpec(
        num_scalar_prefetch=1, grid=(n // tr,), in_specs=[slot(0), slot(1), slot(2), slot(3), wspec, wspec, wspec],
        out_specs=[wspec] * 4)
    return _pcall(
        body, name=name, grid_spec=grid_spec, out_shape=[jax.ShapeDtypeStruct(w.shape, F32)] * 4,
        compiler_params=pltpu.CompilerParams(dimension_semantics=("arbitrary",)),
    )(order, own, got, got, got, w, m, v)


def _sum_adam_rows(own, got, order, ws, ms, vs, name):
    n, rows = len(ws), ws[0].shape[0]

    def body(o_ref, a_ref, b_ref, c_ref, d_ref, *refs):
        ins, outs = refs[:3 * n], refs[3 * n:]
        for t in range(n):
            r = slice(t * rows, (t + 1) * rows)
            f = lambda ref: ref[r, :].astype(F32)
            g = ((f(a_ref) + f(b_ref)) + f(c_ref)) + f(d_ref)
            out = (g,) + _adam_math(ins[t][...], g, ins[n + t][...], ins[2 * n + t][...])
            for o, val in zip(outs[4 * t:4 * t + 4], out):
                o[...] = val

    slot = lambda k: pl.BlockSpec((None, n * rows, D), lambda i, o: (o[k], 0, 0), pipeline_mode=pl.Buffered(1))
    wspec = pl.BlockSpec((rows, D), lambda i, o: (0, 0), pipeline_mode=pl.Buffered(1))
    grid_spec = pltpu.PrefetchScalarGridSpec(
        num_scalar_prefetch=1, grid=(1,), in_specs=[slot(0), slot(1), slot(2), slot(3)] + [wspec] * (3 * n),
        out_specs=[pl.BlockSpec((rows, D), lambda i, o: (0, 0))] * (4 * n))
    return _pcall(
        body, name=name, grid_spec=grid_spec, out_shape=[jax.ShapeDtypeStruct((rows, D), F32)] * (4 * n),
        compiler_params=pltpu.CompilerParams(dimension_semantics=("arbitrary",), vmem_limit_bytes=VMEM_BIG),
    )(order, own, got, got, got, *ws, *ms, *vs)


def _place():
    return lax.axis_index("x"), lax.axis_index("y"), lax.axis_index("c")


def _allgather(block, name, after=()):
    rows = block.shape[0]
    split = (rows // 2 + 15) // 16 * 16

    def body(x_ref, out_ref, token, send_sems, recv_sems, local_sem):
        token[...] = jnp.zeros_like(token)
        x, y, c = _place()
        me, sib = (x, y, c), (x, y, 1 - c)
        xn, yn, dg = (1 - x, y), (x, 1 - y), (1 - x, 1 - y)
        lo, hi = pl.ds(0, split), pl.ds(split, rows - split)

        def copy(k, blk, to, part=None, src=None):
            index = 4 * blk[0] + 2 * blk[1] + blk[2]
            view = out_ref.at[index] if part is None else out_ref.at[index, part]
            return pltpu.make_async_remote_copy(
                src_ref=view if src is None else src, dst_ref=view,
                send_sem=send_sems.at[k], recv_sem=recv_sems.at[k], device_id=to, device_id_type=MESH)

        def start(*copies):
            for cp in copies:
                cp.start()
            return list(copies)

        mine = pltpu.make_async_copy(x_ref, out_ref.at[4 * x + 2 * y + c], local_sem)
        mine.start()
        sent = start(copy(0, me, sib, src=x_ref), copy(1, me, (*xn, c), src=x_ref), copy(2, me, (*yn, c), src=x_ref))
        copy(1, (*xn, c), me).wait_recv()
        sent += start(copy(3, (*xn, c), sib), copy(5, (*xn, c), (*yn, c), part=lo))
        copy(2, (*yn, c), me).wait_recv()
        sent += start(copy(4, (*yn, c), sib), copy(6, (*yn, c), (*xn, c), part=hi))
        copy(5, (*dg, c), me, part=lo).wait_recv()
        copy(6, (*dg, c), me, part=hi).wait_recv()
        sent += start(copy(7, (*dg, c), sib))
        for k, blk in ((0, sib), (3, (*xn, 1 - c)), (4, (*yn, 1 - c)), (7, (*dg, 1 - c))):
            copy(k, blk, me).wait_recv()
        for cp in sent:
            cp.wait_send()
        mine.wait()

    return _pcall(
        _behind(body, 1, after), name=name,
        out_shape=[jax.ShapeDtypeStruct((8,) + block.shape, block.dtype), jax.ShapeDtypeStruct((8, 128), F32)],
        in_specs=[pl.BlockSpec(memory_space=pl.ANY)] * (1 + len(after)),
        out_specs=[pl.BlockSpec(memory_space=pl.ANY), pl.BlockSpec(memory_space=pltpu.VMEM)],
        scratch_shapes=[pltpu.SemaphoreType.DMA((8,)), pltpu.SemaphoreType.DMA((8,)), pltpu.SemaphoreType.DMA(())],
        compiler_params=pltpu.CompilerParams(has_side_effects=True),
    )(block, *after)


HBM_SPEC = pl.BlockSpec(memory_space=pltpu.HBM)
SEM_SPEC = pl.BlockSpec(memory_space=pltpu.SEMAPHORE)
ANY_SPEC = pl.BlockSpec(memory_space=pl.ANY)
EFFECT = pltpu.SideEffectType.DATAFLOW_SIDE_EFFECTING


def _in_hbm(a):
    return pltpu.with_memory_space_constraint(a, pltpu.HBM)


def _start_copies(name, src, land_shape, plan, n):
    def body(src_ref, land_ref, send_sems, recv_sems, src_thru, land_thru, token):
        for k, (s, d, to, _) in enumerate(plan(src_ref, land_ref)):
            pltpu.make_async_remote_copy(src_ref=s, dst_ref=d, send_sem=send_sems.at[k], recv_sem=recv_sems.at[k],
                                         device_id=to, device_id_type=MESH).start()
        token[...] = jnp.zeros_like(token)

    return _pcall(
        body, name=name,
        out_shape=(pltpu.SemaphoreType.DMA((n,)), pltpu.SemaphoreType.DMA((n,)), pltpu.HBM(src.shape, src.dtype),
                   pltpu.HBM(land_shape, src.dtype), jax.ShapeDtypeStruct((8, 128), F32)),
        in_specs=(HBM_SPEC, HBM_SPEC),
        out_specs=(SEM_SPEC, SEM_SPEC, HBM_SPEC, HBM_SPEC, pl.BlockSpec(memory_space=pltpu.VMEM)),
        input_output_aliases={0: 2, 1: 3}, compiler_params=pltpu.CompilerParams(has_side_effects=EFFECT),
    )(_in_hbm(src), _in_hbm(lax.empty(land_shape, src.dtype)))


def _wait_copies(name, started, after, plan):
    send_sems, recv_sems, src_thru, land_thru, _ = started

    def body(src_ref, land_ref, send_sems, recv_sems, *rest):
        for k, (s, _, to, mine) in enumerate(plan(src_ref, land_ref)):
            cp = pltpu.make_async_remote_copy(src_ref=s, dst_ref=mine, send_sem=send_sems.at[k],
                                              recv_sem=recv_sems.at[k], device_id=to, device_id_type=MESH)
            cp.wait_send()
            cp.wait_recv()

    return _pcall(
        body, name=name,
        out_shape=(pltpu.HBM(src_thru.shape, src_thru.dtype), pltpu.HBM(land_thru.shape, land_thru.dtype)),
        in_specs=(HBM_SPEC, HBM_SPEC, SEM_SPEC, SEM_SPEC) + (ANY_SPEC,) * len(after), out_specs=(HBM_SPEC, HBM_SPEC),
        input_output_aliases={0: 0, 1: 1}, compiler_params=pltpu.CompilerParams(has_side_effects=EFFECT),
    )(src_thru, land_thru, send_sems, recv_sems, *after)


def _start_inplace(name, buf, plan, n):
    def body(buf_ref, send_sems, recv_sems, buf_thru, token):
        for k, (s, d, to, _) in enumerate(plan(buf_ref, buf_ref)):
            pltpu.make_async_remote_copy(src_ref=s, dst_ref=d, send_sem=send_sems.at[k], recv_sem=recv_sems.at[k],
                                         device_id=to, device_id_type=MESH).start()
        token[...] = jnp.zeros_like(token)

    return _pcall(
        body, name=name,
        out_shape=(pltpu.SemaphoreType.DMA((n,)), pltpu.SemaphoreType.DMA((n,)), pltpu.HBM(buf.shape, buf.dtype),
                   jax.ShapeDtypeStruct((8, 128), F32)),
        in_specs=(HBM_SPEC,), out_specs=(SEM_SPEC, SEM_SPEC, HBM_SPEC, pl.BlockSpec(memory_space=pltpu.VMEM)),
        input_output_aliases={0: 2}, compiler_params=pltpu.CompilerParams(has_side_effects=EFFECT),
    )(_in_hbm(buf))


def _wait_inplace(name, started, after, plan):
    send_sems, recv_sems, buf_thru, _ = started

    def body(buf_ref, send_sems, recv_sems, *rest):
        for k, (s, _, to, mine) in enumerate(plan(buf_ref, buf_ref)):
            cp = pltpu.make_async_remote_copy(src_ref=s, dst_ref=mine, send_sem=send_sems.at[k],
                                              recv_sem=recv_sems.at[k], device_id=to, device_id_type=MESH)
            cp.wait_send()
            cp.wait_recv()

    return _pcall(
        body, name=name, out_shape=pltpu.HBM(buf_thru.shape, buf_thru.dtype),
        in_specs=(HBM_SPEC, SEM_SPEC, SEM_SPEC) + (ANY_SPEC,) * len(after), out_specs=HBM_SPEC,
        input_output_aliases={0: 0}, compiler_params=pltpu.CompilerParams(has_side_effects=EFFECT),
    )(buf_thru, send_sems, recv_sems, *after)


def _gather_plan(src_ref, land_ref):
    x, y, c = _place()
    peers = [(x, y, 1 - c), (1 - x, y, c), (x, 1 - y, c)]
    return [(src_ref, land_ref.at[4 * x + 2 * y + c], p, land_ref.at[4 * p[0] + 2 * p[1] + p[2]]) for p in peers]


def _relay_plan(buf_ref, _):
    x, y, c = _place()
    slot = lambda p, pc: 4 * p[0] + 2 * p[1] + pc
    xn, yn, dg, sib = (1 - x, y), (x, 1 - y), (1 - x, 1 - y), (x, y, 1 - c)
    half = buf_ref.shape[1] // 2
    lo, hi = pl.ds(0, half), pl.ds(half, half)
    return [(buf_ref.at[slot(xn, c)], buf_ref.at[slot(xn, c)], sib, buf_ref.at[slot(xn, 1 - c)]),
            (buf_ref.at[slot(yn, c)], buf_ref.at[slot(yn, c)], sib, buf_ref.at[slot(yn, 1 - c)]),
            (buf_ref.at[slot(xn, c), lo], buf_ref.at[slot(xn, c), lo], (*yn, c), buf_ref.at[slot(dg, c), lo]),
            (buf_ref.at[slot(yn, c), hi], buf_ref.at[slot(yn, c), hi], (*xn, c), buf_ref.at[slot(dg, c), hi])]


def _swap_plan(src_ref, land_ref):
    x, y, c = _place()
    return [(src_ref.at[:, pl.ds(1 - c, 1)], land_ref, (x, y, 1 - c), land_ref)]


def _exchange_plan(src_ref, land_ref):
    x, y, c = _place()
    chips = [(1 - x, y), (x, 1 - y), (1 - x, 1 - y)]
    return [(src_ref.at[2 * px + py], land_ref.at[2 * x + y], (px, py, c), land_ref.at[2 * px + py]) for px, py in chips]


def _gather_forward(land, block):
    def body(land_ref, out_ref, send_sems, recv_sems):
        x, y, c = _place()
        chips = [(1 - x, 1 - y)]

        def copy(k, px, py, pc):
            blk = out_ref.at[4 * px + 2 * py + pc]
            return pltpu.make_async_remote_copy(src_ref=blk, dst_ref=blk, send_sem=send_sems.at[k],
                                                recv_sem=recv_sems.at[k], device_id=(x, y, 1 - c), device_id_type=MESH)

        sent = [copy(k, px, py, c) for k, (px, py) in enumerate(chips)]
        for cp in sent:
            cp.start()
        for k, (px, py) in enumerate(chips):
            copy(k, px, py, 1 - c).wait_recv()
        for cp in sent:
            cp.wait_send()

    land = _pcall(
        body, name="allgather_rest_forward", out_shape=jax.ShapeDtypeStruct(land.shape, land.dtype),
        in_specs=[ANY_SPEC], out_specs=ANY_SPEC, input_output_aliases={0: 0},
        scratch_shapes=[pltpu.SemaphoreType.DMA((1,)), pltpu.SemaphoreType.DMA((1,))],
        compiler_params=pltpu.CompilerParams(has_side_effects=True),
    )(land)

    rows = block.shape[0]
    tr = rows // 4

    def place(me_ref, x_ref, land_ref, out_ref):
        out_ref[...] = x_ref[...]

    x, y, c = _place()
    grid_spec = pltpu.PrefetchScalarGridSpec(
        num_scalar_prefetch=1, grid=(rows // tr,),
        in_specs=[pl.BlockSpec((tr, D), lambda i, me: (i, 0)), ANY_SPEC],
        out_specs=pl.BlockSpec((None, tr, D), lambda i, me: (me[0], i, 0)))
    return _pcall(
        place, name="allgather_rest_own", grid_spec=grid_spec, out_shape=jax.ShapeDtypeStruct(land.shape, land.dtype),
        input_output_aliases={2: 0}, compiler_params=pltpu.CompilerParams(dimension_semantics=("arbitrary",)),
    )((4 * x + 2 * y + c).reshape(1), block, land)


class _ReduceScatter:
    def __init__(self, name, g):
        self.name = name
        rows = g.shape[1]
        self.started = _start_copies(name + "_swap_start", g.reshape(4, 2, rows, D), (4, 1, rows, D), _swap_plan, 1)
        self.token = self.started[4]

    def halfway(self, after):
        g4, theirs = _wait_copies(self.name + "_swap_wait", self.started, after, _swap_plan)
        self.own = _add_halves(g4, theirs, lax.axis_index("c").reshape(1), self.name + "_add_halves")
        self.started = _start_copies(self.name + "_exch_start", self.own, self.own.shape, _exchange_plan, 3)
        self.token = self.started[4]

    def finish(self, after):
        own, got = _wait_copies(self.name + "_exch_wait", self.started, after, _exchange_plan)
        chip = 2 * lax.axis_index("x") + lax.axis_index("y")
        return own, got, (chip + jnp.arange(4, dtype=jnp.int32)) % 4


def _pack_small(p, loss):
    def body(*refs):
        o_ref = refs[-1]
        o_ref[...] = jnp.zeros_like(o_ref)
        for ref, name in zip(refs, SMALL):
            r0, nr, c0, nc = SMALL_SLOT[name]
            o_ref[r0:r0 + nr, c0:c0 + nc] = ref[...]
        o_ref[17:18, 0:128] = refs[len(SMALL)][0:1, :]

    return _one_call(body, "pack_small_grads", [p[n] for n in SMALL] + [loss], [((SMALL_ROWS, D), F32)])[0]


_GAP_DEV, _GAP_ROW = divmod(GATE0 + 8, N_IN)
_GAP = CHK0 - GATE0 - 8


def _in_rows_to_proj(g):
    runs = [(j, 0, N_IN, N_IN * j) for j in range(_GAP_DEV)]
    runs += [(_GAP_DEV, 0, _GAP_ROW, N_IN * _GAP_DEV), (_GAP_DEV, _GAP_ROW, N_IN, N_IN * _GAP_DEV + _GAP_ROW + _GAP)]
    runs += [(j, 0, N_IN, N_IN * j + _GAP) for j in range(_GAP_DEV + 1, 8)]

    def body(g_ref, o_ref, acc_ref):
        acc_ref[...] = jnp.zeros_like(acc_ref)
        for j, r0, r1, dest in runs:
            start, shift = dest // 16 * 16, dest % 16
            win = -(-(shift + r1 - r0) // 16) * 16
            r = lax.broadcasted_iota(jnp.int32, (win, R_IN), 0)
            c = lax.broadcasted_iota(jnp.int32, (win, R_IN), 1)
            move = jnp.where((c >= r0) & (c < r1) & (r == c - r0 + shift), 1.0, 0.0).astype(BF16)
            acc_ref[start:start + win, :] += _nn(move, g_ref[j])
        o_ref[...] = acc_ref[...].astype(BF16)

    return _pcall(
        body, name="w_in_layout", out_shape=jax.ShapeDtypeStruct((PROJ, D), BF16), grid=(1,),
        in_specs=[pl.BlockSpec(g.shape, lambda i: (0, 0, 0))], out_specs=pl.BlockSpec((PROJ, D), lambda i: (0, 0)),
        scratch_shapes=[pltpu.VMEM((PROJ, D), F32)],
        compiler_params=pltpu.CompilerParams(dimension_semantics=("arbitrary",), vmem_limit_bytes=VMEM_BIG),
    )(g)


def _wgrad_in(dproj, h1):
    tm, steps = 640, PROJ // 640

    def body(a_ref, b_ref, o_ref, g_ref):
        i = pl.program_id(0)

        @pl.when(i == 0)
        def _():
            g_ref[PROJ:, :] = jnp.zeros((R_IN - N_IN + 1, D), F32)

        g_ref[pl.ds(pl.multiple_of(i * tm, tm), tm), :] = _tn(a_ref[...], b_ref[...])
        row = lax.broadcasted_iota(jnp.int32, (R_IN, D), 0)
        for j in range(8):
            lo = N_IN * j + (_GAP if j > _GAP_DEV else 0)
            hi = N_IN * j + (_GAP if j >= _GAP_DEV else 0)
            ready = (hi + N_IN - 1) // tm
            assert hi + R_IN <= tm * (ready + 1) or ready == steps - 1

            @pl.when(i == ready)
            def _(j=j, lo=lo, hi=hi):
                v = g_ref[hi:hi + R_IN, :]
                if lo != hi:
                    v = jnp.where(row < _GAP_ROW, g_ref[lo:lo + R_IN, :], v)
                o_ref[j] = jnp.where(row < N_IN, v, 0.0).astype(BF16)

    return _pcall(
        body, name="wgrad_in", grid=(steps,),
        in_specs=[pl.BlockSpec((T, tm), lambda i: (0, i)), _resident(h1)],
        out_specs=pl.BlockSpec((8, R_IN, D), lambda i: (0, 0, 0)),
        out_shape=jax.ShapeDtypeStruct((8, R_IN, D), BF16),
        scratch_shapes=[pltpu.VMEM((PROJ + R_IN - N_IN + 1, D), F32)],
        compiler_params=pltpu.CompilerParams(dimension_semantics=("arbitrary",), vmem_limit_bytes=VMEM_BIG),
    )(dproj, h1)


def _local_grads(x, mem, tgt, win_t, gw_of, sm, on_grads, after=()):
    b_pad = jnp.pad(sm['b_fgt'], ((0, 0), (0, 120)))
    tbl = jnp.pad(sm['rel_bias'], ((0, 0), (0, NREL_PAD - 257)))

    h1, proj, flog = _premix_fwd(x, sm['g_mix_pre'], win_t, after)
    c = _gate_fwd(flog, b_pad)
    ct3 = c[:, :8].T.reshape(4, 2, T)
    o_f, lse_f = _fox_fwd(proj, c, ct3)
    vt3 = _relvec_fwd(tbl).reshape(4, 2, VW)
    kvp = jnp.pad(proj[:, CHK0 + 512:], ((LEFT, 0), (0, 0)))
    o_c, lse_c = _chk_fwd(proj, kvp, vt3, [gw_of('relay', [o_f])])
    gw = gw_of('done', [o_c])
    w_out, w_mq, w_mk, w_mv, w_mo, w1_t, w2 = (_wblk(gw, n) for n in ('w_out', 'w_mq', 'w_mk', 'w_mv', 'w_mo', 'w_ff1', 'w_ff2'))
    ycat, z, x1, h2, qm = _postmix_fwd(x, o_f, o_c, sm['g_fox_out'], sm['g_chk_out'], w_out,
                                       sm['g_mix_post'], sm['g_mem_pre'], w_mq)
    memn, km, vm = _memkv_fwd(mem, sm['g_mem_kv'], w_mk, w_mv)
    om, ym, x2, h3 = _mem_fwd(qm, x1, km, vm, w_mo, sm['g_mem_post'], sm['g_ff_pre'])

    gs = {}
    dx2, da, dy3, r, loss_acc, gs['g_ff_post'], gs['g_ff_pre'] = _ffn_step(h3, x2, tgt, w1_t, w2, sm['g_ff_post'],
                                                                         sm['g_ff_pre'])
    tok = on_grads('A', _wgrad_group("wgrad_ff", [(da, h3), (r, dy3)], 512), None)
    dx1, dym, dqm, dkm, dvm, gs['g_mem_post'], gs['g_mem_pre'] = _mem_bwd(
        dx2, ym, x1, qm, km, vm, w_mo, w_mq, sm['g_mem_post'], sm['g_mem_pre'], [tok])
    tok = on_grads('A halfway', None, [dx1])
    gs['g_mem_kv'] = _memkv_bwd(dkm, dvm, mem, w_mk, w_mv)
    dz, dof, doc, gs['g_mix_post'], gs['g_fox_out'], gs['g_chk_out'] = _postmix_bwd(
        dx1, z, o_f, o_c, w_out, sm['g_mix_post'], sm['g_fox_out'], sm['g_chk_out'], [tok])
    tok = on_grads('B', _wgrad_group("wgrad_mem_out", [(ycat, dz), (h2, dqm), (memn, dkm), (memn, dvm), (om, dym)], 128), None)
    dq_f, dk_f, dv_f, dct, dcq = _fox_bwd(proj, c, ct3, o_f, lse_f, dof, [tok])
    tok = on_grads('B halfway', None, [dq_f])
    dq_c, dk_c, dv_c, gv = _chk_bwd(proj, kvp, vt3, o_c, lse_c, doc, [tok])
    gs['rel_bias'] = _relvec_bwd(gv.reshape(8, VW))[:, :257]
    dc = jnp.pad(dct.reshape(8, T).T + dcq[:, :, :2].transpose(1, 0, 2).reshape(T, 8), ((0, 0), (0, 120)))
    dflog, db = _gate_bwd(dc, flog, b_pad)
    gs['b_fgt'] = db[0:1, :8]
    grad_x, dproj, gs['g_mix_pre'] = _premix_bwd(dx1, x, [dq_f, dk_f, dv_f, dflog, dq_c, dk_c, dv_c], win_t, sm['g_mix_pre'])
    on_grads('C', _wgrad_in(dproj, h1), None)
    return loss_acc, grad_x, gs


def kernel(x, mem, w_in, b_fgt, rel_bias, g_fox_out, g_chk_out, w_out, g_mix_pre, g_mix_post, g_mem_kv, w_mq, w_mk, w_mv, w_mo, g_mem_pre, g_mem_post, w_ff1, w_ff2, g_ff_pre, g_ff_post, loss_target, m_w_in, m_b_fgt, m_rel_bias, m_g_fox_out, m_g_chk_out, m_w_out, m_g_mix_pre, m_g_mix_post, m_g_mem_kv, m_w_mq, m_w_mk, m_w_mv, m_w_mo, m_g_mem_pre, m_g_mem_post, m_w_ff1, m_w_ff2, m_g_ff_pre, m_g_ff_post, v_w_in, v_b_fgt, v_rel_bias, v_g_fox_out, v_g_chk_out, v_w_out, v_g_mix_pre, v_g_mix_post, v_g_mem_kv, v_w_mq, v_w_mk, v_w_mv, v_w_mo, v_g_mem_pre, v_g_mem_post, v_w_ff1, v_w_ff2, v_g_ff_pre, v_g_ff_post):
    args = dict(locals())
    two_d = lambda a: a.reshape(a.shape[-2:])
    w = {n: two_d(args[n]) for n in WEIGHTS}
    m = {n: two_d(args['m_' + n]) for n in WEIGHTS}
    v = {n: two_d(args['v_' + n]) for n in WEIGHTS}

    sm = {n: w[n] for n in SMALL}
    shard_in = jnp.pad(w['w_in'].T, ((0, R_IN - N_IN), (0, 0))).astype(BF16)
    gathered_in, zero = _allgather(shard_in, "allgather_w_in")
    win_t = _in_rows_to_proj(gathered_in)
    shard_rest = (jnp.concatenate([w['w_ff1'].T, w['w_ff2'], w['w_out'], w['w_mq'], w['w_mk'], w['w_mv'], w['w_mo']],
                                  axis=0) + zero[0, 0]).astype(BF16)
    gather = {'first': _start_copies("allgather_rest_start", shard_rest, (8, R_REST, D), _gather_plan, 3)}

    def gw_of(stage, after):
        if stage == 'relay':
            gather['block'], land = _wait_copies("allgather_rest_wait", gather['first'], after, _gather_plan)
            gather['second'] = _start_inplace("allgather_rest_relay_start", land, _relay_plan, 4)
            return gather['second'][3]
        land = _wait_inplace("allgather_rest_relay_wait", gather['second'], after, _relay_plan)
        return _gather_forward(land, gather['block'])

    rs = {}

    def on_grads(stage, g, after):
        if stage.endswith('halfway'):
            rs[stage[0]].halfway(after)
            return rs[stage[0]].token
        rs[stage] = _ReduceScatter("rs_" + stage.lower(), g)
        return rs[stage].token

    loss_local, grad_x, gs = _local_grads(x[0], mem[0], loss_target[0], win_t, gw_of, sm, on_grads, [gather['first'][4]])
    grads, deltas, new_m, new_v = {}, {}, {}, {}

    def update(n, out):
        grads[n], deltas[n], new_m[n], new_v[n] = out

    gparts, _ = _allgather(_pack_small(gs, loss_local), "allgather_small_grads", [rs['C'].token])
    small = _adamw_small(gparts, [w[n] for n in SMALL], [m[n] for n in SMALL], [v[n] for n in SMALL])
    loss = small[0][0, 0]
    for t, n in enumerate(SMALL):
        update(n, small[1 + 4 * t:5 + 4 * t])
    rs['C'].halfway([small[0]])

    own, got, order = rs['A'].finish([grad_x, rs['C'].started[4]])
    update('w_ff1', _sum_adam(own, got, order, 0, w['w_ff1'], m['w_ff1'], v['w_ff1'], "adamw_w_ff1", transposed=True))
    update('w_ff2', _sum_adam(own, got, order, 512, w['w_ff2'], m['w_ff2'], v['w_ff2'], "adamw_w_ff2"))
    own, got, order = rs['B'].finish([grad_x, rs['C'].started[4]])
    names_b = ('w_out', 'w_mq', 'w_mk', 'w_mv', 'w_mo')
    done = _sum_adam_rows(own, got, order, [w[n] for n in names_b], [m[n] for n in names_b], [v[n] for n in names_b],
                          "adamw_group_b")
    for k, n in enumerate(names_b):
        update(n, done[4 * k:4 * k + 4])

    own, got, order = rs['C'].finish([new_v[n] for n in BIG if n != 'w_in'])
    done = _sum_adam(own, got, order, 0, w['w_in'].T, m['w_in'].T, v['w_in'].T, "adamw_w_in")
    update('w_in', [a.T for a in done])

    out = [loss, grad_x[None]]
    for group in (grads, deltas, new_m, new_v):
        out += [group[n].reshape(args[n].shape) for n in WEIGHTS]
    return tuple(out)
```

```python
import jax
import jax.numpy as jnp
from jax import lax
from jax.experimental import pallas as pl
from jax.experimental.pallas import tpu as pltpu

F32 = jnp.float32
BF16 = jnp.bfloat16
MESH = pl.DeviceIdType.MESH

T = 2048
D = 1024
NMEM = 256
DFF = 4096
EPS = 1e-6
TM = 256
TM_WIDE = 512
TQ = 256
FQ = 512
HD = 64
SCALE = HD ** -0.5
MEM_HEADS = 4
MEM_HD = 256
MEM_SCALE = MEM_HD ** -0.5
NEG = -1e30
LEFT = 512
WIN = LEFT + TQ
VW = 1024
NREL_PAD = 384
PROJ = 3200
GATE0 = 1536
CHK0 = 1664
VMEM_BIG = 56 * 1024 * 1024

ADAM_LR = 0.001
ADAM_B1 = 0.9
ADAM_B2 = 0.999
ADAM_EPS = 1e-08
ADAM_WD = 0.01
ADAM_STEP = 10

N_IN = 385
R_IN = 400
R_REST = 1664
W_ROWS = {'w_ff1': (0, 512), 'w_ff2': (512, 512),
          'w_out': (1024, 128), 'w_mq': (1152, 128), 'w_mk': (1280, 128), 'w_mv': (1408, 128), 'w_mo': (1536, 128)}
SMALL_ROWS = 24
SMALL_SLOT = {'rel_bias': (0, 8, 0, 257), 'b_fgt': (8, 1, 0, 8), 'g_fox_out': (9, 1, 0, 512), 'g_chk_out': (9, 1, 512, 512),
              'g_mix_pre': (10, 1, 0, 1024), 'g_mix_post': (11, 1, 0, 1024), 'g_mem_kv': (12, 1, 0, 1024),
              'g_mem_pre': (13, 1, 0, 1024), 'g_mem_post': (14, 1, 0, 1024), 'g_ff_pre': (15, 1, 0, 1024),
              'g_ff_post': (16, 1, 0, 1024)}

WEIGHTS = ['w_in', 'b_fgt', 'rel_bias', 'g_fox_out', 'g_chk_out', 'w_out', 'g_mix_pre', 'g_mix_post', 'g_mem_kv',
           'w_mq', 'w_mk', 'w_mv', 'w_mo', 'g_mem_pre', 'g_mem_post', 'w_ff1', 'w_ff2', 'g_ff_pre', 'g_ff_post']
BIG = ['w_in', 'w_out', 'w_mq', 'w_mk', 'w_mv', 'w_mo', 'w_ff1', 'w_ff2']
SMALL = [n for n in WEIGHTS if n not in BIG]


def _pcall(body, **kw):
    return pl.pallas_call(body, **kw)


def _nn(a, b):
    return jnp.dot(a, b, preferred_element_type=F32)


def _nt(a, b):
    return lax.dot_general(a, b, (((1,), (1,)), ((), ())), preferred_element_type=F32)


def _tn(a, b):
    return lax.dot_general(a, b, (((0,), (0,)), ((), ())), preferred_element_type=F32)


def _w(ref):
    v = ref[...]
    return v if v.ndim == 2 else v.reshape(-1, v.shape[-1])


def _rstd(x):
    return lax.rsqrt(jnp.mean(x * x, axis=-1, keepdims=True) + EPS)


def _rms(x, g):
    return x * _rstd(x) * g


def _rms_bwd(x, g, dy):
    r = _rstd(x)
    xh = x * r
    dg = jnp.sum(dy * xh, axis=0, keepdims=True)
    dxh = dy * g
    dx = r * (dxh - xh * jnp.mean(dxh * xh, axis=-1, keepdims=True))
    return dx, dg


def _resident(a):
    if isinstance(a, tuple):
        _, shape, index = a
        return pl.BlockSpec(shape, lambda *_: index, pipeline_mode=pl.Buffered(1))
    return pl.BlockSpec(a.shape, lambda *_, nd=a.ndim: (0,) * nd, pipeline_mode=pl.Buffered(1))


def _wblk(gw, name):
    r0, rows = W_ROWS[name]
    return (gw, (8, rows, D), (0, r0 // rows, 0))


def _behind(body, n_in, after):
    if not after:
        return body
    return lambda *refs: body(*refs[:n_in], *refs[n_in + len(after):])


def _tok_call(body, name, tiled, full, outs_tiled, outs_acc=(), rows=T, tm=TM, vmem=None, after=()):
    in_specs = [pl.BlockSpec((tm, a.shape[1]), lambda i: (i, 0)) for a in tiled]
    in_specs += [_resident(a) for a in full] + [ANY_SPEC] * len(after)
    full = [a[0] if isinstance(a, tuple) else a for a in full] + list(after)
    body = _behind(body, len(tiled) + len(full) - len(after), after)
    out_shape = [jax.ShapeDtypeStruct((rows, c), dt) for c, dt in outs_tiled]
    out_shape += [jax.ShapeDtypeStruct(s, F32) for s in outs_acc]
    out_specs = [pl.BlockSpec((tm, c), lambda i: (i, 0)) for c, _ in outs_tiled]
    out_specs += [pl.BlockSpec(s, lambda i, nd=len(s): (0,) * nd) for s in outs_acc]
    return _pcall(
        body, name=name, grid=(rows // tm,), in_specs=in_specs, out_specs=out_specs, out_shape=out_shape,
        compiler_params=pltpu.CompilerParams(dimension_semantics=("arbitrary",), vmem_limit_bytes=vmem),
    )(*tiled, *full)


def _one_call(body, name, ins, outs, vmem=None):
    whole = lambda s: pl.BlockSpec(s, lambda i, nd=len(s): (0,) * nd)
    return _pcall(
        body, name=name, grid=(1,), in_specs=[_resident(a) for a in ins], out_specs=[whole(s) for s, _ in outs],
        out_shape=[jax.ShapeDtypeStruct(s, dt) for s, dt in outs],
        compiler_params=pltpu.CompilerParams(dimension_semantics=("arbitrary",), vmem_limit_bytes=vmem),
    )(*[a[0] if isinstance(a, tuple) else a for a in ins])


def _premix_fwd(x, g_pre, win_t, after=()):
    def body(x_ref, g_ref, w_ref, h_ref, proj_ref, flog_ref):
        h = _rms(x_ref[...], g_ref[...]).astype(BF16)
        h_ref[...] = h
        p = _nt(h, w_ref[...])
        proj_ref[...] = p.astype(BF16)
        flog_ref[...] = p[:, GATE0:GATE0 + 128]

    return _tok_call(body, "premix_fwd", [x], [g_pre, win_t],
                     [(D, BF16), (PROJ, BF16), (128, F32)], tm=TM_WIDE, vmem=VMEM_BIG, after=after)


def _postmix_fwd(x, o_f, o_c, g_fo, g_co, w_out, g_post, g_mpre, w_mq):
    def body(x_ref, of_ref, oc_ref, gfo_ref, gco_ref, wo_ref, gp_ref, gm_ref, wq_ref,
             y_ref, z_ref, x1_ref, h2_ref, qm_ref):
        y_ref[:, :512] = _rms(of_ref[...], gfo_ref[...]).astype(BF16)
        y_ref[:, 512:] = _rms(oc_ref[...], gco_ref[...]).astype(BF16)
        z = _nn(y_ref[...], _w(wo_ref))
        z_ref[...] = z
        x1 = x_ref[...] + _rms(z, gp_ref[...])
        x1_ref[...] = x1
        h2 = _rms(x1, gm_ref[...]).astype(BF16)
        h2_ref[...] = h2
        qm_ref[...] = _nn(h2, _w(wq_ref)).astype(BF16)

    return _tok_call(body, "postmix_fwd", [x, o_f, o_c], [g_fo, g_co, w_out, g_post, g_mpre, w_mq],
                     [(D, BF16), (D, F32), (D, F32), (D, BF16), (D, BF16)], tm=TM_WIDE, vmem=VMEM_BIG)


def _memkv_fwd(mem, g_kv, w_mk, w_mv):
    def body(m_ref, g_ref, wk_ref, wv_ref, mn_ref, k_ref, v_ref):
        mn = _rms(m_ref[...], g_ref[...]).astype(BF16)
        mn_ref[...] = mn
        k_ref[...] = _nn(mn, _w(wk_ref)).astype(BF16)
        v_ref[...] = _nn(mn, _w(wv_ref)).astype(BF16)

    return _tok_call(body, "memkv_fwd", [mem], [g_kv, w_mk, w_mv],
                     [(D, BF16), (D, BF16), (D, BF16)], rows=NMEM, tm=NMEM, vmem=VMEM_BIG)


def _mem_fwd(qm, x1, km, vm, w_mo, g_post, g_fpre):
    def body(q_ref, x1_ref, k_ref, v_ref, wo_ref, gp_ref, gf_ref, om_ref, ym_ref, x2_ref, h3_ref):
        for h in range(MEM_HEADS):
            sl = slice(h * MEM_HD, (h + 1) * MEM_HD)
            s = _nt(q_ref[:, sl], k_ref[:, sl]) * MEM_SCALE
            p = jnp.exp(s - jnp.max(s, axis=-1, keepdims=True))
            p = p / jnp.sum(p, axis=-1, keepdims=True)
            om_ref[:, sl] = _nn(p.astype(BF16), v_ref[:, sl]).astype(BF16)
        ym = _nn(om_ref[...], _w(wo_ref))
        ym_ref[...] = ym
        x2 = x1_ref[...] + _rms(ym, gp_ref[...])
        x2_ref[...] = x2
        h3_ref[...] = _rms(x2, gf_ref[...]).astype(BF16)

    return _tok_call(body, "mem_fwd", [qm, x1], [km, vm, w_mo, g_post, g_fpre],
                     [(D, BF16), (D, F32), (D, F32), (D, BF16)], tm=TM_WIDE, vmem=VMEM_BIG)


def _tri(lower):
    r = lax.broadcasted_iota(jnp.int32, (128, 128), 0)
    c = lax.broadcasted_iota(jnp.int32, (128, 128), 1)
    return jnp.where(r >= c if lower else c >= r, 1.0, 0.0).astype(F32)


def _hdot(a, b):
    return jnp.dot(a, b, preferred_element_type=F32, precision=lax.Precision.HIGHEST)


def _gate_fwd(flog, b_pad):
    def body(f_ref, b_ref, c_ref):
        tri = _tri(True)

        def step(i, carry):
            rows = pl.ds(pl.multiple_of(i * 128, 128), 128)
            z = f_ref[rows, :] + b_ref[...]
            lf = jnp.minimum(z, 0.0) - jnp.log(1.0 + jnp.exp(-jnp.abs(z)))
            cb = _hdot(tri, lf) + carry
            c_ref[rows, :] = cb
            return cb[127:128, :]

        lax.fori_loop(0, T // 128, step, jnp.zeros((1, 128), F32))

    return _one_call(body, "gate_fwd", [flog, b_pad], [((T, 128), F32)])[0]


def _gate_bwd(dc, flog, b_pad):
    def body(dc_ref, f_ref, b_ref, df_ref, db_ref):
        tri = _tri(False)

        def step(j, carry):
            run, db = carry
            i = T // 128 - 1 - j
            rows = pl.ds(pl.multiple_of(i * 128, 128), 128)
            dcb = dc_ref[rows, :]
            rb = _hdot(tri, dcb) + run
            z = f_ref[rows, :] + b_ref[...]
            df = rb * (1.0 / (1.0 + jnp.exp(z)))
            df_ref[rows, :] = df.astype(BF16)
            return run + jnp.sum(dcb, axis=0, keepdims=True), db + jnp.sum(df, axis=0, keepdims=True)

        _, db = lax.fori_loop(0, T // 128, step, (jnp.zeros((1, 128), F32), jnp.zeros((1, 128), F32)))
        db_ref[...] = jnp.broadcast_to(db, (8, 128))

    return _one_call(body, "gate_bwd", [dc, flog, b_pad], [((T, 128), BF16), ((8, 128), F32)])


def _lane_lo(rows=TQ):
    return lax.broadcasted_iota(jnp.int32, (rows, 128), 1) < HD


def _half(v, lo, a, scale=None):
    keep = lo if a == 0 else jnp.logical_not(lo)
    v = v.astype(F32) if scale is None else v.astype(F32) * scale
    return jnp.where(keep, v, 0.0).astype(BF16)


def _fox_specs():
    return [pl.BlockSpec((FQ, 128), lambda h, i: (i, h)),
            pl.BlockSpec((T, 128), lambda h, i: (0, 4 + h)),
            pl.BlockSpec((T, 128), lambda h, i: (0, 8 + h))]


def _lane_pick(x, at):
    lane = lax.broadcasted_iota(jnp.int32, x.shape, 1)
    return jnp.sum(jnp.where(lane == at, x, 0.0), axis=-1, keepdims=True)


def _fox_fwd(proj, c, ct3):
    def body(q_ref, k_ref, v_ref, c_ref, ct_ref, o_ref, l_ref):
        i = pl.program_id(1)
        lo = _lane_lo(FQ)
        causal = lax.broadcasted_iota(jnp.int32, (FQ, FQ), 1) <= lax.broadcasted_iota(jnp.int32, (FQ, FQ), 0)
        q = q_ref[...]
        qs = [_half(q, lo, a, SCALE) for a in range(2)]
        cqs = [_lane_pick(c_ref[...], 2 * pl.program_id(0) + a) for a in range(2)]

        def tile(off, carry, diagonal):
            kblk = k_ref[pl.ds(off, FQ), :]
            vblk = v_ref[pl.ds(off, FQ), :]
            new = []
            for a in range(2):
                m, l, acc = carry[a]
                s = _nt(qs[a], kblk) + (cqs[a] - ct_ref[a:a + 1, pl.ds(off, FQ)])
                if diagonal:
                    s = jnp.where(causal, s, NEG)
                m2 = jnp.maximum(m, jnp.max(s, axis=-1, keepdims=True))
                p = jnp.exp(s - m2)
                alpha = jnp.exp(m - m2)
                new.append((m2, alpha * l + jnp.sum(p, axis=-1, keepdims=True),
                            alpha * acc + _nn(p.astype(BF16), vblk)))
            return tuple(new)

        init = (jnp.full((FQ, 1), NEG, F32), jnp.zeros((FQ, 1), F32), jnp.zeros((FQ, 128), F32))
        carry = lax.fori_loop(0, i, lambda kb, c: tile(pl.multiple_of(kb * FQ, FQ), c, False), (init, init))
        carry = tile(pl.multiple_of(i * FQ, FQ), carry, True)
        outs = []
        for a in range(2):
            m, l, acc = carry[a]
            outs.append(acc / l)
            l_ref[:, 128 * a:128 * a + 128] = jnp.broadcast_to(m + jnp.log(l), (FQ, 128))
        o_ref[...] = jnp.where(lo, outs[0], outs[1])

    return _pcall(
        body, name="fox_fwd", grid=(4, T // FQ),
        in_specs=_fox_specs() + [pl.BlockSpec((FQ, 128), lambda h, i: (i, 0)),
                                 pl.BlockSpec((None, 2, T), lambda h, i: (h, 0, 0))],
        out_specs=[pl.BlockSpec((FQ, 128), lambda h, i: (i, h)), pl.BlockSpec((FQ, 256), lambda h, i: (i, h))],
        out_shape=[jax.ShapeDtypeStruct((T, 512), F32), jax.ShapeDtypeStruct((T, 1024), F32)],
        compiler_params=pltpu.CompilerParams(dimension_semantics=("arbitrary", "arbitrary"), vmem_limit_bytes=VMEM_BIG),
    )(proj, proj, proj, c, ct3)


def _fox_bwd(proj, c, ct3, o, lse, do, after=()):
    def body(q_ref, k_ref, v_ref, c_ref, ct_ref, o_ref, l_ref, do_ref, dq_ref, dkb_ref, dvb_ref, dct_ref, dcq_ref,
             dk_ref, dv_ref):
        i = pl.program_id(1)

        @pl.when(i == 0)
        def _():
            dk_ref[...] = jnp.zeros_like(dk_ref)
            dv_ref[...] = jnp.zeros_like(dv_ref)
            dct_ref[...] = jnp.zeros_like(dct_ref)

        lo = _lane_lo(FQ)
        causal = lax.broadcasted_iota(jnp.int32, (FQ, FQ), 1) <= lax.broadcasted_iota(jnp.int32, (FQ, FQ), 0)
        q = q_ref[...]
        do_v = do_ref[...]
        prod = do_v * o_ref[...]
        qs = [_half(q, lo, a, SCALE) for a in range(2)]
        dos = [_half(do_v, lo, a) for a in range(2)]
        deltas = [jnp.sum(jnp.where(lo if a == 0 else jnp.logical_not(lo), prod, 0.0), axis=-1, keepdims=True)
                  for a in range(2)]
        cqs = [_lane_pick(c_ref[...], 2 * pl.program_id(0) + a) for a in range(2)]
        las = [l_ref[:, 128 * a:128 * a + 1] for a in range(2)]

        def tile(off, carry, diagonal):
            kblk = k_ref[pl.ds(off, FQ), :]
            vblk = v_ref[pl.ds(off, FQ), :]
            new = []
            dk = jnp.zeros((128, FQ), F32)
            dv = jnp.zeros((128, FQ), F32)
            for a in range(2):
                dq_acc, rs = carry[a]
                s = _nt(qs[a], kblk) + (cqs[a] - ct_ref[a:a + 1, pl.ds(off, FQ)])
                if diagonal:
                    s = jnp.where(causal, s, NEG)
                p = jnp.exp(s - las[a])
                ds = p * (_nt(dos[a], vblk) - deltas[a])
                dsb = ds.astype(BF16)
                dk = dk + _tn(qs[a], dsb)
                dv = dv + _tn(dos[a], p.astype(BF16))
                dct_ref[a:a + 1, pl.ds(off, FQ)] -= jnp.sum(ds, axis=0, keepdims=True)
                new.append((dq_acc + _nn(dsb, kblk), rs + jnp.sum(ds, axis=-1, keepdims=True)))
            dk_ref[:, pl.ds(off, FQ)] += dk
            dv_ref[:, pl.ds(off, FQ)] += dv
            return tuple(new)

        init = (jnp.zeros((FQ, 128), F32), jnp.zeros((FQ, 1), F32))
        carry = lax.fori_loop(0, i, lambda kb, c: tile(pl.multiple_of(kb * FQ, FQ), c, False), (init, init))
        carry = tile(pl.multiple_of(i * FQ, FQ), carry, True)
        lane = lax.broadcasted_iota(jnp.int32, (FQ, 128), 1)
        dcq_ref[...] = jnp.where(lane == 0, carry[0][1], jnp.where(lane == 1, carry[1][1], 0.0))
        dq_ref[...] = (jnp.where(lo, carry[0][0], carry[1][0]) * SCALE).astype(BF16)

        @pl.when(i == T // FQ - 1)
        def _():
            dkb_ref[...] = dk_ref[...].T.astype(BF16)
            dvb_ref[...] = dv_ref[...].T.astype(BF16)

    blk = pl.BlockSpec((FQ, 128), lambda h, i: (i, h))
    wide = pl.BlockSpec((FQ, 256), lambda h, i: (i, h))
    rows = pl.BlockSpec((None, 2, T), lambda h, i: (h, 0, 0))
    col = pl.BlockSpec((T, 128), lambda h, i: (0, h))
    return _pcall(
        _behind(body, 8, after), name="fox_bwd", grid=(4, T // FQ),
        in_specs=_fox_specs() + [pl.BlockSpec((FQ, 128), lambda h, i: (i, 0)), rows, blk, wide, blk] + [ANY_SPEC] * len(after),
        out_specs=[blk, col, col, rows, pl.BlockSpec((None, FQ, 128), lambda h, i: (h, i, 0))],
        out_shape=[jax.ShapeDtypeStruct((T, 512), BF16), jax.ShapeDtypeStruct((T, 512), BF16),
                   jax.ShapeDtypeStruct((T, 512), BF16), jax.ShapeDtypeStruct((4, 2, T), F32),
                   jax.ShapeDtypeStruct((4, T, 128), F32)],
        scratch_shapes=[pltpu.VMEM((128, T), F32), pltpu.VMEM((128, T), F32)],
        compiler_params=pltpu.CompilerParams(dimension_semantics=("arbitrary", "arbitrary"), vmem_limit_bytes=VMEM_BIG),
    )(proj, proj, proj, c, ct3, o, lse, do, *after)


def _rel_onehot():
    ridx = lax.broadcasted_iota(jnp.int32, (NREL_PAD, VW), 0)
    j = lax.broadcasted_iota(jnp.int32, (NREL_PAD, VW), 1)
    return jnp.where(ridx == jnp.clip(TQ + LEFT - 1 - j, -128, 128) + 128, 1.0, 0.0).astype(F32)


def _relvec_fwd(tbl):
    def body(t_ref, v_ref):
        v_ref[...] = _hdot(t_ref[...], _rel_onehot())

    return _one_call(body, "relvec_fwd", [tbl], [((8, VW), F32)])[0]


def _relvec_bwd(gv):
    def body(g_ref, t_ref):
        t_ref[...] = lax.dot_general(g_ref[...], _rel_onehot(), (((1,), (1,)), ((), ())),
                                     preferred_element_type=F32, precision=lax.Precision.HIGHEST)

    return _one_call(body, "relvec_bwd", [gv], [((8, NREL_PAD), F32)])[0]


def _chk_bias(vt_ref, a, hidden):
    vb = jnp.broadcast_to(vt_ref[a:a + 1, :], (TQ, VW))
    y = pltpu.roll(vb, VW - (TQ - 1), 1, stride=1, stride_axis=0)[:, :WIN]
    cr = lax.broadcasted_iota(jnp.int32, (TQ, WIN), 0) // 64
    m = lax.broadcasted_iota(jnp.int32, (TQ, WIN), 1)
    return jnp.where((m // 64 >= cr) & (m // 64 <= cr + 8) & (m >= hidden), y, NEG)


def _chk_specs():
    return [pl.BlockSpec((TQ, 128), lambda h, i: (i, CHK0 // 128 + h)),
            pl.BlockSpec((T + LEFT, 128), lambda h, i: (0, h)),
            pl.BlockSpec((T + LEFT, 128), lambda h, i: (0, 4 + h)),
            pl.BlockSpec((None, 2, VW), lambda h, i: (h, 0, 0))]


def _chk_fwd(proj, kvp, vt3, after=()):
    def body(q_ref, k_ref, v_ref, vt_ref, o_ref, l_ref, bias_ref):
        i = pl.program_id(1)

        @pl.when(i == 0)
        def _():
            for first in range(3):
                for a in range(2):
                    bias_ref[first, a] = _chk_bias(vt_ref, a, max(LEFT - first * TQ, 0))

        lo = _lane_lo()
        off = pl.multiple_of(i * TQ, TQ)
        kw = k_ref[pl.ds(off, WIN), :]
        vw = v_ref[pl.ds(off, WIN), :]
        bias_at = jnp.minimum(i, 2)
        q = q_ref[...]
        outs = []
        for a in range(2):
            s = _nt(_half(q, lo, a, SCALE), kw) + bias_ref[bias_at, a]
            m = jnp.max(s, axis=-1, keepdims=True)
            p = jnp.exp(s - m)
            l = jnp.sum(p, axis=-1, keepdims=True)
            outs.append(_nn(p.astype(BF16), vw) / l)
            l_ref[:, 128 * a:128 * a + 128] = jnp.broadcast_to(m + jnp.log(l), (TQ, 128))
        o_ref[...] = jnp.where(lo, outs[0], outs[1])

    return _pcall(
        _behind(body, 4, after), name="chk_fwd", grid=(4, T // TQ), in_specs=_chk_specs() + [ANY_SPEC] * len(after),
        out_specs=[pl.BlockSpec((TQ, 128), lambda h, i: (i, h)), pl.BlockSpec((TQ, 256), lambda h, i: (i, h))],
        out_shape=[jax.ShapeDtypeStruct((T, 512), F32), jax.ShapeDtypeStruct((T, 1024), F32)],
        scratch_shapes=[pltpu.VMEM((3, 2, TQ, WIN), F32)],
        compiler_params=pltpu.CompilerParams(dimension_semantics=("arbitrary", "arbitrary")),
    )(proj, kvp, kvp, vt3, *after)


def _chk_bwd(proj, kvp, vt3, o, lse, do, after=()):
    nq = T // TQ

    def body(q_ref, k_ref, v_ref, vt_ref, o_ref, l_ref, do_ref, dq_ref, dkb_ref, dvb_ref, gv_ref, bias_ref, dsum_ref,
             dk_ref, dv_ref):
        i = pl.program_id(1)

        @pl.when(i == 0)
        def _():
            for first in range(3):
                for a in range(2):
                    bias_ref[first, a] = _chk_bias(vt_ref, a, max(LEFT - first * TQ, 0))
            dsum_ref[...] = jnp.zeros_like(dsum_ref)
            dk_ref[...] = jnp.zeros_like(dk_ref)
            dv_ref[...] = jnp.zeros_like(dv_ref)

        lo = _lane_lo()
        off = pl.multiple_of(i * TQ, TQ)
        kw = k_ref[pl.ds(off, WIN), :]
        vw = v_ref[pl.ds(off, WIN), :]
        bias_at = jnp.minimum(i, 2)
        q = q_ref[...]
        do_v = do_ref[...]
        prod = do_v * o_ref[...]
        dqs = []
        for a in range(2):
            keep = lo if a == 0 else jnp.logical_not(lo)
            qa = _half(q, lo, a, SCALE)
            doa = _half(do_v, lo, a)
            delta = jnp.sum(jnp.where(keep, prod, 0.0), axis=-1, keepdims=True)
            s = _nt(qa, kw) + bias_ref[bias_at, a]
            p = jnp.exp(s - l_ref[:, 128 * a:128 * a + 1])
            ds = p * (_nt(doa, vw) - delta)
            dsum_ref[a] += ds
            dsb = ds.astype(BF16)
            dk_ref[:, pl.ds(off, WIN)] += _tn(qa, dsb)
            dv_ref[:, pl.ds(off, WIN)] += _tn(doa, p.astype(BF16))
            dqs.append(_nn(dsb, kw))
        dq_ref[...] = (jnp.where(lo, dqs[0], dqs[1]) * SCALE).astype(BF16)

        @pl.when(i == nq - 1)
        def _():
            dkb_ref[...] = dk_ref[:, LEFT:].T.astype(BF16)
            dvb_ref[...] = dv_ref[:, LEFT:].T.astype(BF16)
            rr = lax.broadcasted_iota(jnp.int32, (TQ, TQ), 0)
            cc = lax.broadcasted_iota(jnp.int32, (TQ, TQ), 1)
            flip = jnp.where(rr + cc == TQ - 1, 1.0, 0.0).astype(F32)
            for a in range(2):
                dpad = jnp.concatenate([dsum_ref[a], jnp.zeros((TQ, VW - WIN), F32)], axis=1)
                z = pltpu.roll(_hdot(flip, dpad), 0, 1, stride=1, stride_axis=0)
                gv_ref[a:a + 1, :] = jnp.sum(z, axis=0, keepdims=True)

    blk = pl.BlockSpec((TQ, 128), lambda h, i: (i, h))
    wide = pl.BlockSpec((TQ, 256), lambda h, i: (i, h))
    col = pl.BlockSpec((T, 128), lambda h, i: (0, h))
    return _pcall(
        _behind(body, 7, after), name="chk_bwd", grid=(4, nq), in_specs=_chk_specs() + [blk, wide, blk] + [ANY_SPEC] * len(after),
        out_specs=[blk, col, col, pl.BlockSpec((None, 2, VW), lambda h, i: (h, 0, 0))],
        out_shape=[jax.ShapeDtypeStruct((T, 512), BF16), jax.ShapeDtypeStruct((T, 512), BF16),
                   jax.ShapeDtypeStruct((T, 512), BF16), jax.ShapeDtypeStruct((4, 2, VW), F32)],
        scratch_shapes=[pltpu.VMEM((3, 2, TQ, WIN), F32), pltpu.VMEM((2, TQ, WIN), F32),
                        pltpu.VMEM((128, T + LEFT), F32), pltpu.VMEM((128, T + LEFT), F32)],
        compiler_params=pltpu.CompilerParams(dimension_semantics=("arbitrary", "arbitrary")),
    )(proj, kvp, kvp, vt3, o, lse, do, *after)


def _zero_at_start(*refs):
    @pl.when(pl.program_id(0) == 0)
    def _():
        for r in refs:
            r[...] = jnp.zeros_like(r)


def _ffn_step(h3, x2, tgt, w1_t, w2, g_post, g_pre):
    def body(h_ref, x2_ref, t_ref, w1_ref, w2_ref, gp_ref, gf_ref, dx2_ref, da_ref, dy_ref, r_ref, loss_ref, dgp_ref, dgf_ref):
        _zero_at_start(loss_ref, dgp_ref, dgf_ref)
        w1, w2v = _w(w1_ref), _w(w2_ref)
        ra = jnp.maximum(_nt(h_ref[...], w1), 0.0)
        r = jnp.square(ra).astype(BF16)
        r_ref[...] = r
        y = _nn(r, w2v)
        x2v = x2_ref[...]
        e = x2v + _rms(y, gp_ref[...]) - t_ref[...]
        loss_ref[...] += 0.5 * jnp.sum(jnp.sum(e * e, axis=-1, keepdims=True) * (1.0 / D))
        dx3 = e * (1.0 / D)
        dy, dgp = _rms_bwd(y, gp_ref[...], dx3)
        dgp_ref[...] += dgp
        dyb = dy.astype(BF16)
        dy_ref[...] = dyb
        da = (_nt(dyb, w2v) * (2.0 * ra)).astype(BF16)
        da_ref[...] = da
        dh, dgf = _rms_bwd(x2v, gf_ref[...], _nn(da, w1))
        dgf_ref[...] += dgf
        dx2_ref[...] = dx3 + dh

    return _tok_call(body, "ffn_step", [h3, x2, tgt], [w1_t, w2, g_post, g_pre],
                     [(D, F32), (DFF, BF16), (D, BF16), (DFF, BF16)], [(8, 128), (1, D), (1, D)], vmem=VMEM_BIG)


def _mem_bwd(dx2, ym, x1, qm, km, vm, w_mo, w_mq, g_post, g_pre, after=()):
    def body(dx2_ref, ym_ref, x1_ref, q_ref, k_ref, v_ref, wo_ref, wq_ref, gp_ref, gm_ref,
             dx1_ref, dym_ref, dq_ref, dk_ref, dv_ref, dgp_ref, dgm_ref, dom_ref):
        _zero_at_start(dk_ref, dv_ref, dgp_ref, dgm_ref)
        dx2_v = dx2_ref[...]
        dym, dgp = _rms_bwd(ym_ref[...], gp_ref[...], dx2_v)
        dgp_ref[...] += dgp
        dymb = dym.astype(BF16)
        dym_ref[...] = dymb
        dom_ref[...] = _nt(dymb, _w(wo_ref)).astype(BF16)
        for h in range(MEM_HEADS):
            sl = slice(h * MEM_HD, (h + 1) * MEM_HD)
            qh, kh, doh = q_ref[:, sl], k_ref[:, sl], dom_ref[:, sl]
            s = _nt(qh, kh) * MEM_SCALE
            p = jnp.exp(s - jnp.max(s, axis=-1, keepdims=True))
            p = p / jnp.sum(p, axis=-1, keepdims=True)
            dp = _nt(doh, v_ref[:, sl])
            ds = (p * (dp - jnp.sum(p * dp, axis=-1, keepdims=True))).astype(BF16)
            dq_ref[:, sl] = (_nn(ds, kh) * MEM_SCALE).astype(BF16)
            dk_ref[:, sl] += _tn(ds, qh) * MEM_SCALE
            dv_ref[:, sl] += _tn(p.astype(BF16), doh)
        dh, dgm = _rms_bwd(x1_ref[...], gm_ref[...], _nt(dq_ref[...], _w(wq_ref)))
        dgm_ref[...] += dgm
        dx1_ref[...] = dx2_v + dh

    tiled = pl.BlockSpec((TM_WIDE, D), lambda i: (i, 0))
    in_specs = [tiled] * 4 + [_resident(a) for a in (km, vm, w_mo, w_mq, g_post, g_pre)] + [ANY_SPEC] * len(after)
    w_mo, w_mq = w_mo[0], w_mq[0]
    kv = pl.BlockSpec((NMEM, D), lambda i: (0, 0))
    vec = pl.BlockSpec((1, D), lambda i: (0, 0))
    return _pcall(
        _behind(body, 10, after), name="mem_bwd", grid=(T // TM_WIDE,), in_specs=in_specs,
        out_specs=[tiled, tiled, tiled, kv, kv, vec, vec],
        out_shape=[jax.ShapeDtypeStruct((T, D), F32), jax.ShapeDtypeStruct((T, D), BF16),
                   jax.ShapeDtypeStruct((T, D), BF16), jax.ShapeDtypeStruct((NMEM, D), F32),
                   jax.ShapeDtypeStruct((NMEM, D), F32), jax.ShapeDtypeStruct((1, D), F32),
                   jax.ShapeDtypeStruct((1, D), F32)],
        scratch_shapes=[pltpu.VMEM((TM_WIDE, D), BF16)],
        compiler_params=pltpu.CompilerParams(dimension_semantics=("arbitrary",), vmem_limit_bytes=VMEM_BIG),
    )(dx2, ym, x1, qm, km, vm, w_mo, w_mq, g_post, g_pre, *after)


def _memkv_bwd(dkm, dvm, mem, w_mk, w_mv):
    def body(dk_ref, dv_ref, m_ref, wk_ref, wv_ref, dg_ref):
        dmn = _nt(dk_ref[...].astype(BF16), _w(wk_ref)) + _nt(dv_ref[...].astype(BF16), _w(wv_ref))
        mv = m_ref[...]
        dg_ref[...] = jnp.sum(dmn * (mv * _rstd(mv)), axis=0, keepdims=True)

    return _one_call(body, "memkv_bwd", [dkm, dvm, mem, w_mk, w_mv], [((1, D), F32)], vmem=VMEM_BIG)[0]


def _postmix_bwd(dx1, z, o_f, o_c, w_out, g_post, g_fo, g_co, after=()):
    def body(dx1_ref, z_ref, of_ref, oc_ref, wo_ref, gp_ref, gfo_ref, gco_ref,
             dz_ref, dof_ref, doc_ref, dgp_ref, dgfo_ref, dgco_ref):
        _zero_at_start(dgp_ref, dgfo_ref, dgco_ref)
        dz, dgp = _rms_bwd(z_ref[...], gp_ref[...], dx1_ref[...])
        dgp_ref[...] += dgp
        dzb = dz.astype(BF16)
        dz_ref[...] = dzb
        dy = _nt(dzb, _w(wo_ref))
        dof, dgfo = _rms_bwd(of_ref[...], gfo_ref[...], dy[:, :512])
        doc, dgco = _rms_bwd(oc_ref[...], gco_ref[...], dy[:, 512:])
        dof_ref[...] = dof
        doc_ref[...] = doc
        dgfo_ref[...] += dgfo
        dgco_ref[...] += dgco

    return _tok_call(body, "postmix_bwd", [dx1, z, o_f, o_c], [w_out, g_post, g_fo, g_co],
                     [(D, BF16), (512, F32), (512, F32)], [(1, D), (1, 512), (1, 512)], tm=TM_WIDE, vmem=VMEM_BIG,
                     after=after)


def _premix_bwd(dx1, x, pieces, win_t, g_pre, after=()):
    def body(dx1_ref, x_ref, *refs):
        piece_refs, (w_ref, g_ref, dx_ref, dp_ref, dg_ref) = refs[:len(pieces)], refs[len(pieces):]
        _zero_at_start(dg_ref)
        col = 0
        for p in piece_refs:
            dp_ref[:, col:col + p.shape[1]] = p[...]
            col += p.shape[1]
        dh, dg = _rms_bwd(x_ref[...], g_ref[...], _nn(dp_ref[...], w_ref[...]))
        dg_ref[...] += dg
        dx_ref[...] = dx1_ref[...] + dh

    return _tok_call(body, "premix_bwd", [dx1, x] + list(pieces), [win_t, g_pre], [(D, F32), (PROJ, BF16)], [(1, D)],
                     tm=TM_WIDE, vmem=VMEM_BIG, after=after)


def _wgrad_group(name, pairs, rows):
    def body(*refs):
        o_ref = refs[-1]
        for k in range(len(pairs)):
            g = _tn(refs[2 * k][...].astype(BF16), refs[2 * k + 1][...].astype(BF16))
            o_ref[k * rows:(k + 1) * rows, :] = g.astype(BF16)

    in_specs, ops = [], []
    for a, b in pairs:
        in_specs += [pl.BlockSpec((a.shape[0], rows), lambda j: (0, j)), _resident(b)]
        ops += [a, b]
    return _pcall(
        body, name=name, grid=(8,), in_specs=in_specs,
        out_specs=pl.BlockSpec((None, len(pairs) * rows, D), lambda j: (j, 0, 0)),
        out_shape=jax.ShapeDtypeStruct((8, len(pairs) * rows, D), BF16),
        compiler_params=pltpu.CompilerParams(dimension_semantics=("arbitrary",), vmem_limit_bytes=VMEM_BIG),
    )(*ops)


def _adam_math(w, g, m, v):
    m2 = ADAM_B1 * m + (1.0 - ADAM_B1) * g
    v2 = ADAM_B2 * v + (1.0 - ADAM_B2) * jnp.square(g)
    m_hat = m2 / (1.0 - ADAM_B1 ** ADAM_STEP)
    v_hat = v2 / (1.0 - ADAM_B2 ** ADAM_STEP)
    delta = -ADAM_LR * (m_hat / (jnp.sqrt(v_hat) + ADAM_EPS) + ADAM_WD * w)
    return delta, m2, v2


def _adamw_small(gparts, ws, ms, vs):
    n = len(SMALL)

    def body(g_ref, *refs):
        w_refs, m_refs, v_refs = refs[:n], refs[n:2 * n], refs[2 * n:3 * n]
        outs, sum_ref = refs[3 * n:-1], refs[-1]
        g = g_ref[0]
        for k in range(1, 8):
            g = g + g_ref[k]
        sum_ref[...] = g
        outs[0][...] = sum_ref[17:18, 0:128]
        for t, name in enumerate(SMALL):
            r0, nr, c0, nc = SMALL_SLOT[name]
            gt = sum_ref[r0:r0 + nr, c0:c0 + nc]
            out = (gt,) + _adam_math(w_refs[t][...], gt, m_refs[t][...], v_refs[t][...])
            for o_ref, val in zip(outs[1 + 4 * t:5 + 4 * t], out):
                o_ref[...] = val

    whole = lambda s: pl.BlockSpec(s, lambda i, nd=len(s): (0,) * nd)
    ins = [gparts] + list(ws) + list(ms) + list(vs)
    out_shapes = [(1, 128)] + [a.shape for a in ws for _ in range(4)]
    return _pcall(
        body, name="adamw_small", grid=(1,), in_specs=[whole(a.shape) for a in ins],
        out_specs=[whole(s) for s in out_shapes], out_shape=[jax.ShapeDtypeStruct(s, F32) for s in out_shapes],
        scratch_shapes=[pltpu.VMEM((SMALL_ROWS, D), F32)],
        compiler_params=pltpu.CompilerParams(dimension_semantics=("arbitrary",)),
    )(*ins)


def _row_tile(rows):
    return next(t for t in (512, 400, 320) if rows % t == 0)


def _add_halves(g4, theirs, core, name):
    rows = g4.shape[2]
    tr = _row_tile(rows)

    def body(c_ref, a_ref, b_ref, o_ref):
        o_ref[...] = (a_ref[...].astype(F32) + b_ref[...].astype(F32)).astype(BF16)

    grid_spec = pltpu.PrefetchScalarGridSpec(
        num_scalar_prefetch=1, grid=(4, rows // tr),
        in_specs=[pl.BlockSpec((None, None, tr, D), lambda j, i, c: (j, c[0], i, 0)),
                  pl.BlockSpec((None, None, tr, D), lambda j, i, c: (j, 0, i, 0))],
        out_specs=pl.BlockSpec((None, tr, D), lambda j, i, c: (j, i, 0)))
    return _pcall(
        body, name=name, grid_spec=grid_spec, out_shape=jax.ShapeDtypeStruct((4, rows, D), BF16),
        compiler_params=pltpu.CompilerParams(dimension_semantics=("arbitrary", "arbitrary")),
    )(core, g4, theirs)


def _sum_adam(own, got, order, r0, w, m, v, name, transposed=False):
    n = w.shape[1] if transposed else w.shape[0]
    tr = min(n, 256) if n % 8 == 0 else n
    rows = tr if n % 8 == 0 else own.shape[1]

    def body(o_ref, a_ref, b_ref, c_ref, d_ref, w_ref, m_ref, v_ref, g_ref, dl_ref, m2_ref, v2_ref):
        f = lambda r: r[0:tr, :].astype(F32)
        g = ((f(a_ref) + f(b_ref)) + f(c_ref)) + f(d_ref)
        g = g.T if transposed else g
        g_ref[...] = g
        dl_ref[...], m2_ref[...], v2_ref[...] = _adam_math(w_ref[...], g, m_ref[...], v_ref[...])

    slot = lambda k: pl.BlockSpec((None, rows, D), lambda i, o: (o[k], r0 // rows + i, 0))
    wspec = pl.BlockSpec((D, tr), lambda i, o: (0, i)) if transposed else pl.BlockSpec((tr, D), lambda i, o: (i, 0))
    grid_spec = pltpu.PrefetchScalarGridSpec(
        num_scalar_prefetch=1, grid=(n // tr,), in_specs=[slot(0), slot(1), slot(2), slot(3), wspec, wspec, wspec],
        out_specs=[wspec] * 4)
    return _pcall(
        body, name=name, grid_spec=grid_spec, out_shape=[jax.ShapeDtypeStruct(w.shape, F32)] * 4,
        compiler_params=pltpu.CompilerParams(dimension_semantics=("arbitrary",)),
    )(order, own, got, got, got, w, m, v)


def _sum_adam_rows(own, got, order, ws, ms, vs, name):
    n, rows = len(ws), ws[0].shape[0]

    def body(o_ref, a_ref, b_ref, c_ref, d_ref, *refs):
        ins, outs = refs[:3 * n], refs[3 * n:]
        for t in range(n):
            r = slice(t * rows, (t + 1) * rows)
            f = lambda ref: ref[r, :].astype(F32)
            g = ((f(a_ref) + f(b_ref)) + f(c_ref)) + f(d_ref)
            out = (g,) + _adam_math(ins[t][...], g, ins[n + t][...], ins[2 * n + t][...])
            for o, val in zip(outs[4 * t:4 * t + 4], out):
                o[...] = val

    slot = lambda k: pl.BlockSpec((None, n * rows, D), lambda i, o: (o[k], 0, 0), pipeline_mode=pl.Buffered(1))
    wspec = pl.BlockSpec((rows, D), lambda i, o: (0, 0), pipeline_mode=pl.Buffered(1))
    grid_spec = pltpu.PrefetchScalarGridSpec(
        num_scalar_prefetch=1, grid=(1,), in_specs=[slot(0), slot(1), slot(2), slot(3)] + [wspec] * (3 * n),
        out_specs=[pl.BlockSpec((rows, D), lambda i, o: (0, 0))] * (4 * n))
    return _pcall(
        body, name=name, grid_spec=grid_spec, out_shape=[jax.ShapeDtypeStruct((rows, D), F32)] * (4 * n),
        compiler_params=pltpu.CompilerParams(dimension_semantics=("arbitrary",), vmem_limit_bytes=VMEM_BIG),
    )(order, own, got, got, got, *ws, *ms, *vs)


def _place():
    return lax.axis_index("x"), lax.axis_index("y"), lax.axis_index("c")


def _allgather(block, name, after=()):
    rows = block.shape[0]
    split = (rows // 2 + 15) // 16 * 16

    def body(x_ref, out_ref, token, send_sems, recv_sems, local_sem):
        token[...] = jnp.zeros_like(token)
        x, y, c = _place()
        me, sib = (x, y, c), (x, y, 1 - c)
        xn, yn, dg = (1 - x, y), (x, 1 - y), (1 - x, 1 - y)
        lo, hi = pl.ds(0, split), pl.ds(split, rows - split)

        def copy(k, blk, to, part=None, src=None):
            index = 4 * blk[0] + 2 * blk[1] + blk[2]
            view = out_ref.at[index] if part is None else out_ref.at[index, part]
            return pltpu.make_async_remote_copy(
                src_ref=view if src is None else src, dst_ref=view,
                send_sem=send_sems.at[k], recv_sem=recv_sems.at[k], device_id=to, device_id_type=MESH)

        def start(*copies):
            for cp in copies:
                cp.start()
            return list(copies)

        mine = pltpu.make_async_copy(x_ref, out_ref.at[4 * x + 2 * y + c], local_sem)
        mine.start()
        sent = start(copy(0, me, sib, src=x_ref), copy(1, me, (*xn, c), src=x_ref), copy(2, me, (*yn, c), src=x_ref))
        copy(1, (*xn, c), me).wait_recv()
        sent += start(copy(3, (*xn, c), sib), copy(5, (*xn, c), (*yn, c), part=lo))
        copy(2, (*yn, c), me).wait_recv()
        sent += start(copy(4, (*yn, c), sib), copy(6, (*yn, c), (*xn, c), part=hi))
        copy(5, (*dg, c), me, part=lo).wait_recv()
        copy(6, (*dg, c), me, part=hi).wait_recv()
        sent += start(copy(7, (*dg, c), sib))
        for k, blk in ((0, sib), (3, (*xn, 1 - c)), (4, (*yn, 1 - c)), (7, (*dg, 1 - c))):
            copy(k, blk, me).wait_recv()
        for cp in sent:
            cp.wait_send()
        mine.wait()

    return _pcall(
        _behind(body, 1, after), name=name,
        out_shape=[jax.ShapeDtypeStruct((8,) + block.shape, block.dtype), jax.ShapeDtypeStruct((8, 128), F32)],
        in_specs=[pl.BlockSpec(memory_space=pl.ANY)] * (1 + len(after)),
        out_specs=[pl.BlockSpec(memory_space=pl.ANY), pl.BlockSpec(memory_space=pltpu.VMEM)],
        scratch_shapes=[pltpu.SemaphoreType.DMA((8,)), pltpu.SemaphoreType.DMA((8,)), pltpu.SemaphoreType.DMA(())],
        compiler_params=pltpu.CompilerParams(has_side_effects=True),
    )(block, *after)


HBM_SPEC = pl.BlockSpec(memory_space=pltpu.HBM)
SEM_SPEC = pl.BlockSpec(memory_space=pltpu.SEMAPHORE)
ANY_SPEC = pl.BlockSpec(memory_space=pl.ANY)
EFFECT = pltpu.SideEffectType.DATAFLOW_SIDE_EFFECTING


def _in_hbm(a):
    return pltpu.with_memory_space_constraint(a, pltpu.HBM)


def _start_copies(name, src, land_shape, plan, n):
    def body(src_ref, land_ref, send_sems, recv_sems, src_thru, land_thru, token):
        for k, (s, d, to, _) in enumerate(plan(src_ref, land_ref)):
            pltpu.make_async_remote_copy(src_ref=s, dst_ref=d, send_sem=send_sems.at[k], recv_sem=recv_sems.at[k],
                                         device_id=to, device_id_type=MESH).start()
        token[...] = jnp.zeros_like(token)

    return _pcall(
        body, name=name,
        out_shape=(pltpu.SemaphoreType.DMA((n,)), pltpu.SemaphoreType.DMA((n,)), pltpu.HBM(src.shape, src.dtype),
                   pltpu.HBM(land_shape, src.dtype), jax.ShapeDtypeStruct((8, 128), F32)),
        in_specs=(HBM_SPEC, HBM_SPEC),
        out_specs=(SEM_SPEC, SEM_SPEC, HBM_SPEC, HBM_SPEC, pl.BlockSpec(memory_space=pltpu.VMEM)),
        input_output_aliases={0: 2, 1: 3}, compiler_params=pltpu.CompilerParams(has_side_effects=EFFECT),
    )(_in_hbm(src), _in_hbm(lax.empty(land_shape, src.dtype)))


def _wait_copies(name, started, after, plan):
    send_sems, recv_sems, src_thru, land_thru, _ = started

    def body(src_ref, land_ref, send_sems, recv_sems, *rest):
        for k, (s, _, to, mine) in enumerate(plan(src_ref, land_ref)):
            cp = pltpu.make_async_remote_copy(src_ref=s, dst_ref=mine, send_sem=send_sems.at[k],
                                              recv_sem=recv_sems.at[k], device_id=to, device_id_type=MESH)
            cp.wait_send()
            cp.wait_recv()

    return _pcall(
        body, name=name,
        out_shape=(pltpu.HBM(src_thru.shape, src_thru.dtype), pltpu.HBM(land_thru.shape, land_thru.dtype)),
        in_specs=(HBM_SPEC, HBM_SPEC, SEM_SPEC, SEM_SPEC) + (ANY_SPEC,) * len(after), out_specs=(HBM_SPEC, HBM_SPEC),
        input_output_aliases={0: 0, 1: 1}, compiler_params=pltpu.CompilerParams(has_side_effects=EFFECT),
    )(src_thru, land_thru, send_sems, recv_sems, *after)


def _start_inplace(name, buf, plan, n):
    def body(buf_ref, send_sems, recv_sems, buf_thru, token):
        for k, (s, d, to, _) in enumerate(plan(buf_ref, buf_ref)):
            pltpu.make_async_remote_copy(src_ref=s, dst_ref=d, send_sem=send_sems.at[k], recv_sem=recv_sems.at[k],
                                         device_id=to, device_id_type=MESH).start()
        token[...] = jnp.zeros_like(token)

    return _pcall(
        body, name=name,
        out_shape=(pltpu.SemaphoreType.DMA((n,)), pltpu.SemaphoreType.DMA((n,)), pltpu.HBM(buf.shape, buf.dtype),
                   jax.ShapeDtypeStruct((8, 128), F32)),
        in_specs=(HBM_SPEC,), out_specs=(SEM_SPEC, SEM_SPEC, HBM_SPEC, pl.BlockSpec(memory_space=pltpu.VMEM)),
        input_output_aliases={0: 2}, compiler_params=pltpu.CompilerParams(has_side_effects=EFFECT),
    )(_in_hbm(buf))


def _wait_inplace(name, started, after, plan):
    send_sems, recv_sems, buf_thru, _ = started

    def body(buf_ref, send_sems, recv_sems, *rest):
        for k, (s, _, to, mine) in enumerate(plan(buf_ref, buf_ref)):
            cp = pltpu.make_async_remote_copy(src_ref=s, dst_ref=mine, send_sem=send_sems.at[k],
                                              recv_sem=recv_sems.at[k], device_id=to, device_id_type=MESH)
            cp.wait_send()
            cp.wait_recv()

    return _pcall(
        body, name=name, out_shape=pltpu.HBM(buf_thru.shape, buf_thru.dtype),
        in_specs=(HBM_SPEC, SEM_SPEC, SEM_SPEC) + (ANY_SPEC,) * len(after), out_specs=HBM_SPEC,
        input_output_aliases={0: 0}, compiler_params=pltpu.CompilerParams(has_side_effects=EFFECT),
    )(buf_thru, send_sems, recv_sems, *after)


def _gather_plan(src_ref, land_ref):
    x, y, c = _place()
    peers = [(x, y, 1 - c), (1 - x, y, c), (x, 1 - y, c)]
    return [(src_ref, land_ref.at[4 * x + 2 * y + c], p, land_ref.at[4 * p[0] + 2 * p[1] + p[2]]) for p in peers]


def _relay_plan(buf_ref, _):
    x, y, c = _place()
    slot = lambda p, pc: 4 * p[0] + 2 * p[1] + pc
    xn, yn, dg, sib = (1 - x, y), (x, 1 - y), (1 - x, 1 - y), (x, y, 1 - c)
    half = buf_ref.shape[1] // 2
    lo, hi = pl.ds(0, half), pl.ds(half, half)
    return [(buf_ref.at[slot(xn, c)], buf_ref.at[slot(xn, c)], sib, buf_ref.at[slot(xn, 1 - c)]),
            (buf_ref.at[slot(yn, c)], buf_ref.at[slot(yn, c)], sib, buf_ref.at[slot(yn, 1 - c)]),
            (buf_ref.at[slot(xn, c), lo], buf_ref.at[slot(xn, c), lo], (*yn, c), buf_ref.at[slot(dg, c), lo]),
            (buf_ref.at[slot(yn, c), hi], buf_ref.at[slot(yn, c), hi], (*xn, c), buf_ref.at[slot(dg, c), hi])]


def _swap_plan(src_ref, land_ref):
    x, y, c = _place()
    return [(src_ref.at[:, pl.ds(1 - c, 1)], land_ref, (x, y, 1 - c), land_ref)]


def _exchange_plan(src_ref, land_ref):
    x, y, c = _place()
    chips = [(1 - x, y), (x, 1 - y), (1 - x, 1 - y)]
    return [(src_ref.at[2 * px + py], land_ref.at[2 * x + y], (px, py, c), land_ref.at[2 * px + py]) for px, py in chips]


def _gather_forward(land, block):
    def body(land_ref, out_ref, send_sems, recv_sems):
        x, y, c = _place()
        chips = [(1 - x, 1 - y)]

        def copy(k, px, py, pc):
            blk = out_ref.at[4 * px + 2 * py + pc]
            return pltpu.make_async_remote_copy(src_ref=blk, dst_ref=blk, send_sem=send_sems.at[k],
                                                recv_sem=recv_sems.at[k], device_id=(x, y, 1 - c), device_id_type=MESH)

        sent = [copy(k, px, py, c) for k, (px, py) in enumerate(chips)]
        for cp in sent:
            cp.start()
        for k, (px, py) in enumerate(chips):
            copy(k, px, py, 1 - c).wait_recv()
        for cp in sent:
            cp.wait_send()

    land = _pcall(
        body, name="allgather_rest_forward", out_shape=jax.ShapeDtypeStruct(land.shape, land.dtype),
        in_specs=[ANY_SPEC], out_specs=ANY_SPEC, input_output_aliases={0: 0},
        scratch_shapes=[pltpu.SemaphoreType.DMA((1,)), pltpu.SemaphoreType.DMA((1,))],
        compiler_params=pltpu.CompilerParams(has_side_effects=True),
    )(land)

    rows = block.shape[0]
    tr = rows // 4

    def place(me_ref, x_ref, land_ref, out_ref):
        out_ref[...] = x_ref[...]

    x, y, c = _place()
    grid_spec = pltpu.PrefetchScalarGridSpec(
        num_scalar_prefetch=1, grid=(rows // tr,),
        in_specs=[pl.BlockSpec((tr, D), lambda i, me: (i, 0)), ANY_SPEC],
        out_specs=pl.BlockSpec((None, tr, D), lambda i, me: (me[0], i, 0)))
    return _pcall(
        place, name="allgather_rest_own", grid_spec=grid_spec, out_shape=jax.ShapeDtypeStruct(land.shape, land.dtype),
        input_output_aliases={2: 0}, compiler_params=pltpu.CompilerParams(dimension_semantics=("arbitrary",)),
    )((4 * x + 2 * y + c).reshape(1), block, land)


class _ReduceScatter:
    def __init__(self, name, g):
        self.name = name
        rows = g.shape[1]
        self.started = _start_copies(name + "_swap_start", g.reshape(4, 2, rows, D), (4, 1, rows, D), _swap_plan, 1)
        self.token = self.started[4]

    def halfway(self, after):
        g4, theirs = _wait_copies(self.name + "_swap_wait", self.started, after, _swap_plan)
        self.own = _add_halves(g4, theirs, lax.axis_index("c").reshape(1), self.name + "_add_halves")
        self.started = _start_copies(self.name + "_exch_start", self.own, self.own.shape, _exchange_plan, 3)
        self.token = self.started[4]

    def finish(self, after):
        own, got = _wait_copies(self.name + "_exch_wait", self.started, after, _exchange_plan)
        chip = 2 * lax.axis_index("x") + lax.axis_index("y")
        return own, got, (chip + jnp.arange(4, dtype=jnp.int32)) % 4


def _pack_small(p, loss):
    def body(*refs):
        o_ref = refs[-1]
        o_ref[...] = jnp.zeros_like(o_ref)
        for ref, name in zip(refs, SMALL):
            r0, nr, c0, nc = SMALL_SLOT[name]
            o_ref[r0:r0 + nr, c0:c0 + nc] = ref[...]
        o_ref[17:18, 0:128] = refs[len(SMALL)][0:1, :]

    return _one_call(body, "pack_small_grads", [p[n] for n in SMALL] + [loss], [((SMALL_ROWS, D), F32)])[0]


_GAP_DEV, _GAP_ROW = divmod(GATE0 + 8, N_IN)
_GAP = CHK0 - GATE0 - 8


def _in_rows_to_proj(g):
    runs = [(j, 0, N_IN, N_IN * j) for j in range(_GAP_DEV)]
    runs += [(_GAP_DEV, 0, _GAP_ROW, N_IN * _GAP_DEV), (_GAP_DEV, _GAP_ROW, N_IN, N_IN * _GAP_DEV + _GAP_ROW + _GAP)]
    runs += [(j, 0, N_IN, N_IN * j + _GAP) for j in range(_GAP_DEV + 1, 8)]

    def body(g_ref, o_ref, acc_ref):
        acc_ref[...] = jnp.zeros_like(acc_ref)
        for j, r0, r1, dest in runs:
            start, shift = dest // 16 * 16, dest % 16
            win = -(-(shift + r1 - r0) // 16) * 16
            r = lax.broadcasted_iota(jnp.int32, (win, R_IN), 0)
            c = lax.broadcasted_iota(jnp.int32, (win, R_IN), 1)
            move = jnp.where((c >= r0) & (c < r1) & (r == c - r0 + shift), 1.0, 0.0).astype(BF16)
            acc_ref[start:start + win, :] += _nn(move, g_ref[j])
        o_ref[...] = acc_ref[...].astype(BF16)

    return _pcall(
        body, name="w_in_layout", out_shape=jax.ShapeDtypeStruct((PROJ, D), BF16), grid=(1,),
        in_specs=[pl.BlockSpec(g.shape, lambda i: (0, 0, 0))], out_specs=pl.BlockSpec((PROJ, D), lambda i: (0, 0)),
        scratch_shapes=[pltpu.VMEM((PROJ, D), F32)],
        compiler_params=pltpu.CompilerParams(dimension_semantics=("arbitrary",), vmem_limit_bytes=VMEM_BIG),
    )(g)


def _wgrad_in(pieces, h1):
    n = len(pieces)
    ends = [sum(p.shape[1] for p in pieces[:k + 1]) for k in range(n)]
    assert ends[-1] == PROJ

    def body(*refs):
        piece_refs, (b_ref, o_ref, g_ref), bufs, sem = refs[:n], refs[n:n + 3], refs[n + 3:2 * n + 3], refs[2 * n + 3]
        i = pl.program_id(0)
        copies = [pltpu.make_async_copy(piece_refs[k], bufs[k], sem.at[k]) for k in range(n)]

        @pl.when(i == 0)
        def _():
            for copy in copies:
                copy.start()
            g_ref[PROJ:, :] = jnp.zeros((R_IN - N_IN + 1, D), F32)

        for k in range(n):
            @pl.when(i == k)
            def _(k=k):
                copies[k].wait()
                g_ref[ends[k] - pieces[k].shape[1]:ends[k], :] = _tn(bufs[k][...], b_ref[...])

        row = lax.broadcasted_iota(jnp.int32, (R_IN, D), 0)
        for j in range(8):
            lo = N_IN * j + (_GAP if j > _GAP_DEV else 0)
            hi = N_IN * j + (_GAP if j >= _GAP_DEV else 0)
            ready = min(k for k in range(n) if ends[k] >= min(hi + R_IN, PROJ))

            @pl.when(i == ready)
            def _(j=j, lo=lo, hi=hi):
                v = g_ref[hi:hi + R_IN, :]
                if lo != hi:
                    v = jnp.where(row < _GAP_ROW, g_ref[lo:lo + R_IN, :], v)
                o_ref[j] = jnp.where(row < N_IN, v, 0.0).astype(BF16)

    return _pcall(
        body, name="wgrad_in", grid=(n,),
        in_specs=[ANY_SPEC] * n + [_resident(h1)],
        out_specs=pl.BlockSpec((8, R_IN, D), lambda i: (0, 0, 0)),
        out_shape=jax.ShapeDtypeStruct((8, R_IN, D), BF16),
        scratch_shapes=[pltpu.VMEM((PROJ + R_IN - N_IN + 1, D), F32)] + [pltpu.VMEM(p.shape, BF16) for p in pieces]
        + [pltpu.SemaphoreType.DMA((n,))],
        compiler_params=pltpu.CompilerParams(dimension_semantics=("arbitrary",), vmem_limit_bytes=VMEM_BIG),
    )(*pieces, h1)


def _local_grads(x, mem, tgt, win_t, gw_of, sm, on_grads, after=()):
    b_pad = jnp.pad(sm['b_fgt'], ((0, 0), (0, 120)))
    tbl = jnp.pad(sm['rel_bias'], ((0, 0), (0, NREL_PAD - 257)))

    h1, proj, flog = _premix_fwd(x, sm['g_mix_pre'], win_t, after)
    c = _gate_fwd(flog, b_pad)
    ct3 = c[:, :8].T.reshape(4, 2, T)
    o_f, lse_f = _fox_fwd(proj, c, ct3)
    vt3 = _relvec_fwd(tbl).reshape(4, 2, VW)
    kvp = jnp.pad(proj[:, CHK0 + 512:], ((LEFT, 0), (0, 0)))
    o_c, lse_c = _chk_fwd(proj, kvp, vt3, [gw_of('relay', [o_f])])
    gw = gw_of('done', [o_c])
    w_out, w_mq, w_mk, w_mv, w_mo, w1_t, w2 = (_wblk(gw, n) for n in ('w_out', 'w_mq', 'w_mk', 'w_mv', 'w_mo', 'w_ff1', 'w_ff2'))
    ycat, z, x1, h2, qm = _postmix_fwd(x, o_f, o_c, sm['g_fox_out'], sm['g_chk_out'], w_out,
                                       sm['g_mix_post'], sm['g_mem_pre'], w_mq)
    memn, km, vm = _memkv_fwd(mem, sm['g_mem_kv'], w_mk, w_mv)
    om, ym, x2, h3 = _mem_fwd(qm, x1, km, vm, w_mo, sm['g_mem_post'], sm['g_ff_pre'])

    gs = {}
    dx2, da, dy3, r, loss_acc, gs['g_ff_post'], gs['g_ff_pre'] = _ffn_step(h3, x2, tgt, w1_t, w2, sm['g_ff_post'],
                                                                         sm['g_ff_pre'])
    tok = on_grads('A', _wgrad_group("wgrad_ff", [(da, h3), (r, dy3)], 512), None)
    dx1, dym, dqm, dkm, dvm, gs['g_mem_post'], gs['g_mem_pre'] = _mem_bwd(
        dx2, ym, x1, qm, km, vm, w_mo, w_mq, sm['g_mem_post'], sm['g_mem_pre'], [tok])
    tok = on_grads('A halfway', None, [dx1])
    gs['g_mem_kv'] = _memkv_bwd(dkm, dvm, mem, w_mk, w_mv)
    dz, dof, doc, gs['g_mix_post'], gs['g_fox_out'], gs['g_chk_out'] = _postmix_bwd(
        dx1, z, o_f, o_c, w_out, sm['g_mix_post'], sm['g_fox_out'], sm['g_chk_out'], [tok])
    tok = on_grads('B', _wgrad_group("wgrad_mem_out", [(ycat, dz), (h2, dqm), (memn, dkm), (memn, dvm), (om, dym)], 128), None)
    dq_f, dk_f, dv_f, dct, dcq = _fox_bwd(proj, c, ct3, o_f, lse_f, dof, [tok])
    tok = on_grads('B halfway', None, [dq_f])
    dq_c, dk_c, dv_c, gv = _chk_bwd(proj, kvp, vt3, o_c, lse_c, doc, [tok])
    gs['rel_bias'] = _relvec_bwd(gv.reshape(8, VW))[:, :257]
    dc = jnp.pad(dct.reshape(8, T).T + dcq[:, :, :2].transpose(1, 0, 2).reshape(T, 8), ((0, 0), (0, 120)))
    dflog, db = _gate_bwd(dc, flog, b_pad)
    gs['b_fgt'] = db[0:1, :8]
    pieces = [dq_f, dk_f, dv_f, dflog, dq_c, dk_c, dv_c]
    on_grads('C', _wgrad_in(pieces, h1), None)
    tok = on_grads('C halfway', None, [gs['g_mem_kv']])
    grad_x, _, gs['g_mix_pre'] = _premix_bwd(dx1, x, pieces, win_t, sm['g_mix_pre'], [tok])
    return loss_acc, grad_x, gs


def kernel(x, mem, w_in, b_fgt, rel_bias, g_fox_out, g_chk_out, w_out, g_mix_pre, g_mix_post, g_mem_kv, w_mq, w_mk, w_mv, w_mo, g_mem_pre, g_mem_post, w_ff1, w_ff2, g_ff_pre, g_ff_post, loss_target, m_w_in, m_b_fgt, m_rel_bias, m_g_fox_out, m_g_chk_out, m_w_out, m_g_mix_pre, m_g_mix_post, m_g_mem_kv, m_w_mq, m_w_mk, m_w_mv, m_w_mo, m_g_mem_pre, m_g_mem_post, m_w_ff1, m_w_ff2, m_g_ff_pre, m_g_ff_post, v_w_in, v_b_fgt, v_rel_bias, v_g_fox_out, v_g_chk_out, v_w_out, v_g_mix_pre, v_g_mix_post, v_g_mem_kv, v_w_mq, v_w_mk, v_w_mv, v_w_mo, v_g_mem_pre, v_g_mem_post, v_w_ff1, v_w_ff2, v_g_ff_pre, v_g_ff_post):
    args = dict(locals())
    two_d = lambda a: a.reshape(a.shape[-2:])
    w = {n: two_d(args[n]) for n in WEIGHTS}
    m = {n: two_d(args['m_' + n]) for n in WEIGHTS}
    v = {n: two_d(args['v_' + n]) for n in WEIGHTS}

    sm = {n: w[n] for n in SMALL}
    shard_in = jnp.pad(w['w_in'].T, ((0, R_IN - N_IN), (0, 0))).astype(BF16)
    gathered_in, zero = _allgather(shard_in, "allgather_w_in")
    win_t = _in_rows_to_proj(gathered_in)
    shard_rest = (jnp.concatenate([w['w_ff1'].T, w['w_ff2'], w['w_out'], w['w_mq'], w['w_mk'], w['w_mv'], w['w_mo']],
                                  axis=0) + zero[0, 0]).astype(BF16)
    gather = {'first': _start_copies("allgather_rest_start", shard_rest, (8, R_REST, D), _gather_plan, 3)}

    def gw_of(stage, after):
        if stage == 'relay':
            gather['block'], land = _wait_copies("allgather_rest_wait", gather['first'], after, _gather_plan)
            gather['second'] = _start_inplace("allgather_rest_relay_start", land, _relay_plan, 4)
            return gather['second'][3]
        land = _wait_inplace("allgather_rest_relay_wait", gather['second'], after, _relay_plan)
        return _gather_forward(land, gather['block'])

    rs = {}

    def on_grads(stage, g, after):
        if stage.endswith('halfway'):
            rs[stage[0]].halfway(after)
            return rs[stage[0]].token
        rs[stage] = _ReduceScatter("rs_" + stage.lower(), g)
        return rs[stage].token

    loss_local, grad_x, gs = _local_grads(x[0], mem[0], loss_target[0], win_t, gw_of, sm, on_grads, [gather['first'][4]])
    grads, deltas, new_m, new_v = {}, {}, {}, {}

    def update(n, out):
        grads[n], deltas[n], new_m[n], new_v[n] = out

    own, got, order = rs['A'].finish([grad_x, rs['C'].token])
    update('w_ff1', _sum_adam(own, got, order, 0, w['w_ff1'], m['w_ff1'], v['w_ff1'], "adamw_w_ff1", transposed=True))
    update('w_ff2', _sum_adam(own, got, order, 512, w['w_ff2'], m['w_ff2'], v['w_ff2'], "adamw_w_ff2"))
    own, got, order = rs['B'].finish([grad_x, rs['C'].token])
    names_b = ('w_out', 'w_mq', 'w_mk', 'w_mv', 'w_mo')
    done = _sum_adam_rows(own, got, order, [w[n] for n in names_b], [m[n] for n in names_b], [v[n] for n in names_b],
                          "adamw_group_b")
    for k, n in enumerate(names_b):
        update(n, done[4 * k:4 * k + 4])

    own, got, order = rs['C'].finish([new_v[n] for n in BIG if n != 'w_in'])
    done = _sum_adam(own, got, order, 0, w['w_in'].T, m['w_in'].T, v['w_in'].T, "adamw_w_in")
    update('w_in', [a.T for a in done])

    gparts, _ = _allgather(_pack_small(gs, loss_local), "allgather_small_grads", [got])
    small = _adamw_small(gparts, [w[n] for n in SMALL], [m[n] for n in SMALL], [v[n] for n in SMALL])
    loss = small[0][0, 0]
    for t, n in enumerate(SMALL):
        update(n, small[1 + 4 * t:5 + 4 * t])

    out = [loss, grad_x[None]]
    for group in (grads, deltas, new_m, new_v):
        out += [group[n].reshape(args[n].shape) for n in WEIGHTS]
    return tuple(out)
```

```python
import jax
import jax.numpy as jnp
from jax import lax
from jax.experimental import pallas as pl
from jax.experimental.pallas import tpu as pltpu

F32 = jnp.float32
BF16 = jnp.bfloat16
MESH = pl.DeviceIdType.MESH

T = 2048
D = 1024
NMEM = 256
DFF = 4096
EPS = 1e-6
TM = 256
TM_WIDE = 512
TQ = 256
FQ = 512
HD = 64
SCALE = HD ** -0.5
MEM_HEADS = 4
MEM_HD = 256
MEM_SCALE = MEM_HD ** -0.5
NEG = -1e30
LEFT = 512
WIN = LEFT + TQ
VW = 1024
NREL_PAD = 384
PROJ = 3200
GATE0 = 1536
CHK0 = 1664
VMEM_BIG = 56 * 1024 * 1024

ADAM_LR = 0.001
ADAM_B1 = 0.9
ADAM_B2 = 0.999
ADAM_EPS = 1e-08
ADAM_WD = 0.01
ADAM_STEP = 10

N_IN = 385
R_IN = 400
R_REST = 1664
W_ROWS = {'w_ff1': (0, 512), 'w_ff2': (512, 512),
          'w_out': (1024, 128), 'w_mq': (1152, 128), 'w_mk': (1280, 128), 'w_mv': (1408, 128), 'w_mo': (1536, 128)}
SMALL_ROWS = 24
SMALL_SLOT = {'rel_bias': (0, 8, 0, 257), 'b_fgt': (8, 1, 0, 8), 'g_fox_out': (9, 1, 0, 512), 'g_chk_out': (9, 1, 512, 512),
              'g_mix_pre': (10, 1, 0, 1024), 'g_mix_post': (11, 1, 0, 1024), 'g_mem_kv': (12, 1, 0, 1024),
              'g_mem_pre': (13, 1, 0, 1024), 'g_mem_post': (14, 1, 0, 1024), 'g_ff_pre': (15, 1, 0, 1024),
              'g_ff_post': (16, 1, 0, 1024)}

WEIGHTS = ['w_in', 'b_fgt', 'rel_bias', 'g_fox_out', 'g_chk_out', 'w_out', 'g_mix_pre', 'g_mix_post', 'g_mem_kv',
           'w_mq', 'w_mk', 'w_mv', 'w_mo', 'g_mem_pre', 'g_mem_post', 'w_ff1', 'w_ff2', 'g_ff_pre', 'g_ff_post']
BIG = ['w_in', 'w_out', 'w_mq', 'w_mk', 'w_mv', 'w_mo', 'w_ff1', 'w_ff2']
SMALL = [n for n in WEIGHTS if n not in BIG]


def _pcall(body, **kw):
    return pl.pallas_call(body, **kw)


def _nn(a, b):
    return jnp.dot(a, b, preferred_element_type=F32)


def _nt(a, b):
    return lax.dot_general(a, b, (((1,), (1,)), ((), ())), preferred_element_type=F32)


def _tn(a, b):
    return lax.dot_general(a, b, (((0,), (0,)), ((), ())), preferred_element_type=F32)


def _w(ref):
    v = ref[...]
    return v if v.ndim == 2 else v.reshape(-1, v.shape[-1])


def _rstd(x):
    return lax.rsqrt(jnp.mean(x * x, axis=-1, keepdims=True) + EPS)


def _rms(x, g):
    return x * _rstd(x) * g


def _rms_bwd(x, g, dy):
    r = _rstd(x)
    xh = x * r
    dg = jnp.sum(dy * xh, axis=0, keepdims=True)
    dxh = dy * g
    dx = r * (dxh - xh * jnp.mean(dxh * xh, axis=-1, keepdims=True))
    return dx, dg


def _resident(a):
    if isinstance(a, tuple):
        _, shape, index = a
        return pl.BlockSpec(shape, lambda *_: index, pipeline_mode=pl.Buffered(1))
    return pl.BlockSpec(a.shape, lambda *_, nd=a.ndim: (0,) * nd, pipeline_mode=pl.Buffered(1))


def _wblk(gw, name):
    r0, rows = W_ROWS[name]
    return (gw, (8, rows, D), (0, r0 // rows, 0))


def _behind(body, n_in, after):
    if not after:
        return body
    return lambda *refs: body(*refs[:n_in], *refs[n_in + len(after):])


def _tok_call(body, name, tiled, full, outs_tiled, outs_acc=(), rows=T, tm=TM, vmem=None, after=()):
    in_specs = [pl.BlockSpec((tm, a.shape[1]), lambda i: (i, 0)) for a in tiled]
    in_specs += [_resident(a) for a in full] + [ANY_SPEC] * len(after)
    full = [a[0] if isinstance(a, tuple) else a for a in full] + list(after)
    body = _behind(body, len(tiled) + len(full) - len(after), after)
    out_shape = [jax.ShapeDtypeStruct((rows, c), dt) for c, dt in outs_tiled]
    out_shape += [jax.ShapeDtypeStruct(s, F32) for s in outs_acc]
    out_specs = [pl.BlockSpec((tm, c), lambda i: (i, 0)) for c, _ in outs_tiled]
    out_specs += [pl.BlockSpec(s, lambda i, nd=len(s): (0,) * nd) for s in outs_acc]
    return _pcall(
        body, name=name, grid=(rows // tm,), in_specs=in_specs, out_specs=out_specs, out_shape=out_shape,
        compiler_params=pltpu.CompilerParams(dimension_semantics=("arbitrary",), vmem_limit_bytes=vmem),
    )(*tiled, *full)


def _one_call(body, name, ins, outs, vmem=None):
    whole = lambda s: pl.BlockSpec(s, lambda i, nd=len(s): (0,) * nd)
    return _pcall(
        body, name=name, grid=(1,), in_specs=[_resident(a) for a in ins], out_specs=[whole(s) for s, _ in outs],
        out_shape=[jax.ShapeDtypeStruct(s, dt) for s, dt in outs],
        compiler_params=pltpu.CompilerParams(dimension_semantics=("arbitrary",), vmem_limit_bytes=vmem),
    )(*[a[0] if isinstance(a, tuple) else a for a in ins])


def _premix_fwd(x, g_pre, win_t, after=()):
    def body(x_ref, g_ref, w_ref, h_ref, proj_ref, flog_ref):
        h = _rms(x_ref[...], g_ref[...]).astype(BF16)
        h_ref[...] = h
        p = _nt(h, w_ref[...])
        proj_ref[...] = p.astype(BF16)
        flog_ref[...] = p[:, GATE0:GATE0 + 128]

    return _tok_call(body, "premix_fwd", [x], [g_pre, win_t],
                     [(D, BF16), (PROJ, BF16), (128, F32)], tm=TM_WIDE, vmem=VMEM_BIG, after=after)


def _postmix_fwd(x, o_f, o_c, g_fo, g_co, w_out, g_post, g_mpre, w_mq):
    def body(x_ref, of_ref, oc_ref, gfo_ref, gco_ref, wo_ref, gp_ref, gm_ref, wq_ref,
             y_ref, z_ref, x1_ref, h2_ref, qm_ref):
        y_ref[:, :512] = _rms(of_ref[...], gfo_ref[...]).astype(BF16)
        y_ref[:, 512:] = _rms(oc_ref[...], gco_ref[...]).astype(BF16)
        z = _nn(y_ref[...], _w(wo_ref))
        z_ref[...] = z
        x1 = x_ref[...] + _rms(z, gp_ref[...])
        x1_ref[...] = x1
        h2 = _rms(x1, gm_ref[...]).astype(BF16)
        h2_ref[...] = h2
        qm_ref[...] = _nn(h2, _w(wq_ref)).astype(BF16)

    return _tok_call(body, "postmix_fwd", [x, o_f, o_c], [g_fo, g_co, w_out, g_post, g_mpre, w_mq],
                     [(D, BF16), (D, F32), (D, F32), (D, BF16), (D, BF16)], tm=TM_WIDE, vmem=VMEM_BIG)


def _memkv_fwd(mem, g_kv, w_mk, w_mv):
    def body(m_ref, g_ref, wk_ref, wv_ref, mn_ref, k_ref, v_ref):
        mn = _rms(m_ref[...], g_ref[...]).astype(BF16)
        mn_ref[...] = mn
        k_ref[...] = _nn(mn, _w(wk_ref)).astype(BF16)
        v_ref[...] = _nn(mn, _w(wv_ref)).astype(BF16)

    return _tok_call(body, "memkv_fwd", [mem], [g_kv, w_mk, w_mv],
                     [(D, BF16), (D, BF16), (D, BF16)], rows=NMEM, tm=NMEM, vmem=VMEM_BIG)


def _mem_fwd(qm, x1, km, vm, w_mo, g_post, g_fpre):
    def body(q_ref, x1_ref, k_ref, v_ref, wo_ref, gp_ref, gf_ref, om_ref, ym_ref, x2_ref, h3_ref):
        for h in range(MEM_HEADS):
            sl = slice(h * MEM_HD, (h + 1) * MEM_HD)
            s = _nt(q_ref[:, sl], k_ref[:, sl]) * MEM_SCALE
            p = jnp.exp(s - jnp.max(s, axis=-1, keepdims=True))
            p = p / jnp.sum(p, axis=-1, keepdims=True)
            om_ref[:, sl] = _nn(p.astype(BF16), v_ref[:, sl]).astype(BF16)
        ym = _nn(om_ref[...], _w(wo_ref))
        ym_ref[...] = ym
        x2 = x1_ref[...] + _rms(ym, gp_ref[...])
        x2_ref[...] = x2
        h3_ref[...] = _rms(x2, gf_ref[...]).astype(BF16)

    return _tok_call(body, "mem_fwd", [qm, x1], [km, vm, w_mo, g_post, g_fpre],
                     [(D, BF16), (D, F32), (D, F32), (D, BF16)], tm=TM_WIDE, vmem=VMEM_BIG)


def _tri(lower):
    r = lax.broadcasted_iota(jnp.int32, (128, 128), 0)
    c = lax.broadcasted_iota(jnp.int32, (128, 128), 1)
    return jnp.where(r >= c if lower else c >= r, 1.0, 0.0).astype(F32)


def _hdot(a, b):
    return jnp.dot(a, b, preferred_element_type=F32, precision=lax.Precision.HIGHEST)


def _gate_fwd(flog, b_pad):
    def body(f_ref, b_ref, c_ref):
        tri = _tri(True)

        def step(i, carry):
            rows = pl.ds(pl.multiple_of(i * 128, 128), 128)
            z = f_ref[rows, :] + b_ref[...]
            lf = jnp.minimum(z, 0.0) - jnp.log(1.0 + jnp.exp(-jnp.abs(z)))
            cb = _hdot(tri, lf) + carry
            c_ref[rows, :] = cb
            return cb[127:128, :]

        lax.fori_loop(0, T // 128, step, jnp.zeros((1, 128), F32))

    return _one_call(body, "gate_fwd", [flog, b_pad], [((T, 128), F32)])[0]


def _gate_bwd(dc, flog, b_pad):
    def body(dc_ref, f_ref, b_ref, df_ref, db_ref):
        tri = _tri(False)

        def step(j, carry):
            run, db = carry
            i = T // 128 - 1 - j
            rows = pl.ds(pl.multiple_of(i * 128, 128), 128)
            dcb = dc_ref[rows, :]
            rb = _hdot(tri, dcb) + run
            z = f_ref[rows, :] + b_ref[...]
            df = rb * (1.0 / (1.0 + jnp.exp(z)))
            df_ref[rows, :] = df.astype(BF16)
            return run + jnp.sum(dcb, axis=0, keepdims=True), db + jnp.sum(df, axis=0, keepdims=True)

        _, db = lax.fori_loop(0, T // 128, step, (jnp.zeros((1, 128), F32), jnp.zeros((1, 128), F32)))
        db_ref[...] = jnp.broadcast_to(db, (8, 128))

    return _one_call(body, "gate_bwd", [dc, flog, b_pad], [((T, 128), BF16), ((8, 128), F32)])


def _lane_lo(rows=TQ):
    return lax.broadcasted_iota(jnp.int32, (rows, 128), 1) < HD


def _half(v, lo, a, scale=None):
    keep = lo if a == 0 else jnp.logical_not(lo)
    v = v.astype(F32) if scale is None else v.astype(F32) * scale
    return jnp.where(keep, v, 0.0).astype(BF16)


def _fox_specs():
    return [pl.BlockSpec((FQ, 128), lambda h, i: (i, h)),
            pl.BlockSpec((T, 128), lambda h, i: (0, 4 + h)),
            pl.BlockSpec((T, 128), lambda h, i: (0, 8 + h))]


def _lane_pick(x, at):
    lane = lax.broadcasted_iota(jnp.int32, x.shape, 1)
    return jnp.sum(jnp.where(lane == at, x, 0.0), axis=-1, keepdims=True)


def _fox_fwd(proj, c, ct3):
    def body(q_ref, k_ref, v_ref, c_ref, ct_ref, o_ref, l_ref):
        i = pl.program_id(1)
        lo = _lane_lo(FQ)
        causal = lax.broadcasted_iota(jnp.int32, (FQ, FQ), 1) <= lax.broadcasted_iota(jnp.int32, (FQ, FQ), 0)
        q = q_ref[...]
        qs = [_half(q, lo, a, SCALE) for a in range(2)]
        cqs = [_lane_pick(c_ref[...], 2 * pl.program_id(0) + a) for a in range(2)]

        def tile(off, carry, diagonal):
            kblk = k_ref[pl.ds(off, FQ), :]
            vblk = v_ref[pl.ds(off, FQ), :]
            new = []
            for a in range(2):
                m, l, acc = carry[a]
                s = _nt(qs[a], kblk) + (cqs[a] - ct_ref[a:a + 1, pl.ds(off, FQ)])
                if diagonal:
                    s = jnp.where(causal, s, NEG)
                m2 = jnp.maximum(m, jnp.max(s, axis=-1, keepdims=True))
                p = jnp.exp(s - m2)
                alpha = jnp.exp(m - m2)
                new.append((m2, alpha * l + jnp.sum(p, axis=-1, keepdims=True),
                            alpha * acc + _nn(p.astype(BF16), vblk)))
            return tuple(new)

        init = (jnp.full((FQ, 1), NEG, F32), jnp.zeros((FQ, 1), F32), jnp.zeros((FQ, 128), F32))
        carry = lax.fori_loop(0, i, lambda kb, c: tile(pl.multiple_of(kb * FQ, FQ), c, False), (init, init))
        carry = tile(pl.multiple_of(i * FQ, FQ), carry, True)
        outs = []
        for a in range(2):
            m, l, acc = carry[a]
            outs.append(acc / l)
            l_ref[:, 128 * a:128 * a + 128] = jnp.broadcast_to(m + jnp.log(l), (FQ, 128))
        o_ref[...] = jnp.where(lo, outs[0], outs[1])

    return _pcall(
        body, name="fox_fwd", grid=(4, T // FQ),
        in_specs=_fox_specs() + [pl.BlockSpec((FQ, 128), lambda h, i: (i, 0)),
                                 pl.BlockSpec((None, 2, T), lambda h, i: (h, 0, 0))],
        out_specs=[pl.BlockSpec((FQ, 128), lambda h, i: (i, h)), pl.BlockSpec((FQ, 256), lambda h, i: (i, h))],
        out_shape=[jax.ShapeDtypeStruct((T, 512), F32), jax.ShapeDtypeStruct((T, 1024), F32)],
        compiler_params=pltpu.CompilerParams(dimension_semantics=("arbitrary", "arbitrary"), vmem_limit_bytes=VMEM_BIG),
    )(proj, proj, proj, c, ct3)


def _fox_bwd(proj, c, ct3, o, lse, do, after=()):
    def body(q_ref, k_ref, v_ref, c_ref, ct_ref, o_ref, l_ref, do_ref, dq_ref, dkb_ref, dvb_ref, dct_ref, dcq_ref,
             dk_ref, dv_ref):
        i = pl.program_id(1)

        @pl.when(i == 0)
        def _():
            dk_ref[...] = jnp.zeros_like(dk_ref)
            dv_ref[...] = jnp.zeros_like(dv_ref)
            dct_ref[...] = jnp.zeros_like(dct_ref)

        lo = _lane_lo(FQ)
        causal = lax.broadcasted_iota(jnp.int32, (FQ, FQ), 1) <= lax.broadcasted_iota(jnp.int32, (FQ, FQ), 0)
        q = q_ref[...]
        do_v = do_ref[...]
        prod = do_v * o_ref[...]
        qs = [_half(q, lo, a, SCALE) for a in range(2)]
        dos = [_half(do_v, lo, a) for a in range(2)]
        deltas = [jnp.sum(jnp.where(lo if a == 0 else jnp.logical_not(lo), prod, 0.0), axis=-1, keepdims=True)
                  for a in range(2)]
        cqs = [_lane_pick(c_ref[...], 2 * pl.program_id(0) + a) for a in range(2)]
        las = [l_ref[:, 128 * a:128 * a + 1] for a in range(2)]

        def tile(off, carry, diagonal):
            kblk = k_ref[pl.ds(off, FQ), :]
            vblk = v_ref[pl.ds(off, FQ), :]
            new = []
            dk = jnp.zeros((128, FQ), F32)
            dv = jnp.zeros((128, FQ), F32)
            for a in range(2):
                dq_acc, rs = carry[a]
                s = _nt(qs[a], kblk) + (cqs[a] - ct_ref[a:a + 1, pl.ds(off, FQ)])
                if diagonal:
                    s = jnp.where(causal, s, NEG)
                p = jnp.exp(s - las[a])
                ds = p * (_nt(dos[a], vblk) - deltas[a])
                dsb = ds.astype(BF16)
                dk = dk + _tn(qs[a], dsb)
                dv = dv + _tn(dos[a], p.astype(BF16))
                dct_ref[a:a + 1, pl.ds(off, FQ)] -= jnp.sum(ds, axis=0, keepdims=True)
                new.append((dq_acc + _nn(dsb, kblk), rs + jnp.sum(ds, axis=-1, keepdims=True)))
            dk_ref[:, pl.ds(off, FQ)] += dk
            dv_ref[:, pl.ds(off, FQ)] += dv
            return tuple(new)

        init = (jnp.zeros((FQ, 128), F32), jnp.zeros((FQ, 1), F32))
        carry = lax.fori_loop(0, i, lambda kb, c: tile(pl.multiple_of(kb * FQ, FQ), c, False), (init, init))
        carry = tile(pl.multiple_of(i * FQ, FQ), carry, True)
        lane = lax.broadcasted_iota(jnp.int32, (FQ, 128), 1)
        dcq_ref[...] = jnp.where(lane == 0, carry[0][1], jnp.where(lane == 1, carry[1][1], 0.0))
        dq_ref[...] = (jnp.where(lo, carry[0][0], carry[1][0]) * SCALE).astype(BF16)

        @pl.when(i == T // FQ - 1)
        def _():
            dkb_ref[...] = dk_ref[...].T.astype(BF16)
            dvb_ref[...] = dv_ref[...].T.astype(BF16)

    blk = pl.BlockSpec((FQ, 128), lambda h, i: (i, h))
    wide = pl.BlockSpec((FQ, 256), lambda h, i: (i, h))
    rows = pl.BlockSpec((None, 2, T), lambda h, i: (h, 0, 0))
    col = pl.BlockSpec((T, 128), lambda h, i: (0, h))
    return _pcall(
        _behind(body, 8, after), name="fox_bwd", grid=(4, T // FQ),
        in_specs=_fox_specs() + [pl.BlockSpec((FQ, 128), lambda h, i: (i, 0)), rows, blk, wide, blk] + [ANY_SPEC] * len(after),
        out_specs=[blk, col, col, rows, pl.BlockSpec((None, FQ, 128), lambda h, i: (h, i, 0))],
        out_shape=[jax.ShapeDtypeStruct((T, 512), BF16), jax.ShapeDtypeStruct((T, 512), BF16),
                   jax.ShapeDtypeStruct((T, 512), BF16), jax.ShapeDtypeStruct((4, 2, T), F32),
                   jax.ShapeDtypeStruct((4, T, 128), F32)],
        scratch_shapes=[pltpu.VMEM((128, T), F32), pltpu.VMEM((128, T), F32)],
        compiler_params=pltpu.CompilerParams(dimension_semantics=("arbitrary", "arbitrary"), vmem_limit_bytes=VMEM_BIG),
    )(proj, proj, proj, c, ct3, o, lse, do, *after)


def _rel_onehot():
    ridx = lax.broadcasted_iota(jnp.int32, (NREL_PAD, VW), 0)
    j = lax.broadcasted_iota(jnp.int32, (NREL_PAD, VW), 1)
    return jnp.where(ridx == jnp.clip(TQ + LEFT - 1 - j, -128, 128) + 128, 1.0, 0.0).astype(F32)


def _relvec_fwd(tbl):
    def body(t_ref, v_ref):
        v_ref[...] = _hdot(t_ref[...], _rel_onehot())

    return _one_call(body, "relvec_fwd", [tbl], [((8, VW), F32)])[0]


def _relvec_bwd(gv):
    def body(g_ref, t_ref):
        t_ref[...] = lax.dot_general(g_ref[...], _rel_onehot(), (((1,), (1,)), ((), ())),
                                     preferred_element_type=F32, precision=lax.Precision.HIGHEST)

    return _one_call(body, "relvec_bwd", [gv], [((8, NREL_PAD), F32)])[0]


def _chk_bias(vt_ref, a, hidden):
    vb = jnp.broadcast_to(vt_ref[a:a + 1, :], (TQ, VW))
    y = pltpu.roll(vb, VW - (TQ - 1), 1, stride=1, stride_axis=0)[:, :WIN]
    cr = lax.broadcasted_iota(jnp.int32, (TQ, WIN), 0) // 64
    m = lax.broadcasted_iota(jnp.int32, (TQ, WIN), 1)
    return jnp.where((m // 64 >= cr) & (m // 64 <= cr + 8) & (m >= hidden), y, NEG)


def _chk_specs():
    return [pl.BlockSpec((TQ, 128), lambda h, i: (i, CHK0 // 128 + h)),
            pl.BlockSpec((T + LEFT, 128), lambda h, i: (0, h)),
            pl.BlockSpec((T + LEFT, 128), lambda h, i: (0, 4 + h)),
            pl.BlockSpec((None, 2, VW), lambda h, i: (h, 0, 0))]


def _chk_fwd(proj, kvp, vt3, after=()):
    def body(q_ref, k_ref, v_ref, vt_ref, o_ref, l_ref, bias_ref):
        i = pl.program_id(1)

        @pl.when(i == 0)
        def _():
            for first in range(3):
                for a in range(2):
                    bias_ref[first, a] = _chk_bias(vt_ref, a, max(LEFT - first * TQ, 0))

        lo = _lane_lo()
        off = pl.multiple_of(i * TQ, TQ)
        kw = k_ref[pl.ds(off, WIN), :]
        vw = v_ref[pl.ds(off, WIN), :]
        bias_at = jnp.minimum(i, 2)
        q = q_ref[...]
        outs = []
        for a in range(2):
            s = _nt(_half(q, lo, a, SCALE), kw) + bias_ref[bias_at, a]
            m = jnp.max(s, axis=-1, keepdims=True)
            p = jnp.exp(s - m)
            l = jnp.sum(p, axis=-1, keepdims=True)
            outs.append(_nn(p.astype(BF16), vw) / l)
            l_ref[:, 128 * a:128 * a + 128] = jnp.broadcast_to(m + jnp.log(l), (TQ, 128))
        o_ref[...] = jnp.where(lo, outs[0], outs[1])

    return _pcall(
        _behind(body, 4, after), name="chk_fwd", grid=(4, T // TQ), in_specs=_chk_specs() + [ANY_SPEC] * len(after),
        out_specs=[pl.BlockSpec((TQ, 128), lambda h, i: (i, h)), pl.BlockSpec((TQ, 256), lambda h, i: (i, h))],
        out_shape=[jax.ShapeDtypeStruct((T, 512), F32), jax.ShapeDtypeStruct((T, 1024), F32)],
        scratch_shapes=[pltpu.VMEM((3, 2, TQ, WIN), F32)],
        compiler_params=pltpu.CompilerParams(dimension_semantics=("arbitrary", "arbitrary")),
    )(proj, kvp, kvp, vt3, *after)


def _chk_bwd(proj, kvp, vt3, o, lse, do, after=()):
    nq = T // TQ

    def body(q_ref, k_ref, v_ref, vt_ref, o_ref, l_ref, do_ref, dq_ref, dkb_ref, dvb_ref, gv_ref, bias_ref, dsum_ref,
             dk_ref, dv_ref):
        i = pl.program_id(1)

        @pl.when(i == 0)
        def _():
            for first in range(3):
                for a in range(2):
                    bias_ref[first, a] = _chk_bias(vt_ref, a, max(LEFT - first * TQ, 0))
            dsum_ref[...] = jnp.zeros_like(dsum_ref)
            dk_ref[...] = jnp.zeros_like(dk_ref)
            dv_ref[...] = jnp.zeros_like(dv_ref)

        lo = _lane_lo()
        off = pl.multiple_of(i * TQ, TQ)
        kw = k_ref[pl.ds(off, WIN), :]
        vw = v_ref[pl.ds(off, WIN), :]
        bias_at = jnp.minimum(i, 2)
        q = q_ref[...]
        do_v = do_ref[...]
        prod = do_v * o_ref[...]
        dqs = []
        for a in range(2):
            keep = lo if a == 0 else jnp.logical_not(lo)
            qa = _half(q, lo, a, SCALE)
            doa = _half(do_v, lo, a)
            delta = jnp.sum(jnp.where(keep, prod, 0.0), axis=-1, keepdims=True)
            s = _nt(qa, kw) + bias_ref[bias_at, a]
            p = jnp.exp(s - l_ref[:, 128 * a:128 * a + 1])
            ds = p * (_nt(doa, vw) - delta)
            dsum_ref[a] += ds
            dsb = ds.astype(BF16)
            dk_ref[:, pl.ds(off, WIN)] += _tn(qa, dsb)
            dv_ref[:, pl.ds(off, WIN)] += _tn(doa, p.astype(BF16))
            dqs.append(_nn(dsb, kw))
        dq_ref[...] = (jnp.where(lo, dqs[0], dqs[1]) * SCALE).astype(BF16)

        @pl.when(i == nq - 1)
        def _():
            dkb_ref[...] = dk_ref[:, LEFT:].T.astype(BF16)
            dvb_ref[...] = dv_ref[:, LEFT:].T.astype(BF16)
            rr = lax.broadcasted_iota(jnp.int32, (TQ, TQ), 0)
            cc = lax.broadcasted_iota(jnp.int32, (TQ, TQ), 1)
            flip = jnp.where(rr + cc == TQ - 1, 1.0, 0.0).astype(F32)
            for a in range(2):
                dpad = jnp.concatenate([dsum_ref[a], jnp.zeros((TQ, VW - WIN), F32)], axis=1)
                z = pltpu.roll(_hdot(flip, dpad), 0, 1, stride=1, stride_axis=0)
                gv_ref[a:a + 1, :] = jnp.sum(z, axis=0, keepdims=True)

    blk = pl.BlockSpec((TQ, 128), lambda h, i: (i, h))
    wide = pl.BlockSpec((TQ, 256), lambda h, i: (i, h))
    col = pl.BlockSpec((T, 128), lambda h, i: (0, h))
    return _pcall(
        _behind(body, 7, after), name="chk_bwd", grid=(4, nq), in_specs=_chk_specs() + [blk, wide, blk] + [ANY_SPEC] * len(after),
        out_specs=[blk, col, col, pl.BlockSpec((None, 2, VW), lambda h, i: (h, 0, 0))],
        out_shape=[jax.ShapeDtypeStruct((T, 512), BF16), jax.ShapeDtypeStruct((T, 512), BF16),
                   jax.ShapeDtypeStruct((T, 512), BF16), jax.ShapeDtypeStruct((4, 2, VW), F32)],
        scratch_shapes=[pltpu.VMEM((3, 2, TQ, WIN), F32), pltpu.VMEM((2, TQ, WIN), F32),
                        pltpu.VMEM((128, T + LEFT), F32), pltpu.VMEM((128, T + LEFT), F32)],
        compiler_params=pltpu.CompilerParams(dimension_semantics=("arbitrary", "arbitrary")),
    )(proj, kvp, kvp, vt3, o, lse, do, *after)


def _zero_at_start(*refs):
    @pl.when(pl.program_id(0) == 0)
    def _():
        for r in refs:
            r[...] = jnp.zeros_like(r)


def _ffn_step(h3, x2, tgt, w1_t, w2, g_post, g_pre):
    def body(h_ref, x2_ref, t_ref, w1_ref, w2_ref, gp_ref, gf_ref, dx2_ref, da_ref, dy_ref, r_ref, loss_ref, dgp_ref, dgf_ref):
        _zero_at_start(loss_ref, dgp_ref, dgf_ref)
        w1, w2v = _w(w1_ref), _w(w2_ref)
        ra = jnp.maximum(_nt(h_ref[...], w1), 0.0)
        r = jnp.square(ra).astype(BF16)
        r_ref[...] = r
        y = _nn(r, w2v)
        x2v = x2_ref[...]
        e = x2v + _rms(y, gp_ref[...]) - t_ref[...]
        loss_ref[...] += 0.5 * jnp.sum(jnp.sum(e * e, axis=-1, keepdims=True) * (1.0 / D))
        dx3 = e * (1.0 / D)
        dy, dgp = _rms_bwd(y, gp_ref[...], dx3)
        dgp_ref[...] += dgp
        dyb = dy.astype(BF16)
        dy_ref[...] = dyb
        da = (_nt(dyb, w2v) * (2.0 * ra)).astype(BF16)
        da_ref[...] = da
        dh, dgf = _rms_bwd(x2v, gf_ref[...], _nn(da, w1))
        dgf_ref[...] += dgf
        dx2_ref[...] = dx3 + dh

    return _tok_call(body, "ffn_step", [h3, x2, tgt], [w1_t, w2, g_post, g_pre],
                     [(D, F32), (DFF, BF16), (D, BF16), (DFF, BF16)], [(8, 128), (1, D), (1, D)], vmem=VMEM_BIG)


def _mem_bwd(dx2, ym, x1, qm, km, vm, w_mo, w_mq, g_post, g_pre, after=()):
    def body(dx2_ref, ym_ref, x1_ref, q_ref, k_ref, v_ref, wo_ref, wq_ref, gp_ref, gm_ref,
             dx1_ref, dym_ref, dq_ref, dk_ref, dv_ref, dgp_ref, dgm_ref, dom_ref):
        _zero_at_start(dk_ref, dv_ref, dgp_ref, dgm_ref)
        dx2_v = dx2_ref[...]
        dym, dgp = _rms_bwd(ym_ref[...], gp_ref[...], dx2_v)
        dgp_ref[...] += dgp
        dymb = dym.astype(BF16)
        dym_ref[...] = dymb
        dom_ref[...] = _nt(dymb, _w(wo_ref)).astype(BF16)
        for h in range(MEM_HEADS):
            sl = slice(h * MEM_HD, (h + 1) * MEM_HD)
            qh, kh, doh = q_ref[:, sl], k_ref[:, sl], dom_ref[:, sl]
            s = _nt(qh, kh) * MEM_SCALE
            p = jnp.exp(s - jnp.max(s, axis=-1, keepdims=True))
            p = p / jnp.sum(p, axis=-1, keepdims=True)
            dp = _nt(doh, v_ref[:, sl])
            ds = (p * (dp - jnp.sum(p * dp, axis=-1, keepdims=True))).astype(BF16)
            dq_ref[:, sl] = (_nn(ds, kh) * MEM_SCALE).astype(BF16)
            dk_ref[:, sl] += _tn(ds, qh) * MEM_SCALE
            dv_ref[:, sl] += _tn(p.astype(BF16), doh)
        dh, dgm = _rms_bwd(x1_ref[...], gm_ref[...], _nt(dq_ref[...], _w(wq_ref)))
        dgm_ref[...] += dgm
        dx1_ref[...] = dx2_v + dh

    tiled = pl.BlockSpec((TM_WIDE, D), lambda i: (i, 0))
    in_specs = [tiled] * 4 + [_resident(a) for a in (km, vm, w_mo, w_mq, g_post, g_pre)] + [ANY_SPEC] * len(after)
    w_mo, w_mq = w_mo[0], w_mq[0]
    kv = pl.BlockSpec((NMEM, D), lambda i: (0, 0))
    vec = pl.BlockSpec((1, D), lambda i: (0, 0))
    return _pcall(
        _behind(body, 10, after), name="mem_bwd", grid=(T // TM_WIDE,), in_specs=in_specs,
        out_specs=[tiled, tiled, tiled, kv, kv, vec, vec],
        out_shape=[jax.ShapeDtypeStruct((T, D), F32), jax.ShapeDtypeStruct((T, D), BF16),
                   jax.ShapeDtypeStruct((T, D), BF16), jax.ShapeDtypeStruct((NMEM, D), F32),
                   jax.ShapeDtypeStruct((NMEM, D), F32), jax.ShapeDtypeStruct((1, D), F32),
                   jax.ShapeDtypeStruct((1, D), F32)],
        scratch_shapes=[pltpu.VMEM((TM_WIDE, D), BF16)],
        compiler_params=pltpu.CompilerParams(dimension_semantics=("arbitrary",), vmem_limit_bytes=VMEM_BIG),
    )(dx2, ym, x1, qm, km, vm, w_mo, w_mq, g_post, g_pre, *after)


def _memkv_bwd(dkm, dvm, mem, w_mk, w_mv):
    def body(dk_ref, dv_ref, m_ref, wk_ref, wv_ref, dg_ref):
        dmn = _nt(dk_ref[...].astype(BF16), _w(wk_ref)) + _nt(dv_ref[...].astype(BF16), _w(wv_ref))
        mv = m_ref[...]
        dg_ref[...] = jnp.sum(dmn * (mv * _rstd(mv)), axis=0, keepdims=True)

    return _one_call(body, "memkv_bwd", [dkm, dvm, mem, w_mk, w_mv], [((1, D), F32)], vmem=VMEM_BIG)[0]


def _postmix_bwd(dx1, z, o_f, o_c, w_out, g_post, g_fo, g_co, after=()):
    def body(dx1_ref, z_ref, of_ref, oc_ref, wo_ref, gp_ref, gfo_ref, gco_ref,
             dz_ref, dof_ref, doc_ref, dgp_ref, dgfo_ref, dgco_ref):
        _zero_at_start(dgp_ref, dgfo_ref, dgco_ref)
        dz, dgp = _rms_bwd(z_ref[...], gp_ref[...], dx1_ref[...])
        dgp_ref[...] += dgp
        dzb = dz.astype(BF16)
        dz_ref[...] = dzb
        dy = _nt(dzb, _w(wo_ref))
        dof, dgfo = _rms_bwd(of_ref[...], gfo_ref[...], dy[:, :512])
        doc, dgco = _rms_bwd(oc_ref[...], gco_ref[...], dy[:, 512:])
        dof_ref[...] = dof
        doc_ref[...] = doc
        dgfo_ref[...] += dgfo
        dgco_ref[...] += dgco

    return _tok_call(body, "postmix_bwd", [dx1, z, o_f, o_c], [w_out, g_post, g_fo, g_co],
                     [(D, BF16), (512, F32), (512, F32)], [(1, D), (1, 512), (1, 512)], tm=TM_WIDE, vmem=VMEM_BIG,
                     after=after)


def _premix_bwd(dx1, x, pieces, win_t, g_pre, after=()):
    def body(dx1_ref, x_ref, *refs):
        piece_refs, (w_ref, g_ref, dx_ref, dp_ref, dg_ref) = refs[:len(pieces)], refs[len(pieces):]
        _zero_at_start(dg_ref)
        col = 0
        for p in piece_refs:
            dp_ref[:, col:col + p.shape[1]] = p[...]
            col += p.shape[1]
        dh, dg = _rms_bwd(x_ref[...], g_ref[...], _nn(dp_ref[...], w_ref[...]))
        dg_ref[...] += dg
        dx_ref[...] = dx1_ref[...] + dh

    return _tok_call(body, "premix_bwd", [dx1, x] + list(pieces), [win_t, g_pre], [(D, F32), (PROJ, BF16)], [(1, D)],
                     tm=TM_WIDE, vmem=VMEM_BIG, after=after)


def _wgrad_group(name, pairs, rows):
    def body(*refs):
        o_ref = refs[-1]
        for k in range(len(pairs)):
            g = _tn(refs[2 * k][...].astype(BF16), refs[2 * k + 1][...].astype(BF16))
            o_ref[k * rows:(k + 1) * rows, :] = g.astype(BF16)

    in_specs, ops = [], []
    for a, b in pairs:
        in_specs += [pl.BlockSpec((a.shape[0], rows), lambda j: (0, j)), _resident(b)]
        ops += [a, b]
    return _pcall(
        body, name=name, grid=(8,), in_specs=in_specs,
        out_specs=pl.BlockSpec((None, len(pairs) * rows, D), lambda j: (j, 0, 0)),
        out_shape=jax.ShapeDtypeStruct((8, len(pairs) * rows, D), BF16),
        compiler_params=pltpu.CompilerParams(dimension_semantics=("arbitrary",), vmem_limit_bytes=VMEM_BIG),
    )(*ops)


def _wgrad_whole(name, pairs):
    n = len(pairs)
    ops = [x for pair in pairs for x in pair]
    rows = pairs[0][0].shape[1] // 8

    def body(*refs):
        hbm, o_ref, bufs, sem = refs[:2 * n], refs[2 * n], refs[2 * n + 1:4 * n + 1], refs[4 * n + 1]
        k = pl.program_id(0)
        copies = [pltpu.make_async_copy(hbm[t], bufs[t], sem.at[t]) for t in range(2 * n)]

        @pl.when(k == 0)
        def _():
            for copy in copies:
                copy.start()

        for t in range(n):
            @pl.when(k == t)
            def _(t=t):
                copies[2 * t].wait()
                copies[2 * t + 1].wait()
                g = _tn(bufs[2 * t][...].astype(BF16), bufs[2 * t + 1][...].astype(BF16))
                o_ref[...] = g.reshape(8, rows, D).astype(BF16)

    return _pcall(
        body, name=name, grid=(n,), in_specs=[ANY_SPEC] * (2 * n),
        out_specs=pl.BlockSpec((8, rows, D), lambda k: (0, k, 0)),
        out_shape=jax.ShapeDtypeStruct((8, n * rows, D), BF16),
        scratch_shapes=[pltpu.VMEM(x.shape, x.dtype) for x in ops] + [pltpu.SemaphoreType.DMA((2 * n,))],
        compiler_params=pltpu.CompilerParams(dimension_semantics=("arbitrary",), vmem_limit_bytes=VMEM_BIG),
    )(*ops)


def _adam_math(w, g, m, v):
    m2 = ADAM_B1 * m + (1.0 - ADAM_B1) * g
    v2 = ADAM_B2 * v + (1.0 - ADAM_B2) * jnp.square(g)
    m_hat = m2 / (1.0 - ADAM_B1 ** ADAM_STEP)
    v_hat = v2 / (1.0 - ADAM_B2 ** ADAM_STEP)
    delta = -ADAM_LR * (m_hat / (jnp.sqrt(v_hat) + ADAM_EPS) + ADAM_WD * w)
    return delta, m2, v2


def _adamw_small(gparts, ws, ms, vs):
    n = len(SMALL)

    def body(g_ref, *refs):
        w_refs, m_refs, v_refs = refs[:n], refs[n:2 * n], refs[2 * n:3 * n]
        outs, sum_ref = refs[3 * n:-1], refs[-1]
        g = g_ref[0]
        for k in range(1, 8):
            g = g + g_ref[k]
        sum_ref[...] = g
        outs[0][...] = sum_ref[17:18, 0:128]
        for t, name in enumerate(SMALL):
            r0, nr, c0, nc = SMALL_SLOT[name]
            gt = sum_ref[r0:r0 + nr, c0:c0 + nc]
            out = (gt,) + _adam_math(w_refs[t][...], gt, m_refs[t][...], v_refs[t][...])
            for o_ref, val in zip(outs[1 + 4 * t:5 + 4 * t], out):
                o_ref[...] = val

    whole = lambda s: pl.BlockSpec(s, lambda i, nd=len(s): (0,) * nd)
    ins = [gparts] + list(ws) + list(ms) + list(vs)
    out_shapes = [(1, 128)] + [a.shape for a in ws for _ in range(4)]
    return _pcall(
        body, name="adamw_small", grid=(1,), in_specs=[whole(a.shape) for a in ins],
        out_specs=[whole(s) for s in out_shapes], out_shape=[jax.ShapeDtypeStruct(s, F32) for s in out_shapes],
        scratch_shapes=[pltpu.VMEM((SMALL_ROWS, D), F32)],
        compiler_params=pltpu.CompilerParams(dimension_semantics=("arbitrary",)),
    )(*ins)


def _row_tile(rows):
    return next(t for t in (512, 400, 320) if rows % t == 0)


def _add_halves(g4, theirs, core, name):
    rows = g4.shape[2]
    tr = _row_tile(rows)

    def body(c_ref, a_ref, b_ref, o_ref):
        o_ref[...] = (a_ref[...].astype(F32) + b_ref[...].astype(F32)).astype(BF16)

    grid_spec = pltpu.PrefetchScalarGridSpec(
        num_scalar_prefetch=1, grid=(4, rows // tr),
        in_specs=[pl.BlockSpec((None, None, tr, D), lambda j, i, c: (j, c[0], i, 0)),
                  pl.BlockSpec((None, None, tr, D), lambda j, i, c: (j, 0, i, 0))],
        out_specs=pl.BlockSpec((None, tr, D), lambda j, i, c: (j, i, 0)))
    return _pcall(
        body, name=name, grid_spec=grid_spec, out_shape=jax.ShapeDtypeStruct((4, rows, D), BF16),
        compiler_params=pltpu.CompilerParams(dimension_semantics=("arbitrary", "arbitrary")),
    )(core, g4, theirs)


def _sum_adam(own, got, order, r0, w, m, v, name, transposed=False):
    n = w.shape[1] if transposed else w.shape[0]
    tr = min(n, 256) if n % 8 == 0 else n
    rows = tr if n % 8 == 0 else own.shape[1]

    def body(o_ref, a_ref, b_ref, c_ref, d_ref, w_ref, m_ref, v_ref, g_ref, dl_ref, m2_ref, v2_ref):
        f = lambda r: r[0:tr, :].astype(F32)
        g = ((f(a_ref) + f(b_ref)) + f(c_ref)) + f(d_ref)
        g = g.T if transposed else g
        g_ref[...] = g
        dl_ref[...], m2_ref[...], v2_ref[...] = _adam_math(w_ref[...], g, m_ref[...], v_ref[...])

    slot = lambda k: pl.BlockSpec((None, rows, D), lambda i, o: (o[k], r0 // rows + i, 0))
    wspec = pl.BlockSpec((D, tr), lambda i, o: (0, i)) if transposed else pl.BlockSpec((tr, D), lambda i, o: (i, 0))
    grid_spec = pltpu.PrefetchScalarGridSpec(
        num_scalar_prefetch=1, grid=(n // tr,), in_specs=[slot(0), slot(1), slot(2), slot(3), wspec, wspec, wspec],
        out_specs=[wspec] * 4)
    return _pcall(
        body, name=name, grid_spec=grid_spec, out_shape=[jax.ShapeDtypeStruct(w.shape, F32)] * 4,
        compiler_params=pltpu.CompilerParams(dimension_semantics=("arbitrary",)),
    )(order, own, got, got, got, w, m, v)


def _sum_adam_rows(own, got, order, ws, ms, vs, name):
    n, rows = len(ws), ws[0].shape[0]

    def body(o_ref, a_ref, b_ref, c_ref, d_ref, *refs):
        ins, outs = refs[:3 * n], refs[3 * n:]
        for t in range(n):
            r = slice(t * rows, (t + 1) * rows)
            f = lambda ref: ref[r, :].astype(F32)
            g = ((f(a_ref) + f(b_ref)) + f(c_ref)) + f(d_ref)
            out = (g,) + _adam_math(ins[t][...], g, ins[n + t][...], ins[2 * n + t][...])
            for o, val in zip(outs[4 * t:4 * t + 4], out):
                o[...] = val

    slot = lambda k: pl.BlockSpec((None, n * rows, D), lambda i, o: (o[k], 0, 0), pipeline_mode=pl.Buffered(1))
    wspec = pl.BlockSpec((rows, D), lambda i, o: (0, 0), pipeline_mode=pl.Buffered(1))
    grid_spec = pltpu.PrefetchScalarGridSpec(
        num_scalar_prefetch=1, grid=(1,), in_specs=[slot(0), slot(1), slot(2), slot(3)] + [wspec] * (3 * n),
        out_specs=[pl.BlockSpec((rows, D), lambda i, o: (0, 0))] * (4 * n))
    return _pcall(
        body, name=name, grid_spec=grid_spec, out_shape=[jax.ShapeDtypeStruct((rows, D), F32)] * (4 * n),
        compiler_params=pltpu.CompilerParams(dimension_semantics=("arbitrary",), vmem_limit_bytes=VMEM_BIG),
    )(order, own, got, got, got, *ws, *ms, *vs)


def _place():
    return lax.axis_index("x"), lax.axis_index("y"), lax.axis_index("c")


def _allgather(block, name, after=()):
    rows = block.shape[0]
    split = (rows // 2 + 15) // 16 * 16

    def body(x_ref, out_ref, token, send_sems, recv_sems, local_sem):
        token[...] = jnp.zeros_like(token)
        x, y, c = _place()
        me, sib = (x, y, c), (x, y, 1 - c)
        xn, yn, dg = (1 - x, y), (x, 1 - y), (1 - x, 1 - y)
        lo, hi = pl.ds(0, split), pl.ds(split, rows - split)

        def copy(k, blk, to, part=None, src=None):
            index = 4 * blk[0] + 2 * blk[1] + blk[2]
            view = out_ref.at[index] if part is None else out_ref.at[index, part]
            return pltpu.make_async_remote_copy(
                src_ref=view if src is None else src, dst_ref=view,
                send_sem=send_sems.at[k], recv_sem=recv_sems.at[k], device_id=to, device_id_type=MESH)

        def start(*copies):
            for cp in copies:
                cp.start()
            return list(copies)

        mine = pltpu.make_async_copy(x_ref, out_ref.at[4 * x + 2 * y + c], local_sem)
        mine.start()
        sent = start(copy(0, me, sib, src=x_ref), copy(1, me, (*xn, c), src=x_ref), copy(2, me, (*yn, c), src=x_ref))
        copy(1, (*xn, c), me).wait_recv()
        sent += start(copy(3, (*xn, c), sib), copy(5, (*xn, c), (*yn, c), part=lo))
        copy(2, (*yn, c), me).wait_recv()
        sent += start(copy(4, (*yn, c), sib), copy(6, (*yn, c), (*xn, c), part=hi))
        copy(5, (*dg, c), me, part=lo).wait_recv()
        copy(6, (*dg, c), me, part=hi).wait_recv()
        sent += start(copy(7, (*dg, c), sib))
        for k, blk in ((0, sib), (3, (*xn, 1 - c)), (4, (*yn, 1 - c)), (7, (*dg, 1 - c))):
            copy(k, blk, me).wait_recv()
        for cp in sent:
            cp.wait_send()
        mine.wait()

    return _pcall(
        _behind(body, 1, after), name=name,
        out_shape=[jax.ShapeDtypeStruct((8,) + block.shape, block.dtype), jax.ShapeDtypeStruct((8, 128), F32)],
        in_specs=[pl.BlockSpec(memory_space=pl.ANY)] * (1 + len(after)),
        out_specs=[pl.BlockSpec(memory_space=pl.ANY), pl.BlockSpec(memory_space=pltpu.VMEM)],
        scratch_shapes=[pltpu.SemaphoreType.DMA((8,)), pltpu.SemaphoreType.DMA((8,)), pltpu.SemaphoreType.DMA(())],
        compiler_params=pltpu.CompilerParams(has_side_effects=True),
    )(block, *after)


HBM_SPEC = pl.BlockSpec(memory_space=pltpu.HBM)
SEM_SPEC = pl.BlockSpec(memory_space=pltpu.SEMAPHORE)
ANY_SPEC = pl.BlockSpec(memory_space=pl.ANY)
EFFECT = pltpu.SideEffectType.DATAFLOW_SIDE_EFFECTING


def _in_hbm(a):
    return pltpu.with_memory_space_constraint(a, pltpu.HBM)


def _start_copies(name, src, land_shape, plan, n):
    def body(src_ref, land_ref, send_sems, recv_sems, src_thru, land_thru, token):
        for k, (s, d, to, _) in enumerate(plan(src_ref, land_ref)):
            pltpu.make_async_remote_copy(src_ref=s, dst_ref=d, send_sem=send_sems.at[k], recv_sem=recv_sems.at[k],
                                         device_id=to, device_id_type=MESH).start()
        token[...] = jnp.zeros_like(token)

    return _pcall(
        body, name=name,
        out_shape=(pltpu.SemaphoreType.DMA((n,)), pltpu.SemaphoreType.DMA((n,)), pltpu.HBM(src.shape, src.dtype),
                   pltpu.HBM(land_shape, src.dtype), jax.ShapeDtypeStruct((8, 128), F32)),
        in_specs=(HBM_SPEC, HBM_SPEC),
        out_specs=(SEM_SPEC, SEM_SPEC, HBM_SPEC, HBM_SPEC, pl.BlockSpec(memory_space=pltpu.VMEM)),
        input_output_aliases={0: 2, 1: 3}, compiler_params=pltpu.CompilerParams(has_side_effects=EFFECT),
    )(_in_hbm(src), _in_hbm(lax.empty(land_shape, src.dtype)))


def _wait_copies(name, started, after, plan):
    send_sems, recv_sems, src_thru, land_thru, _ = started

    def body(src_ref, land_ref, send_sems, recv_sems, *rest):
        for k, (s, _, to, mine) in enumerate(plan(src_ref, land_ref)):
            cp = pltpu.make_async_remote_copy(src_ref=s, dst_ref=mine, send_sem=send_sems.at[k],
                                              recv_sem=recv_sems.at[k], device_id=to, device_id_type=MESH)
            cp.wait_send()
            cp.wait_recv()

    return _pcall(
        body, name=name,
        out_shape=(pltpu.HBM(src_thru.shape, src_thru.dtype), pltpu.HBM(land_thru.shape, land_thru.dtype)),
        in_specs=(HBM_SPEC, HBM_SPEC, SEM_SPEC, SEM_SPEC) + (ANY_SPEC,) * len(after), out_specs=(HBM_SPEC, HBM_SPEC),
        input_output_aliases={0: 0, 1: 1}, compiler_params=pltpu.CompilerParams(has_side_effects=EFFECT),
    )(src_thru, land_thru, send_sems, recv_sems, *after)


def _start_inplace(name, buf, plan, n):
    def body(buf_ref, send_sems, recv_sems, buf_thru, token):
        for k, (s, d, to, _) in enumerate(plan(buf_ref, buf_ref)):
            pltpu.make_async_remote_copy(src_ref=s, dst_ref=d, send_sem=send_sems.at[k], recv_sem=recv_sems.at[k],
                                         device_id=to, device_id_type=MESH).start()
        token[...] = jnp.zeros_like(token)

    return _pcall(
        body, name=name,
        out_shape=(pltpu.SemaphoreType.DMA((n,)), pltpu.SemaphoreType.DMA((n,)), pltpu.HBM(buf.shape, buf.dtype),
                   jax.ShapeDtypeStruct((8, 128), F32)),
        in_specs=(HBM_SPEC,), out_specs=(SEM_SPEC, SEM_SPEC, HBM_SPEC, pl.BlockSpec(memory_space=pltpu.VMEM)),
        input_output_aliases={0: 2}, compiler_params=pltpu.CompilerParams(has_side_effects=EFFECT),
    )(_in_hbm(buf))


def _wait_inplace(name, started, after, plan):
    send_sems, recv_sems, buf_thru, _ = started

    def body(buf_ref, send_sems, recv_sems, *rest):
        for k, (s, _, to, mine) in enumerate(plan(buf_ref, buf_ref)):
            cp = pltpu.make_async_remote_copy(src_ref=s, dst_ref=mine, send_sem=send_sems.at[k],
                                              recv_sem=recv_sems.at[k], device_id=to, device_id_type=MESH)
            cp.wait_send()
            cp.wait_recv()

    return _pcall(
        body, name=name, out_shape=pltpu.HBM(buf_thru.shape, buf_thru.dtype),
        in_specs=(HBM_SPEC, SEM_SPEC, SEM_SPEC) + (ANY_SPEC,) * len(after), out_specs=HBM_SPEC,
        input_output_aliases={0: 0}, compiler_params=pltpu.CompilerParams(has_side_effects=EFFECT),
    )(buf_thru, send_sems, recv_sems, *after)


def _gather_plan(src_ref, land_ref):
    x, y, c = _place()
    peers = [(x, y, 1 - c), (1 - x, y, c), (x, 1 - y, c)]
    return [(src_ref, land_ref.at[4 * x + 2 * y + c], p, land_ref.at[4 * p[0] + 2 * p[1] + p[2]]) for p in peers]


def _relay_plan(buf_ref, _):
    x, y, c = _place()
    slot = lambda p, pc: 4 * p[0] + 2 * p[1] + pc
    xn, yn, dg, sib = (1 - x, y), (x, 1 - y), (1 - x, 1 - y), (x, y, 1 - c)
    half = buf_ref.shape[1] // 2
    lo, hi = pl.ds(0, half), pl.ds(half, half)
    return [(buf_ref.at[slot(xn, c)], buf_ref.at[slot(xn, c)], sib, buf_ref.at[slot(xn, 1 - c)]),
            (buf_ref.at[slot(yn, c)], buf_ref.at[slot(yn, c)], sib, buf_ref.at[slot(yn, 1 - c)]),
            (buf_ref.at[slot(xn, c), lo], buf_ref.at[slot(xn, c), lo], (*yn, c), buf_ref.at[slot(dg, c), lo]),
            (buf_ref.at[slot(yn, c), hi], buf_ref.at[slot(yn, c), hi], (*xn, c), buf_ref.at[slot(dg, c), hi])]


def _swap_plan(src_ref, land_ref):
    x, y, c = _place()
    return [(src_ref.at[:, pl.ds(1 - c, 1)], land_ref, (x, y, 1 - c), land_ref)]


def _exchange_plan(src_ref, land_ref):
    x, y, c = _place()
    chips = [(1 - x, y), (x, 1 - y), (1 - x, 1 - y)]
    return [(src_ref.at[2 * px + py], land_ref.at[2 * x + y], (px, py, c), land_ref.at[2 * px + py]) for px, py in chips]


def _gather_forward(land, block):
    def body(land_ref, out_ref, send_sems, recv_sems):
        x, y, c = _place()
        chips = [(1 - x, 1 - y)]

        def copy(k, px, py, pc):
            blk = out_ref.at[4 * px + 2 * py + pc]
            return pltpu.make_async_remote_copy(src_ref=blk, dst_ref=blk, send_sem=send_sems.at[k],
                                                recv_sem=recv_sems.at[k], device_id=(x, y, 1 - c), device_id_type=MESH)

        sent = [copy(k, px, py, c) for k, (px, py) in enumerate(chips)]
        for cp in sent:
            cp.start()
        for k, (px, py) in enumerate(chips):
            copy(k, px, py, 1 - c).wait_recv()
        for cp in sent:
            cp.wait_send()

    land = _pcall(
        body, name="allgather_rest_forward", out_shape=jax.ShapeDtypeStruct(land.shape, land.dtype),
        in_specs=[ANY_SPEC], out_specs=ANY_SPEC, input_output_aliases={0: 0},
        scratch_shapes=[pltpu.SemaphoreType.DMA((1,)), pltpu.SemaphoreType.DMA((1,))],
        compiler_params=pltpu.CompilerParams(has_side_effects=True),
    )(land)

    rows = block.shape[0]
    tr = rows // 4

    def place(me_ref, x_ref, land_ref, out_ref):
        out_ref[...] = x_ref[...]

    x, y, c = _place()
    grid_spec = pltpu.PrefetchScalarGridSpec(
        num_scalar_prefetch=1, grid=(rows // tr,),
        in_specs=[pl.BlockSpec((tr, D), lambda i, me: (i, 0)), ANY_SPEC],
        out_specs=pl.BlockSpec((None, tr, D), lambda i, me: (me[0], i, 0)))
    return _pcall(
        place, name="allgather_rest_own", grid_spec=grid_spec, out_shape=jax.ShapeDtypeStruct(land.shape, land.dtype),
        input_output_aliases={2: 0}, compiler_params=pltpu.CompilerParams(dimension_semantics=("arbitrary",)),
    )((4 * x + 2 * y + c).reshape(1), block, land)


class _ReduceScatter:
    def __init__(self, name, g):
        self.name = name
        rows = g.shape[1]
        self.started = _start_copies(name + "_swap_start", g.reshape(4, 2, rows, D), (4, 1, rows, D), _swap_plan, 1)
        self.token = self.started[4]

    def halfway(self, after):
        g4, theirs = _wait_copies(self.name + "_swap_wait", self.started, after, _swap_plan)
        self.own = _add_halves(g4, theirs, lax.axis_index("c").reshape(1), self.name + "_add_halves")
        self.started = _start_copies(self.name + "_exch_start", self.own, self.own.shape, _exchange_plan, 3)
        self.token = self.started[4]

    def finish(self, after):
        own, got = _wait_copies(self.name + "_exch_wait", self.started, after, _exchange_plan)
        chip = 2 * lax.axis_index("x") + lax.axis_index("y")
        return own, got, (chip + jnp.arange(4, dtype=jnp.int32)) % 4


def _pack_small(p, loss):
    def body(*refs):
        o_ref = refs[-1]
        o_ref[...] = jnp.zeros_like(o_ref)
        for ref, name in zip(refs, SMALL):
            r0, nr, c0, nc = SMALL_SLOT[name]
            o_ref[r0:r0 + nr, c0:c0 + nc] = ref[...]
        o_ref[17:18, 0:128] = refs[len(SMALL)][0:1, :]

    return _one_call(body, "pack_small_grads", [p[n] for n in SMALL] + [loss], [((SMALL_ROWS, D), F32)])[0]


_GAP_DEV, _GAP_ROW = divmod(GATE0 + 8, N_IN)
_GAP = CHK0 - GATE0 - 8


def _in_rows_to_proj(g):
    runs = [(j, 0, N_IN, N_IN * j) for j in range(_GAP_DEV)]
    runs += [(_GAP_DEV, 0, _GAP_ROW, N_IN * _GAP_DEV), (_GAP_DEV, _GAP_ROW, N_IN, N_IN * _GAP_DEV + _GAP_ROW + _GAP)]
    runs += [(j, 0, N_IN, N_IN * j + _GAP) for j in range(_GAP_DEV + 1, 8)]

    def body(g_ref, o_ref, acc_ref):
        acc_ref[...] = jnp.zeros_like(acc_ref)
        for j, r0, r1, dest in runs:
            start, shift = dest // 16 * 16, dest % 16
            win = -(-(shift + r1 - r0) // 16) * 16
            r = lax.broadcasted_iota(jnp.int32, (win, R_IN), 0)
            c = lax.broadcasted_iota(jnp.int32, (win, R_IN), 1)
            move = jnp.where((c >= r0) & (c < r1) & (r == c - r0 + shift), 1.0, 0.0).astype(BF16)
            acc_ref[start:start + win, :] += _nn(move, g_ref[j])
        o_ref[...] = acc_ref[...].astype(BF16)

    return _pcall(
        body, name="w_in_layout", out_shape=jax.ShapeDtypeStruct((PROJ, D), BF16), grid=(1,),
        in_specs=[pl.BlockSpec(g.shape, lambda i: (0, 0, 0))], out_specs=pl.BlockSpec((PROJ, D), lambda i: (0, 0)),
        scratch_shapes=[pltpu.VMEM((PROJ, D), F32)],
        compiler_params=pltpu.CompilerParams(dimension_semantics=("arbitrary",), vmem_limit_bytes=VMEM_BIG),
    )(g)


def _wgrad_in(pieces, h1):
    n = len(pieces)
    ends = [sum(p.shape[1] for p in pieces[:k + 1]) for k in range(n)]
    assert ends[-1] == PROJ

    def body(*refs):
        piece_refs, (b_ref, o_ref, g_ref), bufs, sem = refs[:n], refs[n:n + 3], refs[n + 3:2 * n + 3], refs[2 * n + 3]
        i = pl.program_id(0)
        copies = [pltpu.make_async_copy(piece_refs[k], bufs[k], sem.at[k]) for k in range(n)]

        @pl.when(i == 0)
        def _():
            for copy in copies:
                copy.start()
            g_ref[PROJ:, :] = jnp.zeros((R_IN - N_IN + 1, D), F32)

        for k in range(n):
            @pl.when(i == k)
            def _(k=k):
                copies[k].wait()
                g_ref[ends[k] - pieces[k].shape[1]:ends[k], :] = _tn(bufs[k][...], b_ref[...])

        row = lax.broadcasted_iota(jnp.int32, (R_IN, D), 0)
        for j in range(8):
            lo = N_IN * j + (_GAP if j > _GAP_DEV else 0)
            hi = N_IN * j + (_GAP if j >= _GAP_DEV else 0)
            ready = min(k for k in range(n) if ends[k] >= min(hi + R_IN, PROJ))

            @pl.when(i == ready)
            def _(j=j, lo=lo, hi=hi):
                v = g_ref[hi:hi + R_IN, :]
                if lo != hi:
                    v = jnp.where(row < _GAP_ROW, g_ref[lo:lo + R_IN, :], v)
                o_ref[j] = jnp.where(row < N_IN, v, 0.0).astype(BF16)

    return _pcall(
        body, name="wgrad_in", grid=(n,),
        in_specs=[ANY_SPEC] * n + [_resident(h1)],
        out_specs=pl.BlockSpec((8, R_IN, D), lambda i: (0, 0, 0)),
        out_shape=jax.ShapeDtypeStruct((8, R_IN, D), BF16),
        scratch_shapes=[pltpu.VMEM((PROJ + R_IN - N_IN + 1, D), F32)] + [pltpu.VMEM(p.shape, BF16) for p in pieces]
        + [pltpu.SemaphoreType.DMA((n,))],
        compiler_params=pltpu.CompilerParams(dimension_semantics=("arbitrary",), vmem_limit_bytes=VMEM_BIG),
    )(*pieces, h1)


def _local_grads(x, mem, tgt, win_t, gw_of, sm, on_grads, after=()):
    b_pad = jnp.pad(sm['b_fgt'], ((0, 0), (0, 120)))
    tbl = jnp.pad(sm['rel_bias'], ((0, 0), (0, NREL_PAD - 257)))

    h1, proj, flog = _premix_fwd(x, sm['g_mix_pre'], win_t, after)
    c = _gate_fwd(flog, b_pad)
    ct3 = c[:, :8].T.reshape(4, 2, T)
    o_f, lse_f = _fox_fwd(proj, c, ct3)
    vt3 = _relvec_fwd(tbl).reshape(4, 2, VW)
    kvp = jnp.pad(proj[:, CHK0 + 512:], ((LEFT, 0), (0, 0)))
    o_c, lse_c = _chk_fwd(proj, kvp, vt3, [gw_of('relay', [o_f])])
    gw = gw_of('done', [o_c])
    w_out, w_mq, w_mk, w_mv, w_mo, w1_t, w2 = (_wblk(gw, n) for n in ('w_out', 'w_mq', 'w_mk', 'w_mv', 'w_mo', 'w_ff1', 'w_ff2'))
    ycat, z, x1, h2, qm = _postmix_fwd(x, o_f, o_c, sm['g_fox_out'], sm['g_chk_out'], w_out,
                                       sm['g_mix_post'], sm['g_mem_pre'], w_mq)
    memn, km, vm = _memkv_fwd(mem, sm['g_mem_kv'], w_mk, w_mv)
    om, ym, x2, h3 = _mem_fwd(qm, x1, km, vm, w_mo, sm['g_mem_post'], sm['g_ff_pre'])

    gs = {}
    dx2, da, dy3, r, loss_acc, gs['g_ff_post'], gs['g_ff_pre'] = _ffn_step(h3, x2, tgt, w1_t, w2, sm['g_ff_post'],
                                                                         sm['g_ff_pre'])
    tok = on_grads('A', _wgrad_group("wgrad_ff", [(da, h3), (r, dy3)], 512), None)
    dx1, dym, dqm, dkm, dvm, gs['g_mem_post'], gs['g_mem_pre'] = _mem_bwd(
        dx2, ym, x1, qm, km, vm, w_mo, w_mq, sm['g_mem_post'], sm['g_mem_pre'], [tok])
    tok = on_grads('A halfway', None, [dx1])
    gs['g_mem_kv'] = _memkv_bwd(dkm, dvm, mem, w_mk, w_mv)
    dz, dof, doc, gs['g_mix_post'], gs['g_fox_out'], gs['g_chk_out'] = _postmix_bwd(
        dx1, z, o_f, o_c, w_out, sm['g_mix_post'], sm['g_fox_out'], sm['g_chk_out'], [tok])
    tok = on_grads('B', _wgrad_whole("wgrad_mem_out", [(ycat, dz), (h2, dqm), (memn, dkm), (memn, dvm), (om, dym)]), None)
    dq_f, dk_f, dv_f, dct, dcq = _fox_bwd(proj, c, ct3, o_f, lse_f, dof, [tok])
    tok = on_grads('B halfway', None, [dq_f])
    dq_c, dk_c, dv_c, gv = _chk_bwd(proj, kvp, vt3, o_c, lse_c, doc, [tok])
    gs['rel_bias'] = _relvec_bwd(gv.reshape(8, VW))[:, :257]
    dc = jnp.pad(dct.reshape(8, T).T + dcq[:, :, :2].transpose(1, 0, 2).reshape(T, 8), ((0, 0), (0, 120)))
    dflog, db = _gate_bwd(dc, flog, b_pad)
    gs['b_fgt'] = db[0:1, :8]
    pieces = [dq_f, dk_f, dv_f, dflog, dq_c, dk_c, dv_c]
    on_grads('C', _wgrad_in(pieces, h1), None)
    tok = on_grads('C halfway', None, [gs['g_mem_kv']])
    grad_x, _, gs['g_mix_pre'] = _premix_bwd(dx1, x, pieces, win_t, sm['g_mix_pre'], [tok])
    return loss_acc, grad_x, gs


def kernel(x, mem, w_in, b_fgt, rel_bias, g_fox_out, g_chk_out, w_out, g_mix_pre, g_mix_post, g_mem_kv, w_mq, w_mk, w_mv, w_mo, g_mem_pre, g_mem_post, w_ff1, w_ff2, g_ff_pre, g_ff_post, loss_target, m_w_in, m_b_fgt, m_rel_bias, m_g_fox_out, m_g_chk_out, m_w_out, m_g_mix_pre, m_g_mix_post, m_g_mem_kv, m_w_mq, m_w_mk, m_w_mv, m_w_mo, m_g_mem_pre, m_g_mem_post, m_w_ff1, m_w_ff2, m_g_ff_pre, m_g_ff_post, v_w_in, v_b_fgt, v_rel_bias, v_g_fox_out, v_g_chk_out, v_w_out, v_g_mix_pre, v_g_mix_post, v_g_mem_kv, v_w_mq, v_w_mk, v_w_mv, v_w_mo, v_g_mem_pre, v_g_mem_post, v_w_ff1, v_w_ff2, v_g_ff_pre, v_g_ff_post):
    args = dict(locals())
    two_d = lambda a: a.reshape(a.shape[-2:])
    w = {n: two_d(args[n]) for n in WEIGHTS}
    m = {n: two_d(args['m_' + n]) for n in WEIGHTS}
    v = {n: two_d(args['v_' + n]) for n in WEIGHTS}

    sm = {n: w[n] for n in SMALL}
    shard_in = jnp.pad(w['w_in'].T, ((0, R_IN - N_IN), (0, 0))).astype(BF16)
    gathered_in, zero = _allgather(shard_in, "allgather_w_in")
    win_t = _in_rows_to_proj(gathered_in)
    shard_rest = (jnp.concatenate([w['w_ff1'].T, w['w_ff2'], w['w_out'], w['w_mq'], w['w_mk'], w['w_mv'], w['w_mo']],
                                  axis=0) + zero[0, 0]).astype(BF16)
    gather = {'first': _start_copies("allgather_rest_start", shard_rest, (8, R_REST, D), _gather_plan, 3)}

    def gw_of(stage, after):
        if stage == 'relay':
            gather['block'], land = _wait_copies("allgather_rest_wait", gather['first'], after, _gather_plan)
            gather['second'] = _start_inplace("allgather_rest_relay_start", land, _relay_plan, 4)
            return gather['second'][3]
        land = _wait_inplace("allgather_rest_relay_wait", gather['second'], after, _relay_plan)
        return _gather_forward(land, gather['block'])

    rs = {}

    def on_grads(stage, g, after):
        if stage.endswith('halfway'):
            rs[stage[0]].halfway(after)
            return rs[stage[0]].token
        rs[stage] = _ReduceScatter("rs_" + stage.lower(), g)
        return rs[stage].token

    loss_local, grad_x, gs = _local_grads(x[0], mem[0], loss_target[0], win_t, gw_of, sm, on_grads, [gather['first'][4]])
    grads, deltas, new_m, new_v = {}, {}, {}, {}

    def update(n, out):
        grads[n], deltas[n], new_m[n], new_v[n] = out

    own, got, order = rs['A'].finish([grad_x, rs['C'].token])
    update('w_ff1', _sum_adam(own, got, order, 0, w['w_ff1'], m['w_ff1'], v['w_ff1'], "adamw_w_ff1", transposed=True))
    update('w_ff2', _sum_adam(own, got, order, 512, w['w_ff2'], m['w_ff2'], v['w_ff2'], "adamw_w_ff2"))
    own, got, order = rs['B'].finish([grad_x, rs['C'].token])
    names_b = ('w_out', 'w_mq', 'w_mk', 'w_mv', 'w_mo')
    done = _sum_adam_rows(own, got, order, [w[n] for n in names_b], [m[n] for n in names_b], [v[n] for n in names_b],
                          "adamw_group_b")
    for k, n in enumerate(names_b):
        update(n, done[4 * k:4 * k + 4])

    own, got, order = rs['C'].finish([new_v[n] for n in BIG if n != 'w_in'])
    done = _sum_adam(own, got, order, 0, w['w_in'].T, m['w_in'].T, v['w_in'].T, "adamw_w_in")
    update('w_in', [a.T for a in done])

    gparts, _ = _allgather(_pack_small(gs, loss_local), "allgather_small_grads", [got])
    small = _adamw_small(gparts, [w[n] for n in SMALL], [m[n] for n in SMALL], [v[n] for n in SMALL])
    loss = small[0][0, 0]
    for t, n in enumerate(SMALL):
        update(n, small[1 + 4 * t:5 + 4 * t])

    out = [loss, grad_x[None]]
    for group in (grads, deltas, new_m, new_v):
        out += [group[n].reshape(args[n].shape) for n in WEIGHTS]
    return tuple(out)
```

```python
import jax
import jax.numpy as jnp
from jax import lax
from jax.experimental import pallas as pl
from jax.experimental.pallas import tpu as pltpu

F32 = jnp.float32
BF16 = jnp.bfloat16
MESH = pl.DeviceIdType.MESH

T = 2048
D = 1024
NMEM = 256
DFF = 4096
EPS = 1e-6
TM = 256
TM_WIDE = 512
TQ = 256
FQ = 512
HD = 64
SCALE = HD ** -0.5
MEM_HEADS = 4
MEM_HD = 256
MEM_SCALE = MEM_HD ** -0.5
NEG = -1e30
LEFT = 512
WIN = LEFT + TQ
VW = 1024
NREL_PAD = 384
PROJ = 3200
GATE0 = 1536
CHK0 = 1664
VMEM_BIG = 56 * 1024 * 1024

ADAM_LR = 0.001
ADAM_B1 = 0.9
ADAM_B2 = 0.999
ADAM_EPS = 1e-08
ADAM_WD = 0.01
ADAM_STEP = 10

N_IN = 385
R_IN = 400
R_REST = 1664
W_ROWS = {'w_ff1': (0, 512), 'w_ff2': (512, 512),
          'w_out': (1024, 128), 'w_mq': (1152, 128), 'w_mk': (1280, 128), 'w_mv': (1408, 128), 'w_mo': (1536, 128)}
SMALL_ROWS = 24
SMALL_SLOT = {'rel_bias': (0, 8, 0, 257), 'b_fgt': (8, 1, 0, 8), 'g_fox_out': (9, 1, 0, 512), 'g_chk_out': (9, 1, 512, 512),
              'g_mix_pre': (10, 1, 0, 1024), 'g_mix_post': (11, 1, 0, 1024), 'g_mem_kv': (12, 1, 0, 1024),
              'g_mem_pre': (13, 1, 0, 1024), 'g_mem_post': (14, 1, 0, 1024), 'g_ff_pre': (15, 1, 0, 1024),
              'g_ff_post': (16, 1, 0, 1024)}

WEIGHTS = ['w_in', 'b_fgt', 'rel_bias', 'g_fox_out', 'g_chk_out', 'w_out', 'g_mix_pre', 'g_mix_post', 'g_mem_kv',
           'w_mq', 'w_mk', 'w_mv', 'w_mo', 'g_mem_pre', 'g_mem_post', 'w_ff1', 'w_ff2', 'g_ff_pre', 'g_ff_post']
BIG = ['w_in', 'w_out', 'w_mq', 'w_mk', 'w_mv', 'w_mo', 'w_ff1', 'w_ff2']
SMALL = [n for n in WEIGHTS if n not in BIG]


def _pcall(body, **kw):
    return pl.pallas_call(body, **kw)


def _nn(a, b):
    return jnp.dot(a, b, preferred_element_type=F32)


def _nt(a, b):
    return lax.dot_general(a, b, (((1,), (1,)), ((), ())), preferred_element_type=F32)


def _tn(a, b):
    return lax.dot_general(a, b, (((0,), (0,)), ((), ())), preferred_element_type=F32)


def _w(ref):
    v = ref[...]
    return v if v.ndim == 2 else v.reshape(-1, v.shape[-1])


def _rstd(x):
    return lax.rsqrt(jnp.mean(x * x, axis=-1, keepdims=True) + EPS)


def _rms(x, g):
    return x * _rstd(x) * g


def _rms_bwd(x, g, dy):
    r = _rstd(x)
    xh = x * r
    dg = jnp.sum(dy * xh, axis=0, keepdims=True)
    dxh = dy * g
    dx = r * (dxh - xh * jnp.mean(dxh * xh, axis=-1, keepdims=True))
    return dx, dg


def _resident(a):
    if isinstance(a, tuple):
        _, shape, index = a
        return pl.BlockSpec(shape, lambda *_: index, pipeline_mode=pl.Buffered(1))
    return pl.BlockSpec(a.shape, lambda *_, nd=a.ndim: (0,) * nd, pipeline_mode=pl.Buffered(1))


def _wblk(gw, name):
    r0, rows = W_ROWS[name]
    return (gw, (8, rows, D), (0, r0 // rows, 0))


def _behind(body, n_in, after):
    if not after:
        return body
    return lambda *refs: body(*refs[:n_in], *refs[n_in + len(after):])


def _tok_call(body, name, tiled, full, outs_tiled, outs_acc=(), rows=T, tm=TM, vmem=None, after=()):
    in_specs = [pl.BlockSpec((tm, a.shape[1]), lambda i: (i, 0)) for a in tiled]
    in_specs += [_resident(a) for a in full] + [ANY_SPEC] * len(after)
    full = [a[0] if isinstance(a, tuple) else a for a in full] + list(after)
    body = _behind(body, len(tiled) + len(full) - len(after), after)
    out_shape = [jax.ShapeDtypeStruct((rows, c), dt) for c, dt in outs_tiled]
    out_shape += [jax.ShapeDtypeStruct(s, F32) for s in outs_acc]
    out_specs = [pl.BlockSpec((tm, c), lambda i: (i, 0)) for c, _ in outs_tiled]
    out_specs += [pl.BlockSpec(s, lambda i, nd=len(s): (0,) * nd) for s in outs_acc]
    return _pcall(
        body, name=name, grid=(rows // tm,), in_specs=in_specs, out_specs=out_specs, out_shape=out_shape,
        compiler_params=pltpu.CompilerParams(dimension_semantics=("arbitrary",), vmem_limit_bytes=vmem),
    )(*tiled, *full)


def _one_call(body, name, ins, outs, vmem=None):
    whole = lambda s: pl.BlockSpec(s, lambda i, nd=len(s): (0,) * nd)
    return _pcall(
        body, name=name, grid=(1,), in_specs=[_resident(a) for a in ins], out_specs=[whole(s) for s, _ in outs],
        out_shape=[jax.ShapeDtypeStruct(s, dt) for s, dt in outs],
        compiler_params=pltpu.CompilerParams(dimension_semantics=("arbitrary",), vmem_limit_bytes=vmem),
    )(*[a[0] if isinstance(a, tuple) else a for a in ins])


def _premix_fwd(x, g_pre, win_t, after=()):
    def body(x_ref, g_ref, w_ref, h_ref, proj_ref, flog_ref):
        h = _rms(x_ref[...], g_ref[...]).astype(BF16)
        h_ref[...] = h
        p = _nt(h, w_ref[...])
        proj_ref[...] = p.astype(BF16)
        flog_ref[...] = p[:, GATE0:GATE0 + 128]

    return _tok_call(body, "premix_fwd", [x], [g_pre, win_t],
                     [(D, BF16), (PROJ, BF16), (128, F32)], tm=TM_WIDE, vmem=VMEM_BIG, after=after)


def _postmix_fwd(x, o_f, o_c, g_fo, g_co, w_out, g_post, g_mpre, w_mq):
    def body(x_ref, of_ref, oc_ref, gfo_ref, gco_ref, wo_ref, gp_ref, gm_ref, wq_ref,
             y_ref, z_ref, x1_ref, h2_ref, qm_ref):
        y_ref[:, :512] = _rms(of_ref[...], gfo_ref[...]).astype(BF16)
        y_ref[:, 512:] = _rms(oc_ref[...], gco_ref[...]).astype(BF16)
        z = _nn(y_ref[...], _w(wo_ref))
        z_ref[...] = z
        x1 = x_ref[...] + _rms(z, gp_ref[...])
        x1_ref[...] = x1
        h2 = _rms(x1, gm_ref[...]).astype(BF16)
        h2_ref[...] = h2
        qm_ref[...] = _nn(h2, _w(wq_ref)).astype(BF16)

    return _tok_call(body, "postmix_fwd", [x, o_f, o_c], [g_fo, g_co, w_out, g_post, g_mpre, w_mq],
                     [(D, BF16), (D, F32), (D, F32), (D, BF16), (D, BF16)], tm=TM_WIDE, vmem=VMEM_BIG)


def _memkv_fwd(mem, g_kv, w_mk, w_mv):
    def body(m_ref, g_ref, wk_ref, wv_ref, mn_ref, k_ref, v_ref):
        mn = _rms(m_ref[...], g_ref[...]).astype(BF16)
        mn_ref[...] = mn
        k_ref[...] = _nn(mn, _w(wk_ref)).astype(BF16)
        v_ref[...] = _nn(mn, _w(wv_ref)).astype(BF16)

    return _tok_call(body, "memkv_fwd", [mem], [g_kv, w_mk, w_mv],
                     [(D, BF16), (D, BF16), (D, BF16)], rows=NMEM, tm=NMEM, vmem=VMEM_BIG)


def _mem_fwd(qm, x1, km, vm, w_mo, g_post, g_fpre):
    def body(q_ref, x1_ref, k_ref, v_ref, wo_ref, gp_ref, gf_ref, om_ref, ym_ref, x2_ref, h3_ref):
        for h in range(MEM_HEADS):
            sl = slice(h * MEM_HD, (h + 1) * MEM_HD)
            s = _nt(q_ref[:, sl], k_ref[:, sl]) * MEM_SCALE
            p = jnp.exp(s - jnp.max(s, axis=-1, keepdims=True))
            p = p / jnp.sum(p, axis=-1, keepdims=True)
            om_ref[:, sl] = _nn(p.astype(BF16), v_ref[:, sl]).astype(BF16)
        ym = _nn(om_ref[...], _w(wo_ref))
        ym_ref[...] = ym
        x2 = x1_ref[...] + _rms(ym, gp_ref[...])
        x2_ref[...] = x2
        h3_ref[...] = _rms(x2, gf_ref[...]).astype(BF16)

    return _tok_call(body, "mem_fwd", [qm, x1], [km, vm, w_mo, g_post, g_fpre],
                     [(D, BF16), (D, F32), (D, F32), (D, BF16)], tm=TM_WIDE, vmem=VMEM_BIG)


def _tri(lower):
    r = lax.broadcasted_iota(jnp.int32, (128, 128), 0)
    c = lax.broadcasted_iota(jnp.int32, (128, 128), 1)
    return jnp.where(r >= c if lower else c >= r, 1.0, 0.0).astype(F32)


def _hdot(a, b):
    return jnp.dot(a, b, preferred_element_type=F32, precision=lax.Precision.HIGHEST)


def _gate_fwd(flog, b_pad):
    def body(f_ref, b_ref, c_ref):
        tri = _tri(True)

        def step(i, carry):
            rows = pl.ds(pl.multiple_of(i * 128, 128), 128)
            z = f_ref[rows, :] + b_ref[...]
            lf = jnp.minimum(z, 0.0) - jnp.log(1.0 + jnp.exp(-jnp.abs(z)))
            cb = _hdot(tri, lf) + carry
            c_ref[rows, :] = cb
            return cb[127:128, :]

        lax.fori_loop(0, T // 128, step, jnp.zeros((1, 128), F32))

    return _one_call(body, "gate_fwd", [flog, b_pad], [((T, 128), F32)])[0]


def _gate_bwd(dc, flog, b_pad):
    def body(dc_ref, f_ref, b_ref, df_ref, db_ref):
        tri = _tri(False)

        def step(j, carry):
            run, db = carry
            i = T // 128 - 1 - j
            rows = pl.ds(pl.multiple_of(i * 128, 128), 128)
            dcb = dc_ref[rows, :]
            rb = _hdot(tri, dcb) + run
            z = f_ref[rows, :] + b_ref[...]
            df = rb * (1.0 / (1.0 + jnp.exp(z)))
            df_ref[rows, :] = df.astype(BF16)
            return run + jnp.sum(dcb, axis=0, keepdims=True), db + jnp.sum(df, axis=0, keepdims=True)

        _, db = lax.fori_loop(0, T // 128, step, (jnp.zeros((1, 128), F32), jnp.zeros((1, 128), F32)))
        db_ref[...] = jnp.broadcast_to(db, (8, 128))

    return _one_call(body, "gate_bwd", [dc, flog, b_pad], [((T, 128), BF16), ((8, 128), F32)])


def _lane_lo(rows=TQ):
    return lax.broadcasted_iota(jnp.int32, (rows, 128), 1) < HD


def _half(v, lo, a, scale=None):
    keep = lo if a == 0 else jnp.logical_not(lo)
    v = v.astype(F32) if scale is None else v.astype(F32) * scale
    return jnp.where(keep, v, 0.0).astype(BF16)


def _fox_specs():
    return [pl.BlockSpec((FQ, 128), lambda h, i: (i, h)),
            pl.BlockSpec((T, 128), lambda h, i: (0, 4 + h)),
            pl.BlockSpec((T, 128), lambda h, i: (0, 8 + h))]


def _lane_pick(x, at):
    lane = lax.broadcasted_iota(jnp.int32, x.shape, 1)
    return jnp.sum(jnp.where(lane == at, x, 0.0), axis=-1, keepdims=True)


def _fox_fwd(proj, c, ct3):
    def body(q_ref, k_ref, v_ref, c_ref, ct_ref, o_ref, l_ref):
        i = pl.program_id(1)
        lo = _lane_lo(FQ)
        causal = lax.broadcasted_iota(jnp.int32, (FQ, FQ), 1) <= lax.broadcasted_iota(jnp.int32, (FQ, FQ), 0)
        q = q_ref[...]
        qs = [_half(q, lo, a, SCALE) for a in range(2)]
        cqs = [_lane_pick(c_ref[...], 2 * pl.program_id(0) + a) for a in range(2)]

        def tile(off, carry, diagonal):
            kblk = k_ref[pl.ds(off, FQ), :]
            vblk = v_ref[pl.ds(off, FQ), :]
            new = []
            for a in range(2):
                m, l, acc = carry[a]
                s = _nt(qs[a], kblk) + (cqs[a] - ct_ref[a:a + 1, pl.ds(off, FQ)])
                if diagonal:
                    s = jnp.where(causal, s, NEG)
                m2 = jnp.maximum(m, jnp.max(s, axis=-1, keepdims=True))
                p = jnp.exp(s - m2)
                alpha = jnp.exp(m - m2)
                new.append((m2, alpha * l + jnp.sum(p, axis=-1, keepdims=True),
                            alpha * acc + _nn(p.astype(BF16), vblk)))
            return tuple(new)

        init = (jnp.full((FQ, 1), NEG, F32), jnp.zeros((FQ, 1), F32), jnp.zeros((FQ, 128), F32))
        carry = lax.fori_loop(0, i, lambda kb, c: tile(pl.multiple_of(kb * FQ, FQ), c, False), (init, init))
        carry = tile(pl.multiple_of(i * FQ, FQ), carry, True)
        outs = []
        for a in range(2):
            m, l, acc = carry[a]
            outs.append(acc / l)
            l_ref[:, 128 * a:128 * a + 128] = jnp.broadcast_to(m + jnp.log(l), (FQ, 128))
        o_ref[...] = jnp.where(lo, outs[0], outs[1])

    return _pcall(
        body, name="fox_fwd", grid=(4, T // FQ),
        in_specs=_fox_specs() + [pl.BlockSpec((FQ, 128), lambda h, i: (i, 0)),
                                 pl.BlockSpec((None, 2, T), lambda h, i: (h, 0, 0))],
        out_specs=[pl.BlockSpec((FQ, 128), lambda h, i: (i, h)), pl.BlockSpec((FQ, 256), lambda h, i: (i, h))],
        out_shape=[jax.ShapeDtypeStruct((T, 512), F32), jax.ShapeDtypeStruct((T, 1024), F32)],
        compiler_params=pltpu.CompilerParams(dimension_semantics=("arbitrary", "arbitrary"), vmem_limit_bytes=VMEM_BIG),
    )(proj, proj, proj, c, ct3)


def _fox_bwd(proj, c, ct3, o, lse, do, after=()):
    def body(q_ref, k_ref, v_ref, c_ref, ct_ref, o_ref, l_ref, do_ref, dq_ref, dkb_ref, dvb_ref, dct_ref, dcq_ref,
             dk_ref, dv_ref):
        i = pl.program_id(1)

        @pl.when(i == 0)
        def _():
            dk_ref[...] = jnp.zeros_like(dk_ref)
            dv_ref[...] = jnp.zeros_like(dv_ref)
            dct_ref[...] = jnp.zeros_like(dct_ref)

        lo = _lane_lo(FQ)
        causal = lax.broadcasted_iota(jnp.int32, (FQ, FQ), 1) <= lax.broadcasted_iota(jnp.int32, (FQ, FQ), 0)
        q = q_ref[...]
        do_v = do_ref[...]
        prod = do_v * o_ref[...]
        qs = [_half(q, lo, a, SCALE) for a in range(2)]
        dos = [_half(do_v, lo, a) for a in range(2)]
        deltas = [jnp.sum(jnp.where(lo if a == 0 else jnp.logical_not(lo), prod, 0.0), axis=-1, keepdims=True)
                  for a in range(2)]
        cqs = [_lane_pick(c_ref[...], 2 * pl.program_id(0) + a) for a in range(2)]
        las = [l_ref[:, 128 * a:128 * a + 1] for a in range(2)]

        def tile(off, carry, diagonal):
            kblk = k_ref[pl.ds(off, FQ), :]
            vblk = v_ref[pl.ds(off, FQ), :]
            new = []
            dk = jnp.zeros((128, FQ), F32)
            dv = jnp.zeros((128, FQ), F32)
            for a in range(2):
                dq_acc, rs = carry[a]
                s = _nt(qs[a], kblk) + (cqs[a] - ct_ref[a:a + 1, pl.ds(off, FQ)])
                if diagonal:
                    s = jnp.where(causal, s, NEG)
                p = jnp.exp(s - las[a])
                ds = p * (_nt(dos[a], vblk) - deltas[a])
                dsb = ds.astype(BF16)
                dk = dk + _tn(qs[a], dsb)
                dv = dv + _tn(dos[a], p.astype(BF16))
                dct_ref[a:a + 1, pl.ds(off, FQ)] -= jnp.sum(ds, axis=0, keepdims=True)
                new.append((dq_acc + _nn(dsb, kblk), rs + jnp.sum(ds, axis=-1, keepdims=True)))
            dk_ref[:, pl.ds(off, FQ)] += dk
            dv_ref[:, pl.ds(off, FQ)] += dv
            return tuple(new)

        init = (jnp.zeros((FQ, 128), F32), jnp.zeros((FQ, 1), F32))
        carry = lax.fori_loop(0, i, lambda kb, c: tile(pl.multiple_of(kb * FQ, FQ), c, False), (init, init))
        carry = tile(pl.multiple_of(i * FQ, FQ), carry, True)
        lane = lax.broadcasted_iota(jnp.int32, (FQ, 128), 1)
        dcq_ref[...] = jnp.where(lane == 0, carry[0][1], jnp.where(lane == 1, carry[1][1], 0.0))
        dq_ref[...] = (jnp.where(lo, carry[0][0], carry[1][0]) * SCALE).astype(BF16)

        @pl.when(i == T // FQ - 1)
        def _():
            dkb_ref[...] = dk_ref[...].T.astype(BF16)
            dvb_ref[...] = dv_ref[...].T.astype(BF16)

    blk = pl.BlockSpec((FQ, 128), lambda h, i: (i, h))
    wide = pl.BlockSpec((FQ, 256), lambda h, i: (i, h))
    rows = pl.BlockSpec((None, 2, T), lambda h, i: (h, 0, 0))
    col = pl.BlockSpec((T, 128), lambda h, i: (0, h))
    return _pcall(
        _behind(body, 8, after), name="fox_bwd", grid=(4, T // FQ),
        in_specs=_fox_specs() + [pl.BlockSpec((FQ, 128), lambda h, i: (i, 0)), rows, blk, wide, blk] + [ANY_SPEC] * len(after),
        out_specs=[blk, col, col, rows, pl.BlockSpec((None, FQ, 128), lambda h, i: (h, i, 0))],
        out_shape=[jax.ShapeDtypeStruct((T, 512), BF16), jax.ShapeDtypeStruct((T, 512), BF16),
                   jax.ShapeDtypeStruct((T, 512), BF16), jax.ShapeDtypeStruct((4, 2, T), F32),
                   jax.ShapeDtypeStruct((4, T, 128), F32)],
        scratch_shapes=[pltpu.VMEM((128, T), F32), pltpu.VMEM((128, T), F32)],
        compiler_params=pltpu.CompilerParams(dimension_semantics=("arbitrary", "arbitrary"), vmem_limit_bytes=VMEM_BIG),
    )(proj, proj, proj, c, ct3, o, lse, do, *after)


def _rel_onehot():
    ridx = lax.broadcasted_iota(jnp.int32, (NREL_PAD, VW), 0)
    j = lax.broadcasted_iota(jnp.int32, (NREL_PAD, VW), 1)
    return jnp.where(ridx == jnp.clip(TQ + LEFT - 1 - j, -128, 128) + 128, 1.0, 0.0).astype(F32)


def _relvec_fwd(tbl):
    def body(t_ref, v_ref):
        v_ref[...] = _hdot(t_ref[...], _rel_onehot())

    return _one_call(body, "relvec_fwd", [tbl], [((8, VW), F32)])[0]


def _relvec_bwd(gv):
    def body(g_ref, t_ref):
        t_ref[...] = lax.dot_general(g_ref[...], _rel_onehot(), (((1,), (1,)), ((), ())),
                                     preferred_element_type=F32, precision=lax.Precision.HIGHEST)

    return _one_call(body, "relvec_bwd", [gv], [((8, NREL_PAD), F32)])[0]


def _chk_bias(vt_ref, a, hidden):
    vb = jnp.broadcast_to(vt_ref[a:a + 1, :], (TQ, VW))
    y = pltpu.roll(vb, VW - (TQ - 1), 1, stride=1, stride_axis=0)[:, :WIN]
    cr = lax.broadcasted_iota(jnp.int32, (TQ, WIN), 0) // 64
    m = lax.broadcasted_iota(jnp.int32, (TQ, WIN), 1)
    return jnp.where((m // 64 >= cr) & (m // 64 <= cr + 8) & (m >= hidden), y, NEG)


def _chk_specs():
    return [pl.BlockSpec((TQ, 128), lambda h, i: (i, CHK0 // 128 + h)),
            pl.BlockSpec((T + LEFT, 128), lambda h, i: (0, h)),
            pl.BlockSpec((T + LEFT, 128), lambda h, i: (0, 4 + h)),
            pl.BlockSpec((None, 2, VW), lambda h, i: (h, 0, 0))]


def _chk_fwd(proj, kvp, vt3, after=()):
    def body(q_ref, k_ref, v_ref, vt_ref, o_ref, l_ref, bias_ref):
        i = pl.program_id(1)

        @pl.when(i == 0)
        def _():
            for first in range(3):
                for a in range(2):
                    bias_ref[first, a] = _chk_bias(vt_ref, a, max(LEFT - first * TQ, 0))

        lo = _lane_lo()
        off = pl.multiple_of(i * TQ, TQ)
        kw = k_ref[pl.ds(off, WIN), :]
        vw = v_ref[pl.ds(off, WIN), :]
        bias_at = jnp.minimum(i, 2)
        q = q_ref[...]
        outs = []
        for a in range(2):
            s = _nt(_half(q, lo, a, SCALE), kw) + bias_ref[bias_at, a]
            m = jnp.max(s, axis=-1, keepdims=True)
            p = jnp.exp(s - m)
            l = jnp.sum(p, axis=-1, keepdims=True)
            outs.append(_nn(p.astype(BF16), vw) / l)
            l_ref[:, 128 * a:128 * a + 128] = jnp.broadcast_to(m + jnp.log(l), (TQ, 128))
        o_ref[...] = jnp.where(lo, outs[0], outs[1])

    return _pcall(
        _behind(body, 4, after), name="chk_fwd", grid=(4, T // TQ), in_specs=_chk_specs() + [ANY_SPEC] * len(after),
        out_specs=[pl.BlockSpec((TQ, 128), lambda h, i: (i, h)), pl.BlockSpec((TQ, 256), lambda h, i: (i, h))],
        out_shape=[jax.ShapeDtypeStruct((T, 512), F32), jax.ShapeDtypeStruct((T, 1024), F32)],
        scratch_shapes=[pltpu.VMEM((3, 2, TQ, WIN), F32)],
        compiler_params=pltpu.CompilerParams(dimension_semantics=("arbitrary", "arbitrary")),
    )(proj, kvp, kvp, vt3, *after)


def _chk_bwd(proj, kvp, vt3, o, lse, do, after=()):
    nq = T // TQ

    def body(q_ref, k_ref, v_ref, vt_ref, o_ref, l_ref, do_ref, dq_ref, dkb_ref, dvb_ref, gv_ref, bias_ref, dsum_ref,
             dk_ref, dv_ref):
        i = pl.program_id(1)

        @pl.when(i == 0)
        def _():
            for first in range(3):
                for a in range(2):
                    bias_ref[first, a] = _chk_bias(vt_ref, a, max(LEFT - first * TQ, 0))
            dsum_ref[...] = jnp.zeros_like(dsum_ref)
            dk_ref[...] = jnp.zeros_like(dk_ref)
            dv_ref[...] = jnp.zeros_like(dv_ref)

        lo = _lane_lo()
        off = pl.multiple_of(i * TQ, TQ)
        kw = k_ref[pl.ds(off, WIN), :]
        vw = v_ref[pl.ds(off, WIN), :]
        bias_at = jnp.minimum(i, 2)
        q = q_ref[...]
        do_v = do_ref[...]
        prod = do_v * o_ref[...]
        dqs = []
        for a in range(2):
            keep = lo if a == 0 else jnp.logical_not(lo)
            qa = _half(q, lo, a, SCALE)
            doa = _half(do_v, lo, a)
            delta = jnp.sum(jnp.where(keep, prod, 0.0), axis=-1, keepdims=True)
            s = _nt(qa, kw) + bias_ref[bias_at, a]
            p = jnp.exp(s - l_ref[:, 128 * a:128 * a + 1])
            ds = p * (_nt(doa, vw) - delta)
            dsum_ref[a] += ds
            dsb = ds.astype(BF16)
            dk_ref[:, pl.ds(off, WIN)] += _tn(qa, dsb)
            dv_ref[:, pl.ds(off, WIN)] += _tn(doa, p.astype(BF16))
            dqs.append(_nn(dsb, kw))
        dq_ref[...] = (jnp.where(lo, dqs[0], dqs[1]) * SCALE).astype(BF16)

        @pl.when(i == nq - 1)
        def _():
            dkb_ref[...] = dk_ref[:, LEFT:].T.astype(BF16)
            dvb_ref[...] = dv_ref[:, LEFT:].T.astype(BF16)
            rr = lax.broadcasted_iota(jnp.int32, (TQ, TQ), 0)
            cc = lax.broadcasted_iota(jnp.int32, (TQ, TQ), 1)
            flip = jnp.where(rr + cc == TQ - 1, 1.0, 0.0).astype(F32)
            for a in range(2):
                dpad = jnp.concatenate([dsum_ref[a], jnp.zeros((TQ, VW - WIN), F32)], axis=1)
                z = pltpu.roll(_hdot(flip, dpad), 0, 1, stride=1, stride_axis=0)
                gv_ref[a:a + 1, :] = jnp.sum(z, axis=0, keepdims=True)

    blk = pl.BlockSpec((TQ, 128), lambda h, i: (i, h))
    wide = pl.BlockSpec((TQ, 256), lambda h, i: (i, h))
    col = pl.BlockSpec((T, 128), lambda h, i: (0, h))
    return _pcall(
        _behind(body, 7, after), name="chk_bwd", grid=(4, nq), in_specs=_chk_specs() + [blk, wide, blk] + [ANY_SPEC] * len(after),
        out_specs=[blk, col, col, pl.BlockSpec((None, 2, VW), lambda h, i: (h, 0, 0))],
        out_shape=[jax.ShapeDtypeStruct((T, 512), BF16), jax.ShapeDtypeStruct((T, 512), BF16),
                   jax.ShapeDtypeStruct((T, 512), BF16), jax.ShapeDtypeStruct((4, 2, VW), F32)],
        scratch_shapes=[pltpu.VMEM((3, 2, TQ, WIN), F32), pltpu.VMEM((2, TQ, WIN), F32),
                        pltpu.VMEM((128, T + LEFT), F32), pltpu.VMEM((128, T + LEFT), F32)],
        compiler_params=pltpu.CompilerParams(dimension_semantics=("arbitrary", "arbitrary")),
    )(proj, kvp, kvp, vt3, o, lse, do, *after)


def _zero_at_start(*refs):
    @pl.when(pl.program_id(0) == 0)
    def _():
        for r in refs:
            r[...] = jnp.zeros_like(r)


def _ffn_step(h3, x2, tgt, w1_t, w2, g_post, g_pre):
    def body(h_ref, x2_ref, t_ref, w1_ref, w2_ref, gp_ref, gf_ref, dx2_ref, da_ref, dy_ref, r_ref, loss_ref, dgp_ref, dgf_ref):
        _zero_at_start(loss_ref, dgp_ref, dgf_ref)
        w1, w2v = _w(w1_ref), _w(w2_ref)
        ra = jnp.maximum(_nt(h_ref[...], w1), 0.0)
        r = jnp.square(ra).astype(BF16)
        r_ref[...] = r
        y = _nn(r, w2v)
        x2v = x2_ref[...]
        e = x2v + _rms(y, gp_ref[...]) - t_ref[...]
        loss_ref[...] += 0.5 * jnp.sum(jnp.sum(e * e, axis=-1, keepdims=True) * (1.0 / D))
        dx3 = e * (1.0 / D)
        dy, dgp = _rms_bwd(y, gp_ref[...], dx3)
        dgp_ref[...] += dgp
        dyb = dy.astype(BF16)
        dy_ref[...] = dyb
        da = (_nt(dyb, w2v) * (2.0 * ra)).astype(BF16)
        da_ref[...] = da
        dh, dgf = _rms_bwd(x2v, gf_ref[...], _nn(da, w1))
        dgf_ref[...] += dgf
        dx2_ref[...] = dx3 + dh

    return _tok_call(body, "ffn_step", [h3, x2, tgt], [w1_t, w2, g_post, g_pre],
                     [(D, F32), (DFF, BF16), (D, BF16), (DFF, BF16)], [(8, 128), (1, D), (1, D)], vmem=VMEM_BIG)


def _mem_bwd(dx2, ym, x1, qm, km, vm, w_mo, w_mq, g_post, g_pre, after=()):
    def body(dx2_ref, ym_ref, x1_ref, q_ref, k_ref, v_ref, wo_ref, wq_ref, gp_ref, gm_ref,
             dx1_ref, dym_ref, dq_ref, dk_ref, dv_ref, dgp_ref, dgm_ref, dom_ref):
        _zero_at_start(dk_ref, dv_ref, dgp_ref, dgm_ref)
        dx2_v = dx2_ref[...]
        dym, dgp = _rms_bwd(ym_ref[...], gp_ref[...], dx2_v)
        dgp_ref[...] += dgp
        dymb = dym.astype(BF16)
        dym_ref[...] = dymb
        dom_ref[...] = _nt(dymb, _w(wo_ref)).astype(BF16)
        for h in range(MEM_HEADS):
            sl = slice(h * MEM_HD, (h + 1) * MEM_HD)
            qh, kh, doh = q_ref[:, sl], k_ref[:, sl], dom_ref[:, sl]
            s = _nt(qh, kh) * MEM_SCALE
            p = jnp.exp(s - jnp.max(s, axis=-1, keepdims=True))
            p = p / jnp.sum(p, axis=-1, keepdims=True)
            dp = _nt(doh, v_ref[:, sl])
            ds = (p * (dp - jnp.sum(p * dp, axis=-1, keepdims=True))).astype(BF16)
            dq_ref[:, sl] = (_nn(ds, kh) * MEM_SCALE).astype(BF16)
            dk_ref[:, sl] += _tn(ds, qh) * MEM_SCALE
            dv_ref[:, sl] += _tn(p.astype(BF16), doh)
        dh, dgm = _rms_bwd(x1_ref[...], gm_ref[...], _nt(dq_ref[...], _w(wq_ref)))
        dgm_ref[...] += dgm
        dx1_ref[...] = dx2_v + dh

    tiled = pl.BlockSpec((TM_WIDE, D), lambda i: (i, 0))
    in_specs = [tiled] * 4 + [_resident(a) for a in (km, vm, w_mo, w_mq, g_post, g_pre)] + [ANY_SPEC] * len(after)
    w_mo, w_mq = w_mo[0], w_mq[0]
    kv = pl.BlockSpec((NMEM, D), lambda i: (0, 0))
    vec = pl.BlockSpec((1, D), lambda i: (0, 0))
    return _pcall(
        _behind(body, 10, after), name="mem_bwd", grid=(T // TM_WIDE,), in_specs=in_specs,
        out_specs=[tiled, tiled, tiled, kv, kv, vec, vec],
        out_shape=[jax.ShapeDtypeStruct((T, D), F32), jax.ShapeDtypeStruct((T, D), BF16),
                   jax.ShapeDtypeStruct((T, D), BF16), jax.ShapeDtypeStruct((NMEM, D), F32),
                   jax.ShapeDtypeStruct((NMEM, D), F32), jax.ShapeDtypeStruct((1, D), F32),
                   jax.ShapeDtypeStruct((1, D), F32)],
        scratch_shapes=[pltpu.VMEM((TM_WIDE, D), BF16)],
        compiler_params=pltpu.CompilerParams(dimension_semantics=("arbitrary",), vmem_limit_bytes=VMEM_BIG),
    )(dx2, ym, x1, qm, km, vm, w_mo, w_mq, g_post, g_pre, *after)


def _memkv_bwd(dkm, dvm, mem, w_mk, w_mv):
    def body(dk_ref, dv_ref, m_ref, wk_ref, wv_ref, dg_ref):
        dmn = _nt(dk_ref[...].astype(BF16), _w(wk_ref)) + _nt(dv_ref[...].astype(BF16), _w(wv_ref))
        mv = m_ref[...]
        dg_ref[...] = jnp.sum(dmn * (mv * _rstd(mv)), axis=0, keepdims=True)

    return _one_call(body, "memkv_bwd", [dkm, dvm, mem, w_mk, w_mv], [((1, D), F32)], vmem=VMEM_BIG)[0]


def _postmix_bwd(dx1, z, o_f, o_c, w_out, g_post, g_fo, g_co, after=()):
    def body(dx1_ref, z_ref, of_ref, oc_ref, wo_ref, gp_ref, gfo_ref, gco_ref,
             dz_ref, dof_ref, doc_ref, dgp_ref, dgfo_ref, dgco_ref):
        _zero_at_start(dgp_ref, dgfo_ref, dgco_ref)
        dz, dgp = _rms_bwd(z_ref[...], gp_ref[...], dx1_ref[...])
        dgp_ref[...] += dgp
        dzb = dz.astype(BF16)
        dz_ref[...] = dzb
        dy = _nt(dzb, _w(wo_ref))
        dof, dgfo = _rms_bwd(of_ref[...], gfo_ref[...], dy[:, :512])
        doc, dgco = _rms_bwd(oc_ref[...], gco_ref[...], dy[:, 512:])
        dof_ref[...] = dof
        doc_ref[...] = doc
        dgfo_ref[...] += dgfo
        dgco_ref[...] += dgco

    return _tok_call(body, "postmix_bwd", [dx1, z, o_f, o_c], [w_out, g_post, g_fo, g_co],
                     [(D, BF16), (512, F32), (512, F32)], [(1, D), (1, 512), (1, 512)], tm=TM_WIDE, vmem=VMEM_BIG,
                     after=after)


def _premix_bwd(dx1, x, pieces, win_t, g_pre, after=()):
    def body(dx1_ref, x_ref, *refs):
        piece_refs, (w_ref, g_ref, dx_ref, dp_ref, dg_ref) = refs[:len(pieces)], refs[len(pieces):]
        _zero_at_start(dg_ref)
        col = 0
        for p in piece_refs:
            dp_ref[:, col:col + p.shape[1]] = p[...]
            col += p.shape[1]
        dh, dg = _rms_bwd(x_ref[...], g_ref[...], _nn(dp_ref[...], w_ref[...]))
        dg_ref[...] += dg
        dx_ref[...] = dx1_ref[...] + dh

    return _tok_call(body, "premix_bwd", [dx1, x] + list(pieces), [win_t, g_pre], [(D, F32), (PROJ, BF16)], [(1, D)],
                     tm=TM_WIDE, vmem=VMEM_BIG, after=after)


def _wgrad_group(name, pairs, rows):
    def body(*refs):
        o_ref = refs[-1]
        for k in range(len(pairs)):
            g = _tn(refs[2 * k][...].astype(BF16), refs[2 * k + 1][...].astype(BF16))
            o_ref[k * rows:(k + 1) * rows, :] = g.astype(BF16)

    in_specs, ops = [], []
    for a, b in pairs:
        in_specs += [pl.BlockSpec((a.shape[0], rows), lambda j: (0, j)), _resident(b)]
        ops += [a, b]
    return _pcall(
        body, name=name, grid=(8,), in_specs=in_specs,
        out_specs=pl.BlockSpec((None, len(pairs) * rows, D), lambda j: (j, 0, 0)),
        out_shape=jax.ShapeDtypeStruct((8, len(pairs) * rows, D), BF16),
        compiler_params=pltpu.CompilerParams(dimension_semantics=("arbitrary",), vmem_limit_bytes=VMEM_BIG),
    )(*ops)


def _wgrad_whole(name, pairs):
    n = len(pairs)
    ops = [x for pair in pairs for x in pair]
    rows = pairs[0][0].shape[1] // 8

    def body(*refs):
        hbm, o_ref, bufs, sem = refs[:2 * n], refs[2 * n], refs[2 * n + 1:4 * n + 1], refs[4 * n + 1]
        k = pl.program_id(0)
        copies = [pltpu.make_async_copy(hbm[t], bufs[t], sem.at[t]) for t in range(2 * n)]

        @pl.when(k == 0)
        def _():
            for copy in copies:
                copy.start()

        for t in range(n):
            @pl.when(k == t)
            def _(t=t):
                copies[2 * t].wait()
                copies[2 * t + 1].wait()
                g = _tn(bufs[2 * t][...].astype(BF16), bufs[2 * t + 1][...].astype(BF16))
                o_ref[...] = g.reshape(8, rows, D).astype(BF16)

    return _pcall(
        body, name=name, grid=(n,), in_specs=[ANY_SPEC] * (2 * n),
        out_specs=pl.BlockSpec((8, rows, D), lambda k: (0, k, 0)),
        out_shape=jax.ShapeDtypeStruct((8, n * rows, D), BF16),
        scratch_shapes=[pltpu.VMEM(x.shape, x.dtype) for x in ops] + [pltpu.SemaphoreType.DMA((2 * n,))],
        compiler_params=pltpu.CompilerParams(dimension_semantics=("arbitrary",), vmem_limit_bytes=VMEM_BIG),
    )(*ops)


def _adam_math(w, g, m, v):
    m2 = ADAM_B1 * m + (1.0 - ADAM_B1) * g
    v2 = ADAM_B2 * v + (1.0 - ADAM_B2) * jnp.square(g)
    m_hat = m2 / (1.0 - ADAM_B1 ** ADAM_STEP)
    v_hat = v2 / (1.0 - ADAM_B2 ** ADAM_STEP)
    delta = -ADAM_LR * (m_hat / (jnp.sqrt(v_hat) + ADAM_EPS) + ADAM_WD * w)
    return delta, m2, v2


def _adamw_small(gparts, ws, ms, vs):
    n = len(SMALL)

    def body(g_ref, *refs):
        w_refs, m_refs, v_refs = refs[:n], refs[n:2 * n], refs[2 * n:3 * n]
        outs, sum_ref = refs[3 * n:-1], refs[-1]
        g = g_ref[0]
        for k in range(1, 8):
            g = g + g_ref[k]
        sum_ref[...] = g
        outs[0][...] = sum_ref[17:18, 0:128]
        for t, name in enumerate(SMALL):
            r0, nr, c0, nc = SMALL_SLOT[name]
            gt = sum_ref[r0:r0 + nr, c0:c0 + nc]
            out = (gt,) + _adam_math(w_refs[t][...], gt, m_refs[t][...], v_refs[t][...])
            for o_ref, val in zip(outs[1 + 4 * t:5 + 4 * t], out):
                o_ref[...] = val

    whole = lambda s: pl.BlockSpec(s, lambda i, nd=len(s): (0,) * nd)
    ins = [gparts] + list(ws) + list(ms) + list(vs)
    out_shapes = [(1, 128)] + [a.shape for a in ws for _ in range(4)]
    return _pcall(
        body, name="adamw_small", grid=(1,), in_specs=[whole(a.shape) for a in ins],
        out_specs=[whole(s) for s in out_shapes], out_shape=[jax.ShapeDtypeStruct(s, F32) for s in out_shapes],
        scratch_shapes=[pltpu.VMEM((SMALL_ROWS, D), F32)],
        compiler_params=pltpu.CompilerParams(dimension_semantics=("arbitrary",)),
    )(*ins)


def _row_tile(rows):
    return next(t for t in (512, 400, 320) if rows % t == 0)


def _add_halves(g4, theirs, core, name):
    rows = g4.shape[2]
    tr = _row_tile(rows)

    def body(c_ref, a_ref, b_ref, o_ref):
        o_ref[...] = (a_ref[...].astype(F32) + b_ref[...].astype(F32)).astype(BF16)

    grid_spec = pltpu.PrefetchScalarGridSpec(
        num_scalar_prefetch=1, grid=(4, rows // tr),
        in_specs=[pl.BlockSpec((None, None, tr, D), lambda j, i, c: (j, c[0], i, 0)),
                  pl.BlockSpec((None, None, tr, D), lambda j, i, c: (j, 0, i, 0))],
        out_specs=pl.BlockSpec((None, tr, D), lambda j, i, c: (j, i, 0)))
    return _pcall(
        body, name=name, grid_spec=grid_spec, out_shape=jax.ShapeDtypeStruct((4, rows, D), BF16),
        compiler_params=pltpu.CompilerParams(dimension_semantics=("arbitrary", "arbitrary")),
    )(core, g4, theirs)


def _sum_adam(own, got, order, r0, w, m, v, name, transposed=False):
    ragged = w.ndim == 3
    n = w.shape[1] if transposed else w.shape[0]
    tr = n if ragged else min(n, 256)
    rows = own.shape[1] if ragged else tr
    at = (slice(None), 0, slice(None)) if ragged else Ellipsis

    def body(o_ref, a_ref, b_ref, c_ref, d_ref, w_ref, m_ref, v_ref, *out_refs):
        f = lambda r: r[0:tr, :].astype(F32)
        g = ((f(a_ref) + f(b_ref)) + f(c_ref)) + f(d_ref)
        g = g.T if transposed else g
        for ref, val in zip(out_refs, (g,) + _adam_math(w_ref[at], g, m_ref[at], v_ref[at])):
            ref[at] = val

    slot = lambda k: pl.BlockSpec((None, rows, D), lambda i, o: (o[k], r0 // rows + i, 0))
    if ragged:
        wspec = pl.BlockSpec((n, 1, D), lambda i, o: (0, 0, 0))
    else:
        wspec = pl.BlockSpec((D, tr), lambda i, o: (0, i)) if transposed else pl.BlockSpec((tr, D), lambda i, o: (i, 0))
    grid_spec = pltpu.PrefetchScalarGridSpec(
        num_scalar_prefetch=1, grid=(n // tr,), in_specs=[slot(0), slot(1), slot(2), slot(3), wspec, wspec, wspec],
        out_specs=[wspec] * 4)
    return _pcall(
        body, name=name, grid_spec=grid_spec, out_shape=[jax.ShapeDtypeStruct(w.shape, F32)] * 4,
        compiler_params=pltpu.CompilerParams(dimension_semantics=("arbitrary",)),
    )(order, own, got, got, got, w, m, v)


def _sum_adam_rows(own, got, order, ws, ms, vs, name):
    n, rows = len(ws), ws[0].shape[0]

    def body(o_ref, a_ref, b_ref, c_ref, d_ref, *refs):
        ins, outs = refs[:3 * n], refs[3 * n:]
        for t in range(n):
            r = slice(t * rows, (t + 1) * rows)
            f = lambda ref: ref[r, :].astype(F32)
            g = ((f(a_ref) + f(b_ref)) + f(c_ref)) + f(d_ref)
            out = (g,) + _adam_math(ins[t][...], g, ins[n + t][...], ins[2 * n + t][...])
            for o, val in zip(outs[4 * t:4 * t + 4], out):
                o[...] = val

    slot = lambda k: pl.BlockSpec((None, n * rows, D), lambda i, o: (o[k], 0, 0), pipeline_mode=pl.Buffered(1))
    wspec = pl.BlockSpec((rows, D), lambda i, o: (0, 0), pipeline_mode=pl.Buffered(1))
    grid_spec = pltpu.PrefetchScalarGridSpec(
        num_scalar_prefetch=1, grid=(1,), in_specs=[slot(0), slot(1), slot(2), slot(3)] + [wspec] * (3 * n),
        out_specs=[pl.BlockSpec((rows, D), lambda i, o: (0, 0))] * (4 * n))
    return _pcall(
        body, name=name, grid_spec=grid_spec, out_shape=[jax.ShapeDtypeStruct((rows, D), F32)] * (4 * n),
        compiler_params=pltpu.CompilerParams(dimension_semantics=("arbitrary",), vmem_limit_bytes=VMEM_BIG),
    )(order, own, got, got, got, *ws, *ms, *vs)


def _place():
    return lax.axis_index("x"), lax.axis_index("y"), lax.axis_index("c")


def _allgather(block, name, after=()):
    rows = block.shape[0]
    split = (rows // 2 + 15) // 16 * 16

    def body(x_ref, out_ref, token, send_sems, recv_sems, local_sem):
        token[...] = jnp.zeros_like(token)
        x, y, c = _place()
        me, sib = (x, y, c), (x, y, 1 - c)
        xn, yn, dg = (1 - x, y), (x, 1 - y), (1 - x, 1 - y)
        lo, hi = pl.ds(0, split), pl.ds(split, rows - split)

        def copy(k, blk, to, part=None, src=None):
            index = 4 * blk[0] + 2 * blk[1] + blk[2]
            view = out_ref.at[index] if part is None else out_ref.at[index, part]
            return pltpu.make_async_remote_copy(
                src_ref=view if src is None else src, dst_ref=view,
                send_sem=send_sems.at[k], recv_sem=recv_sems.at[k], device_id=to, device_id_type=MESH)

        def start(*copies):
            for cp in copies:
                cp.start()
            return list(copies)

        mine = pltpu.make_async_copy(x_ref, out_ref.at[4 * x + 2 * y + c], local_sem)
        mine.start()
        sent = start(copy(0, me, sib, src=x_ref), copy(1, me, (*xn, c), src=x_ref), copy(2, me, (*yn, c), src=x_ref))
        copy(1, (*xn, c), me).wait_recv()
        sent += start(copy(3, (*xn, c), sib), copy(5, (*xn, c), (*yn, c), part=lo))
        copy(2, (*yn, c), me).wait_recv()
        sent += start(copy(4, (*yn, c), sib), copy(6, (*yn, c), (*xn, c), part=hi))
        copy(5, (*dg, c), me, part=lo).wait_recv()
        copy(6, (*dg, c), me, part=hi).wait_recv()
        sent += start(copy(7, (*dg, c), sib))
        for k, blk in ((0, sib), (3, (*xn, 1 - c)), (4, (*yn, 1 - c)), (7, (*dg, 1 - c))):
            copy(k, blk, me).wait_recv()
        for cp in sent:
            cp.wait_send()
        mine.wait()

    return _pcall(
        _behind(body, 1, after), name=name,
        out_shape=[jax.ShapeDtypeStruct((8,) + block.shape, block.dtype), jax.ShapeDtypeStruct((8, 128), F32)],
        in_specs=[pl.BlockSpec(memory_space=pl.ANY)] * (1 + len(after)),
        out_specs=[pl.BlockSpec(memory_space=pl.ANY), pl.BlockSpec(memory_space=pltpu.VMEM)],
        scratch_shapes=[pltpu.SemaphoreType.DMA((8,)), pltpu.SemaphoreType.DMA((8,)), pltpu.SemaphoreType.DMA(())],
        compiler_params=pltpu.CompilerParams(has_side_effects=True),
    )(block, *after)


HBM_SPEC = pl.BlockSpec(memory_space=pltpu.HBM)
SEM_SPEC = pl.BlockSpec(memory_space=pltpu.SEMAPHORE)
ANY_SPEC = pl.BlockSpec(memory_space=pl.ANY)
EFFECT = pltpu.SideEffectType.DATAFLOW_SIDE_EFFECTING


def _in_hbm(a):
    return pltpu.with_memory_space_constraint(a, pltpu.HBM)


def _start_copies(name, src, land_shape, plan, n):
    def body(src_ref, land_ref, send_sems, recv_sems, src_thru, land_thru, token):
        for k, (s, d, to, _) in enumerate(plan(src_ref, land_ref)):
            pltpu.make_async_remote_copy(src_ref=s, dst_ref=d, send_sem=send_sems.at[k], recv_sem=recv_sems.at[k],
                                         device_id=to, device_id_type=MESH).start()
        token[...] = jnp.zeros_like(token)

    return _pcall(
        body, name=name,
        out_shape=(pltpu.SemaphoreType.DMA((n,)), pltpu.SemaphoreType.DMA((n,)), pltpu.HBM(src.shape, src.dtype),
                   pltpu.HBM(land_shape, src.dtype), jax.ShapeDtypeStruct((8, 128), F32)),
        in_specs=(HBM_SPEC, HBM_SPEC),
        out_specs=(SEM_SPEC, SEM_SPEC, HBM_SPEC, HBM_SPEC, pl.BlockSpec(memory_space=pltpu.VMEM)),
        input_output_aliases={0: 2, 1: 3}, compiler_params=pltpu.CompilerParams(has_side_effects=EFFECT),
    )(_in_hbm(src), _in_hbm(lax.empty(land_shape, src.dtype)))


def _wait_copies(name, started, after, plan):
    send_sems, recv_sems, src_thru, land_thru, _ = started

    def body(src_ref, land_ref, send_sems, recv_sems, *rest):
        for k, (s, _, to, mine) in enumerate(plan(src_ref, land_ref)):
            cp = pltpu.make_async_remote_copy(src_ref=s, dst_ref=mine, send_sem=send_sems.at[k],
                                              recv_sem=recv_sems.at[k], device_id=to, device_id_type=MESH)
            cp.wait_send()
            cp.wait_recv()

    return _pcall(
        body, name=name,
        out_shape=(pltpu.HBM(src_thru.shape, src_thru.dtype), pltpu.HBM(land_thru.shape, land_thru.dtype)),
        in_specs=(HBM_SPEC, HBM_SPEC, SEM_SPEC, SEM_SPEC) + (ANY_SPEC,) * len(after), out_specs=(HBM_SPEC, HBM_SPEC),
        input_output_aliases={0: 0, 1: 1}, compiler_params=pltpu.CompilerParams(has_side_effects=EFFECT),
    )(src_thru, land_thru, send_sems, recv_sems, *after)


def _start_inplace(name, buf, plan, n):
    def body(buf_ref, send_sems, recv_sems, buf_thru, token):
        for k, (s, d, to, _) in enumerate(plan(buf_ref, buf_ref)):
            pltpu.make_async_remote_copy(src_ref=s, dst_ref=d, send_sem=send_sems.at[k], recv_sem=recv_sems.at[k],
                                         device_id=to, device_id_type=MESH).start()
        token[...] = jnp.zeros_like(token)

    return _pcall(
        body, name=name,
        out_shape=(pltpu.SemaphoreType.DMA((n,)), pltpu.SemaphoreType.DMA((n,)), pltpu.HBM(buf.shape, buf.dtype),
                   jax.ShapeDtypeStruct((8, 128), F32)),
        in_specs=(HBM_SPEC,), out_specs=(SEM_SPEC, SEM_SPEC, HBM_SPEC, pl.BlockSpec(memory_space=pltpu.VMEM)),
        input_output_aliases={0: 2}, compiler_params=pltpu.CompilerParams(has_side_effects=EFFECT),
    )(_in_hbm(buf))


def _wait_inplace(name, started, after, plan):
    send_sems, recv_sems, buf_thru, _ = started

    def body(buf_ref, send_sems, recv_sems, *rest):
        for k, (s, _, to, mine) in enumerate(plan(buf_ref, buf_ref)):
            cp = pltpu.make_async_remote_copy(src_ref=s, dst_ref=mine, send_sem=send_sems.at[k],
                                              recv_sem=recv_sems.at[k], device_id=to, device_id_type=MESH)
            cp.wait_send()
            cp.wait_recv()

    return _pcall(
        body, name=name, out_shape=pltpu.HBM(buf_thru.shape, buf_thru.dtype),
        in_specs=(HBM_SPEC, SEM_SPEC, SEM_SPEC) + (ANY_SPEC,) * len(after), out_specs=HBM_SPEC,
        input_output_aliases={0: 0}, compiler_params=pltpu.CompilerParams(has_side_effects=EFFECT),
    )(buf_thru, send_sems, recv_sems, *after)


def _gather_plan(src_ref, land_ref):
    x, y, c = _place()
    peers = [(x, y, 1 - c), (1 - x, y, c), (x, 1 - y, c)]
    return [(src_ref, land_ref.at[4 * x + 2 * y + c], p, land_ref.at[4 * p[0] + 2 * p[1] + p[2]]) for p in peers]


def _relay_plan(buf_ref, _):
    x, y, c = _place()
    slot = lambda p, pc: 4 * p[0] + 2 * p[1] + pc
    xn, yn, dg, sib = (1 - x, y), (x, 1 - y), (1 - x, 1 - y), (x, y, 1 - c)
    half = buf_ref.shape[1] // 2
    lo, hi = pl.ds(0, half), pl.ds(half, half)
    return [(buf_ref.at[slot(xn, c)], buf_ref.at[slot(xn, c)], sib, buf_ref.at[slot(xn, 1 - c)]),
            (buf_ref.at[slot(yn, c)], buf_ref.at[slot(yn, c)], sib, buf_ref.at[slot(yn, 1 - c)]),
            (buf_ref.at[slot(xn, c), lo], buf_ref.at[slot(xn, c), lo], (*yn, c), buf_ref.at[slot(dg, c), lo]),
            (buf_ref.at[slot(yn, c), hi], buf_ref.at[slot(yn, c), hi], (*xn, c), buf_ref.at[slot(dg, c), hi])]


def _swap_plan(src_ref, land_ref):
    x, y, c = _place()
    return [(src_ref.at[:, pl.ds(1 - c, 1)], land_ref, (x, y, 1 - c), land_ref)]


def _exchange_plan(src_ref, land_ref):
    x, y, c = _place()
    chips = [(1 - x, y), (x, 1 - y), (1 - x, 1 - y)]
    return [(src_ref.at[2 * px + py], land_ref.at[2 * x + y], (px, py, c), land_ref.at[2 * px + py]) for px, py in chips]


def _gather_forward(land, block):
    def body(land_ref, out_ref, send_sems, recv_sems):
        x, y, c = _place()
        chips = [(1 - x, 1 - y)]

        def copy(k, px, py, pc):
            blk = out_ref.at[4 * px + 2 * py + pc]
            return pltpu.make_async_remote_copy(src_ref=blk, dst_ref=blk, send_sem=send_sems.at[k],
                                                recv_sem=recv_sems.at[k], device_id=(x, y, 1 - c), device_id_type=MESH)

        sent = [copy(k, px, py, c) for k, (px, py) in enumerate(chips)]
        for cp in sent:
            cp.start()
        for k, (px, py) in enumerate(chips):
            copy(k, px, py, 1 - c).wait_recv()
        for cp in sent:
            cp.wait_send()

    land = _pcall(
        body, name="allgather_rest_forward", out_shape=jax.ShapeDtypeStruct(land.shape, land.dtype),
        in_specs=[ANY_SPEC], out_specs=ANY_SPEC, input_output_aliases={0: 0},
        scratch_shapes=[pltpu.SemaphoreType.DMA((1,)), pltpu.SemaphoreType.DMA((1,))],
        compiler_params=pltpu.CompilerParams(has_side_effects=True),
    )(land)

    rows = block.shape[0]
    tr = rows // 4

    def place(me_ref, x_ref, land_ref, out_ref):
        out_ref[...] = x_ref[...]

    x, y, c = _place()
    grid_spec = pltpu.PrefetchScalarGridSpec(
        num_scalar_prefetch=1, grid=(rows // tr,),
        in_specs=[pl.BlockSpec((tr, D), lambda i, me: (i, 0)), ANY_SPEC],
        out_specs=pl.BlockSpec((None, tr, D), lambda i, me: (me[0], i, 0)))
    return _pcall(
        place, name="allgather_rest_own", grid_spec=grid_spec, out_shape=jax.ShapeDtypeStruct(land.shape, land.dtype),
        input_output_aliases={2: 0}, compiler_params=pltpu.CompilerParams(dimension_semantics=("arbitrary",)),
    )((4 * x + 2 * y + c).reshape(1), block, land)


class _ReduceScatter:
    def __init__(self, name, g):
        self.name = name
        rows = g.shape[1]
        self.started = _start_copies(name + "_swap_start", g.reshape(4, 2, rows, D), (4, 1, rows, D), _swap_plan, 1)
        self.token = self.started[4]

    def halfway(self, after):
        g4, theirs = _wait_copies(self.name + "_swap_wait", self.started, after, _swap_plan)
        self.own = _add_halves(g4, theirs, lax.axis_index("c").reshape(1), self.name + "_add_halves")
        self.started = _start_copies(self.name + "_exch_start", self.own, self.own.shape, _exchange_plan, 3)
        self.token = self.started[4]

    def finish(self, after):
        own, got = _wait_copies(self.name + "_exch_wait", self.started, after, _exchange_plan)
        chip = 2 * lax.axis_index("x") + lax.axis_index("y")
        return own, got, (chip + jnp.arange(4, dtype=jnp.int32)) % 4


def _pack_small(p, loss):
    def body(*refs):
        o_ref = refs[-1]
        o_ref[...] = jnp.zeros_like(o_ref)
        for ref, name in zip(refs, SMALL):
            r0, nr, c0, nc = SMALL_SLOT[name]
            o_ref[r0:r0 + nr, c0:c0 + nc] = ref[...]
        o_ref[17:18, 0:128] = refs[len(SMALL)][0:1, :]

    return _one_call(body, "pack_small_grads", [p[n] for n in SMALL] + [loss], [((SMALL_ROWS, D), F32)])[0]


_GAP_DEV, _GAP_ROW = divmod(GATE0 + 8, N_IN)
_GAP = CHK0 - GATE0 - 8


def _in_rows_to_proj(g):
    runs = [(j, 0, N_IN, N_IN * j) for j in range(_GAP_DEV)]
    runs += [(_GAP_DEV, 0, _GAP_ROW, N_IN * _GAP_DEV), (_GAP_DEV, _GAP_ROW, N_IN, N_IN * _GAP_DEV + _GAP_ROW + _GAP)]
    runs += [(j, 0, N_IN, N_IN * j + _GAP) for j in range(_GAP_DEV + 1, 8)]

    def body(g_ref, o_ref, acc_ref):
        acc_ref[...] = jnp.zeros_like(acc_ref)
        for j, r0, r1, dest in runs:
            start, shift = dest // 16 * 16, dest % 16
            win = -(-(shift + r1 - r0) // 16) * 16
            r = lax.broadcasted_iota(jnp.int32, (win, R_IN), 0)
            c = lax.broadcasted_iota(jnp.int32, (win, R_IN), 1)
            move = jnp.where((c >= r0) & (c < r1) & (r == c - r0 + shift), 1.0, 0.0).astype(BF16)
            acc_ref[start:start + win, :] += _nn(move, g_ref[j])
        o_ref[...] = acc_ref[...].astype(BF16)

    return _pcall(
        body, name="w_in_layout", out_shape=jax.ShapeDtypeStruct((PROJ, D), BF16), grid=(1,),
        in_specs=[pl.BlockSpec(g.shape, lambda i: (0, 0, 0))], out_specs=pl.BlockSpec((PROJ, D), lambda i: (0, 0)),
        scratch_shapes=[pltpu.VMEM((PROJ, D), F32)],
        compiler_params=pltpu.CompilerParams(dimension_semantics=("arbitrary",), vmem_limit_bytes=VMEM_BIG),
    )(g)


def _wgrad_in(pieces, h1):
    n = len(pieces)
    ends = [sum(p.shape[1] for p in pieces[:k + 1]) for k in range(n)]
    assert ends[-1] == PROJ

    def body(*refs):
        piece_refs, (b_ref, o_ref, g_ref), bufs, sem = refs[:n], refs[n:n + 3], refs[n + 3:2 * n + 3], refs[2 * n + 3]
        i = pl.program_id(0)
        copies = [pltpu.make_async_copy(piece_refs[k], bufs[k], sem.at[k]) for k in range(n)]

        @pl.when(i == 0)
        def _():
            for copy in copies:
                copy.start()
            g_ref[PROJ:, :] = jnp.zeros((R_IN - N_IN + 1, D), F32)

        for k in range(n):
            @pl.when(i == k)
            def _(k=k):
                copies[k].wait()
                g_ref[ends[k] - pieces[k].shape[1]:ends[k], :] = _tn(bufs[k][...], b_ref[...])

        row = lax.broadcasted_iota(jnp.int32, (R_IN, D), 0)
        for j in range(8):
            lo = N_IN * j + (_GAP if j > _GAP_DEV else 0)
            hi = N_IN * j + (_GAP if j >= _GAP_DEV else 0)
            ready = min(k for k in range(n) if ends[k] >= min(hi + R_IN, PROJ))

            @pl.when(i == ready)
            def _(j=j, lo=lo, hi=hi):
                v = g_ref[hi:hi + R_IN, :]
                if lo != hi:
                    v = jnp.where(row < _GAP_ROW, g_ref[lo:lo + R_IN, :], v)
                o_ref[j] = jnp.where(row < N_IN, v, 0.0).astype(BF16)

    return _pcall(
        body, name="wgrad_in", grid=(n,),
        in_specs=[ANY_SPEC] * n + [_resident(h1)],
        out_specs=pl.BlockSpec((8, R_IN, D), lambda i: (0, 0, 0)),
        out_shape=jax.ShapeDtypeStruct((8, R_IN, D), BF16),
        scratch_shapes=[pltpu.VMEM((PROJ + R_IN - N_IN + 1, D), F32)] + [pltpu.VMEM(p.shape, BF16) for p in pieces]
        + [pltpu.SemaphoreType.DMA((n,))],
        compiler_params=pltpu.CompilerParams(dimension_semantics=("arbitrary",), vmem_limit_bytes=VMEM_BIG),
    )(*pieces, h1)


def _local_grads(x, mem, tgt, win_t, gw_of, sm, on_grads, after=()):
    b_pad = jnp.pad(sm['b_fgt'], ((0, 0), (0, 120)))
    tbl = jnp.pad(sm['rel_bias'], ((0, 0), (0, NREL_PAD - 257)))

    h1, proj, flog = _premix_fwd(x, sm['g_mix_pre'], win_t, after)
    c = _gate_fwd(flog, b_pad)
    ct3 = c[:, :8].T.reshape(4, 2, T)
    o_f, lse_f = _fox_fwd(proj, c, ct3)
    vt3 = _relvec_fwd(tbl).reshape(4, 2, VW)
    kvp = jnp.pad(proj[:, CHK0 + 512:], ((LEFT, 0), (0, 0)))
    o_c, lse_c = _chk_fwd(proj, kvp, vt3, [gw_of('relay', [o_f])])
    gw = gw_of('done', [o_c])
    w_out, w_mq, w_mk, w_mv, w_mo, w1_t, w2 = (_wblk(gw, n) for n in ('w_out', 'w_mq', 'w_mk', 'w_mv', 'w_mo', 'w_ff1', 'w_ff2'))
    ycat, z, x1, h2, qm = _postmix_fwd(x, o_f, o_c, sm['g_fox_out'], sm['g_chk_out'], w_out,
                                       sm['g_mix_post'], sm['g_mem_pre'], w_mq)
    memn, km, vm = _memkv_fwd(mem, sm['g_mem_kv'], w_mk, w_mv)
    om, ym, x2, h3 = _mem_fwd(qm, x1, km, vm, w_mo, sm['g_mem_post'], sm['g_ff_pre'])

    gs = {}
    dx2, da, dy3, r, loss_acc, gs['g_ff_post'], gs['g_ff_pre'] = _ffn_step(h3, x2, tgt, w1_t, w2, sm['g_ff_post'],
                                                                         sm['g_ff_pre'])
    tok = on_grads('A', _wgrad_group("wgrad_ff", [(da, h3), (r, dy3)], 512), None)
    dx1, dym, dqm, dkm, dvm, gs['g_mem_post'], gs['g_mem_pre'] = _mem_bwd(
        dx2, ym, x1, qm, km, vm, w_mo, w_mq, sm['g_mem_post'], sm['g_mem_pre'], [tok])
    tok = on_grads('A halfway', None, [dx1])
    gs['g_mem_kv'] = _memkv_bwd(dkm, dvm, mem, w_mk, w_mv)
    dz, dof, doc, gs['g_mix_post'], gs['g_fox_out'], gs['g_chk_out'] = _postmix_bwd(
        dx1, z, o_f, o_c, w_out, sm['g_mix_post'], sm['g_fox_out'], sm['g_chk_out'], [tok])
    tok = on_grads('B', _wgrad_whole("wgrad_mem_out", [(ycat, dz), (h2, dqm), (memn, dkm), (memn, dvm), (om, dym)]), None)
    dq_f, dk_f, dv_f, dct, dcq = _fox_bwd(proj, c, ct3, o_f, lse_f, dof, [tok])
    tok = on_grads('B halfway', None, [dq_f])
    dq_c, dk_c, dv_c, gv = _chk_bwd(proj, kvp, vt3, o_c, lse_c, doc, [tok])
    gs['rel_bias'] = _relvec_bwd(gv.reshape(8, VW))[:, :257]
    dc = jnp.pad(dct.reshape(8, T).T + dcq[:, :, :2].transpose(1, 0, 2).reshape(T, 8), ((0, 0), (0, 120)))
    dflog, db = _gate_bwd(dc, flog, b_pad)
    gs['b_fgt'] = db[0:1, :8]
    pieces = [dq_f, dk_f, dv_f, dflog, dq_c, dk_c, dv_c]
    on_grads('C', _wgrad_in(pieces, h1), None)
    tok = on_grads('C halfway', None, [gs['g_mem_kv']])
    grad_x, _, gs['g_mix_pre'] = _premix_bwd(dx1, x, pieces, win_t, sm['g_mix_pre'], [tok])
    return loss_acc, grad_x, gs


def kernel(x, mem, w_in, b_fgt, rel_bias, g_fox_out, g_chk_out, w_out, g_mix_pre, g_mix_post, g_mem_kv, w_mq, w_mk, w_mv, w_mo, g_mem_pre, g_mem_post, w_ff1, w_ff2, g_ff_pre, g_ff_post, loss_target, m_w_in, m_b_fgt, m_rel_bias, m_g_fox_out, m_g_chk_out, m_w_out, m_g_mix_pre, m_g_mix_post, m_g_mem_kv, m_w_mq, m_w_mk, m_w_mv, m_w_mo, m_g_mem_pre, m_g_mem_post, m_w_ff1, m_w_ff2, m_g_ff_pre, m_g_ff_post, v_w_in, v_b_fgt, v_rel_bias, v_g_fox_out, v_g_chk_out, v_w_out, v_g_mix_pre, v_g_mix_post, v_g_mem_kv, v_w_mq, v_w_mk, v_w_mv, v_w_mo, v_g_mem_pre, v_g_mem_post, v_w_ff1, v_w_ff2, v_g_ff_pre, v_g_ff_post):
    args = dict(locals())
    two_d = lambda a: a.reshape(a.shape[-2:])
    w = {n: two_d(args[n]) for n in WEIGHTS}
    m = {n: two_d(args['m_' + n]) for n in WEIGHTS}
    v = {n: two_d(args['v_' + n]) for n in WEIGHTS}

    sm = {n: w[n] for n in SMALL}
    shard_in = jnp.pad(w['w_in'].T, ((0, R_IN - N_IN), (0, 0))).astype(BF16)
    gathered_in, zero = _allgather(shard_in, "allgather_w_in")
    win_t = _in_rows_to_proj(gathered_in)
    shard_rest = (jnp.concatenate([w['w_ff1'].T, w['w_ff2'], w['w_out'], w['w_mq'], w['w_mk'], w['w_mv'], w['w_mo']],
                                  axis=0) + zero[0, 0]).astype(BF16)
    gather = {'first': _start_copies("allgather_rest_start", shard_rest, (8, R_REST, D), _gather_plan, 3)}

    def gw_of(stage, after):
        if stage == 'relay':
            gather['block'], land = _wait_copies("allgather_rest_wait", gather['first'], after, _gather_plan)
            gather['second'] = _start_inplace("allgather_rest_relay_start", land, _relay_plan, 4)
            return gather['second'][3]
        land = _wait_inplace("allgather_rest_relay_wait", gather['second'], after, _relay_plan)
        return _gather_forward(land, gather['block'])

    rs = {}

    def on_grads(stage, g, after):
        if stage.endswith('halfway'):
            rs[stage[0]].halfway(after)
            return rs[stage[0]].token
        rs[stage] = _ReduceScatter("rs_" + stage.lower(), g)
        return rs[stage].token

    loss_local, grad_x, gs = _local_grads(x[0], mem[0], loss_target[0], win_t, gw_of, sm, on_grads, [gather['first'][4]])
    grads, deltas, new_m, new_v = {}, {}, {}, {}

    def update(n, out):
        grads[n], deltas[n], new_m[n], new_v[n] = out

    own, got, order = rs['A'].finish([grad_x, rs['C'].token])
    update('w_ff1', _sum_adam(own, got, order, 0, w['w_ff1'], m['w_ff1'], v['w_ff1'], "adamw_w_ff1", transposed=True))
    update('w_ff2', _sum_adam(own, got, order, 512, w['w_ff2'], m['w_ff2'], v['w_ff2'], "adamw_w_ff2"))
    own, got, order = rs['B'].finish([grad_x, rs['C'].token])
    names_b = ('w_out', 'w_mq', 'w_mk', 'w_mv', 'w_mo')
    done = _sum_adam_rows(own, got, order, [w[n] for n in names_b], [m[n] for n in names_b], [v[n] for n in names_b],
                          "adamw_group_b")
    for k, n in enumerate(names_b):
        update(n, done[4 * k:4 * k + 4])

    own, got, order = rs['C'].finish([new_v[n] for n in BIG if n != 'w_in'])
    rows_of = lambda a: a.T[:, None, :]
    done = _sum_adam(own, got, order, 0, rows_of(w['w_in']), rows_of(m['w_in']), rows_of(v['w_in']), "adamw_w_in")
    update('w_in', [a[:, 0, :].T for a in done])

    gparts, _ = _allgather(_pack_small(gs, loss_local), "allgather_small_grads", [got])
    small = _adamw_small(gparts, [w[n] for n in SMALL], [m[n] for n in SMALL], [v[n] for n in SMALL])
    loss = small[0][0, 0]
    for t, n in enumerate(SMALL):
        update(n, small[1 + 4 * t:5 + 4 * t])

    out = [loss, grad_x[None]]
    for group in (grads, deltas, new_m, new_v):
        out += [group[n].reshape(args[n].shape) for n in WEIGHTS]
    return tuple(out)
```

```python
import jax
import jax.numpy as jnp
from jax import lax
from jax.experimental import pallas as pl
from jax.experimental.pallas import tpu as pltpu

F32 = jnp.float32
BF16 = jnp.bfloat16
MESH = pl.DeviceIdType.MESH

T = 2048
D = 1024
NMEM = 256
DFF = 4096
EPS = 1e-6
TM = 256
TM_WIDE = 512
TQ = 256
FQ = 512
HD = 64
SCALE = HD ** -0.5
MEM_HEADS = 4
MEM_HD = 256
MEM_SCALE = MEM_HD ** -0.5
NEG = -1e30
LEFT = 512
WIN = LEFT + TQ
VW = 1024
NREL_PAD = 384
PROJ = 3200
GATE0 = 1536
CHK0 = 1664
VMEM_BIG = 56 * 1024 * 1024

ADAM_LR = 0.001
ADAM_B1 = 0.9
ADAM_B2 = 0.999
ADAM_EPS = 1e-08
ADAM_WD = 0.01
ADAM_STEP = 10

N_IN = 385
R_IN = 400
R_REST = 1664
W_ROWS = {'w_ff1': (0, 512), 'w_ff2': (512, 512),
          'w_out': (1024, 128), 'w_mq': (1152, 128), 'w_mk': (1280, 128), 'w_mv': (1408, 128), 'w_mo': (1536, 128)}
SMALL_ROWS = 24
SMALL_SLOT = {'rel_bias': (0, 8, 0, 257), 'b_fgt': (8, 1, 0, 8), 'g_fox_out': (9, 1, 0, 512), 'g_chk_out': (9, 1, 512, 512),
              'g_mix_pre': (10, 1, 0, 1024), 'g_mix_post': (11, 1, 0, 1024), 'g_mem_kv': (12, 1, 0, 1024),
              'g_mem_pre': (13, 1, 0, 1024), 'g_mem_post': (14, 1, 0, 1024), 'g_ff_pre': (15, 1, 0, 1024),
              'g_ff_post': (16, 1, 0, 1024)}

WEIGHTS = ['w_in', 'b_fgt', 'rel_bias', 'g_fox_out', 'g_chk_out', 'w_out', 'g_mix_pre', 'g_mix_post', 'g_mem_kv',
           'w_mq', 'w_mk', 'w_mv', 'w_mo', 'g_mem_pre', 'g_mem_post', 'w_ff1', 'w_ff2', 'g_ff_pre', 'g_ff_post']
BIG = ['w_in', 'w_out', 'w_mq', 'w_mk', 'w_mv', 'w_mo', 'w_ff1', 'w_ff2']
SMALL = [n for n in WEIGHTS if n not in BIG]


def _pcall(body, **kw):
    return pl.pallas_call(body, **kw)


def _nn(a, b):
    return jnp.dot(a, b, preferred_element_type=F32)


def _nt(a, b):
    return lax.dot_general(a, b, (((1,), (1,)), ((), ())), preferred_element_type=F32)


def _tn(a, b):
    return lax.dot_general(a, b, (((0,), (0,)), ((), ())), preferred_element_type=F32)


def _w(ref):
    v = ref[...]
    return v if v.ndim == 2 else v.reshape(-1, v.shape[-1])


def _rstd(x):
    return lax.rsqrt(jnp.mean(x * x, axis=-1, keepdims=True) + EPS)


def _rms(x, g):
    return x * _rstd(x) * g


def _rms_bwd(x, g, dy):
    r = _rstd(x)
    xh = x * r
    dg = jnp.sum(dy * xh, axis=0, keepdims=True)
    dxh = dy * g
    dx = r * (dxh - xh * jnp.mean(dxh * xh, axis=-1, keepdims=True))
    return dx, dg


def _resident(a):
    if isinstance(a, tuple):
        _, shape, index = a
        return pl.BlockSpec(shape, lambda *_: index, pipeline_mode=pl.Buffered(1))
    return pl.BlockSpec(a.shape, lambda *_, nd=a.ndim: (0,) * nd, pipeline_mode=pl.Buffered(1))


def _wblk(gw, name):
    r0, rows = W_ROWS[name]
    return (gw, (8, rows, D), (0, r0 // rows, 0))


def _behind(body, n_in, after):
    if not after:
        return body
    return lambda *refs: body(*refs[:n_in], *refs[n_in + len(after):])


def _tok_call(body, name, tiled, full, outs_tiled, outs_acc=(), rows=T, tm=TM, vmem=None, after=()):
    in_specs = [pl.BlockSpec((tm, a.shape[1]), lambda i: (i, 0)) for a in tiled]
    in_specs += [_resident(a) for a in full] + [ANY_SPEC] * len(after)
    full = [a[0] if isinstance(a, tuple) else a for a in full] + list(after)
    body = _behind(body, len(tiled) + len(full) - len(after), after)
    out_shape = [jax.ShapeDtypeStruct((rows, c), dt) for c, dt in outs_tiled]
    out_shape += [jax.ShapeDtypeStruct(s, F32) for s in outs_acc]
    out_specs = [pl.BlockSpec((tm, c), lambda i: (i, 0)) for c, _ in outs_tiled]
    out_specs += [pl.BlockSpec(s, lambda i, nd=len(s): (0,) * nd) for s in outs_acc]
    return _pcall(
        body, name=name, grid=(rows // tm,), in_specs=in_specs, out_specs=out_specs, out_shape=out_shape,
        compiler_params=pltpu.CompilerParams(dimension_semantics=("arbitrary",), vmem_limit_bytes=vmem),
    )(*tiled, *full)


def _one_call(body, name, ins, outs, vmem=None):
    whole = lambda s: pl.BlockSpec(s, lambda i, nd=len(s): (0,) * nd)
    return _pcall(
        body, name=name, grid=(1,), in_specs=[_resident(a) for a in ins], out_specs=[whole(s) for s, _ in outs],
        out_shape=[jax.ShapeDtypeStruct(s, dt) for s, dt in outs],
        compiler_params=pltpu.CompilerParams(dimension_semantics=("arbitrary",), vmem_limit_bytes=vmem),
    )(*[a[0] if isinstance(a, tuple) else a for a in ins])


def _premix_fwd(x, g_pre, win_t, after=()):
    def body(x_ref, g_ref, w_ref, h_ref, proj_ref, flog_ref):
        h = _rms(x_ref[...], g_ref[...]).astype(BF16)
        h_ref[...] = h
        p = _nt(h, w_ref[...])
        proj_ref[...] = p.astype(BF16)
        flog_ref[...] = p[:, GATE0:GATE0 + 128]

    return _tok_call(body, "premix_fwd", [x], [g_pre, win_t],
                     [(D, BF16), (PROJ, BF16), (128, F32)], tm=TM_WIDE, vmem=VMEM_BIG, after=after)


def _postmix_fwd(x, o_f, o_c, g_fo, g_co, w_out, g_post, g_mpre, w_mq):
    def body(x_ref, of_ref, oc_ref, gfo_ref, gco_ref, wo_ref, gp_ref, gm_ref, wq_ref,
             y_ref, z_ref, x1_ref, h2_ref, qm_ref):
        y_ref[:, :512] = _rms(of_ref[...], gfo_ref[...]).astype(BF16)
        y_ref[:, 512:] = _rms(oc_ref[...], gco_ref[...]).astype(BF16)
        z = _nn(y_ref[...], _w(wo_ref))
        z_ref[...] = z
        x1 = x_ref[...] + _rms(z, gp_ref[...])
        x1_ref[...] = x1
        h2 = _rms(x1, gm_ref[...]).astype(BF16)
        h2_ref[...] = h2
        qm_ref[...] = _nn(h2, _w(wq_ref)).astype(BF16)

    return _tok_call(body, "postmix_fwd", [x, o_f, o_c], [g_fo, g_co, w_out, g_post, g_mpre, w_mq],
                     [(D, BF16), (D, F32), (D, F32), (D, BF16), (D, BF16)], tm=TM_WIDE, vmem=VMEM_BIG)


def _memkv_fwd(mem, g_kv, w_mk, w_mv):
    def body(m_ref, g_ref, wk_ref, wv_ref, mn_ref, k_ref, v_ref):
        mn = _rms(m_ref[...], g_ref[...]).astype(BF16)
        mn_ref[...] = mn
        k_ref[...] = _nn(mn, _w(wk_ref)).astype(BF16)
        v_ref[...] = _nn(mn, _w(wv_ref)).astype(BF16)

    return _tok_call(body, "memkv_fwd", [mem], [g_kv, w_mk, w_mv],
                     [(D, BF16), (D, BF16), (D, BF16)], rows=NMEM, tm=NMEM, vmem=VMEM_BIG)


def _mem_fwd(qm, x1, km, vm, w_mo, g_post, g_fpre):
    def body(q_ref, x1_ref, k_ref, v_ref, wo_ref, gp_ref, gf_ref, om_ref, ym_ref, x2_ref, h3_ref):
        for h in range(MEM_HEADS):
            sl = slice(h * MEM_HD, (h + 1) * MEM_HD)
            s = _nt(q_ref[:, sl], k_ref[:, sl]) * MEM_SCALE
            p = jnp.exp(s - jnp.max(s, axis=-1, keepdims=True))
            p = p / jnp.sum(p, axis=-1, keepdims=True)
            om_ref[:, sl] = _nn(p.astype(BF16), v_ref[:, sl]).astype(BF16)
        ym = _nn(om_ref[...], _w(wo_ref))
        ym_ref[...] = ym
        x2 = x1_ref[...] + _rms(ym, gp_ref[...])
        x2_ref[...] = x2
        h3_ref[...] = _rms(x2, gf_ref[...]).astype(BF16)

    return _tok_call(body, "mem_fwd", [qm, x1], [km, vm, w_mo, g_post, g_fpre],
                     [(D, BF16), (D, F32), (D, F32), (D, BF16)], tm=TM_WIDE, vmem=VMEM_BIG)


def _tri(lower):
    r = lax.broadcasted_iota(jnp.int32, (128, 128), 0)
    c = lax.broadcasted_iota(jnp.int32, (128, 128), 1)
    return jnp.where(r >= c if lower else c >= r, 1.0, 0.0).astype(F32)


def _hdot(a, b):
    return jnp.dot(a, b, preferred_element_type=F32, precision=lax.Precision.HIGHEST)


def _gate_fwd(flog, b_pad):
    def body(f_ref, b_ref, c_ref):
        tri = _tri(True)

        def step(i, carry):
            rows = pl.ds(pl.multiple_of(i * 128, 128), 128)
            z = f_ref[rows, :] + b_ref[...]
            lf = jnp.minimum(z, 0.0) - jnp.log(1.0 + jnp.exp(-jnp.abs(z)))
            cb = _hdot(tri, lf) + carry
            c_ref[rows, :] = cb
            return cb[127:128, :]

        lax.fori_loop(0, T // 128, step, jnp.zeros((1, 128), F32))

    return _one_call(body, "gate_fwd", [flog, b_pad], [((T, 128), F32)])[0]


def _gate_bwd(dc, flog, b_pad):
    def body(dc_ref, f_ref, b_ref, df_ref, db_ref):
        tri = _tri(False)

        def step(j, carry):
            run, db = carry
            i = T // 128 - 1 - j
            rows = pl.ds(pl.multiple_of(i * 128, 128), 128)
            dcb = dc_ref[rows, :]
            rb = _hdot(tri, dcb) + run
            z = f_ref[rows, :] + b_ref[...]
            df = rb * (1.0 / (1.0 + jnp.exp(z)))
            df_ref[rows, :] = df.astype(BF16)
            return run + jnp.sum(dcb, axis=0, keepdims=True), db + jnp.sum(df, axis=0, keepdims=True)

        _, db = lax.fori_loop(0, T // 128, step, (jnp.zeros((1, 128), F32), jnp.zeros((1, 128), F32)))
        db_ref[...] = jnp.broadcast_to(db, (8, 128))

    return _one_call(body, "gate_bwd", [dc, flog, b_pad], [((T, 128), BF16), ((8, 128), F32)])


def _lane_lo(rows=TQ):
    return lax.broadcasted_iota(jnp.int32, (rows, 128), 1) < HD


def _half(v, lo, a, scale=None):
    keep = lo if a == 0 else jnp.logical_not(lo)
    v = v.astype(F32) if scale is None else v.astype(F32) * scale
    return jnp.where(keep, v, 0.0).astype(BF16)


def _fox_specs():
    return [pl.BlockSpec((FQ, 128), lambda h, i: (i, h)),
            pl.BlockSpec((T, 128), lambda h, i: (0, 4 + h)),
            pl.BlockSpec((T, 128), lambda h, i: (0, 8 + h))]


def _lane_pick(x, at):
    lane = lax.broadcasted_iota(jnp.int32, x.shape, 1)
    return jnp.sum(jnp.where(lane == at, x, 0.0), axis=-1, keepdims=True)


def _fox_fwd(proj, c, ct3):
    def body(q_ref, k_ref, v_ref, c_ref, ct_ref, o_ref, l_ref):
        i = pl.program_id(1)
        lo = _lane_lo(FQ)
        causal = lax.broadcasted_iota(jnp.int32, (FQ, FQ), 1) <= lax.broadcasted_iota(jnp.int32, (FQ, FQ), 0)
        q = q_ref[...]
        qs = [_half(q, lo, a, SCALE) for a in range(2)]
        cqs = [_lane_pick(c_ref[...], 2 * pl.program_id(0) + a) for a in range(2)]

        def tile(off, carry, diagonal):
            kblk = k_ref[pl.ds(off, FQ), :]
            vblk = v_ref[pl.ds(off, FQ), :]
            new = []
            for a in range(2):
                m, l, acc = carry[a]
                s = _nt(qs[a], kblk) + (cqs[a] - ct_ref[a:a + 1, pl.ds(off, FQ)])
                if diagonal:
                    s = jnp.where(causal, s, NEG)
                m2 = jnp.maximum(m, jnp.max(s, axis=-1, keepdims=True))
                p = jnp.exp(s - m2)
                alpha = jnp.exp(m - m2)
                new.append((m2, alpha * l + jnp.sum(p, axis=-1, keepdims=True),
                            alpha * acc + _nn(p.astype(BF16), vblk)))
            return tuple(new)

        init = (jnp.full((FQ, 1), NEG, F32), jnp.zeros((FQ, 1), F32), jnp.zeros((FQ, 128), F32))
        carry = lax.fori_loop(0, i, lambda kb, c: tile(pl.multiple_of(kb * FQ, FQ), c, False), (init, init))
        carry = tile(pl.multiple_of(i * FQ, FQ), carry, True)
        outs = []
        for a in range(2):
            m, l, acc = carry[a]
            outs.append(acc / l)
            l_ref[:, 128 * a:128 * a + 128] = jnp.broadcast_to(m + jnp.log(l), (FQ, 128))
        o_ref[...] = jnp.where(lo, outs[0], outs[1])

    return _pcall(
        body, name="fox_fwd", grid=(4, T // FQ),
        in_specs=_fox_specs() + [pl.BlockSpec((FQ, 128), lambda h, i: (i, 0)),
                                 pl.BlockSpec((None, 2, T), lambda h, i: (h, 0, 0))],
        out_specs=[pl.BlockSpec((FQ, 128), lambda h, i: (i, h)), pl.BlockSpec((FQ, 256), lambda h, i: (i, h))],
        out_shape=[jax.ShapeDtypeStruct((T, 512), F32), jax.ShapeDtypeStruct((T, 1024), F32)],
        compiler_params=pltpu.CompilerParams(dimension_semantics=("arbitrary", "arbitrary"), vmem_limit_bytes=VMEM_BIG),
    )(proj, proj, proj, c, ct3)


def _fox_bwd(proj, c, ct3, o, lse, do, after=()):
    def body(q_ref, k_ref, v_ref, c_ref, ct_ref, o_ref, l_ref, do_ref, dq_ref, dkb_ref, dvb_ref, dct_ref, dcq_ref,
             dk_ref, dv_ref):
        i = pl.program_id(1)

        @pl.when(i == 0)
        def _():
            dk_ref[...] = jnp.zeros_like(dk_ref)
            dv_ref[...] = jnp.zeros_like(dv_ref)
            dct_ref[...] = jnp.zeros_like(dct_ref)

        lo = _lane_lo(FQ)
        causal = lax.broadcasted_iota(jnp.int32, (FQ, FQ), 1) <= lax.broadcasted_iota(jnp.int32, (FQ, FQ), 0)
        q = q_ref[...]
        do_v = do_ref[...]
        prod = do_v * o_ref[...]
        qs = [_half(q, lo, a, SCALE) for a in range(2)]
        dos = [_half(do_v, lo, a) for a in range(2)]
        deltas = [jnp.sum(jnp.where(lo if a == 0 else jnp.logical_not(lo), prod, 0.0), axis=-1, keepdims=True)
                  for a in range(2)]
        cqs = [_lane_pick(c_ref[...], 2 * pl.program_id(0) + a) for a in range(2)]
        las = [l_ref[:, 128 * a:128 * a + 1] for a in range(2)]

        def tile(off, carry, diagonal):
            kblk = k_ref[pl.ds(off, FQ), :]
            vblk = v_ref[pl.ds(off, FQ), :]
            new = []
            dk = jnp.zeros((128, FQ), F32)
            dv = jnp.zeros((128, FQ), F32)
            for a in range(2):
                dq_acc, rs = carry[a]
                s = _nt(qs[a], kblk) + (cqs[a] - ct_ref[a:a + 1, pl.ds(off, FQ)])
                if diagonal:
                    s = jnp.where(causal, s, NEG)
                p = jnp.exp(s - las[a])
                ds = p * (_nt(dos[a], vblk) - deltas[a])
                dsb = ds.astype(BF16)
                dk = dk + _tn(qs[a], dsb)
                dv = dv + _tn(dos[a], p.astype(BF16))
                dct_ref[a:a + 1, pl.ds(off, FQ)] -= jnp.sum(ds, axis=0, keepdims=True)
                new.append((dq_acc + _nn(dsb, kblk), rs + jnp.sum(ds, axis=-1, keepdims=True)))
            dk_ref[:, pl.ds(off, FQ)] += dk
            dv_ref[:, pl.ds(off, FQ)] += dv
            return tuple(new)

        init = (jnp.zeros((FQ, 128), F32), jnp.zeros((FQ, 1), F32))
        carry = lax.fori_loop(0, i, lambda kb, c: tile(pl.multiple_of(kb * FQ, FQ), c, False), (init, init))
        carry = tile(pl.multiple_of(i * FQ, FQ), carry, True)
        lane = lax.broadcasted_iota(jnp.int32, (FQ, 128), 1)
        dcq_ref[...] = jnp.where(lane == 0, carry[0][1], jnp.where(lane == 1, carry[1][1], 0.0))
        dq_ref[...] = (jnp.where(lo, carry[0][0], carry[1][0]) * SCALE).astype(BF16)

        @pl.when(i == T // FQ - 1)
        def _():
            dkb_ref[...] = dk_ref[...].T.astype(BF16)
            dvb_ref[...] = dv_ref[...].T.astype(BF16)

    blk = pl.BlockSpec((FQ, 128), lambda h, i: (i, h))
    wide = pl.BlockSpec((FQ, 256), lambda h, i: (i, h))
    rows = pl.BlockSpec((None, 2, T), lambda h, i: (h, 0, 0))
    col = pl.BlockSpec((T, 128), lambda h, i: (0, h))
    return _pcall(
        _behind(body, 8, after), name="fox_bwd", grid=(4, T // FQ),
        in_specs=_fox_specs() + [pl.BlockSpec((FQ, 128), lambda h, i: (i, 0)), rows, blk, wide, blk] + [ANY_SPEC] * len(after),
        out_specs=[blk, col, col, rows, pl.BlockSpec((None, FQ, 128), lambda h, i: (h, i, 0))],
        out_shape=[jax.ShapeDtypeStruct((T, 512), BF16), jax.ShapeDtypeStruct((T, 512), BF16),
                   jax.ShapeDtypeStruct((T, 512), BF16), jax.ShapeDtypeStruct((4, 2, T), F32),
                   jax.ShapeDtypeStruct((4, T, 128), F32)],
        scratch_shapes=[pltpu.VMEM((128, T), F32), pltpu.VMEM((128, T), F32)],
        compiler_params=pltpu.CompilerParams(dimension_semantics=("arbitrary", "arbitrary"), vmem_limit_bytes=VMEM_BIG),
    )(proj, proj, proj, c, ct3, o, lse, do, *after)


def _rel_onehot():
    ridx = lax.broadcasted_iota(jnp.int32, (NREL_PAD, VW), 0)
    j = lax.broadcasted_iota(jnp.int32, (NREL_PAD, VW), 1)
    return jnp.where(ridx == jnp.clip(TQ + LEFT - 1 - j, -128, 128) + 128, 1.0, 0.0).astype(F32)


def _relvec_fwd(tbl):
    def body(t_ref, v_ref):
        v_ref[...] = _hdot(t_ref[...], _rel_onehot())

    return _one_call(body, "relvec_fwd", [tbl], [((8, VW), F32)])[0]


def _relvec_bwd(gv):
    def body(g_ref, t_ref):
        t_ref[...] = lax.dot_general(g_ref[...], _rel_onehot(), (((1,), (1,)), ((), ())),
                                     preferred_element_type=F32, precision=lax.Precision.HIGHEST)

    return _one_call(body, "relvec_bwd", [gv], [((8, NREL_PAD), F32)])[0]


def _chk_bias(vt_ref, a, hidden):
    vb = jnp.broadcast_to(vt_ref[a:a + 1, :], (TQ, VW))
    y = pltpu.roll(vb, VW - (TQ - 1), 1, stride=1, stride_axis=0)[:, :WIN]
    cr = lax.broadcasted_iota(jnp.int32, (TQ, WIN), 0) // 64
    m = lax.broadcasted_iota(jnp.int32, (TQ, WIN), 1)
    return jnp.where((m // 64 >= cr) & (m // 64 <= cr + 8) & (m >= hidden), y, NEG)


def _chk_specs():
    return [pl.BlockSpec((TQ, 128), lambda h, i: (i, CHK0 // 128 + h)),
            pl.BlockSpec((T + LEFT, 128), lambda h, i: (0, h)),
            pl.BlockSpec((T + LEFT, 128), lambda h, i: (0, 4 + h)),
            pl.BlockSpec((None, 2, VW), lambda h, i: (h, 0, 0))]


def _chk_fwd(proj, kvp, vt3, after=()):
    def body(q_ref, k_ref, v_ref, vt_ref, o_ref, l_ref, bias_ref):
        i = pl.program_id(1)

        @pl.when(i == 0)
        def _():
            for first in range(3):
                for a in range(2):
                    bias_ref[first, a] = _chk_bias(vt_ref, a, max(LEFT - first * TQ, 0))

        lo = _lane_lo()
        off = pl.multiple_of(i * TQ, TQ)
        kw = k_ref[pl.ds(off, WIN), :]
        vw = v_ref[pl.ds(off, WIN), :]
        bias_at = jnp.minimum(i, 2)
        q = q_ref[...]
        outs = []
        for a in range(2):
            s = _nt(_half(q, lo, a, SCALE), kw) + bias_ref[bias_at, a]
            m = jnp.max(s, axis=-1, keepdims=True)
            p = jnp.exp(s - m)
            l = jnp.sum(p, axis=-1, keepdims=True)
            outs.append(_nn(p.astype(BF16), vw) / l)
            l_ref[:, 128 * a:128 * a + 128] = jnp.broadcast_to(m + jnp.log(l), (TQ, 128))
        o_ref[...] = jnp.where(lo, outs[0], outs[1])

    return _pcall(
        _behind(body, 4, after), name="chk_fwd", grid=(4, T // TQ), in_specs=_chk_specs() + [ANY_SPEC] * len(after),
        out_specs=[pl.BlockSpec((TQ, 128), lambda h, i: (i, h)), pl.BlockSpec((TQ, 256), lambda h, i: (i, h))],
        out_shape=[jax.ShapeDtypeStruct((T, 512), F32), jax.ShapeDtypeStruct((T, 1024), F32)],
        scratch_shapes=[pltpu.VMEM((3, 2, TQ, WIN), F32)],
        compiler_params=pltpu.CompilerParams(dimension_semantics=("arbitrary", "arbitrary")),
    )(proj, kvp, kvp, vt3, *after)


def _chk_bwd(proj, kvp, vt3, o, lse, do, after=()):
    nq = T // TQ

    def body(q_ref, k_ref, v_ref, vt_ref, o_ref, l_ref, do_ref, dq_ref, dkb_ref, dvb_ref, gv_ref, bias_ref, dsum_ref,
             dk_ref, dv_ref):
        i = pl.program_id(1)

        @pl.when(i == 0)
        def _():
            for first in range(3):
                for a in range(2):
                    bias_ref[first, a] = _chk_bias(vt_ref, a, max(LEFT - first * TQ, 0))
            dsum_ref[...] = jnp.zeros_like(dsum_ref)
            dk_ref[...] = jnp.zeros_like(dk_ref)
            dv_ref[...] = jnp.zeros_like(dv_ref)

        lo = _lane_lo()
        off = pl.multiple_of(i * TQ, TQ)
        kw = k_ref[pl.ds(off, WIN), :]
        vw = v_ref[pl.ds(off, WIN), :]
        bias_at = jnp.minimum(i, 2)
        q = q_ref[...]
        do_v = do_ref[...]
        prod = do_v * o_ref[...]
        dqs = []
        for a in range(2):
            keep = lo if a == 0 else jnp.logical_not(lo)
            qa = _half(q, lo, a, SCALE)
            doa = _half(do_v, lo, a)
            delta = jnp.sum(jnp.where(keep, prod, 0.0), axis=-1, keepdims=True)
            s = _nt(qa, kw) + bias_ref[bias_at, a]
            p = jnp.exp(s - l_ref[:, 128 * a:128 * a + 1])
            ds = p * (_nt(doa, vw) - delta)
            dsum_ref[a] += ds
            dsb = ds.astype(BF16)
            dk_ref[:, pl.ds(off, WIN)] += _tn(qa, dsb)
            dv_ref[:, pl.ds(off, WIN)] += _tn(doa, p.astype(BF16))
            dqs.append(_nn(dsb, kw))
        dq_ref[...] = (jnp.where(lo, dqs[0], dqs[1]) * SCALE).astype(BF16)

        @pl.when(i == nq - 1)
        def _():
            dkb_ref[...] = dk_ref[:, LEFT:].T.astype(BF16)
            dvb_ref[...] = dv_ref[:, LEFT:].T.astype(BF16)
            rr = lax.broadcasted_iota(jnp.int32, (TQ, TQ), 0)
            cc = lax.broadcasted_iota(jnp.int32, (TQ, TQ), 1)
            flip = jnp.where(rr + cc == TQ - 1, 1.0, 0.0).astype(F32)
            for a in range(2):
                dpad = jnp.concatenate([dsum_ref[a], jnp.zeros((TQ, VW - WIN), F32)], axis=1)
                z = pltpu.roll(_hdot(flip, dpad), 0, 1, stride=1, stride_axis=0)
                gv_ref[a:a + 1, :] = jnp.sum(z, axis=0, keepdims=True)

    blk = pl.BlockSpec((TQ, 128), lambda h, i: (i, h))
    wide = pl.BlockSpec((TQ, 256), lambda h, i: (i, h))
    col = pl.BlockSpec((T, 128), lambda h, i: (0, h))
    return _pcall(
        _behind(body, 7, after), name="chk_bwd", grid=(4, nq), in_specs=_chk_specs() + [blk, wide, blk] + [ANY_SPEC] * len(after),
        out_specs=[blk, col, col, pl.BlockSpec((None, 2, VW), lambda h, i: (h, 0, 0))],
        out_shape=[jax.ShapeDtypeStruct((T, 512), BF16), jax.ShapeDtypeStruct((T, 512), BF16),
                   jax.ShapeDtypeStruct((T, 512), BF16), jax.ShapeDtypeStruct((4, 2, VW), F32)],
        scratch_shapes=[pltpu.VMEM((3, 2, TQ, WIN), F32), pltpu.VMEM((2, TQ, WIN), F32),
                        pltpu.VMEM((128, T + LEFT), F32), pltpu.VMEM((128, T + LEFT), F32)],
        compiler_params=pltpu.CompilerParams(dimension_semantics=("arbitrary", "arbitrary")),
    )(proj, kvp, kvp, vt3, o, lse, do, *after)


def _zero_at_start(*refs):
    @pl.when(pl.program_id(0) == 0)
    def _():
        for r in refs:
            r[...] = jnp.zeros_like(r)


def _ffn_step(h3, x2, tgt, w1_t, w2, g_post, g_pre):
    def body(h_ref, x2_ref, t_ref, w1_ref, w2_ref, gp_ref, gf_ref, dx2_ref, da_ref, dy_ref, r_ref, loss_ref, dgp_ref, dgf_ref):
        _zero_at_start(loss_ref, dgp_ref, dgf_ref)
        w1, w2v = _w(w1_ref), _w(w2_ref)
        ra = jnp.maximum(_nt(h_ref[...], w1), 0.0)
        r = jnp.square(ra).astype(BF16)
        r_ref[...] = r
        y = _nn(r, w2v)
        x2v = x2_ref[...]
        e = x2v + _rms(y, gp_ref[...]) - t_ref[...]
        loss_ref[...] += 0.5 * jnp.sum(jnp.sum(e * e, axis=-1, keepdims=True) * (1.0 / D))
        dx3 = e * (1.0 / D)
        dy, dgp = _rms_bwd(y, gp_ref[...], dx3)
        dgp_ref[...] += dgp
        dyb = dy.astype(BF16)
        dy_ref[...] = dyb
        da = (_nt(dyb, w2v) * (2.0 * ra)).astype(BF16)
        da_ref[...] = da
        dh, dgf = _rms_bwd(x2v, gf_ref[...], _nn(da, w1))
        dgf_ref[...] += dgf
        dx2_ref[...] = dx3 + dh

    return _tok_call(body, "ffn_step", [h3, x2, tgt], [w1_t, w2, g_post, g_pre],
                     [(D, F32), (DFF, BF16), (D, BF16), (DFF, BF16)], [(8, 128), (1, D), (1, D)], vmem=VMEM_BIG)


def _mem_bwd(dx2, ym, x1, qm, km, vm, w_mo, w_mq, g_post, g_pre, after=()):
    def body(dx2_ref, ym_ref, x1_ref, q_ref, k_ref, v_ref, wo_ref, wq_ref, gp_ref, gm_ref,
             dx1_ref, dym_ref, dq_ref, dk_ref, dv_ref, dgp_ref, dgm_ref, dom_ref):
        _zero_at_start(dk_ref, dv_ref, dgp_ref, dgm_ref)
        dx2_v = dx2_ref[...]
        dym, dgp = _rms_bwd(ym_ref[...], gp_ref[...], dx2_v)
        dgp_ref[...] += dgp
        dymb = dym.astype(BF16)
        dym_ref[...] = dymb
        dom_ref[...] = _nt(dymb, _w(wo_ref)).astype(BF16)
        for h in range(MEM_HEADS):
            sl = slice(h * MEM_HD, (h + 1) * MEM_HD)
            qh, kh, doh = q_ref[:, sl], k_ref[:, sl], dom_ref[:, sl]
            s = _nt(qh, kh) * MEM_SCALE
            p = jnp.exp(s - jnp.max(s, axis=-1, keepdims=True))
            p = p / jnp.sum(p, axis=-1, keepdims=True)
            dp = _nt(doh, v_ref[:, sl])
            ds = (p * (dp - jnp.sum(p * dp, axis=-1, keepdims=True))).astype(BF16)
            dq_ref[:, sl] = (_nn(ds, kh) * MEM_SCALE).astype(BF16)
            dk_ref[:, sl] += _tn(ds, qh) * MEM_SCALE
            dv_ref[:, sl] += _tn(p.astype(BF16), doh)
        dh, dgm = _rms_bwd(x1_ref[...], gm_ref[...], _nt(dq_ref[...], _w(wq_ref)))
        dgm_ref[...] += dgm
        dx1_ref[...] = dx2_v + dh

    tiled = pl.BlockSpec((TM_WIDE, D), lambda i: (i, 0))
    in_specs = [tiled] * 4 + [_resident(a) for a in (km, vm, w_mo, w_mq, g_post, g_pre)] + [ANY_SPEC] * len(after)
    w_mo, w_mq = w_mo[0], w_mq[0]
    kv = pl.BlockSpec((NMEM, D), lambda i: (0, 0))
    vec = pl.BlockSpec((1, D), lambda i: (0, 0))
    return _pcall(
        _behind(body, 10, after), name="mem_bwd", grid=(T // TM_WIDE,), in_specs=in_specs,
        out_specs=[tiled, tiled, tiled, kv, kv, vec, vec],
        out_shape=[jax.ShapeDtypeStruct((T, D), F32), jax.ShapeDtypeStruct((T, D), BF16),
                   jax.ShapeDtypeStruct((T, D), BF16), jax.ShapeDtypeStruct((NMEM, D), F32),
                   jax.ShapeDtypeStruct((NMEM, D), F32), jax.ShapeDtypeStruct((1, D), F32),
                   jax.ShapeDtypeStruct((1, D), F32)],
        scratch_shapes=[pltpu.VMEM((TM_WIDE, D), BF16)],
        compiler_params=pltpu.CompilerParams(dimension_semantics=("arbitrary",), vmem_limit_bytes=VMEM_BIG),
    )(dx2, ym, x1, qm, km, vm, w_mo, w_mq, g_post, g_pre, *after)


def _memkv_bwd(dkm, dvm, mem, w_mk, w_mv):
    def body(dk_ref, dv_ref, m_ref, wk_ref, wv_ref, dg_ref):
        dmn = _nt(dk_ref[...].astype(BF16), _w(wk_ref)) + _nt(dv_ref[...].astype(BF16), _w(wv_ref))
        mv = m_ref[...]
        dg_ref[...] = jnp.sum(dmn * (mv * _rstd(mv)), axis=0, keepdims=True)

    return _one_call(body, "memkv_bwd", [dkm, dvm, mem, w_mk, w_mv], [((1, D), F32)], vmem=VMEM_BIG)[0]


def _postmix_bwd(dx1, z, o_f, o_c, w_out, g_post, g_fo, g_co, after=()):
    def body(dx1_ref, z_ref, of_ref, oc_ref, wo_ref, gp_ref, gfo_ref, gco_ref,
             dz_ref, dof_ref, doc_ref, dgp_ref, dgfo_ref, dgco_ref):
        _zero_at_start(dgp_ref, dgfo_ref, dgco_ref)
        dz, dgp = _rms_bwd(z_ref[...], gp_ref[...], dx1_ref[...])
        dgp_ref[...] += dgp
        dzb = dz.astype(BF16)
        dz_ref[...] = dzb
        dy = _nt(dzb, _w(wo_ref))
        dof, dgfo = _rms_bwd(of_ref[...], gfo_ref[...], dy[:, :512])
        doc, dgco = _rms_bwd(oc_ref[...], gco_ref[...], dy[:, 512:])
        dof_ref[...] = dof
        doc_ref[...] = doc
        dgfo_ref[...] += dgfo
        dgco_ref[...] += dgco

    return _tok_call(body, "postmix_bwd", [dx1, z, o_f, o_c], [w_out, g_post, g_fo, g_co],
                     [(D, BF16), (512, F32), (512, F32)], [(1, D), (1, 512), (1, 512)], tm=TM_WIDE, vmem=VMEM_BIG,
                     after=after)


def _premix_bwd(dx1, x, pieces, win_t, g_pre, after=()):
    def body(dx1_ref, x_ref, *refs):
        piece_refs, (w_ref, g_ref, dx_ref, dp_ref, dg_ref) = refs[:len(pieces)], refs[len(pieces):]
        _zero_at_start(dg_ref)
        col = 0
        for p in piece_refs:
            dp_ref[:, col:col + p.shape[1]] = p[...]
            col += p.shape[1]
        dh, dg = _rms_bwd(x_ref[...], g_ref[...], _nn(dp_ref[...], w_ref[...]))
        dg_ref[...] += dg
        dx_ref[...] = dx1_ref[...] + dh

    return _tok_call(body, "premix_bwd", [dx1, x] + list(pieces), [win_t, g_pre], [(D, F32), (PROJ, BF16)], [(1, D)],
                     tm=TM_WIDE, vmem=VMEM_BIG, after=after)


def _wgrad_group(name, pairs, rows):
    def body(*refs):
        o_ref = refs[-1]
        for k in range(len(pairs)):
            g = _tn(refs[2 * k][...].astype(BF16), refs[2 * k + 1][...].astype(BF16))
            o_ref[k * rows:(k + 1) * rows, :] = g.astype(BF16)

    in_specs, ops = [], []
    for a, b in pairs:
        in_specs += [pl.BlockSpec((a.shape[0], rows), lambda j: (0, j)), _resident(b)]
        ops += [a, b]
    return _pcall(
        body, name=name, grid=(8,), in_specs=in_specs,
        out_specs=pl.BlockSpec((None, len(pairs) * rows, D), lambda j: (j, 0, 0)),
        out_shape=jax.ShapeDtypeStruct((8, len(pairs) * rows, D), BF16),
        compiler_params=pltpu.CompilerParams(dimension_semantics=("arbitrary",), vmem_limit_bytes=VMEM_BIG),
    )(*ops)


def _wgrad_whole(name, pairs):
    n = len(pairs)
    ops = [x for pair in pairs for x in pair]
    rows = pairs[0][0].shape[1] // 8

    def body(*refs):
        hbm, o_ref, bufs, sem = refs[:2 * n], refs[2 * n], refs[2 * n + 1:4 * n + 1], refs[4 * n + 1]
        k = pl.program_id(0)
        copies = [pltpu.make_async_copy(hbm[t], bufs[t], sem.at[t]) for t in range(2 * n)]

        @pl.when(k == 0)
        def _():
            for copy in copies:
                copy.start()

        for t in range(n):
            @pl.when(k == t)
            def _(t=t):
                copies[2 * t].wait()
                copies[2 * t + 1].wait()
                g = _tn(bufs[2 * t][...].astype(BF16), bufs[2 * t + 1][...].astype(BF16))
                o_ref[...] = g.reshape(8, rows, D).astype(BF16)

    return _pcall(
        body, name=name, grid=(n,), in_specs=[ANY_SPEC] * (2 * n),
        out_specs=pl.BlockSpec((8, rows, D), lambda k: (0, k, 0)),
        out_shape=jax.ShapeDtypeStruct((8, n * rows, D), BF16),
        scratch_shapes=[pltpu.VMEM(x.shape, x.dtype) for x in ops] + [pltpu.SemaphoreType.DMA((2 * n,))],
        compiler_params=pltpu.CompilerParams(dimension_semantics=("arbitrary",), vmem_limit_bytes=VMEM_BIG),
    )(*ops)


def _adam_math(w, g, m, v):
    m2 = ADAM_B1 * m + (1.0 - ADAM_B1) * g
    v2 = ADAM_B2 * v + (1.0 - ADAM_B2) * jnp.square(g)
    m_hat = m2 / (1.0 - ADAM_B1 ** ADAM_STEP)
    v_hat = v2 / (1.0 - ADAM_B2 ** ADAM_STEP)
    delta = -ADAM_LR * (m_hat / (jnp.sqrt(v_hat) + ADAM_EPS) + ADAM_WD * w)
    return delta, m2, v2


def _adamw_small(gparts, ws, ms, vs):
    n = len(SMALL)

    def body(g_ref, *refs):
        w_refs, m_refs, v_refs = refs[:n], refs[n:2 * n], refs[2 * n:3 * n]
        outs, sum_ref = refs[3 * n:-1], refs[-1]
        g = g_ref[0]
        for k in range(1, 8):
            g = g + g_ref[k]
        sum_ref[...] = g
        outs[0][...] = sum_ref[17:18, 0:128]
        for t, name in enumerate(SMALL):
            r0, nr, c0, nc = SMALL_SLOT[name]
            gt = sum_ref[r0:r0 + nr, c0:c0 + nc]
            out = (gt,) + _adam_math(w_refs[t][...], gt, m_refs[t][...], v_refs[t][...])
            for o_ref, val in zip(outs[1 + 4 * t:5 + 4 * t], out):
                o_ref[...] = val

    whole = lambda s: pl.BlockSpec(s, lambda i, nd=len(s): (0,) * nd)
    ins = [gparts] + list(ws) + list(ms) + list(vs)
    out_shapes = [(1, 128)] + [a.shape for a in ws for _ in range(4)]
    return _pcall(
        body, name="adamw_small", grid=(1,), in_specs=[whole(a.shape) for a in ins],
        out_specs=[whole(s) for s in out_shapes], out_shape=[jax.ShapeDtypeStruct(s, F32) for s in out_shapes],
        scratch_shapes=[pltpu.VMEM((SMALL_ROWS, D), F32)],
        compiler_params=pltpu.CompilerParams(dimension_semantics=("arbitrary",)),
    )(*ins)


def _row_tile(rows):
    return next(t for t in (512, 400, 320) if rows % t == 0)


def _add_halves(g4, theirs, core, name):
    rows = g4.shape[2]
    tr = _row_tile(rows)

    def body(c_ref, a_ref, b_ref, o_ref):
        o_ref[...] = (a_ref[...].astype(F32) + b_ref[...].astype(F32)).astype(BF16)

    grid_spec = pltpu.PrefetchScalarGridSpec(
        num_scalar_prefetch=1, grid=(4, rows // tr),
        in_specs=[pl.BlockSpec((None, None, tr, D), lambda j, i, c: (j, c[0], i, 0)),
                  pl.BlockSpec((None, None, tr, D), lambda j, i, c: (j, 0, i, 0))],
        out_specs=pl.BlockSpec((None, tr, D), lambda j, i, c: (j, i, 0)))
    return _pcall(
        body, name=name, grid_spec=grid_spec, out_shape=jax.ShapeDtypeStruct((4, rows, D), BF16),
        compiler_params=pltpu.CompilerParams(dimension_semantics=("arbitrary", "arbitrary")),
    )(core, g4, theirs)


def _sum_adam(own, got, order, r0, w, m, v, name, transposed=False):
    ragged = w.ndim == 3
    n = w.shape[1] if transposed else w.shape[0]
    tr = n if ragged else min(n, 256)
    rows = own.shape[1] if ragged else tr
    at = (slice(None), 0, slice(None)) if ragged else Ellipsis

    def body(o_ref, a_ref, b_ref, c_ref, d_ref, w_ref, m_ref, v_ref, *out_refs):
        f = lambda r: r[0:tr, :].astype(F32)
        g = ((f(a_ref) + f(b_ref)) + f(c_ref)) + f(d_ref)
        g = g.T if transposed else g
        for ref, val in zip(out_refs, (g,) + _adam_math(w_ref[at], g, m_ref[at], v_ref[at])):
            ref[at] = val

    slot = lambda k: pl.BlockSpec((None, rows, D), lambda i, o: (o[k], r0 // rows + i, 0))
    if ragged:
        wspec = pl.BlockSpec((n, 1, D), lambda i, o: (0, 0, 0))
    else:
        wspec = pl.BlockSpec((D, tr), lambda i, o: (0, i)) if transposed else pl.BlockSpec((tr, D), lambda i, o: (i, 0))
    grid_spec = pltpu.PrefetchScalarGridSpec(
        num_scalar_prefetch=1, grid=(n // tr,), in_specs=[slot(0), slot(1), slot(2), slot(3), wspec, wspec, wspec],
        out_specs=[wspec] * 4)
    return _pcall(
        body, name=name, grid_spec=grid_spec, out_shape=[jax.ShapeDtypeStruct(w.shape, F32)] * 4,
        compiler_params=pltpu.CompilerParams(dimension_semantics=("arbitrary",)),
    )(order, own, got, got, got, w, m, v)


def _sum_adam_rows(own, got, order, ws, ms, vs, name):
    n, rows = len(ws), ws[0].shape[0]

    def body(o_ref, a_ref, b_ref, c_ref, d_ref, *refs):
        ins, outs = refs[:3 * n], refs[3 * n:]
        for t in range(n):
            r = slice(t * rows, (t + 1) * rows)
            f = lambda ref: ref[r, :].astype(F32)
            g = ((f(a_ref) + f(b_ref)) + f(c_ref)) + f(d_ref)
            out = (g,) + _adam_math(ins[t][...], g, ins[n + t][...], ins[2 * n + t][...])
            for o, val in zip(outs[4 * t:4 * t + 4], out):
                o[...] = val

    slot = lambda k: pl.BlockSpec((None, n * rows, D), lambda i, o: (o[k], 0, 0), pipeline_mode=pl.Buffered(1))
    wspec = pl.BlockSpec((rows, D), lambda i, o: (0, 0), pipeline_mode=pl.Buffered(1))
    grid_spec = pltpu.PrefetchScalarGridSpec(
        num_scalar_prefetch=1, grid=(1,), in_specs=[slot(0), slot(1), slot(2), slot(3)] + [wspec] * (3 * n),
        out_specs=[pl.BlockSpec((rows, D), lambda i, o: (0, 0))] * (4 * n))
    return _pcall(
        body, name=name, grid_spec=grid_spec, out_shape=[jax.ShapeDtypeStruct((rows, D), F32)] * (4 * n),
        compiler_params=pltpu.CompilerParams(dimension_semantics=("arbitrary",), vmem_limit_bytes=VMEM_BIG),
    )(order, own, got, got, got, *ws, *ms, *vs)


def _place():
    return lax.axis_index("x"), lax.axis_index("y"), lax.axis_index("c")


def _allgather(block, name, after=()):
    rows = block.shape[0]
    split = (rows // 2 + 15) // 16 * 16

    def body(x_ref, out_ref, token, send_sems, recv_sems, local_sem):
        token[...] = jnp.zeros_like(token)
        x, y, c = _place()
        me, sib = (x, y, c), (x, y, 1 - c)
        xn, yn, dg = (1 - x, y), (x, 1 - y), (1 - x, 1 - y)
        lo, hi = pl.ds(0, split), pl.ds(split, rows - split)

        def copy(k, blk, to, part=None, src=None):
            index = 4 * blk[0] + 2 * blk[1] + blk[2]
            view = out_ref.at[index] if part is None else out_ref.at[index, part]
            return pltpu.make_async_remote_copy(
                src_ref=view if src is None else src, dst_ref=view,
                send_sem=send_sems.at[k], recv_sem=recv_sems.at[k], device_id=to, device_id_type=MESH)

        def start(*copies):
            for cp in copies:
                cp.start()
            return list(copies)

        mine = pltpu.make_async_copy(x_ref, out_ref.at[4 * x + 2 * y + c], local_sem)
        mine.start()
        sent = start(copy(0, me, sib, src=x_ref), copy(1, me, (*xn, c), src=x_ref), copy(2, me, (*yn, c), src=x_ref))
        copy(1, (*xn, c), me).wait_recv()
        sent += start(copy(3, (*xn, c), sib), copy(5, (*xn, c), (*yn, c), part=lo))
        copy(2, (*yn, c), me).wait_recv()
        sent += start(copy(4, (*yn, c), sib), copy(6, (*yn, c), (*xn, c), part=hi))
        copy(5, (*dg, c), me, part=lo).wait_recv()
        copy(6, (*dg, c), me, part=hi).wait_recv()
        sent += start(copy(7, (*dg, c), sib))
        for k, blk in ((0, sib), (3, (*xn, 1 - c)), (4, (*yn, 1 - c)), (7, (*dg, 1 - c))):
            copy(k, blk, me).wait_recv()
        for cp in sent:
            cp.wait_send()
        mine.wait()

    return _pcall(
        _behind(body, 1, after), name=name,
        out_shape=[jax.ShapeDtypeStruct((8,) + block.shape, block.dtype), jax.ShapeDtypeStruct((8, 128), F32)],
        in_specs=[pl.BlockSpec(memory_space=pl.ANY)] * (1 + len(after)),
        out_specs=[pl.BlockSpec(memory_space=pl.ANY), pl.BlockSpec(memory_space=pltpu.VMEM)],
        scratch_shapes=[pltpu.SemaphoreType.DMA((8,)), pltpu.SemaphoreType.DMA((8,)), pltpu.SemaphoreType.DMA(())],
        compiler_params=pltpu.CompilerParams(has_side_effects=True),
    )(block, *after)


HBM_SPEC = pl.BlockSpec(memory_space=pltpu.HBM)
SEM_SPEC = pl.BlockSpec(memory_space=pltpu.SEMAPHORE)
ANY_SPEC = pl.BlockSpec(memory_space=pl.ANY)
EFFECT = pltpu.SideEffectType.DATAFLOW_SIDE_EFFECTING


def _in_hbm(a):
    return pltpu.with_memory_space_constraint(a, pltpu.HBM)


def _start_copies(name, src, land_shape, plan, n):
    def body(src_ref, land_ref, send_sems, recv_sems, src_thru, land_thru, token):
        for k, (s, d, to, _) in enumerate(plan(src_ref, land_ref)):
            pltpu.make_async_remote_copy(src_ref=s, dst_ref=d, send_sem=send_sems.at[k], recv_sem=recv_sems.at[k],
                                         device_id=to, device_id_type=MESH).start()
        token[...] = jnp.zeros_like(token)

    return _pcall(
        body, name=name,
        out_shape=(pltpu.SemaphoreType.DMA((n,)), pltpu.SemaphoreType.DMA((n,)), pltpu.HBM(src.shape, src.dtype),
                   pltpu.HBM(land_shape, src.dtype), jax.ShapeDtypeStruct((8, 128), F32)),
        in_specs=(HBM_SPEC, HBM_SPEC),
        out_specs=(SEM_SPEC, SEM_SPEC, HBM_SPEC, HBM_SPEC, pl.BlockSpec(memory_space=pltpu.VMEM)),
        input_output_aliases={0: 2, 1: 3}, compiler_params=pltpu.CompilerParams(has_side_effects=EFFECT),
    )(_in_hbm(src), _in_hbm(lax.empty(land_shape, src.dtype)))


def _wait_copies(name, started, after, plan):
    send_sems, recv_sems, src_thru, land_thru, _ = started

    def body(src_ref, land_ref, send_sems, recv_sems, *rest):
        for k, (s, _, to, mine) in enumerate(plan(src_ref, land_ref)):
            cp = pltpu.make_async_remote_copy(src_ref=s, dst_ref=mine, send_sem=send_sems.at[k],
                                              recv_sem=recv_sems.at[k], device_id=to, device_id_type=MESH)
            cp.wait_send()
            cp.wait_recv()

    return _pcall(
        body, name=name,
        out_shape=(pltpu.HBM(src_thru.shape, src_thru.dtype), pltpu.HBM(land_thru.shape, land_thru.dtype)),
        in_specs=(HBM_SPEC, HBM_SPEC, SEM_SPEC, SEM_SPEC) + (ANY_SPEC,) * len(after), out_specs=(HBM_SPEC, HBM_SPEC),
        input_output_aliases={0: 0, 1: 1}, compiler_params=pltpu.CompilerParams(has_side_effects=EFFECT),
    )(src_thru, land_thru, send_sems, recv_sems, *after)


def _start_inplace(name, buf, plan, n):
    def body(buf_ref, send_sems, recv_sems, buf_thru, token):
        for k, (s, d, to, _) in enumerate(plan(buf_ref, buf_ref)):
            pltpu.make_async_remote_copy(src_ref=s, dst_ref=d, send_sem=send_sems.at[k], recv_sem=recv_sems.at[k],
                                         device_id=to, device_id_type=MESH).start()
        token[...] = jnp.zeros_like(token)

    return _pcall(
        body, name=name,
        out_shape=(pltpu.SemaphoreType.DMA((n,)), pltpu.SemaphoreType.DMA((n,)), pltpu.HBM(buf.shape, buf.dtype),
                   jax.ShapeDtypeStruct((8, 128), F32)),
        in_specs=(HBM_SPEC,), out_specs=(SEM_SPEC, SEM_SPEC, HBM_SPEC, pl.BlockSpec(memory_space=pltpu.VMEM)),
        input_output_aliases={0: 2}, compiler_params=pltpu.CompilerParams(has_side_effects=EFFECT),
    )(_in_hbm(buf))


def _wait_inplace(name, started, after, plan):
    send_sems, recv_sems, buf_thru, _ = started

    def body(buf_ref, send_sems, recv_sems, *rest):
        for k, (s, _, to, mine) in enumerate(plan(buf_ref, buf_ref)):
            cp = pltpu.make_async_remote_copy(src_ref=s, dst_ref=mine, send_sem=send_sems.at[k],
                                              recv_sem=recv_sems.at[k], device_id=to, device_id_type=MESH)
            cp.wait_send()
            cp.wait_recv()

    return _pcall(
        body, name=name, out_shape=pltpu.HBM(buf_thru.shape, buf_thru.dtype),
        in_specs=(HBM_SPEC, SEM_SPEC, SEM_SPEC) + (ANY_SPEC,) * len(after), out_specs=HBM_SPEC,
        input_output_aliases={0: 0}, compiler_params=pltpu.CompilerParams(has_side_effects=EFFECT),
    )(buf_thru, send_sems, recv_sems, *after)


def _gather_plan(src_ref, land_ref):
    x, y, c = _place()
    peers = [(x, y, 1 - c), (1 - x, y, c), (x, 1 - y, c)]
    return [(src_ref, land_ref.at[4 * x + 2 * y + c], p, land_ref.at[4 * p[0] + 2 * p[1] + p[2]]) for p in peers]


def _relay_plan(buf_ref, _):
    x, y, c = _place()
    slot = lambda p, pc: 4 * p[0] + 2 * p[1] + pc
    xn, yn, dg, sib = (1 - x, y), (x, 1 - y), (1 - x, 1 - y), (x, y, 1 - c)
    half = buf_ref.shape[1] // 2
    lo, hi = pl.ds(0, half), pl.ds(half, half)
    return [(buf_ref.at[slot(xn, c)], buf_ref.at[slot(xn, c)], sib, buf_ref.at[slot(xn, 1 - c)]),
            (buf_ref.at[slot(yn, c)], buf_ref.at[slot(yn, c)], sib, buf_ref.at[slot(yn, 1 - c)]),
            (buf_ref.at[slot(xn, c), lo], buf_ref.at[slot(xn, c), lo], (*yn, c), buf_ref.at[slot(dg, c), lo]),
            (buf_ref.at[slot(yn, c), hi], buf_ref.at[slot(yn, c), hi], (*xn, c), buf_ref.at[slot(dg, c), hi])]


def _swap_plan(src_ref, land_ref):
    x, y, c = _place()
    return [(src_ref.at[:, pl.ds(1 - c, 1)], land_ref, (x, y, 1 - c), land_ref)]


def _exchange_plan(src_ref, land_ref):
    x, y, c = _place()
    chips = [(1 - x, y), (x, 1 - y), (1 - x, 1 - y)]
    return [(src_ref.at[2 * px + py], land_ref.at[2 * x + y], (px, py, c), land_ref.at[2 * px + py]) for px, py in chips]


def _gather_forward(land, block):
    def body(land_ref, out_ref, send_sems, recv_sems):
        x, y, c = _place()
        chips = [(1 - x, 1 - y)]

        def copy(k, px, py, pc):
            blk = out_ref.at[4 * px + 2 * py + pc]
            return pltpu.make_async_remote_copy(src_ref=blk, dst_ref=blk, send_sem=send_sems.at[k],
                                                recv_sem=recv_sems.at[k], device_id=(x, y, 1 - c), device_id_type=MESH)

        sent = [copy(k, px, py, c) for k, (px, py) in enumerate(chips)]
        for cp in sent:
            cp.start()
        for k, (px, py) in enumerate(chips):
            copy(k, px, py, 1 - c).wait_recv()
        for cp in sent:
            cp.wait_send()

    land = _pcall(
        body, name="allgather_rest_forward", out_shape=jax.ShapeDtypeStruct(land.shape, land.dtype),
        in_specs=[ANY_SPEC], out_specs=ANY_SPEC, input_output_aliases={0: 0},
        scratch_shapes=[pltpu.SemaphoreType.DMA((1,)), pltpu.SemaphoreType.DMA((1,))],
        compiler_params=pltpu.CompilerParams(has_side_effects=True),
    )(land)

    rows = block.shape[0]
    tr = rows // 4

    def place(me_ref, x_ref, land_ref, out_ref):
        out_ref[...] = x_ref[...]

    x, y, c = _place()
    grid_spec = pltpu.PrefetchScalarGridSpec(
        num_scalar_prefetch=1, grid=(rows // tr,),
        in_specs=[pl.BlockSpec((tr, D), lambda i, me: (i, 0)), ANY_SPEC],
        out_specs=pl.BlockSpec((None, tr, D), lambda i, me: (me[0], i, 0)))
    return _pcall(
        place, name="allgather_rest_own", grid_spec=grid_spec, out_shape=jax.ShapeDtypeStruct(land.shape, land.dtype),
        input_output_aliases={2: 0}, compiler_params=pltpu.CompilerParams(dimension_semantics=("arbitrary",)),
    )((4 * x + 2 * y + c).reshape(1), block, land)


class _ReduceScatter:
    def __init__(self, name, g):
        self.name = name
        rows = g.shape[1]
        self.started = _start_copies(name + "_swap_start", g.reshape(4, 2, rows, D), (4, 1, rows, D), _swap_plan, 1)
        self.token = self.started[4]

    def halfway(self, after):
        g4, theirs = _wait_copies(self.name + "_swap_wait", self.started, after, _swap_plan)
        self.own = _add_halves(g4, theirs, lax.axis_index("c").reshape(1), self.name + "_add_halves")
        self.started = _start_copies(self.name + "_exch_start", self.own, self.own.shape, _exchange_plan, 3)
        self.token = self.started[4]

    def finish(self, after):
        own, got = _wait_copies(self.name + "_exch_wait", self.started, after, _exchange_plan)
        chip = 2 * lax.axis_index("x") + lax.axis_index("y")
        return own, got, (chip + jnp.arange(4, dtype=jnp.int32)) % 4


def _pack_small(p, loss):
    def body(*refs):
        o_ref = refs[-1]
        o_ref[...] = jnp.zeros_like(o_ref)
        for ref, name in zip(refs, SMALL):
            r0, nr, c0, nc = SMALL_SLOT[name]
            o_ref[r0:r0 + nr, c0:c0 + nc] = ref[...]
        o_ref[17:18, 0:128] = refs[len(SMALL)][0:1, :]

    return _one_call(body, "pack_small_grads", [p[n] for n in SMALL] + [loss], [((SMALL_ROWS, D), F32)])[0]


_GAP_DEV, _GAP_ROW = divmod(GATE0 + 8, N_IN)
_GAP = CHK0 - GATE0 - 8


def _in_rows_to_proj(g):
    runs = [(j, 0, N_IN, N_IN * j) for j in range(_GAP_DEV)]
    runs += [(_GAP_DEV, 0, _GAP_ROW, N_IN * _GAP_DEV), (_GAP_DEV, _GAP_ROW, N_IN, N_IN * _GAP_DEV + _GAP_ROW + _GAP)]
    runs += [(j, 0, N_IN, N_IN * j + _GAP) for j in range(_GAP_DEV + 1, 8)]

    def body(g_ref, o_ref, acc_ref):
        acc_ref[...] = jnp.zeros_like(acc_ref)
        for j, r0, r1, dest in runs:
            start, shift = dest // 16 * 16, dest % 16
            win = -(-(shift + r1 - r0) // 16) * 16
            r = lax.broadcasted_iota(jnp.int32, (win, R_IN), 0)
            c = lax.broadcasted_iota(jnp.int32, (win, R_IN), 1)
            move = jnp.where((c >= r0) & (c < r1) & (r == c - r0 + shift), 1.0, 0.0).astype(BF16)
            acc_ref[start:start + win, :] += _nn(move, g_ref[j])
        o_ref[...] = acc_ref[...].astype(BF16)

    return _pcall(
        body, name="w_in_layout", out_shape=jax.ShapeDtypeStruct((PROJ, D), BF16), grid=(1,),
        in_specs=[pl.BlockSpec(g.shape, lambda i: (0, 0, 0))], out_specs=pl.BlockSpec((PROJ, D), lambda i: (0, 0)),
        scratch_shapes=[pltpu.VMEM((PROJ, D), F32)],
        compiler_params=pltpu.CompilerParams(dimension_semantics=("arbitrary",), vmem_limit_bytes=VMEM_BIG),
    )(g)


def _wgrad_in(pieces, h1):
    n = len(pieces)
    ends = [sum(p.shape[1] for p in pieces[:k + 1]) for k in range(n)]
    assert ends[-1] == PROJ

    def body(*refs):
        piece_refs, (b_ref, o_ref, g_ref), bufs, sem = refs[:n], refs[n:n + 3], refs[n + 3:2 * n + 3], refs[2 * n + 3]
        i = pl.program_id(0)
        copies = [pltpu.make_async_copy(piece_refs[k], bufs[k], sem.at[k]) for k in range(n)]

        @pl.when(i == 0)
        def _():
            for copy in copies:
                copy.start()
            g_ref[PROJ:, :] = jnp.zeros((R_IN - N_IN + 1, D), F32)

        for k in range(n):
            @pl.when(i == k)
            def _(k=k):
                copies[k].wait()
                g_ref[ends[k] - pieces[k].shape[1]:ends[k], :] = _tn(bufs[k][...], b_ref[...])

        row = lax.broadcasted_iota(jnp.int32, (R_IN, D), 0)
        for j in range(8):
            lo = N_IN * j + (_GAP if j > _GAP_DEV else 0)
            hi = N_IN * j + (_GAP if j >= _GAP_DEV else 0)
            ready = min(k for k in range(n) if ends[k] >= min(hi + R_IN, PROJ))

            @pl.when(i == ready)
            def _(j=j, lo=lo, hi=hi):
                v = g_ref[hi:hi + R_IN, :]
                if lo != hi:
                    v = jnp.where(row < _GAP_ROW, g_ref[lo:lo + R_IN, :], v)
                o_ref[j] = jnp.where(row < N_IN, v, 0.0).astype(BF16)

    return _pcall(
        body, name="wgrad_in", grid=(n,),
        in_specs=[ANY_SPEC] * n + [_resident(h1)],
        out_specs=pl.BlockSpec((8, R_IN, D), lambda i: (0, 0, 0)),
        out_shape=jax.ShapeDtypeStruct((8, R_IN, D), BF16),
        scratch_shapes=[pltpu.VMEM((PROJ + R_IN - N_IN + 1, D), F32)] + [pltpu.VMEM(p.shape, BF16) for p in pieces]
        + [pltpu.SemaphoreType.DMA((n,))],
        compiler_params=pltpu.CompilerParams(dimension_semantics=("arbitrary",), vmem_limit_bytes=VMEM_BIG),
    )(*pieces, h1)


def _local_grads(x, mem, tgt, win_t, gw_of, sm, on_grads, after=()):
    b_pad = jnp.pad(sm['b_fgt'], ((0, 0), (0, 120)))
    tbl = jnp.pad(sm['rel_bias'], ((0, 0), (0, NREL_PAD - 257)))

    h1, proj, flog = _premix_fwd(x, sm['g_mix_pre'], win_t, after)
    c = _gate_fwd(flog, b_pad)
    ct3 = c[:, :8].T.reshape(4, 2, T)
    o_f, lse_f = _fox_fwd(proj, c, ct3)
    vt3 = _relvec_fwd(tbl).reshape(4, 2, VW)
    kvp = jnp.pad(proj[:, CHK0 + 512:], ((LEFT, 0), (0, 0)))
    o_c, lse_c = _chk_fwd(proj, kvp, vt3, [gw_of('relay', [o_f])])
    gw = gw_of('done', [o_c])
    w_out, w_mq, w_mk, w_mv, w_mo, w1_t, w2 = (_wblk(gw, n) for n in ('w_out', 'w_mq', 'w_mk', 'w_mv', 'w_mo', 'w_ff1', 'w_ff2'))
    ycat, z, x1, h2, qm = _postmix_fwd(x, o_f, o_c, sm['g_fox_out'], sm['g_chk_out'], w_out,
                                       sm['g_mix_post'], sm['g_mem_pre'], w_mq)
    memn, km, vm = _memkv_fwd(mem, sm['g_mem_kv'], w_mk, w_mv)
    om, ym, x2, h3 = _mem_fwd(qm, x1, km, vm, w_mo, sm['g_mem_post'], sm['g_ff_pre'])

    gs = {}
    dx2, da, dy3, r, loss_acc, gs['g_ff_post'], gs['g_ff_pre'] = _ffn_step(h3, x2, tgt, w1_t, w2, sm['g_ff_post'],
                                                                         sm['g_ff_pre'])
    tok = on_grads('A', _wgrad_group("wgrad_ff", [(da, h3), (r, dy3)], 512), None)
    dx1, dym, dqm, dkm, dvm, gs['g_mem_post'], gs['g_mem_pre'] = _mem_bwd(
        dx2, ym, x1, qm, km, vm, w_mo, w_mq, sm['g_mem_post'], sm['g_mem_pre'], [tok])
    tok = on_grads('A halfway', None, [dx1])
    gs['g_mem_kv'] = _memkv_bwd(dkm, dvm, mem, w_mk, w_mv)
    dz, dof, doc, gs['g_mix_post'], gs['g_fox_out'], gs['g_chk_out'] = _postmix_bwd(
        dx1, z, o_f, o_c, w_out, sm['g_mix_post'], sm['g_fox_out'], sm['g_chk_out'], [tok])
    tok = on_grads('B', _wgrad_whole("wgrad_mem_out", [(ycat, dz), (h2, dqm), (memn, dkm), (memn, dvm), (om, dym)]), None)
    dq_f, dk_f, dv_f, dct, dcq = _fox_bwd(proj, c, ct3, o_f, lse_f, dof, [tok])
    tok = on_grads('B halfway', None, [dq_f])
    dq_c, dk_c, dv_c, gv = _chk_bwd(proj, kvp, vt3, o_c, lse_c, doc, [tok])
    gs['rel_bias'] = _relvec_bwd(gv.reshape(8, VW))[:, :257]
    dc = jnp.pad(dct.reshape(8, T).T + dcq[:, :, :2].transpose(1, 0, 2).reshape(T, 8), ((0, 0), (0, 120)))
    dflog, db = _gate_bwd(dc, flog, b_pad)
    gs['b_fgt'] = db[0:1, :8]
    pieces = [dq_f, dk_f, dv_f, dflog, dq_c, dk_c, dv_c]
    on_grads('C', _wgrad_in(pieces, h1), None)
    tok = on_grads('C halfway', None, [gs['g_mem_kv']])
    grad_x, _, gs['g_mix_pre'] = _premix_bwd(dx1, x, pieces, win_t, sm['g_mix_pre'], [tok])
    return loss_acc, grad_x, gs


def kernel(x, mem, w_in, b_fgt, rel_bias, g_fox_out, g_chk_out, w_out, g_mix_pre, g_mix_post, g_mem_kv, w_mq, w_mk, w_mv, w_mo, g_mem_pre, g_mem_post, w_ff1, w_ff2, g_ff_pre, g_ff_post, loss_target, m_w_in, m_b_fgt, m_rel_bias, m_g_fox_out, m_g_chk_out, m_w_out, m_g_mix_pre, m_g_mix_post, m_g_mem_kv, m_w_mq, m_w_mk, m_w_mv, m_w_mo, m_g_mem_pre, m_g_mem_post, m_w_ff1, m_w_ff2, m_g_ff_pre, m_g_ff_post, v_w_in, v_b_fgt, v_rel_bias, v_g_fox_out, v_g_chk_out, v_w_out, v_g_mix_pre, v_g_mix_post, v_g_mem_kv, v_w_mq, v_w_mk, v_w_mv, v_w_mo, v_g_mem_pre, v_g_mem_post, v_w_ff1, v_w_ff2, v_g_ff_pre, v_g_ff_post):
    args = dict(locals())
    two_d = lambda a: a.reshape(a.shape[-2:])
    w = {n: two_d(args[n]) for n in WEIGHTS}
    m = {n: two_d(args['m_' + n]) for n in WEIGHTS}
    v = {n: two_d(args['v_' + n]) for n in WEIGHTS}

    sm = {n: w[n] for n in SMALL}
    shard_in = jnp.pad(w['w_in'].T, ((0, R_IN - N_IN), (0, 0))).astype(BF16)
    gathered_in, zero = _allgather(shard_in, "allgather_w_in")
    win_t = _in_rows_to_proj(gathered_in)
    shard_rest = (jnp.concatenate([w['w_ff1'].T, w['w_ff2'], w['w_out'], w['w_mq'], w['w_mk'], w['w_mv'], w['w_mo']],
                                  axis=0) + zero[0, 0]).astype(BF16)
    gather = {'first': _start_copies("allgather_rest_start", shard_rest, (8, R_REST, D), _gather_plan, 3)}

    def gw_of(stage, after):
        if stage == 'relay':
            gather['block'], land = _wait_copies("allgather_rest_wait", gather['first'], after, _gather_plan)
            gather['second'] = _start_inplace("allgather_rest_relay_start", land, _relay_plan, 4)
            return gather['second'][3]
        land = _wait_inplace("allgather_rest_relay_wait", gather['second'], after, _relay_plan)
        return _gather_forward(land, gather['block'])

    rs = {}

    def on_grads(stage, g, after):
        if stage.endswith('halfway'):
            rs[stage[0]].halfway(after)
            return rs[stage[0]].token
        rs[stage] = _ReduceScatter("rs_" + stage.lower(), g)
        return rs[stage].token

    loss_local, grad_x, gs = _local_grads(x[0], mem[0], loss_target[0], win_t, gw_of, sm, on_grads, [gather['first'][4]])
    grads, deltas, new_m, new_v = {}, {}, {}, {}

    def update(n, out):
        grads[n], deltas[n], new_m[n], new_v[n] = out

    own, got, order = rs['A'].finish([grad_x, rs['C'].token])
    update('w_ff1', _sum_adam(own, got, order, 0, w['w_ff1'], m['w_ff1'], v['w_ff1'], "adamw_w_ff1", transposed=True))
    update('w_ff2', _sum_adam(own, got, order, 512, w['w_ff2'], m['w_ff2'], v['w_ff2'], "adamw_w_ff2"))
    own, got, order = rs['B'].finish([grad_x, rs['C'].token])
    names_b = ('w_out', 'w_mq', 'w_mk', 'w_mv', 'w_mo')
    done = _sum_adam_rows(own, got, order, [w[n] for n in names_b], [m[n] for n in names_b], [v[n] for n in names_b],
                          "adamw_group_b")
    for k, n in enumerate(names_b):
        update(n, done[4 * k:4 * k + 4])

    own, got, order = rs['C'].finish([new_v[n] for n in BIG if n != 'w_in'])
    rows_of = lambda a: jnp.transpose(a, (2, 0, 1))
    done = _sum_adam(own, got, order, 0, rows_of(w_in), rows_of(m_w_in), rows_of(v_w_in), "adamw_w_in")
    update('w_in', [jnp.transpose(a, (1, 2, 0)) for a in done])

    gparts, _ = _allgather(_pack_small(gs, loss_local), "allgather_small_grads", [got])
    small = _adamw_small(gparts, [w[n] for n in SMALL], [m[n] for n in SMALL], [v[n] for n in SMALL])
    loss = small[0][0, 0]
    for t, n in enumerate(SMALL):
        update(n, small[1 + 4 * t:5 + 4 * t])

    out = [loss, grad_x[None]]
    for group in (grads, deltas, new_m, new_v):
        out += [group[n].reshape(args[n].shape) for n in WEIGHTS]
    return tuple(out)
```

```python
import jax
import jax.numpy as jnp
from jax import lax
from jax.experimental import pallas as pl
from jax.experimental.pallas import tpu as pltpu

F32 = jnp.float32
BF16 = jnp.bfloat16
MESH = pl.DeviceIdType.MESH

T = 2048
D = 1024
NMEM = 256
DFF = 4096
EPS = 1e-6
TM = 256
TM_WIDE = 512
TQ = 256
FQ = 512
HD = 64
SCALE = HD ** -0.5
MEM_HEADS = 4
MEM_HD = 256
MEM_SCALE = MEM_HD ** -0.5
NEG = -1e30
LEFT = 512
WIN = LEFT + TQ
VW = 1024
NREL_PAD = 384
PROJ = 3200
GATE0 = 1536
CHK0 = 1664
VMEM_BIG = 56 * 1024 * 1024

ADAM_LR = 0.001
ADAM_B1 = 0.9
ADAM_B2 = 0.999
ADAM_EPS = 1e-08
ADAM_WD = 0.01
ADAM_STEP = 10

N_IN = 385
R_IN = 400
R_REST = 1664
W_ROWS = {'w_ff1': (0, 512), 'w_ff2': (512, 512),
          'w_out': (1024, 128), 'w_mq': (1152, 128), 'w_mk': (1280, 128), 'w_mv': (1408, 128), 'w_mo': (1536, 128)}
SMALL_ROWS = 24
SMALL_SLOT = {'rel_bias': (0, 8, 0, 257), 'b_fgt': (8, 1, 0, 8), 'g_fox_out': (9, 1, 0, 512), 'g_chk_out': (9, 1, 512, 512),
              'g_mix_pre': (10, 1, 0, 1024), 'g_mix_post': (11, 1, 0, 1024), 'g_mem_kv': (12, 1, 0, 1024),
              'g_mem_pre': (13, 1, 0, 1024), 'g_mem_post': (14, 1, 0, 1024), 'g_ff_pre': (15, 1, 0, 1024),
              'g_ff_post': (16, 1, 0, 1024)}

WEIGHTS = ['w_in', 'b_fgt', 'rel_bias', 'g_fox_out', 'g_chk_out', 'w_out', 'g_mix_pre', 'g_mix_post', 'g_mem_kv',
           'w_mq', 'w_mk', 'w_mv', 'w_mo', 'g_mem_pre', 'g_mem_post', 'w_ff1', 'w_ff2', 'g_ff_pre', 'g_ff_post']
BIG = ['w_in', 'w_out', 'w_mq', 'w_mk', 'w_mv', 'w_mo', 'w_ff1', 'w_ff2']
SMALL = [n for n in WEIGHTS if n not in BIG]


def _pcall(body, **kw):
    return pl.pallas_call(body, **kw)


def _nn(a, b):
    return jnp.dot(a, b, preferred_element_type=F32)


def _nt(a, b):
    return lax.dot_general(a, b, (((1,), (1,)), ((), ())), preferred_element_type=F32)


def _tn(a, b):
    return lax.dot_general(a, b, (((0,), (0,)), ((), ())), preferred_element_type=F32)


def _w(ref):
    v = ref[...]
    return v if v.ndim == 2 else v.reshape(-1, v.shape[-1])


def _rstd(x):
    return lax.rsqrt(jnp.mean(x * x, axis=-1, keepdims=True) + EPS)


def _rms(x, g):
    return x * _rstd(x) * g


def _rms_bwd(x, g, dy):
    r = _rstd(x)
    xh = x * r
    dg = jnp.sum(dy * xh, axis=0, keepdims=True)
    dxh = dy * g
    dx = r * (dxh - xh * jnp.mean(dxh * xh, axis=-1, keepdims=True))
    return dx, dg


def _resident(a):
    if isinstance(a, tuple):
        _, shape, index = a
        return pl.BlockSpec(shape, lambda *_: index, pipeline_mode=pl.Buffered(1))
    return pl.BlockSpec(a.shape, lambda *_, nd=a.ndim: (0,) * nd, pipeline_mode=pl.Buffered(1))


def _wblk(gw, name):
    r0, rows = W_ROWS[name]
    return (gw, (8, rows, D), (0, r0 // rows, 0))


def _behind(body, n_in, after):
    if not after:
        return body
    return lambda *refs: body(*refs[:n_in], *refs[n_in + len(after):])


def _tok_call(body, name, tiled, full, outs_tiled, outs_acc=(), rows=T, tm=TM, vmem=None, after=()):
    in_specs = [pl.BlockSpec((tm, a.shape[1]), lambda i: (i, 0)) for a in tiled]
    in_specs += [_resident(a) for a in full] + [ANY_SPEC] * len(after)
    full = [a[0] if isinstance(a, tuple) else a for a in full] + list(after)
    body = _behind(body, len(tiled) + len(full) - len(after), after)
    out_shape = [jax.ShapeDtypeStruct((rows, c), dt) for c, dt in outs_tiled]
    out_shape += [jax.ShapeDtypeStruct(s, F32) for s in outs_acc]
    out_specs = [pl.BlockSpec((tm, c), lambda i: (i, 0)) for c, _ in outs_tiled]
    out_specs += [pl.BlockSpec(s, lambda i, nd=len(s): (0,) * nd) for s in outs_acc]
    return _pcall(
        body, name=name, grid=(rows // tm,), in_specs=in_specs, out_specs=out_specs, out_shape=out_shape,
        compiler_params=pltpu.CompilerParams(dimension_semantics=("arbitrary",), vmem_limit_bytes=vmem),
    )(*tiled, *full)


def _one_call(body, name, ins, outs, vmem=None):
    whole = lambda s: pl.BlockSpec(s, lambda i, nd=len(s): (0,) * nd)
    return _pcall(
        body, name=name, grid=(1,), in_specs=[_resident(a) for a in ins], out_specs=[whole(s) for s, _ in outs],
        out_shape=[jax.ShapeDtypeStruct(s, dt) for s, dt in outs],
        compiler_params=pltpu.CompilerParams(dimension_semantics=("arbitrary",), vmem_limit_bytes=vmem),
    )(*[a[0] if isinstance(a, tuple) else a for a in ins])


def _premix_fwd(x, g_pre, win_t, after=()):
    def body(x_ref, g_ref, w_ref, h_ref, proj_ref, flog_ref, kvp_ref):
        s = pl.program_id(0)

        @pl.when(s == 0)
        def _():
            kvp_ref[...] = jnp.zeros_like(kvp_ref)

        @pl.when(s > 0)
        def _():
            h = _rms(x_ref[...], g_ref[...]).astype(BF16)
            h_ref[...] = h
            p = _nt(h, w_ref[...])
            proj_ref[...] = p.astype(BF16)
            flog_ref[...] = p[:, GATE0:GATE0 + 128]
            kvp_ref[...] = p[:, CHK0 + 512:].astype(BF16)

    tile = lambda c: pl.BlockSpec((LEFT, c), lambda s: (jnp.maximum(s - 1, 0), 0))
    return _pcall(
        _behind(body, 3, after), name="premix_fwd", grid=(T // LEFT + 1,),
        in_specs=[tile(D), _resident(g_pre), _resident(win_t)] + [ANY_SPEC] * len(after),
        out_specs=[tile(D), tile(PROJ), tile(128), pl.BlockSpec((LEFT, 1024), lambda s: (s, 0))],
        out_shape=[jax.ShapeDtypeStruct((T, D), BF16), jax.ShapeDtypeStruct((T, PROJ), BF16),
                   jax.ShapeDtypeStruct((T, 128), F32), jax.ShapeDtypeStruct((T + LEFT, 1024), BF16)],
        compiler_params=pltpu.CompilerParams(dimension_semantics=("arbitrary",), vmem_limit_bytes=VMEM_BIG),
    )(x, g_pre, win_t, *after)


def _postmix_fwd(x, o_f, o_c, g_fo, g_co, w_out, g_post, g_mpre, w_mq):
    def body(x_ref, of_ref, oc_ref, gfo_ref, gco_ref, wo_ref, gp_ref, gm_ref, wq_ref,
             y_ref, z_ref, x1_ref, h2_ref, qm_ref):
        y_ref[:, :512] = _rms(of_ref[...], gfo_ref[...]).astype(BF16)
        y_ref[:, 512:] = _rms(oc_ref[...], gco_ref[...]).astype(BF16)
        z = _nn(y_ref[...], _w(wo_ref))
        z_ref[...] = z
        x1 = x_ref[...] + _rms(z, gp_ref[...])
        x1_ref[...] = x1
        h2 = _rms(x1, gm_ref[...]).astype(BF16)
        h2_ref[...] = h2
        qm_ref[...] = _nn(h2, _w(wq_ref)).astype(BF16)

    return _tok_call(body, "postmix_fwd", [x, o_f, o_c], [g_fo, g_co, w_out, g_post, g_mpre, w_mq],
                     [(D, BF16), (D, F32), (D, F32), (D, BF16), (D, BF16)], tm=TM_WIDE, vmem=VMEM_BIG)


def _memkv_fwd(mem, g_kv, w_mk, w_mv):
    def body(m_ref, g_ref, wk_ref, wv_ref, mn_ref, k_ref, v_ref):
        mn = _rms(m_ref[...], g_ref[...]).astype(BF16)
        mn_ref[...] = mn
        k_ref[...] = _nn(mn, _w(wk_ref)).astype(BF16)
        v_ref[...] = _nn(mn, _w(wv_ref)).astype(BF16)

    return _tok_call(body, "memkv_fwd", [mem], [g_kv, w_mk, w_mv],
                     [(D, BF16), (D, BF16), (D, BF16)], rows=NMEM, tm=NMEM, vmem=VMEM_BIG)


def _mem_fwd(qm, x1, km, vm, w_mo, g_post, g_fpre):
    def body(q_ref, x1_ref, k_ref, v_ref, wo_ref, gp_ref, gf_ref, om_ref, ym_ref, x2_ref, h3_ref):
        for h in range(MEM_HEADS):
            sl = slice(h * MEM_HD, (h + 1) * MEM_HD)
            s = _nt(q_ref[:, sl], k_ref[:, sl]) * MEM_SCALE
            p = jnp.exp(s - jnp.max(s, axis=-1, keepdims=True))
            p = p / jnp.sum(p, axis=-1, keepdims=True)
            om_ref[:, sl] = _nn(p.astype(BF16), v_ref[:, sl]).astype(BF16)
        ym = _nn(om_ref[...], _w(wo_ref))
        ym_ref[...] = ym
        x2 = x1_ref[...] + _rms(ym, gp_ref[...])
        x2_ref[...] = x2
        h3_ref[...] = _rms(x2, gf_ref[...]).astype(BF16)

    return _tok_call(body, "mem_fwd", [qm, x1], [km, vm, w_mo, g_post, g_fpre],
                     [(D, BF16), (D, F32), (D, F32), (D, BF16)], tm=TM_WIDE, vmem=VMEM_BIG)


def _tri(lower):
    r = lax.broadcasted_iota(jnp.int32, (128, 128), 0)
    c = lax.broadcasted_iota(jnp.int32, (128, 128), 1)
    return jnp.where(r >= c if lower else c >= r, 1.0, 0.0).astype(F32)


def _hdot(a, b):
    return jnp.dot(a, b, preferred_element_type=F32, precision=lax.Precision.HIGHEST)


def _gate_fwd(flog, b_pad):
    def body(f_ref, b_ref, c_ref):
        tri = _tri(True)

        def step(i, carry):
            rows = pl.ds(pl.multiple_of(i * 128, 128), 128)
            z = f_ref[rows, :] + b_ref[...]
            lf = jnp.minimum(z, 0.0) - jnp.log(1.0 + jnp.exp(-jnp.abs(z)))
            cb = _hdot(tri, lf) + carry
            c_ref[rows, :] = cb
            return cb[127:128, :]

        lax.fori_loop(0, T // 128, step, jnp.zeros((1, 128), F32))

    return _one_call(body, "gate_fwd", [flog, b_pad], [((T, 128), F32)])[0]


def _gate_bwd(dc, flog, b_pad):
    def body(dc_ref, f_ref, b_ref, df_ref, db_ref):
        tri = _tri(False)

        def step(j, carry):
            run, db = carry
            i = T // 128 - 1 - j
            rows = pl.ds(pl.multiple_of(i * 128, 128), 128)
            dcb = dc_ref[rows, :]
            rb = _hdot(tri, dcb) + run
            z = f_ref[rows, :] + b_ref[...]
            df = rb * (1.0 / (1.0 + jnp.exp(z)))
            df_ref[rows, :] = df.astype(BF16)
            return run + jnp.sum(dcb, axis=0, keepdims=True), db + jnp.sum(df, axis=0, keepdims=True)

        _, db = lax.fori_loop(0, T // 128, step, (jnp.zeros((1, 128), F32), jnp.zeros((1, 128), F32)))
        db_ref[...] = jnp.broadcast_to(db, (8, 128))

    return _one_call(body, "gate_bwd", [dc, flog, b_pad], [((T, 128), BF16), ((8, 128), F32)])


def _lane_lo(rows=TQ):
    return lax.broadcasted_iota(jnp.int32, (rows, 128), 1) < HD


def _half(v, lo, a, scale=None):
    keep = lo if a == 0 else jnp.logical_not(lo)
    v = v.astype(F32) if scale is None else v.astype(F32) * scale
    return jnp.where(keep, v, 0.0).astype(BF16)


def _fox_specs():
    return [pl.BlockSpec((FQ, 128), lambda h, i: (i, h)),
            pl.BlockSpec((T, 128), lambda h, i: (0, 4 + h)),
            pl.BlockSpec((T, 128), lambda h, i: (0, 8 + h))]


def _lane_pick(x, at):
    lane = lax.broadcasted_iota(jnp.int32, x.shape, 1)
    return jnp.sum(jnp.where(lane == at, x, 0.0), axis=-1, keepdims=True)


def _fox_fwd(proj, c, ct3):
    def body(q_ref, k_ref, v_ref, c_ref, ct_ref, o_ref, l_ref):
        i = pl.program_id(1)
        lo = _lane_lo(FQ)
        causal = lax.broadcasted_iota(jnp.int32, (FQ, FQ), 1) <= lax.broadcasted_iota(jnp.int32, (FQ, FQ), 0)
        q = q_ref[...]
        qs = [_half(q, lo, a, SCALE) for a in range(2)]
        cqs = [_lane_pick(c_ref[...], 2 * pl.program_id(0) + a) for a in range(2)]

        def tile(off, carry, diagonal):
            kblk = k_ref[pl.ds(off, FQ), :]
            vblk = v_ref[pl.ds(off, FQ), :]
            new = []
            for a in range(2):
                m, l, acc = carry[a]
                s = _nt(qs[a], kblk) + (cqs[a] - ct_ref[a:a + 1, pl.ds(off, FQ)])
                if diagonal:
                    s = jnp.where(causal, s, NEG)
                m2 = jnp.maximum(m, jnp.max(s, axis=-1, keepdims=True))
                p = jnp.exp(s - m2)
                alpha = jnp.exp(m - m2)
                new.append((m2, alpha * l + jnp.sum(p, axis=-1, keepdims=True),
                            alpha * acc + _nn(p.astype(BF16), vblk)))
            return tuple(new)

        init = (jnp.full((FQ, 1), NEG, F32), jnp.zeros((FQ, 1), F32), jnp.zeros((FQ, 128), F32))
        carry = lax.fori_loop(0, i, lambda kb, c: tile(pl.multiple_of(kb * FQ, FQ), c, False), (init, init))
        carry = tile(pl.multiple_of(i * FQ, FQ), carry, True)
        outs = []
        for a in range(2):
            m, l, acc = carry[a]
            outs.append(acc / l)
            l_ref[:, 128 * a:128 * a + 128] = jnp.broadcast_to(m + jnp.log(l), (FQ, 128))
        o_ref[...] = jnp.where(lo, outs[0], outs[1])

    return _pcall(
        body, name="fox_fwd", grid=(4, T // FQ),
        in_specs=_fox_specs() + [pl.BlockSpec((FQ, 128), lambda h, i: (i, 0)),
                                 pl.BlockSpec((None, 2, T), lambda h, i: (h, 0, 0))],
        out_specs=[pl.BlockSpec((FQ, 128), lambda h, i: (i, h)), pl.BlockSpec((FQ, 256), lambda h, i: (i, h))],
        out_shape=[jax.ShapeDtypeStruct((T, 512), F32), jax.ShapeDtypeStruct((T, 1024), F32)],
        compiler_params=pltpu.CompilerParams(dimension_semantics=("arbitrary", "arbitrary"), vmem_limit_bytes=VMEM_BIG),
    )(proj, proj, proj, c, ct3)


def _fox_bwd(proj, c, ct3, o, lse, do, after=()):
    def body(q_ref, k_ref, v_ref, c_ref, ct_ref, o_ref, l_ref, do_ref, dq_ref, dkb_ref, dvb_ref, dct_ref, dcq_ref,
             dk_ref, dv_ref):
        i = pl.program_id(1)

        @pl.when(i == 0)
        def _():
            dk_ref[...] = jnp.zeros_like(dk_ref)
            dv_ref[...] = jnp.zeros_like(dv_ref)
            dct_ref[...] = jnp.zeros_like(dct_ref)

        lo = _lane_lo(FQ)
        causal = lax.broadcasted_iota(jnp.int32, (FQ, FQ), 1) <= lax.broadcasted_iota(jnp.int32, (FQ, FQ), 0)
        q = q_ref[...]
        do_v = do_ref[...]
        prod = do_v * o_ref[...]
        qs = [_half(q, lo, a, SCALE) for a in range(2)]
        dos = [_half(do_v, lo, a) for a in range(2)]
        deltas = [jnp.sum(jnp.where(lo if a == 0 else jnp.logical_not(lo), prod, 0.0), axis=-1, keepdims=True)
                  for a in range(2)]
        cqs = [_lane_pick(c_ref[...], 2 * pl.program_id(0) + a) for a in range(2)]
        las = [l_ref[:, 128 * a:128 * a + 1] for a in range(2)]

        def tile(off, carry, diagonal):
            kblk = k_ref[pl.ds(off, FQ), :]
            vblk = v_ref[pl.ds(off, FQ), :]
            new = []
            dk = jnp.zeros((128, FQ), F32)
            dv = jnp.zeros((128, FQ), F32)
            for a in range(2):
                dq_acc, rs = carry[a]
                s = _nt(qs[a], kblk) + (cqs[a] - ct_ref[a:a + 1, pl.ds(off, FQ)])
                if diagonal:
                    s = jnp.where(causal, s, NEG)
                p = jnp.exp(s - las[a])
                ds = p * (_nt(dos[a], vblk) - deltas[a])
                dsb = ds.astype(BF16)
                dk = dk + _tn(qs[a], dsb)
                dv = dv + _tn(dos[a], p.astype(BF16))
                dct_ref[a:a + 1, pl.ds(off, FQ)] -= jnp.sum(ds, axis=0, keepdims=True)
                new.append((dq_acc + _nn(dsb, kblk), rs + jnp.sum(ds, axis=-1, keepdims=True)))
            dk_ref[:, pl.ds(off, FQ)] += dk
            dv_ref[:, pl.ds(off, FQ)] += dv
            return tuple(new)

        init = (jnp.zeros((FQ, 128), F32), jnp.zeros((FQ, 1), F32))
        carry = lax.fori_loop(0, i, lambda kb, c: tile(pl.multiple_of(kb * FQ, FQ), c, False), (init, init))
        carry = tile(pl.multiple_of(i * FQ, FQ), carry, True)
        lane = lax.broadcasted_iota(jnp.int32, (FQ, 128), 1)
        dcq_ref[...] = jnp.where(lane == 0, carry[0][1], jnp.where(lane == 1, carry[1][1], 0.0))
        dq_ref[...] = (jnp.where(lo, carry[0][0], carry[1][0]) * SCALE).astype(BF16)

        @pl.when(i == T // FQ - 1)
        def _():
            dkb_ref[...] = dk_ref[...].T.astype(BF16)
            dvb_ref[...] = dv_ref[...].T.astype(BF16)

    blk = pl.BlockSpec((FQ, 128), lambda h, i: (i, h))
    wide = pl.BlockSpec((FQ, 256), lambda h, i: (i, h))
    rows = pl.BlockSpec((None, 2, T), lambda h, i: (h, 0, 0))
    col = pl.BlockSpec((T, 128), lambda h, i: (0, h))
    return _pcall(
        _behind(body, 8, after), name="fox_bwd", grid=(4, T // FQ),
        in_specs=_fox_specs() + [pl.BlockSpec((FQ, 128), lambda h, i: (i, 0)), rows, blk, wide, blk] + [ANY_SPEC] * len(after),
        out_specs=[blk, col, col, rows, pl.BlockSpec((None, FQ, 128), lambda h, i: (h, i, 0))],
        out_shape=[jax.ShapeDtypeStruct((T, 512), BF16), jax.ShapeDtypeStruct((T, 512), BF16),
                   jax.ShapeDtypeStruct((T, 512), BF16), jax.ShapeDtypeStruct((4, 2, T), F32),
                   jax.ShapeDtypeStruct((4, T, 128), F32)],
        scratch_shapes=[pltpu.VMEM((128, T), F32), pltpu.VMEM((128, T), F32)],
        compiler_params=pltpu.CompilerParams(dimension_semantics=("arbitrary", "arbitrary"), vmem_limit_bytes=VMEM_BIG),
    )(proj, proj, proj, c, ct3, o, lse, do, *after)


def _rel_onehot():
    ridx = lax.broadcasted_iota(jnp.int32, (NREL_PAD, VW), 0)
    j = lax.broadcasted_iota(jnp.int32, (NREL_PAD, VW), 1)
    return jnp.where(ridx == jnp.clip(TQ + LEFT - 1 - j, -128, 128) + 128, 1.0, 0.0).astype(F32)


def _relvec_fwd(tbl):
    def body(t_ref, v_ref):
        v_ref[...] = _hdot(t_ref[...], _rel_onehot())

    return _one_call(body, "relvec_fwd", [tbl], [((8, VW), F32)])[0]


def _relvec_bwd(gv):
    def body(g_ref, t_ref):
        t_ref[...] = lax.dot_general(g_ref[...], _rel_onehot(), (((1,), (1,)), ((), ())),
                                     preferred_element_type=F32, precision=lax.Precision.HIGHEST)

    return _one_call(body, "relvec_bwd", [gv], [((8, NREL_PAD), F32)])[0]


def _chk_bias(vt_ref, a, hidden):
    vb = jnp.broadcast_to(vt_ref[a:a + 1, :], (TQ, VW))
    y = pltpu.roll(vb, VW - (TQ - 1), 1, stride=1, stride_axis=0)[:, :WIN]
    cr = lax.broadcasted_iota(jnp.int32, (TQ, WIN), 0) // 64
    m = lax.broadcasted_iota(jnp.int32, (TQ, WIN), 1)
    return jnp.where((m // 64 >= cr) & (m // 64 <= cr + 8) & (m >= hidden), y, NEG)


def _chk_specs():
    return [pl.BlockSpec((TQ, 128), lambda h, i: (i, CHK0 // 128 + h)),
            pl.BlockSpec((T + LEFT, 128), lambda h, i: (0, h)),
            pl.BlockSpec((T + LEFT, 128), lambda h, i: (0, 4 + h)),
            pl.BlockSpec((None, 2, VW), lambda h, i: (h, 0, 0))]


def _chk_fwd(proj, kvp, vt3, after=()):
    def body(q_ref, k_ref, v_ref, vt_ref, o_ref, l_ref, bias_ref):
        i = pl.program_id(1)

        @pl.when(i == 0)
        def _():
            for first in range(3):
                for a in range(2):
                    bias_ref[first, a] = _chk_bias(vt_ref, a, max(LEFT - first * TQ, 0))

        lo = _lane_lo()
        off = pl.multiple_of(i * TQ, TQ)
        kw = k_ref[pl.ds(off, WIN), :]
        vw = v_ref[pl.ds(off, WIN), :]
        bias_at = jnp.minimum(i, 2)
        q = q_ref[...]
        outs = []
        for a in range(2):
            s = _nt(_half(q, lo, a, SCALE), kw) + bias_ref[bias_at, a]
            m = jnp.max(s, axis=-1, keepdims=True)
            p = jnp.exp(s - m)
            l = jnp.sum(p, axis=-1, keepdims=True)
            outs.append(_nn(p.astype(BF16), vw) / l)
            l_ref[:, 128 * a:128 * a + 128] = jnp.broadcast_to(m + jnp.log(l), (TQ, 128))
        o_ref[...] = jnp.where(lo, outs[0], outs[1])

    return _pcall(
        _behind(body, 4, after), name="chk_fwd", grid=(4, T // TQ), in_specs=_chk_specs() + [ANY_SPEC] * len(after),
        out_specs=[pl.BlockSpec((TQ, 128), lambda h, i: (i, h)), pl.BlockSpec((TQ, 256), lambda h, i: (i, h))],
        out_shape=[jax.ShapeDtypeStruct((T, 512), F32), jax.ShapeDtypeStruct((T, 1024), F32)],
        scratch_shapes=[pltpu.VMEM((3, 2, TQ, WIN), F32)],
        compiler_params=pltpu.CompilerParams(dimension_semantics=("arbitrary", "arbitrary")),
    )(proj, kvp, kvp, vt3, *after)


def _chk_bwd(proj, kvp, vt3, o, lse, do, after=()):
    nq = T // TQ

    def body(q_ref, k_ref, v_ref, vt_ref, o_ref, l_ref, do_ref, dq_ref, dkb_ref, dvb_ref, gv_ref, bias_ref, dsum_ref,
             dk_ref, dv_ref):
        i = pl.program_id(1)

        @pl.when(i == 0)
        def _():
            for first in range(3):
                for a in range(2):
                    bias_ref[first, a] = _chk_bias(vt_ref, a, max(LEFT - first * TQ, 0))
            dsum_ref[...] = jnp.zeros_like(dsum_ref)
            dk_ref[...] = jnp.zeros_like(dk_ref)
            dv_ref[...] = jnp.zeros_like(dv_ref)

        lo = _lane_lo()
        off = pl.multiple_of(i * TQ, TQ)
        kw = k_ref[pl.ds(off, WIN), :]
        vw = v_ref[pl.ds(off, WIN), :]
        bias_at = jnp.minimum(i, 2)
        q = q_ref[...]
        do_v = do_ref[...]
        prod = do_v * o_ref[...]
        dqs = []
        for a in range(2):
            keep = lo if a == 0 else jnp.logical_not(lo)
            qa = _half(q, lo, a, SCALE)
            doa = _half(do_v, lo, a)
            delta = jnp.sum(jnp.where(keep, prod, 0.0), axis=-1, keepdims=True)
            s = _nt(qa, kw) + bias_ref[bias_at, a]
            p = jnp.exp(s - l_ref[:, 128 * a:128 * a + 1])
            ds = p * (_nt(doa, vw) - delta)
            dsum_ref[a] += ds
            dsb = ds.astype(BF16)
            dk_ref[:, pl.ds(off, WIN)] += _tn(qa, dsb)
            dv_ref[:, pl.ds(off, WIN)] += _tn(doa, p.astype(BF16))
            dqs.append(_nn(dsb, kw))
        dq_ref[...] = (jnp.where(lo, dqs[0], dqs[1]) * SCALE).astype(BF16)

        @pl.when(i == nq - 1)
        def _():
            dkb_ref[...] = dk_ref[:, LEFT:].T.astype(BF16)
            dvb_ref[...] = dv_ref[:, LEFT:].T.astype(BF16)
            rr = lax.broadcasted_iota(jnp.int32, (TQ, TQ), 0)
            cc = lax.broadcasted_iota(jnp.int32, (TQ, TQ), 1)
            flip = jnp.where(rr + cc == TQ - 1, 1.0, 0.0).astype(F32)
            for a in range(2):
                dpad = jnp.concatenate([dsum_ref[a], jnp.zeros((TQ, VW - WIN), F32)], axis=1)
                z = pltpu.roll(_hdot(flip, dpad), 0, 1, stride=1, stride_axis=0)
                gv_ref[a:a + 1, :] = jnp.sum(z, axis=0, keepdims=True)

    blk = pl.BlockSpec((TQ, 128), lambda h, i: (i, h))
    wide = pl.BlockSpec((TQ, 256), lambda h, i: (i, h))
    col = pl.BlockSpec((T, 128), lambda h, i: (0, h))
    return _pcall(
        _behind(body, 7, after), name="chk_bwd", grid=(4, nq), in_specs=_chk_specs() + [blk, wide, blk] + [ANY_SPEC] * len(after),
        out_specs=[blk, col, col, pl.BlockSpec((None, 2, VW), lambda h, i: (h, 0, 0))],
        out_shape=[jax.ShapeDtypeStruct((T, 512), BF16), jax.ShapeDtypeStruct((T, 512), BF16),
                   jax.ShapeDtypeStruct((T, 512), BF16), jax.ShapeDtypeStruct((4, 2, VW), F32)],
        scratch_shapes=[pltpu.VMEM((3, 2, TQ, WIN), F32), pltpu.VMEM((2, TQ, WIN), F32),
                        pltpu.VMEM((128, T + LEFT), F32), pltpu.VMEM((128, T + LEFT), F32)],
        compiler_params=pltpu.CompilerParams(dimension_semantics=("arbitrary", "arbitrary")),
    )(proj, kvp, kvp, vt3, o, lse, do, *after)


def _zero_at_start(*refs):
    @pl.when(pl.program_id(0) == 0)
    def _():
        for r in refs:
            r[...] = jnp.zeros_like(r)


def _ffn_step(h3, x2, tgt, w1_t, w2, g_post, g_pre):
    def body(h_ref, x2_ref, t_ref, w1_ref, w2_ref, gp_ref, gf_ref, dx2_ref, da_ref, dy_ref, r_ref, loss_ref, dgp_ref, dgf_ref):
        _zero_at_start(loss_ref, dgp_ref, dgf_ref)
        w1, w2v = _w(w1_ref), _w(w2_ref)
        ra = jnp.maximum(_nt(h_ref[...], w1), 0.0)
        r = jnp.square(ra).astype(BF16)
        r_ref[...] = r
        y = _nn(r, w2v)
        x2v = x2_ref[...]
        e = x2v + _rms(y, gp_ref[...]) - t_ref[...]
        loss_ref[...] += 0.5 * jnp.sum(jnp.sum(e * e, axis=-1, keepdims=True) * (1.0 / D))
        dx3 = e * (1.0 / D)
        dy, dgp = _rms_bwd(y, gp_ref[...], dx3)
        dgp_ref[...] += dgp
        dyb = dy.astype(BF16)
        dy_ref[...] = dyb
        da = (_nt(dyb, w2v) * (2.0 * ra)).astype(BF16)
        da_ref[...] = da
        dh, dgf = _rms_bwd(x2v, gf_ref[...], _nn(da, w1))
        dgf_ref[...] += dgf
        dx2_ref[...] = dx3 + dh

    return _tok_call(body, "ffn_step", [h3, x2, tgt], [w1_t, w2, g_post, g_pre],
                     [(D, F32), (DFF, BF16), (D, BF16), (DFF, BF16)], [(8, 128), (1, D), (1, D)], vmem=VMEM_BIG)


def _mem_bwd(dx2, ym, x1, qm, km, vm, w_mo, w_mq, g_post, g_pre, after=()):
    def body(dx2_ref, ym_ref, x1_ref, q_ref, k_ref, v_ref, wo_ref, wq_ref, gp_ref, gm_ref,
             dx1_ref, dym_ref, dq_ref, dk_ref, dv_ref, dgp_ref, dgm_ref, dom_ref):
        _zero_at_start(dk_ref, dv_ref, dgp_ref, dgm_ref)
        dx2_v = dx2_ref[...]
        dym, dgp = _rms_bwd(ym_ref[...], gp_ref[...], dx2_v)
        dgp_ref[...] += dgp
        dymb = dym.astype(BF16)
        dym_ref[...] = dymb
        dom_ref[...] = _nt(dymb, _w(wo_ref)).astype(BF16)
        for h in range(MEM_HEADS):
            sl = slice(h * MEM_HD, (h + 1) * MEM_HD)
            qh, kh, doh = q_ref[:, sl], k_ref[:, sl], dom_ref[:, sl]
            s = _nt(qh, kh) * MEM_SCALE
            p = jnp.exp(s - jnp.max(s, axis=-1, keepdims=True))
            p = p / jnp.sum(p, axis=-1, keepdims=True)
            dp = _nt(doh, v_ref[:, sl])
            ds = (p * (dp - jnp.sum(p * dp, axis=-1, keepdims=True))).astype(BF16)
            dq_ref[:, sl] = (_nn(ds, kh) * MEM_SCALE).astype(BF16)
            dk_ref[:, sl] += _tn(ds, qh) * MEM_SCALE
            dv_ref[:, sl] += _tn(p.astype(BF16), doh)
        dh, dgm = _rms_bwd(x1_ref[...], gm_ref[...], _nt(dq_ref[...], _w(wq_ref)))
        dgm_ref[...] += dgm
        dx1_ref[...] = dx2_v + dh

    tiled = pl.BlockSpec((TM_WIDE, D), lambda i: (i, 0))
    in_specs = [tiled] * 4 + [_resident(a) for a in (km, vm, w_mo, w_mq, g_post, g_pre)] + [ANY_SPEC] * len(after)
    w_mo, w_mq = w_mo[0], w_mq[0]
    kv = pl.BlockSpec((NMEM, D), lambda i: (0, 0))
    vec = pl.BlockSpec((1, D), lambda i: (0, 0))
    return _pcall(
        _behind(body, 10, after), name="mem_bwd", grid=(T // TM_WIDE,), in_specs=in_specs,
        out_specs=[tiled, tiled, tiled, kv, kv, vec, vec],
        out_shape=[jax.ShapeDtypeStruct((T, D), F32), jax.ShapeDtypeStruct((T, D), BF16),
                   jax.ShapeDtypeStruct((T, D), BF16), jax.ShapeDtypeStruct((NMEM, D), F32),
                   jax.ShapeDtypeStruct((NMEM, D), F32), jax.ShapeDtypeStruct((1, D), F32),
                   jax.ShapeDtypeStruct((1, D), F32)],
        scratch_shapes=[pltpu.VMEM((TM_WIDE, D), BF16)],
        compiler_params=pltpu.CompilerParams(dimension_semantics=("arbitrary",), vmem_limit_bytes=VMEM_BIG),
    )(dx2, ym, x1, qm, km, vm, w_mo, w_mq, g_post, g_pre, *after)


def _memkv_bwd(dkm, dvm, mem, w_mk, w_mv):
    def body(dk_ref, dv_ref, m_ref, wk_ref, wv_ref, dg_ref):
        dmn = _nt(dk_ref[...].astype(BF16), _w(wk_ref)) + _nt(dv_ref[...].astype(BF16), _w(wv_ref))
        mv = m_ref[...]
        dg_ref[...] = jnp.sum(dmn * (mv * _rstd(mv)), axis=0, keepdims=True)

    return _one_call(body, "memkv_bwd", [dkm, dvm, mem, w_mk, w_mv], [((1, D), F32)], vmem=VMEM_BIG)[0]


def _postmix_bwd(dx1, z, o_f, o_c, w_out, g_post, g_fo, g_co, after=()):
    def body(dx1_ref, z_ref, of_ref, oc_ref, wo_ref, gp_ref, gfo_ref, gco_ref,
             dz_ref, dof_ref, doc_ref, dgp_ref, dgfo_ref, dgco_ref):
        _zero_at_start(dgp_ref, dgfo_ref, dgco_ref)
        dz, dgp = _rms_bwd(z_ref[...], gp_ref[...], dx1_ref[...])
        dgp_ref[...] += dgp
        dzb = dz.astype(BF16)
        dz_ref[...] = dzb
        dy = _nt(dzb, _w(wo_ref))
        dof, dgfo = _rms_bwd(of_ref[...], gfo_ref[...], dy[:, :512])
        doc, dgco = _rms_bwd(oc_ref[...], gco_ref[...], dy[:, 512:])
        dof_ref[...] = dof
        doc_ref[...] = doc
        dgfo_ref[...] += dgfo
        dgco_ref[...] += dgco

    return _tok_call(body, "postmix_bwd", [dx1, z, o_f, o_c], [w_out, g_post, g_fo, g_co],
                     [(D, BF16), (512, F32), (512, F32)], [(1, D), (1, 512), (1, 512)], tm=TM_WIDE, vmem=VMEM_BIG,
                     after=after)


def _premix_bwd(dx1, x, pieces, win_t, g_pre, after=()):
    def body(dx1_ref, x_ref, *refs):
        piece_refs, (w_ref, g_ref, dx_ref, dp_ref, dg_ref) = refs[:len(pieces)], refs[len(pieces):]
        _zero_at_start(dg_ref)
        col = 0
        for p in piece_refs:
            dp_ref[:, col:col + p.shape[1]] = p[...]
            col += p.shape[1]
        dh, dg = _rms_bwd(x_ref[...], g_ref[...], _nn(dp_ref[...], w_ref[...]))
        dg_ref[...] += dg
        dx_ref[...] = dx1_ref[...] + dh

    return _tok_call(body, "premix_bwd", [dx1, x] + list(pieces), [win_t, g_pre], [(D, F32), (PROJ, BF16)], [(1, D)],
                     tm=TM_WIDE, vmem=VMEM_BIG, after=after)


def _wgrad_group(name, pairs, rows):
    def body(*refs):
        o_ref = refs[-1]
        for k in range(len(pairs)):
            g = _tn(refs[2 * k][...].astype(BF16), refs[2 * k + 1][...].astype(BF16))
            o_ref[k * rows:(k + 1) * rows, :] = g.astype(BF16)

    in_specs, ops = [], []
    for a, b in pairs:
        in_specs += [pl.BlockSpec((a.shape[0], rows), lambda j: (0, j)), _resident(b)]
        ops += [a, b]
    return _pcall(
        body, name=name, grid=(8,), in_specs=in_specs,
        out_specs=pl.BlockSpec((None, len(pairs) * rows, D), lambda j: (j, 0, 0)),
        out_shape=jax.ShapeDtypeStruct((8, len(pairs) * rows, D), BF16),
        compiler_params=pltpu.CompilerParams(dimension_semantics=("arbitrary",), vmem_limit_bytes=VMEM_BIG),
    )(*ops)


def _wgrad_whole(name, pairs):
    n = len(pairs)
    ops = [x for pair in pairs for x in pair]
    rows = pairs[0][0].shape[1] // 8

    def body(*refs):
        hbm, o_ref, bufs, sem = refs[:2 * n], refs[2 * n], refs[2 * n + 1:4 * n + 1], refs[4 * n + 1]
        k = pl.program_id(0)
        copies = [pltpu.make_async_copy(hbm[t], bufs[t], sem.at[t]) for t in range(2 * n)]

        @pl.when(k == 0)
        def _():
            for copy in copies:
                copy.start()

        for t in range(n):
            @pl.when(k == t)
            def _(t=t):
                copies[2 * t].wait()
                copies[2 * t + 1].wait()
                g = _tn(bufs[2 * t][...].astype(BF16), bufs[2 * t + 1][...].astype(BF16))
                o_ref[...] = g.reshape(8, rows, D).astype(BF16)

    return _pcall(
        body, name=name, grid=(n,), in_specs=[ANY_SPEC] * (2 * n),
        out_specs=pl.BlockSpec((8, rows, D), lambda k: (0, k, 0)),
        out_shape=jax.ShapeDtypeStruct((8, n * rows, D), BF16),
        scratch_shapes=[pltpu.VMEM(x.shape, x.dtype) for x in ops] + [pltpu.SemaphoreType.DMA((2 * n,))],
        compiler_params=pltpu.CompilerParams(dimension_semantics=("arbitrary",), vmem_limit_bytes=VMEM_BIG),
    )(*ops)


def _adam_math(w, g, m, v):
    m2 = ADAM_B1 * m + (1.0 - ADAM_B1) * g
    v2 = ADAM_B2 * v + (1.0 - ADAM_B2) * jnp.square(g)
    m_hat = m2 / (1.0 - ADAM_B1 ** ADAM_STEP)
    v_hat = v2 / (1.0 - ADAM_B2 ** ADAM_STEP)
    delta = -ADAM_LR * (m_hat / (jnp.sqrt(v_hat) + ADAM_EPS) + ADAM_WD * w)
    return delta, m2, v2


def _adamw_small(gparts, ws, ms, vs):
    n = len(SMALL)

    def body(g_ref, *refs):
        w_refs, m_refs, v_refs = refs[:n], refs[n:2 * n], refs[2 * n:3 * n]
        outs, sum_ref = refs[3 * n:-1], refs[-1]
        g = g_ref[0]
        for k in range(1, 8):
            g = g + g_ref[k]
        sum_ref[...] = g
        outs[0][...] = sum_ref[17:18, 0:128]
        for t, name in enumerate(SMALL):
            r0, nr, c0, nc = SMALL_SLOT[name]
            gt = sum_ref[r0:r0 + nr, c0:c0 + nc]
            out = (gt,) + _adam_math(w_refs[t][...], gt, m_refs[t][...], v_refs[t][...])
            for o_ref, val in zip(outs[1 + 4 * t:5 + 4 * t], out):
                o_ref[...] = val

    whole = lambda s: pl.BlockSpec(s, lambda i, nd=len(s): (0,) * nd)
    ins = [gparts] + list(ws) + list(ms) + list(vs)
    out_shapes = [(1, 128)] + [a.shape for a in ws for _ in range(4)]
    return _pcall(
        body, name="adamw_small", grid=(1,), in_specs=[whole(a.shape) for a in ins],
        out_specs=[whole(s) for s in out_shapes], out_shape=[jax.ShapeDtypeStruct(s, F32) for s in out_shapes],
        scratch_shapes=[pltpu.VMEM((SMALL_ROWS, D), F32)],
        compiler_params=pltpu.CompilerParams(dimension_semantics=("arbitrary",)),
    )(*ins)


def _row_tile(rows):
    return next(t for t in (512, 400, 320) if rows % t == 0)


def _add_halves(g4, theirs, core, name):
    rows = g4.shape[2]
    tr = _row_tile(rows)

    def body(c_ref, a_ref, b_ref, o_ref):
        o_ref[...] = (a_ref[...].astype(F32) + b_ref[...].astype(F32)).astype(BF16)

    grid_spec = pltpu.PrefetchScalarGridSpec(
        num_scalar_prefetch=1, grid=(4, rows // tr),
        in_specs=[pl.BlockSpec((None, None, tr, D), lambda j, i, c: (j, c[0], i, 0)),
                  pl.BlockSpec((None, None, tr, D), lambda j, i, c: (j, 0, i, 0))],
        out_specs=pl.BlockSpec((None, tr, D), lambda j, i, c: (j, i, 0)))
    return _pcall(
        body, name=name, grid_spec=grid_spec, out_shape=jax.ShapeDtypeStruct((4, rows, D), BF16),
        compiler_params=pltpu.CompilerParams(dimension_semantics=("arbitrary", "arbitrary")),
    )(core, g4, theirs)


def _sum_adam(own, got, order, r0, w, m, v, name, transposed=False):
    ragged = w.ndim == 3
    n = w.shape[1] if transposed else w.shape[0]
    tr = n if ragged else min(n, 256)
    rows = own.shape[1] if ragged else tr
    at = (slice(None), 0, slice(None)) if ragged else Ellipsis

    def body(o_ref, a_ref, b_ref, c_ref, d_ref, w_ref, m_ref, v_ref, *out_refs):
        f = lambda r: r[0:tr, :].astype(F32)
        g = ((f(a_ref) + f(b_ref)) + f(c_ref)) + f(d_ref)
        g = g.T if transposed else g
        for ref, val in zip(out_refs, (g,) + _adam_math(w_ref[at], g, m_ref[at], v_ref[at])):
            ref[at] = val

    slot = lambda k: pl.BlockSpec((None, rows, D), lambda i, o: (o[k], r0 // rows + i, 0))
    if ragged:
        wspec = pl.BlockSpec((n, 1, D), lambda i, o: (0, 0, 0))
    else:
        wspec = pl.BlockSpec((D, tr), lambda i, o: (0, i)) if transposed else pl.BlockSpec((tr, D), lambda i, o: (i, 0))
    grid_spec = pltpu.PrefetchScalarGridSpec(
        num_scalar_prefetch=1, grid=(n // tr,), in_specs=[slot(0), slot(1), slot(2), slot(3), wspec, wspec, wspec],
        out_specs=[wspec] * 4)
    return _pcall(
        body, name=name, grid_spec=grid_spec, out_shape=[jax.ShapeDtypeStruct(w.shape, F32)] * 4,
        compiler_params=pltpu.CompilerParams(dimension_semantics=("arbitrary",)),
    )(order, own, got, got, got, w, m, v)


def _sum_adam_rows(own, got, order, ws, ms, vs, name):
    n, rows = len(ws), ws[0].shape[0]

    def body(o_ref, a_ref, b_ref, c_ref, d_ref, *refs):
        ins, outs = refs[:3 * n], refs[3 * n:]
        for t in range(n):
            r = slice(t * rows, (t + 1) * rows)
            f = lambda ref: ref[r, :].astype(F32)
            g = ((f(a_ref) + f(b_ref)) + f(c_ref)) + f(d_ref)
            out = (g,) + _adam_math(ins[t][...], g, ins[n + t][...], ins[2 * n + t][...])
            for o, val in zip(outs[4 * t:4 * t + 4], out):
                o[...] = val

    slot = lambda k: pl.BlockSpec((None, n * rows, D), lambda i, o: (o[k], 0, 0), pipeline_mode=pl.Buffered(1))
    wspec = pl.BlockSpec((rows, D), lambda i, o: (0, 0), pipeline_mode=pl.Buffered(1))
    grid_spec = pltpu.PrefetchScalarGridSpec(
        num_scalar_prefetch=1, grid=(1,), in_specs=[slot(0), slot(1), slot(2), slot(3)] + [wspec] * (3 * n),
        out_specs=[pl.BlockSpec((rows, D), lambda i, o: (0, 0))] * (4 * n))
    return _pcall(
        body, name=name, grid_spec=grid_spec, out_shape=[jax.ShapeDtypeStruct((rows, D), F32)] * (4 * n),
        compiler_params=pltpu.CompilerParams(dimension_semantics=("arbitrary",), vmem_limit_bytes=VMEM_BIG),
    )(order, own, got, got, got, *ws, *ms, *vs)


def _place():
    return lax.axis_index("x"), lax.axis_index("y"), lax.axis_index("c")


def _allgather(block, name, after=()):
    rows = block.shape[0]
    split = (rows // 2 + 15) // 16 * 16

    def body(x_ref, out_ref, token, send_sems, recv_sems, local_sem):
        token[...] = jnp.zeros_like(token)
        x, y, c = _place()
        me, sib = (x, y, c), (x, y, 1 - c)
        xn, yn, dg = (1 - x, y), (x, 1 - y), (1 - x, 1 - y)
        lo, hi = pl.ds(0, split), pl.ds(split, rows - split)

        def copy(k, blk, to, part=None, src=None):
            index = 4 * blk[0] + 2 * blk[1] + blk[2]
            view = out_ref.at[index] if part is None else out_ref.at[index, part]
            return pltpu.make_async_remote_copy(
                src_ref=view if src is None else src, dst_ref=view,
                send_sem=send_sems.at[k], recv_sem=recv_sems.at[k], device_id=to, device_id_type=MESH)

        def start(*copies):
            for cp in copies:
                cp.start()
            return list(copies)

        mine = pltpu.make_async_copy(x_ref, out_ref.at[4 * x + 2 * y + c], local_sem)
        mine.start()
        sent = start(copy(0, me, sib, src=x_ref), copy(1, me, (*xn, c), src=x_ref), copy(2, me, (*yn, c), src=x_ref))
        copy(1, (*xn, c), me).wait_recv()
        sent += start(copy(3, (*xn, c), sib), copy(5, (*xn, c), (*yn, c), part=lo))
        copy(2, (*yn, c), me).wait_recv()
        sent += start(copy(4, (*yn, c), sib), copy(6, (*yn, c), (*xn, c), part=hi))
        copy(5, (*dg, c), me, part=lo).wait_recv()
        copy(6, (*dg, c), me, part=hi).wait_recv()
        sent += start(copy(7, (*dg, c), sib))
        for k, blk in ((0, sib), (3, (*xn, 1 - c)), (4, (*yn, 1 - c)), (7, (*dg, 1 - c))):
            copy(k, blk, me).wait_recv()
        for cp in sent:
            cp.wait_send()
        mine.wait()

    return _pcall(
        _behind(body, 1, after), name=name,
        out_shape=[jax.ShapeDtypeStruct((8,) + block.shape, block.dtype), jax.ShapeDtypeStruct((8, 128), F32)],
        in_specs=[pl.BlockSpec(memory_space=pl.ANY)] * (1 + len(after)),
        out_specs=[pl.BlockSpec(memory_space=pl.ANY), pl.BlockSpec(memory_space=pltpu.VMEM)],
        scratch_shapes=[pltpu.SemaphoreType.DMA((8,)), pltpu.SemaphoreType.DMA((8,)), pltpu.SemaphoreType.DMA(())],
        compiler_params=pltpu.CompilerParams(has_side_effects=True),
    )(block, *after)


HBM_SPEC = pl.BlockSpec(memory_space=pltpu.HBM)
SEM_SPEC = pl.BlockSpec(memory_space=pltpu.SEMAPHORE)
ANY_SPEC = pl.BlockSpec(memory_space=pl.ANY)
EFFECT = pltpu.SideEffectType.DATAFLOW_SIDE_EFFECTING


def _in_hbm(a):
    return pltpu.with_memory_space_constraint(a, pltpu.HBM)


def _start_copies(name, src, land_shape, plan, n):
    def body(src_ref, land_ref, send_sems, recv_sems, src_thru, land_thru, token):
        for k, (s, d, to, _) in enumerate(plan(src_ref, land_ref)):
            pltpu.make_async_remote_copy(src_ref=s, dst_ref=d, send_sem=send_sems.at[k], recv_sem=recv_sems.at[k],
                                         device_id=to, device_id_type=MESH).start()
        token[...] = jnp.zeros_like(token)

    return _pcall(
        body, name=name,
        out_shape=(pltpu.SemaphoreType.DMA((n,)), pltpu.SemaphoreType.DMA((n,)), pltpu.HBM(src.shape, src.dtype),
                   pltpu.HBM(land_shape, src.dtype), jax.ShapeDtypeStruct((8, 128), F32)),
        in_specs=(HBM_SPEC, HBM_SPEC),
        out_specs=(SEM_SPEC, SEM_SPEC, HBM_SPEC, HBM_SPEC, pl.BlockSpec(memory_space=pltpu.VMEM)),
        input_output_aliases={0: 2, 1: 3}, compiler_params=pltpu.CompilerParams(has_side_effects=EFFECT),
    )(_in_hbm(src), _in_hbm(lax.empty(land_shape, src.dtype)))


def _wait_copies(name, started, after, plan):
    send_sems, recv_sems, src_thru, land_thru, _ = started

    def body(src_ref, land_ref, send_sems, recv_sems, *rest):
        for k, (s, _, to, mine) in enumerate(plan(src_ref, land_ref)):
            cp = pltpu.make_async_remote_copy(src_ref=s, dst_ref=mine, send_sem=send_sems.at[k],
                                              recv_sem=recv_sems.at[k], device_id=to, device_id_type=MESH)
            cp.wait_send()
            cp.wait_recv()

    return _pcall(
        body, name=name,
        out_shape=(pltpu.HBM(src_thru.shape, src_thru.dtype), pltpu.HBM(land_thru.shape, land_thru.dtype)),
        in_specs=(HBM_SPEC, HBM_SPEC, SEM_SPEC, SEM_SPEC) + (ANY_SPEC,) * len(after), out_specs=(HBM_SPEC, HBM_SPEC),
        input_output_aliases={0: 0, 1: 1}, compiler_params=pltpu.CompilerParams(has_side_effects=EFFECT),
    )(src_thru, land_thru, send_sems, recv_sems, *after)


def _start_inplace(name, buf, plan, n):
    def body(buf_ref, send_sems, recv_sems, buf_thru, token):
        for k, (s, d, to, _) in enumerate(plan(buf_ref, buf_ref)):
            pltpu.make_async_remote_copy(src_ref=s, dst_ref=d, send_sem=send_sems.at[k], recv_sem=recv_sems.at[k],
                                         device_id=to, device_id_type=MESH).start()
        token[...] = jnp.zeros_like(token)

    return _pcall(
        body, name=name,
        out_shape=(pltpu.SemaphoreType.DMA((n,)), pltpu.SemaphoreType.DMA((n,)), pltpu.HBM(buf.shape, buf.dtype),
                   jax.ShapeDtypeStruct((8, 128), F32)),
        in_specs=(HBM_SPEC,), out_specs=(SEM_SPEC, SEM_SPEC, HBM_SPEC, pl.BlockSpec(memory_space=pltpu.VMEM)),
        input_output_aliases={0: 2}, compiler_params=pltpu.CompilerParams(has_side_effects=EFFECT),
    )(_in_hbm(buf))


def _wait_inplace(name, started, after, plan):
    send_sems, recv_sems, buf_thru, _ = started

    def body(buf_ref, send_sems, recv_sems, *rest):
        for k, (s, _, to, mine) in enumerate(plan(buf_ref, buf_ref)):
            cp = pltpu.make_async_remote_copy(src_ref=s, dst_ref=mine, send_sem=send_sems.at[k],
                                              recv_sem=recv_sems.at[k], device_id=to, device_id_type=MESH)
            cp.wait_send()
            cp.wait_recv()

    return _pcall(
        body, name=name, out_shape=pltpu.HBM(buf_thru.shape, buf_thru.dtype),
        in_specs=(HBM_SPEC, SEM_SPEC, SEM_SPEC) + (ANY_SPEC,) * len(after), out_specs=HBM_SPEC,
        input_output_aliases={0: 0}, compiler_params=pltpu.CompilerParams(has_side_effects=EFFECT),
    )(buf_thru, send_sems, recv_sems, *after)


def _gather_plan(src_ref, land_ref):
    x, y, c = _place()
    peers = [(x, y, 1 - c), (1 - x, y, c), (x, 1 - y, c)]
    return [(src_ref, land_ref.at[4 * x + 2 * y + c], p, land_ref.at[4 * p[0] + 2 * p[1] + p[2]]) for p in peers]


def _relay_plan(buf_ref, _):
    x, y, c = _place()
    slot = lambda p, pc: 4 * p[0] + 2 * p[1] + pc
    xn, yn, dg, sib = (1 - x, y), (x, 1 - y), (1 - x, 1 - y), (x, y, 1 - c)
    half = buf_ref.shape[1] // 2
    lo, hi = pl.ds(0, half), pl.ds(half, half)
    return [(buf_ref.at[slot(xn, c)], buf_ref.at[slot(xn, c)], sib, buf_ref.at[slot(xn, 1 - c)]),
            (buf_ref.at[slot(yn, c)], buf_ref.at[slot(yn, c)], sib, buf_ref.at[slot(yn, 1 - c)]),
            (buf_ref.at[slot(xn, c), lo], buf_ref.at[slot(xn, c), lo], (*yn, c), buf_ref.at[slot(dg, c), lo]),
            (buf_ref.at[slot(yn, c), hi], buf_ref.at[slot(yn, c), hi], (*xn, c), buf_ref.at[slot(dg, c), hi])]


def _swap_plan(src_ref, land_ref):
    x, y, c = _place()
    return [(src_ref.at[:, pl.ds(1 - c, 1)], land_ref, (x, y, 1 - c), land_ref)]


def _exchange_plan(src_ref, land_ref):
    x, y, c = _place()
    chips = [(1 - x, y), (x, 1 - y), (1 - x, 1 - y)]
    return [(src_ref.at[2 * px + py], land_ref.at[2 * x + y], (px, py, c), land_ref.at[2 * px + py]) for px, py in chips]


def _gather_forward(land, block):
    def body(land_ref, out_ref, send_sems, recv_sems):
        x, y, c = _place()
        chips = [(1 - x, 1 - y)]

        def copy(k, px, py, pc):
            blk = out_ref.at[4 * px + 2 * py + pc]
            return pltpu.make_async_remote_copy(src_ref=blk, dst_ref=blk, send_sem=send_sems.at[k],
                                                recv_sem=recv_sems.at[k], device_id=(x, y, 1 - c), device_id_type=MESH)

        sent = [copy(k, px, py, c) for k, (px, py) in enumerate(chips)]
        for cp in sent:
            cp.start()
        for k, (px, py) in enumerate(chips):
            copy(k, px, py, 1 - c).wait_recv()
        for cp in sent:
            cp.wait_send()

    land = _pcall(
        body, name="allgather_rest_forward", out_shape=jax.ShapeDtypeStruct(land.shape, land.dtype),
        in_specs=[ANY_SPEC], out_specs=ANY_SPEC, input_output_aliases={0: 0},
        scratch_shapes=[pltpu.SemaphoreType.DMA((1,)), pltpu.SemaphoreType.DMA((1,))],
        compiler_params=pltpu.CompilerParams(has_side_effects=True),
    )(land)

    rows = block.shape[0]
    tr = rows // 4

    def place(me_ref, x_ref, land_ref, out_ref):
        out_ref[...] = x_ref[...]

    x, y, c = _place()
    grid_spec = pltpu.PrefetchScalarGridSpec(
        num_scalar_prefetch=1, grid=(rows // tr,),
        in_specs=[pl.BlockSpec((tr, D), lambda i, me: (i, 0)), ANY_SPEC],
        out_specs=pl.BlockSpec((None, tr, D), lambda i, me: (me[0], i, 0)))
    return _pcall(
        place, name="allgather_rest_own", grid_spec=grid_spec, out_shape=jax.ShapeDtypeStruct(land.shape, land.dtype),
        input_output_aliases={2: 0}, compiler_params=pltpu.CompilerParams(dimension_semantics=("arbitrary",)),
    )((4 * x + 2 * y + c).reshape(1), block, land)


class _ReduceScatter:
    def __init__(self, name, g):
        self.name = name
        rows = g.shape[1]
        self.started = _start_copies(name + "_swap_start", g.reshape(4, 2, rows, D), (4, 1, rows, D), _swap_plan, 1)
        self.token = self.started[4]

    def halfway(self, after):
        g4, theirs = _wait_copies(self.name + "_swap_wait", self.started, after, _swap_plan)
        self.own = _add_halves(g4, theirs, lax.axis_index("c").reshape(1), self.name + "_add_halves")
        self.started = _start_copies(self.name + "_exch_start", self.own, self.own.shape, _exchange_plan, 3)
        self.token = self.started[4]

    def finish(self, after):
        own, got = _wait_copies(self.name + "_exch_wait", self.started, after, _exchange_plan)
        chip = 2 * lax.axis_index("x") + lax.axis_index("y")
        return own, got, (chip + jnp.arange(4, dtype=jnp.int32)) % 4


def _pack_small(p, loss):
    def body(*refs):
        o_ref = refs[-1]
        o_ref[...] = jnp.zeros_like(o_ref)
        for ref, name in zip(refs, SMALL):
            r0, nr, c0, nc = SMALL_SLOT[name]
            o_ref[r0:r0 + nr, c0:c0 + nc] = ref[...]
        o_ref[17:18, 0:128] = refs[len(SMALL)][0:1, :]

    return _one_call(body, "pack_small_grads", [p[n] for n in SMALL] + [loss], [((SMALL_ROWS, D), F32)])[0]


_GAP_DEV, _GAP_ROW = divmod(GATE0 + 8, N_IN)
_GAP = CHK0 - GATE0 - 8


def _in_rows_to_proj(g):
    runs = [(j, 0, N_IN, N_IN * j) for j in range(_GAP_DEV)]
    runs += [(_GAP_DEV, 0, _GAP_ROW, N_IN * _GAP_DEV), (_GAP_DEV, _GAP_ROW, N_IN, N_IN * _GAP_DEV + _GAP_ROW + _GAP)]
    runs += [(j, 0, N_IN, N_IN * j + _GAP) for j in range(_GAP_DEV + 1, 8)]

    def body(g_ref, o_ref, acc_ref):
        acc_ref[...] = jnp.zeros_like(acc_ref)
        for j, r0, r1, dest in runs:
            start, shift = dest // 16 * 16, dest % 16
            win = -(-(shift + r1 - r0) // 16) * 16
            r = lax.broadcasted_iota(jnp.int32, (win, R_IN), 0)
            c = lax.broadcasted_iota(jnp.int32, (win, R_IN), 1)
            move = jnp.where((c >= r0) & (c < r1) & (r == c - r0 + shift), 1.0, 0.0).astype(BF16)
            acc_ref[start:start + win, :] += _nn(move, g_ref[j])
        o_ref[...] = acc_ref[...].astype(BF16)

    return _pcall(
        body, name="w_in_layout", out_shape=jax.ShapeDtypeStruct((PROJ, D), BF16), grid=(1,),
        in_specs=[pl.BlockSpec(g.shape, lambda i: (0, 0, 0))], out_specs=pl.BlockSpec((PROJ, D), lambda i: (0, 0)),
        scratch_shapes=[pltpu.VMEM((PROJ, D), F32)],
        compiler_params=pltpu.CompilerParams(dimension_semantics=("arbitrary",), vmem_limit_bytes=VMEM_BIG),
    )(g)


def _wgrad_in(pieces, h1):
    n = len(pieces)
    ends = [sum(p.shape[1] for p in pieces[:k + 1]) for k in range(n)]
    assert ends[-1] == PROJ

    def body(*refs):
        piece_refs, (b_ref, o_ref, g_ref), bufs, sem = refs[:n], refs[n:n + 3], refs[n + 3:2 * n + 3], refs[2 * n + 3]
        i = pl.program_id(0)
        copies = [pltpu.make_async_copy(piece_refs[k], bufs[k], sem.at[k]) for k in range(n)]

        @pl.when(i == 0)
        def _():
            for copy in copies:
                copy.start()
            g_ref[PROJ:, :] = jnp.zeros((R_IN - N_IN + 1, D), F32)

        for k in range(n):
            @pl.when(i == k)
            def _(k=k):
                copies[k].wait()
                g_ref[ends[k] - pieces[k].shape[1]:ends[k], :] = _tn(bufs[k][...], b_ref[...])

        row = lax.broadcasted_iota(jnp.int32, (R_IN, D), 0)
        for j in range(8):
            lo = N_IN * j + (_GAP if j > _GAP_DEV else 0)
            hi = N_IN * j + (_GAP if j >= _GAP_DEV else 0)
            ready = min(k for k in range(n) if ends[k] >= min(hi + R_IN, PROJ))

            @pl.when(i == ready)
            def _(j=j, lo=lo, hi=hi):
                v = g_ref[hi:hi + R_IN, :]
                if lo != hi:
                    v = jnp.where(row < _GAP_ROW, g_ref[lo:lo + R_IN, :], v)
                o_ref[j] = jnp.where(row < N_IN, v, 0.0).astype(BF16)

    return _pcall(
        body, name="wgrad_in", grid=(n,),
        in_specs=[ANY_SPEC] * n + [_resident(h1)],
        out_specs=pl.BlockSpec((8, R_IN, D), lambda i: (0, 0, 0)),
        out_shape=jax.ShapeDtypeStruct((8, R_IN, D), BF16),
        scratch_shapes=[pltpu.VMEM((PROJ + R_IN - N_IN + 1, D), F32)] + [pltpu.VMEM(p.shape, BF16) for p in pieces]
        + [pltpu.SemaphoreType.DMA((n,))],
        compiler_params=pltpu.CompilerParams(dimension_semantics=("arbitrary",), vmem_limit_bytes=VMEM_BIG),
    )(*pieces, h1)


def _local_grads(x, mem, tgt, win_t, gw_of, sm, on_grads, after=()):
    b_pad = jnp.pad(sm['b_fgt'], ((0, 0), (0, 120)))
    tbl = jnp.pad(sm['rel_bias'], ((0, 0), (0, NREL_PAD - 257)))

    h1, proj, flog, kvp = _premix_fwd(x, sm['g_mix_pre'], win_t, after)
    c = _gate_fwd(flog, b_pad)
    ct3 = c[:, :8].T.reshape(4, 2, T)
    o_f, lse_f = _fox_fwd(proj, c, ct3)
    vt3 = _relvec_fwd(tbl).reshape(4, 2, VW)
    o_c, lse_c = _chk_fwd(proj, kvp, vt3, [gw_of('relay', [o_f])])
    gw = gw_of('done', [o_c])
    w_out, w_mq, w_mk, w_mv, w_mo, w1_t, w2 = (_wblk(gw, n) for n in ('w_out', 'w_mq', 'w_mk', 'w_mv', 'w_mo', 'w_ff1', 'w_ff2'))
    ycat, z, x1, h2, qm = _postmix_fwd(x, o_f, o_c, sm['g_fox_out'], sm['g_chk_out'], w_out,
                                       sm['g_mix_post'], sm['g_mem_pre'], w_mq)
    memn, km, vm = _memkv_fwd(mem, sm['g_mem_kv'], w_mk, w_mv)
    om, ym, x2, h3 = _mem_fwd(qm, x1, km, vm, w_mo, sm['g_mem_post'], sm['g_ff_pre'])

    gs = {}
    dx2, da, dy3, r, loss_acc, gs['g_ff_post'], gs['g_ff_pre'] = _ffn_step(h3, x2, tgt, w1_t, w2, sm['g_ff_post'],
                                                                         sm['g_ff_pre'])
    tok = on_grads('A', _wgrad_group("wgrad_ff", [(da, h3), (r, dy3)], 512), None)
    dx1, dym, dqm, dkm, dvm, gs['g_mem_post'], gs['g_mem_pre'] = _mem_bwd(
        dx2, ym, x1, qm, km, vm, w_mo, w_mq, sm['g_mem_post'], sm['g_mem_pre'], [tok])
    tok = on_grads('A halfway', None, [dx1])
    gs['g_mem_kv'] = _memkv_bwd(dkm, dvm, mem, w_mk, w_mv)
    dz, dof, doc, gs['g_mix_post'], gs['g_fox_out'], gs['g_chk_out'] = _postmix_bwd(
        dx1, z, o_f, o_c, w_out, sm['g_mix_post'], sm['g_fox_out'], sm['g_chk_out'], [tok])
    tok = on_grads('B', _wgrad_whole("wgrad_mem_out", [(ycat, dz), (h2, dqm), (memn, dkm), (memn, dvm), (om, dym)]), None)
    dq_f, dk_f, dv_f, dct, dcq = _fox_bwd(proj, c, ct3, o_f, lse_f, dof, [tok])
    tok = on_grads('B halfway', None, [dq_f])
    dq_c, dk_c, dv_c, gv = _chk_bwd(proj, kvp, vt3, o_c, lse_c, doc, [tok])
    gs['rel_bias'] = _relvec_bwd(gv.reshape(8, VW))[:, :257]
    dc = jnp.pad(dct.reshape(8, T).T + dcq[:, :, :2].transpose(1, 0, 2).reshape(T, 8), ((0, 0), (0, 120)))
    dflog, db = _gate_bwd(dc, flog, b_pad)
    gs['b_fgt'] = db[0:1, :8]
    pieces = [dq_f, dk_f, dv_f, dflog, dq_c, dk_c, dv_c]
    on_grads('C', _wgrad_in(pieces, h1), None)
    tok = on_grads('C halfway', None, [gs['g_mem_kv']])
    grad_x, _, gs['g_mix_pre'] = _premix_bwd(dx1, x, pieces, win_t, sm['g_mix_pre'], [tok])
    return loss_acc, grad_x, gs


def kernel(x, mem, w_in, b_fgt, rel_bias, g_fox_out, g_chk_out, w_out, g_mix_pre, g_mix_post, g_mem_kv, w_mq, w_mk, w_mv, w_mo, g_mem_pre, g_mem_post, w_ff1, w_ff2, g_ff_pre, g_ff_post, loss_target, m_w_in, m_b_fgt, m_rel_bias, m_g_fox_out, m_g_chk_out, m_w_out, m_g_mix_pre, m_g_mix_post, m_g_mem_kv, m_w_mq, m_w_mk, m_w_mv, m_w_mo, m_g_mem_pre, m_g_mem_post, m_w_ff1, m_w_ff2, m_g_ff_pre, m_g_ff_post, v_w_in, v_b_fgt, v_rel_bias, v_g_fox_out, v_g_chk_out, v_w_out, v_g_mix_pre, v_g_mix_post, v_g_mem_kv, v_w_mq, v_w_mk, v_w_mv, v_w_mo, v_g_mem_pre, v_g_mem_post, v_w_ff1, v_w_ff2, v_g_ff_pre, v_g_ff_post):
    args = dict(locals())
    two_d = lambda a: a.reshape(a.shape[-2:])
    w = {n: two_d(args[n]) for n in WEIGHTS}
    m = {n: two_d(args['m_' + n]) for n in WEIGHTS}
    v = {n: two_d(args['v_' + n]) for n in WEIGHTS}

    sm = {n: w[n] for n in SMALL}
    shard_in = jnp.pad(w['w_in'].T, ((0, R_IN - N_IN), (0, 0))).astype(BF16)
    gathered_in, zero = _allgather(shard_in, "allgather_w_in")
    win_t = _in_rows_to_proj(gathered_in)
    shard_rest = (jnp.concatenate([w['w_ff1'].T, w['w_ff2'], w['w_out'], w['w_mq'], w['w_mk'], w['w_mv'], w['w_mo']],
                                  axis=0) + zero[0, 0]).astype(BF16)
    gather = {'first': _start_copies("allgather_rest_start", shard_rest, (8, R_REST, D), _gather_plan, 3)}

    def gw_of(stage, after):
        if stage == 'relay':
            gather['block'], land = _wait_copies("allgather_rest_wait", gather['first'], after, _gather_plan)
            gather['second'] = _start_inplace("allgather_rest_relay_start", land, _relay_plan, 4)
            return gather['second'][3]
        land = _wait_inplace("allgather_rest_relay_wait", gather['second'], after, _relay_plan)
        return _gather_forward(land, gather['block'])

    rs = {}

    def on_grads(stage, g, after):
        if stage.endswith('halfway'):
            rs[stage[0]].halfway(after)
            return rs[stage[0]].token
        rs[stage] = _ReduceScatter("rs_" + stage.lower(), g)
        return rs[stage].token

    loss_local, grad_x, gs = _local_grads(x[0], mem[0], loss_target[0], win_t, gw_of, sm, on_grads, [gather['first'][4]])
    grads, deltas, new_m, new_v = {}, {}, {}, {}

    def update(n, out):
        grads[n], deltas[n], new_m[n], new_v[n] = out

    own, got, order = rs['A'].finish([grad_x, rs['C'].token])
    update('w_ff1', _sum_adam(own, got, order, 0, w['w_ff1'], m['w_ff1'], v['w_ff1'], "adamw_w_ff1", transposed=True))
    update('w_ff2', _sum_adam(own, got, order, 512, w['w_ff2'], m['w_ff2'], v['w_ff2'], "adamw_w_ff2"))
    own, got, order = rs['B'].finish([grad_x, rs['C'].token])
    names_b = ('w_out', 'w_mq', 'w_mk', 'w_mv', 'w_mo')
    done = _sum_adam_rows(own, got, order, [w[n] for n in names_b], [m[n] for n in names_b], [v[n] for n in names_b],
                          "adamw_group_b")
    for k, n in enumerate(names_b):
        update(n, done[4 * k:4 * k + 4])

    own, got, order = rs['C'].finish([new_v[n] for n in BIG if n != 'w_in'])
    rows_of = lambda a: jnp.transpose(a, (2, 0, 1))
    done = _sum_adam(own, got, order, 0, rows_of(w_in), rows_of(m_w_in), rows_of(v_w_in), "adamw_w_in")
    update('w_in', [jnp.transpose(a, (1, 2, 0)) for a in done])

    gparts, _ = _allgather(_pack_small(gs, loss_local), "allgather_small_grads", [got])
    small = _adamw_small(gparts, [w[n] for n in SMALL], [m[n] for n in SMALL], [v[n] for n in SMALL])
    loss = small[0][0, 0]
    for t, n in enumerate(SMALL):
        update(n, small[1 + 4 * t:5 + 4 * t])

    out = [loss, grad_x[None]]
    for group in (grads, deltas, new_m, new_v):
        out += [group[n].reshape(args[n].shape) for n in WEIGHTS]
    return tuple(out)
```

```python
import jax
import jax.numpy as jnp
from jax import lax
from jax.experimental import pallas as pl
from jax.experimental.pallas import tpu as pltpu

F32 = jnp.float32
BF16 = jnp.bfloat16
MESH = pl.DeviceIdType.MESH

T = 2048
D = 1024
NMEM = 256
DFF = 4096
EPS = 1e-6
TM = 256
TM_WIDE = 512
TQ = 256
FQ = 512
HD = 64
SCALE = HD ** -0.5
MEM_HEADS = 4
MEM_HD = 256
MEM_SCALE = MEM_HD ** -0.5
NEG = -1e30
LEFT = 512
WIN = LEFT + TQ
VW = 1024
NREL_PAD = 384
PROJ = 3200
GATE0 = 1536
CHK0 = 1664
VMEM_BIG = 56 * 1024 * 1024

ADAM_LR = 0.001
ADAM_B1 = 0.9
ADAM_B2 = 0.999
ADAM_EPS = 1e-08
ADAM_WD = 0.01
ADAM_STEP = 10

N_IN = 385
R_IN = 400
R_REST = 1664
W_ROWS = {'w_ff1': (0, 512), 'w_ff2': (512, 512),
          'w_out': (1024, 128), 'w_mq': (1152, 128), 'w_mk': (1280, 128), 'w_mv': (1408, 128), 'w_mo': (1536, 128)}
SMALL_ROWS = 24
SMALL_SLOT = {'rel_bias': (0, 8, 0, 257), 'b_fgt': (8, 1, 0, 8), 'g_fox_out': (9, 1, 0, 512), 'g_chk_out': (9, 1, 512, 512),
              'g_mix_pre': (10, 1, 0, 1024), 'g_mix_post': (11, 1, 0, 1024), 'g_mem_kv': (12, 1, 0, 1024),
              'g_mem_pre': (13, 1, 0, 1024), 'g_mem_post': (14, 1, 0, 1024), 'g_ff_pre': (15, 1, 0, 1024),
              'g_ff_post': (16, 1, 0, 1024)}

WEIGHTS = ['w_in', 'b_fgt', 'rel_bias', 'g_fox_out', 'g_chk_out', 'w_out', 'g_mix_pre', 'g_mix_post', 'g_mem_kv',
           'w_mq', 'w_mk', 'w_mv', 'w_mo', 'g_mem_pre', 'g_mem_post', 'w_ff1', 'w_ff2', 'g_ff_pre', 'g_ff_post']
BIG = ['w_in', 'w_out', 'w_mq', 'w_mk', 'w_mv', 'w_mo', 'w_ff1', 'w_ff2']
SMALL = [n for n in WEIGHTS if n not in BIG]


def _pcall(body, **kw):
    return pl.pallas_call(body, **kw)


def _nn(a, b):
    return jnp.dot(a, b, preferred_element_type=F32)


def _nt(a, b):
    return lax.dot_general(a, b, (((1,), (1,)), ((), ())), preferred_element_type=F32)


def _tn(a, b):
    return lax.dot_general(a, b, (((0,), (0,)), ((), ())), preferred_element_type=F32)


def _w(ref):
    v = ref[...]
    return v if v.ndim == 2 else v.reshape(-1, v.shape[-1])


def _rstd(x):
    return lax.rsqrt(jnp.mean(x * x, axis=-1, keepdims=True) + EPS)


def _rms(x, g):
    return x * _rstd(x) * g


def _rms_bwd(x, g, dy):
    r = _rstd(x)
    xh = x * r
    dg = jnp.sum(dy * xh, axis=0, keepdims=True)
    dxh = dy * g
    dx = r * (dxh - xh * jnp.mean(dxh * xh, axis=-1, keepdims=True))
    return dx, dg


def _resident(a):
    if isinstance(a, tuple):
        _, shape, index = a
        return pl.BlockSpec(shape, lambda *_: index, pipeline_mode=pl.Buffered(1))
    return pl.BlockSpec(a.shape, lambda *_, nd=a.ndim: (0,) * nd, pipeline_mode=pl.Buffered(1))


def _wblk(gw, name):
    r0, rows = W_ROWS[name]
    return (gw, (8, rows, D), (0, r0 // rows, 0))


def _behind(body, n_in, after):
    if not after:
        return body
    return lambda *refs: body(*refs[:n_in], *refs[n_in + len(after):])


def _tok_call(body, name, tiled, full, outs_tiled, outs_acc=(), rows=T, tm=TM, vmem=None, after=()):
    in_specs = [pl.BlockSpec((tm, a.shape[1]), lambda i: (i, 0)) for a in tiled]
    in_specs += [_resident(a) for a in full] + [ANY_SPEC] * len(after)
    full = [a[0] if isinstance(a, tuple) else a for a in full] + list(after)
    body = _behind(body, len(tiled) + len(full) - len(after), after)
    out_shape = [jax.ShapeDtypeStruct((rows, c), dt) for c, dt in outs_tiled]
    out_shape += [jax.ShapeDtypeStruct(s, F32) for s in outs_acc]
    out_specs = [pl.BlockSpec((tm, c), lambda i: (i, 0)) for c, _ in outs_tiled]
    out_specs += [pl.BlockSpec(s, lambda i, nd=len(s): (0,) * nd) for s in outs_acc]
    return _pcall(
        body, name=name, grid=(rows // tm,), in_specs=in_specs, out_specs=out_specs, out_shape=out_shape,
        compiler_params=pltpu.CompilerParams(dimension_semantics=("arbitrary",), vmem_limit_bytes=vmem),
    )(*tiled, *full)


def _one_call(body, name, ins, outs, vmem=None):
    whole = lambda s: pl.BlockSpec(s, lambda i, nd=len(s): (0,) * nd)
    return _pcall(
        body, name=name, grid=(1,), in_specs=[_resident(a) for a in ins], out_specs=[whole(s) for s, _ in outs],
        out_shape=[jax.ShapeDtypeStruct(s, dt) for s, dt in outs],
        compiler_params=pltpu.CompilerParams(dimension_semantics=("arbitrary",), vmem_limit_bytes=vmem),
    )(*[a[0] if isinstance(a, tuple) else a for a in ins])


def _premix_fwd(x, g_pre, win_t, after=()):
    def body(x_ref, g_ref, w_ref, h_ref, proj_ref, flog_ref, kvp_ref):
        s = pl.program_id(0)

        @pl.when(s == 0)
        def _():
            kvp_ref[...] = jnp.zeros_like(kvp_ref)

        @pl.when(s > 0)
        def _():
            h = _rms(x_ref[...], g_ref[...]).astype(BF16)
            h_ref[...] = h
            p = _nt(h, w_ref[...])
            proj_ref[...] = p.astype(BF16)
            flog_ref[...] = p[:, GATE0:GATE0 + 128]
            kvp_ref[...] = p[:, CHK0 + 512:].astype(BF16)

    tile = lambda c: pl.BlockSpec((LEFT, c), lambda s: (jnp.maximum(s - 1, 0), 0))
    return _pcall(
        _behind(body, 3, after), name="premix_fwd", grid=(T // LEFT + 1,),
        in_specs=[tile(D), _resident(g_pre), _resident(win_t)] + [ANY_SPEC] * len(after),
        out_specs=[tile(D), tile(PROJ), tile(128), pl.BlockSpec((LEFT, 1024), lambda s: (s, 0))],
        out_shape=[jax.ShapeDtypeStruct((T, D), BF16), jax.ShapeDtypeStruct((T, PROJ), BF16),
                   jax.ShapeDtypeStruct((T, 128), F32), jax.ShapeDtypeStruct((T + LEFT, 1024), BF16)],
        compiler_params=pltpu.CompilerParams(dimension_semantics=("arbitrary",), vmem_limit_bytes=VMEM_BIG),
    )(x, g_pre, win_t, *after)


def _postmix_fwd(x, o_f, o_c, g_fo, g_co, w_out, g_post, g_mpre, w_mq):
    def body(x_ref, of_ref, oc_ref, gfo_ref, gco_ref, wo_ref, gp_ref, gm_ref, wq_ref,
             y_ref, z_ref, x1_ref, h2_ref, qm_ref):
        y_ref[:, :512] = _rms(of_ref[...], gfo_ref[...]).astype(BF16)
        y_ref[:, 512:] = _rms(oc_ref[...], gco_ref[...]).astype(BF16)
        z = _nn(y_ref[...], _w(wo_ref))
        z_ref[...] = z
        x1 = x_ref[...] + _rms(z, gp_ref[...])
        x1_ref[...] = x1
        h2 = _rms(x1, gm_ref[...]).astype(BF16)
        h2_ref[...] = h2
        qm_ref[...] = _nn(h2, _w(wq_ref)).astype(BF16)

    return _tok_call(body, "postmix_fwd", [x, o_f, o_c], [g_fo, g_co, w_out, g_post, g_mpre, w_mq],
                     [(D, BF16), (D, F32), (D, F32), (D, BF16), (D, BF16)], tm=TM_WIDE, vmem=VMEM_BIG)


def _memkv_fwd(mem, g_kv, w_mk, w_mv):
    def body(m_ref, g_ref, wk_ref, wv_ref, mn_ref, k_ref, v_ref):
        mn = _rms(m_ref[...], g_ref[...]).astype(BF16)
        mn_ref[...] = mn
        k_ref[...] = _nn(mn, _w(wk_ref)).astype(BF16)
        v_ref[...] = _nn(mn, _w(wv_ref)).astype(BF16)

    return _tok_call(body, "memkv_fwd", [mem], [g_kv, w_mk, w_mv],
                     [(D, BF16), (D, BF16), (D, BF16)], rows=NMEM, tm=NMEM, vmem=VMEM_BIG)


def _mem_fwd(qm, x1, km, vm, w_mo, g_post, g_fpre):
    def body(q_ref, x1_ref, k_ref, v_ref, wo_ref, gp_ref, gf_ref, om_ref, ym_ref, x2_ref, h3_ref):
        for h in range(MEM_HEADS):
            sl = slice(h * MEM_HD, (h + 1) * MEM_HD)
            s = _nt(q_ref[:, sl], k_ref[:, sl]) * MEM_SCALE
            p = jnp.exp(s - jnp.max(s, axis=-1, keepdims=True))
            p = p / jnp.sum(p, axis=-1, keepdims=True)
            om_ref[:, sl] = _nn(p.astype(BF16), v_ref[:, sl]).astype(BF16)
        ym = _nn(om_ref[...], _w(wo_ref))
        ym_ref[...] = ym
        x2 = x1_ref[...] + _rms(ym, gp_ref[...])
        x2_ref[...] = x2
        h3_ref[...] = _rms(x2, gf_ref[...]).astype(BF16)

    return _tok_call(body, "mem_fwd", [qm, x1], [km, vm, w_mo, g_post, g_fpre],
                     [(D, BF16), (D, F32), (D, F32), (D, BF16)], tm=TM_WIDE, vmem=VMEM_BIG)


def _tri(lower):
    r = lax.broadcasted_iota(jnp.int32, (128, 128), 0)
    c = lax.broadcasted_iota(jnp.int32, (128, 128), 1)
    return jnp.where(r >= c if lower else c >= r, 1.0, 0.0).astype(F32)


def _hdot(a, b):
    return jnp.dot(a, b, preferred_element_type=F32, precision=lax.Precision.HIGHEST)


def _gate_fwd(flog, b_pad):
    def body(f_ref, b_ref, c_ref):
        tri = _tri(True)

        def step(i, carry):
            rows = pl.ds(pl.multiple_of(i * 128, 128), 128)
            z = f_ref[rows, :] + b_ref[...]
            lf = jnp.minimum(z, 0.0) - jnp.log(1.0 + jnp.exp(-jnp.abs(z)))
            cb = _hdot(tri, lf) + carry
            c_ref[rows, :] = cb
            return cb[127:128, :]

        lax.fori_loop(0, T // 128, step, jnp.zeros((1, 128), F32))

    return _one_call(body, "gate_fwd", [flog, b_pad], [((T, 128), F32)])[0]


def _gate_bwd(dct, dcq, flog, b_pad):
    def body(dct_ref, dcq_ref, f_ref, b_ref, df_ref, db_ref):
        tri = _tri(False)
        lane = lax.broadcasted_iota(jnp.int32, (128, 128), 1)

        def step(j, carry):
            run, db = carry
            i = T // 128 - 1 - j
            rows = pl.ds(pl.multiple_of(i * 128, 128), 128)
            dcb = jnp.concatenate([dct_ref[:, rows], jnp.zeros((120, 128), F32)], axis=0).T
            for h in range(4):
                part = jnp.where(lane < 2, dcq_ref[h, rows, :], 0.0)
                dcb = dcb + (pltpu.roll(part, 2 * h, 1) if h else part)
            rb = _hdot(tri, dcb) + run
            z = f_ref[rows, :] + b_ref[...]
            df = rb * (1.0 / (1.0 + jnp.exp(z)))
            df_ref[rows, :] = df.astype(BF16)
            return run + jnp.sum(dcb, axis=0, keepdims=True), db + jnp.sum(df, axis=0, keepdims=True)

        _, db = lax.fori_loop(0, T // 128, step, (jnp.zeros((1, 128), F32), jnp.zeros((1, 128), F32)))
        db_ref[...] = jnp.broadcast_to(db, (8, 128))

    return _one_call(body, "gate_bwd", [dct, dcq, flog, b_pad], [((T, 128), BF16), ((8, 128), F32)])


def _lane_lo(rows=TQ):
    return lax.broadcasted_iota(jnp.int32, (rows, 128), 1) < HD


def _half(v, lo, a, scale=None):
    keep = lo if a == 0 else jnp.logical_not(lo)
    v = v.astype(F32) if scale is None else v.astype(F32) * scale
    return jnp.where(keep, v, 0.0).astype(BF16)


def _fox_specs():
    return [pl.BlockSpec((FQ, 128), lambda h, i: (i, h)),
            pl.BlockSpec((T, 128), lambda h, i: (0, 4 + h)),
            pl.BlockSpec((T, 128), lambda h, i: (0, 8 + h))]


def _lane_pick(x, at):
    lane = lax.broadcasted_iota(jnp.int32, x.shape, 1)
    return jnp.sum(jnp.where(lane == at, x, 0.0), axis=-1, keepdims=True)


def _fox_fwd(proj, c, ct3):
    def body(q_ref, k_ref, v_ref, c_ref, ct_ref, o_ref, l_ref):
        i = pl.program_id(1)
        lo = _lane_lo(FQ)
        causal = lax.broadcasted_iota(jnp.int32, (FQ, FQ), 1) <= lax.broadcasted_iota(jnp.int32, (FQ, FQ), 0)
        q = q_ref[...]
        qs = [_half(q, lo, a, SCALE) for a in range(2)]
        cqs = [_lane_pick(c_ref[...], 2 * pl.program_id(0) + a) for a in range(2)]

        def tile(off, carry, diagonal):
            kblk = k_ref[pl.ds(off, FQ), :]
            vblk = v_ref[pl.ds(off, FQ), :]
            new = []
            for a in range(2):
                m, l, acc = carry[a]
                s = _nt(qs[a], kblk) + (cqs[a] - ct_ref[a:a + 1, pl.ds(off, FQ)])
                if diagonal:
                    s = jnp.where(causal, s, NEG)
                m2 = jnp.maximum(m, jnp.max(s, axis=-1, keepdims=True))
                p = jnp.exp(s - m2)
                alpha = jnp.exp(m - m2)
                new.append((m2, alpha * l + jnp.sum(p, axis=-1, keepdims=True),
                            alpha * acc + _nn(p.astype(BF16), vblk)))
            return tuple(new)

        init = (jnp.full((FQ, 1), NEG, F32), jnp.zeros((FQ, 1), F32), jnp.zeros((FQ, 128), F32))
        carry = lax.fori_loop(0, i, lambda kb, c: tile(pl.multiple_of(kb * FQ, FQ), c, False), (init, init))
        carry = tile(pl.multiple_of(i * FQ, FQ), carry, True)
        outs = []
        for a in range(2):
            m, l, acc = carry[a]
            outs.append(acc / l)
            l_ref[:, 128 * a:128 * a + 128] = jnp.broadcast_to(m + jnp.log(l), (FQ, 128))
        o_ref[...] = jnp.where(lo, outs[0], outs[1])

    return _pcall(
        body, name="fox_fwd", grid=(4, T // FQ),
        in_specs=_fox_specs() + [pl.BlockSpec((FQ, 128), lambda h, i: (i, 0)),
                                 pl.BlockSpec((None, 2, T), lambda h, i: (h, 0, 0))],
        out_specs=[pl.BlockSpec((FQ, 128), lambda h, i: (i, h)), pl.BlockSpec((FQ, 256), lambda h, i: (i, h))],
        out_shape=[jax.ShapeDtypeStruct((T, 512), F32), jax.ShapeDtypeStruct((T, 1024), F32)],
        compiler_params=pltpu.CompilerParams(dimension_semantics=("arbitrary", "arbitrary"), vmem_limit_bytes=VMEM_BIG),
    )(proj, proj, proj, c, ct3)


def _fox_bwd(proj, c, ct3, o, lse, do, after=()):
    def body(q_ref, k_ref, v_ref, c_ref, ct_ref, o_ref, l_ref, do_ref, dq_ref, dkb_ref, dvb_ref, dct_ref, dcq_ref,
             dk_ref, dv_ref):
        i = pl.program_id(1)

        @pl.when(i == 0)
        def _():
            dk_ref[...] = jnp.zeros_like(dk_ref)
            dv_ref[...] = jnp.zeros_like(dv_ref)
            dct_ref[...] = jnp.zeros_like(dct_ref)

        lo = _lane_lo(FQ)
        causal = lax.broadcasted_iota(jnp.int32, (FQ, FQ), 1) <= lax.broadcasted_iota(jnp.int32, (FQ, FQ), 0)
        q = q_ref[...]
        do_v = do_ref[...]
        prod = do_v * o_ref[...]
        qs = [_half(q, lo, a, SCALE) for a in range(2)]
        dos = [_half(do_v, lo, a) for a in range(2)]
        deltas = [jnp.sum(jnp.where(lo if a == 0 else jnp.logical_not(lo), prod, 0.0), axis=-1, keepdims=True)
                  for a in range(2)]
        cqs = [_lane_pick(c_ref[...], 2 * pl.program_id(0) + a) for a in range(2)]
        las = [l_ref[:, 128 * a:128 * a + 1] for a in range(2)]

        def tile(off, carry, diagonal):
            kblk = k_ref[pl.ds(off, FQ), :]
            vblk = v_ref[pl.ds(off, FQ), :]
            new = []
            dk = jnp.zeros((128, FQ), F32)
            dv = jnp.zeros((128, FQ), F32)
            for a in range(2):
                dq_acc, rs = carry[a]
                s = _nt(qs[a], kblk) + (cqs[a] - ct_ref[a:a + 1, pl.ds(off, FQ)])
                if diagonal:
                    s = jnp.where(causal, s, NEG)
                p = jnp.exp(s - las[a])
                ds = p * (_nt(dos[a], vblk) - deltas[a])
                dsb = ds.astype(BF16)
                dk = dk + _tn(qs[a], dsb)
                dv = dv + _tn(dos[a], p.astype(BF16))
                dct_ref[a:a + 1, pl.ds(off, FQ)] -= jnp.sum(ds, axis=0, keepdims=True)
                new.append((dq_acc + _nn(dsb, kblk), rs + jnp.sum(ds, axis=-1, keepdims=True)))
            dk_ref[:, pl.ds(off, FQ)] += dk
            dv_ref[:, pl.ds(off, FQ)] += dv
            return tuple(new)

        init = (jnp.zeros((FQ, 128), F32), jnp.zeros((FQ, 1), F32))
        carry = lax.fori_loop(0, i, lambda kb, c: tile(pl.multiple_of(kb * FQ, FQ), c, False), (init, init))
        carry = tile(pl.multiple_of(i * FQ, FQ), carry, True)
        lane = lax.broadcasted_iota(jnp.int32, (FQ, 128), 1)
        dcq_ref[...] = jnp.where(lane == 0, carry[0][1], jnp.where(lane == 1, carry[1][1], 0.0))
        dq_ref[...] = (jnp.where(lo, carry[0][0], carry[1][0]) * SCALE).astype(BF16)

        @pl.when(i == T // FQ - 1)
        def _():
            dkb_ref[...] = dk_ref[...].T.astype(BF16)
            dvb_ref[...] = dv_ref[...].T.astype(BF16)

    blk = pl.BlockSpec((FQ, 128), lambda h, i: (i, h))
    wide = pl.BlockSpec((FQ, 256), lambda h, i: (i, h))
    rows = pl.BlockSpec((None, 2, T), lambda h, i: (h, 0, 0))
    col = pl.BlockSpec((T, 128), lambda h, i: (0, h))
    return _pcall(
        _behind(body, 8, after), name="fox_bwd", grid=(4, T // FQ),
        in_specs=_fox_specs() + [pl.BlockSpec((FQ, 128), lambda h, i: (i, 0)), rows, blk, wide, blk] + [ANY_SPEC] * len(after),
        out_specs=[blk, col, col, rows, pl.BlockSpec((None, FQ, 128), lambda h, i: (h, i, 0))],
        out_shape=[jax.ShapeDtypeStruct((T, 512), BF16), jax.ShapeDtypeStruct((T, 512), BF16),
                   jax.ShapeDtypeStruct((T, 512), BF16), jax.ShapeDtypeStruct((4, 2, T), F32),
                   jax.ShapeDtypeStruct((4, T, 128), F32)],
        scratch_shapes=[pltpu.VMEM((128, T), F32), pltpu.VMEM((128, T), F32)],
        compiler_params=pltpu.CompilerParams(dimension_semantics=("arbitrary", "arbitrary"), vmem_limit_bytes=VMEM_BIG),
    )(proj, proj, proj, c, ct3, o, lse, do, *after)


def _rel_onehot():
    ridx = lax.broadcasted_iota(jnp.int32, (NREL_PAD, VW), 0)
    j = lax.broadcasted_iota(jnp.int32, (NREL_PAD, VW), 1)
    return jnp.where(ridx == jnp.clip(TQ + LEFT - 1 - j, -128, 128) + 128, 1.0, 0.0).astype(F32)


def _relvec_fwd(tbl):
    def body(t_ref, v_ref):
        v_ref[...] = _hdot(t_ref[...], _rel_onehot())

    return _one_call(body, "relvec_fwd", [tbl], [((8, VW), F32)])[0]


def _relvec_bwd(gv):
    def body(g_ref, t_ref):
        t_ref[...] = lax.dot_general(g_ref[...], _rel_onehot(), (((1,), (1,)), ((), ())),
                                     preferred_element_type=F32, precision=lax.Precision.HIGHEST)

    return _one_call(body, "relvec_bwd", [gv], [((8, NREL_PAD), F32)])[0]


def _chk_bias(vt_ref, a, hidden):
    vb = jnp.broadcast_to(vt_ref[a:a + 1, :], (TQ, VW))
    y = pltpu.roll(vb, VW - (TQ - 1), 1, stride=1, stride_axis=0)[:, :WIN]
    cr = lax.broadcasted_iota(jnp.int32, (TQ, WIN), 0) // 64
    m = lax.broadcasted_iota(jnp.int32, (TQ, WIN), 1)
    return jnp.where((m // 64 >= cr) & (m // 64 <= cr + 8) & (m >= hidden), y, NEG)


def _chk_specs():
    return [pl.BlockSpec((TQ, 128), lambda h, i: (i, CHK0 // 128 + h)),
            pl.BlockSpec((T + LEFT, 128), lambda h, i: (0, h)),
            pl.BlockSpec((T + LEFT, 128), lambda h, i: (0, 4 + h)),
            pl.BlockSpec((None, 2, VW), lambda h, i: (h, 0, 0))]


def _chk_fwd(proj, kvp, vt3, after=()):
    def body(q_ref, k_ref, v_ref, vt_ref, o_ref, l_ref, bias_ref):
        i = pl.program_id(1)

        @pl.when(i == 0)
        def _():
            for first in range(3):
                for a in range(2):
                    bias_ref[first, a] = _chk_bias(vt_ref, a, max(LEFT - first * TQ, 0))

        lo = _lane_lo()
        off = pl.multiple_of(i * TQ, TQ)
        kw = k_ref[pl.ds(off, WIN), :]
        vw = v_ref[pl.ds(off, WIN), :]
        bias_at = jnp.minimum(i, 2)
        q = q_ref[...]
        outs = []
        for a in range(2):
            s = _nt(_half(q, lo, a, SCALE), kw) + bias_ref[bias_at, a]
            m = jnp.max(s, axis=-1, keepdims=True)
            p = jnp.exp(s - m)
            l = jnp.sum(p, axis=-1, keepdims=True)
            outs.append(_nn(p.astype(BF16), vw) / l)
            l_ref[:, 128 * a:128 * a + 128] = jnp.broadcast_to(m + jnp.log(l), (TQ, 128))
        o_ref[...] = jnp.where(lo, outs[0], outs[1])

    return _pcall(
        _behind(body, 4, after), name="chk_fwd", grid=(4, T // TQ), in_specs=_chk_specs() + [ANY_SPEC] * len(after),
        out_specs=[pl.BlockSpec((TQ, 128), lambda h, i: (i, h)), pl.BlockSpec((TQ, 256), lambda h, i: (i, h))],
        out_shape=[jax.ShapeDtypeStruct((T, 512), F32), jax.ShapeDtypeStruct((T, 1024), F32)],
        scratch_shapes=[pltpu.VMEM((3, 2, TQ, WIN), F32)],
        compiler_params=pltpu.CompilerParams(dimension_semantics=("arbitrary", "arbitrary")),
    )(proj, kvp, kvp, vt3, *after)


def _chk_bwd(proj, kvp, vt3, o, lse, do, after=()):
    nq = T // TQ

    def body(q_ref, k_ref, v_ref, vt_ref, o_ref, l_ref, do_ref, dq_ref, dkb_ref, dvb_ref, gv_ref, bias_ref, dsum_ref,
             dk_ref, dv_ref):
        i = pl.program_id(1)

        @pl.when(i == 0)
        def _():
            for first in range(3):
                for a in range(2):
                    bias_ref[first, a] = _chk_bias(vt_ref, a, max(LEFT - first * TQ, 0))
            dsum_ref[...] = jnp.zeros_like(dsum_ref)
            dk_ref[...] = jnp.zeros_like(dk_ref)
            dv_ref[...] = jnp.zeros_like(dv_ref)

        lo = _lane_lo()
        off = pl.multiple_of(i * TQ, TQ)
        kw = k_ref[pl.ds(off, WIN), :]
        vw = v_ref[pl.ds(off, WIN), :]
        bias_at = jnp.minimum(i, 2)
        q = q_ref[...]
        do_v = do_ref[...]
        prod = do_v * o_ref[...]
        dqs = []
        for a in range(2):
            keep = lo if a == 0 else jnp.logical_not(lo)
            qa = _half(q, lo, a, SCALE)
            doa = _half(do_v, lo, a)
            delta = jnp.sum(jnp.where(keep, prod, 0.0), axis=-1, keepdims=True)
            s = _nt(qa, kw) + bias_ref[bias_at, a]
            p = jnp.exp(s - l_ref[:, 128 * a:128 * a + 1])
            ds = p * (_nt(doa, vw) - delta)
            dsum_ref[a] += ds
            dsb = ds.astype(BF16)
            dk_ref[:, pl.ds(off, WIN)] += _tn(qa, dsb)
            dv_ref[:, pl.ds(off, WIN)] += _tn(doa, p.astype(BF16))
            dqs.append(_nn(dsb, kw))
        dq_ref[...] = (jnp.where(lo, dqs[0], dqs[1]) * SCALE).astype(BF16)

        @pl.when(i == nq - 1)
        def _():
            dkb_ref[...] = dk_ref[:, LEFT:].T.astype(BF16)
            dvb_ref[...] = dv_ref[:, LEFT:].T.astype(BF16)
            rr = lax.broadcasted_iota(jnp.int32, (TQ, TQ), 0)
            cc = lax.broadcasted_iota(jnp.int32, (TQ, TQ), 1)
            flip = jnp.where(rr + cc == TQ - 1, 1.0, 0.0).astype(F32)
            for a in range(2):
                dpad = jnp.concatenate([dsum_ref[a], jnp.zeros((TQ, VW - WIN), F32)], axis=1)
                z = pltpu.roll(_hdot(flip, dpad), 0, 1, stride=1, stride_axis=0)
                gv_ref[a:a + 1, :] = jnp.sum(z, axis=0, keepdims=True)

    blk = pl.BlockSpec((TQ, 128), lambda h, i: (i, h))
    wide = pl.BlockSpec((TQ, 256), lambda h, i: (i, h))
    col = pl.BlockSpec((T, 128), lambda h, i: (0, h))
    return _pcall(
        _behind(body, 7, after), name="chk_bwd", grid=(4, nq), in_specs=_chk_specs() + [blk, wide, blk] + [ANY_SPEC] * len(after),
        out_specs=[blk, col, col, pl.BlockSpec((None, 2, VW), lambda h, i: (h, 0, 0))],
        out_shape=[jax.ShapeDtypeStruct((T, 512), BF16), jax.ShapeDtypeStruct((T, 512), BF16),
                   jax.ShapeDtypeStruct((T, 512), BF16), jax.ShapeDtypeStruct((4, 2, VW), F32)],
        scratch_shapes=[pltpu.VMEM((3, 2, TQ, WIN), F32), pltpu.VMEM((2, TQ, WIN), F32),
                        pltpu.VMEM((128, T + LEFT), F32), pltpu.VMEM((128, T + LEFT), F32)],
        compiler_params=pltpu.CompilerParams(dimension_semantics=("arbitrary", "arbitrary")),
    )(proj, kvp, kvp, vt3, o, lse, do, *after)


def _zero_at_start(*refs):
    @pl.when(pl.program_id(0) == 0)
    def _():
        for r in refs:
            r[...] = jnp.zeros_like(r)


def _ffn_step(h3, x2, tgt, w1_t, w2, g_post, g_pre):
    def body(h_ref, x2_ref, t_ref, w1_ref, w2_ref, gp_ref, gf_ref, dx2_ref, da_ref, dy_ref, r_ref, loss_ref, dgp_ref, dgf_ref):
        _zero_at_start(loss_ref, dgp_ref, dgf_ref)
        w1, w2v = _w(w1_ref), _w(w2_ref)
        ra = jnp.maximum(_nt(h_ref[...], w1), 0.0)
        r = jnp.square(ra).astype(BF16)
        r_ref[...] = r
        y = _nn(r, w2v)
        x2v = x2_ref[...]
        e = x2v + _rms(y, gp_ref[...]) - t_ref[...]
        loss_ref[...] += 0.5 * jnp.sum(jnp.sum(e * e, axis=-1, keepdims=True) * (1.0 / D))
        dx3 = e * (1.0 / D)
        dy, dgp = _rms_bwd(y, gp_ref[...], dx3)
        dgp_ref[...] += dgp
        dyb = dy.astype(BF16)
        dy_ref[...] = dyb
        da = (_nt(dyb, w2v) * (2.0 * ra)).astype(BF16)
        da_ref[...] = da
        dh, dgf = _rms_bwd(x2v, gf_ref[...], _nn(da, w1))
        dgf_ref[...] += dgf
        dx2_ref[...] = dx3 + dh

    return _tok_call(body, "ffn_step", [h3, x2, tgt], [w1_t, w2, g_post, g_pre],
                     [(D, F32), (DFF, BF16), (D, BF16), (DFF, BF16)], [(8, 128), (1, D), (1, D)], vmem=VMEM_BIG)


def _mem_bwd(dx2, ym, x1, qm, km, vm, w_mo, w_mq, g_post, g_pre, after=()):
    def body(dx2_ref, ym_ref, x1_ref, q_ref, k_ref, v_ref, wo_ref, wq_ref, gp_ref, gm_ref,
             dx1_ref, dym_ref, dq_ref, dk_ref, dv_ref, dgp_ref, dgm_ref, dom_ref):
        _zero_at_start(dk_ref, dv_ref, dgp_ref, dgm_ref)
        dx2_v = dx2_ref[...]
        dym, dgp = _rms_bwd(ym_ref[...], gp_ref[...], dx2_v)
        dgp_ref[...] += dgp
        dymb = dym.astype(BF16)
        dym_ref[...] = dymb
        dom_ref[...] = _nt(dymb, _w(wo_ref)).astype(BF16)
        for h in range(MEM_HEADS):
            sl = slice(h * MEM_HD, (h + 1) * MEM_HD)
            qh, kh, doh = q_ref[:, sl], k_ref[:, sl], dom_ref[:, sl]
            s = _nt(qh, kh) * MEM_SCALE
            p = jnp.exp(s - jnp.max(s, axis=-1, keepdims=True))
            p = p / jnp.sum(p, axis=-1, keepdims=True)
            dp = _nt(doh, v_ref[:, sl])
            ds = (p * (dp - jnp.sum(p * dp, axis=-1, keepdims=True))).astype(BF16)
            dq_ref[:, sl] = (_nn(ds, kh) * MEM_SCALE).astype(BF16)
            dk_ref[:, sl] += _tn(ds, qh) * MEM_SCALE
            dv_ref[:, sl] += _tn(p.astype(BF16), doh)
        dh, dgm = _rms_bwd(x1_ref[...], gm_ref[...], _nt(dq_ref[...], _w(wq_ref)))
        dgm_ref[...] += dgm
        dx1_ref[...] = dx2_v + dh

    tiled = pl.BlockSpec((TM_WIDE, D), lambda i: (i, 0))
    in_specs = [tiled] * 4 + [_resident(a) for a in (km, vm, w_mo, w_mq, g_post, g_pre)] + [ANY_SPEC] * len(after)
    w_mo, w_mq = w_mo[0], w_mq[0]
    kv = pl.BlockSpec((NMEM, D), lambda i: (0, 0))
    vec = pl.BlockSpec((1, D), lambda i: (0, 0))
    return _pcall(
        _behind(body, 10, after), name="mem_bwd", grid=(T // TM_WIDE,), in_specs=in_specs,
        out_specs=[tiled, tiled, tiled, kv, kv, vec, vec],
        out_shape=[jax.ShapeDtypeStruct((T, D), F32), jax.ShapeDtypeStruct((T, D), BF16),
                   jax.ShapeDtypeStruct((T, D), BF16), jax.ShapeDtypeStruct((NMEM, D), F32),
                   jax.ShapeDtypeStruct((NMEM, D), F32), jax.ShapeDtypeStruct((1, D), F32),
                   jax.ShapeDtypeStruct((1, D), F32)],
        scratch_shapes=[pltpu.VMEM((TM_WIDE, D), BF16)],
        compiler_params=pltpu.CompilerParams(dimension_semantics=("arbitrary",), vmem_limit_bytes=VMEM_BIG),
    )(dx2, ym, x1, qm, km, vm, w_mo, w_mq, g_post, g_pre, *after)


def _memkv_bwd(dkm, dvm, mem, w_mk, w_mv):
    def body(dk_ref, dv_ref, m_ref, wk_ref, wv_ref, dg_ref):
        dmn = _nt(dk_ref[...].astype(BF16), _w(wk_ref)) + _nt(dv_ref[...].astype(BF16), _w(wv_ref))
        mv = m_ref[...]
        dg_ref[...] = jnp.sum(dmn * (mv * _rstd(mv)), axis=0, keepdims=True)

    return _one_call(body, "memkv_bwd", [dkm, dvm, mem, w_mk, w_mv], [((1, D), F32)], vmem=VMEM_BIG)[0]


def _postmix_bwd(dx1, z, o_f, o_c, w_out, g_post, g_fo, g_co, after=()):
    def body(dx1_ref, z_ref, of_ref, oc_ref, wo_ref, gp_ref, gfo_ref, gco_ref,
             dz_ref, dof_ref, doc_ref, dgp_ref, dgfo_ref, dgco_ref):
        _zero_at_start(dgp_ref, dgfo_ref, dgco_ref)
        dz, dgp = _rms_bwd(z_ref[...], gp_ref[...], dx1_ref[...])
        dgp_ref[...] += dgp
        dzb = dz.astype(BF16)
        dz_ref[...] = dzb
        dy = _nt(dzb, _w(wo_ref))
        dof, dgfo = _rms_bwd(of_ref[...], gfo_ref[...], dy[:, :512])
        doc, dgco = _rms_bwd(oc_ref[...], gco_ref[...], dy[:, 512:])
        dof_ref[...] = dof
        doc_ref[...] = doc
        dgfo_ref[...] += dgfo
        dgco_ref[...] += dgco

    return _tok_call(body, "postmix_bwd", [dx1, z, o_f, o_c], [w_out, g_post, g_fo, g_co],
                     [(D, BF16), (512, F32), (512, F32)], [(1, D), (1, 512), (1, 512)], tm=TM_WIDE, vmem=VMEM_BIG,
                     after=after)


def _premix_bwd(dx1, x, pieces, win_t, g_pre, after=()):
    def body(dx1_ref, x_ref, *refs):
        piece_refs, (w_ref, g_ref, dx_ref, dp_ref, dg_ref) = refs[:len(pieces)], refs[len(pieces):]
        _zero_at_start(dg_ref)
        col = 0
        for p in piece_refs:
            dp_ref[:, col:col + p.shape[1]] = p[...]
            col += p.shape[1]
        dh, dg = _rms_bwd(x_ref[...], g_ref[...], _nn(dp_ref[...], w_ref[...]))
        dg_ref[...] += dg
        dx_ref[...] = dx1_ref[...] + dh

    return _tok_call(body, "premix_bwd", [dx1, x] + list(pieces), [win_t, g_pre], [(D, F32), (PROJ, BF16)], [(1, D)],
                     tm=TM_WIDE, vmem=VMEM_BIG, after=after)


def _wgrad_group(name, pairs, rows):
    def body(*refs):
        o_ref = refs[-1]
        for k in range(len(pairs)):
            g = _tn(refs[2 * k][...].astype(BF16), refs[2 * k + 1][...].astype(BF16))
            o_ref[k * rows:(k + 1) * rows, :] = g.astype(BF16)

    in_specs, ops = [], []
    for a, b in pairs:
        in_specs += [pl.BlockSpec((a.shape[0], rows), lambda j: (0, j)), _resident(b)]
        ops += [a, b]
    return _pcall(
        body, name=name, grid=(8,), in_specs=in_specs,
        out_specs=pl.BlockSpec((None, len(pairs) * rows, D), lambda j: (j, 0, 0)),
        out_shape=jax.ShapeDtypeStruct((8, len(pairs) * rows, D), BF16),
        compiler_params=pltpu.CompilerParams(dimension_semantics=("arbitrary",), vmem_limit_bytes=VMEM_BIG),
    )(*ops)


def _wgrad_whole(name, pairs):
    n = len(pairs)
    ops = [x for pair in pairs for x in pair]
    rows = pairs[0][0].shape[1] // 8

    def body(*refs):
        hbm, o_ref, bufs, sem = refs[:2 * n], refs[2 * n], refs[2 * n + 1:4 * n + 1], refs[4 * n + 1]
        k = pl.program_id(0)
        copies = [pltpu.make_async_copy(hbm[t], bufs[t], sem.at[t]) for t in range(2 * n)]

        @pl.when(k == 0)
        def _():
            for copy in copies:
                copy.start()

        for t in range(n):
            @pl.when(k == t)
            def _(t=t):
                copies[2 * t].wait()
                copies[2 * t + 1].wait()
                g = _tn(bufs[2 * t][...].astype(BF16), bufs[2 * t + 1][...].astype(BF16))
                o_ref[...] = g.reshape(8, rows, D).astype(BF16)

    return _pcall(
        body, name=name, grid=(n,), in_specs=[ANY_SPEC] * (2 * n),
        out_specs=pl.BlockSpec((8, rows, D), lambda k: (0, k, 0)),
        out_shape=jax.ShapeDtypeStruct((8, n * rows, D), BF16),
        scratch_shapes=[pltpu.VMEM(x.shape, x.dtype) for x in ops] + [pltpu.SemaphoreType.DMA((2 * n,))],
        compiler_params=pltpu.CompilerParams(dimension_semantics=("arbitrary",), vmem_limit_bytes=VMEM_BIG),
    )(*ops)


def _adam_math(w, g, m, v):
    m2 = ADAM_B1 * m + (1.0 - ADAM_B1) * g
    v2 = ADAM_B2 * v + (1.0 - ADAM_B2) * jnp.square(g)
    m_hat = m2 / (1.0 - ADAM_B1 ** ADAM_STEP)
    v_hat = v2 / (1.0 - ADAM_B2 ** ADAM_STEP)
    delta = -ADAM_LR * (m_hat / (jnp.sqrt(v_hat) + ADAM_EPS) + ADAM_WD * w)
    return delta, m2, v2


def _adamw_small(gparts, ws, ms, vs):
    n = len(SMALL)

    def body(g_ref, *refs):
        w_refs, m_refs, v_refs = refs[:n], refs[n:2 * n], refs[2 * n:3 * n]
        outs, sum_ref = refs[3 * n:-1], refs[-1]
        g = g_ref[0]
        for k in range(1, 8):
            g = g + g_ref[k]
        sum_ref[...] = g
        outs[0][...] = sum_ref[17:18, 0:128]
        for t, name in enumerate(SMALL):
            r0, nr, c0, nc = SMALL_SLOT[name]
            gt = sum_ref[r0:r0 + nr, c0:c0 + nc]
            out = (gt,) + _adam_math(w_refs[t][...], gt, m_refs[t][...], v_refs[t][...])
            for o_ref, val in zip(outs[1 + 4 * t:5 + 4 * t], out):
                o_ref[...] = val

    whole = lambda s: pl.BlockSpec(s, lambda i, nd=len(s): (0,) * nd)
    ins = [gparts] + list(ws) + list(ms) + list(vs)
    out_shapes = [(1, 128)] + [a.shape for a in ws for _ in range(4)]
    return _pcall(
        body, name="adamw_small", grid=(1,), in_specs=[whole(a.shape) for a in ins],
        out_specs=[whole(s) for s in out_shapes], out_shape=[jax.ShapeDtypeStruct(s, F32) for s in out_shapes],
        scratch_shapes=[pltpu.VMEM((SMALL_ROWS, D), F32)],
        compiler_params=pltpu.CompilerParams(dimension_semantics=("arbitrary",)),
    )(*ins)


def _row_tile(rows):
    return next(t for t in (512, 400, 320) if rows % t == 0)


def _add_halves(g4, theirs, core, name):
    rows = g4.shape[2]
    tr = _row_tile(rows)

    def body(c_ref, a_ref, b_ref, o_ref):
        o_ref[...] = (a_ref[...].astype(F32) + b_ref[...].astype(F32)).astype(BF16)

    grid_spec = pltpu.PrefetchScalarGridSpec(
        num_scalar_prefetch=1, grid=(4, rows // tr),
        in_specs=[pl.BlockSpec((None, None, tr, D), lambda j, i, c: (j, c[0], i, 0)),
                  pl.BlockSpec((None, None, tr, D), lambda j, i, c: (j, 0, i, 0))],
        out_specs=pl.BlockSpec((None, tr, D), lambda j, i, c: (j, i, 0)))
    return _pcall(
        body, name=name, grid_spec=grid_spec, out_shape=jax.ShapeDtypeStruct((4, rows, D), BF16),
        compiler_params=pltpu.CompilerParams(dimension_semantics=("arbitrary", "arbitrary")),
    )(core, g4, theirs)


def _sum_adam(own, got, order, r0, w, m, v, name, transposed=False):
    ragged = w.ndim == 3
    n = w.shape[1] if transposed else w.shape[0]
    tr = n if ragged else min(n, 256)
    rows = own.shape[1] if ragged else tr
    at = (slice(None), 0, slice(None)) if ragged else Ellipsis

    def body(o_ref, a_ref, b_ref, c_ref, d_ref, w_ref, m_ref, v_ref, *out_refs):
        f = lambda r: r[0:tr, :].astype(F32)
        g = ((f(a_ref) + f(b_ref)) + f(c_ref)) + f(d_ref)
        g = g.T if transposed else g
        for ref, val in zip(out_refs, (g,) + _adam_math(w_ref[at], g, m_ref[at], v_ref[at])):
            ref[at] = val

    slot = lambda k: pl.BlockSpec((None, rows, D), lambda i, o: (o[k], r0 // rows + i, 0))
    if ragged:
        wspec = pl.BlockSpec((n, 1, D), lambda i, o: (0, 0, 0))
    else:
        wspec = pl.BlockSpec((D, tr), lambda i, o: (0, i)) if transposed else pl.BlockSpec((tr, D), lambda i, o: (i, 0))
    grid_spec = pltpu.PrefetchScalarGridSpec(
        num_scalar_prefetch=1, grid=(n // tr,), in_specs=[slot(0), slot(1), slot(2), slot(3), wspec, wspec, wspec],
        out_specs=[wspec] * 4)
    return _pcall(
        body, name=name, grid_spec=grid_spec, out_shape=[jax.ShapeDtypeStruct(w.shape, F32)] * 4,
        compiler_params=pltpu.CompilerParams(dimension_semantics=("arbitrary",)),
    )(order, own, got, got, got, w, m, v)


def _sum_adam_rows(own, got, order, ws, ms, vs, name):
    n, rows = len(ws), ws[0].shape[0]

    def body(o_ref, a_ref, b_ref, c_ref, d_ref, *refs):
        ins, outs = refs[:3 * n], refs[3 * n:]
        for t in range(n):
            r = slice(t * rows, (t + 1) * rows)
            f = lambda ref: ref[r, :].astype(F32)
            g = ((f(a_ref) + f(b_ref)) + f(c_ref)) + f(d_ref)
            out = (g,) + _adam_math(ins[t][...], g, ins[n + t][...], ins[2 * n + t][...])
            for o, val in zip(outs[4 * t:4 * t + 4], out):
                o[...] = val

    slot = lambda k: pl.BlockSpec((None, n * rows, D), lambda i, o: (o[k], 0, 0), pipeline_mode=pl.Buffered(1))
    wspec = pl.BlockSpec((rows, D), lambda i, o: (0, 0), pipeline_mode=pl.Buffered(1))
    grid_spec = pltpu.PrefetchScalarGridSpec(
        num_scalar_prefetch=1, grid=(1,), in_specs=[slot(0), slot(1), slot(2), slot(3)] + [wspec] * (3 * n),
        out_specs=[pl.BlockSpec((rows, D), lambda i, o: (0, 0))] * (4 * n))
    return _pcall(
        body, name=name, grid_spec=grid_spec, out_shape=[jax.ShapeDtypeStruct((rows, D), F32)] * (4 * n),
        compiler_params=pltpu.CompilerParams(dimension_semantics=("arbitrary",), vmem_limit_bytes=VMEM_BIG),
    )(order, own, got, got, got, *ws, *ms, *vs)


def _place():
    return lax.axis_index("x"), lax.axis_index("y"), lax.axis_index("c")


def _allgather(block, name, after=()):
    rows = block.shape[0]
    split = (rows // 2 + 15) // 16 * 16

    def body(x_ref, out_ref, token, send_sems, recv_sems, local_sem):
        token[...] = jnp.zeros_like(token)
        x, y, c = _place()
        me, sib = (x, y, c), (x, y, 1 - c)
        xn, yn, dg = (1 - x, y), (x, 1 - y), (1 - x, 1 - y)
        lo, hi = pl.ds(0, split), pl.ds(split, rows - split)

        def copy(k, blk, to, part=None, src=None):
            index = 4 * blk[0] + 2 * blk[1] + blk[2]
            view = out_ref.at[index] if part is None else out_ref.at[index, part]
            return pltpu.make_async_remote_copy(
                src_ref=view if src is None else src, dst_ref=view,
                send_sem=send_sems.at[k], recv_sem=recv_sems.at[k], device_id=to, device_id_type=MESH)

        def start(*copies):
            for cp in copies:
                cp.start()
            return list(copies)

        mine = pltpu.make_async_copy(x_ref, out_ref.at[4 * x + 2 * y + c], local_sem)
        mine.start()
        sent = start(copy(0, me, sib, src=x_ref), copy(1, me, (*xn, c), src=x_ref), copy(2, me, (*yn, c), src=x_ref))
        copy(1, (*xn, c), me).wait_recv()
        sent += start(copy(3, (*xn, c), sib), copy(5, (*xn, c), (*yn, c), part=lo))
        copy(2, (*yn, c), me).wait_recv()
        sent += start(copy(4, (*yn, c), sib), copy(6, (*yn, c), (*xn, c), part=hi))
        copy(5, (*dg, c), me, part=lo).wait_recv()
        copy(6, (*dg, c), me, part=hi).wait_recv()
        sent += start(copy(7, (*dg, c), sib))
        for k, blk in ((0, sib), (3, (*xn, 1 - c)), (4, (*yn, 1 - c)), (7, (*dg, 1 - c))):
            copy(k, blk, me).wait_recv()
        for cp in sent:
            cp.wait_send()
        mine.wait()

    return _pcall(
        _behind(body, 1, after), name=name,
        out_shape=[jax.ShapeDtypeStruct((8,) + block.shape, block.dtype), jax.ShapeDtypeStruct((8, 128), F32)],
        in_specs=[pl.BlockSpec(memory_space=pl.ANY)] * (1 + len(after)),
        out_specs=[pl.BlockSpec(memory_space=pl.ANY), pl.BlockSpec(memory_space=pltpu.VMEM)],
        scratch_shapes=[pltpu.SemaphoreType.DMA((8,)), pltpu.SemaphoreType.DMA((8,)), pltpu.SemaphoreType.DMA(())],
        compiler_params=pltpu.CompilerParams(has_side_effects=True),
    )(block, *after)


HBM_SPEC = pl.BlockSpec(memory_space=pltpu.HBM)
SEM_SPEC = pl.BlockSpec(memory_space=pltpu.SEMAPHORE)
ANY_SPEC = pl.BlockSpec(memory_space=pl.ANY)
EFFECT = pltpu.SideEffectType.DATAFLOW_SIDE_EFFECTING


def _in_hbm(a):
    return pltpu.with_memory_space_constraint(a, pltpu.HBM)


def _start_copies(name, src, land_shape, plan, n):
    def body(src_ref, land_ref, send_sems, recv_sems, src_thru, land_thru, token):
        for k, (s, d, to, _) in enumerate(plan(src_ref, land_ref)):
            pltpu.make_async_remote_copy(src_ref=s, dst_ref=d, send_sem=send_sems.at[k], recv_sem=recv_sems.at[k],
                                         device_id=to, device_id_type=MESH).start()
        token[...] = jnp.zeros_like(token)

    return _pcall(
        body, name=name,
        out_shape=(pltpu.SemaphoreType.DMA((n,)), pltpu.SemaphoreType.DMA((n,)), pltpu.HBM(src.shape, src.dtype),
                   pltpu.HBM(land_shape, src.dtype), jax.ShapeDtypeStruct((8, 128), F32)),
        in_specs=(HBM_SPEC, HBM_SPEC),
        out_specs=(SEM_SPEC, SEM_SPEC, HBM_SPEC, HBM_SPEC, pl.BlockSpec(memory_space=pltpu.VMEM)),
        input_output_aliases={0: 2, 1: 3}, compiler_params=pltpu.CompilerParams(has_side_effects=EFFECT),
    )(_in_hbm(src), _in_hbm(lax.empty(land_shape, src.dtype)))


def _wait_copies(name, started, after, plan):
    send_sems, recv_sems, src_thru, land_thru, _ = started

    def body(src_ref, land_ref, send_sems, recv_sems, *rest):
        for k, (s, _, to, mine) in enumerate(plan(src_ref, land_ref)):
            cp = pltpu.make_async_remote_copy(src_ref=s, dst_ref=mine, send_sem=send_sems.at[k],
                                              recv_sem=recv_sems.at[k], device_id=to, device_id_type=MESH)
            cp.wait_send()
            cp.wait_recv()

    return _pcall(
        body, name=name,
        out_shape=(pltpu.HBM(src_thru.shape, src_thru.dtype), pltpu.HBM(land_thru.shape, land_thru.dtype)),
        in_specs=(HBM_SPEC, HBM_SPEC, SEM_SPEC, SEM_SPEC) + (ANY_SPEC,) * len(after), out_specs=(HBM_SPEC, HBM_SPEC),
        input_output_aliases={0: 0, 1: 1}, compiler_params=pltpu.CompilerParams(has_side_effects=EFFECT),
    )(src_thru, land_thru, send_sems, recv_sems, *after)


def _start_inplace(name, buf, plan, n):
    def body(buf_ref, send_sems, recv_sems, buf_thru, token):
        for k, (s, d, to, _) in enumerate(plan(buf_ref, buf_ref)):
            pltpu.make_async_remote_copy(src_ref=s, dst_ref=d, send_sem=send_sems.at[k], recv_sem=recv_sems.at[k],
                                         device_id=to, device_id_type=MESH).start()
        token[...] = jnp.zeros_like(token)

    return _pcall(
        body, name=name,
        out_shape=(pltpu.SemaphoreType.DMA((n,)), pltpu.SemaphoreType.DMA((n,)), pltpu.HBM(buf.shape, buf.dtype),
                   jax.ShapeDtypeStruct((8, 128), F32)),
        in_specs=(HBM_SPEC,), out_specs=(SEM_SPEC, SEM_SPEC, HBM_SPEC, pl.BlockSpec(memory_space=pltpu.VMEM)),
        input_output_aliases={0: 2}, compiler_params=pltpu.CompilerParams(has_side_effects=EFFECT),
    )(_in_hbm(buf))


def _wait_inplace(name, started, after, plan):
    send_sems, recv_sems, buf_thru, _ = started

    def body(buf_ref, send_sems, recv_sems, *rest):
        for k, (s, _, to, mine) in enumerate(plan(buf_ref, buf_ref)):
            cp = pltpu.make_async_remote_copy(src_ref=s, dst_ref=mine, send_sem=send_sems.at[k],
                                              recv_sem=recv_sems.at[k], device_id=to, device_id_type=MESH)
            cp.wait_send()
            cp.wait_recv()

    return _pcall(
        body, name=name, out_shape=pltpu.HBM(buf_thru.shape, buf_thru.dtype),
        in_specs=(HBM_SPEC, SEM_SPEC, SEM_SPEC) + (ANY_SPEC,) * len(after), out_specs=HBM_SPEC,
        input_output_aliases={0: 0}, compiler_params=pltpu.CompilerParams(has_side_effects=EFFECT),
    )(buf_thru, send_sems, recv_sems, *after)


def _gather_plan(src_ref, land_ref):
    x, y, c = _place()
    peers = [(x, y, 1 - c), (1 - x, y, c), (x, 1 - y, c)]
    return [(src_ref, land_ref.at[4 * x + 2 * y + c], p, land_ref.at[4 * p[0] + 2 * p[1] + p[2]]) for p in peers]


def _relay_plan(buf_ref, _):
    x, y, c = _place()
    slot = lambda p, pc: 4 * p[0] + 2 * p[1] + pc
    xn, yn, dg, sib = (1 - x, y), (x, 1 - y), (1 - x, 1 - y), (x, y, 1 - c)
    half = buf_ref.shape[1] // 2
    lo, hi = pl.ds(0, half), pl.ds(half, half)
    return [(buf_ref.at[slot(xn, c)], buf_ref.at[slot(xn, c)], sib, buf_ref.at[slot(xn, 1 - c)]),
            (buf_ref.at[slot(yn, c)], buf_ref.at[slot(yn, c)], sib, buf_ref.at[slot(yn, 1 - c)]),
            (buf_ref.at[slot(xn, c), lo], buf_ref.at[slot(xn, c), lo], (*yn, c), buf_ref.at[slot(dg, c), lo]),
            (buf_ref.at[slot(yn, c), hi], buf_ref.at[slot(yn, c), hi], (*xn, c), buf_ref.at[slot(dg, c), hi])]


def _swap_plan(src_ref, land_ref):
    x, y, c = _place()
    return [(src_ref.at[:, pl.ds(1 - c, 1)], land_ref, (x, y, 1 - c), land_ref)]


def _exchange_plan(src_ref, land_ref):
    x, y, c = _place()
    chips = [(1 - x, y), (x, 1 - y), (1 - x, 1 - y)]
    return [(src_ref.at[2 * px + py], land_ref.at[2 * x + y], (px, py, c), land_ref.at[2 * px + py]) for px, py in chips]


def _gather_forward(land, block):
    def body(land_ref, out_ref, send_sems, recv_sems):
        x, y, c = _place()
        chips = [(1 - x, 1 - y)]

        def copy(k, px, py, pc):
            blk = out_ref.at[4 * px + 2 * py + pc]
            return pltpu.make_async_remote_copy(src_ref=blk, dst_ref=blk, send_sem=send_sems.at[k],
                                                recv_sem=recv_sems.at[k], device_id=(x, y, 1 - c), device_id_type=MESH)

        sent = [copy(k, px, py, c) for k, (px, py) in enumerate(chips)]
        for cp in sent:
            cp.start()
        for k, (px, py) in enumerate(chips):
            copy(k, px, py, 1 - c).wait_recv()
        for cp in sent:
            cp.wait_send()

    land = _pcall(
        body, name="allgather_rest_forward", out_shape=jax.ShapeDtypeStruct(land.shape, land.dtype),
        in_specs=[ANY_SPEC], out_specs=ANY_SPEC, input_output_aliases={0: 0},
        scratch_shapes=[pltpu.SemaphoreType.DMA((1,)), pltpu.SemaphoreType.DMA((1,))],
        compiler_params=pltpu.CompilerParams(has_side_effects=True),
    )(land)

    rows = block.shape[0]
    tr = rows // 4

    def place(me_ref, x_ref, land_ref, out_ref):
        out_ref[...] = x_ref[...]

    x, y, c = _place()
    grid_spec = pltpu.PrefetchScalarGridSpec(
        num_scalar_prefetch=1, grid=(rows // tr,),
        in_specs=[pl.BlockSpec((tr, D), lambda i, me: (i, 0)), ANY_SPEC],
        out_specs=pl.BlockSpec((None, tr, D), lambda i, me: (me[0], i, 0)))
    return _pcall(
        place, name="allgather_rest_own", grid_spec=grid_spec, out_shape=jax.ShapeDtypeStruct(land.shape, land.dtype),
        input_output_aliases={2: 0}, compiler_params=pltpu.CompilerParams(dimension_semantics=("arbitrary",)),
    )((4 * x + 2 * y + c).reshape(1), block, land)


class _ReduceScatter:
    def __init__(self, name, g):
        self.name = name
        rows = g.shape[1]
        self.started = _start_copies(name + "_swap_start", g.reshape(4, 2, rows, D), (4, 1, rows, D), _swap_plan, 1)
        self.token = self.started[4]

    def halfway(self, after):
        g4, theirs = _wait_copies(self.name + "_swap_wait", self.started, after, _swap_plan)
        self.own = _add_halves(g4, theirs, lax.axis_index("c").reshape(1), self.name + "_add_halves")
        self.started = _start_copies(self.name + "_exch_start", self.own, self.own.shape, _exchange_plan, 3)
        self.token = self.started[4]

    def finish(self, after):
        own, got = _wait_copies(self.name + "_exch_wait", self.started, after, _exchange_plan)
        chip = 2 * lax.axis_index("x") + lax.axis_index("y")
        return own, got, (chip + jnp.arange(4, dtype=jnp.int32)) % 4


def _pack_small(p, loss):
    def body(*refs):
        o_ref = refs[-1]
        o_ref[...] = jnp.zeros_like(o_ref)
        for ref, name in zip(refs, SMALL):
            r0, nr, c0, nc = SMALL_SLOT[name]
            o_ref[r0:r0 + nr, c0:c0 + nc] = ref[...]
        o_ref[17:18, 0:128] = refs[len(SMALL)][0:1, :]

    return _one_call(body, "pack_small_grads", [p[n] for n in SMALL] + [loss], [((SMALL_ROWS, D), F32)])[0]


_GAP_DEV, _GAP_ROW = divmod(GATE0 + 8, N_IN)
_GAP = CHK0 - GATE0 - 8


def _in_rows_to_proj(g):
    runs = [(j, 0, N_IN, N_IN * j) for j in range(_GAP_DEV)]
    runs += [(_GAP_DEV, 0, _GAP_ROW, N_IN * _GAP_DEV), (_GAP_DEV, _GAP_ROW, N_IN, N_IN * _GAP_DEV + _GAP_ROW + _GAP)]
    runs += [(j, 0, N_IN, N_IN * j + _GAP) for j in range(_GAP_DEV + 1, 8)]

    def body(g_ref, o_ref, acc_ref):
        acc_ref[...] = jnp.zeros_like(acc_ref)
        for j, r0, r1, dest in runs:
            start, shift = dest // 16 * 16, dest % 16
            win = -(-(shift + r1 - r0) // 16) * 16
            r = lax.broadcasted_iota(jnp.int32, (win, R_IN), 0)
            c = lax.broadcasted_iota(jnp.int32, (win, R_IN), 1)
            move = jnp.where((c >= r0) & (c < r1) & (r == c - r0 + shift), 1.0, 0.0).astype(BF16)
            acc_ref[start:start + win, :] += _nn(move, g_ref[j])
        o_ref[...] = acc_ref[...].astype(BF16)

    return _pcall(
        body, name="w_in_layout", out_shape=jax.ShapeDtypeStruct((PROJ, D), BF16), grid=(1,),
        in_specs=[pl.BlockSpec(g.shape, lambda i: (0, 0, 0))], out_specs=pl.BlockSpec((PROJ, D), lambda i: (0, 0)),
        scratch_shapes=[pltpu.VMEM((PROJ, D), F32)],
        compiler_params=pltpu.CompilerParams(dimension_semantics=("arbitrary",), vmem_limit_bytes=VMEM_BIG),
    )(g)


def _wgrad_in(pieces, h1):
    n = len(pieces)
    ends = [sum(p.shape[1] for p in pieces[:k + 1]) for k in range(n)]
    assert ends[-1] == PROJ

    def body(*refs):
        piece_refs, (b_ref, o_ref, g_ref), bufs, sem = refs[:n], refs[n:n + 3], refs[n + 3:2 * n + 3], refs[2 * n + 3]
        i = pl.program_id(0)
        copies = [pltpu.make_async_copy(piece_refs[k], bufs[k], sem.at[k]) for k in range(n)]

        @pl.when(i == 0)
        def _():
            for copy in copies:
                copy.start()
            g_ref[PROJ:, :] = jnp.zeros((R_IN - N_IN + 1, D), F32)

        for k in range(n):
            @pl.when(i == k)
            def _(k=k):
                copies[k].wait()
                g_ref[ends[k] - pieces[k].shape[1]:ends[k], :] = _tn(bufs[k][...], b_ref[...])

        row = lax.broadcasted_iota(jnp.int32, (R_IN, D), 0)
        for j in range(8):
            lo = N_IN * j + (_GAP if j > _GAP_DEV else 0)
            hi = N_IN * j + (_GAP if j >= _GAP_DEV else 0)
            ready = min(k for k in range(n) if ends[k] >= min(hi + R_IN, PROJ))

            @pl.when(i == ready)
            def _(j=j, lo=lo, hi=hi):
                v = g_ref[hi:hi + R_IN, :]
                if lo != hi:
                    v = jnp.where(row < _GAP_ROW, g_ref[lo:lo + R_IN, :], v)
                o_ref[j] = jnp.where(row < N_IN, v, 0.0).astype(BF16)

    return _pcall(
        body, name="wgrad_in", grid=(n,),
        in_specs=[ANY_SPEC] * n + [_resident(h1)],
        out_specs=pl.BlockSpec((8, R_IN, D), lambda i: (0, 0, 0)),
        out_shape=jax.ShapeDtypeStruct((8, R_IN, D), BF16),
        scratch_shapes=[pltpu.VMEM((PROJ + R_IN - N_IN + 1, D), F32)] + [pltpu.VMEM(p.shape, BF16) for p in pieces]
        + [pltpu.SemaphoreType.DMA((n,))],
        compiler_params=pltpu.CompilerParams(dimension_semantics=("arbitrary",), vmem_limit_bytes=VMEM_BIG),
    )(*pieces, h1)


def _local_grads(x, mem, tgt, win_t, gw_of, sm, on_grads, after=()):
    b_pad = jnp.pad(sm['b_fgt'], ((0, 0), (0, 120)))
    tbl = jnp.pad(sm['rel_bias'], ((0, 0), (0, NREL_PAD - 257)))

    h1, proj, flog, kvp = _premix_fwd(x, sm['g_mix_pre'], win_t, after)
    c = _gate_fwd(flog, b_pad)
    ct3 = c[:, :8].T.reshape(4, 2, T)
    o_f, lse_f = _fox_fwd(proj, c, ct3)
    vt3 = _relvec_fwd(tbl).reshape(4, 2, VW)
    o_c, lse_c = _chk_fwd(proj, kvp, vt3, [gw_of('relay', [o_f])])
    gw = gw_of('done', [o_c])
    w_out, w_mq, w_mk, w_mv, w_mo, w1_t, w2 = (_wblk(gw, n) for n in ('w_out', 'w_mq', 'w_mk', 'w_mv', 'w_mo', 'w_ff1', 'w_ff2'))
    ycat, z, x1, h2, qm = _postmix_fwd(x, o_f, o_c, sm['g_fox_out'], sm['g_chk_out'], w_out,
                                       sm['g_mix_post'], sm['g_mem_pre'], w_mq)
    memn, km, vm = _memkv_fwd(mem, sm['g_mem_kv'], w_mk, w_mv)
    om, ym, x2, h3 = _mem_fwd(qm, x1, km, vm, w_mo, sm['g_mem_post'], sm['g_ff_pre'])

    gs = {}
    dx2, da, dy3, r, loss_acc, gs['g_ff_post'], gs['g_ff_pre'] = _ffn_step(h3, x2, tgt, w1_t, w2, sm['g_ff_post'],
                                                                         sm['g_ff_pre'])
    tok = on_grads('A', _wgrad_group("wgrad_ff", [(da, h3), (r, dy3)], 512), None)
    dx1, dym, dqm, dkm, dvm, gs['g_mem_post'], gs['g_mem_pre'] = _mem_bwd(
        dx2, ym, x1, qm, km, vm, w_mo, w_mq, sm['g_mem_post'], sm['g_mem_pre'], [tok])
    tok = on_grads('A halfway', None, [dx1])
    gs['g_mem_kv'] = _memkv_bwd(dkm, dvm, mem, w_mk, w_mv)
    dz, dof, doc, gs['g_mix_post'], gs['g_fox_out'], gs['g_chk_out'] = _postmix_bwd(
        dx1, z, o_f, o_c, w_out, sm['g_mix_post'], sm['g_fox_out'], sm['g_chk_out'], [tok])
    tok = on_grads('B', _wgrad_whole("wgrad_mem_out", [(ycat, dz), (h2, dqm), (memn, dkm), (memn, dvm), (om, dym)]), None)
    dq_f, dk_f, dv_f, dct, dcq = _fox_bwd(proj, c, ct3, o_f, lse_f, dof, [tok])
    tok = on_grads('B halfway', None, [dq_f])
    dq_c, dk_c, dv_c, gv = _chk_bwd(proj, kvp, vt3, o_c, lse_c, doc, [tok])
    gs['rel_bias'] = _relvec_bwd(gv.reshape(8, VW))[:, :257]
    dflog, db = _gate_bwd(dct.reshape(8, T), dcq, flog, b_pad)
    gs['b_fgt'] = db[0:1, :8]
    pieces = [dq_f, dk_f, dv_f, dflog, dq_c, dk_c, dv_c]
    on_grads('C', _wgrad_in(pieces, h1), None)
    tok = on_grads('C halfway', None, [gs['g_mem_kv']])
    grad_x, _, gs['g_mix_pre'] = _premix_bwd(dx1, x, pieces, win_t, sm['g_mix_pre'], [tok])
    return loss_acc, grad_x, gs


def kernel(x, mem, w_in, b_fgt, rel_bias, g_fox_out, g_chk_out, w_out, g_mix_pre, g_mix_post, g_mem_kv, w_mq, w_mk, w_mv, w_mo, g_mem_pre, g_mem_post, w_ff1, w_ff2, g_ff_pre, g_ff_post, loss_target, m_w_in, m_b_fgt, m_rel_bias, m_g_fox_out, m_g_chk_out, m_w_out, m_g_mix_pre, m_g_mix_post, m_g_mem_kv, m_w_mq, m_w_mk, m_w_mv, m_w_mo, m_g_mem_pre, m_g_mem_post, m_w_ff1, m_w_ff2, m_g_ff_pre, m_g_ff_post, v_w_in, v_b_fgt, v_rel_bias, v_g_fox_out, v_g_chk_out, v_w_out, v_g_mix_pre, v_g_mix_post, v_g_mem_kv, v_w_mq, v_w_mk, v_w_mv, v_w_mo, v_g_mem_pre, v_g_mem_post, v_w_ff1, v_w_ff2, v_g_ff_pre, v_g_ff_post):
    args = dict(locals())
    two_d = lambda a: a.reshape(a.shape[-2:])
    w = {n: two_d(args[n]) for n in WEIGHTS}
    m = {n: two_d(args['m_' + n]) for n in WEIGHTS}
    v = {n: two_d(args['v_' + n]) for n in WEIGHTS}

    sm = {n: w[n] for n in SMALL}
    shard_in = jnp.pad(w['w_in'].T, ((0, R_IN - N_IN), (0, 0))).astype(BF16)
    gathered_in, zero = _allgather(shard_in, "allgather_w_in")
    win_t = _in_rows_to_proj(gathered_in)
    shard_rest = (jnp.concatenate([w['w_ff1'].T, w['w_ff2'], w['w_out'], w['w_mq'], w['w_mk'], w['w_mv'], w['w_mo']],
                                  axis=0) + zero[0, 0]).astype(BF16)
    gather = {'first': _start_copies("allgather_rest_start", shard_rest, (8, R_REST, D), _gather_plan, 3)}

    def gw_of(stage, after):
        if stage == 'relay':
            gather['block'], land = _wait_copies("allgather_rest_wait", gather['first'], after, _gather_plan)
            gather['second'] = _start_inplace("allgather_rest_relay_start", land, _relay_plan, 4)
            return gather['second'][3]
        land = _wait_inplace("allgather_rest_relay_wait", gather['second'], after, _relay_plan)
        return _gather_forward(land, gather['block'])

    rs = {}

    def on_grads(stage, g, after):
        if stage.endswith('halfway'):
            rs[stage[0]].halfway(after)
            return rs[stage[0]].token
        rs[stage] = _ReduceScatter("rs_" + stage.lower(), g)
        return rs[stage].token

    loss_local, grad_x, gs = _local_grads(x[0], mem[0], loss_target[0], win_t, gw_of, sm, on_grads, [gather['first'][4]])
    grads, deltas, new_m, new_v = {}, {}, {}, {}

    def update(n, out):
        grads[n], deltas[n], new_m[n], new_v[n] = out

    own, got, order = rs['A'].finish([grad_x, rs['C'].token])
    update('w_ff1', _sum_adam(own, got, order, 0, w['w_ff1'], m['w_ff1'], v['w_ff1'], "adamw_w_ff1", transposed=True))
    update('w_ff2', _sum_adam(own, got, order, 512, w['w_ff2'], m['w_ff2'], v['w_ff2'], "adamw_w_ff2"))
    own, got, order = rs['B'].finish([grad_x, rs['C'].token])
    names_b = ('w_out', 'w_mq', 'w_mk', 'w_mv', 'w_mo')
    done = _sum_adam_rows(own, got, order, [w[n] for n in names_b], [m[n] for n in names_b], [v[n] for n in names_b],
                          "adamw_group_b")
    for k, n in enumerate(names_b):
        update(n, done[4 * k:4 * k + 4])

    own, got, order = rs['C'].finish([new_v[n] for n in BIG if n != 'w_in'])
    rows_of = lambda a: jnp.transpose(a, (2, 0, 1))
    done = _sum_adam(own, got, order, 0, rows_of(w_in), rows_of(m_w_in), rows_of(v_w_in), "adamw_w_in")
    update('w_in', [jnp.transpose(a, (1, 2, 0)) for a in done])

    gparts, _ = _allgather(_pack_small(gs, loss_local), "allgather_small_grads", [got])
    small = _adamw_small(gparts, [w[n] for n in SMALL], [m[n] for n in SMALL], [v[n] for n in SMALL])
    loss = small[0][0, 0]
    for t, n in enumerate(SMALL):
        update(n, small[1 + 4 * t:5 + 4 * t])

    out = [loss, grad_x[None]]
    for group in (grads, deltas, new_m, new_v):
        out += [group[n].reshape(args[n].shape) for n in WEIGHTS]
    return tuple(out)
```

```python
import jax
import jax.numpy as jnp
from jax import lax
from jax.experimental import pallas as pl
from jax.experimental.pallas import tpu as pltpu

F32 = jnp.float32
BF16 = jnp.bfloat16
MESH = pl.DeviceIdType.MESH

T = 2048
D = 1024
NMEM = 256
DFF = 4096
EPS = 1e-6
TM = 256
TM_WIDE = 512
TQ = 256
FQ = 512
HD = 64
SCALE = HD ** -0.5
MEM_HEADS = 4
MEM_HD = 256
MEM_SCALE = MEM_HD ** -0.5
NEG = -1e30
LEFT = 512
WIN = LEFT + TQ
VW = 1024
NREL_PAD = 384
PROJ = 3200
GATE0 = 1536
CHK0 = 1664
VMEM_BIG = 56 * 1024 * 1024

ADAM_LR = 0.001
ADAM_B1 = 0.9
ADAM_B2 = 0.999
ADAM_EPS = 1e-08
ADAM_WD = 0.01
ADAM_STEP = 10

N_IN = 385
R_IN = 400
R_REST = 1664
W_ROWS = {'w_ff1': (0, 512), 'w_ff2': (512, 512),
          'w_out': (1024, 128), 'w_mq': (1152, 128), 'w_mk': (1280, 128), 'w_mv': (1408, 128), 'w_mo': (1536, 128)}
SMALL_ROWS = 24
SMALL_SLOT = {'rel_bias': (0, 8, 0, 257), 'b_fgt': (8, 1, 0, 8), 'g_fox_out': (9, 1, 0, 512), 'g_chk_out': (9, 1, 512, 512),
              'g_mix_pre': (10, 1, 0, 1024), 'g_mix_post': (11, 1, 0, 1024), 'g_mem_kv': (12, 1, 0, 1024),
              'g_mem_pre': (13, 1, 0, 1024), 'g_mem_post': (14, 1, 0, 1024), 'g_ff_pre': (15, 1, 0, 1024),
              'g_ff_post': (16, 1, 0, 1024)}

WEIGHTS = ['w_in', 'b_fgt', 'rel_bias', 'g_fox_out', 'g_chk_out', 'w_out', 'g_mix_pre', 'g_mix_post', 'g_mem_kv',
           'w_mq', 'w_mk', 'w_mv', 'w_mo', 'g_mem_pre', 'g_mem_post', 'w_ff1', 'w_ff2', 'g_ff_pre', 'g_ff_post']
BIG = ['w_in', 'w_out', 'w_mq', 'w_mk', 'w_mv', 'w_mo', 'w_ff1', 'w_ff2']
SMALL = [n for n in WEIGHTS if n not in BIG]


def _pcall(body, **kw):
    return pl.pallas_call(body, **kw)


def _nn(a, b):
    return jnp.dot(a, b, preferred_element_type=F32)


def _nt(a, b):
    return lax.dot_general(a, b, (((1,), (1,)), ((), ())), preferred_element_type=F32)


def _tn(a, b):
    return lax.dot_general(a, b, (((0,), (0,)), ((), ())), preferred_element_type=F32)


def _w(ref):
    v = ref[...]
    return v if v.ndim == 2 else v.reshape(-1, v.shape[-1])


def _rstd(x):
    return lax.rsqrt(jnp.mean(x * x, axis=-1, keepdims=True) + EPS)


def _rms(x, g):
    return x * _rstd(x) * g


def _rms_bwd(x, g, dy):
    r = _rstd(x)
    xh = x * r
    dg = jnp.sum(dy * xh, axis=0, keepdims=True)
    dxh = dy * g
    dx = r * (dxh - xh * jnp.mean(dxh * xh, axis=-1, keepdims=True))
    return dx, dg


def _resident(a):
    if isinstance(a, tuple):
        _, shape, index = a
        return pl.BlockSpec(shape, lambda *_: index, pipeline_mode=pl.Buffered(1))
    return pl.BlockSpec(a.shape, lambda *_, nd=a.ndim: (0,) * nd, pipeline_mode=pl.Buffered(1))


def _wblk(gw, name):
    r0, rows = W_ROWS[name]
    return (gw, (8, rows, D), (0, r0 // rows, 0))


def _behind(body, n_in, after):
    if not after:
        return body
    return lambda *refs: body(*refs[:n_in], *refs[n_in + len(after):])


def _tok_call(body, name, tiled, full, outs_tiled, outs_acc=(), rows=T, tm=TM, vmem=None, after=(), scratch=()):
    in_specs = [pl.BlockSpec((tm, a.shape[1]), lambda i: (i, 0)) for a in tiled]
    in_specs += [_resident(a) for a in full] + [ANY_SPEC] * len(after)
    full = [a[0] if isinstance(a, tuple) else a for a in full] + list(after)
    body = _behind(body, len(tiled) + len(full) - len(after), after)
    out_shape = [jax.ShapeDtypeStruct((rows, c), dt) for c, dt in outs_tiled]
    out_shape += [jax.ShapeDtypeStruct(s, F32) for s in outs_acc]
    out_specs = [pl.BlockSpec((tm, c), lambda i: (i, 0)) for c, _ in outs_tiled]
    out_specs += [pl.BlockSpec(s, lambda i, nd=len(s): (0,) * nd) for s in outs_acc]
    return _pcall(
        body, name=name, grid=(rows // tm,), in_specs=in_specs, out_specs=out_specs, out_shape=out_shape,
        scratch_shapes=list(scratch),
        compiler_params=pltpu.CompilerParams(dimension_semantics=("arbitrary",), vmem_limit_bytes=vmem),
    )(*tiled, *full)


def _one_call(body, name, ins, outs, vmem=None):
    whole = lambda s: pl.BlockSpec(s, lambda i, nd=len(s): (0,) * nd)
    return _pcall(
        body, name=name, grid=(1,), in_specs=[_resident(a) for a in ins], out_specs=[whole(s) for s, _ in outs],
        out_shape=[jax.ShapeDtypeStruct(s, dt) for s, dt in outs],
        compiler_params=pltpu.CompilerParams(dimension_semantics=("arbitrary",), vmem_limit_bytes=vmem),
    )(*[a[0] if isinstance(a, tuple) else a for a in ins])


def _premix_fwd(x, g_pre, win_t, after=()):
    def body(x_ref, g_ref, w_ref, h_ref, proj_ref, flog_ref, kvp_ref):
        s = pl.program_id(0)

        @pl.when(s == 0)
        def _():
            kvp_ref[...] = jnp.zeros_like(kvp_ref)

        @pl.when(s > 0)
        def _():
            h = _rms(x_ref[...], g_ref[...]).astype(BF16)
            h_ref[...] = h
            p = _nt(h, w_ref[...])
            proj_ref[...] = p.astype(BF16)
            flog_ref[...] = p[:, GATE0:GATE0 + 128]
            kvp_ref[...] = p[:, CHK0 + 512:].astype(BF16)

    tile = lambda c: pl.BlockSpec((LEFT, c), lambda s: (jnp.maximum(s - 1, 0), 0))
    return _pcall(
        _behind(body, 3, after), name="premix_fwd", grid=(T // LEFT + 1,),
        in_specs=[tile(D), _resident(g_pre), _resident(win_t)] + [ANY_SPEC] * len(after),
        out_specs=[tile(D), tile(PROJ), tile(128), pl.BlockSpec((LEFT, 1024), lambda s: (s, 0))],
        out_shape=[jax.ShapeDtypeStruct((T, D), BF16), jax.ShapeDtypeStruct((T, PROJ), BF16),
                   jax.ShapeDtypeStruct((T, 128), F32), jax.ShapeDtypeStruct((T + LEFT, 1024), BF16)],
        compiler_params=pltpu.CompilerParams(dimension_semantics=("arbitrary",), vmem_limit_bytes=VMEM_BIG),
    )(x, g_pre, win_t, *after)


def _postmix_fwd(x, o_f, o_c, g_fo, g_co, w_out, g_post, g_mpre, w_mq):
    def body(x_ref, of_ref, oc_ref, gfo_ref, gco_ref, wo_ref, gp_ref, gm_ref, wq_ref,
             y_ref, z_ref, x1_ref, h2_ref, qm_ref):
        y_ref[:, :512] = _rms(of_ref[...], gfo_ref[...]).astype(BF16)
        y_ref[:, 512:] = _rms(oc_ref[...], gco_ref[...]).astype(BF16)
        z = _nn(y_ref[...], _w(wo_ref))
        z_ref[...] = z
        x1 = x_ref[...] + _rms(z, gp_ref[...])
        x1_ref[...] = x1
        h2 = _rms(x1, gm_ref[...]).astype(BF16)
        h2_ref[...] = h2
        qm_ref[...] = _nn(h2, _w(wq_ref)).astype(BF16)

    return _tok_call(body, "postmix_fwd", [x, o_f, o_c], [g_fo, g_co, w_out, g_post, g_mpre, w_mq],
                     [(D, BF16), (D, F32), (D, F32), (D, BF16), (D, BF16)], tm=TM_WIDE, vmem=VMEM_BIG)


def _memkv_fwd(mem, g_kv, w_mk, w_mv):
    def body(m_ref, g_ref, wk_ref, wv_ref, mn_ref, k_ref, v_ref):
        mn = _rms(m_ref[...], g_ref[...]).astype(BF16)
        mn_ref[...] = mn
        k_ref[...] = _nn(mn, _w(wk_ref)).astype(BF16)
        v_ref[...] = _nn(mn, _w(wv_ref)).astype(BF16)

    return _tok_call(body, "memkv_fwd", [mem], [g_kv, w_mk, w_mv],
                     [(D, BF16), (D, BF16), (D, BF16)], rows=NMEM, tm=NMEM, vmem=VMEM_BIG)


def _mem_fwd(qm, x1, km, vm, w_mo, g_post, g_fpre):
    def body(q_ref, x1_ref, k_ref, v_ref, wo_ref, gp_ref, gf_ref, om_ref, ym_ref, x2_ref, h3_ref):
        for h in range(MEM_HEADS):
            sl = slice(h * MEM_HD, (h + 1) * MEM_HD)
            s = _nt(q_ref[:, sl], k_ref[:, sl]) * MEM_SCALE
            p = jnp.exp(s - jnp.max(s, axis=-1, keepdims=True))
            p = p / jnp.sum(p, axis=-1, keepdims=True)
            om_ref[:, sl] = _nn(p.astype(BF16), v_ref[:, sl]).astype(BF16)
        ym = _nn(om_ref[...], _w(wo_ref))
        ym_ref[...] = ym
        x2 = x1_ref[...] + _rms(ym, gp_ref[...])
        x2_ref[...] = x2
        h3_ref[...] = _rms(x2, gf_ref[...]).astype(BF16)

    return _tok_call(body, "mem_fwd", [qm, x1], [km, vm, w_mo, g_post, g_fpre],
                     [(D, BF16), (D, F32), (D, F32), (D, BF16)], tm=TM_WIDE, vmem=VMEM_BIG)


def _tri(lower):
    r = lax.broadcasted_iota(jnp.int32, (128, 128), 0)
    c = lax.broadcasted_iota(jnp.int32, (128, 128), 1)
    return jnp.where(r >= c if lower else c >= r, 1.0, 0.0).astype(F32)


def _hdot(a, b):
    return jnp.dot(a, b, preferred_element_type=F32, precision=lax.Precision.HIGHEST)


def _gate_fwd(flog, b_pad):
    def body(f_ref, b_ref, c_ref):
        tri = _tri(True)

        def step(i, carry):
            rows = pl.ds(pl.multiple_of(i * 128, 128), 128)
            z = f_ref[rows, :] + b_ref[...]
            lf = jnp.minimum(z, 0.0) - jnp.log(1.0 + jnp.exp(-jnp.abs(z)))
            cb = _hdot(tri, lf) + carry
            c_ref[rows, :] = cb
            return cb[127:128, :]

        lax.fori_loop(0, T // 128, step, jnp.zeros((1, 128), F32))

    return _one_call(body, "gate_fwd", [flog, b_pad], [((T, 128), F32)])[0]


def _gate_bwd(dc, flog, b_pad):
    def body(dc_ref, f_ref, b_ref, df_ref, db_ref):
        tri = _tri(False)

        def step(j, carry):
            run, db = carry
            i = T // 128 - 1 - j
            rows = pl.ds(pl.multiple_of(i * 128, 128), 128)
            dcb = dc_ref[rows, :]
            rb = _hdot(tri, dcb) + run
            z = f_ref[rows, :] + b_ref[...]
            df = rb * (1.0 / (1.0 + jnp.exp(z)))
            df_ref[rows, :] = df.astype(BF16)
            return run + jnp.sum(dcb, axis=0, keepdims=True), db + jnp.sum(df, axis=0, keepdims=True)

        _, db = lax.fori_loop(0, T // 128, step, (jnp.zeros((1, 128), F32), jnp.zeros((1, 128), F32)))
        db_ref[...] = jnp.broadcast_to(db, (8, 128))

    return _one_call(body, "gate_bwd", [dc, flog, b_pad], [((T, 128), BF16), ((8, 128), F32)])


def _lane_lo(rows=TQ):
    return lax.broadcasted_iota(jnp.int32, (rows, 128), 1) < HD


def _half(v, lo, a, scale=None):
    keep = lo if a == 0 else jnp.logical_not(lo)
    v = v.astype(F32) if scale is None else v.astype(F32) * scale
    return jnp.where(keep, v, 0.0).astype(BF16)


def _fox_specs():
    return [pl.BlockSpec((FQ, 128), lambda h, i: (i, h)),
            pl.BlockSpec((T, 128), lambda h, i: (0, 4 + h)),
            pl.BlockSpec((T, 128), lambda h, i: (0, 8 + h))]


def _lane_pick(x, at):
    lane = lax.broadcasted_iota(jnp.int32, x.shape, 1)
    return jnp.sum(jnp.where(lane == at, x, 0.0), axis=-1, keepdims=True)


def _fox_fwd(proj, c, ct3):
    def body(q_ref, k_ref, v_ref, c_ref, ct_ref, o_ref, l_ref):
        i = pl.program_id(1)
        lo = _lane_lo(FQ)
        causal = lax.broadcasted_iota(jnp.int32, (FQ, FQ), 1) <= lax.broadcasted_iota(jnp.int32, (FQ, FQ), 0)
        q = q_ref[...]
        qs = [_half(q, lo, a, SCALE) for a in range(2)]
        cqs = [_lane_pick(c_ref[...], 2 * pl.program_id(0) + a) for a in range(2)]

        def tile(off, carry, diagonal):
            kblk = k_ref[pl.ds(off, FQ), :]
            vblk = v_ref[pl.ds(off, FQ), :]
            new = []
            for a in range(2):
                m, l, acc = carry[a]
                s = _nt(qs[a], kblk) + (cqs[a] - ct_ref[a:a + 1, pl.ds(off, FQ)])
                if diagonal:
                    s = jnp.where(causal, s, NEG)
                m2 = jnp.maximum(m, jnp.max(s, axis=-1, keepdims=True))
                p = jnp.exp(s - m2)
                alpha = jnp.exp(m - m2)
                new.append((m2, alpha * l + jnp.sum(p, axis=-1, keepdims=True),
                            alpha * acc + _nn(p.astype(BF16), vblk)))
            return tuple(new)

        init = (jnp.full((FQ, 1), NEG, F32), jnp.zeros((FQ, 1), F32), jnp.zeros((FQ, 128), F32))
        carry = lax.fori_loop(0, i, lambda kb, c: tile(pl.multiple_of(kb * FQ, FQ), c, False), (init, init))
        carry = tile(pl.multiple_of(i * FQ, FQ), carry, True)
        outs = []
        for a in range(2):
            m, l, acc = carry[a]
            outs.append(acc / l)
            l_ref[:, 128 * a:128 * a + 128] = jnp.broadcast_to(m + jnp.log(l), (FQ, 128))
        o_ref[...] = jnp.where(lo, outs[0], outs[1])

    return _pcall(
        body, name="fox_fwd", grid=(4, T // FQ),
        in_specs=_fox_specs() + [pl.BlockSpec((FQ, 128), lambda h, i: (i, 0)),
                                 pl.BlockSpec((None, 2, T), lambda h, i: (h, 0, 0))],
        out_specs=[pl.BlockSpec((FQ, 128), lambda h, i: (i, h)), pl.BlockSpec((FQ, 256), lambda h, i: (i, h))],
        out_shape=[jax.ShapeDtypeStruct((T, 512), F32), jax.ShapeDtypeStruct((T, 1024), F32)],
        compiler_params=pltpu.CompilerParams(dimension_semantics=("arbitrary", "arbitrary"), vmem_limit_bytes=VMEM_BIG),
    )(proj, proj, proj, c, ct3)


def _fox_bwd(proj, c, ct3, o, lse, do, after=()):
    def body(q_ref, k_ref, v_ref, c_ref, ct_ref, o_ref, l_ref, do_ref, dq_ref, dkb_ref, dvb_ref, dct_ref, dcq_ref,
             dk_ref, dv_ref):
        i = pl.program_id(1)

        @pl.when(i == 0)
        def _():
            dk_ref[...] = jnp.zeros_like(dk_ref)
            dv_ref[...] = jnp.zeros_like(dv_ref)
            dct_ref[...] = jnp.zeros_like(dct_ref)

        lo = _lane_lo(FQ)
        causal = lax.broadcasted_iota(jnp.int32, (FQ, FQ), 1) <= lax.broadcasted_iota(jnp.int32, (FQ, FQ), 0)
        q = q_ref[...]
        do_v = do_ref[...]
        prod = do_v * o_ref[...]
        qs = [_half(q, lo, a, SCALE) for a in range(2)]
        dos = [_half(do_v, lo, a) for a in range(2)]
        deltas = [jnp.sum(jnp.where(lo if a == 0 else jnp.logical_not(lo), prod, 0.0), axis=-1, keepdims=True)
                  for a in range(2)]
        cqs = [_lane_pick(c_ref[...], 2 * pl.program_id(0) + a) for a in range(2)]
        las = [l_ref[:, 128 * a:128 * a + 1] for a in range(2)]

        def tile(off, carry, diagonal):
            kblk = k_ref[pl.ds(off, FQ), :]
            vblk = v_ref[pl.ds(off, FQ), :]
            new = []
            dk = jnp.zeros((128, FQ), F32)
            dv = jnp.zeros((128, FQ), F32)
            for a in range(2):
                dq_acc, rs = carry[a]
                s = _nt(qs[a], kblk) + (cqs[a] - ct_ref[a:a + 1, pl.ds(off, FQ)])
                if diagonal:
                    s = jnp.where(causal, s, NEG)
                p = jnp.exp(s - las[a])
                ds = p * (_nt(dos[a], vblk) - deltas[a])
                dsb = ds.astype(BF16)
                dk = dk + _tn(qs[a], dsb)
                dv = dv + _tn(dos[a], p.astype(BF16))
                dct_ref[a:a + 1, pl.ds(off, FQ)] -= jnp.sum(ds, axis=0, keepdims=True)
                new.append((dq_acc + _nn(dsb, kblk), rs + jnp.sum(ds, axis=-1, keepdims=True)))
            dk_ref[:, pl.ds(off, FQ)] += dk
            dv_ref[:, pl.ds(off, FQ)] += dv
            return tuple(new)

        init = (jnp.zeros((FQ, 128), F32), jnp.zeros((FQ, 1), F32))
        carry = lax.fori_loop(0, i, lambda kb, c: tile(pl.multiple_of(kb * FQ, FQ), c, False), (init, init))
        carry = tile(pl.multiple_of(i * FQ, FQ), carry, True)
        lane = lax.broadcasted_iota(jnp.int32, (FQ, 128), 1)
        dcq_ref[...] = jnp.where(lane == 0, carry[0][1], jnp.where(lane == 1, carry[1][1], 0.0))
        dq_ref[...] = (jnp.where(lo, carry[0][0], carry[1][0]) * SCALE).astype(BF16)

        @pl.when(i == T // FQ - 1)
        def _():
            dkb_ref[...] = dk_ref[...].T.astype(BF16)
            dvb_ref[...] = dv_ref[...].T.astype(BF16)

    blk = pl.BlockSpec((FQ, 128), lambda h, i: (i, h))
    wide = pl.BlockSpec((FQ, 256), lambda h, i: (i, h))
    rows = pl.BlockSpec((None, 2, T), lambda h, i: (h, 0, 0))
    col = pl.BlockSpec((T, 128), lambda h, i: (0, h))
    return _pcall(
        _behind(body, 8, after), name="fox_bwd", grid=(4, T // FQ),
        in_specs=_fox_specs() + [pl.BlockSpec((FQ, 128), lambda h, i: (i, 0)), rows, blk, wide, blk] + [ANY_SPEC] * len(after),
        out_specs=[blk, col, col, rows, pl.BlockSpec((None, FQ, 128), lambda h, i: (h, i, 0))],
        out_shape=[jax.ShapeDtypeStruct((T, 512), BF16), jax.ShapeDtypeStruct((T, 512), BF16),
                   jax.ShapeDtypeStruct((T, 512), BF16), jax.ShapeDtypeStruct((4, 2, T), F32),
                   jax.ShapeDtypeStruct((4, T, 128), F32)],
        scratch_shapes=[pltpu.VMEM((128, T), F32), pltpu.VMEM((128, T), F32)],
        compiler_params=pltpu.CompilerParams(dimension_semantics=("arbitrary", "arbitrary"), vmem_limit_bytes=VMEM_BIG),
    )(proj, proj, proj, c, ct3, o, lse, do, *after)


def _rel_onehot():
    ridx = lax.broadcasted_iota(jnp.int32, (NREL_PAD, VW), 0)
    j = lax.broadcasted_iota(jnp.int32, (NREL_PAD, VW), 1)
    return jnp.where(ridx == jnp.clip(TQ + LEFT - 1 - j, -128, 128) + 128, 1.0, 0.0).astype(F32)


def _relvec_fwd(tbl):
    def body(t_ref, v_ref):
        v_ref[...] = _hdot(t_ref[...], _rel_onehot())

    return _one_call(body, "relvec_fwd", [tbl], [((8, VW), F32)])[0]


def _relvec_bwd(gv):
    def body(g_ref, t_ref):
        t_ref[...] = lax.dot_general(g_ref[...], _rel_onehot(), (((1,), (1,)), ((), ())),
                                     preferred_element_type=F32, precision=lax.Precision.HIGHEST)

    return _one_call(body, "relvec_bwd", [gv], [((8, NREL_PAD), F32)])[0]


def _chk_bias(vt_ref, a, hidden):
    vb = jnp.broadcast_to(vt_ref[a:a + 1, :], (TQ, VW))
    y = pltpu.roll(vb, VW - (TQ - 1), 1, stride=1, stride_axis=0)[:, :WIN]
    cr = lax.broadcasted_iota(jnp.int32, (TQ, WIN), 0) // 64
    m = lax.broadcasted_iota(jnp.int32, (TQ, WIN), 1)
    return jnp.where((m // 64 >= cr) & (m // 64 <= cr + 8) & (m >= hidden), y, NEG)


def _chk_specs():
    return [pl.BlockSpec((TQ, 128), lambda h, i: (i, CHK0 // 128 + h)),
            pl.BlockSpec((T + LEFT, 128), lambda h, i: (0, h)),
            pl.BlockSpec((T + LEFT, 128), lambda h, i: (0, 4 + h)),
            pl.BlockSpec((None, 2, VW), lambda h, i: (h, 0, 0))]


def _chk_fwd(proj, kvp, vt3, after=()):
    def body(q_ref, k_ref, v_ref, vt_ref, o_ref, l_ref, bias_ref):
        i = pl.program_id(1)

        @pl.when(i == 0)
        def _():
            for first in range(3):
                for a in range(2):
                    bias_ref[first, a] = _chk_bias(vt_ref, a, max(LEFT - first * TQ, 0))

        lo = _lane_lo()
        off = pl.multiple_of(i * TQ, TQ)
        kw = k_ref[pl.ds(off, WIN), :]
        vw = v_ref[pl.ds(off, WIN), :]
        bias_at = jnp.minimum(i, 2)
        q = q_ref[...]
        outs = []
        for a in range(2):
            s = _nt(_half(q, lo, a, SCALE), kw) + bias_ref[bias_at, a]
            m = jnp.max(s, axis=-1, keepdims=True)
            p = jnp.exp(s - m)
            l = jnp.sum(p, axis=-1, keepdims=True)
            outs.append(_nn(p.astype(BF16), vw) / l)
            l_ref[:, 128 * a:128 * a + 128] = jnp.broadcast_to(m + jnp.log(l), (TQ, 128))
        o_ref[...] = jnp.where(lo, outs[0], outs[1])

    return _pcall(
        _behind(body, 4, after), name="chk_fwd", grid=(4, T // TQ), in_specs=_chk_specs() + [ANY_SPEC] * len(after),
        out_specs=[pl.BlockSpec((TQ, 128), lambda h, i: (i, h)), pl.BlockSpec((TQ, 256), lambda h, i: (i, h))],
        out_shape=[jax.ShapeDtypeStruct((T, 512), F32), jax.ShapeDtypeStruct((T, 1024), F32)],
        scratch_shapes=[pltpu.VMEM((3, 2, TQ, WIN), F32)],
        compiler_params=pltpu.CompilerParams(dimension_semantics=("arbitrary", "arbitrary")),
    )(proj, kvp, kvp, vt3, *after)


def _chk_bwd(proj, kvp, vt3, o, lse, do, after=()):
    nq = T // TQ

    def body(q_ref, k_ref, v_ref, vt_ref, o_ref, l_ref, do_ref, dq_ref, dkb_ref, dvb_ref, gv_ref, bias_ref, dsum_ref,
             dk_ref, dv_ref):
        i = pl.program_id(1)

        @pl.when(i == 0)
        def _():
            for first in range(3):
                for a in range(2):
                    bias_ref[first, a] = _chk_bias(vt_ref, a, max(LEFT - first * TQ, 0))
            dsum_ref[...] = jnp.zeros_like(dsum_ref)
            dk_ref[...] = jnp.zeros_like(dk_ref)
            dv_ref[...] = jnp.zeros_like(dv_ref)

        lo = _lane_lo()
        off = pl.multiple_of(i * TQ, TQ)
        kw = k_ref[pl.ds(off, WIN), :]
        vw = v_ref[pl.ds(off, WIN), :]
        bias_at = jnp.minimum(i, 2)
        q = q_ref[...]
        do_v = do_ref[...]
        prod = do_v * o_ref[...]
        dqs = []
        for a in range(2):
            keep = lo if a == 0 else jnp.logical_not(lo)
            qa = _half(q, lo, a, SCALE)
            doa = _half(do_v, lo, a)
            delta = jnp.sum(jnp.where(keep, prod, 0.0), axis=-1, keepdims=True)
            s = _nt(qa, kw) + bias_ref[bias_at, a]
            p = jnp.exp(s - l_ref[:, 128 * a:128 * a + 1])
            ds = p * (_nt(doa, vw) - delta)
            dsum_ref[a] += ds
            dsb = ds.astype(BF16)
            dk_ref[:, pl.ds(off, WIN)] += _tn(qa, dsb)
            dv_ref[:, pl.ds(off, WIN)] += _tn(doa, p.astype(BF16))
            dqs.append(_nn(dsb, kw))
        dq_ref[...] = (jnp.where(lo, dqs[0], dqs[1]) * SCALE).astype(BF16)

        @pl.when(i == nq - 1)
        def _():
            dkb_ref[...] = dk_ref[:, LEFT:].T.astype(BF16)
            dvb_ref[...] = dv_ref[:, LEFT:].T.astype(BF16)
            rr = lax.broadcasted_iota(jnp.int32, (TQ, TQ), 0)
            cc = lax.broadcasted_iota(jnp.int32, (TQ, TQ), 1)
            flip = jnp.where(rr + cc == TQ - 1, 1.0, 0.0).astype(F32)
            for a in range(2):
                dpad = jnp.concatenate([dsum_ref[a], jnp.zeros((TQ, VW - WIN), F32)], axis=1)
                z = pltpu.roll(_hdot(flip, dpad), 0, 1, stride=1, stride_axis=0)
                gv_ref[a:a + 1, :] = jnp.sum(z, axis=0, keepdims=True)

    blk = pl.BlockSpec((TQ, 128), lambda h, i: (i, h))
    wide = pl.BlockSpec((TQ, 256), lambda h, i: (i, h))
    col = pl.BlockSpec((T, 128), lambda h, i: (0, h))
    return _pcall(
        _behind(body, 7, after), name="chk_bwd", grid=(4, nq), in_specs=_chk_specs() + [blk, wide, blk] + [ANY_SPEC] * len(after),
        out_specs=[blk, col, col, pl.BlockSpec((None, 2, VW), lambda h, i: (h, 0, 0))],
        out_shape=[jax.ShapeDtypeStruct((T, 512), BF16), jax.ShapeDtypeStruct((T, 512), BF16),
                   jax.ShapeDtypeStruct((T, 512), BF16), jax.ShapeDtypeStruct((4, 2, VW), F32)],
        scratch_shapes=[pltpu.VMEM((3, 2, TQ, WIN), F32), pltpu.VMEM((2, TQ, WIN), F32),
                        pltpu.VMEM((128, T + LEFT), F32), pltpu.VMEM((128, T + LEFT), F32)],
        compiler_params=pltpu.CompilerParams(dimension_semantics=("arbitrary", "arbitrary")),
    )(proj, kvp, kvp, vt3, o, lse, do, *after)


def _zero_at_start(*refs):
    @pl.when(pl.program_id(0) == 0)
    def _():
        for r in refs:
            r[...] = jnp.zeros_like(r)


def _ffn_step(h3, x2, tgt, w1_t, w2, g_post, g_pre):
    def body(h_ref, x2_ref, t_ref, w1_ref, w2_ref, gp_ref, gf_ref, dx2_ref, da_ref, dy_ref, r_ref, loss_ref, dgp_ref, dgf_ref):
        _zero_at_start(loss_ref, dgp_ref, dgf_ref)
        w1, w2v = _w(w1_ref), _w(w2_ref)
        ra = jnp.maximum(_nt(h_ref[...], w1), 0.0)
        r = jnp.square(ra).astype(BF16)
        r_ref[...] = r
        y = _nn(r, w2v)
        x2v = x2_ref[...]
        e = x2v + _rms(y, gp_ref[...]) - t_ref[...]
        loss_ref[...] += 0.5 * jnp.sum(jnp.sum(e * e, axis=-1, keepdims=True) * (1.0 / D))
        dx3 = e * (1.0 / D)
        dy, dgp = _rms_bwd(y, gp_ref[...], dx3)
        dgp_ref[...] += dgp
        dyb = dy.astype(BF16)
        dy_ref[...] = dyb
        da = (_nt(dyb, w2v) * (2.0 * ra)).astype(BF16)
        da_ref[...] = da
        dh, dgf = _rms_bwd(x2v, gf_ref[...], _nn(da, w1))
        dgf_ref[...] += dgf
        dx2_ref[...] = dx3 + dh

    return _tok_call(body, "ffn_step", [h3, x2, tgt], [w1_t, w2, g_post, g_pre],
                     [(D, F32), (DFF, BF16), (D, BF16), (DFF, BF16)], [(8, 128), (1, D), (1, D)], vmem=VMEM_BIG)


def _mem_bwd(dx2, ym, x1, qm, km, vm, w_mo, w_mq, g_post, g_pre, after=()):
    def body(dx2_ref, ym_ref, x1_ref, q_ref, k_ref, v_ref, wo_ref, wq_ref, gp_ref, gm_ref,
             dx1_ref, dym_ref, dq_ref, dk_ref, dv_ref, dgp_ref, dgm_ref, dom_ref):
        _zero_at_start(dk_ref, dv_ref, dgp_ref, dgm_ref)
        dx2_v = dx2_ref[...]
        dym, dgp = _rms_bwd(ym_ref[...], gp_ref[...], dx2_v)
        dgp_ref[...] += dgp
        dymb = dym.astype(BF16)
        dym_ref[...] = dymb
        dom_ref[...] = _nt(dymb, _w(wo_ref)).astype(BF16)
        for h in range(MEM_HEADS):
            sl = slice(h * MEM_HD, (h + 1) * MEM_HD)
            qh, kh, doh = q_ref[:, sl], k_ref[:, sl], dom_ref[:, sl]
            s = _nt(qh, kh) * MEM_SCALE
            p = jnp.exp(s - jnp.max(s, axis=-1, keepdims=True))
            p = p / jnp.sum(p, axis=-1, keepdims=True)
            dp = _nt(doh, v_ref[:, sl])
            ds = (p * (dp - jnp.sum(p * dp, axis=-1, keepdims=True))).astype(BF16)
            dq_ref[:, sl] = (_nn(ds, kh) * MEM_SCALE).astype(BF16)
            dk_ref[:, sl] += _tn(ds, qh) * MEM_SCALE
            dv_ref[:, sl] += _tn(p.astype(BF16), doh)
        dh, dgm = _rms_bwd(x1_ref[...], gm_ref[...], _nt(dq_ref[...], _w(wq_ref)))
        dgm_ref[...] += dgm
        dx1_ref[...] = dx2_v + dh

    tiled = pl.BlockSpec((TM_WIDE, D), lambda i: (i, 0))
    in_specs = [tiled] * 4 + [_resident(a) for a in (km, vm, w_mo, w_mq, g_post, g_pre)] + [ANY_SPEC] * len(after)
    w_mo, w_mq = w_mo[0], w_mq[0]
    kv = pl.BlockSpec((NMEM, D), lambda i: (0, 0))
    vec = pl.BlockSpec((1, D), lambda i: (0, 0))
    return _pcall(
        _behind(body, 10, after), name="mem_bwd", grid=(T // TM_WIDE,), in_specs=in_specs,
        out_specs=[tiled, tiled, tiled, kv, kv, vec, vec],
        out_shape=[jax.ShapeDtypeStruct((T, D), F32), jax.ShapeDtypeStruct((T, D), BF16),
                   jax.ShapeDtypeStruct((T, D), BF16), jax.ShapeDtypeStruct((NMEM, D), F32),
                   jax.ShapeDtypeStruct((NMEM, D), F32), jax.ShapeDtypeStruct((1, D), F32),
                   jax.ShapeDtypeStruct((1, D), F32)],
        scratch_shapes=[pltpu.VMEM((TM_WIDE, D), BF16)],
        compiler_params=pltpu.CompilerParams(dimension_semantics=("arbitrary",), vmem_limit_bytes=VMEM_BIG),
    )(dx2, ym, x1, qm, km, vm, w_mo, w_mq, g_post, g_pre, *after)


def _memkv_bwd(dkm, dvm, mem, w_mk, w_mv):
    def body(dk_ref, dv_ref, m_ref, wk_ref, wv_ref, dg_ref):
        dmn = _nt(dk_ref[...].astype(BF16), _w(wk_ref)) + _nt(dv_ref[...].astype(BF16), _w(wv_ref))
        mv = m_ref[...]
        dg_ref[...] = jnp.sum(dmn * (mv * _rstd(mv)), axis=0, keepdims=True)

    return _one_call(body, "memkv_bwd", [dkm, dvm, mem, w_mk, w_mv], [((1, D), F32)], vmem=VMEM_BIG)[0]


def _postmix_bwd(dx1, z, o_f, o_c, w_out, g_post, g_fo, g_co, after=()):
    def body(dx1_ref, z_ref, of_ref, oc_ref, wo_ref, gp_ref, gfo_ref, gco_ref,
             dz_ref, dof_ref, doc_ref, dgp_ref, dgfo_ref, dgco_ref):
        _zero_at_start(dgp_ref, dgfo_ref, dgco_ref)
        dz, dgp = _rms_bwd(z_ref[...], gp_ref[...], dx1_ref[...])
        dgp_ref[...] += dgp
        dzb = dz.astype(BF16)
        dz_ref[...] = dzb
        dy = _nt(dzb, _w(wo_ref))
        dof, dgfo = _rms_bwd(of_ref[...], gfo_ref[...], dy[:, :512])
        doc, dgco = _rms_bwd(oc_ref[...], gco_ref[...], dy[:, 512:])
        dof_ref[...] = dof
        doc_ref[...] = doc
        dgfo_ref[...] += dgfo
        dgco_ref[...] += dgco

    return _tok_call(body, "postmix_bwd", [dx1, z, o_f, o_c], [w_out, g_post, g_fo, g_co],
                     [(D, BF16), (512, F32), (512, F32)], [(1, D), (1, 512), (1, 512)], tm=TM_WIDE, vmem=VMEM_BIG,
                     after=after)


def _premix_bwd(dx1, x, pieces, win_t, g_pre, after=()):
    def body(dx1_ref, x_ref, *refs):
        piece_refs, (w_ref, g_ref, dx_ref, dg_ref, dp_ref) = refs[:len(pieces)], refs[len(pieces):]
        _zero_at_start(dg_ref)
        col = 0
        for p in piece_refs:
            dp_ref[:, col:col + p.shape[1]] = p[...]
            col += p.shape[1]
        dh, dg = _rms_bwd(x_ref[...], g_ref[...], _nn(dp_ref[...], w_ref[...]))
        dg_ref[...] += dg
        dx_ref[...] = dx1_ref[...] + dh

    return _tok_call(body, "premix_bwd", [dx1, x] + list(pieces), [win_t, g_pre], [(D, F32)], [(1, D)],
                     tm=TM_WIDE, vmem=VMEM_BIG, after=after, scratch=[pltpu.VMEM((TM_WIDE, PROJ), BF16)])


def _wgrad_group(name, pairs, rows):
    def body(*refs):
        o_ref = refs[-1]
        for k in range(len(pairs)):
            g = _tn(refs[2 * k][...].astype(BF16), refs[2 * k + 1][...].astype(BF16))
            o_ref[k * rows:(k + 1) * rows, :] = g.astype(BF16)

    in_specs, ops = [], []
    for a, b in pairs:
        in_specs += [pl.BlockSpec((a.shape[0], rows), lambda j: (0, j)), _resident(b)]
        ops += [a, b]
    return _pcall(
        body, name=name, grid=(8,), in_specs=in_specs,
        out_specs=pl.BlockSpec((None, len(pairs) * rows, D), lambda j: (j, 0, 0)),
        out_shape=jax.ShapeDtypeStruct((8, len(pairs) * rows, D), BF16),
        compiler_params=pltpu.CompilerParams(dimension_semantics=("arbitrary",), vmem_limit_bytes=VMEM_BIG),
    )(*ops)


def _wgrad_whole(name, pairs):
    n = len(pairs)
    ops = [x for pair in pairs for x in pair]
    rows = pairs[0][0].shape[1] // 8

    def body(*refs):
        hbm, o_ref, bufs, sem = refs[:2 * n], refs[2 * n], refs[2 * n + 1:4 * n + 1], refs[4 * n + 1]
        k = pl.program_id(0)
        copies = [pltpu.make_async_copy(hbm[t], bufs[t], sem.at[t]) for t in range(2 * n)]

        @pl.when(k == 0)
        def _():
            for copy in copies:
                copy.start()

        for t in range(n):
            @pl.when(k == t)
            def _(t=t):
                copies[2 * t].wait()
                copies[2 * t + 1].wait()
                g = _tn(bufs[2 * t][...].astype(BF16), bufs[2 * t + 1][...].astype(BF16))
                o_ref[...] = g.reshape(8, rows, D).astype(BF16)

    return _pcall(
        body, name=name, grid=(n,), in_specs=[ANY_SPEC] * (2 * n),
        out_specs=pl.BlockSpec((8, rows, D), lambda k: (0, k, 0)),
        out_shape=jax.ShapeDtypeStruct((8, n * rows, D), BF16),
        scratch_shapes=[pltpu.VMEM(x.shape, x.dtype) for x in ops] + [pltpu.SemaphoreType.DMA((2 * n,))],
        compiler_params=pltpu.CompilerParams(dimension_semantics=("arbitrary",), vmem_limit_bytes=VMEM_BIG),
    )(*ops)


def _adam_math(w, g, m, v):
    m2 = ADAM_B1 * m + (1.0 - ADAM_B1) * g
    v2 = ADAM_B2 * v + (1.0 - ADAM_B2) * jnp.square(g)
    m_hat = m2 / (1.0 - ADAM_B1 ** ADAM_STEP)
    v_hat = v2 / (1.0 - ADAM_B2 ** ADAM_STEP)
    delta = -ADAM_LR * (m_hat / (jnp.sqrt(v_hat) + ADAM_EPS) + ADAM_WD * w)
    return delta, m2, v2


def _adamw_small(gparts, ws, ms, vs):
    n = len(SMALL)

    def body(g_ref, *refs):
        w_refs, m_refs, v_refs = refs[:n], refs[n:2 * n], refs[2 * n:3 * n]
        outs, sum_ref = refs[3 * n:-1], refs[-1]
        g = g_ref[0]
        for k in range(1, 8):
            g = g + g_ref[k]
        sum_ref[...] = g
        outs[0][...] = sum_ref[17:18, 0:128]
        for t, name in enumerate(SMALL):
            r0, nr, c0, nc = SMALL_SLOT[name]
            gt = sum_ref[r0:r0 + nr, c0:c0 + nc]
            out = (gt,) + _adam_math(w_refs[t][...], gt, m_refs[t][...], v_refs[t][...])
            for o_ref, val in zip(outs[1 + 4 * t:5 + 4 * t], out):
                o_ref[...] = val

    whole = lambda s: pl.BlockSpec(s, lambda i, nd=len(s): (0,) * nd)
    ins = [gparts] + list(ws) + list(ms) + list(vs)
    out_shapes = [(1, 128)] + [a.shape for a in ws for _ in range(4)]
    return _pcall(
        body, name="adamw_small", grid=(1,), in_specs=[whole(a.shape) for a in ins],
        out_specs=[whole(s) for s in out_shapes], out_shape=[jax.ShapeDtypeStruct(s, F32) for s in out_shapes],
        scratch_shapes=[pltpu.VMEM((SMALL_ROWS, D), F32)],
        compiler_params=pltpu.CompilerParams(dimension_semantics=("arbitrary",)),
    )(*ins)


def _row_tile(rows):
    return next(t for t in (512, 400, 320) if rows % t == 0)


def _add_halves(g4, theirs, core, name):
    rows = g4.shape[2]
    tr = _row_tile(rows)

    def body(c_ref, a_ref, b_ref, o_ref):
        o_ref[...] = (a_ref[...].astype(F32) + b_ref[...].astype(F32)).astype(BF16)

    grid_spec = pltpu.PrefetchScalarGridSpec(
        num_scalar_prefetch=1, grid=(4, rows // tr),
        in_specs=[pl.BlockSpec((None, None, tr, D), lambda j, i, c: (j, c[0], i, 0)),
                  pl.BlockSpec((None, None, tr, D), lambda j, i, c: (j, 0, i, 0))],
        out_specs=pl.BlockSpec((None, tr, D), lambda j, i, c: (j, i, 0)))
    return _pcall(
        body, name=name, grid_spec=grid_spec, out_shape=jax.ShapeDtypeStruct((4, rows, D), BF16),
        compiler_params=pltpu.CompilerParams(dimension_semantics=("arbitrary", "arbitrary")),
    )(core, g4, theirs)


def _sum_adam(own, got, order, r0, w, m, v, name, transposed=False):
    ragged = w.ndim == 3
    n = w.shape[1] if transposed else w.shape[0]
    tr = n if ragged else min(n, 256)
    rows = own.shape[1] if ragged else tr
    at = (slice(None), 0, slice(None)) if ragged else Ellipsis

    def body(o_ref, a_ref, b_ref, c_ref, d_ref, w_ref, m_ref, v_ref, *out_refs):
        f = lambda r: r[0:tr, :].astype(F32)
        g = ((f(a_ref) + f(b_ref)) + f(c_ref)) + f(d_ref)
        g = g.T if transposed else g
        for ref, val in zip(out_refs, (g,) + _adam_math(w_ref[at], g, m_ref[at], v_ref[at])):
            ref[at] = val

    slot = lambda k: pl.BlockSpec((None, rows, D), lambda i, o: (o[k], r0 // rows + i, 0))
    if ragged:
        wspec = pl.BlockSpec((n, 1, D), lambda i, o: (0, 0, 0))
    else:
        wspec = pl.BlockSpec((D, tr), lambda i, o: (0, i)) if transposed else pl.BlockSpec((tr, D), lambda i, o: (i, 0))
    grid_spec = pltpu.PrefetchScalarGridSpec(
        num_scalar_prefetch=1, grid=(n // tr,), in_specs=[slot(0), slot(1), slot(2), slot(3), wspec, wspec, wspec],
        out_specs=[wspec] * 4)
    return _pcall(
        body, name=name, grid_spec=grid_spec, out_shape=[jax.ShapeDtypeStruct(w.shape, F32)] * 4,
        compiler_params=pltpu.CompilerParams(dimension_semantics=("arbitrary",)),
    )(order, own, got, got, got, w, m, v)


def _sum_adam_rows(own, got, order, ws, ms, vs, name):
    n, rows = len(ws), ws[0].shape[0]

    def body(o_ref, a_ref, b_ref, c_ref, d_ref, *refs):
        ins, outs = refs[:3 * n], refs[3 * n:]
        for t in range(n):
            r = slice(t * rows, (t + 1) * rows)
            f = lambda ref: ref[r, :].astype(F32)
            g = ((f(a_ref) + f(b_ref)) + f(c_ref)) + f(d_ref)
            out = (g,) + _adam_math(ins[t][...], g, ins[n + t][...], ins[2 * n + t][...])
            for o, val in zip(outs[4 * t:4 * t + 4], out):
                o[...] = val

    slot = lambda k: pl.BlockSpec((None, n * rows, D), lambda i, o: (o[k], 0, 0), pipeline_mode=pl.Buffered(1))
    wspec = pl.BlockSpec((rows, D), lambda i, o: (0, 0), pipeline_mode=pl.Buffered(1))
    grid_spec = pltpu.PrefetchScalarGridSpec(
        num_scalar_prefetch=1, grid=(1,), in_specs=[slot(0), slot(1), slot(2), slot(3)] + [wspec] * (3 * n),
        out_specs=[pl.BlockSpec((rows, D), lambda i, o: (0, 0))] * (4 * n))
    return _pcall(
        body, name=name, grid_spec=grid_spec, out_shape=[jax.ShapeDtypeStruct((rows, D), F32)] * (4 * n),
        compiler_params=pltpu.CompilerParams(dimension_semantics=("arbitrary",), vmem_limit_bytes=VMEM_BIG),
    )(order, own, got, got, got, *ws, *ms, *vs)


def _place():
    return lax.axis_index("x"), lax.axis_index("y"), lax.axis_index("c")


def _allgather(block, name, after=()):
    rows = block.shape[0]
    split = (rows // 2 + 15) // 16 * 16

    def body(x_ref, out_ref, token, send_sems, recv_sems, local_sem):
        token[...] = jnp.zeros_like(token)
        x, y, c = _place()
        me, sib = (x, y, c), (x, y, 1 - c)
        xn, yn, dg = (1 - x, y), (x, 1 - y), (1 - x, 1 - y)
        lo, hi = pl.ds(0, split), pl.ds(split, rows - split)

        def copy(k, blk, to, part=None, src=None):
            index = 4 * blk[0] + 2 * blk[1] + blk[2]
            view = out_ref.at[index] if part is None else out_ref.at[index, part]
            return pltpu.make_async_remote_copy(
                src_ref=view if src is None else src, dst_ref=view,
                send_sem=send_sems.at[k], recv_sem=recv_sems.at[k], device_id=to, device_id_type=MESH)

        def start(*copies):
            for cp in copies:
                cp.start()
            return list(copies)

        mine = pltpu.make_async_copy(x_ref, out_ref.at[4 * x + 2 * y + c], local_sem)
        mine.start()
        sent = start(copy(0, me, sib, src=x_ref), copy(1, me, (*xn, c), src=x_ref), copy(2, me, (*yn, c), src=x_ref))
        copy(1, (*xn, c), me).wait_recv()
        sent += start(copy(3, (*xn, c), sib), copy(5, (*xn, c), (*yn, c), part=lo))
        copy(2, (*yn, c), me).wait_recv()
        sent += start(copy(4, (*yn, c), sib), copy(6, (*yn, c), (*xn, c), part=hi))
        copy(5, (*dg, c), me, part=lo).wait_recv()
        copy(6, (*dg, c), me, part=hi).wait_recv()
        sent += start(copy(7, (*dg, c), sib))
        for k, blk in ((0, sib), (3, (*xn, 1 - c)), (4, (*yn, 1 - c)), (7, (*dg, 1 - c))):
            copy(k, blk, me).wait_recv()
        for cp in sent:
            cp.wait_send()
        mine.wait()

    return _pcall(
        _behind(body, 1, after), name=name,
        out_shape=[jax.ShapeDtypeStruct((8,) + block.shape, block.dtype), jax.ShapeDtypeStruct((8, 128), F32)],
        in_specs=[pl.BlockSpec(memory_space=pl.ANY)] * (1 + len(after)),
        out_specs=[pl.BlockSpec(memory_space=pl.ANY), pl.BlockSpec(memory_space=pltpu.VMEM)],
        scratch_shapes=[pltpu.SemaphoreType.DMA((8,)), pltpu.SemaphoreType.DMA((8,)), pltpu.SemaphoreType.DMA(())],
        compiler_params=pltpu.CompilerParams(has_side_effects=True),
    )(block, *after)


HBM_SPEC = pl.BlockSpec(memory_space=pltpu.HBM)
SEM_SPEC = pl.BlockSpec(memory_space=pltpu.SEMAPHORE)
ANY_SPEC = pl.BlockSpec(memory_space=pl.ANY)
EFFECT = pltpu.SideEffectType.DATAFLOW_SIDE_EFFECTING


def _in_hbm(a):
    return pltpu.with_memory_space_constraint(a, pltpu.HBM)


def _start_copies(name, src, land_shape, plan, n):
    def body(src_ref, land_ref, send_sems, recv_sems, src_thru, land_thru, token):
        for k, (s, d, to, _) in enumerate(plan(src_ref, land_ref)):
            pltpu.make_async_remote_copy(src_ref=s, dst_ref=d, send_sem=send_sems.at[k], recv_sem=recv_sems.at[k],
                                         device_id=to, device_id_type=MESH).start()
        token[...] = jnp.zeros_like(token)

    return _pcall(
        body, name=name,
        out_shape=(pltpu.SemaphoreType.DMA((n,)), pltpu.SemaphoreType.DMA((n,)), pltpu.HBM(src.shape, src.dtype),
                   pltpu.HBM(land_shape, src.dtype), jax.ShapeDtypeStruct((8, 128), F32)),
        in_specs=(HBM_SPEC, HBM_SPEC),
        out_specs=(SEM_SPEC, SEM_SPEC, HBM_SPEC, HBM_SPEC, pl.BlockSpec(memory_space=pltpu.VMEM)),
        input_output_aliases={0: 2, 1: 3}, compiler_params=pltpu.CompilerParams(has_side_effects=EFFECT),
    )(_in_hbm(src), _in_hbm(lax.empty(land_shape, src.dtype)))


def _wait_copies(name, started, after, plan):
    send_sems, recv_sems, src_thru, land_thru, _ = started

    def body(src_ref, land_ref, send_sems, recv_sems, *rest):
        for k, (s, _, to, mine) in enumerate(plan(src_ref, land_ref)):
            cp = pltpu.make_async_remote_copy(src_ref=s, dst_ref=mine, send_sem=send_sems.at[k],
                                              recv_sem=recv_sems.at[k], device_id=to, device_id_type=MESH)
            cp.wait_send()
            cp.wait_recv()

    return _pcall(
        body, name=name,
        out_shape=(pltpu.HBM(src_thru.shape, src_thru.dtype), pltpu.HBM(land_thru.shape, land_thru.dtype)),
        in_specs=(HBM_SPEC, HBM_SPEC, SEM_SPEC, SEM_SPEC) + (ANY_SPEC,) * len(after), out_specs=(HBM_SPEC, HBM_SPEC),
        input_output_aliases={0: 0, 1: 1}, compiler_params=pltpu.CompilerParams(has_side_effects=EFFECT),
    )(src_thru, land_thru, send_sems, recv_sems, *after)


def _start_inplace(name, buf, plan, n):
    def body(buf_ref, send_sems, recv_sems, buf_thru, token):
        for k, (s, d, to, _) in enumerate(plan(buf_ref, buf_ref)):
            pltpu.make_async_remote_copy(src_ref=s, dst_ref=d, send_sem=send_sems.at[k], recv_sem=recv_sems.at[k],
                                         device_id=to, device_id_type=MESH).start()
        token[...] = jnp.zeros_like(token)

    return _pcall(
        body, name=name,
        out_shape=(pltpu.SemaphoreType.DMA((n,)), pltpu.SemaphoreType.DMA((n,)), pltpu.HBM(buf.shape, buf.dtype),
                   jax.ShapeDtypeStruct((8, 128), F32)),
        in_specs=(HBM_SPEC,), out_specs=(SEM_SPEC, SEM_SPEC, HBM_SPEC, pl.BlockSpec(memory_space=pltpu.VMEM)),
        input_output_aliases={0: 2}, compiler_params=pltpu.CompilerParams(has_side_effects=EFFECT),
    )(_in_hbm(buf))


def _wait_inplace(name, started, after, plan):
    send_sems, recv_sems, buf_thru, _ = started

    def body(buf_ref, send_sems, recv_sems, *rest):
        for k, (s, _, to, mine) in enumerate(plan(buf_ref, buf_ref)):
            cp = pltpu.make_async_remote_copy(src_ref=s, dst_ref=mine, send_sem=send_sems.at[k],
                                              recv_sem=recv_sems.at[k], device_id=to, device_id_type=MESH)
            cp.wait_send()
            cp.wait_recv()

    return _pcall(
        body, name=name, out_shape=pltpu.HBM(buf_thru.shape, buf_thru.dtype),
        in_specs=(HBM_SPEC, SEM_SPEC, SEM_SPEC) + (ANY_SPEC,) * len(after), out_specs=HBM_SPEC,
        input_output_aliases={0: 0}, compiler_params=pltpu.CompilerParams(has_side_effects=EFFECT),
    )(buf_thru, send_sems, recv_sems, *after)


def _gather_plan(src_ref, land_ref):
    x, y, c = _place()
    peers = [(x, y, 1 - c), (1 - x, y, c), (x, 1 - y, c)]
    return [(src_ref, land_ref.at[4 * x + 2 * y + c], p, land_ref.at[4 * p[0] + 2 * p[1] + p[2]]) for p in peers]


def _relay_plan(buf_ref, _):
    x, y, c = _place()
    slot = lambda p, pc: 4 * p[0] + 2 * p[1] + pc
    xn, yn, dg, sib = (1 - x, y), (x, 1 - y), (1 - x, 1 - y), (x, y, 1 - c)
    half = buf_ref.shape[1] // 2
    lo, hi = pl.ds(0, half), pl.ds(half, half)
    return [(buf_ref.at[slot(xn, c)], buf_ref.at[slot(xn, c)], sib, buf_ref.at[slot(xn, 1 - c)]),
            (buf_ref.at[slot(yn, c)], buf_ref.at[slot(yn, c)], sib, buf_ref.at[slot(yn, 1 - c)]),
            (buf_ref.at[slot(xn, c), lo], buf_ref.at[slot(xn, c), lo], (*yn, c), buf_ref.at[slot(dg, c), lo]),
            (buf_ref.at[slot(yn, c), hi], buf_ref.at[slot(yn, c), hi], (*xn, c), buf_ref.at[slot(dg, c), hi])]


def _swap_plan(src_ref, land_ref):
    x, y, c = _place()
    return [(src_ref.at[:, pl.ds(1 - c, 1)], land_ref, (x, y, 1 - c), land_ref)]


def _exchange_plan(src_ref, land_ref):
    x, y, c = _place()
    chips = [(1 - x, y), (x, 1 - y), (1 - x, 1 - y)]
    return [(src_ref.at[2 * px + py], land_ref.at[2 * x + y], (px, py, c), land_ref.at[2 * px + py]) for px, py in chips]


def _gather_forward(land, block):
    def body(land_ref, out_ref, send_sems, recv_sems):
        x, y, c = _place()
        chips = [(1 - x, 1 - y)]

        def copy(k, px, py, pc):
            blk = out_ref.at[4 * px + 2 * py + pc]
            return pltpu.make_async_remote_copy(src_ref=blk, dst_ref=blk, send_sem=send_sems.at[k],
                                                recv_sem=recv_sems.at[k], device_id=(x, y, 1 - c), device_id_type=MESH)

        sent = [copy(k, px, py, c) for k, (px, py) in enumerate(chips)]
        for cp in sent:
            cp.start()
        for k, (px, py) in enumerate(chips):
            copy(k, px, py, 1 - c).wait_recv()
        for cp in sent:
            cp.wait_send()

    land = _pcall(
        body, name="allgather_rest_forward", out_shape=jax.ShapeDtypeStruct(land.shape, land.dtype),
        in_specs=[ANY_SPEC], out_specs=ANY_SPEC, input_output_aliases={0: 0},
        scratch_shapes=[pltpu.SemaphoreType.DMA((1,)), pltpu.SemaphoreType.DMA((1,))],
        compiler_params=pltpu.CompilerParams(has_side_effects=True),
    )(land)

    rows = block.shape[0]
    tr = rows // 4

    def place(me_ref, x_ref, land_ref, out_ref):
        out_ref[...] = x_ref[...]

    x, y, c = _place()
    grid_spec = pltpu.PrefetchScalarGridSpec(
        num_scalar_prefetch=1, grid=(rows // tr,),
        in_specs=[pl.BlockSpec((tr, D), lambda i, me: (i, 0)), ANY_SPEC],
        out_specs=pl.BlockSpec((None, tr, D), lambda i, me: (me[0], i, 0)))
    return _pcall(
        place, name="allgather_rest_own", grid_spec=grid_spec, out_shape=jax.ShapeDtypeStruct(land.shape, land.dtype),
        input_output_aliases={2: 0}, compiler_params=pltpu.CompilerParams(dimension_semantics=("arbitrary",)),
    )((4 * x + 2 * y + c).reshape(1), block, land)


class _ReduceScatter:
    def __init__(self, name, g):
        self.name = name
        rows = g.shape[1]
        self.started = _start_copies(name + "_swap_start", g.reshape(4, 2, rows, D), (4, 1, rows, D), _swap_plan, 1)
        self.token = self.started[4]

    def halfway(self, after):
        g4, theirs = _wait_copies(self.name + "_swap_wait", self.started, after, _swap_plan)
        self.own = _add_halves(g4, theirs, lax.axis_index("c").reshape(1), self.name + "_add_halves")
        self.started = _start_copies(self.name + "_exch_start", self.own, self.own.shape, _exchange_plan, 3)
        self.token = self.started[4]

    def finish(self, after):
        own, got = _wait_copies(self.name + "_exch_wait", self.started, after, _exchange_plan)
        chip = 2 * lax.axis_index("x") + lax.axis_index("y")
        return own, got, (chip + jnp.arange(4, dtype=jnp.int32)) % 4


def _pack_small(p, loss):
    def body(*refs):
        o_ref = refs[-1]
        o_ref[...] = jnp.zeros_like(o_ref)
        for ref, name in zip(refs, SMALL):
            r0, nr, c0, nc = SMALL_SLOT[name]
            o_ref[r0:r0 + nr, c0:c0 + nc] = ref[...]
        o_ref[17:18, 0:128] = refs[len(SMALL)][0:1, :]

    return _one_call(body, "pack_small_grads", [p[n] for n in SMALL] + [loss], [((SMALL_ROWS, D), F32)])[0]


_GAP_DEV, _GAP_ROW = divmod(GATE0 + 8, N_IN)
_GAP = CHK0 - GATE0 - 8


def _in_rows_to_proj(g):
    runs = [(j, 0, N_IN, N_IN * j) for j in range(_GAP_DEV)]
    runs += [(_GAP_DEV, 0, _GAP_ROW, N_IN * _GAP_DEV), (_GAP_DEV, _GAP_ROW, N_IN, N_IN * _GAP_DEV + _GAP_ROW + _GAP)]
    runs += [(j, 0, N_IN, N_IN * j + _GAP) for j in range(_GAP_DEV + 1, 8)]

    def body(g_ref, o_ref, acc_ref):
        acc_ref[...] = jnp.zeros_like(acc_ref)
        for j, r0, r1, dest in runs:
            start, shift = dest // 16 * 16, dest % 16
            win = -(-(shift + r1 - r0) // 16) * 16
            r = lax.broadcasted_iota(jnp.int32, (win, R_IN), 0)
            c = lax.broadcasted_iota(jnp.int32, (win, R_IN), 1)
            move = jnp.where((c >= r0) & (c < r1) & (r == c - r0 + shift), 1.0, 0.0).astype(BF16)
            acc_ref[start:start + win, :] += _nn(move, g_ref[j])
        o_ref[...] = acc_ref[...].astype(BF16)

    return _pcall(
        body, name="w_in_layout", out_shape=jax.ShapeDtypeStruct((PROJ, D), BF16), grid=(1,),
        in_specs=[pl.BlockSpec(g.shape, lambda i: (0, 0, 0))], out_specs=pl.BlockSpec((PROJ, D), lambda i: (0, 0)),
        scratch_shapes=[pltpu.VMEM((PROJ, D), F32)],
        compiler_params=pltpu.CompilerParams(dimension_semantics=("arbitrary",), vmem_limit_bytes=VMEM_BIG),
    )(g)


def _wgrad_in(pieces, h1):
    n = len(pieces)
    ends = [sum(p.shape[1] for p in pieces[:k + 1]) for k in range(n)]
    assert ends[-1] == PROJ

    def body(*refs):
        piece_refs, (b_ref, o_ref, g_ref), bufs, sem = refs[:n], refs[n:n + 3], refs[n + 3:2 * n + 3], refs[2 * n + 3]
        i = pl.program_id(0)
        copies = [pltpu.make_async_copy(piece_refs[k], bufs[k], sem.at[k]) for k in range(n)]

        @pl.when(i == 0)
        def _():
            for copy in copies:
                copy.start()
            g_ref[PROJ:, :] = jnp.zeros((R_IN - N_IN + 1, D), F32)

        for k in range(n):
            @pl.when(i == k)
            def _(k=k):
                copies[k].wait()
                g_ref[ends[k] - pieces[k].shape[1]:ends[k], :] = _tn(bufs[k][...], b_ref[...])

        row = lax.broadcasted_iota(jnp.int32, (R_IN, D), 0)
        for j in range(8):
            lo = N_IN * j + (_GAP if j > _GAP_DEV else 0)
            hi = N_IN * j + (_GAP if j >= _GAP_DEV else 0)
            ready = min(k for k in range(n) if ends[k] >= min(hi + R_IN, PROJ))

            @pl.when(i == ready)
            def _(j=j, lo=lo, hi=hi):
                v = g_ref[hi:hi + R_IN, :]
                if lo != hi:
                    v = jnp.where(row < _GAP_ROW, g_ref[lo:lo + R_IN, :], v)
                o_ref[j] = jnp.where(row < N_IN, v, 0.0).astype(BF16)

    return _pcall(
        body, name="wgrad_in", grid=(n,),
        in_specs=[ANY_SPEC] * n + [_resident(h1)],
        out_specs=pl.BlockSpec((8, R_IN, D), lambda i: (0, 0, 0)),
        out_shape=jax.ShapeDtypeStruct((8, R_IN, D), BF16),
        scratch_shapes=[pltpu.VMEM((PROJ + R_IN - N_IN + 1, D), F32)] + [pltpu.VMEM(p.shape, BF16) for p in pieces]
        + [pltpu.SemaphoreType.DMA((n,))],
        compiler_params=pltpu.CompilerParams(dimension_semantics=("arbitrary",), vmem_limit_bytes=VMEM_BIG),
    )(*pieces, h1)


def _local_grads(x, mem, tgt, win_t, gw_of, sm, on_grads, after=()):
    b_pad = jnp.pad(sm['b_fgt'], ((0, 0), (0, 120)))
    tbl = jnp.pad(sm['rel_bias'], ((0, 0), (0, NREL_PAD - 257)))

    h1, proj, flog, kvp = _premix_fwd(x, sm['g_mix_pre'], win_t, after)
    c = _gate_fwd(flog, b_pad)
    ct3 = c[:, :8].T.reshape(4, 2, T)
    o_f, lse_f = _fox_fwd(proj, c, ct3)
    vt3 = _relvec_fwd(tbl).reshape(4, 2, VW)
    o_c, lse_c = _chk_fwd(proj, kvp, vt3, [gw_of('relay', [o_f])])
    gw = gw_of('done', [o_c])
    w_out, w_mq, w_mk, w_mv, w_mo, w1_t, w2 = (_wblk(gw, n) for n in ('w_out', 'w_mq', 'w_mk', 'w_mv', 'w_mo', 'w_ff1', 'w_ff2'))
    ycat, z, x1, h2, qm = _postmix_fwd(x, o_f, o_c, sm['g_fox_out'], sm['g_chk_out'], w_out,
                                       sm['g_mix_post'], sm['g_mem_pre'], w_mq)
    memn, km, vm = _memkv_fwd(mem, sm['g_mem_kv'], w_mk, w_mv)
    om, ym, x2, h3 = _mem_fwd(qm, x1, km, vm, w_mo, sm['g_mem_post'], sm['g_ff_pre'])

    gs = {}
    dx2, da, dy3, r, loss_acc, gs['g_ff_post'], gs['g_ff_pre'] = _ffn_step(h3, x2, tgt, w1_t, w2, sm['g_ff_post'],
                                                                         sm['g_ff_pre'])
    tok = on_grads('A', _wgrad_group("wgrad_ff", [(da, h3), (r, dy3)], 512), None)
    dx1, dym, dqm, dkm, dvm, gs['g_mem_post'], gs['g_mem_pre'] = _mem_bwd(
        dx2, ym, x1, qm, km, vm, w_mo, w_mq, sm['g_mem_post'], sm['g_mem_pre'], [tok])
    tok = on_grads('A halfway', None, [dx1])
    gs['g_mem_kv'] = _memkv_bwd(dkm, dvm, mem, w_mk, w_mv)
    dz, dof, doc, gs['g_mix_post'], gs['g_fox_out'], gs['g_chk_out'] = _postmix_bwd(
        dx1, z, o_f, o_c, w_out, sm['g_mix_post'], sm['g_fox_out'], sm['g_chk_out'], [tok])
    tok = on_grads('B', _wgrad_whole("wgrad_mem_out", [(ycat, dz), (h2, dqm), (memn, dkm), (memn, dvm), (om, dym)]), None)
    dq_f, dk_f, dv_f, dct, dcq = _fox_bwd(proj, c, ct3, o_f, lse_f, dof, [tok])
    tok = on_grads('B halfway', None, [dq_f])
    dq_c, dk_c, dv_c, gv = _chk_bwd(proj, kvp, vt3, o_c, lse_c, doc, [tok])
    gs['rel_bias'] = _relvec_bwd(gv.reshape(8, VW))[:, :257]
    dc = jnp.pad(dct.reshape(8, T).T + dcq[:, :, :2].transpose(1, 0, 2).reshape(T, 8), ((0, 0), (0, 120)))
    dflog, db = _gate_bwd(dc, flog, b_pad)
    gs['b_fgt'] = db[0:1, :8]
    pieces = [dq_f, dk_f, dv_f, dflog, dq_c, dk_c, dv_c]
    on_grads('C', _wgrad_in(pieces, h1), None)
    tok = on_grads('C halfway', None, [gs['g_mem_kv']])
    grad_x, gs['g_mix_pre'] = _premix_bwd(dx1, x, pieces, win_t, sm['g_mix_pre'], [tok])
    return loss_acc, grad_x, gs


def kernel(x, mem, w_in, b_fgt, rel_bias, g_fox_out, g_chk_out, w_out, g_mix_pre, g_mix_post, g_mem_kv, w_mq, w_mk, w_mv, w_mo, g_mem_pre, g_mem_post, w_ff1, w_ff2, g_ff_pre, g_ff_post, loss_target, m_w_in, m_b_fgt, m_rel_bias, m_g_fox_out, m_g_chk_out, m_w_out, m_g_mix_pre, m_g_mix_post, m_g_mem_kv, m_w_mq, m_w_mk, m_w_mv, m_w_mo, m_g_mem_pre, m_g_mem_post, m_w_ff1, m_w_ff2, m_g_ff_pre, m_g_ff_post, v_w_in, v_b_fgt, v_rel_bias, v_g_fox_out, v_g_chk_out, v_w_out, v_g_mix_pre, v_g_mix_post, v_g_mem_kv, v_w_mq, v_w_mk, v_w_mv, v_w_mo, v_g_mem_pre, v_g_mem_post, v_w_ff1, v_w_ff2, v_g_ff_pre, v_g_ff_post):
    args = dict(locals())
    two_d = lambda a: a.reshape(a.shape[-2:])
    w = {n: two_d(args[n]) for n in WEIGHTS}
    m = {n: two_d(args['m_' + n]) for n in WEIGHTS}
    v = {n: two_d(args['v_' + n]) for n in WEIGHTS}

    sm = {n: w[n] for n in SMALL}
    shard_in = jnp.pad(w['w_in'].T, ((0, R_IN - N_IN), (0, 0))).astype(BF16)
    gathered_in, zero = _allgather(shard_in, "allgather_w_in")
    win_t = _in_rows_to_proj(gathered_in)
    shard_rest = (jnp.concatenate([w['w_ff1'].T, w['w_ff2'], w['w_out'], w['w_mq'], w['w_mk'], w['w_mv'], w['w_mo']],
                                  axis=0) + zero[0, 0]).astype(BF16)
    gather = {'first': _start_copies("allgather_rest_start", shard_rest, (8, R_REST, D), _gather_plan, 3)}

    def gw_of(stage, after):
        if stage == 'relay':
            gather['block'], land = _wait_copies("allgather_rest_wait", gather['first'], after, _gather_plan)
            gather['second'] = _start_inplace("allgather_rest_relay_start", land, _relay_plan, 4)
            return gather['second'][3]
        land = _wait_inplace("allgather_rest_relay_wait", gather['second'], after, _relay_plan)
        return _gather_forward(land, gather['block'])

    rs = {}

    def on_grads(stage, g, after):
        if stage.endswith('halfway'):
            rs[stage[0]].halfway(after)
            return rs[stage[0]].token
        rs[stage] = _ReduceScatter("rs_" + stage.lower(), g)
        return rs[stage].token

    loss_local, grad_x, gs = _local_grads(x[0], mem[0], loss_target[0], win_t, gw_of, sm, on_grads, [gather['first'][4]])
    grads, deltas, new_m, new_v = {}, {}, {}, {}

    def update(n, out):
        grads[n], deltas[n], new_m[n], new_v[n] = out

    own, got, order = rs['A'].finish([grad_x, rs['C'].token])
    update('w_ff1', _sum_adam(own, got, order, 0, w['w_ff1'], m['w_ff1'], v['w_ff1'], "adamw_w_ff1", transposed=True))
    update('w_ff2', _sum_adam(own, got, order, 512, w['w_ff2'], m['w_ff2'], v['w_ff2'], "adamw_w_ff2"))
    own, got, order = rs['B'].finish([grad_x, rs['C'].token])
    names_b = ('w_out', 'w_mq', 'w_mk', 'w_mv', 'w_mo')
    done = _sum_adam_rows(own, got, order, [w[n] for n in names_b], [m[n] for n in names_b], [v[n] for n in names_b],
                          "adamw_group_b")
    for k, n in enumerate(names_b):
        update(n, done[4 * k:4 * k + 4])

    own, got, order = rs['C'].finish([new_v[n] for n in BIG if n != 'w_in'])
    rows_of = lambda a: jnp.transpose(a, (2, 0, 1))
    done = _sum_adam(own, got, order, 0, rows_of(w_in), rows_of(m_w_in), rows_of(v_w_in), "adamw_w_in")
    update('w_in', [jnp.transpose(a, (1, 2, 0)) for a in done])

    gparts, _ = _allgather(_pack_small(gs, loss_local), "allgather_small_grads", [got])
    small = _adamw_small(gparts, [w[n] for n in SMALL], [m[n] for n in SMALL], [v[n] for n in SMALL])
    loss = small[0][0, 0]
    for t, n in enumerate(SMALL):
        update(n, small[1 + 4 * t:5 + 4 * t])

    out = [loss, grad_x[None]]
    for group in (grads, deltas, new_m, new_v):
        out += [group[n].reshape(args[n].shape) for n in WEIGHTS]
    return tuple(out)
```

```python
import jax
import jax.numpy as jnp
from jax import lax
from jax.experimental import pallas as pl
from jax.experimental.pallas import tpu as pltpu

F32 = jnp.float32
BF16 = jnp.bfloat16
MESH = pl.DeviceIdType.MESH

T = 2048
D = 1024
NMEM = 256
DFF = 4096
EPS = 1e-6
TM = 256
TM_WIDE = 512
TQ = 256
FQ = 512
HD = 64
SCALE = HD ** -0.5
MEM_HEADS = 4
MEM_HD = 256
MEM_SCALE = MEM_HD ** -0.5
NEG = -1e30
LEFT = 512
WIN = LEFT + TQ
VW = 1024
NREL_PAD = 384
PROJ = 3200
GATE0 = 1536
CHK0 = 1664
VMEM_BIG = 56 * 1024 * 1024

ADAM_LR = 0.001
ADAM_B1 = 0.9
ADAM_B2 = 0.999
ADAM_EPS = 1e-08
ADAM_WD = 0.01
ADAM_STEP = 10

N_IN = 385
R_IN = 400
R_REST = 1664
W_ROWS = {'w_ff1': (0, 512), 'w_ff2': (512, 512),
          'w_out': (1024, 128), 'w_mq': (1152, 128), 'w_mk': (1280, 128), 'w_mv': (1408, 128), 'w_mo': (1536, 128)}
SMALL_ROWS = 24
SMALL_SLOT = {'rel_bias': (0, 8, 0, 257), 'b_fgt': (8, 1, 0, 8), 'g_fox_out': (9, 1, 0, 512), 'g_chk_out': (9, 1, 512, 512),
              'g_mix_pre': (10, 1, 0, 1024), 'g_mix_post': (11, 1, 0, 1024), 'g_mem_kv': (12, 1, 0, 1024),
              'g_mem_pre': (13, 1, 0, 1024), 'g_mem_post': (14, 1, 0, 1024), 'g_ff_pre': (15, 1, 0, 1024),
              'g_ff_post': (16, 1, 0, 1024)}

WEIGHTS = ['w_in', 'b_fgt', 'rel_bias', 'g_fox_out', 'g_chk_out', 'w_out', 'g_mix_pre', 'g_mix_post', 'g_mem_kv',
           'w_mq', 'w_mk', 'w_mv', 'w_mo', 'g_mem_pre', 'g_mem_post', 'w_ff1', 'w_ff2', 'g_ff_pre', 'g_ff_post']
BIG = ['w_in', 'w_out', 'w_mq', 'w_mk', 'w_mv', 'w_mo', 'w_ff1', 'w_ff2']
SMALL = [n for n in WEIGHTS if n not in BIG]


def _pcall(body, **kw):
    return pl.pallas_call(body, **kw)


def _nn(a, b):
    return jnp.dot(a, b, preferred_element_type=F32)


def _nt(a, b):
    return lax.dot_general(a, b, (((1,), (1,)), ((), ())), preferred_element_type=F32)


def _tn(a, b):
    return lax.dot_general(a, b, (((0,), (0,)), ((), ())), preferred_element_type=F32)


def _w(ref):
    v = ref[...]
    return v if v.ndim == 2 else v.reshape(-1, v.shape[-1])


def _rstd(x):
    return lax.rsqrt(jnp.mean(x * x, axis=-1, keepdims=True) + EPS)


def _rms(x, g):
    return x * _rstd(x) * g


def _rms_bwd(x, g, dy):
    r = _rstd(x)
    xh = x * r
    dg = jnp.sum(dy * xh, axis=0, keepdims=True)
    dxh = dy * g
    dx = r * (dxh - xh * jnp.mean(dxh * xh, axis=-1, keepdims=True))
    return dx, dg


def _resident(a):
    if isinstance(a, tuple):
        _, shape, index = a
        return pl.BlockSpec(shape, lambda *_: index, pipeline_mode=pl.Buffered(1))
    return pl.BlockSpec(a.shape, lambda *_, nd=a.ndim: (0,) * nd, pipeline_mode=pl.Buffered(1))


def _wblk(gw, name):
    r0, rows = W_ROWS[name]
    return (gw, (8, rows, D), (0, r0 // rows, 0))


def _behind(body, n_in, after):
    if not after:
        return body
    return lambda *refs: body(*refs[:n_in], *refs[n_in + len(after):])


def _tok_call(body, name, tiled, full, outs_tiled, outs_acc=(), rows=T, tm=TM, vmem=None, after=(), scratch=()):
    in_specs = [pl.BlockSpec((tm, a.shape[1]), lambda i: (i, 0)) for a in tiled]
    in_specs += [_resident(a) for a in full] + [ANY_SPEC] * len(after)
    full = [a[0] if isinstance(a, tuple) else a for a in full] + list(after)
    body = _behind(body, len(tiled) + len(full) - len(after), after)
    out_shape = [jax.ShapeDtypeStruct((rows, c), dt) for c, dt in outs_tiled]
    out_shape += [jax.ShapeDtypeStruct(s, F32) for s in outs_acc]
    out_specs = [pl.BlockSpec((tm, c), lambda i: (i, 0)) for c, _ in outs_tiled]
    out_specs += [pl.BlockSpec(s, lambda i, nd=len(s): (0,) * nd) for s in outs_acc]
    return _pcall(
        body, name=name, grid=(rows // tm,), in_specs=in_specs, out_specs=out_specs, out_shape=out_shape,
        scratch_shapes=list(scratch),
        compiler_params=pltpu.CompilerParams(dimension_semantics=("arbitrary",), vmem_limit_bytes=vmem),
    )(*tiled, *full)


def _one_call(body, name, ins, outs, vmem=None):
    whole = lambda s: pl.BlockSpec(s, lambda i, nd=len(s): (0,) * nd)
    return _pcall(
        body, name=name, grid=(1,), in_specs=[_resident(a) for a in ins], out_specs=[whole(s) for s, _ in outs],
        out_shape=[jax.ShapeDtypeStruct(s, dt) for s, dt in outs],
        compiler_params=pltpu.CompilerParams(dimension_semantics=("arbitrary",), vmem_limit_bytes=vmem),
    )(*[a[0] if isinstance(a, tuple) else a for a in ins])


def _premix_fwd(x, g_pre, win_t, after=()):
    def body(x_ref, g_ref, w_ref, h_ref, proj_ref, flog_ref, kvp_ref):
        s = pl.program_id(0)

        @pl.when(s == 0)
        def _():
            kvp_ref[...] = jnp.zeros_like(kvp_ref)

        @pl.when(s > 0)
        def _():
            h = _rms(x_ref[...], g_ref[...]).astype(BF16)
            h_ref[...] = h
            p = _nt(h, w_ref[...])
            proj_ref[...] = p.astype(BF16)
            flog_ref[...] = p[:, GATE0:GATE0 + 128]
            kvp_ref[...] = p[:, CHK0 + 512:].astype(BF16)

    tile = lambda c: pl.BlockSpec((LEFT, c), lambda s: (jnp.maximum(s - 1, 0), 0))
    return _pcall(
        _behind(body, 3, after), name="premix_fwd", grid=(T // LEFT + 1,),
        in_specs=[tile(D), _resident(g_pre), _resident(win_t)] + [ANY_SPEC] * len(after),
        out_specs=[tile(D), tile(PROJ), tile(128), pl.BlockSpec((LEFT, 1024), lambda s: (s, 0))],
        out_shape=[jax.ShapeDtypeStruct((T, D), BF16), jax.ShapeDtypeStruct((T, PROJ), BF16),
                   jax.ShapeDtypeStruct((T, 128), F32), jax.ShapeDtypeStruct((T + LEFT, 1024), BF16)],
        compiler_params=pltpu.CompilerParams(dimension_semantics=("arbitrary",), vmem_limit_bytes=VMEM_BIG),
    )(x, g_pre, win_t, *after)


def _postmix_fwd(x, o_f, o_c, g_fo, g_co, w_out, g_post, g_mpre, w_mq):
    def body(x_ref, of_ref, oc_ref, gfo_ref, gco_ref, wo_ref, gp_ref, gm_ref, wq_ref,
             y_ref, z_ref, x1_ref, h2_ref, qm_ref):
        y_ref[:, :512] = _rms(of_ref[...], gfo_ref[...]).astype(BF16)
        y_ref[:, 512:] = _rms(oc_ref[...], gco_ref[...]).astype(BF16)
        z = _nn(y_ref[...], _w(wo_ref))
        z_ref[...] = z
        x1 = x_ref[...] + _rms(z, gp_ref[...])
        x1_ref[...] = x1
        h2 = _rms(x1, gm_ref[...]).astype(BF16)
        h2_ref[...] = h2
        qm_ref[...] = _nn(h2, _w(wq_ref)).astype(BF16)

    return _tok_call(body, "postmix_fwd", [x, o_f, o_c], [g_fo, g_co, w_out, g_post, g_mpre, w_mq],
                     [(D, BF16), (D, F32), (D, F32), (D, BF16), (D, BF16)], tm=TM_WIDE, vmem=VMEM_BIG)


def _memkv_fwd(mem, g_kv, w_mk, w_mv):
    def body(m_ref, g_ref, wk_ref, wv_ref, mn_ref, k_ref, v_ref):
        mn = _rms(m_ref[...], g_ref[...]).astype(BF16)
        mn_ref[...] = mn
        k_ref[...] = _nn(mn, _w(wk_ref)).astype(BF16)
        v_ref[...] = _nn(mn, _w(wv_ref)).astype(BF16)

    return _tok_call(body, "memkv_fwd", [mem], [g_kv, w_mk, w_mv],
                     [(D, BF16), (D, BF16), (D, BF16)], rows=NMEM, tm=NMEM, vmem=VMEM_BIG)


def _mem_fwd(qm, x1, km, vm, w_mo, g_post, g_fpre):
    def body(q_ref, x1_ref, k_ref, v_ref, wo_ref, gp_ref, gf_ref, om_ref, ym_ref, x2_ref, h3_ref):
        for h in range(MEM_HEADS):
            sl = slice(h * MEM_HD, (h + 1) * MEM_HD)
            s = _nt(q_ref[:, sl], k_ref[:, sl]) * MEM_SCALE
            p = jnp.exp(s - jnp.max(s, axis=-1, keepdims=True))
            p = p / jnp.sum(p, axis=-1, keepdims=True)
            om_ref[:, sl] = _nn(p.astype(BF16), v_ref[:, sl]).astype(BF16)
        ym = _nn(om_ref[...], _w(wo_ref))
        ym_ref[...] = ym
        x2 = x1_ref[...] + _rms(ym, gp_ref[...])
        x2_ref[...] = x2
        h3_ref[...] = _rms(x2, gf_ref[...]).astype(BF16)

    return _tok_call(body, "mem_fwd", [qm, x1], [km, vm, w_mo, g_post, g_fpre],
                     [(D, BF16), (D, F32), (D, F32), (D, BF16)], tm=TM_WIDE, vmem=VMEM_BIG)


def _tri(lower):
    r = lax.broadcasted_iota(jnp.int32, (128, 128), 0)
    c = lax.broadcasted_iota(jnp.int32, (128, 128), 1)
    return jnp.where(r >= c if lower else c >= r, 1.0, 0.0).astype(F32)


def _hdot(a, b):
    return jnp.dot(a, b, preferred_element_type=F32, precision=lax.Precision.HIGHEST)


def _gate_fwd(flog, b_pad):
    def body(f_ref, b_ref, c_ref):
        tri = _tri(True)

        def step(i, carry):
            rows = pl.ds(pl.multiple_of(i * 128, 128), 128)
            z = f_ref[rows, :] + b_ref[...]
            lf = jnp.minimum(z, 0.0) - jnp.log(1.0 + jnp.exp(-jnp.abs(z)))
            cb = _hdot(tri, lf) + carry
            c_ref[rows, :] = cb
            return cb[127:128, :]

        lax.fori_loop(0, T // 128, step, jnp.zeros((1, 128), F32))

    return _one_call(body, "gate_fwd", [flog, b_pad], [((T, 128), F32)])[0]


def _gate_bwd(dc, flog, b_pad):
    def body(dc_ref, f_ref, b_ref, df_ref, db_ref):
        tri = _tri(False)

        def step(j, carry):
            run, db = carry
            i = T // 128 - 1 - j
            rows = pl.ds(pl.multiple_of(i * 128, 128), 128)
            dcb = dc_ref[rows, :]
            rb = _hdot(tri, dcb) + run
            z = f_ref[rows, :] + b_ref[...]
            df = rb * (1.0 / (1.0 + jnp.exp(z)))
            df_ref[rows, :] = df.astype(BF16)
            return run + jnp.sum(dcb, axis=0, keepdims=True), db + jnp.sum(df, axis=0, keepdims=True)

        _, db = lax.fori_loop(0, T // 128, step, (jnp.zeros((1, 128), F32), jnp.zeros((1, 128), F32)))
        db_ref[...] = jnp.broadcast_to(db, (8, 128))

    return _one_call(body, "gate_bwd", [dc, flog, b_pad], [((T, 128), BF16), ((8, 128), F32)])


def _lane_lo(rows=TQ):
    return lax.broadcasted_iota(jnp.int32, (rows, 128), 1) < HD


def _half(v, lo, a, scale=None):
    keep = lo if a == 0 else jnp.logical_not(lo)
    v = v.astype(F32) if scale is None else v.astype(F32) * scale
    return jnp.where(keep, v, 0.0).astype(BF16)


def _fox_specs():
    return [pl.BlockSpec((FQ, 128), lambda h, i: (i, h)),
            pl.BlockSpec((T, 128), lambda h, i: (0, 4 + h)),
            pl.BlockSpec((T, 128), lambda h, i: (0, 8 + h))]


def _lane_pick(x, at):
    lane = lax.broadcasted_iota(jnp.int32, x.shape, 1)
    return jnp.sum(jnp.where(lane == at, x, 0.0), axis=-1, keepdims=True)


def _fox_fwd(proj, c, ct3):
    def body(q_ref, k_ref, v_ref, c_ref, ct_ref, o_ref, l_ref):
        i = pl.program_id(1)
        lo = _lane_lo(FQ)
        causal = lax.broadcasted_iota(jnp.int32, (FQ, FQ), 1) <= lax.broadcasted_iota(jnp.int32, (FQ, FQ), 0)
        q = q_ref[...]
        qs = [_half(q, lo, a, SCALE) for a in range(2)]
        cqs = [_lane_pick(c_ref[...], 2 * pl.program_id(0) + a) for a in range(2)]

        def tile(off, carry, diagonal):
            kblk = k_ref[pl.ds(off, FQ), :]
            vblk = v_ref[pl.ds(off, FQ), :]
            new = []
            for a in range(2):
                m, l, acc = carry[a]
                s = _nt(qs[a], kblk) + (cqs[a] - ct_ref[a:a + 1, pl.ds(off, FQ)])
                if diagonal:
                    s = jnp.where(causal, s, NEG)
                m2 = jnp.maximum(m, jnp.max(s, axis=-1, keepdims=True))
                p = jnp.exp(s - m2)
                alpha = jnp.exp(m - m2)
                new.append((m2, alpha * l + jnp.sum(p, axis=-1, keepdims=True),
                            alpha * acc + _nn(p.astype(BF16), vblk)))
            return tuple(new)

        init = (jnp.full((FQ, 1), NEG, F32), jnp.zeros((FQ, 1), F32), jnp.zeros((FQ, 128), F32))
        carry = lax.fori_loop(0, i, lambda kb, c: tile(pl.multiple_of(kb * FQ, FQ), c, False), (init, init))
        carry = tile(pl.multiple_of(i * FQ, FQ), carry, True)
        outs = []
        for a in range(2):
            m, l, acc = carry[a]
            outs.append(acc / l)
            l_ref[:, 128 * a:128 * a + 128] = jnp.broadcast_to(m + jnp.log(l), (FQ, 128))
        o_ref[...] = jnp.where(lo, outs[0], outs[1])

    return _pcall(
        body, name="fox_fwd", grid=(4, T // FQ),
        in_specs=_fox_specs() + [pl.BlockSpec((FQ, 128), lambda h, i: (i, 0)),
                                 pl.BlockSpec((None, 2, T), lambda h, i: (h, 0, 0))],
        out_specs=[pl.BlockSpec((FQ, 128), lambda h, i: (i, h)), pl.BlockSpec((FQ, 256), lambda h, i: (i, h))],
        out_shape=[jax.ShapeDtypeStruct((T, 512), F32), jax.ShapeDtypeStruct((T, 1024), F32)],
        compiler_params=pltpu.CompilerParams(dimension_semantics=("arbitrary", "arbitrary"), vmem_limit_bytes=VMEM_BIG),
    )(proj, proj, proj, c, ct3)


def _fox_bwd(proj, c, ct3, o, lse, do, after=()):
    def body(q_ref, k_ref, v_ref, c_ref, ct_ref, o_ref, l_ref, do_ref, dq_ref, dkb_ref, dvb_ref, dct_ref, dcq_ref,
             dk_ref, dv_ref):
        i = pl.program_id(1)

        @pl.when(i == 0)
        def _():
            dk_ref[...] = jnp.zeros_like(dk_ref)
            dv_ref[...] = jnp.zeros_like(dv_ref)
            dct_ref[...] = jnp.zeros_like(dct_ref)

        lo = _lane_lo(FQ)
        causal = lax.broadcasted_iota(jnp.int32, (FQ, FQ), 1) <= lax.broadcasted_iota(jnp.int32, (FQ, FQ), 0)
        q = q_ref[...]
        do_v = do_ref[...]
        prod = do_v * o_ref[...]
        qs = [_half(q, lo, a, SCALE) for a in range(2)]
        dos = [_half(do_v, lo, a) for a in range(2)]
        deltas = [jnp.sum(jnp.where(lo if a == 0 else jnp.logical_not(lo), prod, 0.0), axis=-1, keepdims=True)
                  for a in range(2)]
        cqs = [_lane_pick(c_ref[...], 2 * pl.program_id(0) + a) for a in range(2)]
        las = [l_ref[:, 128 * a:128 * a + 1] for a in range(2)]

        def tile(off, carry, diagonal):
            kblk = k_ref[pl.ds(off, FQ), :]
            vblk = v_ref[pl.ds(off, FQ), :]
            new = []
            dk = jnp.zeros((128, FQ), F32)
            dv = jnp.zeros((128, FQ), F32)
            for a in range(2):
                dq_acc, rs = carry[a]
                s = _nt(qs[a], kblk) + (cqs[a] - ct_ref[a:a + 1, pl.ds(off, FQ)])
                if diagonal:
                    s = jnp.where(causal, s, NEG)
                p = jnp.exp(s - las[a])
                ds = p * (_nt(dos[a], vblk) - deltas[a])
                dsb = ds.astype(BF16)
                dk = dk + _tn(qs[a], dsb)
                dv = dv + _tn(dos[a], p.astype(BF16))
                dct_ref[a:a + 1, pl.ds(off, FQ)] -= jnp.sum(ds, axis=0, keepdims=True)
                new.append((dq_acc + _nn(dsb, kblk), rs + jnp.sum(ds, axis=-1, keepdims=True)))
            dk_ref[:, pl.ds(off, FQ)] += dk
            dv_ref[:, pl.ds(off, FQ)] += dv
            return tuple(new)

        init = (jnp.zeros((FQ, 128), F32), jnp.zeros((FQ, 1), F32))
        carry = lax.fori_loop(0, i, lambda kb, c: tile(pl.multiple_of(kb * FQ, FQ), c, False), (init, init))
        carry = tile(pl.multiple_of(i * FQ, FQ), carry, True)
        lane = lax.broadcasted_iota(jnp.int32, (FQ, 128), 1)
        dcq_ref[...] = jnp.where(lane == 0, carry[0][1], jnp.where(lane == 1, carry[1][1], 0.0))
        dq_ref[...] = (jnp.where(lo, carry[0][0], carry[1][0]) * SCALE).astype(BF16)

        @pl.when(i == T // FQ - 1)
        def _():
            dkb_ref[...] = dk_ref[...].T.astype(BF16)
            dvb_ref[...] = dv_ref[...].T.astype(BF16)

    blk = pl.BlockSpec((FQ, 128), lambda h, i: (i, h))
    wide = pl.BlockSpec((FQ, 256), lambda h, i: (i, h))
    rows = pl.BlockSpec((None, 2, T), lambda h, i: (h, 0, 0))
    col = pl.BlockSpec((T, 128), lambda h, i: (0, h))
    return _pcall(
        _behind(body, 8, after), name="fox_bwd", grid=(4, T // FQ),
        in_specs=_fox_specs() + [pl.BlockSpec((FQ, 128), lambda h, i: (i, 0)), rows, blk, wide, blk] + [ANY_SPEC] * len(after),
        out_specs=[blk, col, col, rows, pl.BlockSpec((None, FQ, 128), lambda h, i: (h, i, 0))],
        out_shape=[jax.ShapeDtypeStruct((T, 512), BF16), jax.ShapeDtypeStruct((T, 512), BF16),
                   jax.ShapeDtypeStruct((T, 512), BF16), jax.ShapeDtypeStruct((4, 2, T), F32),
                   jax.ShapeDtypeStruct((4, T, 128), F32)],
        scratch_shapes=[pltpu.VMEM((128, T), F32), pltpu.VMEM((128, T), F32)],
        compiler_params=pltpu.CompilerParams(dimension_semantics=("arbitrary", "arbitrary"), vmem_limit_bytes=VMEM_BIG),
    )(proj, proj, proj, c, ct3, o, lse, do, *after)


def _rel_onehot():
    ridx = lax.broadcasted_iota(jnp.int32, (NREL_PAD, VW), 0)
    j = lax.broadcasted_iota(jnp.int32, (NREL_PAD, VW), 1)
    return jnp.where(ridx == jnp.clip(TQ + LEFT - 1 - j, -128, 128) + 128, 1.0, 0.0).astype(F32)


def _relvec_fwd(tbl):
    def body(t_ref, v_ref):
        v_ref[...] = _hdot(t_ref[...], _rel_onehot())

    return _one_call(body, "relvec_fwd", [tbl], [((8, VW), F32)])[0]


def _relvec_bwd(gv):
    def body(g_ref, t_ref):
        t_ref[...] = lax.dot_general(g_ref[...], _rel_onehot(), (((1,), (1,)), ((), ())),
                                     preferred_element_type=F32, precision=lax.Precision.HIGHEST)

    return _one_call(body, "relvec_bwd", [gv], [((8, NREL_PAD), F32)])[0]


def _chk_bias(vt_ref, a, hidden):
    vb = jnp.broadcast_to(vt_ref[a:a + 1, :], (TQ, VW))
    y = pltpu.roll(vb, VW - (TQ - 1), 1, stride=1, stride_axis=0)[:, :WIN]
    cr = lax.broadcasted_iota(jnp.int32, (TQ, WIN), 0) // 64
    m = lax.broadcasted_iota(jnp.int32, (TQ, WIN), 1)
    return jnp.where((m // 64 >= cr) & (m // 64 <= cr + 8) & (m >= hidden), y, NEG)


def _chk_specs():
    return [pl.BlockSpec((TQ, 128), lambda h, i: (i, CHK0 // 128 + h)),
            pl.BlockSpec((T + LEFT, 128), lambda h, i: (0, h)),
            pl.BlockSpec((T + LEFT, 128), lambda h, i: (0, 4 + h)),
            pl.BlockSpec((None, 2, VW), lambda h, i: (h, 0, 0))]


def _chk_fwd(proj, kvp, vt3, after=()):
    def body(q_ref, k_ref, v_ref, vt_ref, o_ref, l_ref, bias_ref):
        i = pl.program_id(1)

        @pl.when(i == 0)
        def _():
            for first in range(3):
                for a in range(2):
                    bias_ref[first, a] = _chk_bias(vt_ref, a, max(LEFT - first * TQ, 0))

        lo = _lane_lo()
        off = pl.multiple_of(i * TQ, TQ)
        kw = k_ref[pl.ds(off, WIN), :]
        vw = v_ref[pl.ds(off, WIN), :]
        bias_at = jnp.minimum(i, 2)
        q = q_ref[...]
        outs = []
        for a in range(2):
            s = _nt(_half(q, lo, a, SCALE), kw) + bias_ref[bias_at, a]
            m = jnp.max(s, axis=-1, keepdims=True)
            p = jnp.exp(s - m)
            l = jnp.sum(p, axis=-1, keepdims=True)
            outs.append(_nn(p.astype(BF16), vw) / l)
            l_ref[:, 128 * a:128 * a + 128] = jnp.broadcast_to(m + jnp.log(l), (TQ, 128))
        o_ref[...] = jnp.where(lo, outs[0], outs[1])

    return _pcall(
        _behind(body, 4, after), name="chk_fwd", grid=(4, T // TQ), in_specs=_chk_specs() + [ANY_SPEC] * len(after),
        out_specs=[pl.BlockSpec((TQ, 128), lambda h, i: (i, h)), pl.BlockSpec((TQ, 256), lambda h, i: (i, h))],
        out_shape=[jax.ShapeDtypeStruct((T, 512), F32), jax.ShapeDtypeStruct((T, 1024), F32)],
        scratch_shapes=[pltpu.VMEM((3, 2, TQ, WIN), F32)],
        compiler_params=pltpu.CompilerParams(dimension_semantics=("arbitrary", "arbitrary")),
    )(proj, kvp, kvp, vt3, *after)


def _chk_bwd(proj, kvp, vt3, o, lse, do, after=()):
    nq = T // TQ

    def body(q_ref, k_ref, v_ref, vt_ref, o_ref, l_ref, do_ref, dq_ref, dkb_ref, dvb_ref, gv_ref, bias_ref, dsum_ref,
             dk_ref, dv_ref):
        i = pl.program_id(1)

        @pl.when(i == 0)
        def _():
            for first in range(3):
                for a in range(2):
                    bias_ref[first, a] = _chk_bias(vt_ref, a, max(LEFT - first * TQ, 0))
            dsum_ref[...] = jnp.zeros_like(dsum_ref)
            dk_ref[...] = jnp.zeros_like(dk_ref)
            dv_ref[...] = jnp.zeros_like(dv_ref)

        lo = _lane_lo()
        off = pl.multiple_of(i * TQ, TQ)
        kw = k_ref[pl.ds(off, WIN), :]
        vw = v_ref[pl.ds(off, WIN), :]
        bias_at = jnp.minimum(i, 2)
        q = q_ref[...]
        do_v = do_ref[...]
        prod = do_v * o_ref[...]
        dqs = []
        for a in range(2):
            keep = lo if a == 0 else jnp.logical_not(lo)
            qa = _half(q, lo, a, SCALE)
            doa = _half(do_v, lo, a)
            delta = jnp.sum(jnp.where(keep, prod, 0.0), axis=-1, keepdims=True)
            s = _nt(qa, kw) + bias_ref[bias_at, a]
            p = jnp.exp(s - l_ref[:, 128 * a:128 * a + 1])
            ds = p * (_nt(doa, vw) - delta)
            dsum_ref[a] += ds
            dsb = ds.astype(BF16)
            dk_ref[:, pl.ds(off, WIN)] += _tn(qa, dsb)
            dv_ref[:, pl.ds(off, WIN)] += _tn(doa, p.astype(BF16))
            dqs.append(_nn(dsb, kw))
        dq_ref[...] = (jnp.where(lo, dqs[0], dqs[1]) * SCALE).astype(BF16)

        @pl.when(i == nq - 1)
        def _():
            dkb_ref[...] = dk_ref[:, LEFT:].T.astype(BF16)
            dvb_ref[...] = dv_ref[:, LEFT:].T.astype(BF16)
            rr = lax.broadcasted_iota(jnp.int32, (TQ, TQ), 0)
            cc = lax.broadcasted_iota(jnp.int32, (TQ, TQ), 1)
            flip = jnp.where(rr + cc == TQ - 1, 1.0, 0.0).astype(F32)
            for a in range(2):
                dpad = jnp.concatenate([dsum_ref[a], jnp.zeros((TQ, VW - WIN), F32)], axis=1)
                z = pltpu.roll(_hdot(flip, dpad), 0, 1, stride=1, stride_axis=0)
                gv_ref[a:a + 1, :] = jnp.sum(z, axis=0, keepdims=True)

    blk = pl.BlockSpec((TQ, 128), lambda h, i: (i, h))
    wide = pl.BlockSpec((TQ, 256), lambda h, i: (i, h))
    col = pl.BlockSpec((T, 128), lambda h, i: (0, h))
    return _pcall(
        _behind(body, 7, after), name="chk_bwd", grid=(4, nq), in_specs=_chk_specs() + [blk, wide, blk] + [ANY_SPEC] * len(after),
        out_specs=[blk, col, col, pl.BlockSpec((None, 2, VW), lambda h, i: (h, 0, 0))],
        out_shape=[jax.ShapeDtypeStruct((T, 512), BF16), jax.ShapeDtypeStruct((T, 512), BF16),
                   jax.ShapeDtypeStruct((T, 512), BF16), jax.ShapeDtypeStruct((4, 2, VW), F32)],
        scratch_shapes=[pltpu.VMEM((3, 2, TQ, WIN), F32), pltpu.VMEM((2, TQ, WIN), F32),
                        pltpu.VMEM((128, T + LEFT), F32), pltpu.VMEM((128, T + LEFT), F32)],
        compiler_params=pltpu.CompilerParams(dimension_semantics=("arbitrary", "arbitrary")),
    )(proj, kvp, kvp, vt3, o, lse, do, *after)


def _zero_at_start(*refs):
    @pl.when(pl.program_id(0) == 0)
    def _():
        for r in refs:
            r[...] = jnp.zeros_like(r)


def _ffn_step(h3, x2, tgt, w1_t, w2, g_post, g_pre):
    def body(h_ref, x2_ref, t_ref, w1_ref, w2_ref, gp_ref, gf_ref, dx2_ref, da_ref, dy_ref, r_ref, loss_ref, dgp_ref, dgf_ref):
        _zero_at_start(loss_ref, dgp_ref, dgf_ref)
        w1, w2v = _w(w1_ref), _w(w2_ref)
        ra = jnp.maximum(_nt(h_ref[...], w1), 0.0)
        r = jnp.square(ra).astype(BF16)
        r_ref[...] = r
        y = _nn(r, w2v)
        x2v = x2_ref[...]
        e = x2v + _rms(y, gp_ref[...]) - t_ref[...]
        loss_ref[...] += 0.5 * jnp.sum(jnp.sum(e * e, axis=-1, keepdims=True) * (1.0 / D))
        dx3 = e * (1.0 / D)
        dy, dgp = _rms_bwd(y, gp_ref[...], dx3)
        dgp_ref[...] += dgp
        dyb = dy.astype(BF16)
        dy_ref[...] = dyb
        da = (_nt(dyb, w2v) * (2.0 * ra)).astype(BF16)
        da_ref[...] = da
        dh, dgf = _rms_bwd(x2v, gf_ref[...], _nn(da, w1))
        dgf_ref[...] += dgf
        dx2_ref[...] = dx3 + dh

    return _tok_call(body, "ffn_step", [h3, x2, tgt], [w1_t, w2, g_post, g_pre],
                     [(D, F32), (DFF, BF16), (D, BF16), (DFF, BF16)], [(8, 128), (1, D), (1, D)], vmem=VMEM_BIG)


def _mem_bwd(dx2, ym, x1, qm, km, vm, w_mo, w_mq, g_post, g_pre, after=()):
    def body(dx2_ref, ym_ref, x1_ref, q_ref, k_ref, v_ref, wo_ref, wq_ref, gp_ref, gm_ref,
             dx1_ref, dym_ref, dq_ref, dk_ref, dv_ref, dgp_ref, dgm_ref, dom_ref):
        _zero_at_start(dk_ref, dv_ref, dgp_ref, dgm_ref)
        dx2_v = dx2_ref[...]
        dym, dgp = _rms_bwd(ym_ref[...], gp_ref[...], dx2_v)
        dgp_ref[...] += dgp
        dymb = dym.astype(BF16)
        dym_ref[...] = dymb
        dom_ref[...] = _nt(dymb, _w(wo_ref)).astype(BF16)
        for h in range(MEM_HEADS):
            sl = slice(h * MEM_HD, (h + 1) * MEM_HD)
            qh, kh, doh = q_ref[:, sl], k_ref[:, sl], dom_ref[:, sl]
            s = _nt(qh, kh) * MEM_SCALE
            p = jnp.exp(s - jnp.max(s, axis=-1, keepdims=True))
            p = p / jnp.sum(p, axis=-1, keepdims=True)
            dp = _nt(doh, v_ref[:, sl])
            ds = (p * (dp - jnp.sum(p * dp, axis=-1, keepdims=True))).astype(BF16)
            dq_ref[:, sl] = (_nn(ds, kh) * MEM_SCALE).astype(BF16)
            dk_ref[:, sl] += _tn(ds, qh) * MEM_SCALE
            dv_ref[:, sl] += _tn(p.astype(BF16), doh)
        dh, dgm = _rms_bwd(x1_ref[...], gm_ref[...], _nt(dq_ref[...], _w(wq_ref)))
        dgm_ref[...] += dgm
        dx1_ref[...] = dx2_v + dh

    tiled = pl.BlockSpec((TM_WIDE, D), lambda i: (i, 0))
    in_specs = [tiled] * 4 + [_resident(a) for a in (km, vm, w_mo, w_mq, g_post, g_pre)] + [ANY_SPEC] * len(after)
    w_mo, w_mq = w_mo[0], w_mq[0]
    kv = pl.BlockSpec((NMEM, D), lambda i: (0, 0))
    vec = pl.BlockSpec((1, D), lambda i: (0, 0))
    return _pcall(
        _behind(body, 10, after), name="mem_bwd", grid=(T // TM_WIDE,), in_specs=in_specs,
        out_specs=[tiled, tiled, tiled, kv, kv, vec, vec],
        out_shape=[jax.ShapeDtypeStruct((T, D), F32), jax.ShapeDtypeStruct((T, D), BF16),
                   jax.ShapeDtypeStruct((T, D), BF16), jax.ShapeDtypeStruct((NMEM, D), F32),
                   jax.ShapeDtypeStruct((NMEM, D), F32), jax.ShapeDtypeStruct((1, D), F32),
                   jax.ShapeDtypeStruct((1, D), F32)],
        scratch_shapes=[pltpu.VMEM((TM_WIDE, D), BF16)],
        compiler_params=pltpu.CompilerParams(dimension_semantics=("arbitrary",), vmem_limit_bytes=VMEM_BIG),
    )(dx2, ym, x1, qm, km, vm, w_mo, w_mq, g_post, g_pre, *after)


def _memkv_bwd(dkm, dvm, mem, w_mk, w_mv):
    def body(dk_ref, dv_ref, m_ref, wk_ref, wv_ref, dg_ref):
        dmn = _nt(dk_ref[...].astype(BF16), _w(wk_ref)) + _nt(dv_ref[...].astype(BF16), _w(wv_ref))
        mv = m_ref[...]
        dg_ref[...] = jnp.sum(dmn * (mv * _rstd(mv)), axis=0, keepdims=True)

    return _one_call(body, "memkv_bwd", [dkm, dvm, mem, w_mk, w_mv], [((1, D), F32)], vmem=VMEM_BIG)[0]


def _postmix_bwd(dx1, z, o_f, o_c, w_out, g_post, g_fo, g_co, after=()):
    def body(dx1_ref, z_ref, of_ref, oc_ref, wo_ref, gp_ref, gfo_ref, gco_ref,
             dz_ref, dof_ref, doc_ref, dgp_ref, dgfo_ref, dgco_ref):
        _zero_at_start(dgp_ref, dgfo_ref, dgco_ref)
        dz, dgp = _rms_bwd(z_ref[...], gp_ref[...], dx1_ref[...])
        dgp_ref[...] += dgp
        dzb = dz.astype(BF16)
        dz_ref[...] = dzb
        dy = _nt(dzb, _w(wo_ref))
        dof, dgfo = _rms_bwd(of_ref[...], gfo_ref[...], dy[:, :512])
        doc, dgco = _rms_bwd(oc_ref[...], gco_ref[...], dy[:, 512:])
        dof_ref[...] = dof
        doc_ref[...] = doc
        dgfo_ref[...] += dgfo
        dgco_ref[...] += dgco

    return _tok_call(body, "postmix_bwd", [dx1, z, o_f, o_c], [w_out, g_post, g_fo, g_co],
                     [(D, BF16), (512, F32), (512, F32)], [(1, D), (1, 512), (1, 512)], tm=TM_WIDE, vmem=VMEM_BIG,
                     after=after)


def _premix_bwd(dx1, x, pieces, win_t, g_pre, after=()):
    def body(dx1_ref, x_ref, *refs):
        piece_refs, (w_ref, g_ref, dx_ref, dg_ref, dp_ref) = refs[:len(pieces)], refs[len(pieces):]
        _zero_at_start(dg_ref)
        col = 0
        for p in piece_refs:
            dp_ref[:, col:col + p.shape[1]] = p[...]
            col += p.shape[1]
        dh, dg = _rms_bwd(x_ref[...], g_ref[...], _nn(dp_ref[...], w_ref[...]))
        dg_ref[...] += dg
        dx_ref[...] = dx1_ref[...] + dh

    return _tok_call(body, "premix_bwd", [dx1, x] + list(pieces), [win_t, g_pre], [(D, F32)], [(1, D)],
                     tm=TM_WIDE, vmem=VMEM_BIG, after=after, scratch=[pltpu.VMEM((TM_WIDE, PROJ), BF16)])


def _wgrad_group(name, pairs, rows):
    def body(*refs):
        o_ref = refs[-1]
        for k in range(len(pairs)):
            g = _tn(refs[2 * k][...].astype(BF16), refs[2 * k + 1][...].astype(BF16))
            o_ref[k * rows:(k + 1) * rows, :] = g.astype(BF16)

    in_specs, ops = [], []
    for a, b in pairs:
        in_specs += [pl.BlockSpec((a.shape[0], rows), lambda j: (0, j)), _resident(b)]
        ops += [a, b]
    return _pcall(
        body, name=name, grid=(8,), in_specs=in_specs,
        out_specs=pl.BlockSpec((None, len(pairs) * rows, D), lambda j: (j, 0, 0)),
        out_shape=jax.ShapeDtypeStruct((8, len(pairs) * rows, D), BF16),
        compiler_params=pltpu.CompilerParams(dimension_semantics=("arbitrary",), vmem_limit_bytes=VMEM_BIG),
    )(*ops)


def _wgrad_whole(name, pairs):
    n = len(pairs)
    ops = [x for pair in pairs for x in pair]
    rows = pairs[0][0].shape[1] // 8

    def body(*refs):
        hbm, o_ref, bufs, sem = refs[:2 * n], refs[2 * n], refs[2 * n + 1:4 * n + 1], refs[4 * n + 1]
        k = pl.program_id(0)
        copies = [pltpu.make_async_copy(hbm[t], bufs[t], sem.at[t]) for t in range(2 * n)]

        @pl.when(k == 0)
        def _():
            for copy in copies:
                copy.start()

        for t in range(n):
            @pl.when(k == t)
            def _(t=t):
                copies[2 * t].wait()
                copies[2 * t + 1].wait()
                g = _tn(bufs[2 * t][...].astype(BF16), bufs[2 * t + 1][...].astype(BF16))
                o_ref[...] = g.reshape(8, rows, D).astype(BF16)

    return _pcall(
        body, name=name, grid=(n,), in_specs=[ANY_SPEC] * (2 * n),
        out_specs=pl.BlockSpec((8, rows, D), lambda k: (0, k, 0)),
        out_shape=jax.ShapeDtypeStruct((8, n * rows, D), BF16),
        scratch_shapes=[pltpu.VMEM(x.shape, x.dtype) for x in ops] + [pltpu.SemaphoreType.DMA((2 * n,))],
        compiler_params=pltpu.CompilerParams(dimension_semantics=("arbitrary",), vmem_limit_bytes=VMEM_BIG),
    )(*ops)


def _adam_math(w, g, m, v):
    m2 = ADAM_B1 * m + (1.0 - ADAM_B1) * g
    v2 = ADAM_B2 * v + (1.0 - ADAM_B2) * jnp.square(g)
    m_hat = m2 / (1.0 - ADAM_B1 ** ADAM_STEP)
    v_hat = v2 / (1.0 - ADAM_B2 ** ADAM_STEP)
    delta = -ADAM_LR * (m_hat / (jnp.sqrt(v_hat) + ADAM_EPS) + ADAM_WD * w)
    return delta, m2, v2


def _adamw_small(gparts, ws, ms, vs):
    n = len(SMALL)

    def body(g_ref, *refs):
        w_refs, m_refs, v_refs = refs[:n], refs[n:2 * n], refs[2 * n:3 * n]
        outs, sum_ref = refs[3 * n:-1], refs[-1]
        g = g_ref[0]
        for k in range(1, 8):
            g = g + g_ref[k]
        sum_ref[...] = g
        outs[0][...] = sum_ref[17:18, 0:128]
        for t, name in enumerate(SMALL):
            r0, nr, c0, nc = SMALL_SLOT[name]
            gt = sum_ref[r0:r0 + nr, c0:c0 + nc]
            out = (gt,) + _adam_math(w_refs[t][...], gt, m_refs[t][...], v_refs[t][...])
            for o_ref, val in zip(outs[1 + 4 * t:5 + 4 * t], out):
                o_ref[...] = val

    whole = lambda s: pl.BlockSpec(s, lambda i, nd=len(s): (0,) * nd)
    ins = [gparts] + list(ws) + list(ms) + list(vs)
    out_shapes = [(1, 128)] + [a.shape for a in ws for _ in range(4)]
    return _pcall(
        body, name="adamw_small", grid=(1,), in_specs=[whole(a.shape) for a in ins],
        out_specs=[whole(s) for s in out_shapes], out_shape=[jax.ShapeDtypeStruct(s, F32) for s in out_shapes],
        scratch_shapes=[pltpu.VMEM((SMALL_ROWS, D), F32)],
        compiler_params=pltpu.CompilerParams(dimension_semantics=("arbitrary",)),
    )(*ins)


def _row_tile(rows):
    return next(t for t in (512, 400, 320) if rows % t == 0)


def _add_halves(g4, theirs, core, name):
    rows = g4.shape[2]
    tr = _row_tile(rows)

    def body(c_ref, a_ref, b_ref, o_ref):
        o_ref[...] = (a_ref[...].astype(F32) + b_ref[...].astype(F32)).astype(BF16)

    grid_spec = pltpu.PrefetchScalarGridSpec(
        num_scalar_prefetch=1, grid=(4, rows // tr),
        in_specs=[pl.BlockSpec((None, None, tr, D), lambda j, i, c: (j, c[0], i, 0)),
                  pl.BlockSpec((None, None, tr, D), lambda j, i, c: (j, 0, i, 0))],
        out_specs=pl.BlockSpec((None, tr, D), lambda j, i, c: (j, i, 0)))
    return _pcall(
        body, name=name, grid_spec=grid_spec, out_shape=jax.ShapeDtypeStruct((4, rows, D), BF16),
        compiler_params=pltpu.CompilerParams(dimension_semantics=("arbitrary", "arbitrary")),
    )(core, g4, theirs)


def _sum_adam(own, got, order, r0, w, m, v, name, transposed=False):
    ragged = w.ndim == 3
    n = w.shape[1] if transposed else w.shape[0]
    tr = n if ragged else min(n, 128)
    rows = own.shape[1] if ragged else tr
    at = (slice(None), 0, slice(None)) if ragged else Ellipsis

    def body(o_ref, a_ref, b_ref, c_ref, d_ref, w_ref, m_ref, v_ref, *out_refs):
        f = lambda r: r[0:tr, :].astype(F32)
        g = ((f(a_ref) + f(b_ref)) + f(c_ref)) + f(d_ref)
        g = g.T if transposed else g
        for ref, val in zip(out_refs, (g,) + _adam_math(w_ref[at], g, m_ref[at], v_ref[at])):
            ref[at] = val

    slot = lambda k: pl.BlockSpec((None, rows, D), lambda i, o: (o[k], r0 // rows + i, 0))
    if ragged:
        wspec = pl.BlockSpec((n, 1, D), lambda i, o: (0, 0, 0))
    else:
        wspec = pl.BlockSpec((D, tr), lambda i, o: (0, i)) if transposed else pl.BlockSpec((tr, D), lambda i, o: (i, 0))
    grid_spec = pltpu.PrefetchScalarGridSpec(
        num_scalar_prefetch=1, grid=(n // tr,), in_specs=[slot(0), slot(1), slot(2), slot(3), wspec, wspec, wspec],
        out_specs=[wspec] * 4)
    return _pcall(
        body, name=name, grid_spec=grid_spec, out_shape=[jax.ShapeDtypeStruct(w.shape, F32)] * 4,
        compiler_params=pltpu.CompilerParams(dimension_semantics=("arbitrary",)),
    )(order, own, got, got, got, w, m, v)


def _sum_adam_rows(own, got, order, ws, ms, vs, name):
    n, rows = len(ws), ws[0].shape[0]

    def body(o_ref, a_ref, b_ref, c_ref, d_ref, *refs):
        ins, outs = refs[:3 * n], refs[3 * n:]
        for t in range(n):
            r = slice(t * rows, (t + 1) * rows)
            f = lambda ref: ref[r, :].astype(F32)
            g = ((f(a_ref) + f(b_ref)) + f(c_ref)) + f(d_ref)
            out = (g,) + _adam_math(ins[t][...], g, ins[n + t][...], ins[2 * n + t][...])
            for o, val in zip(outs[4 * t:4 * t + 4], out):
                o[...] = val

    slot = lambda k: pl.BlockSpec((None, n * rows, D), lambda i, o: (o[k], 0, 0), pipeline_mode=pl.Buffered(1))
    wspec = pl.BlockSpec((rows, D), lambda i, o: (0, 0), pipeline_mode=pl.Buffered(1))
    grid_spec = pltpu.PrefetchScalarGridSpec(
        num_scalar_prefetch=1, grid=(1,), in_specs=[slot(0), slot(1), slot(2), slot(3)] + [wspec] * (3 * n),
        out_specs=[pl.BlockSpec((rows, D), lambda i, o: (0, 0))] * (4 * n))
    return _pcall(
        body, name=name, grid_spec=grid_spec, out_shape=[jax.ShapeDtypeStruct((rows, D), F32)] * (4 * n),
        compiler_params=pltpu.CompilerParams(dimension_semantics=("arbitrary",), vmem_limit_bytes=VMEM_BIG),
    )(order, own, got, got, got, *ws, *ms, *vs)


def _place():
    return lax.axis_index("x"), lax.axis_index("y"), lax.axis_index("c")


def _allgather(block, name, after=()):
    rows = block.shape[0]
    split = (rows // 2 + 15) // 16 * 16

    def body(x_ref, out_ref, token, send_sems, recv_sems, local_sem):
        token[...] = jnp.zeros_like(token)
        x, y, c = _place()
        me, sib = (x, y, c), (x, y, 1 - c)
        xn, yn, dg = (1 - x, y), (x, 1 - y), (1 - x, 1 - y)
        lo, hi = pl.ds(0, split), pl.ds(split, rows - split)

        def copy(k, blk, to, part=None, src=None):
            index = 4 * blk[0] + 2 * blk[1] + blk[2]
            view = out_ref.at[index] if part is None else out_ref.at[index, part]
            return pltpu.make_async_remote_copy(
                src_ref=view if src is None else src, dst_ref=view,
                send_sem=send_sems.at[k], recv_sem=recv_sems.at[k], device_id=to, device_id_type=MESH)

        def start(*copies):
            for cp in copies:
                cp.start()
            return list(copies)

        mine = pltpu.make_async_copy(x_ref, out_ref.at[4 * x + 2 * y + c], local_sem)
        mine.start()
        sent = start(copy(0, me, sib, src=x_ref), copy(1, me, (*xn, c), src=x_ref), copy(2, me, (*yn, c), src=x_ref))
        copy(1, (*xn, c), me).wait_recv()
        sent += start(copy(3, (*xn, c), sib), copy(5, (*xn, c), (*yn, c), part=lo))
        copy(2, (*yn, c), me).wait_recv()
        sent += start(copy(4, (*yn, c), sib), copy(6, (*yn, c), (*xn, c), part=hi))
        copy(5, (*dg, c), me, part=lo).wait_recv()
        copy(6, (*dg, c), me, part=hi).wait_recv()
        sent += start(copy(7, (*dg, c), sib))
        for k, blk in ((0, sib), (3, (*xn, 1 - c)), (4, (*yn, 1 - c)), (7, (*dg, 1 - c))):
            copy(k, blk, me).wait_recv()
        for cp in sent:
            cp.wait_send()
        mine.wait()

    return _pcall(
        _behind(body, 1, after), name=name,
        out_shape=[jax.ShapeDtypeStruct((8,) + block.shape, block.dtype), jax.ShapeDtypeStruct((8, 128), F32)],
        in_specs=[pl.BlockSpec(memory_space=pl.ANY)] * (1 + len(after)),
        out_specs=[pl.BlockSpec(memory_space=pl.ANY), pl.BlockSpec(memory_space=pltpu.VMEM)],
        scratch_shapes=[pltpu.SemaphoreType.DMA((8,)), pltpu.SemaphoreType.DMA((8,)), pltpu.SemaphoreType.DMA(())],
        compiler_params=pltpu.CompilerParams(has_side_effects=True),
    )(block, *after)


HBM_SPEC = pl.BlockSpec(memory_space=pltpu.HBM)
SEM_SPEC = pl.BlockSpec(memory_space=pltpu.SEMAPHORE)
ANY_SPEC = pl.BlockSpec(memory_space=pl.ANY)
EFFECT = pltpu.SideEffectType.DATAFLOW_SIDE_EFFECTING


def _in_hbm(a):
    return pltpu.with_memory_space_constraint(a, pltpu.HBM)


def _start_copies(name, src, land_shape, plan, n):
    def body(src_ref, land_ref, send_sems, recv_sems, src_thru, land_thru, token):
        for k, (s, d, to, _) in enumerate(plan(src_ref, land_ref)):
            pltpu.make_async_remote_copy(src_ref=s, dst_ref=d, send_sem=send_sems.at[k], recv_sem=recv_sems.at[k],
                                         device_id=to, device_id_type=MESH).start()
        token[...] = jnp.zeros_like(token)

    return _pcall(
        body, name=name,
        out_shape=(pltpu.SemaphoreType.DMA((n,)), pltpu.SemaphoreType.DMA((n,)), pltpu.HBM(src.shape, src.dtype),
                   pltpu.HBM(land_shape, src.dtype), jax.ShapeDtypeStruct((8, 128), F32)),
        in_specs=(HBM_SPEC, HBM_SPEC),
        out_specs=(SEM_SPEC, SEM_SPEC, HBM_SPEC, HBM_SPEC, pl.BlockSpec(memory_space=pltpu.VMEM)),
        input_output_aliases={0: 2, 1: 3}, compiler_params=pltpu.CompilerParams(has_side_effects=EFFECT),
    )(_in_hbm(src), _in_hbm(lax.empty(land_shape, src.dtype)))


def _wait_copies(name, started, after, plan):
    send_sems, recv_sems, src_thru, land_thru, _ = started

    def body(src_ref, land_ref, send_sems, recv_sems, *rest):
        for k, (s, _, to, mine) in enumerate(plan(src_ref, land_ref)):
            cp = pltpu.make_async_remote_copy(src_ref=s, dst_ref=mine, send_sem=send_sems.at[k],
                                              recv_sem=recv_sems.at[k], device_id=to, device_id_type=MESH)
            cp.wait_send()
            cp.wait_recv()

    return _pcall(
        body, name=name,
        out_shape=(pltpu.HBM(src_thru.shape, src_thru.dtype), pltpu.HBM(land_thru.shape, land_thru.dtype)),
        in_specs=(HBM_SPEC, HBM_SPEC, SEM_SPEC, SEM_SPEC) + (ANY_SPEC,) * len(after), out_specs=(HBM_SPEC, HBM_SPEC),
        input_output_aliases={0: 0, 1: 1}, compiler_params=pltpu.CompilerParams(has_side_effects=EFFECT),
    )(src_thru, land_thru, send_sems, recv_sems, *after)


def _start_inplace(name, buf, plan, n):
    def body(buf_ref, send_sems, recv_sems, buf_thru, token):
        for k, (s, d, to, _) in enumerate(plan(buf_ref, buf_ref)):
            pltpu.make_async_remote_copy(src_ref=s, dst_ref=d, send_sem=send_sems.at[k], recv_sem=recv_sems.at[k],
                                         device_id=to, device_id_type=MESH).start()
        token[...] = jnp.zeros_like(token)

    return _pcall(
        body, name=name,
        out_shape=(pltpu.SemaphoreType.DMA((n,)), pltpu.SemaphoreType.DMA((n,)), pltpu.HBM(buf.shape, buf.dtype),
                   jax.ShapeDtypeStruct((8, 128), F32)),
        in_specs=(HBM_SPEC,), out_specs=(SEM_SPEC, SEM_SPEC, HBM_SPEC, pl.BlockSpec(memory_space=pltpu.VMEM)),
        input_output_aliases={0: 2}, compiler_params=pltpu.CompilerParams(has_side_effects=EFFECT),
    )(_in_hbm(buf))


def _wait_inplace(name, started, after, plan):
    send_sems, recv_sems, buf_thru, _ = started

    def body(buf_ref, send_sems, recv_sems, *rest):
        for k, (s, _, to, mine) in enumerate(plan(buf_ref, buf_ref)):
            cp = pltpu.make_async_remote_copy(src_ref=s, dst_ref=mine, send_sem=send_sems.at[k],
                                              recv_sem=recv_sems.at[k], device_id=to, device_id_type=MESH)
            cp.wait_send()
            cp.wait_recv()

    return _pcall(
        body, name=name, out_shape=pltpu.HBM(buf_thru.shape, buf_thru.dtype),
        in_specs=(HBM_SPEC, SEM_SPEC, SEM_SPEC) + (ANY_SPEC,) * len(after), out_specs=HBM_SPEC,
        input_output_aliases={0: 0}, compiler_params=pltpu.CompilerParams(has_side_effects=EFFECT),
    )(buf_thru, send_sems, recv_sems, *after)


def _gather_plan(src_ref, land_ref):
    x, y, c = _place()
    peers = [(x, y, 1 - c), (1 - x, y, c), (x, 1 - y, c)]
    return [(src_ref, land_ref.at[4 * x + 2 * y + c], p, land_ref.at[4 * p[0] + 2 * p[1] + p[2]]) for p in peers]


def _relay_plan(buf_ref, _):
    x, y, c = _place()
    slot = lambda p, pc: 4 * p[0] + 2 * p[1] + pc
    xn, yn, dg, sib = (1 - x, y), (x, 1 - y), (1 - x, 1 - y), (x, y, 1 - c)
    half = buf_ref.shape[1] // 2
    lo, hi = pl.ds(0, half), pl.ds(half, half)
    return [(buf_ref.at[slot(xn, c)], buf_ref.at[slot(xn, c)], sib, buf_ref.at[slot(xn, 1 - c)]),
            (buf_ref.at[slot(yn, c)], buf_ref.at[slot(yn, c)], sib, buf_ref.at[slot(yn, 1 - c)]),
            (buf_ref.at[slot(xn, c), lo], buf_ref.at[slot(xn, c), lo], (*yn, c), buf_ref.at[slot(dg, c), lo]),
            (buf_ref.at[slot(yn, c), hi], buf_ref.at[slot(yn, c), hi], (*xn, c), buf_ref.at[slot(dg, c), hi])]


def _swap_plan(src_ref, land_ref):
    x, y, c = _place()
    return [(src_ref.at[:, pl.ds(1 - c, 1)], land_ref, (x, y, 1 - c), land_ref)]


def _exchange_plan(src_ref, land_ref):
    x, y, c = _place()
    chips = [(1 - x, y), (x, 1 - y), (1 - x, 1 - y)]
    return [(src_ref.at[2 * px + py], land_ref.at[2 * x + y], (px, py, c), land_ref.at[2 * px + py]) for px, py in chips]


def _gather_forward(land, block):
    def body(land_ref, out_ref, send_sems, recv_sems):
        x, y, c = _place()
        chips = [(1 - x, 1 - y)]

        def copy(k, px, py, pc):
            blk = out_ref.at[4 * px + 2 * py + pc]
            return pltpu.make_async_remote_copy(src_ref=blk, dst_ref=blk, send_sem=send_sems.at[k],
                                                recv_sem=recv_sems.at[k], device_id=(x, y, 1 - c), device_id_type=MESH)

        sent = [copy(k, px, py, c) for k, (px, py) in enumerate(chips)]
        for cp in sent:
            cp.start()
        for k, (px, py) in enumerate(chips):
            copy(k, px, py, 1 - c).wait_recv()
        for cp in sent:
            cp.wait_send()

    land = _pcall(
        body, name="allgather_rest_forward", out_shape=jax.ShapeDtypeStruct(land.shape, land.dtype),
        in_specs=[ANY_SPEC], out_specs=ANY_SPEC, input_output_aliases={0: 0},
        scratch_shapes=[pltpu.SemaphoreType.DMA((1,)), pltpu.SemaphoreType.DMA((1,))],
        compiler_params=pltpu.CompilerParams(has_side_effects=True),
    )(land)

    rows = block.shape[0]
    tr = rows // 4

    def place(me_ref, x_ref, land_ref, out_ref):
        out_ref[...] = x_ref[...]

    x, y, c = _place()
    grid_spec = pltpu.PrefetchScalarGridSpec(
        num_scalar_prefetch=1, grid=(rows // tr,),
        in_specs=[pl.BlockSpec((tr, D), lambda i, me: (i, 0)), ANY_SPEC],
        out_specs=pl.BlockSpec((None, tr, D), lambda i, me: (me[0], i, 0)))
    return _pcall(
        place, name="allgather_rest_own", grid_spec=grid_spec, out_shape=jax.ShapeDtypeStruct(land.shape, land.dtype),
        input_output_aliases={2: 0}, compiler_params=pltpu.CompilerParams(dimension_semantics=("arbitrary",)),
    )((4 * x + 2 * y + c).reshape(1), block, land)


class _ReduceScatter:
    def __init__(self, name, g):
        self.name = name
        rows = g.shape[1]
        self.started = _start_copies(name + "_swap_start", g.reshape(4, 2, rows, D), (4, 1, rows, D), _swap_plan, 1)
        self.token = self.started[4]

    def halfway(self, after):
        g4, theirs = _wait_copies(self.name + "_swap_wait", self.started, after, _swap_plan)
        self.own = _add_halves(g4, theirs, lax.axis_index("c").reshape(1), self.name + "_add_halves")
        self.started = _start_copies(self.name + "_exch_start", self.own, self.own.shape, _exchange_plan, 3)
        self.token = self.started[4]

    def finish(self, after):
        own, got = _wait_copies(self.name + "_exch_wait", self.started, after, _exchange_plan)
        chip = 2 * lax.axis_index("x") + lax.axis_index("y")
        return own, got, (chip + jnp.arange(4, dtype=jnp.int32)) % 4


def _pack_small(p, loss):
    def body(*refs):
        o_ref = refs[-1]
        o_ref[...] = jnp.zeros_like(o_ref)
        for ref, name in zip(refs, SMALL):
            r0, nr, c0, nc = SMALL_SLOT[name]
            o_ref[r0:r0 + nr, c0:c0 + nc] = ref[...]
        o_ref[17:18, 0:128] = refs[len(SMALL)][0:1, :]

    return _one_call(body, "pack_small_grads", [p[n] for n in SMALL] + [loss], [((SMALL_ROWS, D), F32)])[0]


_GAP_DEV, _GAP_ROW = divmod(GATE0 + 8, N_IN)
_GAP = CHK0 - GATE0 - 8


def _in_rows_to_proj(g):
    runs = [(j, 0, N_IN, N_IN * j) for j in range(_GAP_DEV)]
    runs += [(_GAP_DEV, 0, _GAP_ROW, N_IN * _GAP_DEV), (_GAP_DEV, _GAP_ROW, N_IN, N_IN * _GAP_DEV + _GAP_ROW + _GAP)]
    runs += [(j, 0, N_IN, N_IN * j + _GAP) for j in range(_GAP_DEV + 1, 8)]

    def body(g_ref, o_ref, acc_ref):
        acc_ref[...] = jnp.zeros_like(acc_ref)
        for j, r0, r1, dest in runs:
            start, shift = dest // 16 * 16, dest % 16
            win = -(-(shift + r1 - r0) // 16) * 16
            r = lax.broadcasted_iota(jnp.int32, (win, R_IN), 0)
            c = lax.broadcasted_iota(jnp.int32, (win, R_IN), 1)
            move = jnp.where((c >= r0) & (c < r1) & (r == c - r0 + shift), 1.0, 0.0).astype(BF16)
            acc_ref[start:start + win, :] += _nn(move, g_ref[j])
        o_ref[...] = acc_ref[...].astype(BF16)

    return _pcall(
        body, name="w_in_layout", out_shape=jax.ShapeDtypeStruct((PROJ, D), BF16), grid=(1,),
        in_specs=[pl.BlockSpec(g.shape, lambda i: (0, 0, 0))], out_specs=pl.BlockSpec((PROJ, D), lambda i: (0, 0)),
        scratch_shapes=[pltpu.VMEM((PROJ, D), F32)],
        compiler_params=pltpu.CompilerParams(dimension_semantics=("arbitrary",), vmem_limit_bytes=VMEM_BIG),
    )(g)


def _wgrad_in(pieces, h1):
    n = len(pieces)
    ends = [sum(p.shape[1] for p in pieces[:k + 1]) for k in range(n)]
    assert ends[-1] == PROJ

    def body(*refs):
        piece_refs, (b_ref, o_ref, g_ref), bufs, sem = refs[:n], refs[n:n + 3], refs[n + 3:2 * n + 3], refs[2 * n + 3]
        i = pl.program_id(0)
        copies = [pltpu.make_async_copy(piece_refs[k], bufs[k], sem.at[k]) for k in range(n)]

        @pl.when(i == 0)
        def _():
            for copy in copies:
                copy.start()
            g_ref[PROJ:, :] = jnp.zeros((R_IN - N_IN + 1, D), F32)

        for k in range(n):
            @pl.when(i == k)
            def _(k=k):
                copies[k].wait()
                g_ref[ends[k] - pieces[k].shape[1]:ends[k], :] = _tn(bufs[k][...], b_ref[...])

        row = lax.broadcasted_iota(jnp.int32, (R_IN, D), 0)
        for j in range(8):
            lo = N_IN * j + (_GAP if j > _GAP_DEV else 0)
            hi = N_IN * j + (_GAP if j >= _GAP_DEV else 0)
            ready = min(k for k in range(n) if ends[k] >= min(hi + R_IN, PROJ))

            @pl.when(i == ready)
            def _(j=j, lo=lo, hi=hi):
                v = g_ref[hi:hi + R_IN, :]
                if lo != hi:
                    v = jnp.where(row < _GAP_ROW, g_ref[lo:lo + R_IN, :], v)
                o_ref[j] = jnp.where(row < N_IN, v, 0.0).astype(BF16)

    return _pcall(
        body, name="wgrad_in", grid=(n,),
        in_specs=[ANY_SPEC] * n + [_resident(h1)],
        out_specs=pl.BlockSpec((8, R_IN, D), lambda i: (0, 0, 0)),
        out_shape=jax.ShapeDtypeStruct((8, R_IN, D), BF16),
        scratch_shapes=[pltpu.VMEM((PROJ + R_IN - N_IN + 1, D), F32)] + [pltpu.VMEM(p.shape, BF16) for p in pieces]
        + [pltpu.SemaphoreType.DMA((n,))],
        compiler_params=pltpu.CompilerParams(dimension_semantics=("arbitrary",), vmem_limit_bytes=VMEM_BIG),
    )(*pieces, h1)


def _local_grads(x, mem, tgt, win_t, gw_of, sm, on_grads, after=()):
    b_pad = jnp.pad(sm['b_fgt'], ((0, 0), (0, 120)))
    tbl = jnp.pad(sm['rel_bias'], ((0, 0), (0, NREL_PAD - 257)))

    h1, proj, flog, kvp = _premix_fwd(x, sm['g_mix_pre'], win_t, after)
    c = _gate_fwd(flog, b_pad)
    ct3 = c[:, :8].T.reshape(4, 2, T)
    o_f, lse_f = _fox_fwd(proj, c, ct3)
    vt3 = _relvec_fwd(tbl).reshape(4, 2, VW)
    o_c, lse_c = _chk_fwd(proj, kvp, vt3, [gw_of('relay', [o_f])])
    gw = gw_of('done', [o_c])
    w_out, w_mq, w_mk, w_mv, w_mo, w1_t, w2 = (_wblk(gw, n) for n in ('w_out', 'w_mq', 'w_mk', 'w_mv', 'w_mo', 'w_ff1', 'w_ff2'))
    ycat, z, x1, h2, qm = _postmix_fwd(x, o_f, o_c, sm['g_fox_out'], sm['g_chk_out'], w_out,
                                       sm['g_mix_post'], sm['g_mem_pre'], w_mq)
    memn, km, vm = _memkv_fwd(mem, sm['g_mem_kv'], w_mk, w_mv)
    om, ym, x2, h3 = _mem_fwd(qm, x1, km, vm, w_mo, sm['g_mem_post'], sm['g_ff_pre'])

    gs = {}
    dx2, da, dy3, r, loss_acc, gs['g_ff_post'], gs['g_ff_pre'] = _ffn_step(h3, x2, tgt, w1_t, w2, sm['g_ff_post'],
                                                                         sm['g_ff_pre'])
    tok = on_grads('A', _wgrad_group("wgrad_ff", [(da, h3), (r, dy3)], 512), None)
    dx1, dym, dqm, dkm, dvm, gs['g_mem_post'], gs['g_mem_pre'] = _mem_bwd(
        dx2, ym, x1, qm, km, vm, w_mo, w_mq, sm['g_mem_post'], sm['g_mem_pre'], [tok])
    tok = on_grads('A halfway', None, [dx1])
    gs['g_mem_kv'] = _memkv_bwd(dkm, dvm, mem, w_mk, w_mv)
    dz, dof, doc, gs['g_mix_post'], gs['g_fox_out'], gs['g_chk_out'] = _postmix_bwd(
        dx1, z, o_f, o_c, w_out, sm['g_mix_post'], sm['g_fox_out'], sm['g_chk_out'], [tok])
    tok = on_grads('B', _wgrad_whole("wgrad_mem_out", [(ycat, dz), (h2, dqm), (memn, dkm), (memn, dvm), (om, dym)]), None)
    dq_f, dk_f, dv_f, dct, dcq = _fox_bwd(proj, c, ct3, o_f, lse_f, dof, [tok])
    tok = on_grads('B halfway', None, [dq_f])
    dq_c, dk_c, dv_c, gv = _chk_bwd(proj, kvp, vt3, o_c, lse_c, doc, [tok])
    gs['rel_bias'] = _relvec_bwd(gv.reshape(8, VW))[:, :257]
    dc = jnp.pad(dct.reshape(8, T).T + dcq[:, :, :2].transpose(1, 0, 2).reshape(T, 8), ((0, 0), (0, 120)))
    dflog, db = _gate_bwd(dc, flog, b_pad)
    gs['b_fgt'] = db[0:1, :8]
    pieces = [dq_f, dk_f, dv_f, dflog, dq_c, dk_c, dv_c]
    on_grads('C', _wgrad_in(pieces, h1), None)
    tok = on_grads('C halfway', None, [gs['g_mem_kv']])
    grad_x, gs['g_mix_pre'] = _premix_bwd(dx1, x, pieces, win_t, sm['g_mix_pre'], [tok])
    return loss_acc, grad_x, gs


def kernel(x, mem, w_in, b_fgt, rel_bias, g_fox_out, g_chk_out, w_out, g_mix_pre, g_mix_post, g_mem_kv, w_mq, w_mk, w_mv, w_mo, g_mem_pre, g_mem_post, w_ff1, w_ff2, g_ff_pre, g_ff_post, loss_target, m_w_in, m_b_fgt, m_rel_bias, m_g_fox_out, m_g_chk_out, m_w_out, m_g_mix_pre, m_g_mix_post, m_g_mem_kv, m_w_mq, m_w_mk, m_w_mv, m_w_mo, m_g_mem_pre, m_g_mem_post, m_w_ff1, m_w_ff2, m_g_ff_pre, m_g_ff_post, v_w_in, v_b_fgt, v_rel_bias, v_g_fox_out, v_g_chk_out, v_w_out, v_g_mix_pre, v_g_mix_post, v_g_mem_kv, v_w_mq, v_w_mk, v_w_mv, v_w_mo, v_g_mem_pre, v_g_mem_post, v_w_ff1, v_w_ff2, v_g_ff_pre, v_g_ff_post):
    args = dict(locals())
    two_d = lambda a: a.reshape(a.shape[-2:])
    w = {n: two_d(args[n]) for n in WEIGHTS}
    m = {n: two_d(args['m_' + n]) for n in WEIGHTS}
    v = {n: two_d(args['v_' + n]) for n in WEIGHTS}

    sm = {n: w[n] for n in SMALL}
    shard_in = jnp.pad(w['w_in'].T, ((0, R_IN - N_IN), (0, 0))).astype(BF16)
    gathered_in, zero = _allgather(shard_in, "allgather_w_in")
    win_t = _in_rows_to_proj(gathered_in)
    shard_rest = (jnp.concatenate([w['w_ff1'].T, w['w_ff2'], w['w_out'], w['w_mq'], w['w_mk'], w['w_mv'], w['w_mo']],
                                  axis=0) + zero[0, 0]).astype(BF16)
    gather = {'first': _start_copies("allgather_rest_start", shard_rest, (8, R_REST, D), _gather_plan, 3)}

    def gw_of(stage, after):
        if stage == 'relay':
            gather['block'], land = _wait_copies("allgather_rest_wait", gather['first'], after, _gather_plan)
            gather['second'] = _start_inplace("allgather_rest_relay_start", land, _relay_plan, 4)
            return gather['second'][3]
        land = _wait_inplace("allgather_rest_relay_wait", gather['second'], after, _relay_plan)
        return _gather_forward(land, gather['block'])

    rs = {}

    def on_grads(stage, g, after):
        if stage.endswith('halfway'):
            rs[stage[0]].halfway(after)
            return rs[stage[0]].token
        rs[stage] = _ReduceScatter("rs_" + stage.lower(), g)
        return rs[stage].token

    loss_local, grad_x, gs = _local_grads(x[0], mem[0], loss_target[0], win_t, gw_of, sm, on_grads, [gather['first'][4]])
    grads, deltas, new_m, new_v = {}, {}, {}, {}

    def update(n, out):
        grads[n], deltas[n], new_m[n], new_v[n] = out

    own, got, order = rs['A'].finish([grad_x, rs['C'].token])
    update('w_ff1', _sum_adam(own, got, order, 0, w['w_ff1'], m['w_ff1'], v['w_ff1'], "adamw_w_ff1", transposed=True))
    update('w_ff2', _sum_adam(own, got, order, 512, w['w_ff2'], m['w_ff2'], v['w_ff2'], "adamw_w_ff2"))
    own, got, order = rs['B'].finish([grad_x, rs['C'].token])
    names_b = ('w_out', 'w_mq', 'w_mk', 'w_mv', 'w_mo')
    done = _sum_adam_rows(own, got, order, [w[n] for n in names_b], [m[n] for n in names_b], [v[n] for n in names_b],
                          "adamw_group_b")
    for k, n in enumerate(names_b):
        update(n, done[4 * k:4 * k + 4])

    own, got, order = rs['C'].finish([new_v[n] for n in BIG if n != 'w_in'])
    rows_of = lambda a: jnp.transpose(a, (2, 0, 1))
    done = _sum_adam(own, got, order, 0, rows_of(w_in), rows_of(m_w_in), rows_of(v_w_in), "adamw_w_in")
    update('w_in', [jnp.transpose(a, (1, 2, 0)) for a in done])

    gparts, _ = _allgather(_pack_small(gs, loss_local), "allgather_small_grads", [got])
    small = _adamw_small(gparts, [w[n] for n in SMALL], [m[n] for n in SMALL], [v[n] for n in SMALL])
    loss = small[0][0, 0]
    for t, n in enumerate(SMALL):
        update(n, small[1 + 4 * t:5 + 4 * t])

    out = [loss, grad_x[None]]
    for group in (grads, deltas, new_m, new_v):
        out += [group[n].reshape(args[n].shape) for n in WEIGHTS]
    return tuple(out)
```

```python
import jax
import jax.numpy as jnp
from jax import lax
from jax.experimental import pallas as pl
from jax.experimental.pallas import tpu as pltpu

F32 = jnp.float32
BF16 = jnp.bfloat16
MESH = pl.DeviceIdType.MESH

T = 2048
D = 1024
NMEM = 256
DFF = 4096
EPS = 1e-6
TM = 256
TM_WIDE = 512
TQ = 256
FQ = 512
HD = 64
SCALE = HD ** -0.5
MEM_HEADS = 4
MEM_HD = 256
MEM_SCALE = MEM_HD ** -0.5
NEG = -1e30
LEFT = 512
WIN = LEFT + TQ
VW = 1024
NREL_PAD = 384
PROJ = 3200
GATE0 = 1536
CHK0 = 1664
VMEM_BIG = 56 * 1024 * 1024

ADAM_LR = 0.001
ADAM_B1 = 0.9
ADAM_B2 = 0.999
ADAM_EPS = 1e-08
ADAM_WD = 0.01
ADAM_STEP = 10

N_IN = 385
R_IN = 400
R_REST = 1664
W_ROWS = {'w_ff1': (0, 512), 'w_ff2': (512, 512),
          'w_out': (1024, 128), 'w_mq': (1152, 128), 'w_mk': (1280, 128), 'w_mv': (1408, 128), 'w_mo': (1536, 128)}
SMALL_ROWS = 24
SMALL_SLOT = {'rel_bias': (0, 8, 0, 257), 'b_fgt': (8, 1, 0, 8), 'g_fox_out': (9, 1, 0, 512), 'g_chk_out': (9, 1, 512, 512),
              'g_mix_pre': (10, 1, 0, 1024), 'g_mix_post': (11, 1, 0, 1024), 'g_mem_kv': (12, 1, 0, 1024),
              'g_mem_pre': (13, 1, 0, 1024), 'g_mem_post': (14, 1, 0, 1024), 'g_ff_pre': (15, 1, 0, 1024),
              'g_ff_post': (16, 1, 0, 1024)}

WEIGHTS = ['w_in', 'b_fgt', 'rel_bias', 'g_fox_out', 'g_chk_out', 'w_out', 'g_mix_pre', 'g_mix_post', 'g_mem_kv',
           'w_mq', 'w_mk', 'w_mv', 'w_mo', 'g_mem_pre', 'g_mem_post', 'w_ff1', 'w_ff2', 'g_ff_pre', 'g_ff_post']
BIG = ['w_in', 'w_out', 'w_mq', 'w_mk', 'w_mv', 'w_mo', 'w_ff1', 'w_ff2']
SMALL = [n for n in WEIGHTS if n not in BIG]


def _pcall(body, **kw):
    return pl.pallas_call(body, **kw)


def _nn(a, b):
    return jnp.dot(a, b, preferred_element_type=F32)


def _nt(a, b):
    return lax.dot_general(a, b, (((1,), (1,)), ((), ())), preferred_element_type=F32)


def _tn(a, b):
    return lax.dot_general(a, b, (((0,), (0,)), ((), ())), preferred_element_type=F32)


def _w(ref):
    v = ref[...]
    return v if v.ndim == 2 else v.reshape(-1, v.shape[-1])


def _rstd(x):
    return lax.rsqrt(jnp.mean(x * x, axis=-1, keepdims=True) + EPS)


def _rms(x, g):
    return x * _rstd(x) * g


def _rms_bwd(x, g, dy):
    r = _rstd(x)
    xh = x * r
    dg = jnp.sum(dy * xh, axis=0, keepdims=True)
    dxh = dy * g
    dx = r * (dxh - xh * jnp.mean(dxh * xh, axis=-1, keepdims=True))
    return dx, dg


def _resident(a):
    if isinstance(a, tuple):
        _, shape, index = a
        return pl.BlockSpec(shape, lambda *_: index, pipeline_mode=pl.Buffered(1))
    return pl.BlockSpec(a.shape, lambda *_, nd=a.ndim: (0,) * nd, pipeline_mode=pl.Buffered(1))


def _wblk(gw, name):
    r0, rows = W_ROWS[name]
    return (gw, (8, rows, D), (0, r0 // rows, 0))


def _behind(body, n_in, after):
    if not after:
        return body
    return lambda *refs: body(*refs[:n_in], *refs[n_in + len(after):])


def _tok_call(body, name, tiled, full, outs_tiled, outs_acc=(), rows=T, tm=TM, vmem=None, after=(), scratch=()):
    in_specs = [pl.BlockSpec((tm, a.shape[1]), lambda i: (i, 0)) for a in tiled]
    in_specs += [_resident(a) for a in full] + [ANY_SPEC] * len(after)
    full = [a[0] if isinstance(a, tuple) else a for a in full] + list(after)
    body = _behind(body, len(tiled) + len(full) - len(after), after)
    out_shape = [jax.ShapeDtypeStruct((rows, c), dt) for c, dt in outs_tiled]
    out_shape += [jax.ShapeDtypeStruct(s, F32) for s in outs_acc]
    out_specs = [pl.BlockSpec((tm, c), lambda i: (i, 0)) for c, _ in outs_tiled]
    out_specs += [pl.BlockSpec(s, lambda i, nd=len(s): (0,) * nd) for s in outs_acc]
    return _pcall(
        body, name=name, grid=(rows // tm,), in_specs=in_specs, out_specs=out_specs, out_shape=out_shape,
        scratch_shapes=list(scratch),
        compiler_params=pltpu.CompilerParams(dimension_semantics=("arbitrary",), vmem_limit_bytes=vmem),
    )(*tiled, *full)


def _one_call(body, name, ins, outs, vmem=None):
    whole = lambda s: pl.BlockSpec(s, lambda i, nd=len(s): (0,) * nd)
    return _pcall(
        body, name=name, grid=(1,), in_specs=[_resident(a) for a in ins], out_specs=[whole(s) for s, _ in outs],
        out_shape=[jax.ShapeDtypeStruct(s, dt) for s, dt in outs],
        compiler_params=pltpu.CompilerParams(dimension_semantics=("arbitrary",), vmem_limit_bytes=vmem),
    )(*[a[0] if isinstance(a, tuple) else a for a in ins])


def _premix_fwd(x, g_pre, gathered_in, after=()):
    def body(x_ref, g_ref, gin_ref, h_ref, proj_ref, flog_ref, kvp_ref, w_ref, acc_ref):
        s = pl.program_id(0)

        @pl.when(s == 0)
        def _():
            kvp_ref[...] = jnp.zeros_like(kvp_ref)
            _in_rows_to_proj(gin_ref, w_ref, acc_ref)

        @pl.when(s > 0)
        def _():
            h = _rms(x_ref[...], g_ref[...]).astype(BF16)
            h_ref[...] = h
            p = _nt(h, w_ref[...])
            proj_ref[...] = p.astype(BF16)
            flog_ref[...] = p[:, GATE0:GATE0 + 128]
            kvp_ref[...] = p[:, CHK0 + 512:].astype(BF16)

    tile = lambda c: pl.BlockSpec((LEFT, c), lambda s: (jnp.maximum(s - 1, 0), 0))
    return _pcall(
        _behind(body, 3, after), name="premix_fwd", grid=(T // LEFT + 1,),
        in_specs=[tile(D), _resident(g_pre), _resident(gathered_in)] + [ANY_SPEC] * len(after),
        out_specs=[tile(D), tile(PROJ), tile(128), pl.BlockSpec((LEFT, 1024), lambda s: (s, 0)),
                   pl.BlockSpec((PROJ, D), lambda s: (0, 0))],
        out_shape=[jax.ShapeDtypeStruct((T, D), BF16), jax.ShapeDtypeStruct((T, PROJ), BF16),
                   jax.ShapeDtypeStruct((T, 128), F32), jax.ShapeDtypeStruct((T + LEFT, 1024), BF16),
                   jax.ShapeDtypeStruct((PROJ, D), BF16)],
        scratch_shapes=[pltpu.VMEM((PROJ, D), F32)],
        compiler_params=pltpu.CompilerParams(dimension_semantics=("arbitrary",), vmem_limit_bytes=VMEM_BIG),
    )(x, g_pre, gathered_in, *after)


def _postmix_fwd(x, o_f, o_c, g_fo, g_co, w_out, g_post, g_mpre, w_mq):
    def body(x_ref, of_ref, oc_ref, gfo_ref, gco_ref, wo_ref, gp_ref, gm_ref, wq_ref,
             y_ref, z_ref, x1_ref, h2_ref, qm_ref):
        y_ref[:, :512] = _rms(of_ref[...], gfo_ref[...]).astype(BF16)
        y_ref[:, 512:] = _rms(oc_ref[...], gco_ref[...]).astype(BF16)
        z = _nn(y_ref[...], _w(wo_ref))
        z_ref[...] = z
        x1 = x_ref[...] + _rms(z, gp_ref[...])
        x1_ref[...] = x1
        h2 = _rms(x1, gm_ref[...]).astype(BF16)
        h2_ref[...] = h2
        qm_ref[...] = _nn(h2, _w(wq_ref)).astype(BF16)

    return _tok_call(body, "postmix_fwd", [x, o_f, o_c], [g_fo, g_co, w_out, g_post, g_mpre, w_mq],
                     [(D, BF16), (D, F32), (D, F32), (D, BF16), (D, BF16)], tm=TM_WIDE, vmem=VMEM_BIG)


def _memkv_fwd(mem, g_kv, w_mk, w_mv):
    def body(m_ref, g_ref, wk_ref, wv_ref, mn_ref, k_ref, v_ref):
        mn = _rms(m_ref[...], g_ref[...]).astype(BF16)
        mn_ref[...] = mn
        k_ref[...] = _nn(mn, _w(wk_ref)).astype(BF16)
        v_ref[...] = _nn(mn, _w(wv_ref)).astype(BF16)

    return _tok_call(body, "memkv_fwd", [mem], [g_kv, w_mk, w_mv],
                     [(D, BF16), (D, BF16), (D, BF16)], rows=NMEM, tm=NMEM, vmem=VMEM_BIG)


def _mem_fwd(qm, x1, km, vm, w_mo, g_post, g_fpre):
    def body(q_ref, x1_ref, k_ref, v_ref, wo_ref, gp_ref, gf_ref, om_ref, ym_ref, x2_ref, h3_ref):
        for h in range(MEM_HEADS):
            sl = slice(h * MEM_HD, (h + 1) * MEM_HD)
            s = _nt(q_ref[:, sl], k_ref[:, sl]) * MEM_SCALE
            p = jnp.exp(s - jnp.max(s, axis=-1, keepdims=True))
            p = p / jnp.sum(p, axis=-1, keepdims=True)
            om_ref[:, sl] = _nn(p.astype(BF16), v_ref[:, sl]).astype(BF16)
        ym = _nn(om_ref[...], _w(wo_ref))
        ym_ref[...] = ym
        x2 = x1_ref[...] + _rms(ym, gp_ref[...])
        x2_ref[...] = x2
        h3_ref[...] = _rms(x2, gf_ref[...]).astype(BF16)

    return _tok_call(body, "mem_fwd", [qm, x1], [km, vm, w_mo, g_post, g_fpre],
                     [(D, BF16), (D, F32), (D, F32), (D, BF16)], tm=TM_WIDE, vmem=VMEM_BIG)


def _tri(lower):
    r = lax.broadcasted_iota(jnp.int32, (128, 128), 0)
    c = lax.broadcasted_iota(jnp.int32, (128, 128), 1)
    return jnp.where(r >= c if lower else c >= r, 1.0, 0.0).astype(F32)


def _hdot(a, b):
    return jnp.dot(a, b, preferred_element_type=F32, precision=lax.Precision.HIGHEST)


def _gate_fwd(flog, b_pad):
    def body(f_ref, b_ref, c_ref):
        tri = _tri(True)

        def step(i, carry):
            rows = pl.ds(pl.multiple_of(i * 128, 128), 128)
            z = f_ref[rows, :] + b_ref[...]
            lf = jnp.minimum(z, 0.0) - jnp.log(1.0 + jnp.exp(-jnp.abs(z)))
            cb = _hdot(tri, lf) + carry
            c_ref[rows, :] = cb
            return cb[127:128, :]

        lax.fori_loop(0, T // 128, step, jnp.zeros((1, 128), F32))

    return _one_call(body, "gate_fwd", [flog, b_pad], [((T, 128), F32)])[0]


def _gate_bwd(dc, flog, b_pad):
    def body(dc_ref, f_ref, b_ref, df_ref, db_ref):
        tri = _tri(False)

        def step(j, carry):
            run, db = carry
            i = T // 128 - 1 - j
            rows = pl.ds(pl.multiple_of(i * 128, 128), 128)
            dcb = dc_ref[rows, :]
            rb = _hdot(tri, dcb) + run
            z = f_ref[rows, :] + b_ref[...]
            df = rb * (1.0 / (1.0 + jnp.exp(z)))
            df_ref[rows, :] = df.astype(BF16)
            return run + jnp.sum(dcb, axis=0, keepdims=True), db + jnp.sum(df, axis=0, keepdims=True)

        _, db = lax.fori_loop(0, T // 128, step, (jnp.zeros((1, 128), F32), jnp.zeros((1, 128), F32)))
        db_ref[...] = jnp.broadcast_to(db, (8, 128))

    return _one_call(body, "gate_bwd", [dc, flog, b_pad], [((T, 128), BF16), ((8, 128), F32)])


def _lane_lo(rows=TQ):
    return lax.broadcasted_iota(jnp.int32, (rows, 128), 1) < HD


def _half(v, lo, a, scale=None):
    keep = lo if a == 0 else jnp.logical_not(lo)
    v = v.astype(F32) if scale is None else v.astype(F32) * scale
    return jnp.where(keep, v, 0.0).astype(BF16)


def _fox_specs():
    return [pl.BlockSpec((FQ, 128), lambda h, i: (i, h)),
            pl.BlockSpec((T, 128), lambda h, i: (0, 4 + h)),
            pl.BlockSpec((T, 128), lambda h, i: (0, 8 + h))]


def _lane_pick(x, at):
    lane = lax.broadcasted_iota(jnp.int32, x.shape, 1)
    return jnp.sum(jnp.where(lane == at, x, 0.0), axis=-1, keepdims=True)


def _fox_fwd(proj, c, ct3):
    def body(q_ref, k_ref, v_ref, c_ref, ct_ref, o_ref, l_ref):
        i = pl.program_id(1)
        lo = _lane_lo(FQ)
        causal = lax.broadcasted_iota(jnp.int32, (FQ, FQ), 1) <= lax.broadcasted_iota(jnp.int32, (FQ, FQ), 0)
        q = q_ref[...]
        qs = [_half(q, lo, a, SCALE) for a in range(2)]
        cqs = [_lane_pick(c_ref[...], 2 * pl.program_id(0) + a) for a in range(2)]

        def tile(off, carry, diagonal):
            kblk = k_ref[pl.ds(off, FQ), :]
            vblk = v_ref[pl.ds(off, FQ), :]
            new = []
            for a in range(2):
                m, l, acc = carry[a]
                s = _nt(qs[a], kblk) + (cqs[a] - ct_ref[a:a + 1, pl.ds(off, FQ)])
                if diagonal:
                    s = jnp.where(causal, s, NEG)
                m2 = jnp.maximum(m, jnp.max(s, axis=-1, keepdims=True))
                p = jnp.exp(s - m2)
                alpha = jnp.exp(m - m2)
                new.append((m2, alpha * l + jnp.sum(p, axis=-1, keepdims=True),
                            alpha * acc + _nn(p.astype(BF16), vblk)))
            return tuple(new)

        init = (jnp.full((FQ, 1), NEG, F32), jnp.zeros((FQ, 1), F32), jnp.zeros((FQ, 128), F32))
        carry = lax.fori_loop(0, i, lambda kb, c: tile(pl.multiple_of(kb * FQ, FQ), c, False), (init, init))
        carry = tile(pl.multiple_of(i * FQ, FQ), carry, True)
        outs = []
        for a in range(2):
            m, l, acc = carry[a]
            outs.append(acc / l)
            l_ref[:, 128 * a:128 * a + 128] = jnp.broadcast_to(m + jnp.log(l), (FQ, 128))
        o_ref[...] = jnp.where(lo, outs[0], outs[1])

    return _pcall(
        body, name="fox_fwd", grid=(4, T // FQ),
        in_specs=_fox_specs() + [pl.BlockSpec((FQ, 128), lambda h, i: (i, 0)),
                                 pl.BlockSpec((None, 2, T), lambda h, i: (h, 0, 0))],
        out_specs=[pl.BlockSpec((FQ, 128), lambda h, i: (i, h)), pl.BlockSpec((FQ, 256), lambda h, i: (i, h))],
        out_shape=[jax.ShapeDtypeStruct((T, 512), F32), jax.ShapeDtypeStruct((T, 1024), F32)],
        compiler_params=pltpu.CompilerParams(dimension_semantics=("arbitrary", "arbitrary"), vmem_limit_bytes=VMEM_BIG),
    )(proj, proj, proj, c, ct3)


def _fox_bwd(proj, c, ct3, o, lse, do, after=()):
    def body(q_ref, k_ref, v_ref, c_ref, ct_ref, o_ref, l_ref, do_ref, dq_ref, dkb_ref, dvb_ref, dct_ref, dcq_ref,
             dk_ref, dv_ref):
        i = pl.program_id(1)

        @pl.when(i == 0)
        def _():
            dk_ref[...] = jnp.zeros_like(dk_ref)
            dv_ref[...] = jnp.zeros_like(dv_ref)
            dct_ref[...] = jnp.zeros_like(dct_ref)

        lo = _lane_lo(FQ)
        causal = lax.broadcasted_iota(jnp.int32, (FQ, FQ), 1) <= lax.broadcasted_iota(jnp.int32, (FQ, FQ), 0)
        q = q_ref[...]
        do_v = do_ref[...]
        prod = do_v * o_ref[...]
        qs = [_half(q, lo, a, SCALE) for a in range(2)]
        dos = [_half(do_v, lo, a) for a in range(2)]
        deltas = [jnp.sum(jnp.where(lo if a == 0 else jnp.logical_not(lo), prod, 0.0), axis=-1, keepdims=True)
                  for a in range(2)]
        cqs = [_lane_pick(c_ref[...], 2 * pl.program_id(0) + a) for a in range(2)]
        las = [l_ref[:, 128 * a:128 * a + 1] for a in range(2)]

        def tile(off, carry, diagonal):
            kblk = k_ref[pl.ds(off, FQ), :]
            vblk = v_ref[pl.ds(off, FQ), :]
            new = []
            dk = jnp.zeros((128, FQ), F32)
            dv = jnp.zeros((128, FQ), F32)
            for a in range(2):
                dq_acc, rs = carry[a]
                s = _nt(qs[a], kblk) + (cqs[a] - ct_ref[a:a + 1, pl.ds(off, FQ)])
                if diagonal:
                    s = jnp.where(causal, s, NEG)
                p = jnp.exp(s - las[a])
                ds = p * (_nt(dos[a], vblk) - deltas[a])
                dsb = ds.astype(BF16)
                dk = dk + _tn(qs[a], dsb)
                dv = dv + _tn(dos[a], p.astype(BF16))
                dct_ref[a:a + 1, pl.ds(off, FQ)] -= jnp.sum(ds, axis=0, keepdims=True)
                new.append((dq_acc + _nn(dsb, kblk), rs + jnp.sum(ds, axis=-1, keepdims=True)))
            dk_ref[:, pl.ds(off, FQ)] += dk
            dv_ref[:, pl.ds(off, FQ)] += dv
            return tuple(new)

        init = (jnp.zeros((FQ, 128), F32), jnp.zeros((FQ, 1), F32))
        carry = lax.fori_loop(0, i, lambda kb, c: tile(pl.multiple_of(kb * FQ, FQ), c, False), (init, init))
        carry = tile(pl.multiple_of(i * FQ, FQ), carry, True)
        lane = lax.broadcasted_iota(jnp.int32, (FQ, 128), 1)
        dcq_ref[...] = jnp.where(lane == 0, carry[0][1], jnp.where(lane == 1, carry[1][1], 0.0))
        dq_ref[...] = (jnp.where(lo, carry[0][0], carry[1][0]) * SCALE).astype(BF16)

        @pl.when(i == T // FQ - 1)
        def _():
            dkb_ref[...] = dk_ref[...].T.astype(BF16)
            dvb_ref[...] = dv_ref[...].T.astype(BF16)

    blk = pl.BlockSpec((FQ, 128), lambda h, i: (i, h))
    wide = pl.BlockSpec((FQ, 256), lambda h, i: (i, h))
    rows = pl.BlockSpec((None, 2, T), lambda h, i: (h, 0, 0))
    col = pl.BlockSpec((T, 128), lambda h, i: (0, h))
    return _pcall(
        _behind(body, 8, after), name="fox_bwd", grid=(4, T // FQ),
        in_specs=_fox_specs() + [pl.BlockSpec((FQ, 128), lambda h, i: (i, 0)), rows, blk, wide, blk] + [ANY_SPEC] * len(after),
        out_specs=[blk, col, col, rows, pl.BlockSpec((None, FQ, 128), lambda h, i: (h, i, 0))],
        out_shape=[jax.ShapeDtypeStruct((T, 512), BF16), jax.ShapeDtypeStruct((T, 512), BF16),
                   jax.ShapeDtypeStruct((T, 512), BF16), jax.ShapeDtypeStruct((4, 2, T), F32),
                   jax.ShapeDtypeStruct((4, T, 128), F32)],
        scratch_shapes=[pltpu.VMEM((128, T), F32), pltpu.VMEM((128, T), F32)],
        compiler_params=pltpu.CompilerParams(dimension_semantics=("arbitrary", "arbitrary"), vmem_limit_bytes=VMEM_BIG),
    )(proj, proj, proj, c, ct3, o, lse, do, *after)


def _rel_onehot():
    ridx = lax.broadcasted_iota(jnp.int32, (NREL_PAD, VW), 0)
    j = lax.broadcasted_iota(jnp.int32, (NREL_PAD, VW), 1)
    return jnp.where(ridx == jnp.clip(TQ + LEFT - 1 - j, -128, 128) + 128, 1.0, 0.0).astype(F32)


def _relvec_fwd(tbl):
    def body(t_ref, v_ref):
        v_ref[...] = _hdot(t_ref[...], _rel_onehot())

    return _one_call(body, "relvec_fwd", [tbl], [((8, VW), F32)])[0]


def _relvec_bwd(gv):
    def body(g_ref, t_ref):
        t_ref[...] = lax.dot_general(g_ref[...], _rel_onehot(), (((1,), (1,)), ((), ())),
                                     preferred_element_type=F32, precision=lax.Precision.HIGHEST)

    return _one_call(body, "relvec_bwd", [gv], [((8, NREL_PAD), F32)])[0]


def _chk_bias(vt_ref, a, hidden):
    vb = jnp.broadcast_to(vt_ref[a:a + 1, :], (TQ, VW))
    y = pltpu.roll(vb, VW - (TQ - 1), 1, stride=1, stride_axis=0)[:, :WIN]
    cr = lax.broadcasted_iota(jnp.int32, (TQ, WIN), 0) // 64
    m = lax.broadcasted_iota(jnp.int32, (TQ, WIN), 1)
    return jnp.where((m // 64 >= cr) & (m // 64 <= cr + 8) & (m >= hidden), y, NEG)


def _chk_specs():
    return [pl.BlockSpec((TQ, 128), lambda h, i: (i, CHK0 // 128 + h)),
            pl.BlockSpec((T + LEFT, 128), lambda h, i: (0, h)),
            pl.BlockSpec((T + LEFT, 128), lambda h, i: (0, 4 + h)),
            pl.BlockSpec((None, 2, VW), lambda h, i: (h, 0, 0))]


def _chk_fwd(proj, kvp, vt3, after=()):
    def body(q_ref, k_ref, v_ref, vt_ref, o_ref, l_ref, bias_ref):
        i = pl.program_id(1)

        @pl.when(i == 0)
        def _():
            for first in range(3):
                for a in range(2):
                    bias_ref[first, a] = _chk_bias(vt_ref, a, max(LEFT - first * TQ, 0))

        lo = _lane_lo()
        off = pl.multiple_of(i * TQ, TQ)
        kw = k_ref[pl.ds(off, WIN), :]
        vw = v_ref[pl.ds(off, WIN), :]
        bias_at = jnp.minimum(i, 2)
        q = q_ref[...]
        outs = []
        for a in range(2):
            s = _nt(_half(q, lo, a, SCALE), kw) + bias_ref[bias_at, a]
            m = jnp.max(s, axis=-1, keepdims=True)
            p = jnp.exp(s - m)
            l = jnp.sum(p, axis=-1, keepdims=True)
            outs.append(_nn(p.astype(BF16), vw) / l)
            l_ref[:, 128 * a:128 * a + 128] = jnp.broadcast_to(m + jnp.log(l), (TQ, 128))
        o_ref[...] = jnp.where(lo, outs[0], outs[1])

    return _pcall(
        _behind(body, 4, after), name="chk_fwd", grid=(4, T // TQ), in_specs=_chk_specs() + [ANY_SPEC] * len(after),
        out_specs=[pl.BlockSpec((TQ, 128), lambda h, i: (i, h)), pl.BlockSpec((TQ, 256), lambda h, i: (i, h))],
        out_shape=[jax.ShapeDtypeStruct((T, 512), F32), jax.ShapeDtypeStruct((T, 1024), F32)],
        scratch_shapes=[pltpu.VMEM((3, 2, TQ, WIN), F32)],
        compiler_params=pltpu.CompilerParams(dimension_semantics=("arbitrary", "arbitrary")),
    )(proj, kvp, kvp, vt3, *after)


def _chk_bwd(proj, kvp, vt3, o, lse, do, after=()):
    nq = T // TQ

    def body(q_ref, k_ref, v_ref, vt_ref, o_ref, l_ref, do_ref, dq_ref, dkb_ref, dvb_ref, gv_ref, bias_ref, dsum_ref,
             dk_ref, dv_ref):
        i = pl.program_id(1)

        @pl.when(i == 0)
        def _():
            for first in range(3):
                for a in range(2):
                    bias_ref[first, a] = _chk_bias(vt_ref, a, max(LEFT - first * TQ, 0))
            dsum_ref[...] = jnp.zeros_like(dsum_ref)
            dk_ref[...] = jnp.zeros_like(dk_ref)
            dv_ref[...] = jnp.zeros_like(dv_ref)

        lo = _lane_lo()
        off = pl.multiple_of(i * TQ, TQ)
        kw = k_ref[pl.ds(off, WIN), :]
        vw = v_ref[pl.ds(off, WIN), :]
        bias_at = jnp.minimum(i, 2)
        q = q_ref[...]
        do_v = do_ref[...]
        prod = do_v * o_ref[...]
        dqs = []
        for a in range(2):
            keep = lo if a == 0 else jnp.logical_not(lo)
            qa = _half(q, lo, a, SCALE)
            doa = _half(do_v, lo, a)
            delta = jnp.sum(jnp.where(keep, prod, 0.0), axis=-1, keepdims=True)
            s = _nt(qa, kw) + bias_ref[bias_at, a]
            p = jnp.exp(s - l_ref[:, 128 * a:128 * a + 1])
            ds = p * (_nt(doa, vw) - delta)
            dsum_ref[a] += ds
            dsb = ds.astype(BF16)
            dk_ref[:, pl.ds(off, WIN)] += _tn(qa, dsb)
            dv_ref[:, pl.ds(off, WIN)] += _tn(doa, p.astype(BF16))
            dqs.append(_nn(dsb, kw))
        dq_ref[...] = (jnp.where(lo, dqs[0], dqs[1]) * SCALE).astype(BF16)

        @pl.when(i == nq - 1)
        def _():
            dkb_ref[...] = dk_ref[:, LEFT:].T.astype(BF16)
            dvb_ref[...] = dv_ref[:, LEFT:].T.astype(BF16)
            rr = lax.broadcasted_iota(jnp.int32, (TQ, TQ), 0)
            cc = lax.broadcasted_iota(jnp.int32, (TQ, TQ), 1)
            flip = jnp.where(rr + cc == TQ - 1, 1.0, 0.0).astype(F32)
            for a in range(2):
                dpad = jnp.concatenate([dsum_ref[a], jnp.zeros((TQ, VW - WIN), F32)], axis=1)
                z = pltpu.roll(_hdot(flip, dpad), 0, 1, stride=1, stride_axis=0)
                gv_ref[a:a + 1, :] = jnp.sum(z, axis=0, keepdims=True)

    blk = pl.BlockSpec((TQ, 128), lambda h, i: (i, h))
    wide = pl.BlockSpec((TQ, 256), lambda h, i: (i, h))
    col = pl.BlockSpec((T, 128), lambda h, i: (0, h))
    return _pcall(
        _behind(body, 7, after), name="chk_bwd", grid=(4, nq), in_specs=_chk_specs() + [blk, wide, blk] + [ANY_SPEC] * len(after),
        out_specs=[blk, col, col, pl.BlockSpec((None, 2, VW), lambda h, i: (h, 0, 0))],
        out_shape=[jax.ShapeDtypeStruct((T, 512), BF16), jax.ShapeDtypeStruct((T, 512), BF16),
                   jax.ShapeDtypeStruct((T, 512), BF16), jax.ShapeDtypeStruct((4, 2, VW), F32)],
        scratch_shapes=[pltpu.VMEM((3, 2, TQ, WIN), F32), pltpu.VMEM((2, TQ, WIN), F32),
                        pltpu.VMEM((128, T + LEFT), F32), pltpu.VMEM((128, T + LEFT), F32)],
        compiler_params=pltpu.CompilerParams(dimension_semantics=("arbitrary", "arbitrary")),
    )(proj, kvp, kvp, vt3, o, lse, do, *after)


def _zero_at_start(*refs):
    @pl.when(pl.program_id(0) == 0)
    def _():
        for r in refs:
            r[...] = jnp.zeros_like(r)


def _ffn_step(h3, x2, tgt, w1_t, w2, g_post, g_pre):
    def body(h_ref, x2_ref, t_ref, w1_ref, w2_ref, gp_ref, gf_ref, dx2_ref, da_ref, dy_ref, r_ref, loss_ref, dgp_ref, dgf_ref):
        _zero_at_start(loss_ref, dgp_ref, dgf_ref)
        w1, w2v = _w(w1_ref), _w(w2_ref)
        ra = jnp.maximum(_nt(h_ref[...], w1), 0.0)
        r = jnp.square(ra).astype(BF16)
        r_ref[...] = r
        y = _nn(r, w2v)
        x2v = x2_ref[...]
        e = x2v + _rms(y, gp_ref[...]) - t_ref[...]
        loss_ref[...] += 0.5 * jnp.sum(jnp.sum(e * e, axis=-1, keepdims=True) * (1.0 / D))
        dx3 = e * (1.0 / D)
        dy, dgp = _rms_bwd(y, gp_ref[...], dx3)
        dgp_ref[...] += dgp
        dyb = dy.astype(BF16)
        dy_ref[...] = dyb
        da = (_nt(dyb, w2v) * (2.0 * ra)).astype(BF16)
        da_ref[...] = da
        dh, dgf = _rms_bwd(x2v, gf_ref[...], _nn(da, w1))
        dgf_ref[...] += dgf
        dx2_ref[...] = dx3 + dh

    return _tok_call(body, "ffn_step", [h3, x2, tgt], [w1_t, w2, g_post, g_pre],
                     [(D, F32), (DFF, BF16), (D, BF16), (DFF, BF16)], [(8, 128), (1, D), (1, D)], vmem=VMEM_BIG)


def _mem_bwd(dx2, ym, x1, qm, km, vm, w_mo, w_mq, g_post, g_pre, after=()):
    def body(dx2_ref, ym_ref, x1_ref, q_ref, k_ref, v_ref, wo_ref, wq_ref, gp_ref, gm_ref,
             dx1_ref, dym_ref, dq_ref, dk_ref, dv_ref, dgp_ref, dgm_ref, dom_ref):
        _zero_at_start(dk_ref, dv_ref, dgp_ref, dgm_ref)
        dx2_v = dx2_ref[...]
        dym, dgp = _rms_bwd(ym_ref[...], gp_ref[...], dx2_v)
        dgp_ref[...] += dgp
        dymb = dym.astype(BF16)
        dym_ref[...] = dymb
        dom_ref[...] = _nt(dymb, _w(wo_ref)).astype(BF16)
        for h in range(MEM_HEADS):
            sl = slice(h * MEM_HD, (h + 1) * MEM_HD)
            qh, kh, doh = q_ref[:, sl], k_ref[:, sl], dom_ref[:, sl]
            s = _nt(qh, kh) * MEM_SCALE
            p = jnp.exp(s - jnp.max(s, axis=-1, keepdims=True))
            p = p / jnp.sum(p, axis=-1, keepdims=True)
            dp = _nt(doh, v_ref[:, sl])
            ds = (p * (dp - jnp.sum(p * dp, axis=-1, keepdims=True))).astype(BF16)
            dq_ref[:, sl] = (_nn(ds, kh) * MEM_SCALE).astype(BF16)
            dk_ref[:, sl] += _tn(ds, qh) * MEM_SCALE
            dv_ref[:, sl] += _tn(p.astype(BF16), doh)
        dh, dgm = _rms_bwd(x1_ref[...], gm_ref[...], _nt(dq_ref[...], _w(wq_ref)))
        dgm_ref[...] += dgm
        dx1_ref[...] = dx2_v + dh

    tiled = pl.BlockSpec((TM_WIDE, D), lambda i: (i, 0))
    in_specs = [tiled] * 4 + [_resident(a) for a in (km, vm, w_mo, w_mq, g_post, g_pre)] + [ANY_SPEC] * len(after)
    w_mo, w_mq = w_mo[0], w_mq[0]
    kv = pl.BlockSpec((NMEM, D), lambda i: (0, 0))
    vec = pl.BlockSpec((1, D), lambda i: (0, 0))
    return _pcall(
        _behind(body, 10, after), name="mem_bwd", grid=(T // TM_WIDE,), in_specs=in_specs,
        out_specs=[tiled, tiled, tiled, kv, kv, vec, vec],
        out_shape=[jax.ShapeDtypeStruct((T, D), F32), jax.ShapeDtypeStruct((T, D), BF16),
                   jax.ShapeDtypeStruct((T, D), BF16), jax.ShapeDtypeStruct((NMEM, D), F32),
                   jax.ShapeDtypeStruct((NMEM, D), F32), jax.ShapeDtypeStruct((1, D), F32),
                   jax.ShapeDtypeStruct((1, D), F32)],
        scratch_shapes=[pltpu.VMEM((TM_WIDE, D), BF16)],
        compiler_params=pltpu.CompilerParams(dimension_semantics=("arbitrary",), vmem_limit_bytes=VMEM_BIG),
    )(dx2, ym, x1, qm, km, vm, w_mo, w_mq, g_post, g_pre, *after)


def _memkv_bwd(dkm, dvm, mem, w_mk, w_mv):
    def body(dk_ref, dv_ref, m_ref, wk_ref, wv_ref, dg_ref):
        dmn = _nt(dk_ref[...].astype(BF16), _w(wk_ref)) + _nt(dv_ref[...].astype(BF16), _w(wv_ref))
        mv = m_ref[...]
        dg_ref[...] = jnp.sum(dmn * (mv * _rstd(mv)), axis=0, keepdims=True)

    return _one_call(body, "memkv_bwd", [dkm, dvm, mem, w_mk, w_mv], [((1, D), F32)], vmem=VMEM_BIG)[0]


def _postmix_bwd(dx1, z, o_f, o_c, w_out, g_post, g_fo, g_co, after=()):
    def body(dx1_ref, z_ref, of_ref, oc_ref, wo_ref, gp_ref, gfo_ref, gco_ref,
             dz_ref, dof_ref, doc_ref, dgp_ref, dgfo_ref, dgco_ref):
        _zero_at_start(dgp_ref, dgfo_ref, dgco_ref)
        dz, dgp = _rms_bwd(z_ref[...], gp_ref[...], dx1_ref[...])
        dgp_ref[...] += dgp
        dzb = dz.astype(BF16)
        dz_ref[...] = dzb
        dy = _nt(dzb, _w(wo_ref))
        dof, dgfo = _rms_bwd(of_ref[...], gfo_ref[...], dy[:, :512])
        doc, dgco = _rms_bwd(oc_ref[...], gco_ref[...], dy[:, 512:])
        dof_ref[...] = dof
        doc_ref[...] = doc
        dgfo_ref[...] += dgfo
        dgco_ref[...] += dgco

    return _tok_call(body, "postmix_bwd", [dx1, z, o_f, o_c], [w_out, g_post, g_fo, g_co],
                     [(D, BF16), (512, F32), (512, F32)], [(1, D), (1, 512), (1, 512)], tm=TM_WIDE, vmem=VMEM_BIG,
                     after=after)


def _premix_bwd(dx1, x, pieces, win_t, g_pre, after=()):
    def body(dx1_ref, x_ref, *refs):
        piece_refs, (w_ref, g_ref, dx_ref, dg_ref, dp_ref) = refs[:len(pieces)], refs[len(pieces):]
        _zero_at_start(dg_ref)
        col = 0
        for p in piece_refs:
            dp_ref[:, col:col + p.shape[1]] = p[...]
            col += p.shape[1]
        dh, dg = _rms_bwd(x_ref[...], g_ref[...], _nn(dp_ref[...], w_ref[...]))
        dg_ref[...] += dg
        dx_ref[...] = dx1_ref[...] + dh

    return _tok_call(body, "premix_bwd", [dx1, x] + list(pieces), [win_t, g_pre], [(D, F32)], [(1, D)],
                     tm=TM_WIDE, vmem=VMEM_BIG, after=after, scratch=[pltpu.VMEM((TM_WIDE, PROJ), BF16)])


def _wgrad_group(name, pairs, rows):
    def body(*refs):
        o_ref = refs[-1]
        for k in range(len(pairs)):
            g = _tn(refs[2 * k][...].astype(BF16), refs[2 * k + 1][...].astype(BF16))
            o_ref[k * rows:(k + 1) * rows, :] = g.astype(BF16)

    in_specs, ops = [], []
    for a, b in pairs:
        in_specs += [pl.BlockSpec((a.shape[0], rows), lambda j: (0, j)), _resident(b)]
        ops += [a, b]
    return _pcall(
        body, name=name, grid=(8,), in_specs=in_specs,
        out_specs=pl.BlockSpec((None, len(pairs) * rows, D), lambda j: (j, 0, 0)),
        out_shape=jax.ShapeDtypeStruct((8, len(pairs) * rows, D), BF16),
        compiler_params=pltpu.CompilerParams(dimension_semantics=("arbitrary",), vmem_limit_bytes=VMEM_BIG),
    )(*ops)


def _wgrad_whole(name, pairs):
    n = len(pairs)
    ops = [x for pair in pairs for x in pair]
    rows = pairs[0][0].shape[1] // 8

    def body(*refs):
        hbm, o_ref, bufs, sem = refs[:2 * n], refs[2 * n], refs[2 * n + 1:4 * n + 1], refs[4 * n + 1]
        k = pl.program_id(0)
        copies = [pltpu.make_async_copy(hbm[t], bufs[t], sem.at[t]) for t in range(2 * n)]

        @pl.when(k == 0)
        def _():
            for copy in copies:
                copy.start()

        for t in range(n):
            @pl.when(k == t)
            def _(t=t):
                copies[2 * t].wait()
                copies[2 * t + 1].wait()
                g = _tn(bufs[2 * t][...].astype(BF16), bufs[2 * t + 1][...].astype(BF16))
                o_ref[...] = g.reshape(8, rows, D).astype(BF16)

    return _pcall(
        body, name=name, grid=(n,), in_specs=[ANY_SPEC] * (2 * n),
        out_specs=pl.BlockSpec((8, rows, D), lambda k: (0, k, 0)),
        out_shape=jax.ShapeDtypeStruct((8, n * rows, D), BF16),
        scratch_shapes=[pltpu.VMEM(x.shape, x.dtype) for x in ops] + [pltpu.SemaphoreType.DMA((2 * n,))],
        compiler_params=pltpu.CompilerParams(dimension_semantics=("arbitrary",), vmem_limit_bytes=VMEM_BIG),
    )(*ops)


def _adam_math(w, g, m, v):
    m2 = ADAM_B1 * m + (1.0 - ADAM_B1) * g
    v2 = ADAM_B2 * v + (1.0 - ADAM_B2) * jnp.square(g)
    m_hat = m2 / (1.0 - ADAM_B1 ** ADAM_STEP)
    v_hat = v2 / (1.0 - ADAM_B2 ** ADAM_STEP)
    delta = -ADAM_LR * (m_hat / (jnp.sqrt(v_hat) + ADAM_EPS) + ADAM_WD * w)
    return delta, m2, v2


def _adamw_small(gparts, ws, ms, vs):
    n = len(SMALL)

    def body(g_ref, *refs):
        w_refs, m_refs, v_refs = refs[:n], refs[n:2 * n], refs[2 * n:3 * n]
        outs, sum_ref = refs[3 * n:-1], refs[-1]
        g = g_ref[0]
        for k in range(1, 8):
            g = g + g_ref[k]
        sum_ref[...] = g
        outs[0][...] = sum_ref[17:18, 0:128]
        for t, name in enumerate(SMALL):
            r0, nr, c0, nc = SMALL_SLOT[name]
            gt = sum_ref[r0:r0 + nr, c0:c0 + nc]
            out = (gt,) + _adam_math(w_refs[t][...], gt, m_refs[t][...], v_refs[t][...])
            for o_ref, val in zip(outs[1 + 4 * t:5 + 4 * t], out):
                o_ref[...] = val

    whole = lambda s: pl.BlockSpec(s, lambda i, nd=len(s): (0,) * nd)
    ins = [gparts] + list(ws) + list(ms) + list(vs)
    out_shapes = [(1, 128)] + [a.shape for a in ws for _ in range(4)]
    return _pcall(
        body, name="adamw_small", grid=(1,), in_specs=[whole(a.shape) for a in ins],
        out_specs=[whole(s) for s in out_shapes], out_shape=[jax.ShapeDtypeStruct(s, F32) for s in out_shapes],
        scratch_shapes=[pltpu.VMEM((SMALL_ROWS, D), F32)],
        compiler_params=pltpu.CompilerParams(dimension_semantics=("arbitrary",)),
    )(*ins)


def _row_tile(rows):
    return next(t for t in (512, 400, 320) if rows % t == 0)


def _add_halves(g4, theirs, core, name):
    rows = g4.shape[2]
    tr = _row_tile(rows)

    def body(c_ref, a_ref, b_ref, o_ref):
        o_ref[...] = (a_ref[...].astype(F32) + b_ref[...].astype(F32)).astype(BF16)

    grid_spec = pltpu.PrefetchScalarGridSpec(
        num_scalar_prefetch=1, grid=(4, rows // tr),
        in_specs=[pl.BlockSpec((None, None, tr, D), lambda j, i, c: (j, c[0], i, 0)),
                  pl.BlockSpec((None, None, tr, D), lambda j, i, c: (j, 0, i, 0))],
        out_specs=pl.BlockSpec((None, tr, D), lambda j, i, c: (j, i, 0)))
    return _pcall(
        body, name=name, grid_spec=grid_spec, out_shape=jax.ShapeDtypeStruct((4, rows, D), BF16),
        compiler_params=pltpu.CompilerParams(dimension_semantics=("arbitrary", "arbitrary")),
    )(core, g4, theirs)


def _sum_adam(own, got, order, r0, w, m, v, name, transposed=False):
    ragged = w.ndim == 3
    n = w.shape[1] if transposed else w.shape[0]
    tr = n if ragged else min(n, 256)
    rows = own.shape[1] if ragged else tr
    at = (slice(None), 0, slice(None)) if ragged else Ellipsis

    def body(o_ref, a_ref, b_ref, c_ref, d_ref, w_ref, m_ref, v_ref, *out_refs):
        f = lambda r: r[0:tr, :].astype(F32)
        g = ((f(a_ref) + f(b_ref)) + f(c_ref)) + f(d_ref)
        g = g.T if transposed else g
        for ref, val in zip(out_refs, (g,) + _adam_math(w_ref[at], g, m_ref[at], v_ref[at])):
            ref[at] = val

    slot = lambda k: pl.BlockSpec((None, rows, D), lambda i, o: (o[k], r0 // rows + i, 0))
    if ragged:
        wspec = pl.BlockSpec((n, 1, D), lambda i, o: (0, 0, 0))
    else:
        wspec = pl.BlockSpec((D, tr), lambda i, o: (0, i)) if transposed else pl.BlockSpec((tr, D), lambda i, o: (i, 0))
    grid_spec = pltpu.PrefetchScalarGridSpec(
        num_scalar_prefetch=1, grid=(n // tr,), in_specs=[slot(0), slot(1), slot(2), slot(3), wspec, wspec, wspec],
        out_specs=[wspec] * 4)
    return _pcall(
        body, name=name, grid_spec=grid_spec, out_shape=[jax.ShapeDtypeStruct(w.shape, F32)] * 4,
        compiler_params=pltpu.CompilerParams(dimension_semantics=("arbitrary",)),
    )(order, own, got, got, got, w, m, v)


def _sum_adam_rows(own, got, order, ws, ms, vs, name):
    n, rows = len(ws), ws[0].shape[0]

    def body(o_ref, a_ref, b_ref, c_ref, d_ref, *refs):
        ins, outs = refs[:3 * n], refs[3 * n:]
        for t in range(n):
            r = slice(t * rows, (t + 1) * rows)
            f = lambda ref: ref[r, :].astype(F32)
            g = ((f(a_ref) + f(b_ref)) + f(c_ref)) + f(d_ref)
            out = (g,) + _adam_math(ins[t][...], g, ins[n + t][...], ins[2 * n + t][...])
            for o, val in zip(outs[4 * t:4 * t + 4], out):
                o[...] = val

    slot = lambda k: pl.BlockSpec((None, n * rows, D), lambda i, o: (o[k], 0, 0), pipeline_mode=pl.Buffered(1))
    wspec = pl.BlockSpec((rows, D), lambda i, o: (0, 0), pipeline_mode=pl.Buffered(1))
    grid_spec = pltpu.PrefetchScalarGridSpec(
        num_scalar_prefetch=1, grid=(1,), in_specs=[slot(0), slot(1), slot(2), slot(3)] + [wspec] * (3 * n),
        out_specs=[pl.BlockSpec((rows, D), lambda i, o: (0, 0))] * (4 * n))
    return _pcall(
        body, name=name, grid_spec=grid_spec, out_shape=[jax.ShapeDtypeStruct((rows, D), F32)] * (4 * n),
        compiler_params=pltpu.CompilerParams(dimension_semantics=("arbitrary",), vmem_limit_bytes=VMEM_BIG),
    )(order, own, got, got, got, *ws, *ms, *vs)


def _place():
    return lax.axis_index("x"), lax.axis_index("y"), lax.axis_index("c")


def _allgather(block, name, after=()):
    rows = block.shape[0]
    split = (rows // 2 + 15) // 16 * 16

    def body(x_ref, out_ref, token, send_sems, recv_sems, local_sem):
        token[...] = jnp.zeros_like(token)
        x, y, c = _place()
        me, sib = (x, y, c), (x, y, 1 - c)
        xn, yn, dg = (1 - x, y), (x, 1 - y), (1 - x, 1 - y)
        lo, hi = pl.ds(0, split), pl.ds(split, rows - split)

        def copy(k, blk, to, part=None, src=None):
            index = 4 * blk[0] + 2 * blk[1] + blk[2]
            view = out_ref.at[index] if part is None else out_ref.at[index, part]
            return pltpu.make_async_remote_copy(
                src_ref=view if src is None else src, dst_ref=view,
                send_sem=send_sems.at[k], recv_sem=recv_sems.at[k], device_id=to, device_id_type=MESH)

        def start(*copies):
            for cp in copies:
                cp.start()
            return list(copies)

        mine = pltpu.make_async_copy(x_ref, out_ref.at[4 * x + 2 * y + c], local_sem)
        mine.start()
        sent = start(copy(0, me, sib, src=x_ref), copy(1, me, (*xn, c), src=x_ref), copy(2, me, (*yn, c), src=x_ref))
        copy(1, (*xn, c), me).wait_recv()
        sent += start(copy(3, (*xn, c), sib), copy(5, (*xn, c), (*yn, c), part=lo))
        copy(2, (*yn, c), me).wait_recv()
        sent += start(copy(4, (*yn, c), sib), copy(6, (*yn, c), (*xn, c), part=hi))
        copy(5, (*dg, c), me, part=lo).wait_recv()
        copy(6, (*dg, c), me, part=hi).wait_recv()
        sent += start(copy(7, (*dg, c), sib))
        for k, blk in ((0, sib), (3, (*xn, 1 - c)), (4, (*yn, 1 - c)), (7, (*dg, 1 - c))):
            copy(k, blk, me).wait_recv()
        for cp in sent:
            cp.wait_send()
        mine.wait()

    return _pcall(
        _behind(body, 1, after), name=name,
        out_shape=[jax.ShapeDtypeStruct((8,) + block.shape, block.dtype), jax.ShapeDtypeStruct((8, 128), F32)],
        in_specs=[pl.BlockSpec(memory_space=pl.ANY)] * (1 + len(after)),
        out_specs=[pl.BlockSpec(memory_space=pl.ANY), pl.BlockSpec(memory_space=pltpu.VMEM)],
        scratch_shapes=[pltpu.SemaphoreType.DMA((8,)), pltpu.SemaphoreType.DMA((8,)), pltpu.SemaphoreType.DMA(())],
        compiler_params=pltpu.CompilerParams(has_side_effects=True),
    )(block, *after)


HBM_SPEC = pl.BlockSpec(memory_space=pltpu.HBM)
SEM_SPEC = pl.BlockSpec(memory_space=pltpu.SEMAPHORE)
ANY_SPEC = pl.BlockSpec(memory_space=pl.ANY)
EFFECT = pltpu.SideEffectType.DATAFLOW_SIDE_EFFECTING


def _in_hbm(a):
    return pltpu.with_memory_space_constraint(a, pltpu.HBM)


def _start_copies(name, src, land_shape, plan, n):
    def body(src_ref, land_ref, send_sems, recv_sems, src_thru, land_thru, token):
        for k, (s, d, to, _) in enumerate(plan(src_ref, land_ref)):
            pltpu.make_async_remote_copy(src_ref=s, dst_ref=d, send_sem=send_sems.at[k], recv_sem=recv_sems.at[k],
                                         device_id=to, device_id_type=MESH).start()
        token[...] = jnp.zeros_like(token)

    return _pcall(
        body, name=name,
        out_shape=(pltpu.SemaphoreType.DMA((n,)), pltpu.SemaphoreType.DMA((n,)), pltpu.HBM(src.shape, src.dtype),
                   pltpu.HBM(land_shape, src.dtype), jax.ShapeDtypeStruct((8, 128), F32)),
        in_specs=(HBM_SPEC, HBM_SPEC),
        out_specs=(SEM_SPEC, SEM_SPEC, HBM_SPEC, HBM_SPEC, pl.BlockSpec(memory_space=pltpu.VMEM)),
        input_output_aliases={0: 2, 1: 3}, compiler_params=pltpu.CompilerParams(has_side_effects=EFFECT),
    )(_in_hbm(src), _in_hbm(lax.empty(land_shape, src.dtype)))


def _wait_copies(name, started, after, plan):
    send_sems, recv_sems, src_thru, land_thru, _ = started

    def body(src_ref, land_ref, send_sems, recv_sems, *rest):
        for k, (s, _, to, mine) in enumerate(plan(src_ref, land_ref)):
            cp = pltpu.make_async_remote_copy(src_ref=s, dst_ref=mine, send_sem=send_sems.at[k],
                                              recv_sem=recv_sems.at[k], device_id=to, device_id_type=MESH)
            cp.wait_send()
            cp.wait_recv()

    return _pcall(
        body, name=name,
        out_shape=(pltpu.HBM(src_thru.shape, src_thru.dtype), pltpu.HBM(land_thru.shape, land_thru.dtype)),
        in_specs=(HBM_SPEC, HBM_SPEC, SEM_SPEC, SEM_SPEC) + (ANY_SPEC,) * len(after), out_specs=(HBM_SPEC, HBM_SPEC),
        input_output_aliases={0: 0, 1: 1}, compiler_params=pltpu.CompilerParams(has_side_effects=EFFECT),
    )(src_thru, land_thru, send_sems, recv_sems, *after)


def _start_inplace(name, buf, plan, n):
    def body(buf_ref, send_sems, recv_sems, buf_thru, token):
        for k, (s, d, to, _) in enumerate(plan(buf_ref, buf_ref)):
            pltpu.make_async_remote_copy(src_ref=s, dst_ref=d, send_sem=send_sems.at[k], recv_sem=recv_sems.at[k],
                                         device_id=to, device_id_type=MESH).start()
        token[...] = jnp.zeros_like(token)

    return _pcall(
        body, name=name,
        out_shape=(pltpu.SemaphoreType.DMA((n,)), pltpu.SemaphoreType.DMA((n,)), pltpu.HBM(buf.shape, buf.dtype),
                   jax.ShapeDtypeStruct((8, 128), F32)),
        in_specs=(HBM_SPEC,), out_specs=(SEM_SPEC, SEM_SPEC, HBM_SPEC, pl.BlockSpec(memory_space=pltpu.VMEM)),
        input_output_aliases={0: 2}, compiler_params=pltpu.CompilerParams(has_side_effects=EFFECT),
    )(_in_hbm(buf))


def _wait_inplace(name, started, after, plan):
    send_sems, recv_sems, buf_thru, _ = started

    def body(buf_ref, send_sems, recv_sems, *rest):
        for k, (s, _, to, mine) in enumerate(plan(buf_ref, buf_ref)):
            cp = pltpu.make_async_remote_copy(src_ref=s, dst_ref=mine, send_sem=send_sems.at[k],
                                              recv_sem=recv_sems.at[k], device_id=to, device_id_type=MESH)
            cp.wait_send()
            cp.wait_recv()

    return _pcall(
        body, name=name, out_shape=pltpu.HBM(buf_thru.shape, buf_thru.dtype),
        in_specs=(HBM_SPEC, SEM_SPEC, SEM_SPEC) + (ANY_SPEC,) * len(after), out_specs=HBM_SPEC,
        input_output_aliases={0: 0}, compiler_params=pltpu.CompilerParams(has_side_effects=EFFECT),
    )(buf_thru, send_sems, recv_sems, *after)


def _gather_plan(src_ref, land_ref):
    x, y, c = _place()
    peers = [(x, y, 1 - c), (1 - x, y, c), (x, 1 - y, c)]
    return [(src_ref, land_ref.at[4 * x + 2 * y + c], p, land_ref.at[4 * p[0] + 2 * p[1] + p[2]]) for p in peers]


def _relay_plan(buf_ref, _):
    x, y, c = _place()
    slot = lambda p, pc: 4 * p[0] + 2 * p[1] + pc
    xn, yn, dg, sib = (1 - x, y), (x, 1 - y), (1 - x, 1 - y), (x, y, 1 - c)
    half = buf_ref.shape[1] // 2
    lo, hi = pl.ds(0, half), pl.ds(half, half)
    return [(buf_ref.at[slot(xn, c)], buf_ref.at[slot(xn, c)], sib, buf_ref.at[slot(xn, 1 - c)]),
            (buf_ref.at[slot(yn, c)], buf_ref.at[slot(yn, c)], sib, buf_ref.at[slot(yn, 1 - c)]),
            (buf_ref.at[slot(xn, c), lo], buf_ref.at[slot(xn, c), lo], (*yn, c), buf_ref.at[slot(dg, c), lo]),
            (buf_ref.at[slot(yn, c), hi], buf_ref.at[slot(yn, c), hi], (*xn, c), buf_ref.at[slot(dg, c), hi])]


def _swap_plan(src_ref, land_ref):
    x, y, c = _place()
    return [(src_ref.at[:, pl.ds(1 - c, 1)], land_ref, (x, y, 1 - c), land_ref)]


def _exchange_plan(src_ref, land_ref):
    x, y, c = _place()
    chips = [(1 - x, y), (x, 1 - y), (1 - x, 1 - y)]
    return [(src_ref.at[2 * px + py], land_ref.at[2 * x + y], (px, py, c), land_ref.at[2 * px + py]) for px, py in chips]


def _gather_forward(land, block):
    def body(land_ref, out_ref, send_sems, recv_sems):
        x, y, c = _place()
        chips = [(1 - x, 1 - y)]

        def copy(k, px, py, pc):
            blk = out_ref.at[4 * px + 2 * py + pc]
            return pltpu.make_async_remote_copy(src_ref=blk, dst_ref=blk, send_sem=send_sems.at[k],
                                                recv_sem=recv_sems.at[k], device_id=(x, y, 1 - c), device_id_type=MESH)

        sent = [copy(k, px, py, c) for k, (px, py) in enumerate(chips)]
        for cp in sent:
            cp.start()
        for k, (px, py) in enumerate(chips):
            copy(k, px, py, 1 - c).wait_recv()
        for cp in sent:
            cp.wait_send()

    land = _pcall(
        body, name="allgather_rest_forward", out_shape=jax.ShapeDtypeStruct(land.shape, land.dtype),
        in_specs=[ANY_SPEC], out_specs=ANY_SPEC, input_output_aliases={0: 0},
        scratch_shapes=[pltpu.SemaphoreType.DMA((1,)), pltpu.SemaphoreType.DMA((1,))],
        compiler_params=pltpu.CompilerParams(has_side_effects=True),
    )(land)

    rows = block.shape[0]
    tr = rows // 4

    def place(me_ref, x_ref, land_ref, out_ref):
        out_ref[...] = x_ref[...]

    x, y, c = _place()
    grid_spec = pltpu.PrefetchScalarGridSpec(
        num_scalar_prefetch=1, grid=(rows // tr,),
        in_specs=[pl.BlockSpec((tr, D), lambda i, me: (i, 0)), ANY_SPEC],
        out_specs=pl.BlockSpec((None, tr, D), lambda i, me: (me[0], i, 0)))
    return _pcall(
        place, name="allgather_rest_own", grid_spec=grid_spec, out_shape=jax.ShapeDtypeStruct(land.shape, land.dtype),
        input_output_aliases={2: 0}, compiler_params=pltpu.CompilerParams(dimension_semantics=("arbitrary",)),
    )((4 * x + 2 * y + c).reshape(1), block, land)


class _ReduceScatter:
    def __init__(self, name, g):
        self.name = name
        rows = g.shape[1]
        self.started = _start_copies(name + "_swap_start", g.reshape(4, 2, rows, D), (4, 1, rows, D), _swap_plan, 1)
        self.token = self.started[4]

    def halfway(self, after):
        g4, theirs = _wait_copies(self.name + "_swap_wait", self.started, after, _swap_plan)
        self.own = _add_halves(g4, theirs, lax.axis_index("c").reshape(1), self.name + "_add_halves")
        self.started = _start_copies(self.name + "_exch_start", self.own, self.own.shape, _exchange_plan, 3)
        self.token = self.started[4]

    def finish(self, after):
        own, got = _wait_copies(self.name + "_exch_wait", self.started, after, _exchange_plan)
        chip = 2 * lax.axis_index("x") + lax.axis_index("y")
        return own, got, (chip + jnp.arange(4, dtype=jnp.int32)) % 4


def _pack_small(p, loss):
    def body(*refs):
        o_ref = refs[-1]
        o_ref[...] = jnp.zeros_like(o_ref)
        for ref, name in zip(refs, SMALL):
            r0, nr, c0, nc = SMALL_SLOT[name]
            o_ref[r0:r0 + nr, c0:c0 + nc] = ref[...]
        o_ref[17:18, 0:128] = refs[len(SMALL)][0:1, :]

    return _one_call(body, "pack_small_grads", [p[n] for n in SMALL] + [loss], [((SMALL_ROWS, D), F32)])[0]


_GAP_DEV, _GAP_ROW = divmod(GATE0 + 8, N_IN)
_GAP = CHK0 - GATE0 - 8


def _in_rows_to_proj(g_ref, o_ref, acc_ref):
    runs = [(j, 0, N_IN, N_IN * j) for j in range(_GAP_DEV)]
    runs += [(_GAP_DEV, 0, _GAP_ROW, N_IN * _GAP_DEV), (_GAP_DEV, _GAP_ROW, N_IN, N_IN * _GAP_DEV + _GAP_ROW + _GAP)]
    runs += [(j, 0, N_IN, N_IN * j + _GAP) for j in range(_GAP_DEV + 1, 8)]
    acc_ref[...] = jnp.zeros_like(acc_ref)
    for j, r0, r1, dest in runs:
        start, shift = dest // 16 * 16, dest % 16
        win = -(-(shift + r1 - r0) // 16) * 16
        r = lax.broadcasted_iota(jnp.int32, (win, R_IN), 0)
        c = lax.broadcasted_iota(jnp.int32, (win, R_IN), 1)
        move = jnp.where((c >= r0) & (c < r1) & (r == c - r0 + shift), 1.0, 0.0).astype(BF16)
        acc_ref[start:start + win, :] += _nn(move, g_ref[j])
    o_ref[...] = acc_ref[...].astype(BF16)


def _wgrad_in(pieces, h1):
    n = len(pieces)
    ends = [sum(p.shape[1] for p in pieces[:k + 1]) for k in range(n)]
    assert ends[-1] == PROJ

    def body(*refs):
        piece_refs, (b_ref, o_ref, g_ref), bufs, sem = refs[:n], refs[n:n + 3], refs[n + 3:2 * n + 3], refs[2 * n + 3]
        i = pl.program_id(0)
        copies = [pltpu.make_async_copy(piece_refs[k], bufs[k], sem.at[k]) for k in range(n)]

        @pl.when(i == 0)
        def _():
            for copy in copies:
                copy.start()
            g_ref[PROJ:, :] = jnp.zeros((R_IN - N_IN + 1, D), F32)

        for k in range(n):
            @pl.when(i == k)
            def _(k=k):
                copies[k].wait()
                g_ref[ends[k] - pieces[k].shape[1]:ends[k], :] = _tn(bufs[k][...], b_ref[...])

        row = lax.broadcasted_iota(jnp.int32, (R_IN, D), 0)
        for j in range(8):
            lo = N_IN * j + (_GAP if j > _GAP_DEV else 0)
            hi = N_IN * j + (_GAP if j >= _GAP_DEV else 0)
            ready = min(k for k in range(n) if ends[k] >= min(hi + R_IN, PROJ))

            @pl.when(i == ready)
            def _(j=j, lo=lo, hi=hi):
                v = g_ref[hi:hi + R_IN, :]
                if lo != hi:
                    v = jnp.where(row < _GAP_ROW, g_ref[lo:lo + R_IN, :], v)
                o_ref[j] = jnp.where(row < N_IN, v, 0.0).astype(BF16)

    return _pcall(
        body, name="wgrad_in", grid=(n,),
        in_specs=[ANY_SPEC] * n + [_resident(h1)],
        out_specs=pl.BlockSpec((8, R_IN, D), lambda i: (0, 0, 0)),
        out_shape=jax.ShapeDtypeStruct((8, R_IN, D), BF16),
        scratch_shapes=[pltpu.VMEM((PROJ + R_IN - N_IN + 1, D), F32)] + [pltpu.VMEM(p.shape, BF16) for p in pieces]
        + [pltpu.SemaphoreType.DMA((n,))],
        compiler_params=pltpu.CompilerParams(dimension_semantics=("arbitrary",), vmem_limit_bytes=VMEM_BIG),
    )(*pieces, h1)


def _local_grads(x, mem, tgt, gathered_in, gw_of, sm, on_grads, after=()):
    b_pad = jnp.pad(sm['b_fgt'], ((0, 0), (0, 120)))
    tbl = jnp.pad(sm['rel_bias'], ((0, 0), (0, NREL_PAD - 257)))

    h1, proj, flog, kvp, win_t = _premix_fwd(x, sm['g_mix_pre'], gathered_in, after)
    c = _gate_fwd(flog, b_pad)
    ct3 = c[:, :8].T.reshape(4, 2, T)
    o_f, lse_f = _fox_fwd(proj, c, ct3)
    vt3 = _relvec_fwd(tbl).reshape(4, 2, VW)
    o_c, lse_c = _chk_fwd(proj, kvp, vt3, [gw_of('relay', [o_f])])
    gw = gw_of('done', [o_c])
    w_out, w_mq, w_mk, w_mv, w_mo, w1_t, w2 = (_wblk(gw, n) for n in ('w_out', 'w_mq', 'w_mk', 'w_mv', 'w_mo', 'w_ff1', 'w_ff2'))
    ycat, z, x1, h2, qm = _postmix_fwd(x, o_f, o_c, sm['g_fox_out'], sm['g_chk_out'], w_out,
                                       sm['g_mix_post'], sm['g_mem_pre'], w_mq)
    memn, km, vm = _memkv_fwd(mem, sm['g_mem_kv'], w_mk, w_mv)
    om, ym, x2, h3 = _mem_fwd(qm, x1, km, vm, w_mo, sm['g_mem_post'], sm['g_ff_pre'])

    gs = {}
    dx2, da, dy3, r, loss_acc, gs['g_ff_post'], gs['g_ff_pre'] = _ffn_step(h3, x2, tgt, w1_t, w2, sm['g_ff_post'],
                                                                         sm['g_ff_pre'])
    tok = on_grads('A', _wgrad_group("wgrad_ff", [(da, h3), (r, dy3)], 512), None)
    dx1, dym, dqm, dkm, dvm, gs['g_mem_post'], gs['g_mem_pre'] = _mem_bwd(
        dx2, ym, x1, qm, km, vm, w_mo, w_mq, sm['g_mem_post'], sm['g_mem_pre'], [tok])
    tok = on_grads('A halfway', None, [dx1])
    gs['g_mem_kv'] = _memkv_bwd(dkm, dvm, mem, w_mk, w_mv)
    dz, dof, doc, gs['g_mix_post'], gs['g_fox_out'], gs['g_chk_out'] = _postmix_bwd(
        dx1, z, o_f, o_c, w_out, sm['g_mix_post'], sm['g_fox_out'], sm['g_chk_out'], [tok])
    tok = on_grads('B', _wgrad_whole("wgrad_mem_out", [(ycat, dz), (h2, dqm), (memn, dkm), (memn, dvm), (om, dym)]), None)
    dq_f, dk_f, dv_f, dct, dcq = _fox_bwd(proj, c, ct3, o_f, lse_f, dof, [tok])
    tok = on_grads('B halfway', None, [dq_f])
    dq_c, dk_c, dv_c, gv = _chk_bwd(proj, kvp, vt3, o_c, lse_c, doc, [tok])
    gs['rel_bias'] = _relvec_bwd(gv.reshape(8, VW))[:, :257]
    dc = jnp.pad(dct.reshape(8, T).T + dcq[:, :, :2].transpose(1, 0, 2).reshape(T, 8), ((0, 0), (0, 120)))
    dflog, db = _gate_bwd(dc, flog, b_pad)
    gs['b_fgt'] = db[0:1, :8]
    pieces = [dq_f, dk_f, dv_f, dflog, dq_c, dk_c, dv_c]
    on_grads('C', _wgrad_in(pieces, h1), None)
    tok = on_grads('C halfway', None, [gs['g_mem_kv']])
    grad_x, gs['g_mix_pre'] = _premix_bwd(dx1, x, pieces, win_t, sm['g_mix_pre'], [tok])
    return loss_acc, grad_x, gs


def kernel(x, mem, w_in, b_fgt, rel_bias, g_fox_out, g_chk_out, w_out, g_mix_pre, g_mix_post, g_mem_kv, w_mq, w_mk, w_mv, w_mo, g_mem_pre, g_mem_post, w_ff1, w_ff2, g_ff_pre, g_ff_post, loss_target, m_w_in, m_b_fgt, m_rel_bias, m_g_fox_out, m_g_chk_out, m_w_out, m_g_mix_pre, m_g_mix_post, m_g_mem_kv, m_w_mq, m_w_mk, m_w_mv, m_w_mo, m_g_mem_pre, m_g_mem_post, m_w_ff1, m_w_ff2, m_g_ff_pre, m_g_ff_post, v_w_in, v_b_fgt, v_rel_bias, v_g_fox_out, v_g_chk_out, v_w_out, v_g_mix_pre, v_g_mix_post, v_g_mem_kv, v_w_mq, v_w_mk, v_w_mv, v_w_mo, v_g_mem_pre, v_g_mem_post, v_w_ff1, v_w_ff2, v_g_ff_pre, v_g_ff_post):
    args = dict(locals())
    two_d = lambda a: a.reshape(a.shape[-2:])
    w = {n: two_d(args[n]) for n in WEIGHTS}
    m = {n: two_d(args['m_' + n]) for n in WEIGHTS}
    v = {n: two_d(args['v_' + n]) for n in WEIGHTS}

    sm = {n: w[n] for n in SMALL}
    shard_in = jnp.pad(w['w_in'].T, ((0, R_IN - N_IN), (0, 0))).astype(BF16)
    gathered_in, zero = _allgather(shard_in, "allgather_w_in")
    shard_rest = (jnp.concatenate([w['w_ff1'].T, w['w_ff2'], w['w_out'], w['w_mq'], w['w_mk'], w['w_mv'], w['w_mo']],
                                  axis=0) + zero[0, 0]).astype(BF16)
    gather = {'first': _start_copies("allgather_rest_start", shard_rest, (8, R_REST, D), _gather_plan, 3)}

    def gw_of(stage, after):
        if stage == 'relay':
            gather['block'], land = _wait_copies("allgather_rest_wait", gather['first'], after, _gather_plan)
            gather['second'] = _start_inplace("allgather_rest_relay_start", land, _relay_plan, 4)
            return gather['second'][3]
        land = _wait_inplace("allgather_rest_relay_wait", gather['second'], after, _relay_plan)
        return _gather_forward(land, gather['block'])

    rs = {}

    def on_grads(stage, g, after):
        if stage.endswith('halfway'):
            rs[stage[0]].halfway(after)
            return rs[stage[0]].token
        rs[stage] = _ReduceScatter("rs_" + stage.lower(), g)
        return rs[stage].token

    loss_local, grad_x, gs = _local_grads(x[0], mem[0], loss_target[0], gathered_in, gw_of, sm, on_grads,
                                          [gather['first'][4]])
    grads, deltas, new_m, new_v = {}, {}, {}, {}

    def update(n, out):
        grads[n], deltas[n], new_m[n], new_v[n] = out

    own, got, order = rs['A'].finish([grad_x, rs['C'].token])
    update('w_ff1', _sum_adam(own, got, order, 0, w['w_ff1'], m['w_ff1'], v['w_ff1'], "adamw_w_ff1", transposed=True))
    update('w_ff2', _sum_adam(own, got, order, 512, w['w_ff2'], m['w_ff2'], v['w_ff2'], "adamw_w_ff2"))
    own, got, order = rs['B'].finish([grad_x, rs['C'].token])
    names_b = ('w_out', 'w_mq', 'w_mk', 'w_mv', 'w_mo')
    done = _sum_adam_rows(own, got, order, [w[n] for n in names_b], [m[n] for n in names_b], [v[n] for n in names_b],
                          "adamw_group_b")
    for k, n in enumerate(names_b):
        update(n, done[4 * k:4 * k + 4])

    own, got, order = rs['C'].finish([new_v[n] for n in BIG if n != 'w_in'])
    rows_of = lambda a: jnp.transpose(a, (2, 0, 1))
    done = _sum_adam(own, got, order, 0, rows_of(w_in), rows_of(m_w_in), rows_of(v_w_in), "adamw_w_in")
    update('w_in', [jnp.transpose(a, (1, 2, 0)) for a in done])

    gparts, _ = _allgather(_pack_small(gs, loss_local), "allgather_small_grads", [got])
    small = _adamw_small(gparts, [w[n] for n in SMALL], [m[n] for n in SMALL], [v[n] for n in SMALL])
    loss = small[0][0, 0]
    for t, n in enumerate(SMALL):
        update(n, small[1 + 4 * t:5 + 4 * t])

    out = [loss, grad_x[None]]
    for group in (grads, deltas, new_m, new_v):
        out += [group[n].reshape(args[n].shape) for n in WEIGHTS]
    return tuple(out)
```

```python
import jax
import jax.numpy as jnp
from jax import lax
from jax.experimental import pallas as pl
from jax.experimental.pallas import tpu as pltpu

F32 = jnp.float32
BF16 = jnp.bfloat16
MESH = pl.DeviceIdType.MESH

T = 2048
D = 1024
NMEM = 256
DFF = 4096
EPS = 1e-6
TM = 256
TM_WIDE = 512
TQ = 256
FQ = 512
HD = 64
SCALE = HD ** -0.5
MEM_HEADS = 4
MEM_HD = 256
MEM_SCALE = MEM_HD ** -0.5
NEG = -1e30
LEFT = 512
WIN = LEFT + TQ
VW = 1024
NREL_PAD = 384
PROJ = 3200
GATE0 = 1536
CHK0 = 1664
VMEM_BIG = 56 * 1024 * 1024

ADAM_LR = 0.001
ADAM_B1 = 0.9
ADAM_B2 = 0.999
ADAM_EPS = 1e-08
ADAM_WD = 0.01
ADAM_STEP = 10

N_IN = 385
R_IN = 400
R_REST = 1664
W_ROWS = {'w_ff1': (0, 512), 'w_ff2': (512, 512),
          'w_out': (1024, 128), 'w_mq': (1152, 128), 'w_mk': (1280, 128), 'w_mv': (1408, 128), 'w_mo': (1536, 128)}
SMALL_ROWS = 24
SMALL_SLOT = {'rel_bias': (0, 8, 0, 257), 'b_fgt': (8, 1, 0, 8), 'g_fox_out': (9, 1, 0, 512), 'g_chk_out': (9, 1, 512, 512),
              'g_mix_pre': (10, 1, 0, 1024), 'g_mix_post': (11, 1, 0, 1024), 'g_mem_kv': (12, 1, 0, 1024),
              'g_mem_pre': (13, 1, 0, 1024), 'g_mem_post': (14, 1, 0, 1024), 'g_ff_pre': (15, 1, 0, 1024),
              'g_ff_post': (16, 1, 0, 1024)}

WEIGHTS = ['w_in', 'b_fgt', 'rel_bias', 'g_fox_out', 'g_chk_out', 'w_out', 'g_mix_pre', 'g_mix_post', 'g_mem_kv',
           'w_mq', 'w_mk', 'w_mv', 'w_mo', 'g_mem_pre', 'g_mem_post', 'w_ff1', 'w_ff2', 'g_ff_pre', 'g_ff_post']
BIG = ['w_in', 'w_out', 'w_mq', 'w_mk', 'w_mv', 'w_mo', 'w_ff1', 'w_ff2']
SMALL = [n for n in WEIGHTS if n not in BIG]


def _pcall(body, **kw):
    return pl.pallas_call(body, **kw)


def _nn(a, b):
    return jnp.dot(a, b, preferred_element_type=F32)


def _nt(a, b):
    return lax.dot_general(a, b, (((1,), (1,)), ((), ())), preferred_element_type=F32)


def _tn(a, b):
    return lax.dot_general(a, b, (((0,), (0,)), ((), ())), preferred_element_type=F32)


def _w(ref):
    v = ref[...]
    return v if v.ndim == 2 else v.reshape(-1, v.shape[-1])


def _rstd(x):
    return lax.rsqrt(jnp.mean(x * x, axis=-1, keepdims=True) + EPS)


def _rms(x, g):
    return x * _rstd(x) * g


def _rms_bwd(x, g, dy):
    r = _rstd(x)
    xh = x * r
    dg = jnp.sum(dy * xh, axis=0, keepdims=True)
    dxh = dy * g
    dx = r * (dxh - xh * jnp.mean(dxh * xh, axis=-1, keepdims=True))
    return dx, dg


def _resident(a):
    if isinstance(a, tuple):
        _, shape, index = a
        return pl.BlockSpec(shape, lambda *_: index, pipeline_mode=pl.Buffered(1))
    return pl.BlockSpec(a.shape, lambda *_, nd=a.ndim: (0,) * nd, pipeline_mode=pl.Buffered(1))


def _wblk(gw, name):
    r0, rows = W_ROWS[name]
    return (gw, (8, rows, D), (0, r0 // rows, 0))


def _behind(body, n_in, after):
    if not after:
        return body
    return lambda *refs: body(*refs[:n_in], *refs[n_in + len(after):])


def _tok_call(body, name, tiled, full, outs_tiled, outs_acc=(), rows=T, tm=TM, vmem=None, after=(), scratch=()):
    in_specs = [pl.BlockSpec((tm, a.shape[1]), lambda i: (i, 0)) for a in tiled]
    in_specs += [_resident(a) for a in full] + [ANY_SPEC] * len(after)
    full = [a[0] if isinstance(a, tuple) else a for a in full] + list(after)
    body = _behind(body, len(tiled) + len(full) - len(after), after)
    out_shape = [jax.ShapeDtypeStruct((rows, c), dt) for c, dt in outs_tiled]
    out_shape += [jax.ShapeDtypeStruct(s, F32) for s in outs_acc]
    out_specs = [pl.BlockSpec((tm, c), lambda i: (i, 0)) for c, _ in outs_tiled]
    out_specs += [pl.BlockSpec(s, lambda i, nd=len(s): (0,) * nd) for s in outs_acc]
    return _pcall(
        body, name=name, grid=(rows // tm,), in_specs=in_specs, out_specs=out_specs, out_shape=out_shape,
        scratch_shapes=list(scratch),
        compiler_params=pltpu.CompilerParams(dimension_semantics=("arbitrary",), vmem_limit_bytes=vmem),
    )(*tiled, *full)


def _one_call(body, name, ins, outs, vmem=None):
    whole = lambda s: pl.BlockSpec(s, lambda i, nd=len(s): (0,) * nd)
    return _pcall(
        body, name=name, grid=(1,), in_specs=[_resident(a) for a in ins], out_specs=[whole(s) for s, _ in outs],
        out_shape=[jax.ShapeDtypeStruct(s, dt) for s, dt in outs],
        compiler_params=pltpu.CompilerParams(dimension_semantics=("arbitrary",), vmem_limit_bytes=vmem),
    )(*[a[0] if isinstance(a, tuple) else a for a in ins])


def _premix_fwd(x, g_pre, gathered_in, after=()):
    def body(x_ref, g_ref, gin_ref, h_ref, proj_ref, flog_ref, kvp_ref, w_ref, acc_ref):
        s = pl.program_id(0)

        @pl.when(s == 0)
        def _():
            kvp_ref[...] = jnp.zeros_like(kvp_ref)
            _in_rows_to_proj(gin_ref, w_ref, acc_ref)

        @pl.when(s > 0)
        def _():
            h = _rms(x_ref[...], g_ref[...]).astype(BF16)
            h_ref[...] = h
            p = _nt(h, w_ref[...])
            proj_ref[...] = p.astype(BF16)
            flog_ref[...] = p[:, GATE0:GATE0 + 128]
            kvp_ref[...] = p[:, CHK0 + 512:].astype(BF16)

    tile = lambda c: pl.BlockSpec((LEFT, c), lambda s: (jnp.maximum(s - 1, 0), 0))
    return _pcall(
        _behind(body, 3, after), name="premix_fwd", grid=(T // LEFT + 1,),
        in_specs=[tile(D), _resident(g_pre), _resident(gathered_in)] + [ANY_SPEC] * len(after),
        out_specs=[tile(D), tile(PROJ), tile(128), pl.BlockSpec((LEFT, 1024), lambda s: (s, 0)),
                   pl.BlockSpec((PROJ, D), lambda s: (0, 0))],
        out_shape=[jax.ShapeDtypeStruct((T, D), BF16), jax.ShapeDtypeStruct((T, PROJ), BF16),
                   jax.ShapeDtypeStruct((T, 128), F32), jax.ShapeDtypeStruct((T + LEFT, 1024), BF16),
                   jax.ShapeDtypeStruct((PROJ, D), BF16)],
        scratch_shapes=[pltpu.VMEM((PROJ, D), F32)],
        compiler_params=pltpu.CompilerParams(dimension_semantics=("arbitrary",), vmem_limit_bytes=VMEM_BIG),
    )(x, g_pre, gathered_in, *after)


def _postmix_fwd(x, o_f, o_c, g_fo, g_co, w_out, g_post, g_mpre, w_mq):
    def body(x_ref, of_ref, oc_ref, gfo_ref, gco_ref, wo_ref, gp_ref, gm_ref, wq_ref,
             y_ref, z_ref, x1_ref, h2_ref, qm_ref):
        y_ref[:, :512] = _rms(of_ref[...], gfo_ref[...]).astype(BF16)
        y_ref[:, 512:] = _rms(oc_ref[...], gco_ref[...]).astype(BF16)
        z = _nn(y_ref[...], _w(wo_ref))
        z_ref[...] = z
        x1 = x_ref[...] + _rms(z, gp_ref[...])
        x1_ref[...] = x1
        h2 = _rms(x1, gm_ref[...]).astype(BF16)
        h2_ref[...] = h2
        qm_ref[...] = _nn(h2, _w(wq_ref)).astype(BF16)

    return _tok_call(body, "postmix_fwd", [x, o_f, o_c], [g_fo, g_co, w_out, g_post, g_mpre, w_mq],
                     [(D, BF16), (D, F32), (D, F32), (D, BF16), (D, BF16)], tm=TM_WIDE, vmem=VMEM_BIG)


def _memkv_fwd(mem, g_kv, w_mk, w_mv):
    def body(m_ref, g_ref, wk_ref, wv_ref, mn_ref, k_ref, v_ref):
        mn = _rms(m_ref[...], g_ref[...]).astype(BF16)
        mn_ref[...] = mn
        k_ref[...] = _nn(mn, _w(wk_ref)).astype(BF16)
        v_ref[...] = _nn(mn, _w(wv_ref)).astype(BF16)

    return _tok_call(body, "memkv_fwd", [mem], [g_kv, w_mk, w_mv],
                     [(D, BF16), (D, BF16), (D, BF16)], rows=NMEM, tm=NMEM, vmem=VMEM_BIG)


def _mem_fwd(qm, x1, km, vm, w_mo, g_post, g_fpre):
    def body(q_ref, x1_ref, k_ref, v_ref, wo_ref, gp_ref, gf_ref, om_ref, ym_ref, x2_ref, h3_ref):
        for h in range(MEM_HEADS):
            sl = slice(h * MEM_HD, (h + 1) * MEM_HD)
            s = _nt(q_ref[:, sl], k_ref[:, sl]) * MEM_SCALE
            p = jnp.exp(s - jnp.max(s, axis=-1, keepdims=True))
            p = p / jnp.sum(p, axis=-1, keepdims=True)
            om_ref[:, sl] = _nn(p.astype(BF16), v_ref[:, sl]).astype(BF16)
        ym = _nn(om_ref[...], _w(wo_ref))
        ym_ref[...] = ym
        x2 = x1_ref[...] + _rms(ym, gp_ref[...])
        x2_ref[...] = x2
        h3_ref[...] = _rms(x2, gf_ref[...]).astype(BF16)

    return _tok_call(body, "mem_fwd", [qm, x1], [km, vm, w_mo, g_post, g_fpre],
                     [(D, BF16), (D, F32), (D, F32), (D, BF16)], tm=TM_WIDE, vmem=VMEM_BIG)


def _tri(lower):
    r = lax.broadcasted_iota(jnp.int32, (128, 128), 0)
    c = lax.broadcasted_iota(jnp.int32, (128, 128), 1)
    return jnp.where(r >= c if lower else c >= r, 1.0, 0.0).astype(F32)


def _hdot(a, b):
    return jnp.dot(a, b, preferred_element_type=F32, precision=lax.Precision.HIGHEST)


def _gate_fwd(flog, b_pad):
    def body(f_ref, b_ref, c_ref):
        tri = _tri(True)

        def step(i, carry):
            rows = pl.ds(pl.multiple_of(i * 128, 128), 128)
            z = f_ref[rows, :] + b_ref[...]
            lf = jnp.minimum(z, 0.0) - jnp.log(1.0 + jnp.exp(-jnp.abs(z)))
            cb = _hdot(tri, lf) + carry
            c_ref[rows, :] = cb
            return cb[127:128, :]

        lax.fori_loop(0, T // 128, step, jnp.zeros((1, 128), F32))

    return _one_call(body, "gate_fwd", [flog, b_pad], [((T, 128), F32)])[0]


def _gate_bwd(dc, flog, b_pad):
    def body(dc_ref, f_ref, b_ref, df_ref, db_ref):
        tri = _tri(False)

        def step(j, carry):
            run, db = carry
            i = T // 128 - 1 - j
            rows = pl.ds(pl.multiple_of(i * 128, 128), 128)
            dcb = dc_ref[rows, :]
            rb = _hdot(tri, dcb) + run
            z = f_ref[rows, :] + b_ref[...]
            df = rb * (1.0 / (1.0 + jnp.exp(z)))
            df_ref[rows, :] = df.astype(BF16)
            return run + jnp.sum(dcb, axis=0, keepdims=True), db + jnp.sum(df, axis=0, keepdims=True)

        _, db = lax.fori_loop(0, T // 128, step, (jnp.zeros((1, 128), F32), jnp.zeros((1, 128), F32)))
        db_ref[...] = jnp.broadcast_to(db, (8, 128))

    return _one_call(body, "gate_bwd", [dc, flog, b_pad], [((T, 128), BF16), ((8, 128), F32)])


def _lane_lo(rows=TQ):
    return lax.broadcasted_iota(jnp.int32, (rows, 128), 1) < HD


def _half(v, lo, a, scale=None):
    keep = lo if a == 0 else jnp.logical_not(lo)
    v = v.astype(F32) if scale is None else v.astype(F32) * scale
    return jnp.where(keep, v, 0.0).astype(BF16)


def _fox_specs():
    return [pl.BlockSpec((FQ, 128), lambda h, i: (i, h)),
            pl.BlockSpec((T, 128), lambda h, i: (0, 4 + h)),
            pl.BlockSpec((T, 128), lambda h, i: (0, 8 + h))]


def _lane_pick(x, at):
    lane = lax.broadcasted_iota(jnp.int32, x.shape, 1)
    return jnp.sum(jnp.where(lane == at, x, 0.0), axis=-1, keepdims=True)


def _fox_fwd(proj, c, ct3):
    def body(q_ref, k_ref, v_ref, c_ref, ct_ref, o_ref, l_ref):
        i = pl.program_id(1)
        lo = _lane_lo(FQ)
        causal = lax.broadcasted_iota(jnp.int32, (FQ, FQ), 1) <= lax.broadcasted_iota(jnp.int32, (FQ, FQ), 0)
        q = q_ref[...]
        qs = [_half(q, lo, a, SCALE) for a in range(2)]
        cqs = [_lane_pick(c_ref[...], 2 * pl.program_id(0) + a) for a in range(2)]

        def tile(off, carry, diagonal):
            kblk = k_ref[pl.ds(off, FQ), :]
            vblk = v_ref[pl.ds(off, FQ), :]
            new = []
            for a in range(2):
                m, l, acc = carry[a]
                s = _nt(qs[a], kblk) + (cqs[a] - ct_ref[a:a + 1, pl.ds(off, FQ)])
                if diagonal:
                    s = jnp.where(causal, s, NEG)
                m2 = jnp.maximum(m, jnp.max(s, axis=-1, keepdims=True))
                p = jnp.exp(s - m2)
                alpha = jnp.exp(m - m2)
                new.append((m2, alpha * l + jnp.sum(p, axis=-1, keepdims=True),
                            alpha * acc + _nn(p.astype(BF16), vblk)))
            return tuple(new)

        init = (jnp.full((FQ, 1), NEG, F32), jnp.zeros((FQ, 1), F32), jnp.zeros((FQ, 128), F32))
        carry = lax.fori_loop(0, i, lambda kb, c: tile(pl.multiple_of(kb * FQ, FQ), c, False), (init, init))
        carry = tile(pl.multiple_of(i * FQ, FQ), carry, True)
        outs = []
        for a in range(2):
            m, l, acc = carry[a]
            outs.append(acc / l)
            l_ref[:, 128 * a:128 * a + 128] = jnp.broadcast_to(m + jnp.log(l), (FQ, 128))
        o_ref[...] = jnp.where(lo, outs[0], outs[1])

    return _pcall(
        body, name="fox_fwd", grid=(4, T // FQ),
        in_specs=_fox_specs() + [pl.BlockSpec((FQ, 128), lambda h, i: (i, 0)),
                                 pl.BlockSpec((None, 2, T), lambda h, i: (h, 0, 0))],
        out_specs=[pl.BlockSpec((FQ, 128), lambda h, i: (i, h)), pl.BlockSpec((FQ, 256), lambda h, i: (i, h))],
        out_shape=[jax.ShapeDtypeStruct((T, 512), F32), jax.ShapeDtypeStruct((T, 1024), F32)],
        compiler_params=pltpu.CompilerParams(dimension_semantics=("arbitrary", "arbitrary"), vmem_limit_bytes=VMEM_BIG),
    )(proj, proj, proj, c, ct3)


def _fox_bwd(proj, c, ct3, o, lse, do, after=()):
    def body(q_ref, k_ref, v_ref, c_ref, ct_ref, o_ref, l_ref, do_ref, dq_ref, dkb_ref, dvb_ref, dct_ref, dcq_ref,
             dk_ref, dv_ref):
        i = pl.program_id(1)

        @pl.when(i == 0)
        def _():
            dk_ref[...] = jnp.zeros_like(dk_ref)
            dv_ref[...] = jnp.zeros_like(dv_ref)
            dct_ref[...] = jnp.zeros_like(dct_ref)

        lo = _lane_lo(FQ)
        causal = lax.broadcasted_iota(jnp.int32, (FQ, FQ), 1) <= lax.broadcasted_iota(jnp.int32, (FQ, FQ), 0)
        q = q_ref[...]
        do_v = do_ref[...]
        prod = do_v * o_ref[...]
        qs = [_half(q, lo, a, SCALE) for a in range(2)]
        dos = [_half(do_v, lo, a) for a in range(2)]
        deltas = [jnp.sum(jnp.where(lo if a == 0 else jnp.logical_not(lo), prod, 0.0), axis=-1, keepdims=True)
                  for a in range(2)]
        cqs = [_lane_pick(c_ref[...], 2 * pl.program_id(0) + a) for a in range(2)]
        las = [l_ref[:, 128 * a:128 * a + 1] for a in range(2)]

        def tile(off, carry, diagonal):
            kblk = k_ref[pl.ds(off, FQ), :]
            vblk = v_ref[pl.ds(off, FQ), :]
            new = []
            dk = jnp.zeros((128, FQ), F32)
            dv = jnp.zeros((128, FQ), F32)
            for a in range(2):
                dq_acc, rs = carry[a]
                s = _nt(qs[a], kblk) + (cqs[a] - ct_ref[a:a + 1, pl.ds(off, FQ)])
                if diagonal:
                    s = jnp.where(causal, s, NEG)
                p = jnp.exp(s - las[a])
                ds = p * (_nt(dos[a], vblk) - deltas[a])
                dsb = ds.astype(BF16)
                dk = dk + _tn(qs[a], dsb)
                dv = dv + _tn(dos[a], p.astype(BF16))
                dct_ref[a:a + 1, pl.ds(off, FQ)] -= jnp.sum(ds, axis=0, keepdims=True)
                new.append((dq_acc + _nn(dsb, kblk), rs + jnp.sum(ds, axis=-1, keepdims=True)))
            dk_ref[:, pl.ds(off, FQ)] += dk
            dv_ref[:, pl.ds(off, FQ)] += dv
            return tuple(new)

        init = (jnp.zeros((FQ, 128), F32), jnp.zeros((FQ, 1), F32))
        carry = lax.fori_loop(0, i, lambda kb, c: tile(pl.multiple_of(kb * FQ, FQ), c, False), (init, init))
        carry = tile(pl.multiple_of(i * FQ, FQ), carry, True)
        lane = lax.broadcasted_iota(jnp.int32, (FQ, 128), 1)
        dcq_ref[...] = jnp.where(lane == 0, carry[0][1], jnp.where(lane == 1, carry[1][1], 0.0))
        dq_ref[...] = (jnp.where(lo, carry[0][0], carry[1][0]) * SCALE).astype(BF16)

        @pl.when(i == T // FQ - 1)
        def _():
            dkb_ref[...] = dk_ref[...].T.astype(BF16)
            dvb_ref[...] = dv_ref[...].T.astype(BF16)

    blk = pl.BlockSpec((FQ, 128), lambda h, i: (i, h))
    wide = pl.BlockSpec((FQ, 256), lambda h, i: (i, h))
    rows = pl.BlockSpec((None, 2, T), lambda h, i: (h, 0, 0))
    col = pl.BlockSpec((T, 128), lambda h, i: (0, h))
    return _pcall(
        _behind(body, 8, after), name="fox_bwd", grid=(4, T // FQ),
        in_specs=_fox_specs() + [pl.BlockSpec((FQ, 128), lambda h, i: (i, 0)), rows, blk, wide, blk] + [ANY_SPEC] * len(after),
        out_specs=[blk, col, col, rows, pl.BlockSpec((None, FQ, 128), lambda h, i: (h, i, 0))],
        out_shape=[jax.ShapeDtypeStruct((T, 512), BF16), jax.ShapeDtypeStruct((T, 512), BF16),
                   jax.ShapeDtypeStruct((T, 512), BF16), jax.ShapeDtypeStruct((4, 2, T), F32),
                   jax.ShapeDtypeStruct((4, T, 128), F32)],
        scratch_shapes=[pltpu.VMEM((128, T), F32), pltpu.VMEM((128, T), F32)],
        compiler_params=pltpu.CompilerParams(dimension_semantics=("arbitrary", "arbitrary"), vmem_limit_bytes=VMEM_BIG),
    )(proj, proj, proj, c, ct3, o, lse, do, *after)


def _rel_onehot():
    ridx = lax.broadcasted_iota(jnp.int32, (NREL_PAD, VW), 0)
    j = lax.broadcasted_iota(jnp.int32, (NREL_PAD, VW), 1)
    return jnp.where(ridx == jnp.clip(TQ + LEFT - 1 - j, -128, 128) + 128, 1.0, 0.0).astype(F32)


def _relvec_fwd(tbl):
    def body(t_ref, v_ref):
        v_ref[...] = _hdot(t_ref[...], _rel_onehot())

    return _one_call(body, "relvec_fwd", [tbl], [((8, VW), F32)])[0]


def _relvec_bwd(gv):
    def body(g_ref, t_ref):
        t_ref[...] = lax.dot_general(g_ref[...], _rel_onehot(), (((1,), (1,)), ((), ())),
                                     preferred_element_type=F32, precision=lax.Precision.HIGHEST)

    return _one_call(body, "relvec_bwd", [gv], [((8, NREL_PAD), F32)])[0]


def _chk_bias(vt_ref, a, hidden):
    vb = jnp.broadcast_to(vt_ref[a:a + 1, :], (TQ, VW))
    y = pltpu.roll(vb, VW - (TQ - 1), 1, stride=1, stride_axis=0)[:, :WIN]
    cr = lax.broadcasted_iota(jnp.int32, (TQ, WIN), 0) // 64
    m = lax.broadcasted_iota(jnp.int32, (TQ, WIN), 1)
    return jnp.where((m // 64 >= cr) & (m // 64 <= cr + 8) & (m >= hidden), y, NEG)


def _chk_specs():
    return [pl.BlockSpec((TQ, 128), lambda h, i: (i, CHK0 // 128 + h)),
            pl.BlockSpec((T + LEFT, 128), lambda h, i: (0, h)),
            pl.BlockSpec((T + LEFT, 128), lambda h, i: (0, 4 + h)),
            pl.BlockSpec((None, 2, VW), lambda h, i: (h, 0, 0))]


def _chk_fwd(proj, kvp, vt3, after=()):
    def body(q_ref, k_ref, v_ref, vt_ref, o_ref, l_ref, bias_ref):
        i = pl.program_id(1)

        @pl.when(i == 0)
        def _():
            for first in range(3):
                for a in range(2):
                    bias_ref[first, a] = _chk_bias(vt_ref, a, max(LEFT - first * TQ, 0))

        lo = _lane_lo()
        off = pl.multiple_of(i * TQ, TQ)
        kw = k_ref[pl.ds(off, WIN), :]
        vw = v_ref[pl.ds(off, WIN), :]
        bias_at = jnp.minimum(i, 2)
        q = q_ref[...]
        outs = []
        for a in range(2):
            s = _nt(_half(q, lo, a, SCALE), kw) + bias_ref[bias_at, a]
            m = jnp.max(s, axis=-1, keepdims=True)
            p = jnp.exp(s - m)
            l = jnp.sum(p, axis=-1, keepdims=True)
            outs.append(_nn(p.astype(BF16), vw) / l)
            l_ref[:, 128 * a:128 * a + 128] = jnp.broadcast_to(m + jnp.log(l), (TQ, 128))
        o_ref[...] = jnp.where(lo, outs[0], outs[1])

    return _pcall(
        _behind(body, 4, after), name="chk_fwd", grid=(4, T // TQ), in_specs=_chk_specs() + [ANY_SPEC] * len(after),
        out_specs=[pl.BlockSpec((TQ, 128), lambda h, i: (i, h)), pl.BlockSpec((TQ, 256), lambda h, i: (i, h))],
        out_shape=[jax.ShapeDtypeStruct((T, 512), F32), jax.ShapeDtypeStruct((T, 1024), F32)],
        scratch_shapes=[pltpu.VMEM((3, 2, TQ, WIN), F32)],
        compiler_params=pltpu.CompilerParams(dimension_semantics=("arbitrary", "arbitrary")),
    )(proj, kvp, kvp, vt3, *after)


def _chk_bwd(proj, kvp, vt3, o, lse, do, after=()):
    nq = T // TQ

    def body(q_ref, k_ref, v_ref, vt_ref, o_ref, l_ref, do_ref, dq_ref, dkb_ref, dvb_ref, gv_ref, bias_ref, dsum_ref,
             dk_ref, dv_ref):
        i = pl.program_id(1)

        @pl.when(i == 0)
        def _():
            for first in range(3):
                for a in range(2):
                    bias_ref[first, a] = _chk_bias(vt_ref, a, max(LEFT - first * TQ, 0))
            dsum_ref[...] = jnp.zeros_like(dsum_ref)
            dk_ref[...] = jnp.zeros_like(dk_ref)
            dv_ref[...] = jnp.zeros_like(dv_ref)

        lo = _lane_lo()
        off = pl.multiple_of(i * TQ, TQ)
        kw = k_ref[pl.ds(off, WIN), :]
        vw = v_ref[pl.ds(off, WIN), :]
        bias_at = jnp.minimum(i, 2)
        q = q_ref[...]
        do_v = do_ref[...]
        prod = do_v * o_ref[...]
        dqs = []
        for a in range(2):
            keep = lo if a == 0 else jnp.logical_not(lo)
            qa = _half(q, lo, a, SCALE)
            doa = _half(do_v, lo, a)
            delta = jnp.sum(jnp.where(keep, prod, 0.0), axis=-1, keepdims=True)
            s = _nt(qa, kw) + bias_ref[bias_at, a]
            p = jnp.exp(s - l_ref[:, 128 * a:128 * a + 1])
            ds = p * (_nt(doa, vw) - delta)
            dsum_ref[a] += ds
            dsb = ds.astype(BF16)
            dk_ref[:, pl.ds(off, WIN)] += _tn(qa, dsb)
            dv_ref[:, pl.ds(off, WIN)] += _tn(doa, p.astype(BF16))
            dqs.append(_nn(dsb, kw))
        dq_ref[...] = (jnp.where(lo, dqs[0], dqs[1]) * SCALE).astype(BF16)

        @pl.when(i == nq - 1)
        def _():
            dkb_ref[...] = dk_ref[:, LEFT:].T.astype(BF16)
            dvb_ref[...] = dv_ref[:, LEFT:].T.astype(BF16)
            rr = lax.broadcasted_iota(jnp.int32, (TQ, TQ), 0)
            cc = lax.broadcasted_iota(jnp.int32, (TQ, TQ), 1)
            flip = jnp.where(rr + cc == TQ - 1, 1.0, 0.0).astype(F32)
            for a in range(2):
                dpad = jnp.concatenate([dsum_ref[a], jnp.zeros((TQ, VW - WIN), F32)], axis=1)
                z = pltpu.roll(_hdot(flip, dpad), 0, 1, stride=1, stride_axis=0)
                gv_ref[a:a + 1, :] = jnp.sum(z, axis=0, keepdims=True)

    blk = pl.BlockSpec((TQ, 128), lambda h, i: (i, h))
    wide = pl.BlockSpec((TQ, 256), lambda h, i: (i, h))
    col = pl.BlockSpec((T, 128), lambda h, i: (0, h))
    return _pcall(
        _behind(body, 7, after), name="chk_bwd", grid=(4, nq), in_specs=_chk_specs() + [blk, wide, blk] + [ANY_SPEC] * len(after),
        out_specs=[blk, col, col, pl.BlockSpec((None, 2, VW), lambda h, i: (h, 0, 0))],
        out_shape=[jax.ShapeDtypeStruct((T, 512), BF16), jax.ShapeDtypeStruct((T, 512), BF16),
                   jax.ShapeDtypeStruct((T, 512), BF16), jax.ShapeDtypeStruct((4, 2, VW), F32)],
        scratch_shapes=[pltpu.VMEM((3, 2, TQ, WIN), F32), pltpu.VMEM((2, TQ, WIN), F32),
                        pltpu.VMEM((128, T + LEFT), F32), pltpu.VMEM((128, T + LEFT), F32)],
        compiler_params=pltpu.CompilerParams(dimension_semantics=("arbitrary", "arbitrary")),
    )(proj, kvp, kvp, vt3, o, lse, do, *after)


def _zero_at_start(*refs):
    @pl.when(pl.program_id(0) == 0)
    def _():
        for r in refs:
            r[...] = jnp.zeros_like(r)


def _ffn_step(h3, x2, tgt, w1_t, w2, g_post, g_pre):
    def body(h_ref, x2_ref, t_ref, w1_ref, w2_ref, gp_ref, gf_ref, dx2_ref, da_ref, dy_ref, r_ref, loss_ref, dgp_ref, dgf_ref):
        _zero_at_start(loss_ref, dgp_ref, dgf_ref)
        w1, w2v = _w(w1_ref), _w(w2_ref)
        ra = jnp.maximum(_nt(h_ref[...], w1), 0.0)
        r = jnp.square(ra).astype(BF16)
        r_ref[...] = r
        y = _nn(r, w2v)
        x2v = x2_ref[...]
        e = x2v + _rms(y, gp_ref[...]) - t_ref[...]
        loss_ref[...] += 0.5 * jnp.sum(jnp.sum(e * e, axis=-1, keepdims=True) * (1.0 / D))
        dx3 = e * (1.0 / D)
        dy, dgp = _rms_bwd(y, gp_ref[...], dx3)
        dgp_ref[...] += dgp
        dyb = dy.astype(BF16)
        dy_ref[...] = dyb
        da = (_nt(dyb, w2v) * (2.0 * ra)).astype(BF16)
        da_ref[...] = da
        dh, dgf = _rms_bwd(x2v, gf_ref[...], _nn(da, w1))
        dgf_ref[...] += dgf
        dx2_ref[...] = dx3 + dh

    return _tok_call(body, "ffn_step", [h3, x2, tgt], [w1_t, w2, g_post, g_pre],
                     [(D, F32), (DFF, BF16), (D, BF16), (DFF, BF16)], [(8, 128), (1, D), (1, D)], vmem=VMEM_BIG)


def _mem_bwd(dx2, ym, x1, qm, km, vm, w_mo, w_mq, g_post, g_pre, after=()):
    def body(dx2_ref, ym_ref, x1_ref, q_ref, k_ref, v_ref, wo_ref, wq_ref, gp_ref, gm_ref,
             dx1_ref, dym_ref, dq_ref, dk_ref, dv_ref, dgp_ref, dgm_ref, dom_ref):
        _zero_at_start(dk_ref, dv_ref, dgp_ref, dgm_ref)
        dx2_v = dx2_ref[...]
        dym, dgp = _rms_bwd(ym_ref[...], gp_ref[...], dx2_v)
        dgp_ref[...] += dgp
        dymb = dym.astype(BF16)
        dym_ref[...] = dymb
        dom_ref[...] = _nt(dymb, _w(wo_ref)).astype(BF16)
        for h in range(MEM_HEADS):
            sl = slice(h * MEM_HD, (h + 1) * MEM_HD)
            qh, kh, doh = q_ref[:, sl], k_ref[:, sl], dom_ref[:, sl]
            s = _nt(qh, kh) * MEM_SCALE
            p = jnp.exp(s - jnp.max(s, axis=-1, keepdims=True))
            p = p / jnp.sum(p, axis=-1, keepdims=True)
            dp = _nt(doh, v_ref[:, sl])
            ds = (p * (dp - jnp.sum(p * dp, axis=-1, keepdims=True))).astype(BF16)
            dq_ref[:, sl] = (_nn(ds, kh) * MEM_SCALE).astype(BF16)
            dk_ref[:, sl] += _tn(ds, qh) * MEM_SCALE
            dv_ref[:, sl] += _tn(p.astype(BF16), doh)
        dh, dgm = _rms_bwd(x1_ref[...], gm_ref[...], _nt(dq_ref[...], _w(wq_ref)))
        dgm_ref[...] += dgm
        dx1_ref[...] = dx2_v + dh

    tiled = pl.BlockSpec((TM_WIDE, D), lambda i: (i, 0))
    in_specs = [tiled] * 4 + [_resident(a) for a in (km, vm, w_mo, w_mq, g_post, g_pre)] + [ANY_SPEC] * len(after)
    w_mo, w_mq = w_mo[0], w_mq[0]
    kv = pl.BlockSpec((NMEM, D), lambda i: (0, 0))
    vec = pl.BlockSpec((1, D), lambda i: (0, 0))
    return _pcall(
        _behind(body, 10, after), name="mem_bwd", grid=(T // TM_WIDE,), in_specs=in_specs,
        out_specs=[tiled, tiled, tiled, kv, kv, vec, vec],
        out_shape=[jax.ShapeDtypeStruct((T, D), F32), jax.ShapeDtypeStruct((T, D), BF16),
                   jax.ShapeDtypeStruct((T, D), BF16), jax.ShapeDtypeStruct((NMEM, D), F32),
                   jax.ShapeDtypeStruct((NMEM, D), F32), jax.ShapeDtypeStruct((1, D), F32),
                   jax.ShapeDtypeStruct((1, D), F32)],
        scratch_shapes=[pltpu.VMEM((TM_WIDE, D), BF16)],
        compiler_params=pltpu.CompilerParams(dimension_semantics=("arbitrary",), vmem_limit_bytes=VMEM_BIG),
    )(dx2, ym, x1, qm, km, vm, w_mo, w_mq, g_post, g_pre, *after)


def _memkv_bwd(dkm, dvm, mem, w_mk, w_mv):
    def body(dk_ref, dv_ref, m_ref, wk_ref, wv_ref, dg_ref):
        dmn = _nt(dk_ref[...].astype(BF16), _w(wk_ref)) + _nt(dv_ref[...].astype(BF16), _w(wv_ref))
        mv = m_ref[...]
        dg_ref[...] = jnp.sum(dmn * (mv * _rstd(mv)), axis=0, keepdims=True)

    return _one_call(body, "memkv_bwd", [dkm, dvm, mem, w_mk, w_mv], [((1, D), F32)], vmem=VMEM_BIG)[0]


def _postmix_bwd(dx1, z, o_f, o_c, w_out, g_post, g_fo, g_co, after=()):
    def body(dx1_ref, z_ref, of_ref, oc_ref, wo_ref, gp_ref, gfo_ref, gco_ref,
             dz_ref, dof_ref, doc_ref, dgp_ref, dgfo_ref, dgco_ref):
        _zero_at_start(dgp_ref, dgfo_ref, dgco_ref)
        dz, dgp = _rms_bwd(z_ref[...], gp_ref[...], dx1_ref[...])
        dgp_ref[...] += dgp
        dzb = dz.astype(BF16)
        dz_ref[...] = dzb
        dy = _nt(dzb, _w(wo_ref))
        dof, dgfo = _rms_bwd(of_ref[...], gfo_ref[...], dy[:, :512])
        doc, dgco = _rms_bwd(oc_ref[...], gco_ref[...], dy[:, 512:])
        dof_ref[...] = dof
        doc_ref[...] = doc
        dgfo_ref[...] += dgfo
        dgco_ref[...] += dgco

    return _tok_call(body, "postmix_bwd", [dx1, z, o_f, o_c], [w_out, g_post, g_fo, g_co],
                     [(D, BF16), (512, F32), (512, F32)], [(1, D), (1, 512), (1, 512)], tm=TM_WIDE, vmem=VMEM_BIG,
                     after=after)


def _premix_bwd(dx1, x, pieces, win_t, g_pre, after=()):
    def body(dx1_ref, x_ref, *refs):
        piece_refs, (w_ref, g_ref, dx_ref, dg_ref, dp_ref) = refs[:len(pieces)], refs[len(pieces):]
        _zero_at_start(dg_ref)
        col = 0
        for p in piece_refs:
            dp_ref[:, col:col + p.shape[1]] = p[...]
            col += p.shape[1]
        dh, dg = _rms_bwd(x_ref[...], g_ref[...], _nn(dp_ref[...], w_ref[...]))
        dg_ref[...] += dg
        dx_ref[...] = dx1_ref[...] + dh

    return _tok_call(body, "premix_bwd", [dx1, x] + list(pieces), [win_t, g_pre], [(D, F32)], [(1, D)],
                     tm=TM_WIDE, vmem=VMEM_BIG, after=after, scratch=[pltpu.VMEM((TM_WIDE, PROJ), BF16)])


def _wgrad_group(name, pairs, rows):
    def body(*refs):
        o_ref = refs[-1]
        for k in range(len(pairs)):
            g = _tn(refs[2 * k][...].astype(BF16), refs[2 * k + 1][...].astype(BF16))
            o_ref[k * rows:(k + 1) * rows, :] = g.astype(BF16)

    in_specs, ops = [], []
    for a, b in pairs:
        in_specs += [pl.BlockSpec((a.shape[0], rows), lambda j: (0, j)), _resident(b)]
        ops += [a, b]
    return _pcall(
        body, name=name, grid=(8,), in_specs=in_specs,
        out_specs=pl.BlockSpec((None, len(pairs) * rows, D), lambda j: (j, 0, 0)),
        out_shape=jax.ShapeDtypeStruct((8, len(pairs) * rows, D), BF16),
        compiler_params=pltpu.CompilerParams(dimension_semantics=("arbitrary",), vmem_limit_bytes=VMEM_BIG),
    )(*ops)


def _wgrad_whole(name, pairs):
    n = len(pairs)
    ops = [x for pair in pairs for x in pair]
    rows = pairs[0][0].shape[1] // 8

    def body(*refs):
        hbm, o_ref, bufs, sem = refs[:2 * n], refs[2 * n], refs[2 * n + 1:4 * n + 1], refs[4 * n + 1]
        k = pl.program_id(0)
        copies = [pltpu.make_async_copy(hbm[t], bufs[t], sem.at[t]) for t in range(2 * n)]

        @pl.when(k == 0)
        def _():
            for copy in copies:
                copy.start()

        for t in range(n):
            @pl.when(k == t)
            def _(t=t):
                copies[2 * t].wait()
                copies[2 * t + 1].wait()
                g = _tn(bufs[2 * t][...].astype(BF16), bufs[2 * t + 1][...].astype(BF16))
                o_ref[...] = g.reshape(8, rows, D).astype(BF16)

    return _pcall(
        body, name=name, grid=(n,), in_specs=[ANY_SPEC] * (2 * n),
        out_specs=pl.BlockSpec((8, rows, D), lambda k: (0, k, 0)),
        out_shape=jax.ShapeDtypeStruct((8, n * rows, D), BF16),
        scratch_shapes=[pltpu.VMEM(x.shape, x.dtype) for x in ops] + [pltpu.SemaphoreType.DMA((2 * n,))],
        compiler_params=pltpu.CompilerParams(dimension_semantics=("arbitrary",), vmem_limit_bytes=VMEM_BIG),
    )(*ops)


def _adam_math(w, g, m, v):
    m2 = ADAM_B1 * m + (1.0 - ADAM_B1) * g
    v2 = ADAM_B2 * v + (1.0 - ADAM_B2) * jnp.square(g)
    m_hat = m2 / (1.0 - ADAM_B1 ** ADAM_STEP)
    v_hat = v2 / (1.0 - ADAM_B2 ** ADAM_STEP)
    delta = -ADAM_LR * (m_hat / (jnp.sqrt(v_hat) + ADAM_EPS) + ADAM_WD * w)
    return delta, m2, v2


def _adamw_small(gparts, ws, ms, vs):
    n = len(SMALL)

    def body(g_ref, *refs):
        w_refs, m_refs, v_refs = refs[:n], refs[n:2 * n], refs[2 * n:3 * n]
        outs, sum_ref = refs[3 * n:-1], refs[-1]
        g = g_ref[0]
        for k in range(1, 8):
            g = g + g_ref[k]
        sum_ref[...] = g
        outs[0][...] = sum_ref[17:18, 0:128]
        for t, name in enumerate(SMALL):
            r0, nr, c0, nc = SMALL_SLOT[name]
            gt = sum_ref[r0:r0 + nr, c0:c0 + nc]
            out = (gt,) + _adam_math(w_refs[t][...], gt, m_refs[t][...], v_refs[t][...])
            for o_ref, val in zip(outs[1 + 4 * t:5 + 4 * t], out):
                o_ref[...] = val

    whole = lambda s: pl.BlockSpec(s, lambda i, nd=len(s): (0,) * nd)
    ins = [gparts] + list(ws) + list(ms) + list(vs)
    out_shapes = [(1, 128)] + [a.shape for a in ws for _ in range(4)]
    return _pcall(
        body, name="adamw_small", grid=(1,), in_specs=[whole(a.shape) for a in ins],
        out_specs=[whole(s) for s in out_shapes], out_shape=[jax.ShapeDtypeStruct(s, F32) for s in out_shapes],
        scratch_shapes=[pltpu.VMEM((SMALL_ROWS, D), F32)],
        compiler_params=pltpu.CompilerParams(dimension_semantics=("arbitrary",)),
    )(*ins)


def _row_tile(rows):
    return next(t for t in (512, 400, 320) if rows % t == 0)


def _add_halves(g4, theirs, core, name):
    rows = g4.shape[2]
    tr = _row_tile(rows)

    def body(c_ref, a_ref, b_ref, o_ref):
        o_ref[...] = (a_ref[...].astype(F32) + b_ref[...].astype(F32)).astype(BF16)

    grid_spec = pltpu.PrefetchScalarGridSpec(
        num_scalar_prefetch=1, grid=(4, rows // tr),
        in_specs=[pl.BlockSpec((None, None, tr, D), lambda j, i, c: (j, c[0], i, 0)),
                  pl.BlockSpec((None, None, tr, D), lambda j, i, c: (j, 0, i, 0))],
        out_specs=pl.BlockSpec((None, tr, D), lambda j, i, c: (j, i, 0)))
    return _pcall(
        body, name=name, grid_spec=grid_spec, out_shape=jax.ShapeDtypeStruct((4, rows, D), BF16),
        compiler_params=pltpu.CompilerParams(dimension_semantics=("arbitrary", "arbitrary")),
    )(core, g4, theirs)


def _sum_adam(own, got, order, r0, w, m, v, name, transposed=False):
    ragged = w.ndim == 3
    n = w.shape[1] if transposed else w.shape[0]
    tr = n if ragged else min(n, 256)
    rows = own.shape[1] if ragged else tr
    at = (slice(None), 0, slice(None)) if ragged else Ellipsis

    def body(o_ref, a_ref, b_ref, c_ref, d_ref, w_ref, m_ref, v_ref, *out_refs):
        f = lambda r: r[0:tr, :].astype(F32)
        g = ((f(a_ref) + f(b_ref)) + f(c_ref)) + f(d_ref)
        g = g.T if transposed else g
        for ref, val in zip(out_refs, (g,) + _adam_math(w_ref[at], g, m_ref[at], v_ref[at])):
            ref[at] = val

    slot = lambda k: pl.BlockSpec((None, rows, D), lambda i, o: (o[k], r0 // rows + i, 0))
    if ragged:
        wspec = pl.BlockSpec((n, 1, D), lambda i, o: (0, 0, 0))
    else:
        wspec = pl.BlockSpec((D, tr), lambda i, o: (0, i)) if transposed else pl.BlockSpec((tr, D), lambda i, o: (i, 0))
    grid_spec = pltpu.PrefetchScalarGridSpec(
        num_scalar_prefetch=1, grid=(n // tr,), in_specs=[slot(0), slot(1), slot(2), slot(3), wspec, wspec, wspec],
        out_specs=[wspec] * 4)
    return _pcall(
        body, name=name, grid_spec=grid_spec, out_shape=[jax.ShapeDtypeStruct(w.shape, F32)] * 4,
        compiler_params=pltpu.CompilerParams(dimension_semantics=("arbitrary",)),
    )(order, own, got, got, got, w, m, v)


def _sum_adam_rows(own, got, order, ws, ms, vs, name):
    n, rows = len(ws), ws[0].shape[0]

    def body(o_ref, a_ref, b_ref, c_ref, d_ref, *refs):
        ins, outs = refs[:3 * n], refs[3 * n:]
        for t in range(n):
            r = slice(t * rows, (t + 1) * rows)
            f = lambda ref: ref[r, :].astype(F32)
            g = ((f(a_ref) + f(b_ref)) + f(c_ref)) + f(d_ref)
            out = (g,) + _adam_math(ins[t][...], g, ins[n + t][...], ins[2 * n + t][...])
            for o, val in zip(outs[4 * t:4 * t + 4], out):
                o[...] = val

    slot = lambda k: pl.BlockSpec((None, n * rows, D), lambda i, o: (o[k], 0, 0), pipeline_mode=pl.Buffered(1))
    wspec = pl.BlockSpec((rows, D), lambda i, o: (0, 0), pipeline_mode=pl.Buffered(1))
    grid_spec = pltpu.PrefetchScalarGridSpec(
        num_scalar_prefetch=1, grid=(1,), in_specs=[slot(0), slot(1), slot(2), slot(3)] + [wspec] * (3 * n),
        out_specs=[pl.BlockSpec((rows, D), lambda i, o: (0, 0))] * (4 * n))
    return _pcall(
        body, name=name, grid_spec=grid_spec, out_shape=[jax.ShapeDtypeStruct((rows, D), F32)] * (4 * n),
        compiler_params=pltpu.CompilerParams(dimension_semantics=("arbitrary",), vmem_limit_bytes=VMEM_BIG),
    )(order, own, got, got, got, *ws, *ms, *vs)


def _place():
    return lax.axis_index("x"), lax.axis_index("y"), lax.axis_index("c")


def _allgather(block, name, after=()):
    rows = block.shape[0]
    split = (rows // 2 + 15) // 16 * 16

    def body(x_ref, out_ref, token, send_sems, recv_sems, local_sem):
        token[...] = jnp.zeros_like(token)
        x, y, c = _place()
        me, sib = (x, y, c), (x, y, 1 - c)
        xn, yn, dg = (1 - x, y), (x, 1 - y), (1 - x, 1 - y)
        lo, hi = pl.ds(0, split), pl.ds(split, rows - split)

        def copy(k, blk, to, part=None, src=None):
            index = 4 * blk[0] + 2 * blk[1] + blk[2]
            view = out_ref.at[index] if part is None else out_ref.at[index, part]
            return pltpu.make_async_remote_copy(
                src_ref=view if src is None else src, dst_ref=view,
                send_sem=send_sems.at[k], recv_sem=recv_sems.at[k], device_id=to, device_id_type=MESH)

        def start(*copies):
            for cp in copies:
                cp.start()
            return list(copies)

        mine = pltpu.make_async_copy(x_ref, out_ref.at[4 * x + 2 * y + c], local_sem)
        mine.start()
        sent = start(copy(0, me, sib, src=x_ref), copy(1, me, (*xn, c), src=x_ref), copy(2, me, (*yn, c), src=x_ref))
        copy(1, (*xn, c), me).wait_recv()
        sent += start(copy(3, (*xn, c), sib), copy(5, (*xn, c), (*yn, c), part=lo))
        copy(2, (*yn, c), me).wait_recv()
        sent += start(copy(4, (*yn, c), sib), copy(6, (*yn, c), (*xn, c), part=hi))
        copy(5, (*dg, c), me, part=lo).wait_recv()
        copy(6, (*dg, c), me, part=hi).wait_recv()
        sent += start(copy(7, (*dg, c), sib))
        for k, blk in ((0, sib), (3, (*xn, 1 - c)), (4, (*yn, 1 - c)), (7, (*dg, 1 - c))):
            copy(k, blk, me).wait_recv()
        for cp in sent:
            cp.wait_send()
        mine.wait()

    return _pcall(
        _behind(body, 1, after), name=name,
        out_shape=[jax.ShapeDtypeStruct((8,) + block.shape, block.dtype), jax.ShapeDtypeStruct((8, 128), F32)],
        in_specs=[pl.BlockSpec(memory_space=pl.ANY)] * (1 + len(after)),
        out_specs=[pl.BlockSpec(memory_space=pl.ANY), pl.BlockSpec(memory_space=pltpu.VMEM)],
        scratch_shapes=[pltpu.SemaphoreType.DMA((8,)), pltpu.SemaphoreType.DMA((8,)), pltpu.SemaphoreType.DMA(())],
        compiler_params=pltpu.CompilerParams(has_side_effects=True),
    )(block, *after)


HBM_SPEC = pl.BlockSpec(memory_space=pltpu.HBM)
SEM_SPEC = pl.BlockSpec(memory_space=pltpu.SEMAPHORE)
ANY_SPEC = pl.BlockSpec(memory_space=pl.ANY)
EFFECT = pltpu.SideEffectType.DATAFLOW_SIDE_EFFECTING


def _in_hbm(a):
    return pltpu.with_memory_space_constraint(a, pltpu.HBM)


def _start_copies(name, src, land_shape, plan, n):
    def body(src_ref, land_ref, send_sems, recv_sems, src_thru, land_thru, token):
        for k, (s, d, to, _) in enumerate(plan(src_ref, land_ref)):
            pltpu.make_async_remote_copy(src_ref=s, dst_ref=d, send_sem=send_sems.at[k], recv_sem=recv_sems.at[k],
                                         device_id=to, device_id_type=MESH).start()
        token[...] = jnp.zeros_like(token)

    return _pcall(
        body, name=name,
        out_shape=(pltpu.SemaphoreType.DMA((n,)), pltpu.SemaphoreType.DMA((n,)), pltpu.HBM(src.shape, src.dtype),
                   pltpu.HBM(land_shape, src.dtype), jax.ShapeDtypeStruct((8, 128), F32)),
        in_specs=(HBM_SPEC, HBM_SPEC),
        out_specs=(SEM_SPEC, SEM_SPEC, HBM_SPEC, HBM_SPEC, pl.BlockSpec(memory_space=pltpu.VMEM)),
        input_output_aliases={0: 2, 1: 3}, compiler_params=pltpu.CompilerParams(has_side_effects=EFFECT),
    )(_in_hbm(src), _in_hbm(lax.empty(land_shape, src.dtype)))


def _wait_copies(name, started, after, plan):
    send_sems, recv_sems, src_thru, land_thru, _ = started

    def body(src_ref, land_ref, send_sems, recv_sems, *rest):
        for k, (s, _, to, mine) in enumerate(plan(src_ref, land_ref)):
            cp = pltpu.make_async_remote_copy(src_ref=s, dst_ref=mine, send_sem=send_sems.at[k],
                                              recv_sem=recv_sems.at[k], device_id=to, device_id_type=MESH)
            cp.wait_send()
            cp.wait_recv()

    return _pcall(
        body, name=name,
        out_shape=(pltpu.HBM(src_thru.shape, src_thru.dtype), pltpu.HBM(land_thru.shape, land_thru.dtype)),
        in_specs=(HBM_SPEC, HBM_SPEC, SEM_SPEC, SEM_SPEC) + (ANY_SPEC,) * len(after), out_specs=(HBM_SPEC, HBM_SPEC),
        input_output_aliases={0: 0, 1: 1}, compiler_params=pltpu.CompilerParams(has_side_effects=EFFECT),
    )(src_thru, land_thru, send_sems, recv_sems, *after)


def _start_inplace(name, buf, plan, n):
    def body(buf_ref, send_sems, recv_sems, buf_thru, token):
        for k, (s, d, to, _) in enumerate(plan(buf_ref, buf_ref)):
            pltpu.make_async_remote_copy(src_ref=s, dst_ref=d, send_sem=send_sems.at[k], recv_sem=recv_sems.at[k],
                                         device_id=to, device_id_type=MESH).start()
        token[...] = jnp.zeros_like(token)

    return _pcall(
        body, name=name,
        out_shape=(pltpu.SemaphoreType.DMA((n,)), pltpu.SemaphoreType.DMA((n,)), pltpu.HBM(buf.shape, buf.dtype),
                   jax.ShapeDtypeStruct((8, 128), F32)),
        in_specs=(HBM_SPEC,), out_specs=(SEM_SPEC, SEM_SPEC, HBM_SPEC, pl.BlockSpec(memory_space=pltpu.VMEM)),
        input_output_aliases={0: 2}, compiler_params=pltpu.CompilerParams(has_side_effects=EFFECT),
    )(_in_hbm(buf))


def _wait_inplace(name, started, after, plan):
    send_sems, recv_sems, buf_thru, _ = started

    def body(buf_ref, send_sems, recv_sems, *rest):
        for k, (s, _, to, mine) in enumerate(plan(buf_ref, buf_ref)):
            cp = pltpu.make_async_remote_copy(src_ref=s, dst_ref=mine, send_sem=send_sems.at[k],
                                              recv_sem=recv_sems.at[k], device_id=to, device_id_type=MESH)
            cp.wait_send()
            cp.wait_recv()

    return _pcall(
        body, name=name, out_shape=pltpu.HBM(buf_thru.shape, buf_thru.dtype),
        in_specs=(HBM_SPEC, SEM_SPEC, SEM_SPEC) + (ANY_SPEC,) * len(after), out_specs=HBM_SPEC,
        input_output_aliases={0: 0}, compiler_params=pltpu.CompilerParams(has_side_effects=EFFECT),
    )(buf_thru, send_sems, recv_sems, *after)


def _gather_plan(src_ref, land_ref):
    x, y, c = _place()
    peers = [(x, y, 1 - c), (1 - x, y, c), (x, 1 - y, c)]
    return [(src_ref, land_ref.at[4 * x + 2 * y + c], p, land_ref.at[4 * p[0] + 2 * p[1] + p[2]]) for p in peers]


def _relay_plan(buf_ref, _):
    x, y, c = _place()
    slot = lambda p, pc: 4 * p[0] + 2 * p[1] + pc
    xn, yn, dg, sib = (1 - x, y), (x, 1 - y), (1 - x, 1 - y), (x, y, 1 - c)
    half = buf_ref.shape[1] // 2
    lo, hi = pl.ds(0, half), pl.ds(half, half)
    return [(buf_ref.at[slot(xn, c)], buf_ref.at[slot(xn, c)], sib, buf_ref.at[slot(xn, 1 - c)]),
            (buf_ref.at[slot(yn, c)], buf_ref.at[slot(yn, c)], sib, buf_ref.at[slot(yn, 1 - c)]),
            (buf_ref.at[slot(xn, c), lo], buf_ref.at[slot(xn, c), lo], (*yn, c), buf_ref.at[slot(dg, c), lo]),
            (buf_ref.at[slot(yn, c), hi], buf_ref.at[slot(yn, c), hi], (*xn, c), buf_ref.at[slot(dg, c), hi])]


def _swap_plan(src_ref, land_ref):
    x, y, c = _place()
    return [(src_ref.at[:, pl.ds(1 - c, 1)], land_ref, (x, y, 1 - c), land_ref)]


def _exchange_plan(src_ref, land_ref):
    x, y, c = _place()
    chips = [(1 - x, y), (x, 1 - y), (1 - x, 1 - y)]
    return [(src_ref.at[2 * px + py], land_ref.at[2 * x + y], (px, py, c), land_ref.at[2 * px + py]) for px, py in chips]


def _gather_forward(land, block):
    def body(land_ref, out_ref, send_sems, recv_sems):
        x, y, c = _place()
        chips = [(1 - x, 1 - y)]

        def copy(k, px, py, pc):
            blk = out_ref.at[4 * px + 2 * py + pc]
            return pltpu.make_async_remote_copy(src_ref=blk, dst_ref=blk, send_sem=send_sems.at[k],
                                                recv_sem=recv_sems.at[k], device_id=(x, y, 1 - c), device_id_type=MESH)

        sent = [copy(k, px, py, c) for k, (px, py) in enumerate(chips)]
        for cp in sent:
            cp.start()
        for k, (px, py) in enumerate(chips):
            copy(k, px, py, 1 - c).wait_recv()
        for cp in sent:
            cp.wait_send()

    land = _pcall(
        body, name="allgather_rest_forward", out_shape=jax.ShapeDtypeStruct(land.shape, land.dtype),
        in_specs=[ANY_SPEC], out_specs=ANY_SPEC, input_output_aliases={0: 0},
        scratch_shapes=[pltpu.SemaphoreType.DMA((1,)), pltpu.SemaphoreType.DMA((1,))],
        compiler_params=pltpu.CompilerParams(has_side_effects=True),
    )(land)

    rows = block.shape[0]
    tr = rows // 4

    def place(me_ref, x_ref, land_ref, out_ref):
        out_ref[...] = x_ref[...]

    x, y, c = _place()
    grid_spec = pltpu.PrefetchScalarGridSpec(
        num_scalar_prefetch=1, grid=(rows // tr,),
        in_specs=[pl.BlockSpec((tr, D), lambda i, me: (i, 0)), ANY_SPEC],
        out_specs=pl.BlockSpec((None, tr, D), lambda i, me: (me[0], i, 0)))
    return _pcall(
        place, name="allgather_rest_own", grid_spec=grid_spec, out_shape=jax.ShapeDtypeStruct(land.shape, land.dtype),
        input_output_aliases={2: 0}, compiler_params=pltpu.CompilerParams(dimension_semantics=("arbitrary",)),
    )((4 * x + 2 * y + c).reshape(1), block, land)


class _ReduceScatter:
    def __init__(self, name, g):
        self.name = name
        rows = g.shape[1]
        self.started = _start_copies(name + "_swap_start", g.reshape(4, 2, rows, D), (4, 1, rows, D), _swap_plan, 1)
        self.token = self.started[4]

    def halfway(self, after):
        g4, theirs = _wait_copies(self.name + "_swap_wait", self.started, after, _swap_plan)
        self.own = _add_halves(g4, theirs, lax.axis_index("c").reshape(1), self.name + "_add_halves")
        self.started = _start_copies(self.name + "_exch_start", self.own, self.own.shape, _exchange_plan, 3)
        self.token = self.started[4]

    def finish(self, after):
        own, got = _wait_copies(self.name + "_exch_wait", self.started, after, _exchange_plan)
        chip = 2 * lax.axis_index("x") + lax.axis_index("y")
        return own, got, (chip + jnp.arange(4, dtype=jnp.int32)) % 4


def _pack_small(p, loss):
    def body(*refs):
        o_ref = refs[-1]
        o_ref[...] = jnp.zeros_like(o_ref)
        for ref, name in zip(refs, SMALL):
            r0, nr, c0, nc = SMALL_SLOT[name]
            o_ref[r0:r0 + nr, c0:c0 + nc] = ref[...]
        o_ref[17:18, 0:128] = refs[len(SMALL)][0:1, :]

    return _one_call(body, "pack_small_grads", [p[n] for n in SMALL] + [loss], [((SMALL_ROWS, D), F32)])[0]


_GAP_DEV, _GAP_ROW = divmod(GATE0 + 8, N_IN)
_GAP = CHK0 - GATE0 - 8


def _in_rows_to_proj(g_ref, o_ref, acc_ref):
    runs = [(j, 0, N_IN, N_IN * j) for j in range(_GAP_DEV)]
    runs += [(_GAP_DEV, 0, _GAP_ROW, N_IN * _GAP_DEV), (_GAP_DEV, _GAP_ROW, N_IN, N_IN * _GAP_DEV + _GAP_ROW + _GAP)]
    runs += [(j, 0, N_IN, N_IN * j + _GAP) for j in range(_GAP_DEV + 1, 8)]
    acc_ref[...] = jnp.zeros_like(acc_ref)
    for j, r0, r1, dest in runs:
        start, shift = dest // 16 * 16, dest % 16
        win = -(-(shift + r1 - r0) // 16) * 16
        r = lax.broadcasted_iota(jnp.int32, (win, R_IN), 0)
        c = lax.broadcasted_iota(jnp.int32, (win, R_IN), 1)
        move = jnp.where((c >= r0) & (c < r1) & (r == c - r0 + shift), 1.0, 0.0).astype(BF16)
        acc_ref[start:start + win, :] += _nn(move, g_ref[j])
    o_ref[...] = acc_ref[...].astype(BF16)


def _wgrad_in(pieces, h1):
    n = len(pieces)
    ends = [sum(p.shape[1] for p in pieces[:k + 1]) for k in range(n)]
    assert ends[-1] == PROJ

    def body(*refs):
        piece_refs, (b_ref, o_ref, g_ref), bufs, sem = refs[:n], refs[n:n + 3], refs[n + 3:2 * n + 3], refs[2 * n + 3]
        i = pl.program_id(0)
        copies = [pltpu.make_async_copy(piece_refs[k], bufs[k], sem.at[k]) for k in range(n)]

        @pl.when(i == 0)
        def _():
            for copy in copies:
                copy.start()
            g_ref[PROJ:, :] = jnp.zeros((R_IN - N_IN + 1, D), F32)

        for k in range(n):
            @pl.when(i == k)
            def _(k=k):
                copies[k].wait()
                g_ref[ends[k] - pieces[k].shape[1]:ends[k], :] = _tn(bufs[k][...], b_ref[...])

        row = lax.broadcasted_iota(jnp.int32, (R_IN, D), 0)
        for j in range(8):
            lo = N_IN * j + (_GAP if j > _GAP_DEV else 0)
            hi = N_IN * j + (_GAP if j >= _GAP_DEV else 0)
            assert ends[min(j, n - 1)] >= min(hi + R_IN, PROJ)

            @pl.when(i == j)
            def _(lo=lo, hi=hi):
                v = g_ref[hi:hi + R_IN, :]
                if lo != hi:
                    v = jnp.where(row < _GAP_ROW, g_ref[lo:lo + R_IN, :], v)
                o_ref[...] = jnp.where(row < N_IN, v, 0.0).astype(BF16)

    return _pcall(
        body, name="wgrad_in", grid=(8,),
        in_specs=[ANY_SPEC] * n + [_resident(h1)],
        out_specs=pl.BlockSpec((None, R_IN, D), lambda i: (i, 0, 0)),
        out_shape=jax.ShapeDtypeStruct((8, R_IN, D), BF16),
        scratch_shapes=[pltpu.VMEM((PROJ + R_IN - N_IN + 1, D), F32)] + [pltpu.VMEM(p.shape, BF16) for p in pieces]
        + [pltpu.SemaphoreType.DMA((n,))],
        compiler_params=pltpu.CompilerParams(dimension_semantics=("arbitrary",), vmem_limit_bytes=VMEM_BIG),
    )(*pieces, h1)


def _local_grads(x, mem, tgt, gathered_in, gw_of, sm, on_grads, after=()):
    b_pad = jnp.pad(sm['b_fgt'], ((0, 0), (0, 120)))
    tbl = jnp.pad(sm['rel_bias'], ((0, 0), (0, NREL_PAD - 257)))

    h1, proj, flog, kvp, win_t = _premix_fwd(x, sm['g_mix_pre'], gathered_in, after)
    c = _gate_fwd(flog, b_pad)
    ct3 = c[:, :8].T.reshape(4, 2, T)
    o_f, lse_f = _fox_fwd(proj, c, ct3)
    vt3 = _relvec_fwd(tbl).reshape(4, 2, VW)
    o_c, lse_c = _chk_fwd(proj, kvp, vt3, [gw_of('relay', [o_f])])
    gw = gw_of('done', [o_c])
    w_out, w_mq, w_mk, w_mv, w_mo, w1_t, w2 = (_wblk(gw, n) for n in ('w_out', 'w_mq', 'w_mk', 'w_mv', 'w_mo', 'w_ff1', 'w_ff2'))
    ycat, z, x1, h2, qm = _postmix_fwd(x, o_f, o_c, sm['g_fox_out'], sm['g_chk_out'], w_out,
                                       sm['g_mix_post'], sm['g_mem_pre'], w_mq)
    memn, km, vm = _memkv_fwd(mem, sm['g_mem_kv'], w_mk, w_mv)
    om, ym, x2, h3 = _mem_fwd(qm, x1, km, vm, w_mo, sm['g_mem_post'], sm['g_ff_pre'])

    gs = {}
    dx2, da, dy3, r, loss_acc, gs['g_ff_post'], gs['g_ff_pre'] = _ffn_step(h3, x2, tgt, w1_t, w2, sm['g_ff_post'],
                                                                         sm['g_ff_pre'])
    tok = on_grads('A', _wgrad_group("wgrad_ff", [(da, h3), (r, dy3)], 512), None)
    dx1, dym, dqm, dkm, dvm, gs['g_mem_post'], gs['g_mem_pre'] = _mem_bwd(
        dx2, ym, x1, qm, km, vm, w_mo, w_mq, sm['g_mem_post'], sm['g_mem_pre'], [tok])
    tok = on_grads('A halfway', None, [dx1])
    gs['g_mem_kv'] = _memkv_bwd(dkm, dvm, mem, w_mk, w_mv)
    dz, dof, doc, gs['g_mix_post'], gs['g_fox_out'], gs['g_chk_out'] = _postmix_bwd(
        dx1, z, o_f, o_c, w_out, sm['g_mix_post'], sm['g_fox_out'], sm['g_chk_out'], [tok])
    tok = on_grads('B', _wgrad_whole("wgrad_mem_out", [(ycat, dz), (h2, dqm), (memn, dkm), (memn, dvm), (om, dym)]), None)
    dq_f, dk_f, dv_f, dct, dcq = _fox_bwd(proj, c, ct3, o_f, lse_f, dof, [tok])
    tok = on_grads('B halfway', None, [dq_f])
    dq_c, dk_c, dv_c, gv = _chk_bwd(proj, kvp, vt3, o_c, lse_c, doc, [tok])
    gs['rel_bias'] = _relvec_bwd(gv.reshape(8, VW))[:, :257]
    dc = jnp.pad(dct.reshape(8, T).T + dcq[:, :, :2].transpose(1, 0, 2).reshape(T, 8), ((0, 0), (0, 120)))
    dflog, db = _gate_bwd(dc, flog, b_pad)
    gs['b_fgt'] = db[0:1, :8]
    pieces = [dq_f, dk_f, dv_f, dflog, dq_c, dk_c, dv_c]
    on_grads('C', _wgrad_in(pieces, h1), None)
    tok = on_grads('C halfway', None, [gs['g_mem_kv']])
    grad_x, gs['g_mix_pre'] = _premix_bwd(dx1, x, pieces, win_t, sm['g_mix_pre'], [tok])
    return loss_acc, grad_x, gs


def kernel(x, mem, w_in, b_fgt, rel_bias, g_fox_out, g_chk_out, w_out, g_mix_pre, g_mix_post, g_mem_kv, w_mq, w_mk, w_mv, w_mo, g_mem_pre, g_mem_post, w_ff1, w_ff2, g_ff_pre, g_ff_post, loss_target, m_w_in, m_b_fgt, m_rel_bias, m_g_fox_out, m_g_chk_out, m_w_out, m_g_mix_pre, m_g_mix_post, m_g_mem_kv, m_w_mq, m_w_mk, m_w_mv, m_w_mo, m_g_mem_pre, m_g_mem_post, m_w_ff1, m_w_ff2, m_g_ff_pre, m_g_ff_post, v_w_in, v_b_fgt, v_rel_bias, v_g_fox_out, v_g_chk_out, v_w_out, v_g_mix_pre, v_g_mix_post, v_g_mem_kv, v_w_mq, v_w_mk, v_w_mv, v_w_mo, v_g_mem_pre, v_g_mem_post, v_w_ff1, v_w_ff2, v_g_ff_pre, v_g_ff_post):
    args = dict(locals())
    two_d = lambda a: a.reshape(a.shape[-2:])
    w = {n: two_d(args[n]) for n in WEIGHTS}
    m = {n: two_d(args['m_' + n]) for n in WEIGHTS}
    v = {n: two_d(args['v_' + n]) for n in WEIGHTS}

    sm = {n: w[n] for n in SMALL}
    shard_in = jnp.pad(w['w_in'].T, ((0, R_IN - N_IN), (0, 0))).astype(BF16)
    gathered_in, zero = _allgather(shard_in, "allgather_w_in")
    shard_rest = (jnp.concatenate([w['w_ff1'].T, w['w_ff2'], w['w_out'], w['w_mq'], w['w_mk'], w['w_mv'], w['w_mo']],
                                  axis=0) + zero[0, 0]).astype(BF16)
    gather = {'first': _start_copies("allgather_rest_start", shard_rest, (8, R_REST, D), _gather_plan, 3)}

    def gw_of(stage, after):
        if stage == 'relay':
            gather['block'], land = _wait_copies("allgather_rest_wait", gather['first'], after, _gather_plan)
            gather['second'] = _start_inplace("allgather_rest_relay_start", land, _relay_plan, 4)
            return gather['second'][3]
        land = _wait_inplace("allgather_rest_relay_wait", gather['second'], after, _relay_plan)
        return _gather_forward(land, gather['block'])

    rs = {}

    def on_grads(stage, g, after):
        if stage.endswith('halfway'):
            rs[stage[0]].halfway(after)
            return rs[stage[0]].token
        rs[stage] = _ReduceScatter("rs_" + stage.lower(), g)
        return rs[stage].token

    loss_local, grad_x, gs = _local_grads(x[0], mem[0], loss_target[0], gathered_in, gw_of, sm, on_grads,
                                          [gather['first'][4]])
    grads, deltas, new_m, new_v = {}, {}, {}, {}

    def update(n, out):
        grads[n], deltas[n], new_m[n], new_v[n] = out

    own, got, order = rs['A'].finish([grad_x, rs['C'].token])
    update('w_ff1', _sum_adam(own, got, order, 0, w['w_ff1'], m['w_ff1'], v['w_ff1'], "adamw_w_ff1", transposed=True))
    update('w_ff2', _sum_adam(own, got, order, 512, w['w_ff2'], m['w_ff2'], v['w_ff2'], "adamw_w_ff2"))
    own, got, order = rs['B'].finish([grad_x, rs['C'].token])
    names_b = ('w_out', 'w_mq', 'w_mk', 'w_mv', 'w_mo')
    done = _sum_adam_rows(own, got, order, [w[n] for n in names_b], [m[n] for n in names_b], [v[n] for n in names_b],
                          "adamw_group_b")
    for k, n in enumerate(names_b):
        update(n, done[4 * k:4 * k + 4])

    own, got, order = rs['C'].finish([new_v[n] for n in BIG if n != 'w_in'])
    rows_of = lambda a: jnp.transpose(a, (2, 0, 1))
    done = _sum_adam(own, got, order, 0, rows_of(w_in), rows_of(m_w_in), rows_of(v_w_in), "adamw_w_in")
    update('w_in', [jnp.transpose(a, (1, 2, 0)) for a in done])

    gparts, _ = _allgather(_pack_small(gs, loss_local), "allgather_small_grads", [got])
    small = _adamw_small(gparts, [w[n] for n in SMALL], [m[n] for n in SMALL], [v[n] for n in SMALL])
    loss = small[0][0, 0]
    for t, n in enumerate(SMALL):
        update(n, small[1 + 4 * t:5 + 4 * t])

    out = [loss, grad_x[None]]
    for group in (grads, deltas, new_m, new_v):
        out += [group[n].reshape(args[n].shape) for n in WEIGHTS]
    return tuple(out)
```

```python
import jax
import jax.numpy as jnp
from jax import lax
from jax.experimental import pallas as pl
from jax.experimental.pallas import tpu as pltpu

F32 = jnp.float32
BF16 = jnp.bfloat16
MESH = pl.DeviceIdType.MESH

T = 2048
D = 1024
NMEM = 256
DFF = 4096
EPS = 1e-6
TM = 256
TM_WIDE = 512
TQ = 256
FQ = 512
HD = 64
SCALE = HD ** -0.5
MEM_HEADS = 4
MEM_HD = 256
MEM_SCALE = MEM_HD ** -0.5
NEG = -1e30
LEFT = 512
WIN = LEFT + TQ
VW = 1024
NREL_PAD = 384
PROJ = 3200
GATE0 = 1536
CHK0 = 1664
VMEM_BIG = 56 * 1024 * 1024

ADAM_LR = 0.001
ADAM_B1 = 0.9
ADAM_B2 = 0.999
ADAM_EPS = 1e-08
ADAM_WD = 0.01
ADAM_STEP = 10

N_IN = 385
R_IN = 400
R_REST = 1664
W_ROWS = {'w_ff1': (0, 512), 'w_ff2': (512, 512),
          'w_out': (1024, 128), 'w_mq': (1152, 128), 'w_mk': (1280, 128), 'w_mv': (1408, 128), 'w_mo': (1536, 128)}
SMALL_ROWS = 24
SMALL_SLOT = {'rel_bias': (0, 8, 0, 257), 'b_fgt': (8, 1, 0, 8), 'g_fox_out': (9, 1, 0, 512), 'g_chk_out': (9, 1, 512, 512),
              'g_mix_pre': (10, 1, 0, 1024), 'g_mix_post': (11, 1, 0, 1024), 'g_mem_kv': (12, 1, 0, 1024),
              'g_mem_pre': (13, 1, 0, 1024), 'g_mem_post': (14, 1, 0, 1024), 'g_ff_pre': (15, 1, 0, 1024),
              'g_ff_post': (16, 1, 0, 1024)}

WEIGHTS = ['w_in', 'b_fgt', 'rel_bias', 'g_fox_out', 'g_chk_out', 'w_out', 'g_mix_pre', 'g_mix_post', 'g_mem_kv',
           'w_mq', 'w_mk', 'w_mv', 'w_mo', 'g_mem_pre', 'g_mem_post', 'w_ff1', 'w_ff2', 'g_ff_pre', 'g_ff_post']
BIG = ['w_in', 'w_out', 'w_mq', 'w_mk', 'w_mv', 'w_mo', 'w_ff1', 'w_ff2']
SMALL = [n for n in WEIGHTS if n not in BIG]


def _pcall(body, **kw):
    return pl.pallas_call(body, **kw)


def _nn(a, b):
    return jnp.dot(a, b, preferred_element_type=F32)


def _nt(a, b):
    return lax.dot_general(a, b, (((1,), (1,)), ((), ())), preferred_element_type=F32)


def _tn(a, b):
    return lax.dot_general(a, b, (((0,), (0,)), ((), ())), preferred_element_type=F32)


def _w(ref):
    v = ref[...]
    return v if v.ndim == 2 else v.reshape(-1, v.shape[-1])


def _rstd(x):
    return lax.rsqrt(jnp.mean(x * x, axis=-1, keepdims=True) + EPS)


def _rms(x, g):
    return x * _rstd(x) * g


def _rms_bwd(x, g, dy):
    r = _rstd(x)
    xh = x * r
    dg = jnp.sum(dy * xh, axis=0, keepdims=True)
    dxh = dy * g
    dx = r * (dxh - xh * jnp.mean(dxh * xh, axis=-1, keepdims=True))
    return dx, dg


def _resident(a):
    if isinstance(a, tuple):
        _, shape, index = a
        return pl.BlockSpec(shape, lambda *_: index, pipeline_mode=pl.Buffered(1))
    return pl.BlockSpec(a.shape, lambda *_, nd=a.ndim: (0,) * nd, pipeline_mode=pl.Buffered(1))


def _wblk(gw, name):
    r0, rows = W_ROWS[name]
    return (gw, (8, rows, D), (0, r0 // rows, 0))


def _behind(body, n_in, after):
    if not after:
        return body
    return lambda *refs: body(*refs[:n_in], *refs[n_in + len(after):])


def _tok_call(body, name, tiled, full, outs_tiled, outs_acc=(), rows=T, tm=TM, vmem=None, after=(), scratch=()):
    in_specs = [pl.BlockSpec((tm, a.shape[1]), lambda i: (i, 0)) for a in tiled]
    in_specs += [_resident(a) for a in full] + [ANY_SPEC] * len(after)
    full = [a[0] if isinstance(a, tuple) else a for a in full] + list(after)
    body = _behind(body, len(tiled) + len(full) - len(after), after)
    out_shape = [jax.ShapeDtypeStruct((rows, c), dt) for c, dt in outs_tiled]
    out_shape += [jax.ShapeDtypeStruct(s, F32) for s in outs_acc]
    out_specs = [pl.BlockSpec((tm, c), lambda i: (i, 0)) for c, _ in outs_tiled]
    out_specs += [pl.BlockSpec(s, lambda i, nd=len(s): (0,) * nd) for s in outs_acc]
    return _pcall(
        body, name=name, grid=(rows // tm,), in_specs=in_specs, out_specs=out_specs, out_shape=out_shape,
        scratch_shapes=list(scratch),
        compiler_params=pltpu.CompilerParams(dimension_semantics=("arbitrary",), vmem_limit_bytes=vmem),
    )(*tiled, *full)


def _one_call(body, name, ins, outs, vmem=None):
    whole = lambda s: pl.BlockSpec(s, lambda i, nd=len(s): (0,) * nd)
    return _pcall(
        body, name=name, grid=(1,), in_specs=[_resident(a) for a in ins], out_specs=[whole(s) for s, _ in outs],
        out_shape=[jax.ShapeDtypeStruct(s, dt) for s, dt in outs],
        compiler_params=pltpu.CompilerParams(dimension_semantics=("arbitrary",), vmem_limit_bytes=vmem),
    )(*[a[0] if isinstance(a, tuple) else a for a in ins])


def _premix_fwd(x, g_pre, gathered_in, after=()):
    def body(x_ref, g_ref, gin_ref, h_ref, proj_ref, flog_ref, kvp_ref, w_ref, acc_ref):
        s = pl.program_id(0)

        @pl.when(s == 0)
        def _():
            kvp_ref[...] = jnp.zeros_like(kvp_ref)
            _in_rows_to_proj(gin_ref, w_ref, acc_ref)

        @pl.when(s > 0)
        def _():
            h = _rms(x_ref[...], g_ref[...]).astype(BF16)
            h_ref[...] = h
            p = _nt(h, w_ref[...])
            proj_ref[...] = p.astype(BF16)
            flog_ref[...] = p[:, GATE0:GATE0 + 128]
            kvp_ref[...] = p[:, CHK0 + 512:].astype(BF16)

    tile = lambda c: pl.BlockSpec((LEFT, c), lambda s: (jnp.maximum(s - 1, 0), 0))
    return _pcall(
        _behind(body, 3, after), name="premix_fwd", grid=(T // LEFT + 1,),
        in_specs=[tile(D), _resident(g_pre), _resident(gathered_in)] + [ANY_SPEC] * len(after),
        out_specs=[tile(D), tile(PROJ), tile(128), pl.BlockSpec((LEFT, 1024), lambda s: (s, 0)),
                   pl.BlockSpec((PROJ, D), lambda s: (0, 0))],
        out_shape=[jax.ShapeDtypeStruct((T, D), BF16), jax.ShapeDtypeStruct((T, PROJ), BF16),
                   jax.ShapeDtypeStruct((T, 128), F32), jax.ShapeDtypeStruct((T + LEFT, 1024), BF16),
                   jax.ShapeDtypeStruct((PROJ, D), BF16)],
        scratch_shapes=[pltpu.VMEM((PROJ, D), F32)],
        compiler_params=pltpu.CompilerParams(dimension_semantics=("arbitrary",), vmem_limit_bytes=VMEM_BIG),
    )(x, g_pre, gathered_in, *after)


def _postmix_fwd(x, o_f, o_c, g_fo, g_co, w_out, g_post, g_mpre, w_mq):
    def body(x_ref, of_ref, oc_ref, gfo_ref, gco_ref, wo_ref, gp_ref, gm_ref, wq_ref,
             y_ref, z_ref, x1_ref, h2_ref, qm_ref):
        y_ref[:, :512] = _rms(of_ref[...], gfo_ref[...]).astype(BF16)
        y_ref[:, 512:] = _rms(oc_ref[...], gco_ref[...]).astype(BF16)
        z = _nn(y_ref[...], _w(wo_ref))
        z_ref[...] = z
        x1 = x_ref[...] + _rms(z, gp_ref[...])
        x1_ref[...] = x1
        h2 = _rms(x1, gm_ref[...]).astype(BF16)
        h2_ref[...] = h2
        qm_ref[...] = _nn(h2, _w(wq_ref)).astype(BF16)

    return _tok_call(body, "postmix_fwd", [x, o_f, o_c], [g_fo, g_co, w_out, g_post, g_mpre, w_mq],
                     [(D, BF16), (D, F32), (D, F32), (D, BF16), (D, BF16)], tm=TM_WIDE, vmem=VMEM_BIG)


def _memkv_fwd(mem, g_kv, w_mk, w_mv):
    def body(m_ref, g_ref, wk_ref, wv_ref, mn_ref, k_ref, v_ref):
        mn = _rms(m_ref[...], g_ref[...]).astype(BF16)
        mn_ref[...] = mn
        k_ref[...] = _nn(mn, _w(wk_ref)).astype(BF16)
        v_ref[...] = _nn(mn, _w(wv_ref)).astype(BF16)

    return _tok_call(body, "memkv_fwd", [mem], [g_kv, w_mk, w_mv],
                     [(D, BF16), (D, BF16), (D, BF16)], rows=NMEM, tm=NMEM, vmem=VMEM_BIG)


def _mem_fwd(qm, x1, km, vm, w_mo, g_post, g_fpre):
    def body(q_ref, x1_ref, k_ref, v_ref, wo_ref, gp_ref, gf_ref, om_ref, ym_ref, x2_ref, h3_ref):
        for h in range(MEM_HEADS):
            sl = slice(h * MEM_HD, (h + 1) * MEM_HD)
            s = _nt(q_ref[:, sl], k_ref[:, sl]) * MEM_SCALE
            p = jnp.exp(s - jnp.max(s, axis=-1, keepdims=True))
            p = p / jnp.sum(p, axis=-1, keepdims=True)
            om_ref[:, sl] = _nn(p.astype(BF16), v_ref[:, sl]).astype(BF16)
        ym = _nn(om_ref[...], _w(wo_ref))
        ym_ref[...] = ym
        x2 = x1_ref[...] + _rms(ym, gp_ref[...])
        x2_ref[...] = x2
        h3_ref[...] = _rms(x2, gf_ref[...]).astype(BF16)

    return _tok_call(body, "mem_fwd", [qm, x1], [km, vm, w_mo, g_post, g_fpre],
                     [(D, BF16), (D, F32), (D, F32), (D, BF16)], tm=TM_WIDE, vmem=VMEM_BIG)


def _tri(lower):
    r = lax.broadcasted_iota(jnp.int32, (128, 128), 0)
    c = lax.broadcasted_iota(jnp.int32, (128, 128), 1)
    return jnp.where(r >= c if lower else c >= r, 1.0, 0.0).astype(F32)


def _hdot(a, b):
    return jnp.dot(a, b, preferred_element_type=F32, precision=lax.Precision.HIGHEST)


def _gate_fwd(flog, b_pad):
    def body(f_ref, b_ref, c_ref):
        tri = _tri(True)

        def step(i, carry):
            rows = pl.ds(pl.multiple_of(i * 128, 128), 128)
            z = f_ref[rows, :] + b_ref[...]
            lf = jnp.minimum(z, 0.0) - jnp.log(1.0 + jnp.exp(-jnp.abs(z)))
            cb = _hdot(tri, lf) + carry
            c_ref[rows, :] = cb
            return cb[127:128, :]

        lax.fori_loop(0, T // 128, step, jnp.zeros((1, 128), F32))

    return _one_call(body, "gate_fwd", [flog, b_pad], [((T, 128), F32)])[0]


def _gate_bwd(dc, flog, b_pad):
    def body(dc_ref, f_ref, b_ref, df_ref, db_ref):
        tri = _tri(False)

        def step(j, carry):
            run, db = carry
            i = T // 128 - 1 - j
            rows = pl.ds(pl.multiple_of(i * 128, 128), 128)
            dcb = dc_ref[rows, :]
            rb = _hdot(tri, dcb) + run
            z = f_ref[rows, :] + b_ref[...]
            df = rb * (1.0 / (1.0 + jnp.exp(z)))
            df_ref[rows, :] = df.astype(BF16)
            return run + jnp.sum(dcb, axis=0, keepdims=True), db + jnp.sum(df, axis=0, keepdims=True)

        _, db = lax.fori_loop(0, T // 128, step, (jnp.zeros((1, 128), F32), jnp.zeros((1, 128), F32)))
        db_ref[...] = jnp.broadcast_to(db, (8, 128))

    return _one_call(body, "gate_bwd", [dc, flog, b_pad], [((T, 128), BF16), ((8, 128), F32)])


def _lane_lo(rows=TQ):
    return lax.broadcasted_iota(jnp.int32, (rows, 128), 1) < HD


def _half(v, lo, a, scale=None):
    keep = lo if a == 0 else jnp.logical_not(lo)
    v = v.astype(F32) if scale is None else v.astype(F32) * scale
    return jnp.where(keep, v, 0.0).astype(BF16)


def _fox_specs():
    return [pl.BlockSpec((FQ, 128), lambda h, i: (i, h)),
            pl.BlockSpec((T, 128), lambda h, i: (0, 4 + h)),
            pl.BlockSpec((T, 128), lambda h, i: (0, 8 + h))]


def _lane_pick(x, at):
    lane = lax.broadcasted_iota(jnp.int32, x.shape, 1)
    return jnp.sum(jnp.where(lane == at, x, 0.0), axis=-1, keepdims=True)


def _fox_fwd(proj, c, ct3):
    def body(q_ref, k_ref, v_ref, c_ref, ct_ref, o_ref, l_ref):
        i = pl.program_id(1)
        lo = _lane_lo(FQ)
        causal = lax.broadcasted_iota(jnp.int32, (FQ, FQ), 1) <= lax.broadcasted_iota(jnp.int32, (FQ, FQ), 0)
        q = q_ref[...]
        qs = [_half(q, lo, a, SCALE) for a in range(2)]
        cqs = [_lane_pick(c_ref[...], 2 * pl.program_id(0) + a) for a in range(2)]

        def tile(off, carry, diagonal):
            kblk = k_ref[pl.ds(off, FQ), :]
            vblk = v_ref[pl.ds(off, FQ), :]
            new = []
            for a in range(2):
                m, l, acc = carry[a]
                s = _nt(qs[a], kblk) + (cqs[a] - ct_ref[a:a + 1, pl.ds(off, FQ)])
                if diagonal:
                    s = jnp.where(causal, s, NEG)
                m2 = jnp.maximum(m, jnp.max(s, axis=-1, keepdims=True))
                p = jnp.exp(s - m2)
                alpha = jnp.exp(m - m2)
                new.append((m2, alpha * l + jnp.sum(p, axis=-1, keepdims=True),
                            alpha * acc + _nn(p.astype(BF16), vblk)))
            return tuple(new)

        init = (jnp.full((FQ, 1), NEG, F32), jnp.zeros((FQ, 1), F32), jnp.zeros((FQ, 128), F32))
        carry = lax.fori_loop(0, i, lambda kb, c: tile(pl.multiple_of(kb * FQ, FQ), c, False), (init, init))
        carry = tile(pl.multiple_of(i * FQ, FQ), carry, True)
        outs = []
        for a in range(2):
            m, l, acc = carry[a]
            outs.append(acc / l)
            l_ref[:, 128 * a:128 * a + 128] = jnp.broadcast_to(m + jnp.log(l), (FQ, 128))
        o_ref[...] = jnp.where(lo, outs[0], outs[1])

    return _pcall(
        body, name="fox_fwd", grid=(4, T // FQ),
        in_specs=_fox_specs() + [pl.BlockSpec((FQ, 128), lambda h, i: (i, 0)),
                                 pl.BlockSpec((None, 2, T), lambda h, i: (h, 0, 0))],
        out_specs=[pl.BlockSpec((FQ, 128), lambda h, i: (i, h)), pl.BlockSpec((FQ, 256), lambda h, i: (i, h))],
        out_shape=[jax.ShapeDtypeStruct((T, 512), F32), jax.ShapeDtypeStruct((T, 1024), F32)],
        compiler_params=pltpu.CompilerParams(dimension_semantics=("arbitrary", "arbitrary"), vmem_limit_bytes=VMEM_BIG),
    )(proj, proj, proj, c, ct3)


def _fox_bwd(proj, c, ct3, o, lse, do, after=()):
    def body(q_ref, k_ref, v_ref, c_ref, ct_ref, o_ref, l_ref, do_ref, dq_ref, dkb_ref, dvb_ref, dct_ref, dcq_ref,
             dk_ref, dv_ref):
        i = pl.program_id(1)

        @pl.when(i == 0)
        def _():
            dk_ref[...] = jnp.zeros_like(dk_ref)
            dv_ref[...] = jnp.zeros_like(dv_ref)
            dct_ref[...] = jnp.zeros_like(dct_ref)

        lo = _lane_lo(FQ)
        causal = lax.broadcasted_iota(jnp.int32, (FQ, FQ), 1) <= lax.broadcasted_iota(jnp.int32, (FQ, FQ), 0)
        q = q_ref[...]
        do_v = do_ref[...]
        prod = do_v * o_ref[...]
        qs = [_half(q, lo, a, SCALE) for a in range(2)]
        dos = [_half(do_v, lo, a) for a in range(2)]
        deltas = [jnp.sum(jnp.where(lo if a == 0 else jnp.logical_not(lo), prod, 0.0), axis=-1, keepdims=True)
                  for a in range(2)]
        cqs = [_lane_pick(c_ref[...], 2 * pl.program_id(0) + a) for a in range(2)]
        las = [l_ref[:, 128 * a:128 * a + 1] for a in range(2)]

        def tile(off, carry, diagonal):
            kblk = k_ref[pl.ds(off, FQ), :]
            vblk = v_ref[pl.ds(off, FQ), :]
            new = []
            dk = jnp.zeros((128, FQ), F32)
            dv = jnp.zeros((128, FQ), F32)
            for a in range(2):
                dq_acc, rs = carry[a]
                s = _nt(qs[a], kblk) + (cqs[a] - ct_ref[a:a + 1, pl.ds(off, FQ)])
                if diagonal:
                    s = jnp.where(causal, s, NEG)
                p = jnp.exp(s - las[a])
                ds = p * (_nt(dos[a], vblk) - deltas[a])
                dsb = ds.astype(BF16)
                dk = dk + _tn(qs[a], dsb)
                dv = dv + _tn(dos[a], p.astype(BF16))
                dct_ref[a:a + 1, pl.ds(off, FQ)] -= jnp.sum(ds, axis=0, keepdims=True)
                new.append((dq_acc + _nn(dsb, kblk), rs + jnp.sum(ds, axis=-1, keepdims=True)))
            dk_ref[:, pl.ds(off, FQ)] += dk
            dv_ref[:, pl.ds(off, FQ)] += dv
            return tuple(new)

        init = (jnp.zeros((FQ, 128), F32), jnp.zeros((FQ, 1), F32))
        carry = lax.fori_loop(0, i, lambda kb, c: tile(pl.multiple_of(kb * FQ, FQ), c, False), (init, init))
        carry = tile(pl.multiple_of(i * FQ, FQ), carry, True)
        lane = lax.broadcasted_iota(jnp.int32, (FQ, 128), 1)
        dcq_ref[...] = jnp.where(lane == 0, carry[0][1], jnp.where(lane == 1, carry[1][1], 0.0))
        dq_ref[...] = (jnp.where(lo, carry[0][0], carry[1][0]) * SCALE).astype(BF16)

        @pl.when(i == T // FQ - 1)
        def _():
            dkb_ref[...] = dk_ref[...].T.astype(BF16)
            dvb_ref[...] = dv_ref[...].T.astype(BF16)

    blk = pl.BlockSpec((FQ, 128), lambda h, i: (i, h))
    wide = pl.BlockSpec((FQ, 256), lambda h, i: (i, h))
    rows = pl.BlockSpec((None, 2, T), lambda h, i: (h, 0, 0))
    col = pl.BlockSpec((T, 128), lambda h, i: (0, h))
    return _pcall(
        _behind(body, 8, after), name="fox_bwd", grid=(4, T // FQ),
        in_specs=_fox_specs() + [pl.BlockSpec((FQ, 128), lambda h, i: (i, 0)), rows, blk, wide, blk] + [ANY_SPEC] * len(after),
        out_specs=[blk, col, col, rows, pl.BlockSpec((None, FQ, 128), lambda h, i: (h, i, 0))],
        out_shape=[jax.ShapeDtypeStruct((T, 512), BF16), jax.ShapeDtypeStruct((T, 512), BF16),
                   jax.ShapeDtypeStruct((T, 512), BF16), jax.ShapeDtypeStruct((4, 2, T), F32),
                   jax.ShapeDtypeStruct((4, T, 128), F32)],
        scratch_shapes=[pltpu.VMEM((128, T), F32), pltpu.VMEM((128, T), F32)],
        compiler_params=pltpu.CompilerParams(dimension_semantics=("arbitrary", "arbitrary"), vmem_limit_bytes=VMEM_BIG),
    )(proj, proj, proj, c, ct3, o, lse, do, *after)


def _rel_onehot():
    ridx = lax.broadcasted_iota(jnp.int32, (NREL_PAD, VW), 0)
    j = lax.broadcasted_iota(jnp.int32, (NREL_PAD, VW), 1)
    return jnp.where(ridx == jnp.clip(TQ + LEFT - 1 - j, -128, 128) + 128, 1.0, 0.0).astype(F32)


def _relvec_fwd(tbl):
    def body(t_ref, v_ref):
        v_ref[...] = _hdot(t_ref[...], _rel_onehot())

    return _one_call(body, "relvec_fwd", [tbl], [((8, VW), F32)])[0]


def _relvec_bwd(gv):
    def body(g_ref, t_ref):
        t_ref[...] = lax.dot_general(g_ref[...], _rel_onehot(), (((1,), (1,)), ((), ())),
                                     preferred_element_type=F32, precision=lax.Precision.HIGHEST)

    return _one_call(body, "relvec_bwd", [gv], [((8, NREL_PAD), F32)])[0]


def _chk_bias(vt_ref, a, hidden):
    vb = jnp.broadcast_to(vt_ref[a:a + 1, :], (TQ, VW))
    y = pltpu.roll(vb, VW - (TQ - 1), 1, stride=1, stride_axis=0)[:, :WIN]
    cr = lax.broadcasted_iota(jnp.int32, (TQ, WIN), 0) // 64
    m = lax.broadcasted_iota(jnp.int32, (TQ, WIN), 1)
    return jnp.where((m // 64 >= cr) & (m // 64 <= cr + 8) & (m >= hidden), y, NEG)


def _chk_specs():
    return [pl.BlockSpec((TQ, 128), lambda h, i: (i, CHK0 // 128 + h)),
            pl.BlockSpec((T + LEFT, 128), lambda h, i: (0, h)),
            pl.BlockSpec((T + LEFT, 128), lambda h, i: (0, 4 + h)),
            pl.BlockSpec((None, 2, VW), lambda h, i: (h, 0, 0))]


def _chk_fwd(proj, kvp, vt3, after=()):
    def body(q_ref, k_ref, v_ref, vt_ref, o_ref, l_ref, bias_ref):
        i = pl.program_id(1)

        @pl.when(i == 0)
        def _():
            for first in range(3):
                for a in range(2):
                    bias_ref[first, a] = _chk_bias(vt_ref, a, max(LEFT - first * TQ, 0))

        lo = _lane_lo()
        off = pl.multiple_of(i * TQ, TQ)
        kw = k_ref[pl.ds(off, WIN), :]
        vw = v_ref[pl.ds(off, WIN), :]
        bias_at = jnp.minimum(i, 2)
        q = q_ref[...]
        outs = []
        for a in range(2):
            s = _nt(_half(q, lo, a, SCALE), kw) + bias_ref[bias_at, a]
            m = jnp.max(s, axis=-1, keepdims=True)
            p = jnp.exp(s - m)
            l = jnp.sum(p, axis=-1, keepdims=True)
            outs.append(_nn(p.astype(BF16), vw) / l)
            l_ref[:, 128 * a:128 * a + 128] = jnp.broadcast_to(m + jnp.log(l), (TQ, 128))
        o_ref[...] = jnp.where(lo, outs[0], outs[1])

    return _pcall(
        _behind(body, 4, after), name="chk_fwd", grid=(4, T // TQ), in_specs=_chk_specs() + [ANY_SPEC] * len(after),
        out_specs=[pl.BlockSpec((TQ, 128), lambda h, i: (i, h)), pl.BlockSpec((TQ, 256), lambda h, i: (i, h))],
        out_shape=[jax.ShapeDtypeStruct((T, 512), F32), jax.ShapeDtypeStruct((T, 1024), F32)],
        scratch_shapes=[pltpu.VMEM((3, 2, TQ, WIN), F32)],
        compiler_params=pltpu.CompilerParams(dimension_semantics=("arbitrary", "arbitrary")),
    )(proj, kvp, kvp, vt3, *after)


def _chk_bwd(proj, kvp, vt3, o, lse, do, after=()):
    nq = T // TQ

    def body(q_ref, k_ref, v_ref, vt_ref, o_ref, l_ref, do_ref, dq_ref, dkb_ref, dvb_ref, gv_ref, bias_ref, dsum_ref,
             dk_ref, dv_ref):
        i = pl.program_id(1)

        @pl.when(i == 0)
        def _():
            for first in range(3):
                for a in range(2):
                    bias_ref[first, a] = _chk_bias(vt_ref, a, max(LEFT - first * TQ, 0))
            dsum_ref[...] = jnp.zeros_like(dsum_ref)
            dk_ref[...] = jnp.zeros_like(dk_ref)
            dv_ref[...] = jnp.zeros_like(dv_ref)

        lo = _lane_lo()
        off = pl.multiple_of(i * TQ, TQ)
        kw = k_ref[pl.ds(off, WIN), :]
        vw = v_ref[pl.ds(off, WIN), :]
        bias_at = jnp.minimum(i, 2)
        q = q_ref[...]
        do_v = do_ref[...]
        prod = do_v * o_ref[...]
        dqs = []
        for a in range(2):
            keep = lo if a == 0 else jnp.logical_not(lo)
            qa = _half(q, lo, a, SCALE)
            doa = _half(do_v, lo, a)
            delta = jnp.sum(jnp.where(keep, prod, 0.0), axis=-1, keepdims=True)
            s = _nt(qa, kw) + bias_ref[bias_at, a]
            p = jnp.exp(s - l_ref[:, 128 * a:128 * a + 1])
            ds = p * (_nt(doa, vw) - delta)
            dsum_ref[a] += ds
            dsb = ds.astype(BF16)
            dk_ref[:, pl.ds(off, WIN)] += _tn(qa, dsb)
            dv_ref[:, pl.ds(off, WIN)] += _tn(doa, p.astype(BF16))
            dqs.append(_nn(dsb, kw))
        dq_ref[...] = (jnp.where(lo, dqs[0], dqs[1]) * SCALE).astype(BF16)

        @pl.when(i == nq - 1)
        def _():
            dkb_ref[...] = dk_ref[:, LEFT:].T.astype(BF16)
            dvb_ref[...] = dv_ref[:, LEFT:].T.astype(BF16)
            rr = lax.broadcasted_iota(jnp.int32, (TQ, TQ), 0)
            cc = lax.broadcasted_iota(jnp.int32, (TQ, TQ), 1)
            flip = jnp.where(rr + cc == TQ - 1, 1.0, 0.0).astype(F32)
            for a in range(2):
                dpad = jnp.concatenate([dsum_ref[a], jnp.zeros((TQ, VW - WIN), F32)], axis=1)
                z = pltpu.roll(_hdot(flip, dpad), 0, 1, stride=1, stride_axis=0)
                gv_ref[a:a + 1, :] = jnp.sum(z, axis=0, keepdims=True)

    blk = pl.BlockSpec((TQ, 128), lambda h, i: (i, h))
    wide = pl.BlockSpec((TQ, 256), lambda h, i: (i, h))
    col = pl.BlockSpec((T, 128), lambda h, i: (0, h))
    return _pcall(
        _behind(body, 7, after), name="chk_bwd", grid=(4, nq), in_specs=_chk_specs() + [blk, wide, blk] + [ANY_SPEC] * len(after),
        out_specs=[blk, col, col, pl.BlockSpec((None, 2, VW), lambda h, i: (h, 0, 0))],
        out_shape=[jax.ShapeDtypeStruct((T, 512), BF16), jax.ShapeDtypeStruct((T, 512), BF16),
                   jax.ShapeDtypeStruct((T, 512), BF16), jax.ShapeDtypeStruct((4, 2, VW), F32)],
        scratch_shapes=[pltpu.VMEM((3, 2, TQ, WIN), F32), pltpu.VMEM((2, TQ, WIN), F32),
                        pltpu.VMEM((128, T + LEFT), F32), pltpu.VMEM((128, T + LEFT), F32)],
        compiler_params=pltpu.CompilerParams(dimension_semantics=("arbitrary", "arbitrary")),
    )(proj, kvp, kvp, vt3, o, lse, do, *after)


def _zero_at_start(*refs):
    @pl.when(pl.program_id(0) == 0)
    def _():
        for r in refs:
            r[...] = jnp.zeros_like(r)


def _ffn_step(h3, x2, tgt, w1_t, w2, g_post, g_pre):
    def body(h_ref, x2_ref, t_ref, w1_ref, w2_ref, gp_ref, gf_ref, dx2_ref, da_ref, dy_ref, r_ref, loss_ref, dgp_ref, dgf_ref):
        _zero_at_start(loss_ref, dgp_ref, dgf_ref)
        w1, w2v = _w(w1_ref), _w(w2_ref)
        ra = jnp.maximum(_nt(h_ref[...], w1), 0.0)
        r = jnp.square(ra).astype(BF16)
        r_ref[...] = r
        y = _nn(r, w2v)
        x2v = x2_ref[...]
        e = x2v + _rms(y, gp_ref[...]) - t_ref[...]
        loss_ref[...] += 0.5 * jnp.sum(jnp.sum(e * e, axis=-1, keepdims=True) * (1.0 / D))
        dx3 = e * (1.0 / D)
        dy, dgp = _rms_bwd(y, gp_ref[...], dx3)
        dgp_ref[...] += dgp
        dyb = dy.astype(BF16)
        dy_ref[...] = dyb
        da = (_nt(dyb, w2v) * (2.0 * ra)).astype(BF16)
        da_ref[...] = da
        dh, dgf = _rms_bwd(x2v, gf_ref[...], _nn(da, w1))
        dgf_ref[...] += dgf
        dx2_ref[...] = dx3 + dh

    return _tok_call(body, "ffn_step", [h3, x2, tgt], [w1_t, w2, g_post, g_pre],
                     [(D, F32), (DFF, BF16), (D, BF16), (DFF, BF16)], [(8, 128), (1, D), (1, D)], vmem=VMEM_BIG)


def _mem_bwd(dx2, ym, x1, qm, km, vm, w_mo, w_mq, g_post, g_pre, after=()):
    def body(dx2_ref, ym_ref, x1_ref, q_ref, k_ref, v_ref, wo_ref, wq_ref, gp_ref, gm_ref,
             dx1_ref, dym_ref, dq_ref, dk_ref, dv_ref, dgp_ref, dgm_ref, dom_ref):
        _zero_at_start(dk_ref, dv_ref, dgp_ref, dgm_ref)
        dx2_v = dx2_ref[...]
        dym, dgp = _rms_bwd(ym_ref[...], gp_ref[...], dx2_v)
        dgp_ref[...] += dgp
        dymb = dym.astype(BF16)
        dym_ref[...] = dymb
        dom_ref[...] = _nt(dymb, _w(wo_ref)).astype(BF16)
        for h in range(MEM_HEADS):
            sl = slice(h * MEM_HD, (h + 1) * MEM_HD)
            qh, kh, doh = q_ref[:, sl], k_ref[:, sl], dom_ref[:, sl]
            s = _nt(qh, kh) * MEM_SCALE
            p = jnp.exp(s - jnp.max(s, axis=-1, keepdims=True))
            p = p / jnp.sum(p, axis=-1, keepdims=True)
            dp = _nt(doh, v_ref[:, sl])
            ds = (p * (dp - jnp.sum(p * dp, axis=-1, keepdims=True))).astype(BF16)
            dq_ref[:, sl] = (_nn(ds, kh) * MEM_SCALE).astype(BF16)
            dk_ref[:, sl] += _tn(ds, qh) * MEM_SCALE
            dv_ref[:, sl] += _tn(p.astype(BF16), doh)
        dh, dgm = _rms_bwd(x1_ref[...], gm_ref[...], _nt(dq_ref[...], _w(wq_ref)))
        dgm_ref[...] += dgm
        dx1_ref[...] = dx2_v + dh

    tiled = pl.BlockSpec((TM_WIDE, D), lambda i: (i, 0))
    in_specs = [tiled] * 4 + [_resident(a) for a in (km, vm, w_mo, w_mq, g_post, g_pre)] + [ANY_SPEC] * len(after)
    w_mo, w_mq = w_mo[0], w_mq[0]
    kv = pl.BlockSpec((NMEM, D), lambda i: (0, 0))
    vec = pl.BlockSpec((1, D), lambda i: (0, 0))
    return _pcall(
        _behind(body, 10, after), name="mem_bwd", grid=(T // TM_WIDE,), in_specs=in_specs,
        out_specs=[tiled, tiled, tiled, kv, kv, vec, vec],
        out_shape=[jax.ShapeDtypeStruct((T, D), F32), jax.ShapeDtypeStruct((T, D), BF16),
                   jax.ShapeDtypeStruct((T, D), BF16), jax.ShapeDtypeStruct((NMEM, D), F32),
                   jax.ShapeDtypeStruct((NMEM, D), F32), jax.ShapeDtypeStruct((1, D), F32),
                   jax.ShapeDtypeStruct((1, D), F32)],
        scratch_shapes=[pltpu.VMEM((TM_WIDE, D), BF16)],
        compiler_params=pltpu.CompilerParams(dimension_semantics=("arbitrary",), vmem_limit_bytes=VMEM_BIG),
    )(dx2, ym, x1, qm, km, vm, w_mo, w_mq, g_post, g_pre, *after)


def _memkv_bwd(dkm, dvm, mem, w_mk, w_mv):
    def body(dk_ref, dv_ref, m_ref, wk_ref, wv_ref, dg_ref):
        dmn = _nt(dk_ref[...].astype(BF16), _w(wk_ref)) + _nt(dv_ref[...].astype(BF16), _w(wv_ref))
        mv = m_ref[...]
        dg_ref[...] = jnp.sum(dmn * (mv * _rstd(mv)), axis=0, keepdims=True)

    return _one_call(body, "memkv_bwd", [dkm, dvm, mem, w_mk, w_mv], [((1, D), F32)], vmem=VMEM_BIG)[0]


def _postmix_bwd(dx1, z, o_f, o_c, w_out, g_post, g_fo, g_co, after=()):
    def body(dx1_ref, z_ref, of_ref, oc_ref, wo_ref, gp_ref, gfo_ref, gco_ref,
             dz_ref, dof_ref, doc_ref, dgp_ref, dgfo_ref, dgco_ref):
        _zero_at_start(dgp_ref, dgfo_ref, dgco_ref)
        dz, dgp = _rms_bwd(z_ref[...], gp_ref[...], dx1_ref[...])
        dgp_ref[...] += dgp
        dzb = dz.astype(BF16)
        dz_ref[...] = dzb
        dy = _nt(dzb, _w(wo_ref))
        dof, dgfo = _rms_bwd(of_ref[...], gfo_ref[...], dy[:, :512])
        doc, dgco = _rms_bwd(oc_ref[...], gco_ref[...], dy[:, 512:])
        dof_ref[...] = dof
        doc_ref[...] = doc
        dgfo_ref[...] += dgfo
        dgco_ref[...] += dgco

    return _tok_call(body, "postmix_bwd", [dx1, z, o_f, o_c], [w_out, g_post, g_fo, g_co],
                     [(D, BF16), (512, F32), (512, F32)], [(1, D), (1, 512), (1, 512)], tm=TM_WIDE, vmem=VMEM_BIG,
                     after=after)


def _premix_bwd(dx1, x, pieces, win_t, g_pre, after=()):
    def body(dx1_ref, x_ref, *refs):
        piece_refs, (w_ref, g_ref, dx_ref, dg_ref, dp_ref) = refs[:len(pieces)], refs[len(pieces):]
        _zero_at_start(dg_ref)
        col = 0
        for p in piece_refs:
            dp_ref[:, col:col + p.shape[1]] = p[...]
            col += p.shape[1]
        dh, dg = _rms_bwd(x_ref[...], g_ref[...], _nn(dp_ref[...], w_ref[...]))
        dg_ref[...] += dg
        dx_ref[...] = dx1_ref[...] + dh

    return _tok_call(body, "premix_bwd", [dx1, x] + list(pieces), [win_t, g_pre], [(D, F32)], [(1, D)],
                     tm=TM_WIDE, vmem=VMEM_BIG, after=after, scratch=[pltpu.VMEM((TM_WIDE, PROJ), BF16)])


def _wgrad_group(name, pairs, rows):
    def body(*refs):
        o_ref = refs[-1]
        for k in range(len(pairs)):
            g = _tn(refs[2 * k][...].astype(BF16), refs[2 * k + 1][...].astype(BF16))
            o_ref[k * rows:(k + 1) * rows, :] = g.astype(BF16)

    in_specs, ops = [], []
    for a, b in pairs:
        in_specs += [pl.BlockSpec((a.shape[0], rows), lambda j: (0, j)), _resident(b)]
        ops += [a, b]
    return _pcall(
        body, name=name, grid=(8,), in_specs=in_specs,
        out_specs=pl.BlockSpec((None, len(pairs) * rows, D), lambda j: (j, 0, 0)),
        out_shape=jax.ShapeDtypeStruct((8, len(pairs) * rows, D), BF16),
        compiler_params=pltpu.CompilerParams(dimension_semantics=("arbitrary",), vmem_limit_bytes=VMEM_BIG),
    )(*ops)


def _wgrad_whole(name, pairs):
    n = len(pairs)
    ops = [x for pair in pairs for x in pair]
    rows = pairs[0][0].shape[1] // 8

    def body(*refs):
        hbm, o_ref, bufs, sem = refs[:2 * n], refs[2 * n], refs[2 * n + 1:4 * n + 1], refs[4 * n + 1]
        k = pl.program_id(0)
        copies = [pltpu.make_async_copy(hbm[t], bufs[t], sem.at[t]) for t in range(2 * n)]

        @pl.when(k == 0)
        def _():
            for copy in copies:
                copy.start()

        for t in range(n):
            @pl.when(k == t)
            def _(t=t):
                copies[2 * t].wait()
                copies[2 * t + 1].wait()
                g = _tn(bufs[2 * t][...].astype(BF16), bufs[2 * t + 1][...].astype(BF16))
                o_ref[...] = g.reshape(8, rows, D).astype(BF16)

    return _pcall(
        body, name=name, grid=(n,), in_specs=[ANY_SPEC] * (2 * n),
        out_specs=pl.BlockSpec((8, rows, D), lambda k: (0, k, 0)),
        out_shape=jax.ShapeDtypeStruct((8, n * rows, D), BF16),
        scratch_shapes=[pltpu.VMEM(x.shape, x.dtype) for x in ops] + [pltpu.SemaphoreType.DMA((2 * n,))],
        compiler_params=pltpu.CompilerParams(dimension_semantics=("arbitrary",), vmem_limit_bytes=VMEM_BIG),
    )(*ops)


def _adam_math(w, g, m, v):
    m2 = ADAM_B1 * m + (1.0 - ADAM_B1) * g
    v2 = ADAM_B2 * v + (1.0 - ADAM_B2) * jnp.square(g)
    m_hat = m2 / (1.0 - ADAM_B1 ** ADAM_STEP)
    v_hat = v2 / (1.0 - ADAM_B2 ** ADAM_STEP)
    delta = -ADAM_LR * (m_hat / (jnp.sqrt(v_hat) + ADAM_EPS) + ADAM_WD * w)
    return delta, m2, v2


def _adamw_small(gparts, ws, ms, vs):
    n = len(SMALL)

    def body(g_ref, *refs):
        w_refs, m_refs, v_refs = refs[:n], refs[n:2 * n], refs[2 * n:3 * n]
        outs, sum_ref = refs[3 * n:-1], refs[-1]
        g = g_ref[0]
        for k in range(1, 8):
            g = g + g_ref[k]
        sum_ref[...] = g
        outs[0][...] = sum_ref[17:18, 0:128]
        for t, name in enumerate(SMALL):
            r0, nr, c0, nc = SMALL_SLOT[name]
            gt = sum_ref[r0:r0 + nr, c0:c0 + nc]
            out = (gt,) + _adam_math(w_refs[t][...], gt, m_refs[t][...], v_refs[t][...])
            for o_ref, val in zip(outs[1 + 4 * t:5 + 4 * t], out):
                o_ref[...] = val

    whole = lambda s: pl.BlockSpec(s, lambda i, nd=len(s): (0,) * nd)
    ins = [gparts] + list(ws) + list(ms) + list(vs)
    out_shapes = [(1, 128)] + [a.shape for a in ws for _ in range(4)]
    return _pcall(
        body, name="adamw_small", grid=(1,), in_specs=[whole(a.shape) for a in ins],
        out_specs=[whole(s) for s in out_shapes], out_shape=[jax.ShapeDtypeStruct(s, F32) for s in out_shapes],
        scratch_shapes=[pltpu.VMEM((SMALL_ROWS, D), F32)],
        compiler_params=pltpu.CompilerParams(dimension_semantics=("arbitrary",)),
    )(*ins)


def _add_halves(g4, theirs, core, name):
    rows = tr = g4.shape[2]

    def body(c_ref, a_ref, b_ref, o_ref):
        o_ref[...] = (a_ref[...].astype(F32) + b_ref[...].astype(F32)).astype(BF16)

    grid_spec = pltpu.PrefetchScalarGridSpec(
        num_scalar_prefetch=1, grid=(4, rows // tr),
        in_specs=[pl.BlockSpec((None, None, tr, D), lambda j, i, c: (j, c[0], i, 0)),
                  pl.BlockSpec((None, None, tr, D), lambda j, i, c: (j, 0, i, 0))],
        out_specs=pl.BlockSpec((None, tr, D), lambda j, i, c: (j, i, 0)))
    return _pcall(
        body, name=name, grid_spec=grid_spec, out_shape=jax.ShapeDtypeStruct((4, rows, D), BF16),
        compiler_params=pltpu.CompilerParams(dimension_semantics=("arbitrary", "arbitrary"), vmem_limit_bytes=VMEM_BIG),
    )(core, g4, theirs)


def _sum_adam(own, got, order, r0, w, m, v, name, transposed=False):
    ragged = w.ndim == 3
    n = w.shape[1] if transposed else w.shape[0]
    tr = n if ragged else min(n, 256)
    rows = own.shape[1] if ragged else tr
    at = (slice(None), 0, slice(None)) if ragged else Ellipsis

    def body(o_ref, a_ref, b_ref, c_ref, d_ref, w_ref, m_ref, v_ref, *out_refs):
        f = lambda r: r[0:tr, :].astype(F32)
        g = ((f(a_ref) + f(b_ref)) + f(c_ref)) + f(d_ref)
        g = g.T if transposed else g
        for ref, val in zip(out_refs, (g,) + _adam_math(w_ref[at], g, m_ref[at], v_ref[at])):
            ref[at] = val

    slot = lambda k: pl.BlockSpec((None, rows, D), lambda i, o: (o[k], r0 // rows + i, 0))
    if ragged:
        wspec = pl.BlockSpec((n, 1, D), lambda i, o: (0, 0, 0))
    else:
        wspec = pl.BlockSpec((D, tr), lambda i, o: (0, i)) if transposed else pl.BlockSpec((tr, D), lambda i, o: (i, 0))
    grid_spec = pltpu.PrefetchScalarGridSpec(
        num_scalar_prefetch=1, grid=(n // tr,), in_specs=[slot(0), slot(1), slot(2), slot(3), wspec, wspec, wspec],
        out_specs=[wspec] * 4)
    return _pcall(
        body, name=name, grid_spec=grid_spec, out_shape=[jax.ShapeDtypeStruct(w.shape, F32)] * 4,
        compiler_params=pltpu.CompilerParams(dimension_semantics=("arbitrary",)),
    )(order, own, got, got, got, w, m, v)


def _sum_adam_rows(own, got, order, ws, ms, vs, name):
    n, rows = len(ws), ws[0].shape[0]

    def body(o_ref, a_ref, b_ref, c_ref, d_ref, *refs):
        ins, outs = refs[:3 * n], refs[3 * n:]
        for t in range(n):
            r = slice(t * rows, (t + 1) * rows)
            f = lambda ref: ref[r, :].astype(F32)
            g = ((f(a_ref) + f(b_ref)) + f(c_ref)) + f(d_ref)
            out = (g,) + _adam_math(ins[t][...], g, ins[n + t][...], ins[2 * n + t][...])
            for o, val in zip(outs[4 * t:4 * t + 4], out):
                o[...] = val

    slot = lambda k: pl.BlockSpec((None, n * rows, D), lambda i, o: (o[k], 0, 0), pipeline_mode=pl.Buffered(1))
    wspec = pl.BlockSpec((rows, D), lambda i, o: (0, 0), pipeline_mode=pl.Buffered(1))
    grid_spec = pltpu.PrefetchScalarGridSpec(
        num_scalar_prefetch=1, grid=(1,), in_specs=[slot(0), slot(1), slot(2), slot(3)] + [wspec] * (3 * n),
        out_specs=[pl.BlockSpec((rows, D), lambda i, o: (0, 0))] * (4 * n))
    return _pcall(
        body, name=name, grid_spec=grid_spec, out_shape=[jax.ShapeDtypeStruct((rows, D), F32)] * (4 * n),
        compiler_params=pltpu.CompilerParams(dimension_semantics=("arbitrary",), vmem_limit_bytes=VMEM_BIG),
    )(order, own, got, got, got, *ws, *ms, *vs)


def _place():
    return lax.axis_index("x"), lax.axis_index("y"), lax.axis_index("c")


def _allgather(block, name, after=()):
    rows = block.shape[0]
    split = (rows // 2 + 15) // 16 * 16

    def body(x_ref, out_ref, token, send_sems, recv_sems, local_sem):
        token[...] = jnp.zeros_like(token)
        x, y, c = _place()
        me, sib = (x, y, c), (x, y, 1 - c)
        xn, yn, dg = (1 - x, y), (x, 1 - y), (1 - x, 1 - y)
        lo, hi = pl.ds(0, split), pl.ds(split, rows - split)

        def copy(k, blk, to, part=None, src=None):
            index = 4 * blk[0] + 2 * blk[1] + blk[2]
            view = out_ref.at[index] if part is None else out_ref.at[index, part]
            return pltpu.make_async_remote_copy(
                src_ref=view if src is None else src, dst_ref=view,
                send_sem=send_sems.at[k], recv_sem=recv_sems.at[k], device_id=to, device_id_type=MESH)

        def start(*copies):
            for cp in copies:
                cp.start()
            return list(copies)

        mine = pltpu.make_async_copy(x_ref, out_ref.at[4 * x + 2 * y + c], local_sem)
        mine.start()
        sent = start(copy(0, me, sib, src=x_ref), copy(1, me, (*xn, c), src=x_ref), copy(2, me, (*yn, c), src=x_ref))
        copy(1, (*xn, c), me).wait_recv()
        sent += start(copy(3, (*xn, c), sib), copy(5, (*xn, c), (*yn, c), part=lo))
        copy(2, (*yn, c), me).wait_recv()
        sent += start(copy(4, (*yn, c), sib), copy(6, (*yn, c), (*xn, c), part=hi))
        copy(5, (*dg, c), me, part=lo).wait_recv()
        copy(6, (*dg, c), me, part=hi).wait_recv()
        sent += start(copy(7, (*dg, c), sib))
        for k, blk in ((0, sib), (3, (*xn, 1 - c)), (4, (*yn, 1 - c)), (7, (*dg, 1 - c))):
            copy(k, blk, me).wait_recv()
        for cp in sent:
            cp.wait_send()
        mine.wait()

    return _pcall(
        _behind(body, 1, after), name=name,
        out_shape=[jax.ShapeDtypeStruct((8,) + block.shape, block.dtype), jax.ShapeDtypeStruct((8, 128), F32)],
        in_specs=[pl.BlockSpec(memory_space=pl.ANY)] * (1 + len(after)),
        out_specs=[pl.BlockSpec(memory_space=pl.ANY), pl.BlockSpec(memory_space=pltpu.VMEM)],
        scratch_shapes=[pltpu.SemaphoreType.DMA((8,)), pltpu.SemaphoreType.DMA((8,)), pltpu.SemaphoreType.DMA(())],
        compiler_params=pltpu.CompilerParams(has_side_effects=True),
    )(block, *after)


HBM_SPEC = pl.BlockSpec(memory_space=pltpu.HBM)
SEM_SPEC = pl.BlockSpec(memory_space=pltpu.SEMAPHORE)
ANY_SPEC = pl.BlockSpec(memory_space=pl.ANY)
EFFECT = pltpu.SideEffectType.DATAFLOW_SIDE_EFFECTING


def _in_hbm(a):
    return pltpu.with_memory_space_constraint(a, pltpu.HBM)


def _start_copies(name, src, land_shape, plan, n):
    def body(src_ref, land_ref, send_sems, recv_sems, src_thru, land_thru, token):
        for k, (s, d, to, _) in enumerate(plan(src_ref, land_ref)):
            pltpu.make_async_remote_copy(src_ref=s, dst_ref=d, send_sem=send_sems.at[k], recv_sem=recv_sems.at[k],
                                         device_id=to, device_id_type=MESH).start()
        token[...] = jnp.zeros_like(token)

    return _pcall(
        body, name=name,
        out_shape=(pltpu.SemaphoreType.DMA((n,)), pltpu.SemaphoreType.DMA((n,)), pltpu.HBM(src.shape, src.dtype),
                   pltpu.HBM(land_shape, src.dtype), jax.ShapeDtypeStruct((8, 128), F32)),
        in_specs=(HBM_SPEC, HBM_SPEC),
        out_specs=(SEM_SPEC, SEM_SPEC, HBM_SPEC, HBM_SPEC, pl.BlockSpec(memory_space=pltpu.VMEM)),
        input_output_aliases={0: 2, 1: 3}, compiler_params=pltpu.CompilerParams(has_side_effects=EFFECT),
    )(_in_hbm(src), _in_hbm(lax.empty(land_shape, src.dtype)))


def _wait_copies(name, started, after, plan):
    send_sems, recv_sems, src_thru, land_thru, _ = started

    def body(src_ref, land_ref, send_sems, recv_sems, *rest):
        for k, (s, _, to, mine) in enumerate(plan(src_ref, land_ref)):
            cp = pltpu.make_async_remote_copy(src_ref=s, dst_ref=mine, send_sem=send_sems.at[k],
                                              recv_sem=recv_sems.at[k], device_id=to, device_id_type=MESH)
            cp.wait_send()
            cp.wait_recv()

    return _pcall(
        body, name=name,
        out_shape=(pltpu.HBM(src_thru.shape, src_thru.dtype), pltpu.HBM(land_thru.shape, land_thru.dtype)),
        in_specs=(HBM_SPEC, HBM_SPEC, SEM_SPEC, SEM_SPEC) + (ANY_SPEC,) * len(after), out_specs=(HBM_SPEC, HBM_SPEC),
        input_output_aliases={0: 0, 1: 1}, compiler_params=pltpu.CompilerParams(has_side_effects=EFFECT),
    )(src_thru, land_thru, send_sems, recv_sems, *after)


def _start_inplace(name, buf, plan, n):
    def body(buf_ref, send_sems, recv_sems, buf_thru, token):
        for k, (s, d, to, _) in enumerate(plan(buf_ref, buf_ref)):
            pltpu.make_async_remote_copy(src_ref=s, dst_ref=d, send_sem=send_sems.at[k], recv_sem=recv_sems.at[k],
                                         device_id=to, device_id_type=MESH).start()
        token[...] = jnp.zeros_like(token)

    return _pcall(
        body, name=name,
        out_shape=(pltpu.SemaphoreType.DMA((n,)), pltpu.SemaphoreType.DMA((n,)), pltpu.HBM(buf.shape, buf.dtype),
                   jax.ShapeDtypeStruct((8, 128), F32)),
        in_specs=(HBM_SPEC,), out_specs=(SEM_SPEC, SEM_SPEC, HBM_SPEC, pl.BlockSpec(memory_space=pltpu.VMEM)),
        input_output_aliases={0: 2}, compiler_params=pltpu.CompilerParams(has_side_effects=EFFECT),
    )(_in_hbm(buf))


def _wait_inplace(name, started, after, plan):
    send_sems, recv_sems, buf_thru, _ = started

    def body(buf_ref, send_sems, recv_sems, *rest):
        for k, (s, _, to, mine) in enumerate(plan(buf_ref, buf_ref)):
            cp = pltpu.make_async_remote_copy(src_ref=s, dst_ref=mine, send_sem=send_sems.at[k],
                                              recv_sem=recv_sems.at[k], device_id=to, device_id_type=MESH)
            cp.wait_send()
            cp.wait_recv()

    return _pcall(
        body, name=name, out_shape=pltpu.HBM(buf_thru.shape, buf_thru.dtype),
        in_specs=(HBM_SPEC, SEM_SPEC, SEM_SPEC) + (ANY_SPEC,) * len(after), out_specs=HBM_SPEC,
        input_output_aliases={0: 0}, compiler_params=pltpu.CompilerParams(has_side_effects=EFFECT),
    )(buf_thru, send_sems, recv_sems, *after)


def _gather_plan(src_ref, land_ref):
    x, y, c = _place()
    peers = [(x, y, 1 - c), (1 - x, y, c), (x, 1 - y, c)]
    return [(src_ref, land_ref.at[4 * x + 2 * y + c], p, land_ref.at[4 * p[0] + 2 * p[1] + p[2]]) for p in peers]


def _relay_plan(buf_ref, _):
    x, y, c = _place()
    slot = lambda p, pc: 4 * p[0] + 2 * p[1] + pc
    xn, yn, dg, sib = (1 - x, y), (x, 1 - y), (1 - x, 1 - y), (x, y, 1 - c)
    half = buf_ref.shape[1] // 2
    lo, hi = pl.ds(0, half), pl.ds(half, half)
    return [(buf_ref.at[slot(xn, c)], buf_ref.at[slot(xn, c)], sib, buf_ref.at[slot(xn, 1 - c)]),
            (buf_ref.at[slot(yn, c)], buf_ref.at[slot(yn, c)], sib, buf_ref.at[slot(yn, 1 - c)]),
            (buf_ref.at[slot(xn, c), lo], buf_ref.at[slot(xn, c), lo], (*yn, c), buf_ref.at[slot(dg, c), lo]),
            (buf_ref.at[slot(yn, c), hi], buf_ref.at[slot(yn, c), hi], (*xn, c), buf_ref.at[slot(dg, c), hi])]


def _swap_plan(src_ref, land_ref):
    x, y, c = _place()
    return [(src_ref.at[:, pl.ds(1 - c, 1)], land_ref, (x, y, 1 - c), land_ref)]


def _exchange_plan(src_ref, land_ref):
    x, y, c = _place()
    chips = [(1 - x, y), (x, 1 - y), (1 - x, 1 - y)]
    return [(src_ref.at[2 * px + py], land_ref.at[2 * x + y], (px, py, c), land_ref.at[2 * px + py]) for px, py in chips]


def _gather_forward(land, block):
    def body(land_ref, out_ref, send_sems, recv_sems):
        x, y, c = _place()
        chips = [(1 - x, 1 - y)]

        def copy(k, px, py, pc):
            blk = out_ref.at[4 * px + 2 * py + pc]
            return pltpu.make_async_remote_copy(src_ref=blk, dst_ref=blk, send_sem=send_sems.at[k],
                                                recv_sem=recv_sems.at[k], device_id=(x, y, 1 - c), device_id_type=MESH)

        sent = [copy(k, px, py, c) for k, (px, py) in enumerate(chips)]
        for cp in sent:
            cp.start()
        for k, (px, py) in enumerate(chips):
            copy(k, px, py, 1 - c).wait_recv()
        for cp in sent:
            cp.wait_send()

    land = _pcall(
        body, name="allgather_rest_forward", out_shape=jax.ShapeDtypeStruct(land.shape, land.dtype),
        in_specs=[ANY_SPEC], out_specs=ANY_SPEC, input_output_aliases={0: 0},
        scratch_shapes=[pltpu.SemaphoreType.DMA((1,)), pltpu.SemaphoreType.DMA((1,))],
        compiler_params=pltpu.CompilerParams(has_side_effects=True),
    )(land)

    rows = block.shape[0]
    tr = rows // 4

    def place(me_ref, x_ref, land_ref, out_ref):
        out_ref[...] = x_ref[...]

    x, y, c = _place()
    grid_spec = pltpu.PrefetchScalarGridSpec(
        num_scalar_prefetch=1, grid=(rows // tr,),
        in_specs=[pl.BlockSpec((tr, D), lambda i, me: (i, 0)), ANY_SPEC],
        out_specs=pl.BlockSpec((None, tr, D), lambda i, me: (me[0], i, 0)))
    return _pcall(
        place, name="allgather_rest_own", grid_spec=grid_spec, out_shape=jax.ShapeDtypeStruct(land.shape, land.dtype),
        input_output_aliases={2: 0}, compiler_params=pltpu.CompilerParams(dimension_semantics=("arbitrary",)),
    )((4 * x + 2 * y + c).reshape(1), block, land)


class _ReduceScatter:
    def __init__(self, name, g):
        self.name = name
        rows = g.shape[1]
        self.started = _start_copies(name + "_swap_start", g.reshape(4, 2, rows, D), (4, 1, rows, D), _swap_plan, 1)
        self.token = self.started[4]

    def halfway(self, after):
        g4, theirs = _wait_copies(self.name + "_swap_wait", self.started, after, _swap_plan)
        self.own = _add_halves(g4, theirs, lax.axis_index("c").reshape(1), self.name + "_add_halves")
        self.started = _start_copies(self.name + "_exch_start", self.own, self.own.shape, _exchange_plan, 3)
        self.token = self.started[4]

    def finish(self, after):
        own, got = _wait_copies(self.name + "_exch_wait", self.started, after, _exchange_plan)
        chip = 2 * lax.axis_index("x") + lax.axis_index("y")
        return own, got, (chip + jnp.arange(4, dtype=jnp.int32)) % 4


def _pack_small(p, loss):
    def body(*refs):
        o_ref = refs[-1]
        o_ref[...] = jnp.zeros_like(o_ref)
        for ref, name in zip(refs, SMALL):
            r0, nr, c0, nc = SMALL_SLOT[name]
            o_ref[r0:r0 + nr, c0:c0 + nc] = ref[...]
        o_ref[17:18, 0:128] = refs[len(SMALL)][0:1, :]

    return _one_call(body, "pack_small_grads", [p[n] for n in SMALL] + [loss], [((SMALL_ROWS, D), F32)])[0]


_GAP_DEV, _GAP_ROW = divmod(GATE0 + 8, N_IN)
_GAP = CHK0 - GATE0 - 8


def _in_rows_to_proj(g_ref, o_ref, acc_ref):
    runs = [(j, 0, N_IN, N_IN * j) for j in range(_GAP_DEV)]
    runs += [(_GAP_DEV, 0, _GAP_ROW, N_IN * _GAP_DEV), (_GAP_DEV, _GAP_ROW, N_IN, N_IN * _GAP_DEV + _GAP_ROW + _GAP)]
    runs += [(j, 0, N_IN, N_IN * j + _GAP) for j in range(_GAP_DEV + 1, 8)]
    acc_ref[...] = jnp.zeros_like(acc_ref)
    for j, r0, r1, dest in runs:
        start, shift = dest // 16 * 16, dest % 16
        win = -(-(shift + r1 - r0) // 16) * 16
        r = lax.broadcasted_iota(jnp.int32, (win, R_IN), 0)
        c = lax.broadcasted_iota(jnp.int32, (win, R_IN), 1)
        move = jnp.where((c >= r0) & (c < r1) & (r == c - r0 + shift), 1.0, 0.0).astype(BF16)
        acc_ref[start:start + win, :] += _nn(move, g_ref[j])
    o_ref[...] = acc_ref[...].astype(BF16)


def _wgrad_in(pieces, h1):
    n = len(pieces)
    ends = [sum(p.shape[1] for p in pieces[:k + 1]) for k in range(n)]
    assert ends[-1] == PROJ

    def body(*refs):
        piece_refs, (b_ref, o_ref, g_ref), bufs, sem = refs[:n], refs[n:n + 3], refs[n + 3:2 * n + 3], refs[2 * n + 3]
        i = pl.program_id(0)
        copies = [pltpu.make_async_copy(piece_refs[k], bufs[k], sem.at[k]) for k in range(n)]

        @pl.when(i == 0)
        def _():
            for copy in copies:
                copy.start()
            g_ref[PROJ:, :] = jnp.zeros((R_IN - N_IN + 1, D), F32)

        for k in range(n):
            @pl.when(i == k)
            def _(k=k):
                copies[k].wait()
                g_ref[ends[k] - pieces[k].shape[1]:ends[k], :] = _tn(bufs[k][...], b_ref[...])

        row = lax.broadcasted_iota(jnp.int32, (R_IN, D), 0)
        for j in range(8):
            lo = N_IN * j + (_GAP if j > _GAP_DEV else 0)
            hi = N_IN * j + (_GAP if j >= _GAP_DEV else 0)
            assert ends[min(j, n - 1)] >= min(hi + R_IN, PROJ)

            @pl.when(i == j)
            def _(lo=lo, hi=hi):
                v = g_ref[hi:hi + R_IN, :]
                if lo != hi:
                    v = jnp.where(row < _GAP_ROW, g_ref[lo:lo + R_IN, :], v)
                o_ref[...] = jnp.where(row < N_IN, v, 0.0).astype(BF16)

    return _pcall(
        body, name="wgrad_in", grid=(8,),
        in_specs=[ANY_SPEC] * n + [_resident(h1)],
        out_specs=pl.BlockSpec((None, R_IN, D), lambda i: (i, 0, 0)),
        out_shape=jax.ShapeDtypeStruct((8, R_IN, D), BF16),
        scratch_shapes=[pltpu.VMEM((PROJ + R_IN - N_IN + 1, D), F32)] + [pltpu.VMEM(p.shape, BF16) for p in pieces]
        + [pltpu.SemaphoreType.DMA((n,))],
        compiler_params=pltpu.CompilerParams(dimension_semantics=("arbitrary",), vmem_limit_bytes=VMEM_BIG),
    )(*pieces, h1)


def _local_grads(x, mem, tgt, gathered_in, gw_of, sm, on_grads, after=()):
    b_pad = jnp.pad(sm['b_fgt'], ((0, 0), (0, 120)))
    tbl = jnp.pad(sm['rel_bias'], ((0, 0), (0, NREL_PAD - 257)))

    h1, proj, flog, kvp, win_t = _premix_fwd(x, sm['g_mix_pre'], gathered_in, after)
    c = _gate_fwd(flog, b_pad)
    ct3 = c[:, :8].T.reshape(4, 2, T)
    o_f, lse_f = _fox_fwd(proj, c, ct3)
    vt3 = _relvec_fwd(tbl).reshape(4, 2, VW)
    o_c, lse_c = _chk_fwd(proj, kvp, vt3, [gw_of('relay', [o_f])])
    gw = gw_of('done', [o_c])
    w_out, w_mq, w_mk, w_mv, w_mo, w1_t, w2 = (_wblk(gw, n) for n in ('w_out', 'w_mq', 'w_mk', 'w_mv', 'w_mo', 'w_ff1', 'w_ff2'))
    ycat, z, x1, h2, qm = _postmix_fwd(x, o_f, o_c, sm['g_fox_out'], sm['g_chk_out'], w_out,
                                       sm['g_mix_post'], sm['g_mem_pre'], w_mq)
    memn, km, vm = _memkv_fwd(mem, sm['g_mem_kv'], w_mk, w_mv)
    om, ym, x2, h3 = _mem_fwd(qm, x1, km, vm, w_mo, sm['g_mem_post'], sm['g_ff_pre'])

    gs = {}
    dx2, da, dy3, r, loss_acc, gs['g_ff_post'], gs['g_ff_pre'] = _ffn_step(h3, x2, tgt, w1_t, w2, sm['g_ff_post'],
                                                                         sm['g_ff_pre'])
    tok = on_grads('A', _wgrad_group("wgrad_ff", [(da, h3), (r, dy3)], 512), None)
    dx1, dym, dqm, dkm, dvm, gs['g_mem_post'], gs['g_mem_pre'] = _mem_bwd(
        dx2, ym, x1, qm, km, vm, w_mo, w_mq, sm['g_mem_post'], sm['g_mem_pre'], [tok])
    tok = on_grads('A halfway', None, [dx1])
    gs['g_mem_kv'] = _memkv_bwd(dkm, dvm, mem, w_mk, w_mv)
    dz, dof, doc, gs['g_mix_post'], gs['g_fox_out'], gs['g_chk_out'] = _postmix_bwd(
        dx1, z, o_f, o_c, w_out, sm['g_mix_post'], sm['g_fox_out'], sm['g_chk_out'], [tok])
    tok = on_grads('B', _wgrad_whole("wgrad_mem_out", [(ycat, dz), (h2, dqm), (memn, dkm), (memn, dvm), (om, dym)]), None)
    dq_f, dk_f, dv_f, dct, dcq = _fox_bwd(proj, c, ct3, o_f, lse_f, dof, [tok])
    tok = on_grads('B halfway', None, [dq_f])
    dq_c, dk_c, dv_c, gv = _chk_bwd(proj, kvp, vt3, o_c, lse_c, doc, [tok])
    gs['rel_bias'] = _relvec_bwd(gv.reshape(8, VW))[:, :257]
    dc = jnp.pad(dct.reshape(8, T).T + dcq[:, :, :2].transpose(1, 0, 2).reshape(T, 8), ((0, 0), (0, 120)))
    dflog, db = _gate_bwd(dc, flog, b_pad)
    gs['b_fgt'] = db[0:1, :8]
    pieces = [dq_f, dk_f, dv_f, dflog, dq_c, dk_c, dv_c]
    on_grads('C', _wgrad_in(pieces, h1), None)
    tok = on_grads('C halfway', None, [gs['g_mem_kv']])
    grad_x, gs['g_mix_pre'] = _premix_bwd(dx1, x, pieces, win_t, sm['g_mix_pre'], [tok])
    return loss_acc, grad_x, gs


def kernel(x, mem, w_in, b_fgt, rel_bias, g_fox_out, g_chk_out, w_out, g_mix_pre, g_mix_post, g_mem_kv, w_mq, w_mk, w_mv, w_mo, g_mem_pre, g_mem_post, w_ff1, w_ff2, g_ff_pre, g_ff_post, loss_target, m_w_in, m_b_fgt, m_rel_bias, m_g_fox_out, m_g_chk_out, m_w_out, m_g_mix_pre, m_g_mix_post, m_g_mem_kv, m_w_mq, m_w_mk, m_w_mv, m_w_mo, m_g_mem_pre, m_g_mem_post, m_w_ff1, m_w_ff2, m_g_ff_pre, m_g_ff_post, v_w_in, v_b_fgt, v_rel_bias, v_g_fox_out, v_g_chk_out, v_w_out, v_g_mix_pre, v_g_mix_post, v_g_mem_kv, v_w_mq, v_w_mk, v_w_mv, v_w_mo, v_g_mem_pre, v_g_mem_post, v_w_ff1, v_w_ff2, v_g_ff_pre, v_g_ff_post):
    args = dict(locals())
    two_d = lambda a: a.reshape(a.shape[-2:])
    w = {n: two_d(args[n]) for n in WEIGHTS}
    m = {n: two_d(args['m_' + n]) for n in WEIGHTS}
    v = {n: two_d(args['v_' + n]) for n in WEIGHTS}

    sm = {n: w[n] for n in SMALL}
    shard_in = jnp.pad(w['w_in'].T, ((0, R_IN - N_IN), (0, 0))).astype(BF16)
    gathered_in, zero = _allgather(shard_in, "allgather_w_in")
    shard_rest = (jnp.concatenate([w['w_ff1'].T, w['w_ff2'], w['w_out'], w['w_mq'], w['w_mk'], w['w_mv'], w['w_mo']],
                                  axis=0) + zero[0, 0]).astype(BF16)
    gather = {'first': _start_copies("allgather_rest_start", shard_rest, (8, R_REST, D), _gather_plan, 3)}

    def gw_of(stage, after):
        if stage == 'relay':
            gather['block'], land = _wait_copies("allgather_rest_wait", gather['first'], after, _gather_plan)
            gather['second'] = _start_inplace("allgather_rest_relay_start", land, _relay_plan, 4)
            return gather['second'][3]
        land = _wait_inplace("allgather_rest_relay_wait", gather['second'], after, _relay_plan)
        return _gather_forward(land, gather['block'])

    rs = {}

    def on_grads(stage, g, after):
        if stage.endswith('halfway'):
            rs[stage[0]].halfway(after)
            return rs[stage[0]].token
        rs[stage] = _ReduceScatter("rs_" + stage.lower(), g)
        return rs[stage].token

    loss_local, grad_x, gs = _local_grads(x[0], mem[0], loss_target[0], gathered_in, gw_of, sm, on_grads,
                                          [gather['first'][4]])
    grads, deltas, new_m, new_v = {}, {}, {}, {}

    def update(n, out):
        grads[n], deltas[n], new_m[n], new_v[n] = out

    own, got, order = rs['A'].finish([grad_x, rs['C'].token])
    update('w_ff1', _sum_adam(own, got, order, 0, w['w_ff1'], m['w_ff1'], v['w_ff1'], "adamw_w_ff1", transposed=True))
    update('w_ff2', _sum_adam(own, got, order, 512, w['w_ff2'], m['w_ff2'], v['w_ff2'], "adamw_w_ff2"))
    own, got, order = rs['B'].finish([grad_x, rs['C'].token])
    names_b = ('w_out', 'w_mq', 'w_mk', 'w_mv', 'w_mo')
    done = _sum_adam_rows(own, got, order, [w[n] for n in names_b], [m[n] for n in names_b], [v[n] for n in names_b],
                          "adamw_group_b")
    for k, n in enumerate(names_b):
        update(n, done[4 * k:4 * k + 4])

    own, got, order = rs['C'].finish([new_v[n] for n in BIG if n != 'w_in'])
    rows_of = lambda a: jnp.transpose(a, (2, 0, 1))
    done = _sum_adam(own, got, order, 0, rows_of(w_in), rows_of(m_w_in), rows_of(v_w_in), "adamw_w_in")
    update('w_in', [jnp.transpose(a, (1, 2, 0)) for a in done])

    gparts, _ = _allgather(_pack_small(gs, loss_local), "allgather_small_grads", [got])
    small = _adamw_small(gparts, [w[n] for n in SMALL], [m[n] for n in SMALL], [v[n] for n in SMALL])
    loss = small[0][0, 0]
    for t, n in enumerate(SMALL):
        update(n, small[1 + 4 * t:5 + 4 * t])

    out = [loss, grad_x[None]]
    for group in (grads, deltas, new_m, new_v):
        out += [group[n].reshape(args[n].shape) for n in WEIGHTS]
    return tuple(out)
```

```python
import jax
import jax.numpy as jnp
from jax import lax
from jax.experimental import pallas as pl
from jax.experimental.pallas import tpu as pltpu

F32 = jnp.float32
BF16 = jnp.bfloat16
MESH = pl.DeviceIdType.MESH

T = 2048
D = 1024
NMEM = 256
DFF = 4096
EPS = 1e-6
TM = 256
TM_WIDE = 512
TQ = 256
FQ = 512
HD = 64
SCALE = HD ** -0.5
MEM_HEADS = 4
MEM_HD = 256
MEM_SCALE = MEM_HD ** -0.5
NEG = -1e30
LEFT = 512
WIN = LEFT + TQ
VW = 1024
NREL_PAD = 384
PROJ = 3200
GATE0 = 1536
CHK0 = 1664
VMEM_BIG = 56 * 1024 * 1024

ADAM_LR = 0.001
ADAM_B1 = 0.9
ADAM_B2 = 0.999
ADAM_EPS = 1e-08
ADAM_WD = 0.01
ADAM_STEP = 10

N_IN = 385
R_IN = 400
R_REST = 1664
W_ROWS = {'w_ff1': (0, 512), 'w_ff2': (512, 512),
          'w_out': (1024, 128), 'w_mq': (1152, 128), 'w_mk': (1280, 128), 'w_mv': (1408, 128), 'w_mo': (1536, 128)}
SMALL_ROWS = 24
SMALL_SLOT = {'rel_bias': (0, 8, 0, 257), 'b_fgt': (8, 1, 0, 8), 'g_fox_out': (9, 1, 0, 512), 'g_chk_out': (9, 1, 512, 512),
              'g_mix_pre': (10, 1, 0, 1024), 'g_mix_post': (11, 1, 0, 1024), 'g_mem_kv': (12, 1, 0, 1024),
              'g_mem_pre': (13, 1, 0, 1024), 'g_mem_post': (14, 1, 0, 1024), 'g_ff_pre': (15, 1, 0, 1024),
              'g_ff_post': (16, 1, 0, 1024)}

WEIGHTS = ['w_in', 'b_fgt', 'rel_bias', 'g_fox_out', 'g_chk_out', 'w_out', 'g_mix_pre', 'g_mix_post', 'g_mem_kv',
           'w_mq', 'w_mk', 'w_mv', 'w_mo', 'g_mem_pre', 'g_mem_post', 'w_ff1', 'w_ff2', 'g_ff_pre', 'g_ff_post']
BIG = ['w_in', 'w_out', 'w_mq', 'w_mk', 'w_mv', 'w_mo', 'w_ff1', 'w_ff2']
SMALL = [n for n in WEIGHTS if n not in BIG]


def _pcall(body, **kw):
    return pl.pallas_call(body, **kw)


def _nn(a, b):
    return jnp.dot(a, b, preferred_element_type=F32)


def _nt(a, b):
    return lax.dot_general(a, b, (((1,), (1,)), ((), ())), preferred_element_type=F32)


def _tn(a, b):
    return lax.dot_general(a, b, (((0,), (0,)), ((), ())), preferred_element_type=F32)


def _w(ref):
    v = ref[...]
    return v if v.ndim == 2 else v.reshape(-1, v.shape[-1])


def _rstd(x):
    return lax.rsqrt(jnp.mean(x * x, axis=-1, keepdims=True) + EPS)


def _rms(x, g):
    return x * _rstd(x) * g


def _rms_bwd(x, g, dy):
    r = _rstd(x)
    xh = x * r
    dg = jnp.sum(dy * xh, axis=0, keepdims=True)
    dxh = dy * g
    dx = r * (dxh - xh * jnp.mean(dxh * xh, axis=-1, keepdims=True))
    return dx, dg


def _resident(a):
    if isinstance(a, tuple):
        _, shape, index = a
        return pl.BlockSpec(shape, lambda *_: index, pipeline_mode=pl.Buffered(1))
    return pl.BlockSpec(a.shape, lambda *_, nd=a.ndim: (0,) * nd, pipeline_mode=pl.Buffered(1))


def _wblk(gw, name):
    r0, rows = W_ROWS[name]
    return (gw, (8, rows, D), (0, r0 // rows, 0))


def _behind(body, n_in, after):
    if not after:
        return body
    return lambda *refs: body(*refs[:n_in], *refs[n_in + len(after):])


def _tok_call(body, name, tiled, full, outs_tiled, outs_acc=(), rows=T, tm=TM, vmem=None, after=(), scratch=()):
    in_specs = [pl.BlockSpec((tm, a.shape[1]), lambda i: (i, 0)) for a in tiled]
    in_specs += [_resident(a) for a in full] + [ANY_SPEC] * len(after)
    full = [a[0] if isinstance(a, tuple) else a for a in full] + list(after)
    body = _behind(body, len(tiled) + len(full) - len(after), after)
    out_shape = [jax.ShapeDtypeStruct((rows, c), dt) for c, dt in outs_tiled]
    out_shape += [jax.ShapeDtypeStruct(s, F32) for s in outs_acc]
    out_specs = [pl.BlockSpec((tm, c), lambda i: (i, 0)) for c, _ in outs_tiled]
    out_specs += [pl.BlockSpec(s, lambda i, nd=len(s): (0,) * nd) for s in outs_acc]
    return _pcall(
        body, name=name, grid=(rows // tm,), in_specs=in_specs, out_specs=out_specs, out_shape=out_shape,
        scratch_shapes=list(scratch),
        compiler_params=pltpu.CompilerParams(dimension_semantics=("arbitrary",), vmem_limit_bytes=vmem),
    )(*tiled, *full)


def _one_call(body, name, ins, outs, vmem=None):
    whole = lambda s: pl.BlockSpec(s, lambda i, nd=len(s): (0,) * nd)
    return _pcall(
        body, name=name, grid=(1,), in_specs=[_resident(a) for a in ins], out_specs=[whole(s) for s, _ in outs],
        out_shape=[jax.ShapeDtypeStruct(s, dt) for s, dt in outs],
        compiler_params=pltpu.CompilerParams(dimension_semantics=("arbitrary",), vmem_limit_bytes=vmem),
    )(*[a[0] if isinstance(a, tuple) else a for a in ins])


def _premix_fwd(x, g_pre, gathered_in, after=()):
    def body(x_ref, g_ref, gin_ref, h_ref, proj_ref, flog_ref, kvp_ref, w_ref, acc_ref):
        s = pl.program_id(0)

        @pl.when(s == 0)
        def _():
            kvp_ref[...] = jnp.zeros_like(kvp_ref)
            _in_rows_to_proj(gin_ref, w_ref, acc_ref)

        @pl.when(s > 0)
        def _():
            h = _rms(x_ref[...], g_ref[...]).astype(BF16)
            h_ref[...] = h
            p = _nt(h, w_ref[...])
            proj_ref[...] = p.astype(BF16)
            flog_ref[...] = p[:, GATE0:GATE0 + 128]
            kvp_ref[...] = p[:, CHK0 + 512:].astype(BF16)

    tile = lambda c: pl.BlockSpec((LEFT, c), lambda s: (jnp.maximum(s - 1, 0), 0))
    return _pcall(
        _behind(body, 3, after), name="premix_fwd", grid=(T // LEFT + 1,),
        in_specs=[tile(D), _resident(g_pre), _resident(gathered_in)] + [ANY_SPEC] * len(after),
        out_specs=[tile(D), tile(PROJ), tile(128), pl.BlockSpec((LEFT, 1024), lambda s: (s, 0)),
                   pl.BlockSpec((PROJ, D), lambda s: (0, 0))],
        out_shape=[jax.ShapeDtypeStruct((T, D), BF16), jax.ShapeDtypeStruct((T, PROJ), BF16),
                   jax.ShapeDtypeStruct((T, 128), F32), jax.ShapeDtypeStruct((T + LEFT, 1024), BF16),
                   jax.ShapeDtypeStruct((PROJ, D), BF16)],
        scratch_shapes=[pltpu.VMEM((PROJ, D), F32)],
        compiler_params=pltpu.CompilerParams(dimension_semantics=("arbitrary",), vmem_limit_bytes=VMEM_BIG),
    )(x, g_pre, gathered_in, *after)


def _postmix_fwd(x, o_f, o_c, g_fo, g_co, w_out, g_post, g_mpre, w_mq):
    def body(x_ref, of_ref, oc_ref, gfo_ref, gco_ref, wo_ref, gp_ref, gm_ref, wq_ref,
             y_ref, z_ref, x1_ref, h2_ref, qm_ref):
        y_ref[:, :512] = _rms(of_ref[...], gfo_ref[...]).astype(BF16)
        y_ref[:, 512:] = _rms(oc_ref[...], gco_ref[...]).astype(BF16)
        z = _nn(y_ref[...], _w(wo_ref))
        z_ref[...] = z
        x1 = x_ref[...] + _rms(z, gp_ref[...])
        x1_ref[...] = x1
        h2 = _rms(x1, gm_ref[...]).astype(BF16)
        h2_ref[...] = h2
        qm_ref[...] = _nn(h2, _w(wq_ref)).astype(BF16)

    return _tok_call(body, "postmix_fwd", [x, o_f, o_c], [g_fo, g_co, w_out, g_post, g_mpre, w_mq],
                     [(D, BF16), (D, F32), (D, F32), (D, BF16), (D, BF16)], tm=TM_WIDE, vmem=VMEM_BIG)


def _memkv_fwd(mem, g_kv, w_mk, w_mv):
    def body(m_ref, g_ref, wk_ref, wv_ref, mn_ref, k_ref, v_ref):
        mn = _rms(m_ref[...], g_ref[...]).astype(BF16)
        mn_ref[...] = mn
        k_ref[...] = _nn(mn, _w(wk_ref)).astype(BF16)
        v_ref[...] = _nn(mn, _w(wv_ref)).astype(BF16)

    return _tok_call(body, "memkv_fwd", [mem], [g_kv, w_mk, w_mv],
                     [(D, BF16), (D, BF16), (D, BF16)], rows=NMEM, tm=NMEM, vmem=VMEM_BIG)


def _mem_fwd(qm, x1, km, vm, w_mo, g_post, g_fpre):
    def body(q_ref, x1_ref, k_ref, v_ref, wo_ref, gp_ref, gf_ref, om_ref, ym_ref, x2_ref, h3_ref):
        for h in range(MEM_HEADS):
            sl = slice(h * MEM_HD, (h + 1) * MEM_HD)
            s = _nt(q_ref[:, sl], k_ref[:, sl]) * MEM_SCALE
            p = jnp.exp(s - jnp.max(s, axis=-1, keepdims=True))
            p = p / jnp.sum(p, axis=-1, keepdims=True)
            om_ref[:, sl] = _nn(p.astype(BF16), v_ref[:, sl]).astype(BF16)
        ym = _nn(om_ref[...], _w(wo_ref))
        ym_ref[...] = ym
        x2 = x1_ref[...] + _rms(ym, gp_ref[...])
        x2_ref[...] = x2
        h3_ref[...] = _rms(x2, gf_ref[...]).astype(BF16)

    return _tok_call(body, "mem_fwd", [qm, x1], [km, vm, w_mo, g_post, g_fpre],
                     [(D, BF16), (D, F32), (D, F32), (D, BF16)], tm=TM_WIDE, vmem=VMEM_BIG)


def _tri(lower):
    r = lax.broadcasted_iota(jnp.int32, (128, 128), 0)
    c = lax.broadcasted_iota(jnp.int32, (128, 128), 1)
    return jnp.where(r >= c if lower else c >= r, 1.0, 0.0).astype(F32)


def _hdot(a, b):
    return jnp.dot(a, b, preferred_element_type=F32, precision=lax.Precision.HIGHEST)


def _gate_fwd(flog, b_pad):
    def body(f_ref, b_ref, c_ref):
        tri = _tri(True)

        def step(i, carry):
            rows = pl.ds(pl.multiple_of(i * 128, 128), 128)
            z = f_ref[rows, :] + b_ref[...]
            lf = jnp.minimum(z, 0.0) - jnp.log(1.0 + jnp.exp(-jnp.abs(z)))
            cb = _hdot(tri, lf) + carry
            c_ref[rows, :] = cb
            return cb[127:128, :]

        lax.fori_loop(0, T // 128, step, jnp.zeros((1, 128), F32))

    return _one_call(body, "gate_fwd", [flog, b_pad], [((T, 128), F32)])[0]


def _gate_bwd(dc, flog, b_pad):
    def body(dc_ref, f_ref, b_ref, df_ref, db_ref):
        tri = _tri(False)

        def step(j, carry):
            run, db = carry
            i = T // 128 - 1 - j
            rows = pl.ds(pl.multiple_of(i * 128, 128), 128)
            dcb = dc_ref[rows, :]
            rb = _hdot(tri, dcb) + run
            z = f_ref[rows, :] + b_ref[...]
            df = rb * (1.0 / (1.0 + jnp.exp(z)))
            df_ref[rows, :] = df.astype(BF16)
            return run + jnp.sum(dcb, axis=0, keepdims=True), db + jnp.sum(df, axis=0, keepdims=True)

        _, db = lax.fori_loop(0, T // 128, step, (jnp.zeros((1, 128), F32), jnp.zeros((1, 128), F32)))
        db_ref[...] = jnp.broadcast_to(db, (8, 128))

    return _one_call(body, "gate_bwd", [dc, flog, b_pad], [((T, 128), BF16), ((8, 128), F32)])


def _lane_lo(rows=TQ):
    return lax.broadcasted_iota(jnp.int32, (rows, 128), 1) < HD


def _half(v, lo, a, scale=None):
    keep = lo if a == 0 else jnp.logical_not(lo)
    v = v.astype(F32) if scale is None else v.astype(F32) * scale
    return jnp.where(keep, v, 0.0).astype(BF16)


def _fox_specs():
    return [pl.BlockSpec((FQ, 128), lambda h, i: (i, h)),
            pl.BlockSpec((T, 128), lambda h, i: (0, 4 + h)),
            pl.BlockSpec((T, 128), lambda h, i: (0, 8 + h))]


def _lane_pick(x, at):
    lane = lax.broadcasted_iota(jnp.int32, x.shape, 1)
    return jnp.sum(jnp.where(lane == at, x, 0.0), axis=-1, keepdims=True)


def _fox_fwd(proj, c, ct3):
    def body(q_ref, k_ref, v_ref, c_ref, ct_ref, o_ref, l_ref):
        i = pl.program_id(1)
        lo = _lane_lo(FQ)
        causal = lax.broadcasted_iota(jnp.int32, (FQ, FQ), 1) <= lax.broadcasted_iota(jnp.int32, (FQ, FQ), 0)
        q = q_ref[...]
        qs = [_half(q, lo, a, SCALE) for a in range(2)]
        cqs = [_lane_pick(c_ref[...], 2 * pl.program_id(0) + a) for a in range(2)]

        def tile(off, carry, diagonal):
            kblk = k_ref[pl.ds(off, FQ), :]
            vblk = v_ref[pl.ds(off, FQ), :]
            new = []
            for a in range(2):
                m, l, acc = carry[a]
                s = _nt(qs[a], kblk) + (cqs[a] - ct_ref[a:a + 1, pl.ds(off, FQ)])
                if diagonal:
                    s = jnp.where(causal, s, NEG)
                m2 = jnp.maximum(m, jnp.max(s, axis=-1, keepdims=True))
                p = jnp.exp(s - m2)
                alpha = jnp.exp(m - m2)
                new.append((m2, alpha * l + jnp.sum(p, axis=-1, keepdims=True),
                            alpha * acc + _nn(p.astype(BF16), vblk)))
            return tuple(new)

        init = (jnp.full((FQ, 1), NEG, F32), jnp.zeros((FQ, 1), F32), jnp.zeros((FQ, 128), F32))
        carry = lax.fori_loop(0, i, lambda kb, c: tile(pl.multiple_of(kb * FQ, FQ), c, False), (init, init))
        carry = tile(pl.multiple_of(i * FQ, FQ), carry, True)
        outs = []
        for a in range(2):
            m, l, acc = carry[a]
            outs.append(acc / l)
            l_ref[:, 128 * a:128 * a + 128] = jnp.broadcast_to(m + jnp.log(l), (FQ, 128))
        o_ref[...] = jnp.where(lo, outs[0], outs[1])

    return _pcall(
        body, name="fox_fwd", grid=(4, T // FQ),
        in_specs=_fox_specs() + [pl.BlockSpec((FQ, 128), lambda h, i: (i, 0)),
                                 pl.BlockSpec((None, 2, T), lambda h, i: (h, 0, 0))],
        out_specs=[pl.BlockSpec((FQ, 128), lambda h, i: (i, h)), pl.BlockSpec((FQ, 256), lambda h, i: (i, h))],
        out_shape=[jax.ShapeDtypeStruct((T, 512), F32), jax.ShapeDtypeStruct((T, 1024), F32)],
        compiler_params=pltpu.CompilerParams(dimension_semantics=("arbitrary", "arbitrary"), vmem_limit_bytes=VMEM_BIG),
    )(proj, proj, proj, c, ct3)


def _fox_bwd(proj, c, ct3, o, lse, do, after=()):
    def body(q_ref, k_ref, v_ref, c_ref, ct_ref, o_ref, l_ref, do_ref, dq_ref, dkb_ref, dvb_ref, dct_ref, dcq_ref,
             dk_ref, dv_ref):
        i = pl.program_id(1)

        @pl.when(i == 0)
        def _():
            dk_ref[...] = jnp.zeros_like(dk_ref)
            dv_ref[...] = jnp.zeros_like(dv_ref)
            dct_ref[...] = jnp.zeros_like(dct_ref)

        lo = _lane_lo(FQ)
        causal = lax.broadcasted_iota(jnp.int32, (FQ, FQ), 1) <= lax.broadcasted_iota(jnp.int32, (FQ, FQ), 0)
        q = q_ref[...]
        do_v = do_ref[...]
        prod = do_v * o_ref[...]
        qs = [_half(q, lo, a, SCALE) for a in range(2)]
        dos = [_half(do_v, lo, a) for a in range(2)]
        deltas = [jnp.sum(jnp.where(lo if a == 0 else jnp.logical_not(lo), prod, 0.0), axis=-1, keepdims=True)
                  for a in range(2)]
        cqs = [_lane_pick(c_ref[...], 2 * pl.program_id(0) + a) for a in range(2)]
        las = [l_ref[:, 128 * a:128 * a + 1] for a in range(2)]

        def tile(off, carry, diagonal):
            kblk = k_ref[pl.ds(off, FQ), :]
            vblk = v_ref[pl.ds(off, FQ), :]
            new = []
            dk = jnp.zeros((128, FQ), F32)
            dv = jnp.zeros((128, FQ), F32)
            for a in range(2):
                dq_acc, rs = carry[a]
                s = _nt(qs[a], kblk) + (cqs[a] - ct_ref[a:a + 1, pl.ds(off, FQ)])
                if diagonal:
                    s = jnp.where(causal, s, NEG)
                p = jnp.exp(s - las[a])
                ds = p * (_nt(dos[a], vblk) - deltas[a])
                dsb = ds.astype(BF16)
                dk = dk + _tn(qs[a], dsb)
                dv = dv + _tn(dos[a], p.astype(BF16))
                dct_ref[a:a + 1, pl.ds(off, FQ)] -= jnp.sum(ds, axis=0, keepdims=True)
                new.append((dq_acc + _nn(dsb, kblk), rs + jnp.sum(ds, axis=-1, keepdims=True)))
            dk_ref[:, pl.ds(off, FQ)] += dk
            dv_ref[:, pl.ds(off, FQ)] += dv
            return tuple(new)

        init = (jnp.zeros((FQ, 128), F32), jnp.zeros((FQ, 1), F32))
        carry = lax.fori_loop(0, i, lambda kb, c: tile(pl.multiple_of(kb * FQ, FQ), c, False), (init, init))
        carry = tile(pl.multiple_of(i * FQ, FQ), carry, True)
        lane = lax.broadcasted_iota(jnp.int32, (FQ, 128), 1)
        dcq_ref[...] = jnp.where(lane == 0, carry[0][1], jnp.where(lane == 1, carry[1][1], 0.0))
        dq_ref[...] = (jnp.where(lo, carry[0][0], carry[1][0]) * SCALE).astype(BF16)

        @pl.when(i == T // FQ - 1)
        def _():
            dkb_ref[...] = dk_ref[...].T.astype(BF16)
            dvb_ref[...] = dv_ref[...].T.astype(BF16)

    blk = pl.BlockSpec((FQ, 128), lambda h, i: (i, h))
    wide = pl.BlockSpec((FQ, 256), lambda h, i: (i, h))
    rows = pl.BlockSpec((None, 2, T), lambda h, i: (h, 0, 0))
    col = pl.BlockSpec((T, 128), lambda h, i: (0, h))
    return _pcall(
        _behind(body, 8, after), name="fox_bwd", grid=(4, T // FQ),
        in_specs=_fox_specs() + [pl.BlockSpec((FQ, 128), lambda h, i: (i, 0)), rows, blk, wide, blk] + [ANY_SPEC] * len(after),
        out_specs=[blk, col, col, rows, pl.BlockSpec((None, FQ, 128), lambda h, i: (h, i, 0))],
        out_shape=[jax.ShapeDtypeStruct((T, 512), BF16), jax.ShapeDtypeStruct((T, 512), BF16),
                   jax.ShapeDtypeStruct((T, 512), BF16), jax.ShapeDtypeStruct((4, 2, T), F32),
                   jax.ShapeDtypeStruct((4, T, 128), F32)],
        scratch_shapes=[pltpu.VMEM((128, T), F32), pltpu.VMEM((128, T), F32)],
        compiler_params=pltpu.CompilerParams(dimension_semantics=("arbitrary", "arbitrary"), vmem_limit_bytes=VMEM_BIG),
    )(proj, proj, proj, c, ct3, o, lse, do, *after)


def _rel_onehot():
    ridx = lax.broadcasted_iota(jnp.int32, (NREL_PAD, VW), 0)
    j = lax.broadcasted_iota(jnp.int32, (NREL_PAD, VW), 1)
    return jnp.where(ridx == jnp.clip(TQ + LEFT - 1 - j, -128, 128) + 128, 1.0, 0.0).astype(F32)


def _relvec_fwd(tbl):
    def body(t_ref, v_ref):
        v_ref[...] = _hdot(t_ref[...], _rel_onehot())

    return _one_call(body, "relvec_fwd", [tbl], [((8, VW), F32)])[0]


def _relvec_bwd(gv):
    def body(g_ref, t_ref):
        t_ref[...] = lax.dot_general(g_ref[...], _rel_onehot(), (((1,), (1,)), ((), ())),
                                     preferred_element_type=F32, precision=lax.Precision.HIGHEST)

    return _one_call(body, "relvec_bwd", [gv], [((8, NREL_PAD), F32)])[0]


def _chk_bias(vt_ref, a, hidden):
    vb = jnp.broadcast_to(vt_ref[a:a + 1, :], (TQ, VW))
    y = pltpu.roll(vb, VW - (TQ - 1), 1, stride=1, stride_axis=0)[:, :WIN]
    cr = lax.broadcasted_iota(jnp.int32, (TQ, WIN), 0) // 64
    m = lax.broadcasted_iota(jnp.int32, (TQ, WIN), 1)
    return jnp.where((m // 64 >= cr) & (m // 64 <= cr + 8) & (m >= hidden), y, NEG)


def _chk_specs():
    return [pl.BlockSpec((TQ, 128), lambda h, i: (i, CHK0 // 128 + h)),
            pl.BlockSpec((T + LEFT, 128), lambda h, i: (0, h)),
            pl.BlockSpec((T + LEFT, 128), lambda h, i: (0, 4 + h)),
            pl.BlockSpec((None, 2, VW), lambda h, i: (h, 0, 0))]


def _chk_fwd(proj, kvp, vt3, after=()):
    def body(q_ref, k_ref, v_ref, vt_ref, o_ref, l_ref, bias_ref):
        i = pl.program_id(1)

        @pl.when(i == 0)
        def _():
            for first in range(3):
                for a in range(2):
                    bias_ref[first, a] = _chk_bias(vt_ref, a, max(LEFT - first * TQ, 0))

        lo = _lane_lo()
        off = pl.multiple_of(i * TQ, TQ)
        kw = k_ref[pl.ds(off, WIN), :]
        vw = v_ref[pl.ds(off, WIN), :]
        bias_at = jnp.minimum(i, 2)
        q = q_ref[...]
        outs = []
        for a in range(2):
            s = _nt(_half(q, lo, a, SCALE), kw) + bias_ref[bias_at, a]
            m = jnp.max(s, axis=-1, keepdims=True)
            p = jnp.exp(s - m)
            l = jnp.sum(p, axis=-1, keepdims=True)
            outs.append(_nn(p.astype(BF16), vw) / l)
            l_ref[:, 128 * a:128 * a + 128] = jnp.broadcast_to(m + jnp.log(l), (TQ, 128))
        o_ref[...] = jnp.where(lo, outs[0], outs[1])

    return _pcall(
        _behind(body, 4, after), name="chk_fwd", grid=(4, T // TQ), in_specs=_chk_specs() + [ANY_SPEC] * len(after),
        out_specs=[pl.BlockSpec((TQ, 128), lambda h, i: (i, h)), pl.BlockSpec((TQ, 256), lambda h, i: (i, h))],
        out_shape=[jax.ShapeDtypeStruct((T, 512), F32), jax.ShapeDtypeStruct((T, 1024), F32)],
        scratch_shapes=[pltpu.VMEM((3, 2, TQ, WIN), F32)],
        compiler_params=pltpu.CompilerParams(dimension_semantics=("arbitrary", "arbitrary")),
    )(proj, kvp, kvp, vt3, *after)


def _chk_bwd(proj, kvp, vt3, o, lse, do, after=()):
    nq = T // TQ

    def body(q_ref, k_ref, v_ref, vt_ref, o_ref, l_ref, do_ref, dq_ref, dkb_ref, dvb_ref, gv_ref, bias_ref, dsum_ref,
             dk_ref, dv_ref):
        i = pl.program_id(1)

        @pl.when(i == 0)
        def _():
            for first in range(3):
                for a in range(2):
                    bias_ref[first, a] = _chk_bias(vt_ref, a, max(LEFT - first * TQ, 0))
            dsum_ref[...] = jnp.zeros_like(dsum_ref)
            dk_ref[...] = jnp.zeros_like(dk_ref)
            dv_ref[...] = jnp.zeros_like(dv_ref)

        lo = _lane_lo()
        off = pl.multiple_of(i * TQ, TQ)
        kw = k_ref[pl.ds(off, WIN), :]
        vw = v_ref[pl.ds(off, WIN), :]
        bias_at = jnp.minimum(i, 2)
        q = q_ref[...]
        do_v = do_ref[...]
        prod = do_v * o_ref[...]
        dqs = []
        for a in range(2):
            keep = lo if a == 0 else jnp.logical_not(lo)
            qa = _half(q, lo, a, SCALE)
            doa = _half(do_v, lo, a)
            delta = jnp.sum(jnp.where(keep, prod, 0.0), axis=-1, keepdims=True)
            s = _nt(qa, kw) + bias_ref[bias_at, a]
            p = jnp.exp(s - l_ref[:, 128 * a:128 * a + 1])
            ds = p * (_nt(doa, vw) - delta)
            dsum_ref[a] += ds
            dsb = ds.astype(BF16)
            dk_ref[:, pl.ds(off, WIN)] += _tn(qa, dsb)
            dv_ref[:, pl.ds(off, WIN)] += _tn(doa, p.astype(BF16))
            dqs.append(_nn(dsb, kw))
        dq_ref[...] = (jnp.where(lo, dqs[0], dqs[1]) * SCALE).astype(BF16)

        @pl.when(i == nq - 1)
        def _():
            dkb_ref[...] = dk_ref[:, LEFT:].T.astype(BF16)
            dvb_ref[...] = dv_ref[:, LEFT:].T.astype(BF16)
            rr = lax.broadcasted_iota(jnp.int32, (TQ, TQ), 0)
            cc = lax.broadcasted_iota(jnp.int32, (TQ, TQ), 1)
            flip = jnp.where(rr + cc == TQ - 1, 1.0, 0.0).astype(F32)
            for a in range(2):
                dpad = jnp.concatenate([dsum_ref[a], jnp.zeros((TQ, VW - WIN), F32)], axis=1)
                z = pltpu.roll(_hdot(flip, dpad), 0, 1, stride=1, stride_axis=0)
                gv_ref[a:a + 1, :] = jnp.sum(z, axis=0, keepdims=True)

    blk = pl.BlockSpec((TQ, 128), lambda h, i: (i, h))
    wide = pl.BlockSpec((TQ, 256), lambda h, i: (i, h))
    col = pl.BlockSpec((T, 128), lambda h, i: (0, h))
    return _pcall(
        _behind(body, 7, after), name="chk_bwd", grid=(4, nq), in_specs=_chk_specs() + [blk, wide, blk] + [ANY_SPEC] * len(after),
        out_specs=[blk, col, col, pl.BlockSpec((None, 2, VW), lambda h, i: (h, 0, 0))],
        out_shape=[jax.ShapeDtypeStruct((T, 512), BF16), jax.ShapeDtypeStruct((T, 512), BF16),
                   jax.ShapeDtypeStruct((T, 512), BF16), jax.ShapeDtypeStruct((4, 2, VW), F32)],
        scratch_shapes=[pltpu.VMEM((3, 2, TQ, WIN), F32), pltpu.VMEM((2, TQ, WIN), F32),
                        pltpu.VMEM((128, T + LEFT), F32), pltpu.VMEM((128, T + LEFT), F32)],
        compiler_params=pltpu.CompilerParams(dimension_semantics=("arbitrary", "arbitrary")),
    )(proj, kvp, kvp, vt3, o, lse, do, *after)


def _zero_at_start(*refs):
    @pl.when(pl.program_id(0) == 0)
    def _():
        for r in refs:
            r[...] = jnp.zeros_like(r)


def _ffn_step(h3, x2, tgt, w1_t, w2, g_post, g_pre):
    def body(h_ref, x2_ref, t_ref, w1_ref, w2_ref, gp_ref, gf_ref, dx2_ref, da_ref, dy_ref, r_ref, loss_ref, dgp_ref, dgf_ref):
        _zero_at_start(loss_ref, dgp_ref, dgf_ref)
        w1, w2v = _w(w1_ref), _w(w2_ref)
        ra = jnp.maximum(_nt(h_ref[...], w1), 0.0)
        r = jnp.square(ra).astype(BF16)
        r_ref[...] = r
        y = _nn(r, w2v)
        x2v = x2_ref[...]
        e = x2v + _rms(y, gp_ref[...]) - t_ref[...]
        loss_ref[...] += 0.5 * jnp.sum(jnp.sum(e * e, axis=-1, keepdims=True) * (1.0 / D))
        dx3 = e * (1.0 / D)
        dy, dgp = _rms_bwd(y, gp_ref[...], dx3)
        dgp_ref[...] += dgp
        dyb = dy.astype(BF16)
        dy_ref[...] = dyb
        da = (_nt(dyb, w2v) * (2.0 * ra)).astype(BF16)
        da_ref[...] = da
        dh, dgf = _rms_bwd(x2v, gf_ref[...], _nn(da, w1))
        dgf_ref[...] += dgf
        dx2_ref[...] = dx3 + dh

    return _tok_call(body, "ffn_step", [h3, x2, tgt], [w1_t, w2, g_post, g_pre],
                     [(D, F32), (DFF, BF16), (D, BF16), (DFF, BF16)], [(8, 128), (1, D), (1, D)], vmem=VMEM_BIG)


def _mem_bwd(dx2, ym, x1, qm, km, vm, w_mo, w_mq, g_post, g_pre, after=()):
    def body(dx2_ref, ym_ref, x1_ref, q_ref, k_ref, v_ref, wo_ref, wq_ref, gp_ref, gm_ref,
             dx1_ref, dym_ref, dq_ref, dk_ref, dv_ref, dgp_ref, dgm_ref, dom_ref):
        _zero_at_start(dk_ref, dv_ref, dgp_ref, dgm_ref)
        dx2_v = dx2_ref[...]
        dym, dgp = _rms_bwd(ym_ref[...], gp_ref[...], dx2_v)
        dgp_ref[...] += dgp
        dymb = dym.astype(BF16)
        dym_ref[...] = dymb
        dom_ref[...] = _nt(dymb, _w(wo_ref)).astype(BF16)
        for h in range(MEM_HEADS):
            sl = slice(h * MEM_HD, (h + 1) * MEM_HD)
            qh, kh, doh = q_ref[:, sl], k_ref[:, sl], dom_ref[:, sl]
            s = _nt(qh, kh) * MEM_SCALE
            p = jnp.exp(s - jnp.max(s, axis=-1, keepdims=True))
            p = p / jnp.sum(p, axis=-1, keepdims=True)
            dp = _nt(doh, v_ref[:, sl])
            ds = (p * (dp - jnp.sum(p * dp, axis=-1, keepdims=True))).astype(BF16)
            dq_ref[:, sl] = (_nn(ds, kh) * MEM_SCALE).astype(BF16)
            dk_ref[:, sl] += _tn(ds, qh) * MEM_SCALE
            dv_ref[:, sl] += _tn(p.astype(BF16), doh)
        dh, dgm = _rms_bwd(x1_ref[...], gm_ref[...], _nt(dq_ref[...], _w(wq_ref)))
        dgm_ref[...] += dgm
        dx1_ref[...] = dx2_v + dh

    tiled = pl.BlockSpec((TM_WIDE, D), lambda i: (i, 0))
    in_specs = [tiled] * 4 + [_resident(a) for a in (km, vm, w_mo, w_mq, g_post, g_pre)] + [ANY_SPEC] * len(after)
    w_mo, w_mq = w_mo[0], w_mq[0]
    kv = pl.BlockSpec((NMEM, D), lambda i: (0, 0))
    vec = pl.BlockSpec((1, D), lambda i: (0, 0))
    return _pcall(
        _behind(body, 10, after), name="mem_bwd", grid=(T // TM_WIDE,), in_specs=in_specs,
        out_specs=[tiled, tiled, tiled, kv, kv, vec, vec],
        out_shape=[jax.ShapeDtypeStruct((T, D), F32), jax.ShapeDtypeStruct((T, D), BF16),
                   jax.ShapeDtypeStruct((T, D), BF16), jax.ShapeDtypeStruct((NMEM, D), F32),
                   jax.ShapeDtypeStruct((NMEM, D), F32), jax.ShapeDtypeStruct((1, D), F32),
                   jax.ShapeDtypeStruct((1, D), F32)],
        scratch_shapes=[pltpu.VMEM((TM_WIDE, D), BF16)],
        compiler_params=pltpu.CompilerParams(dimension_semantics=("arbitrary",), vmem_limit_bytes=VMEM_BIG),
    )(dx2, ym, x1, qm, km, vm, w_mo, w_mq, g_post, g_pre, *after)


def _memkv_bwd(dkm, dvm, mem, w_mk, w_mv):
    def body(dk_ref, dv_ref, m_ref, wk_ref, wv_ref, dg_ref):
        dmn = _nt(dk_ref[...].astype(BF16), _w(wk_ref)) + _nt(dv_ref[...].astype(BF16), _w(wv_ref))
        mv = m_ref[...]
        dg_ref[...] = jnp.sum(dmn * (mv * _rstd(mv)), axis=0, keepdims=True)

    return _one_call(body, "memkv_bwd", [dkm, dvm, mem, w_mk, w_mv], [((1, D), F32)], vmem=VMEM_BIG)[0]


def _postmix_bwd(dx1, z, o_f, o_c, w_out, g_post, g_fo, g_co, after=()):
    def body(dx1_ref, z_ref, of_ref, oc_ref, wo_ref, gp_ref, gfo_ref, gco_ref,
             dz_ref, dof_ref, doc_ref, dgp_ref, dgfo_ref, dgco_ref):
        _zero_at_start(dgp_ref, dgfo_ref, dgco_ref)
        dz, dgp = _rms_bwd(z_ref[...], gp_ref[...], dx1_ref[...])
        dgp_ref[...] += dgp
        dzb = dz.astype(BF16)
        dz_ref[...] = dzb
        dy = _nt(dzb, _w(wo_ref))
        dof, dgfo = _rms_bwd(of_ref[...], gfo_ref[...], dy[:, :512])
        doc, dgco = _rms_bwd(oc_ref[...], gco_ref[...], dy[:, 512:])
        dof_ref[...] = dof
        doc_ref[...] = doc
        dgfo_ref[...] += dgfo
        dgco_ref[...] += dgco

    return _tok_call(body, "postmix_bwd", [dx1, z, o_f, o_c], [w_out, g_post, g_fo, g_co],
                     [(D, BF16), (512, F32), (512, F32)], [(1, D), (1, 512), (1, 512)], tm=TM_WIDE, vmem=VMEM_BIG,
                     after=after)


def _premix_bwd(dx1, x, pieces, win_t, g_pre, after=()):
    def body(dx1_ref, x_ref, *refs):
        piece_refs, (w_ref, g_ref, dx_ref, dg_ref, dp_ref) = refs[:len(pieces)], refs[len(pieces):]
        _zero_at_start(dg_ref)
        col = 0
        for p in piece_refs:
            dp_ref[:, col:col + p.shape[1]] = p[...]
            col += p.shape[1]
        dh, dg = _rms_bwd(x_ref[...], g_ref[...], _nn(dp_ref[...], w_ref[...]))
        dg_ref[...] += dg
        dx_ref[...] = dx1_ref[...] + dh

    return _tok_call(body, "premix_bwd", [dx1, x] + list(pieces), [win_t, g_pre], [(D, F32)], [(1, D)],
                     tm=TM_WIDE, vmem=VMEM_BIG, after=after, scratch=[pltpu.VMEM((TM_WIDE, PROJ), BF16)])


def _wgrad_group(name, pairs, rows):
    def body(*refs):
        o_ref = refs[-1]
        for k in range(len(pairs)):
            g = _tn(refs[2 * k][...].astype(BF16), refs[2 * k + 1][...].astype(BF16))
            o_ref[k * rows:(k + 1) * rows, :] = g.astype(BF16)

    in_specs, ops = [], []
    for a, b in pairs:
        in_specs += [pl.BlockSpec((a.shape[0], rows), lambda j: (0, j)), _resident(b)]
        ops += [a, b]
    return _pcall(
        body, name=name, grid=(8,), in_specs=in_specs,
        out_specs=pl.BlockSpec((None, len(pairs) * rows, D), lambda j: (j, 0, 0)),
        out_shape=jax.ShapeDtypeStruct((8, len(pairs) * rows, D), BF16),
        compiler_params=pltpu.CompilerParams(dimension_semantics=("arbitrary",), vmem_limit_bytes=VMEM_BIG),
    )(*ops)


def _wgrad_whole(name, pairs):
    n = len(pairs)
    ops = [x for pair in pairs for x in pair]
    rows = pairs[0][0].shape[1] // 8

    def body(*refs):
        hbm, o_ref, bufs, sem = refs[:2 * n], refs[2 * n], refs[2 * n + 1:4 * n + 1], refs[4 * n + 1]
        k = pl.program_id(0)
        copies = [pltpu.make_async_copy(hbm[t], bufs[t], sem.at[t]) for t in range(2 * n)]

        @pl.when(k == 0)
        def _():
            for copy in copies:
                copy.start()

        for t in range(n):
            @pl.when(k == t)
            def _(t=t):
                copies[2 * t].wait()
                copies[2 * t + 1].wait()
                g = _tn(bufs[2 * t][...].astype(BF16), bufs[2 * t + 1][...].astype(BF16))
                o_ref[...] = g.reshape(8, rows, D).astype(BF16)

    return _pcall(
        body, name=name, grid=(n,), in_specs=[ANY_SPEC] * (2 * n),
        out_specs=pl.BlockSpec((8, rows, D), lambda k: (0, k, 0)),
        out_shape=jax.ShapeDtypeStruct((8, n * rows, D), BF16),
        scratch_shapes=[pltpu.VMEM(x.shape, x.dtype) for x in ops] + [pltpu.SemaphoreType.DMA((2 * n,))],
        compiler_params=pltpu.CompilerParams(dimension_semantics=("arbitrary",), vmem_limit_bytes=VMEM_BIG),
    )(*ops)


def _adam_math(w, g, m, v):
    m2 = ADAM_B1 * m + (1.0 - ADAM_B1) * g
    v2 = ADAM_B2 * v + (1.0 - ADAM_B2) * jnp.square(g)
    m_hat = m2 / (1.0 - ADAM_B1 ** ADAM_STEP)
    v_hat = v2 / (1.0 - ADAM_B2 ** ADAM_STEP)
    delta = -ADAM_LR * (m_hat / (jnp.sqrt(v_hat) + ADAM_EPS) + ADAM_WD * w)
    return delta, m2, v2


def _adamw_small(gparts, ws, ms, vs):
    n = len(SMALL)

    def body(g_ref, *refs):
        w_refs, m_refs, v_refs = refs[:n], refs[n:2 * n], refs[2 * n:3 * n]
        outs, sum_ref = refs[3 * n:-1], refs[-1]
        g = g_ref[0]
        for k in range(1, 8):
            g = g + g_ref[k]
        sum_ref[...] = g
        outs[0][...] = sum_ref[17:18, 0:128]
        for t, name in enumerate(SMALL):
            r0, nr, c0, nc = SMALL_SLOT[name]
            gt = sum_ref[r0:r0 + nr, c0:c0 + nc]
            out = (gt,) + _adam_math(w_refs[t][...], gt, m_refs[t][...], v_refs[t][...])
            for o_ref, val in zip(outs[1 + 4 * t:5 + 4 * t], out):
                o_ref[...] = val

    whole = lambda s: pl.BlockSpec(s, lambda i, nd=len(s): (0,) * nd)
    ins = [gparts] + list(ws) + list(ms) + list(vs)
    out_shapes = [(1, 128)] + [a.shape for a in ws for _ in range(4)]
    return _pcall(
        body, name="adamw_small", grid=(1,), in_specs=[whole(a.shape) for a in ins],
        out_specs=[whole(s) for s in out_shapes], out_shape=[jax.ShapeDtypeStruct(s, F32) for s in out_shapes],
        scratch_shapes=[pltpu.VMEM((SMALL_ROWS, D), F32)],
        compiler_params=pltpu.CompilerParams(dimension_semantics=("arbitrary",)),
    )(*ins)


def _add_halves(g4, theirs, core, name):
    rows = tr = g4.shape[2]

    def body(c_ref, a_ref, b_ref, o_ref):
        o_ref[...] = (a_ref[...].astype(F32) + b_ref[...].astype(F32)).astype(BF16)

    grid_spec = pltpu.PrefetchScalarGridSpec(
        num_scalar_prefetch=1, grid=(4, rows // tr),
        in_specs=[pl.BlockSpec((None, None, tr, D), lambda j, i, c: (j, c[0], i, 0)),
                  pl.BlockSpec((None, None, tr, D), lambda j, i, c: (j, 0, i, 0))],
        out_specs=pl.BlockSpec((None, tr, D), lambda j, i, c: (j, i, 0)))
    return _pcall(
        body, name=name, grid_spec=grid_spec, out_shape=jax.ShapeDtypeStruct((4, rows, D), BF16),
        compiler_params=pltpu.CompilerParams(dimension_semantics=("arbitrary", "arbitrary"), vmem_limit_bytes=VMEM_BIG),
    )(core, g4, theirs)


def _sum_adam(own, got, order, r0, w, m, v, name, transposed=False):
    ragged = w.ndim == 3
    n = w.shape[1] if transposed else w.shape[0]
    tr = n if ragged else min(n, 256)
    rows = own.shape[1] if ragged else tr
    at = (slice(None), 0, slice(None)) if ragged else Ellipsis

    def body(o_ref, a_ref, b_ref, c_ref, d_ref, w_ref, m_ref, v_ref, *out_refs):
        f = lambda r: r[0:tr, :].astype(F32)
        g = ((f(a_ref) + f(b_ref)) + f(c_ref)) + f(d_ref)
        g = g.T if transposed else g
        for ref, val in zip(out_refs, (g,) + _adam_math(w_ref[at], g, m_ref[at], v_ref[at])):
            ref[at] = val

    slot = lambda k: pl.BlockSpec((None, rows, D), lambda i, o: (o[k], r0 // rows + i, 0))
    if ragged:
        wspec = pl.BlockSpec((n, 1, D), lambda i, o: (0, 0, 0))
    else:
        wspec = pl.BlockSpec((D, tr), lambda i, o: (0, i)) if transposed else pl.BlockSpec((tr, D), lambda i, o: (i, 0))
    grid_spec = pltpu.PrefetchScalarGridSpec(
        num_scalar_prefetch=1, grid=(n // tr,), in_specs=[slot(0), slot(1), slot(2), slot(3), wspec, wspec, wspec],
        out_specs=[wspec] * 4)
    return _pcall(
        body, name=name, grid_spec=grid_spec, out_shape=[jax.ShapeDtypeStruct(w.shape, F32)] * 4,
        compiler_params=pltpu.CompilerParams(dimension_semantics=("arbitrary",)),
    )(order, own, got, got, got, w, m, v)


def _sum_adam_tiled(own, got, order, parts, name):
    k, tr = len(parts), 256
    n = parts[0][1].shape[1] if parts[0][4] else parts[0][1].shape[0]

    def body(o_ref, *refs):
        slots, ins, outs = refs[:4 * k], refs[4 * k:7 * k], refs[7 * k:]
        for t, part in enumerate(parts):
            f = lambda r: r[...].astype(F32)
            a_ref, b_ref, c_ref, d_ref = slots[4 * t:4 * t + 4]
            g = ((f(a_ref) + f(b_ref)) + f(c_ref)) + f(d_ref)
            g = g.T if part[4] else g
            w_ref, m_ref, v_ref = ins[3 * t:3 * t + 3]
            for ref, val in zip(outs[4 * t:4 * t + 4], (g,) + _adam_math(w_ref[...], g, m_ref[...], v_ref[...])):
                ref[...] = val

    slot_specs, w_specs, out_shape = [], [], []
    for r0, w, _, _, transposed in parts:
        slot_specs += [pl.BlockSpec((None, tr, D), lambda i, o, s=s, b=r0 // tr: (o[s], b + i, 0)) for s in range(4)]
        spec = pl.BlockSpec((D, tr), lambda i, o: (0, i)) if transposed else pl.BlockSpec((tr, D), lambda i, o: (i, 0))
        w_specs.append(spec)
        out_shape += [jax.ShapeDtypeStruct(w.shape, F32)] * 4
    grid_spec = pltpu.PrefetchScalarGridSpec(
        num_scalar_prefetch=1, grid=(n // tr,), in_specs=slot_specs + [s for s in w_specs for _ in range(3)],
        out_specs=[s for s in w_specs for _ in range(4)])
    return _pcall(
        body, name=name, grid_spec=grid_spec, out_shape=out_shape,
        compiler_params=pltpu.CompilerParams(dimension_semantics=("arbitrary",), vmem_limit_bytes=VMEM_BIG),
    )(order, *([own, got, got, got] * k), *[a for _, w, m, v, _ in parts for a in (w, m, v)])


def _sum_adam_rows(own, got, order, ws, ms, vs, name):
    n, rows = len(ws), ws[0].shape[0]

    def body(o_ref, a_ref, b_ref, c_ref, d_ref, *refs):
        ins, outs = refs[:3 * n], refs[3 * n:]
        for t in range(n):
            r = slice(t * rows, (t + 1) * rows)
            f = lambda ref: ref[r, :].astype(F32)
            g = ((f(a_ref) + f(b_ref)) + f(c_ref)) + f(d_ref)
            out = (g,) + _adam_math(ins[t][...], g, ins[n + t][...], ins[2 * n + t][...])
            for o, val in zip(outs[4 * t:4 * t + 4], out):
                o[...] = val

    slot = lambda k: pl.BlockSpec((None, n * rows, D), lambda i, o: (o[k], 0, 0), pipeline_mode=pl.Buffered(1))
    wspec = pl.BlockSpec((rows, D), lambda i, o: (0, 0), pipeline_mode=pl.Buffered(1))
    grid_spec = pltpu.PrefetchScalarGridSpec(
        num_scalar_prefetch=1, grid=(1,), in_specs=[slot(0), slot(1), slot(2), slot(3)] + [wspec] * (3 * n),
        out_specs=[pl.BlockSpec((rows, D), lambda i, o: (0, 0))] * (4 * n))
    return _pcall(
        body, name=name, grid_spec=grid_spec, out_shape=[jax.ShapeDtypeStruct((rows, D), F32)] * (4 * n),
        compiler_params=pltpu.CompilerParams(dimension_semantics=("arbitrary",), vmem_limit_bytes=VMEM_BIG),
    )(order, own, got, got, got, *ws, *ms, *vs)


def _place():
    return lax.axis_index("x"), lax.axis_index("y"), lax.axis_index("c")


def _allgather(block, name, after=()):
    rows = block.shape[0]
    split = (rows // 2 + 15) // 16 * 16

    def body(x_ref, out_ref, token, send_sems, recv_sems, local_sem):
        token[...] = jnp.zeros_like(token)
        x, y, c = _place()
        me, sib = (x, y, c), (x, y, 1 - c)
        xn, yn, dg = (1 - x, y), (x, 1 - y), (1 - x, 1 - y)
        lo, hi = pl.ds(0, split), pl.ds(split, rows - split)

        def copy(k, blk, to, part=None, src=None):
            index = 4 * blk[0] + 2 * blk[1] + blk[2]
            view = out_ref.at[index] if part is None else out_ref.at[index, part]
            return pltpu.make_async_remote_copy(
                src_ref=view if src is None else src, dst_ref=view,
                send_sem=send_sems.at[k], recv_sem=recv_sems.at[k], device_id=to, device_id_type=MESH)

        def start(*copies):
            for cp in copies:
                cp.start()
            return list(copies)

        mine = pltpu.make_async_copy(x_ref, out_ref.at[4 * x + 2 * y + c], local_sem)
        mine.start()
        sent = start(copy(0, me, sib, src=x_ref), copy(1, me, (*xn, c), src=x_ref), copy(2, me, (*yn, c), src=x_ref))
        copy(1, (*xn, c), me).wait_recv()
        sent += start(copy(3, (*xn, c), sib), copy(5, (*xn, c), (*yn, c), part=lo))
        copy(2, (*yn, c), me).wait_recv()
        sent += start(copy(4, (*yn, c), sib), copy(6, (*yn, c), (*xn, c), part=hi))
        copy(5, (*dg, c), me, part=lo).wait_recv()
        copy(6, (*dg, c), me, part=hi).wait_recv()
        sent += start(copy(7, (*dg, c), sib))
        for k, blk in ((0, sib), (3, (*xn, 1 - c)), (4, (*yn, 1 - c)), (7, (*dg, 1 - c))):
            copy(k, blk, me).wait_recv()
        for cp in sent:
            cp.wait_send()
        mine.wait()

    return _pcall(
        _behind(body, 1, after), name=name,
        out_shape=[jax.ShapeDtypeStruct((8,) + block.shape, block.dtype), jax.ShapeDtypeStruct((8, 128), F32)],
        in_specs=[pl.BlockSpec(memory_space=pl.ANY)] * (1 + len(after)),
        out_specs=[pl.BlockSpec(memory_space=pl.ANY), pl.BlockSpec(memory_space=pltpu.VMEM)],
        scratch_shapes=[pltpu.SemaphoreType.DMA((8,)), pltpu.SemaphoreType.DMA((8,)), pltpu.SemaphoreType.DMA(())],
        compiler_params=pltpu.CompilerParams(has_side_effects=True),
    )(block, *after)


HBM_SPEC = pl.BlockSpec(memory_space=pltpu.HBM)
SEM_SPEC = pl.BlockSpec(memory_space=pltpu.SEMAPHORE)
ANY_SPEC = pl.BlockSpec(memory_space=pl.ANY)
EFFECT = pltpu.SideEffectType.DATAFLOW_SIDE_EFFECTING


def _in_hbm(a):
    return pltpu.with_memory_space_constraint(a, pltpu.HBM)


def _start_copies(name, src, land_shape, plan, n):
    def body(src_ref, land_ref, send_sems, recv_sems, src_thru, land_thru, token):
        for k, (s, d, to, _) in enumerate(plan(src_ref, land_ref)):
            pltpu.make_async_remote_copy(src_ref=s, dst_ref=d, send_sem=send_sems.at[k], recv_sem=recv_sems.at[k],
                                         device_id=to, device_id_type=MESH).start()
        token[...] = jnp.zeros_like(token)

    return _pcall(
        body, name=name,
        out_shape=(pltpu.SemaphoreType.DMA((n,)), pltpu.SemaphoreType.DMA((n,)), pltpu.HBM(src.shape, src.dtype),
                   pltpu.HBM(land_shape, src.dtype), jax.ShapeDtypeStruct((8, 128), F32)),
        in_specs=(HBM_SPEC, HBM_SPEC),
        out_specs=(SEM_SPEC, SEM_SPEC, HBM_SPEC, HBM_SPEC, pl.BlockSpec(memory_space=pltpu.VMEM)),
        input_output_aliases={0: 2, 1: 3}, compiler_params=pltpu.CompilerParams(has_side_effects=EFFECT),
    )(_in_hbm(src), _in_hbm(lax.empty(land_shape, src.dtype)))


def _wait_copies(name, started, after, plan):
    send_sems, recv_sems, src_thru, land_thru, _ = started

    def body(src_ref, land_ref, send_sems, recv_sems, *rest):
        for k, (s, _, to, mine) in enumerate(plan(src_ref, land_ref)):
            cp = pltpu.make_async_remote_copy(src_ref=s, dst_ref=mine, send_sem=send_sems.at[k],
                                              recv_sem=recv_sems.at[k], device_id=to, device_id_type=MESH)
            cp.wait_send()
            cp.wait_recv()

    return _pcall(
        body, name=name,
        out_shape=(pltpu.HBM(src_thru.shape, src_thru.dtype), pltpu.HBM(land_thru.shape, land_thru.dtype)),
        in_specs=(HBM_SPEC, HBM_SPEC, SEM_SPEC, SEM_SPEC) + (ANY_SPEC,) * len(after), out_specs=(HBM_SPEC, HBM_SPEC),
        input_output_aliases={0: 0, 1: 1}, compiler_params=pltpu.CompilerParams(has_side_effects=EFFECT),
    )(src_thru, land_thru, send_sems, recv_sems, *after)


def _start_inplace(name, buf, plan, n):
    def body(buf_ref, send_sems, recv_sems, buf_thru, token):
        for k, (s, d, to, _) in enumerate(plan(buf_ref, buf_ref)):
            pltpu.make_async_remote_copy(src_ref=s, dst_ref=d, send_sem=send_sems.at[k], recv_sem=recv_sems.at[k],
                                         device_id=to, device_id_type=MESH).start()
        token[...] = jnp.zeros_like(token)

    return _pcall(
        body, name=name,
        out_shape=(pltpu.SemaphoreType.DMA((n,)), pltpu.SemaphoreType.DMA((n,)), pltpu.HBM(buf.shape, buf.dtype),
                   jax.ShapeDtypeStruct((8, 128), F32)),
        in_specs=(HBM_SPEC,), out_specs=(SEM_SPEC, SEM_SPEC, HBM_SPEC, pl.BlockSpec(memory_space=pltpu.VMEM)),
        input_output_aliases={0: 2}, compiler_params=pltpu.CompilerParams(has_side_effects=EFFECT),
    )(_in_hbm(buf))


def _wait_inplace(name, started, after, plan):
    send_sems, recv_sems, buf_thru, _ = started

    def body(buf_ref, send_sems, recv_sems, *rest):
        for k, (s, _, to, mine) in enumerate(plan(buf_ref, buf_ref)):
            cp = pltpu.make_async_remote_copy(src_ref=s, dst_ref=mine, send_sem=send_sems.at[k],
                                              recv_sem=recv_sems.at[k], device_id=to, device_id_type=MESH)
            cp.wait_send()
            cp.wait_recv()

    return _pcall(
        body, name=name, out_shape=pltpu.HBM(buf_thru.shape, buf_thru.dtype),
        in_specs=(HBM_SPEC, SEM_SPEC, SEM_SPEC) + (ANY_SPEC,) * len(after), out_specs=HBM_SPEC,
        input_output_aliases={0: 0}, compiler_params=pltpu.CompilerParams(has_side_effects=EFFECT),
    )(buf_thru, send_sems, recv_sems, *after)


def _gather_plan(src_ref, land_ref):
    x, y, c = _place()
    peers = [(x, y, 1 - c), (1 - x, y, c), (x, 1 - y, c)]
    return [(src_ref, land_ref.at[4 * x + 2 * y + c], p, land_ref.at[4 * p[0] + 2 * p[1] + p[2]]) for p in peers]


def _relay_plan(buf_ref, _):
    x, y, c = _place()
    slot = lambda p, pc: 4 * p[0] + 2 * p[1] + pc
    xn, yn, dg, sib = (1 - x, y), (x, 1 - y), (1 - x, 1 - y), (x, y, 1 - c)
    half = buf_ref.shape[1] // 2
    lo, hi = pl.ds(0, half), pl.ds(half, half)
    return [(buf_ref.at[slot(xn, c)], buf_ref.at[slot(xn, c)], sib, buf_ref.at[slot(xn, 1 - c)]),
            (buf_ref.at[slot(yn, c)], buf_ref.at[slot(yn, c)], sib, buf_ref.at[slot(yn, 1 - c)]),
            (buf_ref.at[slot(xn, c), lo], buf_ref.at[slot(xn, c), lo], (*yn, c), buf_ref.at[slot(dg, c), lo]),
            (buf_ref.at[slot(yn, c), hi], buf_ref.at[slot(yn, c), hi], (*xn, c), buf_ref.at[slot(dg, c), hi])]


def _swap_plan(src_ref, land_ref):
    x, y, c = _place()
    return [(src_ref.at[:, pl.ds(1 - c, 1)], land_ref, (x, y, 1 - c), land_ref)]


def _exchange_plan(src_ref, land_ref):
    x, y, c = _place()
    chips = [(1 - x, y), (x, 1 - y), (1 - x, 1 - y)]
    return [(src_ref.at[2 * px + py], land_ref.at[2 * x + y], (px, py, c), land_ref.at[2 * px + py]) for px, py in chips]


def _gather_forward(land, block):
    def body(land_ref, out_ref, send_sems, recv_sems):
        x, y, c = _place()
        chips = [(1 - x, 1 - y)]

        def copy(k, px, py, pc):
            blk = out_ref.at[4 * px + 2 * py + pc]
            return pltpu.make_async_remote_copy(src_ref=blk, dst_ref=blk, send_sem=send_sems.at[k],
                                                recv_sem=recv_sems.at[k], device_id=(x, y, 1 - c), device_id_type=MESH)

        sent = [copy(k, px, py, c) for k, (px, py) in enumerate(chips)]
        for cp in sent:
            cp.start()
        for k, (px, py) in enumerate(chips):
            copy(k, px, py, 1 - c).wait_recv()
        for cp in sent:
            cp.wait_send()

    land = _pcall(
        body, name="allgather_rest_forward", out_shape=jax.ShapeDtypeStruct(land.shape, land.dtype),
        in_specs=[ANY_SPEC], out_specs=ANY_SPEC, input_output_aliases={0: 0},
        scratch_shapes=[pltpu.SemaphoreType.DMA((1,)), pltpu.SemaphoreType.DMA((1,))],
        compiler_params=pltpu.CompilerParams(has_side_effects=True),
    )(land)

    rows = block.shape[0]
    tr = rows // 4

    def place(me_ref, x_ref, land_ref, out_ref):
        out_ref[...] = x_ref[...]

    x, y, c = _place()
    grid_spec = pltpu.PrefetchScalarGridSpec(
        num_scalar_prefetch=1, grid=(rows // tr,),
        in_specs=[pl.BlockSpec((tr, D), lambda i, me: (i, 0)), ANY_SPEC],
        out_specs=pl.BlockSpec((None, tr, D), lambda i, me: (me[0], i, 0)))
    return _pcall(
        place, name="allgather_rest_own", grid_spec=grid_spec, out_shape=jax.ShapeDtypeStruct(land.shape, land.dtype),
        input_output_aliases={2: 0}, compiler_params=pltpu.CompilerParams(dimension_semantics=("arbitrary",)),
    )((4 * x + 2 * y + c).reshape(1), block, land)


class _ReduceScatter:
    def __init__(self, name, g):
        self.name = name
        rows = g.shape[1]
        self.started = _start_copies(name + "_swap_start", g.reshape(4, 2, rows, D), (4, 1, rows, D), _swap_plan, 1)
        self.token = self.started[4]

    def halfway(self, after):
        g4, theirs = _wait_copies(self.name + "_swap_wait", self.started, after, _swap_plan)
        self.own = _add_halves(g4, theirs, lax.axis_index("c").reshape(1), self.name + "_add_halves")
        self.started = _start_copies(self.name + "_exch_start", self.own, self.own.shape, _exchange_plan, 3)
        self.token = self.started[4]

    def finish(self, after):
        own, got = _wait_copies(self.name + "_exch_wait", self.started, after, _exchange_plan)
        chip = 2 * lax.axis_index("x") + lax.axis_index("y")
        return own, got, (chip + jnp.arange(4, dtype=jnp.int32)) % 4


def _pack_small(p, loss):
    def body(*refs):
        o_ref = refs[-1]
        o_ref[...] = jnp.zeros_like(o_ref)
        for ref, name in zip(refs, SMALL):
            r0, nr, c0, nc = SMALL_SLOT[name]
            o_ref[r0:r0 + nr, c0:c0 + nc] = ref[...]
        o_ref[17:18, 0:128] = refs[len(SMALL)][0:1, :]

    return _one_call(body, "pack_small_grads", [p[n] for n in SMALL] + [loss], [((SMALL_ROWS, D), F32)])[0]


_GAP_DEV, _GAP_ROW = divmod(GATE0 + 8, N_IN)
_GAP = CHK0 - GATE0 - 8


def _in_rows_to_proj(g_ref, o_ref, acc_ref):
    runs = [(j, 0, N_IN, N_IN * j) for j in range(_GAP_DEV)]
    runs += [(_GAP_DEV, 0, _GAP_ROW, N_IN * _GAP_DEV), (_GAP_DEV, _GAP_ROW, N_IN, N_IN * _GAP_DEV + _GAP_ROW + _GAP)]
    runs += [(j, 0, N_IN, N_IN * j + _GAP) for j in range(_GAP_DEV + 1, 8)]
    acc_ref[...] = jnp.zeros_like(acc_ref)
    for j, r0, r1, dest in runs:
        start, shift = dest // 16 * 16, dest % 16
        win = -(-(shift + r1 - r0) // 16) * 16
        r = lax.broadcasted_iota(jnp.int32, (win, R_IN), 0)
        c = lax.broadcasted_iota(jnp.int32, (win, R_IN), 1)
        move = jnp.where((c >= r0) & (c < r1) & (r == c - r0 + shift), 1.0, 0.0).astype(BF16)
        acc_ref[start:start + win, :] += _nn(move, g_ref[j])
    o_ref[...] = acc_ref[...].astype(BF16)


def _wgrad_in(pieces, h1):
    n = len(pieces)
    ends = [sum(p.shape[1] for p in pieces[:k + 1]) for k in range(n)]
    assert ends[-1] == PROJ

    def body(*refs):
        piece_refs, (b_ref, o_ref, g_ref), bufs, sem = refs[:n], refs[n:n + 3], refs[n + 3:2 * n + 3], refs[2 * n + 3]
        i = pl.program_id(0)
        copies = [pltpu.make_async_copy(piece_refs[k], bufs[k], sem.at[k]) for k in range(n)]

        @pl.when(i == 0)
        def _():
            for copy in copies:
                copy.start()
            g_ref[PROJ:, :] = jnp.zeros((R_IN - N_IN + 1, D), F32)

        for k in range(n):
            @pl.when(i == k)
            def _(k=k):
                copies[k].wait()
                g_ref[ends[k] - pieces[k].shape[1]:ends[k], :] = _tn(bufs[k][...], b_ref[...])

        row = lax.broadcasted_iota(jnp.int32, (R_IN, D), 0)
        for j in range(8):
            lo = N_IN * j + (_GAP if j > _GAP_DEV else 0)
            hi = N_IN * j + (_GAP if j >= _GAP_DEV else 0)
            assert ends[min(j, n - 1)] >= min(hi + R_IN, PROJ)

            @pl.when(i == j)
            def _(lo=lo, hi=hi):
                v = g_ref[hi:hi + R_IN, :]
                if lo != hi:
                    v = jnp.where(row < _GAP_ROW, g_ref[lo:lo + R_IN, :], v)
                o_ref[...] = jnp.where(row < N_IN, v, 0.0).astype(BF16)

    return _pcall(
        body, name="wgrad_in", grid=(8,),
        in_specs=[ANY_SPEC] * n + [_resident(h1)],
        out_specs=pl.BlockSpec((None, R_IN, D), lambda i: (i, 0, 0)),
        out_shape=jax.ShapeDtypeStruct((8, R_IN, D), BF16),
        scratch_shapes=[pltpu.VMEM((PROJ + R_IN - N_IN + 1, D), F32)] + [pltpu.VMEM(p.shape, BF16) for p in pieces]
        + [pltpu.SemaphoreType.DMA((n,))],
        compiler_params=pltpu.CompilerParams(dimension_semantics=("arbitrary",), vmem_limit_bytes=VMEM_BIG),
    )(*pieces, h1)


def _local_grads(x, mem, tgt, gathered_in, gw_of, sm, on_grads, after=()):
    b_pad = jnp.pad(sm['b_fgt'], ((0, 0), (0, 120)))
    tbl = jnp.pad(sm['rel_bias'], ((0, 0), (0, NREL_PAD - 257)))

    h1, proj, flog, kvp, win_t = _premix_fwd(x, sm['g_mix_pre'], gathered_in, after)
    c = _gate_fwd(flog, b_pad)
    ct3 = c[:, :8].T.reshape(4, 2, T)
    o_f, lse_f = _fox_fwd(proj, c, ct3)
    vt3 = _relvec_fwd(tbl).reshape(4, 2, VW)
    o_c, lse_c = _chk_fwd(proj, kvp, vt3, [gw_of('relay', [o_f])])
    gw = gw_of('done', [o_c])
    w_out, w_mq, w_mk, w_mv, w_mo, w1_t, w2 = (_wblk(gw, n) for n in ('w_out', 'w_mq', 'w_mk', 'w_mv', 'w_mo', 'w_ff1', 'w_ff2'))
    ycat, z, x1, h2, qm = _postmix_fwd(x, o_f, o_c, sm['g_fox_out'], sm['g_chk_out'], w_out,
                                       sm['g_mix_post'], sm['g_mem_pre'], w_mq)
    memn, km, vm = _memkv_fwd(mem, sm['g_mem_kv'], w_mk, w_mv)
    om, ym, x2, h3 = _mem_fwd(qm, x1, km, vm, w_mo, sm['g_mem_post'], sm['g_ff_pre'])

    gs = {}
    dx2, da, dy3, r, loss_acc, gs['g_ff_post'], gs['g_ff_pre'] = _ffn_step(h3, x2, tgt, w1_t, w2, sm['g_ff_post'],
                                                                         sm['g_ff_pre'])
    tok = on_grads('A', _wgrad_group("wgrad_ff", [(da, h3), (r, dy3)], 512), None)
    dx1, dym, dqm, dkm, dvm, gs['g_mem_post'], gs['g_mem_pre'] = _mem_bwd(
        dx2, ym, x1, qm, km, vm, w_mo, w_mq, sm['g_mem_post'], sm['g_mem_pre'], [tok])
    tok = on_grads('A halfway', None, [dx1])
    gs['g_mem_kv'] = _memkv_bwd(dkm, dvm, mem, w_mk, w_mv)
    dz, dof, doc, gs['g_mix_post'], gs['g_fox_out'], gs['g_chk_out'] = _postmix_bwd(
        dx1, z, o_f, o_c, w_out, sm['g_mix_post'], sm['g_fox_out'], sm['g_chk_out'], [tok])
    tok = on_grads('B', _wgrad_whole("wgrad_mem_out", [(ycat, dz), (h2, dqm), (memn, dkm), (memn, dvm), (om, dym)]), None)
    dq_f, dk_f, dv_f, dct, dcq = _fox_bwd(proj, c, ct3, o_f, lse_f, dof, [tok])
    tok = on_grads('B halfway', None, [dq_f])
    dq_c, dk_c, dv_c, gv = _chk_bwd(proj, kvp, vt3, o_c, lse_c, doc, [tok])
    gs['rel_bias'] = _relvec_bwd(gv.reshape(8, VW))[:, :257]
    dc = jnp.pad(dct.reshape(8, T).T + dcq[:, :, :2].transpose(1, 0, 2).reshape(T, 8), ((0, 0), (0, 120)))
    dflog, db = _gate_bwd(dc, flog, b_pad)
    gs['b_fgt'] = db[0:1, :8]
    pieces = [dq_f, dk_f, dv_f, dflog, dq_c, dk_c, dv_c]
    on_grads('C', _wgrad_in(pieces, h1), None)
    tok = on_grads('C halfway', None, [gs['g_mem_kv']])
    grad_x, gs['g_mix_pre'] = _premix_bwd(dx1, x, pieces, win_t, sm['g_mix_pre'], [tok])
    return loss_acc, grad_x, gs


def kernel(x, mem, w_in, b_fgt, rel_bias, g_fox_out, g_chk_out, w_out, g_mix_pre, g_mix_post, g_mem_kv, w_mq, w_mk, w_mv, w_mo, g_mem_pre, g_mem_post, w_ff1, w_ff2, g_ff_pre, g_ff_post, loss_target, m_w_in, m_b_fgt, m_rel_bias, m_g_fox_out, m_g_chk_out, m_w_out, m_g_mix_pre, m_g_mix_post, m_g_mem_kv, m_w_mq, m_w_mk, m_w_mv, m_w_mo, m_g_mem_pre, m_g_mem_post, m_w_ff1, m_w_ff2, m_g_ff_pre, m_g_ff_post, v_w_in, v_b_fgt, v_rel_bias, v_g_fox_out, v_g_chk_out, v_w_out, v_g_mix_pre, v_g_mix_post, v_g_mem_kv, v_w_mq, v_w_mk, v_w_mv, v_w_mo, v_g_mem_pre, v_g_mem_post, v_w_ff1, v_w_ff2, v_g_ff_pre, v_g_ff_post):
    args = dict(locals())
    two_d = lambda a: a.reshape(a.shape[-2:])
    w = {n: two_d(args[n]) for n in WEIGHTS}
    m = {n: two_d(args['m_' + n]) for n in WEIGHTS}
    v = {n: two_d(args['v_' + n]) for n in WEIGHTS}

    sm = {n: w[n] for n in SMALL}
    shard_in = jnp.pad(w['w_in'].T, ((0, R_IN - N_IN), (0, 0))).astype(BF16)
    gathered_in, zero = _allgather(shard_in, "allgather_w_in")
    shard_rest = (jnp.concatenate([w['w_ff1'].T, w['w_ff2'], w['w_out'], w['w_mq'], w['w_mk'], w['w_mv'], w['w_mo']],
                                  axis=0) + zero[0, 0]).astype(BF16)
    gather = {'first': _start_copies("allgather_rest_start", shard_rest, (8, R_REST, D), _gather_plan, 3)}

    def gw_of(stage, after):
        if stage == 'relay':
            gather['block'], land = _wait_copies("allgather_rest_wait", gather['first'], after, _gather_plan)
            gather['second'] = _start_inplace("allgather_rest_relay_start", land, _relay_plan, 4)
            return gather['second'][3]
        land = _wait_inplace("allgather_rest_relay_wait", gather['second'], after, _relay_plan)
        return _gather_forward(land, gather['block'])

    rs = {}

    def on_grads(stage, g, after):
        if stage.endswith('halfway'):
            rs[stage[0]].halfway(after)
            return rs[stage[0]].token
        rs[stage] = _ReduceScatter("rs_" + stage.lower(), g)
        return rs[stage].token

    loss_local, grad_x, gs = _local_grads(x[0], mem[0], loss_target[0], gathered_in, gw_of, sm, on_grads,
                                          [gather['first'][4]])
    grads, deltas, new_m, new_v = {}, {}, {}, {}

    def update(n, out):
        grads[n], deltas[n], new_m[n], new_v[n] = out

    own, got, order = rs['A'].finish([grad_x, rs['C'].token])
    done = _sum_adam_tiled(own, got, order, [(0, w['w_ff1'], m['w_ff1'], v['w_ff1'], True),
                                             (512, w['w_ff2'], m['w_ff2'], v['w_ff2'], False)], "adamw_group_a")
    update('w_ff1', done[:4])
    update('w_ff2', done[4:])
    own, got, order = rs['B'].finish([grad_x, rs['C'].token])
    names_b = ('w_out', 'w_mq', 'w_mk', 'w_mv', 'w_mo')
    done = _sum_adam_rows(own, got, order, [w[n] for n in names_b], [m[n] for n in names_b], [v[n] for n in names_b],
                          "adamw_group_b")
    for k, n in enumerate(names_b):
        update(n, done[4 * k:4 * k + 4])

    own, got, order = rs['C'].finish([new_v[n] for n in BIG if n != 'w_in'])
    rows_of = lambda a: jnp.transpose(a, (2, 0, 1))
    done = _sum_adam(own, got, order, 0, rows_of(w_in), rows_of(m_w_in), rows_of(v_w_in), "adamw_w_in")
    update('w_in', [jnp.transpose(a, (1, 2, 0)) for a in done])

    gparts, _ = _allgather(_pack_small(gs, loss_local), "allgather_small_grads", [got])
    small = _adamw_small(gparts, [w[n] for n in SMALL], [m[n] for n in SMALL], [v[n] for n in SMALL])
    loss = small[0][0, 0]
    for t, n in enumerate(SMALL):
        update(n, small[1 + 4 * t:5 + 4 * t])

    out = [loss, grad_x[None]]
    for group in (grads, deltas, new_m, new_v):
        out += [group[n].reshape(args[n].shape) for n in WEIGHTS]
    return tuple(out)
```
